```python
import jax, jax.numpy as jnp
from jax import lax
import numpy as np

D_MODEL = 1024
BATCH = 16
SEQ = 2048
DEPTH = 2

MEM_LEN = 256
CHUNK = 128
SG_GROUPS = 8
SG_WIDTH = D_MODEL
SG_GDIM = SG_WIDTH // SG_GROUPS
SSM_INNER = 2 * D_MODEL
SSM_HEADDIM = 64
SSM_HEADS = SSM_INNER // SSM_HEADDIM
SSM_STATE = 128
SSM_GROUPS = 4
SSM_RPG = SSM_HEADS // SSM_GROUPS
SSM_CONV = 4
SSM_CONV_DIM = SSM_INNER + 2 * SSM_GROUPS * SSM_STATE
X_HEADS = 4
X_HEADDIM = D_MODEL // X_HEADS
FFN_HIDDEN = -(-(8 * D_MODEL) // (3 * 256)) * 256
ALPHA = float((2 * DEPTH) ** 0.25)
BETA = float((8 * DEPTH) ** -0.25)
LN_EPS = 1e-5
RMS_EPS = 1e-5

U_END = SG_WIDTH
V_END = U_END + SG_WIDTH
Z_END = V_END + SSM_INNER
XBC_END = Z_END + SSM_CONV_DIM
DT_END = XBC_END + SSM_HEADS
GA_END = DT_END + D_MODEL
IN_COLS = GA_END + D_MODEL
IN_SPLITS = (U_END, V_END, Z_END, XBC_END, DT_END, GA_END)

kernel_name = "hybrid_gmlp_ssd_gated_deepnorm"


def layer_norm(x, g, b):
    xf = x.astype(jnp.float32)
    mu = jnp.mean(xf, axis=-1, keepdims=True)
    var = jnp.mean(jnp.square(xf - mu), axis=-1, keepdims=True)
    return ((xf - mu) * lax.rsqrt(var + LN_EPS) * g.astype(jnp.float32) + b.astype(jnp.float32)).astype(x.dtype)


def spatial_gating(u, v, ln_g, ln_b, w_s, b_s):
    bn, s, _ = v.shape
    u = jax.nn.gelu(u, approximate=False)
    v = layer_norm(jax.nn.gelu(v, approximate=False), ln_g, ln_b)
    vc = v.reshape(bn, s // CHUNK, CHUNK, SG_GROUPS, SG_GDIM)
    causal = jnp.tril(jnp.ones((CHUNK, CHUNK), dtype=bool))
    w = jnp.where(causal[None], w_s, jnp.zeros((), w_s.dtype))
    mixed = jnp.einsum('gts,bcsgd->bctgd', w, vc) + jnp.transpose(b_s)[None, None, :, :, None]
    return u * mixed.reshape(bn, s, SG_WIDTH)


def ssd_branch(z, xbc, dt, conv_w, conv_b, dt_bias, a_log, d_skip, norm_g):
    bn, s, _ = xbc.shape
    nc = s // CHUNK
    xbc = lax.conv_general_dilated(xbc, conv_w[:, None, :], window_strides=(1,),
                                   padding=[(SSM_CONV - 1, 0)],
                                   dimension_numbers=('NWC', 'WIO', 'NWC'),
                                   feature_group_count=SSM_CONV_DIM) + conv_b
    xbc = jax.nn.silu(xbc).astype(jnp.float32)
    xs = xbc[..., :SSM_INNER]
    bm = xbc[..., SSM_INNER:SSM_INNER + SSM_GROUPS * SSM_STATE]
    cm = xbc[..., SSM_INNER + SSM_GROUPS * SSM_STATE:]
    dt = jax.nn.softplus(dt.astype(jnp.float32) + dt_bias.astype(jnp.float32))
    a = -jnp.exp(a_log.astype(jnp.float32)).reshape(SSM_GROUPS, SSM_RPG)

    x = xs.reshape(bn, nc, CHUNK, SSM_GROUPS, SSM_RPG, SSM_HEADDIM)
    dtc = dt.reshape(bn, nc, CHUNK, SSM_GROUPS, SSM_RPG)
    bm = bm.reshape(bn, nc, CHUNK, SSM_GROUPS, SSM_STATE)
    cm = cm.reshape(bn, nc, CHUNK, SSM_GROUPS, SSM_STATE)
    xdt = x * dtc[..., None]
    da = jnp.moveaxis(dtc * a, 2, -1)
    da_cs = jnp.cumsum(da, axis=-1)

    causal = jnp.tril(jnp.ones((CHUNK, CHUNK), dtype=bool))
    seg = da_cs[..., :, None] - da_cs[..., None, :]
    decay = jnp.exp(jnp.where(causal, seg, -jnp.inf))
    cb = jnp.einsum('bclgn,bcsgn->bcgls', cm, bm)
    y_diag = jnp.einsum('bcgls,bcgrls,bcsgrp->bclgrp', cb, decay, xdt)

    decay_states = jnp.exp(da_cs[..., -1:] - da_cs)
    states = jnp.einsum('bclgn,bcgrl,bclgrp->bcgrpn', bm, decay_states, xdt)
    chunk_decay = jnp.exp(da_cs[..., -1])

    def step(carry, inp):
        st, dec = inp
        return carry * dec[..., None, None] + st, carry

    init = jnp.zeros((bn, SSM_GROUPS, SSM_RPG, SSM_HEADDIM, SSM_STATE), jnp.float32)
    _, prev = lax.scan(step, init, (jnp.moveaxis(states, 1, 0), jnp.moveaxis(chunk_decay, 1, 0)))
    prev = jnp.moveaxis(prev, 0, 1)

    y_off = jnp.einsum('bclgn,bcgrpn,bcgrl->bclgrp', cm, prev, jnp.exp(da_cs))
    y = y_diag + y_off + x * d_skip.astype(jnp.float32).reshape(SSM_GROUPS, SSM_RPG)[..., None]
    y = y.reshape(bn, s, SSM_INNER) * jax.nn.silu(z.astype(jnp.float32))
    yg = y.reshape(bn, s, SSM_GROUPS, SSM_INNER // SSM_GROUPS)
    yg = yg * lax.rsqrt(jnp.mean(jnp.square(yg), axis=-1, keepdims=True) + RMS_EPS)
    y = yg.reshape(bn, s, SSM_INNER) * norm_g.astype(jnp.float32)
    return y.astype(z.dtype)


def mixer(h, w_in, sg_ln_g, sg_ln_b, sg_w, sg_b, conv_w, conv_b, dt_bias, a_log,
          d_skip, ssm_norm_g, p_a, p_b, w_mix_o):
    proj = h @ w_in
    u, v, z, xbc, dt, g_a, g_b = jnp.split(proj, IN_SPLITS, axis=-1)
    br_a = spatial_gating(u, v, sg_ln_g, sg_ln_b, sg_w, sg_b) @ p_a
    br_b = ssd_branch(z, xbc, dt, conv_w, conv_b, dt_bias, a_log, d_skip, ssm_norm_g) @ p_b
    merged = jax.nn.sigmoid(g_a) * br_a + jax.nn.sigmoid(g_b) * br_b
    return merged @ w_mix_o


def cross_attention(h, mem_n, w_xq, w_xkv, w_xo):
    bn, s, _ = h.shape
    q = (h @ w_xq).reshape(bn, s, X_HEADS, X_HEADDIM)
    k, v = jnp.split(mem_n @ w_xkv, 2, axis=-1)
    k = k.reshape(bn, -1, X_HEADS, X_HEADDIM)
    v = v.reshape(bn, -1, X_HEADS, X_HEADDIM)
    scores = jnp.einsum('bshd,bmhd->bhsm', q, k).astype(jnp.float32) * (X_HEADDIM ** -0.5)
    p = jax.nn.softmax(scores, axis=-1).astype(v.dtype)
    o = jnp.einsum('bhsm,bmhd->bshd', p, v).reshape(bn, s, D_MODEL)
    return o @ w_xo


def swiglu(h, w_ffn_in, w_ffn_out):
    g, u = jnp.split(h @ w_ffn_in, 2, axis=-1)
    return (jax.nn.silu(g) * u) @ w_ffn_out


def _fwd_setup_inputs(seed: int = 0) -> dict:
    key = jax.random.key(seed)
    ks = iter(jax.random.split(key, 40))
    f32 = jnp.float32

    def nrm(shape, scale):
        return jax.random.normal(next(ks), shape, f32) * scale

    x = jax.random.normal(next(ks), (BATCH, SEQ, D_MODEL), f32)
    mem = jax.random.normal(next(ks), (BATCH, MEM_LEN, D_MODEL), f32)
    mem_ln_g = 1.0 + nrm((D_MODEL,), 0.02)
    mem_ln_b = nrm((D_MODEL,), 0.02)
    w_in = nrm((DEPTH, D_MODEL, IN_COLS), D_MODEL ** -0.5)
    sg_ln_g = 1.0 + nrm((DEPTH, SG_WIDTH), 0.02)
    sg_ln_b = nrm((DEPTH, SG_WIDTH), 0.02)
    sg_w = nrm((DEPTH, SG_GROUPS, CHUNK, CHUNK), CHUNK ** -0.5)
    sg_b = 1.0 + nrm((DEPTH, SG_GROUPS, CHUNK), 0.01)
    conv_w = nrm((DEPTH, SSM_CONV, SSM_CONV_DIM), SSM_CONV ** -0.5)
    conv_b = nrm((DEPTH, SSM_CONV_DIM), 0.02)
    dt0 = jnp.exp(jax.random.uniform(next(ks), (DEPTH, SSM_HEADS), f32,
                                     float(np.log(1e-3)), float(np.log(1e-1))))
    dt_bias = dt0 + jnp.log(-jnp.expm1(-dt0))
    a_log = jnp.log(jax.random.uniform(next(ks), (DEPTH, SSM_HEADS), f32, 1.0, 16.0))
    d_skip = 1.0 + nrm((DEPTH, SSM_HEADS), 0.02)
    ssm_norm_g = 1.0 + nrm((DEPTH, SSM_INNER), 0.02)
    p_a = nrm((DEPTH, SG_WIDTH, D_MODEL), BETA * SG_WIDTH ** -0.5)
    p_b = nrm((DEPTH, SSM_INNER, D_MODEL), BETA * SSM_INNER ** -0.5)
    w_mix_o = nrm((DEPTH, D_MODEL, D_MODEL), BETA * D_MODEL ** -0.5)
    w_xq = nrm((DEPTH, D_MODEL, D_MODEL), D_MODEL ** -0.5)
    w_xkv = jnp.concatenate([nrm((DEPTH, D_MODEL, D_MODEL), D_MODEL ** -0.5),
                             nrm((DEPTH, D_MODEL, D_MODEL), BETA * D_MODEL ** -0.5)], axis=-1)
    w_xo = nrm((DEPTH, D_MODEL, D_MODEL), BETA * D_MODEL ** -0.5)
    w_ffn_in = nrm((DEPTH, D_MODEL, 2 * FFN_HIDDEN), BETA * D_MODEL ** -0.5)
    w_ffn_out = nrm((DEPTH, FFN_HIDDEN, D_MODEL), BETA * FFN_HIDDEN ** -0.5)
    ln_g = 1.0 + nrm((DEPTH, 3, D_MODEL), 0.02)
    ln_b = nrm((DEPTH, 3, D_MODEL), 0.02)
    return {"x": x, "mem": mem, "mem_ln_g": mem_ln_g, "mem_ln_b": mem_ln_b,
            "w_in": w_in, "sg_ln_g": sg_ln_g, "sg_ln_b": sg_ln_b, "sg_w": sg_w, "sg_b": sg_b,
            "conv_w": conv_w, "conv_b": conv_b, "dt_bias": dt_bias, "a_log": a_log,
            "d_skip": d_skip, "ssm_norm_g": ssm_norm_g, "p_a": p_a, "p_b": p_b,
            "w_mix_o": w_mix_o, "w_xq": w_xq, "w_xkv": w_xkv, "w_xo": w_xo,
            "w_ffn_in": w_ffn_in, "w_ffn_out": w_ffn_out, "ln_g": ln_g, "ln_b": ln_b}


def _fwd_reference(x, mem, mem_ln_g, mem_ln_b, w_in, sg_ln_g, sg_ln_b, sg_w, sg_b, conv_w, conv_b,
              dt_bias, a_log, d_skip, ssm_norm_g, p_a, p_b, w_mix_o, w_xq, w_xkv, w_xo,
              w_ffn_in, w_ffn_out, ln_g, ln_b):
    mem_n = layer_norm(mem, mem_ln_g, mem_ln_b)
    for i in range(DEPTH):
        y = mixer(x, w_in[i], sg_ln_g[i], sg_ln_b[i], sg_w[i], sg_b[i], conv_w[i], conv_b[i],
                  dt_bias[i], a_log[i], d_skip[i], ssm_norm_g[i], p_a[i], p_b[i], w_mix_o[i])
        x = layer_norm(ALPHA * x + y, ln_g[i, 0], ln_b[i, 0])
        y = cross_attention(x, mem_n, w_xq[i], w_xkv[i], w_xo[i])
        x = layer_norm(ALPHA * x + y, ln_g[i, 1], ln_b[i, 1])
        y = swiglu(x, w_ffn_in[i], w_ffn_out[i])
        x = layer_norm(ALPHA * x + y, ln_g[i, 2], ln_b[i, 2])
    return x


import jax as _jax
import jax.numpy as _jnp

TWIN_FORMAT = 'train_step'
FWD_PARAMS = ['x', 'mem', 'mem_ln_g', 'mem_ln_b', 'w_in', 'sg_ln_g', 'sg_ln_b', 'sg_w', 'sg_b', 'conv_w', 'conv_b', 'dt_bias', 'a_log', 'd_skip', 'ssm_norm_g', 'p_a', 'p_b', 'w_mix_o', 'w_xq', 'w_xkv', 'w_xo', 'w_ffn_in', 'w_ffn_out', 'ln_g', 'ln_b']
TWIN_WEIGHTS = ['mem_ln_g', 'mem_ln_b', 'w_in', 'sg_ln_g', 'sg_ln_b', 'sg_w', 'sg_b', 'conv_w', 'conv_b', 'dt_bias', 'a_log', 'd_skip', 'ssm_norm_g', 'p_a', 'p_b', 'w_mix_o', 'w_xq', 'w_xkv', 'w_xo', 'w_ffn_in', 'w_ffn_out', 'ln_g', 'ln_b']
TWIN_DIFF_INPUT = 'x'
TWIN_INPUTS = ['x', 'mem', 'mem_ln_g', 'mem_ln_b', 'w_in', 'sg_ln_g', 'sg_ln_b', 'sg_w', 'sg_b', 'conv_w', 'conv_b', 'dt_bias', 'a_log', 'd_skip', 'ssm_norm_g', 'p_a', 'p_b', 'w_mix_o', 'w_xq', 'w_xkv', 'w_xo', 'w_ffn_in', 'w_ffn_out', 'ln_g', 'ln_b', 'loss_target', 'm_mem_ln_g', 'm_mem_ln_b', 'm_w_in', 'm_sg_ln_g', 'm_sg_ln_b', 'm_sg_w', 'm_sg_b', 'm_conv_w', 'm_conv_b', 'm_dt_bias', 'm_a_log', 'm_d_skip', 'm_ssm_norm_g', 'm_p_a', 'm_p_b', 'm_w_mix_o', 'm_w_xq', 'm_w_xkv', 'm_w_xo', 'm_w_ffn_in', 'm_w_ffn_out', 'm_ln_g', 'm_ln_b', 'v_mem_ln_g', 'v_mem_ln_b', 'v_w_in', 'v_sg_ln_g', 'v_sg_ln_b', 'v_sg_w', 'v_sg_b', 'v_conv_w', 'v_conv_b', 'v_dt_bias', 'v_a_log', 'v_d_skip', 'v_ssm_norm_g', 'v_p_a', 'v_p_b', 'v_w_mix_o', 'v_w_xq', 'v_w_xkv', 'v_w_xo', 'v_w_ffn_in', 'v_w_ffn_out', 'v_ln_g', 'v_ln_b']
TWIN_OUTPUTS = ['loss', 'grad_x', 'grad_mem_ln_g', 'grad_mem_ln_b', 'grad_w_in', 'grad_sg_ln_g', 'grad_sg_ln_b', 'grad_sg_w', 'grad_sg_b', 'grad_conv_w', 'grad_conv_b', 'grad_dt_bias', 'grad_a_log', 'grad_d_skip', 'grad_ssm_norm_g', 'grad_p_a', 'grad_p_b', 'grad_w_mix_o', 'grad_w_xq', 'grad_w_xkv', 'grad_w_xo', 'grad_w_ffn_in', 'grad_w_ffn_out', 'grad_ln_g', 'grad_ln_b', 'delta_mem_ln_g', 'delta_mem_ln_b', 'delta_w_in', 'delta_sg_ln_g', 'delta_sg_ln_b', 'delta_sg_w', 'delta_sg_b', 'delta_conv_w', 'delta_conv_b', 'delta_dt_bias', 'delta_a_log', 'delta_d_skip', 'delta_ssm_norm_g', 'delta_p_a', 'delta_p_b', 'delta_w_mix_o', 'delta_w_xq', 'delta_w_xkv', 'delta_w_xo', 'delta_w_ffn_in', 'delta_w_ffn_out', 'delta_ln_g', 'delta_ln_b', 'new_m_mem_ln_g', 'new_m_mem_ln_b', 'new_m_w_in', 'new_m_sg_ln_g', 'new_m_sg_ln_b', 'new_m_sg_w', 'new_m_sg_b', 'new_m_conv_w', 'new_m_conv_b', 'new_m_dt_bias', 'new_m_a_log', 'new_m_d_skip', 'new_m_ssm_norm_g', 'new_m_p_a', 'new_m_p_b', 'new_m_w_mix_o', 'new_m_w_xq', 'new_m_w_xkv', 'new_m_w_xo', 'new_m_w_ffn_in', 'new_m_w_ffn_out', 'new_m_ln_g', 'new_m_ln_b', 'new_v_mem_ln_g', 'new_v_mem_ln_b', 'new_v_w_in', 'new_v_sg_ln_g', 'new_v_sg_ln_b', 'new_v_sg_w', 'new_v_sg_b', 'new_v_conv_w', 'new_v_conv_b', 'new_v_dt_bias', 'new_v_a_log', 'new_v_d_skip', 'new_v_ssm_norm_g', 'new_v_p_a', 'new_v_p_b', 'new_v_w_mix_o', 'new_v_w_xq', 'new_v_w_xkv', 'new_v_w_xo', 'new_v_w_ffn_in', 'new_v_w_ffn_out', 'new_v_ln_g', 'new_v_ln_b']
TWIN_LEAF_KINDS = {'loss': 'loss', 'grad_x': 'grad_x', 'grad_mem_ln_g': 'grad_w', 'grad_mem_ln_b': 'grad_w', 'grad_w_in': 'grad_w', 'grad_sg_ln_g': 'grad_w', 'grad_sg_ln_b': 'grad_w', 'grad_sg_w': 'grad_w', 'grad_sg_b': 'grad_w', 'grad_conv_w': 'grad_w', 'grad_conv_b': 'grad_w', 'grad_dt_bias': 'grad_w', 'grad_a_log': 'grad_w', 'grad_d_skip': 'grad_w', 'grad_ssm_norm_g': 'grad_w', 'grad_p_a': 'grad_w', 'grad_p_b': 'grad_w', 'grad_w_mix_o': 'grad_w', 'grad_w_xq': 'grad_w', 'grad_w_xkv': 'grad_w', 'grad_w_xo': 'grad_w', 'grad_w_ffn_in': 'grad_w', 'grad_w_ffn_out': 'grad_w', 'grad_ln_g': 'grad_w', 'grad_ln_b': 'grad_w', 'delta_mem_ln_g': 'delta_w', 'delta_mem_ln_b': 'delta_w', 'delta_w_in': 'delta_w', 'delta_sg_ln_g': 'delta_w', 'delta_sg_ln_b': 'delta_w', 'delta_sg_w': 'delta_w', 'delta_sg_b': 'delta_w', 'delta_conv_w': 'delta_w', 'delta_conv_b': 'delta_w', 'delta_dt_bias': 'delta_w', 'delta_a_log': 'delta_w', 'delta_d_skip': 'delta_w', 'delta_ssm_norm_g': 'delta_w', 'delta_p_a': 'delta_w', 'delta_p_b': 'delta_w', 'delta_w_mix_o': 'delta_w', 'delta_w_xq': 'delta_w', 'delta_w_xkv': 'delta_w', 'delta_w_xo': 'delta_w', 'delta_w_ffn_in': 'delta_w', 'delta_w_ffn_out': 'delta_w', 'delta_ln_g': 'delta_w', 'delta_ln_b': 'delta_w', 'new_m_mem_ln_g': 'new_m', 'new_m_mem_ln_b': 'new_m', 'new_m_w_in': 'new_m', 'new_m_sg_ln_g': 'new_m', 'new_m_sg_ln_b': 'new_m', 'new_m_sg_w': 'new_m', 'new_m_sg_b': 'new_m', 'new_m_conv_w': 'new_m', 'new_m_conv_b': 'new_m', 'new_m_dt_bias': 'new_m', 'new_m_a_log': 'new_m', 'new_m_d_skip': 'new_m', 'new_m_ssm_norm_g': 'new_m', 'new_m_p_a': 'new_m', 'new_m_p_b': 'new_m', 'new_m_w_mix_o': 'new_m', 'new_m_w_xq': 'new_m', 'new_m_w_xkv': 'new_m', 'new_m_w_xo': 'new_m', 'new_m_w_ffn_in': 'new_m', 'new_m_w_ffn_out': 'new_m', 'new_m_ln_g': 'new_m', 'new_m_ln_b': 'new_m', 'new_v_mem_ln_g': 'new_v', 'new_v_mem_ln_b': 'new_v', 'new_v_w_in': 'new_v', 'new_v_sg_ln_g': 'new_v', 'new_v_sg_ln_b': 'new_v', 'new_v_sg_w': 'new_v', 'new_v_sg_b': 'new_v', 'new_v_conv_w': 'new_v', 'new_v_conv_b': 'new_v', 'new_v_dt_bias': 'new_v', 'new_v_a_log': 'new_v', 'new_v_d_skip': 'new_v', 'new_v_ssm_norm_g': 'new_v', 'new_v_p_a': 'new_v', 'new_v_p_b': 'new_v', 'new_v_w_mix_o': 'new_v', 'new_v_w_xq': 'new_v', 'new_v_w_xkv': 'new_v', 'new_v_w_xo': 'new_v', 'new_v_w_ffn_in': 'new_v', 'new_v_w_ffn_out': 'new_v', 'new_v_ln_g': 'new_v', 'new_v_ln_b': 'new_v'}


def _forward(args):
    return _fwd_reference(*[args[k] for k in FWD_PARAMS])


def _output_shape():
    out = _jax.eval_shape(lambda: _forward(_fwd_setup_inputs(0)))
    return out.shape, out.dtype

N_MICROBATCH = 1
ADAM_LR = 0.001
ADAM_B1 = 0.9
ADAM_B2 = 0.999
ADAM_EPS = 1e-08
ADAM_WD = 0.01
ADAM_STEP = 10
PER_EXAMPLE_BATCH_AXIS = {'x': 0, 'mem': 0, 'loss_target': 0}
SHARED_INPUTS = []
_WEIGHT_DTYPES = {'mem_ln_g': _jnp.float32, 'mem_ln_b': _jnp.float32, 'w_in': _jnp.float32, 'sg_ln_g': _jnp.float32, 'sg_ln_b': _jnp.float32, 'sg_w': _jnp.float32, 'sg_b': _jnp.float32, 'conv_w': _jnp.float32, 'conv_b': _jnp.float32, 'dt_bias': _jnp.float32, 'a_log': _jnp.float32, 'd_skip': _jnp.float32, 'ssm_norm_g': _jnp.float32, 'p_a': _jnp.float32, 'p_b': _jnp.float32, 'w_mix_o': _jnp.float32, 'w_xq': _jnp.float32, 'w_xkv': _jnp.float32, 'w_xo': _jnp.float32, 'w_ffn_in': _jnp.float32, 'w_ffn_out': _jnp.float32, 'ln_g': _jnp.float32, 'ln_b': _jnp.float32}
MOMENT_SCALE = {'mem_ln_g': 7.102760e-03, 'mem_ln_b': 1.251695e-01, 'w_in': 1.046386e-02, 'sg_ln_g': 8.428866e-03, 'sg_ln_b': 7.905857e-03, 'sg_w': 7.845750e-03, 'sg_b': 1.081385e-02, 'conv_w': 1.073997e-02, 'conv_b': 1.575498e-02, 'dt_bias': 2.057708e-02, 'a_log': 3.506448e-02, 'd_skip': 9.107233e-02, 'ssm_norm_g': 1.251450e-02, 'p_a': 3.210793e-02, 'p_b': 3.554330e-02, 'w_mix_o': 4.777686e-02, 'w_xq': 3.290858e-03, 'w_xkv': 6.202276e-03, 'w_xo': 7.998264e-03, 'w_ffn_in': 1.026287e-02, 'w_ffn_out': 1.688561e-02, 'ln_g': 1.313262e+01, 'ln_b': 8.077772e-01}


def _to_microbatches(a, axis):
    t = _jnp.moveaxis(a, axis, 0)
    t = t.reshape((N_MICROBATCH, t.shape[0] // N_MICROBATCH) + t.shape[1:])
    return _jnp.moveaxis(t, 1, axis + 1)


def setup_inputs(seed: int = 0) -> dict:
    inp = _fwd_setup_inputs(seed)
    key = _jax.random.fold_in(_jax.random.key(seed), 7919)
    shape, _ = _output_shape()
    out = dict(inp)
    out["loss_target"] = _jax.random.normal(_jax.random.fold_in(key, 0), shape, _jnp.float32)
    for i, name in enumerate(TWIN_WEIGHTS):
        w = inp[name].astype(_jnp.float32)
        if MOMENT_SCALE is None:
            s = _jnp.sqrt(_jnp.mean(_jnp.square(w)) + 1e-30)
        else:
            s = MOMENT_SCALE[name]
        km, kv = _jax.random.split(_jax.random.fold_in(key, i + 1))
        out[name] = w
        out["m_" + name] = s * _jax.random.normal(km, w.shape, _jnp.float32)
        out["v_" + name] = (s * s) * _jax.random.uniform(kv, w.shape, _jnp.float32, 0.5, 1.5)
    if N_MICROBATCH > 1:
        for name, axis in PER_EXAMPLE_BATCH_AXIS.items():
            out[name] = _to_microbatches(out[name], axis)
    return {'x': out['x'], 'mem': out['mem'], 'mem_ln_g': out['mem_ln_g'], 'mem_ln_b': out['mem_ln_b'], 'w_in': out['w_in'], 'sg_ln_g': out['sg_ln_g'], 'sg_ln_b': out['sg_ln_b'], 'sg_w': out['sg_w'], 'sg_b': out['sg_b'], 'conv_w': out['conv_w'], 'conv_b': out['conv_b'], 'dt_bias': out['dt_bias'], 'a_log': out['a_log'], 'd_skip': out['d_skip'], 'ssm_norm_g': out['ssm_norm_g'], 'p_a': out['p_a'], 'p_b': out['p_b'], 'w_mix_o': out['w_mix_o'], 'w_xq': out['w_xq'], 'w_xkv': out['w_xkv'], 'w_xo': out['w_xo'], 'w_ffn_in': out['w_ffn_in'], 'w_ffn_out': out['w_ffn_out'], 'ln_g': out['ln_g'], 'ln_b': out['ln_b'], 'loss_target': out['loss_target'], 'm_mem_ln_g': out['m_mem_ln_g'], 'm_mem_ln_b': out['m_mem_ln_b'], 'm_w_in': out['m_w_in'], 'm_sg_ln_g': out['m_sg_ln_g'], 'm_sg_ln_b': out['m_sg_ln_b'], 'm_sg_w': out['m_sg_w'], 'm_sg_b': out['m_sg_b'], 'm_conv_w': out['m_conv_w'], 'm_conv_b': out['m_conv_b'], 'm_dt_bias': out['m_dt_bias'], 'm_a_log': out['m_a_log'], 'm_d_skip': out['m_d_skip'], 'm_ssm_norm_g': out['m_ssm_norm_g'], 'm_p_a': out['m_p_a'], 'm_p_b': out['m_p_b'], 'm_w_mix_o': out['m_w_mix_o'], 'm_w_xq': out['m_w_xq'], 'm_w_xkv': out['m_w_xkv'], 'm_w_xo': out['m_w_xo'], 'm_w_ffn_in': out['m_w_ffn_in'], 'm_w_ffn_out': out['m_w_ffn_out'], 'm_ln_g': out['m_ln_g'], 'm_ln_b': out['m_ln_b'], 'v_mem_ln_g': out['v_mem_ln_g'], 'v_mem_ln_b': out['v_mem_ln_b'], 'v_w_in': out['v_w_in'], 'v_sg_ln_g': out['v_sg_ln_g'], 'v_sg_ln_b': out['v_sg_ln_b'], 'v_sg_w': out['v_sg_w'], 'v_sg_b': out['v_sg_b'], 'v_conv_w': out['v_conv_w'], 'v_conv_b': out['v_conv_b'], 'v_dt_bias': out['v_dt_bias'], 'v_a_log': out['v_a_log'], 'v_d_skip': out['v_d_skip'], 'v_ssm_norm_g': out['v_ssm_norm_g'], 'v_p_a': out['v_p_a'], 'v_p_b': out['v_p_b'], 'v_w_mix_o': out['v_w_mix_o'], 'v_w_xq': out['v_w_xq'], 'v_w_xkv': out['v_w_xkv'], 'v_w_xo': out['v_w_xo'], 'v_w_ffn_in': out['v_w_ffn_in'], 'v_w_ffn_out': out['v_w_ffn_out'], 'v_ln_g': out['v_ln_g'], 'v_ln_b': out['v_ln_b']}


def _loss(weights, diff, rest, loss_target):
    with _jax.named_scope("forward"):
        args = {**rest, TWIN_DIFF_INPUT: diff, **{k: w.astype(_WEIGHT_DTYPES[k]) for k, w in weights.items()}}
        y = _forward(args)
    with _jax.named_scope("loss_head"):
        err = _jnp.square(y.astype(_jnp.float32) - loss_target)
        return 0.5 * _jnp.sum(_jnp.mean(err, axis=-1)) if err.ndim else 0.5 * err


def _adamw(w, g, m, v):
    m = ADAM_B1 * m + (1.0 - ADAM_B1) * g
    v = ADAM_B2 * v + (1.0 - ADAM_B2) * _jnp.square(g)
    m_hat = m / (1.0 - ADAM_B1 ** ADAM_STEP)
    v_hat = v / (1.0 - ADAM_B2 ** ADAM_STEP)
    delta = -ADAM_LR * (m_hat / (_jnp.sqrt(v_hat) + ADAM_EPS) + ADAM_WD * w)
    return delta, m, v


def reference(x, mem, mem_ln_g, mem_ln_b, w_in, sg_ln_g, sg_ln_b, sg_w, sg_b, conv_w, conv_b, dt_bias, a_log, d_skip, ssm_norm_g, p_a, p_b, w_mix_o, w_xq, w_xkv, w_xo, w_ffn_in, w_ffn_out, ln_g, ln_b, loss_target, m_mem_ln_g, m_mem_ln_b, m_w_in, m_sg_ln_g, m_sg_ln_b, m_sg_w, m_sg_b, m_conv_w, m_conv_b, m_dt_bias, m_a_log, m_d_skip, m_ssm_norm_g, m_p_a, m_p_b, m_w_mix_o, m_w_xq, m_w_xkv, m_w_xo, m_w_ffn_in, m_w_ffn_out, m_ln_g, m_ln_b, v_mem_ln_g, v_mem_ln_b, v_w_in, v_sg_ln_g, v_sg_ln_b, v_sg_w, v_sg_b, v_conv_w, v_conv_b, v_dt_bias, v_a_log, v_d_skip, v_ssm_norm_g, v_p_a, v_p_b, v_w_mix_o, v_w_xq, v_w_xkv, v_w_xo, v_w_ffn_in, v_w_ffn_out, v_ln_g, v_ln_b):
    given = dict(x=x, mem=mem, mem_ln_g=mem_ln_g, mem_ln_b=mem_ln_b, w_in=w_in, sg_ln_g=sg_ln_g, sg_ln_b=sg_ln_b, sg_w=sg_w, sg_b=sg_b, conv_w=conv_w, conv_b=conv_b, dt_bias=dt_bias, a_log=a_log, d_skip=d_skip, ssm_norm_g=ssm_norm_g, p_a=p_a, p_b=p_b, w_mix_o=w_mix_o, w_xq=w_xq, w_xkv=w_xkv, w_xo=w_xo, w_ffn_in=w_ffn_in, w_ffn_out=w_ffn_out, ln_g=ln_g, ln_b=ln_b, loss_target=loss_target, m_mem_ln_g=m_mem_ln_g, m_mem_ln_b=m_mem_ln_b, m_w_in=m_w_in, m_sg_ln_g=m_sg_ln_g, m_sg_ln_b=m_sg_ln_b, m_sg_w=m_sg_w, m_sg_b=m_sg_b, m_conv_w=m_conv_w, m_conv_b=m_conv_b, m_dt_bias=m_dt_bias, m_a_log=m_a_log, m_d_skip=m_d_skip, m_ssm_norm_g=m_ssm_norm_g, m_p_a=m_p_a, m_p_b=m_p_b, m_w_mix_o=m_w_mix_o, m_w_xq=m_w_xq, m_w_xkv=m_w_xkv, m_w_xo=m_w_xo, m_w_ffn_in=m_w_ffn_in, m_w_ffn_out=m_w_ffn_out, m_ln_g=m_ln_g, m_ln_b=m_ln_b, v_mem_ln_g=v_mem_ln_g, v_mem_ln_b=v_mem_ln_b, v_w_in=v_w_in, v_sg_ln_g=v_sg_ln_g, v_sg_ln_b=v_sg_ln_b, v_sg_w=v_sg_w, v_sg_b=v_sg_b, v_conv_w=v_conv_w, v_conv_b=v_conv_b, v_dt_bias=v_dt_bias, v_a_log=v_a_log, v_d_skip=v_d_skip, v_ssm_norm_g=v_ssm_norm_g, v_p_a=v_p_a, v_p_b=v_p_b, v_w_mix_o=v_w_mix_o, v_w_xq=v_w_xq, v_w_xkv=v_w_xkv, v_w_xo=v_w_xo, v_w_ffn_in=v_w_ffn_in, v_w_ffn_out=v_w_ffn_out, v_ln_g=v_ln_g, v_ln_b=v_ln_b)
    weights = {n: given[n] for n in TWIN_WEIGHTS}
    shared = {n: given[n] for n in SHARED_INPUTS}
    per_example = {n: given[n] for n in ['x', 'mem']}
    grad_fn = _jax.value_and_grad(_loss, argnums=(0, 1))

    def one_microbatch(ex, loss_target):
        ex = dict(ex)
        diff = ex.pop(TWIN_DIFF_INPUT)
        return grad_fn(weights, diff, {**shared, **ex}, loss_target)

    if N_MICROBATCH == 1:
        loss, (grad_w, grad_x) = one_microbatch(per_example, given["loss_target"])
    else:
        def body(carry, xs):
            loss_sum, grad_sum = carry
            l_k, (gw_k, gx_k) = one_microbatch(xs[0], xs[1])
            with _jax.named_scope("update"):
                return (loss_sum + l_k, _jax.tree.map(_jnp.add, grad_sum, gw_k)), gx_k

        init = (_jnp.zeros((), _jnp.float32), _jax.tree.map(_jnp.zeros_like, weights))
        (loss, grad_w), grad_x = _jax.lax.scan(body, init, (per_example, given["loss_target"]))
    with _jax.named_scope("update"):
        delta_w, new_m, new_v = {}, {}, {}
        for n in TWIN_WEIGHTS:
            delta_w[n], new_m[n], new_v[n] = _adamw(weights[n], grad_w[n], given["m_" + n], given["v_" + n])
    return (loss, grad_x, *[grad_w[n] for n in TWIN_WEIGHTS], *[delta_w[n] for n in TWIN_WEIGHTS],
            *[new_m[n] for n in TWIN_WEIGHTS], *[new_v[n] for n in TWIN_WEIGHTS])
```

```python
import functools
import math

import jax
import jax.numpy as jnp
from jax import lax
from jax.experimental import pallas as pl
from jax.experimental.pallas import tpu as pltpu

F32 = jnp.float32
MXU_DTYPE = jnp.bfloat16
HIGHEST = lax.Precision.HIGHEST

D_MODEL = 1024
DEPTH = 2
CHUNK = 128
SG_GROUPS = 8
SSM_INNER = 2048
SSM_HEADDIM = 64
SSM_HEADS = 32
SSM_STATE = 128
SSM_GROUPS = 4
SSM_CONV = 4
SSM_CONV_DIM = 3072
X_HEADS = 4
X_HEADDIM = 256
FFN_HIDDEN = 2816
ALPHA = float((2 * DEPTH) ** 0.25)
LN_EPS = 1e-5
RMS_EPS = 1e-5
ADAM_LR = 0.001
ADAM_B1 = 0.9
ADAM_B2 = 0.999
ADAM_EPS = 1e-08
ADAM_WD = 0.01
ADAM_STEP = 10

MAIN_COLS = 9216
UVZ_COLS = 4096
GAB_COL0 = 4096
XBC_COL0 = 6144
HEAD_PAD = 128

VMEM_LIMIT = 56 * 1024 * 1024
LANE = 128
SUBLANE = 8

N_CHIPS = 4
N_DEV = 8


def _pick(n, cands):
    for c in cands:
        if n % c == 0:
            return c
    return n


def _params(*sem):
    return pltpu.CompilerParams(dimension_semantics=tuple(sem), vmem_limit_bytes=VMEM_LIMIT)


_ANY = pl.BlockSpec(memory_space=pl.ANY)
_MESH = pl.DeviceIdType.MESH


def _nt(a, b):
    return lax.dot_general(a, b, (((1,), (1,)), ((), ())), preferred_element_type=F32)


def _tn(a, b):
    return lax.dot_general(a, b, (((0,), (0,)), ((), ())), preferred_element_type=F32)


def _nn(a, b):
    return jnp.dot(a, b, preferred_element_type=F32)


def _sigmoid(x):
    return 1.0 / (1.0 + jnp.exp(-x))


def _gelu(x):
    return 0.5 * x * (1.0 + lax.erf(x * (2.0 ** -0.5)))


def _gelu_grad(x):
    return 0.5 * (1.0 + lax.erf(x * (2.0 ** -0.5))) + x * jnp.exp(-0.5 * x * x) * (1.0 / math.sqrt(2.0 * math.pi))


def _mm(a, b, *, ta=False, tb=False, out_dtype=F32, name):
    b, bl = b if isinstance(b, tuple) else (b, None)
    if ta:
        kdim, m = a.shape
    else:
        m, kdim = a.shape
    if tb:
        n, k2 = b.shape[-2:]
    else:
        k2, n = b.shape[-2:]
    assert kdim == k2, (a.shape, b.shape, ta, tb)
    tm = _pick(m, (1024, 512, 256, 128))
    tn = _pick(n, (1024, 512, 256, 128))
    tk = _pick(kdim, (512, 256, 128))
    nk = kdim // tk
    dn = (((0 if ta else 1,), (1 if tb else 0,)), ((), ()))

    def body(a_ref, b_ref, o_ref, acc_ref):
        k = pl.program_id(2)

        @pl.when(k == 0)
        def _():
            acc_ref[...] = jnp.zeros_like(acc_ref)

        acc_ref[...] += lax.dot_general(a_ref[...].astype(MXU_DTYPE), b_ref[...].astype(MXU_DTYPE), dn,
                                        preferred_element_type=F32)

        @pl.when(k == nk - 1)
        def _():
            o_ref[...] = acc_ref[...].astype(out_dtype)

    a_spec = pl.BlockSpec((tk, tm), lambda i, j, k: (k, i)) if ta else pl.BlockSpec((tm, tk), lambda i, j, k: (i, k))
    if bl is None:
        b_spec = pl.BlockSpec((tn, tk), lambda i, j, k: (j, k)) if tb else pl.BlockSpec((tk, tn), lambda i, j, k: (k, j))
    elif tb:
        b_spec = pl.BlockSpec((None, tn, tk), lambda i, j, k: (bl, j, k))
    else:
        b_spec = pl.BlockSpec((None, tk, tn), lambda i, j, k: (bl, k, j))
    return pl.pallas_call(
        body, grid=(m // tm, n // tn, nk), in_specs=[a_spec, b_spec],
        out_specs=pl.BlockSpec((tm, tn), lambda i, j, k: (i, j)),
        out_shape=jax.ShapeDtypeStruct((m, n), out_dtype),
        scratch_shapes=[pltpu.VMEM((tm, tn), F32)],
        compiler_params=_params("parallel", "parallel", "arbitrary"), name=name)(a, b)


def _row_spec(tm, c, col=0):
    return pl.BlockSpec((tm, c), lambda i: (i, col))


def _par_spec(shape):
    nd = len(shape)
    return pl.BlockSpec(shape, lambda i: (0,) * nd)


def _ln_fwd(x, f, g, b, *, name):
    t, c = x.shape
    tm = _pick(t, (256, 128))
    has_f = f is not None

    def body(*refs):
        if has_f:
            x_ref, f_ref, g_ref, b_ref, y_ref, yb_ref, xh_ref, rs_ref = refs
            r = ALPHA * x_ref[...] + f_ref[...]
        else:
            x_ref, g_ref, b_ref, y_ref, yb_ref, xh_ref, rs_ref = refs
            r = x_ref[...]
        mu = jnp.mean(r, axis=-1, keepdims=True)
        xc = r - mu
        var = jnp.mean(xc * xc, axis=-1, keepdims=True)
        rstd = lax.rsqrt(var + LN_EPS)
        xh = xc * rstd
        y = xh * g_ref[...] + b_ref[...]
        y_ref[...] = y
        yb_ref[...] = y.astype(MXU_DTYPE)
        xh_ref[...] = xh
        rs_ref[...] = rstd

    ins = [x] + ([f] if has_f else []) + [g.reshape(1, c), b.reshape(1, c)]
    in_specs = [_row_spec(tm, c)] * (2 if has_f else 1) + [_par_spec((1, c))] * 2
    return pl.pallas_call(
        body, grid=(t // tm,), in_specs=in_specs,
        out_specs=[_row_spec(tm, c), _row_spec(tm, c), _row_spec(tm, c), _row_spec(tm, 1)],
        out_shape=[jax.ShapeDtypeStruct((t, c), F32), jax.ShapeDtypeStruct((t, c), MXU_DTYPE),
                   jax.ShapeDtypeStruct((t, c), F32), jax.ShapeDtypeStruct((t, 1), F32)],
        compiler_params=_params("parallel"), name=name)(*ins)


def _ln_bwd(addends, scales, xh, rs, g, *, name):
    t, c = xh.shape
    tm = _pick(t, (256, 128))
    na = len(addends)

    def body(*refs):
        a_refs = refs[:na]
        xh_ref, rs_ref, g_ref, dp_ref, dpb_ref, dg_ref, db_ref = refs[na:]

        @pl.when(pl.program_id(0) == 0)
        def _():
            dg_ref[...] = jnp.zeros_like(dg_ref)
            db_ref[...] = jnp.zeros_like(db_ref)

        dy = None
        for s, r in zip(scales, a_refs):
            term = r[...] if s == 1.0 else s * r[...]
            dy = term if dy is None else dy + term
        xhv = xh_ref[...]
        dxh = dy * g_ref[...]
        m1 = jnp.mean(dxh, axis=-1, keepdims=True)
        m2 = jnp.mean(dxh * xhv, axis=-1, keepdims=True)
        dp = rs_ref[...] * (dxh - m1 - xhv * m2)
        dp_ref[...] = dp
        dpb_ref[...] = dp.astype(MXU_DTYPE)
        dg_ref[...] += jnp.sum(dy * xhv, axis=0, keepdims=True)
        db_ref[...] += jnp.sum(dy, axis=0, keepdims=True)

    in_specs = [_row_spec(tm, c)] * (na + 1) + [_row_spec(tm, 1), _par_spec((1, c))]
    return pl.pallas_call(
        body, grid=(t // tm,), in_specs=in_specs,
        out_specs=[_row_spec(tm, c), _row_spec(tm, c), _par_spec((1, c)), _par_spec((1, c))],
        out_shape=[jax.ShapeDtypeStruct((t, c), F32), jax.ShapeDtypeStruct((t, c), MXU_DTYPE),
                   jax.ShapeDtypeStruct((1, c), F32), jax.ShapeDtypeStruct((1, c), F32)],
        compiler_params=_params("arbitrary"), name=name)(*addends, xh, rs, g.reshape(1, c))


def _add_scaled(addends, scales, *, name):
    t, c = addends[0].shape
    tm = _pick(t, (256, 128))
    na = len(addends)

    def body(*refs):
        acc = None
        for s, r in zip(scales, refs[:na]):
            term = r[...] if s == 1.0 else s * r[...]
            acc = term if acc is None else acc + term
        refs[na][...] = acc

    return pl.pallas_call(
        body, grid=(t // tm,), in_specs=[_row_spec(tm, c)] * na, out_specs=_row_spec(tm, c),
        out_shape=jax.ShapeDtypeStruct((t, c), F32), compiler_params=_params("parallel"), name=name)(*addends)


def _loss_head(y, tgt, *, name):
    t, c = y.shape
    tm = _pick(t, (256, 128))

    def body(y_ref, t_ref, dy_ref, ls_ref):
        @pl.when(pl.program_id(0) == 0)
        def _():
            ls_ref[...] = jnp.zeros_like(ls_ref)

        e = y_ref[...] - t_ref[...]
        dy_ref[...] = e * (1.0 / c)
        ls_ref[...] += jnp.sum(e * e, axis=0, keepdims=True)

    return pl.pallas_call(
        body, grid=(t // tm,), in_specs=[_row_spec(tm, c)] * 2,
        out_specs=[_row_spec(tm, c), _par_spec((1, c))],
        out_shape=[jax.ShapeDtypeStruct((t, c), F32), jax.ShapeDtypeStruct((1, c), F32)],
        compiler_params=_params("arbitrary"), name=name)(y, tgt)


def _swiglu_fwd(h, *, name):
    t, two_f = h.shape
    fh = two_f // 2
    tm = _pick(t, (256, 128))

    def body(g_ref, u_ref, a_ref):
        g = g_ref[...]
        a_ref[...] = (g * _sigmoid(g) * u_ref[...]).astype(MXU_DTYPE)

    return pl.pallas_call(
        body, grid=(t // tm,), in_specs=[_row_spec(tm, fh, 0), _row_spec(tm, fh, 1)], out_specs=_row_spec(tm, fh),
        out_shape=jax.ShapeDtypeStruct((t, fh), MXU_DTYPE), compiler_params=_params("parallel"), name=name)(h, h)


def _swiglu_bwd(h, da, *, name):
    t, two_f = h.shape
    fh = two_f // 2
    tm = _pick(t, (256, 128))

    def body(g_ref, u_ref, da_ref, dh_ref):
        g = g_ref[...]
        s = _sigmoid(g)
        dav = da_ref[...]
        dh_ref[:, :fh] = (dav * u_ref[...] * (s * (1.0 + g * (1.0 - s)))).astype(MXU_DTYPE)
        dh_ref[:, fh:] = (dav * g * s).astype(MXU_DTYPE)

    return pl.pallas_call(
        body, grid=(t // tm,), in_specs=[_row_spec(tm, fh, 0), _row_spec(tm, fh, 1), _row_spec(tm, fh)],
        out_specs=_row_spec(tm, two_f), out_shape=jax.ShapeDtypeStruct((t, two_f), MXU_DTYPE),
        compiler_params=_params("parallel"), name=name)(h, h, da)


def _attn_probs(q, k):
    s = _nt(q, k) * (X_HEADDIM ** -0.5)
    s = s - jnp.max(s, axis=-1, keepdims=True)
    p = jnp.exp(s)
    return p / jnp.sum(p, axis=-1, keepdims=True)


def _attn_fwd(q, kv, *, bsz, name):
    t = q.shape[0]
    s = t // bsz
    ml = kv.shape[0] // bsz
    hd = X_HEADDIM

    def body(q_ref, k_ref, v_ref, o_ref):
        p = _attn_probs(q_ref[...], k_ref[...])
        o_ref[...] = _nn(p.astype(MXU_DTYPE), v_ref[...]).astype(MXU_DTYPE)

    return pl.pallas_call(
        body, grid=(bsz, X_HEADS),
        in_specs=[pl.BlockSpec((s, hd), lambda b, h: (b, h)), pl.BlockSpec((ml, hd), lambda b, h: (b, h)),
                  pl.BlockSpec((ml, hd), lambda b, h: (b, X_HEADS + h))],
        out_specs=pl.BlockSpec((s, hd), lambda b, h: (b, h)),
        out_shape=jax.ShapeDtypeStruct((t, D_MODEL), MXU_DTYPE),
        compiler_params=_params("parallel", "parallel"), name=name)(q, kv, kv)


def _attn_bwd(q, kv, do, *, bsz, name):
    t = q.shape[0]
    s = t // bsz
    ml = kv.shape[0] // bsz
    hd = X_HEADDIM

    def body(q_ref, k_ref, v_ref, do_ref, dq_ref, dk_ref, dv_ref):
        qv, kk, vv, dov = q_ref[...], k_ref[...], v_ref[...], do_ref[...]
        p = _attn_probs(qv, kk)
        dp = _nt(dov, vv)
        dv_ref[...] = _tn(p.astype(MXU_DTYPE), dov).astype(MXU_DTYPE)
        ds = (p * (dp - jnp.sum(dp * p, axis=-1, keepdims=True)) * (X_HEADDIM ** -0.5)).astype(MXU_DTYPE)
        dq_ref[...] = _nn(ds, kk).astype(MXU_DTYPE)
        dk_ref[...] = _tn(ds, qv).astype(MXU_DTYPE)

    blk_q = pl.BlockSpec((s, hd), lambda b, h: (b, h))
    blk_m = pl.BlockSpec((ml, hd), lambda b, h: (b, h))
    return pl.pallas_call(
        body, grid=(bsz, X_HEADS),
        in_specs=[blk_q, blk_m, pl.BlockSpec((ml, hd), lambda b, h: (b, X_HEADS + h)), blk_q],
        out_specs=[blk_q, blk_m, blk_m],
        out_shape=[jax.ShapeDtypeStruct((t, D_MODEL), MXU_DTYPE), jax.ShapeDtypeStruct((bsz * ml, D_MODEL), MXU_DTYPE),
                   jax.ShapeDtypeStruct((bsz * ml, D_MODEL), MXU_DTYPE)],
        compiler_params=_params("parallel", "parallel"), name=name)(q, kv, kv, do)


def _causal(n):
    row = lax.broadcasted_iota(jnp.int32, (n, n), 0)
    col = lax.broadcasted_iota(jnp.int32, (n, n), 1)
    return row >= col


def _sg_norm(v, g, b):
    gv = _gelu(v)
    mu = jnp.mean(gv, axis=-1, keepdims=True)
    xc = gv - mu
    var = jnp.mean(xc * xc, axis=-1, keepdims=True)
    rstd = lax.rsqrt(var + LN_EPS)
    xh = xc * rstd
    return xh, rstd, xh * g + b


def _sg_fwd(proj, ln_g, ln_b, w, bcol, *, name):
    t = proj.shape[0]
    c = D_MODEL
    gd = c // SG_GROUPS

    def body(u_ref, v_ref, g_ref, b_ref, w_ref, bc_ref, o_ref):
        gu = _gelu(u_ref[...])
        _, _, vn = _sg_norm(v_ref[...], g_ref[...], b_ref[...])
        mask = _causal(CHUNK)
        for g in range(SG_GROUPS):
            sl = slice(g * gd, (g + 1) * gd)
            wg = jnp.where(mask, w_ref[g], 0.0).astype(MXU_DTYPE)
            mixed = _nn(wg, vn[:, sl].astype(MXU_DTYPE)) + bc_ref[g]
            o_ref[:, sl] = (gu[:, sl] * mixed).astype(MXU_DTYPE)

    return pl.pallas_call(
        body, grid=(t // CHUNK,),
        in_specs=[_row_spec(CHUNK, c, 0), _row_spec(CHUNK, c, 1), _par_spec((1, c)), _par_spec((1, c)),
                  _par_spec((SG_GROUPS, CHUNK, CHUNK)), _par_spec((SG_GROUPS, CHUNK, 1))],
        out_specs=_row_spec(CHUNK, c), out_shape=jax.ShapeDtypeStruct((t, c), MXU_DTYPE),
        compiler_params=_params("parallel"), name=name)(proj, proj, ln_g.reshape(1, c), ln_b.reshape(1, c), w, bcol)


def _sg_bwd(proj, dsgo, ln_g, ln_b, w, bcol, dproj, *, name):
    t = proj.shape[0]
    c = D_MODEL
    gd = c // SG_GROUPS

    def body(u_ref, v_ref, d_ref, g_ref, b_ref, w_ref, bc_ref, _, duv_ref, dw_ref, dbc_ref, dg_ref, db_ref, dvn_ref):
        @pl.when(pl.program_id(0) == 0)
        def _():
            dw_ref[...] = jnp.zeros_like(dw_ref)
            dbc_ref[...] = jnp.zeros_like(dbc_ref)
            dg_ref[...] = jnp.zeros_like(dg_ref)
            db_ref[...] = jnp.zeros_like(db_ref)

        u = u_ref[...]
        v = v_ref[...]
        dso = d_ref[...]
        gu = _gelu(u)
        xh, rstd, vn = _sg_norm(v, g_ref[...], b_ref[...])
        mask = _causal(CHUNK)
        for g in range(SG_GROUPS):
            sl = slice(g * gd, (g + 1) * gd)
            wg = jnp.where(mask, w_ref[g], 0.0).astype(MXU_DTYPE)
            vng = vn[:, sl].astype(MXU_DTYPE)
            mixed = _nn(wg, vng) + bc_ref[g]
            duv_ref[:, sl] = (dso[:, sl] * mixed * _gelu_grad(u[:, sl])).astype(MXU_DTYPE)
            dmix = dso[:, sl] * gu[:, sl]
            dmb = dmix.astype(MXU_DTYPE)
            dbc_ref[g] += jnp.sum(dmix, axis=-1, keepdims=True)
            dw_ref[g] += jnp.where(mask, _nt(dmb, vng), 0.0)
            dvn_ref[:, sl] = _tn(wg, dmb)
        dvn = dvn_ref[...]
        dg_ref[...] += jnp.sum(dvn * xh, axis=0, keepdims=True)
        db_ref[...] += jnp.sum(dvn, axis=0, keepdims=True)
        dxh = dvn * g_ref[...]
        m1 = jnp.mean(dxh, axis=-1, keepdims=True)
        m2 = jnp.mean(dxh * xh, axis=-1, keepdims=True)
        dgv = rstd * (dxh - m1 - xh * m2)
        duv_ref[:, c:] = (dgv * _gelu_grad(v)).astype(MXU_DTYPE)

    return pl.pallas_call(
        body, grid=(t // CHUNK,),
        in_specs=[_row_spec(CHUNK, c, 0), _row_spec(CHUNK, c, 1), _row_spec(CHUNK, c), _par_spec((1, c)),
                  _par_spec((1, c)), _par_spec((SG_GROUPS, CHUNK, CHUNK)), _par_spec((SG_GROUPS, CHUNK, 1)), _ANY],
        out_specs=[_row_spec(CHUNK, 2 * c), _par_spec((SG_GROUPS, CHUNK, CHUNK)), _par_spec((SG_GROUPS, CHUNK, 1)),
                   _par_spec((1, c)), _par_spec((1, c))],
        out_shape=[jax.ShapeDtypeStruct(dproj.shape, dproj.dtype), jax.ShapeDtypeStruct((SG_GROUPS, CHUNK, CHUNK), F32),
                   jax.ShapeDtypeStruct((SG_GROUPS, CHUNK, 1), F32), jax.ShapeDtypeStruct((1, c), F32),
                   jax.ShapeDtypeStruct((1, c), F32)],
        scratch_shapes=[pltpu.VMEM((CHUNK, c), F32)], input_output_aliases={7: 0},
        compiler_params=_params("arbitrary"), name=name)(proj, proj, dsgo, ln_g.reshape(1, c), ln_b.reshape(1, c), w, bcol, dproj)


CONV_TC = 512


def _conv_pre(x, w_ref, b_ref, rows):
    acc = x * w_ref[SSM_CONV - 1:SSM_CONV, :] + b_ref[...]
    for k in range(SSM_CONV - 1):
        sh = SSM_CONV - 1 - k
        xs = jnp.where(rows >= sh, pltpu.roll(x, sh, axis=0), 0.0)
        acc = acc + xs * w_ref[k:k + 1, :]
    return acc


def _conv_fwd(proj, w, b, *, bsz, name):
    t = proj.shape[0]
    s = t // bsz
    nj = SSM_CONV_DIM // CONV_TC
    c0 = XBC_COL0 // CONV_TC

    def body(x_ref, w_ref, b_ref, o_ref):
        x = x_ref[...]
        rows = lax.broadcasted_iota(jnp.int32, x.shape, 0)
        pre = _conv_pre(x, w_ref, b_ref, rows)
        o_ref[...] = pre * _sigmoid(pre)

    return pl.pallas_call(
        body, grid=(bsz, nj),
        in_specs=[pl.BlockSpec((s, CONV_TC), lambda bb, j: (bb, c0 + j)), pl.BlockSpec((SSM_CONV, CONV_TC), lambda bb, j: (0, j)),
                  pl.BlockSpec((1, CONV_TC), lambda bb, j: (0, j))],
        out_specs=pl.BlockSpec((s, CONV_TC), lambda bb, j: (bb, j)),
        out_shape=jax.ShapeDtypeStruct((t, SSM_CONV_DIM), F32),
        compiler_params=_params("parallel", "parallel"), name=name)(proj, w, b.reshape(1, -1))


def _conv_bwd(proj, dact, w, b, dproj, *, bsz, name):
    t = proj.shape[0]
    s = t // bsz
    nj = SSM_CONV_DIM // CONV_TC
    c0 = XBC_COL0 // CONV_TC

    def body(x_ref, d_ref, w_ref, b_ref, _, dx_ref, dw_ref, db_ref):
        @pl.when(pl.program_id(1) == 0)
        def _():
            dw_ref[...] = jnp.zeros_like(dw_ref)
            db_ref[...] = jnp.zeros_like(db_ref)

        x = x_ref[...]
        rows = lax.broadcasted_iota(jnp.int32, x.shape, 0)
        pre = _conv_pre(x, w_ref, b_ref, rows)
        sg = _sigmoid(pre)
        dpre = d_ref[...] * (sg * (1.0 + pre * (1.0 - sg)))
        db_ref[...] += jnp.sum(dpre, axis=0, keepdims=True)
        dx = dpre * w_ref[SSM_CONV - 1:SSM_CONV, :]
        dw_ref[SSM_CONV - 1:SSM_CONV, :] += jnp.sum(dpre * x, axis=0, keepdims=True)
        for k in range(SSM_CONV - 1):
            sh = SSM_CONV - 1 - k
            xs = jnp.where(rows >= sh, pltpu.roll(x, sh, axis=0), 0.0)
            dw_ref[k:k + 1, :] += jnp.sum(dpre * xs, axis=0, keepdims=True)
            dsh = jnp.where(rows < s - sh, pltpu.roll(dpre, s - sh, axis=0), 0.0)
            dx = dx + dsh * w_ref[k:k + 1, :]
        dx_ref[...] = dx.astype(MXU_DTYPE)

    return pl.pallas_call(
        body, grid=(nj, bsz),
        in_specs=[pl.BlockSpec((s, CONV_TC), lambda j, bb: (bb, c0 + j)), pl.BlockSpec((s, CONV_TC), lambda j, bb: (bb, j)),
                  pl.BlockSpec((SSM_CONV, CONV_TC), lambda j, bb: (0, j)), pl.BlockSpec((1, CONV_TC), lambda j, bb: (0, j)), _ANY],
        out_specs=[pl.BlockSpec((s, CONV_TC), lambda j, bb: (bb, c0 + j)), pl.BlockSpec((SSM_CONV, CONV_TC), lambda j, bb: (0, j)),
                   pl.BlockSpec((1, CONV_TC), lambda j, bb: (0, j))],
        out_shape=[jax.ShapeDtypeStruct(dproj.shape, dproj.dtype), jax.ShapeDtypeStruct((SSM_CONV, SSM_CONV_DIM), F32),
                   jax.ShapeDtypeStruct((1, SSM_CONV_DIM), F32)],
        input_output_aliases={4: 0},
        compiler_params=_params("parallel", "arbitrary"), name=name)(proj, dact, w, b.reshape(1, -1), dproj)


def _softplus(x):
    return jnp.maximum(x, 0.0) + jnp.log1p(jnp.exp(-jnp.abs(x)))


def _pad_heads(v):
    return jnp.broadcast_to(jnp.pad(v.astype(F32), (0, HEAD_PAD - SSM_HEADS))[None, :], (SUBLANE, HEAD_PAD))


def _ssd_prep(dt_raw, dt_bias8, a_log8, *, name):
    t = dt_raw.shape[0]
    n = CHUNK

    def body(r_ref, b_ref, al_ref, dt_ref, cs_ref, dtt_ref, cst_ref):
        dt = _softplus(r_ref[...] + b_ref[0:1, :])
        da = dt * (-jnp.exp(al_ref[0:1, :]))
        row = lax.broadcasted_iota(jnp.int32, (n, n), 0)
        col = lax.broadcasted_iota(jnp.int32, (n, n), 1)
        lower = (col <= row).astype(F32)
        upper = (row <= col).astype(F32)
        eye = (row == col).astype(F32)
        dt_ref[...] = dt
        cs_ref[...] = jnp.dot(lower, da, precision=HIGHEST, preferred_element_type=F32)
        dn = (((0,), (0,)), ((), ()))
        cst_ref[0] = lax.dot_general(da, upper, dn, precision=HIGHEST, preferred_element_type=F32)
        dtt_ref[0] = lax.dot_general(dt, eye, dn, precision=HIGHEST, preferred_element_type=F32)

    hp = HEAD_PAD
    return pl.pallas_call(
        body, grid=(t // n,),
        in_specs=[_row_spec(n, hp), _par_spec((SUBLANE, hp)), _par_spec((SUBLANE, hp))],
        out_specs=[_row_spec(n, hp), _row_spec(n, hp), pl.BlockSpec((1, hp, n), lambda i: (i, 0, 0)),
                   pl.BlockSpec((1, hp, n), lambda i: (i, 0, 0))],
        out_shape=[jax.ShapeDtypeStruct((t, hp), F32), jax.ShapeDtypeStruct((t, hp), F32),
                   jax.ShapeDtypeStruct((t // n, hp, n), F32), jax.ShapeDtypeStruct((t // n, hp, n), F32)],
        compiler_params=_params("parallel"), name=name)(dt_raw, dt_bias8, a_log8)


def _expand_mat():
    h = lax.broadcasted_iota(jnp.int32, (HEAD_PAD, SSM_INNER), 0)
    ch = lax.broadcasted_iota(jnp.int32, (HEAD_PAD, SSM_INNER), 1)
    return (ch // SSM_HEADDIM == h).astype(F32)


def _reduce_mat():
    ch = lax.broadcasted_iota(jnp.int32, (SSM_INNER, HEAD_PAD), 0)
    h = lax.broadcasted_iota(jnp.int32, (SSM_INNER, HEAD_PAD), 1)
    return (ch // SSM_HEADDIM == h).astype(F32)


def _expand(v, em):
    return jnp.dot(v, em, precision=HIGHEST, preferred_element_type=F32)


def _decay_mat(cs_ref, cst_ref, h, mask):
    seg = cs_ref[:, h:h + 1] - cst_ref[0, h:h + 1, :]
    return jnp.where(mask, jnp.exp(jnp.minimum(seg, 0.0)), 0.0)


GROUP_CH = SSM_INNER // SSM_GROUPS
PAIRS_PER_GROUP = GROUP_CH // LANE
HEADS_PER_GROUP = SSM_HEADS // SSM_GROUPS
BM_COL0 = SSM_INNER
CM_COL0 = SSM_INNER + SSM_GROUPS * SSM_STATE


def _ssd_specs(nc, rev):
    def cidx(i):
        return (i // nc) * nc + (nc - 1 - i % nc) if rev else i

    n = CHUNK
    xs = pl.BlockSpec((n, SSM_INNER), lambda i: (cidx(i), 0))
    bm = pl.BlockSpec((n, GROUP_CH), lambda i: (cidx(i), BM_COL0 // GROUP_CH))
    cm = pl.BlockSpec((n, GROUP_CH), lambda i: (cidx(i), CM_COL0 // GROUP_CH))
    hv = pl.BlockSpec((n, HEAD_PAD), lambda i: (cidx(i), 0))
    hvt = pl.BlockSpec((1, HEAD_PAD, n), lambda i: (cidx(i), 0, 0))
    st = pl.BlockSpec((1, SSM_INNER, SSM_STATE), lambda i: (cidx(i), 0, 0))
    return xs, bm, cm, hv, hvt, st


def _ssd_fwd(xbc, dt, cs, dtt, cst, dskip8, *, nc, name):
    t = xbc.shape[0]
    n = CHUNK
    xs_s, bm_s, cm_s, hv_s, hvt_s, st_s = _ssd_specs(nc, False)

    def body(xs_ref, bm_ref, cm_ref, dt_ref, cs_ref, dtt_ref, cst_ref, dsk_ref, y_ref, st_ref, prev):
        @pl.when(pl.program_id(0) % nc == 0)
        def _():
            prev[...] = jnp.zeros_like(prev)

        st_ref[0] = prev[...]
        em = _expand_mat()
        dtx = _expand(dt_ref[...], em)
        csx = _expand(cs_ref[...], em)
        dskx = _expand(dsk_ref[...], em)[0:1, :]
        xs = xs_ref[...]
        xdt = xs * dtx
        ecs = jnp.exp(csx)
        dec = jnp.exp(csx[n - 1:n, :] - csx)
        mask = _causal(n)
        lane = lax.broadcasted_iota(jnp.int32, (n, LANE), 1)
        for g in range(SSM_GROUPS):
            gs = slice(g * SSM_STATE, (g + 1) * SSM_STATE)
            gc = slice(g * GROUP_CH, (g + 1) * GROUP_CH)
            cmat = cm_ref[:, gs].astype(MXU_DTYPE)
            bmat = bm_ref[:, gs].astype(MXU_DTYPE)
            cb = _nt(cmat, bmat)
            yoff = ecs[:, gc] * _nt(cmat, prev[gc, :].astype(MXU_DTYPE))
            for q in range(PAIRS_PER_GROUP):
                hp = g * PAIRS_PER_GROUP + q
                sl = slice(hp * LANE, (hp + 1) * LANE)
                xp = xdt[:, sl].astype(MXU_DTYPE)
                m0 = (cb * _decay_mat(cs_ref, cst_ref, 2 * hp, mask)).astype(MXU_DTYPE)
                m1 = (cb * _decay_mat(cs_ref, cst_ref, 2 * hp + 1, mask)).astype(MXU_DTYPE)
                yd = jnp.where(lane < SSM_HEADDIM, _nn(m0, xp), _nn(m1, xp))
                y_ref[:, sl] = yd + yoff[:, q * LANE:(q + 1) * LANE] + xs[:, sl] * dskx[:, sl]
            snew = _tn((xdt[:, gc] * dec[:, gc]).astype(MXU_DTYPE), bmat)
            for r in range(HEADS_PER_GROUP):
                h = g * HEADS_PER_GROUP + r
                rows = slice(h * SSM_HEADDIM, (h + 1) * SSM_HEADDIM)
                e = jnp.exp(cst_ref[0, h:h + 1, n - 1:n])
                prev[rows, :] = prev[rows, :] * e + snew[r * SSM_HEADDIM:(r + 1) * SSM_HEADDIM, :]

    return pl.pallas_call(
        body, grid=(t // n,),
        in_specs=[xs_s, bm_s, cm_s, hv_s, hv_s, hvt_s, hvt_s, _par_spec((SUBLANE, HEAD_PAD))],
        out_specs=[xs_s, st_s],
        out_shape=[jax.ShapeDtypeStruct((t, SSM_INNER), F32), jax.ShapeDtypeStruct((t // n, SSM_INNER, SSM_STATE), F32)],
        scratch_shapes=[pltpu.VMEM((SSM_INNER, SSM_STATE), F32)],
        compiler_params=_params("arbitrary"), name=name)(xbc, xbc, xbc, dt, cs, dtt, cst, dskip8)


def _ssd_bwd(dy, xbc, dt, cs, dtt, cst, st, dskip8, a_log8, dt_raw, dt_bias8, *, nc, name):
    t = xbc.shape[0]
    n = CHUNK
    xs_s, bm_s, cm_s, hv_s, hvt_s, st_s = _ssd_specs(nc, True)
    acc_s = _par_spec((1, HEAD_PAD))
    xbc_s = pl.BlockSpec((n, SSM_CONV_DIM), xs_s.index_map)

    def body(dy_ref, xs_ref, bm_ref, cm_ref, dt_ref, cs_ref, dtt_ref, cst_ref, st_ref, dsk_ref, al_ref, raw_ref, bias_ref,
             dxbc_ref, ddr_ref, dal_ref, dds_ref, dbias_ref, dprev, dxdt_s, tdec_s, tcs_s):
        @pl.when(pl.program_id(0) % nc == 0)
        def _():
            dprev[...] = jnp.zeros_like(dprev)

        @pl.when(pl.program_id(0) == 0)
        def _():
            dal_ref[...] = jnp.zeros_like(dal_ref)
            dds_ref[...] = jnp.zeros_like(dds_ref)
            dbias_ref[...] = jnp.zeros_like(dbias_ref)

        em = _expand_mat()
        rm = _reduce_mat()

        def head_reduce(v):
            return jnp.dot(v, rm, precision=HIGHEST, preferred_element_type=F32)

        dtv = dt_ref[...]
        csv = cs_ref[...]
        dtx = _expand(dtv, em)
        csx = _expand(csv, em)
        dskx = _expand(dsk_ref[...], em)[0:1, :]
        xs = xs_ref[...]
        dyv = dy_ref[...]
        xdt = xs * dtx
        ecs = jnp.exp(csx)
        dec = jnp.exp(csx[n - 1:n, :] - csx)
        mask = _causal(n)
        lane = lax.broadcasted_iota(jnp.int32, (n, LANE), 1)
        hlane = lax.broadcasted_iota(jnp.int32, (1, HEAD_PAD), 1)
        hsub = lax.broadcasted_iota(jnp.int32, (HEAD_PAD, 1), 0)
        rsum = jnp.zeros((n, HEAD_PAD), F32)
        csum = jnp.zeros((HEAD_PAD, n), F32)
        for g in range(SSM_GROUPS):
            gs = slice(g * SSM_STATE, (g + 1) * SSM_STATE)
            gc = slice(g * GROUP_CH, (g + 1) * GROUP_CH)
            cmat = cm_ref[:, gs].astype(MXU_DTYPE)
            bmat = bm_ref[:, gs].astype(MXU_DTYPE)
            cb = _nt(cmat, bmat)
            pg = st_ref[0, gc, :].astype(MXU_DTYPE)
            dpg = dprev[gc, :]
            dpgb = dpg.astype(MXU_DTYPE)
            z = _nt(cmat, pg)
            dyg = dyv[:, gc]
            dz = (dyg * ecs[:, gc]).astype(MXU_DTYPE)
            dc = _nn(dz, pg)
            dprev_y = _tn(dz, cmat)
            tcs_s[:, gc] = dyg * z * ecs[:, gc]
            xd = xdt[:, gc] * dec[:, gc]
            wmat = _nt(bmat, dpgb)
            db = _nn(xd.astype(MXU_DTYPE), dpgb)
            tdec_s[:, gc] = wmat * xd
            dxdt_g = wmat * dec[:, gc]
            dcb = jnp.zeros((n, n), F32)
            for q in range(PAIRS_PER_GROUP):
                hp = g * PAIRS_PER_GROUP + q
                sl = slice(hp * LANE, (hp + 1) * LANE)
                xp = xdt[:, sl].astype(MXU_DTYPE)
                dyp = dyv[:, sl]
                dypb = dyp.astype(MXU_DTYPE)
                dxp = None
                for hh in range(2):
                    h = 2 * hp + hh
                    lm = _decay_mat(cs_ref, cst_ref, h, mask)
                    mine = (lane < SSM_HEADDIM) if hh == 0 else (lane >= SSM_HEADDIM)
                    dm = _nt(jnp.where(mine, dyp, 0.0).astype(MXU_DTYPE), xp)
                    dml = dm * lm
                    dcb = dcb + dml
                    gseg = dml * cb
                    rsum = rsum + jnp.sum(gseg, axis=1, keepdims=True) * (hlane == h).astype(F32)
                    csum = csum + (hsub == h).astype(F32) * jnp.sum(gseg, axis=0, keepdims=True)
                    dxh = _tn((cb * lm).astype(MXU_DTYPE), dypb)
                    dxp = dxh if dxp is None else jnp.where(mine, dxh, dxp)
                dxdt_s[:, sl] = dxdt_g[:, q * LANE:(q + 1) * LANE] + dxp
            dcbb = dcb.astype(MXU_DTYPE)
            dxbc_ref[:, CM_COL0 + g * SSM_STATE:CM_COL0 + (g + 1) * SSM_STATE] = dc + _nn(dcbb, bmat)
            dxbc_ref[:, BM_COL0 + g * SSM_STATE:BM_COL0 + (g + 1) * SSM_STATE] = db + _tn(dcbb, cmat)
            for r in range(HEADS_PER_GROUP):
                h = g * HEADS_PER_GROUP + r
                rows = slice(h * SSM_HEADDIM, (h + 1) * SSM_HEADDIM)
                lr = slice(r * SSM_HEADDIM, (r + 1) * SSM_HEADDIM)
                e = jnp.exp(cst_ref[0, h:h + 1, n - 1:n])
                dprev[rows, :] = dpg[lr, :] * e + dprev_y[lr, :]
            tq = lax.dot_general(dpg * st_ref[0, gc, :], rm[gc, :], (((0,), (0,)), ((), ())), precision=HIGHEST,
                                 preferred_element_type=F32)
            if g == 0:
                qsum = jnp.sum(tq, axis=0, keepdims=True)
            else:
                qsum = qsum + jnp.sum(tq, axis=0, keepdims=True)
        dxdt = dxdt_s[...]
        dxbc_ref[:, 0:SSM_INNER] = dxdt * dtx + dyv * dskx
        ddt = head_reduce(dxdt * xs)
        edec = head_reduce(tdec_s[...])
        ycs = head_reduce(tcs_s[...])
        row = lax.broadcasted_iota(jnp.int32, (n, HEAD_PAD), 0)
        extra = jnp.sum(edec, axis=0, keepdims=True) + qsum * jnp.exp(csv[n - 1:n, :])
        dcs = rsum - csum.T + ycs - edec + jnp.where(row == n - 1, extra, 0.0)
        r2 = lax.broadcasted_iota(jnp.int32, (n, n), 0)
        c2 = lax.broadcasted_iota(jnp.int32, (n, n), 1)
        dda = jnp.dot((c2 >= r2).astype(F32), dcs, precision=HIGHEST, preferred_element_type=F32)
        a_row = -jnp.exp(al_ref[0:1, :])
        ddt = ddt + dda * a_row
        dal_ref[...] += jnp.sum(dda * dtv, axis=0, keepdims=True) * a_row
        dds_ref[...] += jnp.sum(head_reduce(dyv * xs), axis=0, keepdims=True)
        ddr = ddt * _sigmoid(raw_ref[...] + bias_ref[0:1, :])
        ddr_ref[...] = ddr
        dbias_ref[...] += jnp.sum(ddr, axis=0, keepdims=True)

    par8 = _par_spec((SUBLANE, HEAD_PAD))
    return pl.pallas_call(
        body, grid=(t // n,),
        in_specs=[xs_s, xs_s, bm_s, cm_s, hv_s, hv_s, hvt_s, hvt_s, st_s, par8, par8, hv_s, par8],
        out_specs=[xbc_s, hv_s, acc_s, acc_s, acc_s],
        out_shape=[jax.ShapeDtypeStruct((t, SSM_CONV_DIM), F32), jax.ShapeDtypeStruct((t, HEAD_PAD), F32),
                   jax.ShapeDtypeStruct((1, HEAD_PAD), F32), jax.ShapeDtypeStruct((1, HEAD_PAD), F32),
                   jax.ShapeDtypeStruct((1, HEAD_PAD), F32)],
        scratch_shapes=[pltpu.VMEM((SSM_INNER, SSM_STATE), F32), pltpu.VMEM((n, SSM_INNER), F32),
                        pltpu.VMEM((n, SSM_INNER), F32), pltpu.VMEM((n, SSM_INNER), F32)],
        compiler_params=_params("arbitrary"), name=name)(dy, xbc, xbc, xbc, dt, cs, dtt, cst, st, dskip8, a_log8, dt_raw, dt_bias8)


def _gate_norm_fwd(y, proj, norm_g, *, name):
    t, c = y.shape
    tm = _pick(t, (256, 128))

    def body(y_ref, z_ref, g_ref, o_ref):
        z = z_ref[...]
        yz = y_ref[...] * z * _sigmoid(z)
        for g in range(SSM_GROUPS):
            gc = slice(g * GROUP_CH, (g + 1) * GROUP_CH)
            seg = yz[:, gc]
            r = lax.rsqrt(jnp.mean(seg * seg, axis=-1, keepdims=True) + RMS_EPS)
            o_ref[:, gc] = (seg * r * g_ref[:, gc]).astype(MXU_DTYPE)

    return pl.pallas_call(
        body, grid=(t // tm,), in_specs=[_row_spec(tm, c), _row_spec(tm, c, 1), _par_spec((1, c))],
        out_specs=_row_spec(tm, c), out_shape=jax.ShapeDtypeStruct((t, c), MXU_DTYPE),
        compiler_params=_params("parallel"), name=name)(y, proj, norm_g.reshape(1, c))


def _gate_norm_bwd(dyb, y, proj, norm_g, dproj, *, name):
    t, c = y.shape
    tm = _pick(t, (256, 128))

    def body(d_ref, y_ref, z_ref, g_ref, _, dy_ref, dz_ref, dg_ref):
        @pl.when(pl.program_id(0) == 0)
        def _():
            dg_ref[...] = jnp.zeros_like(dg_ref)

        z = z_ref[...]
        yv = y_ref[...]
        sz = _sigmoid(z)
        silu = z * sz
        yz = yv * silu
        dv = d_ref[...]
        for g in range(SSM_GROUPS):
            gc = slice(g * GROUP_CH, (g + 1) * GROUP_CH)
            seg = yz[:, gc]
            r = lax.rsqrt(jnp.mean(seg * seg, axis=-1, keepdims=True) + RMS_EPS)
            nrm = seg * r
            dn = dv[:, gc] * g_ref[:, gc]
            dg_ref[:, gc] += jnp.sum(dv[:, gc] * nrm, axis=0, keepdims=True)
            dyz = r * (dn - nrm * jnp.mean(dn * nrm, axis=-1, keepdims=True))
            dy_ref[:, gc] = dyz * silu[:, gc]
            dz_ref[:, gc] = (dyz * yv[:, gc] * (sz[:, gc] * (1.0 + z[:, gc] * (1.0 - sz[:, gc])))).astype(MXU_DTYPE)

    return pl.pallas_call(
        body, grid=(t // tm,), in_specs=[_row_spec(tm, c), _row_spec(tm, c), _row_spec(tm, c, 1), _par_spec((1, c)), _ANY],
        out_specs=[_row_spec(tm, c), _row_spec(tm, c, 1), _par_spec((1, c))],
        out_shape=[jax.ShapeDtypeStruct((t, c), F32), jax.ShapeDtypeStruct(dproj.shape, dproj.dtype),
                   jax.ShapeDtypeStruct((1, c), F32)],
        input_output_aliases={4: 1},
        compiler_params=_params("arbitrary"), name=name)(dyb, y, proj, norm_g.reshape(1, c), dproj)


GA_COLBLK = GAB_COL0 // D_MODEL


def _merge_fwd(br_a, br_b, proj, *, name):
    t, c = br_a.shape
    tm = _pick(t, (256, 128))

    def body(a_ref, b_ref, ga_ref, gb_ref, o_ref):
        o_ref[...] = (_sigmoid(ga_ref[...]) * a_ref[...] + _sigmoid(gb_ref[...]) * b_ref[...]).astype(MXU_DTYPE)

    return pl.pallas_call(
        body, grid=(t // tm,),
        in_specs=[_row_spec(tm, c), _row_spec(tm, c), _row_spec(tm, c, GA_COLBLK), _row_spec(tm, c, GA_COLBLK + 1)],
        out_specs=_row_spec(tm, c), out_shape=jax.ShapeDtypeStruct((t, c), MXU_DTYPE),
        compiler_params=_params("parallel"), name=name)(br_a, br_b, proj, proj)


def _merge_bwd(dm, br_a, br_b, proj, *, name):
    t, c = br_a.shape
    tm = _pick(t, (256, 128))

    def body(dm_ref, a_ref, b_ref, ga_ref, gb_ref, da_ref, db_ref, dg_ref):
        d = dm_ref[...]
        sa = _sigmoid(ga_ref[...])
        sb = _sigmoid(gb_ref[...])
        da_ref[...] = (d * sa).astype(MXU_DTYPE)
        db_ref[...] = (d * sb).astype(MXU_DTYPE)
        dg_ref[:, :c] = (d * a_ref[...] * sa * (1.0 - sa)).astype(MXU_DTYPE)
        dg_ref[:, c:] = (d * b_ref[...] * sb * (1.0 - sb)).astype(MXU_DTYPE)

    return pl.pallas_call(
        body, grid=(t // tm,),
        in_specs=[_row_spec(tm, c), _row_spec(tm, c), _row_spec(tm, c), _row_spec(tm, c, GA_COLBLK), _row_spec(tm, c, GA_COLBLK + 1)],
        out_specs=[_row_spec(tm, c), _row_spec(tm, c), _row_spec(tm, 2 * c, GAB_COL0 // (2 * c))],
        out_shape=[jax.ShapeDtypeStruct((t, c), MXU_DTYPE), jax.ShapeDtypeStruct((t, c), MXU_DTYPE),
                   jax.ShapeDtypeStruct((t, MAIN_COLS), MXU_DTYPE)],
        compiler_params=_params("parallel"), name=name)(dm, br_a, br_b, proj, proj)


def _layer_fwd(x, xb, memn_b, w, *, bsz, tag):
    nc = x.shape[0] // bsz // CHUNK
    sv = {"x_in": xb}
    proj = _mm(xb, w["w_main"], name=f"{tag}_proj")
    dt_raw = _mm(xb, w["w_dt"], name=f"{tag}_dtproj")
    sgo = _sg_fwd(proj, w["sg_ln_g"], w["sg_ln_b"], w["sg_w"], w["sg_bcol"], name=f"{tag}_sg_fwd")
    xbc = _conv_fwd(proj, w["conv_w"], w["conv_b"], bsz=bsz, name=f"{tag}_conv_fwd")
    dt, cs, dtt, cst = _ssd_prep(dt_raw, w["dt_bias8"], w["a_log8"], name=f"{tag}_ssd_prep")
    y, st = _ssd_fwd(xbc, dt, cs, dtt, cst, w["d_skip8"], nc=nc, name=f"{tag}_ssd_fwd")
    yb = _gate_norm_fwd(y, proj, w["ssm_norm_g"], name=f"{tag}_gate_norm_fwd")
    br_a = _mm(sgo, w["p_a"], name=f"{tag}_br_a")
    br_b = _mm(yb, w["p_b"], name=f"{tag}_br_b")
    merged = _merge_fwd(br_a, br_b, proj, name=f"{tag}_merge_fwd")
    mix = _mm(merged, w["w_mix_o"], name=f"{tag}_mix_o")
    x1, x1b, xh1, rs1 = _ln_fwd(x, mix, w["ln_g"][0], w["ln_b"][0], name=f"{tag}_ln1_fwd")
    sv.update(proj=proj, dt_raw=dt_raw, sgo=sgo, xbc=xbc, dt=dt, cs=cs, dtt=dtt, cst=cst, y=y, st=st, yb=yb,
              br_a=br_a, br_b=br_b, merged=merged, xh1=xh1, rs1=rs1, x1b=x1b)
    q = _mm(x1b, w["w_xq"], out_dtype=MXU_DTYPE, name=f"{tag}_q")
    kv = _mm(memn_b, w["w_xkv"], out_dtype=MXU_DTYPE, name=f"{tag}_kv")
    o = _attn_fwd(q, kv, bsz=bsz, name=f"{tag}_attn_fwd")
    att = _mm(o, w["w_xo"], name=f"{tag}_xo")
    x2, x2b, xh2, rs2 = _ln_fwd(x1, att, w["ln_g"][1], w["ln_b"][1], name=f"{tag}_ln2_fwd")
    sv.update(q=q, kv=kv, o=o, xh2=xh2, rs2=rs2, x2b=x2b)
    h = _mm(x2b, w["w_ffn_in"], name=f"{tag}_ffn_in")
    a = _swiglu_fwd(h, name=f"{tag}_swiglu_fwd")
    ffn = _mm(a, w["w_ffn_out"], name=f"{tag}_ffn_out")
    x3, x3b, xh3, rs3 = _ln_fwd(x2, ffn, w["ln_g"][2], w["ln_b"][2], name=f"{tag}_ln3_fwd")
    sv.update(h=h, a=a, xh3=xh3, rs3=rs3)
    return x3, x3b, sv


def _layer_bwd(dx3_addends, dx3_scales, memn_b, w, sv, *, bsz, tag):
    nc = sv["xh1"].shape[0] // bsz // CHUNK
    gr = {}
    dp3, dp3b, dg3, db3 = _ln_bwd(dx3_addends, dx3_scales, sv["xh3"], sv["rs3"], w["ln_g"][2], name=f"{tag}_ln3_bwd")
    da = _mm(dp3b, w["w_ffn_out"], tb=True, name=f"{tag}_d_a")
    gr["w_ffn_out"] = _mm(sv["a"], dp3b, ta=True, name=f"{tag}_dw_ffn_out")
    dh = _swiglu_bwd(sv["h"], da, name=f"{tag}_swiglu_bwd")
    gr["w_ffn_in"] = _mm(sv["x2b"], dh, ta=True, name=f"{tag}_dw_ffn_in")
    dx2_br = _mm(dh, w["w_ffn_in"], tb=True, name=f"{tag}_dx2")
    dp2, dp2b, dg2, db2 = _ln_bwd([dp3, dx2_br], [ALPHA, 1.0], sv["xh2"], sv["rs2"], w["ln_g"][1], name=f"{tag}_ln2_bwd")
    do = _mm(dp2b, w["w_xo"], tb=True, out_dtype=MXU_DTYPE, name=f"{tag}_d_o")
    gr["w_xo"] = _mm(sv["o"], dp2b, ta=True, name=f"{tag}_dw_xo")
    dq, dk, dv = _attn_bwd(sv["q"], sv["kv"], do, bsz=bsz, name=f"{tag}_attn_bwd")
    dkv = jnp.concatenate([dk, dv], axis=1)
    gr["w_xq"] = _mm(sv["x1b"], dq, ta=True, name=f"{tag}_dw_xq")
    gr["w_xkv"] = _mm(memn_b, dkv, ta=True, name=f"{tag}_dw_xkv")
    dmemn = _mm(dkv, w["w_xkv"], tb=True, name=f"{tag}_d_memn")
    dx1_br = _mm(dq, w["w_xq"], tb=True, name=f"{tag}_dx1")
    dp1, dp1b, dg1, db1 = _ln_bwd([dp2, dx1_br], [ALPHA, 1.0], sv["xh1"], sv["rs1"], w["ln_g"][0], name=f"{tag}_ln1_bwd")
    gr["ln_g"] = jnp.concatenate([dg1, dg2, dg3], axis=0)
    gr["ln_b"] = jnp.concatenate([db1, db2, db3], axis=0)
    dmerged = _mm(dp1b, w["w_mix_o"], tb=True, name=f"{tag}_d_merged")
    gr["w_mix_o"] = _mm(sv["merged"], dp1b, ta=True, name=f"{tag}_dw_mix_o")
    dbr_a, dbr_b, dproj = _merge_bwd(dmerged, sv["br_a"], sv["br_b"], sv["proj"], name=f"{tag}_merge_bwd")
    gr["p_a"] = _mm(sv["sgo"], dbr_a, ta=True, name=f"{tag}_dw_p_a")
    gr["p_b"] = _mm(sv["yb"], dbr_b, ta=True, name=f"{tag}_dw_p_b")
    dsgo = _mm(dbr_a, w["p_a"], tb=True, name=f"{tag}_d_sgo")
    dyb = _mm(dbr_b, w["p_b"], tb=True, name=f"{tag}_d_yb")
    dy, dproj, gr["ssm_norm_g"] = _gate_norm_bwd(dyb, sv["y"], sv["proj"], w["ssm_norm_g"], dproj, name=f"{tag}_gate_norm_bwd")
    dxbc, ddr, gr["a_log"], gr["d_skip"], gr["dt_bias"] = _ssd_bwd(
        dy, sv["xbc"], sv["dt"], sv["cs"], sv["dtt"], sv["cst"], sv["st"], w["d_skip8"], w["a_log8"], sv["dt_raw"],
        w["dt_bias8"], nc=nc, name=f"{tag}_ssd_bwd")
    dproj, gr["conv_w"], gr["conv_b"] = _conv_bwd(sv["proj"], dxbc, w["conv_w"], w["conv_b"], dproj, bsz=bsz, name=f"{tag}_conv_bwd")
    dproj, gr["sg_w"], dsg_bcol, gr["sg_ln_g"], gr["sg_ln_b"] = _sg_bwd(
        sv["proj"], dsgo, w["sg_ln_g"], w["sg_ln_b"], w["sg_w"], w["sg_bcol"], dproj, name=f"{tag}_sg_bwd")
    gr["sg_b"] = dsg_bcol[..., 0]
    gr["w_main"] = _mm(sv["x_in"], dproj, ta=True, name=f"{tag}_dw_main")
    gr["w_dt"] = _mm(sv["x_in"], ddr, ta=True, name=f"{tag}_dw_dt")
    dx_main = _mm(dproj, w["w_main"], tb=True, name=f"{tag}_dx_main")
    dx_dt = _mm(ddr, w["w_dt"], tb=True, name=f"{tag}_dx_dt")
    return [dp1, dx_main, dx_dt], [ALPHA, 1.0, 1.0], gr, dmemn


def _local_step(x, mem, tgt, mem_ln_g, mem_ln_b, layers):
    bsz, s, d = x.shape
    xf = x.reshape(bsz * s, d)
    memf = mem.reshape(-1, d)
    _, memn_b, mxh, mrs = _ln_fwd(memf, None, mem_ln_g, mem_ln_b, name="mem_ln_fwd")
    cur, curb, saved = xf, xf, []
    for li, w in enumerate(layers):
        cur, curb, sv = _layer_fwd(cur, curb, memn_b, w, bsz=bsz, tag=f"l{li}")
        saved.append(sv)
    dy, lsum = _loss_head(cur, tgt.reshape(bsz * s, d), name="loss_head")
    addends, scales = [dy], [1.0]
    grads, dmem = [None] * len(layers), []
    for li in reversed(range(len(layers))):
        addends, scales, grads[li], dm = _layer_bwd(addends, scales, memn_b, layers[li], saved[li], bsz=bsz, tag=f"l{li}")
        dmem.append(dm)
    grad_x = _add_scaled(addends, scales, name="grad_x").reshape(bsz, s, d)
    _, _, dmg, dmb = _ln_bwd(dmem, [1.0] * len(dmem), mxh, mrs, mem_ln_g, name="mem_ln_bwd")
    return lsum, grad_x, grads, dmg[0], dmb[0]


_ANY = pl.BlockSpec(memory_space=pl.ANY)
_MESH = pl.DeviceIdType.MESH


def _all_gather8(x, *, name):
    def body(x_ref, out_ref, send_sems, recv_sems, local_sem):
        mx, my, mc = lax.axis_index("x"), lax.axis_index("y"), lax.axis_index("c")
        me, sibling = (mx, my, mc), (mx, my, 1 - mc)
        chips = [(1 - mx, my), (mx, 1 - my), (1 - mx, 1 - my)]

        def blk(px, py, pc):
            return out_ref.at[4 * px + 2 * py + pc]

        def copy(k, block, to, src=None):
            return pltpu.make_async_remote_copy(
                src_ref=blk(*block) if src is None else src, dst_ref=blk(*block), send_sem=send_sems.at[k],
                recv_sem=recv_sems.at[k], device_id=to, device_id_type=_MESH)

        mine = pltpu.make_async_copy(x_ref, blk(*me), local_sem)
        mine.start()
        first = [copy(0, me, sibling, src=x_ref)]
        first += [copy(1 + j, me, (*chip, mc), src=x_ref) for j, chip in enumerate(chips)]
        for cp in first:
            cp.start()
        passed = [copy(4 + j, (*chip, mc), sibling) for j, chip in enumerate(chips)]
        for j, chip in enumerate(chips):
            copy(1 + j, (*chip, mc), me).wait_recv()
            passed[j].start()
        copy(0, sibling, me).wait_recv()
        for j, chip in enumerate(chips):
            copy(4 + j, (*chip, 1 - mc), me).wait_recv()
        for cp in first + passed:
            cp.wait_send()
        mine.wait()

    return pl.pallas_call(
        body, out_shape=jax.ShapeDtypeStruct((N_DEV,) + x.shape, x.dtype), in_specs=[_ANY], out_specs=_ANY,
        scratch_shapes=[pltpu.SemaphoreType.DMA((7,)), pltpu.SemaphoreType.DMA((7,)), pltpu.SemaphoreType.DMA],
        name=name)(x)


def _gather_params(arrs, kinds, *, name):
    n = len(arrs)

    def out_shape(a, kind):
        _, r, c = a.shape
        shp = {"row": (2, N_CHIPS * r, c), "col": (2, r, N_CHIPS * c), "chip": (2, N_CHIPS, r, c)}[kind]
        return jax.ShapeDtypeStruct(shp, a.dtype)

    def body(*refs):
        ins, outs = refs[:n], refs[n:2 * n]
        send_sems, recv_sems, local_sems = refs[2 * n:]
        mx, my, mc = lax.axis_index("x"), lax.axis_index("y"), lax.axis_index("c")
        me, sibling = (mx, my, mc), (mx, my, 1 - mc)
        chips = [(1 - mx, my), (mx, 1 - my), (1 - mx, 1 - my)]

        def blk(i, px, py, pc):
            _, r, c = arrs[i].shape
            j = 2 * px + py
            if kinds[i] == "row":
                return outs[i].at[pc, pl.ds(pl.multiple_of(j * r, r), r)]
            if kinds[i] == "col":
                return outs[i].at[pc, :, pl.ds(pl.multiple_of(j * c, c), c)]
            return outs[i].at[pc, j]

        def copy(i, k, block, to, src=None):
            return pltpu.make_async_remote_copy(
                src_ref=blk(i, *block) if src is None else src, dst_ref=blk(i, *block), send_sem=send_sems.at[7 * i + k],
                recv_sem=recv_sems.at[7 * i + k], device_id=to, device_id_type=_MESH)

        local, sent = [], []
        for i in range(n):
            own = ins[i].at[mc]
            mine = pltpu.make_async_copy(own, blk(i, *me), local_sems.at[i])
            mine.start()
            local.append(mine)
            first = [copy(i, 0, me, sibling, src=own)]
            first += [copy(i, 1 + j, me, (*chip, mc), src=own) for j, chip in enumerate(chips)]
            for cp in first:
                cp.start()
            sent += first
        for j, chip in enumerate(chips):
            for i in range(n):
                copy(i, 1 + j, (*chip, mc), me).wait_recv()
                fwd = copy(i, 4 + j, (*chip, mc), sibling)
                fwd.start()
                sent.append(fwd)
        for i in range(n):
            copy(i, 0, sibling, me).wait_recv()
            for j, chip in enumerate(chips):
                copy(i, 4 + j, (*chip, 1 - mc), me).wait_recv()
        for cp in sent:
            cp.wait_send()
        for cp in local:
            cp.wait()

    return pl.pallas_call(
        body, out_shape=[out_shape(a, k) for a, k in zip(arrs, kinds)], in_specs=[_ANY] * n, out_specs=[_ANY] * n,
        scratch_shapes=[pltpu.SemaphoreType.DMA((7 * n,)), pltpu.SemaphoreType.DMA((7 * n,)), pltpu.SemaphoreType.DMA((n,))],
        name=name)(*arrs)


def _half(r, h):
    return pl.ds(pl.multiple_of(h * (r // 2), r // 2), r // 2)


def _grads_to_sibling(gs, views, *, name):
    n = len(gs)

    def recv_shape(g, view):
        if view == "chip":
            return jax.ShapeDtypeStruct((g.shape[0], g.shape[1] // 2, g.shape[2]), g.dtype)
        return jax.ShapeDtypeStruct((g.shape[0] // 2, g.shape[1]), g.dtype)

    def body(*refs):
        ins, outs = refs[:n], refs[n:2 * n]
        send_sems, recv_sems = refs[2 * n:]
        mx, my, mc = lax.axis_index("x"), lax.axis_index("y"), lax.axis_index("c")
        copies = []
        for i in range(n):
            if views[i] == "chip":
                src = ins[i].at[:, _half(gs[i].shape[1], 1 - mc)]
            else:
                src = ins[i].at[_half(gs[i].shape[0], 1 - mc)]
            cp = pltpu.make_async_remote_copy(src_ref=src, dst_ref=outs[i], send_sem=send_sems.at[i], recv_sem=recv_sems.at[i],
                                              device_id=(mx, my, 1 - mc), device_id_type=_MESH)
            cp.start()
            copies.append(cp)
        for cp in copies:
            cp.wait()

    return pl.pallas_call(
        body, out_shape=[recv_shape(g, v) for g, v in zip(gs, views)], in_specs=[_ANY] * n, out_specs=[_ANY] * n,
        scratch_shapes=[pltpu.SemaphoreType.DMA((n,)), pltpu.SemaphoreType.DMA((n,))], name=name)(*gs)


def _grads_to_chips(pairs, views, *, name):
    n = len(pairs)

    def quad_shape(p, view):
        if view == "chip":
            return jax.ShapeDtypeStruct(p.shape, p.dtype)
        return jax.ShapeDtypeStruct((N_CHIPS, p.shape[0], p.shape[1] // N_CHIPS), p.dtype)

    def body(*refs):
        ins, outs = refs[:n], refs[n:2 * n]
        send_sems, recv_sems, local_sems = refs[2 * n:]
        mx, my, mc = lax.axis_index("x"), lax.axis_index("y"), lax.axis_index("c")
        me = 2 * mx + my
        chips = [(1 - mx, my), (mx, 1 - my), (1 - mx, 1 - my)]

        def blk(i, j):
            if views[i] == "chip":
                return ins[i].at[j]
            c = pairs[i].shape[1] // N_CHIPS
            return ins[i].at[:, pl.ds(pl.multiple_of(j * c, c), c)]

        copies = []
        for i in range(n):
            mine = pltpu.make_async_copy(blk(i, me), outs[i].at[me], local_sems.at[i])
            mine.start()
            copies.append(mine)
            for k, (px, py) in enumerate(chips):
                cp = pltpu.make_async_remote_copy(src_ref=blk(i, 2 * px + py), dst_ref=outs[i].at[me], send_sem=send_sems.at[3 * i + k],
                                                  recv_sem=recv_sems.at[3 * i + k], device_id=(px, py, mc), device_id_type=_MESH)
                cp.start()
                copies.append(cp)
        for cp in copies:
            cp.wait()

    return pl.pallas_call(
        body, out_shape=[quad_shape(p, v) for p, v in zip(pairs, views)], in_specs=[_ANY] * n, out_specs=[_ANY] * n,
        scratch_shapes=[pltpu.SemaphoreType.DMA((3 * n,)), pltpu.SemaphoreType.DMA((3 * n,)), pltpu.SemaphoreType.DMA((n,))],
        name=name)(*pairs)


def _grads_share(tots, *, name):
    n = len(tots)

    def body(*refs):
        ins, outs = refs[:n], refs[n:2 * n]
        send_sems, recv_sems, local_sems = refs[2 * n:]
        mx, my, mc = lax.axis_index("x"), lax.axis_index("y"), lax.axis_index("c")
        copies = []
        for i in range(n):
            dst = outs[i].at[:, _half(2 * tots[i].shape[1], mc)]
            mine = pltpu.make_async_copy(ins[i], dst, local_sems.at[i])
            mine.start()
            cp = pltpu.make_async_remote_copy(src_ref=ins[i], dst_ref=dst, send_sem=send_sems.at[i], recv_sem=recv_sems.at[i],
                                              device_id=(mx, my, 1 - mc), device_id_type=_MESH)
            cp.start()
            copies += [mine, cp]
        for cp in copies:
            cp.wait()

    return pl.pallas_call(
        body, out_shape=[jax.ShapeDtypeStruct((t.shape[0], 2 * t.shape[1], t.shape[2]), t.dtype) for t in tots],
        in_specs=[_ANY] * n, out_specs=[_ANY] * n,
        scratch_shapes=[pltpu.SemaphoreType.DMA((n,)), pltpu.SemaphoreType.DMA((n,)), pltpu.SemaphoreType.DMA((n,))],
        name=name)(*tots)


BLOCK_BYTES = 2 * 1024 * 1024


def _row_tile(rows, row_bytes):
    best = SUBLANE
    for tr in range(SUBLANE, rows + 1, SUBLANE):
        if rows % tr == 0 and tr * row_bytes <= BLOCK_BYTES:
            best = tr
    return best


def _pair_sum(g, recv, view, c_idx, *, name):
    def body(c_ref, a_ref, b_ref, o_ref):
        o_ref[...] = a_ref[...] + b_ref[...]

    if view == "chip":
        nch, r, c = g.shape
        tr = _row_tile(r // 2, c * 4)
        gv = g.reshape(nch, 2, r // 2, c)
        grid = (nch, (r // 2) // tr)
        in_specs = [pl.BlockSpec((None, None, tr, c), lambda j, i, c_ref: (j, c_ref[0], i, 0)),
                    pl.BlockSpec((None, tr, c), lambda j, i, c_ref: (j, i, 0))]
        out_spec = pl.BlockSpec((None, tr, c), lambda j, i, c_ref: (j, i, 0))
        sem = ("parallel", "parallel")
    else:
        r, c4 = g.shape
        tr = _row_tile(r // 2, c4 * 4)
        gv = g.reshape(2, r // 2, c4)
        grid = ((r // 2) // tr,)
        in_specs = [pl.BlockSpec((None, tr, c4), lambda i, c_ref: (c_ref[0], i, 0)), pl.BlockSpec((tr, c4), lambda i, c_ref: (i, 0))]
        out_spec = pl.BlockSpec((tr, c4), lambda i, c_ref: (i, 0))
        sem = ("parallel",)
    grid_spec = pltpu.PrefetchScalarGridSpec(num_scalar_prefetch=1, grid=grid, in_specs=in_specs, out_specs=out_spec)
    return pl.pallas_call(body, grid_spec=grid_spec, out_shape=jax.ShapeDtypeStruct(recv.shape, recv.dtype),
                          compiler_params=_params(*sem), name=name)(c_idx, gv, recv)


def _quad_sum(quads, *, name):
    nl = len(quads)
    nch, rh, c = quads[0].shape
    tr = _row_tile(rh, nch * c * 4)

    def body(*refs):
        o_ref = refs[nl]
        for l in range(nl):
            acc = refs[l][0]
            for j in range(1, nch):
                acc = acc + refs[l][j]
            o_ref[l] = acc

    return pl.pallas_call(
        body, grid=(rh // tr,), in_specs=[pl.BlockSpec((nch, tr, c), lambda i: (0, i, 0))] * nl,
        out_specs=pl.BlockSpec((nl, tr, c), lambda i: (0, i, 0)), out_shape=jax.ShapeDtypeStruct((nl, rh, c), quads[0].dtype),
        compiler_params=_params("parallel"), name=name)(*quads)


def _sum_leading(a, tr, *, name):
    k, rows, cols = a.shape

    def body(a_ref, o_ref):
        acc = a_ref[0]
        for i in range(1, k):
            acc = acc + a_ref[i]
        o_ref[...] = acc

    return pl.pallas_call(
        body, grid=(rows // tr,), in_specs=[pl.BlockSpec((k, tr, cols), lambda i: (0, i, 0))],
        out_specs=pl.BlockSpec((tr, cols), lambda i: (i, 0)), out_shape=jax.ShapeDtypeStruct((rows, cols), a.dtype),
        compiler_params=_params("parallel"), name=name)(a)


def _adamw(w, g, m, v, *, name):
    rows, cols = w.shape
    tr = rows
    for cand in (256, 128, 64, 32, 16, 8):
        if rows % cand == 0 and cand * cols <= 512 * 1024:
            tr = cand
            break
    c1 = 1.0 - ADAM_B1 ** ADAM_STEP
    c2 = 1.0 - ADAM_B2 ** ADAM_STEP

    def body(w_ref, g_ref, m_ref, v_ref, d_ref, nm_ref, nv_ref):
        gv = g_ref[...]
        nm = ADAM_B1 * m_ref[...] + (1.0 - ADAM_B1) * gv
        nv = ADAM_B2 * v_ref[...] + (1.0 - ADAM_B2) * (gv * gv)
        d_ref[...] = -ADAM_LR * ((nm / c1) / (jnp.sqrt(nv / c2) + ADAM_EPS) + ADAM_WD * w_ref[...])
        nm_ref[...] = nm
        nv_ref[...] = nv

    spec = pl.BlockSpec((tr, cols), lambda i: (i, 0))
    shp = jax.ShapeDtypeStruct((rows, cols), F32)
    return pl.pallas_call(body, grid=(rows // tr,), in_specs=[spec] * 4, out_specs=[spec] * 3, out_shape=[shp] * 3,
                          compiler_params=_params("parallel"), name=name)(w, g, m, v)


WEIGHTS = ["mem_ln_g", "mem_ln_b", "w_in", "sg_ln_g", "sg_ln_b", "sg_w", "sg_b", "conv_w", "conv_b", "dt_bias", "a_log",
           "d_skip", "ssm_norm_g", "p_a", "p_b", "w_mix_o", "w_xq", "w_xkv", "w_xo", "w_ffn_in", "w_ffn_out", "ln_g", "ln_b"]
ARG_NAMES = ["x", "mem"] + WEIGHTS + ["loss_target"] + ["m_" + n for n in WEIGHTS] + ["v_" + n for n in WEIGHTS]
BIG = {"w_in": (1, (1024, 9248)), "p_a": (0, (1024, 1024)), "p_b": (0, (2048, 1024)), "w_mix_o": (0, (1024, 1024)),
       "w_xq": (0, (1024, 1024)), "w_xkv": (1, (1024, 2048)), "w_xo": (0, (1024, 1024)), "w_ffn_in": (1, (1024, 5632)),
       "w_ffn_out": (0, (2816, 1024))}
SMALL_SHARDED = {"conv_w": (4, 3072), "ln_g": (3, 1024), "ln_b": (3, 1024)}
SMALL = [n for n in WEIGHTS if n not in BIG]
XBC_IN0, DT_COL0, DT_COL1 = 4096, 7168, 7200
GATHER_KIND = {"w_in": "chip", "p_a": "row", "p_b": "row", "w_mix_o": "row", "w_xq": "row", "w_xkv": "col", "w_xo": "row",
               "w_ffn_in": "col", "w_ffn_out": "row", "conv_w": "chip", "ln_g": "chip", "ln_b": "chip"}
GRAD_VIEW = {n: ("col" if k == "col" else "chip") for n, k in GATHER_KIND.items() if n in BIG}


def _shard_shape(name):
    axis, (r, c) = BIG[name]
    return (r // N_CHIPS, c) if axis == 0 else (r, c // N_CHIPS)


def _pad_rows(flat, cols, row_mult):
    n = flat.shape[0]
    rows = -(-n // cols)
    rows = -(-rows // row_mult) * row_mult
    return jnp.pad(flat, (0, rows * cols - n)).reshape(rows, cols)


def _gather_weights(a, c_idx):
    names = list(BIG) + list(SMALL_SHARDED)
    arrs = [a[n].astype(MXU_DTYPE) for n in BIG] + [a[n] for n in SMALL_SHARDED]
    outs = _gather_params(arrs, [GATHER_KIND[n] for n in names], name="gather_weights")
    full = dict(zip(names, outs))
    for n in names:
        if GATHER_KIND[n] == "chip":
            _, _, r, c = full[n].shape
            full[n] = jnp.transpose(full[n], (0, 2, 1, 3)).reshape(DEPTH, r, N_CHIPS * c)
    return full


def _layer_weights(a, full, l):
    w_in = full["w_in"][l]
    w = {n: (full[n], l) for n in BIG if n != "w_in"}
    w["w_main"] = jnp.concatenate([w_in[:, :XBC_IN0], w_in[:, DT_COL1:], w_in[:, XBC_IN0:DT_COL0]], axis=1)
    w["w_dt"] = jnp.pad(w_in[:, DT_COL0:DT_COL1], ((0, 0), (0, HEAD_PAD - SSM_HEADS)))
    for n in SMALL_SHARDED:
        w[n] = full[n][l]
    for n in ["sg_ln_g", "sg_ln_b", "sg_w", "conv_b", "ssm_norm_g"]:
        w[n] = a[n][l]
    w["sg_bcol"] = a["sg_b"][l][..., None]
    for n in ["dt_bias", "a_log", "d_skip"]:
        w[n + "8"] = _pad_heads(a[n][l])
    return w


def _reduce_big_grads(grads, c_idx):
    gs, views, keys = [], [], []
    for n in BIG:
        axis, _ = BIG[n]
        r, c = _shard_shape(n)
        for l in range(DEPTH):
            if n == "w_in":
                gm, gd = grads[l]["w_main"], grads[l]["w_dt"]
                gfull = jnp.concatenate([gm[:, :XBC_IN0], gm[:, XBC_COL0:], gd[:, :SSM_HEADS], gm[:, GAB_COL0:XBC_COL0]], axis=1)
                g = jnp.transpose(gfull.reshape(r, N_CHIPS, c), (1, 0, 2))
            elif axis == 0:
                g = grads[l][n].reshape(N_CHIPS, r, c)
            else:
                g = grads[l][n]
            gs.append(g)
            views.append(GRAD_VIEW[n])
            keys.append((n, l))
    recv = _grads_to_sibling(gs, views, name="grads_to_sibling")
    cpre = c_idx.reshape(1)
    pairs = [_pair_sum(g, rv, v, cpre, name=f"grads_pair_sum_{n}_{l}") for g, rv, v, (n, l) in zip(gs, recv, views, keys)]
    quads = _grads_to_chips(pairs, views, name="grads_to_chips")
    tots = [_quad_sum(quads[DEPTH * i:DEPTH * (i + 1)], name=f"grads_chip_sum_{n}") for i, n in enumerate(BIG)]
    both = _grads_share(tots, name="grads_share")
    return dict(zip(BIG, both))


def _reduce_small_grads(small, chip):
    names = list(small)
    flat = jnp.concatenate([small[n].reshape(-1) for n in names])
    packed = _pad_rows(flat, LANE, SUBLANE)
    g8 = _all_gather8(packed, name="gather_small_grads")
    tot = _sum_leading(g8, packed.shape[0], name="small_grads_sum").reshape(-1)
    out, off = {}, 0
    for n in names:
        sz = small[n].size
        full = tot[off:off + sz].reshape(small[n].shape)
        off += sz
        if n in SMALL_SHARDED:
            cs = SMALL_SHARDED[n][1] // N_CHIPS
            full = lax.dynamic_slice_in_dim(full, chip * cs, cs, axis=-1)
        out[n] = full
    return out


def _view2d(v):
    if v.ndim >= 2:
        return v.reshape(-1, v.shape[-1])
    return v.reshape(1, -1)


def kernel(x, mem, mem_ln_g, mem_ln_b, w_in, sg_ln_g, sg_ln_b, sg_w, sg_b, conv_w, conv_b, dt_bias, a_log, d_skip, ssm_norm_g, p_a, p_b, w_mix_o, w_xq, w_xkv, w_xo, w_ffn_in, w_ffn_out, ln_g, ln_b, loss_target, m_mem_ln_g, m_mem_ln_b, m_w_in, m_sg_ln_g, m_sg_ln_b, m_sg_w, m_sg_b, m_conv_w, m_conv_b, m_dt_bias, m_a_log, m_d_skip, m_ssm_norm_g, m_p_a, m_p_b, m_w_mix_o, m_w_xq, m_w_xkv, m_w_xo, m_w_ffn_in, m_w_ffn_out, m_ln_g, m_ln_b, v_mem_ln_g, v_mem_ln_b, v_w_in, v_sg_ln_g, v_sg_ln_b, v_sg_w, v_sg_b, v_conv_w, v_conv_b, v_dt_bias, v_a_log, v_d_skip, v_ssm_norm_g, v_p_a, v_p_b, v_w_mix_o, v_w_xq, v_w_xkv, v_w_xo, v_w_ffn_in, v_w_ffn_out, v_ln_g, v_ln_b):
    a = dict(zip(ARG_NAMES, (x, mem, mem_ln_g, mem_ln_b, w_in, sg_ln_g, sg_ln_b, sg_w, sg_b, conv_w, conv_b, dt_bias, a_log, d_skip, ssm_norm_g, p_a, p_b, w_mix_o, w_xq, w_xkv, w_xo, w_ffn_in, w_ffn_out, ln_g, ln_b, loss_target, m_mem_ln_g, m_mem_ln_b, m_w_in, m_sg_ln_g, m_sg_ln_b, m_sg_w, m_sg_b, m_conv_w, m_conv_b, m_dt_bias, m_a_log, m_d_skip, m_ssm_norm_g, m_p_a, m_p_b, m_w_mix_o, m_w_xq, m_w_xkv, m_w_xo, m_w_ffn_in, m_w_ffn_out, m_ln_g, m_ln_b, v_mem_ln_g, v_mem_ln_b, v_w_in, v_sg_ln_g, v_sg_ln_b, v_sg_w, v_sg_b, v_conv_w, v_conv_b, v_dt_bias, v_a_log, v_d_skip, v_ssm_norm_g, v_p_a, v_p_b, v_w_mix_o, v_w_xq, v_w_xkv, v_w_xo, v_w_ffn_in, v_w_ffn_out, v_ln_g, v_ln_b)))
    c_idx = lax.axis_index("c").astype(jnp.int32)
    chip = (2 * lax.axis_index("x") + lax.axis_index("y")).astype(jnp.int32)

    full = _gather_weights(a, c_idx)
    layers = [_layer_weights(a, full, l) for l in range(DEPTH)]
    lsum, grad_x, grads, d_mem_g, d_mem_b = _local_step(x, mem, loss_target, mem_ln_g, mem_ln_b, layers)
    loss = lax.psum(0.5 * jnp.sum(lsum) / D_MODEL, ("x", "y", "c"))

    gw = _reduce_big_grads(grads, c_idx)
    small = {"mem_ln_g": d_mem_g, "mem_ln_b": d_mem_b}
    for n in SMALL:
        if n in small:
            continue
        per_layer = []
        for l in range(DEPTH):
            g = grads[l][n]
            if n in ("dt_bias", "a_log", "d_skip"):
                g = g[0, :SSM_HEADS]
            per_layer.append(g.reshape(a[n].shape[1:-1] + (-1,)))
        small[n] = jnp.stack(per_layer)
    gw.update(_reduce_small_grads(small, chip))

    delta, new_m, new_v = {}, {}, {}
    for n in BIG:
        shp = a[n].shape
        d, nm, nv = _adamw(_view2d(a[n]), _view2d(gw[n]), _view2d(a["m_" + n]), _view2d(a["v_" + n]), name=f"adamw_{n}")
        delta[n], new_m[n], new_v[n] = d.reshape(shp), nm.reshape(shp), nv.reshape(shp)
    packs = [_pad_rows(jnp.concatenate([src(n).reshape(-1) for n in SMALL]), LANE, SUBLANE)
             for src in (lambda n: a[n], lambda n: gw[n], lambda n: a["m_" + n], lambda n: a["v_" + n])]
    outs = _adamw(*packs, name="adamw_small")
    off = 0
    for n in SMALL:
        sz, shp = a[n].size, a[n].shape
        delta[n], new_m[n], new_v[n] = (o.reshape(-1)[off:off + sz].reshape(shp) for o in outs)
        off += sz
    return (loss, grad_x, *[gw[n].reshape(a[n].shape) for n in WEIGHTS], *[delta[n] for n in WEIGHTS],
            *[new_m[n] for n in WEIGHTS], *[new_v[n] for n in WEIGHTS])
```

```python
import functools
import math

import jax
import jax.numpy as jnp
from jax import lax
from jax.experimental import pallas as pl
from jax.experimental.pallas import tpu as pltpu

F32 = jnp.float32
MXU_DTYPE = jnp.bfloat16
HIGHEST = lax.Precision.HIGHEST

D_MODEL = 1024
DEPTH = 2
CHUNK = 128
SG_GROUPS = 8
SSM_INNER = 2048
SSM_HEADDIM = 64
SSM_HEADS = 32
SSM_STATE = 128
SSM_GROUPS = 4
SSM_CONV = 4
SSM_CONV_DIM = 3072
X_HEADS = 4
X_HEADDIM = 256
FFN_HIDDEN = 2816
ALPHA = float((2 * DEPTH) ** 0.25)
LN_EPS = 1e-5
RMS_EPS = 1e-5
ADAM_LR = 0.001
ADAM_B1 = 0.9
ADAM_B2 = 0.999
ADAM_EPS = 1e-08
ADAM_WD = 0.01
ADAM_STEP = 10

MAIN_COLS = 9216
UVZ_COLS = 4096
GAB_COL0 = 4096
XBC_COL0 = 6144
HEAD_PAD = 128

VMEM_LIMIT = 56 * 1024 * 1024
BLOCK_BYTES = 1024 * 1024
LANE = 128
SUBLANE = 8

N_CHIPS = 4
N_DEV = 8


def _pick(n, cands):
    for c in cands:
        if n % c == 0:
            return c
    return n


def _params(*sem):
    return pltpu.CompilerParams(dimension_semantics=tuple(sem), vmem_limit_bytes=VMEM_LIMIT)


_ANY = pl.BlockSpec(memory_space=pl.ANY)
_MESH = pl.DeviceIdType.MESH


def _nt(a, b):
    return lax.dot_general(a, b, (((1,), (1,)), ((), ())), preferred_element_type=F32)


def _tn(a, b):
    return lax.dot_general(a, b, (((0,), (0,)), ((), ())), preferred_element_type=F32)


def _nn(a, b):
    return jnp.dot(a, b, preferred_element_type=F32)


def _sigmoid(x):
    return 1.0 / (1.0 + jnp.exp(-x))


def _gelu(x):
    return 0.5 * x * (1.0 + lax.erf(x * (2.0 ** -0.5)))


def _gelu_grad(x):
    return 0.5 * (1.0 + lax.erf(x * (2.0 ** -0.5))) + x * jnp.exp(-0.5 * x * x) * (1.0 / math.sqrt(2.0 * math.pi))


def _mm(a, b, *, ta=False, tb=False, out_dtype=F32, name):
    b, bl = b if isinstance(b, tuple) else (b, None)
    if ta:
        kdim, m = a.shape
    else:
        m, kdim = a.shape
    if tb:
        n, k2 = b.shape[-2:]
    else:
        k2, n = b.shape[-2:]
    assert kdim == k2, (a.shape, b.shape, ta, tb)
    tm = _pick(m, (1024, 512, 256, 128))
    tn = _pick(n, (1024, 512, 256, 128))
    tk = _pick(kdim, (512, 256, 128))
    nk = kdim // tk
    dn = (((0 if ta else 1,), (1 if tb else 0,)), ((), ()))

    def body(a_ref, b_ref, o_ref, acc_ref):
        k = pl.program_id(2)

        @pl.when(k == 0)
        def _():
            acc_ref[...] = jnp.zeros_like(acc_ref)

        acc_ref[...] += lax.dot_general(a_ref[...].astype(MXU_DTYPE), b_ref[...].astype(MXU_DTYPE), dn,
                                        preferred_element_type=F32)

        @pl.when(k == nk - 1)
        def _():
            o_ref[...] = acc_ref[...].astype(out_dtype)

    a_spec = pl.BlockSpec((tk, tm), lambda i, j, k: (k, i)) if ta else pl.BlockSpec((tm, tk), lambda i, j, k: (i, k))
    if bl is None:
        b_spec = pl.BlockSpec((tn, tk), lambda i, j, k: (j, k)) if tb else pl.BlockSpec((tk, tn), lambda i, j, k: (k, j))
    elif tb:
        b_spec = pl.BlockSpec((None, tn, tk), lambda i, j, k: (bl, j, k))
    else:
        b_spec = pl.BlockSpec((None, tk, tn), lambda i, j, k: (bl, k, j))
    return pl.pallas_call(
        body, grid=(m // tm, n // tn, nk), in_specs=[a_spec, b_spec],
        out_specs=pl.BlockSpec((tm, tn), lambda i, j, k: (i, j)),
        out_shape=jax.ShapeDtypeStruct((m, n), out_dtype),
        scratch_shapes=[pltpu.VMEM((tm, tn), F32)],
        compiler_params=_params("parallel", "parallel", "arbitrary"), name=name)(a, b)


def _row_spec(tm, c, col=0):
    return pl.BlockSpec((tm, c), lambda i: (i, col))


def _par_spec(shape):
    nd = len(shape)
    return pl.BlockSpec(shape, lambda i: (0,) * nd)


def _ln_fwd(x, f, g, b, *, name):
    t, c = x.shape
    tm = _pick(t, (256, 128))
    has_f = f is not None

    def body(*refs):
        if has_f:
            x_ref, f_ref, g_ref, b_ref, y_ref, yb_ref, xh_ref, rs_ref = refs
            r = ALPHA * x_ref[...] + f_ref[...]
        else:
            x_ref, g_ref, b_ref, y_ref, yb_ref, xh_ref, rs_ref = refs
            r = x_ref[...]
        mu = jnp.mean(r, axis=-1, keepdims=True)
        xc = r - mu
        var = jnp.mean(xc * xc, axis=-1, keepdims=True)
        rstd = lax.rsqrt(var + LN_EPS)
        xh = xc * rstd
        y = xh * g_ref[...] + b_ref[...]
        y_ref[...] = y
        yb_ref[...] = y.astype(MXU_DTYPE)
        xh_ref[...] = xh
        rs_ref[...] = rstd

    ins = [x] + ([f] if has_f else []) + [g.reshape(1, c), b.reshape(1, c)]
    in_specs = [_row_spec(tm, c)] * (2 if has_f else 1) + [_par_spec((1, c))] * 2
    return pl.pallas_call(
        body, grid=(t // tm,), in_specs=in_specs,
        out_specs=[_row_spec(tm, c), _row_spec(tm, c), _row_spec(tm, c), _row_spec(tm, 1)],
        out_shape=[jax.ShapeDtypeStruct((t, c), F32), jax.ShapeDtypeStruct((t, c), MXU_DTYPE),
                   jax.ShapeDtypeStruct((t, c), F32), jax.ShapeDtypeStruct((t, 1), F32)],
        compiler_params=_params("parallel"), name=name)(*ins)


def _ln_bwd(addends, scales, xh, rs, g, *, name):
    t, c = xh.shape
    tm = _pick(t, (256, 128))
    na = len(addends)

    def body(*refs):
        a_refs = refs[:na]
        xh_ref, rs_ref, g_ref, dp_ref, dpb_ref, dg_ref, db_ref = refs[na:]

        @pl.when(pl.program_id(0) == 0)
        def _():
            dg_ref[...] = jnp.zeros_like(dg_ref)
            db_ref[...] = jnp.zeros_like(db_ref)

        dy = None
        for s, r in zip(scales, a_refs):
            term = r[...] if s == 1.0 else s * r[...]
            dy = term if dy is None else dy + term
        xhv = xh_ref[...]
        dxh = dy * g_ref[...]
        m1 = jnp.mean(dxh, axis=-1, keepdims=True)
        m2 = jnp.mean(dxh * xhv, axis=-1, keepdims=True)
        dp = rs_ref[...] * (dxh - m1 - xhv * m2)
        dp_ref[...] = dp
        dpb_ref[...] = dp.astype(MXU_DTYPE)
        dg_ref[...] += jnp.sum(dy * xhv, axis=0, keepdims=True)
        db_ref[...] += jnp.sum(dy, axis=0, keepdims=True)

    in_specs = [_row_spec(tm, c)] * (na + 1) + [_row_spec(tm, 1), _par_spec((1, c))]
    return pl.pallas_call(
        body, grid=(t // tm,), in_specs=in_specs,
        out_specs=[_row_spec(tm, c), _row_spec(tm, c), _par_spec((1, c)), _par_spec((1, c))],
        out_shape=[jax.ShapeDtypeStruct((t, c), F32), jax.ShapeDtypeStruct((t, c), MXU_DTYPE),
                   jax.ShapeDtypeStruct((1, c), F32), jax.ShapeDtypeStruct((1, c), F32)],
        compiler_params=_params("arbitrary"), name=name)(*addends, xh, rs, g.reshape(1, c))


def _add_scaled(addends, scales, *, name):
    t, c = addends[0].shape
    tm = _pick(t, (256, 128))
    na = len(addends)

    def body(*refs):
        acc = None
        for s, r in zip(scales, refs[:na]):
            term = r[...] if s == 1.0 else s * r[...]
            acc = term if acc is None else acc + term
        refs[na][...] = acc

    return pl.pallas_call(
        body, grid=(t // tm,), in_specs=[_row_spec(tm, c)] * na, out_specs=_row_spec(tm, c),
        out_shape=jax.ShapeDtypeStruct((t, c), F32), compiler_params=_params("parallel"), name=name)(*addends)


def _loss_head(y, tgt, *, name):
    t, c = y.shape
    tm = _pick(t, (256, 128))

    def body(y_ref, t_ref, dy_ref, ls_ref):
        @pl.when(pl.program_id(0) == 0)
        def _():
            ls_ref[...] = jnp.zeros_like(ls_ref)

        e = y_ref[...] - t_ref[...]
        dy_ref[...] = e * (1.0 / c)
        ls_ref[...] += jnp.sum(e * e, axis=0, keepdims=True)

    return pl.pallas_call(
        body, grid=(t // tm,), in_specs=[_row_spec(tm, c)] * 2,
        out_specs=[_row_spec(tm, c), _par_spec((1, c))],
        out_shape=[jax.ShapeDtypeStruct((t, c), F32), jax.ShapeDtypeStruct((1, c), F32)],
        compiler_params=_params("arbitrary"), name=name)(y, tgt)


def _swiglu_fwd(h, *, name):
    t, two_f = h.shape
    fh = two_f // 2
    tm = _pick(t, (256, 128))

    def body(g_ref, u_ref, a_ref):
        g = g_ref[...]
        a_ref[...] = (g * _sigmoid(g) * u_ref[...]).astype(MXU_DTYPE)

    return pl.pallas_call(
        body, grid=(t // tm,), in_specs=[_row_spec(tm, fh, 0), _row_spec(tm, fh, 1)], out_specs=_row_spec(tm, fh),
        out_shape=jax.ShapeDtypeStruct((t, fh), MXU_DTYPE), compiler_params=_params("parallel"), name=name)(h, h)


def _swiglu_bwd(h, da, *, name):
    t, two_f = h.shape
    fh = two_f // 2
    tm = _pick(t, (256, 128))

    def body(g_ref, u_ref, da_ref, dh_ref):
        g = g_ref[...]
        s = _sigmoid(g)
        dav = da_ref[...]
        dh_ref[:, :fh] = (dav * u_ref[...] * (s * (1.0 + g * (1.0 - s)))).astype(MXU_DTYPE)
        dh_ref[:, fh:] = (dav * g * s).astype(MXU_DTYPE)

    return pl.pallas_call(
        body, grid=(t // tm,), in_specs=[_row_spec(tm, fh, 0), _row_spec(tm, fh, 1), _row_spec(tm, fh)],
        out_specs=_row_spec(tm, two_f), out_shape=jax.ShapeDtypeStruct((t, two_f), MXU_DTYPE),
        compiler_params=_params("parallel"), name=name)(h, h, da)


def _attn_probs(q, k):
    s = _nt(q, k) * (X_HEADDIM ** -0.5)
    s = s - jnp.max(s, axis=-1, keepdims=True)
    p = jnp.exp(s)
    return p / jnp.sum(p, axis=-1, keepdims=True)


def _attn_fwd(q, kv, *, bsz, name):
    t = q.shape[0]
    s = t // bsz
    ml = kv.shape[0] // bsz
    hd = X_HEADDIM

    def body(q_ref, k_ref, v_ref, o_ref):
        p = _attn_probs(q_ref[...], k_ref[...])
        o_ref[...] = _nn(p.astype(MXU_DTYPE), v_ref[...]).astype(MXU_DTYPE)

    return pl.pallas_call(
        body, grid=(bsz, X_HEADS),
        in_specs=[pl.BlockSpec((s, hd), lambda b, h: (b, h)), pl.BlockSpec((ml, hd), lambda b, h: (b, h)),
                  pl.BlockSpec((ml, hd), lambda b, h: (b, X_HEADS + h))],
        out_specs=pl.BlockSpec((s, hd), lambda b, h: (b, h)),
        out_shape=jax.ShapeDtypeStruct((t, D_MODEL), MXU_DTYPE),
        compiler_params=_params("parallel", "parallel"), name=name)(q, kv, kv)


def _attn_bwd(q, kv, do, *, bsz, name):
    t = q.shape[0]
    s = t // bsz
    ml = kv.shape[0] // bsz
    hd = X_HEADDIM

    def body(q_ref, k_ref, v_ref, do_ref, dq_ref, dk_ref, dv_ref):
        qv, kk, vv, dov = q_ref[...], k_ref[...], v_ref[...], do_ref[...]
        p = _attn_probs(qv, kk)
        dp = _nt(dov, vv)
        dv_ref[...] = _tn(p.astype(MXU_DTYPE), dov).astype(MXU_DTYPE)
        ds = (p * (dp - jnp.sum(dp * p, axis=-1, keepdims=True)) * (X_HEADDIM ** -0.5)).astype(MXU_DTYPE)
        dq_ref[...] = _nn(ds, kk).astype(MXU_DTYPE)
        dk_ref[...] = _tn(ds, qv).astype(MXU_DTYPE)

    blk_q = pl.BlockSpec((s, hd), lambda b, h: (b, h))
    blk_m = pl.BlockSpec((ml, hd), lambda b, h: (b, h))
    return pl.pallas_call(
        body, grid=(bsz, X_HEADS),
        in_specs=[blk_q, blk_m, pl.BlockSpec((ml, hd), lambda b, h: (b, X_HEADS + h)), blk_q],
        out_specs=[blk_q, blk_m, blk_m],
        out_shape=[jax.ShapeDtypeStruct((t, D_MODEL), MXU_DTYPE), jax.ShapeDtypeStruct((bsz * ml, D_MODEL), MXU_DTYPE),
                   jax.ShapeDtypeStruct((bsz * ml, D_MODEL), MXU_DTYPE)],
        compiler_params=_params("parallel", "parallel"), name=name)(q, kv, kv, do)


def _causal(n):
    row = lax.broadcasted_iota(jnp.int32, (n, n), 0)
    col = lax.broadcasted_iota(jnp.int32, (n, n), 1)
    return row >= col


def _sg_norm(v, g, b):
    gv = _gelu(v)
    mu = jnp.mean(gv, axis=-1, keepdims=True)
    xc = gv - mu
    var = jnp.mean(xc * xc, axis=-1, keepdims=True)
    rstd = lax.rsqrt(var + LN_EPS)
    xh = xc * rstd
    return xh, rstd, xh * g + b


def _sg_fwd(proj, ln_g, ln_b, w, bcol, *, name):
    t = proj.shape[0]
    c = D_MODEL
    gd = c // SG_GROUPS

    def body(u_ref, v_ref, g_ref, b_ref, w_ref, bc_ref, o_ref):
        gu = _gelu(u_ref[...])
        _, _, vn = _sg_norm(v_ref[...], g_ref[...], b_ref[...])
        mask = _causal(CHUNK)
        for g in range(SG_GROUPS):
            sl = slice(g * gd, (g + 1) * gd)
            wg = jnp.where(mask, w_ref[g], 0.0).astype(MXU_DTYPE)
            mixed = _nn(wg, vn[:, sl].astype(MXU_DTYPE)) + bc_ref[g]
            o_ref[:, sl] = (gu[:, sl] * mixed).astype(MXU_DTYPE)

    return pl.pallas_call(
        body, grid=(t // CHUNK,),
        in_specs=[_row_spec(CHUNK, c, 0), _row_spec(CHUNK, c, 1), _par_spec((1, c)), _par_spec((1, c)),
                  _par_spec((SG_GROUPS, CHUNK, CHUNK)), _par_spec((SG_GROUPS, CHUNK, 1))],
        out_specs=_row_spec(CHUNK, c), out_shape=jax.ShapeDtypeStruct((t, c), MXU_DTYPE),
        compiler_params=_params("parallel"), name=name)(proj, proj, ln_g.reshape(1, c), ln_b.reshape(1, c), w, bcol)


def _sg_bwd(proj, dsgo, ln_g, ln_b, w, bcol, dproj, *, name):
    t = proj.shape[0]
    c = D_MODEL
    gd = c // SG_GROUPS

    def body(u_ref, v_ref, d_ref, g_ref, b_ref, w_ref, bc_ref, _, duv_ref, dw_ref, dbc_ref, dg_ref, db_ref, dvn_ref):
        @pl.when(pl.program_id(0) == 0)
        def _():
            dw_ref[...] = jnp.zeros_like(dw_ref)
            dbc_ref[...] = jnp.zeros_like(dbc_ref)
            dg_ref[...] = jnp.zeros_like(dg_ref)
            db_ref[...] = jnp.zeros_like(db_ref)

        u = u_ref[...]
        v = v_ref[...]
        dso = d_ref[...]
        gu = _gelu(u)
        xh, rstd, vn = _sg_norm(v, g_ref[...], b_ref[...])
        mask = _causal(CHUNK)
        for g in range(SG_GROUPS):
            sl = slice(g * gd, (g + 1) * gd)
            wg = jnp.where(mask, w_ref[g], 0.0).astype(MXU_DTYPE)
            vng = vn[:, sl].astype(MXU_DTYPE)
            mixed = _nn(wg, vng) + bc_ref[g]
            duv_ref[:, sl] = (dso[:, sl] * mixed * _gelu_grad(u[:, sl])).astype(MXU_DTYPE)
            dmix = dso[:, sl] * gu[:, sl]
            dmb = dmix.astype(MXU_DTYPE)
            dbc_ref[g] += jnp.sum(dmix, axis=-1, keepdims=True)
            dw_ref[g] += jnp.where(mask, _nt(dmb, vng), 0.0)
            dvn_ref[:, sl] = _tn(wg, dmb)
        dvn = dvn_ref[...]
        dg_ref[...] += jnp.sum(dvn * xh, axis=0, keepdims=True)
        db_ref[...] += jnp.sum(dvn, axis=0, keepdims=True)
        dxh = dvn * g_ref[...]
        m1 = jnp.mean(dxh, axis=-1, keepdims=True)
        m2 = jnp.mean(dxh * xh, axis=-1, keepdims=True)
        dgv = rstd * (dxh - m1 - xh * m2)
        duv_ref[:, c:] = (dgv * _gelu_grad(v)).astype(MXU_DTYPE)

    return pl.pallas_call(
        body, grid=(t // CHUNK,),
        in_specs=[_row_spec(CHUNK, c, 0), _row_spec(CHUNK, c, 1), _row_spec(CHUNK, c), _par_spec((1, c)),
                  _par_spec((1, c)), _par_spec((SG_GROUPS, CHUNK, CHUNK)), _par_spec((SG_GROUPS, CHUNK, 1)), _ANY],
        out_specs=[_row_spec(CHUNK, 2 * c), _par_spec((SG_GROUPS, CHUNK, CHUNK)), _par_spec((SG_GROUPS, CHUNK, 1)),
                   _par_spec((1, c)), _par_spec((1, c))],
        out_shape=[jax.ShapeDtypeStruct(dproj.shape, dproj.dtype), jax.ShapeDtypeStruct((SG_GROUPS, CHUNK, CHUNK), F32),
                   jax.ShapeDtypeStruct((SG_GROUPS, CHUNK, 1), F32), jax.ShapeDtypeStruct((1, c), F32),
                   jax.ShapeDtypeStruct((1, c), F32)],
        scratch_shapes=[pltpu.VMEM((CHUNK, c), F32)], input_output_aliases={7: 0},
        compiler_params=_params("arbitrary"), name=name)(proj, proj, dsgo, ln_g.reshape(1, c), ln_b.reshape(1, c), w, bcol, dproj)


CONV_TC = 512


def _conv_pre(x, w_ref, b_ref, rows):
    acc = x * w_ref[SSM_CONV - 1:SSM_CONV, :] + b_ref[...]
    for k in range(SSM_CONV - 1):
        sh = SSM_CONV - 1 - k
        xs = jnp.where(rows >= sh, pltpu.roll(x, sh, axis=0), 0.0)
        acc = acc + xs * w_ref[k:k + 1, :]
    return acc


def _conv_fwd(proj, w, b, *, bsz, name):
    t = proj.shape[0]
    s = t // bsz
    nj = SSM_CONV_DIM // CONV_TC
    c0 = XBC_COL0 // CONV_TC

    def body(x_ref, w_ref, b_ref, o_ref):
        x = x_ref[...]
        rows = lax.broadcasted_iota(jnp.int32, x.shape, 0)
        pre = _conv_pre(x, w_ref, b_ref, rows)
        o_ref[...] = pre * _sigmoid(pre)

    return pl.pallas_call(
        body, grid=(bsz, nj),
        in_specs=[pl.BlockSpec((s, CONV_TC), lambda bb, j: (bb, c0 + j)), pl.BlockSpec((SSM_CONV, CONV_TC), lambda bb, j: (0, j)),
                  pl.BlockSpec((1, CONV_TC), lambda bb, j: (0, j))],
        out_specs=pl.BlockSpec((s, CONV_TC), lambda bb, j: (bb, j)),
        out_shape=jax.ShapeDtypeStruct((t, SSM_CONV_DIM), F32),
        compiler_params=_params("parallel", "parallel"), name=name)(proj, w, b.reshape(1, -1))


def _conv_bwd(proj, dact, w, b, dproj, *, bsz, name):
    t = proj.shape[0]
    s = t // bsz
    nj = SSM_CONV_DIM // CONV_TC
    c0 = XBC_COL0 // CONV_TC

    def body(x_ref, d_ref, w_ref, b_ref, _, dx_ref, dw_ref, db_ref):
        @pl.when(pl.program_id(1) == 0)
        def _():
            dw_ref[...] = jnp.zeros_like(dw_ref)
            db_ref[...] = jnp.zeros_like(db_ref)

        x = x_ref[...]
        rows = lax.broadcasted_iota(jnp.int32, x.shape, 0)
        pre = _conv_pre(x, w_ref, b_ref, rows)
        sg = _sigmoid(pre)
        dpre = d_ref[...] * (sg * (1.0 + pre * (1.0 - sg)))
        db_ref[...] += jnp.sum(dpre, axis=0, keepdims=True)
        dx = dpre * w_ref[SSM_CONV - 1:SSM_CONV, :]
        dw_ref[SSM_CONV - 1:SSM_CONV, :] += jnp.sum(dpre * x, axis=0, keepdims=True)
        for k in range(SSM_CONV - 1):
            sh = SSM_CONV - 1 - k
            xs = jnp.where(rows >= sh, pltpu.roll(x, sh, axis=0), 0.0)
            dw_ref[k:k + 1, :] += jnp.sum(dpre * xs, axis=0, keepdims=True)
            dsh = jnp.where(rows < s - sh, pltpu.roll(dpre, s - sh, axis=0), 0.0)
            dx = dx + dsh * w_ref[k:k + 1, :]
        dx_ref[...] = dx.astype(MXU_DTYPE)

    return pl.pallas_call(
        body, grid=(nj, bsz),
        in_specs=[pl.BlockSpec((s, CONV_TC), lambda j, bb: (bb, c0 + j)), pl.BlockSpec((s, CONV_TC), lambda j, bb: (bb, j)),
                  pl.BlockSpec((SSM_CONV, CONV_TC), lambda j, bb: (0, j)), pl.BlockSpec((1, CONV_TC), lambda j, bb: (0, j)), _ANY],
        out_specs=[pl.BlockSpec((s, CONV_TC), lambda j, bb: (bb, c0 + j)), pl.BlockSpec((SSM_CONV, CONV_TC), lambda j, bb: (0, j)),
                   pl.BlockSpec((1, CONV_TC), lambda j, bb: (0, j))],
        out_shape=[jax.ShapeDtypeStruct(dproj.shape, dproj.dtype), jax.ShapeDtypeStruct((SSM_CONV, SSM_CONV_DIM), F32),
                   jax.ShapeDtypeStruct((1, SSM_CONV_DIM), F32)],
        input_output_aliases={4: 0},
        compiler_params=_params("parallel", "arbitrary"), name=name)(proj, dact, w, b.reshape(1, -1), dproj)


def _softplus(x):
    return jnp.maximum(x, 0.0) + jnp.log1p(jnp.exp(-jnp.abs(x)))


def _pad_heads(v):
    return jnp.broadcast_to(jnp.pad(v.astype(F32), (0, HEAD_PAD - SSM_HEADS))[None, :], (SUBLANE, HEAD_PAD))


def _ssd_prep(dt_raw, dt_bias8, a_log8, *, name):
    t = dt_raw.shape[0]
    n = CHUNK

    def body(r_ref, b_ref, al_ref, dt_ref, cs_ref, dtt_ref, cst_ref):
        dt = _softplus(r_ref[...] + b_ref[0:1, :])
        da = dt * (-jnp.exp(al_ref[0:1, :]))
        row = lax.broadcasted_iota(jnp.int32, (n, n), 0)
        col = lax.broadcasted_iota(jnp.int32, (n, n), 1)
        lower = (col <= row).astype(F32)
        upper = (row <= col).astype(F32)
        eye = (row == col).astype(F32)
        dt_ref[...] = dt
        cs_ref[...] = jnp.dot(lower, da, precision=HIGHEST, preferred_element_type=F32)
        dn = (((0,), (0,)), ((), ()))
        cst_ref[0] = lax.dot_general(da, upper, dn, precision=HIGHEST, preferred_element_type=F32)
        dtt_ref[0] = lax.dot_general(dt, eye, dn, precision=HIGHEST, preferred_element_type=F32)

    hp = HEAD_PAD
    return pl.pallas_call(
        body, grid=(t // n,),
        in_specs=[_row_spec(n, hp), _par_spec((SUBLANE, hp)), _par_spec((SUBLANE, hp))],
        out_specs=[_row_spec(n, hp), _row_spec(n, hp), pl.BlockSpec((1, hp, n), lambda i: (i, 0, 0)),
                   pl.BlockSpec((1, hp, n), lambda i: (i, 0, 0))],
        out_shape=[jax.ShapeDtypeStruct((t, hp), F32), jax.ShapeDtypeStruct((t, hp), F32),
                   jax.ShapeDtypeStruct((t // n, hp, n), F32), jax.ShapeDtypeStruct((t // n, hp, n), F32)],
        compiler_params=_params("parallel"), name=name)(dt_raw, dt_bias8, a_log8)


def _expand_mat():
    h = lax.broadcasted_iota(jnp.int32, (HEAD_PAD, SSM_INNER), 0)
    ch = lax.broadcasted_iota(jnp.int32, (HEAD_PAD, SSM_INNER), 1)
    return (ch // SSM_HEADDIM == h).astype(F32)


def _reduce_mat():
    ch = lax.broadcasted_iota(jnp.int32, (SSM_INNER, HEAD_PAD), 0)
    h = lax.broadcasted_iota(jnp.int32, (SSM_INNER, HEAD_PAD), 1)
    return (ch // SSM_HEADDIM == h).astype(F32)


def _expand(v, em):
    return jnp.dot(v, em, precision=HIGHEST, preferred_element_type=F32)


def _decay_mat(cs_ref, cst_ref, h, mask):
    seg = cs_ref[:, h:h + 1] - cst_ref[0, h:h + 1, :]
    return jnp.where(mask, jnp.exp(jnp.minimum(seg, 0.0)), 0.0)


GROUP_CH = SSM_INNER // SSM_GROUPS
PAIRS_PER_GROUP = GROUP_CH // LANE
HEADS_PER_GROUP = SSM_HEADS // SSM_GROUPS
BM_COL0 = SSM_INNER
CM_COL0 = SSM_INNER + SSM_GROUPS * SSM_STATE


def _ssd_specs(nc, rev):
    def cidx(i):
        return (i // nc) * nc + (nc - 1 - i % nc) if rev else i

    n = CHUNK
    xs = pl.BlockSpec((n, SSM_INNER), lambda i: (cidx(i), 0))
    bm = pl.BlockSpec((n, GROUP_CH), lambda i: (cidx(i), BM_COL0 // GROUP_CH))
    cm = pl.BlockSpec((n, GROUP_CH), lambda i: (cidx(i), CM_COL0 // GROUP_CH))
    hv = pl.BlockSpec((n, HEAD_PAD), lambda i: (cidx(i), 0))
    hvt = pl.BlockSpec((1, HEAD_PAD, n), lambda i: (cidx(i), 0, 0))
    st = pl.BlockSpec((1, SSM_INNER, SSM_STATE), lambda i: (cidx(i), 0, 0))
    return xs, bm, cm, hv, hvt, st


def _ssd_fwd(xbc, dt, cs, dtt, cst, dskip8, *, nc, name):
    t = xbc.shape[0]
    n = CHUNK
    xs_s, bm_s, cm_s, hv_s, hvt_s, st_s = _ssd_specs(nc, False)

    def body(xs_ref, bm_ref, cm_ref, dt_ref, cs_ref, dtt_ref, cst_ref, dsk_ref, y_ref, st_ref, prev):
        @pl.when(pl.program_id(0) % nc == 0)
        def _():
            prev[...] = jnp.zeros_like(prev)

        st_ref[0] = prev[...]
        em = _expand_mat()
        dtx = _expand(dt_ref[...], em)
        csx = _expand(cs_ref[...], em)
        dskx = _expand(dsk_ref[...], em)[0:1, :]
        xs = xs_ref[...]
        xdt = xs * dtx
        ecs = jnp.exp(csx)
        dec = jnp.exp(csx[n - 1:n, :] - csx)
        mask = _causal(n)
        lane = lax.broadcasted_iota(jnp.int32, (n, LANE), 1)
        for g in range(SSM_GROUPS):
            gs = slice(g * SSM_STATE, (g + 1) * SSM_STATE)
            gc = slice(g * GROUP_CH, (g + 1) * GROUP_CH)
            cmat = cm_ref[:, gs].astype(MXU_DTYPE)
            bmat = bm_ref[:, gs].astype(MXU_DTYPE)
            cb = _nt(cmat, bmat)
            yoff = ecs[:, gc] * _nt(cmat, prev[gc, :].astype(MXU_DTYPE))
            for q in range(PAIRS_PER_GROUP):
                hp = g * PAIRS_PER_GROUP + q
                sl = slice(hp * LANE, (hp + 1) * LANE)
                xp = xdt[:, sl].astype(MXU_DTYPE)
                m0 = (cb * _decay_mat(cs_ref, cst_ref, 2 * hp, mask)).astype(MXU_DTYPE)
                m1 = (cb * _decay_mat(cs_ref, cst_ref, 2 * hp + 1, mask)).astype(MXU_DTYPE)
                yd = jnp.where(lane < SSM_HEADDIM, _nn(m0, xp), _nn(m1, xp))
                y_ref[:, sl] = yd + yoff[:, q * LANE:(q + 1) * LANE] + xs[:, sl] * dskx[:, sl]
            snew = _tn((xdt[:, gc] * dec[:, gc]).astype(MXU_DTYPE), bmat)
            for r in range(HEADS_PER_GROUP):
                h = g * HEADS_PER_GROUP + r
                rows = slice(h * SSM_HEADDIM, (h + 1) * SSM_HEADDIM)
                e = jnp.exp(cst_ref[0, h:h + 1, n - 1:n])
                prev[rows, :] = prev[rows, :] * e + snew[r * SSM_HEADDIM:(r + 1) * SSM_HEADDIM, :]

    return pl.pallas_call(
        body, grid=(t // n,),
        in_specs=[xs_s, bm_s, cm_s, hv_s, hv_s, hvt_s, hvt_s, _par_spec((SUBLANE, HEAD_PAD))],
        out_specs=[xs_s, st_s],
        out_shape=[jax.ShapeDtypeStruct((t, SSM_INNER), F32), jax.ShapeDtypeStruct((t // n, SSM_INNER, SSM_STATE), F32)],
        scratch_shapes=[pltpu.VMEM((SSM_INNER, SSM_STATE), F32)],
        compiler_params=_params("arbitrary"), name=name)(xbc, xbc, xbc, dt, cs, dtt, cst, dskip8)


def _ssd_bwd(dy, xbc, dt, cs, dtt, cst, st, dskip8, a_log8, dt_raw, dt_bias8, *, nc, name):
    t = xbc.shape[0]
    n = CHUNK
    xs_s, bm_s, cm_s, hv_s, hvt_s, st_s = _ssd_specs(nc, True)
    acc_s = _par_spec((1, HEAD_PAD))
    xbc_s = pl.BlockSpec((n, SSM_CONV_DIM), xs_s.index_map)

    def body(dy_ref, xs_ref, bm_ref, cm_ref, dt_ref, cs_ref, dtt_ref, cst_ref, st_ref, dsk_ref, al_ref, raw_ref, bias_ref,
             dxbc_ref, ddr_ref, dal_ref, dds_ref, dbias_ref, dprev, dxdt_s, tdec_s, tcs_s):
        @pl.when(pl.program_id(0) % nc == 0)
        def _():
            dprev[...] = jnp.zeros_like(dprev)

        @pl.when(pl.program_id(0) == 0)
        def _():
            dal_ref[...] = jnp.zeros_like(dal_ref)
            dds_ref[...] = jnp.zeros_like(dds_ref)
            dbias_ref[...] = jnp.zeros_like(dbias_ref)

        em = _expand_mat()
        rm = _reduce_mat()

        def head_reduce(v):
            return jnp.dot(v, rm, precision=HIGHEST, preferred_element_type=F32)

        dtv = dt_ref[...]
        csv = cs_ref[...]
        dtx = _expand(dtv, em)
        csx = _expand(csv, em)
        dskx = _expand(dsk_ref[...], em)[0:1, :]
        xs = xs_ref[...]
        dyv = dy_ref[...]
        xdt = xs * dtx
        ecs = jnp.exp(csx)
        dec = jnp.exp(csx[n - 1:n, :] - csx)
        mask = _causal(n)
        lane = lax.broadcasted_iota(jnp.int32, (n, LANE), 1)
        hlane = lax.broadcasted_iota(jnp.int32, (1, HEAD_PAD), 1)
        hsub = lax.broadcasted_iota(jnp.int32, (HEAD_PAD, 1), 0)
        rsum = jnp.zeros((n, HEAD_PAD), F32)
        csum = jnp.zeros((HEAD_PAD, n), F32)
        for g in range(SSM_GROUPS):
            gs = slice(g * SSM_STATE, (g + 1) * SSM_STATE)
            gc = slice(g * GROUP_CH, (g + 1) * GROUP_CH)
            cmat = cm_ref[:, gs].astype(MXU_DTYPE)
            bmat = bm_ref[:, gs].astype(MXU_DTYPE)
            cb = _nt(cmat, bmat)
            pg = st_ref[0, gc, :].astype(MXU_DTYPE)
            dpg = dprev[gc, :]
            dpgb = dpg.astype(MXU_DTYPE)
            z = _nt(cmat, pg)
            dyg = dyv[:, gc]
            dz = (dyg * ecs[:, gc]).astype(MXU_DTYPE)
            dc = _nn(dz, pg)
            dprev_y = _tn(dz, cmat)
            tcs_s[:, gc] = dyg * z * ecs[:, gc]
            xd = xdt[:, gc] * dec[:, gc]
            wmat = _nt(bmat, dpgb)
            db = _nn(xd.astype(MXU_DTYPE), dpgb)
            tdec_s[:, gc] = wmat * xd
            dxdt_g = wmat * dec[:, gc]
            dcb = jnp.zeros((n, n), F32)
            for q in range(PAIRS_PER_GROUP):
                hp = g * PAIRS_PER_GROUP + q
                sl = slice(hp * LANE, (hp + 1) * LANE)
                xp = xdt[:, sl].astype(MXU_DTYPE)
                dyp = dyv[:, sl]
                dypb = dyp.astype(MXU_DTYPE)
                dxp = None
                for hh in range(2):
                    h = 2 * hp + hh
                    lm = _decay_mat(cs_ref, cst_ref, h, mask)
                    mine = (lane < SSM_HEADDIM) if hh == 0 else (lane >= SSM_HEADDIM)
                    dm = _nt(jnp.where(mine, dyp, 0.0).astype(MXU_DTYPE), xp)
                    dml = dm * lm
                    dcb = dcb + dml
                    gseg = dml * cb
                    rsum = rsum + jnp.sum(gseg, axis=1, keepdims=True) * (hlane == h).astype(F32)
                    csum = csum + (hsub == h).astype(F32) * jnp.sum(gseg, axis=0, keepdims=True)
                    dxh = _tn((cb * lm).astype(MXU_DTYPE), dypb)
                    dxp = dxh if dxp is None else jnp.where(mine, dxh, dxp)
                dxdt_s[:, sl] = dxdt_g[:, q * LANE:(q + 1) * LANE] + dxp
            dcbb = dcb.astype(MXU_DTYPE)
            dxbc_ref[:, CM_COL0 + g * SSM_STATE:CM_COL0 + (g + 1) * SSM_STATE] = dc + _nn(dcbb, bmat)
            dxbc_ref[:, BM_COL0 + g * SSM_STATE:BM_COL0 + (g + 1) * SSM_STATE] = db + _tn(dcbb, cmat)
            for r in range(HEADS_PER_GROUP):
                h = g * HEADS_PER_GROUP + r
                rows = slice(h * SSM_HEADDIM, (h + 1) * SSM_HEADDIM)
                lr = slice(r * SSM_HEADDIM, (r + 1) * SSM_HEADDIM)
                e = jnp.exp(cst_ref[0, h:h + 1, n - 1:n])
                dprev[rows, :] = dpg[lr, :] * e + dprev_y[lr, :]
            tq = lax.dot_general(dpg * st_ref[0, gc, :], rm[gc, :], (((0,), (0,)), ((), ())), precision=HIGHEST,
                                 preferred_element_type=F32)
            if g == 0:
                qsum = jnp.sum(tq, axis=0, keepdims=True)
            else:
                qsum = qsum + jnp.sum(tq, axis=0, keepdims=True)
        dxdt = dxdt_s[...]
        dxbc_ref[:, 0:SSM_INNER] = dxdt * dtx + dyv * dskx
        ddt = head_reduce(dxdt * xs)
        edec = head_reduce(tdec_s[...])
        ycs = head_reduce(tcs_s[...])
        row = lax.broadcasted_iota(jnp.int32, (n, HEAD_PAD), 0)
        extra = jnp.sum(edec, axis=0, keepdims=True) + qsum * jnp.exp(csv[n - 1:n, :])
        dcs = rsum - csum.T + ycs - edec + jnp.where(row == n - 1, extra, 0.0)
        r2 = lax.broadcasted_iota(jnp.int32, (n, n), 0)
        c2 = lax.broadcasted_iota(jnp.int32, (n, n), 1)
        dda = jnp.dot((c2 >= r2).astype(F32), dcs, precision=HIGHEST, preferred_element_type=F32)
        a_row = -jnp.exp(al_ref[0:1, :])
        ddt = ddt + dda * a_row
        dal_ref[...] += jnp.sum(dda * dtv, axis=0, keepdims=True) * a_row
        dds_ref[...] += jnp.sum(head_reduce(dyv * xs), axis=0, keepdims=True)
        ddr = ddt * _sigmoid(raw_ref[...] + bias_ref[0:1, :])
        ddr_ref[...] = ddr
        dbias_ref[...] += jnp.sum(ddr, axis=0, keepdims=True)

    par8 = _par_spec((SUBLANE, HEAD_PAD))
    return pl.pallas_call(
        body, grid=(t // n,),
        in_specs=[xs_s, xs_s, bm_s, cm_s, hv_s, hv_s, hvt_s, hvt_s, st_s, par8, par8, hv_s, par8],
        out_specs=[xbc_s, hv_s, acc_s, acc_s, acc_s],
        out_shape=[jax.ShapeDtypeStruct((t, SSM_CONV_DIM), F32), jax.ShapeDtypeStruct((t, HEAD_PAD), F32),
                   jax.ShapeDtypeStruct((1, HEAD_PAD), F32), jax.ShapeDtypeStruct((1, HEAD_PAD), F32),
                   jax.ShapeDtypeStruct((1, HEAD_PAD), F32)],
        scratch_shapes=[pltpu.VMEM((SSM_INNER, SSM_STATE), F32), pltpu.VMEM((n, SSM_INNER), F32),
                        pltpu.VMEM((n, SSM_INNER), F32), pltpu.VMEM((n, SSM_INNER), F32)],
        compiler_params=_params("arbitrary"), name=name)(dy, xbc, xbc, xbc, dt, cs, dtt, cst, st, dskip8, a_log8, dt_raw, dt_bias8)


def _gate_norm_fwd(y, proj, norm_g, *, name):
    t, c = y.shape
    tm = _pick(t, (256, 128))

    def body(y_ref, z_ref, g_ref, o_ref):
        z = z_ref[...]
        yz = y_ref[...] * z * _sigmoid(z)
        for g in range(SSM_GROUPS):
            gc = slice(g * GROUP_CH, (g + 1) * GROUP_CH)
            seg = yz[:, gc]
            r = lax.rsqrt(jnp.mean(seg * seg, axis=-1, keepdims=True) + RMS_EPS)
            o_ref[:, gc] = (seg * r * g_ref[:, gc]).astype(MXU_DTYPE)

    return pl.pallas_call(
        body, grid=(t // tm,), in_specs=[_row_spec(tm, c), _row_spec(tm, c, 1), _par_spec((1, c))],
        out_specs=_row_spec(tm, c), out_shape=jax.ShapeDtypeStruct((t, c), MXU_DTYPE),
        compiler_params=_params("parallel"), name=name)(y, proj, norm_g.reshape(1, c))


def _gate_norm_bwd(dyb, y, proj, norm_g, dproj, *, name):
    t, c = y.shape
    tm = _pick(t, (256, 128))

    def body(d_ref, y_ref, z_ref, g_ref, _, dy_ref, dz_ref, dg_ref):
        @pl.when(pl.program_id(0) == 0)
        def _():
            dg_ref[...] = jnp.zeros_like(dg_ref)

        z = z_ref[...]
        yv = y_ref[...]
        sz = _sigmoid(z)
        silu = z * sz
        yz = yv * silu
        dv = d_ref[...]
        for g in range(SSM_GROUPS):
            gc = slice(g * GROUP_CH, (g + 1) * GROUP_CH)
            seg = yz[:, gc]
            r = lax.rsqrt(jnp.mean(seg * seg, axis=-1, keepdims=True) + RMS_EPS)
            nrm = seg * r
            dn = dv[:, gc] * g_ref[:, gc]
            dg_ref[:, gc] += jnp.sum(dv[:, gc] * nrm, axis=0, keepdims=True)
            dyz = r * (dn - nrm * jnp.mean(dn * nrm, axis=-1, keepdims=True))
            dy_ref[:, gc] = dyz * silu[:, gc]
            dz_ref[:, gc] = (dyz * yv[:, gc] * (sz[:, gc] * (1.0 + z[:, gc] * (1.0 - sz[:, gc])))).astype(MXU_DTYPE)

    return pl.pallas_call(
        body, grid=(t // tm,), in_specs=[_row_spec(tm, c), _row_spec(tm, c), _row_spec(tm, c, 1), _par_spec((1, c)), _ANY],
        out_specs=[_row_spec(tm, c), _row_spec(tm, c, 1), _par_spec((1, c))],
        out_shape=[jax.ShapeDtypeStruct((t, c), F32), jax.ShapeDtypeStruct(dproj.shape, dproj.dtype),
                   jax.ShapeDtypeStruct((1, c), F32)],
        input_output_aliases={4: 1},
        compiler_params=_params("arbitrary"), name=name)(dyb, y, proj, norm_g.reshape(1, c), dproj)


GA_COLBLK = GAB_COL0 // D_MODEL


def _merge_fwd(br_a, br_b, proj, *, name):
    t, c = br_a.shape
    tm = _pick(t, (256, 128))

    def body(a_ref, b_ref, ga_ref, gb_ref, o_ref):
        o_ref[...] = (_sigmoid(ga_ref[...]) * a_ref[...] + _sigmoid(gb_ref[...]) * b_ref[...]).astype(MXU_DTYPE)

    return pl.pallas_call(
        body, grid=(t // tm,),
        in_specs=[_row_spec(tm, c), _row_spec(tm, c), _row_spec(tm, c, GA_COLBLK), _row_spec(tm, c, GA_COLBLK + 1)],
        out_specs=_row_spec(tm, c), out_shape=jax.ShapeDtypeStruct((t, c), MXU_DTYPE),
        compiler_params=_params("parallel"), name=name)(br_a, br_b, proj, proj)


def _merge_bwd(dm, br_a, br_b, proj, *, name):
    t, c = br_a.shape
    tm = _pick(t, (256, 128))

    def body(dm_ref, a_ref, b_ref, ga_ref, gb_ref, da_ref, db_ref, dg_ref):
        d = dm_ref[...]
        sa = _sigmoid(ga_ref[...])
        sb = _sigmoid(gb_ref[...])
        da_ref[...] = (d * sa).astype(MXU_DTYPE)
        db_ref[...] = (d * sb).astype(MXU_DTYPE)
        dg_ref[:, :c] = (d * a_ref[...] * sa * (1.0 - sa)).astype(MXU_DTYPE)
        dg_ref[:, c:] = (d * b_ref[...] * sb * (1.0 - sb)).astype(MXU_DTYPE)

    return pl.pallas_call(
        body, grid=(t // tm,),
        in_specs=[_row_spec(tm, c), _row_spec(tm, c), _row_spec(tm, c), _row_spec(tm, c, GA_COLBLK), _row_spec(tm, c, GA_COLBLK + 1)],
        out_specs=[_row_spec(tm, c), _row_spec(tm, c), _row_spec(tm, 2 * c, GAB_COL0 // (2 * c))],
        out_shape=[jax.ShapeDtypeStruct((t, c), MXU_DTYPE), jax.ShapeDtypeStruct((t, c), MXU_DTYPE),
                   jax.ShapeDtypeStruct((t, MAIN_COLS), MXU_DTYPE)],
        compiler_params=_params("parallel"), name=name)(dm, br_a, br_b, proj, proj)


def _layer_fwd(x, xb, memn_b, w, *, bsz, tag):
    nc = x.shape[0] // bsz // CHUNK
    sv = {"x_in": xb}
    proj = _mm(xb, w["w_main"], name=f"{tag}_proj")
    dt_raw = _mm(xb, w["w_dt"], name=f"{tag}_dtproj")
    sgo = _sg_fwd(proj, w["sg_ln_g"], w["sg_ln_b"], w["sg_w"], w["sg_bcol"], name=f"{tag}_sg_fwd")
    xbc = _conv_fwd(proj, w["conv_w"], w["conv_b"], bsz=bsz, name=f"{tag}_conv_fwd")
    dt, cs, dtt, cst = _ssd_prep(dt_raw, w["dt_bias8"], w["a_log8"], name=f"{tag}_ssd_prep")
    y, st = _ssd_fwd(xbc, dt, cs, dtt, cst, w["d_skip8"], nc=nc, name=f"{tag}_ssd_fwd")
    yb = _gate_norm_fwd(y, proj, w["ssm_norm_g"], name=f"{tag}_gate_norm_fwd")
    br_a = _mm(sgo, w["p_a"], name=f"{tag}_br_a")
    br_b = _mm(yb, w["p_b"], name=f"{tag}_br_b")
    merged = _merge_fwd(br_a, br_b, proj, name=f"{tag}_merge_fwd")
    mix = _mm(merged, w["w_mix_o"], name=f"{tag}_mix_o")
    x1, x1b, xh1, rs1 = _ln_fwd(x, mix, w["ln_g"][0], w["ln_b"][0], name=f"{tag}_ln1_fwd")
    sv.update(proj=proj, dt_raw=dt_raw, sgo=sgo, xbc=xbc, dt=dt, cs=cs, dtt=dtt, cst=cst, y=y, st=st, yb=yb,
              br_a=br_a, br_b=br_b, merged=merged, xh1=xh1, rs1=rs1, x1b=x1b)
    q = _mm(x1b, w["w_xq"], out_dtype=MXU_DTYPE, name=f"{tag}_q")
    kv = _mm(memn_b, w["w_xkv"], out_dtype=MXU_DTYPE, name=f"{tag}_kv")
    o = _attn_fwd(q, kv, bsz=bsz, name=f"{tag}_attn_fwd")
    att = _mm(o, w["w_xo"], name=f"{tag}_xo")
    x2, x2b, xh2, rs2 = _ln_fwd(x1, att, w["ln_g"][1], w["ln_b"][1], name=f"{tag}_ln2_fwd")
    sv.update(q=q, kv=kv, o=o, xh2=xh2, rs2=rs2, x2b=x2b)
    h = _mm(x2b, w["w_ffn_in"], name=f"{tag}_ffn_in")
    a = _swiglu_fwd(h, name=f"{tag}_swiglu_fwd")
    ffn = _mm(a, w["w_ffn_out"], name=f"{tag}_ffn_out")
    x3, x3b, xh3, rs3 = _ln_fwd(x2, ffn, w["ln_g"][2], w["ln_b"][2], name=f"{tag}_ln3_fwd")
    sv.update(h=h, a=a, xh3=xh3, rs3=rs3)
    return x3, x3b, sv


def _layer_bwd(dx3_addends, dx3_scales, memn_b, w, sv, *, bsz, tag):
    nc = sv["xh1"].shape[0] // bsz // CHUNK
    gr = {}
    dp3, dp3b, dg3, db3 = _ln_bwd(dx3_addends, dx3_scales, sv["xh3"], sv["rs3"], w["ln_g"][2], name=f"{tag}_ln3_bwd")
    da = _mm(dp3b, w["w_ffn_out"], tb=True, name=f"{tag}_d_a")
    gr["w_ffn_out"] = _mm(sv["a"], dp3b, ta=True, name=f"{tag}_dw_ffn_out")
    dh = _swiglu_bwd(sv["h"], da, name=f"{tag}_swiglu_bwd")
    gr["w_ffn_in"] = _mm(sv["x2b"], dh, ta=True, name=f"{tag}_dw_ffn_in")
    dx2_br = _mm(dh, w["w_ffn_in"], tb=True, name=f"{tag}_dx2")
    dp2, dp2b, dg2, db2 = _ln_bwd([dp3, dx2_br], [ALPHA, 1.0], sv["xh2"], sv["rs2"], w["ln_g"][1], name=f"{tag}_ln2_bwd")
    do = _mm(dp2b, w["w_xo"], tb=True, out_dtype=MXU_DTYPE, name=f"{tag}_d_o")
    gr["w_xo"] = _mm(sv["o"], dp2b, ta=True, name=f"{tag}_dw_xo")
    dq, dk, dv = _attn_bwd(sv["q"], sv["kv"], do, bsz=bsz, name=f"{tag}_attn_bwd")
    dkv = jnp.concatenate([dk, dv], axis=1)
    gr["w_xq"] = _mm(sv["x1b"], dq, ta=True, name=f"{tag}_dw_xq")
    gr["w_xkv"] = _mm(memn_b, dkv, ta=True, name=f"{tag}_dw_xkv")
    dmemn = _mm(dkv, w["w_xkv"], tb=True, name=f"{tag}_d_memn")
    dx1_br = _mm(dq, w["w_xq"], tb=True, name=f"{tag}_dx1")
    dp1, dp1b, dg1, db1 = _ln_bwd([dp2, dx1_br], [ALPHA, 1.0], sv["xh1"], sv["rs1"], w["ln_g"][0], name=f"{tag}_ln1_bwd")
    gr["ln_g"] = jnp.concatenate([dg1, dg2, dg3], axis=0)
    gr["ln_b"] = jnp.concatenate([db1, db2, db3], axis=0)
    dmerged = _mm(dp1b, w["w_mix_o"], tb=True, name=f"{tag}_d_merged")
    gr["w_mix_o"] = _mm(sv["merged"], dp1b, ta=True, name=f"{tag}_dw_mix_o")
    dbr_a, dbr_b, dproj = _merge_bwd(dmerged, sv["br_a"], sv["br_b"], sv["proj"], name=f"{tag}_merge_bwd")
    gr["p_a"] = _mm(sv["sgo"], dbr_a, ta=True, name=f"{tag}_dw_p_a")
    gr["p_b"] = _mm(sv["yb"], dbr_b, ta=True, name=f"{tag}_dw_p_b")
    dsgo = _mm(dbr_a, w["p_a"], tb=True, name=f"{tag}_d_sgo")
    dyb = _mm(dbr_b, w["p_b"], tb=True, name=f"{tag}_d_yb")
    dy, dproj, gr["ssm_norm_g"] = _gate_norm_bwd(dyb, sv["y"], sv["proj"], w["ssm_norm_g"], dproj, name=f"{tag}_gate_norm_bwd")
    dxbc, ddr, gr["a_log"], gr["d_skip"], gr["dt_bias"] = _ssd_bwd(
        dy, sv["xbc"], sv["dt"], sv["cs"], sv["dtt"], sv["cst"], sv["st"], w["d_skip8"], w["a_log8"], sv["dt_raw"],
        w["dt_bias8"], nc=nc, name=f"{tag}_ssd_bwd")
    dproj, gr["conv_w"], gr["conv_b"] = _conv_bwd(sv["proj"], dxbc, w["conv_w"], w["conv_b"], dproj, bsz=bsz, name=f"{tag}_conv_bwd")
    dproj, gr["sg_w"], dsg_bcol, gr["sg_ln_g"], gr["sg_ln_b"] = _sg_bwd(
        sv["proj"], dsgo, w["sg_ln_g"], w["sg_ln_b"], w["sg_w"], w["sg_bcol"], dproj, name=f"{tag}_sg_bwd")
    gr["sg_b"] = dsg_bcol[..., 0]
    gr["w_main"] = _mm(sv["x_in"], dproj, ta=True, name=f"{tag}_dw_main")
    gr["w_dt"] = _mm(sv["x_in"], ddr, ta=True, name=f"{tag}_dw_dt")
    dx_main = _mm(dproj, w["w_main"], tb=True, name=f"{tag}_dx_main")
    dx_dt = _mm(ddr, w["w_dt"], tb=True, name=f"{tag}_dx_dt")
    return [dp1, dx_main, dx_dt], [ALPHA, 1.0, 1.0], gr, dmemn


def _local_step(x, mem, tgt, mem_ln_g, mem_ln_b, layers):
    bsz, s, d = x.shape
    xf = x.reshape(bsz * s, d)
    memf = mem.reshape(-1, d)
    _, memn_b, mxh, mrs = _ln_fwd(memf, None, mem_ln_g, mem_ln_b, name="mem_ln_fwd")
    cur, curb, saved = xf, xf, []
    for li, w in enumerate(layers):
        cur, curb, sv = _layer_fwd(cur, curb, memn_b, w, bsz=bsz, tag=f"l{li}")
        saved.append(sv)
    dy, lsum = _loss_head(cur, tgt.reshape(bsz * s, d), name="loss_head")
    addends, scales = [dy], [1.0]
    grads, dmem = [None] * len(layers), []
    for li in reversed(range(len(layers))):
        addends, scales, grads[li], dm = _layer_bwd(addends, scales, memn_b, layers[li], saved[li], bsz=bsz, tag=f"l{li}")
        dmem.append(dm)
    grad_x = _add_scaled(addends, scales, name="grad_x").reshape(bsz, s, d)
    _, _, dmg, dmb = _ln_bwd(dmem, [1.0] * len(dmem), mxh, mrs, mem_ln_g, name="mem_ln_bwd")
    return lsum, grad_x, grads, dmg[0], dmb[0]


_ANY = pl.BlockSpec(memory_space=pl.ANY)
_MESH = pl.DeviceIdType.MESH


def _all_gather8(x, *, name):
    def body(x_ref, out_ref, send_sems, recv_sems):
        mx, my, mc = lax.axis_index("x"), lax.axis_index("y"), lax.axis_index("c")
        me, sibling = (mx, my, mc), (mx, my, 1 - mc)
        chips = [(1 - mx, my), (mx, 1 - my), (1 - mx, 1 - my)]

        def blk(px, py, pc):
            return out_ref.at[4 * px + 2 * py + pc]

        def copy(k, block, to, src=None):
            return pltpu.make_async_remote_copy(
                src_ref=blk(*block) if src is None else src, dst_ref=blk(*block), send_sem=send_sems.at[k],
                recv_sem=recv_sems.at[k], device_id=to, device_id_type=_MESH)

        first = [copy(0, me, sibling, src=x_ref)]
        first += [copy(1 + j, me, (*chip, mc), src=x_ref) for j, chip in enumerate(chips)]
        for cp in first:
            cp.start()
        passed = [copy(4 + j, (*chip, mc), sibling) for j, chip in enumerate(chips)]
        for j, chip in enumerate(chips):
            copy(1 + j, (*chip, mc), me).wait_recv()
            passed[j].start()
        copy(0, sibling, me).wait_recv()
        for j, chip in enumerate(chips):
            copy(4 + j, (*chip, 1 - mc), me).wait_recv()
        for cp in first + passed:
            cp.wait_send()

    return pl.pallas_call(
        body, out_shape=jax.ShapeDtypeStruct((N_DEV,) + x.shape, x.dtype), in_specs=[_ANY], out_specs=_ANY,
        scratch_shapes=[pltpu.SemaphoreType.DMA((7,)), pltpu.SemaphoreType.DMA((7,))], name=name)(x)


def _row_tile(rows, row_bytes, mult=SUBLANE):
    best = None
    for tr in range(mult, rows + 1, mult):
        if rows % tr == 0 and (best is None or tr * row_bytes <= BLOCK_BYTES):
            best = tr
    return rows if best is None else best


def _gather_shape(r, c, kind):
    return {"row": (2, N_CHIPS * r, c), "col": (2, r, N_CHIPS * c), "chip": (2, N_CHIPS, r, c)}[kind]


def _cast_place(shard, kind, dtype, chip_idx, *, name):
    _, r, c = shard.shape
    tr = _row_tile(r, c * 4, 16)
    nt = r // tr

    def body(_, s_ref, o_ref):
        o_ref[...] = s_ref[...].astype(dtype)

    if kind == "row":
        out_spec = pl.BlockSpec((None, tr, c), lambda l, i, j_ref: (l, j_ref[0] * nt + i, 0))
    elif kind == "col":
        out_spec = pl.BlockSpec((None, tr, c), lambda l, i, j_ref: (l, i, j_ref[0]))
    else:
        out_spec = pl.BlockSpec((None, None, tr, c), lambda l, i, j_ref: (l, j_ref[0], i, 0))
    grid_spec = pltpu.PrefetchScalarGridSpec(
        num_scalar_prefetch=1, grid=(2, nt), in_specs=[pl.BlockSpec((None, tr, c), lambda l, i, j_ref: (l, i, 0))],
        out_specs=out_spec)
    return pl.pallas_call(body, grid_spec=grid_spec, out_shape=jax.ShapeDtypeStruct(_gather_shape(r, c, kind), dtype),
                          compiler_params=_params("parallel", "parallel"), name=name)(chip_idx, shard)


def _gather_params(bufs, shard_shapes, kinds, *, name):
    n = len(bufs)

    def body(*refs):
        outs = refs[n:2 * n]
        send_sems, recv_sems = refs[2 * n:]
        mx, my, mc = lax.axis_index("x"), lax.axis_index("y"), lax.axis_index("c")
        me, sibling = (mx, my, mc), (mx, my, 1 - mc)
        chips = [(1 - mx, my), (mx, 1 - my), (1 - mx, 1 - my)]

        def blk(i, px, py, pc):
            r, c = shard_shapes[i]
            j = 2 * px + py
            if kinds[i] == "row":
                return outs[i].at[pc, pl.ds(pl.multiple_of(j * r, r), r)]
            if kinds[i] == "col":
                return outs[i].at[pc, :, pl.ds(pl.multiple_of(j * c, c), c)]
            return outs[i].at[pc, j]

        def copy(i, k, block, to):
            return pltpu.make_async_remote_copy(
                src_ref=blk(i, *block), dst_ref=blk(i, *block), send_sem=send_sems.at[6 * i + k],
                recv_sem=recv_sems.at[6 * i + k], device_id=to, device_id_type=_MESH)

        sent = []
        for i in range(n):
            for j, chip in enumerate(chips):
                cp = copy(i, j, me, (*chip, mc))
                cp.start()
                sent.append(cp)
        for j, chip in enumerate(chips):
            for i in range(n):
                copy(i, j, (*chip, mc), me).wait_recv()
                fwd = copy(i, 3 + j, (*chip, mc), sibling)
                fwd.start()
                sent.append(fwd)
        for i in range(n):
            for j, chip in enumerate(chips):
                copy(i, 3 + j, (*chip, 1 - mc), me).wait_recv()
        for cp in sent:
            cp.wait_send()

    return pl.pallas_call(
        body, out_shape=[jax.ShapeDtypeStruct(b.shape, b.dtype) for b in bufs], in_specs=[_ANY] * n, out_specs=[_ANY] * n,
        input_output_aliases={i: i for i in range(n)},
        scratch_shapes=[pltpu.SemaphoreType.DMA((6 * n,)), pltpu.SemaphoreType.DMA((6 * n,))], name=name)(*bufs)


def _half(r, h):
    return pl.ds(pl.multiple_of(h * (r // 2), r // 2), r // 2)


def _grads_to_sibling(gs, views, *, name):
    n = len(gs)

    def recv_shape(g, view):
        if view == "chip":
            return jax.ShapeDtypeStruct((g.shape[0], g.shape[1] // 2, g.shape[2]), g.dtype)
        return jax.ShapeDtypeStruct((g.shape[0] // 2, g.shape[1]), g.dtype)

    def body(*refs):
        ins, outs = refs[:n], refs[n:2 * n]
        send_sems, recv_sems = refs[2 * n:]
        mx, my, mc = lax.axis_index("x"), lax.axis_index("y"), lax.axis_index("c")
        copies = []
        for i in range(n):
            if views[i] == "chip":
                src = ins[i].at[:, _half(gs[i].shape[1], 1 - mc)]
            else:
                src = ins[i].at[_half(gs[i].shape[0], 1 - mc)]
            cp = pltpu.make_async_remote_copy(src_ref=src, dst_ref=outs[i], send_sem=send_sems.at[i], recv_sem=recv_sems.at[i],
                                              device_id=(mx, my, 1 - mc), device_id_type=_MESH)
            cp.start()
            copies.append(cp)
        for cp in copies:
            cp.wait()

    return pl.pallas_call(
        body, out_shape=[recv_shape(g, v) for g, v in zip(gs, views)], in_specs=[_ANY] * n, out_specs=[_ANY] * n,
        scratch_shapes=[pltpu.SemaphoreType.DMA((n,)), pltpu.SemaphoreType.DMA((n,))], name=name)(*gs)


def _grads_to_chips(pairs, views, *, name):
    n = len(pairs)

    def quad_shape(p, view):
        if view == "chip":
            return jax.ShapeDtypeStruct(p.shape, p.dtype)
        return jax.ShapeDtypeStruct((N_CHIPS, p.shape[0], p.shape[1] // N_CHIPS), p.dtype)

    def body(*refs):
        ins, outs = refs[:n], refs[n:2 * n]
        send_sems, recv_sems = refs[2 * n:]
        mx, my, mc = lax.axis_index("x"), lax.axis_index("y"), lax.axis_index("c")
        me = 2 * mx + my
        chips = [(1 - mx, my), (mx, 1 - my), (1 - mx, 1 - my)]

        def blk(i, j):
            if views[i] == "chip":
                return ins[i].at[j]
            c = pairs[i].shape[1] // N_CHIPS
            return ins[i].at[:, pl.ds(pl.multiple_of(j * c, c), c)]

        copies = []
        for i in range(n):
            for k, (px, py) in enumerate(chips):
                cp = pltpu.make_async_remote_copy(src_ref=blk(i, 2 * px + py), dst_ref=outs[i].at[me], send_sem=send_sems.at[3 * i + k],
                                                  recv_sem=recv_sems.at[3 * i + k], device_id=(px, py, mc), device_id_type=_MESH)
                cp.start()
                copies.append(cp)
        for cp in copies:
            cp.wait()

    return pl.pallas_call(
        body, out_shape=[quad_shape(p, v) for p, v in zip(pairs, views)], in_specs=[_ANY] * n, out_specs=[_ANY] * n,
        scratch_shapes=[pltpu.SemaphoreType.DMA((3 * n,)), pltpu.SemaphoreType.DMA((3 * n,))], name=name)(*pairs)


def _grads_share(tots, *, name):
    n = len(tots)

    def body(*refs):
        ins, outs = refs[:n], refs[n:2 * n]
        send_sems, recv_sems = refs[2 * n:]
        mx, my, mc = lax.axis_index("x"), lax.axis_index("y"), lax.axis_index("c")
        copies = []
        for i in range(n):
            cp = pltpu.make_async_remote_copy(src_ref=ins[i], dst_ref=outs[i], send_sem=send_sems.at[i], recv_sem=recv_sems.at[i],
                                              device_id=(mx, my, 1 - mc), device_id_type=_MESH)
            cp.start()
            copies.append(cp)
        for cp in copies:
            cp.wait()

    return pl.pallas_call(
        body, out_shape=[jax.ShapeDtypeStruct(t.shape, t.dtype) for t in tots], in_specs=[_ANY] * n, out_specs=[_ANY] * n,
        scratch_shapes=[pltpu.SemaphoreType.DMA((n,)), pltpu.SemaphoreType.DMA((n,))], name=name)(*tots)


def _pair_sum(g, recv, view, c_idx, *, name):
    def body(c_ref, a_ref, b_ref, o_ref):
        o_ref[...] = a_ref[...] + b_ref[...]

    if view == "chip":
        nch, r, c = g.shape
        tr = _row_tile(r // 2, c * 4)
        gv = g.reshape(nch, 2, r // 2, c)
        grid = (nch, (r // 2) // tr)
        in_specs = [pl.BlockSpec((None, None, tr, c), lambda j, i, c_ref: (j, c_ref[0], i, 0)),
                    pl.BlockSpec((None, tr, c), lambda j, i, c_ref: (j, i, 0))]
        out_spec = pl.BlockSpec((None, tr, c), lambda j, i, c_ref: (j, i, 0))
        sem = ("parallel", "parallel")
    else:
        r, c4 = g.shape
        tr = _row_tile(r // 2, c4 * 4)
        gv = g.reshape(2, r // 2, c4)
        grid = ((r // 2) // tr,)
        in_specs = [pl.BlockSpec((None, tr, c4), lambda i, c_ref: (c_ref[0], i, 0)), pl.BlockSpec((tr, c4), lambda i, c_ref: (i, 0))]
        out_spec = pl.BlockSpec((tr, c4), lambda i, c_ref: (i, 0))
        sem = ("parallel",)
    grid_spec = pltpu.PrefetchScalarGridSpec(num_scalar_prefetch=1, grid=grid, in_specs=in_specs, out_specs=out_spec)
    return pl.pallas_call(body, grid_spec=grid_spec, out_shape=jax.ShapeDtypeStruct(recv.shape, recv.dtype),
                          compiler_params=_params(*sem), name=name)(c_idx, gv, recv)


def _quad_sum(pairs, quads, view, chip_idx, *, name):
    nl = len(quads)
    nch, rh, c = quads[0].shape
    tr = _row_tile(rh, c * 4)

    def body(_, *refs):
        o_ref = refs[-1]
        for l in range(nl):
            grp = refs[l * nch:(l + 1) * nch]
            acc = grp[0][...]
            for r in grp[1:]:
                acc = acc + r[...]
            o_ref[l] = acc

    if view == "chip":
        own = pl.BlockSpec((None, tr, c), lambda i, j_ref: (j_ref[0], i, 0))
    else:
        own = pl.BlockSpec((tr, c), lambda i, j_ref: (i, j_ref[0]))
    got = [pl.BlockSpec((None, tr, c), functools.partial(lambda i, j_ref, k: ((j_ref[0] + k) % nch, i, 0), k=k))
           for k in range(1, nch)]
    ins = []
    for l in range(nl):
        ins += [pairs[l]] + [quads[l]] * (nch - 1)
    grid_spec = pltpu.PrefetchScalarGridSpec(
        num_scalar_prefetch=1, grid=(rh // tr,), in_specs=([own] + got) * nl,
        out_specs=pl.BlockSpec((nl, tr, c), lambda i, j_ref: (0, i, 0)))
    return pl.pallas_call(body, grid_spec=grid_spec, out_shape=jax.ShapeDtypeStruct((nl, rh, c), quads[0].dtype),
                          compiler_params=_params("parallel"), name=name)(chip_idx, *ins)


def _sum_devices(g8, own, dev_idx, *, name):
    k, rows, cols = g8.shape

    def body(d_ref, a_ref, x_ref, o_ref):
        acc = None
        for i in range(k):
            term = jnp.where(d_ref[0] == i, x_ref[...], a_ref[i])
            acc = term if acc is None else acc + term
        o_ref[...] = acc

    grid_spec = pltpu.PrefetchScalarGridSpec(
        num_scalar_prefetch=1, grid=(1,),
        in_specs=[pl.BlockSpec((k, rows, cols), lambda i, d_ref: (0, 0, 0)), pl.BlockSpec((rows, cols), lambda i, d_ref: (0, 0))],
        out_specs=pl.BlockSpec((rows, cols), lambda i, d_ref: (0, 0)))
    return pl.pallas_call(body, grid_spec=grid_spec, out_shape=jax.ShapeDtypeStruct((rows, cols), g8.dtype),
                          compiler_params=_params("arbitrary"), name=name)(dev_idx, g8, own)


def _adamw(w, g, m, v, *, name):
    rows, cols = w.shape
    tr = rows
    for cand in (256, 128, 64, 32, 16, 8):
        if rows % cand == 0 and cand * cols <= 512 * 1024:
            tr = cand
            break
    c1 = 1.0 - ADAM_B1 ** ADAM_STEP
    c2 = 1.0 - ADAM_B2 ** ADAM_STEP

    def body(w_ref, g_ref, m_ref, v_ref, d_ref, nm_ref, nv_ref):
        gv = g_ref[...]
        nm = ADAM_B1 * m_ref[...] + (1.0 - ADAM_B1) * gv
        nv = ADAM_B2 * v_ref[...] + (1.0 - ADAM_B2) * (gv * gv)
        d_ref[...] = -ADAM_LR * ((nm / c1) / (jnp.sqrt(nv / c2) + ADAM_EPS) + ADAM_WD * w_ref[...])
        nm_ref[...] = nm
        nv_ref[...] = nv

    spec = pl.BlockSpec((tr, cols), lambda i: (i, 0))
    shp = jax.ShapeDtypeStruct((rows, cols), F32)
    return pl.pallas_call(body, grid=(rows // tr,), in_specs=[spec] * 4, out_specs=[spec] * 3, out_shape=[shp] * 3,
                          compiler_params=_params("parallel"), name=name)(w, g, m, v)


def _adamw_halves(w, m, v, mine, other, c_idx, *, name):
    nl, r, c = w.shape
    rh = r // 2
    tr = _row_tile(rh, c * 4)
    c1 = 1.0 - ADAM_B1 ** ADAM_STEP
    c2 = 1.0 - ADAM_B2 ** ADAM_STEP

    def body(c_ref, w_ref, m_ref, v_ref, a_ref, b_ref, g_ref, d_ref, nm_ref, nv_ref):
        gv = jnp.where(pl.program_id(1) == c_ref[0], a_ref[...], b_ref[...])
        nm = ADAM_B1 * m_ref[...] + (1.0 - ADAM_B1) * gv
        nv = ADAM_B2 * v_ref[...] + (1.0 - ADAM_B2) * (gv * gv)
        g_ref[...] = gv
        d_ref[...] = -ADAM_LR * ((nm / c1) / (jnp.sqrt(nv / c2) + ADAM_EPS) + ADAM_WD * w_ref[...])
        nm_ref[...] = nm
        nv_ref[...] = nv

    full = pl.BlockSpec((None, None, tr, c), lambda l, h, i, c_ref: (l, h, i, 0))
    half = pl.BlockSpec((None, tr, c), lambda l, h, i, c_ref: (l, i, 0))
    grid_spec = pltpu.PrefetchScalarGridSpec(num_scalar_prefetch=1, grid=(nl, 2, rh // tr),
                                             in_specs=[full] * 3 + [half] * 2, out_specs=[full] * 4)
    shp = jax.ShapeDtypeStruct((nl, 2, rh, c), F32)
    view = (nl, 2, rh, c)
    outs = pl.pallas_call(body, grid_spec=grid_spec, out_shape=[shp] * 4, compiler_params=_params("parallel", "parallel", "parallel"),
                          name=name)(c_idx, w.reshape(view), m.reshape(view), v.reshape(view), mine, other)
    return [o.reshape(nl, r, c) for o in outs]


WEIGHTS = ["mem_ln_g", "mem_ln_b", "w_in", "sg_ln_g", "sg_ln_b", "sg_w", "sg_b", "conv_w", "conv_b", "dt_bias", "a_log",
           "d_skip", "ssm_norm_g", "p_a", "p_b", "w_mix_o", "w_xq", "w_xkv", "w_xo", "w_ffn_in", "w_ffn_out", "ln_g", "ln_b"]
ARG_NAMES = ["x", "mem"] + WEIGHTS + ["loss_target"] + ["m_" + n for n in WEIGHTS] + ["v_" + n for n in WEIGHTS]
BIG = {"w_in": (1, (1024, 9248)), "p_a": (0, (1024, 1024)), "p_b": (0, (2048, 1024)), "w_mix_o": (0, (1024, 1024)),
       "w_xq": (0, (1024, 1024)), "w_xkv": (1, (1024, 2048)), "w_xo": (0, (1024, 1024)), "w_ffn_in": (1, (1024, 5632)),
       "w_ffn_out": (0, (2816, 1024))}
SMALL_SHARDED = {"conv_w": (4, 3072), "ln_g": (3, 1024), "ln_b": (3, 1024)}
SMALL = [n for n in WEIGHTS if n not in BIG]
XBC_IN0, DT_COL0, DT_COL1 = 4096, 7168, 7200
GATHER_KIND = {"w_in": "chip", "p_a": "row", "p_b": "row", "w_mix_o": "row", "w_xq": "row", "w_xkv": "col", "w_xo": "row",
               "w_ffn_in": "col", "w_ffn_out": "row", "conv_w": "chip", "ln_g": "chip", "ln_b": "chip"}
GRAD_VIEW = {n: ("col" if k == "col" else "chip") for n, k in GATHER_KIND.items() if n in BIG}


def _shard_shape(name):
    axis, (r, c) = BIG[name]
    return (r // N_CHIPS, c) if axis == 0 else (r, c // N_CHIPS)


def _pad_rows(flat, cols, row_mult):
    n = flat.shape[0]
    rows = -(-n // cols)
    rows = -(-rows // row_mult) * row_mult
    return jnp.pad(flat, (0, rows * cols - n)).reshape(rows, cols)


def _gather_weights(a, chip):
    names = list(BIG) + list(SMALL_SHARDED)
    kinds = [GATHER_KIND[n] for n in names]
    cpre = chip.reshape(1)
    bufs = [_cast_place(a[n], GATHER_KIND[n], MXU_DTYPE if n in BIG else F32, cpre, name=f"place_{n}") for n in names]
    outs = _gather_params(bufs, [a[n].shape[1:] for n in names], kinds, name="gather_weights")
    full = dict(zip(names, outs))
    for n in names:
        if GATHER_KIND[n] == "chip":
            _, _, r, c = full[n].shape
            full[n] = jnp.transpose(full[n], (0, 2, 1, 3)).reshape(DEPTH, r, N_CHIPS * c)
    return full


def _layer_weights(a, full, l):
    w_in = full["w_in"][l]
    w = {n: (full[n], l) for n in BIG if n != "w_in"}
    w["w_main"] = jnp.concatenate([w_in[:, :XBC_IN0], w_in[:, DT_COL1:], w_in[:, XBC_IN0:DT_COL0]], axis=1)
    w["w_dt"] = jnp.pad(w_in[:, DT_COL0:DT_COL1], ((0, 0), (0, HEAD_PAD - SSM_HEADS)))
    for n in SMALL_SHARDED:
        w[n] = full[n][l]
    for n in ["sg_ln_g", "sg_ln_b", "sg_w", "conv_b", "ssm_norm_g"]:
        w[n] = a[n][l]
    w["sg_bcol"] = a["sg_b"][l][..., None]
    for n in ["dt_bias", "a_log", "d_skip"]:
        w[n + "8"] = _pad_heads(a[n][l])
    return w


def _reduce_big_grads(grads, c_idx, chip):
    gs, views, keys = [], [], []
    for n in BIG:
        axis, _ = BIG[n]
        r, c = _shard_shape(n)
        for l in range(DEPTH):
            if n == "w_in":
                gm, gd = grads[l]["w_main"], grads[l]["w_dt"]
                gfull = jnp.concatenate([gm[:, :XBC_IN0], gm[:, XBC_COL0:], gd[:, :SSM_HEADS], gm[:, GAB_COL0:XBC_COL0]], axis=1)
                g = jnp.transpose(gfull.reshape(r, N_CHIPS, c), (1, 0, 2))
            elif axis == 0:
                g = grads[l][n].reshape(N_CHIPS, r, c)
            else:
                g = grads[l][n]
            gs.append(g)
            views.append(GRAD_VIEW[n])
            keys.append((n, l))
    recv = _grads_to_sibling(gs, views, name="grads_to_sibling")
    cpre = c_idx.reshape(1)
    pairs = [_pair_sum(g, rv, v, cpre, name=f"grads_pair_sum_{n}_{l}") for g, rv, v, (n, l) in zip(gs, recv, views, keys)]
    quads = _grads_to_chips(pairs, views, name="grads_to_chips")
    jpre = chip.reshape(1)
    tots = [_quad_sum(pairs[DEPTH * i:DEPTH * (i + 1)], quads[DEPTH * i:DEPTH * (i + 1)], GRAD_VIEW[n], jpre,
                      name=f"grads_chip_sum_{n}") for i, n in enumerate(BIG)]
    others = _grads_share(tots, name="grads_share")
    return {n: (t, o) for n, t, o in zip(BIG, tots, others)}


def _reduce_small_grads(small, chip, c_idx):
    names = list(small)
    flat = jnp.concatenate([small[n].reshape(-1) for n in names])
    packed = _pad_rows(flat, LANE, SUBLANE)
    g8 = _all_gather8(packed, name="gather_small_grads")
    tot = _sum_devices(g8, packed, (2 * chip + c_idx).reshape(1), name="small_grads_sum").reshape(-1)
    out, off = {}, 0
    for n in names:
        sz = small[n].size
        full = tot[off:off + sz].reshape(small[n].shape)
        off += sz
        if n in SMALL_SHARDED:
            cs = SMALL_SHARDED[n][1] // N_CHIPS
            full = lax.dynamic_slice_in_dim(full, chip * cs, cs, axis=-1)
        out[n] = full
    return out


def kernel(x, mem, mem_ln_g, mem_ln_b, w_in, sg_ln_g, sg_ln_b, sg_w, sg_b, conv_w, conv_b, dt_bias, a_log, d_skip, ssm_norm_g, p_a, p_b, w_mix_o, w_xq, w_xkv, w_xo, w_ffn_in, w_ffn_out, ln_g, ln_b, loss_target, m_mem_ln_g, m_mem_ln_b, m_w_in, m_sg_ln_g, m_sg_ln_b, m_sg_w, m_sg_b, m_conv_w, m_conv_b, m_dt_bias, m_a_log, m_d_skip, m_ssm_norm_g, m_p_a, m_p_b, m_w_mix_o, m_w_xq, m_w_xkv, m_w_xo, m_w_ffn_in, m_w_ffn_out, m_ln_g, m_ln_b, v_mem_ln_g, v_mem_ln_b, v_w_in, v_sg_ln_g, v_sg_ln_b, v_sg_w, v_sg_b, v_conv_w, v_conv_b, v_dt_bias, v_a_log, v_d_skip, v_ssm_norm_g, v_p_a, v_p_b, v_w_mix_o, v_w_xq, v_w_xkv, v_w_xo, v_w_ffn_in, v_w_ffn_out, v_ln_g, v_ln_b):
    a = dict(zip(ARG_NAMES, (x, mem, mem_ln_g, mem_ln_b, w_in, sg_ln_g, sg_ln_b, sg_w, sg_b, conv_w, conv_b, dt_bias, a_log, d_skip, ssm_norm_g, p_a, p_b, w_mix_o, w_xq, w_xkv, w_xo, w_ffn_in, w_ffn_out, ln_g, ln_b, loss_target, m_mem_ln_g, m_mem_ln_b, m_w_in, m_sg_ln_g, m_sg_ln_b, m_sg_w, m_sg_b, m_conv_w, m_conv_b, m_dt_bias, m_a_log, m_d_skip, m_ssm_norm_g, m_p_a, m_p_b, m_w_mix_o, m_w_xq, m_w_xkv, m_w_xo, m_w_ffn_in, m_w_ffn_out, m_ln_g, m_ln_b, v_mem_ln_g, v_mem_ln_b, v_w_in, v_sg_ln_g, v_sg_ln_b, v_sg_w, v_sg_b, v_conv_w, v_conv_b, v_dt_bias, v_a_log, v_d_skip, v_ssm_norm_g, v_p_a, v_p_b, v_w_mix_o, v_w_xq, v_w_xkv, v_w_xo, v_w_ffn_in, v_w_ffn_out, v_ln_g, v_ln_b)))
    c_idx = lax.axis_index("c").astype(jnp.int32)
    chip = (2 * lax.axis_index("x") + lax.axis_index("y")).astype(jnp.int32)

    full = _gather_weights(a, chip)
    layers = [_layer_weights(a, full, l) for l in range(DEPTH)]
    lsum, grad_x, grads, d_mem_g, d_mem_b = _local_step(x, mem, loss_target, mem_ln_g, mem_ln_b, layers)
    loss = lax.psum(0.5 * jnp.sum(lsum) / D_MODEL, ("x", "y", "c"))

    halves = _reduce_big_grads(grads, c_idx, chip)
    gw = {}
    small = {"mem_ln_g": d_mem_g, "mem_ln_b": d_mem_b}
    for n in SMALL:
        if n in small:
            continue
        per_layer = []
        for l in range(DEPTH):
            g = grads[l][n]
            if n in ("dt_bias", "a_log", "d_skip"):
                g = g[0, :SSM_HEADS]
            per_layer.append(g.reshape(a[n].shape[1:-1] + (-1,)))
        small[n] = jnp.stack(per_layer)
    gw.update(_reduce_small_grads(small, chip, c_idx))

    delta, new_m, new_v = {}, {}, {}
    for n in BIG:
        mine, other = halves[n]
        gw[n], delta[n], new_m[n], new_v[n] = _adamw_halves(a[n], a["m_" + n], a["v_" + n], mine, other, c_idx.reshape(1),
                                                             name=f"adamw_{n}")
    packs = [_pad_rows(jnp.concatenate([src(n).reshape(-1) for n in SMALL]), LANE, SUBLANE)
             for src in (lambda n: a[n], lambda n: gw[n], lambda n: a["m_" + n], lambda n: a["v_" + n])]
    outs = _adamw(*packs, name="adamw_small")
    off = 0
    for n in SMALL:
        sz, shp = a[n].size, a[n].shape
        delta[n], new_m[n], new_v[n] = (o.reshape(-1)[off:off + sz].reshape(shp) for o in outs)
        off += sz
    return (loss, grad_x, *[gw[n].reshape(a[n].shape) for n in WEIGHTS], *[delta[n] for n in WEIGHTS],
            *[new_m[n] for n in WEIGHTS], *[new_v[n] for n in WEIGHTS])
```

```python
import functools
import math

import jax
import jax.numpy as jnp
from jax import lax
from jax.experimental import pallas as pl
from jax.experimental.pallas import tpu as pltpu

F32 = jnp.float32
MXU_DTYPE = jnp.bfloat16
WIRE_DTYPE = jnp.bfloat16
HIGHEST = lax.Precision.HIGHEST

D_MODEL = 1024
DEPTH = 2
CHUNK = 128
SG_GROUPS = 8
SSM_INNER = 2048
SSM_HEADDIM = 64
SSM_HEADS = 32
SSM_STATE = 128
SSM_GROUPS = 4
SSM_CONV = 4
SSM_CONV_DIM = 3072
X_HEADS = 4
X_HEADDIM = 256
FFN_HIDDEN = 2816
ALPHA = float((2 * DEPTH) ** 0.25)
LN_EPS = 1e-5
RMS_EPS = 1e-5
ADAM_LR = 0.001
ADAM_B1 = 0.9
ADAM_B2 = 0.999
ADAM_EPS = 1e-08
ADAM_WD = 0.01
ADAM_STEP = 10

MAIN_COLS = 9216
UVZ_COLS = 4096
GAB_COL0 = 4096
XBC_COL0 = 6144
HEAD_PAD = 128

VMEM_LIMIT = 56 * 1024 * 1024
BLOCK_BYTES = 1024 * 1024
LANE = 128
SUBLANE = 8

N_CHIPS = 4
N_DEV = 8


def _pick(n, cands):
    for c in cands:
        if n % c == 0:
            return c
    return n


MM_TILE_MAX = 1408
MM_OPERAND_BYTES = 8 * 1024 * 1024


def _div_tile(n, limit):
    best = None
    for t in range(LANE, min(n, limit) + 1, LANE):
        if n % t == 0:
            best = t
    return n if best is None else best


def _params(*sem):
    return pltpu.CompilerParams(dimension_semantics=tuple(sem), vmem_limit_bytes=VMEM_LIMIT)


_ANY = pl.BlockSpec(memory_space=pl.ANY)
_MESH = pl.DeviceIdType.MESH


def _nt(a, b):
    return lax.dot_general(a, b, (((1,), (1,)), ((), ())), preferred_element_type=F32)


def _tn(a, b):
    return lax.dot_general(a, b, (((0,), (0,)), ((), ())), preferred_element_type=F32)


def _nn(a, b):
    return jnp.dot(a, b, preferred_element_type=F32)


def _sigmoid(x):
    return 1.0 / (1.0 + jnp.exp(-x))


def _gelu(x):
    return 0.5 * x * (1.0 + lax.erf(x * (2.0 ** -0.5)))


def _gelu_grad(x):
    return 0.5 * (1.0 + lax.erf(x * (2.0 ** -0.5))) + x * jnp.exp(-0.5 * x * x) * (1.0 / math.sqrt(2.0 * math.pi))


def _mm(a, b, *, ta=False, tb=False, out_dtype=F32, name):
    b, bl = b if isinstance(b, tuple) else (b, None)
    if ta:
        kdim, m = a.shape
    else:
        m, kdim = a.shape
    if tb:
        n, k2 = b.shape[-2:]
    else:
        k2, n = b.shape[-2:]
    assert kdim == k2, (a.shape, b.shape, ta, tb)
    tm = _div_tile(m, MM_TILE_MAX)
    tn = _div_tile(n, MM_TILE_MAX)
    tk = _div_tile(kdim, MM_OPERAND_BYTES // (tm * a.dtype.itemsize + tn * b.dtype.itemsize))
    nk = kdim // tk
    dn = (((0 if ta else 1,), (1 if tb else 0,)), ((), ()))

    def body(a_ref, b_ref, o_ref, *scratch):
        d = lax.dot_general(a_ref[...].astype(MXU_DTYPE), b_ref[...].astype(MXU_DTYPE), dn, preferred_element_type=F32)
        if nk == 1:
            o_ref[...] = d.astype(out_dtype)
            return
        acc_ref, = scratch
        k = pl.program_id(2)

        @pl.when(k == 0)
        def _():
            acc_ref[...] = d

        @pl.when(jnp.logical_and(k > 0, k < nk - 1))
        def _():
            acc_ref[...] += d

        @pl.when(k == nk - 1)
        def _():
            o_ref[...] = (acc_ref[...] + d).astype(out_dtype)

    a_spec = pl.BlockSpec((tk, tm), lambda i, j, k: (k, i)) if ta else pl.BlockSpec((tm, tk), lambda i, j, k: (i, k))
    if bl is None:
        b_spec = pl.BlockSpec((tn, tk), lambda i, j, k: (j, k)) if tb else pl.BlockSpec((tk, tn), lambda i, j, k: (k, j))
    elif tb:
        b_spec = pl.BlockSpec((None, tn, tk), lambda i, j, k: (bl, j, k))
    else:
        b_spec = pl.BlockSpec((None, tk, tn), lambda i, j, k: (bl, k, j))
    return pl.pallas_call(
        body, grid=(m // tm, n // tn, nk), in_specs=[a_spec, b_spec],
        out_specs=pl.BlockSpec((tm, tn), lambda i, j, k: (i, j)),
        out_shape=jax.ShapeDtypeStruct((m, n), out_dtype),
        scratch_shapes=[pltpu.VMEM((tm, tn), F32)] if nk > 1 else [],
        compiler_params=_params("parallel", "parallel", "arbitrary"), name=name)(a, b)


def _row_spec(tm, c, col=0):
    return pl.BlockSpec((tm, c), lambda i: (i, col))


def _par_spec(shape):
    nd = len(shape)
    return pl.BlockSpec(shape, lambda i: (0,) * nd)


def _ln_fwd(x, f, g, b, *, name):
    t, c = x.shape
    tm = _pick(t, (256, 128))
    has_f = f is not None

    def body(*refs):
        if has_f:
            x_ref, f_ref, g_ref, b_ref, y_ref, yb_ref, xh_ref, rs_ref = refs
            r = ALPHA * x_ref[...] + f_ref[...]
        else:
            x_ref, g_ref, b_ref, y_ref, yb_ref, xh_ref, rs_ref = refs
            r = x_ref[...]
        mu = jnp.mean(r, axis=-1, keepdims=True)
        xc = r - mu
        var = jnp.mean(xc * xc, axis=-1, keepdims=True)
        rstd = lax.rsqrt(var + LN_EPS)
        xh = xc * rstd
        y = xh * g_ref[...] + b_ref[...]
        y_ref[...] = y
        yb_ref[...] = y.astype(MXU_DTYPE)
        xh_ref[...] = xh
        rs_ref[...] = rstd

    ins = [x] + ([f] if has_f else []) + [g.reshape(1, c), b.reshape(1, c)]
    in_specs = [_row_spec(tm, c)] * (2 if has_f else 1) + [_par_spec((1, c))] * 2
    return pl.pallas_call(
        body, grid=(t // tm,), in_specs=in_specs,
        out_specs=[_row_spec(tm, c), _row_spec(tm, c), _row_spec(tm, c), _row_spec(tm, 1)],
        out_shape=[jax.ShapeDtypeStruct((t, c), F32), jax.ShapeDtypeStruct((t, c), MXU_DTYPE),
                   jax.ShapeDtypeStruct((t, c), F32), jax.ShapeDtypeStruct((t, 1), F32)],
        compiler_params=_params("parallel"), name=name)(*ins)


def _ln_bwd(addends, scales, xh, rs, g, *, name):
    t, c = xh.shape
    tm = _pick(t, (256, 128))
    na = len(addends)

    def body(*refs):
        a_refs = refs[:na]
        xh_ref, rs_ref, g_ref, dp_ref, dpb_ref, dg_ref, db_ref = refs[na:]

        @pl.when(pl.program_id(0) == 0)
        def _():
            dg_ref[...] = jnp.zeros_like(dg_ref)
            db_ref[...] = jnp.zeros_like(db_ref)

        dy = None
        for s, r in zip(scales, a_refs):
            term = r[...] if s == 1.0 else s * r[...]
            dy = term if dy is None else dy + term
        xhv = xh_ref[...]
        dxh = dy * g_ref[...]
        m1 = jnp.mean(dxh, axis=-1, keepdims=True)
        m2 = jnp.mean(dxh * xhv, axis=-1, keepdims=True)
        dp = rs_ref[...] * (dxh - m1 - xhv * m2)
        dp_ref[...] = dp
        dpb_ref[...] = dp.astype(MXU_DTYPE)
        dg_ref[...] += jnp.sum(dy * xhv, axis=0, keepdims=True)
        db_ref[...] += jnp.sum(dy, axis=0, keepdims=True)

    in_specs = [_row_spec(tm, c)] * (na + 1) + [_row_spec(tm, 1), _par_spec((1, c))]
    return pl.pallas_call(
        body, grid=(t // tm,), in_specs=in_specs,
        out_specs=[_row_spec(tm, c), _row_spec(tm, c), _par_spec((1, c)), _par_spec((1, c))],
        out_shape=[jax.ShapeDtypeStruct((t, c), F32), jax.ShapeDtypeStruct((t, c), MXU_DTYPE),
                   jax.ShapeDtypeStruct((1, c), F32), jax.ShapeDtypeStruct((1, c), F32)],
        compiler_params=_params("arbitrary"), name=name)(*addends, xh, rs, g.reshape(1, c))


def _add_scaled(addends, scales, *, name):
    t, c = addends[0].shape
    tm = _pick(t, (256, 128))
    na = len(addends)

    def body(*refs):
        acc = None
        for s, r in zip(scales, refs[:na]):
            term = r[...] if s == 1.0 else s * r[...]
            acc = term if acc is None else acc + term
        refs[na][...] = acc

    return pl.pallas_call(
        body, grid=(t // tm,), in_specs=[_row_spec(tm, c)] * na, out_specs=_row_spec(tm, c),
        out_shape=jax.ShapeDtypeStruct((t, c), F32), compiler_params=_params("parallel"), name=name)(*addends)


def _loss_head(y, tgt, *, name):
    t, c = y.shape
    tm = _pick(t, (256, 128))

    def body(y_ref, t_ref, dy_ref, ls_ref):
        @pl.when(pl.program_id(0) == 0)
        def _():
            ls_ref[...] = jnp.zeros_like(ls_ref)

        e = y_ref[...] - t_ref[...]
        dy_ref[...] = e * (1.0 / c)
        ls_ref[...] += jnp.sum(e * e, axis=0, keepdims=True)

    return pl.pallas_call(
        body, grid=(t // tm,), in_specs=[_row_spec(tm, c)] * 2,
        out_specs=[_row_spec(tm, c), _par_spec((1, c))],
        out_shape=[jax.ShapeDtypeStruct((t, c), F32), jax.ShapeDtypeStruct((1, c), F32)],
        compiler_params=_params("arbitrary"), name=name)(y, tgt)


def _swiglu_fwd(h, *, name):
    t, two_f = h.shape
    fh = two_f // 2
    tm = _pick(t, (256, 128))

    def body(g_ref, u_ref, a_ref):
        g = g_ref[...]
        a_ref[...] = (g * _sigmoid(g) * u_ref[...]).astype(MXU_DTYPE)

    return pl.pallas_call(
        body, grid=(t // tm,), in_specs=[_row_spec(tm, fh, 0), _row_spec(tm, fh, 1)], out_specs=_row_spec(tm, fh),
        out_shape=jax.ShapeDtypeStruct((t, fh), MXU_DTYPE), compiler_params=_params("parallel"), name=name)(h, h)


def _swiglu_bwd(h, da, *, name):
    t, two_f = h.shape
    fh = two_f // 2
    tm = _pick(t, (256, 128))

    def body(g_ref, u_ref, da_ref, dh_ref):
        g = g_ref[...]
        s = _sigmoid(g)
        dav = da_ref[...]
        dh_ref[:, :fh] = (dav * u_ref[...] * (s * (1.0 + g * (1.0 - s)))).astype(MXU_DTYPE)
        dh_ref[:, fh:] = (dav * g * s).astype(MXU_DTYPE)

    return pl.pallas_call(
        body, grid=(t // tm,), in_specs=[_row_spec(tm, fh, 0), _row_spec(tm, fh, 1), _row_spec(tm, fh)],
        out_specs=_row_spec(tm, two_f), out_shape=jax.ShapeDtypeStruct((t, two_f), MXU_DTYPE),
        compiler_params=_params("parallel"), name=name)(h, h, da)


def _attn_probs(q, k):
    s = _nt(q, k) * (X_HEADDIM ** -0.5)
    s = s - jnp.max(s, axis=-1, keepdims=True)
    p = jnp.exp(s)
    return p / jnp.sum(p, axis=-1, keepdims=True)


def _attn_fwd(q, kv, *, bsz, name):
    t = q.shape[0]
    s = t // bsz
    ml = kv.shape[0] // bsz
    hd = X_HEADDIM

    def body(q_ref, k_ref, v_ref, o_ref):
        p = _attn_probs(q_ref[...], k_ref[...])
        o_ref[...] = _nn(p.astype(MXU_DTYPE), v_ref[...]).astype(MXU_DTYPE)

    return pl.pallas_call(
        body, grid=(bsz, X_HEADS),
        in_specs=[pl.BlockSpec((s, hd), lambda b, h: (b, h)), pl.BlockSpec((ml, hd), lambda b, h: (b, h)),
                  pl.BlockSpec((ml, hd), lambda b, h: (b, X_HEADS + h))],
        out_specs=pl.BlockSpec((s, hd), lambda b, h: (b, h)),
        out_shape=jax.ShapeDtypeStruct((t, D_MODEL), MXU_DTYPE),
        compiler_params=_params("parallel", "parallel"), name=name)(q, kv, kv)


def _attn_bwd(q, kv, do, *, bsz, name):
    t = q.shape[0]
    s = t // bsz
    ml = kv.shape[0] // bsz
    hd = X_HEADDIM

    def body(q_ref, k_ref, v_ref, do_ref, dq_ref, dk_ref, dv_ref):
        qv, kk, vv, dov = q_ref[...], k_ref[...], v_ref[...], do_ref[...]
        p = _attn_probs(qv, kk)
        dp = _nt(dov, vv)
        dv_ref[...] = _tn(p.astype(MXU_DTYPE), dov).astype(MXU_DTYPE)
        ds = (p * (dp - jnp.sum(dp * p, axis=-1, keepdims=True)) * (X_HEADDIM ** -0.5)).astype(MXU_DTYPE)
        dq_ref[...] = _nn(ds, kk).astype(MXU_DTYPE)
        dk_ref[...] = _tn(ds, qv).astype(MXU_DTYPE)

    blk_q = pl.BlockSpec((s, hd), lambda b, h: (b, h))
    blk_m = pl.BlockSpec((ml, hd), lambda b, h: (b, h))
    return pl.pallas_call(
        body, grid=(bsz, X_HEADS),
        in_specs=[blk_q, blk_m, pl.BlockSpec((ml, hd), lambda b, h: (b, X_HEADS + h)), blk_q],
        out_specs=[blk_q, blk_m, blk_m],
        out_shape=[jax.ShapeDtypeStruct((t, D_MODEL), MXU_DTYPE), jax.ShapeDtypeStruct((bsz * ml, D_MODEL), MXU_DTYPE),
                   jax.ShapeDtypeStruct((bsz * ml, D_MODEL), MXU_DTYPE)],
        compiler_params=_params("parallel", "parallel"), name=name)(q, kv, kv, do)


def _causal(n):
    row = lax.broadcasted_iota(jnp.int32, (n, n), 0)
    col = lax.broadcasted_iota(jnp.int32, (n, n), 1)
    return row >= col


def _sg_norm(v, g, b):
    gv = _gelu(v)
    mu = jnp.mean(gv, axis=-1, keepdims=True)
    xc = gv - mu
    var = jnp.mean(xc * xc, axis=-1, keepdims=True)
    rstd = lax.rsqrt(var + LN_EPS)
    xh = xc * rstd
    return xh, rstd, xh * g + b


def _sg_fwd(proj, ln_g, ln_b, w, bcol, *, name):
    t = proj.shape[0]
    c = D_MODEL
    gd = c // SG_GROUPS

    def body(u_ref, v_ref, g_ref, b_ref, w_ref, bc_ref, o_ref):
        gu = _gelu(u_ref[...])
        _, _, vn = _sg_norm(v_ref[...], g_ref[...], b_ref[...])
        mask = _causal(CHUNK)
        for g in range(SG_GROUPS):
            sl = slice(g * gd, (g + 1) * gd)
            wg = jnp.where(mask, w_ref[g], 0.0).astype(MXU_DTYPE)
            mixed = _nn(wg, vn[:, sl].astype(MXU_DTYPE)) + bc_ref[g]
            o_ref[:, sl] = (gu[:, sl] * mixed).astype(MXU_DTYPE)

    return pl.pallas_call(
        body, grid=(t // CHUNK,),
        in_specs=[_row_spec(CHUNK, c, 0), _row_spec(CHUNK, c, 1), _par_spec((1, c)), _par_spec((1, c)),
                  _par_spec((SG_GROUPS, CHUNK, CHUNK)), _par_spec((SG_GROUPS, CHUNK, 1))],
        out_specs=_row_spec(CHUNK, c), out_shape=jax.ShapeDtypeStruct((t, c), MXU_DTYPE),
        compiler_params=_params("parallel"), name=name)(proj, proj, ln_g.reshape(1, c), ln_b.reshape(1, c), w, bcol)


def _sg_bwd(proj, dsgo, ln_g, ln_b, w, bcol, dproj, *, name):
    t = proj.shape[0]
    c = D_MODEL
    gd = c // SG_GROUPS

    def body(u_ref, v_ref, d_ref, g_ref, b_ref, w_ref, bc_ref, _, duv_ref, dw_ref, dbc_ref, dg_ref, db_ref, dvn_ref):
        @pl.when(pl.program_id(0) == 0)
        def _():
            dw_ref[...] = jnp.zeros_like(dw_ref)
            dbc_ref[...] = jnp.zeros_like(dbc_ref)
            dg_ref[...] = jnp.zeros_like(dg_ref)
            db_ref[...] = jnp.zeros_like(db_ref)

        u = u_ref[...]
        v = v_ref[...]
        dso = d_ref[...]
        gu = _gelu(u)
        xh, rstd, vn = _sg_norm(v, g_ref[...], b_ref[...])
        mask = _causal(CHUNK)
        for g in range(SG_GROUPS):
            sl = slice(g * gd, (g + 1) * gd)
            wg = jnp.where(mask, w_ref[g], 0.0).astype(MXU_DTYPE)
            vng = vn[:, sl].astype(MXU_DTYPE)
            mixed = _nn(wg, vng) + bc_ref[g]
            duv_ref[:, sl] = (dso[:, sl] * mixed * _gelu_grad(u[:, sl])).astype(MXU_DTYPE)
            dmix = dso[:, sl] * gu[:, sl]
            dmb = dmix.astype(MXU_DTYPE)
            dbc_ref[g] += jnp.sum(dmix, axis=-1, keepdims=True)
            dw_ref[g] += jnp.where(mask, _nt(dmb, vng), 0.0)
            dvn_ref[:, sl] = _tn(wg, dmb)
        dvn = dvn_ref[...]
        dg_ref[...] += jnp.sum(dvn * xh, axis=0, keepdims=True)
        db_ref[...] += jnp.sum(dvn, axis=0, keepdims=True)
        dxh = dvn * g_ref[...]
        m1 = jnp.mean(dxh, axis=-1, keepdims=True)
        m2 = jnp.mean(dxh * xh, axis=-1, keepdims=True)
        dgv = rstd * (dxh - m1 - xh * m2)
        duv_ref[:, c:] = (dgv * _gelu_grad(v)).astype(MXU_DTYPE)

    return pl.pallas_call(
        body, grid=(t // CHUNK,),
        in_specs=[_row_spec(CHUNK, c, 0), _row_spec(CHUNK, c, 1), _row_spec(CHUNK, c), _par_spec((1, c)),
                  _par_spec((1, c)), _par_spec((SG_GROUPS, CHUNK, CHUNK)), _par_spec((SG_GROUPS, CHUNK, 1)), _ANY],
        out_specs=[_row_spec(CHUNK, 2 * c), _par_spec((SG_GROUPS, CHUNK, CHUNK)), _par_spec((SG_GROUPS, CHUNK, 1)),
                   _par_spec((1, c)), _par_spec((1, c))],
        out_shape=[jax.ShapeDtypeStruct(dproj.shape, dproj.dtype), jax.ShapeDtypeStruct((SG_GROUPS, CHUNK, CHUNK), F32),
                   jax.ShapeDtypeStruct((SG_GROUPS, CHUNK, 1), F32), jax.ShapeDtypeStruct((1, c), F32),
                   jax.ShapeDtypeStruct((1, c), F32)],
        scratch_shapes=[pltpu.VMEM((CHUNK, c), F32)], input_output_aliases={7: 0},
        compiler_params=_params("arbitrary"), name=name)(proj, proj, dsgo, ln_g.reshape(1, c), ln_b.reshape(1, c), w, bcol, dproj)


CONV_TC = 512


def _conv_pre(x, w_ref, b_ref, rows):
    acc = x * w_ref[SSM_CONV - 1:SSM_CONV, :] + b_ref[...]
    for k in range(SSM_CONV - 1):
        sh = SSM_CONV - 1 - k
        xs = jnp.where(rows >= sh, pltpu.roll(x, sh, axis=0), 0.0)
        acc = acc + xs * w_ref[k:k + 1, :]
    return acc


def _conv_fwd(proj, w, b, *, bsz, name):
    t = proj.shape[0]
    s = t // bsz
    nj = SSM_CONV_DIM // CONV_TC
    c0 = XBC_COL0 // CONV_TC

    def body(x_ref, w_ref, b_ref, o_ref):
        x = x_ref[...]
        rows = lax.broadcasted_iota(jnp.int32, x.shape, 0)
        pre = _conv_pre(x, w_ref, b_ref, rows)
        o_ref[...] = pre * _sigmoid(pre)

    return pl.pallas_call(
        body, grid=(bsz, nj),
        in_specs=[pl.BlockSpec((s, CONV_TC), lambda bb, j: (bb, c0 + j)), pl.BlockSpec((SSM_CONV, CONV_TC), lambda bb, j: (0, j)),
                  pl.BlockSpec((1, CONV_TC), lambda bb, j: (0, j))],
        out_specs=pl.BlockSpec((s, CONV_TC), lambda bb, j: (bb, j)),
        out_shape=jax.ShapeDtypeStruct((t, SSM_CONV_DIM), F32),
        compiler_params=_params("parallel", "parallel"), name=name)(proj, w, b.reshape(1, -1))


def _conv_bwd(proj, dact, w, b, dproj, *, bsz, name):
    t = proj.shape[0]
    s = t // bsz
    nj = SSM_CONV_DIM // CONV_TC
    c0 = XBC_COL0 // CONV_TC

    def body(x_ref, d_ref, w_ref, b_ref, _, dx_ref, dw_ref, db_ref):
        @pl.when(pl.program_id(1) == 0)
        def _():
            dw_ref[...] = jnp.zeros_like(dw_ref)
            db_ref[...] = jnp.zeros_like(db_ref)

        x = x_ref[...]
        rows = lax.broadcasted_iota(jnp.int32, x.shape, 0)
        pre = _conv_pre(x, w_ref, b_ref, rows)
        sg = _sigmoid(pre)
        dpre = d_ref[...] * (sg * (1.0 + pre * (1.0 - sg)))
        db_ref[...] += jnp.sum(dpre, axis=0, keepdims=True)
        dx = dpre * w_ref[SSM_CONV - 1:SSM_CONV, :]
        dw_ref[SSM_CONV - 1:SSM_CONV, :] += jnp.sum(dpre * x, axis=0, keepdims=True)
        for k in range(SSM_CONV - 1):
            sh = SSM_CONV - 1 - k
            xs = jnp.where(rows >= sh, pltpu.roll(x, sh, axis=0), 0.0)
            dw_ref[k:k + 1, :] += jnp.sum(dpre * xs, axis=0, keepdims=True)
            dsh = jnp.where(rows < s - sh, pltpu.roll(dpre, s - sh, axis=0), 0.0)
            dx = dx + dsh * w_ref[k:k + 1, :]
        dx_ref[...] = dx.astype(MXU_DTYPE)

    return pl.pallas_call(
        body, grid=(nj, bsz),
        in_specs=[pl.BlockSpec((s, CONV_TC), lambda j, bb: (bb, c0 + j)), pl.BlockSpec((s, CONV_TC), lambda j, bb: (bb, j)),
                  pl.BlockSpec((SSM_CONV, CONV_TC), lambda j, bb: (0, j)), pl.BlockSpec((1, CONV_TC), lambda j, bb: (0, j)), _ANY],
        out_specs=[pl.BlockSpec((s, CONV_TC), lambda j, bb: (bb, c0 + j)), pl.BlockSpec((SSM_CONV, CONV_TC), lambda j, bb: (0, j)),
                   pl.BlockSpec((1, CONV_TC), lambda j, bb: (0, j))],
        out_shape=[jax.ShapeDtypeStruct(dproj.shape, dproj.dtype), jax.ShapeDtypeStruct((SSM_CONV, SSM_CONV_DIM), F32),
                   jax.ShapeDtypeStruct((1, SSM_CONV_DIM), F32)],
        input_output_aliases={4: 0},
        compiler_params=_params("parallel", "arbitrary"), name=name)(proj, dact, w, b.reshape(1, -1), dproj)


def _softplus(x):
    return jnp.maximum(x, 0.0) + jnp.log1p(jnp.exp(-jnp.abs(x)))


def _pad_heads(v):
    return jnp.broadcast_to(jnp.pad(v.astype(F32), (0, HEAD_PAD - SSM_HEADS))[None, :], (SUBLANE, HEAD_PAD))


def _ssd_prep(dt_raw, dt_bias8, a_log8, *, name):
    t = dt_raw.shape[0]
    n = CHUNK

    def body(r_ref, b_ref, al_ref, dt_ref, cs_ref, dtt_ref, cst_ref):
        dt = _softplus(r_ref[...] + b_ref[0:1, :])
        da = dt * (-jnp.exp(al_ref[0:1, :]))
        row = lax.broadcasted_iota(jnp.int32, (n, n), 0)
        col = lax.broadcasted_iota(jnp.int32, (n, n), 1)
        lower = (col <= row).astype(F32)
        upper = (row <= col).astype(F32)
        eye = (row == col).astype(F32)
        dt_ref[...] = dt
        cs_ref[...] = jnp.dot(lower, da, precision=HIGHEST, preferred_element_type=F32)
        dn = (((0,), (0,)), ((), ()))
        cst_ref[0] = lax.dot_general(da, upper, dn, precision=HIGHEST, preferred_element_type=F32)
        dtt_ref[0] = lax.dot_general(dt, eye, dn, precision=HIGHEST, preferred_element_type=F32)

    hp = HEAD_PAD
    return pl.pallas_call(
        body, grid=(t // n,),
        in_specs=[_row_spec(n, hp), _par_spec((SUBLANE, hp)), _par_spec((SUBLANE, hp))],
        out_specs=[_row_spec(n, hp), _row_spec(n, hp), pl.BlockSpec((1, hp, n), lambda i: (i, 0, 0)),
                   pl.BlockSpec((1, hp, n), lambda i: (i, 0, 0))],
        out_shape=[jax.ShapeDtypeStruct((t, hp), F32), jax.ShapeDtypeStruct((t, hp), F32),
                   jax.ShapeDtypeStruct((t // n, hp, n), F32), jax.ShapeDtypeStruct((t // n, hp, n), F32)],
        compiler_params=_params("parallel"), name=name)(dt_raw, dt_bias8, a_log8)


def _expand_mat():
    h = lax.broadcasted_iota(jnp.int32, (HEAD_PAD, SSM_INNER), 0)
    ch = lax.broadcasted_iota(jnp.int32, (HEAD_PAD, SSM_INNER), 1)
    return (ch // SSM_HEADDIM == h).astype(F32)


def _reduce_mat():
    ch = lax.broadcasted_iota(jnp.int32, (SSM_INNER, HEAD_PAD), 0)
    h = lax.broadcasted_iota(jnp.int32, (SSM_INNER, HEAD_PAD), 1)
    return (ch // SSM_HEADDIM == h).astype(F32)


def _expand(v, em):
    return jnp.dot(v, em, precision=HIGHEST, preferred_element_type=F32)


def _decay_mat(cs_ref, cst_ref, h, mask):
    seg = cs_ref[:, h:h + 1] - cst_ref[0, h:h + 1, :]
    return jnp.where(mask, jnp.exp(jnp.minimum(seg, 0.0)), 0.0)


GROUP_CH = SSM_INNER // SSM_GROUPS
PAIRS_PER_GROUP = GROUP_CH // LANE
HEADS_PER_GROUP = SSM_HEADS // SSM_GROUPS
BM_COL0 = SSM_INNER
CM_COL0 = SSM_INNER + SSM_GROUPS * SSM_STATE


def _ssd_specs(nc, rev):
    def cidx(i):
        return (i // nc) * nc + (nc - 1 - i % nc) if rev else i

    n = CHUNK
    xs = pl.BlockSpec((n, SSM_INNER), lambda i: (cidx(i), 0))
    bm = pl.BlockSpec((n, GROUP_CH), lambda i: (cidx(i), BM_COL0 // GROUP_CH))
    cm = pl.BlockSpec((n, GROUP_CH), lambda i: (cidx(i), CM_COL0 // GROUP_CH))
    hv = pl.BlockSpec((n, HEAD_PAD), lambda i: (cidx(i), 0))
    hvt = pl.BlockSpec((1, HEAD_PAD, n), lambda i: (cidx(i), 0, 0))
    st = pl.BlockSpec((1, SSM_INNER, SSM_STATE), lambda i: (cidx(i), 0, 0))
    return xs, bm, cm, hv, hvt, st


def _ssd_fwd(xbc, dt, cs, dtt, cst, dskip8, *, nc, name):
    t = xbc.shape[0]
    n = CHUNK
    xs_s, bm_s, cm_s, hv_s, hvt_s, st_s = _ssd_specs(nc, False)

    def body(xs_ref, bm_ref, cm_ref, dt_ref, cs_ref, dtt_ref, cst_ref, dsk_ref, y_ref, st_ref, prev):
        @pl.when(pl.program_id(0) % nc == 0)
        def _():
            prev[...] = jnp.zeros_like(prev)

        st_ref[0] = prev[...]
        em = _expand_mat()
        dtx = _expand(dt_ref[...], em)
        csx = _expand(cs_ref[...], em)
        dskx = _expand(dsk_ref[...], em)[0:1, :]
        xs = xs_ref[...]
        xdt = xs * dtx
        ecs = jnp.exp(csx)
        dec = jnp.exp(csx[n - 1:n, :] - csx)
        mask = _causal(n)
        lane = lax.broadcasted_iota(jnp.int32, (n, LANE), 1)
        for g in range(SSM_GROUPS):
            gs = slice(g * SSM_STATE, (g + 1) * SSM_STATE)
            gc = slice(g * GROUP_CH, (g + 1) * GROUP_CH)
            cmat = cm_ref[:, gs].astype(MXU_DTYPE)
            bmat = bm_ref[:, gs].astype(MXU_DTYPE)
            cb = _nt(cmat, bmat)
            yoff = ecs[:, gc] * _nt(cmat, prev[gc, :].astype(MXU_DTYPE))
            for q in range(PAIRS_PER_GROUP):
                hp = g * PAIRS_PER_GROUP + q
                sl = slice(hp * LANE, (hp + 1) * LANE)
                xp = xdt[:, sl].astype(MXU_DTYPE)
                m0 = (cb * _decay_mat(cs_ref, cst_ref, 2 * hp, mask)).astype(MXU_DTYPE)
                m1 = (cb * _decay_mat(cs_ref, cst_ref, 2 * hp + 1, mask)).astype(MXU_DTYPE)
                yd = jnp.where(lane < SSM_HEADDIM, _nn(m0, xp), _nn(m1, xp))
                y_ref[:, sl] = yd + yoff[:, q * LANE:(q + 1) * LANE] + xs[:, sl] * dskx[:, sl]
            snew = _tn((xdt[:, gc] * dec[:, gc]).astype(MXU_DTYPE), bmat)
            for r in range(HEADS_PER_GROUP):
                h = g * HEADS_PER_GROUP + r
                rows = slice(h * SSM_HEADDIM, (h + 1) * SSM_HEADDIM)
                e = jnp.exp(cst_ref[0, h:h + 1, n - 1:n])
                prev[rows, :] = prev[rows, :] * e + snew[r * SSM_HEADDIM:(r + 1) * SSM_HEADDIM, :]

    return pl.pallas_call(
        body, grid=(t // n,),
        in_specs=[xs_s, bm_s, cm_s, hv_s, hv_s, hvt_s, hvt_s, _par_spec((SUBLANE, HEAD_PAD))],
        out_specs=[xs_s, st_s],
        out_shape=[jax.ShapeDtypeStruct((t, SSM_INNER), F32), jax.ShapeDtypeStruct((t // n, SSM_INNER, SSM_STATE), F32)],
        scratch_shapes=[pltpu.VMEM((SSM_INNER, SSM_STATE), F32)],
        compiler_params=_params("arbitrary"), name=name)(xbc, xbc, xbc, dt, cs, dtt, cst, dskip8)


def _ssd_bwd(dy, xbc, dt, cs, dtt, cst, st, dskip8, a_log8, dt_raw, dt_bias8, *, nc, name):
    t = xbc.shape[0]
    n = CHUNK
    xs_s, bm_s, cm_s, hv_s, hvt_s, st_s = _ssd_specs(nc, True)
    acc_s = _par_spec((1, HEAD_PAD))
    xbc_s = pl.BlockSpec((n, SSM_CONV_DIM), xs_s.index_map)

    def body(dy_ref, xs_ref, bm_ref, cm_ref, dt_ref, cs_ref, dtt_ref, cst_ref, st_ref, dsk_ref, al_ref, raw_ref, bias_ref,
             dxbc_ref, ddr_ref, dal_ref, dds_ref, dbias_ref, dprev, dxdt_s, tdec_s, tcs_s):
        @pl.when(pl.program_id(0) % nc == 0)
        def _():
            dprev[...] = jnp.zeros_like(dprev)

        @pl.when(pl.program_id(0) == 0)
        def _():
            dal_ref[...] = jnp.zeros_like(dal_ref)
            dds_ref[...] = jnp.zeros_like(dds_ref)
            dbias_ref[...] = jnp.zeros_like(dbias_ref)

        em = _expand_mat()
        rm = _reduce_mat()

        def head_reduce(v):
            return jnp.dot(v, rm, precision=HIGHEST, preferred_element_type=F32)

        dtv = dt_ref[...]
        csv = cs_ref[...]
        dtx = _expand(dtv, em)
        csx = _expand(csv, em)
        dskx = _expand(dsk_ref[...], em)[0:1, :]
        xs = xs_ref[...]
        dyv = dy_ref[...]
        xdt = xs * dtx
        ecs = jnp.exp(csx)
        dec = jnp.exp(csx[n - 1:n, :] - csx)
        mask = _causal(n)
        lane = lax.broadcasted_iota(jnp.int32, (n, LANE), 1)
        hlane = lax.broadcasted_iota(jnp.int32, (1, HEAD_PAD), 1)
        hsub = lax.broadcasted_iota(jnp.int32, (HEAD_PAD, 1), 0)
        rsum = jnp.zeros((n, HEAD_PAD), F32)
        csum = jnp.zeros((HEAD_PAD, n), F32)
        for g in range(SSM_GROUPS):
            gs = slice(g * SSM_STATE, (g + 1) * SSM_STATE)
            gc = slice(g * GROUP_CH, (g + 1) * GROUP_CH)
            cmat = cm_ref[:, gs].astype(MXU_DTYPE)
            bmat = bm_ref[:, gs].astype(MXU_DTYPE)
            cb = _nt(cmat, bmat)
            pg = st_ref[0, gc, :].astype(MXU_DTYPE)
            dpg = dprev[gc, :]
            dpgb = dpg.astype(MXU_DTYPE)
            z = _nt(cmat, pg)
            dyg = dyv[:, gc]
            dz = (dyg * ecs[:, gc]).astype(MXU_DTYPE)
            dc = _nn(dz, pg)
            dprev_y = _tn(dz, cmat)
            tcs_s[:, gc] = dyg * z * ecs[:, gc]
            xd = xdt[:, gc] * dec[:, gc]
            wmat = _nt(bmat, dpgb)
            db = _nn(xd.astype(MXU_DTYPE), dpgb)
            tdec_s[:, gc] = wmat * xd
            dxdt_g = wmat * dec[:, gc]
            dcb = jnp.zeros((n, n), F32)
            for q in range(PAIRS_PER_GROUP):
                hp = g * PAIRS_PER_GROUP + q
                sl = slice(hp * LANE, (hp + 1) * LANE)
                xp = xdt[:, sl].astype(MXU_DTYPE)
                dyp = dyv[:, sl]
                dypb = dyp.astype(MXU_DTYPE)
                dxp = None
                for hh in range(2):
                    h = 2 * hp + hh
                    lm = _decay_mat(cs_ref, cst_ref, h, mask)
                    mine = (lane < SSM_HEADDIM) if hh == 0 else (lane >= SSM_HEADDIM)
                    dm = _nt(jnp.where(mine, dyp, 0.0).astype(MXU_DTYPE), xp)
                    dml = dm * lm
                    dcb = dcb + dml
                    gseg = dml * cb
                    rsum = rsum + jnp.sum(gseg, axis=1, keepdims=True) * (hlane == h).astype(F32)
                    csum = csum + (hsub == h).astype(F32) * jnp.sum(gseg, axis=0, keepdims=True)
                    dxh = _tn((cb * lm).astype(MXU_DTYPE), dypb)
                    dxp = dxh if dxp is None else jnp.where(mine, dxh, dxp)
                dxdt_s[:, sl] = dxdt_g[:, q * LANE:(q + 1) * LANE] + dxp
            dcbb = dcb.astype(MXU_DTYPE)
            dxbc_ref[:, CM_COL0 + g * SSM_STATE:CM_COL0 + (g + 1) * SSM_STATE] = dc + _nn(dcbb, bmat)
            dxbc_ref[:, BM_COL0 + g * SSM_STATE:BM_COL0 + (g + 1) * SSM_STATE] = db + _tn(dcbb, cmat)
            for r in range(HEADS_PER_GROUP):
                h = g * HEADS_PER_GROUP + r
                rows = slice(h * SSM_HEADDIM, (h + 1) * SSM_HEADDIM)
                lr = slice(r * SSM_HEADDIM, (r + 1) * SSM_HEADDIM)
                e = jnp.exp(cst_ref[0, h:h + 1, n - 1:n])
                dprev[rows, :] = dpg[lr, :] * e + dprev_y[lr, :]
            tq = lax.dot_general(dpg * st_ref[0, gc, :], rm[gc, :], (((0,), (0,)), ((), ())), precision=HIGHEST,
                                 preferred_element_type=F32)
            if g == 0:
                qsum = jnp.sum(tq, axis=0, keepdims=True)
            else:
                qsum = qsum + jnp.sum(tq, axis=0, keepdims=True)
        dxdt = dxdt_s[...]
        dxbc_ref[:, 0:SSM_INNER] = dxdt * dtx + dyv * dskx
        ddt = head_reduce(dxdt * xs)
        edec = head_reduce(tdec_s[...])
        ycs = head_reduce(tcs_s[...])
        row = lax.broadcasted_iota(jnp.int32, (n, HEAD_PAD), 0)
        extra = jnp.sum(edec, axis=0, keepdims=True) + qsum * jnp.exp(csv[n - 1:n, :])
        dcs = rsum - csum.T + ycs - edec + jnp.where(row == n - 1, extra, 0.0)
        r2 = lax.broadcasted_iota(jnp.int32, (n, n), 0)
        c2 = lax.broadcasted_iota(jnp.int32, (n, n), 1)
        dda = jnp.dot((c2 >= r2).astype(F32), dcs, precision=HIGHEST, preferred_element_type=F32)
        a_row = -jnp.exp(al_ref[0:1, :])
        ddt = ddt + dda * a_row
        dal_ref[...] += jnp.sum(dda * dtv, axis=0, keepdims=True) * a_row
        dds_ref[...] += jnp.sum(head_reduce(dyv * xs), axis=0, keepdims=True)
        ddr = ddt * _sigmoid(raw_ref[...] + bias_ref[0:1, :])
        ddr_ref[...] = ddr
        dbias_ref[...] += jnp.sum(ddr, axis=0, keepdims=True)

    par8 = _par_spec((SUBLANE, HEAD_PAD))
    return pl.pallas_call(
        body, grid=(t // n,),
        in_specs=[xs_s, xs_s, bm_s, cm_s, hv_s, hv_s, hvt_s, hvt_s, st_s, par8, par8, hv_s, par8],
        out_specs=[xbc_s, hv_s, acc_s, acc_s, acc_s],
        out_shape=[jax.ShapeDtypeStruct((t, SSM_CONV_DIM), F32), jax.ShapeDtypeStruct((t, HEAD_PAD), F32),
                   jax.ShapeDtypeStruct((1, HEAD_PAD), F32), jax.ShapeDtypeStruct((1, HEAD_PAD), F32),
                   jax.ShapeDtypeStruct((1, HEAD_PAD), F32)],
        scratch_shapes=[pltpu.VMEM((SSM_INNER, SSM_STATE), F32), pltpu.VMEM((n, SSM_INNER), F32),
                        pltpu.VMEM((n, SSM_INNER), F32), pltpu.VMEM((n, SSM_INNER), F32)],
        compiler_params=_params("arbitrary"), name=name)(dy, xbc, xbc, xbc, dt, cs, dtt, cst, st, dskip8, a_log8, dt_raw, dt_bias8)


def _gate_norm_fwd(y, proj, norm_g, *, name):
    t, c = y.shape
    tm = _pick(t, (256, 128))

    def body(y_ref, z_ref, g_ref, o_ref):
        z = z_ref[...]
        yz = y_ref[...] * z * _sigmoid(z)
        for g in range(SSM_GROUPS):
            gc = slice(g * GROUP_CH, (g + 1) * GROUP_CH)
            seg = yz[:, gc]
            r = lax.rsqrt(jnp.mean(seg * seg, axis=-1, keepdims=True) + RMS_EPS)
            o_ref[:, gc] = (seg * r * g_ref[:, gc]).astype(MXU_DTYPE)

    return pl.pallas_call(
        body, grid=(t // tm,), in_specs=[_row_spec(tm, c), _row_spec(tm, c, 1), _par_spec((1, c))],
        out_specs=_row_spec(tm, c), out_shape=jax.ShapeDtypeStruct((t, c), MXU_DTYPE),
        compiler_params=_params("parallel"), name=name)(y, proj, norm_g.reshape(1, c))


def _gate_norm_bwd(dyb, y, proj, norm_g, dproj, *, name):
    t, c = y.shape
    tm = _pick(t, (256, 128))

    def body(d_ref, y_ref, z_ref, g_ref, _, dy_ref, dz_ref, dg_ref):
        @pl.when(pl.program_id(0) == 0)
        def _():
            dg_ref[...] = jnp.zeros_like(dg_ref)

        z = z_ref[...]
        yv = y_ref[...]
        sz = _sigmoid(z)
        silu = z * sz
        yz = yv * silu
        dv = d_ref[...]
        for g in range(SSM_GROUPS):
            gc = slice(g * GROUP_CH, (g + 1) * GROUP_CH)
            seg = yz[:, gc]
            r = lax.rsqrt(jnp.mean(seg * seg, axis=-1, keepdims=True) + RMS_EPS)
            nrm = seg * r
            dn = dv[:, gc] * g_ref[:, gc]
            dg_ref[:, gc] += jnp.sum(dv[:, gc] * nrm, axis=0, keepdims=True)
            dyz = r * (dn - nrm * jnp.mean(dn * nrm, axis=-1, keepdims=True))
            dy_ref[:, gc] = dyz * silu[:, gc]
            dz_ref[:, gc] = (dyz * yv[:, gc] * (sz[:, gc] * (1.0 + z[:, gc] * (1.0 - sz[:, gc])))).astype(MXU_DTYPE)

    return pl.pallas_call(
        body, grid=(t // tm,), in_specs=[_row_spec(tm, c), _row_spec(tm, c), _row_spec(tm, c, 1), _par_spec((1, c)), _ANY],
        out_specs=[_row_spec(tm, c), _row_spec(tm, c, 1), _par_spec((1, c))],
        out_shape=[jax.ShapeDtypeStruct((t, c), F32), jax.ShapeDtypeStruct(dproj.shape, dproj.dtype),
                   jax.ShapeDtypeStruct((1, c), F32)],
        input_output_aliases={4: 1},
        compiler_params=_params("arbitrary"), name=name)(dyb, y, proj, norm_g.reshape(1, c), dproj)


GA_COLBLK = GAB_COL0 // D_MODEL


def _merge_fwd(br_a, br_b, proj, *, name):
    t, c = br_a.shape
    tm = _pick(t, (256, 128))

    def body(a_ref, b_ref, ga_ref, gb_ref, o_ref):
        o_ref[...] = (_sigmoid(ga_ref[...]) * a_ref[...] + _sigmoid(gb_ref[...]) * b_ref[...]).astype(MXU_DTYPE)

    return pl.pallas_call(
        body, grid=(t // tm,),
        in_specs=[_row_spec(tm, c), _row_spec(tm, c), _row_spec(tm, c, GA_COLBLK), _row_spec(tm, c, GA_COLBLK + 1)],
        out_specs=_row_spec(tm, c), out_shape=jax.ShapeDtypeStruct((t, c), MXU_DTYPE),
        compiler_params=_params("parallel"), name=name)(br_a, br_b, proj, proj)


def _merge_bwd(dm, br_a, br_b, proj, *, name):
    t, c = br_a.shape
    tm = _pick(t, (256, 128))

    def body(dm_ref, a_ref, b_ref, ga_ref, gb_ref, da_ref, db_ref, dg_ref):
        d = dm_ref[...]
        sa = _sigmoid(ga_ref[...])
        sb = _sigmoid(gb_ref[...])
        da_ref[...] = (d * sa).astype(MXU_DTYPE)
        db_ref[...] = (d * sb).astype(MXU_DTYPE)
        dg_ref[:, :c] = (d * a_ref[...] * sa * (1.0 - sa)).astype(MXU_DTYPE)
        dg_ref[:, c:] = (d * b_ref[...] * sb * (1.0 - sb)).astype(MXU_DTYPE)

    return pl.pallas_call(
        body, grid=(t // tm,),
        in_specs=[_row_spec(tm, c), _row_spec(tm, c), _row_spec(tm, c), _row_spec(tm, c, GA_COLBLK), _row_spec(tm, c, GA_COLBLK + 1)],
        out_specs=[_row_spec(tm, c), _row_spec(tm, c), _row_spec(tm, 2 * c, GAB_COL0 // (2 * c))],
        out_shape=[jax.ShapeDtypeStruct((t, c), MXU_DTYPE), jax.ShapeDtypeStruct((t, c), MXU_DTYPE),
                   jax.ShapeDtypeStruct((t, MAIN_COLS), MXU_DTYPE)],
        compiler_params=_params("parallel"), name=name)(dm, br_a, br_b, proj, proj)


def _layer_fwd(x, xb, memn_b, w, *, bsz, tag):
    nc = x.shape[0] // bsz // CHUNK
    sv = {"x_in": xb}
    proj = _mm(xb, w["w_main"], name=f"{tag}_proj")
    dt_raw = _mm(xb, w["w_dt"], name=f"{tag}_dtproj")
    sgo = _sg_fwd(proj, w["sg_ln_g"], w["sg_ln_b"], w["sg_w"], w["sg_bcol"], name=f"{tag}_sg_fwd")
    xbc = _conv_fwd(proj, w["conv_w"], w["conv_b"], bsz=bsz, name=f"{tag}_conv_fwd")
    dt, cs, dtt, cst = _ssd_prep(dt_raw, w["dt_bias8"], w["a_log8"], name=f"{tag}_ssd_prep")
    y, st = _ssd_fwd(xbc, dt, cs, dtt, cst, w["d_skip8"], nc=nc, name=f"{tag}_ssd_fwd")
    yb = _gate_norm_fwd(y, proj, w["ssm_norm_g"], name=f"{tag}_gate_norm_fwd")
    br_a = _mm(sgo, w["p_a"], name=f"{tag}_br_a")
    br_b = _mm(yb, w["p_b"], name=f"{tag}_br_b")
    merged = _merge_fwd(br_a, br_b, proj, name=f"{tag}_merge_fwd")
    mix = _mm(merged, w["w_mix_o"], name=f"{tag}_mix_o")
    x1, x1b, xh1, rs1 = _ln_fwd(x, mix, w["ln_g"][0], w["ln_b"][0], name=f"{tag}_ln1_fwd")
    sv.update(proj=proj, dt_raw=dt_raw, sgo=sgo, xbc=xbc, dt=dt, cs=cs, dtt=dtt, cst=cst, y=y, st=st, yb=yb,
              br_a=br_a, br_b=br_b, merged=merged, xh1=xh1, rs1=rs1, x1b=x1b)
    q = _mm(x1b, w["w_xq"], out_dtype=MXU_DTYPE, name=f"{tag}_q")
    kv = _mm(memn_b, w["w_xkv"], out_dtype=MXU_DTYPE, name=f"{tag}_kv")
    o = _attn_fwd(q, kv, bsz=bsz, name=f"{tag}_attn_fwd")
    att = _mm(o, w["w_xo"], name=f"{tag}_xo")
    x2, x2b, xh2, rs2 = _ln_fwd(x1, att, w["ln_g"][1], w["ln_b"][1], name=f"{tag}_ln2_fwd")
    sv.update(q=q, kv=kv, o=o, xh2=xh2, rs2=rs2, x2b=x2b)
    h = _mm(x2b, w["w_ffn_in"], name=f"{tag}_ffn_in")
    a = _swiglu_fwd(h, name=f"{tag}_swiglu_fwd")
    ffn = _mm(a, w["w_ffn_out"], name=f"{tag}_ffn_out")
    x3, x3b, xh3, rs3 = _ln_fwd(x2, ffn, w["ln_g"][2], w["ln_b"][2], name=f"{tag}_ln3_fwd")
    sv.update(h=h, a=a, xh3=xh3, rs3=rs3)
    return x3, x3b, sv


def _layer_bwd(dx3_addends, dx3_scales, memn_b, w, sv, *, bsz, tag):
    nc = sv["xh1"].shape[0] // bsz // CHUNK
    gr = {}
    dp3, dp3b, dg3, db3 = _ln_bwd(dx3_addends, dx3_scales, sv["xh3"], sv["rs3"], w["ln_g"][2], name=f"{tag}_ln3_bwd")
    da = _mm(dp3b, w["w_ffn_out"], tb=True, name=f"{tag}_d_a")
    gr["w_ffn_out"] = _mm(sv["a"], dp3b, ta=True, name=f"{tag}_dw_ffn_out")
    dh = _swiglu_bwd(sv["h"], da, name=f"{tag}_swiglu_bwd")
    gr["w_ffn_in"] = _mm(sv["x2b"], dh, ta=True, name=f"{tag}_dw_ffn_in")
    dx2_br = _mm(dh, w["w_ffn_in"], tb=True, name=f"{tag}_dx2")
    dp2, dp2b, dg2, db2 = _ln_bwd([dp3, dx2_br], [ALPHA, 1.0], sv["xh2"], sv["rs2"], w["ln_g"][1], name=f"{tag}_ln2_bwd")
    do = _mm(dp2b, w["w_xo"], tb=True, out_dtype=MXU_DTYPE, name=f"{tag}_d_o")
    gr["w_xo"] = _mm(sv["o"], dp2b, ta=True, name=f"{tag}_dw_xo")
    dq, dk, dv = _attn_bwd(sv["q"], sv["kv"], do, bsz=bsz, name=f"{tag}_attn_bwd")
    dkv = jnp.concatenate([dk, dv], axis=1)
    gr["w_xq"] = _mm(sv["x1b"], dq, ta=True, name=f"{tag}_dw_xq")
    gr["w_xkv"] = _mm(memn_b, dkv, ta=True, name=f"{tag}_dw_xkv")
    dmemn = _mm(dkv, w["w_xkv"], tb=True, name=f"{tag}_d_memn")
    dx1_br = _mm(dq, w["w_xq"], tb=True, name=f"{tag}_dx1")
    dp1, dp1b, dg1, db1 = _ln_bwd([dp2, dx1_br], [ALPHA, 1.0], sv["xh1"], sv["rs1"], w["ln_g"][0], name=f"{tag}_ln1_bwd")
    gr["ln_g"] = jnp.concatenate([dg1, dg2, dg3], axis=0)
    gr["ln_b"] = jnp.concatenate([db1, db2, db3], axis=0)
    dmerged = _mm(dp1b, w["w_mix_o"], tb=True, name=f"{tag}_d_merged")
    gr["w_mix_o"] = _mm(sv["merged"], dp1b, ta=True, name=f"{tag}_dw_mix_o")
    dbr_a, dbr_b, dproj = _merge_bwd(dmerged, sv["br_a"], sv["br_b"], sv["proj"], name=f"{tag}_merge_bwd")
    gr["p_a"] = _mm(sv["sgo"], dbr_a, ta=True, name=f"{tag}_dw_p_a")
    gr["p_b"] = _mm(sv["yb"], dbr_b, ta=True, name=f"{tag}_dw_p_b")
    dsgo = _mm(dbr_a, w["p_a"], tb=True, name=f"{tag}_d_sgo")
    dyb = _mm(dbr_b, w["p_b"], tb=True, name=f"{tag}_d_yb")
    dy, dproj, gr["ssm_norm_g"] = _gate_norm_bwd(dyb, sv["y"], sv["proj"], w["ssm_norm_g"], dproj, name=f"{tag}_gate_norm_bwd")
    dxbc, ddr, gr["a_log"], gr["d_skip"], gr["dt_bias"] = _ssd_bwd(
        dy, sv["xbc"], sv["dt"], sv["cs"], sv["dtt"], sv["cst"], sv["st"], w["d_skip8"], w["a_log8"], sv["dt_raw"],
        w["dt_bias8"], nc=nc, name=f"{tag}_ssd_bwd")
    dproj, gr["conv_w"], gr["conv_b"] = _conv_bwd(sv["proj"], dxbc, w["conv_w"], w["conv_b"], dproj, bsz=bsz, name=f"{tag}_conv_bwd")
    dproj, gr["sg_w"], dsg_bcol, gr["sg_ln_g"], gr["sg_ln_b"] = _sg_bwd(
        sv["proj"], dsgo, w["sg_ln_g"], w["sg_ln_b"], w["sg_w"], w["sg_bcol"], dproj, name=f"{tag}_sg_bwd")
    gr["sg_b"] = dsg_bcol[..., 0]
    gr["w_main"] = _mm(sv["x_in"], dproj, ta=True, name=f"{tag}_dw_main")
    gr["w_dt"] = _mm(sv["x_in"], ddr, ta=True, name=f"{tag}_dw_dt")
    dx_main = _mm(dproj, w["w_main"], tb=True, name=f"{tag}_dx_main")
    dx_dt = _mm(ddr, w["w_dt"], tb=True, name=f"{tag}_dx_dt")
    return [dp1, dx_main, dx_dt], [ALPHA, 1.0, 1.0], gr, dmemn


def _local_step(x, mem, tgt, mem_ln_g, mem_ln_b, layers):
    bsz, s, d = x.shape
    xf = x.reshape(bsz * s, d)
    memf = mem.reshape(-1, d)
    _, memn_b, mxh, mrs = _ln_fwd(memf, None, mem_ln_g, mem_ln_b, name="mem_ln_fwd")
    cur, curb, saved = xf, xf, []
    for li, w in enumerate(layers):
        cur, curb, sv = _layer_fwd(cur, curb, memn_b, w, bsz=bsz, tag=f"l{li}")
        saved.append(sv)
    dy, lsum = _loss_head(cur, tgt.reshape(bsz * s, d), name="loss_head")
    addends, scales = [dy], [1.0]
    grads, dmem = [None] * len(layers), []
    for li in reversed(range(len(layers))):
        addends, scales, grads[li], dm = _layer_bwd(addends, scales, memn_b, layers[li], saved[li], bsz=bsz, tag=f"l{li}")
        dmem.append(dm)
    grad_x = _add_scaled(addends, scales, name="grad_x").reshape(bsz, s, d)
    _, _, dmg, dmb = _ln_bwd(dmem, [1.0] * len(dmem), mxh, mrs, mem_ln_g, name="mem_ln_bwd")
    return lsum, grad_x, grads, dmg[0], dmb[0]


_ANY = pl.BlockSpec(memory_space=pl.ANY)
_MESH = pl.DeviceIdType.MESH


def _all_gather8(x, *, name):
    def body(x_ref, out_ref, send_sems, recv_sems):
        mx, my, mc = lax.axis_index("x"), lax.axis_index("y"), lax.axis_index("c")
        me, sibling = (mx, my, mc), (mx, my, 1 - mc)
        chips = [(1 - mx, my), (mx, 1 - my), (1 - mx, 1 - my)]

        def blk(px, py, pc):
            return out_ref.at[4 * px + 2 * py + pc]

        def copy(k, block, to, src=None):
            return pltpu.make_async_remote_copy(
                src_ref=blk(*block) if src is None else src, dst_ref=blk(*block), send_sem=send_sems.at[k],
                recv_sem=recv_sems.at[k], device_id=to, device_id_type=_MESH)

        first = [copy(0, me, sibling, src=x_ref)]
        first += [copy(1 + j, me, (*chip, mc), src=x_ref) for j, chip in enumerate(chips)]
        for cp in first:
            cp.start()
        passed = [copy(4 + j, (*chip, mc), sibling) for j, chip in enumerate(chips)]
        for j, chip in enumerate(chips):
            copy(1 + j, (*chip, mc), me).wait_recv()
            passed[j].start()
        copy(0, sibling, me).wait_recv()
        for j, chip in enumerate(chips):
            copy(4 + j, (*chip, 1 - mc), me).wait_recv()
        for cp in first + passed:
            cp.wait_send()

    return pl.pallas_call(
        body, out_shape=jax.ShapeDtypeStruct((N_DEV,) + x.shape, x.dtype), in_specs=[_ANY], out_specs=_ANY,
        scratch_shapes=[pltpu.SemaphoreType.DMA((7,)), pltpu.SemaphoreType.DMA((7,))], name=name)(x)


def _row_tile(rows, row_bytes, mult=SUBLANE):
    best = None
    for tr in range(mult, rows + 1, mult):
        if rows % tr == 0 and (best is None or tr * row_bytes <= BLOCK_BYTES):
            best = tr
    return rows if best is None else best


def _gather_shape(r, c, kind):
    return {"row": (2, N_CHIPS * r, c), "col": (2, r, N_CHIPS * c), "chip": (2, N_CHIPS, r, c)}[kind]


def _cast_place(shard, kind, dtype, chip_idx, *, name):
    _, r, c = shard.shape
    tr = _row_tile(r, c * 4, 16)
    nt = r // tr

    def body(_, s_ref, o_ref):
        o_ref[...] = s_ref[...].astype(dtype)

    if kind == "row":
        out_spec = pl.BlockSpec((None, tr, c), lambda l, i, j_ref: (l, j_ref[0] * nt + i, 0))
    elif kind == "col":
        out_spec = pl.BlockSpec((None, tr, c), lambda l, i, j_ref: (l, i, j_ref[0]))
    else:
        out_spec = pl.BlockSpec((None, None, tr, c), lambda l, i, j_ref: (l, j_ref[0], i, 0))
    grid_spec = pltpu.PrefetchScalarGridSpec(
        num_scalar_prefetch=1, grid=(2, nt), in_specs=[pl.BlockSpec((None, tr, c), lambda l, i, j_ref: (l, i, 0))],
        out_specs=out_spec)
    return pl.pallas_call(body, grid_spec=grid_spec, out_shape=jax.ShapeDtypeStruct(_gather_shape(r, c, kind), dtype),
                          compiler_params=_params("parallel", "parallel"), name=name)(chip_idx, shard)


def _gather_params(bufs, shard_shapes, kinds, *, name):
    n = len(bufs)

    def body(*refs):
        outs = refs[n:2 * n]
        send_sems, recv_sems = refs[2 * n:]
        mx, my, mc = lax.axis_index("x"), lax.axis_index("y"), lax.axis_index("c")
        me, sibling = (mx, my, mc), (mx, my, 1 - mc)
        chips = [(1 - mx, my), (mx, 1 - my), (1 - mx, 1 - my)]

        def blk(i, px, py, pc):
            r, c = shard_shapes[i]
            j = 2 * px + py
            if kinds[i] == "row":
                return outs[i].at[pc, pl.ds(pl.multiple_of(j * r, r), r)]
            if kinds[i] == "col":
                return outs[i].at[pc, :, pl.ds(pl.multiple_of(j * c, c), c)]
            return outs[i].at[pc, j]

        def copy(i, k, block, to):
            return pltpu.make_async_remote_copy(
                src_ref=blk(i, *block), dst_ref=blk(i, *block), send_sem=send_sems.at[6 * i + k],
                recv_sem=recv_sems.at[6 * i + k], device_id=to, device_id_type=_MESH)

        sent = []
        for i in range(n):
            for j, chip in enumerate(chips):
                cp = copy(i, j, me, (*chip, mc))
                cp.start()
                sent.append(cp)
        for j, chip in enumerate(chips):
            for i in range(n):
                copy(i, j, (*chip, mc), me).wait_recv()
                fwd = copy(i, 3 + j, (*chip, mc), sibling)
                fwd.start()
                sent.append(fwd)
        for i in range(n):
            for j, chip in enumerate(chips):
                copy(i, 3 + j, (*chip, 1 - mc), me).wait_recv()
        for cp in sent:
            cp.wait_send()

    return pl.pallas_call(
        body, out_shape=[jax.ShapeDtypeStruct(b.shape, b.dtype) for b in bufs], in_specs=[_ANY] * n, out_specs=[_ANY] * n,
        input_output_aliases={i: i for i in range(n)},
        scratch_shapes=[pltpu.SemaphoreType.DMA((6 * n,)), pltpu.SemaphoreType.DMA((6 * n,))], name=name)(*bufs)


def _half(r, h):
    return pl.ds(pl.multiple_of(h * (r // 2), r // 2), r // 2)


def _grads_to_sibling(gs, views, *, name):
    n = len(gs)

    def recv_shape(g, view):
        if view == "chip":
            return jax.ShapeDtypeStruct((g.shape[0], g.shape[1] // 2, g.shape[2]), g.dtype)
        return jax.ShapeDtypeStruct((g.shape[0] // 2, g.shape[1]), g.dtype)

    def body(*refs):
        ins, outs = refs[:n], refs[n:2 * n]
        send_sems, recv_sems = refs[2 * n:]
        mx, my, mc = lax.axis_index("x"), lax.axis_index("y"), lax.axis_index("c")
        copies = []
        for i in range(n):
            if views[i] == "chip":
                src = ins[i].at[:, _half(gs[i].shape[1], 1 - mc)]
            else:
                src = ins[i].at[_half(gs[i].shape[0], 1 - mc)]
            cp = pltpu.make_async_remote_copy(src_ref=src, dst_ref=outs[i], send_sem=send_sems.at[i], recv_sem=recv_sems.at[i],
                                              device_id=(mx, my, 1 - mc), device_id_type=_MESH)
            cp.start()
            copies.append(cp)
        for cp in copies:
            cp.wait()

    return pl.pallas_call(
        body, out_shape=[recv_shape(g, v) for g, v in zip(gs, views)], in_specs=[_ANY] * n, out_specs=[_ANY] * n,
        scratch_shapes=[pltpu.SemaphoreType.DMA((n,)), pltpu.SemaphoreType.DMA((n,))], name=name)(*gs)


def _grads_to_chips(pairs, views, *, name):
    n = len(pairs)

    def quad_shape(p, view):
        if view == "chip":
            return jax.ShapeDtypeStruct(p.shape, p.dtype)
        return jax.ShapeDtypeStruct((N_CHIPS, p.shape[0], p.shape[1] // N_CHIPS), p.dtype)

    def body(*refs):
        ins, outs = refs[:n], refs[n:2 * n]
        send_sems, recv_sems = refs[2 * n:]
        mx, my, mc = lax.axis_index("x"), lax.axis_index("y"), lax.axis_index("c")
        me = 2 * mx + my
        chips = [(1 - mx, my), (mx, 1 - my), (1 - mx, 1 - my)]

        def blk(i, j):
            if views[i] == "chip":
                return ins[i].at[j]
            c = pairs[i].shape[1] // N_CHIPS
            return ins[i].at[:, pl.ds(pl.multiple_of(j * c, c), c)]

        copies = []
        for i in range(n):
            for k, (px, py) in enumerate(chips):
                cp = pltpu.make_async_remote_copy(src_ref=blk(i, 2 * px + py), dst_ref=outs[i].at[me], send_sem=send_sems.at[3 * i + k],
                                                  recv_sem=recv_sems.at[3 * i + k], device_id=(px, py, mc), device_id_type=_MESH)
                cp.start()
                copies.append(cp)
        for cp in copies:
            cp.wait()

    return pl.pallas_call(
        body, out_shape=[quad_shape(p, v) for p, v in zip(pairs, views)], in_specs=[_ANY] * n, out_specs=[_ANY] * n,
        scratch_shapes=[pltpu.SemaphoreType.DMA((3 * n,)), pltpu.SemaphoreType.DMA((3 * n,))], name=name)(*pairs)


def _grads_share(tots, *, name):
    n = len(tots)

    def body(*refs):
        ins, outs = refs[:n], refs[n:2 * n]
        send_sems, recv_sems = refs[2 * n:]
        mx, my, mc = lax.axis_index("x"), lax.axis_index("y"), lax.axis_index("c")
        copies = []
        for i in range(n):
            cp = pltpu.make_async_remote_copy(src_ref=ins[i], dst_ref=outs[i], send_sem=send_sems.at[i], recv_sem=recv_sems.at[i],
                                              device_id=(mx, my, 1 - mc), device_id_type=_MESH)
            cp.start()
            copies.append(cp)
        for cp in copies:
            cp.wait()

    return pl.pallas_call(
        body, out_shape=[jax.ShapeDtypeStruct(t.shape, t.dtype) for t in tots], in_specs=[_ANY] * n, out_specs=[_ANY] * n,
        scratch_shapes=[pltpu.SemaphoreType.DMA((n,)), pltpu.SemaphoreType.DMA((n,))], name=name)(*tots)


def _pair_sum(g, recv, view, c_idx, *, name):
    def body(c_ref, a_ref, b_ref, o_ref):
        o_ref[...] = (a_ref[...] + b_ref[...]).astype(WIRE_DTYPE)

    if view == "chip":
        nch, r, c = g.shape
        tr = _row_tile(r // 2, c * 4, 16)
        gv = g.reshape(nch, 2, r // 2, c)
        grid = (nch, (r // 2) // tr)
        in_specs = [pl.BlockSpec((None, None, tr, c), lambda j, i, c_ref: (j, c_ref[0], i, 0)),
                    pl.BlockSpec((None, tr, c), lambda j, i, c_ref: (j, i, 0))]
        out_spec = pl.BlockSpec((None, tr, c), lambda j, i, c_ref: (j, i, 0))
        sem = ("parallel", "parallel")
    else:
        r, c4 = g.shape
        tr = _row_tile(r // 2, c4 * 4, 16)
        gv = g.reshape(2, r // 2, c4)
        grid = ((r // 2) // tr,)
        in_specs = [pl.BlockSpec((None, tr, c4), lambda i, c_ref: (c_ref[0], i, 0)), pl.BlockSpec((tr, c4), lambda i, c_ref: (i, 0))]
        out_spec = pl.BlockSpec((tr, c4), lambda i, c_ref: (i, 0))
        sem = ("parallel",)
    grid_spec = pltpu.PrefetchScalarGridSpec(num_scalar_prefetch=1, grid=grid, in_specs=in_specs, out_specs=out_spec)
    return pl.pallas_call(body, grid_spec=grid_spec, out_shape=jax.ShapeDtypeStruct(recv.shape, WIRE_DTYPE),
                          compiler_params=_params(*sem), name=name)(c_idx, gv, recv)


def _quad_sum(gs, recvs, quads, view, chip_idx, c_idx, *, name):
    nl = len(quads)
    nch, rh, c = quads[0].shape
    tr = _row_tile(rh, c * 4, 16)

    def body(_, __, *refs):
        o_ref = refs[-1]
        per = nch + 1
        for l in range(nl):
            grp = refs[l * per:(l + 1) * per]
            acc = grp[0][...] + grp[1][...]
            for r in grp[2:]:
                acc = acc + r[...].astype(F32)
            o_ref[l] = acc

    if view == "chip":
        own = [pl.BlockSpec((None, None, tr, c), lambda i, j, h: (j[0], h[0], i, 0)),
               pl.BlockSpec((None, tr, c), lambda i, j, h: (j[0], i, 0))]
        gviews = [g.reshape(nch, 2, rh, c) for g in gs]
    else:
        own = [pl.BlockSpec((None, tr, c), lambda i, j, h: (h[0], i, j[0])), pl.BlockSpec((tr, c), lambda i, j, h: (i, j[0]))]
        gviews = [g.reshape(2, rh, nch * c) for g in gs]
    assert nch & (nch - 1) == 0
    got = [pl.BlockSpec((None, tr, c), functools.partial(lambda i, j, h, k: ((j[0] + k) & (nch - 1), i, 0), k=k))
           for k in range(1, nch)]
    ins = []
    for l in range(nl):
        ins += [gviews[l], recvs[l]] + [quads[l]] * (nch - 1)
    grid_spec = pltpu.PrefetchScalarGridSpec(
        num_scalar_prefetch=2, grid=(rh // tr,), in_specs=(own + got) * nl,
        out_specs=pl.BlockSpec((nl, tr, c), lambda i, j, h: (0, i, 0)))
    return pl.pallas_call(body, grid_spec=grid_spec, out_shape=jax.ShapeDtypeStruct((nl, rh, c), F32),
                          compiler_params=_params("parallel"), name=name)(chip_idx, c_idx, *ins)


def _sum_devices(g8, own, dev_idx, *, name):
    k, rows, cols = g8.shape

    def body(d_ref, a_ref, x_ref, o_ref):
        acc = None
        for i in range(k):
            term = jnp.where(d_ref[0] == i, x_ref[...], a_ref[i])
            acc = term if acc is None else acc + term
        o_ref[...] = acc

    grid_spec = pltpu.PrefetchScalarGridSpec(
        num_scalar_prefetch=1, grid=(1,),
        in_specs=[pl.BlockSpec((k, rows, cols), lambda i, d_ref: (0, 0, 0)), pl.BlockSpec((rows, cols), lambda i, d_ref: (0, 0))],
        out_specs=pl.BlockSpec((rows, cols), lambda i, d_ref: (0, 0)))
    return pl.pallas_call(body, grid_spec=grid_spec, out_shape=jax.ShapeDtypeStruct((rows, cols), g8.dtype),
                          compiler_params=_params("arbitrary"), name=name)(dev_idx, g8, own)


def _adamw(w, g, m, v, *, name):
    rows, cols = w.shape
    tr = rows
    for cand in (256, 128, 64, 32, 16, 8):
        if rows % cand == 0 and cand * cols <= 512 * 1024:
            tr = cand
            break
    c1 = 1.0 - ADAM_B1 ** ADAM_STEP
    c2 = 1.0 - ADAM_B2 ** ADAM_STEP

    def body(w_ref, g_ref, m_ref, v_ref, d_ref, nm_ref, nv_ref):
        gv = g_ref[...]
        nm = ADAM_B1 * m_ref[...] + (1.0 - ADAM_B1) * gv
        nv = ADAM_B2 * v_ref[...] + (1.0 - ADAM_B2) * (gv * gv)
        d_ref[...] = -ADAM_LR * ((nm / c1) / (jnp.sqrt(nv / c2) + ADAM_EPS) + ADAM_WD * w_ref[...])
        nm_ref[...] = nm
        nv_ref[...] = nv

    spec = pl.BlockSpec((tr, cols), lambda i: (i, 0))
    shp = jax.ShapeDtypeStruct((rows, cols), F32)
    return pl.pallas_call(body, grid=(rows // tr,), in_specs=[spec] * 4, out_specs=[spec] * 3, out_shape=[shp] * 3,
                          compiler_params=_params("parallel"), name=name)(w, g, m, v)


def _adamw_halves(w, m, v, mine, other, c_idx, *, name):
    nl, r, c = w.shape
    rh = r // 2
    tr = _row_tile(rh, c * 4)
    c1 = 1.0 - ADAM_B1 ** ADAM_STEP
    c2 = 1.0 - ADAM_B2 ** ADAM_STEP

    def body(c_ref, w_ref, m_ref, v_ref, a_ref, b_ref, g_ref, d_ref, nm_ref, nv_ref):
        gv = jnp.where(pl.program_id(1) == c_ref[0], a_ref[...], b_ref[...])
        nm = ADAM_B1 * m_ref[...] + (1.0 - ADAM_B1) * gv
        nv = ADAM_B2 * v_ref[...] + (1.0 - ADAM_B2) * (gv * gv)
        g_ref[...] = gv
        d_ref[...] = -ADAM_LR * ((nm / c1) / (jnp.sqrt(nv / c2) + ADAM_EPS) + ADAM_WD * w_ref[...])
        nm_ref[...] = nm
        nv_ref[...] = nv

    full = pl.BlockSpec((None, None, tr, c), lambda l, h, i, c_ref: (l, h, i, 0))
    half = pl.BlockSpec((None, tr, c), lambda l, h, i, c_ref: (l, i, 0))
    grid_spec = pltpu.PrefetchScalarGridSpec(num_scalar_prefetch=1, grid=(nl, 2, rh // tr),
                                             in_specs=[full] * 3 + [half] * 2, out_specs=[full] * 4)
    shp = jax.ShapeDtypeStruct((nl, 2, rh, c), F32)
    view = (nl, 2, rh, c)
    outs = pl.pallas_call(body, grid_spec=grid_spec, out_shape=[shp] * 4, compiler_params=_params("parallel", "parallel", "parallel"),
                          name=name)(c_idx, w.reshape(view), m.reshape(view), v.reshape(view), mine, other)
    return [o.reshape(nl, r, c) for o in outs]


WEIGHTS = ["mem_ln_g", "mem_ln_b", "w_in", "sg_ln_g", "sg_ln_b", "sg_w", "sg_b", "conv_w", "conv_b", "dt_bias", "a_log",
           "d_skip", "ssm_norm_g", "p_a", "p_b", "w_mix_o", "w_xq", "w_xkv", "w_xo", "w_ffn_in", "w_ffn_out", "ln_g", "ln_b"]
ARG_NAMES = ["x", "mem"] + WEIGHTS + ["loss_target"] + ["m_" + n for n in WEIGHTS] + ["v_" + n for n in WEIGHTS]
BIG = {"w_in": (1, (1024, 9248)), "p_a": (0, (1024, 1024)), "p_b": (0, (2048, 1024)), "w_mix_o": (0, (1024, 1024)),
       "w_xq": (0, (1024, 1024)), "w_xkv": (1, (1024, 2048)), "w_xo": (0, (1024, 1024)), "w_ffn_in": (1, (1024, 5632)),
       "w_ffn_out": (0, (2816, 1024))}
SMALL_SHARDED = {"conv_w": (4, 3072), "ln_g": (3, 1024), "ln_b": (3, 1024)}
SMALL = [n for n in WEIGHTS if n not in BIG]
XBC_IN0, DT_COL0, DT_COL1 = 4096, 7168, 7200
GATHER_KIND = {"w_in": "chip", "p_a": "row", "p_b": "row", "w_mix_o": "row", "w_xq": "row", "w_xkv": "col", "w_xo": "row",
               "w_ffn_in": "col", "w_ffn_out": "row", "conv_w": "chip", "ln_g": "chip", "ln_b": "chip"}
GRAD_VIEW = {n: ("col" if k == "col" else "chip") for n, k in GATHER_KIND.items() if n in BIG}


def _shard_shape(name):
    axis, (r, c) = BIG[name]
    return (r // N_CHIPS, c) if axis == 0 else (r, c // N_CHIPS)


def _pad_rows(flat, cols, row_mult):
    n = flat.shape[0]
    rows = -(-n // cols)
    rows = -(-rows // row_mult) * row_mult
    return jnp.pad(flat, (0, rows * cols - n)).reshape(rows, cols)


def _gather_weights(a, chip):
    names = list(BIG) + list(SMALL_SHARDED)
    kinds = [GATHER_KIND[n] for n in names]
    cpre = chip.reshape(1)
    bufs = [_cast_place(a[n], GATHER_KIND[n], MXU_DTYPE if n in BIG else F32, cpre, name=f"place_{n}") for n in names]
    outs = _gather_params(bufs, [a[n].shape[1:] for n in names], kinds, name="gather_weights")
    full = dict(zip(names, outs))
    for n in names:
        if GATHER_KIND[n] == "chip":
            _, _, r, c = full[n].shape
            full[n] = jnp.transpose(full[n], (0, 2, 1, 3)).reshape(DEPTH, r, N_CHIPS * c)
    return full


def _layer_weights(a, full, l):
    w_in = full["w_in"][l]
    w = {n: (full[n], l) for n in BIG if n != "w_in"}
    w["w_main"] = jnp.concatenate([w_in[:, :XBC_IN0], w_in[:, DT_COL1:], w_in[:, XBC_IN0:DT_COL0]], axis=1)
    w["w_dt"] = jnp.pad(w_in[:, DT_COL0:DT_COL1], ((0, 0), (0, HEAD_PAD - SSM_HEADS)))
    for n in SMALL_SHARDED:
        w[n] = full[n][l]
    for n in ["sg_ln_g", "sg_ln_b", "sg_w", "conv_b", "ssm_norm_g"]:
        w[n] = a[n][l]
    w["sg_bcol"] = a["sg_b"][l][..., None]
    for n in ["dt_bias", "a_log", "d_skip"]:
        w[n + "8"] = _pad_heads(a[n][l])
    return w


def _reduce_big_grads(grads, c_idx, chip):
    gs, views, keys = [], [], []
    for n in BIG:
        axis, _ = BIG[n]
        r, c = _shard_shape(n)
        for l in range(DEPTH):
            if n == "w_in":
                gm, gd = grads[l]["w_main"], grads[l]["w_dt"]
                gfull = jnp.concatenate([gm[:, :XBC_IN0], gm[:, XBC_COL0:], gd[:, :SSM_HEADS], gm[:, GAB_COL0:XBC_COL0]], axis=1)
                g = jnp.transpose(gfull.reshape(r, N_CHIPS, c), (1, 0, 2))
            elif axis == 0:
                g = grads[l][n].reshape(N_CHIPS, r, c)
            else:
                g = grads[l][n]
            gs.append(g)
            views.append(GRAD_VIEW[n])
            keys.append((n, l))
    recv = _grads_to_sibling(gs, views, name="grads_to_sibling")
    cpre = c_idx.reshape(1)
    pairs = [_pair_sum(g, rv, v, cpre, name=f"grads_pair_sum_{n}_{l}") for g, rv, v, (n, l) in zip(gs, recv, views, keys)]
    quads = _grads_to_chips(pairs, views, name="grads_to_chips")
    tots = [_quad_sum(gs[DEPTH * i:DEPTH * (i + 1)], recv[DEPTH * i:DEPTH * (i + 1)], quads[DEPTH * i:DEPTH * (i + 1)],
                      GRAD_VIEW[n], chip.reshape(1), cpre, name=f"grads_chip_sum_{n}") for i, n in enumerate(BIG)]
    others = _grads_share(tots, name="grads_share")
    return {n: (t, o) for n, t, o in zip(BIG, tots, others)}


def _reduce_small_grads(small, chip, c_idx):
    names = list(small)
    flat = jnp.concatenate([small[n].reshape(-1) for n in names])
    packed = _pad_rows(flat, LANE, SUBLANE)
    g8 = _all_gather8(packed, name="gather_small_grads")
    tot = _sum_devices(g8, packed, (2 * chip + c_idx).reshape(1), name="small_grads_sum").reshape(-1)
    out, off = {}, 0
    for n in names:
        sz = small[n].size
        full = tot[off:off + sz].reshape(small[n].shape)
        off += sz
        if n in SMALL_SHARDED:
            cs = SMALL_SHARDED[n][1] // N_CHIPS
            full = lax.dynamic_slice_in_dim(full, chip * cs, cs, axis=-1)
        out[n] = full
    return out


def kernel(x, mem, mem_ln_g, mem_ln_b, w_in, sg_ln_g, sg_ln_b, sg_w, sg_b, conv_w, conv_b, dt_bias, a_log, d_skip, ssm_norm_g, p_a, p_b, w_mix_o, w_xq, w_xkv, w_xo, w_ffn_in, w_ffn_out, ln_g, ln_b, loss_target, m_mem_ln_g, m_mem_ln_b, m_w_in, m_sg_ln_g, m_sg_ln_b, m_sg_w, m_sg_b, m_conv_w, m_conv_b, m_dt_bias, m_a_log, m_d_skip, m_ssm_norm_g, m_p_a, m_p_b, m_w_mix_o, m_w_xq, m_w_xkv, m_w_xo, m_w_ffn_in, m_w_ffn_out, m_ln_g, m_ln_b, v_mem_ln_g, v_mem_ln_b, v_w_in, v_sg_ln_g, v_sg_ln_b, v_sg_w, v_sg_b, v_conv_w, v_conv_b, v_dt_bias, v_a_log, v_d_skip, v_ssm_norm_g, v_p_a, v_p_b, v_w_mix_o, v_w_xq, v_w_xkv, v_w_xo, v_w_ffn_in, v_w_ffn_out, v_ln_g, v_ln_b):
    a = dict(zip(ARG_NAMES, (x, mem, mem_ln_g, mem_ln_b, w_in, sg_ln_g, sg_ln_b, sg_w, sg_b, conv_w, conv_b, dt_bias, a_log, d_skip, ssm_norm_g, p_a, p_b, w_mix_o, w_xq, w_xkv, w_xo, w_ffn_in, w_ffn_out, ln_g, ln_b, loss_target, m_mem_ln_g, m_mem_ln_b, m_w_in, m_sg_ln_g, m_sg_ln_b, m_sg_w, m_sg_b, m_conv_w, m_conv_b, m_dt_bias, m_a_log, m_d_skip, m_ssm_norm_g, m_p_a, m_p_b, m_w_mix_o, m_w_xq, m_w_xkv, m_w_xo, m_w_ffn_in, m_w_ffn_out, m_ln_g, m_ln_b, v_mem_ln_g, v_mem_ln_b, v_w_in, v_sg_ln_g, v_sg_ln_b, v_sg_w, v_sg_b, v_conv_w, v_conv_b, v_dt_bias, v_a_log, v_d_skip, v_ssm_norm_g, v_p_a, v_p_b, v_w_mix_o, v_w_xq, v_w_xkv, v_w_xo, v_w_ffn_in, v_w_ffn_out, v_ln_g, v_ln_b)))
    c_idx = lax.axis_index("c").astype(jnp.int32)
    chip = (2 * lax.axis_index("x") + lax.axis_index("y")).astype(jnp.int32)

    full = _gather_weights(a, chip)
    layers = [_layer_weights(a, full, l) for l in range(DEPTH)]
    lsum, grad_x, grads, d_mem_g, d_mem_b = _local_step(x, mem, loss_target, mem_ln_g, mem_ln_b, layers)
    loss = lax.psum(0.5 * jnp.sum(lsum) / D_MODEL, ("x", "y", "c"))

    halves = _reduce_big_grads(grads, c_idx, chip)
    gw = {}
    small = {"mem_ln_g": d_mem_g, "mem_ln_b": d_mem_b}
    for n in SMALL:
        if n in small:
            continue
        per_layer = []
        for l in range(DEPTH):
            g = grads[l][n]
            if n in ("dt_bias", "a_log", "d_skip"):
                g = g[0, :SSM_HEADS]
            per_layer.append(g.reshape(a[n].shape[1:-1] + (-1,)))
        small[n] = jnp.stack(per_layer)
    gw.update(_reduce_small_grads(small, chip, c_idx))

    delta, new_m, new_v = {}, {}, {}
    for n in BIG:
        mine, other = halves[n]
        gw[n], delta[n], new_m[n], new_v[n] = _adamw_halves(a[n], a["m_" + n], a["v_" + n], mine, other, c_idx.reshape(1),
                                                             name=f"adamw_{n}")
    packs = [_pad_rows(jnp.concatenate([src(n).reshape(-1) for n in SMALL]), LANE, SUBLANE)
             for src in (lambda n: a[n], lambda n: gw[n], lambda n: a["m_" + n], lambda n: a["v_" + n])]
    outs = _adamw(*packs, name="adamw_small")
    off = 0
    for n in SMALL:
        sz, shp = a[n].size, a[n].shape
        delta[n], new_m[n], new_v[n] = (o.reshape(-1)[off:off + sz].reshape(shp) for o in outs)
        off += sz
    return (loss, grad_x, *[gw[n].reshape(a[n].shape) for n in WEIGHTS], *[delta[n] for n in WEIGHTS],
            *[new_m[n] for n in WEIGHTS], *[new_v[n] for n in WEIGHTS])
```

```python
import functools
import math

import jax
import jax.numpy as jnp
from jax import lax
from jax.experimental import pallas as pl
from jax.experimental.pallas import tpu as pltpu

F32 = jnp.float32
MXU_DTYPE = jnp.bfloat16
WIRE_DTYPE = jnp.bfloat16

D_MODEL = 1024
DEPTH = 2
CHUNK = 128
SG_GROUPS = 8
SSM_INNER = 2048
SSM_HEADDIM = 64
SSM_HEADS = 32
SSM_STATE = 128
SSM_GROUPS = 4
SSM_CONV = 4
SSM_CONV_DIM = 3072
X_HEADS = 4
X_HEADDIM = 256
FFN_HIDDEN = 2816
ALPHA = float((2 * DEPTH) ** 0.25)
LN_EPS = 1e-5
RMS_EPS = 1e-5
ADAM_LR = 0.001
ADAM_B1 = 0.9
ADAM_B2 = 0.999
ADAM_EPS = 1e-08
ADAM_WD = 0.01
ADAM_STEP = 10

MAIN_COLS = 9216
UVZ_COLS = 4096
GAB_COL0 = 4096
XBC_COL0 = 6144
HEAD_PAD = 128

VMEM_LIMIT = 56 * 1024 * 1024
BLOCK_BYTES = 2 * 1024 * 1024
ROW_TILES = (512, 256, 128)
LANE = 128
SUBLANE = 8

N_CHIPS = 4
N_DEV = 8


def _pick(n, cands):
    for c in cands:
        if n % c == 0:
            return c
    return n


MM_TILE_MAX = 1408
MM_OPERAND_BYTES = 8 * 1024 * 1024


def _div_tile(n, limit):
    best = None
    for t in range(LANE, min(n, limit) + 1, LANE):
        if n % t == 0:
            best = t
    return n if best is None else best


def _params(*sem):
    return pltpu.CompilerParams(dimension_semantics=tuple(sem), vmem_limit_bytes=VMEM_LIMIT)


_ANY = pl.BlockSpec(memory_space=pl.ANY)
_MESH = pl.DeviceIdType.MESH


def _nt(a, b):
    return lax.dot_general(a, b, (((1,), (1,)), ((), ())), preferred_element_type=F32)


def _tn(a, b):
    return lax.dot_general(a, b, (((0,), (0,)), ((), ())), preferred_element_type=F32)


def _nn(a, b):
    return jnp.dot(a, b, preferred_element_type=F32)


def _sigmoid(x):
    return 0.5 * jnp.tanh(0.5 * x) + 0.5


def _split3(v):
    def top(x):
        bits = lax.bitcast_convert_type(x, jnp.uint32) & jnp.uint32(0xFFFF0000)
        return lax.bitcast_convert_type(bits, F32)

    v1 = top(v)
    r1 = v - v1
    v2 = top(r1)
    v3 = r1 - v2
    return v1.astype(jnp.bfloat16), v2.astype(jnp.bfloat16), v3.astype(jnp.bfloat16)


def _dot_exact(a, b, dn, data):
    if data == 0:
        mat = b.astype(jnp.bfloat16)
        return sum(lax.dot_general(p, mat, dn, preferred_element_type=F32) for p in _split3(a))
    mat = a.astype(jnp.bfloat16)
    return sum(lax.dot_general(mat, p, dn, preferred_element_type=F32) for p in _split3(b))


_DN_NN = (((1,), (0,)), ((), ()))
_DN_TN = (((0,), (0,)), ((), ()))


def _gelu(x):
    return 0.5 * x * (1.0 + lax.erf(x * (2.0 ** -0.5)))


def _gelu_grad(x):
    return 0.5 * (1.0 + lax.erf(x * (2.0 ** -0.5))) + x * jnp.exp(-0.5 * x * x) * (1.0 / math.sqrt(2.0 * math.pi))


def _mm(a, b, *, ta=False, tb=False, out_dtype=F32, name):
    b, bl = b if isinstance(b, tuple) else (b, None)
    if ta:
        kdim, m = a.shape
    else:
        m, kdim = a.shape
    if tb:
        n, k2 = b.shape[-2:]
    else:
        k2, n = b.shape[-2:]
    assert kdim == k2, (a.shape, b.shape, ta, tb)
    tm = _div_tile(m, MM_TILE_MAX)
    tn = _div_tile(n, MM_TILE_MAX)
    tk = _div_tile(kdim, MM_OPERAND_BYTES // (tm * a.dtype.itemsize + tn * b.dtype.itemsize))
    nk = kdim // tk
    dn = (((0 if ta else 1,), (1 if tb else 0,)), ((), ()))

    def body(a_ref, b_ref, o_ref, *scratch):
        d = lax.dot_general(a_ref[...].astype(MXU_DTYPE), b_ref[...].astype(MXU_DTYPE), dn, preferred_element_type=F32)
        if nk == 1:
            o_ref[...] = d.astype(out_dtype)
            return
        acc_ref, = scratch
        k = pl.program_id(2)

        @pl.when(k == 0)
        def _():
            acc_ref[...] = d

        @pl.when(jnp.logical_and(k > 0, k < nk - 1))
        def _():
            acc_ref[...] += d

        @pl.when(k == nk - 1)
        def _():
            o_ref[...] = (acc_ref[...] + d).astype(out_dtype)

    a_spec = pl.BlockSpec((tk, tm), lambda i, j, k: (k, i)) if ta else pl.BlockSpec((tm, tk), lambda i, j, k: (i, k))
    if bl is None:
        b_spec = pl.BlockSpec((tn, tk), lambda i, j, k: (j, k)) if tb else pl.BlockSpec((tk, tn), lambda i, j, k: (k, j))
    elif tb:
        b_spec = pl.BlockSpec((None, tn, tk), lambda i, j, k: (bl, j, k))
    else:
        b_spec = pl.BlockSpec((None, tk, tn), lambda i, j, k: (bl, k, j))
    return pl.pallas_call(
        body, grid=(m // tm, n // tn, nk), in_specs=[a_spec, b_spec],
        out_specs=pl.BlockSpec((tm, tn), lambda i, j, k: (i, j)),
        out_shape=jax.ShapeDtypeStruct((m, n), out_dtype),
        scratch_shapes=[pltpu.VMEM((tm, tn), F32)] if nk > 1 else [],
        compiler_params=_params("parallel", "parallel", "arbitrary"), name=name)(a, b)


def _row_spec(tm, c, col=0):
    return pl.BlockSpec((tm, c), lambda i: (i, col))


def _par_spec(shape):
    nd = len(shape)
    return pl.BlockSpec(shape, lambda i: (0,) * nd)


def _ln_fwd(x, f, g, b, *, name):
    t, c = x.shape
    tm = _pick(t, ROW_TILES)
    has_f = f is not None

    def body(*refs):
        if has_f:
            x_ref, f_ref, g_ref, b_ref, y_ref, yb_ref, xh_ref, rs_ref = refs
            r = ALPHA * x_ref[...] + f_ref[...]
        else:
            x_ref, g_ref, b_ref, y_ref, yb_ref, xh_ref, rs_ref = refs
            r = x_ref[...]
        mu = jnp.mean(r, axis=-1, keepdims=True)
        xc = r - mu
        var = jnp.mean(xc * xc, axis=-1, keepdims=True)
        rstd = lax.rsqrt(var + LN_EPS)
        xh = xc * rstd
        y = xh * g_ref[...] + b_ref[...]
        y_ref[...] = y
        yb_ref[...] = y.astype(MXU_DTYPE)
        xh_ref[...] = xh
        rs_ref[...] = jnp.broadcast_to(rstd, rs_ref.shape)

    ins = [x] + ([f] if has_f else []) + [g.reshape(1, c), b.reshape(1, c)]
    in_specs = [_row_spec(tm, c)] * (2 if has_f else 1) + [_par_spec((1, c))] * 2
    return pl.pallas_call(
        body, grid=(t // tm,), in_specs=in_specs,
        out_specs=[_row_spec(tm, c), _row_spec(tm, c), _row_spec(tm, c), _row_spec(tm, LANE)],
        out_shape=[jax.ShapeDtypeStruct((t, c), F32), jax.ShapeDtypeStruct((t, c), MXU_DTYPE),
                   jax.ShapeDtypeStruct((t, c), F32), jax.ShapeDtypeStruct((t, LANE), F32)],
        compiler_params=_params("parallel"), name=name)(*ins)


def _ln_bwd(addends, scales, xh, rs, g, *, name):
    t, c = xh.shape
    tm = _pick(t, ROW_TILES)
    na = len(addends)

    def body(*refs):
        a_refs = refs[:na]
        xh_ref, rs_ref, g_ref, dp_ref, dpb_ref, dg_ref, db_ref = refs[na:]

        @pl.when(pl.program_id(0) == 0)
        def _():
            dg_ref[...] = jnp.zeros_like(dg_ref)
            db_ref[...] = jnp.zeros_like(db_ref)

        dy = None
        for s, r in zip(scales, a_refs):
            term = r[...] if s == 1.0 else s * r[...]
            dy = term if dy is None else dy + term
        xhv = xh_ref[...]
        dxh = dy * g_ref[...]
        m1 = jnp.mean(dxh, axis=-1, keepdims=True)
        m2 = jnp.mean(dxh * xhv, axis=-1, keepdims=True)
        dp = rs_ref[:, 0:1] * (dxh - m1 - xhv * m2)
        dp_ref[...] = dp
        dpb_ref[...] = dp.astype(MXU_DTYPE)
        dg_ref[...] += jnp.sum(dy * xhv, axis=0, keepdims=True)
        db_ref[...] += jnp.sum(dy, axis=0, keepdims=True)

    in_specs = [_row_spec(tm, c)] * (na + 1) + [_row_spec(tm, LANE), _par_spec((1, c))]
    return pl.pallas_call(
        body, grid=(t // tm,), in_specs=in_specs,
        out_specs=[_row_spec(tm, c), _row_spec(tm, c), _par_spec((1, c)), _par_spec((1, c))],
        out_shape=[jax.ShapeDtypeStruct((t, c), F32), jax.ShapeDtypeStruct((t, c), MXU_DTYPE),
                   jax.ShapeDtypeStruct((1, c), F32), jax.ShapeDtypeStruct((1, c), F32)],
        compiler_params=_params("arbitrary"), name=name)(*addends, xh, rs, g.reshape(1, c))


def _add_scaled(addends, scales, *, name):
    t, c = addends[0].shape
    tm = _pick(t, ROW_TILES)
    na = len(addends)

    def body(*refs):
        acc = None
        for s, r in zip(scales, refs[:na]):
            term = r[...] if s == 1.0 else s * r[...]
            acc = term if acc is None else acc + term
        refs[na][...] = acc

    return pl.pallas_call(
        body, grid=(t // tm,), in_specs=[_row_spec(tm, c)] * na, out_specs=_row_spec(tm, c),
        out_shape=jax.ShapeDtypeStruct((t, c), F32), compiler_params=_params("parallel"), name=name)(*addends)


def _loss_head(y, tgt, *, name):
    t, c = y.shape
    tm = _pick(t, ROW_TILES)

    def body(y_ref, t_ref, dy_ref, ls_ref):
        @pl.when(pl.program_id(0) == 0)
        def _():
            ls_ref[...] = jnp.zeros_like(ls_ref)

        e = y_ref[...] - t_ref[...]
        dy_ref[...] = e * (1.0 / c)
        ls_ref[...] += jnp.sum(e * e, axis=0, keepdims=True)

    return pl.pallas_call(
        body, grid=(t // tm,), in_specs=[_row_spec(tm, c)] * 2,
        out_specs=[_row_spec(tm, c), _par_spec((1, c))],
        out_shape=[jax.ShapeDtypeStruct((t, c), F32), jax.ShapeDtypeStruct((1, c), F32)],
        compiler_params=_params("arbitrary"), name=name)(y, tgt)


def _swiglu_fwd(h, *, name):
    t, two_f = h.shape
    fh = two_f // 2
    tm = _pick(t, (256, 128))

    def body(g_ref, u_ref, a_ref):
        g = g_ref[...]
        a_ref[...] = (g * _sigmoid(g) * u_ref[...]).astype(MXU_DTYPE)

    return pl.pallas_call(
        body, grid=(t // tm,), in_specs=[_row_spec(tm, fh, 0), _row_spec(tm, fh, 1)], out_specs=_row_spec(tm, fh),
        out_shape=jax.ShapeDtypeStruct((t, fh), MXU_DTYPE), compiler_params=_params("parallel"), name=name)(h, h)


def _swiglu_bwd(h, da, *, name):
    t, two_f = h.shape
    fh = two_f // 2
    tm = _pick(t, (256, 128))

    def body(g_ref, u_ref, da_ref, dh_ref):
        g = g_ref[...]
        s = _sigmoid(g)
        dav = da_ref[...]
        dh_ref[:, :fh] = (dav * u_ref[...] * (s * (1.0 + g * (1.0 - s)))).astype(MXU_DTYPE)
        dh_ref[:, fh:] = (dav * g * s).astype(MXU_DTYPE)

    return pl.pallas_call(
        body, grid=(t // tm,), in_specs=[_row_spec(tm, fh, 0), _row_spec(tm, fh, 1), _row_spec(tm, fh)],
        out_specs=_row_spec(tm, two_f), out_shape=jax.ShapeDtypeStruct((t, two_f), MXU_DTYPE),
        compiler_params=_params("parallel"), name=name)(h, h, da)


def _attn_probs(q, k):
    s = _nt(q, k) * (X_HEADDIM ** -0.5)
    s = s - jnp.max(s, axis=-1, keepdims=True)
    p = jnp.exp(s)
    return p / jnp.sum(p, axis=-1, keepdims=True)


def _attn_fwd(q, kv, *, bsz, name):
    t = q.shape[0]
    s = t // bsz
    ml = kv.shape[0] // bsz
    hd = X_HEADDIM

    def body(q_ref, k_ref, v_ref, o_ref):
        p = _attn_probs(q_ref[...], k_ref[...])
        o_ref[...] = _nn(p.astype(MXU_DTYPE), v_ref[...]).astype(MXU_DTYPE)

    return pl.pallas_call(
        body, grid=(bsz, X_HEADS),
        in_specs=[pl.BlockSpec((s, hd), lambda b, h: (b, h)), pl.BlockSpec((ml, hd), lambda b, h: (b, h)),
                  pl.BlockSpec((ml, hd), lambda b, h: (b, X_HEADS + h))],
        out_specs=pl.BlockSpec((s, hd), lambda b, h: (b, h)),
        out_shape=jax.ShapeDtypeStruct((t, D_MODEL), MXU_DTYPE),
        compiler_params=_params("parallel", "parallel"), name=name)(q, kv, kv)


def _attn_bwd(q, kv, do, *, bsz, name):
    t = q.shape[0]
    s = t // bsz
    ml = kv.shape[0] // bsz
    hd = X_HEADDIM

    def body(q_ref, k_ref, v_ref, do_ref, dq_ref, dk_ref, dv_ref):
        qv, kk, vv, dov = q_ref[...], k_ref[...], v_ref[...], do_ref[...]
        p = _attn_probs(qv, kk)
        dp = _nt(dov, vv)
        dv_ref[...] = _tn(p.astype(MXU_DTYPE), dov).astype(MXU_DTYPE)
        ds = (p * (dp - jnp.sum(dp * p, axis=-1, keepdims=True)) * (X_HEADDIM ** -0.5)).astype(MXU_DTYPE)
        dq_ref[...] = _nn(ds, kk).astype(MXU_DTYPE)
        dk_ref[...] = _tn(ds, qv).astype(MXU_DTYPE)

    blk_q = pl.BlockSpec((s, hd), lambda b, h: (b, h))
    blk_m = pl.BlockSpec((ml, hd), lambda b, h: (b, h))
    return pl.pallas_call(
        body, grid=(bsz, X_HEADS),
        in_specs=[blk_q, blk_m, pl.BlockSpec((ml, hd), lambda b, h: (b, X_HEADS + h)), blk_q],
        out_specs=[blk_q, blk_m, blk_m],
        out_shape=[jax.ShapeDtypeStruct((t, D_MODEL), MXU_DTYPE), jax.ShapeDtypeStruct((bsz * ml, D_MODEL), MXU_DTYPE),
                   jax.ShapeDtypeStruct((bsz * ml, D_MODEL), MXU_DTYPE)],
        compiler_params=_params("parallel", "parallel"), name=name)(q, kv, kv, do)


def _causal(n):
    row = lax.broadcasted_iota(jnp.int32, (n, n), 0)
    col = lax.broadcasted_iota(jnp.int32, (n, n), 1)
    return row >= col


def _sg_norm(v, g, b):
    gv = _gelu(v)
    mu = jnp.mean(gv, axis=-1, keepdims=True)
    xc = gv - mu
    var = jnp.mean(xc * xc, axis=-1, keepdims=True)
    rstd = lax.rsqrt(var + LN_EPS)
    xh = xc * rstd
    return xh, rstd, xh * g + b


def _sg_fwd(proj, ln_g, ln_b, w, bcol, *, name):
    t = proj.shape[0]
    c = D_MODEL
    gd = c // SG_GROUPS

    def body(u_ref, v_ref, g_ref, b_ref, w_ref, bc_ref, o_ref):
        gu = _gelu(u_ref[...])
        _, _, vn = _sg_norm(v_ref[...], g_ref[...], b_ref[...])
        mask = _causal(CHUNK)
        for g in range(SG_GROUPS):
            sl = slice(g * gd, (g + 1) * gd)
            wg = jnp.where(mask, w_ref[g], 0.0).astype(MXU_DTYPE)
            mixed = _nn(wg, vn[:, sl].astype(MXU_DTYPE)) + bc_ref[g]
            o_ref[:, sl] = (gu[:, sl] * mixed).astype(MXU_DTYPE)

    return pl.pallas_call(
        body, grid=(t // CHUNK,),
        in_specs=[_row_spec(CHUNK, c, 0), _row_spec(CHUNK, c, 1), _par_spec((1, c)), _par_spec((1, c)),
                  _par_spec((SG_GROUPS, CHUNK, CHUNK)), _par_spec((SG_GROUPS, CHUNK, 1))],
        out_specs=_row_spec(CHUNK, c), out_shape=jax.ShapeDtypeStruct((t, c), MXU_DTYPE),
        compiler_params=_params("parallel"), name=name)(proj, proj, ln_g.reshape(1, c), ln_b.reshape(1, c), w, bcol)


def _sg_bwd(proj, dsgo, ln_g, ln_b, w, bcol, dproj, *, name):
    t = proj.shape[0]
    c = D_MODEL
    gd = c // SG_GROUPS

    def body(u_ref, v_ref, d_ref, g_ref, b_ref, w_ref, bc_ref, _, duv_ref, dw_ref, dbc_ref, dg_ref, db_ref, dvn_ref):
        @pl.when(pl.program_id(0) == 0)
        def _():
            dw_ref[...] = jnp.zeros_like(dw_ref)
            dbc_ref[...] = jnp.zeros_like(dbc_ref)
            dg_ref[...] = jnp.zeros_like(dg_ref)
            db_ref[...] = jnp.zeros_like(db_ref)

        u = u_ref[...]
        v = v_ref[...]
        dso = d_ref[...]
        gu = _gelu(u)
        xh, rstd, vn = _sg_norm(v, g_ref[...], b_ref[...])
        mask = _causal(CHUNK)
        for g in range(SG_GROUPS):
            sl = slice(g * gd, (g + 1) * gd)
            wg = jnp.where(mask, w_ref[g], 0.0).astype(MXU_DTYPE)
            vng = vn[:, sl].astype(MXU_DTYPE)
            mixed = _nn(wg, vng) + bc_ref[g]
            duv_ref[:, sl] = (dso[:, sl] * mixed * _gelu_grad(u[:, sl])).astype(MXU_DTYPE)
            dmix = dso[:, sl] * gu[:, sl]
            dmb = dmix.astype(MXU_DTYPE)
            dbc_ref[g] += jnp.sum(dmix, axis=-1, keepdims=True)
            dw_ref[g] += jnp.where(mask, _nt(dmb, vng), 0.0)
            dvn_ref[:, sl] = _tn(wg, dmb)
        dvn = dvn_ref[...]
        dg_ref[...] += jnp.sum(dvn * xh, axis=0, keepdims=True)
        db_ref[...] += jnp.sum(dvn, axis=0, keepdims=True)
        dxh = dvn * g_ref[...]
        m1 = jnp.mean(dxh, axis=-1, keepdims=True)
        m2 = jnp.mean(dxh * xh, axis=-1, keepdims=True)
        dgv = rstd * (dxh - m1 - xh * m2)
        duv_ref[:, c:] = (dgv * _gelu_grad(v)).astype(MXU_DTYPE)

    return pl.pallas_call(
        body, grid=(t // CHUNK,),
        in_specs=[_row_spec(CHUNK, c, 0), _row_spec(CHUNK, c, 1), _row_spec(CHUNK, c), _par_spec((1, c)),
                  _par_spec((1, c)), _par_spec((SG_GROUPS, CHUNK, CHUNK)), _par_spec((SG_GROUPS, CHUNK, 1)), _ANY],
        out_specs=[_row_spec(CHUNK, 2 * c), _par_spec((SG_GROUPS, CHUNK, CHUNK)), _par_spec((SG_GROUPS, CHUNK, 1)),
                   _par_spec((1, c)), _par_spec((1, c))],
        out_shape=[jax.ShapeDtypeStruct(dproj.shape, dproj.dtype), jax.ShapeDtypeStruct((SG_GROUPS, CHUNK, CHUNK), F32),
                   jax.ShapeDtypeStruct((SG_GROUPS, CHUNK, 1), F32), jax.ShapeDtypeStruct((1, c), F32),
                   jax.ShapeDtypeStruct((1, c), F32)],
        scratch_shapes=[pltpu.VMEM((CHUNK, c), F32)], input_output_aliases={7: 0},
        compiler_params=_params("arbitrary"), name=name)(proj, proj, dsgo, ln_g.reshape(1, c), ln_b.reshape(1, c), w, bcol, dproj)


CONV_TC = 512


def _conv_taps(x):
    rows = lax.broadcasted_iota(jnp.int32, x.shape, 0)
    taps = [jnp.where(rows >= SSM_CONV - 1 - k, pltpu.roll(x, SSM_CONV - 1 - k, axis=0), 0.0) for k in range(SSM_CONV - 1)]
    return taps + [x]


def _conv_pre(taps, w_ref, b_ref):
    acc = b_ref[...]
    for k in range(SSM_CONV):
        acc = acc + taps[k] * w_ref[k:k + 1, :]
    return acc


def _conv_fwd(proj, w, b, *, bsz, name):
    t = proj.shape[0]
    s = t // bsz
    nj = SSM_CONV_DIM // CONV_TC
    c0 = XBC_COL0 // CONV_TC

    def body(x_ref, w_ref, b_ref, o_ref):
        pre = _conv_pre(_conv_taps(x_ref[...]), w_ref, b_ref)
        o_ref[...] = pre * _sigmoid(pre)

    return pl.pallas_call(
        body, grid=(bsz, nj),
        in_specs=[pl.BlockSpec((s, CONV_TC), lambda bb, j: (bb, c0 + j)), pl.BlockSpec((SSM_CONV, CONV_TC), lambda bb, j: (0, j)),
                  pl.BlockSpec((1, CONV_TC), lambda bb, j: (0, j))],
        out_specs=pl.BlockSpec((s, CONV_TC), lambda bb, j: (bb, j)),
        out_shape=jax.ShapeDtypeStruct((t, SSM_CONV_DIM), F32),
        compiler_params=_params("parallel", "parallel"), name=name)(proj, w, b.reshape(1, -1))


def _conv_bwd(proj, dact, w, b, dproj, *, bsz, name):
    t = proj.shape[0]
    s = t // bsz
    nj = SSM_CONV_DIM // CONV_TC
    c0 = XBC_COL0 // CONV_TC

    def body(x_ref, d_ref, w_ref, b_ref, _, dx_ref, dw_ref, db_ref):
        @pl.when(pl.program_id(1) == 0)
        def _():
            dw_ref[...] = jnp.zeros_like(dw_ref)
            db_ref[...] = jnp.zeros_like(db_ref)

        taps = _conv_taps(x_ref[...])
        pre = _conv_pre(taps, w_ref, b_ref)
        sg = _sigmoid(pre)
        dpre = d_ref[...] * (sg * (1.0 + pre * (1.0 - sg)))
        rows = lax.broadcasted_iota(jnp.int32, dpre.shape, 0)
        db_ref[...] += jnp.sum(dpre, axis=0, keepdims=True)
        dx = dpre * w_ref[SSM_CONV - 1:SSM_CONV, :]
        for k in range(SSM_CONV):
            dw_ref[k:k + 1, :] += jnp.sum(dpre * taps[k], axis=0, keepdims=True)
        for k in range(SSM_CONV - 1):
            sh = SSM_CONV - 1 - k
            dsh = jnp.where(rows < s - sh, pltpu.roll(dpre, s - sh, axis=0), 0.0)
            dx = dx + dsh * w_ref[k:k + 1, :]
        dx_ref[...] = dx.astype(MXU_DTYPE)

    return pl.pallas_call(
        body, grid=(nj, bsz),
        in_specs=[pl.BlockSpec((s, CONV_TC), lambda j, bb: (bb, c0 + j)), pl.BlockSpec((s, CONV_TC), lambda j, bb: (bb, j)),
                  pl.BlockSpec((SSM_CONV, CONV_TC), lambda j, bb: (0, j)), pl.BlockSpec((1, CONV_TC), lambda j, bb: (0, j)), _ANY],
        out_specs=[pl.BlockSpec((s, CONV_TC), lambda j, bb: (bb, c0 + j)), pl.BlockSpec((SSM_CONV, CONV_TC), lambda j, bb: (0, j)),
                   pl.BlockSpec((1, CONV_TC), lambda j, bb: (0, j))],
        out_shape=[jax.ShapeDtypeStruct(dproj.shape, dproj.dtype), jax.ShapeDtypeStruct((SSM_CONV, SSM_CONV_DIM), F32),
                   jax.ShapeDtypeStruct((1, SSM_CONV_DIM), F32)],
        input_output_aliases={4: 0},
        compiler_params=_params("parallel", "arbitrary"), name=name)(proj, dact, w, b.reshape(1, -1), dproj)


def _softplus(x):
    return jnp.maximum(x, 0.0) + jnp.log1p(jnp.exp(-jnp.abs(x)))


def _pad_heads(v):
    return jnp.broadcast_to(jnp.pad(v.astype(F32), (0, HEAD_PAD - SSM_HEADS))[None, :], (SUBLANE, HEAD_PAD))


def _ssd_prep(dt_raw, dt_bias8, a_log8, *, name):
    t = dt_raw.shape[0]
    n = CHUNK

    def body(r_ref, b_ref, al_ref, dt_ref, cs_ref, dtt_ref, cst_ref):
        dt = _softplus(r_ref[...] + b_ref[0:1, :])
        da = dt * (-jnp.exp(al_ref[0:1, :]))
        row = lax.broadcasted_iota(jnp.int32, (n, n), 0)
        col = lax.broadcasted_iota(jnp.int32, (n, n), 1)
        lower = (col <= row).astype(F32)
        upper = (row <= col).astype(F32)
        eye = (row == col).astype(F32)
        dt_ref[...] = dt
        cs_ref[...] = _dot_exact(lower, da, _DN_NN, 1)
        cst_ref[0] = _dot_exact(da, upper, _DN_TN, 0)
        dtt_ref[0] = _dot_exact(dt, eye, _DN_TN, 0)

    hp = HEAD_PAD
    return pl.pallas_call(
        body, grid=(t // n,),
        in_specs=[_row_spec(n, hp), _par_spec((SUBLANE, hp)), _par_spec((SUBLANE, hp))],
        out_specs=[_row_spec(n, hp), _row_spec(n, hp), pl.BlockSpec((1, hp, n), lambda i: (i, 0, 0)),
                   pl.BlockSpec((1, hp, n), lambda i: (i, 0, 0))],
        out_shape=[jax.ShapeDtypeStruct((t, hp), F32), jax.ShapeDtypeStruct((t, hp), F32),
                   jax.ShapeDtypeStruct((t // n, hp, n), F32), jax.ShapeDtypeStruct((t // n, hp, n), F32)],
        compiler_params=_params("parallel"), name=name)(dt_raw, dt_bias8, a_log8)


def _expand_mat():
    h = lax.broadcasted_iota(jnp.int32, (HEAD_PAD, SSM_INNER), 0)
    ch = lax.broadcasted_iota(jnp.int32, (HEAD_PAD, SSM_INNER), 1)
    return (ch // SSM_HEADDIM == h).astype(F32)


def _reduce_mat():
    ch = lax.broadcasted_iota(jnp.int32, (SSM_INNER, HEAD_PAD), 0)
    h = lax.broadcasted_iota(jnp.int32, (SSM_INNER, HEAD_PAD), 1)
    return (ch // SSM_HEADDIM == h).astype(F32)


def _expand(v, em):
    return _dot_exact(v, em, _DN_NN, 0)


def _expand_heads(v):
    return jnp.repeat(v.astype(F32), SSM_HEADDIM)[None, :]


def _decay_mat(cs_ref, cst_ref, h, mask):
    seg = cs_ref[:, h:h + 1] - cst_ref[0, h:h + 1, :]
    return jnp.where(mask, jnp.exp(jnp.minimum(seg, 0.0)), 0.0)


GROUP_CH = SSM_INNER // SSM_GROUPS
PAIRS_PER_GROUP = GROUP_CH // LANE
HEADS_PER_GROUP = SSM_HEADS // SSM_GROUPS
BM_COL0 = SSM_INNER
CM_COL0 = SSM_INNER + SSM_GROUPS * SSM_STATE


def _ssd_specs(nc, rev):
    def cidx(i):
        return (i // nc) * nc + (nc - 1 - i % nc) if rev else i

    n = CHUNK
    xs = pl.BlockSpec((n, SSM_INNER), lambda i: (cidx(i), 0))
    bm = pl.BlockSpec((n, GROUP_CH), lambda i: (cidx(i), BM_COL0 // GROUP_CH))
    cm = pl.BlockSpec((n, GROUP_CH), lambda i: (cidx(i), CM_COL0 // GROUP_CH))
    hv = pl.BlockSpec((n, HEAD_PAD), lambda i: (cidx(i), 0))
    hvt = pl.BlockSpec((1, HEAD_PAD, n), lambda i: (cidx(i), 0, 0))
    st = pl.BlockSpec((1, SSM_INNER, SSM_STATE), lambda i: (cidx(i), 0, 0))
    return xs, bm, cm, hv, hvt, st


def _ssd_fwd(xbc, dt, cs, dtt, cst, dskx, *, nc, name):
    t = xbc.shape[0]
    n = CHUNK
    xs_s, bm_s, cm_s, hv_s, hvt_s, st_s = _ssd_specs(nc, False)

    def body(xs_ref, bm_ref, cm_ref, dt_ref, cs_ref, dtt_ref, cst_ref, dsk_ref, y_ref, st_ref, prev):
        @pl.when(pl.program_id(0) % nc == 0)
        def _():
            prev[...] = jnp.zeros_like(prev)

        st_ref[0] = prev[...]
        em = _expand_mat()
        dtx = _expand(dt_ref[...], em)
        csx = _expand(cs_ref[...], em)
        dskx = dsk_ref[...]
        xs = xs_ref[...]
        xdt = xs * dtx
        ecs = jnp.exp(csx)
        dec = jnp.exp(csx[n - 1:n, :] - csx)
        mask = _causal(n)
        lane = lax.broadcasted_iota(jnp.int32, (n, LANE), 1)
        for g in range(SSM_GROUPS):
            gs = slice(g * SSM_STATE, (g + 1) * SSM_STATE)
            gc = slice(g * GROUP_CH, (g + 1) * GROUP_CH)
            cmat = cm_ref[:, gs].astype(MXU_DTYPE)
            bmat = bm_ref[:, gs].astype(MXU_DTYPE)
            cb = _nt(cmat, bmat)
            yoff = ecs[:, gc] * _nt(cmat, prev[gc, :].astype(MXU_DTYPE))
            for q in range(PAIRS_PER_GROUP):
                hp = g * PAIRS_PER_GROUP + q
                sl = slice(hp * LANE, (hp + 1) * LANE)
                xp = xdt[:, sl].astype(MXU_DTYPE)
                m0 = (cb * _decay_mat(cs_ref, cst_ref, 2 * hp, mask)).astype(MXU_DTYPE)
                m1 = (cb * _decay_mat(cs_ref, cst_ref, 2 * hp + 1, mask)).astype(MXU_DTYPE)
                yd = jnp.where(lane < SSM_HEADDIM, _nn(m0, xp), _nn(m1, xp))
                y_ref[:, sl] = yd + yoff[:, q * LANE:(q + 1) * LANE] + xs[:, sl] * dskx[:, sl]
            snew = _tn((xdt[:, gc] * dec[:, gc]).astype(MXU_DTYPE), bmat)
            for r in range(HEADS_PER_GROUP):
                h = g * HEADS_PER_GROUP + r
                rows = slice(h * SSM_HEADDIM, (h + 1) * SSM_HEADDIM)
                e = jnp.exp(cst_ref[0, h:h + 1, n - 1:n])
                prev[rows, :] = prev[rows, :] * e + snew[r * SSM_HEADDIM:(r + 1) * SSM_HEADDIM, :]

    return pl.pallas_call(
        body, grid=(t // n,),
        in_specs=[xs_s, bm_s, cm_s, hv_s, hv_s, hvt_s, hvt_s, _par_spec((1, SSM_INNER))],
        out_specs=[xs_s, st_s],
        out_shape=[jax.ShapeDtypeStruct((t, SSM_INNER), F32), jax.ShapeDtypeStruct((t // n, SSM_INNER, SSM_STATE), F32)],
        scratch_shapes=[pltpu.VMEM((SSM_INNER, SSM_STATE), F32)],
        compiler_params=_params("arbitrary"), name=name)(xbc, xbc, xbc, dt, cs, dtt, cst, dskx)


def _ssd_bwd(dy, xbc, dt, cs, dtt, cst, st, dskx, a_log8, dt_raw, dt_bias8, *, nc, name):
    t = xbc.shape[0]
    n = CHUNK
    xs_s, bm_s, cm_s, hv_s, hvt_s, st_s = _ssd_specs(nc, True)
    acc_s = _par_spec((1, HEAD_PAD))
    xbc_s = pl.BlockSpec((n, SSM_CONV_DIM), xs_s.index_map)

    def body(dy_ref, xs_ref, bm_ref, cm_ref, dt_ref, cs_ref, dtt_ref, cst_ref, st_ref, dsk_ref, al_ref, raw_ref, bias_ref,
             dxbc_ref, ddr_ref, dal_ref, dds_ref, dbias_ref, dprev, dxdt_s, tdec_s, tcs_s):
        @pl.when(pl.program_id(0) % nc == 0)
        def _():
            dprev[...] = jnp.zeros_like(dprev)

        @pl.when(pl.program_id(0) == 0)
        def _():
            dal_ref[...] = jnp.zeros_like(dal_ref)
            dds_ref[...] = jnp.zeros_like(dds_ref)
            dbias_ref[...] = jnp.zeros_like(dbias_ref)

        em = _expand_mat()
        rm = _reduce_mat()

        def head_reduce(v):
            return _dot_exact(v, rm, _DN_NN, 0)

        dtv = dt_ref[...]
        csv = cs_ref[...]
        dtx = _expand(dtv, em)
        csx = _expand(csv, em)
        dskx = dsk_ref[...]
        xs = xs_ref[...]
        dyv = dy_ref[...]
        xdt = xs * dtx
        ecs = jnp.exp(csx)
        dec = jnp.exp(csx[n - 1:n, :] - csx)
        mask = _causal(n)
        lane = lax.broadcasted_iota(jnp.int32, (n, LANE), 1)
        hlane = lax.broadcasted_iota(jnp.int32, (1, HEAD_PAD), 1)
        hsub = lax.broadcasted_iota(jnp.int32, (HEAD_PAD, 1), 0)
        rsum = jnp.zeros((n, HEAD_PAD), F32)
        csum = jnp.zeros((HEAD_PAD, n), F32)
        for g in range(SSM_GROUPS):
            gs = slice(g * SSM_STATE, (g + 1) * SSM_STATE)
            gc = slice(g * GROUP_CH, (g + 1) * GROUP_CH)
            cmat = cm_ref[:, gs].astype(MXU_DTYPE)
            bmat = bm_ref[:, gs].astype(MXU_DTYPE)
            cb = _nt(cmat, bmat)
            pg = st_ref[0, gc, :].astype(MXU_DTYPE)
            dpg = dprev[gc, :]
            dpgb = dpg.astype(MXU_DTYPE)
            z = _nt(cmat, pg)
            dyg = dyv[:, gc]
            dz = (dyg * ecs[:, gc]).astype(MXU_DTYPE)
            dc = _nn(dz, pg)
            dprev_y = _tn(dz, cmat)
            tcs_s[:, gc] = dyg * z * ecs[:, gc]
            xd = xdt[:, gc] * dec[:, gc]
            wmat = _nt(bmat, dpgb)
            db = _nn(xd.astype(MXU_DTYPE), dpgb)
            tdec_s[:, gc] = wmat * xd
            dxdt_g = wmat * dec[:, gc]
            dcb = jnp.zeros((n, n), F32)
            for q in range(PAIRS_PER_GROUP):
                hp = g * PAIRS_PER_GROUP + q
                sl = slice(hp * LANE, (hp + 1) * LANE)
                xp = xdt[:, sl].astype(MXU_DTYPE)
                dyp = dyv[:, sl]
                dypb = dyp.astype(MXU_DTYPE)
                dxp = None
                for hh in range(2):
                    h = 2 * hp + hh
                    lm = _decay_mat(cs_ref, cst_ref, h, mask)
                    mine = (lane < SSM_HEADDIM) if hh == 0 else (lane >= SSM_HEADDIM)
                    dm = _nt(jnp.where(mine, dyp, 0.0).astype(MXU_DTYPE), xp)
                    dml = dm * lm
                    dcb = dcb + dml
                    gseg = dml * cb
                    rsum = rsum + jnp.sum(gseg, axis=1, keepdims=True) * (hlane == h).astype(F32)
                    csum = csum + (hsub == h).astype(F32) * jnp.sum(gseg, axis=0, keepdims=True)
                    dxh = _tn((cb * lm).astype(MXU_DTYPE), dypb)
                    dxp = dxh if dxp is None else jnp.where(mine, dxh, dxp)
                dxdt_s[:, sl] = dxdt_g[:, q * LANE:(q + 1) * LANE] + dxp
            dcbb = dcb.astype(MXU_DTYPE)
            dxbc_ref[:, CM_COL0 + g * SSM_STATE:CM_COL0 + (g + 1) * SSM_STATE] = dc + _nn(dcbb, bmat)
            dxbc_ref[:, BM_COL0 + g * SSM_STATE:BM_COL0 + (g + 1) * SSM_STATE] = db + _tn(dcbb, cmat)
            for r in range(HEADS_PER_GROUP):
                h = g * HEADS_PER_GROUP + r
                rows = slice(h * SSM_HEADDIM, (h + 1) * SSM_HEADDIM)
                lr = slice(r * SSM_HEADDIM, (r + 1) * SSM_HEADDIM)
                e = jnp.exp(cst_ref[0, h:h + 1, n - 1:n])
                dprev[rows, :] = dpg[lr, :] * e + dprev_y[lr, :]
            tq = _dot_exact(dpg * st_ref[0, gc, :], rm[gc, :], _DN_TN, 0)
            if g == 0:
                qsum = jnp.sum(tq, axis=0, keepdims=True)
            else:
                qsum = qsum + jnp.sum(tq, axis=0, keepdims=True)
        dxdt = dxdt_s[...]
        dxbc_ref[:, 0:SSM_INNER] = dxdt * dtx + dyv * dskx
        ddt = head_reduce(dxdt * xs)
        edec = head_reduce(tdec_s[...])
        ycs = head_reduce(tcs_s[...])
        row = lax.broadcasted_iota(jnp.int32, (n, HEAD_PAD), 0)
        extra = jnp.sum(edec, axis=0, keepdims=True) + qsum * jnp.exp(csv[n - 1:n, :])
        dcs = rsum - csum.T + ycs - edec + jnp.where(row == n - 1, extra, 0.0)
        r2 = lax.broadcasted_iota(jnp.int32, (n, n), 0)
        c2 = lax.broadcasted_iota(jnp.int32, (n, n), 1)
        dda = _dot_exact((c2 >= r2).astype(F32), dcs, _DN_NN, 1)
        a_row = -jnp.exp(al_ref[0:1, :])
        ddt = ddt + dda * a_row
        dal_ref[...] += jnp.sum(dda * dtv, axis=0, keepdims=True) * a_row
        dds_ref[...] += jnp.sum(head_reduce(dyv * xs), axis=0, keepdims=True)
        ddr = ddt * _sigmoid(raw_ref[...] + bias_ref[0:1, :])
        ddr_ref[...] = ddr
        dbias_ref[...] += jnp.sum(ddr, axis=0, keepdims=True)

    par8 = _par_spec((SUBLANE, HEAD_PAD))
    return pl.pallas_call(
        body, grid=(t // n,),
        in_specs=[xs_s, xs_s, bm_s, cm_s, hv_s, hv_s, hvt_s, hvt_s, st_s, _par_spec((1, SSM_INNER)), par8, hv_s, par8],
        out_specs=[xbc_s, hv_s, acc_s, acc_s, acc_s],
        out_shape=[jax.ShapeDtypeStruct((t, SSM_CONV_DIM), F32), jax.ShapeDtypeStruct((t, HEAD_PAD), F32),
                   jax.ShapeDtypeStruct((1, HEAD_PAD), F32), jax.ShapeDtypeStruct((1, HEAD_PAD), F32),
                   jax.ShapeDtypeStruct((1, HEAD_PAD), F32)],
        scratch_shapes=[pltpu.VMEM((SSM_INNER, SSM_STATE), F32), pltpu.VMEM((n, SSM_INNER), F32),
                        pltpu.VMEM((n, SSM_INNER), F32), pltpu.VMEM((n, SSM_INNER), F32)],
        compiler_params=_params("arbitrary"), name=name)(dy, xbc, xbc, xbc, dt, cs, dtt, cst, st, dskx, a_log8, dt_raw, dt_bias8)


def _gate_norm_fwd(y, proj, norm_g, *, name):
    t, c = y.shape
    tm = _pick(t, (256, 128))

    def body(y_ref, z_ref, g_ref, o_ref):
        z = z_ref[...]
        yz = y_ref[...] * z * _sigmoid(z)
        for g in range(SSM_GROUPS):
            gc = slice(g * GROUP_CH, (g + 1) * GROUP_CH)
            seg = yz[:, gc]
            r = lax.rsqrt(jnp.mean(seg * seg, axis=-1, keepdims=True) + RMS_EPS)
            o_ref[:, gc] = (seg * r * g_ref[:, gc]).astype(MXU_DTYPE)

    return pl.pallas_call(
        body, grid=(t // tm,), in_specs=[_row_spec(tm, c), _row_spec(tm, c, 1), _par_spec((1, c))],
        out_specs=_row_spec(tm, c), out_shape=jax.ShapeDtypeStruct((t, c), MXU_DTYPE),
        compiler_params=_params("parallel"), name=name)(y, proj, norm_g.reshape(1, c))


def _gate_norm_bwd(dyb, y, proj, norm_g, dproj, *, name):
    t, c = y.shape
    tm = _pick(t, (256, 128))

    def body(d_ref, y_ref, z_ref, g_ref, _, dy_ref, dz_ref, dg_ref):
        @pl.when(pl.program_id(0) == 0)
        def _():
            dg_ref[...] = jnp.zeros_like(dg_ref)

        z = z_ref[...]
        yv = y_ref[...]
        sz = _sigmoid(z)
        silu = z * sz
        yz = yv * silu
        dv = d_ref[...]
        for g in range(SSM_GROUPS):
            gc = slice(g * GROUP_CH, (g + 1) * GROUP_CH)
            seg = yz[:, gc]
            r = lax.rsqrt(jnp.mean(seg * seg, axis=-1, keepdims=True) + RMS_EPS)
            nrm = seg * r
            dn = dv[:, gc] * g_ref[:, gc]
            dg_ref[:, gc] += jnp.sum(dv[:, gc] * nrm, axis=0, keepdims=True)
            dyz = r * (dn - nrm * jnp.mean(dn * nrm, axis=-1, keepdims=True))
            dy_ref[:, gc] = dyz * silu[:, gc]
            dz_ref[:, gc] = (dyz * yv[:, gc] * (sz[:, gc] * (1.0 + z[:, gc] * (1.0 - sz[:, gc])))).astype(MXU_DTYPE)

    return pl.pallas_call(
        body, grid=(t // tm,), in_specs=[_row_spec(tm, c), _row_spec(tm, c), _row_spec(tm, c, 1), _par_spec((1, c)), _ANY],
        out_specs=[_row_spec(tm, c), _row_spec(tm, c, 1), _par_spec((1, c))],
        out_shape=[jax.ShapeDtypeStruct((t, c), F32), jax.ShapeDtypeStruct(dproj.shape, dproj.dtype),
                   jax.ShapeDtypeStruct((1, c), F32)],
        input_output_aliases={4: 1},
        compiler_params=_params("arbitrary"), name=name)(dyb, y, proj, norm_g.reshape(1, c), dproj)


GA_COLBLK = GAB_COL0 // D_MODEL


def _merge_fwd(br_a, br_b, proj, *, name):
    t, c = br_a.shape
    tm = _pick(t, ROW_TILES)

    def body(a_ref, b_ref, ga_ref, gb_ref, o_ref):
        o_ref[...] = (_sigmoid(ga_ref[...]) * a_ref[...] + _sigmoid(gb_ref[...]) * b_ref[...]).astype(MXU_DTYPE)

    return pl.pallas_call(
        body, grid=(t // tm,),
        in_specs=[_row_spec(tm, c), _row_spec(tm, c), _row_spec(tm, c, GA_COLBLK), _row_spec(tm, c, GA_COLBLK + 1)],
        out_specs=_row_spec(tm, c), out_shape=jax.ShapeDtypeStruct((t, c), MXU_DTYPE),
        compiler_params=_params("parallel"), name=name)(br_a, br_b, proj, proj)


def _merge_bwd(dm, br_a, br_b, proj, *, name):
    t, c = br_a.shape
    tm = _pick(t, ROW_TILES)

    def body(dm_ref, a_ref, b_ref, ga_ref, gb_ref, da_ref, db_ref, dg_ref):
        d = dm_ref[...]
        sa = _sigmoid(ga_ref[...])
        sb = _sigmoid(gb_ref[...])
        da_ref[...] = (d * sa).astype(MXU_DTYPE)
        db_ref[...] = (d * sb).astype(MXU_DTYPE)
        dg_ref[:, :c] = (d * a_ref[...] * sa * (1.0 - sa)).astype(MXU_DTYPE)
        dg_ref[:, c:] = (d * b_ref[...] * sb * (1.0 - sb)).astype(MXU_DTYPE)

    return pl.pallas_call(
        body, grid=(t // tm,),
        in_specs=[_row_spec(tm, c), _row_spec(tm, c), _row_spec(tm, c), _row_spec(tm, c, GA_COLBLK), _row_spec(tm, c, GA_COLBLK + 1)],
        out_specs=[_row_spec(tm, c), _row_spec(tm, c), _row_spec(tm, 2 * c, GAB_COL0 // (2 * c))],
        out_shape=[jax.ShapeDtypeStruct((t, c), MXU_DTYPE), jax.ShapeDtypeStruct((t, c), MXU_DTYPE),
                   jax.ShapeDtypeStruct((t, MAIN_COLS), MXU_DTYPE)],
        compiler_params=_params("parallel"), name=name)(dm, br_a, br_b, proj, proj)


def _layer_fwd(x, xb, memn_b, w, *, bsz, tag):
    nc = x.shape[0] // bsz // CHUNK
    sv = {"x_in": xb}
    proj = _mm(xb, w["w_main"], name=f"{tag}_proj")
    dt_raw = _mm(xb, w["w_dt"], name=f"{tag}_dtproj")
    sgo = _sg_fwd(proj, w["sg_ln_g"], w["sg_ln_b"], w["sg_w"], w["sg_bcol"], name=f"{tag}_sg_fwd")
    xbc = _conv_fwd(proj, w["conv_w"], w["conv_b"], bsz=bsz, name=f"{tag}_conv_fwd")
    dt, cs, dtt, cst = _ssd_prep(dt_raw, w["dt_bias8"], w["a_log8"], name=f"{tag}_ssd_prep")
    y, st = _ssd_fwd(xbc, dt, cs, dtt, cst, w["d_skipx"], nc=nc, name=f"{tag}_ssd_fwd")
    yb = _gate_norm_fwd(y, proj, w["ssm_norm_g"], name=f"{tag}_gate_norm_fwd")
    br_a = _mm(sgo, w["p_a"], name=f"{tag}_br_a")
    br_b = _mm(yb, w["p_b"], name=f"{tag}_br_b")
    merged = _merge_fwd(br_a, br_b, proj, name=f"{tag}_merge_fwd")
    mix = _mm(merged, w["w_mix_o"], name=f"{tag}_mix_o")
    x1, x1b, xh1, rs1 = _ln_fwd(x, mix, w["ln_g"][0], w["ln_b"][0], name=f"{tag}_ln1_fwd")
    sv.update(proj=proj, dt_raw=dt_raw, sgo=sgo, xbc=xbc, dt=dt, cs=cs, dtt=dtt, cst=cst, y=y, st=st, yb=yb,
              br_a=br_a, br_b=br_b, merged=merged, xh1=xh1, rs1=rs1, x1b=x1b)
    q = _mm(x1b, w["w_xq"], out_dtype=MXU_DTYPE, name=f"{tag}_q")
    kv = _mm(memn_b, w["w_xkv"], out_dtype=MXU_DTYPE, name=f"{tag}_kv")
    o = _attn_fwd(q, kv, bsz=bsz, name=f"{tag}_attn_fwd")
    att = _mm(o, w["w_xo"], name=f"{tag}_xo")
    x2, x2b, xh2, rs2 = _ln_fwd(x1, att, w["ln_g"][1], w["ln_b"][1], name=f"{tag}_ln2_fwd")
    sv.update(q=q, kv=kv, o=o, xh2=xh2, rs2=rs2, x2b=x2b)
    h = _mm(x2b, w["w_ffn_in"], name=f"{tag}_ffn_in")
    a = _swiglu_fwd(h, name=f"{tag}_swiglu_fwd")
    ffn = _mm(a, w["w_ffn_out"], name=f"{tag}_ffn_out")
    x3, x3b, xh3, rs3 = _ln_fwd(x2, ffn, w["ln_g"][2], w["ln_b"][2], name=f"{tag}_ln3_fwd")
    sv.update(h=h, a=a, xh3=xh3, rs3=rs3)
    return x3, x3b, sv


def _layer_bwd(dx3_addends, dx3_scales, memn_b, w, sv, *, bsz, tag):
    nc = sv["xh1"].shape[0] // bsz // CHUNK
    gr = {}
    dp3, dp3b, dg3, db3 = _ln_bwd(dx3_addends, dx3_scales, sv["xh3"], sv["rs3"], w["ln_g"][2], name=f"{tag}_ln3_bwd")
    da = _mm(dp3b, w["w_ffn_out"], tb=True, name=f"{tag}_d_a")
    gr["w_ffn_out"] = _mm(sv["a"], dp3b, ta=True, name=f"{tag}_dw_ffn_out")
    dh = _swiglu_bwd(sv["h"], da, name=f"{tag}_swiglu_bwd")
    gr["w_ffn_in"] = _mm(sv["x2b"], dh, ta=True, name=f"{tag}_dw_ffn_in")
    dx2_br = _mm(dh, w["w_ffn_in"], tb=True, name=f"{tag}_dx2")
    dp2, dp2b, dg2, db2 = _ln_bwd([dp3, dx2_br], [ALPHA, 1.0], sv["xh2"], sv["rs2"], w["ln_g"][1], name=f"{tag}_ln2_bwd")
    do = _mm(dp2b, w["w_xo"], tb=True, out_dtype=MXU_DTYPE, name=f"{tag}_d_o")
    gr["w_xo"] = _mm(sv["o"], dp2b, ta=True, name=f"{tag}_dw_xo")
    dq, dk, dv = _attn_bwd(sv["q"], sv["kv"], do, bsz=bsz, name=f"{tag}_attn_bwd")
    dkv = jnp.concatenate([dk, dv], axis=1)
    gr["w_xq"] = _mm(sv["x1b"], dq, ta=True, name=f"{tag}_dw_xq")
    gr["w_xkv"] = _mm(memn_b, dkv, ta=True, name=f"{tag}_dw_xkv")
    dmemn = _mm(dkv, w["w_xkv"], tb=True, name=f"{tag}_d_memn")
    dx1_br = _mm(dq, w["w_xq"], tb=True, name=f"{tag}_dx1")
    dp1, dp1b, dg1, db1 = _ln_bwd([dp2, dx1_br], [ALPHA, 1.0], sv["xh1"], sv["rs1"], w["ln_g"][0], name=f"{tag}_ln1_bwd")
    gr["ln_g"] = jnp.concatenate([dg1, dg2, dg3], axis=0)
    gr["ln_b"] = jnp.concatenate([db1, db2, db3], axis=0)
    dmerged = _mm(dp1b, w["w_mix_o"], tb=True, name=f"{tag}_d_merged")
    gr["w_mix_o"] = _mm(sv["merged"], dp1b, ta=True, name=f"{tag}_dw_mix_o")
    dbr_a, dbr_b, dproj = _merge_bwd(dmerged, sv["br_a"], sv["br_b"], sv["proj"], name=f"{tag}_merge_bwd")
    gr["p_a"] = _mm(sv["sgo"], dbr_a, ta=True, name=f"{tag}_dw_p_a")
    gr["p_b"] = _mm(sv["yb"], dbr_b, ta=True, name=f"{tag}_dw_p_b")
    dsgo = _mm(dbr_a, w["p_a"], tb=True, name=f"{tag}_d_sgo")
    dyb = _mm(dbr_b, w["p_b"], tb=True, name=f"{tag}_d_yb")
    dy, dproj, gr["ssm_norm_g"] = _gate_norm_bwd(dyb, sv["y"], sv["proj"], w["ssm_norm_g"], dproj, name=f"{tag}_gate_norm_bwd")
    dxbc, ddr, gr["a_log"], gr["d_skip"], gr["dt_bias"] = _ssd_bwd(
        dy, sv["xbc"], sv["dt"], sv["cs"], sv["dtt"], sv["cst"], sv["st"], w["d_skipx"], w["a_log8"], sv["dt_raw"],
        w["dt_bias8"], nc=nc, name=f"{tag}_ssd_bwd")
    dproj, gr["conv_w"], gr["conv_b"] = _conv_bwd(sv["proj"], dxbc, w["conv_w"], w["conv_b"], dproj, bsz=bsz, name=f"{tag}_conv_bwd")
    dproj, gr["sg_w"], dsg_bcol, gr["sg_ln_g"], gr["sg_ln_b"] = _sg_bwd(
        sv["proj"], dsgo, w["sg_ln_g"], w["sg_ln_b"], w["sg_w"], w["sg_bcol"], dproj, name=f"{tag}_sg_bwd")
    gr["sg_b"] = dsg_bcol[..., 0]
    gr["w_main"] = _mm(sv["x_in"], dproj, ta=True, name=f"{tag}_dw_main")
    gr["w_dt"] = _mm(sv["x_in"], ddr, ta=True, name=f"{tag}_dw_dt")
    dx_main = _mm(dproj, w["w_main"], tb=True, name=f"{tag}_dx_main")
    dx_dt = _mm(ddr, w["w_dt"], tb=True, name=f"{tag}_dx_dt")
    return [dp1, dx_main, dx_dt], [ALPHA, 1.0, 1.0], gr, dmemn


def _local_step(x, mem, tgt, mem_ln_g, mem_ln_b, layers):
    bsz, s, d = x.shape
    xf = x.reshape(bsz * s, d)
    memf = mem.reshape(-1, d)
    _, memn_b, mxh, mrs = _ln_fwd(memf, None, mem_ln_g, mem_ln_b, name="mem_ln_fwd")
    cur, curb, saved = xf, xf, []
    for li, w in enumerate(layers):
        cur, curb, sv = _layer_fwd(cur, curb, memn_b, w, bsz=bsz, tag=f"l{li}")
        saved.append(sv)
    dy, lsum = _loss_head(cur, tgt.reshape(bsz * s, d), name="loss_head")
    addends, scales = [dy], [1.0]
    grads, dmem = [None] * len(layers), []
    for li in reversed(range(len(layers))):
        addends, scales, grads[li], dm = _layer_bwd(addends, scales, memn_b, layers[li], saved[li], bsz=bsz, tag=f"l{li}")
        dmem.append(dm)
    grad_x = _add_scaled(addends, scales, name="grad_x").reshape(bsz, s, d)
    _, _, dmg, dmb = _ln_bwd(dmem, [1.0] * len(dmem), mxh, mrs, mem_ln_g, name="mem_ln_bwd")
    return lsum, grad_x, grads, dmg[0], dmb[0]


_ANY = pl.BlockSpec(memory_space=pl.ANY)
_MESH = pl.DeviceIdType.MESH


def _all_gather8(x, *, name):
    def body(x_ref, out_ref, send_sems, recv_sems):
        mx, my, mc = lax.axis_index("x"), lax.axis_index("y"), lax.axis_index("c")
        me, sibling = (mx, my, mc), (mx, my, 1 - mc)
        chips = [(1 - mx, my), (mx, 1 - my), (1 - mx, 1 - my)]

        def blk(px, py, pc):
            return out_ref.at[4 * px + 2 * py + pc]

        def copy(k, block, to, src=None):
            return pltpu.make_async_remote_copy(
                src_ref=blk(*block) if src is None else src, dst_ref=blk(*block), send_sem=send_sems.at[k],
                recv_sem=recv_sems.at[k], device_id=to, device_id_type=_MESH)

        first = [copy(0, me, sibling, src=x_ref)]
        first += [copy(1 + j, me, (*chip, mc), src=x_ref) for j, chip in enumerate(chips)]
        for cp in first:
            cp.start()
        passed = [copy(4 + j, (*chip, mc), sibling) for j, chip in enumerate(chips)]
        for j, chip in enumerate(chips):
            copy(1 + j, (*chip, mc), me).wait_recv()
            passed[j].start()
        copy(0, sibling, me).wait_recv()
        for j, chip in enumerate(chips):
            copy(4 + j, (*chip, 1 - mc), me).wait_recv()
        for cp in first + passed:
            cp.wait_send()

    return pl.pallas_call(
        body, out_shape=jax.ShapeDtypeStruct((N_DEV,) + x.shape, x.dtype), in_specs=[_ANY], out_specs=_ANY,
        scratch_shapes=[pltpu.SemaphoreType.DMA((7,)), pltpu.SemaphoreType.DMA((7,))], name=name)(x)


def _row_tile(rows, row_bytes, mult=SUBLANE):
    best = None
    for tr in range(mult, rows + 1, mult):
        if rows % tr == 0 and (best is None or tr * row_bytes <= BLOCK_BYTES):
            best = tr
    return rows if best is None else best


def _gather_shape(r, c, kind):
    return {"row": (2, N_CHIPS * r, c), "col": (2, r, N_CHIPS * c), "chip": (2, N_CHIPS, r, c)}[kind]


def _cast_place(shard, kind, dtype, chip_idx, *, name):
    _, r, c = shard.shape
    tr = _row_tile(r, c * 4, 16)
    nt = r // tr

    def body(_, s_ref, o_ref):
        o_ref[...] = s_ref[...].astype(dtype)

    if kind == "row":
        out_spec = pl.BlockSpec((None, tr, c), lambda l, i, j_ref: (l, j_ref[0] * nt + i, 0))
    elif kind == "col":
        out_spec = pl.BlockSpec((None, tr, c), lambda l, i, j_ref: (l, i, j_ref[0]))
    else:
        out_spec = pl.BlockSpec((None, None, tr, c), lambda l, i, j_ref: (l, j_ref[0], i, 0))
    grid_spec = pltpu.PrefetchScalarGridSpec(
        num_scalar_prefetch=1, grid=(2, nt), in_specs=[pl.BlockSpec((None, tr, c), lambda l, i, j_ref: (l, i, 0))],
        out_specs=out_spec)
    return pl.pallas_call(body, grid_spec=grid_spec, out_shape=jax.ShapeDtypeStruct(_gather_shape(r, c, kind), dtype),
                          compiler_params=_params("parallel", "parallel"), name=name)(chip_idx, shard)


def _gather_params(bufs, shard_shapes, kinds, *, name):
    n = len(bufs)

    def body(*refs):
        outs = refs[n:2 * n]
        send_sems, recv_sems = refs[2 * n:]
        mx, my, mc = lax.axis_index("x"), lax.axis_index("y"), lax.axis_index("c")
        me, sibling = (mx, my, mc), (mx, my, 1 - mc)
        chips = [(1 - mx, my), (mx, 1 - my), (1 - mx, 1 - my)]

        def blk(i, px, py, pc):
            r, c = shard_shapes[i]
            j = 2 * px + py
            if kinds[i] == "row":
                return outs[i].at[pc, pl.ds(pl.multiple_of(j * r, r), r)]
            if kinds[i] == "col":
                return outs[i].at[pc, :, pl.ds(pl.multiple_of(j * c, c), c)]
            return outs[i].at[pc, j]

        def copy(i, k, block, to):
            return pltpu.make_async_remote_copy(
                src_ref=blk(i, *block), dst_ref=blk(i, *block), send_sem=send_sems.at[6 * i + k],
                recv_sem=recv_sems.at[6 * i + k], device_id=to, device_id_type=_MESH)

        sent = []
        for i in range(n):
            for j, chip in enumerate(chips):
                cp = copy(i, j, me, (*chip, mc))
                cp.start()
                sent.append(cp)
        for j, chip in enumerate(chips):
            for i in range(n):
                copy(i, j, (*chip, mc), me).wait_recv()
                fwd = copy(i, 3 + j, (*chip, mc), sibling)
                fwd.start()
                sent.append(fwd)
        for i in range(n):
            for j, chip in enumerate(chips):
                copy(i, 3 + j, (*chip, 1 - mc), me).wait_recv()
        for cp in sent:
            cp.wait_send()

    return pl.pallas_call(
        body, out_shape=[jax.ShapeDtypeStruct(b.shape, b.dtype) for b in bufs], in_specs=[_ANY] * n, out_specs=[_ANY] * n,
        input_output_aliases={i: i for i in range(n)},
        scratch_shapes=[pltpu.SemaphoreType.DMA((6 * n,)), pltpu.SemaphoreType.DMA((6 * n,))], name=name)(*bufs)


def _half(r, h):
    return pl.ds(pl.multiple_of(h * (r // 2), r // 2), r // 2)


def _grads_to_sibling(gs, views, *, name):
    n = len(gs)

    def recv_shape(g, view):
        if view == "chip":
            return jax.ShapeDtypeStruct((g.shape[0], g.shape[1] // 2, g.shape[2]), g.dtype)
        return jax.ShapeDtypeStruct((g.shape[0] // 2, g.shape[1]), g.dtype)

    def body(*refs):
        ins, outs = refs[:n], refs[n:2 * n]
        send_sems, recv_sems = refs[2 * n:]
        mx, my, mc = lax.axis_index("x"), lax.axis_index("y"), lax.axis_index("c")
        copies = []
        for i in range(n):
            if views[i] == "chip":
                src = ins[i].at[:, _half(gs[i].shape[1], 1 - mc)]
            else:
                src = ins[i].at[_half(gs[i].shape[0], 1 - mc)]
            cp = pltpu.make_async_remote_copy(src_ref=src, dst_ref=outs[i], send_sem=send_sems.at[i], recv_sem=recv_sems.at[i],
                                              device_id=(mx, my, 1 - mc), device_id_type=_MESH)
            cp.start()
            copies.append(cp)
        for cp in copies:
            cp.wait()

    return pl.pallas_call(
        body, out_shape=[recv_shape(g, v) for g, v in zip(gs, views)], in_specs=[_ANY] * n, out_specs=[_ANY] * n,
        scratch_shapes=[pltpu.SemaphoreType.DMA((n,)), pltpu.SemaphoreType.DMA((n,))], name=name)(*gs)


def _grads_to_chips(pairs, views, *, name):
    n = len(pairs)

    def quad_shape(p, view):
        if view == "chip":
            return jax.ShapeDtypeStruct(p.shape, p.dtype)
        return jax.ShapeDtypeStruct((N_CHIPS, p.shape[0], p.shape[1] // N_CHIPS), p.dtype)

    def body(*refs):
        ins, outs = refs[:n], refs[n:2 * n]
        send_sems, recv_sems = refs[2 * n:]
        mx, my, mc = lax.axis_index("x"), lax.axis_index("y"), lax.axis_index("c")
        me = 2 * mx + my
        chips = [(1 - mx, my), (mx, 1 - my), (1 - mx, 1 - my)]

        def blk(i, j):
            if views[i] == "chip":
                return ins[i].at[j]
            c = pairs[i].shape[1] // N_CHIPS
            return ins[i].at[:, pl.ds(pl.multiple_of(j * c, c), c)]

        copies = []
        for i in range(n):
            for k, (px, py) in enumerate(chips):
                cp = pltpu.make_async_remote_copy(src_ref=blk(i, 2 * px + py), dst_ref=outs[i].at[me], send_sem=send_sems.at[3 * i + k],
                                                  recv_sem=recv_sems.at[3 * i + k], device_id=(px, py, mc), device_id_type=_MESH)
                cp.start()
                copies.append(cp)
        for cp in copies:
            cp.wait()

    return pl.pallas_call(
        body, out_shape=[quad_shape(p, v) for p, v in zip(pairs, views)], in_specs=[_ANY] * n, out_specs=[_ANY] * n,
        scratch_shapes=[pltpu.SemaphoreType.DMA((3 * n,)), pltpu.SemaphoreType.DMA((3 * n,))], name=name)(*pairs)


def _grads_share(tots, *, name):
    n = len(tots)

    def body(*refs):
        ins, outs = refs[:n], refs[n:2 * n]
        send_sems, recv_sems = refs[2 * n:]
        mx, my, mc = lax.axis_index("x"), lax.axis_index("y"), lax.axis_index("c")
        copies = []
        for i in range(n):
            cp = pltpu.make_async_remote_copy(src_ref=ins[i], dst_ref=outs[i], send_sem=send_sems.at[i], recv_sem=recv_sems.at[i],
                                              device_id=(mx, my, 1 - mc), device_id_type=_MESH)
            cp.start()
            copies.append(cp)
        for cp in copies:
            cp.wait()

    return pl.pallas_call(
        body, out_shape=[jax.ShapeDtypeStruct(t.shape, t.dtype) for t in tots], in_specs=[_ANY] * n, out_specs=[_ANY] * n,
        scratch_shapes=[pltpu.SemaphoreType.DMA((n,)), pltpu.SemaphoreType.DMA((n,))], name=name)(*tots)


def _pair_sum(g, recv, view, c_idx, *, name):
    def body(c_ref, a_ref, b_ref, o_ref):
        o_ref[...] = (a_ref[...] + b_ref[...]).astype(WIRE_DTYPE)

    if view == "chip":
        nch, r, c = g.shape
        tr = _row_tile(r // 2, c * 4, 16)
        gv = g.reshape(nch, 2, r // 2, c)
        grid = (nch, (r // 2) // tr)
        in_specs = [pl.BlockSpec((None, None, tr, c), lambda j, i, c_ref: (j, c_ref[0], i, 0)),
                    pl.BlockSpec((None, tr, c), lambda j, i, c_ref: (j, i, 0))]
        out_spec = pl.BlockSpec((None, tr, c), lambda j, i, c_ref: (j, i, 0))
        sem = ("parallel", "parallel")
    else:
        r, c4 = g.shape
        tr = _row_tile(r // 2, c4 * 4, 16)
        gv = g.reshape(2, r // 2, c4)
        grid = ((r // 2) // tr,)
        in_specs = [pl.BlockSpec((None, tr, c4), lambda i, c_ref: (c_ref[0], i, 0)), pl.BlockSpec((tr, c4), lambda i, c_ref: (i, 0))]
        out_spec = pl.BlockSpec((tr, c4), lambda i, c_ref: (i, 0))
        sem = ("parallel",)
    grid_spec = pltpu.PrefetchScalarGridSpec(num_scalar_prefetch=1, grid=grid, in_specs=in_specs, out_specs=out_spec)
    return pl.pallas_call(body, grid_spec=grid_spec, out_shape=jax.ShapeDtypeStruct(recv.shape, WIRE_DTYPE),
                          compiler_params=_params(*sem), name=name)(c_idx, gv, recv)


def _quad_sum(gs, recvs, quads, view, chip_idx, c_idx, *, name):
    nl = len(quads)
    nch, rh, c = quads[0].shape
    tr = _row_tile(rh, c * 4, 16)

    def body(_, __, *refs):
        o_ref = refs[-1]
        per = nch + 1
        for l in range(nl):
            grp = refs[l * per:(l + 1) * per]
            acc = grp[0][...] + grp[1][...]
            for r in grp[2:]:
                acc = acc + r[...].astype(F32)
            o_ref[l] = acc

    if view == "chip":
        own = [pl.BlockSpec((None, None, tr, c), lambda i, j, h: (j[0], h[0], i, 0)),
               pl.BlockSpec((None, tr, c), lambda i, j, h: (j[0], i, 0))]
        gviews = [g.reshape(nch, 2, rh, c) for g in gs]
    else:
        own = [pl.BlockSpec((None, tr, c), lambda i, j, h: (h[0], i, j[0])), pl.BlockSpec((tr, c), lambda i, j, h: (i, j[0]))]
        gviews = [g.reshape(2, rh, nch * c) for g in gs]
    assert nch & (nch - 1) == 0
    got = [pl.BlockSpec((None, tr, c), functools.partial(lambda i, j, h, k: ((j[0] + k) & (nch - 1), i, 0), k=k))
           for k in range(1, nch)]
    ins = []
    for l in range(nl):
        ins += [gviews[l], recvs[l]] + [quads[l]] * (nch - 1)
    grid_spec = pltpu.PrefetchScalarGridSpec(
        num_scalar_prefetch=2, grid=(rh // tr,), in_specs=(own + got) * nl,
        out_specs=pl.BlockSpec((nl, tr, c), lambda i, j, h: (0, i, 0)))
    return pl.pallas_call(body, grid_spec=grid_spec, out_shape=jax.ShapeDtypeStruct((nl, rh, c), F32),
                          compiler_params=_params("parallel"), name=name)(chip_idx, c_idx, *ins)


def _sum_devices(g8, own, dev_idx, *, name):
    k, rows, cols = g8.shape

    def body(d_ref, a_ref, x_ref, o_ref):
        acc = None
        for i in range(k):
            term = jnp.where(d_ref[0] == i, x_ref[...], a_ref[i])
            acc = term if acc is None else acc + term
        o_ref[...] = acc

    grid_spec = pltpu.PrefetchScalarGridSpec(
        num_scalar_prefetch=1, grid=(1,),
        in_specs=[pl.BlockSpec((k, rows, cols), lambda i, d_ref: (0, 0, 0)), pl.BlockSpec((rows, cols), lambda i, d_ref: (0, 0))],
        out_specs=pl.BlockSpec((rows, cols), lambda i, d_ref: (0, 0)))
    return pl.pallas_call(body, grid_spec=grid_spec, out_shape=jax.ShapeDtypeStruct((rows, cols), g8.dtype),
                          compiler_params=_params("arbitrary"), name=name)(dev_idx, g8, own)


def _adamw(w, g, m, v, *, name):
    rows, cols = w.shape
    tr = rows
    for cand in (256, 128, 64, 32, 16, 8):
        if rows % cand == 0 and cand * cols <= 512 * 1024:
            tr = cand
            break
    c1 = 1.0 - ADAM_B1 ** ADAM_STEP
    c2 = 1.0 - ADAM_B2 ** ADAM_STEP

    def body(w_ref, g_ref, m_ref, v_ref, d_ref, nm_ref, nv_ref):
        gv = g_ref[...]
        nm = ADAM_B1 * m_ref[...] + (1.0 - ADAM_B1) * gv
        nv = ADAM_B2 * v_ref[...] + (1.0 - ADAM_B2) * (gv * gv)
        d_ref[...] = -ADAM_LR * ((nm / c1) / (jnp.sqrt(nv / c2) + ADAM_EPS) + ADAM_WD * w_ref[...])
        nm_ref[...] = nm
        nv_ref[...] = nv

    spec = pl.BlockSpec((tr, cols), lambda i: (i, 0))
    shp = jax.ShapeDtypeStruct((rows, cols), F32)
    return pl.pallas_call(body, grid=(rows // tr,), in_specs=[spec] * 4, out_specs=[spec] * 3, out_shape=[shp] * 3,
                          compiler_params=_params("parallel"), name=name)(w, g, m, v)


def _adamw_halves(w, m, v, mine, other, c_idx, *, name):
    nl, r, c = w.shape
    rh = r // 2
    tr = _row_tile(rh, c * 4)
    c1 = 1.0 - ADAM_B1 ** ADAM_STEP
    c2 = 1.0 - ADAM_B2 ** ADAM_STEP

    def body(c_ref, w_ref, m_ref, v_ref, a_ref, b_ref, g_ref, d_ref, nm_ref, nv_ref):
        gv = jnp.where(pl.program_id(1) == c_ref[0], a_ref[...], b_ref[...])
        nm = ADAM_B1 * m_ref[...] + (1.0 - ADAM_B1) * gv
        nv = ADAM_B2 * v_ref[...] + (1.0 - ADAM_B2) * (gv * gv)
        g_ref[...] = gv
        d_ref[...] = -ADAM_LR * ((nm / c1) / (jnp.sqrt(nv / c2) + ADAM_EPS) + ADAM_WD * w_ref[...])
        nm_ref[...] = nm
        nv_ref[...] = nv

    full = pl.BlockSpec((None, None, tr, c), lambda l, h, i, c_ref: (l, h, i, 0))
    half = pl.BlockSpec((None, tr, c), lambda l, h, i, c_ref: (l, i, 0))
    grid_spec = pltpu.PrefetchScalarGridSpec(num_scalar_prefetch=1, grid=(nl, 2, rh // tr),
                                             in_specs=[full] * 3 + [half] * 2, out_specs=[full] * 4)
    shp = jax.ShapeDtypeStruct((nl, 2, rh, c), F32)
    view = (nl, 2, rh, c)
    outs = pl.pallas_call(body, grid_spec=grid_spec, out_shape=[shp] * 4, compiler_params=_params("parallel", "parallel", "parallel"),
                          name=name)(c_idx, w.reshape(view), m.reshape(view), v.reshape(view), mine, other)
    return [o.reshape(nl, r, c) for o in outs]


WEIGHTS = ["mem_ln_g", "mem_ln_b", "w_in", "sg_ln_g", "sg_ln_b", "sg_w", "sg_b", "conv_w", "conv_b", "dt_bias", "a_log",
           "d_skip", "ssm_norm_g", "p_a", "p_b", "w_mix_o", "w_xq", "w_xkv", "w_xo", "w_ffn_in", "w_ffn_out", "ln_g", "ln_b"]
ARG_NAMES = ["x", "mem"] + WEIGHTS + ["loss_target"] + ["m_" + n for n in WEIGHTS] + ["v_" + n for n in WEIGHTS]
BIG = {"w_in": (1, (1024, 9248)), "p_a": (0, (1024, 1024)), "p_b": (0, (2048, 1024)), "w_mix_o": (0, (1024, 1024)),
       "w_xq": (0, (1024, 1024)), "w_xkv": (1, (1024, 2048)), "w_xo": (0, (1024, 1024)), "w_ffn_in": (1, (1024, 5632)),
       "w_ffn_out": (0, (2816, 1024))}
SMALL_SHARDED = {"conv_w": (4, 3072), "ln_g": (3, 1024), "ln_b": (3, 1024)}
SMALL = [n for n in WEIGHTS if n not in BIG]
XBC_IN0, DT_COL0, DT_COL1 = 4096, 7168, 7200
GATHER_KIND = {"w_in": "chip", "p_a": "row", "p_b": "row", "w_mix_o": "row", "w_xq": "row", "w_xkv": "col", "w_xo": "row",
               "w_ffn_in": "col", "w_ffn_out": "row", "conv_w": "chip", "ln_g": "chip", "ln_b": "chip"}
GRAD_VIEW = {n: ("col" if k == "col" else "chip") for n, k in GATHER_KIND.items() if n in BIG}


def _shard_shape(name):
    axis, (r, c) = BIG[name]
    return (r // N_CHIPS, c) if axis == 0 else (r, c // N_CHIPS)


def _pad_rows(flat, cols, row_mult):
    n = flat.shape[0]
    rows = -(-n // cols)
    rows = -(-rows // row_mult) * row_mult
    return jnp.pad(flat, (0, rows * cols - n)).reshape(rows, cols)


def _gather_weights(a, chip):
    names = list(BIG) + list(SMALL_SHARDED)
    kinds = [GATHER_KIND[n] for n in names]
    cpre = chip.reshape(1)
    bufs = [_cast_place(a[n], GATHER_KIND[n], MXU_DTYPE if n in BIG else F32, cpre, name=f"place_{n}") for n in names]
    outs = _gather_params(bufs, [a[n].shape[1:] for n in names], kinds, name="gather_weights")
    full = dict(zip(names, outs))
    for n in names:
        if GATHER_KIND[n] == "chip":
            _, _, r, c = full[n].shape
            full[n] = jnp.transpose(full[n], (0, 2, 1, 3)).reshape(DEPTH, r, N_CHIPS * c)
    return full


def _layer_weights(a, full, l):
    w_in = full["w_in"][l]
    w = {n: (full[n], l) for n in BIG if n != "w_in"}
    w["w_main"] = jnp.concatenate([w_in[:, :XBC_IN0], w_in[:, DT_COL1:], w_in[:, XBC_IN0:DT_COL0]], axis=1)
    w["w_dt"] = jnp.pad(w_in[:, DT_COL0:DT_COL1], ((0, 0), (0, HEAD_PAD - SSM_HEADS)))
    for n in SMALL_SHARDED:
        w[n] = full[n][l]
    for n in ["sg_ln_g", "sg_ln_b", "sg_w", "conv_b", "ssm_norm_g"]:
        w[n] = a[n][l]
    w["sg_bcol"] = a["sg_b"][l][..., None]
    for n in ["dt_bias", "a_log"]:
        w[n + "8"] = _pad_heads(a[n][l])
    w["d_skipx"] = _expand_heads(a["d_skip"][l])
    return w


def _reduce_big_grads(grads, c_idx, chip):
    gs, views, keys = [], [], []
    for n in BIG:
        axis, _ = BIG[n]
        r, c = _shard_shape(n)
        for l in range(DEPTH):
            if n == "w_in":
                gm, gd = grads[l]["w_main"], grads[l]["w_dt"]
                gfull = jnp.concatenate([gm[:, :XBC_IN0], gm[:, XBC_COL0:], gd[:, :SSM_HEADS], gm[:, GAB_COL0:XBC_COL0]], axis=1)
                g = jnp.transpose(gfull.reshape(r, N_CHIPS, c), (1, 0, 2))
            elif axis == 0:
                g = grads[l][n].reshape(N_CHIPS, r, c)
            else:
                g = grads[l][n]
            gs.append(g)
            views.append(GRAD_VIEW[n])
            keys.append((n, l))
    recv = _grads_to_sibling(gs, views, name="grads_to_sibling")
    cpre = c_idx.reshape(1)
    pairs = [_pair_sum(g, rv, v, cpre, name=f"grads_pair_sum_{n}_{l}") for g, rv, v, (n, l) in zip(gs, recv, views, keys)]
    quads = _grads_to_chips(pairs, views, name="grads_to_chips")
    tots = [_quad_sum(gs[DEPTH * i:DEPTH * (i + 1)], recv[DEPTH * i:DEPTH * (i + 1)], quads[DEPTH * i:DEPTH * (i + 1)],
                      GRAD_VIEW[n], chip.reshape(1), cpre, name=f"grads_chip_sum_{n}") for i, n in enumerate(BIG)]
    others = _grads_share(tots, name="grads_share")
    return {n: (t, o) for n, t, o in zip(BIG, tots, others)}


def _reduce_small_grads(small, chip, c_idx):
    names = list(small)
    flat = jnp.concatenate([small[n].reshape(-1) for n in names])
    packed = _pad_rows(flat, LANE, SUBLANE)
    g8 = _all_gather8(packed, name="gather_small_grads")
    tot = _sum_devices(g8, packed, (2 * chip + c_idx).reshape(1), name="small_grads_sum").reshape(-1)
    out, off = {}, 0
    for n in names:
        sz = small[n].size
        full = tot[off:off + sz].reshape(small[n].shape)
        off += sz
        if n in SMALL_SHARDED:
            cs = SMALL_SHARDED[n][1] // N_CHIPS
            full = lax.dynamic_slice_in_dim(full, chip * cs, cs, axis=-1)
        out[n] = full
    return out


def kernel(x, mem, mem_ln_g, mem_ln_b, w_in, sg_ln_g, sg_ln_b, sg_w, sg_b, conv_w, conv_b, dt_bias, a_log, d_skip, ssm_norm_g, p_a, p_b, w_mix_o, w_xq, w_xkv, w_xo, w_ffn_in, w_ffn_out, ln_g, ln_b, loss_target, m_mem_ln_g, m_mem_ln_b, m_w_in, m_sg_ln_g, m_sg_ln_b, m_sg_w, m_sg_b, m_conv_w, m_conv_b, m_dt_bias, m_a_log, m_d_skip, m_ssm_norm_g, m_p_a, m_p_b, m_w_mix_o, m_w_xq, m_w_xkv, m_w_xo, m_w_ffn_in, m_w_ffn_out, m_ln_g, m_ln_b, v_mem_ln_g, v_mem_ln_b, v_w_in, v_sg_ln_g, v_sg_ln_b, v_sg_w, v_sg_b, v_conv_w, v_conv_b, v_dt_bias, v_a_log, v_d_skip, v_ssm_norm_g, v_p_a, v_p_b, v_w_mix_o, v_w_xq, v_w_xkv, v_w_xo, v_w_ffn_in, v_w_ffn_out, v_ln_g, v_ln_b):
    a = dict(zip(ARG_NAMES, (x, mem, mem_ln_g, mem_ln_b, w_in, sg_ln_g, sg_ln_b, sg_w, sg_b, conv_w, conv_b, dt_bias, a_log, d_skip, ssm_norm_g, p_a, p_b, w_mix_o, w_xq, w_xkv, w_xo, w_ffn_in, w_ffn_out, ln_g, ln_b, loss_target, m_mem_ln_g, m_mem_ln_b, m_w_in, m_sg_ln_g, m_sg_ln_b, m_sg_w, m_sg_b, m_conv_w, m_conv_b, m_dt_bias, m_a_log, m_d_skip, m_ssm_norm_g, m_p_a, m_p_b, m_w_mix_o, m_w_xq, m_w_xkv, m_w_xo, m_w_ffn_in, m_w_ffn_out, m_ln_g, m_ln_b, v_mem_ln_g, v_mem_ln_b, v_w_in, v_sg_ln_g, v_sg_ln_b, v_sg_w, v_sg_b, v_conv_w, v_conv_b, v_dt_bias, v_a_log, v_d_skip, v_ssm_norm_g, v_p_a, v_p_b, v_w_mix_o, v_w_xq, v_w_xkv, v_w_xo, v_w_ffn_in, v_w_ffn_out, v_ln_g, v_ln_b)))
    c_idx = lax.axis_index("c").astype(jnp.int32)
    chip = (2 * lax.axis_index("x") + lax.axis_index("y")).astype(jnp.int32)

    full = _gather_weights(a, chip)
    layers = [_layer_weights(a, full, l) for l in range(DEPTH)]
    lsum, grad_x, grads, d_mem_g, d_mem_b = _local_step(x, mem, loss_target, mem_ln_g, mem_ln_b, layers)
    loss = lax.psum(0.5 * jnp.sum(lsum) / D_MODEL, ("x", "y", "c"))

    halves = _reduce_big_grads(grads, c_idx, chip)
    gw = {}
    small = {"mem_ln_g": d_mem_g, "mem_ln_b": d_mem_b}
    for n in SMALL:
        if n in small:
            continue
        per_layer = []
        for l in range(DEPTH):
            g = grads[l][n]
            if n in ("dt_bias", "a_log", "d_skip"):
                g = g[0, :SSM_HEADS]
            per_layer.append(g.reshape(a[n].shape[1:-1] + (-1,)))
        small[n] = jnp.stack(per_layer)
    gw.update(_reduce_small_grads(small, chip, c_idx))

    delta, new_m, new_v = {}, {}, {}
    for n in BIG:
        mine, other = halves[n]
        gw[n], delta[n], new_m[n], new_v[n] = _adamw_halves(a[n], a["m_" + n], a["v_" + n], mine, other, c_idx.reshape(1),
                                                             name=f"adamw_{n}")
    packs = [_pad_rows(jnp.concatenate([src(n).reshape(-1) for n in SMALL]), LANE, SUBLANE)
             for src in (lambda n: a[n], lambda n: gw[n], lambda n: a["m_" + n], lambda n: a["v_" + n])]
    outs = _adamw(*packs, name="adamw_small")
    off = 0
    for n in SMALL:
        sz, shp = a[n].size, a[n].shape
        delta[n], new_m[n], new_v[n] = (o.reshape(-1)[off:off + sz].reshape(shp) for o in outs)
        off += sz
    return (loss, grad_x, *[gw[n].reshape(a[n].shape) for n in WEIGHTS], *[delta[n] for n in WEIGHTS],
            *[new_m[n] for n in WEIGHTS], *[new_v[n] for n in WEIGHTS])
```

```python
import functools
import math

import jax
import jax.numpy as jnp
from jax import lax
from jax.experimental import pallas as pl
from jax.experimental.pallas import tpu as pltpu

F32 = jnp.float32
MXU_DTYPE = jnp.bfloat16
WIRE_DTYPE = jnp.bfloat16

D_MODEL = 1024
DEPTH = 2
CHUNK = 128
SG_GROUPS = 8
SSM_INNER = 2048
SSM_HEADDIM = 64
SSM_HEADS = 32
SSM_STATE = 128
SSM_GROUPS = 4
SSM_CONV = 4
SSM_CONV_DIM = 3072
X_HEADS = 4
X_HEADDIM = 256
FFN_HIDDEN = 2816
ALPHA = float((2 * DEPTH) ** 0.25)
LN_EPS = 1e-5
RMS_EPS = 1e-5
ADAM_LR = 0.001
ADAM_B1 = 0.9
ADAM_B2 = 0.999
ADAM_EPS = 1e-08
ADAM_WD = 0.01
ADAM_STEP = 10

MAIN_COLS = 9216
UVZ_COLS = 4096
GAB_COL0 = 4096
XBC_COL0 = 6144
HEAD_PAD = 128

VMEM_LIMIT = 56 * 1024 * 1024
BLOCK_BYTES = 2 * 1024 * 1024
ROW_TILES = (512, 256, 128)
LANE = 128
SUBLANE = 8

N_CHIPS = 4
N_DEV = 8


def _pick(n, cands):
    for c in cands:
        if n % c == 0:
            return c
    return n


MM_TILE_MAX = 1408
MM_OPERAND_BYTES = 8 * 1024 * 1024


def _div_tile(n, limit):
    best = None
    for t in range(LANE, min(n, limit) + 1, LANE):
        if n % t == 0:
            best = t
    return n if best is None else best


def _params(*sem):
    return pltpu.CompilerParams(dimension_semantics=tuple(sem), vmem_limit_bytes=VMEM_LIMIT)


_ANY = pl.BlockSpec(memory_space=pl.ANY)
_MESH = pl.DeviceIdType.MESH


def _nt(a, b):
    return lax.dot_general(a, b, (((1,), (1,)), ((), ())), preferred_element_type=F32)


def _tn(a, b):
    return lax.dot_general(a, b, (((0,), (0,)), ((), ())), preferred_element_type=F32)


def _nn(a, b):
    return jnp.dot(a, b, preferred_element_type=F32)


def _sigmoid(x):
    return 0.5 * jnp.tanh(0.5 * x) + 0.5


def _split3(v):
    def top(x):
        bits = lax.bitcast_convert_type(x, jnp.uint32) & jnp.uint32(0xFFFF0000)
        return lax.bitcast_convert_type(bits, F32)

    v1 = top(v)
    r1 = v - v1
    v2 = top(r1)
    v3 = r1 - v2
    return v1.astype(jnp.bfloat16), v2.astype(jnp.bfloat16), v3.astype(jnp.bfloat16)


def _dot_exact(a, b, dn, data):
    if data == 0:
        mat = b.astype(jnp.bfloat16)
        return sum(lax.dot_general(p, mat, dn, preferred_element_type=F32) for p in _split3(a))
    mat = a.astype(jnp.bfloat16)
    return sum(lax.dot_general(mat, p, dn, preferred_element_type=F32) for p in _split3(b))


_DN_NN = (((1,), (0,)), ((), ()))
_DN_TN = (((0,), (0,)), ((), ()))


def _gelu(x):
    return 0.5 * x * (1.0 + lax.erf(x * (2.0 ** -0.5)))


def _gelu_grad(x):
    return 0.5 * (1.0 + lax.erf(x * (2.0 ** -0.5))) + x * jnp.exp(-0.5 * x * x) * (1.0 / math.sqrt(2.0 * math.pi))


def _mm(a, b, *, ta=False, tb=False, out_dtype=F32, name):
    b, bl = b if isinstance(b, tuple) else (b, None)
    if ta:
        kdim, m = a.shape
    else:
        m, kdim = a.shape
    if tb:
        n, k2 = b.shape[-2:]
    else:
        k2, n = b.shape[-2:]
    assert kdim == k2, (a.shape, b.shape, ta, tb)
    tm = _div_tile(m, MM_TILE_MAX)
    tn = _div_tile(n, MM_TILE_MAX)
    tk = _div_tile(kdim, MM_OPERAND_BYTES // (tm * a.dtype.itemsize + tn * b.dtype.itemsize))
    nk = kdim // tk
    dn = (((0 if ta else 1,), (1 if tb else 0,)), ((), ()))

    def body(a_ref, b_ref, o_ref, *scratch):
        d = lax.dot_general(a_ref[...].astype(MXU_DTYPE), b_ref[...].astype(MXU_DTYPE), dn, preferred_element_type=F32)
        if nk == 1:
            o_ref[...] = d.astype(out_dtype)
            return
        acc_ref, = scratch
        k = pl.program_id(2)

        @pl.when(k == 0)
        def _():
            acc_ref[...] = d

        @pl.when(jnp.logical_and(k > 0, k < nk - 1))
        def _():
            acc_ref[...] += d

        @pl.when(k == nk - 1)
        def _():
            o_ref[...] = (acc_ref[...] + d).astype(out_dtype)

    a_spec = pl.BlockSpec((tk, tm), lambda i, j, k: (k, i)) if ta else pl.BlockSpec((tm, tk), lambda i, j, k: (i, k))
    if bl is None:
        b_spec = pl.BlockSpec((tn, tk), lambda i, j, k: (j, k)) if tb else pl.BlockSpec((tk, tn), lambda i, j, k: (k, j))
    elif tb:
        b_spec = pl.BlockSpec((None, tn, tk), lambda i, j, k: (bl, j, k))
    else:
        b_spec = pl.BlockSpec((None, tk, tn), lambda i, j, k: (bl, k, j))
    return pl.pallas_call(
        body, grid=(m // tm, n // tn, nk), in_specs=[a_spec, b_spec],
        out_specs=pl.BlockSpec((tm, tn), lambda i, j, k: (i, j)),
        out_shape=jax.ShapeDtypeStruct((m, n), out_dtype),
        scratch_shapes=[pltpu.VMEM((tm, tn), F32)] if nk > 1 else [],
        compiler_params=_params("parallel", "parallel", "arbitrary"), name=name)(a, b)


def _row_spec(tm, c, col=0):
    return pl.BlockSpec((tm, c), lambda i: (i, col))


def _par_spec(shape):
    nd = len(shape)
    return pl.BlockSpec(shape, lambda i: (0,) * nd)


def _ln_fwd(x, f, g, b, *, name):
    t, c = x.shape
    tm = _pick(t, ROW_TILES)
    has_f = f is not None

    def body(*refs):
        if has_f:
            x_ref, f_ref, g_ref, b_ref, y_ref, yb_ref, xh_ref, rs_ref = refs
            r = ALPHA * x_ref[...] + f_ref[...]
        else:
            x_ref, g_ref, b_ref, y_ref, yb_ref, xh_ref, rs_ref = refs
            r = x_ref[...]
        mu = jnp.mean(r, axis=-1, keepdims=True)
        xc = r - mu
        var = jnp.mean(xc * xc, axis=-1, keepdims=True)
        rstd = lax.rsqrt(var + LN_EPS)
        xh = xc * rstd
        y = xh * g_ref[...] + b_ref[...]
        y_ref[...] = y
        yb_ref[...] = y.astype(MXU_DTYPE)
        xh_ref[...] = xh
        rs_ref[...] = jnp.broadcast_to(rstd, rs_ref.shape)

    ins = [x] + ([f] if has_f else []) + [g.reshape(1, c), b.reshape(1, c)]
    in_specs = [_row_spec(tm, c)] * (2 if has_f else 1) + [_par_spec((1, c))] * 2
    return pl.pallas_call(
        body, grid=(t // tm,), in_specs=in_specs,
        out_specs=[_row_spec(tm, c), _row_spec(tm, c), _row_spec(tm, c), _row_spec(tm, LANE)],
        out_shape=[jax.ShapeDtypeStruct((t, c), F32), jax.ShapeDtypeStruct((t, c), MXU_DTYPE),
                   jax.ShapeDtypeStruct((t, c), F32), jax.ShapeDtypeStruct((t, LANE), F32)],
        compiler_params=_params("parallel"), name=name)(*ins)


def _ln_bwd(addends, scales, xh, rs, g, *, name):
    t, c = xh.shape
    tm = _pick(t, ROW_TILES)
    na = len(addends)

    def body(*refs):
        a_refs = refs[:na]
        xh_ref, rs_ref, g_ref, dp_ref, dpb_ref, dg_ref, db_ref = refs[na:]

        @pl.when(pl.program_id(0) == 0)
        def _():
            dg_ref[...] = jnp.zeros_like(dg_ref)
            db_ref[...] = jnp.zeros_like(db_ref)

        dy = None
        for s, r in zip(scales, a_refs):
            term = r[...] if s == 1.0 else s * r[...]
            dy = term if dy is None else dy + term
        xhv = xh_ref[...]
        dxh = dy * g_ref[...]
        m1 = jnp.mean(dxh, axis=-1, keepdims=True)
        m2 = jnp.mean(dxh * xhv, axis=-1, keepdims=True)
        dp = rs_ref[:, 0:1] * (dxh - m1 - xhv * m2)
        dp_ref[...] = dp
        dpb_ref[...] = dp.astype(MXU_DTYPE)
        dg_ref[...] += jnp.sum(dy * xhv, axis=0, keepdims=True)
        db_ref[...] += jnp.sum(dy, axis=0, keepdims=True)

    in_specs = [_row_spec(tm, c)] * (na + 1) + [_row_spec(tm, LANE), _par_spec((1, c))]
    return pl.pallas_call(
        body, grid=(t // tm,), in_specs=in_specs,
        out_specs=[_row_spec(tm, c), _row_spec(tm, c), _par_spec((1, c)), _par_spec((1, c))],
        out_shape=[jax.ShapeDtypeStruct((t, c), F32), jax.ShapeDtypeStruct((t, c), MXU_DTYPE),
                   jax.ShapeDtypeStruct((1, c), F32), jax.ShapeDtypeStruct((1, c), F32)],
        compiler_params=_params("arbitrary"), name=name)(*addends, xh, rs, g.reshape(1, c))


def _add_scaled(addends, scales, *, name):
    t, c = addends[0].shape
    tm = _pick(t, ROW_TILES)
    na = len(addends)

    def body(*refs):
        acc = None
        for s, r in zip(scales, refs[:na]):
            term = r[...] if s == 1.0 else s * r[...]
            acc = term if acc is None else acc + term
        refs[na][...] = acc

    return pl.pallas_call(
        body, grid=(t // tm,), in_specs=[_row_spec(tm, c)] * na, out_specs=_row_spec(tm, c),
        out_shape=jax.ShapeDtypeStruct((t, c), F32), compiler_params=_params("parallel"), name=name)(*addends)


def _loss_head(y, tgt, *, name):
    t, c = y.shape
    tm = _pick(t, ROW_TILES)

    def body(y_ref, t_ref, dy_ref, ls_ref):
        @pl.when(pl.program_id(0) == 0)
        def _():
            ls_ref[...] = jnp.zeros_like(ls_ref)

        e = y_ref[...] - t_ref[...]
        dy_ref[...] = e * (1.0 / c)
        ls_ref[...] += jnp.sum(e * e, axis=0, keepdims=True)

    return pl.pallas_call(
        body, grid=(t // tm,), in_specs=[_row_spec(tm, c)] * 2,
        out_specs=[_row_spec(tm, c), _par_spec((1, c))],
        out_shape=[jax.ShapeDtypeStruct((t, c), F32), jax.ShapeDtypeStruct((1, c), F32)],
        compiler_params=_params("arbitrary"), name=name)(y, tgt)


def _swiglu_fwd(h, *, name):
    t, two_f = h.shape
    fh = two_f // 2
    tm = _pick(t, (256, 128))

    def body(g_ref, u_ref, a_ref):
        g = g_ref[...]
        a_ref[...] = (g * _sigmoid(g) * u_ref[...]).astype(MXU_DTYPE)

    return pl.pallas_call(
        body, grid=(t // tm,), in_specs=[_row_spec(tm, fh, 0), _row_spec(tm, fh, 1)], out_specs=_row_spec(tm, fh),
        out_shape=jax.ShapeDtypeStruct((t, fh), MXU_DTYPE), compiler_params=_params("parallel"), name=name)(h, h)


def _swiglu_bwd(h, da, *, name):
    t, two_f = h.shape
    fh = two_f // 2
    tm = _pick(t, (256, 128))

    def body(g_ref, u_ref, da_ref, dh_ref):
        g = g_ref[...]
        s = _sigmoid(g)
        dav = da_ref[...]
        dh_ref[:, :fh] = (dav * u_ref[...] * (s * (1.0 + g * (1.0 - s)))).astype(MXU_DTYPE)
        dh_ref[:, fh:] = (dav * g * s).astype(MXU_DTYPE)

    return pl.pallas_call(
        body, grid=(t // tm,), in_specs=[_row_spec(tm, fh, 0), _row_spec(tm, fh, 1), _row_spec(tm, fh)],
        out_specs=_row_spec(tm, two_f), out_shape=jax.ShapeDtypeStruct((t, two_f), MXU_DTYPE),
        compiler_params=_params("parallel"), name=name)(h, h, da)


def _attn_probs(q, k):
    s = _nt(q, k) * (X_HEADDIM ** -0.5)
    s = s - jnp.max(s, axis=-1, keepdims=True)
    p = jnp.exp(s)
    return p / jnp.sum(p, axis=-1, keepdims=True)


def _attn_fwd(q, kv, *, bsz, name):
    t = q.shape[0]
    s = t // bsz
    ml = kv.shape[0] // bsz
    hd = X_HEADDIM

    def body(q_ref, k_ref, v_ref, o_ref):
        p = _attn_probs(q_ref[...], k_ref[...])
        o_ref[...] = _nn(p.astype(MXU_DTYPE), v_ref[...]).astype(MXU_DTYPE)

    return pl.pallas_call(
        body, grid=(bsz, X_HEADS),
        in_specs=[pl.BlockSpec((s, hd), lambda b, h: (b, h)), pl.BlockSpec((ml, hd), lambda b, h: (b, h)),
                  pl.BlockSpec((ml, hd), lambda b, h: (b, X_HEADS + h))],
        out_specs=pl.BlockSpec((s, hd), lambda b, h: (b, h)),
        out_shape=jax.ShapeDtypeStruct((t, D_MODEL), MXU_DTYPE),
        compiler_params=_params("parallel", "parallel"), name=name)(q, kv, kv)


def _attn_bwd(q, kv, do, *, bsz, name):
    t = q.shape[0]
    s = t // bsz
    ml = kv.shape[0] // bsz
    hd = X_HEADDIM

    def body(q_ref, k_ref, v_ref, do_ref, dq_ref, dk_ref, dv_ref):
        qv, kk, vv, dov = q_ref[...], k_ref[...], v_ref[...], do_ref[...]
        p = _attn_probs(qv, kk)
        dp = _nt(dov, vv)
        dv_ref[...] = _tn(p.astype(MXU_DTYPE), dov).astype(MXU_DTYPE)
        ds = (p * (dp - jnp.sum(dp * p, axis=-1, keepdims=True)) * (X_HEADDIM ** -0.5)).astype(MXU_DTYPE)
        dq_ref[...] = _nn(ds, kk).astype(MXU_DTYPE)
        dk_ref[...] = _tn(ds, qv).astype(MXU_DTYPE)

    blk_q = pl.BlockSpec((s, hd), lambda b, h: (b, h))
    blk_m = pl.BlockSpec((ml, hd), lambda b, h: (b, h))
    return pl.pallas_call(
        body, grid=(bsz, X_HEADS),
        in_specs=[blk_q, blk_m, pl.BlockSpec((ml, hd), lambda b, h: (b, X_HEADS + h)), blk_q],
        out_specs=[blk_q, blk_m, blk_m],
        out_shape=[jax.ShapeDtypeStruct((t, D_MODEL), MXU_DTYPE), jax.ShapeDtypeStruct((bsz * ml, D_MODEL), MXU_DTYPE),
                   jax.ShapeDtypeStruct((bsz * ml, D_MODEL), MXU_DTYPE)],
        compiler_params=_params("parallel", "parallel"), name=name)(q, kv, kv, do)


def _causal(n):
    row = lax.broadcasted_iota(jnp.int32, (n, n), 0)
    col = lax.broadcasted_iota(jnp.int32, (n, n), 1)
    return row >= col


def _sg_norm(v, g, b):
    gv = _gelu(v)
    mu = jnp.mean(gv, axis=-1, keepdims=True)
    xc = gv - mu
    var = jnp.mean(xc * xc, axis=-1, keepdims=True)
    rstd = lax.rsqrt(var + LN_EPS)
    xh = xc * rstd
    return xh, rstd, xh * g + b


def _sg_fwd(proj, ln_g, ln_b, w, bcol, *, name):
    t = proj.shape[0]
    c = D_MODEL
    gd = c // SG_GROUPS

    def body(u_ref, v_ref, g_ref, b_ref, w_ref, bc_ref, o_ref):
        gu = _gelu(u_ref[...])
        _, _, vn = _sg_norm(v_ref[...], g_ref[...], b_ref[...])
        mask = _causal(CHUNK)
        for g in range(SG_GROUPS):
            sl = slice(g * gd, (g + 1) * gd)
            wg = jnp.where(mask, w_ref[g], 0.0).astype(MXU_DTYPE)
            mixed = _nn(wg, vn[:, sl].astype(MXU_DTYPE)) + bc_ref[g]
            o_ref[:, sl] = (gu[:, sl] * mixed).astype(MXU_DTYPE)

    return pl.pallas_call(
        body, grid=(t // CHUNK,),
        in_specs=[_row_spec(CHUNK, c, 0), _row_spec(CHUNK, c, 1), _par_spec((1, c)), _par_spec((1, c)),
                  _par_spec((SG_GROUPS, CHUNK, CHUNK)), _par_spec((SG_GROUPS, CHUNK, 1))],
        out_specs=_row_spec(CHUNK, c), out_shape=jax.ShapeDtypeStruct((t, c), MXU_DTYPE),
        compiler_params=_params("parallel"), name=name)(proj, proj, ln_g.reshape(1, c), ln_b.reshape(1, c), w, bcol)


def _sg_bwd(proj, dsgo, ln_g, ln_b, w, bcol, dproj, *, name):
    t = proj.shape[0]
    c = D_MODEL
    gd = c // SG_GROUPS

    def body(u_ref, v_ref, d_ref, g_ref, b_ref, w_ref, bc_ref, _, duv_ref, dw_ref, dbc_ref, dg_ref, db_ref, dvn_ref):
        @pl.when(pl.program_id(0) == 0)
        def _():
            dw_ref[...] = jnp.zeros_like(dw_ref)
            dbc_ref[...] = jnp.zeros_like(dbc_ref)
            dg_ref[...] = jnp.zeros_like(dg_ref)
            db_ref[...] = jnp.zeros_like(db_ref)

        u = u_ref[...]
        v = v_ref[...]
        dso = d_ref[...]
        gu = _gelu(u)
        xh, rstd, vn = _sg_norm(v, g_ref[...], b_ref[...])
        mask = _causal(CHUNK)
        for g in range(SG_GROUPS):
            sl = slice(g * gd, (g + 1) * gd)
            wg = jnp.where(mask, w_ref[g], 0.0).astype(MXU_DTYPE)
            vng = vn[:, sl].astype(MXU_DTYPE)
            mixed = _nn(wg, vng) + bc_ref[g]
            duv_ref[:, sl] = (dso[:, sl] * mixed * _gelu_grad(u[:, sl])).astype(MXU_DTYPE)
            dmix = dso[:, sl] * gu[:, sl]
            dmb = dmix.astype(MXU_DTYPE)
            dbc_ref[g] += jnp.sum(dmix, axis=-1, keepdims=True)
            dw_ref[g] += jnp.where(mask, _nt(dmb, vng), 0.0)
            dvn_ref[:, sl] = _tn(wg, dmb)
        dvn = dvn_ref[...]
        dg_ref[...] += jnp.sum(dvn * xh, axis=0, keepdims=True)
        db_ref[...] += jnp.sum(dvn, axis=0, keepdims=True)
        dxh = dvn * g_ref[...]
        m1 = jnp.mean(dxh, axis=-1, keepdims=True)
        m2 = jnp.mean(dxh * xh, axis=-1, keepdims=True)
        dgv = rstd * (dxh - m1 - xh * m2)
        duv_ref[:, c:] = (dgv * _gelu_grad(v)).astype(MXU_DTYPE)

    return pl.pallas_call(
        body, grid=(t // CHUNK,),
        in_specs=[_row_spec(CHUNK, c, 0), _row_spec(CHUNK, c, 1), _row_spec(CHUNK, c), _par_spec((1, c)),
                  _par_spec((1, c)), _par_spec((SG_GROUPS, CHUNK, CHUNK)), _par_spec((SG_GROUPS, CHUNK, 1)), _ANY],
        out_specs=[_row_spec(CHUNK, 2 * c), _par_spec((SG_GROUPS, CHUNK, CHUNK)), _par_spec((SG_GROUPS, CHUNK, 1)),
                   _par_spec((1, c)), _par_spec((1, c))],
        out_shape=[jax.ShapeDtypeStruct(dproj.shape, dproj.dtype), jax.ShapeDtypeStruct((SG_GROUPS, CHUNK, CHUNK), F32),
                   jax.ShapeDtypeStruct((SG_GROUPS, CHUNK, 1), F32), jax.ShapeDtypeStruct((1, c), F32),
                   jax.ShapeDtypeStruct((1, c), F32)],
        scratch_shapes=[pltpu.VMEM((CHUNK, c), F32)], input_output_aliases={7: 0},
        compiler_params=_params("arbitrary"), name=name)(proj, proj, dsgo, ln_g.reshape(1, c), ln_b.reshape(1, c), w, bcol, dproj)


CONV_TC = 512


def _conv_taps(x):
    rows = lax.broadcasted_iota(jnp.int32, x.shape, 0)
    taps = [jnp.where(rows >= SSM_CONV - 1 - k, pltpu.roll(x, SSM_CONV - 1 - k, axis=0), 0.0) for k in range(SSM_CONV - 1)]
    return taps + [x]


def _conv_pre(taps, w_ref, b_ref):
    acc = b_ref[...]
    for k in range(SSM_CONV):
        acc = acc + taps[k] * w_ref[k:k + 1, :]
    return acc


def _conv_fwd(proj, w, b, *, bsz, name):
    t = proj.shape[0]
    s = t // bsz
    nj = SSM_CONV_DIM // CONV_TC
    c0 = XBC_COL0 // CONV_TC

    def body(x_ref, w_ref, b_ref, o_ref):
        pre = _conv_pre(_conv_taps(x_ref[...]), w_ref, b_ref)
        o_ref[...] = pre * _sigmoid(pre)

    return pl.pallas_call(
        body, grid=(bsz, nj),
        in_specs=[pl.BlockSpec((s, CONV_TC), lambda bb, j: (bb, c0 + j)), pl.BlockSpec((SSM_CONV, CONV_TC), lambda bb, j: (0, j)),
                  pl.BlockSpec((1, CONV_TC), lambda bb, j: (0, j))],
        out_specs=pl.BlockSpec((s, CONV_TC), lambda bb, j: (bb, j)),
        out_shape=jax.ShapeDtypeStruct((t, SSM_CONV_DIM), F32),
        compiler_params=_params("parallel", "parallel"), name=name)(proj, w, b.reshape(1, -1))


def _conv_bwd(proj, dact, w, b, dproj, *, bsz, name):
    t = proj.shape[0]
    s = t // bsz
    nj = SSM_CONV_DIM // CONV_TC
    c0 = XBC_COL0 // CONV_TC

    def body(x_ref, d_ref, w_ref, b_ref, _, dx_ref, dw_ref, db_ref):
        @pl.when(pl.program_id(1) == 0)
        def _():
            dw_ref[...] = jnp.zeros_like(dw_ref)
            db_ref[...] = jnp.zeros_like(db_ref)

        taps = _conv_taps(x_ref[...])
        pre = _conv_pre(taps, w_ref, b_ref)
        sg = _sigmoid(pre)
        dpre = d_ref[...] * (sg * (1.0 + pre * (1.0 - sg)))
        rows = lax.broadcasted_iota(jnp.int32, dpre.shape, 0)
        db_ref[...] += jnp.sum(dpre, axis=0, keepdims=True)
        dx = dpre * w_ref[SSM_CONV - 1:SSM_CONV, :]
        for k in range(SSM_CONV):
            dw_ref[k:k + 1, :] += jnp.sum(dpre * taps[k], axis=0, keepdims=True)
        for k in range(SSM_CONV - 1):
            sh = SSM_CONV - 1 - k
            dsh = jnp.where(rows < s - sh, pltpu.roll(dpre, s - sh, axis=0), 0.0)
            dx = dx + dsh * w_ref[k:k + 1, :]
        dx_ref[...] = dx.astype(MXU_DTYPE)

    return pl.pallas_call(
        body, grid=(nj, bsz),
        in_specs=[pl.BlockSpec((s, CONV_TC), lambda j, bb: (bb, c0 + j)), pl.BlockSpec((s, CONV_TC), lambda j, bb: (bb, j)),
                  pl.BlockSpec((SSM_CONV, CONV_TC), lambda j, bb: (0, j)), pl.BlockSpec((1, CONV_TC), lambda j, bb: (0, j)), _ANY],
        out_specs=[pl.BlockSpec((s, CONV_TC), lambda j, bb: (bb, c0 + j)), pl.BlockSpec((SSM_CONV, CONV_TC), lambda j, bb: (0, j)),
                   pl.BlockSpec((1, CONV_TC), lambda j, bb: (0, j))],
        out_shape=[jax.ShapeDtypeStruct(dproj.shape, dproj.dtype), jax.ShapeDtypeStruct((SSM_CONV, SSM_CONV_DIM), F32),
                   jax.ShapeDtypeStruct((1, SSM_CONV_DIM), F32)],
        input_output_aliases={4: 0},
        compiler_params=_params("parallel", "arbitrary"), name=name)(proj, dact, w, b.reshape(1, -1), dproj)


def _softplus(x):
    return jnp.maximum(x, 0.0) + jnp.log1p(jnp.exp(-jnp.abs(x)))


def _pad_heads(v):
    return jnp.broadcast_to(jnp.pad(v.astype(F32), (0, HEAD_PAD - SSM_HEADS))[None, :], (SUBLANE, HEAD_PAD))


def _ssd_prep(dt_raw, dt_bias8, a_log8, *, name):
    t = dt_raw.shape[0]
    n = CHUNK

    def body(r_ref, b_ref, al_ref, dt_ref, cs_ref, dtt_ref, cst_ref):
        dt = _softplus(r_ref[...] + b_ref[0:1, :])
        da = dt * (-jnp.exp(al_ref[0:1, :]))
        row = lax.broadcasted_iota(jnp.int32, (n, n), 0)
        col = lax.broadcasted_iota(jnp.int32, (n, n), 1)
        lower = (col <= row).astype(F32)
        upper = (row <= col).astype(F32)
        eye = (row == col).astype(F32)
        dt_ref[...] = dt
        cs_ref[...] = _dot_exact(lower, da, _DN_NN, 1)
        cst_ref[0] = _dot_exact(da, upper, _DN_TN, 0)
        dtt_ref[0] = _dot_exact(dt, eye, _DN_TN, 0)

    hp = HEAD_PAD
    return pl.pallas_call(
        body, grid=(t // n,),
        in_specs=[_row_spec(n, hp), _par_spec((SUBLANE, hp)), _par_spec((SUBLANE, hp))],
        out_specs=[_row_spec(n, hp), _row_spec(n, hp), pl.BlockSpec((1, hp, n), lambda i: (i, 0, 0)),
                   pl.BlockSpec((1, hp, n), lambda i: (i, 0, 0))],
        out_shape=[jax.ShapeDtypeStruct((t, hp), F32), jax.ShapeDtypeStruct((t, hp), F32),
                   jax.ShapeDtypeStruct((t // n, hp, n), F32), jax.ShapeDtypeStruct((t // n, hp, n), F32)],
        compiler_params=_params("parallel"), name=name)(dt_raw, dt_bias8, a_log8)


def _expand_mat():
    h = lax.broadcasted_iota(jnp.int32, (HEAD_PAD, SSM_INNER), 0)
    ch = lax.broadcasted_iota(jnp.int32, (HEAD_PAD, SSM_INNER), 1)
    return (ch // SSM_HEADDIM == h).astype(F32)


def _reduce_mat():
    ch = lax.broadcasted_iota(jnp.int32, (SSM_INNER, HEAD_PAD), 0)
    h = lax.broadcasted_iota(jnp.int32, (SSM_INNER, HEAD_PAD), 1)
    return (ch // SSM_HEADDIM == h).astype(F32)


def _expand(v, em):
    return _dot_exact(v, em, _DN_NN, 0)


def _expand_heads(v):
    return jnp.repeat(v.astype(F32), SSM_HEADDIM)[None, :]


def _decay_mat(cs_ref, cst_ref, h, mask):
    seg = cs_ref[:, h:h + 1] - cst_ref[0, h:h + 1, :]
    return jnp.where(mask, jnp.exp(jnp.minimum(seg, 0.0)), 0.0)


GROUP_CH = SSM_INNER // SSM_GROUPS
PAIRS_PER_GROUP = GROUP_CH // LANE
HEADS_PER_GROUP = SSM_HEADS // SSM_GROUPS
BM_COL0 = SSM_INNER
CM_COL0 = SSM_INNER + SSM_GROUPS * SSM_STATE


def _ssd_specs(nc, rev):
    def cidx(i):
        return (i // nc) * nc + (nc - 1 - i % nc) if rev else i

    n = CHUNK
    xs = pl.BlockSpec((n, SSM_INNER), lambda i: (cidx(i), 0))
    bm = pl.BlockSpec((n, GROUP_CH), lambda i: (cidx(i), BM_COL0 // GROUP_CH))
    cm = pl.BlockSpec((n, GROUP_CH), lambda i: (cidx(i), CM_COL0 // GROUP_CH))
    hv = pl.BlockSpec((n, HEAD_PAD), lambda i: (cidx(i), 0))
    hvt = pl.BlockSpec((1, HEAD_PAD, n), lambda i: (cidx(i), 0, 0))
    st = pl.BlockSpec((1, SSM_INNER, SSM_STATE), lambda i: (cidx(i), 0, 0))
    return xs, bm, cm, hv, hvt, st


def _ssd_fwd(xbc, dt, cs, dtt, cst, dskx, *, nc, name):
    t = xbc.shape[0]
    n = CHUNK
    xs_s, bm_s, cm_s, hv_s, hvt_s, st_s = _ssd_specs(nc, False)

    def body(xs_ref, bm_ref, cm_ref, dt_ref, cs_ref, dtt_ref, cst_ref, dsk_ref, y_ref, st_ref, prev):
        @pl.when(pl.program_id(0) % nc == 0)
        def _():
            prev[...] = jnp.zeros_like(prev)

        st_ref[0] = prev[...]
        em = _expand_mat()
        dtx = _expand(dt_ref[...], em)
        csx = _expand(cs_ref[...], em)
        dskx = dsk_ref[...]
        xs = xs_ref[...]
        xdt = xs * dtx
        ecs = jnp.exp(csx)
        dec = jnp.exp(csx[n - 1:n, :] - csx)
        mask = _causal(n)
        lane = lax.broadcasted_iota(jnp.int32, (n, LANE), 1)
        for g in range(SSM_GROUPS):
            gs = slice(g * SSM_STATE, (g + 1) * SSM_STATE)
            gc = slice(g * GROUP_CH, (g + 1) * GROUP_CH)
            cmat = cm_ref[:, gs].astype(MXU_DTYPE)
            bmat = bm_ref[:, gs].astype(MXU_DTYPE)
            cb = _nt(cmat, bmat)
            yoff = ecs[:, gc] * _nt(cmat, prev[gc, :].astype(MXU_DTYPE))
            for q in range(PAIRS_PER_GROUP):
                hp = g * PAIRS_PER_GROUP + q
                sl = slice(hp * LANE, (hp + 1) * LANE)
                xp = xdt[:, sl].astype(MXU_DTYPE)
                m0 = (cb * _decay_mat(cs_ref, cst_ref, 2 * hp, mask)).astype(MXU_DTYPE)
                m1 = (cb * _decay_mat(cs_ref, cst_ref, 2 * hp + 1, mask)).astype(MXU_DTYPE)
                yd = jnp.where(lane < SSM_HEADDIM, _nn(m0, xp), _nn(m1, xp))
                y_ref[:, sl] = yd + yoff[:, q * LANE:(q + 1) * LANE] + xs[:, sl] * dskx[:, sl]
            snew = _tn((xdt[:, gc] * dec[:, gc]).astype(MXU_DTYPE), bmat)
            for r in range(HEADS_PER_GROUP):
                h = g * HEADS_PER_GROUP + r
                rows = slice(h * SSM_HEADDIM, (h + 1) * SSM_HEADDIM)
                e = jnp.exp(cst_ref[0, h:h + 1, n - 1:n])
                prev[rows, :] = prev[rows, :] * e + snew[r * SSM_HEADDIM:(r + 1) * SSM_HEADDIM, :]

    return pl.pallas_call(
        body, grid=(t // n,),
        in_specs=[xs_s, bm_s, cm_s, hv_s, hv_s, hvt_s, hvt_s, _par_spec((1, SSM_INNER))],
        out_specs=[xs_s, st_s],
        out_shape=[jax.ShapeDtypeStruct((t, SSM_INNER), F32), jax.ShapeDtypeStruct((t // n, SSM_INNER, SSM_STATE), F32)],
        scratch_shapes=[pltpu.VMEM((SSM_INNER, SSM_STATE), F32)],
        compiler_params=_params("arbitrary"), name=name)(xbc, xbc, xbc, dt, cs, dtt, cst, dskx)


def _ssd_bwd(dy, xbc, dt, cs, dtt, cst, st, dskx, a_log8, dt_raw, dt_bias8, *, nc, name):
    t = xbc.shape[0]
    n = CHUNK
    xs_s, bm_s, cm_s, hv_s, hvt_s, st_s = _ssd_specs(nc, True)
    acc_s = _par_spec((1, HEAD_PAD))
    xbc_s = pl.BlockSpec((n, SSM_CONV_DIM), xs_s.index_map)

    def body(dy_ref, xs_ref, bm_ref, cm_ref, dt_ref, cs_ref, dtt_ref, cst_ref, st_ref, dsk_ref, al_ref, raw_ref, bias_ref,
             dxbc_ref, ddr_ref, dal_ref, dds_ref, dbias_ref, dprev, dxdt_s, tdec_s, tcs_s):
        @pl.when(pl.program_id(0) % nc == 0)
        def _():
            dprev[...] = jnp.zeros_like(dprev)

        @pl.when(pl.program_id(0) == 0)
        def _():
            dal_ref[...] = jnp.zeros_like(dal_ref)
            dds_ref[...] = jnp.zeros_like(dds_ref)
            dbias_ref[...] = jnp.zeros_like(dbias_ref)

        em = _expand_mat()
        rm = _reduce_mat()

        def head_reduce(v):
            return _dot_exact(v, rm, _DN_NN, 0)

        dtv = dt_ref[...]
        csv = cs_ref[...]
        dtx = _expand(dtv, em)
        csx = _expand(csv, em)
        dskx = dsk_ref[...]
        xs = xs_ref[...]
        dyv = dy_ref[...]
        xdt = xs * dtx
        ecs = jnp.exp(csx)
        dec = jnp.exp(csx[n - 1:n, :] - csx)
        mask = _causal(n)
        lane = lax.broadcasted_iota(jnp.int32, (n, LANE), 1)
        hlane = lax.broadcasted_iota(jnp.int32, (1, HEAD_PAD), 1)
        hsub = lax.broadcasted_iota(jnp.int32, (HEAD_PAD, 1), 0)
        rsum = jnp.zeros((n, HEAD_PAD), F32)
        csum = jnp.zeros((HEAD_PAD, n), F32)
        for g in range(SSM_GROUPS):
            gs = slice(g * SSM_STATE, (g + 1) * SSM_STATE)
            gc = slice(g * GROUP_CH, (g + 1) * GROUP_CH)
            cmat = cm_ref[:, gs].astype(MXU_DTYPE)
            bmat = bm_ref[:, gs].astype(MXU_DTYPE)
            cb = _nt(cmat, bmat)
            pg = st_ref[0, gc, :].astype(MXU_DTYPE)
            dpg = dprev[gc, :]
            dpgb = dpg.astype(MXU_DTYPE)
            z = _nt(cmat, pg)
            dyg = dyv[:, gc]
            dz = (dyg * ecs[:, gc]).astype(MXU_DTYPE)
            dc = _nn(dz, pg)
            dprev_y = _tn(dz, cmat)
            tcs_s[:, gc] = dyg * z * ecs[:, gc]
            xd = xdt[:, gc] * dec[:, gc]
            wmat = _nt(bmat, dpgb)
            db = _nn(xd.astype(MXU_DTYPE), dpgb)
            tdec_s[:, gc] = wmat * xd
            dxdt_g = wmat * dec[:, gc]
            dcb = jnp.zeros((n, n), F32)
            for q in range(PAIRS_PER_GROUP):
                hp = g * PAIRS_PER_GROUP + q
                sl = slice(hp * LANE, (hp + 1) * LANE)
                xp = xdt[:, sl].astype(MXU_DTYPE)
                dyp = dyv[:, sl]
                dypb = dyp.astype(MXU_DTYPE)
                dxp = None
                for hh in range(2):
                    h = 2 * hp + hh
                    lm = _decay_mat(cs_ref, cst_ref, h, mask)
                    mine = (lane < SSM_HEADDIM) if hh == 0 else (lane >= SSM_HEADDIM)
                    dm = _nt(jnp.where(mine, dyp, 0.0).astype(MXU_DTYPE), xp)
                    dml = dm * lm
                    dcb = dcb + dml
                    gseg = dml * cb
                    rsum = rsum + jnp.sum(gseg, axis=1, keepdims=True) * (hlane == h).astype(F32)
                    csum = csum + (hsub == h).astype(F32) * jnp.sum(gseg, axis=0, keepdims=True)
                    dxh = _tn((cb * lm).astype(MXU_DTYPE), dypb)
                    dxp = dxh if dxp is None else jnp.where(mine, dxh, dxp)
                dxdt_s[:, sl] = dxdt_g[:, q * LANE:(q + 1) * LANE] + dxp
            dcbb = dcb.astype(MXU_DTYPE)
            dxbc_ref[:, CM_COL0 + g * SSM_STATE:CM_COL0 + (g + 1) * SSM_STATE] = dc + _nn(dcbb, bmat)
            dxbc_ref[:, BM_COL0 + g * SSM_STATE:BM_COL0 + (g + 1) * SSM_STATE] = db + _tn(dcbb, cmat)
            for r in range(HEADS_PER_GROUP):
                h = g * HEADS_PER_GROUP + r
                rows = slice(h * SSM_HEADDIM, (h + 1) * SSM_HEADDIM)
                lr = slice(r * SSM_HEADDIM, (r + 1) * SSM_HEADDIM)
                e = jnp.exp(cst_ref[0, h:h + 1, n - 1:n])
                dprev[rows, :] = dpg[lr, :] * e + dprev_y[lr, :]
            tq = _dot_exact(dpg * st_ref[0, gc, :], rm[gc, :], _DN_TN, 0)
            if g == 0:
                qsum = jnp.sum(tq, axis=0, keepdims=True)
            else:
                qsum = qsum + jnp.sum(tq, axis=0, keepdims=True)
        dxdt = dxdt_s[...]
        dxbc_ref[:, 0:SSM_INNER] = dxdt * dtx + dyv * dskx
        ddt = head_reduce(dxdt * xs)
        edec = head_reduce(tdec_s[...])
        ycs = head_reduce(tcs_s[...])
        row = lax.broadcasted_iota(jnp.int32, (n, HEAD_PAD), 0)
        extra = jnp.sum(edec, axis=0, keepdims=True) + qsum * jnp.exp(csv[n - 1:n, :])
        dcs = rsum - csum.T + ycs - edec + jnp.where(row == n - 1, extra, 0.0)
        r2 = lax.broadcasted_iota(jnp.int32, (n, n), 0)
        c2 = lax.broadcasted_iota(jnp.int32, (n, n), 1)
        dda = _dot_exact((c2 >= r2).astype(F32), dcs, _DN_NN, 1)
        a_row = -jnp.exp(al_ref[0:1, :])
        ddt = ddt + dda * a_row
        dal_ref[...] += jnp.sum(dda * dtv, axis=0, keepdims=True) * a_row
        dds_ref[...] += jnp.sum(head_reduce(dyv * xs), axis=0, keepdims=True)
        ddr = ddt * _sigmoid(raw_ref[...] + bias_ref[0:1, :])
        ddr_ref[...] = ddr
        dbias_ref[...] += jnp.sum(ddr, axis=0, keepdims=True)

    par8 = _par_spec((SUBLANE, HEAD_PAD))
    return pl.pallas_call(
        body, grid=(t // n,),
        in_specs=[xs_s, xs_s, bm_s, cm_s, hv_s, hv_s, hvt_s, hvt_s, st_s, _par_spec((1, SSM_INNER)), par8, hv_s, par8],
        out_specs=[xbc_s, hv_s, acc_s, acc_s, acc_s],
        out_shape=[jax.ShapeDtypeStruct((t, SSM_CONV_DIM), F32), jax.ShapeDtypeStruct((t, HEAD_PAD), F32),
                   jax.ShapeDtypeStruct((1, HEAD_PAD), F32), jax.ShapeDtypeStruct((1, HEAD_PAD), F32),
                   jax.ShapeDtypeStruct((1, HEAD_PAD), F32)],
        scratch_shapes=[pltpu.VMEM((SSM_INNER, SSM_STATE), F32), pltpu.VMEM((n, SSM_INNER), F32),
                        pltpu.VMEM((n, SSM_INNER), F32), pltpu.VMEM((n, SSM_INNER), F32)],
        compiler_params=_params("arbitrary"), name=name)(dy, xbc, xbc, xbc, dt, cs, dtt, cst, st, dskx, a_log8, dt_raw, dt_bias8)


def _gate_norm_fwd(y, proj, norm_g, *, name):
    t, c = y.shape
    tm = _pick(t, (256, 128))

    def body(y_ref, z_ref, g_ref, o_ref):
        z = z_ref[...]
        yz = y_ref[...] * z * _sigmoid(z)
        for g in range(SSM_GROUPS):
            gc = slice(g * GROUP_CH, (g + 1) * GROUP_CH)
            seg = yz[:, gc]
            r = lax.rsqrt(jnp.mean(seg * seg, axis=-1, keepdims=True) + RMS_EPS)
            o_ref[:, gc] = (seg * r * g_ref[:, gc]).astype(MXU_DTYPE)

    return pl.pallas_call(
        body, grid=(t // tm,), in_specs=[_row_spec(tm, c), _row_spec(tm, c, 1), _par_spec((1, c))],
        out_specs=_row_spec(tm, c), out_shape=jax.ShapeDtypeStruct((t, c), MXU_DTYPE),
        compiler_params=_params("parallel"), name=name)(y, proj, norm_g.reshape(1, c))


def _gate_norm_bwd(dyb, y, proj, norm_g, dproj, *, name):
    t, c = y.shape
    tm = _pick(t, (256, 128))

    def body(d_ref, y_ref, z_ref, g_ref, _, dy_ref, dz_ref, dg_ref):
        @pl.when(pl.program_id(0) == 0)
        def _():
            dg_ref[...] = jnp.zeros_like(dg_ref)

        z = z_ref[...]
        yv = y_ref[...]
        sz = _sigmoid(z)
        silu = z * sz
        yz = yv * silu
        dv = d_ref[...]
        for g in range(SSM_GROUPS):
            gc = slice(g * GROUP_CH, (g + 1) * GROUP_CH)
            seg = yz[:, gc]
            r = lax.rsqrt(jnp.mean(seg * seg, axis=-1, keepdims=True) + RMS_EPS)
            nrm = seg * r
            dn = dv[:, gc] * g_ref[:, gc]
            dg_ref[:, gc] += jnp.sum(dv[:, gc] * nrm, axis=0, keepdims=True)
            dyz = r * (dn - nrm * jnp.mean(dn * nrm, axis=-1, keepdims=True))
            dy_ref[:, gc] = dyz * silu[:, gc]
            dz_ref[:, gc] = (dyz * yv[:, gc] * (sz[:, gc] * (1.0 + z[:, gc] * (1.0 - sz[:, gc])))).astype(MXU_DTYPE)

    return pl.pallas_call(
        body, grid=(t // tm,), in_specs=[_row_spec(tm, c), _row_spec(tm, c), _row_spec(tm, c, 1), _par_spec((1, c)), _ANY],
        out_specs=[_row_spec(tm, c), _row_spec(tm, c, 1), _par_spec((1, c))],
        out_shape=[jax.ShapeDtypeStruct((t, c), F32), jax.ShapeDtypeStruct(dproj.shape, dproj.dtype),
                   jax.ShapeDtypeStruct((1, c), F32)],
        input_output_aliases={4: 1},
        compiler_params=_params("arbitrary"), name=name)(dyb, y, proj, norm_g.reshape(1, c), dproj)


GA_COLBLK = GAB_COL0 // D_MODEL


def _merge_fwd(br_a, br_b, proj, *, name):
    t, c = br_a.shape
    tm = _pick(t, ROW_TILES)

    def body(a_ref, b_ref, ga_ref, gb_ref, o_ref):
        o_ref[...] = (_sigmoid(ga_ref[...]) * a_ref[...] + _sigmoid(gb_ref[...]) * b_ref[...]).astype(MXU_DTYPE)

    return pl.pallas_call(
        body, grid=(t // tm,),
        in_specs=[_row_spec(tm, c), _row_spec(tm, c), _row_spec(tm, c, GA_COLBLK), _row_spec(tm, c, GA_COLBLK + 1)],
        out_specs=_row_spec(tm, c), out_shape=jax.ShapeDtypeStruct((t, c), MXU_DTYPE),
        compiler_params=_params("parallel"), name=name)(br_a, br_b, proj, proj)


def _merge_bwd(dm, br_a, br_b, proj, *, name):
    t, c = br_a.shape
    tm = _pick(t, ROW_TILES)

    def body(dm_ref, a_ref, b_ref, ga_ref, gb_ref, da_ref, db_ref, dg_ref):
        d = dm_ref[...]
        sa = _sigmoid(ga_ref[...])
        sb = _sigmoid(gb_ref[...])
        da_ref[...] = (d * sa).astype(MXU_DTYPE)
        db_ref[...] = (d * sb).astype(MXU_DTYPE)
        dg_ref[:, :c] = (d * a_ref[...] * sa * (1.0 - sa)).astype(MXU_DTYPE)
        dg_ref[:, c:] = (d * b_ref[...] * sb * (1.0 - sb)).astype(MXU_DTYPE)

    return pl.pallas_call(
        body, grid=(t // tm,),
        in_specs=[_row_spec(tm, c), _row_spec(tm, c), _row_spec(tm, c), _row_spec(tm, c, GA_COLBLK), _row_spec(tm, c, GA_COLBLK + 1)],
        out_specs=[_row_spec(tm, c), _row_spec(tm, c), _row_spec(tm, 2 * c, GAB_COL0 // (2 * c))],
        out_shape=[jax.ShapeDtypeStruct((t, c), MXU_DTYPE), jax.ShapeDtypeStruct((t, c), MXU_DTYPE),
                   jax.ShapeDtypeStruct((t, MAIN_COLS), MXU_DTYPE)],
        compiler_params=_params("parallel"), name=name)(dm, br_a, br_b, proj, proj)


def _layer_fwd(x, xb, memn_b, w, *, bsz, tag):
    nc = x.shape[0] // bsz // CHUNK
    sv = {"x_in": xb}
    proj = _mm(xb, w["w_main"], name=f"{tag}_proj")
    dt_raw = _mm(xb, w["w_dt"], name=f"{tag}_dtproj")
    sgo = _sg_fwd(proj, w["sg_ln_g"], w["sg_ln_b"], w["sg_w"], w["sg_bcol"], name=f"{tag}_sg_fwd")
    xbc = _conv_fwd(proj, w["conv_w"], w["conv_b"], bsz=bsz, name=f"{tag}_conv_fwd")
    dt, cs, dtt, cst = _ssd_prep(dt_raw, w["dt_bias8"], w["a_log8"], name=f"{tag}_ssd_prep")
    y, st = _ssd_fwd(xbc, dt, cs, dtt, cst, w["d_skipx"], nc=nc, name=f"{tag}_ssd_fwd")
    yb = _gate_norm_fwd(y, proj, w["ssm_norm_g"], name=f"{tag}_gate_norm_fwd")
    br_a = _mm(sgo, w["p_a"], name=f"{tag}_br_a")
    br_b = _mm(yb, w["p_b"], name=f"{tag}_br_b")
    merged = _merge_fwd(br_a, br_b, proj, name=f"{tag}_merge_fwd")
    mix = _mm(merged, w["w_mix_o"], name=f"{tag}_mix_o")
    x1, x1b, xh1, rs1 = _ln_fwd(x, mix, w["ln_g"][0], w["ln_b"][0], name=f"{tag}_ln1_fwd")
    sv.update(proj=proj, dt_raw=dt_raw, sgo=sgo, xbc=xbc, dt=dt, cs=cs, dtt=dtt, cst=cst, y=y, st=st, yb=yb,
              br_a=br_a, br_b=br_b, merged=merged, xh1=xh1, rs1=rs1, x1b=x1b)
    q = _mm(x1b, w["w_xq"], out_dtype=MXU_DTYPE, name=f"{tag}_q")
    kv = _mm(memn_b, w["w_xkv"], out_dtype=MXU_DTYPE, name=f"{tag}_kv")
    o = _attn_fwd(q, kv, bsz=bsz, name=f"{tag}_attn_fwd")
    att = _mm(o, w["w_xo"], name=f"{tag}_xo")
    x2, x2b, xh2, rs2 = _ln_fwd(x1, att, w["ln_g"][1], w["ln_b"][1], name=f"{tag}_ln2_fwd")
    sv.update(q=q, kv=kv, o=o, xh2=xh2, rs2=rs2, x2b=x2b)
    h = _mm(x2b, w["w_ffn_in"], name=f"{tag}_ffn_in")
    a = _swiglu_fwd(h, name=f"{tag}_swiglu_fwd")
    ffn = _mm(a, w["w_ffn_out"], name=f"{tag}_ffn_out")
    x3, x3b, xh3, rs3 = _ln_fwd(x2, ffn, w["ln_g"][2], w["ln_b"][2], name=f"{tag}_ln3_fwd")
    sv.update(h=h, a=a, xh3=xh3, rs3=rs3)
    return x3, x3b, sv


def _layer_bwd(dx3_addends, dx3_scales, memn_b, w, sv, *, bsz, tag):
    nc = sv["xh1"].shape[0] // bsz // CHUNK
    gr = {}
    dp3, dp3b, dg3, db3 = _ln_bwd(dx3_addends, dx3_scales, sv["xh3"], sv["rs3"], w["ln_g"][2], name=f"{tag}_ln3_bwd")
    da = _mm(dp3b, w["w_ffn_out"], tb=True, name=f"{tag}_d_a")
    gr["w_ffn_out"] = _mm(sv["a"], dp3b, ta=True, name=f"{tag}_dw_ffn_out")
    dh = _swiglu_bwd(sv["h"], da, name=f"{tag}_swiglu_bwd")
    gr["w_ffn_in"] = _mm(sv["x2b"], dh, ta=True, name=f"{tag}_dw_ffn_in")
    dx2_br = _mm(dh, w["w_ffn_in"], tb=True, name=f"{tag}_dx2")
    dp2, dp2b, dg2, db2 = _ln_bwd([dp3, dx2_br], [ALPHA, 1.0], sv["xh2"], sv["rs2"], w["ln_g"][1], name=f"{tag}_ln2_bwd")
    do = _mm(dp2b, w["w_xo"], tb=True, out_dtype=MXU_DTYPE, name=f"{tag}_d_o")
    gr["w_xo"] = _mm(sv["o"], dp2b, ta=True, name=f"{tag}_dw_xo")
    dq, dk, dv = _attn_bwd(sv["q"], sv["kv"], do, bsz=bsz, name=f"{tag}_attn_bwd")
    dkv = jnp.concatenate([dk, dv], axis=1)
    gr["w_xq"] = _mm(sv["x1b"], dq, ta=True, name=f"{tag}_dw_xq")
    gr["w_xkv"] = _mm(memn_b, dkv, ta=True, name=f"{tag}_dw_xkv")
    dmemn = _mm(dkv, w["w_xkv"], tb=True, name=f"{tag}_d_memn")
    dx1_br = _mm(dq, w["w_xq"], tb=True, name=f"{tag}_dx1")
    dp1, dp1b, dg1, db1 = _ln_bwd([dp2, dx1_br], [ALPHA, 1.0], sv["xh1"], sv["rs1"], w["ln_g"][0], name=f"{tag}_ln1_bwd")
    gr["ln_g"] = jnp.concatenate([dg1, dg2, dg3], axis=0)
    gr["ln_b"] = jnp.concatenate([db1, db2, db3], axis=0)
    dmerged = _mm(dp1b, w["w_mix_o"], tb=True, name=f"{tag}_d_merged")
    gr["w_mix_o"] = _mm(sv["merged"], dp1b, ta=True, name=f"{tag}_dw_mix_o")
    dbr_a, dbr_b, dproj = _merge_bwd(dmerged, sv["br_a"], sv["br_b"], sv["proj"], name=f"{tag}_merge_bwd")
    gr["p_a"] = _mm(sv["sgo"], dbr_a, ta=True, name=f"{tag}_dw_p_a")
    gr["p_b"] = _mm(sv["yb"], dbr_b, ta=True, name=f"{tag}_dw_p_b")
    dsgo = _mm(dbr_a, w["p_a"], tb=True, name=f"{tag}_d_sgo")
    dyb = _mm(dbr_b, w["p_b"], tb=True, name=f"{tag}_d_yb")
    dy, dproj, gr["ssm_norm_g"] = _gate_norm_bwd(dyb, sv["y"], sv["proj"], w["ssm_norm_g"], dproj, name=f"{tag}_gate_norm_bwd")
    dxbc, ddr, gr["a_log"], gr["d_skip"], gr["dt_bias"] = _ssd_bwd(
        dy, sv["xbc"], sv["dt"], sv["cs"], sv["dtt"], sv["cst"], sv["st"], w["d_skipx"], w["a_log8"], sv["dt_raw"],
        w["dt_bias8"], nc=nc, name=f"{tag}_ssd_bwd")
    dproj, gr["conv_w"], gr["conv_b"] = _conv_bwd(sv["proj"], dxbc, w["conv_w"], w["conv_b"], dproj, bsz=bsz, name=f"{tag}_conv_bwd")
    dproj, gr["sg_w"], dsg_bcol, gr["sg_ln_g"], gr["sg_ln_b"] = _sg_bwd(
        sv["proj"], dsgo, w["sg_ln_g"], w["sg_ln_b"], w["sg_w"], w["sg_bcol"], dproj, name=f"{tag}_sg_bwd")
    gr["sg_b"] = dsg_bcol[..., 0]
    gr["w_main"] = _mm(sv["x_in"], dproj, ta=True, name=f"{tag}_dw_main")
    gr["w_dt"] = _mm(sv["x_in"], ddr, ta=True, name=f"{tag}_dw_dt")
    dx_main = _mm(dproj, w["w_main"], tb=True, name=f"{tag}_dx_main")
    dx_dt = _mm(ddr, w["w_dt"], tb=True, name=f"{tag}_dx_dt")
    return [dp1, dx_main, dx_dt], [ALPHA, 1.0, 1.0], gr, dmemn


def _local_step(x, mem, tgt, mem_ln_g, mem_ln_b, layers, on_layer_grads=None):
    bsz, s, d = x.shape
    xf = x.reshape(bsz * s, d)
    memf = mem.reshape(-1, d)
    _, memn_b, mxh, mrs = _ln_fwd(memf, None, mem_ln_g, mem_ln_b, name="mem_ln_fwd")
    cur, curb, saved = xf, xf, []
    for li, w in enumerate(layers):
        cur, curb, sv = _layer_fwd(cur, curb, memn_b, w, bsz=bsz, tag=f"l{li}")
        saved.append(sv)
    dy, lsum = _loss_head(cur, tgt.reshape(bsz * s, d), name="loss_head")
    addends, scales = [dy], [1.0]
    grads, dmem = [None] * len(layers), []
    token = None
    for li in reversed(range(len(layers))):
        w = layers[li]
        if token is not None:
            w = dict(w, ln_g=w["ln_g"] + token[0, 0])
        addends, scales, grads[li], dm = _layer_bwd(addends, scales, memn_b, w, saved[li], bsz=bsz, tag=f"l{li}")
        dmem.append(dm)
        token = on_layer_grads(li, grads[li]) if on_layer_grads is not None else None
    grad_x = _add_scaled(addends, scales, name="grad_x").reshape(bsz, s, d)
    _, _, dmg, dmb = _ln_bwd(dmem, [1.0] * len(dmem), mxh, mrs, mem_ln_g, name="mem_ln_bwd")
    return lsum, grad_x, grads, dmg[0], dmb[0]


_ANY = pl.BlockSpec(memory_space=pl.ANY)
_MESH = pl.DeviceIdType.MESH


def _all_gather8(x, *, name):
    def body(x_ref, out_ref, send_sems, recv_sems):
        mx, my, mc = lax.axis_index("x"), lax.axis_index("y"), lax.axis_index("c")
        me, sibling = (mx, my, mc), (mx, my, 1 - mc)
        chips = [(1 - mx, my), (mx, 1 - my), (1 - mx, 1 - my)]

        def blk(px, py, pc):
            return out_ref.at[4 * px + 2 * py + pc]

        def copy(k, block, to, src=None):
            return pltpu.make_async_remote_copy(
                src_ref=blk(*block) if src is None else src, dst_ref=blk(*block), send_sem=send_sems.at[k],
                recv_sem=recv_sems.at[k], device_id=to, device_id_type=_MESH)

        first = [copy(0, me, sibling, src=x_ref)]
        first += [copy(1 + j, me, (*chip, mc), src=x_ref) for j, chip in enumerate(chips)]
        for cp in first:
            cp.start()
        passed = [copy(4 + j, (*chip, mc), sibling) for j, chip in enumerate(chips)]
        for j, chip in enumerate(chips):
            copy(1 + j, (*chip, mc), me).wait_recv()
            passed[j].start()
        copy(0, sibling, me).wait_recv()
        for j, chip in enumerate(chips):
            copy(4 + j, (*chip, 1 - mc), me).wait_recv()
        for cp in first + passed:
            cp.wait_send()

    return pl.pallas_call(
        body, out_shape=jax.ShapeDtypeStruct((N_DEV,) + x.shape, x.dtype), in_specs=[_ANY], out_specs=_ANY,
        scratch_shapes=[pltpu.SemaphoreType.DMA((7,)), pltpu.SemaphoreType.DMA((7,))], name=name)(x)


def _row_tile(rows, row_bytes, mult=SUBLANE):
    best = None
    for tr in range(mult, rows + 1, mult):
        if rows % tr == 0 and (best is None or tr * row_bytes <= BLOCK_BYTES):
            best = tr
    return rows if best is None else best


def _gather_shape(r, c, kind):
    return {"row": (2, N_CHIPS * r, c), "col": (2, r, N_CHIPS * c), "chip": (2, N_CHIPS, r, c)}[kind]


def _cast_place(shard, kind, dtype, chip_idx, *, name):
    _, r, c = shard.shape
    tr = _row_tile(r, c * 4, 16)
    nt = r // tr

    def body(_, s_ref, o_ref):
        o_ref[...] = s_ref[...].astype(dtype)

    if kind == "row":
        out_spec = pl.BlockSpec((None, tr, c), lambda l, i, j_ref: (l, j_ref[0] * nt + i, 0))
    elif kind == "col":
        out_spec = pl.BlockSpec((None, tr, c), lambda l, i, j_ref: (l, i, j_ref[0]))
    else:
        out_spec = pl.BlockSpec((None, None, tr, c), lambda l, i, j_ref: (l, j_ref[0], i, 0))
    grid_spec = pltpu.PrefetchScalarGridSpec(
        num_scalar_prefetch=1, grid=(2, nt), in_specs=[pl.BlockSpec((None, tr, c), lambda l, i, j_ref: (l, i, 0))],
        out_specs=out_spec)
    return pl.pallas_call(body, grid_spec=grid_spec, out_shape=jax.ShapeDtypeStruct(_gather_shape(r, c, kind), dtype),
                          compiler_params=_params("parallel", "parallel"), name=name)(chip_idx, shard)


def _gather_params(bufs, shard_shapes, kinds, *, name):
    n = len(bufs)

    def body(*refs):
        outs = refs[n:2 * n]
        send_sems, recv_sems = refs[2 * n:]
        mx, my, mc = lax.axis_index("x"), lax.axis_index("y"), lax.axis_index("c")
        me, sibling = (mx, my, mc), (mx, my, 1 - mc)
        chips = [(1 - mx, my), (mx, 1 - my), (1 - mx, 1 - my)]

        def blk(i, px, py, pc):
            r, c = shard_shapes[i]
            j = 2 * px + py
            if kinds[i] == "row":
                return outs[i].at[pc, pl.ds(pl.multiple_of(j * r, r), r)]
            if kinds[i] == "col":
                return outs[i].at[pc, :, pl.ds(pl.multiple_of(j * c, c), c)]
            return outs[i].at[pc, j]

        def copy(i, k, block, to):
            return pltpu.make_async_remote_copy(
                src_ref=blk(i, *block), dst_ref=blk(i, *block), send_sem=send_sems.at[6 * i + k],
                recv_sem=recv_sems.at[6 * i + k], device_id=to, device_id_type=_MESH)

        sent = []
        for i in range(n):
            for j, chip in enumerate(chips):
                cp = copy(i, j, me, (*chip, mc))
                cp.start()
                sent.append(cp)
        for j, chip in enumerate(chips):
            for i in range(n):
                copy(i, j, (*chip, mc), me).wait_recv()
                fwd = copy(i, 3 + j, (*chip, mc), sibling)
                fwd.start()
                sent.append(fwd)
        for i in range(n):
            for j, chip in enumerate(chips):
                copy(i, 3 + j, (*chip, 1 - mc), me).wait_recv()
        for cp in sent:
            cp.wait_send()

    return pl.pallas_call(
        body, out_shape=[jax.ShapeDtypeStruct(b.shape, b.dtype) for b in bufs], in_specs=[_ANY] * n, out_specs=[_ANY] * n,
        input_output_aliases={i: i for i in range(n)},
        scratch_shapes=[pltpu.SemaphoreType.DMA((6 * n,)), pltpu.SemaphoreType.DMA((6 * n,))], name=name)(*bufs)


def _half(r, h):
    return pl.ds(pl.multiple_of(h * (r // 2), r // 2), r // 2)


def _grads_to_sibling(gs, views, *, name):
    n = len(gs)

    def recv_shape(g, view):
        if view == "chip":
            return jax.ShapeDtypeStruct((g.shape[0], g.shape[1] // 2, g.shape[2]), g.dtype)
        return jax.ShapeDtypeStruct((g.shape[0] // 2, g.shape[1]), g.dtype)

    def body(*refs):
        ins, outs = refs[:n], refs[n:2 * n]
        send_sems, recv_sems = refs[2 * n:]
        mx, my, mc = lax.axis_index("x"), lax.axis_index("y"), lax.axis_index("c")
        copies = []
        for i in range(n):
            if views[i] == "chip":
                src = ins[i].at[:, _half(gs[i].shape[1], 1 - mc)]
            else:
                src = ins[i].at[_half(gs[i].shape[0], 1 - mc)]
            cp = pltpu.make_async_remote_copy(src_ref=src, dst_ref=outs[i], send_sem=send_sems.at[i], recv_sem=recv_sems.at[i],
                                              device_id=(mx, my, 1 - mc), device_id_type=_MESH)
            cp.start()
            copies.append(cp)
        for cp in copies:
            cp.wait()

    return pl.pallas_call(
        body, out_shape=[recv_shape(g, v) for g, v in zip(gs, views)], in_specs=[_ANY] * n, out_specs=[_ANY] * n,
        scratch_shapes=[pltpu.SemaphoreType.DMA((n,)), pltpu.SemaphoreType.DMA((n,))], name=name)(*gs)


def _grads_to_chips(pairs, views, *, name):
    n = len(pairs)

    def quad_shape(p, view):
        if view == "chip":
            return jax.ShapeDtypeStruct(p.shape, p.dtype)
        return jax.ShapeDtypeStruct((N_CHIPS, p.shape[0], p.shape[1] // N_CHIPS), p.dtype)

    def body(*refs):
        ins, outs = refs[:n], refs[n:2 * n]
        send_sems, recv_sems = refs[2 * n:]
        mx, my, mc = lax.axis_index("x"), lax.axis_index("y"), lax.axis_index("c")
        me = 2 * mx + my
        chips = [(1 - mx, my), (mx, 1 - my), (1 - mx, 1 - my)]

        def blk(i, j):
            if views[i] == "chip":
                return ins[i].at[j]
            c = pairs[i].shape[1] // N_CHIPS
            return ins[i].at[:, pl.ds(pl.multiple_of(j * c, c), c)]

        copies = []
        for i in range(n):
            for k, (px, py) in enumerate(chips):
                cp = pltpu.make_async_remote_copy(src_ref=blk(i, 2 * px + py), dst_ref=outs[i].at[me], send_sem=send_sems.at[3 * i + k],
                                                  recv_sem=recv_sems.at[3 * i + k], device_id=(px, py, mc), device_id_type=_MESH)
                cp.start()
                copies.append(cp)
        for cp in copies:
            cp.wait()

    return pl.pallas_call(
        body, out_shape=[quad_shape(p, v) for p, v in zip(pairs, views)], in_specs=[_ANY] * n, out_specs=[_ANY] * n,
        scratch_shapes=[pltpu.SemaphoreType.DMA((3 * n,)), pltpu.SemaphoreType.DMA((3 * n,))], name=name)(*pairs)


_HBM = pl.BlockSpec(memory_space=pltpu.HBM)
_SEM = pl.BlockSpec(memory_space=pltpu.SEMAPHORE)
_EFFECT = pltpu.SideEffectType.DATAFLOW_SIDE_EFFECTING


def _chip_exchange_copies(pair_refs, land_refs, pairs, views, send_sems, recv_sems):
    mx, my, mc = lax.axis_index("x"), lax.axis_index("y"), lax.axis_index("c")
    me = 2 * mx + my
    chips = [(1 - mx, my), (mx, 1 - my), (1 - mx, 1 - my)]
    copies = []
    for i in range(len(pairs)):
        for k, (px, py) in enumerate(chips):
            j = 2 * px + py
            if views[i] == "chip":
                src = pair_refs[i].at[j]
            else:
                c = pairs[i].shape[1] // N_CHIPS
                src = pair_refs[i].at[:, pl.ds(pl.multiple_of(j * c, c), c)]
            copies.append(pltpu.make_async_remote_copy(
                src_ref=src, dst_ref=land_refs[i].at[me], send_sem=send_sems.at[3 * i + k], recv_sem=recv_sems.at[3 * i + k],
                device_id=(px, py, mc), device_id_type=_MESH))
    return copies


def _quad_shape(p, view):
    return p.shape if view == "chip" else (N_CHIPS, p.shape[0], p.shape[1] // N_CHIPS)


def _grads_to_chips_start(pairs, views, *, name):
    n = len(pairs)
    lands = [pltpu.with_memory_space_constraint(lax.empty(_quad_shape(p, v), p.dtype), pltpu.HBM) for p, v in zip(pairs, views)]

    def body(*refs):
        pair_refs, land_refs = refs[:n], refs[n:2 * n]
        send_sems, recv_sems = refs[2 * n], refs[2 * n + 1]
        token = refs[-1]
        for cp in _chip_exchange_copies(pair_refs, land_refs, pairs, views, send_sems, recv_sems):
            cp.start()
        token[...] = jnp.zeros_like(token)

    outs = pl.pallas_call(
        body, name=name,
        out_shape=(pltpu.SemaphoreType.DMA((3 * n,)), pltpu.SemaphoreType.DMA((3 * n,)),
                   *[pltpu.HBM(p.shape, p.dtype) for p in pairs], *[pltpu.HBM(l.shape, l.dtype) for l in lands],
                   jax.ShapeDtypeStruct((SUBLANE, LANE), F32)),
        in_specs=[_HBM] * (2 * n), out_specs=(_SEM, _SEM, *[_HBM] * (2 * n), pl.BlockSpec(memory_space=pltpu.VMEM)),
        input_output_aliases={i: 2 + i for i in range(2 * n)},
        compiler_params=pltpu.CompilerParams(has_side_effects=_EFFECT),
    )(*[pltpu.with_memory_space_constraint(p, pltpu.HBM) for p in pairs], *lands)
    return outs[0], outs[1], list(outs[2:2 + n]), list(outs[2 + n:2 + 2 * n]), outs[-1]


def _grads_to_chips_wait(send_sems, recv_sems, pairs, lands, views, after, *, name):
    n = len(pairs)

    def body(*refs):
        pair_refs, land_refs = refs[:n], refs[n:2 * n]
        s_sems, r_sems = refs[2 * n], refs[2 * n + 1]
        for cp in _chip_exchange_copies(pair_refs, land_refs, pairs, views, s_sems, r_sems):
            cp.wait_send()
            cp.wait_recv()

    outs = pl.pallas_call(
        body, name=name, out_shape=tuple(pltpu.HBM(x.shape, x.dtype) for x in list(pairs) + list(lands)),
        in_specs=[_HBM] * (2 * n) + [_SEM, _SEM, _ANY], out_specs=tuple([_HBM] * (2 * n)),
        input_output_aliases={i: i for i in range(2 * n)},
        compiler_params=pltpu.CompilerParams(has_side_effects=_EFFECT),
    )(*pairs, *lands, send_sems, recv_sems, after)
    return list(outs[n:])


def _grads_share(tots, *, name):
    n = len(tots)

    def body(*refs):
        ins, outs = refs[:n], refs[n:2 * n]
        send_sems, recv_sems = refs[2 * n:]
        mx, my, mc = lax.axis_index("x"), lax.axis_index("y"), lax.axis_index("c")
        copies = []
        for i in range(n):
            cp = pltpu.make_async_remote_copy(src_ref=ins[i], dst_ref=outs[i], send_sem=send_sems.at[i], recv_sem=recv_sems.at[i],
                                              device_id=(mx, my, 1 - mc), device_id_type=_MESH)
            cp.start()
            copies.append(cp)
        for cp in copies:
            cp.wait()

    return pl.pallas_call(
        body, out_shape=[jax.ShapeDtypeStruct(t.shape, t.dtype) for t in tots], in_specs=[_ANY] * n, out_specs=[_ANY] * n,
        scratch_shapes=[pltpu.SemaphoreType.DMA((n,)), pltpu.SemaphoreType.DMA((n,))], name=name)(*tots)


def _pair_sum(g, recv, view, c_idx, *, name):
    def body(c_ref, a_ref, b_ref, o_ref):
        o_ref[...] = (a_ref[...] + b_ref[...]).astype(WIRE_DTYPE)

    if view == "chip":
        nch, r, c = g.shape
        tr = _row_tile(r // 2, c * 4, 16)
        gv = g.reshape(nch, 2, r // 2, c)
        grid = (nch, (r // 2) // tr)
        in_specs = [pl.BlockSpec((None, None, tr, c), lambda j, i, c_ref: (j, c_ref[0], i, 0)),
                    pl.BlockSpec((None, tr, c), lambda j, i, c_ref: (j, i, 0))]
        out_spec = pl.BlockSpec((None, tr, c), lambda j, i, c_ref: (j, i, 0))
        sem = ("parallel", "parallel")
    else:
        r, c4 = g.shape
        tr = _row_tile(r // 2, c4 * 4, 16)
        gv = g.reshape(2, r // 2, c4)
        grid = ((r // 2) // tr,)
        in_specs = [pl.BlockSpec((None, tr, c4), lambda i, c_ref: (c_ref[0], i, 0)), pl.BlockSpec((tr, c4), lambda i, c_ref: (i, 0))]
        out_spec = pl.BlockSpec((tr, c4), lambda i, c_ref: (i, 0))
        sem = ("parallel",)
    grid_spec = pltpu.PrefetchScalarGridSpec(num_scalar_prefetch=1, grid=grid, in_specs=in_specs, out_specs=out_spec)
    return pl.pallas_call(body, grid_spec=grid_spec, out_shape=jax.ShapeDtypeStruct(recv.shape, WIRE_DTYPE),
                          compiler_params=_params(*sem), name=name)(c_idx, gv, recv)


def _quad_sum(gs, recvs, quads, view, chip_idx, c_idx, *, name):
    nl = len(quads)
    nch, rh, c = quads[0].shape
    tr = _row_tile(rh, c * 4, 16)

    def body(_, __, *refs):
        o_ref = refs[-1]
        per = nch + 1
        for l in range(nl):
            grp = refs[l * per:(l + 1) * per]
            acc = grp[0][...] + grp[1][...]
            for r in grp[2:]:
                acc = acc + r[...].astype(F32)
            o_ref[l] = acc

    if view == "chip":
        own = [pl.BlockSpec((None, None, tr, c), lambda i, j, h: (j[0], h[0], i, 0)),
               pl.BlockSpec((None, tr, c), lambda i, j, h: (j[0], i, 0))]
        gviews = [g.reshape(nch, 2, rh, c) for g in gs]
    else:
        own = [pl.BlockSpec((None, tr, c), lambda i, j, h: (h[0], i, j[0])), pl.BlockSpec((tr, c), lambda i, j, h: (i, j[0]))]
        gviews = [g.reshape(2, rh, nch * c) for g in gs]
    assert nch & (nch - 1) == 0
    got = [pl.BlockSpec((None, tr, c), functools.partial(lambda i, j, h, k: ((j[0] + k) & (nch - 1), i, 0), k=k))
           for k in range(1, nch)]
    ins = []
    for l in range(nl):
        ins += [gviews[l], recvs[l]] + [quads[l]] * (nch - 1)
    grid_spec = pltpu.PrefetchScalarGridSpec(
        num_scalar_prefetch=2, grid=(rh // tr,), in_specs=(own + got) * nl,
        out_specs=pl.BlockSpec((nl, tr, c), lambda i, j, h: (0, i, 0)))
    return pl.pallas_call(body, grid_spec=grid_spec, out_shape=jax.ShapeDtypeStruct((nl, rh, c), F32),
                          compiler_params=_params("parallel"), name=name)(chip_idx, c_idx, *ins)


def _sum_devices(g8, own, dev_idx, *, name):
    k, rows, cols = g8.shape

    def body(d_ref, a_ref, x_ref, o_ref):
        acc = None
        for i in range(k):
            term = jnp.where(d_ref[0] == i, x_ref[...], a_ref[i])
            acc = term if acc is None else acc + term
        o_ref[...] = acc

    grid_spec = pltpu.PrefetchScalarGridSpec(
        num_scalar_prefetch=1, grid=(1,),
        in_specs=[pl.BlockSpec((k, rows, cols), lambda i, d_ref: (0, 0, 0)), pl.BlockSpec((rows, cols), lambda i, d_ref: (0, 0))],
        out_specs=pl.BlockSpec((rows, cols), lambda i, d_ref: (0, 0)))
    return pl.pallas_call(body, grid_spec=grid_spec, out_shape=jax.ShapeDtypeStruct((rows, cols), g8.dtype),
                          compiler_params=_params("arbitrary"), name=name)(dev_idx, g8, own)


def _adamw(w, g, m, v, *, name):
    rows, cols = w.shape
    tr = rows
    for cand in (256, 128, 64, 32, 16, 8):
        if rows % cand == 0 and cand * cols <= 512 * 1024:
            tr = cand
            break
    c1 = 1.0 - ADAM_B1 ** ADAM_STEP
    c2 = 1.0 - ADAM_B2 ** ADAM_STEP

    def body(w_ref, g_ref, m_ref, v_ref, d_ref, nm_ref, nv_ref):
        gv = g_ref[...]
        nm = ADAM_B1 * m_ref[...] + (1.0 - ADAM_B1) * gv
        nv = ADAM_B2 * v_ref[...] + (1.0 - ADAM_B2) * (gv * gv)
        d_ref[...] = -ADAM_LR * ((nm / c1) / (jnp.sqrt(nv / c2) + ADAM_EPS) + ADAM_WD * w_ref[...])
        nm_ref[...] = nm
        nv_ref[...] = nv

    spec = pl.BlockSpec((tr, cols), lambda i: (i, 0))
    shp = jax.ShapeDtypeStruct((rows, cols), F32)
    return pl.pallas_call(body, grid=(rows // tr,), in_specs=[spec] * 4, out_specs=[spec] * 3, out_shape=[shp] * 3,
                          compiler_params=_params("parallel"), name=name)(w, g, m, v)


def _adamw_halves(w, m, v, mine, other, c_idx, *, name):
    nl, r, c = w.shape
    rh = r // 2
    tr = _row_tile(rh, c * 4)
    c1 = 1.0 - ADAM_B1 ** ADAM_STEP
    c2 = 1.0 - ADAM_B2 ** ADAM_STEP

    def body(c_ref, w_ref, m_ref, v_ref, a_ref, b_ref, g_ref, d_ref, nm_ref, nv_ref):
        gv = jnp.where(pl.program_id(1) == c_ref[0], a_ref[...], b_ref[...])
        nm = ADAM_B1 * m_ref[...] + (1.0 - ADAM_B1) * gv
        nv = ADAM_B2 * v_ref[...] + (1.0 - ADAM_B2) * (gv * gv)
        g_ref[...] = gv
        d_ref[...] = -ADAM_LR * ((nm / c1) / (jnp.sqrt(nv / c2) + ADAM_EPS) + ADAM_WD * w_ref[...])
        nm_ref[...] = nm
        nv_ref[...] = nv

    full = pl.BlockSpec((None, None, tr, c), lambda l, h, i, c_ref: (l, h, i, 0))
    half = pl.BlockSpec((None, tr, c), lambda l, h, i, c_ref: (l, i, 0))
    grid_spec = pltpu.PrefetchScalarGridSpec(num_scalar_prefetch=1, grid=(nl, 2, rh // tr),
                                             in_specs=[full] * 3 + [half] * 2, out_specs=[full] * 4)
    shp = jax.ShapeDtypeStruct((nl, 2, rh, c), F32)
    view = (nl, 2, rh, c)
    outs = pl.pallas_call(body, grid_spec=grid_spec, out_shape=[shp] * 4, compiler_params=_params("parallel", "parallel", "parallel"),
                          name=name)(c_idx, w.reshape(view), m.reshape(view), v.reshape(view), mine, other)
    return [o.reshape(nl, r, c) for o in outs]


WEIGHTS = ["mem_ln_g", "mem_ln_b", "w_in", "sg_ln_g", "sg_ln_b", "sg_w", "sg_b", "conv_w", "conv_b", "dt_bias", "a_log",
           "d_skip", "ssm_norm_g", "p_a", "p_b", "w_mix_o", "w_xq", "w_xkv", "w_xo", "w_ffn_in", "w_ffn_out", "ln_g", "ln_b"]
ARG_NAMES = ["x", "mem"] + WEIGHTS + ["loss_target"] + ["m_" + n for n in WEIGHTS] + ["v_" + n for n in WEIGHTS]
BIG = {"w_in": (1, (1024, 9248)), "p_a": (0, (1024, 1024)), "p_b": (0, (2048, 1024)), "w_mix_o": (0, (1024, 1024)),
       "w_xq": (0, (1024, 1024)), "w_xkv": (1, (1024, 2048)), "w_xo": (0, (1024, 1024)), "w_ffn_in": (1, (1024, 5632)),
       "w_ffn_out": (0, (2816, 1024))}
SMALL_SHARDED = {"conv_w": (4, 3072), "ln_g": (3, 1024), "ln_b": (3, 1024)}
SMALL = [n for n in WEIGHTS if n not in BIG]
XBC_IN0, DT_COL0, DT_COL1 = 4096, 7168, 7200
GATHER_KIND = {"w_in": "chip", "p_a": "row", "p_b": "row", "w_mix_o": "row", "w_xq": "row", "w_xkv": "col", "w_xo": "row",
               "w_ffn_in": "col", "w_ffn_out": "row", "conv_w": "chip", "ln_g": "chip", "ln_b": "chip"}
GRAD_VIEW = {n: ("col" if k == "col" else "chip") for n, k in GATHER_KIND.items() if n in BIG}


def _shard_shape(name):
    axis, (r, c) = BIG[name]
    return (r // N_CHIPS, c) if axis == 0 else (r, c // N_CHIPS)


def _pad_rows(flat, cols, row_mult):
    n = flat.shape[0]
    rows = -(-n // cols)
    rows = -(-rows // row_mult) * row_mult
    return jnp.pad(flat, (0, rows * cols - n)).reshape(rows, cols)


def _gather_weights(a, chip):
    names = list(BIG) + list(SMALL_SHARDED)
    kinds = [GATHER_KIND[n] for n in names]
    cpre = chip.reshape(1)
    bufs = [_cast_place(a[n], GATHER_KIND[n], MXU_DTYPE if n in BIG else F32, cpre, name=f"place_{n}") for n in names]
    outs = _gather_params(bufs, [a[n].shape[1:] for n in names], kinds, name="gather_weights")
    full = dict(zip(names, outs))
    for n in names:
        if GATHER_KIND[n] == "chip":
            _, _, r, c = full[n].shape
            full[n] = jnp.transpose(full[n], (0, 2, 1, 3)).reshape(DEPTH, r, N_CHIPS * c)
    return full


def _layer_weights(a, full, l):
    w_in = full["w_in"][l]
    w = {n: (full[n], l) for n in BIG if n != "w_in"}
    w["w_main"] = jnp.concatenate([w_in[:, :XBC_IN0], w_in[:, DT_COL1:], w_in[:, XBC_IN0:DT_COL0]], axis=1)
    w["w_dt"] = jnp.pad(w_in[:, DT_COL0:DT_COL1], ((0, 0), (0, HEAD_PAD - SSM_HEADS)))
    for n in SMALL_SHARDED:
        w[n] = full[n][l]
    for n in ["sg_ln_g", "sg_ln_b", "sg_w", "conv_b", "ssm_norm_g"]:
        w[n] = a[n][l]
    w["sg_bcol"] = a["sg_b"][l][..., None]
    for n in ["dt_bias", "a_log"]:
        w[n + "8"] = _pad_heads(a[n][l])
    w["d_skipx"] = _expand_heads(a["d_skip"][l])
    return w


GRAD_VIEWS = [GRAD_VIEW[n] for n in BIG]


def _pair_sums(grads, c_idx, *, tag):
    gs = []
    for n in BIG:
        axis, _ = BIG[n]
        r, c = _shard_shape(n)
        if n == "w_in":
            gm, gd = grads["w_main"], grads["w_dt"]
            gfull = jnp.concatenate([gm[:, :XBC_IN0], gm[:, XBC_COL0:], gd[:, :SSM_HEADS], gm[:, GAB_COL0:XBC_COL0]], axis=1)
            gs.append(jnp.transpose(gfull.reshape(r, N_CHIPS, c), (1, 0, 2)))
        elif axis == 0:
            gs.append(grads[n].reshape(N_CHIPS, r, c))
        else:
            gs.append(grads[n])
    recv = _grads_to_sibling(gs, GRAD_VIEWS, name=f"grads_to_sibling_{tag}")
    cpre = c_idx.reshape(1)
    pairs = [_pair_sum(g, rv, v, cpre, name=f"grads_pair_sum_{n}_{tag}") for g, rv, v, n in zip(gs, recv, GRAD_VIEWS, BIG)]
    return gs, recv, pairs


def _finish_big_grads(parts, quads, c_idx, chip):
    tots = [_quad_sum([parts[l][0][i] for l in range(DEPTH)], [parts[l][1][i] for l in range(DEPTH)],
                      [quads[l][i] for l in range(DEPTH)], GRAD_VIEWS[i], chip.reshape(1), c_idx.reshape(1),
                      name=f"grads_chip_sum_{n}") for i, n in enumerate(BIG)]
    others = _grads_share(tots, name="grads_share")
    return {n: (t, o) for n, t, o in zip(BIG, tots, others)}


def _reduce_small_grads(small, chip, c_idx):
    names = list(small)
    flat = jnp.concatenate([small[n].reshape(-1) for n in names])
    packed = _pad_rows(flat, LANE, SUBLANE)
    g8 = _all_gather8(packed, name="gather_small_grads")
    tot = _sum_devices(g8, packed, (2 * chip + c_idx).reshape(1), name="small_grads_sum").reshape(-1)
    out, off = {}, 0
    for n in names:
        sz = small[n].size
        full = tot[off:off + sz].reshape(small[n].shape)
        off += sz
        if n in SMALL_SHARDED:
            cs = SMALL_SHARDED[n][1] // N_CHIPS
            full = lax.dynamic_slice_in_dim(full, chip * cs, cs, axis=-1)
        out[n] = full
    return out


def kernel(x, mem, mem_ln_g, mem_ln_b, w_in, sg_ln_g, sg_ln_b, sg_w, sg_b, conv_w, conv_b, dt_bias, a_log, d_skip, ssm_norm_g, p_a, p_b, w_mix_o, w_xq, w_xkv, w_xo, w_ffn_in, w_ffn_out, ln_g, ln_b, loss_target, m_mem_ln_g, m_mem_ln_b, m_w_in, m_sg_ln_g, m_sg_ln_b, m_sg_w, m_sg_b, m_conv_w, m_conv_b, m_dt_bias, m_a_log, m_d_skip, m_ssm_norm_g, m_p_a, m_p_b, m_w_mix_o, m_w_xq, m_w_xkv, m_w_xo, m_w_ffn_in, m_w_ffn_out, m_ln_g, m_ln_b, v_mem_ln_g, v_mem_ln_b, v_w_in, v_sg_ln_g, v_sg_ln_b, v_sg_w, v_sg_b, v_conv_w, v_conv_b, v_dt_bias, v_a_log, v_d_skip, v_ssm_norm_g, v_p_a, v_p_b, v_w_mix_o, v_w_xq, v_w_xkv, v_w_xo, v_w_ffn_in, v_w_ffn_out, v_ln_g, v_ln_b):
    a = dict(zip(ARG_NAMES, (x, mem, mem_ln_g, mem_ln_b, w_in, sg_ln_g, sg_ln_b, sg_w, sg_b, conv_w, conv_b, dt_bias, a_log, d_skip, ssm_norm_g, p_a, p_b, w_mix_o, w_xq, w_xkv, w_xo, w_ffn_in, w_ffn_out, ln_g, ln_b, loss_target, m_mem_ln_g, m_mem_ln_b, m_w_in, m_sg_ln_g, m_sg_ln_b, m_sg_w, m_sg_b, m_conv_w, m_conv_b, m_dt_bias, m_a_log, m_d_skip, m_ssm_norm_g, m_p_a, m_p_b, m_w_mix_o, m_w_xq, m_w_xkv, m_w_xo, m_w_ffn_in, m_w_ffn_out, m_ln_g, m_ln_b, v_mem_ln_g, v_mem_ln_b, v_w_in, v_sg_ln_g, v_sg_ln_b, v_sg_w, v_sg_b, v_conv_w, v_conv_b, v_dt_bias, v_a_log, v_d_skip, v_ssm_norm_g, v_p_a, v_p_b, v_w_mix_o, v_w_xq, v_w_xkv, v_w_xo, v_w_ffn_in, v_w_ffn_out, v_ln_g, v_ln_b)))
    c_idx = lax.axis_index("c").astype(jnp.int32)
    chip = (2 * lax.axis_index("x") + lax.axis_index("y")).astype(jnp.int32)

    full = _gather_weights(a, chip)
    layers = [_layer_weights(a, full, l) for l in range(DEPTH)]
    parts, flight = [None] * DEPTH, {}

    def start_exchange(l, grads_l):
        parts[l] = _pair_sums(grads_l, c_idx, tag=f"l{l}")
        if l == 0:
            return None
        send_sems, recv_sems, pairs, lands, token = _grads_to_chips_start(parts[l][2], GRAD_VIEWS, name=f"grads_to_chips_start_l{l}")
        flight[l] = (send_sems, recv_sems, pairs, lands)
        return token

    lsum, grad_x, grads, d_mem_g, d_mem_b = _local_step(x, mem, loss_target, mem_ln_g, mem_ln_b, layers, start_exchange)
    loss = lax.psum(0.5 * jnp.sum(lsum) / D_MODEL, ("x", "y", "c"))

    quads = [None] * DEPTH
    for l, (send_sems, recv_sems, pairs, lands) in flight.items():
        quads[l] = _grads_to_chips_wait(send_sems, recv_sems, pairs, lands, GRAD_VIEWS, grad_x, name=f"grads_to_chips_wait_l{l}")
    quads[0] = _grads_to_chips(parts[0][2], GRAD_VIEWS, name="grads_to_chips_l0")
    halves = _finish_big_grads(parts, quads, c_idx, chip)
    gw = {}
    small = {"mem_ln_g": d_mem_g, "mem_ln_b": d_mem_b}
    for n in SMALL:
        if n in small:
            continue
        per_layer = []
        for l in range(DEPTH):
            g = grads[l][n]
            if n in ("dt_bias", "a_log", "d_skip"):
                g = g[0, :SSM_HEADS]
            per_layer.append(g.reshape(a[n].shape[1:-1] + (-1,)))
        small[n] = jnp.stack(per_layer)
    gw.update(_reduce_small_grads(small, chip, c_idx))

    delta, new_m, new_v = {}, {}, {}
    for n in BIG:
        mine, other = halves[n]
        gw[n], delta[n], new_m[n], new_v[n] = _adamw_halves(a[n], a["m_" + n], a["v_" + n], mine, other, c_idx.reshape(1),
                                                             name=f"adamw_{n}")
    packs = [_pad_rows(jnp.concatenate([src(n).reshape(-1) for n in SMALL]), LANE, SUBLANE)
             for src in (lambda n: a[n], lambda n: gw[n], lambda n: a["m_" + n], lambda n: a["v_" + n])]
    outs = _adamw(*packs, name="adamw_small")
    off = 0
    for n in SMALL:
        sz, shp = a[n].size, a[n].shape
        delta[n], new_m[n], new_v[n] = (o.reshape(-1)[off:off + sz].reshape(shp) for o in outs)
        off += sz
    return (loss, grad_x, *[gw[n].reshape(a[n].shape) for n in WEIGHTS], *[delta[n] for n in WEIGHTS],
            *[new_m[n] for n in WEIGHTS], *[new_v[n] for n in WEIGHTS])
```

```python
import functools
import math

import jax
import jax.numpy as jnp
from jax import lax
from jax.experimental import pallas as pl
from jax.experimental.pallas import tpu as pltpu

F32 = jnp.float32
MXU_DTYPE = jnp.bfloat16
WIRE_DTYPE = jnp.bfloat16

D_MODEL = 1024
DEPTH = 2
CHUNK = 128
SG_GROUPS = 8
SSM_INNER = 2048
SSM_HEADDIM = 64
SSM_HEADS = 32
SSM_STATE = 128
SSM_GROUPS = 4
SSM_CONV = 4
SSM_CONV_DIM = 3072
X_HEADS = 4
X_HEADDIM = 256
FFN_HIDDEN = 2816
ALPHA = float((2 * DEPTH) ** 0.25)
LN_EPS = 1e-5
RMS_EPS = 1e-5
ADAM_LR = 0.001
ADAM_B1 = 0.9
ADAM_B2 = 0.999
ADAM_EPS = 1e-08
ADAM_WD = 0.01
ADAM_STEP = 10

MAIN_COLS = 9216
UVZ_COLS = 4096
GAB_COL0 = 4096
XBC_COL0 = 6144
HEAD_PAD = 128

VMEM_LIMIT = 56 * 1024 * 1024
BLOCK_BYTES = 2 * 1024 * 1024
ROW_TILES = (512, 256, 128)
LANE = 128
SUBLANE = 8

N_CHIPS = 4
N_DEV = 8


def _pick(n, cands):
    for c in cands:
        if n % c == 0:
            return c
    return n


MM_TILE_MAX = 1408
MM_OPERAND_BYTES = 8 * 1024 * 1024


def _div_tile(n, limit):
    best = None
    for t in range(LANE, min(n, limit) + 1, LANE):
        if n % t == 0:
            best = t
    return n if best is None else best


def _params(*sem):
    return pltpu.CompilerParams(dimension_semantics=tuple(sem), vmem_limit_bytes=VMEM_LIMIT)


_ANY = pl.BlockSpec(memory_space=pl.ANY)
_MESH = pl.DeviceIdType.MESH


def _nt(a, b):
    return lax.dot_general(a, b, (((1,), (1,)), ((), ())), preferred_element_type=F32)


def _tn(a, b):
    return lax.dot_general(a, b, (((0,), (0,)), ((), ())), preferred_element_type=F32)


def _nn(a, b):
    return jnp.dot(a, b, preferred_element_type=F32)


def _sigmoid(x):
    return 0.5 * jnp.tanh(0.5 * x) + 0.5


def _split3(v):
    def top(x):
        bits = lax.bitcast_convert_type(x, jnp.uint32) & jnp.uint32(0xFFFF0000)
        return lax.bitcast_convert_type(bits, F32)

    v1 = top(v)
    r1 = v - v1
    v2 = top(r1)
    v3 = r1 - v2
    return v1.astype(jnp.bfloat16), v2.astype(jnp.bfloat16), v3.astype(jnp.bfloat16)


def _dot_exact(a, b, dn, data):
    if data == 0:
        mat = b.astype(jnp.bfloat16)
        return sum(lax.dot_general(p, mat, dn, preferred_element_type=F32) for p in _split3(a))
    mat = a.astype(jnp.bfloat16)
    return sum(lax.dot_general(mat, p, dn, preferred_element_type=F32) for p in _split3(b))


_DN_NN = (((1,), (0,)), ((), ()))
_DN_TN = (((0,), (0,)), ((), ()))


def _gelu(x):
    return 0.5 * x * (1.0 + lax.erf(x * (2.0 ** -0.5)))


def _gelu_grad(x):
    return 0.5 * (1.0 + lax.erf(x * (2.0 ** -0.5))) + x * jnp.exp(-0.5 * x * x) * (1.0 / math.sqrt(2.0 * math.pi))


def _mm(a, b, *, ta=False, tb=False, out_dtype=F32, name):
    b, bl = b if isinstance(b, tuple) else (b, None)
    if ta:
        kdim, m = a.shape
    else:
        m, kdim = a.shape
    if tb:
        n, k2 = b.shape[-2:]
    else:
        k2, n = b.shape[-2:]
    assert kdim == k2, (a.shape, b.shape, ta, tb)
    tm = _div_tile(m, MM_TILE_MAX)
    tn = _div_tile(n, MM_TILE_MAX)
    tk = _div_tile(kdim, MM_OPERAND_BYTES // (tm * a.dtype.itemsize + tn * b.dtype.itemsize))
    nk = kdim // tk
    dn = (((0 if ta else 1,), (1 if tb else 0,)), ((), ()))

    def body(a_ref, b_ref, o_ref, *scratch):
        d = lax.dot_general(a_ref[...].astype(MXU_DTYPE), b_ref[...].astype(MXU_DTYPE), dn, preferred_element_type=F32)
        if nk == 1:
            o_ref[...] = d.astype(out_dtype)
            return
        acc_ref, = scratch
        k = pl.program_id(2)

        @pl.when(k == 0)
        def _():
            acc_ref[...] = d

        @pl.when(jnp.logical_and(k > 0, k < nk - 1))
        def _():
            acc_ref[...] += d

        @pl.when(k == nk - 1)
        def _():
            o_ref[...] = (acc_ref[...] + d).astype(out_dtype)

    a_spec = pl.BlockSpec((tk, tm), lambda i, j, k: (k, i)) if ta else pl.BlockSpec((tm, tk), lambda i, j, k: (i, k))
    if bl is None:
        b_spec = pl.BlockSpec((tn, tk), lambda i, j, k: (j, k)) if tb else pl.BlockSpec((tk, tn), lambda i, j, k: (k, j))
    elif tb:
        b_spec = pl.BlockSpec((None, tn, tk), lambda i, j, k: (bl, j, k))
    else:
        b_spec = pl.BlockSpec((None, tk, tn), lambda i, j, k: (bl, k, j))
    return pl.pallas_call(
        body, grid=(m // tm, n // tn, nk), in_specs=[a_spec, b_spec],
        out_specs=pl.BlockSpec((tm, tn), lambda i, j, k: (i, j)),
        out_shape=jax.ShapeDtypeStruct((m, n), out_dtype),
        scratch_shapes=[pltpu.VMEM((tm, tn), F32)] if nk > 1 else [],
        compiler_params=_params("parallel", "parallel", "arbitrary"), name=name)(a, b)


def _row_spec(tm, c, col=0):
    return pl.BlockSpec((tm, c), lambda i: (i, col))


def _par_spec(shape):
    nd = len(shape)
    return pl.BlockSpec(shape, lambda i: (0,) * nd)


def _ln_fwd(x, f, g, b, *, name):
    t, c = x.shape
    tm = _pick(t, ROW_TILES)
    has_f = f is not None

    def body(*refs):
        if has_f:
            x_ref, f_ref, g_ref, b_ref, y_ref, yb_ref, xh_ref, rs_ref = refs
            r = ALPHA * x_ref[...] + f_ref[...]
        else:
            x_ref, g_ref, b_ref, y_ref, yb_ref, xh_ref, rs_ref = refs
            r = x_ref[...]
        mu = jnp.mean(r, axis=-1, keepdims=True)
        xc = r - mu
        var = jnp.mean(xc * xc, axis=-1, keepdims=True)
        rstd = lax.rsqrt(var + LN_EPS)
        xh = xc * rstd
        y = xh * g_ref[...] + b_ref[...]
        y_ref[...] = y
        yb_ref[...] = y.astype(MXU_DTYPE)
        xh_ref[...] = xh
        rs_ref[...] = jnp.broadcast_to(rstd, rs_ref.shape)

    ins = [x] + ([f] if has_f else []) + [g.reshape(1, c), b.reshape(1, c)]
    in_specs = [_row_spec(tm, c)] * (2 if has_f else 1) + [_par_spec((1, c))] * 2
    return pl.pallas_call(
        body, grid=(t // tm,), in_specs=in_specs,
        out_specs=[_row_spec(tm, c), _row_spec(tm, c), _row_spec(tm, c), _row_spec(tm, LANE)],
        out_shape=[jax.ShapeDtypeStruct((t, c), F32), jax.ShapeDtypeStruct((t, c), MXU_DTYPE),
                   jax.ShapeDtypeStruct((t, c), F32), jax.ShapeDtypeStruct((t, LANE), F32)],
        compiler_params=_params("parallel"), name=name)(*ins)


def _ln_bwd(addends, scales, xh, rs, g, *, name):
    t, c = xh.shape
    tm = _pick(t, ROW_TILES)
    na = len(addends)

    def body(*refs):
        a_refs = refs[:na]
        xh_ref, rs_ref, g_ref, dp_ref, dpb_ref, dg_ref, db_ref = refs[na:]

        @pl.when(pl.program_id(0) == 0)
        def _():
            dg_ref[...] = jnp.zeros_like(dg_ref)
            db_ref[...] = jnp.zeros_like(db_ref)

        dy = None
        for s, r in zip(scales, a_refs):
            term = r[...] if s == 1.0 else s * r[...]
            dy = term if dy is None else dy + term
        xhv = xh_ref[...]
        dxh = dy * g_ref[...]
        m1 = jnp.mean(dxh, axis=-1, keepdims=True)
        m2 = jnp.mean(dxh * xhv, axis=-1, keepdims=True)
        dp = rs_ref[:, 0:1] * (dxh - m1 - xhv * m2)
        dp_ref[...] = dp
        dpb_ref[...] = dp.astype(MXU_DTYPE)
        dg_ref[...] += jnp.sum(dy * xhv, axis=0, keepdims=True)
        db_ref[...] += jnp.sum(dy, axis=0, keepdims=True)

    in_specs = [_row_spec(tm, c)] * (na + 1) + [_row_spec(tm, LANE), _par_spec((1, c))]
    return pl.pallas_call(
        body, grid=(t // tm,), in_specs=in_specs,
        out_specs=[_row_spec(tm, c), _row_spec(tm, c), _par_spec((1, c)), _par_spec((1, c))],
        out_shape=[jax.ShapeDtypeStruct((t, c), F32), jax.ShapeDtypeStruct((t, c), MXU_DTYPE),
                   jax.ShapeDtypeStruct((1, c), F32), jax.ShapeDtypeStruct((1, c), F32)],
        compiler_params=_params("arbitrary"), name=name)(*addends, xh, rs, g.reshape(1, c))


def _add_scaled(addends, scales, *, name):
    t, c = addends[0].shape
    tm = _pick(t, ROW_TILES)
    na = len(addends)

    def body(*refs):
        acc = None
        for s, r in zip(scales, refs[:na]):
            term = r[...] if s == 1.0 else s * r[...]
            acc = term if acc is None else acc + term
        refs[na][...] = acc

    return pl.pallas_call(
        body, grid=(t // tm,), in_specs=[_row_spec(tm, c)] * na, out_specs=_row_spec(tm, c),
        out_shape=jax.ShapeDtypeStruct((t, c), F32), compiler_params=_params("parallel"), name=name)(*addends)


def _loss_head(y, tgt, *, name):
    t, c = y.shape
    tm = _pick(t, ROW_TILES)

    def body(y_ref, t_ref, dy_ref, ls_ref):
        @pl.when(pl.program_id(0) == 0)
        def _():
            ls_ref[...] = jnp.zeros_like(ls_ref)

        e = y_ref[...] - t_ref[...]
        dy_ref[...] = e * (1.0 / c)
        ls_ref[...] += jnp.sum(e * e, axis=0, keepdims=True)

    return pl.pallas_call(
        body, grid=(t // tm,), in_specs=[_row_spec(tm, c)] * 2,
        out_specs=[_row_spec(tm, c), _par_spec((1, c))],
        out_shape=[jax.ShapeDtypeStruct((t, c), F32), jax.ShapeDtypeStruct((1, c), F32)],
        compiler_params=_params("arbitrary"), name=name)(y, tgt)


def _swiglu_fwd(h, *, name):
    t, two_f = h.shape
    fh = two_f // 2
    tm = _pick(t, (256, 128))

    def body(g_ref, u_ref, a_ref):
        g = g_ref[...]
        a_ref[...] = (g * _sigmoid(g) * u_ref[...]).astype(MXU_DTYPE)

    return pl.pallas_call(
        body, grid=(t // tm,), in_specs=[_row_spec(tm, fh, 0), _row_spec(tm, fh, 1)], out_specs=_row_spec(tm, fh),
        out_shape=jax.ShapeDtypeStruct((t, fh), MXU_DTYPE), compiler_params=_params("parallel"), name=name)(h, h)


def _swiglu_bwd(h, da, *, name):
    t, two_f = h.shape
    fh = two_f // 2
    tm = _pick(t, (256, 128))

    def body(g_ref, u_ref, da_ref, dh_ref):
        g = g_ref[...]
        s = _sigmoid(g)
        dav = da_ref[...]
        dh_ref[:, :fh] = (dav * u_ref[...] * (s * (1.0 + g * (1.0 - s)))).astype(MXU_DTYPE)
        dh_ref[:, fh:] = (dav * g * s).astype(MXU_DTYPE)

    return pl.pallas_call(
        body, grid=(t // tm,), in_specs=[_row_spec(tm, fh, 0), _row_spec(tm, fh, 1), _row_spec(tm, fh)],
        out_specs=_row_spec(tm, two_f), out_shape=jax.ShapeDtypeStruct((t, two_f), MXU_DTYPE),
        compiler_params=_params("parallel"), name=name)(h, h, da)


def _attn_probs(q, k):
    s = _nt(q, k) * (X_HEADDIM ** -0.5)
    s = s - jnp.max(s, axis=-1, keepdims=True)
    p = jnp.exp(s)
    return p / jnp.sum(p, axis=-1, keepdims=True)


def _attn_fwd(q, kv, *, bsz, name):
    t = q.shape[0]
    s = t // bsz
    ml = kv.shape[0] // bsz
    hd = X_HEADDIM

    def body(q_ref, k_ref, v_ref, o_ref):
        p = _attn_probs(q_ref[...], k_ref[...])
        o_ref[...] = _nn(p.astype(MXU_DTYPE), v_ref[...]).astype(MXU_DTYPE)

    return pl.pallas_call(
        body, grid=(bsz, X_HEADS),
        in_specs=[pl.BlockSpec((s, hd), lambda b, h: (b, h)), pl.BlockSpec((ml, hd), lambda b, h: (b, h)),
                  pl.BlockSpec((ml, hd), lambda b, h: (b, X_HEADS + h))],
        out_specs=pl.BlockSpec((s, hd), lambda b, h: (b, h)),
        out_shape=jax.ShapeDtypeStruct((t, D_MODEL), MXU_DTYPE),
        compiler_params=_params("parallel", "parallel"), name=name)(q, kv, kv)


def _attn_bwd(q, kv, do, *, bsz, name):
    t = q.shape[0]
    s = t // bsz
    ml = kv.shape[0] // bsz
    hd = X_HEADDIM

    def body(q_ref, k_ref, v_ref, do_ref, dq_ref, dk_ref, dv_ref):
        qv, kk, vv, dov = q_ref[...], k_ref[...], v_ref[...], do_ref[...]
        p = _attn_probs(qv, kk)
        dp = _nt(dov, vv)
        dv_ref[...] = _tn(p.astype(MXU_DTYPE), dov).astype(MXU_DTYPE)
        ds = (p * (dp - jnp.sum(dp * p, axis=-1, keepdims=True)) * (X_HEADDIM ** -0.5)).astype(MXU_DTYPE)
        dq_ref[...] = _nn(ds, kk).astype(MXU_DTYPE)
        dk_ref[...] = _tn(ds, qv).astype(MXU_DTYPE)

    blk_q = pl.BlockSpec((s, hd), lambda b, h: (b, h))
    blk_m = pl.BlockSpec((ml, hd), lambda b, h: (b, h))
    return pl.pallas_call(
        body, grid=(bsz, X_HEADS),
        in_specs=[blk_q, blk_m, pl.BlockSpec((ml, hd), lambda b, h: (b, X_HEADS + h)), blk_q],
        out_specs=[blk_q, blk_m, blk_m],
        out_shape=[jax.ShapeDtypeStruct((t, D_MODEL), MXU_DTYPE), jax.ShapeDtypeStruct((bsz * ml, D_MODEL), MXU_DTYPE),
                   jax.ShapeDtypeStruct((bsz * ml, D_MODEL), MXU_DTYPE)],
        compiler_params=_params("parallel", "parallel"), name=name)(q, kv, kv, do)


def _causal(n):
    row = lax.broadcasted_iota(jnp.int32, (n, n), 0)
    col = lax.broadcasted_iota(jnp.int32, (n, n), 1)
    return row >= col


def _sg_norm(v, g, b):
    gv = _gelu(v)
    mu = jnp.mean(gv, axis=-1, keepdims=True)
    xc = gv - mu
    var = jnp.mean(xc * xc, axis=-1, keepdims=True)
    rstd = lax.rsqrt(var + LN_EPS)
    xh = xc * rstd
    return xh, rstd, xh * g + b


def _sg_fwd(proj, ln_g, ln_b, w, bcol, *, name):
    t = proj.shape[0]
    c = D_MODEL
    gd = c // SG_GROUPS

    def body(u_ref, v_ref, g_ref, b_ref, w_ref, bc_ref, o_ref):
        gu = _gelu(u_ref[...])
        _, _, vn = _sg_norm(v_ref[...], g_ref[...], b_ref[...])
        mask = _causal(CHUNK)
        for g in range(SG_GROUPS):
            sl = slice(g * gd, (g + 1) * gd)
            wg = jnp.where(mask, w_ref[g], 0.0).astype(MXU_DTYPE)
            mixed = _nn(wg, vn[:, sl].astype(MXU_DTYPE)) + bc_ref[g]
            o_ref[:, sl] = (gu[:, sl] * mixed).astype(MXU_DTYPE)

    return pl.pallas_call(
        body, grid=(t // CHUNK,),
        in_specs=[_row_spec(CHUNK, c, 0), _row_spec(CHUNK, c, 1), _par_spec((1, c)), _par_spec((1, c)),
                  _par_spec((SG_GROUPS, CHUNK, CHUNK)), _par_spec((SG_GROUPS, CHUNK, 1))],
        out_specs=_row_spec(CHUNK, c), out_shape=jax.ShapeDtypeStruct((t, c), MXU_DTYPE),
        compiler_params=_params("parallel"), name=name)(proj, proj, ln_g.reshape(1, c), ln_b.reshape(1, c), w, bcol)


def _sg_bwd(proj, dsgo, ln_g, ln_b, w, bcol, dproj, *, name):
    t = proj.shape[0]
    c = D_MODEL
    gd = c // SG_GROUPS

    def body(u_ref, v_ref, d_ref, g_ref, b_ref, w_ref, bc_ref, _, duv_ref, dw_ref, dbc_ref, dg_ref, db_ref, dvn_ref):
        @pl.when(pl.program_id(0) == 0)
        def _():
            dw_ref[...] = jnp.zeros_like(dw_ref)
            dbc_ref[...] = jnp.zeros_like(dbc_ref)
            dg_ref[...] = jnp.zeros_like(dg_ref)
            db_ref[...] = jnp.zeros_like(db_ref)

        u = u_ref[...]
        v = v_ref[...]
        dso = d_ref[...]
        gu = _gelu(u)
        xh, rstd, vn = _sg_norm(v, g_ref[...], b_ref[...])
        mask = _causal(CHUNK)
        for g in range(SG_GROUPS):
            sl = slice(g * gd, (g + 1) * gd)
            wg = jnp.where(mask, w_ref[g], 0.0).astype(MXU_DTYPE)
            vng = vn[:, sl].astype(MXU_DTYPE)
            mixed = _nn(wg, vng) + bc_ref[g]
            duv_ref[:, sl] = (dso[:, sl] * mixed * _gelu_grad(u[:, sl])).astype(MXU_DTYPE)
            dmix = dso[:, sl] * gu[:, sl]
            dmb = dmix.astype(MXU_DTYPE)
            dbc_ref[g] += jnp.sum(dmix, axis=-1, keepdims=True)
            dw_ref[g] += jnp.where(mask, _nt(dmb, vng), 0.0)
            dvn_ref[:, sl] = _tn(wg, dmb)
        dvn = dvn_ref[...]
        dg_ref[...] += jnp.sum(dvn * xh, axis=0, keepdims=True)
        db_ref[...] += jnp.sum(dvn, axis=0, keepdims=True)
        dxh = dvn * g_ref[...]
        m1 = jnp.mean(dxh, axis=-1, keepdims=True)
        m2 = jnp.mean(dxh * xh, axis=-1, keepdims=True)
        dgv = rstd * (dxh - m1 - xh * m2)
        duv_ref[:, c:] = (dgv * _gelu_grad(v)).astype(MXU_DTYPE)

    return pl.pallas_call(
        body, grid=(t // CHUNK,),
        in_specs=[_row_spec(CHUNK, c, 0), _row_spec(CHUNK, c, 1), _row_spec(CHUNK, c), _par_spec((1, c)),
                  _par_spec((1, c)), _par_spec((SG_GROUPS, CHUNK, CHUNK)), _par_spec((SG_GROUPS, CHUNK, 1)), _ANY],
        out_specs=[_row_spec(CHUNK, 2 * c), _par_spec((SG_GROUPS, CHUNK, CHUNK)), _par_spec((SG_GROUPS, CHUNK, 1)),
                   _par_spec((1, c)), _par_spec((1, c))],
        out_shape=[jax.ShapeDtypeStruct(dproj.shape, dproj.dtype), jax.ShapeDtypeStruct((SG_GROUPS, CHUNK, CHUNK), F32),
                   jax.ShapeDtypeStruct((SG_GROUPS, CHUNK, 1), F32), jax.ShapeDtypeStruct((1, c), F32),
                   jax.ShapeDtypeStruct((1, c), F32)],
        scratch_shapes=[pltpu.VMEM((CHUNK, c), F32)], input_output_aliases={7: 0},
        compiler_params=_params("arbitrary"), name=name)(proj, proj, dsgo, ln_g.reshape(1, c), ln_b.reshape(1, c), w, bcol, dproj)


CONV_TC = 512


def _conv_taps(x):
    rows = lax.broadcasted_iota(jnp.int32, x.shape, 0)
    taps = [jnp.where(rows >= SSM_CONV - 1 - k, pltpu.roll(x, SSM_CONV - 1 - k, axis=0), 0.0) for k in range(SSM_CONV - 1)]
    return taps + [x]


def _conv_pre(taps, w_ref, b_ref):
    acc = b_ref[...]
    for k in range(SSM_CONV):
        acc = acc + taps[k] * w_ref[k:k + 1, :]
    return acc


def _conv_fwd(proj, w, b, *, bsz, name):
    t = proj.shape[0]
    s = t // bsz
    nj = SSM_CONV_DIM // CONV_TC
    c0 = XBC_COL0 // CONV_TC

    def body(x_ref, w_ref, b_ref, o_ref):
        pre = _conv_pre(_conv_taps(x_ref[...]), w_ref, b_ref)
        o_ref[...] = pre * _sigmoid(pre)

    return pl.pallas_call(
        body, grid=(bsz, nj),
        in_specs=[pl.BlockSpec((s, CONV_TC), lambda bb, j: (bb, c0 + j)), pl.BlockSpec((SSM_CONV, CONV_TC), lambda bb, j: (0, j)),
                  pl.BlockSpec((1, CONV_TC), lambda bb, j: (0, j))],
        out_specs=pl.BlockSpec((s, CONV_TC), lambda bb, j: (bb, j)),
        out_shape=jax.ShapeDtypeStruct((t, SSM_CONV_DIM), F32),
        compiler_params=_params("parallel", "parallel"), name=name)(proj, w, b.reshape(1, -1))


def _conv_bwd(proj, dact, w, b, dproj, *, bsz, name):
    t = proj.shape[0]
    s = t // bsz
    nj = SSM_CONV_DIM // CONV_TC
    c0 = XBC_COL0 // CONV_TC

    def body(x_ref, d_ref, w_ref, b_ref, _, dx_ref, dw_ref, db_ref):
        @pl.when(pl.program_id(1) == 0)
        def _():
            dw_ref[...] = jnp.zeros_like(dw_ref)
            db_ref[...] = jnp.zeros_like(db_ref)

        taps = _conv_taps(x_ref[...])
        pre = _conv_pre(taps, w_ref, b_ref)
        sg = _sigmoid(pre)
        dpre = d_ref[...] * (sg * (1.0 + pre * (1.0 - sg)))
        rows = lax.broadcasted_iota(jnp.int32, dpre.shape, 0)
        db_ref[...] += jnp.sum(dpre, axis=0, keepdims=True)
        dx = dpre * w_ref[SSM_CONV - 1:SSM_CONV, :]
        for k in range(SSM_CONV):
            dw_ref[k:k + 1, :] += jnp.sum(dpre * taps[k], axis=0, keepdims=True)
        for k in range(SSM_CONV - 1):
            sh = SSM_CONV - 1 - k
            dsh = jnp.where(rows < s - sh, pltpu.roll(dpre, s - sh, axis=0), 0.0)
            dx = dx + dsh * w_ref[k:k + 1, :]
        dx_ref[...] = dx.astype(MXU_DTYPE)

    return pl.pallas_call(
        body, grid=(nj, bsz),
        in_specs=[pl.BlockSpec((s, CONV_TC), lambda j, bb: (bb, c0 + j)), pl.BlockSpec((s, CONV_TC), lambda j, bb: (bb, j)),
                  pl.BlockSpec((SSM_CONV, CONV_TC), lambda j, bb: (0, j)), pl.BlockSpec((1, CONV_TC), lambda j, bb: (0, j)), _ANY],
        out_specs=[pl.BlockSpec((s, CONV_TC), lambda j, bb: (bb, c0 + j)), pl.BlockSpec((SSM_CONV, CONV_TC), lambda j, bb: (0, j)),
                   pl.BlockSpec((1, CONV_TC), lambda j, bb: (0, j))],
        out_shape=[jax.ShapeDtypeStruct(dproj.shape, dproj.dtype), jax.ShapeDtypeStruct((SSM_CONV, SSM_CONV_DIM), F32),
                   jax.ShapeDtypeStruct((1, SSM_CONV_DIM), F32)],
        input_output_aliases={4: 0},
        compiler_params=_params("parallel", "arbitrary"), name=name)(proj, dact, w, b.reshape(1, -1), dproj)


def _softplus(x):
    return jnp.maximum(x, 0.0) + jnp.log1p(jnp.exp(-jnp.abs(x)))


def _pad_heads(v):
    return jnp.broadcast_to(jnp.pad(v.astype(F32), (0, HEAD_PAD - SSM_HEADS))[None, :], (SUBLANE, HEAD_PAD))


def _ssd_prep(dt_raw, dt_bias8, a_log8, *, name):
    t = dt_raw.shape[0]
    n = CHUNK

    def body(r_ref, b_ref, al_ref, dt_ref, cs_ref, dtt_ref, cst_ref):
        dt = _softplus(r_ref[...] + b_ref[0:1, :])
        da = dt * (-jnp.exp(al_ref[0:1, :]))
        row = lax.broadcasted_iota(jnp.int32, (n, n), 0)
        col = lax.broadcasted_iota(jnp.int32, (n, n), 1)
        lower = (col <= row).astype(F32)
        upper = (row <= col).astype(F32)
        eye = (row == col).astype(F32)
        dt_ref[...] = dt
        cs_ref[...] = _dot_exact(lower, da, _DN_NN, 1)
        cst_ref[0] = _dot_exact(da, upper, _DN_TN, 0)
        dtt_ref[0] = _dot_exact(dt, eye, _DN_TN, 0)

    hp = HEAD_PAD
    return pl.pallas_call(
        body, grid=(t // n,),
        in_specs=[_row_spec(n, hp), _par_spec((SUBLANE, hp)), _par_spec((SUBLANE, hp))],
        out_specs=[_row_spec(n, hp), _row_spec(n, hp), pl.BlockSpec((1, hp, n), lambda i: (i, 0, 0)),
                   pl.BlockSpec((1, hp, n), lambda i: (i, 0, 0))],
        out_shape=[jax.ShapeDtypeStruct((t, hp), F32), jax.ShapeDtypeStruct((t, hp), F32),
                   jax.ShapeDtypeStruct((t // n, hp, n), F32), jax.ShapeDtypeStruct((t // n, hp, n), F32)],
        compiler_params=_params("parallel"), name=name)(dt_raw, dt_bias8, a_log8)


def _expand_mat():
    h = lax.broadcasted_iota(jnp.int32, (HEAD_PAD, SSM_INNER), 0)
    ch = lax.broadcasted_iota(jnp.int32, (HEAD_PAD, SSM_INNER), 1)
    return (ch // SSM_HEADDIM == h).astype(F32)


def _reduce_mat():
    ch = lax.broadcasted_iota(jnp.int32, (SSM_INNER, HEAD_PAD), 0)
    h = lax.broadcasted_iota(jnp.int32, (SSM_INNER, HEAD_PAD), 1)
    return (ch // SSM_HEADDIM == h).astype(F32)


def _expand(v, em):
    return _dot_exact(v, em, _DN_NN, 0)


def _expand_heads(v):
    return jnp.repeat(v.astype(F32), SSM_HEADDIM)[None, :]


def _decay_mat(cs_ref, cst_ref, h, mask):
    seg = cs_ref[:, h:h + 1] - cst_ref[0, h:h + 1, :]
    return jnp.where(mask, jnp.exp(jnp.minimum(seg, 0.0)), 0.0)


GROUP_CH = SSM_INNER // SSM_GROUPS
PAIRS_PER_GROUP = GROUP_CH // LANE
HEADS_PER_GROUP = SSM_HEADS // SSM_GROUPS
BM_COL0 = SSM_INNER
CM_COL0 = SSM_INNER + SSM_GROUPS * SSM_STATE


def _ssd_specs(nc, rev):
    def cidx(i):
        return (i // nc) * nc + (nc - 1 - i % nc) if rev else i

    n = CHUNK
    xs = pl.BlockSpec((n, SSM_INNER), lambda i: (cidx(i), 0))
    bm = pl.BlockSpec((n, GROUP_CH), lambda i: (cidx(i), BM_COL0 // GROUP_CH))
    cm = pl.BlockSpec((n, GROUP_CH), lambda i: (cidx(i), CM_COL0 // GROUP_CH))
    hv = pl.BlockSpec((n, HEAD_PAD), lambda i: (cidx(i), 0))
    hvt = pl.BlockSpec((1, HEAD_PAD, n), lambda i: (cidx(i), 0, 0))
    st = pl.BlockSpec((1, SSM_INNER, SSM_STATE), lambda i: (cidx(i), 0, 0))
    return xs, bm, cm, hv, hvt, st


def _ssd_fwd(xbc, dt, cs, dtt, cst, dskx, *, nc, name):
    t = xbc.shape[0]
    n = CHUNK
    xs_s, bm_s, cm_s, hv_s, hvt_s, st_s = _ssd_specs(nc, False)

    def body(xs_ref, bm_ref, cm_ref, dt_ref, cs_ref, dtt_ref, cst_ref, dsk_ref, y_ref, st_ref, prev):
        @pl.when(pl.program_id(0) % nc == 0)
        def _():
            prev[...] = jnp.zeros_like(prev)

        st_ref[0] = prev[...]
        em = _expand_mat()
        dtx = _expand(dt_ref[...], em)
        csx = _expand(cs_ref[...], em)
        dskx = dsk_ref[...]
        xs = xs_ref[...]
        xdt = xs * dtx
        ecs = jnp.exp(csx)
        dec = jnp.exp(csx[n - 1:n, :] - csx)
        mask = _causal(n)
        lane = lax.broadcasted_iota(jnp.int32, (n, LANE), 1)
        for g in range(SSM_GROUPS):
            gs = slice(g * SSM_STATE, (g + 1) * SSM_STATE)
            gc = slice(g * GROUP_CH, (g + 1) * GROUP_CH)
            cmat = cm_ref[:, gs].astype(MXU_DTYPE)
            bmat = bm_ref[:, gs].astype(MXU_DTYPE)
            cb = _nt(cmat, bmat)
            yoff = ecs[:, gc] * _nt(cmat, prev[gc, :].astype(MXU_DTYPE))
            for q in range(PAIRS_PER_GROUP):
                hp = g * PAIRS_PER_GROUP + q
                sl = slice(hp * LANE, (hp + 1) * LANE)
                xp = xdt[:, sl].astype(MXU_DTYPE)
                m0 = (cb * _decay_mat(cs_ref, cst_ref, 2 * hp, mask)).astype(MXU_DTYPE)
                m1 = (cb * _decay_mat(cs_ref, cst_ref, 2 * hp + 1, mask)).astype(MXU_DTYPE)
                yd = jnp.where(lane < SSM_HEADDIM, _nn(m0, xp), _nn(m1, xp))
                y_ref[:, sl] = yd + yoff[:, q * LANE:(q + 1) * LANE] + xs[:, sl] * dskx[:, sl]
            snew = _tn((xdt[:, gc] * dec[:, gc]).astype(MXU_DTYPE), bmat)
            for r in range(HEADS_PER_GROUP):
                h = g * HEADS_PER_GROUP + r
                rows = slice(h * SSM_HEADDIM, (h + 1) * SSM_HEADDIM)
                e = jnp.exp(cst_ref[0, h:h + 1, n - 1:n])
                prev[rows, :] = prev[rows, :] * e + snew[r * SSM_HEADDIM:(r + 1) * SSM_HEADDIM, :]

    return pl.pallas_call(
        body, grid=(t // n,),
        in_specs=[xs_s, bm_s, cm_s, hv_s, hv_s, hvt_s, hvt_s, _par_spec((1, SSM_INNER))],
        out_specs=[xs_s, st_s],
        out_shape=[jax.ShapeDtypeStruct((t, SSM_INNER), F32), jax.ShapeDtypeStruct((t // n, SSM_INNER, SSM_STATE), F32)],
        scratch_shapes=[pltpu.VMEM((SSM_INNER, SSM_STATE), F32)],
        compiler_params=_params("arbitrary"), name=name)(xbc, xbc, xbc, dt, cs, dtt, cst, dskx)


def _ssd_bwd(dy, xbc, dt, cs, dtt, cst, st, dskx, a_log8, dt_raw, dt_bias8, *, nc, name):
    t = xbc.shape[0]
    n = CHUNK
    xs_s, bm_s, cm_s, hv_s, hvt_s, st_s = _ssd_specs(nc, True)
    acc_s = _par_spec((1, HEAD_PAD))
    xbc_s = pl.BlockSpec((n, SSM_CONV_DIM), xs_s.index_map)

    def body(dy_ref, xs_ref, bm_ref, cm_ref, dt_ref, cs_ref, dtt_ref, cst_ref, st_ref, dsk_ref, al_ref, raw_ref, bias_ref,
             dxbc_ref, ddr_ref, dal_ref, dds_ref, dbias_ref, dprev, dxdt_s, tdec_s, tcs_s):
        @pl.when(pl.program_id(0) % nc == 0)
        def _():
            dprev[...] = jnp.zeros_like(dprev)

        @pl.when(pl.program_id(0) == 0)
        def _():
            dal_ref[...] = jnp.zeros_like(dal_ref)
            dds_ref[...] = jnp.zeros_like(dds_ref)
            dbias_ref[...] = jnp.zeros_like(dbias_ref)

        em = _expand_mat()
        rm = _reduce_mat()

        def head_reduce(v):
            return _dot_exact(v, rm, _DN_NN, 0)

        dtv = dt_ref[...]
        csv = cs_ref[...]
        dtx = _expand(dtv, em)
        csx = _expand(csv, em)
        dskx = dsk_ref[...]
        xs = xs_ref[...]
        dyv = dy_ref[...]
        xdt = xs * dtx
        ecs = jnp.exp(csx)
        dec = jnp.exp(csx[n - 1:n, :] - csx)
        mask = _causal(n)
        lane = lax.broadcasted_iota(jnp.int32, (n, LANE), 1)
        hlane = lax.broadcasted_iota(jnp.int32, (1, HEAD_PAD), 1)
        hsub = lax.broadcasted_iota(jnp.int32, (HEAD_PAD, 1), 0)
        rsum = jnp.zeros((n, HEAD_PAD), F32)
        csum = jnp.zeros((HEAD_PAD, n), F32)
        for g in range(SSM_GROUPS):
            gs = slice(g * SSM_STATE, (g + 1) * SSM_STATE)
            gc = slice(g * GROUP_CH, (g + 1) * GROUP_CH)
            cmat = cm_ref[:, gs].astype(MXU_DTYPE)
            bmat = bm_ref[:, gs].astype(MXU_DTYPE)
            cb = _nt(cmat, bmat)
            pg = st_ref[0, gc, :].astype(MXU_DTYPE)
            dpg = dprev[gc, :]
            dpgb = dpg.astype(MXU_DTYPE)
            z = _nt(cmat, pg)
            dyg = dyv[:, gc]
            dz = (dyg * ecs[:, gc]).astype(MXU_DTYPE)
            dc = _nn(dz, pg)
            dprev_y = _tn(dz, cmat)
            tcs_s[:, gc] = dyg * z * ecs[:, gc]
            xd = xdt[:, gc] * dec[:, gc]
            wmat = _nt(bmat, dpgb)
            db = _nn(xd.astype(MXU_DTYPE), dpgb)
            tdec_s[:, gc] = wmat * xd
            dxdt_g = wmat * dec[:, gc]
            dcb = jnp.zeros((n, n), F32)
            for q in range(PAIRS_PER_GROUP):
                hp = g * PAIRS_PER_GROUP + q
                sl = slice(hp * LANE, (hp + 1) * LANE)
                xp = xdt[:, sl].astype(MXU_DTYPE)
                dyp = dyv[:, sl]
                dypb = dyp.astype(MXU_DTYPE)
                dxp = None
                for hh in range(2):
                    h = 2 * hp + hh
                    lm = _decay_mat(cs_ref, cst_ref, h, mask)
                    mine = (lane < SSM_HEADDIM) if hh == 0 else (lane >= SSM_HEADDIM)
                    dm = _nt(jnp.where(mine, dyp, 0.0).astype(MXU_DTYPE), xp)
                    dml = dm * lm
                    dcb = dcb + dml
                    gseg = dml * cb
                    rsum = rsum + jnp.sum(gseg, axis=1, keepdims=True) * (hlane == h).astype(F32)
                    csum = csum + (hsub == h).astype(F32) * jnp.sum(gseg, axis=0, keepdims=True)
                    dxh = _tn((cb * lm).astype(MXU_DTYPE), dypb)
                    dxp = dxh if dxp is None else jnp.where(mine, dxh, dxp)
                dxdt_s[:, sl] = dxdt_g[:, q * LANE:(q + 1) * LANE] + dxp
            dcbb = dcb.astype(MXU_DTYPE)
            dxbc_ref[:, CM_COL0 + g * SSM_STATE:CM_COL0 + (g + 1) * SSM_STATE] = dc + _nn(dcbb, bmat)
            dxbc_ref[:, BM_COL0 + g * SSM_STATE:BM_COL0 + (g + 1) * SSM_STATE] = db + _tn(dcbb, cmat)
            for r in range(HEADS_PER_GROUP):
                h = g * HEADS_PER_GROUP + r
                rows = slice(h * SSM_HEADDIM, (h + 1) * SSM_HEADDIM)
                lr = slice(r * SSM_HEADDIM, (r + 1) * SSM_HEADDIM)
                e = jnp.exp(cst_ref[0, h:h + 1, n - 1:n])
                dprev[rows, :] = dpg[lr, :] * e + dprev_y[lr, :]
            tq = _dot_exact(dpg * st_ref[0, gc, :], rm[gc, :], _DN_TN, 0)
            if g == 0:
                qsum = jnp.sum(tq, axis=0, keepdims=True)
            else:
                qsum = qsum + jnp.sum(tq, axis=0, keepdims=True)
        dxdt = dxdt_s[...]
        dxbc_ref[:, 0:SSM_INNER] = dxdt * dtx + dyv * dskx
        ddt = head_reduce(dxdt * xs)
        edec = head_reduce(tdec_s[...])
        ycs = head_reduce(tcs_s[...])
        row = lax.broadcasted_iota(jnp.int32, (n, HEAD_PAD), 0)
        extra = jnp.sum(edec, axis=0, keepdims=True) + qsum * jnp.exp(csv[n - 1:n, :])
        dcs = rsum - csum.T + ycs - edec + jnp.where(row == n - 1, extra, 0.0)
        r2 = lax.broadcasted_iota(jnp.int32, (n, n), 0)
        c2 = lax.broadcasted_iota(jnp.int32, (n, n), 1)
        dda = _dot_exact((c2 >= r2).astype(F32), dcs, _DN_NN, 1)
        a_row = -jnp.exp(al_ref[0:1, :])
        ddt = ddt + dda * a_row
        dal_ref[...] += jnp.sum(dda * dtv, axis=0, keepdims=True) * a_row
        dds_ref[...] += jnp.sum(head_reduce(dyv * xs), axis=0, keepdims=True)
        ddr = ddt * _sigmoid(raw_ref[...] + bias_ref[0:1, :])
        ddr_ref[...] = ddr
        dbias_ref[...] += jnp.sum(ddr, axis=0, keepdims=True)

    par8 = _par_spec((SUBLANE, HEAD_PAD))
    return pl.pallas_call(
        body, grid=(t // n,),
        in_specs=[xs_s, xs_s, bm_s, cm_s, hv_s, hv_s, hvt_s, hvt_s, st_s, _par_spec((1, SSM_INNER)), par8, hv_s, par8],
        out_specs=[xbc_s, hv_s, acc_s, acc_s, acc_s],
        out_shape=[jax.ShapeDtypeStruct((t, SSM_CONV_DIM), F32), jax.ShapeDtypeStruct((t, HEAD_PAD), F32),
                   jax.ShapeDtypeStruct((1, HEAD_PAD), F32), jax.ShapeDtypeStruct((1, HEAD_PAD), F32),
                   jax.ShapeDtypeStruct((1, HEAD_PAD), F32)],
        scratch_shapes=[pltpu.VMEM((SSM_INNER, SSM_STATE), F32), pltpu.VMEM((n, SSM_INNER), F32),
                        pltpu.VMEM((n, SSM_INNER), F32), pltpu.VMEM((n, SSM_INNER), F32)],
        compiler_params=_params("arbitrary"), name=name)(dy, xbc, xbc, xbc, dt, cs, dtt, cst, st, dskx, a_log8, dt_raw, dt_bias8)


def _gate_norm_fwd(y, proj, norm_g, *, name):
    t, c = y.shape
    tm = _pick(t, (256, 128))

    def body(y_ref, z_ref, g_ref, o_ref):
        z = z_ref[...]
        yz = y_ref[...] * z * _sigmoid(z)
        for g in range(SSM_GROUPS):
            gc = slice(g * GROUP_CH, (g + 1) * GROUP_CH)
            seg = yz[:, gc]
            r = lax.rsqrt(jnp.mean(seg * seg, axis=-1, keepdims=True) + RMS_EPS)
            o_ref[:, gc] = (seg * r * g_ref[:, gc]).astype(MXU_DTYPE)

    return pl.pallas_call(
        body, grid=(t // tm,), in_specs=[_row_spec(tm, c), _row_spec(tm, c, 1), _par_spec((1, c))],
        out_specs=_row_spec(tm, c), out_shape=jax.ShapeDtypeStruct((t, c), MXU_DTYPE),
        compiler_params=_params("parallel"), name=name)(y, proj, norm_g.reshape(1, c))


def _gate_norm_bwd(dyb, y, proj, norm_g, dproj, *, name):
    t, c = y.shape
    tm = _pick(t, (256, 128))

    def body(d_ref, y_ref, z_ref, g_ref, _, dy_ref, dz_ref, dg_ref):
        @pl.when(pl.program_id(0) == 0)
        def _():
            dg_ref[...] = jnp.zeros_like(dg_ref)

        z = z_ref[...]
        yv = y_ref[...]
        sz = _sigmoid(z)
        silu = z * sz
        yz = yv * silu
        dv = d_ref[...]
        for g in range(SSM_GROUPS):
            gc = slice(g * GROUP_CH, (g + 1) * GROUP_CH)
            seg = yz[:, gc]
            r = lax.rsqrt(jnp.mean(seg * seg, axis=-1, keepdims=True) + RMS_EPS)
            nrm = seg * r
            dn = dv[:, gc] * g_ref[:, gc]
            dg_ref[:, gc] += jnp.sum(dv[:, gc] * nrm, axis=0, keepdims=True)
            dyz = r * (dn - nrm * jnp.mean(dn * nrm, axis=-1, keepdims=True))
            dy_ref[:, gc] = dyz * silu[:, gc]
            dz_ref[:, gc] = (dyz * yv[:, gc] * (sz[:, gc] * (1.0 + z[:, gc] * (1.0 - sz[:, gc])))).astype(MXU_DTYPE)

    return pl.pallas_call(
        body, grid=(t // tm,), in_specs=[_row_spec(tm, c), _row_spec(tm, c), _row_spec(tm, c, 1), _par_spec((1, c)), _ANY],
        out_specs=[_row_spec(tm, c), _row_spec(tm, c, 1), _par_spec((1, c))],
        out_shape=[jax.ShapeDtypeStruct((t, c), F32), jax.ShapeDtypeStruct(dproj.shape, dproj.dtype),
                   jax.ShapeDtypeStruct((1, c), F32)],
        input_output_aliases={4: 1},
        compiler_params=_params("arbitrary"), name=name)(dyb, y, proj, norm_g.reshape(1, c), dproj)


GA_COLBLK = GAB_COL0 // D_MODEL


def _merge_fwd(br_a, br_b, proj, *, name):
    t, c = br_a.shape
    tm = _pick(t, ROW_TILES)

    def body(a_ref, b_ref, ga_ref, gb_ref, o_ref):
        o_ref[...] = (_sigmoid(ga_ref[...]) * a_ref[...] + _sigmoid(gb_ref[...]) * b_ref[...]).astype(MXU_DTYPE)

    return pl.pallas_call(
        body, grid=(t // tm,),
        in_specs=[_row_spec(tm, c), _row_spec(tm, c), _row_spec(tm, c, GA_COLBLK), _row_spec(tm, c, GA_COLBLK + 1)],
        out_specs=_row_spec(tm, c), out_shape=jax.ShapeDtypeStruct((t, c), MXU_DTYPE),
        compiler_params=_params("parallel"), name=name)(br_a, br_b, proj, proj)


def _merge_bwd(dm, br_a, br_b, proj, *, name):
    t, c = br_a.shape
    tm = _pick(t, ROW_TILES)

    def body(dm_ref, a_ref, b_ref, ga_ref, gb_ref, da_ref, db_ref, dg_ref):
        d = dm_ref[...]
        sa = _sigmoid(ga_ref[...])
        sb = _sigmoid(gb_ref[...])
        da_ref[...] = (d * sa).astype(MXU_DTYPE)
        db_ref[...] = (d * sb).astype(MXU_DTYPE)
        dg_ref[:, :c] = (d * a_ref[...] * sa * (1.0 - sa)).astype(MXU_DTYPE)
        dg_ref[:, c:] = (d * b_ref[...] * sb * (1.0 - sb)).astype(MXU_DTYPE)

    return pl.pallas_call(
        body, grid=(t // tm,),
        in_specs=[_row_spec(tm, c), _row_spec(tm, c), _row_spec(tm, c), _row_spec(tm, c, GA_COLBLK), _row_spec(tm, c, GA_COLBLK + 1)],
        out_specs=[_row_spec(tm, c), _row_spec(tm, c), _row_spec(tm, 2 * c, GAB_COL0 // (2 * c))],
        out_shape=[jax.ShapeDtypeStruct((t, c), MXU_DTYPE), jax.ShapeDtypeStruct((t, c), MXU_DTYPE),
                   jax.ShapeDtypeStruct((t, MAIN_COLS), MXU_DTYPE)],
        compiler_params=_params("parallel"), name=name)(dm, br_a, br_b, proj, proj)


def _layer_fwd(x, xb, memn_b, w, *, bsz, tag):
    nc = x.shape[0] // bsz // CHUNK
    sv = {"x_in": xb}
    proj = _mm(xb, w["w_main"], name=f"{tag}_proj")
    dt_raw = _mm(xb, w["w_dt"], name=f"{tag}_dtproj")
    sgo = _sg_fwd(proj, w["sg_ln_g"], w["sg_ln_b"], w["sg_w"], w["sg_bcol"], name=f"{tag}_sg_fwd")
    xbc = _conv_fwd(proj, w["conv_w"], w["conv_b"], bsz=bsz, name=f"{tag}_conv_fwd")
    dt, cs, dtt, cst = _ssd_prep(dt_raw, w["dt_bias8"], w["a_log8"], name=f"{tag}_ssd_prep")
    y, st = _ssd_fwd(xbc, dt, cs, dtt, cst, w["d_skipx"], nc=nc, name=f"{tag}_ssd_fwd")
    yb = _gate_norm_fwd(y, proj, w["ssm_norm_g"], name=f"{tag}_gate_norm_fwd")
    br_a = _mm(sgo, w["p_a"], name=f"{tag}_br_a")
    br_b = _mm(yb, w["p_b"], name=f"{tag}_br_b")
    merged = _merge_fwd(br_a, br_b, proj, name=f"{tag}_merge_fwd")
    mix = _mm(merged, w["w_mix_o"], name=f"{tag}_mix_o")
    x1, x1b, xh1, rs1 = _ln_fwd(x, mix, w["ln_g"][0], w["ln_b"][0], name=f"{tag}_ln1_fwd")
    sv.update(proj=proj, dt_raw=dt_raw, sgo=sgo, xbc=xbc, dt=dt, cs=cs, dtt=dtt, cst=cst, y=y, st=st, yb=yb,
              br_a=br_a, br_b=br_b, merged=merged, xh1=xh1, rs1=rs1, x1b=x1b)
    q = _mm(x1b, w["w_xq"], out_dtype=MXU_DTYPE, name=f"{tag}_q")
    kv = _mm(memn_b, w["w_xkv"], out_dtype=MXU_DTYPE, name=f"{tag}_kv")
    o = _attn_fwd(q, kv, bsz=bsz, name=f"{tag}_attn_fwd")
    att = _mm(o, w["w_xo"], name=f"{tag}_xo")
    x2, x2b, xh2, rs2 = _ln_fwd(x1, att, w["ln_g"][1], w["ln_b"][1], name=f"{tag}_ln2_fwd")
    sv.update(q=q, kv=kv, o=o, xh2=xh2, rs2=rs2, x2b=x2b)
    h = _mm(x2b, w["w_ffn_in"], name=f"{tag}_ffn_in")
    a = _swiglu_fwd(h, name=f"{tag}_swiglu_fwd")
    ffn = _mm(a, w["w_ffn_out"], name=f"{tag}_ffn_out")
    x3, x3b, xh3, rs3 = _ln_fwd(x2, ffn, w["ln_g"][2], w["ln_b"][2], name=f"{tag}_ln3_fwd")
    sv.update(h=h, a=a, xh3=xh3, rs3=rs3)
    return x3, x3b, sv


def _layer_bwd(dx3_addends, dx3_scales, memn_b, w, sv, *, bsz, tag):
    nc = sv["xh1"].shape[0] // bsz // CHUNK
    gr = {}
    dp3, dp3b, dg3, db3 = _ln_bwd(dx3_addends, dx3_scales, sv["xh3"], sv["rs3"], w["ln_g"][2], name=f"{tag}_ln3_bwd")
    da = _mm(dp3b, w["w_ffn_out"], tb=True, name=f"{tag}_d_a")
    gr["w_ffn_out"] = _mm(sv["a"], dp3b, ta=True, name=f"{tag}_dw_ffn_out")
    dh = _swiglu_bwd(sv["h"], da, name=f"{tag}_swiglu_bwd")
    gr["w_ffn_in"] = _mm(sv["x2b"], dh, ta=True, name=f"{tag}_dw_ffn_in")
    dx2_br = _mm(dh, w["w_ffn_in"], tb=True, name=f"{tag}_dx2")
    dp2, dp2b, dg2, db2 = _ln_bwd([dp3, dx2_br], [ALPHA, 1.0], sv["xh2"], sv["rs2"], w["ln_g"][1], name=f"{tag}_ln2_bwd")
    do = _mm(dp2b, w["w_xo"], tb=True, out_dtype=MXU_DTYPE, name=f"{tag}_d_o")
    gr["w_xo"] = _mm(sv["o"], dp2b, ta=True, name=f"{tag}_dw_xo")
    dq, dk, dv = _attn_bwd(sv["q"], sv["kv"], do, bsz=bsz, name=f"{tag}_attn_bwd")
    dkv = jnp.concatenate([dk, dv], axis=1)
    gr["w_xq"] = _mm(sv["x1b"], dq, ta=True, name=f"{tag}_dw_xq")
    gr["w_xkv"] = _mm(memn_b, dkv, ta=True, name=f"{tag}_dw_xkv")
    dmemn = _mm(dkv, w["w_xkv"], tb=True, name=f"{tag}_d_memn")
    dx1_br = _mm(dq, w["w_xq"], tb=True, name=f"{tag}_dx1")
    dp1, dp1b, dg1, db1 = _ln_bwd([dp2, dx1_br], [ALPHA, 1.0], sv["xh1"], sv["rs1"], w["ln_g"][0], name=f"{tag}_ln1_bwd")
    gr["ln_g"] = jnp.concatenate([dg1, dg2, dg3], axis=0)
    gr["ln_b"] = jnp.concatenate([db1, db2, db3], axis=0)
    dmerged = _mm(dp1b, w["w_mix_o"], tb=True, name=f"{tag}_d_merged")
    gr["w_mix_o"] = _mm(sv["merged"], dp1b, ta=True, name=f"{tag}_dw_mix_o")
    dbr_a, dbr_b, dproj = _merge_bwd(dmerged, sv["br_a"], sv["br_b"], sv["proj"], name=f"{tag}_merge_bwd")
    gr["p_a"] = _mm(sv["sgo"], dbr_a, ta=True, name=f"{tag}_dw_p_a")
    gr["p_b"] = _mm(sv["yb"], dbr_b, ta=True, name=f"{tag}_dw_p_b")
    dsgo = _mm(dbr_a, w["p_a"], tb=True, name=f"{tag}_d_sgo")
    dyb = _mm(dbr_b, w["p_b"], tb=True, name=f"{tag}_d_yb")
    dy, dproj, gr["ssm_norm_g"] = _gate_norm_bwd(dyb, sv["y"], sv["proj"], w["ssm_norm_g"], dproj, name=f"{tag}_gate_norm_bwd")
    dxbc, ddr, gr["a_log"], gr["d_skip"], gr["dt_bias"] = _ssd_bwd(
        dy, sv["xbc"], sv["dt"], sv["cs"], sv["dtt"], sv["cst"], sv["st"], w["d_skipx"], w["a_log8"], sv["dt_raw"],
        w["dt_bias8"], nc=nc, name=f"{tag}_ssd_bwd")
    dproj, gr["conv_w"], gr["conv_b"] = _conv_bwd(sv["proj"], dxbc, w["conv_w"], w["conv_b"], dproj, bsz=bsz, name=f"{tag}_conv_bwd")
    dproj, gr["sg_w"], dsg_bcol, gr["sg_ln_g"], gr["sg_ln_b"] = _sg_bwd(
        sv["proj"], dsgo, w["sg_ln_g"], w["sg_ln_b"], w["sg_w"], w["sg_bcol"], dproj, name=f"{tag}_sg_bwd")
    gr["sg_b"] = dsg_bcol[..., 0]
    gr["w_main"] = _mm(sv["x_in"], dproj, ta=True, name=f"{tag}_dw_main")
    gr["w_dt"] = _mm(sv["x_in"], ddr, ta=True, name=f"{tag}_dw_dt")
    dx_main = _mm(dproj, w["w_main"], tb=True, name=f"{tag}_dx_main")
    dx_dt = _mm(ddr, w["w_dt"], tb=True, name=f"{tag}_dx_dt")
    return [dp1, dx_main, dx_dt], [ALPHA, 1.0, 1.0], gr, dmemn


def _local_step(x, mem, tgt, mem_ln_g, mem_ln_b, layers, on_layer_grads=None):
    bsz, s, d = x.shape
    xf = x.reshape(bsz * s, d)
    memf = mem.reshape(-1, d)
    _, memn_b, mxh, mrs = _ln_fwd(memf, None, mem_ln_g, mem_ln_b, name="mem_ln_fwd")
    cur, curb, saved, weights = xf, xf, [], []
    for li, get_weights in enumerate(layers):
        weights.append(get_weights(cur))
        cur, curb, sv = _layer_fwd(cur, curb, memn_b, weights[li], bsz=bsz, tag=f"l{li}")
        saved.append(sv)
    dy, lsum = _loss_head(cur, tgt.reshape(bsz * s, d), name="loss_head")
    addends, scales = [dy], [1.0]
    grads, dmem = [None] * len(layers), []
    token = None
    for li in reversed(range(len(layers))):
        w = weights[li]
        if token is not None:
            w = dict(w, ln_g=w["ln_g"] + token[0, 0])
        addends, scales, grads[li], dm = _layer_bwd(addends, scales, memn_b, w, saved[li], bsz=bsz, tag=f"l{li}")
        dmem.append(dm)
        token = on_layer_grads(li, grads[li]) if on_layer_grads is not None else None
    grad_x = _add_scaled(addends, scales, name="grad_x").reshape(bsz, s, d)
    _, _, dmg, dmb = _ln_bwd(dmem, [1.0] * len(dmem), mxh, mrs, mem_ln_g, name="mem_ln_bwd")
    return lsum, grad_x, grads, dmg[0], dmb[0]


_ANY = pl.BlockSpec(memory_space=pl.ANY)
_MESH = pl.DeviceIdType.MESH


def _all_gather8(x, *, name):
    def body(x_ref, out_ref, send_sems, recv_sems):
        mx, my, mc = lax.axis_index("x"), lax.axis_index("y"), lax.axis_index("c")
        me, sibling = (mx, my, mc), (mx, my, 1 - mc)
        chips = [(1 - mx, my), (mx, 1 - my), (1 - mx, 1 - my)]

        def blk(px, py, pc):
            return out_ref.at[4 * px + 2 * py + pc]

        def copy(k, block, to, src=None):
            return pltpu.make_async_remote_copy(
                src_ref=blk(*block) if src is None else src, dst_ref=blk(*block), send_sem=send_sems.at[k],
                recv_sem=recv_sems.at[k], device_id=to, device_id_type=_MESH)

        first = [copy(0, me, sibling, src=x_ref)]
        first += [copy(1 + j, me, (*chip, mc), src=x_ref) for j, chip in enumerate(chips)]
        for cp in first:
            cp.start()
        passed = [copy(4 + j, (*chip, mc), sibling) for j, chip in enumerate(chips)]
        for j, chip in enumerate(chips):
            copy(1 + j, (*chip, mc), me).wait_recv()
            passed[j].start()
        copy(0, sibling, me).wait_recv()
        for j, chip in enumerate(chips):
            copy(4 + j, (*chip, 1 - mc), me).wait_recv()
        for cp in first + passed:
            cp.wait_send()

    return pl.pallas_call(
        body, out_shape=jax.ShapeDtypeStruct((N_DEV,) + x.shape, x.dtype), in_specs=[_ANY], out_specs=_ANY,
        scratch_shapes=[pltpu.SemaphoreType.DMA((7,)), pltpu.SemaphoreType.DMA((7,))], name=name)(x)


def _row_tile(rows, row_bytes, mult=SUBLANE):
    best = None
    for tr in range(mult, rows + 1, mult):
        if rows % tr == 0 and (best is None or tr * row_bytes <= BLOCK_BYTES):
            best = tr
    return rows if best is None else best


def _gather_shape(r, c, kind):
    return {"row": (2, N_CHIPS * r, c), "col": (2, r, N_CHIPS * c), "chip": (2, N_CHIPS, r, c)}[kind]


def _cast_place(shard, kind, dtype, chip_idx, *, name):
    _, r, c = shard.shape
    tr = _row_tile(r, c * 4, 16)
    nt = r // tr

    def body(_, s_ref, o_ref):
        o_ref[...] = s_ref[...].astype(dtype)

    if kind == "row":
        out_spec = pl.BlockSpec((None, tr, c), lambda l, i, j_ref: (l, j_ref[0] * nt + i, 0))
    elif kind == "col":
        out_spec = pl.BlockSpec((None, tr, c), lambda l, i, j_ref: (l, i, j_ref[0]))
    else:
        out_spec = pl.BlockSpec((None, None, tr, c), lambda l, i, j_ref: (l, j_ref[0], i, 0))
    grid_spec = pltpu.PrefetchScalarGridSpec(
        num_scalar_prefetch=1, grid=(2, nt), in_specs=[pl.BlockSpec((None, tr, c), lambda l, i, j_ref: (l, i, 0))],
        out_specs=out_spec)
    return pl.pallas_call(body, grid_spec=grid_spec, out_shape=jax.ShapeDtypeStruct(_gather_shape(r, c, kind), dtype),
                          compiler_params=_params("parallel", "parallel"), name=name)(chip_idx, shard)


def _gather_params(bufs, shard_shapes, kinds, *, name):
    n = len(bufs)

    def body(*refs):
        outs = refs[n:2 * n]
        send_sems, recv_sems = refs[2 * n:]
        mx, my, mc = lax.axis_index("x"), lax.axis_index("y"), lax.axis_index("c")
        me, sibling = (mx, my, mc), (mx, my, 1 - mc)
        chips = [(1 - mx, my), (mx, 1 - my), (1 - mx, 1 - my)]

        def blk(i, px, py, pc):
            r, c = shard_shapes[i]
            j = 2 * px + py
            if kinds[i] == "row":
                return outs[i].at[pc, pl.ds(pl.multiple_of(j * r, r), r)]
            if kinds[i] == "col":
                return outs[i].at[pc, :, pl.ds(pl.multiple_of(j * c, c), c)]
            return outs[i].at[pc, j]

        def copy(i, k, block, to):
            return pltpu.make_async_remote_copy(
                src_ref=blk(i, *block), dst_ref=blk(i, *block), send_sem=send_sems.at[6 * i + k],
                recv_sem=recv_sems.at[6 * i + k], device_id=to, device_id_type=_MESH)

        sent = []
        for i in range(n):
            for j, chip in enumerate(chips):
                cp = copy(i, j, me, (*chip, mc))
                cp.start()
                sent.append(cp)
        for j, chip in enumerate(chips):
            for i in range(n):
                copy(i, j, (*chip, mc), me).wait_recv()
                fwd = copy(i, 3 + j, (*chip, mc), sibling)
                fwd.start()
                sent.append(fwd)
        for i in range(n):
            for j, chip in enumerate(chips):
                copy(i, 3 + j, (*chip, 1 - mc), me).wait_recv()
        for cp in sent:
            cp.wait_send()

    return pl.pallas_call(
        body, out_shape=[jax.ShapeDtypeStruct(b.shape, b.dtype) for b in bufs], in_specs=[_ANY] * n, out_specs=[_ANY] * n,
        input_output_aliases={i: i for i in range(n)},
        scratch_shapes=[pltpu.SemaphoreType.DMA((6 * n,)), pltpu.SemaphoreType.DMA((6 * n,))], name=name)(*bufs)


def _half(r, h):
    return pl.ds(pl.multiple_of(h * (r // 2), r // 2), r // 2)


def _grads_to_sibling(gs, views, *, name):
    n = len(gs)

    def recv_shape(g, view):
        if view == "chip":
            return jax.ShapeDtypeStruct((g.shape[0], g.shape[1] // 2, g.shape[2]), g.dtype)
        return jax.ShapeDtypeStruct((g.shape[0] // 2, g.shape[1]), g.dtype)

    def body(*refs):
        ins, outs = refs[:n], refs[n:2 * n]
        send_sems, recv_sems = refs[2 * n:]
        mx, my, mc = lax.axis_index("x"), lax.axis_index("y"), lax.axis_index("c")
        copies = []
        for i in range(n):
            if views[i] == "chip":
                src = ins[i].at[:, _half(gs[i].shape[1], 1 - mc)]
            else:
                src = ins[i].at[_half(gs[i].shape[0], 1 - mc)]
            cp = pltpu.make_async_remote_copy(src_ref=src, dst_ref=outs[i], send_sem=send_sems.at[i], recv_sem=recv_sems.at[i],
                                              device_id=(mx, my, 1 - mc), device_id_type=_MESH)
            cp.start()
            copies.append(cp)
        for cp in copies:
            cp.wait()

    return pl.pallas_call(
        body, out_shape=[recv_shape(g, v) for g, v in zip(gs, views)], in_specs=[_ANY] * n, out_specs=[_ANY] * n,
        scratch_shapes=[pltpu.SemaphoreType.DMA((n,)), pltpu.SemaphoreType.DMA((n,))], name=name)(*gs)


def _grads_to_chips(pairs, views, *, name):
    n = len(pairs)

    def quad_shape(p, view):
        if view == "chip":
            return jax.ShapeDtypeStruct(p.shape, p.dtype)
        return jax.ShapeDtypeStruct((N_CHIPS, p.shape[0], p.shape[1] // N_CHIPS), p.dtype)

    def body(*refs):
        ins, outs = refs[:n], refs[n:2 * n]
        send_sems, recv_sems = refs[2 * n:]
        mx, my, mc = lax.axis_index("x"), lax.axis_index("y"), lax.axis_index("c")
        me = 2 * mx + my
        chips = [(1 - mx, my), (mx, 1 - my), (1 - mx, 1 - my)]

        def blk(i, j):
            if views[i] == "chip":
                return ins[i].at[j]
            c = pairs[i].shape[1] // N_CHIPS
            return ins[i].at[:, pl.ds(pl.multiple_of(j * c, c), c)]

        copies = []
        for i in range(n):
            for k, (px, py) in enumerate(chips):
                cp = pltpu.make_async_remote_copy(src_ref=blk(i, 2 * px + py), dst_ref=outs[i].at[me], send_sem=send_sems.at[3 * i + k],
                                                  recv_sem=recv_sems.at[3 * i + k], device_id=(px, py, mc), device_id_type=_MESH)
                cp.start()
                copies.append(cp)
        for cp in copies:
            cp.wait()

    return pl.pallas_call(
        body, out_shape=[quad_shape(p, v) for p, v in zip(pairs, views)], in_specs=[_ANY] * n, out_specs=[_ANY] * n,
        scratch_shapes=[pltpu.SemaphoreType.DMA((3 * n,)), pltpu.SemaphoreType.DMA((3 * n,))], name=name)(*pairs)


_HBM = pl.BlockSpec(memory_space=pltpu.HBM)
_SEM = pl.BlockSpec(memory_space=pltpu.SEMAPHORE)
_EFFECT = pltpu.SideEffectType.DATAFLOW_SIDE_EFFECTING


def _cast_place_layer(shard, l, kind, chip_idx, after, *, name):
    _, r, c = shard.shape
    tr = _row_tile(r, c * 4, 16)
    nt = r // tr

    def body(_, s_ref, *rest):
        rest[-1][...] = s_ref[...].astype(MXU_DTYPE)

    if kind == "row":
        out_spec = pl.BlockSpec((tr, c), lambda i, j_ref: (j_ref[0] * nt + i, 0))
    elif kind == "col":
        out_spec = pl.BlockSpec((tr, c), lambda i, j_ref: (i, j_ref[0]))
    else:
        out_spec = pl.BlockSpec((None, tr, c), lambda i, j_ref: (j_ref[0], i, 0))
    extra = [] if after is None else [after]
    grid_spec = pltpu.PrefetchScalarGridSpec(
        num_scalar_prefetch=1, grid=(nt,), in_specs=[pl.BlockSpec((None, tr, c), lambda i, j_ref: (l, i, 0))] + [_ANY] * len(extra),
        out_specs=out_spec)
    return pl.pallas_call(body, grid_spec=grid_spec, out_shape=jax.ShapeDtypeStruct(_gather_shape(r, c, kind)[1:], MXU_DTYPE),
                          compiler_params=_params("parallel"), name=name)(chip_idx, shard, *extra)


def _half_block(ref, kind, r, c, j, h):
    rows = _half(r, h)
    if kind == "row":
        return ref.at[pl.ds(pl.multiple_of(j * r + h * (r // 2), r // 2), r // 2)]
    if kind == "col":
        return ref.at[rows, pl.ds(pl.multiple_of(j * c, c), c)]
    return ref.at[j, rows]


def _gather_ici_copies(buf_refs, shapes, kinds, send_sems, recv_sems):
    mx, my, mc = lax.axis_index("x"), lax.axis_index("y"), lax.axis_index("c")
    chips = [(1 - mx, my), (mx, 1 - my), (1 - mx, 1 - my)]
    copies = []
    for i, (r, c) in enumerate(shapes):
        mine = _half_block(buf_refs[i], kinds[i], r, c, 2 * mx + my, mc)
        for k, (px, py) in enumerate(chips):
            copies.append(pltpu.make_async_remote_copy(
                src_ref=mine, dst_ref=mine, send_sem=send_sems.at[3 * i + k], recv_sem=recv_sems.at[3 * i + k],
                device_id=(px, py, mc), device_id_type=_MESH))
    return copies


def _gather_start(bufs, shapes, kinds, *, name):
    n = len(bufs)

    def body(*refs):
        send_sems, recv_sems, token = refs[n], refs[n + 1], refs[-1]
        for cp in _gather_ici_copies(refs[:n], shapes, kinds, send_sems, recv_sems):
            cp.start()
        token[...] = jnp.zeros_like(token)

    outs = pl.pallas_call(
        body, name=name,
        out_shape=(pltpu.SemaphoreType.DMA((3 * n,)), pltpu.SemaphoreType.DMA((3 * n,)),
                   *[pltpu.HBM(b.shape, b.dtype) for b in bufs], jax.ShapeDtypeStruct((SUBLANE, LANE), F32)),
        in_specs=[_HBM] * n, out_specs=(_SEM, _SEM, *[_HBM] * n, pl.BlockSpec(memory_space=pltpu.VMEM)),
        input_output_aliases={i: 2 + i for i in range(n)},
        compiler_params=pltpu.CompilerParams(has_side_effects=_EFFECT),
    )(*[pltpu.with_memory_space_constraint(b, pltpu.HBM) for b in bufs])
    return outs[0], outs[1], list(outs[2:2 + n]), outs[-1]


def _gather_wait(send_sems, recv_sems, bufs, shapes, kinds, after, *, name):
    n = len(bufs)

    def body(*refs):
        for cp in _gather_ici_copies(refs[:n], shapes, kinds, refs[n], refs[n + 1]):
            cp.wait_send()
            cp.wait_recv()

    outs = pl.pallas_call(
        body, name=name, out_shape=tuple(pltpu.HBM(b.shape, b.dtype) for b in bufs),
        in_specs=[_HBM] * n + [_SEM, _SEM, _ANY], out_specs=tuple([_HBM] * n), input_output_aliases={i: i for i in range(n)},
        compiler_params=pltpu.CompilerParams(has_side_effects=_EFFECT),
    )(*bufs, send_sems, recv_sems, after)
    return list(outs)


def _gather_forward(bufs, shapes, kinds, *, name):
    n = len(bufs)

    def body(*refs):
        outs = refs[n:2 * n]
        send_sems, recv_sems = refs[2 * n:]
        mx, my, mc = lax.axis_index("x"), lax.axis_index("y"), lax.axis_index("c")
        chips = [(1 - mx, my), (mx, 1 - my), (1 - mx, 1 - my)]
        copies = []
        for i, (r, c) in enumerate(shapes):
            for k, (px, py) in enumerate(chips):
                got = _half_block(outs[i], kinds[i], r, c, 2 * px + py, mc)
                cp = pltpu.make_async_remote_copy(src_ref=got, dst_ref=got, send_sem=send_sems.at[3 * i + k],
                                                  recv_sem=recv_sems.at[3 * i + k], device_id=(mx, my, 1 - mc), device_id_type=_MESH)
                cp.start()
                copies.append(cp)
        for cp in copies:
            cp.wait()

    return pl.pallas_call(
        body, out_shape=[jax.ShapeDtypeStruct(b.shape, b.dtype) for b in bufs], in_specs=[_ANY] * n, out_specs=[_ANY] * n,
        input_output_aliases={i: i for i in range(n)},
        scratch_shapes=[pltpu.SemaphoreType.DMA((3 * n,)), pltpu.SemaphoreType.DMA((3 * n,))], name=name)(*bufs)


def _chip_exchange_copies(pair_refs, land_refs, pairs, views, send_sems, recv_sems):
    mx, my, mc = lax.axis_index("x"), lax.axis_index("y"), lax.axis_index("c")
    me = 2 * mx + my
    chips = [(1 - mx, my), (mx, 1 - my), (1 - mx, 1 - my)]
    copies = []
    for i in range(len(pairs)):
        for k, (px, py) in enumerate(chips):
            j = 2 * px + py
            if views[i] == "chip":
                src = pair_refs[i].at[j]
            else:
                c = pairs[i].shape[1] // N_CHIPS
                src = pair_refs[i].at[:, pl.ds(pl.multiple_of(j * c, c), c)]
            copies.append(pltpu.make_async_remote_copy(
                src_ref=src, dst_ref=land_refs[i].at[me], send_sem=send_sems.at[3 * i + k], recv_sem=recv_sems.at[3 * i + k],
                device_id=(px, py, mc), device_id_type=_MESH))
    return copies


def _quad_shape(p, view):
    return p.shape if view == "chip" else (N_CHIPS, p.shape[0], p.shape[1] // N_CHIPS)


def _grads_to_chips_start(pairs, views, *, name):
    n = len(pairs)
    lands = [pltpu.with_memory_space_constraint(lax.empty(_quad_shape(p, v), p.dtype), pltpu.HBM) for p, v in zip(pairs, views)]

    def body(*refs):
        pair_refs, land_refs = refs[:n], refs[n:2 * n]
        send_sems, recv_sems = refs[2 * n], refs[2 * n + 1]
        token = refs[-1]
        for cp in _chip_exchange_copies(pair_refs, land_refs, pairs, views, send_sems, recv_sems):
            cp.start()
        token[...] = jnp.zeros_like(token)

    outs = pl.pallas_call(
        body, name=name,
        out_shape=(pltpu.SemaphoreType.DMA((3 * n,)), pltpu.SemaphoreType.DMA((3 * n,)),
                   *[pltpu.HBM(p.shape, p.dtype) for p in pairs], *[pltpu.HBM(l.shape, l.dtype) for l in lands],
                   jax.ShapeDtypeStruct((SUBLANE, LANE), F32)),
        in_specs=[_HBM] * (2 * n), out_specs=(_SEM, _SEM, *[_HBM] * (2 * n), pl.BlockSpec(memory_space=pltpu.VMEM)),
        input_output_aliases={i: 2 + i for i in range(2 * n)},
        compiler_params=pltpu.CompilerParams(has_side_effects=_EFFECT),
    )(*[pltpu.with_memory_space_constraint(p, pltpu.HBM) for p in pairs], *lands)
    return outs[0], outs[1], list(outs[2:2 + n]), list(outs[2 + n:2 + 2 * n]), outs[-1]


def _grads_to_chips_wait(send_sems, recv_sems, pairs, lands, views, after, *, name):
    n = len(pairs)

    def body(*refs):
        pair_refs, land_refs = refs[:n], refs[n:2 * n]
        s_sems, r_sems = refs[2 * n], refs[2 * n + 1]
        for cp in _chip_exchange_copies(pair_refs, land_refs, pairs, views, s_sems, r_sems):
            cp.wait_send()
            cp.wait_recv()

    outs = pl.pallas_call(
        body, name=name, out_shape=tuple(pltpu.HBM(x.shape, x.dtype) for x in list(pairs) + list(lands)),
        in_specs=[_HBM] * (2 * n) + [_SEM, _SEM, _ANY], out_specs=tuple([_HBM] * (2 * n)),
        input_output_aliases={i: i for i in range(2 * n)},
        compiler_params=pltpu.CompilerParams(has_side_effects=_EFFECT),
    )(*pairs, *lands, send_sems, recv_sems, after)
    return list(outs[n:])


def _grads_share(tots, *, name):
    n = len(tots)

    def body(*refs):
        ins, outs = refs[:n], refs[n:2 * n]
        send_sems, recv_sems = refs[2 * n:]
        mx, my, mc = lax.axis_index("x"), lax.axis_index("y"), lax.axis_index("c")
        copies = []
        for i in range(n):
            cp = pltpu.make_async_remote_copy(src_ref=ins[i], dst_ref=outs[i], send_sem=send_sems.at[i], recv_sem=recv_sems.at[i],
                                              device_id=(mx, my, 1 - mc), device_id_type=_MESH)
            cp.start()
            copies.append(cp)
        for cp in copies:
            cp.wait()

    return pl.pallas_call(
        body, out_shape=[jax.ShapeDtypeStruct(t.shape, t.dtype) for t in tots], in_specs=[_ANY] * n, out_specs=[_ANY] * n,
        scratch_shapes=[pltpu.SemaphoreType.DMA((n,)), pltpu.SemaphoreType.DMA((n,))], name=name)(*tots)


def _pair_sum(g, recv, view, c_idx, *, name):
    def body(c_ref, a_ref, b_ref, o_ref):
        o_ref[...] = (a_ref[...] + b_ref[...]).astype(WIRE_DTYPE)

    if view == "chip":
        nch, r, c = g.shape
        tr = _row_tile(r // 2, c * 4, 16)
        gv = g.reshape(nch, 2, r // 2, c)
        grid = (nch, (r // 2) // tr)
        in_specs = [pl.BlockSpec((None, None, tr, c), lambda j, i, c_ref: (j, c_ref[0], i, 0)),
                    pl.BlockSpec((None, tr, c), lambda j, i, c_ref: (j, i, 0))]
        out_spec = pl.BlockSpec((None, tr, c), lambda j, i, c_ref: (j, i, 0))
        sem = ("parallel", "parallel")
    else:
        r, c4 = g.shape
        tr = _row_tile(r // 2, c4 * 4, 16)
        gv = g.reshape(2, r // 2, c4)
        grid = ((r // 2) // tr,)
        in_specs = [pl.BlockSpec((None, tr, c4), lambda i, c_ref: (c_ref[0], i, 0)), pl.BlockSpec((tr, c4), lambda i, c_ref: (i, 0))]
        out_spec = pl.BlockSpec((tr, c4), lambda i, c_ref: (i, 0))
        sem = ("parallel",)
    grid_spec = pltpu.PrefetchScalarGridSpec(num_scalar_prefetch=1, grid=grid, in_specs=in_specs, out_specs=out_spec)
    return pl.pallas_call(body, grid_spec=grid_spec, out_shape=jax.ShapeDtypeStruct(recv.shape, WIRE_DTYPE),
                          compiler_params=_params(*sem), name=name)(c_idx, gv, recv)


def _quad_sum(gs, recvs, quads, view, chip_idx, c_idx, *, name):
    nl = len(quads)
    nch, rh, c = quads[0].shape
    tr = _row_tile(rh, c * 4, 16)

    def body(_, __, *refs):
        o_ref = refs[-1]
        per = nch + 1
        for l in range(nl):
            grp = refs[l * per:(l + 1) * per]
            acc = grp[0][...] + grp[1][...]
            for r in grp[2:]:
                acc = acc + r[...].astype(F32)
            o_ref[l] = acc

    if view == "chip":
        own = [pl.BlockSpec((None, None, tr, c), lambda i, j, h: (j[0], h[0], i, 0)),
               pl.BlockSpec((None, tr, c), lambda i, j, h: (j[0], i, 0))]
        gviews = [g.reshape(nch, 2, rh, c) for g in gs]
    else:
        own = [pl.BlockSpec((None, tr, c), lambda i, j, h: (h[0], i, j[0])), pl.BlockSpec((tr, c), lambda i, j, h: (i, j[0]))]
        gviews = [g.reshape(2, rh, nch * c) for g in gs]
    assert nch & (nch - 1) == 0
    got = [pl.BlockSpec((None, tr, c), functools.partial(lambda i, j, h, k: ((j[0] + k) & (nch - 1), i, 0), k=k))
           for k in range(1, nch)]
    ins = []
    for l in range(nl):
        ins += [gviews[l], recvs[l]] + [quads[l]] * (nch - 1)
    grid_spec = pltpu.PrefetchScalarGridSpec(
        num_scalar_prefetch=2, grid=(rh // tr,), in_specs=(own + got) * nl,
        out_specs=pl.BlockSpec((nl, tr, c), lambda i, j, h: (0, i, 0)))
    return pl.pallas_call(body, grid_spec=grid_spec, out_shape=jax.ShapeDtypeStruct((nl, rh, c), F32),
                          compiler_params=_params("parallel"), name=name)(chip_idx, c_idx, *ins)


def _sum_devices(g8, own, dev_idx, *, name):
    k, rows, cols = g8.shape

    def body(d_ref, a_ref, x_ref, o_ref):
        acc = None
        for i in range(k):
            term = jnp.where(d_ref[0] == i, x_ref[...], a_ref[i])
            acc = term if acc is None else acc + term
        o_ref[...] = acc

    grid_spec = pltpu.PrefetchScalarGridSpec(
        num_scalar_prefetch=1, grid=(1,),
        in_specs=[pl.BlockSpec((k, rows, cols), lambda i, d_ref: (0, 0, 0)), pl.BlockSpec((rows, cols), lambda i, d_ref: (0, 0))],
        out_specs=pl.BlockSpec((rows, cols), lambda i, d_ref: (0, 0)))
    return pl.pallas_call(body, grid_spec=grid_spec, out_shape=jax.ShapeDtypeStruct((rows, cols), g8.dtype),
                          compiler_params=_params("arbitrary"), name=name)(dev_idx, g8, own)


def _adamw(w, g, m, v, *, name):
    rows, cols = w.shape
    tr = rows
    for cand in (256, 128, 64, 32, 16, 8):
        if rows % cand == 0 and cand * cols <= 512 * 1024:
            tr = cand
            break
    c1 = 1.0 - ADAM_B1 ** ADAM_STEP
    c2 = 1.0 - ADAM_B2 ** ADAM_STEP

    def body(w_ref, g_ref, m_ref, v_ref, d_ref, nm_ref, nv_ref):
        gv = g_ref[...]
        nm = ADAM_B1 * m_ref[...] + (1.0 - ADAM_B1) * gv
        nv = ADAM_B2 * v_ref[...] + (1.0 - ADAM_B2) * (gv * gv)
        d_ref[...] = -ADAM_LR * ((nm / c1) / (jnp.sqrt(nv / c2) + ADAM_EPS) + ADAM_WD * w_ref[...])
        nm_ref[...] = nm
        nv_ref[...] = nv

    spec = pl.BlockSpec((tr, cols), lambda i: (i, 0))
    shp = jax.ShapeDtypeStruct((rows, cols), F32)
    return pl.pallas_call(body, grid=(rows // tr,), in_specs=[spec] * 4, out_specs=[spec] * 3, out_shape=[shp] * 3,
                          compiler_params=_params("parallel"), name=name)(w, g, m, v)


def _adamw_halves(w, m, v, mine, other, c_idx, *, name):
    nl, r, c = w.shape
    rh = r // 2
    tr = _row_tile(rh, c * 4)
    c1 = 1.0 - ADAM_B1 ** ADAM_STEP
    c2 = 1.0 - ADAM_B2 ** ADAM_STEP

    def body(c_ref, w_ref, m_ref, v_ref, a_ref, b_ref, g_ref, d_ref, nm_ref, nv_ref):
        gv = jnp.where(pl.program_id(1) == c_ref[0], a_ref[...], b_ref[...])
        nm = ADAM_B1 * m_ref[...] + (1.0 - ADAM_B1) * gv
        nv = ADAM_B2 * v_ref[...] + (1.0 - ADAM_B2) * (gv * gv)
        g_ref[...] = gv
        d_ref[...] = -ADAM_LR * ((nm / c1) / (jnp.sqrt(nv / c2) + ADAM_EPS) + ADAM_WD * w_ref[...])
        nm_ref[...] = nm
        nv_ref[...] = nv

    full = pl.BlockSpec((None, None, tr, c), lambda l, h, i, c_ref: (l, h, i, 0))
    half = pl.BlockSpec((None, tr, c), lambda l, h, i, c_ref: (l, i, 0))
    grid_spec = pltpu.PrefetchScalarGridSpec(num_scalar_prefetch=1, grid=(nl, 2, rh // tr),
                                             in_specs=[full] * 3 + [half] * 2, out_specs=[full] * 4)
    shp = jax.ShapeDtypeStruct((nl, 2, rh, c), F32)
    view = (nl, 2, rh, c)
    outs = pl.pallas_call(body, grid_spec=grid_spec, out_shape=[shp] * 4, compiler_params=_params("parallel", "parallel", "parallel"),
                          name=name)(c_idx, w.reshape(view), m.reshape(view), v.reshape(view), mine, other)
    return [o.reshape(nl, r, c) for o in outs]


WEIGHTS = ["mem_ln_g", "mem_ln_b", "w_in", "sg_ln_g", "sg_ln_b", "sg_w", "sg_b", "conv_w", "conv_b", "dt_bias", "a_log",
           "d_skip", "ssm_norm_g", "p_a", "p_b", "w_mix_o", "w_xq", "w_xkv", "w_xo", "w_ffn_in", "w_ffn_out", "ln_g", "ln_b"]
ARG_NAMES = ["x", "mem"] + WEIGHTS + ["loss_target"] + ["m_" + n for n in WEIGHTS] + ["v_" + n for n in WEIGHTS]
BIG = {"w_in": (1, (1024, 9248)), "p_a": (0, (1024, 1024)), "p_b": (0, (2048, 1024)), "w_mix_o": (0, (1024, 1024)),
       "w_xq": (0, (1024, 1024)), "w_xkv": (1, (1024, 2048)), "w_xo": (0, (1024, 1024)), "w_ffn_in": (1, (1024, 5632)),
       "w_ffn_out": (0, (2816, 1024))}
SMALL_SHARDED = {"conv_w": (4, 3072), "ln_g": (3, 1024), "ln_b": (3, 1024)}
SMALL = [n for n in WEIGHTS if n not in BIG]
XBC_IN0, DT_COL0, DT_COL1 = 4096, 7168, 7200
GATHER_KIND = {"w_in": "chip", "p_a": "row", "p_b": "row", "w_mix_o": "row", "w_xq": "row", "w_xkv": "col", "w_xo": "row",
               "w_ffn_in": "col", "w_ffn_out": "row", "conv_w": "chip", "ln_g": "chip", "ln_b": "chip"}
GRAD_VIEW = {n: ("col" if k == "col" else "chip") for n, k in GATHER_KIND.items() if n in BIG}


def _shard_shape(name):
    axis, (r, c) = BIG[name]
    return (r // N_CHIPS, c) if axis == 0 else (r, c // N_CHIPS)


def _pad_rows(flat, cols, row_mult):
    n = flat.shape[0]
    rows = -(-n // cols)
    rows = -(-rows // row_mult) * row_mult
    return jnp.pad(flat, (0, rows * cols - n)).reshape(rows, cols)


BIG_KINDS = [GATHER_KIND[n] for n in BIG]


def _gather_small_params(a, chip):
    names = list(SMALL_SHARDED)
    kinds = [GATHER_KIND[n] for n in names]
    bufs = [_cast_place(a[n], GATHER_KIND[n], F32, chip.reshape(1), name=f"place_{n}") for n in names]
    outs = _gather_params(bufs, [a[n].shape[1:] for n in names], kinds, name="gather_small_params")
    full = {}
    for n, o in zip(names, outs):
        _, _, r, c = o.shape
        full[n] = jnp.transpose(o, (0, 2, 1, 3)).reshape(DEPTH, r, N_CHIPS * c)
    return full


def _gather_layer_start(a, l, chip, after=None):
    bufs = [_cast_place_layer(a[n], l, GATHER_KIND[n], chip.reshape(1), after, name=f"place_{n}_l{l}") for n in BIG]
    return _gather_start(bufs, [a[n].shape[1:] for n in BIG], BIG_KINDS, name=f"gather_start_l{l}")


def _gather_layer_finish(a, l, flight, after):
    send_sems, recv_sems, bufs, token = flight
    shapes = [a[n].shape[1:] for n in BIG]
    bufs = _gather_wait(send_sems, recv_sems, bufs, shapes, BIG_KINDS, token if after is None else after, name=f"gather_wait_l{l}")
    full = dict(zip(BIG, _gather_forward(bufs, shapes, BIG_KINDS, name=f"gather_forward_l{l}")))
    _, r, c = full["w_in"].shape
    w_in = jnp.transpose(full.pop("w_in"), (1, 0, 2)).reshape(r, N_CHIPS * c)
    full["w_main"] = jnp.concatenate([w_in[:, :XBC_IN0], w_in[:, DT_COL1:], w_in[:, XBC_IN0:DT_COL0]], axis=1)
    full["w_dt"] = jnp.pad(w_in[:, DT_COL0:DT_COL1], ((0, 0), (0, HEAD_PAD - SSM_HEADS)))
    return full


def _layer_weights(a, big, small, l):
    w = dict(big)
    for n in SMALL_SHARDED:
        w[n] = small[n][l]
    for n in ["sg_ln_g", "sg_ln_b", "sg_w", "conv_b", "ssm_norm_g"]:
        w[n] = a[n][l]
    w["sg_bcol"] = a["sg_b"][l][..., None]
    for n in ["dt_bias", "a_log"]:
        w[n + "8"] = _pad_heads(a[n][l])
    w["d_skipx"] = _expand_heads(a["d_skip"][l])
    return w


GRAD_VIEWS = [GRAD_VIEW[n] for n in BIG]


def _pair_sums(grads, c_idx, *, tag):
    gs = []
    for n in BIG:
        axis, _ = BIG[n]
        r, c = _shard_shape(n)
        if n == "w_in":
            gm, gd = grads["w_main"], grads["w_dt"]
            gfull = jnp.concatenate([gm[:, :XBC_IN0], gm[:, XBC_COL0:], gd[:, :SSM_HEADS], gm[:, GAB_COL0:XBC_COL0]], axis=1)
            gs.append(jnp.transpose(gfull.reshape(r, N_CHIPS, c), (1, 0, 2)))
        elif axis == 0:
            gs.append(grads[n].reshape(N_CHIPS, r, c))
        else:
            gs.append(grads[n])
    recv = _grads_to_sibling(gs, GRAD_VIEWS, name=f"grads_to_sibling_{tag}")
    cpre = c_idx.reshape(1)
    pairs = [_pair_sum(g, rv, v, cpre, name=f"grads_pair_sum_{n}_{tag}") for g, rv, v, n in zip(gs, recv, GRAD_VIEWS, BIG)]
    return gs, recv, pairs


def _finish_big_grads(parts, quads, c_idx, chip):
    tots = [_quad_sum([parts[l][0][i] for l in range(DEPTH)], [parts[l][1][i] for l in range(DEPTH)],
                      [quads[l][i] for l in range(DEPTH)], GRAD_VIEWS[i], chip.reshape(1), c_idx.reshape(1),
                      name=f"grads_chip_sum_{n}") for i, n in enumerate(BIG)]
    others = _grads_share(tots, name="grads_share")
    return {n: (t, o) for n, t, o in zip(BIG, tots, others)}


def _reduce_small_grads(small, chip, c_idx):
    names = list(small)
    flat = jnp.concatenate([small[n].reshape(-1) for n in names])
    packed = _pad_rows(flat, LANE, SUBLANE)
    g8 = _all_gather8(packed, name="gather_small_grads")
    tot = _sum_devices(g8, packed, (2 * chip + c_idx).reshape(1), name="small_grads_sum").reshape(-1)
    out, off = {}, 0
    for n in names:
        sz = small[n].size
        full = tot[off:off + sz].reshape(small[n].shape)
        off += sz
        if n in SMALL_SHARDED:
            cs = SMALL_SHARDED[n][1] // N_CHIPS
            full = lax.dynamic_slice_in_dim(full, chip * cs, cs, axis=-1)
        out[n] = full
    return out


def kernel(x, mem, mem_ln_g, mem_ln_b, w_in, sg_ln_g, sg_ln_b, sg_w, sg_b, conv_w, conv_b, dt_bias, a_log, d_skip, ssm_norm_g, p_a, p_b, w_mix_o, w_xq, w_xkv, w_xo, w_ffn_in, w_ffn_out, ln_g, ln_b, loss_target, m_mem_ln_g, m_mem_ln_b, m_w_in, m_sg_ln_g, m_sg_ln_b, m_sg_w, m_sg_b, m_conv_w, m_conv_b, m_dt_bias, m_a_log, m_d_skip, m_ssm_norm_g, m_p_a, m_p_b, m_w_mix_o, m_w_xq, m_w_xkv, m_w_xo, m_w_ffn_in, m_w_ffn_out, m_ln_g, m_ln_b, v_mem_ln_g, v_mem_ln_b, v_w_in, v_sg_ln_g, v_sg_ln_b, v_sg_w, v_sg_b, v_conv_w, v_conv_b, v_dt_bias, v_a_log, v_d_skip, v_ssm_norm_g, v_p_a, v_p_b, v_w_mix_o, v_w_xq, v_w_xkv, v_w_xo, v_w_ffn_in, v_w_ffn_out, v_ln_g, v_ln_b):
    a = dict(zip(ARG_NAMES, (x, mem, mem_ln_g, mem_ln_b, w_in, sg_ln_g, sg_ln_b, sg_w, sg_b, conv_w, conv_b, dt_bias, a_log, d_skip, ssm_norm_g, p_a, p_b, w_mix_o, w_xq, w_xkv, w_xo, w_ffn_in, w_ffn_out, ln_g, ln_b, loss_target, m_mem_ln_g, m_mem_ln_b, m_w_in, m_sg_ln_g, m_sg_ln_b, m_sg_w, m_sg_b, m_conv_w, m_conv_b, m_dt_bias, m_a_log, m_d_skip, m_ssm_norm_g, m_p_a, m_p_b, m_w_mix_o, m_w_xq, m_w_xkv, m_w_xo, m_w_ffn_in, m_w_ffn_out, m_ln_g, m_ln_b, v_mem_ln_g, v_mem_ln_b, v_w_in, v_sg_ln_g, v_sg_ln_b, v_sg_w, v_sg_b, v_conv_w, v_conv_b, v_dt_bias, v_a_log, v_d_skip, v_ssm_norm_g, v_p_a, v_p_b, v_w_mix_o, v_w_xq, v_w_xkv, v_w_xo, v_w_ffn_in, v_w_ffn_out, v_ln_g, v_ln_b)))
    c_idx = lax.axis_index("c").astype(jnp.int32)
    chip = (2 * lax.axis_index("x") + lax.axis_index("y")).astype(jnp.int32)

    small = _gather_small_params(a, chip)
    flights = {0: _gather_layer_start(a, 0, chip)}

    def layer_weights(after, l):
        big = _gather_layer_finish(a, l, flights[l], after if l else None)
        if l + 1 < DEPTH:
            flights[l + 1] = _gather_layer_start(a, l + 1, chip, after=big["p_a"])
            big["w_dt"] = big["w_dt"] + flights[l + 1][3][0, 0].astype(MXU_DTYPE)
        return _layer_weights(a, big, small, l)

    layers = [functools.partial(layer_weights, l=l) for l in range(DEPTH)]
    parts, flight = [None] * DEPTH, {}

    def start_exchange(l, grads_l):
        parts[l] = _pair_sums(grads_l, c_idx, tag=f"l{l}")
        if l == 0:
            return None
        send_sems, recv_sems, pairs, lands, token = _grads_to_chips_start(parts[l][2], GRAD_VIEWS, name=f"grads_to_chips_start_l{l}")
        flight[l] = (send_sems, recv_sems, pairs, lands)
        return token

    lsum, grad_x, grads, d_mem_g, d_mem_b = _local_step(x, mem, loss_target, mem_ln_g, mem_ln_b, layers, start_exchange)
    loss = lax.psum(0.5 * jnp.sum(lsum) / D_MODEL, ("x", "y", "c"))

    quads = [None] * DEPTH
    for l, (send_sems, recv_sems, pairs, lands) in flight.items():
        quads[l] = _grads_to_chips_wait(send_sems, recv_sems, pairs, lands, GRAD_VIEWS, grad_x, name=f"grads_to_chips_wait_l{l}")
    quads[0] = _grads_to_chips(parts[0][2], GRAD_VIEWS, name="grads_to_chips_l0")
    halves = _finish_big_grads(parts, quads, c_idx, chip)
    gw = {}
    small = {"mem_ln_g": d_mem_g, "mem_ln_b": d_mem_b}
    for n in SMALL:
        if n in small:
            continue
        per_layer = []
        for l in range(DEPTH):
            g = grads[l][n]
            if n in ("dt_bias", "a_log", "d_skip"):
                g = g[0, :SSM_HEADS]
            per_layer.append(g.reshape(a[n].shape[1:-1] + (-1,)))
        small[n] = jnp.stack(per_layer)
    gw.update(_reduce_small_grads(small, chip, c_idx))

    delta, new_m, new_v = {}, {}, {}
    for n in BIG:
        mine, other = halves[n]
        gw[n], delta[n], new_m[n], new_v[n] = _adamw_halves(a[n], a["m_" + n], a["v_" + n], mine, other, c_idx.reshape(1),
                                                             name=f"adamw_{n}")
    packs = [_pad_rows(jnp.concatenate([src(n).reshape(-1) for n in SMALL]), LANE, SUBLANE)
             for src in (lambda n: a[n], lambda n: gw[n], lambda n: a["m_" + n], lambda n: a["v_" + n])]
    outs = _adamw(*packs, name="adamw_small")
    off = 0
    for n in SMALL:
        sz, shp = a[n].size, a[n].shape
        delta[n], new_m[n], new_v[n] = (o.reshape(-1)[off:off + sz].reshape(shp) for o in outs)
        off += sz
    return (loss, grad_x, *[gw[n].reshape(a[n].shape) for n in WEIGHTS], *[delta[n] for n in WEIGHTS],
            *[new_m[n] for n in WEIGHTS], *[new_v[n] for n in WEIGHTS])
```

```python
import functools
import math

import jax
import jax.numpy as jnp
from jax import lax
from jax.experimental import pallas as pl
from jax.experimental.pallas import tpu as pltpu

F32 = jnp.float32
MXU_DTYPE = jnp.bfloat16
WIRE_DTYPE = jnp.bfloat16

D_MODEL = 1024
DEPTH = 2
CHUNK = 128
SG_GROUPS = 8
SSM_INNER = 2048
SSM_HEADDIM = 64
SSM_HEADS = 32
SSM_STATE = 128
SSM_GROUPS = 4
SSM_CONV = 4
SSM_CONV_DIM = 3072
X_HEADS = 4
X_HEADDIM = 256
FFN_HIDDEN = 2816
ALPHA = float((2 * DEPTH) ** 0.25)
LN_EPS = 1e-5
RMS_EPS = 1e-5
ADAM_LR = 0.001
ADAM_B1 = 0.9
ADAM_B2 = 0.999
ADAM_EPS = 1e-08
ADAM_WD = 0.01
ADAM_STEP = 10

MAIN_COLS = 9216
UVZ_COLS = 4096
GAB_COL0 = 4096
XBC_COL0 = 6144
HEAD_PAD = 128

VMEM_LIMIT = 56 * 1024 * 1024
BLOCK_BYTES = 2 * 1024 * 1024
ROW_TILES = (512, 256, 128)
LANE = 128
SUBLANE = 8

N_CHIPS = 4
N_DEV = 8


def _pick(n, cands):
    for c in cands:
        if n % c == 0:
            return c
    return n


MM_TILE_MAX = 1408
MM_OPERAND_BYTES = 8 * 1024 * 1024


def _div_tile(n, limit):
    best = None
    for t in range(LANE, min(n, limit) + 1, LANE):
        if n % t == 0:
            best = t
    return n if best is None else best


def _params(*sem):
    return pltpu.CompilerParams(dimension_semantics=tuple(sem), vmem_limit_bytes=VMEM_LIMIT)


_ANY = pl.BlockSpec(memory_space=pl.ANY)
_MESH = pl.DeviceIdType.MESH


def _nt(a, b):
    return lax.dot_general(a, b, (((1,), (1,)), ((), ())), preferred_element_type=F32)


def _tn(a, b):
    return lax.dot_general(a, b, (((0,), (0,)), ((), ())), preferred_element_type=F32)


def _nn(a, b):
    return jnp.dot(a, b, preferred_element_type=F32)


def _sigmoid(x):
    return 0.5 * jnp.tanh(0.5 * x) + 0.5


def _split3(v):
    def top(x):
        bits = lax.bitcast_convert_type(x, jnp.uint32) & jnp.uint32(0xFFFF0000)
        return lax.bitcast_convert_type(bits, F32)

    v1 = top(v)
    r1 = v - v1
    v2 = top(r1)
    v3 = r1 - v2
    return v1.astype(jnp.bfloat16), v2.astype(jnp.bfloat16), v3.astype(jnp.bfloat16)


def _dot_exact(a, b, dn, data):
    if data == 0:
        mat = b.astype(jnp.bfloat16)
        return sum(lax.dot_general(p, mat, dn, preferred_element_type=F32) for p in _split3(a))
    mat = a.astype(jnp.bfloat16)
    return sum(lax.dot_general(mat, p, dn, preferred_element_type=F32) for p in _split3(b))


_DN_NN = (((1,), (0,)), ((), ()))
_DN_TN = (((0,), (0,)), ((), ()))


def _gelu(x):
    return 0.5 * x * (1.0 + lax.erf(x * (2.0 ** -0.5)))


def _gelu_grad(x):
    return 0.5 * (1.0 + lax.erf(x * (2.0 ** -0.5))) + x * jnp.exp(-0.5 * x * x) * (1.0 / math.sqrt(2.0 * math.pi))


def _mm(a, b, *, ta=False, tb=False, out_dtype=F32, name):
    b, bl = b if isinstance(b, tuple) else (b, None)
    if ta:
        kdim, m = a.shape
    else:
        m, kdim = a.shape
    if tb:
        n, k2 = b.shape[-2:]
    else:
        k2, n = b.shape[-2:]
    assert kdim == k2, (a.shape, b.shape, ta, tb)
    tm = _div_tile(m, MM_TILE_MAX)
    tn = _div_tile(n, MM_TILE_MAX)
    tk = _div_tile(kdim, MM_OPERAND_BYTES // (tm * a.dtype.itemsize + tn * b.dtype.itemsize))
    nk = kdim // tk
    dn = (((0 if ta else 1,), (1 if tb else 0,)), ((), ()))

    def body(a_ref, b_ref, o_ref, *scratch):
        d = lax.dot_general(a_ref[...].astype(MXU_DTYPE), b_ref[...].astype(MXU_DTYPE), dn, preferred_element_type=F32)
        if nk == 1:
            o_ref[...] = d.astype(out_dtype)
            return
        acc_ref, = scratch
        k = pl.program_id(2)

        @pl.when(k == 0)
        def _():
            acc_ref[...] = d

        @pl.when(jnp.logical_and(k > 0, k < nk - 1))
        def _():
            acc_ref[...] += d

        @pl.when(k == nk - 1)
        def _():
            o_ref[...] = (acc_ref[...] + d).astype(out_dtype)

    a_spec = pl.BlockSpec((tk, tm), lambda i, j, k: (k, i)) if ta else pl.BlockSpec((tm, tk), lambda i, j, k: (i, k))
    if bl is None:
        b_spec = pl.BlockSpec((tn, tk), lambda i, j, k: (j, k)) if tb else pl.BlockSpec((tk, tn), lambda i, j, k: (k, j))
    elif tb:
        b_spec = pl.BlockSpec((None, tn, tk), lambda i, j, k: (bl, j, k))
    else:
        b_spec = pl.BlockSpec((None, tk, tn), lambda i, j, k: (bl, k, j))
    return pl.pallas_call(
        body, grid=(m // tm, n // tn, nk), in_specs=[a_spec, b_spec],
        out_specs=pl.BlockSpec((tm, tn), lambda i, j, k: (i, j)),
        out_shape=jax.ShapeDtypeStruct((m, n), out_dtype),
        scratch_shapes=[pltpu.VMEM((tm, tn), F32)] if nk > 1 else [],
        compiler_params=_params("parallel", "parallel", "arbitrary"), name=name)(a, b)


def _row_spec(tm, c, col=0):
    return pl.BlockSpec((tm, c), lambda i: (i, col))


def _par_spec(shape):
    nd = len(shape)
    return pl.BlockSpec(shape, lambda i: (0,) * nd)


def _ln_fwd(x, f, g, b, *, name):
    t, c = x.shape
    tm = _pick(t, ROW_TILES)
    has_f = f is not None

    def body(*refs):
        if has_f:
            x_ref, f_ref, g_ref, b_ref, y_ref, yb_ref, xh_ref, rs_ref = refs
            r = ALPHA * x_ref[...] + f_ref[...]
        else:
            x_ref, g_ref, b_ref, y_ref, yb_ref, xh_ref, rs_ref = refs
            r = x_ref[...]
        mu = jnp.mean(r, axis=-1, keepdims=True)
        xc = r - mu
        var = jnp.mean(xc * xc, axis=-1, keepdims=True)
        rstd = lax.rsqrt(var + LN_EPS)
        xh = xc * rstd
        y = xh * g_ref[...] + b_ref[...]
        y_ref[...] = y
        yb_ref[...] = y.astype(MXU_DTYPE)
        xh_ref[...] = xh
        rs_ref[...] = jnp.broadcast_to(rstd, rs_ref.shape)

    ins = [x] + ([f] if has_f else []) + [g.reshape(1, c), b.reshape(1, c)]
    in_specs = [_row_spec(tm, c)] * (2 if has_f else 1) + [_par_spec((1, c))] * 2
    return pl.pallas_call(
        body, grid=(t // tm,), in_specs=in_specs,
        out_specs=[_row_spec(tm, c), _row_spec(tm, c), _row_spec(tm, c), _row_spec(tm, LANE)],
        out_shape=[jax.ShapeDtypeStruct((t, c), F32), jax.ShapeDtypeStruct((t, c), MXU_DTYPE),
                   jax.ShapeDtypeStruct((t, c), F32), jax.ShapeDtypeStruct((t, LANE), F32)],
        compiler_params=_params("parallel"), name=name)(*ins)


def _ln_bwd(addends, scales, xh, rs, g, *, name):
    t, c = xh.shape
    tm = _pick(t, ROW_TILES)
    na = len(addends)

    def body(*refs):
        a_refs = refs[:na]
        xh_ref, rs_ref, g_ref, dp_ref, dpb_ref, dg_ref, db_ref = refs[na:]

        @pl.when(pl.program_id(0) == 0)
        def _():
            dg_ref[...] = jnp.zeros_like(dg_ref)
            db_ref[...] = jnp.zeros_like(db_ref)

        dy = None
        for s, r in zip(scales, a_refs):
            term = r[...] if s == 1.0 else s * r[...]
            dy = term if dy is None else dy + term
        xhv = xh_ref[...]
        dxh = dy * g_ref[...]
        m1 = jnp.mean(dxh, axis=-1, keepdims=True)
        m2 = jnp.mean(dxh * xhv, axis=-1, keepdims=True)
        dp = rs_ref[:, 0:1] * (dxh - m1 - xhv * m2)
        dp_ref[...] = dp
        dpb_ref[...] = dp.astype(MXU_DTYPE)
        dg_ref[...] += jnp.sum(dy * xhv, axis=0, keepdims=True)
        db_ref[...] += jnp.sum(dy, axis=0, keepdims=True)

    in_specs = [_row_spec(tm, c)] * (na + 1) + [_row_spec(tm, LANE), _par_spec((1, c))]
    return pl.pallas_call(
        body, grid=(t // tm,), in_specs=in_specs,
        out_specs=[_row_spec(tm, c), _row_spec(tm, c), _par_spec((1, c)), _par_spec((1, c))],
        out_shape=[jax.ShapeDtypeStruct((t, c), F32), jax.ShapeDtypeStruct((t, c), MXU_DTYPE),
                   jax.ShapeDtypeStruct((1, c), F32), jax.ShapeDtypeStruct((1, c), F32)],
        compiler_params=_params("arbitrary"), name=name)(*addends, xh, rs, g.reshape(1, c))


def _add_scaled(addends, scales, *, name):
    t, c = addends[0].shape
    tm = _pick(t, ROW_TILES)
    na = len(addends)

    def body(*refs):
        acc = None
        for s, r in zip(scales, refs[:na]):
            term = r[...] if s == 1.0 else s * r[...]
            acc = term if acc is None else acc + term
        refs[na][...] = acc

    return pl.pallas_call(
        body, grid=(t // tm,), in_specs=[_row_spec(tm, c)] * na, out_specs=_row_spec(tm, c),
        out_shape=jax.ShapeDtypeStruct((t, c), F32), compiler_params=_params("parallel"), name=name)(*addends)


def _loss_head(y, tgt, *, name):
    t, c = y.shape
    tm = _pick(t, ROW_TILES)

    def body(y_ref, t_ref, dy_ref, ls_ref):
        @pl.when(pl.program_id(0) == 0)
        def _():
            ls_ref[...] = jnp.zeros_like(ls_ref)

        e = y_ref[...] - t_ref[...]
        dy_ref[...] = e * (1.0 / c)
        ls_ref[...] += jnp.sum(e * e, axis=0, keepdims=True)

    return pl.pallas_call(
        body, grid=(t // tm,), in_specs=[_row_spec(tm, c)] * 2,
        out_specs=[_row_spec(tm, c), _par_spec((1, c))],
        out_shape=[jax.ShapeDtypeStruct((t, c), F32), jax.ShapeDtypeStruct((1, c), F32)],
        compiler_params=_params("arbitrary"), name=name)(y, tgt)


def _swiglu_fwd(h, *, name):
    t, two_f = h.shape
    fh = two_f // 2
    tm = _pick(t, (256, 128))

    def body(g_ref, u_ref, a_ref):
        g = g_ref[...]
        a_ref[...] = (g * _sigmoid(g) * u_ref[...]).astype(MXU_DTYPE)

    return pl.pallas_call(
        body, grid=(t // tm,), in_specs=[_row_spec(tm, fh, 0), _row_spec(tm, fh, 1)], out_specs=_row_spec(tm, fh),
        out_shape=jax.ShapeDtypeStruct((t, fh), MXU_DTYPE), compiler_params=_params("parallel"), name=name)(h, h)


def _swiglu_bwd(h, da, *, name):
    t, two_f = h.shape
    fh = two_f // 2
    tm = _pick(t, (256, 128))

    def body(g_ref, u_ref, da_ref, dh_ref):
        g = g_ref[...]
        s = _sigmoid(g)
        dav = da_ref[...]
        dh_ref[:, :fh] = (dav * u_ref[...] * (s * (1.0 + g * (1.0 - s)))).astype(MXU_DTYPE)
        dh_ref[:, fh:] = (dav * g * s).astype(MXU_DTYPE)

    return pl.pallas_call(
        body, grid=(t // tm,), in_specs=[_row_spec(tm, fh, 0), _row_spec(tm, fh, 1), _row_spec(tm, fh)],
        out_specs=_row_spec(tm, two_f), out_shape=jax.ShapeDtypeStruct((t, two_f), MXU_DTYPE),
        compiler_params=_params("parallel"), name=name)(h, h, da)


def _attn_probs(q, k):
    s = _nt(q, k) * (X_HEADDIM ** -0.5)
    s = s - jnp.max(s, axis=-1, keepdims=True)
    p = jnp.exp(s)
    return p / jnp.sum(p, axis=-1, keepdims=True)


def _attn_fwd(q, kv, *, bsz, name):
    t = q.shape[0]
    s = t // bsz
    ml = kv.shape[0] // bsz
    hd = X_HEADDIM

    def body(q_ref, k_ref, v_ref, o_ref):
        p = _attn_probs(q_ref[...], k_ref[...])
        o_ref[...] = _nn(p.astype(MXU_DTYPE), v_ref[...]).astype(MXU_DTYPE)

    return pl.pallas_call(
        body, grid=(bsz, X_HEADS),
        in_specs=[pl.BlockSpec((s, hd), lambda b, h: (b, h)), pl.BlockSpec((ml, hd), lambda b, h: (b, h)),
                  pl.BlockSpec((ml, hd), lambda b, h: (b, X_HEADS + h))],
        out_specs=pl.BlockSpec((s, hd), lambda b, h: (b, h)),
        out_shape=jax.ShapeDtypeStruct((t, D_MODEL), MXU_DTYPE),
        compiler_params=_params("parallel", "parallel"), name=name)(q, kv, kv)


def _attn_bwd(q, kv, do, *, bsz, name):
    t = q.shape[0]
    s = t // bsz
    ml = kv.shape[0] // bsz
    hd = X_HEADDIM

    def body(q_ref, k_ref, v_ref, do_ref, dq_ref, dk_ref, dv_ref):
        qv, kk, vv, dov = q_ref[...], k_ref[...], v_ref[...], do_ref[...]
        p = _attn_probs(qv, kk)
        dp = _nt(dov, vv)
        dv_ref[...] = _tn(p.astype(MXU_DTYPE), dov).astype(MXU_DTYPE)
        ds = (p * (dp - jnp.sum(dp * p, axis=-1, keepdims=True)) * (X_HEADDIM ** -0.5)).astype(MXU_DTYPE)
        dq_ref[...] = _nn(ds, kk).astype(MXU_DTYPE)
        dk_ref[...] = _tn(ds, qv).astype(MXU_DTYPE)

    blk_q = pl.BlockSpec((s, hd), lambda b, h: (b, h))
    blk_m = pl.BlockSpec((ml, hd), lambda b, h: (b, h))
    return pl.pallas_call(
        body, grid=(bsz, X_HEADS),
        in_specs=[blk_q, blk_m, pl.BlockSpec((ml, hd), lambda b, h: (b, X_HEADS + h)), blk_q],
        out_specs=[blk_q, blk_m, blk_m],
        out_shape=[jax.ShapeDtypeStruct((t, D_MODEL), MXU_DTYPE), jax.ShapeDtypeStruct((bsz * ml, D_MODEL), MXU_DTYPE),
                   jax.ShapeDtypeStruct((bsz * ml, D_MODEL), MXU_DTYPE)],
        compiler_params=_params("parallel", "parallel"), name=name)(q, kv, kv, do)


def _causal(n):
    row = lax.broadcasted_iota(jnp.int32, (n, n), 0)
    col = lax.broadcasted_iota(jnp.int32, (n, n), 1)
    return row >= col


def _sg_norm(v, g, b):
    gv = _gelu(v)
    mu = jnp.mean(gv, axis=-1, keepdims=True)
    xc = gv - mu
    var = jnp.mean(xc * xc, axis=-1, keepdims=True)
    rstd = lax.rsqrt(var + LN_EPS)
    xh = xc * rstd
    return xh, rstd, xh * g + b


def _sg_fwd(proj, ln_g, ln_b, w, bcol, *, name):
    t = proj.shape[0]
    c = D_MODEL
    gd = c // SG_GROUPS

    def body(u_ref, v_ref, g_ref, b_ref, w_ref, bc_ref, o_ref):
        gu = _gelu(u_ref[...])
        _, _, vn = _sg_norm(v_ref[...], g_ref[...], b_ref[...])
        mask = _causal(CHUNK)
        for g in range(SG_GROUPS):
            sl = slice(g * gd, (g + 1) * gd)
            wg = jnp.where(mask, w_ref[g], 0.0).astype(MXU_DTYPE)
            mixed = _nn(wg, vn[:, sl].astype(MXU_DTYPE)) + bc_ref[g]
            o_ref[:, sl] = (gu[:, sl] * mixed).astype(MXU_DTYPE)

    return pl.pallas_call(
        body, grid=(t // CHUNK,),
        in_specs=[_row_spec(CHUNK, c, 0), _row_spec(CHUNK, c, 1), _par_spec((1, c)), _par_spec((1, c)),
                  _par_spec((SG_GROUPS, CHUNK, CHUNK)), _par_spec((SG_GROUPS, CHUNK, 1))],
        out_specs=_row_spec(CHUNK, c), out_shape=jax.ShapeDtypeStruct((t, c), MXU_DTYPE),
        compiler_params=_params("parallel"), name=name)(proj, proj, ln_g.reshape(1, c), ln_b.reshape(1, c), w, bcol)


def _sg_bwd(proj, dsgo, ln_g, ln_b, w, bcol, dproj, *, name):
    t = proj.shape[0]
    c = D_MODEL
    gd = c // SG_GROUPS

    def body(u_ref, v_ref, d_ref, g_ref, b_ref, w_ref, bc_ref, _, duv_ref, dw_ref, dbc_ref, dg_ref, db_ref, dvn_ref):
        @pl.when(pl.program_id(0) == 0)
        def _():
            dw_ref[...] = jnp.zeros_like(dw_ref)
            dbc_ref[...] = jnp.zeros_like(dbc_ref)
            dg_ref[...] = jnp.zeros_like(dg_ref)
            db_ref[...] = jnp.zeros_like(db_ref)

        u = u_ref[...]
        v = v_ref[...]
        dso = d_ref[...]
        gu = _gelu(u)
        xh, rstd, vn = _sg_norm(v, g_ref[...], b_ref[...])
        mask = _causal(CHUNK)
        for g in range(SG_GROUPS):
            sl = slice(g * gd, (g + 1) * gd)
            wg = jnp.where(mask, w_ref[g], 0.0).astype(MXU_DTYPE)
            vng = vn[:, sl].astype(MXU_DTYPE)
            mixed = _nn(wg, vng) + bc_ref[g]
            duv_ref[:, sl] = (dso[:, sl] * mixed * _gelu_grad(u[:, sl])).astype(MXU_DTYPE)
            dmix = dso[:, sl] * gu[:, sl]
            dmb = dmix.astype(MXU_DTYPE)
            dbc_ref[g] += jnp.sum(dmix, axis=-1, keepdims=True)
            dw_ref[g] += jnp.where(mask, _nt(dmb, vng), 0.0)
            dvn_ref[:, sl] = _tn(wg, dmb)
        dvn = dvn_ref[...]
        dg_ref[...] += jnp.sum(dvn * xh, axis=0, keepdims=True)
        db_ref[...] += jnp.sum(dvn, axis=0, keepdims=True)
        dxh = dvn * g_ref[...]
        m1 = jnp.mean(dxh, axis=-1, keepdims=True)
        m2 = jnp.mean(dxh * xh, axis=-1, keepdims=True)
        dgv = rstd * (dxh - m1 - xh * m2)
        duv_ref[:, c:] = (dgv * _gelu_grad(v)).astype(MXU_DTYPE)

    return pl.pallas_call(
        body, grid=(t // CHUNK,),
        in_specs=[_row_spec(CHUNK, c, 0), _row_spec(CHUNK, c, 1), _row_spec(CHUNK, c), _par_spec((1, c)),
                  _par_spec((1, c)), _par_spec((SG_GROUPS, CHUNK, CHUNK)), _par_spec((SG_GROUPS, CHUNK, 1)), _ANY],
        out_specs=[_row_spec(CHUNK, 2 * c), _par_spec((SG_GROUPS, CHUNK, CHUNK)), _par_spec((SG_GROUPS, CHUNK, 1)),
                   _par_spec((1, c)), _par_spec((1, c))],
        out_shape=[jax.ShapeDtypeStruct(dproj.shape, dproj.dtype), jax.ShapeDtypeStruct((SG_GROUPS, CHUNK, CHUNK), F32),
                   jax.ShapeDtypeStruct((SG_GROUPS, CHUNK, 1), F32), jax.ShapeDtypeStruct((1, c), F32),
                   jax.ShapeDtypeStruct((1, c), F32)],
        scratch_shapes=[pltpu.VMEM((CHUNK, c), F32)], input_output_aliases={7: 0},
        compiler_params=_params("arbitrary"), name=name)(proj, proj, dsgo, ln_g.reshape(1, c), ln_b.reshape(1, c), w, bcol, dproj)


CONV_TC = 512


def _conv_taps(x):
    rows = lax.broadcasted_iota(jnp.int32, x.shape, 0)
    taps = [jnp.where(rows >= SSM_CONV - 1 - k, pltpu.roll(x, SSM_CONV - 1 - k, axis=0), 0.0) for k in range(SSM_CONV - 1)]
    return taps + [x]


def _conv_pre(taps, w_ref, b_ref):
    acc = b_ref[...]
    for k in range(SSM_CONV):
        acc = acc + taps[k] * w_ref[k:k + 1, :]
    return acc


def _conv_fwd(proj, w, b, *, bsz, name):
    t = proj.shape[0]
    s = t // bsz
    nj = SSM_CONV_DIM // CONV_TC
    c0 = XBC_COL0 // CONV_TC

    def body(x_ref, w_ref, b_ref, o_ref):
        pre = _conv_pre(_conv_taps(x_ref[...]), w_ref, b_ref)
        o_ref[...] = pre * _sigmoid(pre)

    return pl.pallas_call(
        body, grid=(bsz, nj),
        in_specs=[pl.BlockSpec((s, CONV_TC), lambda bb, j: (bb, c0 + j)), pl.BlockSpec((SSM_CONV, CONV_TC), lambda bb, j: (0, j)),
                  pl.BlockSpec((1, CONV_TC), lambda bb, j: (0, j))],
        out_specs=pl.BlockSpec((s, CONV_TC), lambda bb, j: (bb, j)),
        out_shape=jax.ShapeDtypeStruct((t, SSM_CONV_DIM), F32),
        compiler_params=_params("parallel", "parallel"), name=name)(proj, w, b.reshape(1, -1))


def _conv_bwd(proj, dact, w, b, dproj, *, bsz, name):
    t = proj.shape[0]
    s = t // bsz
    nj = SSM_CONV_DIM // CONV_TC
    c0 = XBC_COL0 // CONV_TC

    def body(x_ref, d_ref, w_ref, b_ref, _, dx_ref, dw_ref, db_ref):
        @pl.when(pl.program_id(1) == 0)
        def _():
            dw_ref[...] = jnp.zeros_like(dw_ref)
            db_ref[...] = jnp.zeros_like(db_ref)

        taps = _conv_taps(x_ref[...])
        pre = _conv_pre(taps, w_ref, b_ref)
        sg = _sigmoid(pre)
        dpre = d_ref[...] * (sg * (1.0 + pre * (1.0 - sg)))
        rows = lax.broadcasted_iota(jnp.int32, dpre.shape, 0)
        db_ref[...] += jnp.sum(dpre, axis=0, keepdims=True)
        dx = dpre * w_ref[SSM_CONV - 1:SSM_CONV, :]
        for k in range(SSM_CONV):
            dw_ref[k:k + 1, :] += jnp.sum(dpre * taps[k], axis=0, keepdims=True)
        for k in range(SSM_CONV - 1):
            sh = SSM_CONV - 1 - k
            dsh = jnp.where(rows < s - sh, pltpu.roll(dpre, s - sh, axis=0), 0.0)
            dx = dx + dsh * w_ref[k:k + 1, :]
        dx_ref[...] = dx.astype(MXU_DTYPE)

    return pl.pallas_call(
        body, grid=(nj, bsz),
        in_specs=[pl.BlockSpec((s, CONV_TC), lambda j, bb: (bb, c0 + j)), pl.BlockSpec((s, CONV_TC), lambda j, bb: (bb, j)),
                  pl.BlockSpec((SSM_CONV, CONV_TC), lambda j, bb: (0, j)), pl.BlockSpec((1, CONV_TC), lambda j, bb: (0, j)), _ANY],
        out_specs=[pl.BlockSpec((s, CONV_TC), lambda j, bb: (bb, c0 + j)), pl.BlockSpec((SSM_CONV, CONV_TC), lambda j, bb: (0, j)),
                   pl.BlockSpec((1, CONV_TC), lambda j, bb: (0, j))],
        out_shape=[jax.ShapeDtypeStruct(dproj.shape, dproj.dtype), jax.ShapeDtypeStruct((SSM_CONV, SSM_CONV_DIM), F32),
                   jax.ShapeDtypeStruct((1, SSM_CONV_DIM), F32)],
        input_output_aliases={4: 0},
        compiler_params=_params("parallel", "arbitrary"), name=name)(proj, dact, w, b.reshape(1, -1), dproj)


def _softplus(x):
    return jnp.maximum(x, 0.0) + jnp.log1p(jnp.exp(-jnp.abs(x)))


def _pad_heads(v):
    return jnp.broadcast_to(jnp.pad(v.astype(F32), (0, HEAD_PAD - SSM_HEADS))[None, :], (SUBLANE, HEAD_PAD))


def _ssd_prep(dt_raw, dt_bias8, a_log8, *, name):
    t = dt_raw.shape[0]
    n = CHUNK

    def body(r_ref, b_ref, al_ref, dt_ref, cs_ref, dtt_ref, cst_ref):
        dt = _softplus(r_ref[...] + b_ref[0:1, :])
        da = dt * (-jnp.exp(al_ref[0:1, :]))
        row = lax.broadcasted_iota(jnp.int32, (n, n), 0)
        col = lax.broadcasted_iota(jnp.int32, (n, n), 1)
        lower = (col <= row).astype(F32)
        upper = (row <= col).astype(F32)
        eye = (row == col).astype(F32)
        dt_ref[...] = dt
        cs_ref[...] = _dot_exact(lower, da, _DN_NN, 1)
        cst_ref[0] = _dot_exact(da, upper, _DN_TN, 0)
        dtt_ref[0] = _dot_exact(dt, eye, _DN_TN, 0)

    hp = HEAD_PAD
    return pl.pallas_call(
        body, grid=(t // n,),
        in_specs=[_row_spec(n, hp), _par_spec((SUBLANE, hp)), _par_spec((SUBLANE, hp))],
        out_specs=[_row_spec(n, hp), _row_spec(n, hp), pl.BlockSpec((1, hp, n), lambda i: (i, 0, 0)),
                   pl.BlockSpec((1, hp, n), lambda i: (i, 0, 0))],
        out_shape=[jax.ShapeDtypeStruct((t, hp), F32), jax.ShapeDtypeStruct((t, hp), F32),
                   jax.ShapeDtypeStruct((t // n, hp, n), F32), jax.ShapeDtypeStruct((t // n, hp, n), F32)],
        compiler_params=_params("parallel"), name=name)(dt_raw, dt_bias8, a_log8)


def _expand_mat():
    h = lax.broadcasted_iota(jnp.int32, (HEAD_PAD, SSM_INNER), 0)
    ch = lax.broadcasted_iota(jnp.int32, (HEAD_PAD, SSM_INNER), 1)
    return (ch // SSM_HEADDIM == h).astype(F32)


def _reduce_mat():
    ch = lax.broadcasted_iota(jnp.int32, (SSM_INNER, HEAD_PAD), 0)
    h = lax.broadcasted_iota(jnp.int32, (SSM_INNER, HEAD_PAD), 1)
    return (ch // SSM_HEADDIM == h).astype(F32)


def _expand(v, em):
    return _dot_exact(v, em, _DN_NN, 0)


def _expand_heads(v):
    return jnp.repeat(v.astype(F32), SSM_HEADDIM)[None, :]


def _decay_mat(cs_ref, cst_ref, h, mask):
    seg = cs_ref[:, h:h + 1] - cst_ref[0, h:h + 1, :]
    return jnp.where(mask, jnp.exp(jnp.minimum(seg, 0.0)), 0.0)


GROUP_CH = SSM_INNER // SSM_GROUPS
PAIRS_PER_GROUP = GROUP_CH // LANE
HEADS_PER_GROUP = SSM_HEADS // SSM_GROUPS
BM_COL0 = SSM_INNER
CM_COL0 = SSM_INNER + SSM_GROUPS * SSM_STATE


def _ssd_specs(nc, rev):
    def cidx(i):
        return (i // nc) * nc + (nc - 1 - i % nc) if rev else i

    n = CHUNK
    xs = pl.BlockSpec((n, SSM_INNER), lambda i: (cidx(i), 0))
    bm = pl.BlockSpec((n, GROUP_CH), lambda i: (cidx(i), BM_COL0 // GROUP_CH))
    cm = pl.BlockSpec((n, GROUP_CH), lambda i: (cidx(i), CM_COL0 // GROUP_CH))
    hv = pl.BlockSpec((n, HEAD_PAD), lambda i: (cidx(i), 0))
    hvt = pl.BlockSpec((1, HEAD_PAD, n), lambda i: (cidx(i), 0, 0))
    st = pl.BlockSpec((1, SSM_INNER, SSM_STATE), lambda i: (cidx(i), 0, 0))
    return xs, bm, cm, hv, hvt, st


def _ssd_fwd(xbc, dt, cs, dtt, cst, dskx, *, nc, name):
    t = xbc.shape[0]
    n = CHUNK
    xs_s, bm_s, cm_s, hv_s, hvt_s, st_s = _ssd_specs(nc, False)

    def body(xs_ref, bm_ref, cm_ref, dt_ref, cs_ref, dtt_ref, cst_ref, dsk_ref, y_ref, st_ref, prev):
        @pl.when(pl.program_id(0) % nc == 0)
        def _():
            prev[...] = jnp.zeros_like(prev)

        st_ref[0] = prev[...]
        em = _expand_mat()
        dtx = _expand(dt_ref[...], em)
        csx = _expand(cs_ref[...], em)
        dskx = dsk_ref[...]
        xs = xs_ref[...]
        xdt = xs * dtx
        ecs = jnp.exp(csx)
        dec = jnp.exp(csx[n - 1:n, :] - csx)
        mask = _causal(n)
        lane = lax.broadcasted_iota(jnp.int32, (n, LANE), 1)
        for g in range(SSM_GROUPS):
            gs = slice(g * SSM_STATE, (g + 1) * SSM_STATE)
            gc = slice(g * GROUP_CH, (g + 1) * GROUP_CH)
            cmat = cm_ref[:, gs].astype(MXU_DTYPE)
            bmat = bm_ref[:, gs].astype(MXU_DTYPE)
            cb = _nt(cmat, bmat)
            yoff = ecs[:, gc] * _nt(cmat, prev[gc, :].astype(MXU_DTYPE))
            for q in range(PAIRS_PER_GROUP):
                hp = g * PAIRS_PER_GROUP + q
                sl = slice(hp * LANE, (hp + 1) * LANE)
                xp = xdt[:, sl].astype(MXU_DTYPE)
                m0 = (cb * _decay_mat(cs_ref, cst_ref, 2 * hp, mask)).astype(MXU_DTYPE)
                m1 = (cb * _decay_mat(cs_ref, cst_ref, 2 * hp + 1, mask)).astype(MXU_DTYPE)
                yd = jnp.where(lane < SSM_HEADDIM, _nn(m0, xp), _nn(m1, xp))
                y_ref[:, sl] = yd + yoff[:, q * LANE:(q + 1) * LANE] + xs[:, sl] * dskx[:, sl]
            snew = _tn((xdt[:, gc] * dec[:, gc]).astype(MXU_DTYPE), bmat)
            for r in range(HEADS_PER_GROUP):
                h = g * HEADS_PER_GROUP + r
                rows = slice(h * SSM_HEADDIM, (h + 1) * SSM_HEADDIM)
                e = jnp.exp(cst_ref[0, h:h + 1, n - 1:n])
                prev[rows, :] = prev[rows, :] * e + snew[r * SSM_HEADDIM:(r + 1) * SSM_HEADDIM, :]

    return pl.pallas_call(
        body, grid=(t // n,),
        in_specs=[xs_s, bm_s, cm_s, hv_s, hv_s, hvt_s, hvt_s, _par_spec((1, SSM_INNER))],
        out_specs=[xs_s, st_s],
        out_shape=[jax.ShapeDtypeStruct((t, SSM_INNER), F32), jax.ShapeDtypeStruct((t // n, SSM_INNER, SSM_STATE), F32)],
        scratch_shapes=[pltpu.VMEM((SSM_INNER, SSM_STATE), F32)],
        compiler_params=_params("arbitrary"), name=name)(xbc, xbc, xbc, dt, cs, dtt, cst, dskx)


def _ssd_bwd(dy, xbc, dt, cs, dtt, cst, st, dskx, a_log8, dt_raw, dt_bias8, *, nc, name):
    t = xbc.shape[0]
    n = CHUNK
    xs_s, bm_s, cm_s, hv_s, hvt_s, st_s = _ssd_specs(nc, True)
    acc_s = _par_spec((1, HEAD_PAD))
    xbc_s = pl.BlockSpec((n, SSM_CONV_DIM), xs_s.index_map)

    def body(dy_ref, xs_ref, bm_ref, cm_ref, dt_ref, cs_ref, dtt_ref, cst_ref, st_ref, dsk_ref, al_ref, raw_ref, bias_ref,
             dxbc_ref, ddr_ref, dal_ref, dds_ref, dbias_ref, dprev, dxdt_s, tdec_s, tcs_s):
        @pl.when(pl.program_id(0) % nc == 0)
        def _():
            dprev[...] = jnp.zeros_like(dprev)

        @pl.when(pl.program_id(0) == 0)
        def _():
            dal_ref[...] = jnp.zeros_like(dal_ref)
            dds_ref[...] = jnp.zeros_like(dds_ref)
            dbias_ref[...] = jnp.zeros_like(dbias_ref)

        em = _expand_mat()
        rm = _reduce_mat()

        def head_reduce(v):
            return _dot_exact(v, rm, _DN_NN, 0)

        dtv = dt_ref[...]
        csv = cs_ref[...]
        dtx = _expand(dtv, em)
        csx = _expand(csv, em)
        dskx = dsk_ref[...]
        xs = xs_ref[...]
        dyv = dy_ref[...]
        xdt = xs * dtx
        ecs = jnp.exp(csx)
        dec = jnp.exp(csx[n - 1:n, :] - csx)
        mask = _causal(n)
        lane = lax.broadcasted_iota(jnp.int32, (n, LANE), 1)
        hlane = lax.broadcasted_iota(jnp.int32, (1, HEAD_PAD), 1)
        hsub = lax.broadcasted_iota(jnp.int32, (HEAD_PAD, 1), 0)
        rsum = jnp.zeros((n, HEAD_PAD), F32)
        csum = jnp.zeros((HEAD_PAD, n), F32)
        for g in range(SSM_GROUPS):
            gs = slice(g * SSM_STATE, (g + 1) * SSM_STATE)
            gc = slice(g * GROUP_CH, (g + 1) * GROUP_CH)
            cmat = cm_ref[:, gs].astype(MXU_DTYPE)
            bmat = bm_ref[:, gs].astype(MXU_DTYPE)
            cb = _nt(cmat, bmat)
            pg = st_ref[0, gc, :].astype(MXU_DTYPE)
            dpg = dprev[gc, :]
            dpgb = dpg.astype(MXU_DTYPE)
            z = _nt(cmat, pg)
            dyg = dyv[:, gc]
            dz = (dyg * ecs[:, gc]).astype(MXU_DTYPE)
            dc = _nn(dz, pg)
            dprev_y = _tn(dz, cmat)
            tcs_s[:, gc] = dyg * z * ecs[:, gc]
            xd = xdt[:, gc] * dec[:, gc]
            wmat = _nt(bmat, dpgb)
            db = _nn(xd.astype(MXU_DTYPE), dpgb)
            tdec_s[:, gc] = wmat * xd
            dxdt_g = wmat * dec[:, gc]
            dcb = jnp.zeros((n, n), F32)
            for q in range(PAIRS_PER_GROUP):
                hp = g * PAIRS_PER_GROUP + q
                sl = slice(hp * LANE, (hp + 1) * LANE)
                xp = xdt[:, sl].astype(MXU_DTYPE)
                dyp = dyv[:, sl]
                dypb = dyp.astype(MXU_DTYPE)
                dxp = None
                for hh in range(2):
                    h = 2 * hp + hh
                    lm = _decay_mat(cs_ref, cst_ref, h, mask)
                    mine = (lane < SSM_HEADDIM) if hh == 0 else (lane >= SSM_HEADDIM)
                    dm = _nt(jnp.where(mine, dyp, 0.0).astype(MXU_DTYPE), xp)
                    dml = dm * lm
                    dcb = dcb + dml
                    gseg = dml * cb
                    rsum = rsum + jnp.sum(gseg, axis=1, keepdims=True) * (hlane == h).astype(F32)
                    csum = csum + (hsub == h).astype(F32) * jnp.sum(gseg, axis=0, keepdims=True)
                    dxh = _tn((cb * lm).astype(MXU_DTYPE), dypb)
                    dxp = dxh if dxp is None else jnp.where(mine, dxh, dxp)
                dxdt_s[:, sl] = dxdt_g[:, q * LANE:(q + 1) * LANE] + dxp
            dcbb = dcb.astype(MXU_DTYPE)
            dxbc_ref[:, CM_COL0 + g * SSM_STATE:CM_COL0 + (g + 1) * SSM_STATE] = dc + _nn(dcbb, bmat)
            dxbc_ref[:, BM_COL0 + g * SSM_STATE:BM_COL0 + (g + 1) * SSM_STATE] = db + _tn(dcbb, cmat)
            for r in range(HEADS_PER_GROUP):
                h = g * HEADS_PER_GROUP + r
                rows = slice(h * SSM_HEADDIM, (h + 1) * SSM_HEADDIM)
                lr = slice(r * SSM_HEADDIM, (r + 1) * SSM_HEADDIM)
                e = jnp.exp(cst_ref[0, h:h + 1, n - 1:n])
                dprev[rows, :] = dpg[lr, :] * e + dprev_y[lr, :]
            tq = _dot_exact(dpg * st_ref[0, gc, :], rm[gc, :], _DN_TN, 0)
            if g == 0:
                qsum = jnp.sum(tq, axis=0, keepdims=True)
            else:
                qsum = qsum + jnp.sum(tq, axis=0, keepdims=True)
        dxdt = dxdt_s[...]
        dxbc_ref[:, 0:SSM_INNER] = dxdt * dtx + dyv * dskx
        ddt = head_reduce(dxdt * xs)
        edec = head_reduce(tdec_s[...])
        ycs = head_reduce(tcs_s[...])
        row = lax.broadcasted_iota(jnp.int32, (n, HEAD_PAD), 0)
        extra = jnp.sum(edec, axis=0, keepdims=True) + qsum * jnp.exp(csv[n - 1:n, :])
        dcs = rsum - csum.T + ycs - edec + jnp.where(row == n - 1, extra, 0.0)
        r2 = lax.broadcasted_iota(jnp.int32, (n, n), 0)
        c2 = lax.broadcasted_iota(jnp.int32, (n, n), 1)
        dda = _dot_exact((c2 >= r2).astype(F32), dcs, _DN_NN, 1)
        a_row = -jnp.exp(al_ref[0:1, :])
        ddt = ddt + dda * a_row
        dal_ref[...] += jnp.sum(dda * dtv, axis=0, keepdims=True) * a_row
        dds_ref[...] += jnp.sum(head_reduce(dyv * xs), axis=0, keepdims=True)
        ddr = ddt * _sigmoid(raw_ref[...] + bias_ref[0:1, :])
        ddr_ref[...] = ddr
        dbias_ref[...] += jnp.sum(ddr, axis=0, keepdims=True)

    par8 = _par_spec((SUBLANE, HEAD_PAD))
    return pl.pallas_call(
        body, grid=(t // n,),
        in_specs=[xs_s, xs_s, bm_s, cm_s, hv_s, hv_s, hvt_s, hvt_s, st_s, _par_spec((1, SSM_INNER)), par8, hv_s, par8],
        out_specs=[xbc_s, hv_s, acc_s, acc_s, acc_s],
        out_shape=[jax.ShapeDtypeStruct((t, SSM_CONV_DIM), F32), jax.ShapeDtypeStruct((t, HEAD_PAD), F32),
                   jax.ShapeDtypeStruct((1, HEAD_PAD), F32), jax.ShapeDtypeStruct((1, HEAD_PAD), F32),
                   jax.ShapeDtypeStruct((1, HEAD_PAD), F32)],
        scratch_shapes=[pltpu.VMEM((SSM_INNER, SSM_STATE), F32), pltpu.VMEM((n, SSM_INNER), F32),
                        pltpu.VMEM((n, SSM_INNER), F32), pltpu.VMEM((n, SSM_INNER), F32)],
        compiler_params=_params("arbitrary"), name=name)(dy, xbc, xbc, xbc, dt, cs, dtt, cst, st, dskx, a_log8, dt_raw, dt_bias8)


def _gate_norm_fwd(y, proj, norm_g, *, name):
    t, c = y.shape
    tm = _pick(t, (256, 128))

    def body(y_ref, z_ref, g_ref, o_ref):
        z = z_ref[...]
        yz = y_ref[...] * z * _sigmoid(z)
        for g in range(SSM_GROUPS):
            gc = slice(g * GROUP_CH, (g + 1) * GROUP_CH)
            seg = yz[:, gc]
            r = lax.rsqrt(jnp.mean(seg * seg, axis=-1, keepdims=True) + RMS_EPS)
            o_ref[:, gc] = (seg * r * g_ref[:, gc]).astype(MXU_DTYPE)

    return pl.pallas_call(
        body, grid=(t // tm,), in_specs=[_row_spec(tm, c), _row_spec(tm, c, 1), _par_spec((1, c))],
        out_specs=_row_spec(tm, c), out_shape=jax.ShapeDtypeStruct((t, c), MXU_DTYPE),
        compiler_params=_params("parallel"), name=name)(y, proj, norm_g.reshape(1, c))


def _gate_norm_bwd(dyb, y, proj, norm_g, dproj, *, name):
    t, c = y.shape
    tm = _pick(t, (256, 128))

    def body(d_ref, y_ref, z_ref, g_ref, _, dy_ref, dz_ref, dg_ref):
        @pl.when(pl.program_id(0) == 0)
        def _():
            dg_ref[...] = jnp.zeros_like(dg_ref)

        z = z_ref[...]
        yv = y_ref[...]
        sz = _sigmoid(z)
        silu = z * sz
        yz = yv * silu
        dv = d_ref[...]
        for g in range(SSM_GROUPS):
            gc = slice(g * GROUP_CH, (g + 1) * GROUP_CH)
            seg = yz[:, gc]
            r = lax.rsqrt(jnp.mean(seg * seg, axis=-1, keepdims=True) + RMS_EPS)
            nrm = seg * r
            dn = dv[:, gc] * g_ref[:, gc]
            dg_ref[:, gc] += jnp.sum(dv[:, gc] * nrm, axis=0, keepdims=True)
            dyz = r * (dn - nrm * jnp.mean(dn * nrm, axis=-1, keepdims=True))
            dy_ref[:, gc] = dyz * silu[:, gc]
            dz_ref[:, gc] = (dyz * yv[:, gc] * (sz[:, gc] * (1.0 + z[:, gc] * (1.0 - sz[:, gc])))).astype(MXU_DTYPE)

    return pl.pallas_call(
        body, grid=(t // tm,), in_specs=[_row_spec(tm, c), _row_spec(tm, c), _row_spec(tm, c, 1), _par_spec((1, c)), _ANY],
        out_specs=[_row_spec(tm, c), _row_spec(tm, c, 1), _par_spec((1, c))],
        out_shape=[jax.ShapeDtypeStruct((t, c), F32), jax.ShapeDtypeStruct(dproj.shape, dproj.dtype),
                   jax.ShapeDtypeStruct((1, c), F32)],
        input_output_aliases={4: 1},
        compiler_params=_params("arbitrary"), name=name)(dyb, y, proj, norm_g.reshape(1, c), dproj)


GA_COLBLK = GAB_COL0 // D_MODEL


def _merge_fwd(br_a, br_b, proj, *, name):
    t, c = br_a.shape
    tm = _pick(t, ROW_TILES)

    def body(a_ref, b_ref, ga_ref, gb_ref, o_ref):
        o_ref[...] = (_sigmoid(ga_ref[...]) * a_ref[...] + _sigmoid(gb_ref[...]) * b_ref[...]).astype(MXU_DTYPE)

    return pl.pallas_call(
        body, grid=(t // tm,),
        in_specs=[_row_spec(tm, c), _row_spec(tm, c), _row_spec(tm, c, GA_COLBLK), _row_spec(tm, c, GA_COLBLK + 1)],
        out_specs=_row_spec(tm, c), out_shape=jax.ShapeDtypeStruct((t, c), MXU_DTYPE),
        compiler_params=_params("parallel"), name=name)(br_a, br_b, proj, proj)


def _merge_bwd(dm, br_a, br_b, proj, *, name):
    t, c = br_a.shape
    tm = _pick(t, ROW_TILES)

    def body(dm_ref, a_ref, b_ref, ga_ref, gb_ref, da_ref, db_ref, dg_ref):
        d = dm_ref[...]
        sa = _sigmoid(ga_ref[...])
        sb = _sigmoid(gb_ref[...])
        da_ref[...] = (d * sa).astype(MXU_DTYPE)
        db_ref[...] = (d * sb).astype(MXU_DTYPE)
        dg_ref[:, :c] = (d * a_ref[...] * sa * (1.0 - sa)).astype(MXU_DTYPE)
        dg_ref[:, c:] = (d * b_ref[...] * sb * (1.0 - sb)).astype(MXU_DTYPE)

    return pl.pallas_call(
        body, grid=(t // tm,),
        in_specs=[_row_spec(tm, c), _row_spec(tm, c), _row_spec(tm, c), _row_spec(tm, c, GA_COLBLK), _row_spec(tm, c, GA_COLBLK + 1)],
        out_specs=[_row_spec(tm, c), _row_spec(tm, c), _row_spec(tm, 2 * c, GAB_COL0 // (2 * c))],
        out_shape=[jax.ShapeDtypeStruct((t, c), MXU_DTYPE), jax.ShapeDtypeStruct((t, c), MXU_DTYPE),
                   jax.ShapeDtypeStruct((t, MAIN_COLS), MXU_DTYPE)],
        compiler_params=_params("parallel"), name=name)(dm, br_a, br_b, proj, proj)


def _layer_fwd(x, xb, memn_b, w, *, bsz, tag):
    nc = x.shape[0] // bsz // CHUNK
    sv = {"x_in": xb}
    proj = _mm(xb, w["w_main"], name=f"{tag}_proj")
    dt_raw = _mm(xb, w["w_dt"], name=f"{tag}_dtproj")
    sgo = _sg_fwd(proj, w["sg_ln_g"], w["sg_ln_b"], w["sg_w"], w["sg_bcol"], name=f"{tag}_sg_fwd")
    xbc = _conv_fwd(proj, w["conv_w"], w["conv_b"], bsz=bsz, name=f"{tag}_conv_fwd")
    dt, cs, dtt, cst = _ssd_prep(dt_raw, w["dt_bias8"], w["a_log8"], name=f"{tag}_ssd_prep")
    y, st = _ssd_fwd(xbc, dt, cs, dtt, cst, w["d_skipx"], nc=nc, name=f"{tag}_ssd_fwd")
    yb = _gate_norm_fwd(y, proj, w["ssm_norm_g"], name=f"{tag}_gate_norm_fwd")
    if "rest" in w:
        w = w["rest"](w, yb)
    br_a = _mm(sgo, w["p_a"], name=f"{tag}_br_a")
    br_b = _mm(yb, w["p_b"], name=f"{tag}_br_b")
    merged = _merge_fwd(br_a, br_b, proj, name=f"{tag}_merge_fwd")
    mix = _mm(merged, w["w_mix_o"], name=f"{tag}_mix_o")
    x1, x1b, xh1, rs1 = _ln_fwd(x, mix, w["ln_g"][0], w["ln_b"][0], name=f"{tag}_ln1_fwd")
    sv.update(proj=proj, dt_raw=dt_raw, sgo=sgo, xbc=xbc, dt=dt, cs=cs, dtt=dtt, cst=cst, y=y, st=st, yb=yb,
              br_a=br_a, br_b=br_b, merged=merged, xh1=xh1, rs1=rs1, x1b=x1b)
    q = _mm(x1b, w["w_xq"], out_dtype=MXU_DTYPE, name=f"{tag}_q")
    kv = _mm(memn_b, w["w_xkv"], out_dtype=MXU_DTYPE, name=f"{tag}_kv")
    o = _attn_fwd(q, kv, bsz=bsz, name=f"{tag}_attn_fwd")
    att = _mm(o, w["w_xo"], name=f"{tag}_xo")
    x2, x2b, xh2, rs2 = _ln_fwd(x1, att, w["ln_g"][1], w["ln_b"][1], name=f"{tag}_ln2_fwd")
    sv.update(q=q, kv=kv, o=o, xh2=xh2, rs2=rs2, x2b=x2b)
    h = _mm(x2b, w["w_ffn_in"], name=f"{tag}_ffn_in")
    a = _swiglu_fwd(h, name=f"{tag}_swiglu_fwd")
    ffn = _mm(a, w["w_ffn_out"], name=f"{tag}_ffn_out")
    x3, x3b, xh3, rs3 = _ln_fwd(x2, ffn, w["ln_g"][2], w["ln_b"][2], name=f"{tag}_ln3_fwd")
    sv.update(h=h, a=a, xh3=xh3, rs3=rs3)
    return x3, x3b, sv, w


GRAD_GROUPS = (("w_ffn_out", "w_ffn_in", "w_xo", "w_xq", "w_xkv"), ("w_mix_o", "p_a", "p_b"), ("w_in",))


def _layer_bwd(dx3_addends, dx3_scales, memn_b, w, sv, on_group=None, *, bsz, tag):
    nc = sv["xh1"].shape[0] // bsz // CHUNK
    gr = {}

    def group_done(k):
        return on_group(GRAD_GROUPS[k], gr) if on_group is not None else None
    dp3, dp3b, dg3, db3 = _ln_bwd(dx3_addends, dx3_scales, sv["xh3"], sv["rs3"], w["ln_g"][2], name=f"{tag}_ln3_bwd")
    da = _mm(dp3b, w["w_ffn_out"], tb=True, name=f"{tag}_d_a")
    gr["w_ffn_out"] = _mm(sv["a"], dp3b, ta=True, name=f"{tag}_dw_ffn_out")
    dh = _swiglu_bwd(sv["h"], da, name=f"{tag}_swiglu_bwd")
    gr["w_ffn_in"] = _mm(sv["x2b"], dh, ta=True, name=f"{tag}_dw_ffn_in")
    dx2_br = _mm(dh, w["w_ffn_in"], tb=True, name=f"{tag}_dx2")
    dp2, dp2b, dg2, db2 = _ln_bwd([dp3, dx2_br], [ALPHA, 1.0], sv["xh2"], sv["rs2"], w["ln_g"][1], name=f"{tag}_ln2_bwd")
    do = _mm(dp2b, w["w_xo"], tb=True, out_dtype=MXU_DTYPE, name=f"{tag}_d_o")
    gr["w_xo"] = _mm(sv["o"], dp2b, ta=True, name=f"{tag}_dw_xo")
    dq, dk, dv = _attn_bwd(sv["q"], sv["kv"], do, bsz=bsz, name=f"{tag}_attn_bwd")
    dkv = jnp.concatenate([dk, dv], axis=1)
    gr["w_xq"] = _mm(sv["x1b"], dq, ta=True, name=f"{tag}_dw_xq")
    gr["w_xkv"] = _mm(memn_b, dkv, ta=True, name=f"{tag}_dw_xkv")
    dmemn = _mm(dkv, w["w_xkv"], tb=True, name=f"{tag}_d_memn")
    dx1_br = _mm(dq, w["w_xq"], tb=True, name=f"{tag}_dx1")
    token = group_done(0)
    ln_g1 = w["ln_g"][0] if token is None else w["ln_g"][0] + token[0, 0]
    dp1, dp1b, dg1, db1 = _ln_bwd([dp2, dx1_br], [ALPHA, 1.0], sv["xh1"], sv["rs1"], ln_g1, name=f"{tag}_ln1_bwd")
    gr["ln_g"] = jnp.concatenate([dg1, dg2, dg3], axis=0)
    gr["ln_b"] = jnp.concatenate([db1, db2, db3], axis=0)
    dmerged = _mm(dp1b, w["w_mix_o"], tb=True, name=f"{tag}_d_merged")
    gr["w_mix_o"] = _mm(sv["merged"], dp1b, ta=True, name=f"{tag}_dw_mix_o")
    dbr_a, dbr_b, dproj = _merge_bwd(dmerged, sv["br_a"], sv["br_b"], sv["proj"], name=f"{tag}_merge_bwd")
    gr["p_a"] = _mm(sv["sgo"], dbr_a, ta=True, name=f"{tag}_dw_p_a")
    gr["p_b"] = _mm(sv["yb"], dbr_b, ta=True, name=f"{tag}_dw_p_b")
    dsgo = _mm(dbr_a, w["p_a"], tb=True, name=f"{tag}_d_sgo")
    dyb = _mm(dbr_b, w["p_b"], tb=True, name=f"{tag}_d_yb")
    token = group_done(1)
    norm_g = w["ssm_norm_g"] if token is None else w["ssm_norm_g"] + token[0, 0]
    dy, dproj, gr["ssm_norm_g"] = _gate_norm_bwd(dyb, sv["y"], sv["proj"], norm_g, dproj, name=f"{tag}_gate_norm_bwd")
    dxbc, ddr, gr["a_log"], gr["d_skip"], gr["dt_bias"] = _ssd_bwd(
        dy, sv["xbc"], sv["dt"], sv["cs"], sv["dtt"], sv["cst"], sv["st"], w["d_skipx"], w["a_log8"], sv["dt_raw"],
        w["dt_bias8"], nc=nc, name=f"{tag}_ssd_bwd")
    dproj, gr["conv_w"], gr["conv_b"] = _conv_bwd(sv["proj"], dxbc, w["conv_w"], w["conv_b"], dproj, bsz=bsz, name=f"{tag}_conv_bwd")
    dproj, gr["sg_w"], dsg_bcol, gr["sg_ln_g"], gr["sg_ln_b"] = _sg_bwd(
        sv["proj"], dsgo, w["sg_ln_g"], w["sg_ln_b"], w["sg_w"], w["sg_bcol"], dproj, name=f"{tag}_sg_bwd")
    gr["sg_b"] = dsg_bcol[..., 0]
    gr["w_main"] = _mm(sv["x_in"], dproj, ta=True, name=f"{tag}_dw_main")
    gr["w_dt"] = _mm(sv["x_in"], ddr, ta=True, name=f"{tag}_dw_dt")
    token = group_done(2)
    w_dt = w["w_dt"] if token is None else w["w_dt"] + token[0, 0].astype(w["w_dt"].dtype)
    dx_dt = _mm(ddr, w_dt, tb=True, name=f"{tag}_dx_dt")
    dx_main = _mm(dproj, w["w_main"], tb=True, name=f"{tag}_dx_main")
    return [dp1, dx_main, dx_dt], [ALPHA, 1.0, 1.0], gr, dmemn


def _local_step(x, mem, tgt, mem_ln_g, mem_ln_b, layers, on_layer_grads=None):
    bsz, s, d = x.shape
    xf = x.reshape(bsz * s, d)
    memf = mem.reshape(-1, d)
    _, memn_b, mxh, mrs = _ln_fwd(memf, None, mem_ln_g, mem_ln_b, name="mem_ln_fwd")
    cur, curb, saved, weights = xf, xf, [], []
    for li, get_weights in enumerate(layers):
        cur, curb, sv, w = _layer_fwd(cur, curb, memn_b, get_weights(cur), bsz=bsz, tag=f"l{li}")
        saved.append(sv)
        weights.append(w)
    dy, lsum = _loss_head(cur, tgt.reshape(bsz * s, d), name="loss_head")
    addends, scales = [dy], [1.0]
    grads, dmem = [None] * len(layers), []
    for li in reversed(range(len(layers))):
        on_group = None if on_layer_grads is None else functools.partial(on_layer_grads, li)
        addends, scales, grads[li], dm = _layer_bwd(addends, scales, memn_b, weights[li], saved[li], on_group, bsz=bsz, tag=f"l{li}")
        dmem.append(dm)
    grad_x = _add_scaled(addends, scales, name="grad_x").reshape(bsz, s, d)
    _, _, dmg, dmb = _ln_bwd(dmem, [1.0] * len(dmem), mxh, mrs, mem_ln_g, name="mem_ln_bwd")
    return lsum, grad_x, grads, dmg[0], dmb[0]


_ANY = pl.BlockSpec(memory_space=pl.ANY)
_MESH = pl.DeviceIdType.MESH


def _all_gather8(x, *, name):
    def body(x_ref, out_ref, send_sems, recv_sems):
        mx, my, mc = lax.axis_index("x"), lax.axis_index("y"), lax.axis_index("c")
        me, sibling = (mx, my, mc), (mx, my, 1 - mc)
        chips = [(1 - mx, my), (mx, 1 - my), (1 - mx, 1 - my)]

        def blk(px, py, pc):
            return out_ref.at[4 * px + 2 * py + pc]

        def copy(k, block, to, src=None):
            return pltpu.make_async_remote_copy(
                src_ref=blk(*block) if src is None else src, dst_ref=blk(*block), send_sem=send_sems.at[k],
                recv_sem=recv_sems.at[k], device_id=to, device_id_type=_MESH)

        first = [copy(0, me, sibling, src=x_ref)]
        first += [copy(1 + j, me, (*chip, mc), src=x_ref) for j, chip in enumerate(chips)]
        for cp in first:
            cp.start()
        passed = [copy(4 + j, (*chip, mc), sibling) for j, chip in enumerate(chips)]
        for j, chip in enumerate(chips):
            copy(1 + j, (*chip, mc), me).wait_recv()
            passed[j].start()
        copy(0, sibling, me).wait_recv()
        for j, chip in enumerate(chips):
            copy(4 + j, (*chip, 1 - mc), me).wait_recv()
        for cp in first + passed:
            cp.wait_send()

    return pl.pallas_call(
        body, out_shape=jax.ShapeDtypeStruct((N_DEV,) + x.shape, x.dtype), in_specs=[_ANY], out_specs=_ANY,
        scratch_shapes=[pltpu.SemaphoreType.DMA((7,)), pltpu.SemaphoreType.DMA((7,))], name=name)(x)


def _row_tile(rows, row_bytes, mult=SUBLANE):
    best = None
    for tr in range(mult, rows + 1, mult):
        if rows % tr == 0 and (best is None or tr * row_bytes <= BLOCK_BYTES):
            best = tr
    return rows if best is None else best


def _gather_shape(r, c, kind):
    return {"row": (2, N_CHIPS * r, c), "col": (2, r, N_CHIPS * c), "chip": (2, N_CHIPS, r, c)}[kind]


def _cast_place(shard, kind, dtype, chip_idx, *, name):
    _, r, c = shard.shape
    tr = _row_tile(r, c * 4, 16)
    nt = r // tr

    def body(_, s_ref, o_ref):
        o_ref[...] = s_ref[...].astype(dtype)

    if kind == "row":
        out_spec = pl.BlockSpec((None, tr, c), lambda l, i, j_ref: (l, j_ref[0] * nt + i, 0))
    elif kind == "col":
        out_spec = pl.BlockSpec((None, tr, c), lambda l, i, j_ref: (l, i, j_ref[0]))
    else:
        out_spec = pl.BlockSpec((None, None, tr, c), lambda l, i, j_ref: (l, j_ref[0], i, 0))
    grid_spec = pltpu.PrefetchScalarGridSpec(
        num_scalar_prefetch=1, grid=(2, nt), in_specs=[pl.BlockSpec((None, tr, c), lambda l, i, j_ref: (l, i, 0))],
        out_specs=out_spec)
    return pl.pallas_call(body, grid_spec=grid_spec, out_shape=jax.ShapeDtypeStruct(_gather_shape(r, c, kind), dtype),
                          compiler_params=_params("parallel", "parallel"), name=name)(chip_idx, shard)


def _gather_params(bufs, shard_shapes, kinds, *, name):
    n = len(bufs)

    def body(*refs):
        outs = refs[n:2 * n]
        send_sems, recv_sems = refs[2 * n:]
        mx, my, mc = lax.axis_index("x"), lax.axis_index("y"), lax.axis_index("c")
        me, sibling = (mx, my, mc), (mx, my, 1 - mc)
        chips = [(1 - mx, my), (mx, 1 - my), (1 - mx, 1 - my)]

        def blk(i, px, py, pc):
            r, c = shard_shapes[i]
            j = 2 * px + py
            if kinds[i] == "row":
                return outs[i].at[pc, pl.ds(pl.multiple_of(j * r, r), r)]
            if kinds[i] == "col":
                return outs[i].at[pc, :, pl.ds(pl.multiple_of(j * c, c), c)]
            return outs[i].at[pc, j]

        def copy(i, k, block, to):
            return pltpu.make_async_remote_copy(
                src_ref=blk(i, *block), dst_ref=blk(i, *block), send_sem=send_sems.at[6 * i + k],
                recv_sem=recv_sems.at[6 * i + k], device_id=to, device_id_type=_MESH)

        sent = []
        for i in range(n):
            for j, chip in enumerate(chips):
                cp = copy(i, j, me, (*chip, mc))
                cp.start()
                sent.append(cp)
        for j, chip in enumerate(chips):
            for i in range(n):
                copy(i, j, (*chip, mc), me).wait_recv()
                fwd = copy(i, 3 + j, (*chip, mc), sibling)
                fwd.start()
                sent.append(fwd)
        for i in range(n):
            for j, chip in enumerate(chips):
                copy(i, 3 + j, (*chip, 1 - mc), me).wait_recv()
        for cp in sent:
            cp.wait_send()

    return pl.pallas_call(
        body, out_shape=[jax.ShapeDtypeStruct(b.shape, b.dtype) for b in bufs], in_specs=[_ANY] * n, out_specs=[_ANY] * n,
        input_output_aliases={i: i for i in range(n)},
        scratch_shapes=[pltpu.SemaphoreType.DMA((6 * n,)), pltpu.SemaphoreType.DMA((6 * n,))], name=name)(*bufs)


def _half(r, h):
    return pl.ds(pl.multiple_of(h * (r // 2), r // 2), r // 2)


def _grads_to_sibling(gs, views, *, name):
    n = len(gs)

    def recv_shape(g, view):
        if view == "chip":
            return jax.ShapeDtypeStruct((g.shape[0], g.shape[1] // 2, g.shape[2]), g.dtype)
        return jax.ShapeDtypeStruct((g.shape[0] // 2, g.shape[1]), g.dtype)

    def body(*refs):
        ins, outs = refs[:n], refs[n:2 * n]
        send_sems, recv_sems = refs[2 * n:]
        mx, my, mc = lax.axis_index("x"), lax.axis_index("y"), lax.axis_index("c")
        copies = []
        for i in range(n):
            if views[i] == "chip":
                src = ins[i].at[:, _half(gs[i].shape[1], 1 - mc)]
            else:
                src = ins[i].at[_half(gs[i].shape[0], 1 - mc)]
            cp = pltpu.make_async_remote_copy(src_ref=src, dst_ref=outs[i], send_sem=send_sems.at[i], recv_sem=recv_sems.at[i],
                                              device_id=(mx, my, 1 - mc), device_id_type=_MESH)
            cp.start()
            copies.append(cp)
        for cp in copies:
            cp.wait()

    return pl.pallas_call(
        body, out_shape=[recv_shape(g, v) for g, v in zip(gs, views)], in_specs=[_ANY] * n, out_specs=[_ANY] * n,
        scratch_shapes=[pltpu.SemaphoreType.DMA((n,)), pltpu.SemaphoreType.DMA((n,))], name=name)(*gs)


_HBM = pl.BlockSpec(memory_space=pltpu.HBM)
_SEM = pl.BlockSpec(memory_space=pltpu.SEMAPHORE)
_EFFECT = pltpu.SideEffectType.DATAFLOW_SIDE_EFFECTING


def _cast_place_layer(shard, l, kind, chip_idx, after, *, name):
    _, r, c = shard.shape
    tr = _row_tile(r, c * 4, 16)
    nt = r // tr

    def body(_, s_ref, *rest):
        rest[-1][...] = s_ref[...].astype(MXU_DTYPE)

    if kind == "row":
        out_spec = pl.BlockSpec((tr, c), lambda i, j_ref: (j_ref[0] * nt + i, 0))
    elif kind == "col":
        out_spec = pl.BlockSpec((tr, c), lambda i, j_ref: (i, j_ref[0]))
    else:
        out_spec = pl.BlockSpec((None, tr, c), lambda i, j_ref: (j_ref[0], i, 0))
    extra = [] if after is None else [after]
    grid_spec = pltpu.PrefetchScalarGridSpec(
        num_scalar_prefetch=1, grid=(nt,), in_specs=[pl.BlockSpec((None, tr, c), lambda i, j_ref: (l, i, 0))] + [_ANY] * len(extra),
        out_specs=out_spec)
    return pl.pallas_call(body, grid_spec=grid_spec, out_shape=jax.ShapeDtypeStruct(_gather_shape(r, c, kind)[1:], MXU_DTYPE),
                          compiler_params=_params("parallel"), name=name)(chip_idx, shard, *extra)


def _half_block(ref, kind, r, c, j, h):
    rows = _half(r, h)
    if kind == "row":
        return ref.at[pl.ds(pl.multiple_of(j * r + h * (r // 2), r // 2), r // 2)]
    if kind == "col":
        return ref.at[rows, pl.ds(pl.multiple_of(j * c, c), c)]
    return ref.at[j, rows]


def _gather_ici_copies(buf_refs, shapes, kinds, send_sems, recv_sems):
    mx, my, mc = lax.axis_index("x"), lax.axis_index("y"), lax.axis_index("c")
    chips = [(1 - mx, my), (mx, 1 - my), (1 - mx, 1 - my)]
    copies = []
    for i, (r, c) in enumerate(shapes):
        mine = _half_block(buf_refs[i], kinds[i], r, c, 2 * mx + my, mc)
        for k, (px, py) in enumerate(chips):
            copies.append(pltpu.make_async_remote_copy(
                src_ref=mine, dst_ref=mine, send_sem=send_sems.at[3 * i + k], recv_sem=recv_sems.at[3 * i + k],
                device_id=(px, py, mc), device_id_type=_MESH))
    return copies


def _gather_start(bufs, shapes, kinds, *, name):
    n = len(bufs)

    def body(*refs):
        send_sems, recv_sems, token = refs[n], refs[n + 1], refs[-1]
        for cp in _gather_ici_copies(refs[:n], shapes, kinds, send_sems, recv_sems):
            cp.start()
        token[...] = jnp.zeros_like(token)

    outs = pl.pallas_call(
        body, name=name,
        out_shape=(pltpu.SemaphoreType.DMA((3 * n,)), pltpu.SemaphoreType.DMA((3 * n,)),
                   *[pltpu.HBM(b.shape, b.dtype) for b in bufs], jax.ShapeDtypeStruct((SUBLANE, LANE), F32)),
        in_specs=[_HBM] * n, out_specs=(_SEM, _SEM, *[_HBM] * n, pl.BlockSpec(memory_space=pltpu.VMEM)),
        input_output_aliases={i: 2 + i for i in range(n)},
        compiler_params=pltpu.CompilerParams(has_side_effects=_EFFECT),
    )(*[pltpu.with_memory_space_constraint(b, pltpu.HBM) for b in bufs])
    return outs[0], outs[1], list(outs[2:2 + n]), outs[-1]


def _gather_wait(send_sems, recv_sems, bufs, shapes, kinds, after, *, name):
    n = len(bufs)

    def body(*refs):
        for cp in _gather_ici_copies(refs[:n], shapes, kinds, refs[n], refs[n + 1]):
            cp.wait_send()
            cp.wait_recv()

    outs = pl.pallas_call(
        body, name=name, out_shape=tuple(pltpu.HBM(b.shape, b.dtype) for b in bufs),
        in_specs=[_HBM] * n + [_SEM, _SEM, _ANY], out_specs=tuple([_HBM] * n), input_output_aliases={i: i for i in range(n)},
        compiler_params=pltpu.CompilerParams(has_side_effects=_EFFECT),
    )(*bufs, send_sems, recv_sems, after)
    return list(outs)


def _gather_forward(bufs, shapes, kinds, *, name):
    n = len(bufs)

    def body(*refs):
        outs = refs[n:2 * n]
        send_sems, recv_sems = refs[2 * n:]
        mx, my, mc = lax.axis_index("x"), lax.axis_index("y"), lax.axis_index("c")
        chips = [(1 - mx, my), (mx, 1 - my), (1 - mx, 1 - my)]
        copies = []
        for i, (r, c) in enumerate(shapes):
            for k, (px, py) in enumerate(chips):
                got = _half_block(outs[i], kinds[i], r, c, 2 * px + py, mc)
                cp = pltpu.make_async_remote_copy(src_ref=got, dst_ref=got, send_sem=send_sems.at[3 * i + k],
                                                  recv_sem=recv_sems.at[3 * i + k], device_id=(mx, my, 1 - mc), device_id_type=_MESH)
                cp.start()
                copies.append(cp)
        for cp in copies:
            cp.wait()

    return pl.pallas_call(
        body, out_shape=[jax.ShapeDtypeStruct(b.shape, b.dtype) for b in bufs], in_specs=[_ANY] * n, out_specs=[_ANY] * n,
        input_output_aliases={i: i for i in range(n)},
        scratch_shapes=[pltpu.SemaphoreType.DMA((3 * n,)), pltpu.SemaphoreType.DMA((3 * n,))], name=name)(*bufs)


def _chip_exchange_copies(pair_refs, land_refs, pairs, views, send_sems, recv_sems):
    mx, my, mc = lax.axis_index("x"), lax.axis_index("y"), lax.axis_index("c")
    me = 2 * mx + my
    chips = [(1 - mx, my), (mx, 1 - my), (1 - mx, 1 - my)]
    copies = []
    for i in range(len(pairs)):
        for k, (px, py) in enumerate(chips):
            j = 2 * px + py
            if views[i] == "chip":
                src = pair_refs[i].at[j]
            else:
                c = pairs[i].shape[1] // N_CHIPS
                src = pair_refs[i].at[:, pl.ds(pl.multiple_of(j * c, c), c)]
            copies.append(pltpu.make_async_remote_copy(
                src_ref=src, dst_ref=land_refs[i].at[me], send_sem=send_sems.at[3 * i + k], recv_sem=recv_sems.at[3 * i + k],
                device_id=(px, py, mc), device_id_type=_MESH))
    return copies


def _quad_shape(p, view):
    return p.shape if view == "chip" else (N_CHIPS, p.shape[0], p.shape[1] // N_CHIPS)


def _grads_to_chips_start(pairs, views, *, name):
    n = len(pairs)
    lands = [pltpu.with_memory_space_constraint(lax.empty(_quad_shape(p, v), p.dtype), pltpu.HBM) for p, v in zip(pairs, views)]

    def body(*refs):
        pair_refs, land_refs = refs[:n], refs[n:2 * n]
        send_sems, recv_sems = refs[2 * n], refs[2 * n + 1]
        token = refs[-1]
        for cp in _chip_exchange_copies(pair_refs, land_refs, pairs, views, send_sems, recv_sems):
            cp.start()
        token[...] = jnp.zeros_like(token)

    outs = pl.pallas_call(
        body, name=name,
        out_shape=(pltpu.SemaphoreType.DMA((3 * n,)), pltpu.SemaphoreType.DMA((3 * n,)),
                   *[pltpu.HBM(p.shape, p.dtype) for p in pairs], *[pltpu.HBM(l.shape, l.dtype) for l in lands],
                   jax.ShapeDtypeStruct((SUBLANE, LANE), F32)),
        in_specs=[_HBM] * (2 * n), out_specs=(_SEM, _SEM, *[_HBM] * (2 * n), pl.BlockSpec(memory_space=pltpu.VMEM)),
        input_output_aliases={i: 2 + i for i in range(2 * n)},
        compiler_params=pltpu.CompilerParams(has_side_effects=_EFFECT),
    )(*[pltpu.with_memory_space_constraint(p, pltpu.HBM) for p in pairs], *lands)
    return outs[0], outs[1], list(outs[2:2 + n]), list(outs[2 + n:2 + 2 * n]), outs[-1]


def _grads_to_chips_wait(send_sems, recv_sems, pairs, lands, views, after, *, name):
    n = len(pairs)

    def body(*refs):
        pair_refs, land_refs = refs[:n], refs[n:2 * n]
        s_sems, r_sems = refs[2 * n], refs[2 * n + 1]
        for cp in _chip_exchange_copies(pair_refs, land_refs, pairs, views, s_sems, r_sems):
            cp.wait_send()
            cp.wait_recv()

    outs = pl.pallas_call(
        body, name=name, out_shape=tuple(pltpu.HBM(x.shape, x.dtype) for x in list(pairs) + list(lands)),
        in_specs=[_HBM] * (2 * n) + [_SEM, _SEM, _ANY], out_specs=tuple([_HBM] * (2 * n)),
        input_output_aliases={i: i for i in range(2 * n)},
        compiler_params=pltpu.CompilerParams(has_side_effects=_EFFECT),
    )(*pairs, *lands, send_sems, recv_sems, after)
    return list(outs[n:])


def _grads_share(tots, *, name):
    n = len(tots)

    def body(*refs):
        ins, outs = refs[:n], refs[n:2 * n]
        send_sems, recv_sems = refs[2 * n:]
        mx, my, mc = lax.axis_index("x"), lax.axis_index("y"), lax.axis_index("c")
        copies = []
        for i in range(n):
            cp = pltpu.make_async_remote_copy(src_ref=ins[i], dst_ref=outs[i], send_sem=send_sems.at[i], recv_sem=recv_sems.at[i],
                                              device_id=(mx, my, 1 - mc), device_id_type=_MESH)
            cp.start()
            copies.append(cp)
        for cp in copies:
            cp.wait()

    return pl.pallas_call(
        body, out_shape=[jax.ShapeDtypeStruct(t.shape, t.dtype) for t in tots], in_specs=[_ANY] * n, out_specs=[_ANY] * n,
        scratch_shapes=[pltpu.SemaphoreType.DMA((n,)), pltpu.SemaphoreType.DMA((n,))], name=name)(*tots)


def _pair_sum(g, recv, view, c_idx, *, name):
    def body(c_ref, a_ref, b_ref, o_ref):
        o_ref[...] = (a_ref[...] + b_ref[...]).astype(WIRE_DTYPE)

    if view == "chip":
        nch, r, c = g.shape
        tr = _row_tile(r // 2, c * 4, 16)
        gv = g.reshape(nch, 2, r // 2, c)
        grid = (nch, (r // 2) // tr)
        in_specs = [pl.BlockSpec((None, None, tr, c), lambda j, i, c_ref: (j, c_ref[0], i, 0)),
                    pl.BlockSpec((None, tr, c), lambda j, i, c_ref: (j, i, 0))]
        out_spec = pl.BlockSpec((None, tr, c), lambda j, i, c_ref: (j, i, 0))
        sem = ("parallel", "parallel")
    else:
        r, c4 = g.shape
        tr = _row_tile(r // 2, c4 * 4, 16)
        gv = g.reshape(2, r // 2, c4)
        grid = ((r // 2) // tr,)
        in_specs = [pl.BlockSpec((None, tr, c4), lambda i, c_ref: (c_ref[0], i, 0)), pl.BlockSpec((tr, c4), lambda i, c_ref: (i, 0))]
        out_spec = pl.BlockSpec((tr, c4), lambda i, c_ref: (i, 0))
        sem = ("parallel",)
    grid_spec = pltpu.PrefetchScalarGridSpec(num_scalar_prefetch=1, grid=grid, in_specs=in_specs, out_specs=out_spec)
    return pl.pallas_call(body, grid_spec=grid_spec, out_shape=jax.ShapeDtypeStruct(recv.shape, WIRE_DTYPE),
                          compiler_params=_params(*sem), name=name)(c_idx, gv, recv)


def _quad_sum(gs, recvs, quads, view, chip_idx, c_idx, *, name):
    nl = len(quads)
    nch, rh, c = quads[0].shape
    tr = _row_tile(rh, c * 4, 16)

    def body(_, __, *refs):
        o_ref = refs[-1]
        per = nch + 1
        for l in range(nl):
            grp = refs[l * per:(l + 1) * per]
            acc = grp[0][...] + grp[1][...]
            for r in grp[2:]:
                acc = acc + r[...].astype(F32)
            o_ref[l] = acc

    if view == "chip":
        own = [pl.BlockSpec((None, None, tr, c), lambda i, j, h: (j[0], h[0], i, 0)),
               pl.BlockSpec((None, tr, c), lambda i, j, h: (j[0], i, 0))]
        gviews = [g.reshape(nch, 2, rh, c) for g in gs]
    else:
        own = [pl.BlockSpec((None, tr, c), lambda i, j, h: (h[0], i, j[0])), pl.BlockSpec((tr, c), lambda i, j, h: (i, j[0]))]
        gviews = [g.reshape(2, rh, nch * c) for g in gs]
    assert nch & (nch - 1) == 0
    got = [pl.BlockSpec((None, tr, c), functools.partial(lambda i, j, h, k: ((j[0] + k) & (nch - 1), i, 0), k=k))
           for k in range(1, nch)]
    ins = []
    for l in range(nl):
        ins += [gviews[l], recvs[l]] + [quads[l]] * (nch - 1)
    grid_spec = pltpu.PrefetchScalarGridSpec(
        num_scalar_prefetch=2, grid=(rh // tr,), in_specs=(own + got) * nl,
        out_specs=pl.BlockSpec((nl, tr, c), lambda i, j, h: (0, i, 0)))
    return pl.pallas_call(body, grid_spec=grid_spec, out_shape=jax.ShapeDtypeStruct((nl, rh, c), F32),
                          compiler_params=_params("parallel"), name=name)(chip_idx, c_idx, *ins)


def _sum_devices(g8, own, dev_idx, *, name):
    k, rows, cols = g8.shape

    def body(d_ref, a_ref, x_ref, o_ref):
        acc = None
        for i in range(k):
            term = jnp.where(d_ref[0] == i, x_ref[...], a_ref[i])
            acc = term if acc is None else acc + term
        o_ref[...] = acc

    grid_spec = pltpu.PrefetchScalarGridSpec(
        num_scalar_prefetch=1, grid=(1,),
        in_specs=[pl.BlockSpec((k, rows, cols), lambda i, d_ref: (0, 0, 0)), pl.BlockSpec((rows, cols), lambda i, d_ref: (0, 0))],
        out_specs=pl.BlockSpec((rows, cols), lambda i, d_ref: (0, 0)))
    return pl.pallas_call(body, grid_spec=grid_spec, out_shape=jax.ShapeDtypeStruct((rows, cols), g8.dtype),
                          compiler_params=_params("arbitrary"), name=name)(dev_idx, g8, own)


def _adamw(w, g, m, v, *, name):
    rows, cols = w.shape
    tr = rows
    for cand in (256, 128, 64, 32, 16, 8):
        if rows % cand == 0 and cand * cols <= 512 * 1024:
            tr = cand
            break
    c1 = 1.0 - ADAM_B1 ** ADAM_STEP
    c2 = 1.0 - ADAM_B2 ** ADAM_STEP

    def body(w_ref, g_ref, m_ref, v_ref, d_ref, nm_ref, nv_ref):
        gv = g_ref[...]
        nm = ADAM_B1 * m_ref[...] + (1.0 - ADAM_B1) * gv
        nv = ADAM_B2 * v_ref[...] + (1.0 - ADAM_B2) * (gv * gv)
        d_ref[...] = -ADAM_LR * ((nm / c1) / (jnp.sqrt(nv / c2) + ADAM_EPS) + ADAM_WD * w_ref[...])
        nm_ref[...] = nm
        nv_ref[...] = nv

    spec = pl.BlockSpec((tr, cols), lambda i: (i, 0))
    shp = jax.ShapeDtypeStruct((rows, cols), F32)
    return pl.pallas_call(body, grid=(rows // tr,), in_specs=[spec] * 4, out_specs=[spec] * 3, out_shape=[shp] * 3,
                          compiler_params=_params("parallel"), name=name)(w, g, m, v)


def _adamw_halves(w, m, v, mine, other, c_idx, *, name):
    nl, r, c = w.shape
    rh = r // 2
    tr = _row_tile(rh, c * 4)
    c1 = 1.0 - ADAM_B1 ** ADAM_STEP
    c2 = 1.0 - ADAM_B2 ** ADAM_STEP

    def body(c_ref, w_ref, m_ref, v_ref, a_ref, b_ref, g_ref, d_ref, nm_ref, nv_ref):
        gv = jnp.where(pl.program_id(1) == c_ref[0], a_ref[...], b_ref[...])
        nm = ADAM_B1 * m_ref[...] + (1.0 - ADAM_B1) * gv
        nv = ADAM_B2 * v_ref[...] + (1.0 - ADAM_B2) * (gv * gv)
        g_ref[...] = gv
        d_ref[...] = -ADAM_LR * ((nm / c1) / (jnp.sqrt(nv / c2) + ADAM_EPS) + ADAM_WD * w_ref[...])
        nm_ref[...] = nm
        nv_ref[...] = nv

    full = pl.BlockSpec((None, None, tr, c), lambda l, h, i, c_ref: (l, h, i, 0))
    half = pl.BlockSpec((None, tr, c), lambda l, h, i, c_ref: (l, i, 0))
    grid_spec = pltpu.PrefetchScalarGridSpec(num_scalar_prefetch=1, grid=(nl, 2, rh // tr),
                                             in_specs=[full] * 3 + [half] * 2, out_specs=[full] * 4)
    shp = jax.ShapeDtypeStruct((nl, 2, rh, c), F32)
    view = (nl, 2, rh, c)
    outs = pl.pallas_call(body, grid_spec=grid_spec, out_shape=[shp] * 4, compiler_params=_params("parallel", "parallel", "parallel"),
                          name=name)(c_idx, w.reshape(view), m.reshape(view), v.reshape(view), mine, other)
    return [o.reshape(nl, r, c) for o in outs]


WEIGHTS = ["mem_ln_g", "mem_ln_b", "w_in", "sg_ln_g", "sg_ln_b", "sg_w", "sg_b", "conv_w", "conv_b", "dt_bias", "a_log",
           "d_skip", "ssm_norm_g", "p_a", "p_b", "w_mix_o", "w_xq", "w_xkv", "w_xo", "w_ffn_in", "w_ffn_out", "ln_g", "ln_b"]
ARG_NAMES = ["x", "mem"] + WEIGHTS + ["loss_target"] + ["m_" + n for n in WEIGHTS] + ["v_" + n for n in WEIGHTS]
BIG = {"w_in": (1, (1024, 9248)), "p_a": (0, (1024, 1024)), "p_b": (0, (2048, 1024)), "w_mix_o": (0, (1024, 1024)),
       "w_xq": (0, (1024, 1024)), "w_xkv": (1, (1024, 2048)), "w_xo": (0, (1024, 1024)), "w_ffn_in": (1, (1024, 5632)),
       "w_ffn_out": (0, (2816, 1024))}
SMALL_SHARDED = {"conv_w": (4, 3072), "ln_g": (3, 1024), "ln_b": (3, 1024)}
SMALL = [n for n in WEIGHTS if n not in BIG]
XBC_IN0, DT_COL0, DT_COL1 = 4096, 7168, 7200
GATHER_KIND = {"w_in": "chip", "p_a": "row", "p_b": "row", "w_mix_o": "row", "w_xq": "row", "w_xkv": "col", "w_xo": "row",
               "w_ffn_in": "col", "w_ffn_out": "row", "conv_w": "chip", "ln_g": "chip", "ln_b": "chip"}
GRAD_VIEW = {n: ("col" if k == "col" else "chip") for n, k in GATHER_KIND.items() if n in BIG}


def _shard_shape(name):
    axis, (r, c) = BIG[name]
    return (r // N_CHIPS, c) if axis == 0 else (r, c // N_CHIPS)


def _pad_rows(flat, cols, row_mult):
    n = flat.shape[0]
    rows = -(-n // cols)
    rows = -(-rows // row_mult) * row_mult
    return jnp.pad(flat, (0, rows * cols - n)).reshape(rows, cols)


def _gather_small_params(a, chip):
    names = list(SMALL_SHARDED)
    kinds = [GATHER_KIND[n] for n in names]
    bufs = [_cast_place(a[n], GATHER_KIND[n], F32, chip.reshape(1), name=f"place_{n}") for n in names]
    outs = _gather_params(bufs, [a[n].shape[1:] for n in names], kinds, name="gather_small_params")
    full = {}
    for n, o in zip(names, outs):
        _, _, r, c = o.shape
        full[n] = jnp.transpose(o, (0, 2, 1, 3)).reshape(DEPTH, r, N_CHIPS * c)
    return full


GATHER_GROUPS = (("w_in",), tuple(n for n in BIG if n != "w_in"))


def _gather_group_start(a, l, names, chip, after, *, tag):
    bufs = [_cast_place_layer(a[n], l, GATHER_KIND[n], chip.reshape(1), after, name=f"place_{n}_l{l}") for n in names]
    return _gather_start(bufs, [a[n].shape[1:] for n in names], [GATHER_KIND[n] for n in names], name=f"gather_start_{tag}")


def _gather_group_finish(a, names, flight, after, *, tag):
    send_sems, recv_sems, bufs, token = flight
    shapes, kinds = [a[n].shape[1:] for n in names], [GATHER_KIND[n] for n in names]
    bufs = _gather_wait(send_sems, recv_sems, bufs, shapes, kinds, token if after is None else after, name=f"gather_wait_{tag}")
    full = dict(zip(names, _gather_forward(bufs, shapes, kinds, name=f"gather_forward_{tag}")))
    if "w_in" in full:
        _, r, c = full["w_in"].shape
        w_in = jnp.transpose(full.pop("w_in"), (1, 0, 2)).reshape(r, N_CHIPS * c)
        full["w_main"] = jnp.concatenate([w_in[:, :XBC_IN0], w_in[:, DT_COL1:], w_in[:, XBC_IN0:DT_COL0]], axis=1)
        full["w_dt"] = jnp.pad(w_in[:, DT_COL0:DT_COL1], ((0, 0), (0, HEAD_PAD - SSM_HEADS)))
    return full


def _layer_weights(a, big, small, l):
    w = dict(big)
    for n in SMALL_SHARDED:
        w[n] = small[n][l]
    for n in ["sg_ln_g", "sg_ln_b", "sg_w", "conv_b", "ssm_norm_g"]:
        w[n] = a[n][l]
    w["sg_bcol"] = a["sg_b"][l][..., None]
    for n in ["dt_bias", "a_log"]:
        w[n + "8"] = _pad_heads(a[n][l])
    w["d_skipx"] = _expand_heads(a["d_skip"][l])
    return w


def _pair_sums(grads, names, c_idx, *, tag):
    gs = []
    views = [GRAD_VIEW[n] for n in names]
    for n in names:
        axis, _ = BIG[n]
        r, c = _shard_shape(n)
        if n == "w_in":
            gm, gd = grads["w_main"], grads["w_dt"]
            gfull = jnp.concatenate([gm[:, :XBC_IN0], gm[:, XBC_COL0:], gd[:, :SSM_HEADS], gm[:, GAB_COL0:XBC_COL0]], axis=1)
            gs.append(jnp.transpose(gfull.reshape(r, N_CHIPS, c), (1, 0, 2)))
        elif axis == 0:
            gs.append(grads[n].reshape(N_CHIPS, r, c))
        else:
            gs.append(grads[n])
    recv = _grads_to_sibling(gs, views, name=f"grads_to_sibling_{tag}")
    cpre = c_idx.reshape(1)
    pairs = [_pair_sum(g, rv, v, cpre, name=f"grads_pair_sum_{n}_{tag}") for g, rv, v, n in zip(gs, recv, views, names)]
    return gs, recv, pairs


def _finish_big_grads(parts, c_idx, chip):
    tots = [_quad_sum([parts[l][n][0] for l in range(DEPTH)], [parts[l][n][1] for l in range(DEPTH)],
                      [parts[l][n][2] for l in range(DEPTH)], GRAD_VIEW[n], chip.reshape(1), c_idx.reshape(1),
                      name=f"grads_chip_sum_{n}") for n in BIG]
    others = _grads_share(tots, name="grads_share")
    return {n: (t, o) for n, t, o in zip(BIG, tots, others)}


def _reduce_small_grads(small, chip, c_idx):
    names = list(small)
    flat = jnp.concatenate([small[n].reshape(-1) for n in names])
    packed = _pad_rows(flat, LANE, SUBLANE)
    g8 = _all_gather8(packed, name="gather_small_grads")
    tot = _sum_devices(g8, packed, (2 * chip + c_idx).reshape(1), name="small_grads_sum").reshape(-1)
    out, off = {}, 0
    for n in names:
        sz = small[n].size
        full = tot[off:off + sz].reshape(small[n].shape)
        off += sz
        if n in SMALL_SHARDED:
            cs = SMALL_SHARDED[n][1] // N_CHIPS
            full = lax.dynamic_slice_in_dim(full, chip * cs, cs, axis=-1)
        out[n] = full
    return out


def kernel(x, mem, mem_ln_g, mem_ln_b, w_in, sg_ln_g, sg_ln_b, sg_w, sg_b, conv_w, conv_b, dt_bias, a_log, d_skip, ssm_norm_g, p_a, p_b, w_mix_o, w_xq, w_xkv, w_xo, w_ffn_in, w_ffn_out, ln_g, ln_b, loss_target, m_mem_ln_g, m_mem_ln_b, m_w_in, m_sg_ln_g, m_sg_ln_b, m_sg_w, m_sg_b, m_conv_w, m_conv_b, m_dt_bias, m_a_log, m_d_skip, m_ssm_norm_g, m_p_a, m_p_b, m_w_mix_o, m_w_xq, m_w_xkv, m_w_xo, m_w_ffn_in, m_w_ffn_out, m_ln_g, m_ln_b, v_mem_ln_g, v_mem_ln_b, v_w_in, v_sg_ln_g, v_sg_ln_b, v_sg_w, v_sg_b, v_conv_w, v_conv_b, v_dt_bias, v_a_log, v_d_skip, v_ssm_norm_g, v_p_a, v_p_b, v_w_mix_o, v_w_xq, v_w_xkv, v_w_xo, v_w_ffn_in, v_w_ffn_out, v_ln_g, v_ln_b):
    a = dict(zip(ARG_NAMES, (x, mem, mem_ln_g, mem_ln_b, w_in, sg_ln_g, sg_ln_b, sg_w, sg_b, conv_w, conv_b, dt_bias, a_log, d_skip, ssm_norm_g, p_a, p_b, w_mix_o, w_xq, w_xkv, w_xo, w_ffn_in, w_ffn_out, ln_g, ln_b, loss_target, m_mem_ln_g, m_mem_ln_b, m_w_in, m_sg_ln_g, m_sg_ln_b, m_sg_w, m_sg_b, m_conv_w, m_conv_b, m_dt_bias, m_a_log, m_d_skip, m_ssm_norm_g, m_p_a, m_p_b, m_w_mix_o, m_w_xq, m_w_xkv, m_w_xo, m_w_ffn_in, m_w_ffn_out, m_ln_g, m_ln_b, v_mem_ln_g, v_mem_ln_b, v_w_in, v_sg_ln_g, v_sg_ln_b, v_sg_w, v_sg_b, v_conv_w, v_conv_b, v_dt_bias, v_a_log, v_d_skip, v_ssm_norm_g, v_p_a, v_p_b, v_w_mix_o, v_w_xq, v_w_xkv, v_w_xo, v_w_ffn_in, v_w_ffn_out, v_ln_g, v_ln_b)))
    c_idx = lax.axis_index("c").astype(jnp.int32)
    chip = (2 * lax.axis_index("x") + lax.axis_index("y")).astype(jnp.int32)

    small = _gather_small_params(a, chip)
    ga, gb = GATHER_GROUPS
    flights = {(0, 0): _gather_group_start(a, 0, ga, chip, None, tag="l0_a")}
    flights[0, 1] = _gather_group_start(a, 0, gb, chip, None, tag="l0_b")

    def layer_weights(after, l):
        first = _gather_group_finish(a, ga, flights[l, 0], after if l else None, tag=f"l{l}_a")

        def rest(w, after_b):
            more = _gather_group_finish(a, gb, flights[l, 1], after_b, tag=f"l{l}_b")
            if l + 1 < DEPTH:
                flights[l + 1, 0] = _gather_group_start(a, l + 1, ga, chip, more["p_a"], tag=f"l{l + 1}_a")
                flights[l + 1, 1] = _gather_group_start(a, l + 1, gb, chip, more["p_a"], tag=f"l{l + 1}_b")
                more["p_a"] = more["p_a"] + flights[l + 1, 1][3][0, 0].astype(MXU_DTYPE)
            return {k: v for k, v in {**w, **more}.items() if k != "rest"}

        return dict(_layer_weights(a, first, small, l), rest=rest)

    layers = [functools.partial(layer_weights, l=l) for l in range(DEPTH)]
    parts, flying = [{} for _ in range(DEPTH)], []

    def start_exchange(l, names, grads_l):
        gs, recv, pairs = _pair_sums(grads_l, names, c_idx, tag=f"l{l}_{names[0]}")
        views = [GRAD_VIEW[n] for n in names]
        send_sems, recv_sems, pairs, lands, token = _grads_to_chips_start(pairs, views, name=f"grads_to_chips_start_l{l}_{names[0]}")
        flying.append((l, names, gs, recv, views, send_sems, recv_sems, pairs, lands))
        return token

    lsum, grad_x, grads, d_mem_g, d_mem_b = _local_step(x, mem, loss_target, mem_ln_g, mem_ln_b, layers, start_exchange)
    loss = lax.psum(0.5 * jnp.sum(lsum) / D_MODEL, ("x", "y", "c"))

    for l, names, gs, recv, views, send_sems, recv_sems, pairs, lands in flying:
        quads = _grads_to_chips_wait(send_sems, recv_sems, pairs, lands, views, grad_x, name=f"grads_to_chips_wait_l{l}_{names[0]}")
        for n, g, rv, q in zip(names, gs, recv, quads):
            parts[l][n] = (g, rv, q)
    halves = _finish_big_grads(parts, c_idx, chip)
    gw = {}
    small = {"mem_ln_g": d_mem_g, "mem_ln_b": d_mem_b}
    for n in SMALL:
        if n in small:
            continue
        per_layer = []
        for l in range(DEPTH):
            g = grads[l][n]
            if n in ("dt_bias", "a_log", "d_skip"):
                g = g[0, :SSM_HEADS]
            per_layer.append(g.reshape(a[n].shape[1:-1] + (-1,)))
        small[n] = jnp.stack(per_layer)
    gw.update(_reduce_small_grads(small, chip, c_idx))

    delta, new_m, new_v = {}, {}, {}
    for n in BIG:
        mine, other = halves[n]
        gw[n], delta[n], new_m[n], new_v[n] = _adamw_halves(a[n], a["m_" + n], a["v_" + n], mine, other, c_idx.reshape(1),
                                                             name=f"adamw_{n}")
    packs = [_pad_rows(jnp.concatenate([src(n).reshape(-1) for n in SMALL]), LANE, SUBLANE)
             for src in (lambda n: a[n], lambda n: gw[n], lambda n: a["m_" + n], lambda n: a["v_" + n])]
    outs = _adamw(*packs, name="adamw_small")
    off = 0
    for n in SMALL:
        sz, shp = a[n].size, a[n].shape
        delta[n], new_m[n], new_v[n] = (o.reshape(-1)[off:off + sz].reshape(shp) for o in outs)
        off += sz
    return (loss, grad_x, *[gw[n].reshape(a[n].shape) for n in WEIGHTS], *[delta[n] for n in WEIGHTS],
            *[new_m[n] for n in WEIGHTS], *[new_v[n] for n in WEIGHTS])
```

```python
import functools
import math

import jax
import jax.numpy as jnp
from jax import lax
from jax.experimental import pallas as pl
from jax.experimental.pallas import tpu as pltpu

F32 = jnp.float32
MXU_DTYPE = jnp.bfloat16
WIRE_DTYPE = jnp.bfloat16

D_MODEL = 1024
DEPTH = 2
CHUNK = 128
SG_GROUPS = 8
SSM_INNER = 2048
SSM_HEADDIM = 64
SSM_HEADS = 32
SSM_STATE = 128
SSM_GROUPS = 4
SSM_CONV = 4
SSM_CONV_DIM = 3072
X_HEADS = 4
X_HEADDIM = 256
FFN_HIDDEN = 2816
ALPHA = float((2 * DEPTH) ** 0.25)
LN_EPS = 1e-5
RMS_EPS = 1e-5
ADAM_LR = 0.001
ADAM_B1 = 0.9
ADAM_B2 = 0.999
ADAM_EPS = 1e-08
ADAM_WD = 0.01
ADAM_STEP = 10

MAIN_COLS = 9216
UVZ_COLS = 4096
GAB_COL0 = 4096
XBC_COL0 = 6144
HEAD_PAD = 128

VMEM_LIMIT = 56 * 1024 * 1024
BLOCK_BYTES = 2 * 1024 * 1024
ROW_TILES = (512, 256, 128)
LANE = 128
SUBLANE = 8

N_CHIPS = 4
N_DEV = 8


def _pick(n, cands):
    for c in cands:
        if n % c == 0:
            return c
    return n


MM_TILE_MAX = 1408
MM_OPERAND_BYTES = 8 * 1024 * 1024


def _div_tile(n, limit):
    best = None
    for t in range(LANE, min(n, limit) + 1, LANE):
        if n % t == 0:
            best = t
    return n if best is None else best


def _params(*sem):
    return pltpu.CompilerParams(dimension_semantics=tuple(sem), vmem_limit_bytes=VMEM_LIMIT)


_ANY = pl.BlockSpec(memory_space=pl.ANY)
_MESH = pl.DeviceIdType.MESH


def _nt(a, b):
    return lax.dot_general(a, b, (((1,), (1,)), ((), ())), preferred_element_type=F32)


def _tn(a, b):
    return lax.dot_general(a, b, (((0,), (0,)), ((), ())), preferred_element_type=F32)


def _nn(a, b):
    return jnp.dot(a, b, preferred_element_type=F32)


def _sigmoid(x):
    return 0.5 * jnp.tanh(0.5 * x) + 0.5


def _split3(v):
    def top(x):
        bits = lax.bitcast_convert_type(x, jnp.uint32) & jnp.uint32(0xFFFF0000)
        return lax.bitcast_convert_type(bits, F32)

    v1 = top(v)
    r1 = v - v1
    v2 = top(r1)
    v3 = r1 - v2
    return v1.astype(jnp.bfloat16), v2.astype(jnp.bfloat16), v3.astype(jnp.bfloat16)


def _dot_exact(a, b, dn, data):
    if data == 0:
        mat = b.astype(jnp.bfloat16)
        return sum(lax.dot_general(p, mat, dn, preferred_element_type=F32) for p in _split3(a))
    mat = a.astype(jnp.bfloat16)
    return sum(lax.dot_general(mat, p, dn, preferred_element_type=F32) for p in _split3(b))


_DN_NN = (((1,), (0,)), ((), ()))
_DN_TN = (((0,), (0,)), ((), ()))


def _gelu(x):
    return 0.5 * x * (1.0 + lax.erf(x * (2.0 ** -0.5)))


def _gelu_grad(x):
    return 0.5 * (1.0 + lax.erf(x * (2.0 ** -0.5))) + x * jnp.exp(-0.5 * x * x) * (1.0 / math.sqrt(2.0 * math.pi))


def _mm(a, b, *, ta=False, tb=False, out_dtype=F32, after=None, name):
    if ta:
        kdim, m = a.shape
    else:
        m, kdim = a.shape
    if tb:
        n, k2 = b.shape[-2:]
    else:
        k2, n = b.shape[-2:]
    assert kdim == k2, (a.shape, b.shape, ta, tb)
    tm = _div_tile(m, MM_TILE_MAX)
    tn = _div_tile(n, MM_TILE_MAX)
    tk = _div_tile(kdim, MM_OPERAND_BYTES // (tm * a.dtype.itemsize + tn * b.dtype.itemsize))
    nk = kdim // tk
    dn = (((0 if ta else 1,), (1 if tb else 0,)), ((), ()))

    extra = [] if after is None else [after]

    def body(a_ref, b_ref, *rest):
        o_ref = rest[len(extra)]
        d = lax.dot_general(a_ref[...].astype(MXU_DTYPE), b_ref[...].astype(MXU_DTYPE), dn, preferred_element_type=F32)
        if nk == 1:
            o_ref[...] = d.astype(out_dtype)
            return
        acc_ref = rest[len(extra) + 1]
        k = pl.program_id(2)

        @pl.when(k == 0)
        def _():
            acc_ref[...] = d

        @pl.when(jnp.logical_and(k > 0, k < nk - 1))
        def _():
            acc_ref[...] += d

        @pl.when(k == nk - 1)
        def _():
            o_ref[...] = (acc_ref[...] + d).astype(out_dtype)

    a_spec = pl.BlockSpec((tk, tm), lambda i, j, k: (k, i)) if ta else pl.BlockSpec((tm, tk), lambda i, j, k: (i, k))
    b_spec = pl.BlockSpec((tn, tk), lambda i, j, k: (j, k)) if tb else pl.BlockSpec((tk, tn), lambda i, j, k: (k, j))
    return pl.pallas_call(
        body, grid=(m // tm, n // tn, nk), in_specs=[a_spec, b_spec] + [_ANY] * len(extra),
        out_specs=pl.BlockSpec((tm, tn), lambda i, j, k: (i, j)),
        out_shape=jax.ShapeDtypeStruct((m, n), out_dtype),
        scratch_shapes=[pltpu.VMEM((tm, tn), F32)] if nk > 1 else [],
        compiler_params=_params("parallel", "parallel", "arbitrary"), name=name)(a, b, *extra)


def _row_spec(tm, c, col=0):
    return pl.BlockSpec((tm, c), lambda i: (i, col))


def _par_spec(shape):
    nd = len(shape)
    return pl.BlockSpec(shape, lambda i: (0,) * nd)


def _ln_fwd(x, f, g, b, *, name):
    t, c = x.shape
    tm = _pick(t, ROW_TILES)
    has_f = f is not None

    def body(*refs):
        if has_f:
            x_ref, f_ref, g_ref, b_ref, y_ref, yb_ref, xh_ref, rs_ref = refs
            r = ALPHA * x_ref[...] + f_ref[...]
        else:
            x_ref, g_ref, b_ref, y_ref, yb_ref, xh_ref, rs_ref = refs
            r = x_ref[...]
        mu = jnp.mean(r, axis=-1, keepdims=True)
        xc = r - mu
        var = jnp.mean(xc * xc, axis=-1, keepdims=True)
        rstd = lax.rsqrt(var + LN_EPS)
        xh = xc * rstd
        y = xh * g_ref[...] + b_ref[...]
        y_ref[...] = y
        yb_ref[...] = y.astype(MXU_DTYPE)
        xh_ref[...] = xh
        rs_ref[...] = jnp.broadcast_to(rstd, rs_ref.shape)

    ins = [x] + ([f] if has_f else []) + [g.reshape(1, c), b.reshape(1, c)]
    in_specs = [_row_spec(tm, c)] * (2 if has_f else 1) + [_par_spec((1, c))] * 2
    return pl.pallas_call(
        body, grid=(t // tm,), in_specs=in_specs,
        out_specs=[_row_spec(tm, c), _row_spec(tm, c), _row_spec(tm, c), _row_spec(tm, LANE)],
        out_shape=[jax.ShapeDtypeStruct((t, c), F32), jax.ShapeDtypeStruct((t, c), MXU_DTYPE),
                   jax.ShapeDtypeStruct((t, c), F32), jax.ShapeDtypeStruct((t, LANE), F32)],
        compiler_params=_params("parallel"), name=name)(*ins)


def _ln_bwd(addends, scales, xh, rs, g, *, name):
    t, c = xh.shape
    tm = _pick(t, ROW_TILES)
    na = len(addends)

    def body(*refs):
        a_refs = refs[:na]
        xh_ref, rs_ref, g_ref, dp_ref, dpb_ref, dg_ref, db_ref = refs[na:]

        @pl.when(pl.program_id(0) == 0)
        def _():
            dg_ref[...] = jnp.zeros_like(dg_ref)
            db_ref[...] = jnp.zeros_like(db_ref)

        dy = None
        for s, r in zip(scales, a_refs):
            term = r[...] if s == 1.0 else s * r[...]
            dy = term if dy is None else dy + term
        xhv = xh_ref[...]
        dxh = dy * g_ref[...]
        m1 = jnp.mean(dxh, axis=-1, keepdims=True)
        m2 = jnp.mean(dxh * xhv, axis=-1, keepdims=True)
        dp = rs_ref[:, 0:1] * (dxh - m1 - xhv * m2)
        dp_ref[...] = dp
        dpb_ref[...] = dp.astype(MXU_DTYPE)
        dg_ref[...] += jnp.sum(dy * xhv, axis=0, keepdims=True)
        db_ref[...] += jnp.sum(dy, axis=0, keepdims=True)

    in_specs = [_row_spec(tm, c)] * (na + 1) + [_row_spec(tm, LANE), _par_spec((1, c))]
    return pl.pallas_call(
        body, grid=(t // tm,), in_specs=in_specs,
        out_specs=[_row_spec(tm, c), _row_spec(tm, c), _par_spec((1, c)), _par_spec((1, c))],
        out_shape=[jax.ShapeDtypeStruct((t, c), F32), jax.ShapeDtypeStruct((t, c), MXU_DTYPE),
                   jax.ShapeDtypeStruct((1, c), F32), jax.ShapeDtypeStruct((1, c), F32)],
        compiler_params=_params("arbitrary"), name=name)(*addends, xh, rs, g.reshape(1, c))


def _add_scaled(addends, scales, *, name):
    t, c = addends[0].shape
    tm = _pick(t, ROW_TILES)
    na = len(addends)

    def body(*refs):
        acc = None
        for s, r in zip(scales, refs[:na]):
            term = r[...] if s == 1.0 else s * r[...]
            acc = term if acc is None else acc + term
        refs[na][...] = acc

    return pl.pallas_call(
        body, grid=(t // tm,), in_specs=[_row_spec(tm, c)] * na, out_specs=_row_spec(tm, c),
        out_shape=jax.ShapeDtypeStruct((t, c), F32), compiler_params=_params("parallel"), name=name)(*addends)


def _loss_head(y, tgt, *, name):
    t, c = y.shape
    tm = _pick(t, ROW_TILES)

    def body(y_ref, t_ref, dy_ref, ls_ref):
        @pl.when(pl.program_id(0) == 0)
        def _():
            ls_ref[...] = jnp.zeros_like(ls_ref)

        e = y_ref[...] - t_ref[...]
        dy_ref[...] = e * (1.0 / c)
        ls_ref[...] += jnp.sum(e * e, axis=0, keepdims=True)

    return pl.pallas_call(
        body, grid=(t // tm,), in_specs=[_row_spec(tm, c)] * 2,
        out_specs=[_row_spec(tm, c), _par_spec((1, c))],
        out_shape=[jax.ShapeDtypeStruct((t, c), F32), jax.ShapeDtypeStruct((1, c), F32)],
        compiler_params=_params("arbitrary"), name=name)(y, tgt)


def _swiglu_fwd(h, *, name):
    t, two_f = h.shape
    fh = two_f // 2
    tm = _pick(t, (256, 128))

    def body(g_ref, u_ref, a_ref):
        g = g_ref[...]
        a_ref[...] = (g * _sigmoid(g) * u_ref[...]).astype(MXU_DTYPE)

    return pl.pallas_call(
        body, grid=(t // tm,), in_specs=[_row_spec(tm, fh, 0), _row_spec(tm, fh, 1)], out_specs=_row_spec(tm, fh),
        out_shape=jax.ShapeDtypeStruct((t, fh), MXU_DTYPE), compiler_params=_params("parallel"), name=name)(h, h)


def _swiglu_bwd(h, da, *, name):
    t, two_f = h.shape
    fh = two_f // 2
    tm = _pick(t, (256, 128))

    def body(g_ref, u_ref, da_ref, dh_ref):
        g = g_ref[...]
        s = _sigmoid(g)
        dav = da_ref[...]
        dh_ref[:, :fh] = (dav * u_ref[...] * (s * (1.0 + g * (1.0 - s)))).astype(MXU_DTYPE)
        dh_ref[:, fh:] = (dav * g * s).astype(MXU_DTYPE)

    return pl.pallas_call(
        body, grid=(t // tm,), in_specs=[_row_spec(tm, fh, 0), _row_spec(tm, fh, 1), _row_spec(tm, fh)],
        out_specs=_row_spec(tm, two_f), out_shape=jax.ShapeDtypeStruct((t, two_f), MXU_DTYPE),
        compiler_params=_params("parallel"), name=name)(h, h, da)


def _attn_probs(q, k):
    s = _nt(q, k) * (X_HEADDIM ** -0.5)
    s = s - jnp.max(s, axis=-1, keepdims=True)
    p = jnp.exp(s)
    return p / jnp.sum(p, axis=-1, keepdims=True)


def _attn_fwd(q, kv, *, bsz, name):
    t = q.shape[0]
    s = t // bsz
    ml = kv.shape[0] // bsz
    hd = X_HEADDIM

    def body(q_ref, k_ref, v_ref, o_ref):
        p = _attn_probs(q_ref[...], k_ref[...])
        o_ref[...] = _nn(p.astype(MXU_DTYPE), v_ref[...]).astype(MXU_DTYPE)

    return pl.pallas_call(
        body, grid=(bsz, X_HEADS),
        in_specs=[pl.BlockSpec((s, hd), lambda b, h: (b, h)), pl.BlockSpec((ml, hd), lambda b, h: (b, h)),
                  pl.BlockSpec((ml, hd), lambda b, h: (b, X_HEADS + h))],
        out_specs=pl.BlockSpec((s, hd), lambda b, h: (b, h)),
        out_shape=jax.ShapeDtypeStruct((t, D_MODEL), MXU_DTYPE),
        compiler_params=_params("parallel", "parallel"), name=name)(q, kv, kv)


def _attn_bwd(q, kv, do, *, bsz, name):
    t = q.shape[0]
    s = t // bsz
    ml = kv.shape[0] // bsz
    hd = X_HEADDIM

    def body(q_ref, k_ref, v_ref, do_ref, dq_ref, dk_ref, dv_ref):
        qv, kk, vv, dov = q_ref[...], k_ref[...], v_ref[...], do_ref[...]
        p = _attn_probs(qv, kk)
        dp = _nt(dov, vv)
        dv_ref[...] = _tn(p.astype(MXU_DTYPE), dov).astype(MXU_DTYPE)
        ds = (p * (dp - jnp.sum(dp * p, axis=-1, keepdims=True)) * (X_HEADDIM ** -0.5)).astype(MXU_DTYPE)
        dq_ref[...] = _nn(ds, kk).astype(MXU_DTYPE)
        dk_ref[...] = _tn(ds, qv).astype(MXU_DTYPE)

    blk_q = pl.BlockSpec((s, hd), lambda b, h: (b, h))
    blk_m = pl.BlockSpec((ml, hd), lambda b, h: (b, h))
    return pl.pallas_call(
        body, grid=(bsz, X_HEADS),
        in_specs=[blk_q, blk_m, pl.BlockSpec((ml, hd), lambda b, h: (b, X_HEADS + h)), blk_q],
        out_specs=[blk_q, blk_m, blk_m],
        out_shape=[jax.ShapeDtypeStruct((t, D_MODEL), MXU_DTYPE), jax.ShapeDtypeStruct((bsz * ml, D_MODEL), MXU_DTYPE),
                   jax.ShapeDtypeStruct((bsz * ml, D_MODEL), MXU_DTYPE)],
        compiler_params=_params("parallel", "parallel"), name=name)(q, kv, kv, do)


def _causal(n):
    row = lax.broadcasted_iota(jnp.int32, (n, n), 0)
    col = lax.broadcasted_iota(jnp.int32, (n, n), 1)
    return row >= col


def _sg_norm(v, g, b):
    gv = _gelu(v)
    mu = jnp.mean(gv, axis=-1, keepdims=True)
    xc = gv - mu
    var = jnp.mean(xc * xc, axis=-1, keepdims=True)
    rstd = lax.rsqrt(var + LN_EPS)
    xh = xc * rstd
    return xh, rstd, xh * g + b


def _sg_fwd(proj, ln_g, ln_b, w, bcol, *, name):
    t = proj.shape[0]
    c = D_MODEL
    gd = c // SG_GROUPS

    def body(u_ref, v_ref, g_ref, b_ref, w_ref, bc_ref, o_ref):
        gu = _gelu(u_ref[...])
        _, _, vn = _sg_norm(v_ref[...], g_ref[...], b_ref[...])
        mask = _causal(CHUNK)
        for g in range(SG_GROUPS):
            sl = slice(g * gd, (g + 1) * gd)
            wg = jnp.where(mask, w_ref[g], 0.0).astype(MXU_DTYPE)
            mixed = _nn(wg, vn[:, sl].astype(MXU_DTYPE)) + bc_ref[g]
            o_ref[:, sl] = (gu[:, sl] * mixed).astype(MXU_DTYPE)

    return pl.pallas_call(
        body, grid=(t // CHUNK,),
        in_specs=[_row_spec(CHUNK, c, 0), _row_spec(CHUNK, c, 1), _par_spec((1, c)), _par_spec((1, c)),
                  _par_spec((SG_GROUPS, CHUNK, CHUNK)), _par_spec((SG_GROUPS, CHUNK, 1))],
        out_specs=_row_spec(CHUNK, c), out_shape=jax.ShapeDtypeStruct((t, c), MXU_DTYPE),
        compiler_params=_params("parallel"), name=name)(proj, proj, ln_g.reshape(1, c), ln_b.reshape(1, c), w, bcol)


def _sg_bwd(proj, dsgo, ln_g, ln_b, w, bcol, dproj, *, name):
    t = proj.shape[0]
    c = D_MODEL
    gd = c // SG_GROUPS

    def body(u_ref, v_ref, d_ref, g_ref, b_ref, w_ref, bc_ref, _, duv_ref, dw_ref, dbc_ref, dg_ref, db_ref, dvn_ref):
        @pl.when(pl.program_id(0) == 0)
        def _():
            dw_ref[...] = jnp.zeros_like(dw_ref)
            dbc_ref[...] = jnp.zeros_like(dbc_ref)
            dg_ref[...] = jnp.zeros_like(dg_ref)
            db_ref[...] = jnp.zeros_like(db_ref)

        u = u_ref[...]
        v = v_ref[...]
        dso = d_ref[...]
        gu = _gelu(u)
        xh, rstd, vn = _sg_norm(v, g_ref[...], b_ref[...])
        mask = _causal(CHUNK)
        for g in range(SG_GROUPS):
            sl = slice(g * gd, (g + 1) * gd)
            wg = jnp.where(mask, w_ref[g], 0.0).astype(MXU_DTYPE)
            vng = vn[:, sl].astype(MXU_DTYPE)
            mixed = _nn(wg, vng) + bc_ref[g]
            duv_ref[:, sl] = (dso[:, sl] * mixed * _gelu_grad(u[:, sl])).astype(MXU_DTYPE)
            dmix = dso[:, sl] * gu[:, sl]
            dmb = dmix.astype(MXU_DTYPE)
            dbc_ref[g] += jnp.sum(dmix, axis=-1, keepdims=True)
            dw_ref[g] += jnp.where(mask, _nt(dmb, vng), 0.0)
            dvn_ref[:, sl] = _tn(wg, dmb)
        dvn = dvn_ref[...]
        dg_ref[...] += jnp.sum(dvn * xh, axis=0, keepdims=True)
        db_ref[...] += jnp.sum(dvn, axis=0, keepdims=True)
        dxh = dvn * g_ref[...]
        m1 = jnp.mean(dxh, axis=-1, keepdims=True)
        m2 = jnp.mean(dxh * xh, axis=-1, keepdims=True)
        dgv = rstd * (dxh - m1 - xh * m2)
        duv_ref[:, c:] = (dgv * _gelu_grad(v)).astype(MXU_DTYPE)

    return pl.pallas_call(
        body, grid=(t // CHUNK,),
        in_specs=[_row_spec(CHUNK, c, 0), _row_spec(CHUNK, c, 1), _row_spec(CHUNK, c), _par_spec((1, c)),
                  _par_spec((1, c)), _par_spec((SG_GROUPS, CHUNK, CHUNK)), _par_spec((SG_GROUPS, CHUNK, 1)), _ANY],
        out_specs=[_row_spec(CHUNK, 2 * c), _par_spec((SG_GROUPS, CHUNK, CHUNK)), _par_spec((SG_GROUPS, CHUNK, 1)),
                   _par_spec((1, c)), _par_spec((1, c))],
        out_shape=[jax.ShapeDtypeStruct(dproj.shape, dproj.dtype), jax.ShapeDtypeStruct((SG_GROUPS, CHUNK, CHUNK), F32),
                   jax.ShapeDtypeStruct((SG_GROUPS, CHUNK, 1), F32), jax.ShapeDtypeStruct((1, c), F32),
                   jax.ShapeDtypeStruct((1, c), F32)],
        scratch_shapes=[pltpu.VMEM((CHUNK, c), F32)], input_output_aliases={7: 0},
        compiler_params=_params("arbitrary"), name=name)(proj, proj, dsgo, ln_g.reshape(1, c), ln_b.reshape(1, c), w, bcol, dproj)


CONV_TC = 512


def _conv_taps(x):
    rows = lax.broadcasted_iota(jnp.int32, x.shape, 0)
    taps = [jnp.where(rows >= SSM_CONV - 1 - k, pltpu.roll(x, SSM_CONV - 1 - k, axis=0), 0.0) for k in range(SSM_CONV - 1)]
    return taps + [x]


def _conv_pre(taps, w_ref, b_ref):
    acc = b_ref[...]
    for k in range(SSM_CONV):
        acc = acc + taps[k] * w_ref[k:k + 1, :]
    return acc


def _conv_fwd(proj, w, b, *, bsz, name):
    t = proj.shape[0]
    s = t // bsz
    nj = SSM_CONV_DIM // CONV_TC
    c0 = XBC_COL0 // CONV_TC

    def body(x_ref, w_ref, b_ref, o_ref):
        pre = _conv_pre(_conv_taps(x_ref[...]), w_ref, b_ref)
        o_ref[...] = pre * _sigmoid(pre)

    return pl.pallas_call(
        body, grid=(bsz, nj),
        in_specs=[pl.BlockSpec((s, CONV_TC), lambda bb, j: (bb, c0 + j)), pl.BlockSpec((SSM_CONV, CONV_TC), lambda bb, j: (0, j)),
                  pl.BlockSpec((1, CONV_TC), lambda bb, j: (0, j))],
        out_specs=pl.BlockSpec((s, CONV_TC), lambda bb, j: (bb, j)),
        out_shape=jax.ShapeDtypeStruct((t, SSM_CONV_DIM), F32),
        compiler_params=_params("parallel", "parallel"), name=name)(proj, w, b.reshape(1, -1))


def _conv_bwd(proj, dact, w, b, dproj, *, bsz, name):
    t = proj.shape[0]
    s = t // bsz
    nj = SSM_CONV_DIM // CONV_TC
    c0 = XBC_COL0 // CONV_TC

    def body(x_ref, d_ref, w_ref, b_ref, _, dx_ref, dw_ref, db_ref):
        @pl.when(pl.program_id(1) == 0)
        def _():
            dw_ref[...] = jnp.zeros_like(dw_ref)
            db_ref[...] = jnp.zeros_like(db_ref)

        taps = _conv_taps(x_ref[...])
        pre = _conv_pre(taps, w_ref, b_ref)
        sg = _sigmoid(pre)
        dpre = d_ref[...] * (sg * (1.0 + pre * (1.0 - sg)))
        rows = lax.broadcasted_iota(jnp.int32, dpre.shape, 0)
        db_ref[...] += jnp.sum(dpre, axis=0, keepdims=True)
        dx = dpre * w_ref[SSM_CONV - 1:SSM_CONV, :]
        for k in range(SSM_CONV):
            dw_ref[k:k + 1, :] += jnp.sum(dpre * taps[k], axis=0, keepdims=True)
        for k in range(SSM_CONV - 1):
            sh = SSM_CONV - 1 - k
            dsh = jnp.where(rows < s - sh, pltpu.roll(dpre, s - sh, axis=0), 0.0)
            dx = dx + dsh * w_ref[k:k + 1, :]
        dx_ref[...] = dx.astype(MXU_DTYPE)

    return pl.pallas_call(
        body, grid=(nj, bsz),
        in_specs=[pl.BlockSpec((s, CONV_TC), lambda j, bb: (bb, c0 + j)), pl.BlockSpec((s, CONV_TC), lambda j, bb: (bb, j)),
                  pl.BlockSpec((SSM_CONV, CONV_TC), lambda j, bb: (0, j)), pl.BlockSpec((1, CONV_TC), lambda j, bb: (0, j)), _ANY],
        out_specs=[pl.BlockSpec((s, CONV_TC), lambda j, bb: (bb, c0 + j)), pl.BlockSpec((SSM_CONV, CONV_TC), lambda j, bb: (0, j)),
                   pl.BlockSpec((1, CONV_TC), lambda j, bb: (0, j))],
        out_shape=[jax.ShapeDtypeStruct(dproj.shape, dproj.dtype), jax.ShapeDtypeStruct((SSM_CONV, SSM_CONV_DIM), F32),
                   jax.ShapeDtypeStruct((1, SSM_CONV_DIM), F32)],
        input_output_aliases={4: 0},
        compiler_params=_params("parallel", "arbitrary"), name=name)(proj, dact, w, b.reshape(1, -1), dproj)


def _softplus(x):
    return jnp.maximum(x, 0.0) + jnp.log1p(jnp.exp(-jnp.abs(x)))


def _pad_heads(v):
    return jnp.broadcast_to(jnp.pad(v.astype(F32), (0, HEAD_PAD - SSM_HEADS))[None, :], (SUBLANE, HEAD_PAD))


def _ssd_prep(dt_raw, dt_bias8, a_log8, *, name):
    t = dt_raw.shape[0]
    n = CHUNK

    def body(r_ref, b_ref, al_ref, dt_ref, cs_ref, dtt_ref, cst_ref):
        dt = _softplus(r_ref[...] + b_ref[0:1, :])
        da = dt * (-jnp.exp(al_ref[0:1, :]))
        row = lax.broadcasted_iota(jnp.int32, (n, n), 0)
        col = lax.broadcasted_iota(jnp.int32, (n, n), 1)
        lower = (col <= row).astype(F32)
        upper = (row <= col).astype(F32)
        eye = (row == col).astype(F32)
        dt_ref[...] = dt
        cs_ref[...] = _dot_exact(lower, da, _DN_NN, 1)
        cst_ref[0] = _dot_exact(da, upper, _DN_TN, 0)
        dtt_ref[0] = _dot_exact(dt, eye, _DN_TN, 0)

    hp = HEAD_PAD
    return pl.pallas_call(
        body, grid=(t // n,),
        in_specs=[_row_spec(n, hp), _par_spec((SUBLANE, hp)), _par_spec((SUBLANE, hp))],
        out_specs=[_row_spec(n, hp), _row_spec(n, hp), pl.BlockSpec((1, hp, n), lambda i: (i, 0, 0)),
                   pl.BlockSpec((1, hp, n), lambda i: (i, 0, 0))],
        out_shape=[jax.ShapeDtypeStruct((t, hp), F32), jax.ShapeDtypeStruct((t, hp), F32),
                   jax.ShapeDtypeStruct((t // n, hp, n), F32), jax.ShapeDtypeStruct((t // n, hp, n), F32)],
        compiler_params=_params("parallel"), name=name)(dt_raw, dt_bias8, a_log8)


def _expand_mat():
    h = lax.broadcasted_iota(jnp.int32, (HEAD_PAD, SSM_INNER), 0)
    ch = lax.broadcasted_iota(jnp.int32, (HEAD_PAD, SSM_INNER), 1)
    return (ch // SSM_HEADDIM == h).astype(F32)


def _reduce_mat():
    ch = lax.broadcasted_iota(jnp.int32, (SSM_INNER, HEAD_PAD), 0)
    h = lax.broadcasted_iota(jnp.int32, (SSM_INNER, HEAD_PAD), 1)
    return (ch // SSM_HEADDIM == h).astype(F32)


def _expand(v, em):
    return _dot_exact(v, em, _DN_NN, 0)


def _expand_heads(v):
    return jnp.repeat(v.astype(F32), SSM_HEADDIM)[None, :]


def _decay_mat(cs_ref, cst_ref, h, mask):
    seg = cs_ref[:, h:h + 1] - cst_ref[0, h:h + 1, :]
    return jnp.where(mask, jnp.exp(jnp.minimum(seg, 0.0)), 0.0)


GROUP_CH = SSM_INNER // SSM_GROUPS
PAIRS_PER_GROUP = GROUP_CH // LANE
HEADS_PER_GROUP = SSM_HEADS // SSM_GROUPS
BM_COL0 = SSM_INNER
CM_COL0 = SSM_INNER + SSM_GROUPS * SSM_STATE


def _ssd_specs(nc, rev):
    def cidx(i):
        return (i // nc) * nc + (nc - 1 - i % nc) if rev else i

    n = CHUNK
    xs = pl.BlockSpec((n, SSM_INNER), lambda i: (cidx(i), 0))
    bm = pl.BlockSpec((n, GROUP_CH), lambda i: (cidx(i), BM_COL0 // GROUP_CH))
    cm = pl.BlockSpec((n, GROUP_CH), lambda i: (cidx(i), CM_COL0 // GROUP_CH))
    hv = pl.BlockSpec((n, HEAD_PAD), lambda i: (cidx(i), 0))
    hvt = pl.BlockSpec((1, HEAD_PAD, n), lambda i: (cidx(i), 0, 0))
    st = pl.BlockSpec((1, SSM_INNER, SSM_STATE), lambda i: (cidx(i), 0, 0))
    return xs, bm, cm, hv, hvt, st


def _ssd_fwd(xbc, dt, cs, dtt, cst, dskx, *, nc, name):
    t = xbc.shape[0]
    n = CHUNK
    xs_s, bm_s, cm_s, hv_s, hvt_s, st_s = _ssd_specs(nc, False)

    def body(xs_ref, bm_ref, cm_ref, dt_ref, cs_ref, dtt_ref, cst_ref, dsk_ref, y_ref, st_ref, prev):
        @pl.when(pl.program_id(0) % nc == 0)
        def _():
            prev[...] = jnp.zeros_like(prev)

        st_ref[0] = prev[...]
        em = _expand_mat()
        dtx = _expand(dt_ref[...], em)
        csx = _expand(cs_ref[...], em)
        dskx = dsk_ref[...]
        xs = xs_ref[...]
        xdt = xs * dtx
        ecs = jnp.exp(csx)
        dec = jnp.exp(csx[n - 1:n, :] - csx)
        mask = _causal(n)
        lane = lax.broadcasted_iota(jnp.int32, (n, LANE), 1)
        for g in range(SSM_GROUPS):
            gs = slice(g * SSM_STATE, (g + 1) * SSM_STATE)
            gc = slice(g * GROUP_CH, (g + 1) * GROUP_CH)
            cmat = cm_ref[:, gs].astype(MXU_DTYPE)
            bmat = bm_ref[:, gs].astype(MXU_DTYPE)
            cb = _nt(cmat, bmat)
            yoff = ecs[:, gc] * _nt(cmat, prev[gc, :].astype(MXU_DTYPE))
            for q in range(PAIRS_PER_GROUP):
                hp = g * PAIRS_PER_GROUP + q
                sl = slice(hp * LANE, (hp + 1) * LANE)
                xp = xdt[:, sl].astype(MXU_DTYPE)
                m0 = (cb * _decay_mat(cs_ref, cst_ref, 2 * hp, mask)).astype(MXU_DTYPE)
                m1 = (cb * _decay_mat(cs_ref, cst_ref, 2 * hp + 1, mask)).astype(MXU_DTYPE)
                yd = jnp.where(lane < SSM_HEADDIM, _nn(m0, xp), _nn(m1, xp))
                y_ref[:, sl] = yd + yoff[:, q * LANE:(q + 1) * LANE] + xs[:, sl] * dskx[:, sl]
            snew = _tn((xdt[:, gc] * dec[:, gc]).astype(MXU_DTYPE), bmat)
            for r in range(HEADS_PER_GROUP):
                h = g * HEADS_PER_GROUP + r
                rows = slice(h * SSM_HEADDIM, (h + 1) * SSM_HEADDIM)
                e = jnp.exp(cst_ref[0, h:h + 1, n - 1:n])
                prev[rows, :] = prev[rows, :] * e + snew[r * SSM_HEADDIM:(r + 1) * SSM_HEADDIM, :]

    return pl.pallas_call(
        body, grid=(t // n,),
        in_specs=[xs_s, bm_s, cm_s, hv_s, hv_s, hvt_s, hvt_s, _par_spec((1, SSM_INNER))],
        out_specs=[xs_s, st_s],
        out_shape=[jax.ShapeDtypeStruct((t, SSM_INNER), F32), jax.ShapeDtypeStruct((t // n, SSM_INNER, SSM_STATE), F32)],
        scratch_shapes=[pltpu.VMEM((SSM_INNER, SSM_STATE), F32)],
        compiler_params=_params("arbitrary"), name=name)(xbc, xbc, xbc, dt, cs, dtt, cst, dskx)


def _ssd_bwd(dy, xbc, dt, cs, dtt, cst, st, dskx, a_log8, dt_raw, dt_bias8, *, nc, name):
    t = xbc.shape[0]
    n = CHUNK
    xs_s, bm_s, cm_s, hv_s, hvt_s, st_s = _ssd_specs(nc, True)
    acc_s = _par_spec((1, HEAD_PAD))
    xbc_s = pl.BlockSpec((n, SSM_CONV_DIM), xs_s.index_map)

    def body(dy_ref, xs_ref, bm_ref, cm_ref, dt_ref, cs_ref, dtt_ref, cst_ref, st_ref, dsk_ref, al_ref, raw_ref, bias_ref,
             dxbc_ref, ddr_ref, dal_ref, dds_ref, dbias_ref, dprev, dxdt_s, tdec_s, tcs_s):
        @pl.when(pl.program_id(0) % nc == 0)
        def _():
            dprev[...] = jnp.zeros_like(dprev)

        @pl.when(pl.program_id(0) == 0)
        def _():
            dal_ref[...] = jnp.zeros_like(dal_ref)
            dds_ref[...] = jnp.zeros_like(dds_ref)
            dbias_ref[...] = jnp.zeros_like(dbias_ref)

        em = _expand_mat()
        rm = _reduce_mat()

        def head_reduce(v):
            return _dot_exact(v, rm, _DN_NN, 0)

        dtv = dt_ref[...]
        csv = cs_ref[...]
        dtx = _expand(dtv, em)
        csx = _expand(csv, em)
        dskx = dsk_ref[...]
        xs = xs_ref[...]
        dyv = dy_ref[...]
        xdt = xs * dtx
        ecs = jnp.exp(csx)
        dec = jnp.exp(csx[n - 1:n, :] - csx)
        mask = _causal(n)
        lane = lax.broadcasted_iota(jnp.int32, (n, LANE), 1)
        hlane = lax.broadcasted_iota(jnp.int32, (1, HEAD_PAD), 1)
        hsub = lax.broadcasted_iota(jnp.int32, (HEAD_PAD, 1), 0)
        rsum = jnp.zeros((n, HEAD_PAD), F32)
        csum = jnp.zeros((HEAD_PAD, n), F32)
        for g in range(SSM_GROUPS):
            gs = slice(g * SSM_STATE, (g + 1) * SSM_STATE)
            gc = slice(g * GROUP_CH, (g + 1) * GROUP_CH)
            cmat = cm_ref[:, gs].astype(MXU_DTYPE)
            bmat = bm_ref[:, gs].astype(MXU_DTYPE)
            cb = _nt(cmat, bmat)
            pg = st_ref[0, gc, :].astype(MXU_DTYPE)
            dpg = dprev[gc, :]
            dpgb = dpg.astype(MXU_DTYPE)
            z = _nt(cmat, pg)
            dyg = dyv[:, gc]
            dz = (dyg * ecs[:, gc]).astype(MXU_DTYPE)
            dc = _nn(dz, pg)
            dprev_y = _tn(dz, cmat)
            tcs_s[:, gc] = dyg * z * ecs[:, gc]
            xd = xdt[:, gc] * dec[:, gc]
            wmat = _nt(bmat, dpgb)
            db = _nn(xd.astype(MXU_DTYPE), dpgb)
            tdec_s[:, gc] = wmat * xd
            dxdt_g = wmat * dec[:, gc]
            dcb = jnp.zeros((n, n), F32)
            for q in range(PAIRS_PER_GROUP):
                hp = g * PAIRS_PER_GROUP + q
                sl = slice(hp * LANE, (hp + 1) * LANE)
                xp = xdt[:, sl].astype(MXU_DTYPE)
                dyp = dyv[:, sl]
                dypb = dyp.astype(MXU_DTYPE)
                dxp = None
                for hh in range(2):
                    h = 2 * hp + hh
                    lm = _decay_mat(cs_ref, cst_ref, h, mask)
                    mine = (lane < SSM_HEADDIM) if hh == 0 else (lane >= SSM_HEADDIM)
                    dm = _nt(jnp.where(mine, dyp, 0.0).astype(MXU_DTYPE), xp)
                    dml = dm * lm
                    dcb = dcb + dml
                    gseg = dml * cb
                    rsum = rsum + jnp.sum(gseg, axis=1, keepdims=True) * (hlane == h).astype(F32)
                    csum = csum + (hsub == h).astype(F32) * jnp.sum(gseg, axis=0, keepdims=True)
                    dxh = _tn((cb * lm).astype(MXU_DTYPE), dypb)
                    dxp = dxh if dxp is None else jnp.where(mine, dxh, dxp)
                dxdt_s[:, sl] = dxdt_g[:, q * LANE:(q + 1) * LANE] + dxp
            dcbb = dcb.astype(MXU_DTYPE)
            dxbc_ref[:, CM_COL0 + g * SSM_STATE:CM_COL0 + (g + 1) * SSM_STATE] = dc + _nn(dcbb, bmat)
            dxbc_ref[:, BM_COL0 + g * SSM_STATE:BM_COL0 + (g + 1) * SSM_STATE] = db + _tn(dcbb, cmat)
            for r in range(HEADS_PER_GROUP):
                h = g * HEADS_PER_GROUP + r
                rows = slice(h * SSM_HEADDIM, (h + 1) * SSM_HEADDIM)
                lr = slice(r * SSM_HEADDIM, (r + 1) * SSM_HEADDIM)
                e = jnp.exp(cst_ref[0, h:h + 1, n - 1:n])
                dprev[rows, :] = dpg[lr, :] * e + dprev_y[lr, :]
            tq = _dot_exact(dpg * st_ref[0, gc, :], rm[gc, :], _DN_TN, 0)
            if g == 0:
                qsum = jnp.sum(tq, axis=0, keepdims=True)
            else:
                qsum = qsum + jnp.sum(tq, axis=0, keepdims=True)
        dxdt = dxdt_s[...]
        dxbc_ref[:, 0:SSM_INNER] = dxdt * dtx + dyv * dskx
        ddt = head_reduce(dxdt * xs)
        edec = head_reduce(tdec_s[...])
        ycs = head_reduce(tcs_s[...])
        row = lax.broadcasted_iota(jnp.int32, (n, HEAD_PAD), 0)
        extra = jnp.sum(edec, axis=0, keepdims=True) + qsum * jnp.exp(csv[n - 1:n, :])
        dcs = rsum - csum.T + ycs - edec + jnp.where(row == n - 1, extra, 0.0)
        r2 = lax.broadcasted_iota(jnp.int32, (n, n), 0)
        c2 = lax.broadcasted_iota(jnp.int32, (n, n), 1)
        dda = _dot_exact((c2 >= r2).astype(F32), dcs, _DN_NN, 1)
        a_row = -jnp.exp(al_ref[0:1, :])
        ddt = ddt + dda * a_row
        dal_ref[...] += jnp.sum(dda * dtv, axis=0, keepdims=True) * a_row
        dds_ref[...] += jnp.sum(head_reduce(dyv * xs), axis=0, keepdims=True)
        ddr = ddt * _sigmoid(raw_ref[...] + bias_ref[0:1, :])
        ddr_ref[...] = ddr
        dbias_ref[...] += jnp.sum(ddr, axis=0, keepdims=True)

    par8 = _par_spec((SUBLANE, HEAD_PAD))
    return pl.pallas_call(
        body, grid=(t // n,),
        in_specs=[xs_s, xs_s, bm_s, cm_s, hv_s, hv_s, hvt_s, hvt_s, st_s, _par_spec((1, SSM_INNER)), par8, hv_s, par8],
        out_specs=[xbc_s, hv_s, acc_s, acc_s, acc_s],
        out_shape=[jax.ShapeDtypeStruct((t, SSM_CONV_DIM), F32), jax.ShapeDtypeStruct((t, HEAD_PAD), F32),
                   jax.ShapeDtypeStruct((1, HEAD_PAD), F32), jax.ShapeDtypeStruct((1, HEAD_PAD), F32),
                   jax.ShapeDtypeStruct((1, HEAD_PAD), F32)],
        scratch_shapes=[pltpu.VMEM((SSM_INNER, SSM_STATE), F32), pltpu.VMEM((n, SSM_INNER), F32),
                        pltpu.VMEM((n, SSM_INNER), F32), pltpu.VMEM((n, SSM_INNER), F32)],
        compiler_params=_params("arbitrary"), name=name)(dy, xbc, xbc, xbc, dt, cs, dtt, cst, st, dskx, a_log8, dt_raw, dt_bias8)


def _gate_norm_fwd(y, proj, norm_g, *, name):
    t, c = y.shape
    tm = _pick(t, (256, 128))

    def body(y_ref, z_ref, g_ref, o_ref):
        z = z_ref[...]
        yz = y_ref[...] * z * _sigmoid(z)
        for g in range(SSM_GROUPS):
            gc = slice(g * GROUP_CH, (g + 1) * GROUP_CH)
            seg = yz[:, gc]
            r = lax.rsqrt(jnp.mean(seg * seg, axis=-1, keepdims=True) + RMS_EPS)
            o_ref[:, gc] = (seg * r * g_ref[:, gc]).astype(MXU_DTYPE)

    return pl.pallas_call(
        body, grid=(t // tm,), in_specs=[_row_spec(tm, c), _row_spec(tm, c, 1), _par_spec((1, c))],
        out_specs=_row_spec(tm, c), out_shape=jax.ShapeDtypeStruct((t, c), MXU_DTYPE),
        compiler_params=_params("parallel"), name=name)(y, proj, norm_g.reshape(1, c))


def _gate_norm_bwd(dyb, y, proj, norm_g, dproj, *, name):
    t, c = y.shape
    tm = _pick(t, (256, 128))

    def body(d_ref, y_ref, z_ref, g_ref, _, dy_ref, dz_ref, dg_ref):
        @pl.when(pl.program_id(0) == 0)
        def _():
            dg_ref[...] = jnp.zeros_like(dg_ref)

        z = z_ref[...]
        yv = y_ref[...]
        sz = _sigmoid(z)
        silu = z * sz
        yz = yv * silu
        dv = d_ref[...]
        for g in range(SSM_GROUPS):
            gc = slice(g * GROUP_CH, (g + 1) * GROUP_CH)
            seg = yz[:, gc]
            r = lax.rsqrt(jnp.mean(seg * seg, axis=-1, keepdims=True) + RMS_EPS)
            nrm = seg * r
            dn = dv[:, gc] * g_ref[:, gc]
            dg_ref[:, gc] += jnp.sum(dv[:, gc] * nrm, axis=0, keepdims=True)
            dyz = r * (dn - nrm * jnp.mean(dn * nrm, axis=-1, keepdims=True))
            dy_ref[:, gc] = dyz * silu[:, gc]
            dz_ref[:, gc] = (dyz * yv[:, gc] * (sz[:, gc] * (1.0 + z[:, gc] * (1.0 - sz[:, gc])))).astype(MXU_DTYPE)

    return pl.pallas_call(
        body, grid=(t // tm,), in_specs=[_row_spec(tm, c), _row_spec(tm, c), _row_spec(tm, c, 1), _par_spec((1, c)), _ANY],
        out_specs=[_row_spec(tm, c), _row_spec(tm, c, 1), _par_spec((1, c))],
        out_shape=[jax.ShapeDtypeStruct((t, c), F32), jax.ShapeDtypeStruct(dproj.shape, dproj.dtype),
                   jax.ShapeDtypeStruct((1, c), F32)],
        input_output_aliases={4: 1},
        compiler_params=_params("arbitrary"), name=name)(dyb, y, proj, norm_g.reshape(1, c), dproj)


GA_COLBLK = GAB_COL0 // D_MODEL


def _merge_fwd(br_a, br_b, proj, *, name):
    t, c = br_a.shape
    tm = _pick(t, ROW_TILES)

    def body(a_ref, b_ref, ga_ref, gb_ref, o_ref):
        o_ref[...] = (_sigmoid(ga_ref[...]) * a_ref[...] + _sigmoid(gb_ref[...]) * b_ref[...]).astype(MXU_DTYPE)

    return pl.pallas_call(
        body, grid=(t // tm,),
        in_specs=[_row_spec(tm, c), _row_spec(tm, c), _row_spec(tm, c, GA_COLBLK), _row_spec(tm, c, GA_COLBLK + 1)],
        out_specs=_row_spec(tm, c), out_shape=jax.ShapeDtypeStruct((t, c), MXU_DTYPE),
        compiler_params=_params("parallel"), name=name)(br_a, br_b, proj, proj)


def _merge_bwd(dm, br_a, br_b, proj, *, name):
    t, c = br_a.shape
    tm = _pick(t, ROW_TILES)

    def body(dm_ref, a_ref, b_ref, ga_ref, gb_ref, da_ref, db_ref, dg_ref):
        d = dm_ref[...]
        sa = _sigmoid(ga_ref[...])
        sb = _sigmoid(gb_ref[...])
        da_ref[...] = (d * sa).astype(MXU_DTYPE)
        db_ref[...] = (d * sb).astype(MXU_DTYPE)
        dg_ref[:, :c] = (d * a_ref[...] * sa * (1.0 - sa)).astype(MXU_DTYPE)
        dg_ref[:, c:] = (d * b_ref[...] * sb * (1.0 - sb)).astype(MXU_DTYPE)

    return pl.pallas_call(
        body, grid=(t // tm,),
        in_specs=[_row_spec(tm, c), _row_spec(tm, c), _row_spec(tm, c), _row_spec(tm, c, GA_COLBLK), _row_spec(tm, c, GA_COLBLK + 1)],
        out_specs=[_row_spec(tm, c), _row_spec(tm, c), _row_spec(tm, 2 * c, GAB_COL0 // (2 * c))],
        out_shape=[jax.ShapeDtypeStruct((t, c), MXU_DTYPE), jax.ShapeDtypeStruct((t, c), MXU_DTYPE),
                   jax.ShapeDtypeStruct((t, MAIN_COLS), MXU_DTYPE)],
        compiler_params=_params("parallel"), name=name)(dm, br_a, br_b, proj, proj)


def _layer_fwd(x, xb, memn_b, w, *, bsz, tag):
    nc = x.shape[0] // bsz // CHUNK
    sv = {"x_in": xb}
    proj = _mm(xb, w["w_main"], name=f"{tag}_proj")
    dt_raw = _mm(xb, w["w_dt"], name=f"{tag}_dtproj")
    sgo = _sg_fwd(proj, w["sg_ln_g"], w["sg_ln_b"], w["sg_w"], w["sg_bcol"], name=f"{tag}_sg_fwd")
    xbc = _conv_fwd(proj, w["conv_w"], w["conv_b"], bsz=bsz, name=f"{tag}_conv_fwd")
    dt, cs, dtt, cst = _ssd_prep(dt_raw, w["dt_bias8"], w["a_log8"], name=f"{tag}_ssd_prep")
    y, st = _ssd_fwd(xbc, dt, cs, dtt, cst, w["d_skipx"], nc=nc, name=f"{tag}_ssd_fwd")
    yb = _gate_norm_fwd(y, proj, w["ssm_norm_g"], name=f"{tag}_gate_norm_fwd")
    if "rest" in w:
        w = w["rest"](w, yb)
    br_a = _mm(sgo, w["p_a"], name=f"{tag}_br_a")
    br_b = _mm(yb, w["p_b"], name=f"{tag}_br_b")
    merged = _merge_fwd(br_a, br_b, proj, name=f"{tag}_merge_fwd")
    mix = _mm(merged, w["w_mix_o"], name=f"{tag}_mix_o")
    x1, x1b, xh1, rs1 = _ln_fwd(x, mix, w["ln_g"][0], w["ln_b"][0], name=f"{tag}_ln1_fwd")
    sv.update(proj=proj, dt_raw=dt_raw, sgo=sgo, xbc=xbc, dt=dt, cs=cs, dtt=dtt, cst=cst, y=y, st=st, yb=yb,
              br_a=br_a, br_b=br_b, merged=merged, xh1=xh1, rs1=rs1, x1b=x1b)
    q = _mm(x1b, w["w_xq"], out_dtype=MXU_DTYPE, name=f"{tag}_q")
    kv = _mm(memn_b, w["w_xkv"], out_dtype=MXU_DTYPE, name=f"{tag}_kv")
    o = _attn_fwd(q, kv, bsz=bsz, name=f"{tag}_attn_fwd")
    att = _mm(o, w["w_xo"], name=f"{tag}_xo")
    x2, x2b, xh2, rs2 = _ln_fwd(x1, att, w["ln_g"][1], w["ln_b"][1], name=f"{tag}_ln2_fwd")
    sv.update(q=q, kv=kv, o=o, xh2=xh2, rs2=rs2, x2b=x2b)
    h = _mm(x2b, w["w_ffn_in"], name=f"{tag}_ffn_in")
    a = _swiglu_fwd(h, name=f"{tag}_swiglu_fwd")
    ffn = _mm(a, w["w_ffn_out"], name=f"{tag}_ffn_out")
    x3, x3b, xh3, rs3 = _ln_fwd(x2, ffn, w["ln_g"][2], w["ln_b"][2], name=f"{tag}_ln3_fwd")
    sv.update(h=h, a=a, xh3=xh3, rs3=rs3)
    return x3, x3b, sv, w


GRAD_GROUPS = (("w_ffn_out", "w_ffn_in", "w_xo", "w_xq", "w_xkv"), ("w_mix_o", "p_a", "p_b"), ("w_in",))


def _layer_bwd(dx3_addends, dx3_scales, memn_b, w, sv, on_group=None, *, bsz, tag):
    nc = sv["xh1"].shape[0] // bsz // CHUNK
    gr = {}

    def group_done(k):
        return on_group(GRAD_GROUPS[k], gr) if on_group is not None else None
    dp3, dp3b, dg3, db3 = _ln_bwd(dx3_addends, dx3_scales, sv["xh3"], sv["rs3"], w["ln_g"][2], name=f"{tag}_ln3_bwd")
    da = _mm(dp3b, w["w_ffn_out"], tb=True, name=f"{tag}_d_a")
    gr["w_ffn_out"] = _mm(sv["a"], dp3b, ta=True, name=f"{tag}_dw_ffn_out")
    dh = _swiglu_bwd(sv["h"], da, name=f"{tag}_swiglu_bwd")
    gr["w_ffn_in"] = _mm(sv["x2b"], dh, ta=True, name=f"{tag}_dw_ffn_in")
    dx2_br = _mm(dh, w["w_ffn_in"], tb=True, name=f"{tag}_dx2")
    dp2, dp2b, dg2, db2 = _ln_bwd([dp3, dx2_br], [ALPHA, 1.0], sv["xh2"], sv["rs2"], w["ln_g"][1], name=f"{tag}_ln2_bwd")
    do = _mm(dp2b, w["w_xo"], tb=True, out_dtype=MXU_DTYPE, name=f"{tag}_d_o")
    gr["w_xo"] = _mm(sv["o"], dp2b, ta=True, name=f"{tag}_dw_xo")
    dq, dk, dv = _attn_bwd(sv["q"], sv["kv"], do, bsz=bsz, name=f"{tag}_attn_bwd")
    dkv = jnp.concatenate([dk, dv], axis=1)
    gr["w_xq"] = _mm(sv["x1b"], dq, ta=True, name=f"{tag}_dw_xq")
    gr["w_xkv"] = _mm(memn_b, dkv, ta=True, name=f"{tag}_dw_xkv")
    dmemn = _mm(dkv, w["w_xkv"], tb=True, name=f"{tag}_d_memn")
    dx1_br = _mm(dq, w["w_xq"], tb=True, name=f"{tag}_dx1")
    token = group_done(0)
    ln_g1 = w["ln_g"][0] if token is None else w["ln_g"][0] + token[0, 0]
    dp1, dp1b, dg1, db1 = _ln_bwd([dp2, dx1_br], [ALPHA, 1.0], sv["xh1"], sv["rs1"], ln_g1, name=f"{tag}_ln1_bwd")
    gr["ln_g"] = jnp.concatenate([dg1, dg2, dg3], axis=0)
    gr["ln_b"] = jnp.concatenate([db1, db2, db3], axis=0)
    dmerged = _mm(dp1b, w["w_mix_o"], tb=True, name=f"{tag}_d_merged")
    gr["w_mix_o"] = _mm(sv["merged"], dp1b, ta=True, name=f"{tag}_dw_mix_o")
    dbr_a, dbr_b, dproj = _merge_bwd(dmerged, sv["br_a"], sv["br_b"], sv["proj"], name=f"{tag}_merge_bwd")
    gr["p_a"] = _mm(sv["sgo"], dbr_a, ta=True, name=f"{tag}_dw_p_a")
    gr["p_b"] = _mm(sv["yb"], dbr_b, ta=True, name=f"{tag}_dw_p_b")
    dsgo = _mm(dbr_a, w["p_a"], tb=True, name=f"{tag}_d_sgo")
    dyb = _mm(dbr_b, w["p_b"], tb=True, name=f"{tag}_d_yb")
    token = group_done(1)
    norm_g = w["ssm_norm_g"] if token is None else w["ssm_norm_g"] + token[0, 0]
    dy, dproj, gr["ssm_norm_g"] = _gate_norm_bwd(dyb, sv["y"], sv["proj"], norm_g, dproj, name=f"{tag}_gate_norm_bwd")
    dxbc, ddr, gr["a_log"], gr["d_skip"], gr["dt_bias"] = _ssd_bwd(
        dy, sv["xbc"], sv["dt"], sv["cs"], sv["dtt"], sv["cst"], sv["st"], w["d_skipx"], w["a_log8"], sv["dt_raw"],
        w["dt_bias8"], nc=nc, name=f"{tag}_ssd_bwd")
    dproj, gr["conv_w"], gr["conv_b"] = _conv_bwd(sv["proj"], dxbc, w["conv_w"], w["conv_b"], dproj, bsz=bsz, name=f"{tag}_conv_bwd")
    dproj, gr["sg_w"], dsg_bcol, gr["sg_ln_g"], gr["sg_ln_b"] = _sg_bwd(
        sv["proj"], dsgo, w["sg_ln_g"], w["sg_ln_b"], w["sg_w"], w["sg_bcol"], dproj, name=f"{tag}_sg_bwd")
    gr["sg_b"] = dsg_bcol[..., 0]
    gr["w_main"] = _mm(sv["x_in"], dproj, ta=True, name=f"{tag}_dw_main")
    gr["w_dt"] = _mm(sv["x_in"], ddr, ta=True, name=f"{tag}_dw_dt")
    token = group_done(2)
    dx_dt = _mm(ddr, w["w_dt"], tb=True, after=token, name=f"{tag}_dx_dt")
    dx_main = _mm(dproj, w["w_main"], tb=True, after=token, name=f"{tag}_dx_main")
    return [dp1, dx_main, dx_dt], [ALPHA, 1.0, 1.0], gr, dmemn


def _local_step(x, mem, tgt, mem_ln_g, mem_ln_b, layers, on_layer_grads=None):
    bsz, s, d = x.shape
    xf = x.reshape(bsz * s, d)
    memf = mem.reshape(-1, d)
    _, memn_b, mxh, mrs = _ln_fwd(memf, None, mem_ln_g, mem_ln_b, name="mem_ln_fwd")
    cur, curb, saved, weights = xf, xf, [], []
    for li, get_weights in enumerate(layers):
        cur, curb, sv, w = _layer_fwd(cur, curb, memn_b, get_weights(cur), bsz=bsz, tag=f"l{li}")
        saved.append(sv)
        weights.append(w)
    dy, lsum = _loss_head(cur, tgt.reshape(bsz * s, d), name="loss_head")
    addends, scales = [dy], [1.0]
    grads, dmem = [None] * len(layers), []
    for li in reversed(range(len(layers))):
        on_group = None if on_layer_grads is None else functools.partial(on_layer_grads, li)
        addends, scales, grads[li], dm = _layer_bwd(addends, scales, memn_b, weights[li], saved[li], on_group, bsz=bsz, tag=f"l{li}")
        dmem.append(dm)
    grad_x = _add_scaled(addends, scales, name="grad_x").reshape(bsz, s, d)
    _, _, dmg, dmb = _ln_bwd(dmem, [1.0] * len(dmem), mxh, mrs, mem_ln_g, name="mem_ln_bwd")
    return lsum, grad_x, grads, dmg[0], dmb[0]


_ANY = pl.BlockSpec(memory_space=pl.ANY)
_MESH = pl.DeviceIdType.MESH


def _all_gather8(x, *, name):
    def body(x_ref, out_ref, send_sems, recv_sems):
        mx, my, mc = lax.axis_index("x"), lax.axis_index("y"), lax.axis_index("c")
        me, sibling = (mx, my, mc), (mx, my, 1 - mc)
        chips = [(1 - mx, my), (mx, 1 - my), (1 - mx, 1 - my)]

        def blk(px, py, pc):
            return out_ref.at[4 * px + 2 * py + pc]

        def copy(k, block, to, src=None):
            return pltpu.make_async_remote_copy(
                src_ref=blk(*block) if src is None else src, dst_ref=blk(*block), send_sem=send_sems.at[k],
                recv_sem=recv_sems.at[k], device_id=to, device_id_type=_MESH)

        first = [copy(0, me, sibling, src=x_ref)]
        first += [copy(1 + j, me, (*chip, mc), src=x_ref) for j, chip in enumerate(chips)]
        for cp in first:
            cp.start()
        passed = [copy(4 + j, (*chip, mc), sibling) for j, chip in enumerate(chips)]
        for j, chip in enumerate(chips):
            copy(1 + j, (*chip, mc), me).wait_recv()
            passed[j].start()
        copy(0, sibling, me).wait_recv()
        for j, chip in enumerate(chips):
            copy(4 + j, (*chip, 1 - mc), me).wait_recv()
        for cp in first + passed:
            cp.wait_send()

    return pl.pallas_call(
        body, out_shape=jax.ShapeDtypeStruct((N_DEV,) + x.shape, x.dtype), in_specs=[_ANY], out_specs=_ANY,
        scratch_shapes=[pltpu.SemaphoreType.DMA((7,)), pltpu.SemaphoreType.DMA((7,))], name=name)(x)


def _row_tile(rows, row_bytes, mult=SUBLANE):
    best = None
    for tr in range(mult, rows + 1, mult):
        if rows % tr == 0 and (best is None or tr * row_bytes <= BLOCK_BYTES):
            best = tr
    return rows if best is None else best


def _gather_shape(r, c, kind):
    return {"row": (2, N_CHIPS * r, c), "col": (2, r, N_CHIPS * c), "chip": (2, N_CHIPS, r, c)}[kind]


def _cast_place(shard, kind, dtype, chip_idx, *, name):
    _, r, c = shard.shape
    tr = _row_tile(r, c * 4, 16)
    nt = r // tr

    def body(_, s_ref, o_ref):
        o_ref[...] = s_ref[...].astype(dtype)

    if kind == "row":
        out_spec = pl.BlockSpec((None, tr, c), lambda l, i, j_ref: (l, j_ref[0] * nt + i, 0))
    elif kind == "col":
        out_spec = pl.BlockSpec((None, tr, c), lambda l, i, j_ref: (l, i, j_ref[0]))
    else:
        out_spec = pl.BlockSpec((None, None, tr, c), lambda l, i, j_ref: (l, j_ref[0], i, 0))
    grid_spec = pltpu.PrefetchScalarGridSpec(
        num_scalar_prefetch=1, grid=(2, nt), in_specs=[pl.BlockSpec((None, tr, c), lambda l, i, j_ref: (l, i, 0))],
        out_specs=out_spec)
    return pl.pallas_call(body, grid_spec=grid_spec, out_shape=jax.ShapeDtypeStruct(_gather_shape(r, c, kind), dtype),
                          compiler_params=_params("parallel", "parallel"), name=name)(chip_idx, shard)


def _gather_params(bufs, shard_shapes, kinds, *, name):
    n = len(bufs)

    def body(*refs):
        outs = refs[n:2 * n]
        send_sems, recv_sems = refs[2 * n:]
        mx, my, mc = lax.axis_index("x"), lax.axis_index("y"), lax.axis_index("c")
        me, sibling = (mx, my, mc), (mx, my, 1 - mc)
        chips = [(1 - mx, my), (mx, 1 - my), (1 - mx, 1 - my)]

        def blk(i, px, py, pc):
            r, c = shard_shapes[i]
            j = 2 * px + py
            if kinds[i] == "row":
                return outs[i].at[pc, pl.ds(pl.multiple_of(j * r, r), r)]
            if kinds[i] == "col":
                return outs[i].at[pc, :, pl.ds(pl.multiple_of(j * c, c), c)]
            return outs[i].at[pc, j]

        def copy(i, k, block, to):
            return pltpu.make_async_remote_copy(
                src_ref=blk(i, *block), dst_ref=blk(i, *block), send_sem=send_sems.at[6 * i + k],
                recv_sem=recv_sems.at[6 * i + k], device_id=to, device_id_type=_MESH)

        sent = []
        for i in range(n):
            for j, chip in enumerate(chips):
                cp = copy(i, j, me, (*chip, mc))
                cp.start()
                sent.append(cp)
        for j, chip in enumerate(chips):
            for i in range(n):
                copy(i, j, (*chip, mc), me).wait_recv()
                fwd = copy(i, 3 + j, (*chip, mc), sibling)
                fwd.start()
                sent.append(fwd)
        for i in range(n):
            for j, chip in enumerate(chips):
                copy(i, 3 + j, (*chip, 1 - mc), me).wait_recv()
        for cp in sent:
            cp.wait_send()

    return pl.pallas_call(
        body, out_shape=[jax.ShapeDtypeStruct(b.shape, b.dtype) for b in bufs], in_specs=[_ANY] * n, out_specs=[_ANY] * n,
        input_output_aliases={i: i for i in range(n)},
        scratch_shapes=[pltpu.SemaphoreType.DMA((6 * n,)), pltpu.SemaphoreType.DMA((6 * n,))], name=name)(*bufs)


def _half(r, h):
    return pl.ds(pl.multiple_of(h * (r // 2), r // 2), r // 2)


def _grads_to_sibling(gs, views, *, name):
    n = len(gs)

    def recv_shape(g, view):
        if view == "chip":
            return jax.ShapeDtypeStruct((g.shape[0], g.shape[1] // 2, g.shape[2]), g.dtype)
        return jax.ShapeDtypeStruct((g.shape[0] // 2, g.shape[1]), g.dtype)

    def body(*refs):
        ins, outs = refs[:n], refs[n:2 * n]
        send_sems, recv_sems = refs[2 * n:]
        mx, my, mc = lax.axis_index("x"), lax.axis_index("y"), lax.axis_index("c")
        copies = []
        for i in range(n):
            if views[i] == "chip":
                src = ins[i].at[:, _half(gs[i].shape[1], 1 - mc)]
            else:
                src = ins[i].at[_half(gs[i].shape[0], 1 - mc)]
            cp = pltpu.make_async_remote_copy(src_ref=src, dst_ref=outs[i], send_sem=send_sems.at[i], recv_sem=recv_sems.at[i],
                                              device_id=(mx, my, 1 - mc), device_id_type=_MESH)
            cp.start()
            copies.append(cp)
        for cp in copies:
            cp.wait()

    return pl.pallas_call(
        body, out_shape=[recv_shape(g, v) for g, v in zip(gs, views)], in_specs=[_ANY] * n, out_specs=[_ANY] * n,
        scratch_shapes=[pltpu.SemaphoreType.DMA((n,)), pltpu.SemaphoreType.DMA((n,))], name=name)(*gs)


_HBM = pl.BlockSpec(memory_space=pltpu.HBM)
_SEM = pl.BlockSpec(memory_space=pltpu.SEMAPHORE)
_EFFECT = pltpu.SideEffectType.DATAFLOW_SIDE_EFFECTING


def _cast_place_layer(shard, l, kind, chip_idx, after, *, name):
    _, r, c = shard.shape
    tr = _row_tile(r, c * 4, 16)
    nt = r // tr

    def body(_, s_ref, *rest):
        rest[-1][...] = s_ref[...].astype(MXU_DTYPE)

    if kind == "row":
        out_spec = pl.BlockSpec((tr, c), lambda i, j_ref: (j_ref[0] * nt + i, 0))
    elif kind == "col":
        out_spec = pl.BlockSpec((tr, c), lambda i, j_ref: (i, j_ref[0]))
    else:
        out_spec = pl.BlockSpec((None, tr, c), lambda i, j_ref: (j_ref[0], i, 0))
    extra = [] if after is None else [after]
    grid_spec = pltpu.PrefetchScalarGridSpec(
        num_scalar_prefetch=1, grid=(nt,), in_specs=[pl.BlockSpec((None, tr, c), lambda i, j_ref: (l, i, 0))] + [_ANY] * len(extra),
        out_specs=out_spec)
    return pl.pallas_call(body, grid_spec=grid_spec, out_shape=jax.ShapeDtypeStruct(_gather_shape(r, c, kind)[1:], MXU_DTYPE),
                          compiler_params=_params("parallel"), name=name)(chip_idx, shard, *extra)


def _half_block(ref, kind, r, c, j, h):
    rows = _half(r, h)
    if kind == "row":
        return ref.at[pl.ds(pl.multiple_of(j * r + h * (r // 2), r // 2), r // 2)]
    if kind == "col":
        return ref.at[rows, pl.ds(pl.multiple_of(j * c, c), c)]
    return ref.at[j, rows]


def _gather_ici_copies(buf_refs, shapes, kinds, send_sems, recv_sems):
    mx, my, mc = lax.axis_index("x"), lax.axis_index("y"), lax.axis_index("c")
    chips = [(1 - mx, my), (mx, 1 - my), (1 - mx, 1 - my)]
    copies = []
    for i, (r, c) in enumerate(shapes):
        mine = _half_block(buf_refs[i], kinds[i], r, c, 2 * mx + my, mc)
        for k, (px, py) in enumerate(chips):
            copies.append(pltpu.make_async_remote_copy(
                src_ref=mine, dst_ref=mine, send_sem=send_sems.at[3 * i + k], recv_sem=recv_sems.at[3 * i + k],
                device_id=(px, py, mc), device_id_type=_MESH))
    return copies


def _gather_start(bufs, shapes, kinds, *, name):
    n = len(bufs)

    def body(*refs):
        send_sems, recv_sems, token = refs[n], refs[n + 1], refs[-1]
        for cp in _gather_ici_copies(refs[:n], shapes, kinds, send_sems, recv_sems):
            cp.start()
        token[...] = jnp.zeros_like(token)

    outs = pl.pallas_call(
        body, name=name,
        out_shape=(pltpu.SemaphoreType.DMA((3 * n,)), pltpu.SemaphoreType.DMA((3 * n,)),
                   *[pltpu.HBM(b.shape, b.dtype) for b in bufs], jax.ShapeDtypeStruct((SUBLANE, LANE), F32)),
        in_specs=[_HBM] * n, out_specs=(_SEM, _SEM, *[_HBM] * n, pl.BlockSpec(memory_space=pltpu.VMEM)),
        input_output_aliases={i: 2 + i for i in range(n)},
        compiler_params=pltpu.CompilerParams(has_side_effects=_EFFECT),
    )(*[pltpu.with_memory_space_constraint(b, pltpu.HBM) for b in bufs])
    return outs[0], outs[1], list(outs[2:2 + n]), outs[-1]


def _gather_wait(send_sems, recv_sems, bufs, shapes, kinds, after, *, name):
    n = len(bufs)

    def body(*refs):
        for cp in _gather_ici_copies(refs[:n], shapes, kinds, refs[n], refs[n + 1]):
            cp.wait_send()
            cp.wait_recv()

    outs = pl.pallas_call(
        body, name=name, out_shape=tuple(pltpu.HBM(b.shape, b.dtype) for b in bufs),
        in_specs=[_HBM] * n + [_SEM, _SEM, _ANY], out_specs=tuple([_HBM] * n), input_output_aliases={i: i for i in range(n)},
        compiler_params=pltpu.CompilerParams(has_side_effects=_EFFECT),
    )(*bufs, send_sems, recv_sems, after)
    return list(outs)


def _gather_forward(bufs, shapes, kinds, *, name):
    n = len(bufs)

    def body(*refs):
        outs = refs[n:2 * n]
        send_sems, recv_sems = refs[2 * n:]
        mx, my, mc = lax.axis_index("x"), lax.axis_index("y"), lax.axis_index("c")
        chips = [(1 - mx, my), (mx, 1 - my), (1 - mx, 1 - my)]
        copies = []
        for i, (r, c) in enumerate(shapes):
            for k, (px, py) in enumerate(chips):
                got = _half_block(outs[i], kinds[i], r, c, 2 * px + py, mc)
                cp = pltpu.make_async_remote_copy(src_ref=got, dst_ref=got, send_sem=send_sems.at[3 * i + k],
                                                  recv_sem=recv_sems.at[3 * i + k], device_id=(mx, my, 1 - mc), device_id_type=_MESH)
                cp.start()
                copies.append(cp)
        for cp in copies:
            cp.wait()

    return pl.pallas_call(
        body, out_shape=[jax.ShapeDtypeStruct(b.shape, b.dtype) for b in bufs], in_specs=[_ANY] * n, out_specs=[_ANY] * n,
        input_output_aliases={i: i for i in range(n)},
        scratch_shapes=[pltpu.SemaphoreType.DMA((3 * n,)), pltpu.SemaphoreType.DMA((3 * n,))], name=name)(*bufs)


def _chip_exchange_copies(pair_refs, land_refs, pairs, views, send_sems, recv_sems):
    mx, my, mc = lax.axis_index("x"), lax.axis_index("y"), lax.axis_index("c")
    me = 2 * mx + my
    chips = [(1 - mx, my), (mx, 1 - my), (1 - mx, 1 - my)]
    copies = []
    for i in range(len(pairs)):
        for k, (px, py) in enumerate(chips):
            j = 2 * px + py
            if views[i] == "chip":
                src = pair_refs[i].at[j]
            else:
                c = pairs[i].shape[1] // N_CHIPS
                src = pair_refs[i].at[:, pl.ds(pl.multiple_of(j * c, c), c)]
            copies.append(pltpu.make_async_remote_copy(
                src_ref=src, dst_ref=land_refs[i].at[me], send_sem=send_sems.at[3 * i + k], recv_sem=recv_sems.at[3 * i + k],
                device_id=(px, py, mc), device_id_type=_MESH))
    return copies


def _quad_shape(p, view):
    return p.shape if view == "chip" else (N_CHIPS, p.shape[0], p.shape[1] // N_CHIPS)


def _grads_to_chips_start(pairs, views, *, name):
    n = len(pairs)
    lands = [pltpu.with_memory_space_constraint(lax.empty(_quad_shape(p, v), p.dtype), pltpu.HBM) for p, v in zip(pairs, views)]

    def body(*refs):
        pair_refs, land_refs = refs[:n], refs[n:2 * n]
        send_sems, recv_sems = refs[2 * n], refs[2 * n + 1]
        token = refs[-1]
        for cp in _chip_exchange_copies(pair_refs, land_refs, pairs, views, send_sems, recv_sems):
            cp.start()
        token[...] = jnp.zeros_like(token)

    outs = pl.pallas_call(
        body, name=name,
        out_shape=(pltpu.SemaphoreType.DMA((3 * n,)), pltpu.SemaphoreType.DMA((3 * n,)),
                   *[pltpu.HBM(p.shape, p.dtype) for p in pairs], *[pltpu.HBM(l.shape, l.dtype) for l in lands],
                   jax.ShapeDtypeStruct((SUBLANE, LANE), F32)),
        in_specs=[_HBM] * (2 * n), out_specs=(_SEM, _SEM, *[_HBM] * (2 * n), pl.BlockSpec(memory_space=pltpu.VMEM)),
        input_output_aliases={i: 2 + i for i in range(2 * n)},
        compiler_params=pltpu.CompilerParams(has_side_effects=_EFFECT),
    )(*[pltpu.with_memory_space_constraint(p, pltpu.HBM) for p in pairs], *lands)
    return outs[0], outs[1], list(outs[2:2 + n]), list(outs[2 + n:2 + 2 * n]), outs[-1]


def _grads_to_chips_wait(send_sems, recv_sems, pairs, lands, views, after, *, name):
    n = len(pairs)

    def body(*refs):
        pair_refs, land_refs = refs[:n], refs[n:2 * n]
        s_sems, r_sems = refs[2 * n], refs[2 * n + 1]
        for cp in _chip_exchange_copies(pair_refs, land_refs, pairs, views, s_sems, r_sems):
            cp.wait_send()
            cp.wait_recv()

    outs = pl.pallas_call(
        body, name=name, out_shape=tuple(pltpu.HBM(x.shape, x.dtype) for x in list(pairs) + list(lands)),
        in_specs=[_HBM] * (2 * n) + [_SEM, _SEM, _ANY], out_specs=tuple([_HBM] * (2 * n)),
        input_output_aliases={i: i for i in range(2 * n)},
        compiler_params=pltpu.CompilerParams(has_side_effects=_EFFECT),
    )(*pairs, *lands, send_sems, recv_sems, after)
    return list(outs[n:])


def _grads_share(tots, *, name):
    n = len(tots)

    def body(*refs):
        ins, outs = refs[:n], refs[n:2 * n]
        send_sems, recv_sems = refs[2 * n:]
        mx, my, mc = lax.axis_index("x"), lax.axis_index("y"), lax.axis_index("c")
        copies = []
        for i in range(n):
            cp = pltpu.make_async_remote_copy(src_ref=ins[i], dst_ref=outs[i], send_sem=send_sems.at[i], recv_sem=recv_sems.at[i],
                                              device_id=(mx, my, 1 - mc), device_id_type=_MESH)
            cp.start()
            copies.append(cp)
        for cp in copies:
            cp.wait()

    return pl.pallas_call(
        body, out_shape=[jax.ShapeDtypeStruct(t.shape, t.dtype) for t in tots], in_specs=[_ANY] * n, out_specs=[_ANY] * n,
        scratch_shapes=[pltpu.SemaphoreType.DMA((n,)), pltpu.SemaphoreType.DMA((n,))], name=name)(*tots)


def _pair_sum(g, recv, view, c_idx, *, name):
    def body(c_ref, a_ref, b_ref, o_ref):
        o_ref[...] = (a_ref[...] + b_ref[...]).astype(WIRE_DTYPE)

    if view == "chip":
        nch, r, c = g.shape
        tr = _row_tile(r // 2, c * 4, 16)
        gv = g.reshape(nch, 2, r // 2, c)
        grid = (nch, (r // 2) // tr)
        in_specs = [pl.BlockSpec((None, None, tr, c), lambda j, i, c_ref: (j, c_ref[0], i, 0)),
                    pl.BlockSpec((None, tr, c), lambda j, i, c_ref: (j, i, 0))]
        out_spec = pl.BlockSpec((None, tr, c), lambda j, i, c_ref: (j, i, 0))
        sem = ("parallel", "parallel")
    else:
        r, c4 = g.shape
        tr = _row_tile(r // 2, c4 * 4, 16)
        gv = g.reshape(2, r // 2, c4)
        grid = ((r // 2) // tr,)
        in_specs = [pl.BlockSpec((None, tr, c4), lambda i, c_ref: (c_ref[0], i, 0)), pl.BlockSpec((tr, c4), lambda i, c_ref: (i, 0))]
        out_spec = pl.BlockSpec((tr, c4), lambda i, c_ref: (i, 0))
        sem = ("parallel",)
    grid_spec = pltpu.PrefetchScalarGridSpec(num_scalar_prefetch=1, grid=grid, in_specs=in_specs, out_specs=out_spec)
    return pl.pallas_call(body, grid_spec=grid_spec, out_shape=jax.ShapeDtypeStruct(recv.shape, WIRE_DTYPE),
                          compiler_params=_params(*sem), name=name)(c_idx, gv, recv)


def _quad_sum(gs, recvs, quads, view, chip_idx, c_idx, *, name):
    nl = len(quads)
    nch, rh, c = quads[0].shape
    tr = _row_tile(rh, c * 4, 16)

    def body(_, __, *refs):
        o_ref = refs[-1]
        per = nch + 1
        for l in range(nl):
            grp = refs[l * per:(l + 1) * per]
            acc = grp[0][...] + grp[1][...]
            for r in grp[2:]:
                acc = acc + r[...].astype(F32)
            o_ref[l] = acc

    if view == "chip":
        own = [pl.BlockSpec((None, None, tr, c), lambda i, j, h: (j[0], h[0], i, 0)),
               pl.BlockSpec((None, tr, c), lambda i, j, h: (j[0], i, 0))]
        gviews = [g.reshape(nch, 2, rh, c) for g in gs]
    else:
        own = [pl.BlockSpec((None, tr, c), lambda i, j, h: (h[0], i, j[0])), pl.BlockSpec((tr, c), lambda i, j, h: (i, j[0]))]
        gviews = [g.reshape(2, rh, nch * c) for g in gs]
    assert nch & (nch - 1) == 0
    got = [pl.BlockSpec((None, tr, c), functools.partial(lambda i, j, h, k: ((j[0] + k) & (nch - 1), i, 0), k=k))
           for k in range(1, nch)]
    ins = []
    for l in range(nl):
        ins += [gviews[l], recvs[l]] + [quads[l]] * (nch - 1)
    grid_spec = pltpu.PrefetchScalarGridSpec(
        num_scalar_prefetch=2, grid=(rh // tr,), in_specs=(own + got) * nl,
        out_specs=pl.BlockSpec((nl, tr, c), lambda i, j, h: (0, i, 0)))
    return pl.pallas_call(body, grid_spec=grid_spec, out_shape=jax.ShapeDtypeStruct((nl, rh, c), F32),
                          compiler_params=_params("parallel"), name=name)(chip_idx, c_idx, *ins)


def _sum_devices(g8, own, dev_idx, *, name):
    k, rows, cols = g8.shape

    def body(d_ref, a_ref, x_ref, o_ref):
        acc = None
        for i in range(k):
            term = jnp.where(d_ref[0] == i, x_ref[...], a_ref[i])
            acc = term if acc is None else acc + term
        o_ref[...] = acc

    grid_spec = pltpu.PrefetchScalarGridSpec(
        num_scalar_prefetch=1, grid=(1,),
        in_specs=[pl.BlockSpec((k, rows, cols), lambda i, d_ref: (0, 0, 0)), pl.BlockSpec((rows, cols), lambda i, d_ref: (0, 0))],
        out_specs=pl.BlockSpec((rows, cols), lambda i, d_ref: (0, 0)))
    return pl.pallas_call(body, grid_spec=grid_spec, out_shape=jax.ShapeDtypeStruct((rows, cols), g8.dtype),
                          compiler_params=_params("arbitrary"), name=name)(dev_idx, g8, own)


def _adamw(w, g, m, v, *, name):
    rows, cols = w.shape
    tr = rows
    for cand in (256, 128, 64, 32, 16, 8):
        if rows % cand == 0 and cand * cols <= 512 * 1024:
            tr = cand
            break
    c1 = 1.0 - ADAM_B1 ** ADAM_STEP
    c2 = 1.0 - ADAM_B2 ** ADAM_STEP

    def body(w_ref, g_ref, m_ref, v_ref, d_ref, nm_ref, nv_ref):
        gv = g_ref[...]
        nm = ADAM_B1 * m_ref[...] + (1.0 - ADAM_B1) * gv
        nv = ADAM_B2 * v_ref[...] + (1.0 - ADAM_B2) * (gv * gv)
        d_ref[...] = -ADAM_LR * ((nm / c1) / (jnp.sqrt(nv / c2) + ADAM_EPS) + ADAM_WD * w_ref[...])
        nm_ref[...] = nm
        nv_ref[...] = nv

    spec = pl.BlockSpec((tr, cols), lambda i: (i, 0))
    shp = jax.ShapeDtypeStruct((rows, cols), F32)
    return pl.pallas_call(body, grid=(rows // tr,), in_specs=[spec] * 4, out_specs=[spec] * 3, out_shape=[shp] * 3,
                          compiler_params=_params("parallel"), name=name)(w, g, m, v)


def _adamw_halves(w, m, v, mine, other, c_idx, *, name):
    nl, r, c = w.shape
    rh = r // 2
    tr = _row_tile(rh, c * 4)
    c1 = 1.0 - ADAM_B1 ** ADAM_STEP
    c2 = 1.0 - ADAM_B2 ** ADAM_STEP

    def body(c_ref, w_ref, m_ref, v_ref, a_ref, b_ref, g_ref, d_ref, nm_ref, nv_ref):
        gv = jnp.where(pl.program_id(1) == c_ref[0], a_ref[...], b_ref[...])
        nm = ADAM_B1 * m_ref[...] + (1.0 - ADAM_B1) * gv
        nv = ADAM_B2 * v_ref[...] + (1.0 - ADAM_B2) * (gv * gv)
        g_ref[...] = gv
        d_ref[...] = -ADAM_LR * ((nm / c1) / (jnp.sqrt(nv / c2) + ADAM_EPS) + ADAM_WD * w_ref[...])
        nm_ref[...] = nm
        nv_ref[...] = nv

    full = pl.BlockSpec((None, None, tr, c), lambda l, h, i, c_ref: (l, h, i, 0))
    half = pl.BlockSpec((None, tr, c), lambda l, h, i, c_ref: (l, i, 0))
    grid_spec = pltpu.PrefetchScalarGridSpec(num_scalar_prefetch=1, grid=(nl, 2, rh // tr),
                                             in_specs=[full] * 3 + [half] * 2, out_specs=[full] * 4)
    shp = jax.ShapeDtypeStruct((nl, 2, rh, c), F32)
    view = (nl, 2, rh, c)
    outs = pl.pallas_call(body, grid_spec=grid_spec, out_shape=[shp] * 4, compiler_params=_params("parallel", "parallel", "parallel"),
                          name=name)(c_idx, w.reshape(view), m.reshape(view), v.reshape(view), mine, other)
    return [o.reshape(nl, r, c) for o in outs]


WEIGHTS = ["mem_ln_g", "mem_ln_b", "w_in", "sg_ln_g", "sg_ln_b", "sg_w", "sg_b", "conv_w", "conv_b", "dt_bias", "a_log",
           "d_skip", "ssm_norm_g", "p_a", "p_b", "w_mix_o", "w_xq", "w_xkv", "w_xo", "w_ffn_in", "w_ffn_out", "ln_g", "ln_b"]
ARG_NAMES = ["x", "mem"] + WEIGHTS + ["loss_target"] + ["m_" + n for n in WEIGHTS] + ["v_" + n for n in WEIGHTS]
BIG = {"w_in": (1, (1024, 9248)), "p_a": (0, (1024, 1024)), "p_b": (0, (2048, 1024)), "w_mix_o": (0, (1024, 1024)),
       "w_xq": (0, (1024, 1024)), "w_xkv": (1, (1024, 2048)), "w_xo": (0, (1024, 1024)), "w_ffn_in": (1, (1024, 5632)),
       "w_ffn_out": (0, (2816, 1024))}
SMALL_SHARDED = {"conv_w": (4, 3072), "ln_g": (3, 1024), "ln_b": (3, 1024)}
SMALL = [n for n in WEIGHTS if n not in BIG]
XBC_IN0, DT_COL0, DT_COL1 = 4096, 7168, 7200
GATHER_KIND = {"w_in": "chip", "p_a": "row", "p_b": "row", "w_mix_o": "row", "w_xq": "row", "w_xkv": "col", "w_xo": "row",
               "w_ffn_in": "col", "w_ffn_out": "row", "conv_w": "chip", "ln_g": "chip", "ln_b": "chip"}
GRAD_VIEW = {n: ("col" if k == "col" else "chip") for n, k in GATHER_KIND.items() if n in BIG}


def _shard_shape(name):
    axis, (r, c) = BIG[name]
    return (r // N_CHIPS, c) if axis == 0 else (r, c // N_CHIPS)


def _pad_rows(flat, cols, row_mult):
    n = flat.shape[0]
    rows = -(-n // cols)
    rows = -(-rows // row_mult) * row_mult
    return jnp.pad(flat, (0, rows * cols - n)).reshape(rows, cols)


def _gather_small_params(a, chip):
    names = list(SMALL_SHARDED)
    kinds = [GATHER_KIND[n] for n in names]
    bufs = [_cast_place(a[n], GATHER_KIND[n], F32, chip.reshape(1), name=f"place_{n}") for n in names]
    outs = _gather_params(bufs, [a[n].shape[1:] for n in names], kinds, name="gather_small_params")
    full = {}
    for n, o in zip(names, outs):
        _, _, r, c = o.shape
        full[n] = jnp.transpose(o, (0, 2, 1, 3)).reshape(DEPTH, r, N_CHIPS * c)
    return full


GATHER_GROUPS = (("w_in",), tuple(n for n in BIG if n != "w_in"))


def _gather_group_start(a, l, names, chip, after, *, tag):
    bufs = [_cast_place_layer(a[n], l, GATHER_KIND[n], chip.reshape(1), after, name=f"place_{n}_l{l}") for n in names]
    return _gather_start(bufs, [a[n].shape[1:] for n in names], [GATHER_KIND[n] for n in names], name=f"gather_start_{tag}")


def _gather_group_finish(a, names, flight, after, *, tag):
    send_sems, recv_sems, bufs, token = flight
    shapes, kinds = [a[n].shape[1:] for n in names], [GATHER_KIND[n] for n in names]
    bufs = _gather_wait(send_sems, recv_sems, bufs, shapes, kinds, token if after is None else after, name=f"gather_wait_{tag}")
    full = dict(zip(names, _gather_forward(bufs, shapes, kinds, name=f"gather_forward_{tag}")))
    if "w_in" in full:
        _, r, c = full["w_in"].shape
        w_in = jnp.transpose(full.pop("w_in"), (1, 0, 2)).reshape(r, N_CHIPS * c)
        full["w_main"] = jnp.concatenate([w_in[:, :XBC_IN0], w_in[:, DT_COL1:], w_in[:, XBC_IN0:DT_COL0]], axis=1)
        full["w_dt"] = jnp.pad(w_in[:, DT_COL0:DT_COL1], ((0, 0), (0, HEAD_PAD - SSM_HEADS)))
    return full


def _layer_weights(a, big, small, l):
    w = dict(big)
    for n in SMALL_SHARDED:
        w[n] = small[n][l]
    for n in ["sg_ln_g", "sg_ln_b", "sg_w", "conv_b", "ssm_norm_g"]:
        w[n] = a[n][l]
    w["sg_bcol"] = a["sg_b"][l][..., None]
    for n in ["dt_bias", "a_log"]:
        w[n + "8"] = _pad_heads(a[n][l])
    w["d_skipx"] = _expand_heads(a["d_skip"][l])
    return w


def _pair_sums(grads, names, c_idx, *, tag):
    gs = []
    views = [GRAD_VIEW[n] for n in names]
    for n in names:
        axis, _ = BIG[n]
        r, c = _shard_shape(n)
        if n == "w_in":
            gm, gd = grads["w_main"], grads["w_dt"]
            gfull = jnp.concatenate([gm[:, :XBC_IN0], gm[:, XBC_COL0:], gd[:, :SSM_HEADS], gm[:, GAB_COL0:XBC_COL0]], axis=1)
            gs.append(jnp.transpose(gfull.reshape(r, N_CHIPS, c), (1, 0, 2)))
        elif axis == 0:
            gs.append(grads[n].reshape(N_CHIPS, r, c))
        else:
            gs.append(grads[n])
    recv = _grads_to_sibling(gs, views, name=f"grads_to_sibling_{tag}")
    cpre = c_idx.reshape(1)
    pairs = [_pair_sum(g, rv, v, cpre, name=f"grads_pair_sum_{n}_{tag}") for g, rv, v, n in zip(gs, recv, views, names)]
    return gs, recv, pairs


def _finish_big_grads(parts, c_idx, chip):
    tots = [_quad_sum([parts[l][n][0] for l in range(DEPTH)], [parts[l][n][1] for l in range(DEPTH)],
                      [parts[l][n][2] for l in range(DEPTH)], GRAD_VIEW[n], chip.reshape(1), c_idx.reshape(1),
                      name=f"grads_chip_sum_{n}") for n in BIG]
    others = _grads_share(tots, name="grads_share")
    return {n: (t, o) for n, t, o in zip(BIG, tots, others)}


def _reduce_small_grads(small, chip, c_idx):
    names = list(small)
    flat = jnp.concatenate([small[n].reshape(-1) for n in names])
    packed = _pad_rows(flat, LANE, SUBLANE)
    g8 = _all_gather8(packed, name="gather_small_grads")
    tot = _sum_devices(g8, packed, (2 * chip + c_idx).reshape(1), name="small_grads_sum").reshape(-1)
    out, off = {}, 0
    for n in names:
        sz = small[n].size
        full = tot[off:off + sz].reshape(small[n].shape)
        off += sz
        if n in SMALL_SHARDED:
            cs = SMALL_SHARDED[n][1] // N_CHIPS
            full = lax.dynamic_slice_in_dim(full, chip * cs, cs, axis=-1)
        out[n] = full
    return out


def kernel(x, mem, mem_ln_g, mem_ln_b, w_in, sg_ln_g, sg_ln_b, sg_w, sg_b, conv_w, conv_b, dt_bias, a_log, d_skip, ssm_norm_g, p_a, p_b, w_mix_o, w_xq, w_xkv, w_xo, w_ffn_in, w_ffn_out, ln_g, ln_b, loss_target, m_mem_ln_g, m_mem_ln_b, m_w_in, m_sg_ln_g, m_sg_ln_b, m_sg_w, m_sg_b, m_conv_w, m_conv_b, m_dt_bias, m_a_log, m_d_skip, m_ssm_norm_g, m_p_a, m_p_b, m_w_mix_o, m_w_xq, m_w_xkv, m_w_xo, m_w_ffn_in, m_w_ffn_out, m_ln_g, m_ln_b, v_mem_ln_g, v_mem_ln_b, v_w_in, v_sg_ln_g, v_sg_ln_b, v_sg_w, v_sg_b, v_conv_w, v_conv_b, v_dt_bias, v_a_log, v_d_skip, v_ssm_norm_g, v_p_a, v_p_b, v_w_mix_o, v_w_xq, v_w_xkv, v_w_xo, v_w_ffn_in, v_w_ffn_out, v_ln_g, v_ln_b):
    a = dict(zip(ARG_NAMES, (x, mem, mem_ln_g, mem_ln_b, w_in, sg_ln_g, sg_ln_b, sg_w, sg_b, conv_w, conv_b, dt_bias, a_log, d_skip, ssm_norm_g, p_a, p_b, w_mix_o, w_xq, w_xkv, w_xo, w_ffn_in, w_ffn_out, ln_g, ln_b, loss_target, m_mem_ln_g, m_mem_ln_b, m_w_in, m_sg_ln_g, m_sg_ln_b, m_sg_w, m_sg_b, m_conv_w, m_conv_b, m_dt_bias, m_a_log, m_d_skip, m_ssm_norm_g, m_p_a, m_p_b, m_w_mix_o, m_w_xq, m_w_xkv, m_w_xo, m_w_ffn_in, m_w_ffn_out, m_ln_g, m_ln_b, v_mem_ln_g, v_mem_ln_b, v_w_in, v_sg_ln_g, v_sg_ln_b, v_sg_w, v_sg_b, v_conv_w, v_conv_b, v_dt_bias, v_a_log, v_d_skip, v_ssm_norm_g, v_p_a, v_p_b, v_w_mix_o, v_w_xq, v_w_xkv, v_w_xo, v_w_ffn_in, v_w_ffn_out, v_ln_g, v_ln_b)))
    c_idx = lax.axis_index("c").astype(jnp.int32)
    chip = (2 * lax.axis_index("x") + lax.axis_index("y")).astype(jnp.int32)

    small = _gather_small_params(a, chip)
    ga, gb = GATHER_GROUPS
    flights = {(0, 0): _gather_group_start(a, 0, ga, chip, None, tag="l0_a")}
    flights[0, 1] = _gather_group_start(a, 0, gb, chip, flights[0, 0][3], tag="l0_b")

    def layer_weights(after, l):
        first = _gather_group_finish(a, ga, flights[l, 0], after if l else None, tag=f"l{l}_a")

        def rest(w, after_b):
            more = _gather_group_finish(a, gb, flights[l, 1], after_b, tag=f"l{l}_b")
            if l + 1 < DEPTH:
                flights[l + 1, 0] = _gather_group_start(a, l + 1, ga, chip, more["p_a"], tag=f"l{l + 1}_a")
                flights[l + 1, 1] = _gather_group_start(a, l + 1, gb, chip, flights[l + 1, 0][3], tag=f"l{l + 1}_b")
                more["p_a"] = more["p_a"] + flights[l + 1, 1][3][0, 0].astype(MXU_DTYPE)
            return {k: v for k, v in {**w, **more}.items() if k != "rest"}

        return dict(_layer_weights(a, first, small, l), rest=rest)

    layers = [functools.partial(layer_weights, l=l) for l in range(DEPTH)]
    parts, flying = [{} for _ in range(DEPTH)], []

    def start_exchange(l, names, grads_l):
        gs, recv, pairs = _pair_sums(grads_l, names, c_idx, tag=f"l{l}_{names[0]}")
        views = [GRAD_VIEW[n] for n in names]
        send_sems, recv_sems, pairs, lands, token = _grads_to_chips_start(pairs, views, name=f"grads_to_chips_start_l{l}_{names[0]}")
        flying.append((l, names, gs, recv, views, send_sems, recv_sems, pairs, lands))
        return token

    lsum, grad_x, grads, d_mem_g, d_mem_b = _local_step(x, mem, loss_target, mem_ln_g, mem_ln_b, layers, start_exchange)
    loss = lax.psum(0.5 * jnp.sum(lsum) / D_MODEL, ("x", "y", "c"))

    for l, names, gs, recv, views, send_sems, recv_sems, pairs, lands in flying:
        quads = _grads_to_chips_wait(send_sems, recv_sems, pairs, lands, views, grad_x, name=f"grads_to_chips_wait_l{l}_{names[0]}")
        for n, g, rv, q in zip(names, gs, recv, quads):
            parts[l][n] = (g, rv, q)
    halves = _finish_big_grads(parts, c_idx, chip)
    gw = {}
    small = {"mem_ln_g": d_mem_g, "mem_ln_b": d_mem_b}
    for n in SMALL:
        if n in small:
            continue
        per_layer = []
        for l in range(DEPTH):
            g = grads[l][n]
            if n in ("dt_bias", "a_log", "d_skip"):
                g = g[0, :SSM_HEADS]
            per_layer.append(g.reshape(a[n].shape[1:-1] + (-1,)))
        small[n] = jnp.stack(per_layer)
    gw.update(_reduce_small_grads(small, chip, c_idx))

    delta, new_m, new_v = {}, {}, {}
    for n in BIG:
        mine, other = halves[n]
        gw[n], delta[n], new_m[n], new_v[n] = _adamw_halves(a[n], a["m_" + n], a["v_" + n], mine, other, c_idx.reshape(1),
                                                             name=f"adamw_{n}")
    packs = [_pad_rows(jnp.concatenate([src(n).reshape(-1) for n in SMALL]), LANE, SUBLANE)
             for src in (lambda n: a[n], lambda n: gw[n], lambda n: a["m_" + n], lambda n: a["v_" + n])]
    outs = _adamw(*packs, name="adamw_small")
    off = 0
    for n in SMALL:
        sz, shp = a[n].size, a[n].shape
        delta[n], new_m[n], new_v[n] = (o.reshape(-1)[off:off + sz].reshape(shp) for o in outs)
        off += sz
    return (loss, grad_x, *[gw[n].reshape(a[n].shape) for n in WEIGHTS], *[delta[n] for n in WEIGHTS],
            *[new_m[n] for n in WEIGHTS], *[new_v[n] for n in WEIGHTS])
```

```python
import functools
import math

import jax
import jax.numpy as jnp
from jax import lax
from jax.experimental import pallas as pl
from jax.experimental.pallas import tpu as pltpu

F32 = jnp.float32
MXU_DTYPE = jnp.bfloat16
WIRE_DTYPE = jnp.bfloat16

D_MODEL = 1024
DEPTH = 2
CHUNK = 128
SG_GROUPS = 8
SSM_INNER = 2048
SSM_HEADDIM = 64
SSM_HEADS = 32
SSM_STATE = 128
SSM_GROUPS = 4
SSM_CONV = 4
SSM_CONV_DIM = 3072
X_HEADS = 4
X_HEADDIM = 256
FFN_HIDDEN = 2816
ALPHA = float((2 * DEPTH) ** 0.25)
LN_EPS = 1e-5
RMS_EPS = 1e-5
ADAM_LR = 0.001
ADAM_B1 = 0.9
ADAM_B2 = 0.999
ADAM_EPS = 1e-08
ADAM_WD = 0.01
ADAM_STEP = 10

MAIN_COLS = 9216
UVZ_COLS = 4096
GAB_COL0 = 4096
XBC_COL0 = 6144
HEAD_PAD = 128

VMEM_LIMIT = 56 * 1024 * 1024
BLOCK_BYTES = 2 * 1024 * 1024
ROW_TILES = (512, 256, 128)
LANE = 128
SUBLANE = 8

N_CHIPS = 4
N_DEV = 8


def _pick(n, cands):
    for c in cands:
        if n % c == 0:
            return c
    return n


MM_TILE_MAX = 1408
MM_OPERAND_BYTES = 8 * 1024 * 1024


def _div_tile(n, limit):
    best = None
    for t in range(LANE, min(n, limit) + 1, LANE):
        if n % t == 0:
            best = t
    return n if best is None else best


def _params(*sem):
    return pltpu.CompilerParams(dimension_semantics=tuple(sem), vmem_limit_bytes=VMEM_LIMIT)


_ANY = pl.BlockSpec(memory_space=pl.ANY)
_MESH = pl.DeviceIdType.MESH


def _nt(a, b):
    return lax.dot_general(a, b, (((1,), (1,)), ((), ())), preferred_element_type=F32)


def _tn(a, b):
    return lax.dot_general(a, b, (((0,), (0,)), ((), ())), preferred_element_type=F32)


def _nn(a, b):
    return jnp.dot(a, b, preferred_element_type=F32)


def _sigmoid(x):
    return 0.5 * jnp.tanh(0.5 * x) + 0.5


def _split3(v):
    def top(x):
        bits = lax.bitcast_convert_type(x, jnp.uint32) & jnp.uint32(0xFFFF0000)
        return lax.bitcast_convert_type(bits, F32)

    v1 = top(v)
    r1 = v - v1
    v2 = top(r1)
    v3 = r1 - v2
    return v1.astype(jnp.bfloat16), v2.astype(jnp.bfloat16), v3.astype(jnp.bfloat16)


def _dot_exact(a, b, dn, data):
    if data == 0:
        mat = b.astype(jnp.bfloat16)
        return sum(lax.dot_general(p, mat, dn, preferred_element_type=F32) for p in _split3(a))
    mat = a.astype(jnp.bfloat16)
    return sum(lax.dot_general(mat, p, dn, preferred_element_type=F32) for p in _split3(b))


_DN_NN = (((1,), (0,)), ((), ()))
_DN_TN = (((0,), (0,)), ((), ()))


def _gelu(x):
    return 0.5 * x * (1.0 + lax.erf(x * (2.0 ** -0.5)))


def _gelu_grad(x):
    return 0.5 * (1.0 + lax.erf(x * (2.0 ** -0.5))) + x * jnp.exp(-0.5 * x * x) * (1.0 / math.sqrt(2.0 * math.pi))


def _mm(a, b, *, ta=False, tb=False, out_dtype=F32, after=None, name):
    if ta:
        kdim, m = a.shape
    else:
        m, kdim = a.shape
    if tb:
        n, k2 = b.shape[-2:]
    else:
        k2, n = b.shape[-2:]
    assert kdim == k2, (a.shape, b.shape, ta, tb)
    tm = _div_tile(m, MM_TILE_MAX)
    tn = _div_tile(n, MM_TILE_MAX)
    tk = _div_tile(kdim, MM_OPERAND_BYTES // (tm * a.dtype.itemsize + tn * b.dtype.itemsize))
    nk = kdim // tk
    dn = (((0 if ta else 1,), (1 if tb else 0,)), ((), ()))

    extra = [] if after is None else [after]

    def body(a_ref, b_ref, *rest):
        o_ref = rest[len(extra)]
        d = lax.dot_general(a_ref[...].astype(MXU_DTYPE), b_ref[...].astype(MXU_DTYPE), dn, preferred_element_type=F32)
        if nk == 1:
            o_ref[...] = d.astype(out_dtype)
            return
        acc_ref = rest[len(extra) + 1]
        k = pl.program_id(2)

        @pl.when(k == 0)
        def _():
            acc_ref[...] = d

        @pl.when(jnp.logical_and(k > 0, k < nk - 1))
        def _():
            acc_ref[...] += d

        @pl.when(k == nk - 1)
        def _():
            o_ref[...] = (acc_ref[...] + d).astype(out_dtype)

    a_spec = pl.BlockSpec((tk, tm), lambda i, j, k: (k, i)) if ta else pl.BlockSpec((tm, tk), lambda i, j, k: (i, k))
    b_spec = pl.BlockSpec((tn, tk), lambda i, j, k: (j, k)) if tb else pl.BlockSpec((tk, tn), lambda i, j, k: (k, j))
    return pl.pallas_call(
        body, grid=(m // tm, n // tn, nk), in_specs=[a_spec, b_spec] + [_ANY] * len(extra),
        out_specs=pl.BlockSpec((tm, tn), lambda i, j, k: (i, j)),
        out_shape=jax.ShapeDtypeStruct((m, n), out_dtype),
        scratch_shapes=[pltpu.VMEM((tm, tn), F32)] if nk > 1 else [],
        compiler_params=_params("parallel", "parallel", "arbitrary"), name=name)(a, b, *extra)


def _row_spec(tm, c, col=0):
    return pl.BlockSpec((tm, c), lambda i: (i, col))


def _par_spec(shape):
    nd = len(shape)
    return pl.BlockSpec(shape, lambda i: (0,) * nd)


def _ln_fwd(x, f, g, b, *, name):
    t, c = x.shape
    tm = _pick(t, ROW_TILES)
    has_f = f is not None

    def body(*refs):
        if has_f:
            x_ref, f_ref, g_ref, b_ref, y_ref, yb_ref, xh_ref, rs_ref = refs
            r = ALPHA * x_ref[...] + f_ref[...]
        else:
            x_ref, g_ref, b_ref, y_ref, yb_ref, xh_ref, rs_ref = refs
            r = x_ref[...]
        mu = jnp.mean(r, axis=-1, keepdims=True)
        xc = r - mu
        var = jnp.mean(xc * xc, axis=-1, keepdims=True)
        rstd = lax.rsqrt(var + LN_EPS)
        xh = xc * rstd
        y = xh * g_ref[...] + b_ref[...]
        y_ref[...] = y
        yb_ref[...] = y.astype(MXU_DTYPE)
        xh_ref[...] = xh
        rs_ref[...] = jnp.broadcast_to(rstd, rs_ref.shape)

    ins = [x] + ([f] if has_f else []) + [g.reshape(1, c), b.reshape(1, c)]
    in_specs = [_row_spec(tm, c)] * (2 if has_f else 1) + [_par_spec((1, c))] * 2
    return pl.pallas_call(
        body, grid=(t // tm,), in_specs=in_specs,
        out_specs=[_row_spec(tm, c), _row_spec(tm, c), _row_spec(tm, c), _row_spec(tm, LANE)],
        out_shape=[jax.ShapeDtypeStruct((t, c), F32), jax.ShapeDtypeStruct((t, c), MXU_DTYPE),
                   jax.ShapeDtypeStruct((t, c), F32), jax.ShapeDtypeStruct((t, LANE), F32)],
        compiler_params=_params("parallel"), name=name)(*ins)


def _ln_bwd(addends, scales, xh, rs, g, *, name):
    t, c = xh.shape
    tm = _pick(t, ROW_TILES)
    na = len(addends)

    def body(*refs):
        a_refs = refs[:na]
        xh_ref, rs_ref, g_ref, dp_ref, dpb_ref, dg_ref, db_ref = refs[na:]

        @pl.when(pl.program_id(0) == 0)
        def _():
            dg_ref[...] = jnp.zeros_like(dg_ref)
            db_ref[...] = jnp.zeros_like(db_ref)

        dy = None
        for s, r in zip(scales, a_refs):
            term = r[...] if s == 1.0 else s * r[...]
            dy = term if dy is None else dy + term
        xhv = xh_ref[...]
        dxh = dy * g_ref[...]
        m1 = jnp.mean(dxh, axis=-1, keepdims=True)
        m2 = jnp.mean(dxh * xhv, axis=-1, keepdims=True)
        dp = rs_ref[:, 0:1] * (dxh - m1 - xhv * m2)
        dp_ref[...] = dp
        dpb_ref[...] = dp.astype(MXU_DTYPE)
        dg_ref[...] += jnp.sum(dy * xhv, axis=0, keepdims=True)
        db_ref[...] += jnp.sum(dy, axis=0, keepdims=True)

    in_specs = [_row_spec(tm, c)] * (na + 1) + [_row_spec(tm, LANE), _par_spec((1, c))]
    return pl.pallas_call(
        body, grid=(t // tm,), in_specs=in_specs,
        out_specs=[_row_spec(tm, c), _row_spec(tm, c), _par_spec((1, c)), _par_spec((1, c))],
        out_shape=[jax.ShapeDtypeStruct((t, c), F32), jax.ShapeDtypeStruct((t, c), MXU_DTYPE),
                   jax.ShapeDtypeStruct((1, c), F32), jax.ShapeDtypeStruct((1, c), F32)],
        compiler_params=_params("arbitrary"), name=name)(*addends, xh, rs, g.reshape(1, c))


def _add_scaled(addends, scales, *, name):
    t, c = addends[0].shape
    tm = _pick(t, ROW_TILES)
    na = len(addends)

    def body(*refs):
        acc = None
        for s, r in zip(scales, refs[:na]):
            term = r[...] if s == 1.0 else s * r[...]
            acc = term if acc is None else acc + term
        refs[na][...] = acc

    return pl.pallas_call(
        body, grid=(t // tm,), in_specs=[_row_spec(tm, c)] * na, out_specs=_row_spec(tm, c),
        out_shape=jax.ShapeDtypeStruct((t, c), F32), compiler_params=_params("parallel"), name=name)(*addends)


def _loss_head(y, tgt, *, name):
    t, c = y.shape
    tm = _pick(t, ROW_TILES)

    def body(y_ref, t_ref, dy_ref, ls_ref):
        @pl.when(pl.program_id(0) == 0)
        def _():
            ls_ref[...] = jnp.zeros_like(ls_ref)

        e = y_ref[...] - t_ref[...]
        dy_ref[...] = e * (1.0 / c)
        ls_ref[...] += jnp.sum(e * e, axis=0, keepdims=True)

    return pl.pallas_call(
        body, grid=(t // tm,), in_specs=[_row_spec(tm, c)] * 2,
        out_specs=[_row_spec(tm, c), _par_spec((1, c))],
        out_shape=[jax.ShapeDtypeStruct((t, c), F32), jax.ShapeDtypeStruct((1, c), F32)],
        compiler_params=_params("arbitrary"), name=name)(y, tgt)


def _swiglu_fwd(h, *, name):
    t, two_f = h.shape
    fh = two_f // 2
    tm = _pick(t, (256, 128))

    def body(g_ref, u_ref, a_ref):
        g = g_ref[...]
        a_ref[...] = (g * _sigmoid(g) * u_ref[...]).astype(MXU_DTYPE)

    return pl.pallas_call(
        body, grid=(t // tm,), in_specs=[_row_spec(tm, fh, 0), _row_spec(tm, fh, 1)], out_specs=_row_spec(tm, fh),
        out_shape=jax.ShapeDtypeStruct((t, fh), MXU_DTYPE), compiler_params=_params("parallel"), name=name)(h, h)


def _swiglu_bwd(h, da, *, name):
    t, two_f = h.shape
    fh = two_f // 2
    tm = _pick(t, (256, 128))

    def body(g_ref, u_ref, da_ref, dh_ref):
        g = g_ref[...]
        s = _sigmoid(g)
        dav = da_ref[...]
        dh_ref[:, :fh] = (dav * u_ref[...] * (s * (1.0 + g * (1.0 - s)))).astype(MXU_DTYPE)
        dh_ref[:, fh:] = (dav * g * s).astype(MXU_DTYPE)

    return pl.pallas_call(
        body, grid=(t // tm,), in_specs=[_row_spec(tm, fh, 0), _row_spec(tm, fh, 1), _row_spec(tm, fh)],
        out_specs=_row_spec(tm, two_f), out_shape=jax.ShapeDtypeStruct((t, two_f), MXU_DTYPE),
        compiler_params=_params("parallel"), name=name)(h, h, da)


def _attn_probs(q, k):
    s = _nt(q, k) * (X_HEADDIM ** -0.5)
    s = s - jnp.max(s, axis=-1, keepdims=True)
    p = jnp.exp(s)
    return p / jnp.sum(p, axis=-1, keepdims=True)


def _attn_fwd(q, kv, *, bsz, name):
    t = q.shape[0]
    s = t // bsz
    ml = kv.shape[0] // bsz
    hd = X_HEADDIM

    def body(q_ref, k_ref, v_ref, o_ref):
        p = _attn_probs(q_ref[...], k_ref[...])
        o_ref[...] = _nn(p.astype(MXU_DTYPE), v_ref[...]).astype(MXU_DTYPE)

    return pl.pallas_call(
        body, grid=(bsz, X_HEADS),
        in_specs=[pl.BlockSpec((s, hd), lambda b, h: (b, h)), pl.BlockSpec((ml, hd), lambda b, h: (b, h)),
                  pl.BlockSpec((ml, hd), lambda b, h: (b, X_HEADS + h))],
        out_specs=pl.BlockSpec((s, hd), lambda b, h: (b, h)),
        out_shape=jax.ShapeDtypeStruct((t, D_MODEL), MXU_DTYPE),
        compiler_params=_params("parallel", "parallel"), name=name)(q, kv, kv)


def _attn_bwd(q, kv, do, *, bsz, name):
    t = q.shape[0]
    s = t // bsz
    ml = kv.shape[0] // bsz
    hd = X_HEADDIM

    def body(q_ref, k_ref, v_ref, do_ref, dq_ref, dk_ref, dv_ref):
        qv, kk, vv, dov = q_ref[...], k_ref[...], v_ref[...], do_ref[...]
        p = _attn_probs(qv, kk)
        dp = _nt(dov, vv)
        dv_ref[...] = _tn(p.astype(MXU_DTYPE), dov).astype(MXU_DTYPE)
        ds = (p * (dp - jnp.sum(dp * p, axis=-1, keepdims=True)) * (X_HEADDIM ** -0.5)).astype(MXU_DTYPE)
        dq_ref[...] = _nn(ds, kk).astype(MXU_DTYPE)
        dk_ref[...] = _tn(ds, qv).astype(MXU_DTYPE)

    blk_q = pl.BlockSpec((s, hd), lambda b, h: (b, h))
    blk_m = pl.BlockSpec((ml, hd), lambda b, h: (b, h))
    return pl.pallas_call(
        body, grid=(bsz, X_HEADS),
        in_specs=[blk_q, blk_m, pl.BlockSpec((ml, hd), lambda b, h: (b, X_HEADS + h)), blk_q],
        out_specs=[blk_q, blk_m, blk_m],
        out_shape=[jax.ShapeDtypeStruct((t, D_MODEL), MXU_DTYPE), jax.ShapeDtypeStruct((bsz * ml, D_MODEL), MXU_DTYPE),
                   jax.ShapeDtypeStruct((bsz * ml, D_MODEL), MXU_DTYPE)],
        compiler_params=_params("parallel", "parallel"), name=name)(q, kv, kv, do)


def _causal(n):
    row = lax.broadcasted_iota(jnp.int32, (n, n), 0)
    col = lax.broadcasted_iota(jnp.int32, (n, n), 1)
    return row >= col


def _sg_norm(v, g, b):
    gv = _gelu(v)
    mu = jnp.mean(gv, axis=-1, keepdims=True)
    xc = gv - mu
    var = jnp.mean(xc * xc, axis=-1, keepdims=True)
    rstd = lax.rsqrt(var + LN_EPS)
    xh = xc * rstd
    return xh, rstd, xh * g + b


def _sg_fwd(proj, ln_g, ln_b, w, bcol, *, name):
    t = proj.shape[0]
    c = D_MODEL
    gd = c // SG_GROUPS

    def body(u_ref, v_ref, g_ref, b_ref, w_ref, bc_ref, o_ref):
        gu = _gelu(u_ref[...])
        _, _, vn = _sg_norm(v_ref[...], g_ref[...], b_ref[...])
        mask = _causal(CHUNK)
        for g in range(SG_GROUPS):
            sl = slice(g * gd, (g + 1) * gd)
            wg = jnp.where(mask, w_ref[g], 0.0).astype(MXU_DTYPE)
            mixed = _nn(wg, vn[:, sl].astype(MXU_DTYPE)) + bc_ref[g]
            o_ref[:, sl] = (gu[:, sl] * mixed).astype(MXU_DTYPE)

    return pl.pallas_call(
        body, grid=(t // CHUNK,),
        in_specs=[_row_spec(CHUNK, c, 0), _row_spec(CHUNK, c, 1), _par_spec((1, c)), _par_spec((1, c)),
                  _par_spec((SG_GROUPS, CHUNK, CHUNK)), _par_spec((SG_GROUPS, CHUNK, 1))],
        out_specs=_row_spec(CHUNK, c), out_shape=jax.ShapeDtypeStruct((t, c), MXU_DTYPE),
        compiler_params=_params("parallel"), name=name)(proj, proj, ln_g.reshape(1, c), ln_b.reshape(1, c), w, bcol)


def _sg_bwd(proj, dsgo, ln_g, ln_b, w, bcol, dproj, *, name):
    t = proj.shape[0]
    c = D_MODEL
    gd = c // SG_GROUPS

    def body(u_ref, v_ref, d_ref, g_ref, b_ref, w_ref, bc_ref, _, duv_ref, dw_ref, dbc_ref, dg_ref, db_ref, dvn_ref):
        @pl.when(pl.program_id(0) == 0)
        def _():
            dw_ref[...] = jnp.zeros_like(dw_ref)
            dbc_ref[...] = jnp.zeros_like(dbc_ref)
            dg_ref[...] = jnp.zeros_like(dg_ref)
            db_ref[...] = jnp.zeros_like(db_ref)

        u = u_ref[...]
        v = v_ref[...]
        dso = d_ref[...]
        gu = _gelu(u)
        xh, rstd, vn = _sg_norm(v, g_ref[...], b_ref[...])
        mask = _causal(CHUNK)
        for g in range(SG_GROUPS):
            sl = slice(g * gd, (g + 1) * gd)
            wg = jnp.where(mask, w_ref[g], 0.0).astype(MXU_DTYPE)
            vng = vn[:, sl].astype(MXU_DTYPE)
            mixed = _nn(wg, vng) + bc_ref[g]
            duv_ref[:, sl] = (dso[:, sl] * mixed * _gelu_grad(u[:, sl])).astype(MXU_DTYPE)
            dmix = dso[:, sl] * gu[:, sl]
            dmb = dmix.astype(MXU_DTYPE)
            dbc_ref[g] += jnp.sum(dmix, axis=-1, keepdims=True)
            dw_ref[g] += jnp.where(mask, _nt(dmb, vng), 0.0)
            dvn_ref[:, sl] = _tn(wg, dmb)
        dvn = dvn_ref[...]
        dg_ref[...] += jnp.sum(dvn * xh, axis=0, keepdims=True)
        db_ref[...] += jnp.sum(dvn, axis=0, keepdims=True)
        dxh = dvn * g_ref[...]
        m1 = jnp.mean(dxh, axis=-1, keepdims=True)
        m2 = jnp.mean(dxh * xh, axis=-1, keepdims=True)
        dgv = rstd * (dxh - m1 - xh * m2)
        duv_ref[:, c:] = (dgv * _gelu_grad(v)).astype(MXU_DTYPE)

    return pl.pallas_call(
        body, grid=(t // CHUNK,),
        in_specs=[_row_spec(CHUNK, c, 0), _row_spec(CHUNK, c, 1), _row_spec(CHUNK, c), _par_spec((1, c)),
                  _par_spec((1, c)), _par_spec((SG_GROUPS, CHUNK, CHUNK)), _par_spec((SG_GROUPS, CHUNK, 1)), _ANY],
        out_specs=[_row_spec(CHUNK, 2 * c), _par_spec((SG_GROUPS, CHUNK, CHUNK)), _par_spec((SG_GROUPS, CHUNK, 1)),
                   _par_spec((1, c)), _par_spec((1, c))],
        out_shape=[jax.ShapeDtypeStruct(dproj.shape, dproj.dtype), jax.ShapeDtypeStruct((SG_GROUPS, CHUNK, CHUNK), F32),
                   jax.ShapeDtypeStruct((SG_GROUPS, CHUNK, 1), F32), jax.ShapeDtypeStruct((1, c), F32),
                   jax.ShapeDtypeStruct((1, c), F32)],
        scratch_shapes=[pltpu.VMEM((CHUNK, c), F32)], input_output_aliases={7: 0},
        compiler_params=_params("arbitrary"), name=name)(proj, proj, dsgo, ln_g.reshape(1, c), ln_b.reshape(1, c), w, bcol, dproj)


CONV_TC = 512


def _conv_taps(x):
    rows = lax.broadcasted_iota(jnp.int32, x.shape, 0)
    taps = [jnp.where(rows >= SSM_CONV - 1 - k, pltpu.roll(x, SSM_CONV - 1 - k, axis=0), 0.0) for k in range(SSM_CONV - 1)]
    return taps + [x]


def _conv_pre(taps, w_ref, b_ref):
    acc = b_ref[...]
    for k in range(SSM_CONV):
        acc = acc + taps[k] * w_ref[k:k + 1, :]
    return acc


def _conv_fwd(proj, w, b, *, bsz, name):
    t = proj.shape[0]
    s = t // bsz
    nj = SSM_CONV_DIM // CONV_TC
    c0 = XBC_COL0 // CONV_TC

    def body(x_ref, w_ref, b_ref, o_ref):
        pre = _conv_pre(_conv_taps(x_ref[...]), w_ref, b_ref)
        o_ref[...] = pre * _sigmoid(pre)

    return pl.pallas_call(
        body, grid=(bsz, nj),
        in_specs=[pl.BlockSpec((s, CONV_TC), lambda bb, j: (bb, c0 + j)), pl.BlockSpec((SSM_CONV, CONV_TC), lambda bb, j: (0, j)),
                  pl.BlockSpec((1, CONV_TC), lambda bb, j: (0, j))],
        out_specs=pl.BlockSpec((s, CONV_TC), lambda bb, j: (bb, j)),
        out_shape=jax.ShapeDtypeStruct((t, SSM_CONV_DIM), F32),
        compiler_params=_params("parallel", "parallel"), name=name)(proj, w, b.reshape(1, -1))


def _conv_bwd(proj, dact, w, b, dproj, *, bsz, name):
    t = proj.shape[0]
    s = t // bsz
    nj = SSM_CONV_DIM // CONV_TC
    c0 = XBC_COL0 // CONV_TC

    def body(x_ref, d_ref, w_ref, b_ref, _, dx_ref, dw_ref, db_ref):
        @pl.when(pl.program_id(1) == 0)
        def _():
            dw_ref[...] = jnp.zeros_like(dw_ref)
            db_ref[...] = jnp.zeros_like(db_ref)

        taps = _conv_taps(x_ref[...])
        pre = _conv_pre(taps, w_ref, b_ref)
        sg = _sigmoid(pre)
        dpre = d_ref[...] * (sg * (1.0 + pre * (1.0 - sg)))
        rows = lax.broadcasted_iota(jnp.int32, dpre.shape, 0)
        db_ref[...] += jnp.sum(dpre, axis=0, keepdims=True)
        dx = dpre * w_ref[SSM_CONV - 1:SSM_CONV, :]
        for k in range(SSM_CONV):
            dw_ref[k:k + 1, :] += jnp.sum(dpre * taps[k], axis=0, keepdims=True)
        for k in range(SSM_CONV - 1):
            sh = SSM_CONV - 1 - k
            dsh = jnp.where(rows < s - sh, pltpu.roll(dpre, s - sh, axis=0), 0.0)
            dx = dx + dsh * w_ref[k:k + 1, :]
        dx_ref[...] = dx.astype(MXU_DTYPE)

    return pl.pallas_call(
        body, grid=(nj, bsz),
        in_specs=[pl.BlockSpec((s, CONV_TC), lambda j, bb: (bb, c0 + j)), pl.BlockSpec((s, CONV_TC), lambda j, bb: (bb, j)),
                  pl.BlockSpec((SSM_CONV, CONV_TC), lambda j, bb: (0, j)), pl.BlockSpec((1, CONV_TC), lambda j, bb: (0, j)), _ANY],
        out_specs=[pl.BlockSpec((s, CONV_TC), lambda j, bb: (bb, c0 + j)), pl.BlockSpec((SSM_CONV, CONV_TC), lambda j, bb: (0, j)),
                   pl.BlockSpec((1, CONV_TC), lambda j, bb: (0, j))],
        out_shape=[jax.ShapeDtypeStruct(dproj.shape, dproj.dtype), jax.ShapeDtypeStruct((SSM_CONV, SSM_CONV_DIM), F32),
                   jax.ShapeDtypeStruct((1, SSM_CONV_DIM), F32)],
        input_output_aliases={4: 0},
        compiler_params=_params("parallel", "arbitrary"), name=name)(proj, dact, w, b.reshape(1, -1), dproj)


def _softplus(x):
    return jnp.maximum(x, 0.0) + jnp.log1p(jnp.exp(-jnp.abs(x)))


def _pad_heads(v):
    return jnp.broadcast_to(jnp.pad(v.astype(F32), (0, HEAD_PAD - SSM_HEADS))[None, :], (SUBLANE, HEAD_PAD))


def _ssd_prep(dt_raw, dt_bias8, a_log8, *, name):
    t = dt_raw.shape[0]
    n = CHUNK

    def body(r_ref, b_ref, al_ref, dt_ref, cs_ref, dtt_ref, cst_ref):
        dt = _softplus(r_ref[...] + b_ref[0:1, :])
        da = dt * (-jnp.exp(al_ref[0:1, :]))
        row = lax.broadcasted_iota(jnp.int32, (n, n), 0)
        col = lax.broadcasted_iota(jnp.int32, (n, n), 1)
        lower = (col <= row).astype(F32)
        upper = (row <= col).astype(F32)
        eye = (row == col).astype(F32)
        dt_ref[...] = dt
        cs_ref[...] = _dot_exact(lower, da, _DN_NN, 1)
        cst_ref[0] = _dot_exact(da, upper, _DN_TN, 0)
        dtt_ref[0] = _dot_exact(dt, eye, _DN_TN, 0)

    hp = HEAD_PAD
    return pl.pallas_call(
        body, grid=(t // n,),
        in_specs=[_row_spec(n, hp), _par_spec((SUBLANE, hp)), _par_spec((SUBLANE, hp))],
        out_specs=[_row_spec(n, hp), _row_spec(n, hp), pl.BlockSpec((1, hp, n), lambda i: (i, 0, 0)),
                   pl.BlockSpec((1, hp, n), lambda i: (i, 0, 0))],
        out_shape=[jax.ShapeDtypeStruct((t, hp), F32), jax.ShapeDtypeStruct((t, hp), F32),
                   jax.ShapeDtypeStruct((t // n, hp, n), F32), jax.ShapeDtypeStruct((t // n, hp, n), F32)],
        compiler_params=_params("parallel"), name=name)(dt_raw, dt_bias8, a_log8)


def _expand_mat():
    h = lax.broadcasted_iota(jnp.int32, (HEAD_PAD, SSM_INNER), 0)
    ch = lax.broadcasted_iota(jnp.int32, (HEAD_PAD, SSM_INNER), 1)
    return (ch // SSM_HEADDIM == h).astype(F32)


def _reduce_mat():
    ch = lax.broadcasted_iota(jnp.int32, (SSM_INNER, HEAD_PAD), 0)
    h = lax.broadcasted_iota(jnp.int32, (SSM_INNER, HEAD_PAD), 1)
    return (ch // SSM_HEADDIM == h).astype(F32)


def _expand(v, em):
    return _dot_exact(v, em, _DN_NN, 0)


def _expand_heads(v):
    return jnp.repeat(v.astype(F32), SSM_HEADDIM)[None, :]


def _decay_mat(cs_ref, cst_ref, h, mask):
    seg = cs_ref[:, h:h + 1] - cst_ref[0, h:h + 1, :]
    return jnp.where(mask, jnp.exp(jnp.minimum(seg, 0.0)), 0.0)


GROUP_CH = SSM_INNER // SSM_GROUPS
PAIRS_PER_GROUP = GROUP_CH // LANE
HEADS_PER_GROUP = SSM_HEADS // SSM_GROUPS
BM_COL0 = SSM_INNER
CM_COL0 = SSM_INNER + SSM_GROUPS * SSM_STATE


def _ssd_specs(nc, rev):
    def cidx(i):
        return (i // nc) * nc + (nc - 1 - i % nc) if rev else i

    n = CHUNK
    xs = pl.BlockSpec((n, SSM_INNER), lambda i: (cidx(i), 0))
    bm = pl.BlockSpec((n, GROUP_CH), lambda i: (cidx(i), BM_COL0 // GROUP_CH))
    cm = pl.BlockSpec((n, GROUP_CH), lambda i: (cidx(i), CM_COL0 // GROUP_CH))
    hv = pl.BlockSpec((n, HEAD_PAD), lambda i: (cidx(i), 0))
    hvt = pl.BlockSpec((1, HEAD_PAD, n), lambda i: (cidx(i), 0, 0))
    st = pl.BlockSpec((1, SSM_INNER, SSM_STATE), lambda i: (cidx(i), 0, 0))
    return xs, bm, cm, hv, hvt, st


def _ssd_fwd(xbc, dt, cs, dtt, cst, dskx, *, nc, name):
    t = xbc.shape[0]
    n = CHUNK
    xs_s, bm_s, cm_s, hv_s, hvt_s, st_s = _ssd_specs(nc, False)

    def body(xs_ref, bm_ref, cm_ref, dt_ref, cs_ref, dtt_ref, cst_ref, dsk_ref, y_ref, st_ref, prev):
        @pl.when(pl.program_id(0) % nc == 0)
        def _():
            prev[...] = jnp.zeros_like(prev)

        st_ref[0] = prev[...]
        em = _expand_mat()
        dtx = _expand(dt_ref[...], em)
        csx = _expand(cs_ref[...], em)
        dskx = dsk_ref[...]
        xs = xs_ref[...]
        xdt = xs * dtx
        ecs = jnp.exp(csx)
        dec = jnp.exp(csx[n - 1:n, :] - csx)
        mask = _causal(n)
        lane = lax.broadcasted_iota(jnp.int32, (n, LANE), 1)
        for g in range(SSM_GROUPS):
            gs = slice(g * SSM_STATE, (g + 1) * SSM_STATE)
            gc = slice(g * GROUP_CH, (g + 1) * GROUP_CH)
            cmat = cm_ref[:, gs].astype(MXU_DTYPE)
            bmat = bm_ref[:, gs].astype(MXU_DTYPE)
            cb = _nt(cmat, bmat)
            yoff = ecs[:, gc] * _nt(cmat, prev[gc, :].astype(MXU_DTYPE))
            for q in range(PAIRS_PER_GROUP):
                hp = g * PAIRS_PER_GROUP + q
                sl = slice(hp * LANE, (hp + 1) * LANE)
                xp = xdt[:, sl].astype(MXU_DTYPE)
                m0 = (cb * _decay_mat(cs_ref, cst_ref, 2 * hp, mask)).astype(MXU_DTYPE)
                m1 = (cb * _decay_mat(cs_ref, cst_ref, 2 * hp + 1, mask)).astype(MXU_DTYPE)
                yd = jnp.where(lane < SSM_HEADDIM, _nn(m0, xp), _nn(m1, xp))
                y_ref[:, sl] = yd + yoff[:, q * LANE:(q + 1) * LANE] + xs[:, sl] * dskx[:, sl]
            snew = _tn((xdt[:, gc] * dec[:, gc]).astype(MXU_DTYPE), bmat)
            for r in range(HEADS_PER_GROUP):
                h = g * HEADS_PER_GROUP + r
                rows = slice(h * SSM_HEADDIM, (h + 1) * SSM_HEADDIM)
                e = jnp.exp(cst_ref[0, h:h + 1, n - 1:n])
                prev[rows, :] = prev[rows, :] * e + snew[r * SSM_HEADDIM:(r + 1) * SSM_HEADDIM, :]

    return pl.pallas_call(
        body, grid=(t // n,),
        in_specs=[xs_s, bm_s, cm_s, hv_s, hv_s, hvt_s, hvt_s, _par_spec((1, SSM_INNER))],
        out_specs=[xs_s, st_s],
        out_shape=[jax.ShapeDtypeStruct((t, SSM_INNER), F32), jax.ShapeDtypeStruct((t // n, SSM_INNER, SSM_STATE), F32)],
        scratch_shapes=[pltpu.VMEM((SSM_INNER, SSM_STATE), F32)],
        compiler_params=_params("arbitrary"), name=name)(xbc, xbc, xbc, dt, cs, dtt, cst, dskx)


def _ssd_bwd(dy, xbc, dt, cs, dtt, cst, st, dskx, a_log8, dt_raw, dt_bias8, *, nc, name):
    t = xbc.shape[0]
    n = CHUNK
    xs_s, bm_s, cm_s, hv_s, hvt_s, st_s = _ssd_specs(nc, True)
    acc_s = _par_spec((1, HEAD_PAD))
    xbc_s = pl.BlockSpec((n, SSM_CONV_DIM), xs_s.index_map)

    def body(dy_ref, xs_ref, bm_ref, cm_ref, dt_ref, cs_ref, dtt_ref, cst_ref, st_ref, dsk_ref, al_ref, raw_ref, bias_ref,
             dxbc_ref, ddr_ref, dal_ref, dds_ref, dbias_ref, dprev, dxdt_s, tdec_s, tcs_s):
        @pl.when(pl.program_id(0) % nc == 0)
        def _():
            dprev[...] = jnp.zeros_like(dprev)

        @pl.when(pl.program_id(0) == 0)
        def _():
            dal_ref[...] = jnp.zeros_like(dal_ref)
            dds_ref[...] = jnp.zeros_like(dds_ref)
            dbias_ref[...] = jnp.zeros_like(dbias_ref)

        em = _expand_mat()
        rm = _reduce_mat()

        def head_reduce(v):
            return _dot_exact(v, rm, _DN_NN, 0)

        dtv = dt_ref[...]
        csv = cs_ref[...]
        dtx = _expand(dtv, em)
        csx = _expand(csv, em)
        dskx = dsk_ref[...]
        xs = xs_ref[...]
        dyv = dy_ref[...]
        xdt = xs * dtx
        ecs = jnp.exp(csx)
        dec = jnp.exp(csx[n - 1:n, :] - csx)
        mask = _causal(n)
        lane = lax.broadcasted_iota(jnp.int32, (n, LANE), 1)
        hlane = lax.broadcasted_iota(jnp.int32, (1, HEAD_PAD), 1)
        hsub = lax.broadcasted_iota(jnp.int32, (HEAD_PAD, 1), 0)
        rsum = jnp.zeros((n, HEAD_PAD), F32)
        csum = jnp.zeros((HEAD_PAD, n), F32)
        for g in range(SSM_GROUPS):
            gs = slice(g * SSM_STATE, (g + 1) * SSM_STATE)
            gc = slice(g * GROUP_CH, (g + 1) * GROUP_CH)
            cmat = cm_ref[:, gs].astype(MXU_DTYPE)
            bmat = bm_ref[:, gs].astype(MXU_DTYPE)
            cb = _nt(cmat, bmat)
            pg = st_ref[0, gc, :].astype(MXU_DTYPE)
            dpg = dprev[gc, :]
            dpgb = dpg.astype(MXU_DTYPE)
            z = _nt(cmat, pg)
            dyg = dyv[:, gc]
            dz = (dyg * ecs[:, gc]).astype(MXU_DTYPE)
            dc = _nn(dz, pg)
            dprev_y = _tn(dz, cmat)
            tcs_s[:, gc] = dyg * z * ecs[:, gc]
            xd = xdt[:, gc] * dec[:, gc]
            wmat = _nt(bmat, dpgb)
            db = _nn(xd.astype(MXU_DTYPE), dpgb)
            tdec_s[:, gc] = wmat * xd
            dxdt_g = wmat * dec[:, gc]
            dcb = jnp.zeros((n, n), F32)
            for q in range(PAIRS_PER_GROUP):
                hp = g * PAIRS_PER_GROUP + q
                sl = slice(hp * LANE, (hp + 1) * LANE)
                xp = xdt[:, sl].astype(MXU_DTYPE)
                dyp = dyv[:, sl]
                dypb = dyp.astype(MXU_DTYPE)
                dxp = None
                for hh in range(2):
                    h = 2 * hp + hh
                    lm = _decay_mat(cs_ref, cst_ref, h, mask)
                    mine = (lane < SSM_HEADDIM) if hh == 0 else (lane >= SSM_HEADDIM)
                    dm = _nt(jnp.where(mine, dyp, 0.0).astype(MXU_DTYPE), xp)
                    dml = dm * lm
                    dcb = dcb + dml
                    gseg = dml * cb
                    rsum = rsum + jnp.sum(gseg, axis=1, keepdims=True) * (hlane == h).astype(F32)
                    csum = csum + (hsub == h).astype(F32) * jnp.sum(gseg, axis=0, keepdims=True)
                    dxh = _tn((cb * lm).astype(MXU_DTYPE), dypb)
                    dxp = dxh if dxp is None else jnp.where(mine, dxh, dxp)
                dxdt_s[:, sl] = dxdt_g[:, q * LANE:(q + 1) * LANE] + dxp
            dcbb = dcb.astype(MXU_DTYPE)
            dxbc_ref[:, CM_COL0 + g * SSM_STATE:CM_COL0 + (g + 1) * SSM_STATE] = dc + _nn(dcbb, bmat)
            dxbc_ref[:, BM_COL0 + g * SSM_STATE:BM_COL0 + (g + 1) * SSM_STATE] = db + _tn(dcbb, cmat)
            for r in range(HEADS_PER_GROUP):
                h = g * HEADS_PER_GROUP + r
                rows = slice(h * SSM_HEADDIM, (h + 1) * SSM_HEADDIM)
                lr = slice(r * SSM_HEADDIM, (r + 1) * SSM_HEADDIM)
                e = jnp.exp(cst_ref[0, h:h + 1, n - 1:n])
                dprev[rows, :] = dpg[lr, :] * e + dprev_y[lr, :]
            tq = _dot_exact(dpg * st_ref[0, gc, :], rm[gc, :], _DN_TN, 0)
            if g == 0:
                qsum = jnp.sum(tq, axis=0, keepdims=True)
            else:
                qsum = qsum + jnp.sum(tq, axis=0, keepdims=True)
        dxdt = dxdt_s[...]
        dxbc_ref[:, 0:SSM_INNER] = dxdt * dtx + dyv * dskx
        ddt = head_reduce(dxdt * xs)
        edec = head_reduce(tdec_s[...])
        ycs = head_reduce(tcs_s[...])
        row = lax.broadcasted_iota(jnp.int32, (n, HEAD_PAD), 0)
        extra = jnp.sum(edec, axis=0, keepdims=True) + qsum * jnp.exp(csv[n - 1:n, :])
        dcs = rsum - csum.T + ycs - edec + jnp.where(row == n - 1, extra, 0.0)
        r2 = lax.broadcasted_iota(jnp.int32, (n, n), 0)
        c2 = lax.broadcasted_iota(jnp.int32, (n, n), 1)
        dda = _dot_exact((c2 >= r2).astype(F32), dcs, _DN_NN, 1)
        a_row = -jnp.exp(al_ref[0:1, :])
        ddt = ddt + dda * a_row
        dal_ref[...] += jnp.sum(dda * dtv, axis=0, keepdims=True) * a_row
        dds_ref[...] += jnp.sum(head_reduce(dyv * xs), axis=0, keepdims=True)
        ddr = ddt * _sigmoid(raw_ref[...] + bias_ref[0:1, :])
        ddr_ref[...] = ddr
        dbias_ref[...] += jnp.sum(ddr, axis=0, keepdims=True)

    par8 = _par_spec((SUBLANE, HEAD_PAD))
    return pl.pallas_call(
        body, grid=(t // n,),
        in_specs=[xs_s, xs_s, bm_s, cm_s, hv_s, hv_s, hvt_s, hvt_s, st_s, _par_spec((1, SSM_INNER)), par8, hv_s, par8],
        out_specs=[xbc_s, hv_s, acc_s, acc_s, acc_s],
        out_shape=[jax.ShapeDtypeStruct((t, SSM_CONV_DIM), F32), jax.ShapeDtypeStruct((t, HEAD_PAD), F32),
                   jax.ShapeDtypeStruct((1, HEAD_PAD), F32), jax.ShapeDtypeStruct((1, HEAD_PAD), F32),
                   jax.ShapeDtypeStruct((1, HEAD_PAD), F32)],
        scratch_shapes=[pltpu.VMEM((SSM_INNER, SSM_STATE), F32), pltpu.VMEM((n, SSM_INNER), F32),
                        pltpu.VMEM((n, SSM_INNER), F32), pltpu.VMEM((n, SSM_INNER), F32)],
        compiler_params=_params("arbitrary"), name=name)(dy, xbc, xbc, xbc, dt, cs, dtt, cst, st, dskx, a_log8, dt_raw, dt_bias8)


def _gate_norm_fwd(y, proj, norm_g, *, name):
    t, c = y.shape
    tm = _pick(t, (256, 128))

    def body(y_ref, z_ref, g_ref, o_ref):
        z = z_ref[...]
        yz = y_ref[...] * z * _sigmoid(z)
        for g in range(SSM_GROUPS):
            gc = slice(g * GROUP_CH, (g + 1) * GROUP_CH)
            seg = yz[:, gc]
            r = lax.rsqrt(jnp.mean(seg * seg, axis=-1, keepdims=True) + RMS_EPS)
            o_ref[:, gc] = (seg * r * g_ref[:, gc]).astype(MXU_DTYPE)

    return pl.pallas_call(
        body, grid=(t // tm,), in_specs=[_row_spec(tm, c), _row_spec(tm, c, 1), _par_spec((1, c))],
        out_specs=_row_spec(tm, c), out_shape=jax.ShapeDtypeStruct((t, c), MXU_DTYPE),
        compiler_params=_params("parallel"), name=name)(y, proj, norm_g.reshape(1, c))


def _gate_norm_bwd(dyb, y, proj, norm_g, dproj, *, name):
    t, c = y.shape
    tm = _pick(t, (256, 128))

    def body(d_ref, y_ref, z_ref, g_ref, _, dy_ref, dz_ref, dg_ref):
        @pl.when(pl.program_id(0) == 0)
        def _():
            dg_ref[...] = jnp.zeros_like(dg_ref)

        z = z_ref[...]
        yv = y_ref[...]
        sz = _sigmoid(z)
        silu = z * sz
        yz = yv * silu
        dv = d_ref[...]
        for g in range(SSM_GROUPS):
            gc = slice(g * GROUP_CH, (g + 1) * GROUP_CH)
            seg = yz[:, gc]
            r = lax.rsqrt(jnp.mean(seg * seg, axis=-1, keepdims=True) + RMS_EPS)
            nrm = seg * r
            dn = dv[:, gc] * g_ref[:, gc]
            dg_ref[:, gc] += jnp.sum(dv[:, gc] * nrm, axis=0, keepdims=True)
            dyz = r * (dn - nrm * jnp.mean(dn * nrm, axis=-1, keepdims=True))
            dy_ref[:, gc] = dyz * silu[:, gc]
            dz_ref[:, gc] = (dyz * yv[:, gc] * (sz[:, gc] * (1.0 + z[:, gc] * (1.0 - sz[:, gc])))).astype(MXU_DTYPE)

    return pl.pallas_call(
        body, grid=(t // tm,), in_specs=[_row_spec(tm, c), _row_spec(tm, c), _row_spec(tm, c, 1), _par_spec((1, c)), _ANY],
        out_specs=[_row_spec(tm, c), _row_spec(tm, c, 1), _par_spec((1, c))],
        out_shape=[jax.ShapeDtypeStruct((t, c), F32), jax.ShapeDtypeStruct(dproj.shape, dproj.dtype),
                   jax.ShapeDtypeStruct((1, c), F32)],
        input_output_aliases={4: 1},
        compiler_params=_params("arbitrary"), name=name)(dyb, y, proj, norm_g.reshape(1, c), dproj)


GA_COLBLK = GAB_COL0 // D_MODEL


def _merge_fwd(br_a, br_b, proj, *, name):
    t, c = br_a.shape
    tm = _pick(t, ROW_TILES)

    def body(a_ref, b_ref, ga_ref, gb_ref, o_ref):
        o_ref[...] = (_sigmoid(ga_ref[...]) * a_ref[...] + _sigmoid(gb_ref[...]) * b_ref[...]).astype(MXU_DTYPE)

    return pl.pallas_call(
        body, grid=(t // tm,),
        in_specs=[_row_spec(tm, c), _row_spec(tm, c), _row_spec(tm, c, GA_COLBLK), _row_spec(tm, c, GA_COLBLK + 1)],
        out_specs=_row_spec(tm, c), out_shape=jax.ShapeDtypeStruct((t, c), MXU_DTYPE),
        compiler_params=_params("parallel"), name=name)(br_a, br_b, proj, proj)


def _merge_bwd(dm, br_a, br_b, proj, *, name):
    t, c = br_a.shape
    tm = _pick(t, ROW_TILES)

    def body(dm_ref, a_ref, b_ref, ga_ref, gb_ref, da_ref, db_ref, dg_ref):
        d = dm_ref[...]
        sa = _sigmoid(ga_ref[...])
        sb = _sigmoid(gb_ref[...])
        da_ref[...] = (d * sa).astype(MXU_DTYPE)
        db_ref[...] = (d * sb).astype(MXU_DTYPE)
        dg_ref[:, :c] = (d * a_ref[...] * sa * (1.0 - sa)).astype(MXU_DTYPE)
        dg_ref[:, c:] = (d * b_ref[...] * sb * (1.0 - sb)).astype(MXU_DTYPE)

    return pl.pallas_call(
        body, grid=(t // tm,),
        in_specs=[_row_spec(tm, c), _row_spec(tm, c), _row_spec(tm, c), _row_spec(tm, c, GA_COLBLK), _row_spec(tm, c, GA_COLBLK + 1)],
        out_specs=[_row_spec(tm, c), _row_spec(tm, c), _row_spec(tm, 2 * c, GAB_COL0 // (2 * c))],
        out_shape=[jax.ShapeDtypeStruct((t, c), MXU_DTYPE), jax.ShapeDtypeStruct((t, c), MXU_DTYPE),
                   jax.ShapeDtypeStruct((t, MAIN_COLS), MXU_DTYPE)],
        compiler_params=_params("parallel"), name=name)(dm, br_a, br_b, proj, proj)


def _layer_fwd(x, xb, memn_b, w, *, bsz, tag):
    nc = x.shape[0] // bsz // CHUNK
    sv = {"x_in": xb}
    proj = _mm(xb, w["w_main"], name=f"{tag}_proj")
    dt_raw = _mm(xb, w["w_dt"], name=f"{tag}_dtproj")
    sgo = _sg_fwd(proj, w["sg_ln_g"], w["sg_ln_b"], w["sg_w"], w["sg_bcol"], name=f"{tag}_sg_fwd")
    xbc = _conv_fwd(proj, w["conv_w"], w["conv_b"], bsz=bsz, name=f"{tag}_conv_fwd")
    dt, cs, dtt, cst = _ssd_prep(dt_raw, w["dt_bias8"], w["a_log8"], name=f"{tag}_ssd_prep")
    y, st = _ssd_fwd(xbc, dt, cs, dtt, cst, w["d_skipx"], nc=nc, name=f"{tag}_ssd_fwd")
    yb = _gate_norm_fwd(y, proj, w["ssm_norm_g"], name=f"{tag}_gate_norm_fwd")
    if "rest" in w:
        w = w["rest"](w, yb)
    br_a = _mm(sgo, w["p_a"], name=f"{tag}_br_a")
    br_b = _mm(yb, w["p_b"], name=f"{tag}_br_b")
    merged = _merge_fwd(br_a, br_b, proj, name=f"{tag}_merge_fwd")
    mix = _mm(merged, w["w_mix_o"], name=f"{tag}_mix_o")
    x1, x1b, xh1, rs1 = _ln_fwd(x, mix, w["ln_g"][0], w["ln_b"][0], name=f"{tag}_ln1_fwd")
    sv.update(proj=proj, dt_raw=dt_raw, sgo=sgo, xbc=xbc, dt=dt, cs=cs, dtt=dtt, cst=cst, y=y, st=st, yb=yb,
              br_a=br_a, br_b=br_b, merged=merged, xh1=xh1, rs1=rs1, x1b=x1b)
    q = _mm(x1b, w["w_xq"], out_dtype=MXU_DTYPE, name=f"{tag}_q")
    kv = _mm(memn_b, w["w_xkv"], out_dtype=MXU_DTYPE, name=f"{tag}_kv")
    o = _attn_fwd(q, kv, bsz=bsz, name=f"{tag}_attn_fwd")
    att = _mm(o, w["w_xo"], name=f"{tag}_xo")
    x2, x2b, xh2, rs2 = _ln_fwd(x1, att, w["ln_g"][1], w["ln_b"][1], name=f"{tag}_ln2_fwd")
    sv.update(q=q, kv=kv, o=o, xh2=xh2, rs2=rs2, x2b=x2b)
    h = _mm(x2b, w["w_ffn_in"], name=f"{tag}_ffn_in")
    a = _swiglu_fwd(h, name=f"{tag}_swiglu_fwd")
    ffn = _mm(a, w["w_ffn_out"], name=f"{tag}_ffn_out")
    x3, x3b, xh3, rs3 = _ln_fwd(x2, ffn, w["ln_g"][2], w["ln_b"][2], name=f"{tag}_ln3_fwd")
    sv.update(h=h, a=a, xh3=xh3, rs3=rs3)
    return x3, x3b, sv, w


GRAD_GROUPS = (("w_ffn_out", "w_ffn_in", "w_xo", "w_xq", "w_xkv"), ("w_mix_o", "p_a", "p_b"), ("w_in",))


def _layer_bwd(dx3_addends, dx3_scales, memn_b, w, sv, on_group=None, *, bsz, tag):
    nc = sv["xh1"].shape[0] // bsz // CHUNK
    gr = {}

    def group_done(k):
        return on_group(GRAD_GROUPS[k], gr) if on_group is not None else None
    dp3, dp3b, dg3, db3 = _ln_bwd(dx3_addends, dx3_scales, sv["xh3"], sv["rs3"], w["ln_g"][2], name=f"{tag}_ln3_bwd")
    da = _mm(dp3b, w["w_ffn_out"], tb=True, name=f"{tag}_d_a")
    gr["w_ffn_out"] = _mm(sv["a"], dp3b, ta=True, name=f"{tag}_dw_ffn_out")
    dh = _swiglu_bwd(sv["h"], da, name=f"{tag}_swiglu_bwd")
    gr["w_ffn_in"] = _mm(sv["x2b"], dh, ta=True, name=f"{tag}_dw_ffn_in")
    dx2_br = _mm(dh, w["w_ffn_in"], tb=True, name=f"{tag}_dx2")
    dp2, dp2b, dg2, db2 = _ln_bwd([dp3, dx2_br], [ALPHA, 1.0], sv["xh2"], sv["rs2"], w["ln_g"][1], name=f"{tag}_ln2_bwd")
    do = _mm(dp2b, w["w_xo"], tb=True, out_dtype=MXU_DTYPE, name=f"{tag}_d_o")
    gr["w_xo"] = _mm(sv["o"], dp2b, ta=True, name=f"{tag}_dw_xo")
    dq, dk, dv = _attn_bwd(sv["q"], sv["kv"], do, bsz=bsz, name=f"{tag}_attn_bwd")
    dkv = jnp.concatenate([dk, dv], axis=1)
    gr["w_xq"] = _mm(sv["x1b"], dq, ta=True, name=f"{tag}_dw_xq")
    gr["w_xkv"] = _mm(memn_b, dkv, ta=True, name=f"{tag}_dw_xkv")
    dmemn = _mm(dkv, w["w_xkv"], tb=True, name=f"{tag}_d_memn")
    dx1_br = _mm(dq, w["w_xq"], tb=True, name=f"{tag}_dx1")
    token = group_done(0)
    ln_g1 = w["ln_g"][0] if token is None else w["ln_g"][0] + token[0, 0]
    dp1, dp1b, dg1, db1 = _ln_bwd([dp2, dx1_br], [ALPHA, 1.0], sv["xh1"], sv["rs1"], ln_g1, name=f"{tag}_ln1_bwd")
    gr["ln_g"] = jnp.concatenate([dg1, dg2, dg3], axis=0)
    gr["ln_b"] = jnp.concatenate([db1, db2, db3], axis=0)
    dmerged = _mm(dp1b, w["w_mix_o"], tb=True, name=f"{tag}_d_merged")
    gr["w_mix_o"] = _mm(sv["merged"], dp1b, ta=True, name=f"{tag}_dw_mix_o")
    dbr_a, dbr_b, dproj = _merge_bwd(dmerged, sv["br_a"], sv["br_b"], sv["proj"], name=f"{tag}_merge_bwd")
    gr["p_a"] = _mm(sv["sgo"], dbr_a, ta=True, name=f"{tag}_dw_p_a")
    gr["p_b"] = _mm(sv["yb"], dbr_b, ta=True, name=f"{tag}_dw_p_b")
    dsgo = _mm(dbr_a, w["p_a"], tb=True, name=f"{tag}_d_sgo")
    dyb = _mm(dbr_b, w["p_b"], tb=True, name=f"{tag}_d_yb")
    token = group_done(1)
    norm_g = w["ssm_norm_g"] if token is None else w["ssm_norm_g"] + token[0, 0]
    dy, dproj, gr["ssm_norm_g"] = _gate_norm_bwd(dyb, sv["y"], sv["proj"], norm_g, dproj, name=f"{tag}_gate_norm_bwd")
    dxbc, ddr, gr["a_log"], gr["d_skip"], gr["dt_bias"] = _ssd_bwd(
        dy, sv["xbc"], sv["dt"], sv["cs"], sv["dtt"], sv["cst"], sv["st"], w["d_skipx"], w["a_log8"], sv["dt_raw"],
        w["dt_bias8"], nc=nc, name=f"{tag}_ssd_bwd")
    dproj, gr["conv_w"], gr["conv_b"] = _conv_bwd(sv["proj"], dxbc, w["conv_w"], w["conv_b"], dproj, bsz=bsz, name=f"{tag}_conv_bwd")
    dproj, gr["sg_w"], dsg_bcol, gr["sg_ln_g"], gr["sg_ln_b"] = _sg_bwd(
        sv["proj"], dsgo, w["sg_ln_g"], w["sg_ln_b"], w["sg_w"], w["sg_bcol"], dproj, name=f"{tag}_sg_bwd")
    gr["sg_b"] = dsg_bcol[..., 0]
    gr["w_main"] = _mm(sv["x_in"], dproj, ta=True, name=f"{tag}_dw_main")
    gr["w_dt"] = _mm(sv["x_in"], ddr, ta=True, name=f"{tag}_dw_dt")
    token = group_done(2)
    dx_dt = _mm(ddr, w["w_dt"], tb=True, after=token, name=f"{tag}_dx_dt")
    dx_main = _mm(dproj, w["w_main"], tb=True, after=token, name=f"{tag}_dx_main")
    return [dp1, dx_main, dx_dt], [ALPHA, 1.0, 1.0], gr, dmemn


def _local_step(x, mem, tgt, mem_ln_g, mem_ln_b, layers, on_layer_grads=None):
    bsz, s, d = x.shape
    xf = x.reshape(bsz * s, d)
    memf = mem.reshape(-1, d)
    _, memn_b, mxh, mrs = _ln_fwd(memf, None, mem_ln_g, mem_ln_b, name="mem_ln_fwd")
    cur, curb, saved, weights = xf, xf, [], []
    for li, get_weights in enumerate(layers):
        cur, curb, sv, w = _layer_fwd(cur, curb, memn_b, get_weights(cur), bsz=bsz, tag=f"l{li}")
        saved.append(sv)
        weights.append(w)
    dy, lsum = _loss_head(cur, tgt.reshape(bsz * s, d), name="loss_head")
    addends, scales = [dy], [1.0]
    grads, dmem = [None] * len(layers), []
    for li in reversed(range(len(layers))):
        on_group = None if on_layer_grads is None else functools.partial(on_layer_grads, li)
        addends, scales, grads[li], dm = _layer_bwd(addends, scales, memn_b, weights[li], saved[li], on_group, bsz=bsz, tag=f"l{li}")
        dmem.append(dm)
    grad_x = _add_scaled(addends, scales, name="grad_x").reshape(bsz, s, d)
    _, _, dmg, dmb = _ln_bwd(dmem, [1.0] * len(dmem), mxh, mrs, mem_ln_g, name="mem_ln_bwd")
    return lsum, grad_x, grads, dmg[0], dmb[0]


_ANY = pl.BlockSpec(memory_space=pl.ANY)
_MESH = pl.DeviceIdType.MESH


def _all_gather8(x, *, name):
    def body(x_ref, out_ref, send_sems, recv_sems):
        mx, my, mc = lax.axis_index("x"), lax.axis_index("y"), lax.axis_index("c")
        me, sibling = (mx, my, mc), (mx, my, 1 - mc)
        chips = [(1 - mx, my), (mx, 1 - my), (1 - mx, 1 - my)]

        def blk(px, py, pc):
            return out_ref.at[4 * px + 2 * py + pc]

        def copy(k, block, to, src=None):
            return pltpu.make_async_remote_copy(
                src_ref=blk(*block) if src is None else src, dst_ref=blk(*block), send_sem=send_sems.at[k],
                recv_sem=recv_sems.at[k], device_id=to, device_id_type=_MESH)

        first = [copy(0, me, sibling, src=x_ref)]
        first += [copy(1 + j, me, (*chip, mc), src=x_ref) for j, chip in enumerate(chips)]
        for cp in first:
            cp.start()
        passed = [copy(4 + j, (*chip, mc), sibling) for j, chip in enumerate(chips)]
        for j, chip in enumerate(chips):
            copy(1 + j, (*chip, mc), me).wait_recv()
            passed[j].start()
        copy(0, sibling, me).wait_recv()
        for j, chip in enumerate(chips):
            copy(4 + j, (*chip, 1 - mc), me).wait_recv()
        for cp in first + passed:
            cp.wait_send()

    return pl.pallas_call(
        body, out_shape=jax.ShapeDtypeStruct((N_DEV,) + x.shape, x.dtype), in_specs=[_ANY], out_specs=_ANY,
        scratch_shapes=[pltpu.SemaphoreType.DMA((7,)), pltpu.SemaphoreType.DMA((7,))], name=name)(x)


def _row_tile(rows, row_bytes, mult=SUBLANE):
    best = None
    for tr in range(mult, rows + 1, mult):
        if rows % tr == 0 and (best is None or tr * row_bytes <= BLOCK_BYTES):
            best = tr
    return rows if best is None else best


def _gather_shape(r, c, kind):
    return {"row": (2, N_CHIPS * r, c), "col": (2, r, N_CHIPS * c), "chip": (2, N_CHIPS, r, c)}[kind]


def _cast_place(shard, kind, dtype, chip_idx, *, name):
    _, r, c = shard.shape
    tr = _row_tile(r, c * 4, 16)
    nt = r // tr

    def body(_, s_ref, o_ref):
        o_ref[...] = s_ref[...].astype(dtype)

    if kind == "row":
        out_spec = pl.BlockSpec((None, tr, c), lambda l, i, j_ref: (l, j_ref[0] * nt + i, 0))
    elif kind == "col":
        out_spec = pl.BlockSpec((None, tr, c), lambda l, i, j_ref: (l, i, j_ref[0]))
    else:
        out_spec = pl.BlockSpec((None, None, tr, c), lambda l, i, j_ref: (l, j_ref[0], i, 0))
    grid_spec = pltpu.PrefetchScalarGridSpec(
        num_scalar_prefetch=1, grid=(2, nt), in_specs=[pl.BlockSpec((None, tr, c), lambda l, i, j_ref: (l, i, 0))],
        out_specs=out_spec)
    return pl.pallas_call(body, grid_spec=grid_spec, out_shape=jax.ShapeDtypeStruct(_gather_shape(r, c, kind), dtype),
                          compiler_params=_params("parallel", "parallel"), name=name)(chip_idx, shard)


def _gather_params(bufs, shard_shapes, kinds, *, name):
    n = len(bufs)

    def body(*refs):
        outs = refs[n:2 * n]
        send_sems, recv_sems = refs[2 * n:]
        mx, my, mc = lax.axis_index("x"), lax.axis_index("y"), lax.axis_index("c")
        me, sibling = (mx, my, mc), (mx, my, 1 - mc)
        chips = [(1 - mx, my), (mx, 1 - my), (1 - mx, 1 - my)]

        def blk(i, px, py, pc):
            r, c = shard_shapes[i]
            j = 2 * px + py
            if kinds[i] == "row":
                return outs[i].at[pc, pl.ds(pl.multiple_of(j * r, r), r)]
            if kinds[i] == "col":
                return outs[i].at[pc, :, pl.ds(pl.multiple_of(j * c, c), c)]
            return outs[i].at[pc, j]

        def copy(i, k, block, to):
            return pltpu.make_async_remote_copy(
                src_ref=blk(i, *block), dst_ref=blk(i, *block), send_sem=send_sems.at[6 * i + k],
                recv_sem=recv_sems.at[6 * i + k], device_id=to, device_id_type=_MESH)

        sent = []
        for i in range(n):
            for j, chip in enumerate(chips):
                cp = copy(i, j, me, (*chip, mc))
                cp.start()
                sent.append(cp)
        for j, chip in enumerate(chips):
            for i in range(n):
                copy(i, j, (*chip, mc), me).wait_recv()
                fwd = copy(i, 3 + j, (*chip, mc), sibling)
                fwd.start()
                sent.append(fwd)
        for i in range(n):
            for j, chip in enumerate(chips):
                copy(i, 3 + j, (*chip, 1 - mc), me).wait_recv()
        for cp in sent:
            cp.wait_send()

    return pl.pallas_call(
        body, out_shape=[jax.ShapeDtypeStruct(b.shape, b.dtype) for b in bufs], in_specs=[_ANY] * n, out_specs=[_ANY] * n,
        input_output_aliases={i: i for i in range(n)},
        scratch_shapes=[pltpu.SemaphoreType.DMA((6 * n,)), pltpu.SemaphoreType.DMA((6 * n,))], name=name)(*bufs)


def _half(r, h):
    return pl.ds(pl.multiple_of(h * (r // 2), r // 2), r // 2)


_HBM = pl.BlockSpec(memory_space=pltpu.HBM)
_SEM = pl.BlockSpec(memory_space=pltpu.SEMAPHORE)
_EFFECT = pltpu.SideEffectType.DATAFLOW_SIDE_EFFECTING


def _sibling_copies(g_refs, land_refs, gs, views, send_sems, recv_sems):
    mx, my, mc = lax.axis_index("x"), lax.axis_index("y"), lax.axis_index("c")
    copies = []
    for i in range(len(gs)):
        if views[i] == "chip":
            src = g_refs[i].at[:, _half(gs[i].shape[1], 1 - mc)]
        else:
            src = g_refs[i].at[_half(gs[i].shape[0], 1 - mc)]
        copies.append(pltpu.make_async_remote_copy(src_ref=src, dst_ref=land_refs[i], send_sem=send_sems.at[i], recv_sem=recv_sems.at[i],
                                                   device_id=(mx, my, 1 - mc), device_id_type=_MESH))
    return copies


def _half_shape(g, view):
    return (g.shape[0], g.shape[1] // 2, g.shape[2]) if view == "chip" else (g.shape[0] // 2, g.shape[1])


def _grads_to_sibling_start(gs, views, *, name):
    n = len(gs)
    lands = [pltpu.with_memory_space_constraint(lax.empty(_half_shape(g, v), g.dtype), pltpu.HBM) for g, v in zip(gs, views)]

    def body(*refs):
        for cp in _sibling_copies(refs[:n], refs[n:2 * n], gs, views, refs[2 * n], refs[2 * n + 1]):
            cp.start()
        refs[-1][...] = jnp.zeros_like(refs[-1])

    outs = pl.pallas_call(
        body, name=name,
        out_shape=(pltpu.SemaphoreType.DMA((n,)), pltpu.SemaphoreType.DMA((n,)),
                   *[pltpu.HBM(x.shape, x.dtype) for x in list(gs) + lands], jax.ShapeDtypeStruct((SUBLANE, LANE), F32)),
        in_specs=[_HBM] * (2 * n), out_specs=(_SEM, _SEM, *[_HBM] * (2 * n), pl.BlockSpec(memory_space=pltpu.VMEM)),
        input_output_aliases={i: 2 + i for i in range(2 * n)},
        compiler_params=pltpu.CompilerParams(has_side_effects=_EFFECT),
    )(*[pltpu.with_memory_space_constraint(g, pltpu.HBM) for g in gs], *lands)
    return outs[0], outs[1], list(outs[2:2 + n]), list(outs[2 + n:2 + 2 * n]), outs[-1]


def _grads_to_sibling_wait(send_sems, recv_sems, gs, lands, views, after, *, name):
    n = len(gs)

    def body(*refs):
        for cp in _sibling_copies(refs[:n], refs[n:2 * n], gs, views, refs[2 * n], refs[2 * n + 1]):
            cp.wait_send()
            cp.wait_recv()

    outs = pl.pallas_call(
        body, name=name, out_shape=tuple(pltpu.HBM(x.shape, x.dtype) for x in list(gs) + list(lands)),
        in_specs=[_HBM] * (2 * n) + [_SEM, _SEM, _ANY], out_specs=tuple([_HBM] * (2 * n)),
        input_output_aliases={i: i for i in range(2 * n)},
        compiler_params=pltpu.CompilerParams(has_side_effects=_EFFECT),
    )(*gs, *lands, send_sems, recv_sems, after)
    return list(outs[:n]), list(outs[n:])


def _cast_place_layer(shard, l, kind, chip_idx, after, *, name):
    _, r, c = shard.shape
    tr = _row_tile(r, c * 4, 16)
    nt = r // tr

    def body(_, s_ref, *rest):
        rest[-1][...] = s_ref[...].astype(MXU_DTYPE)

    if kind == "row":
        out_spec = pl.BlockSpec((tr, c), lambda i, j_ref: (j_ref[0] * nt + i, 0))
    elif kind == "col":
        out_spec = pl.BlockSpec((tr, c), lambda i, j_ref: (i, j_ref[0]))
    else:
        out_spec = pl.BlockSpec((None, tr, c), lambda i, j_ref: (j_ref[0], i, 0))
    extra = [] if after is None else [after]
    grid_spec = pltpu.PrefetchScalarGridSpec(
        num_scalar_prefetch=1, grid=(nt,), in_specs=[pl.BlockSpec((None, tr, c), lambda i, j_ref: (l, i, 0))] + [_ANY] * len(extra),
        out_specs=out_spec)
    return pl.pallas_call(body, grid_spec=grid_spec, out_shape=jax.ShapeDtypeStruct(_gather_shape(r, c, kind)[1:], MXU_DTYPE),
                          compiler_params=_params("parallel"), name=name)(chip_idx, shard, *extra)


def _half_block(ref, kind, r, c, j, h):
    rows = _half(r, h)
    if kind == "row":
        return ref.at[pl.ds(pl.multiple_of(j * r + h * (r // 2), r // 2), r // 2)]
    if kind == "col":
        return ref.at[rows, pl.ds(pl.multiple_of(j * c, c), c)]
    return ref.at[j, rows]


def _gather_ici_copies(buf_refs, shapes, kinds, send_sems, recv_sems):
    mx, my, mc = lax.axis_index("x"), lax.axis_index("y"), lax.axis_index("c")
    chips = [(1 - mx, my), (mx, 1 - my), (1 - mx, 1 - my)]
    copies = []
    for i, (r, c) in enumerate(shapes):
        mine = _half_block(buf_refs[i], kinds[i], r, c, 2 * mx + my, mc)
        for k, (px, py) in enumerate(chips):
            copies.append(pltpu.make_async_remote_copy(
                src_ref=mine, dst_ref=mine, send_sem=send_sems.at[3 * i + k], recv_sem=recv_sems.at[3 * i + k],
                device_id=(px, py, mc), device_id_type=_MESH))
    return copies


def _gather_start(bufs, shapes, kinds, *, name):
    n = len(bufs)

    def body(*refs):
        send_sems, recv_sems, token = refs[n], refs[n + 1], refs[-1]
        for cp in _gather_ici_copies(refs[:n], shapes, kinds, send_sems, recv_sems):
            cp.start()
        token[...] = jnp.zeros_like(token)

    outs = pl.pallas_call(
        body, name=name,
        out_shape=(pltpu.SemaphoreType.DMA((3 * n,)), pltpu.SemaphoreType.DMA((3 * n,)),
                   *[pltpu.HBM(b.shape, b.dtype) for b in bufs], jax.ShapeDtypeStruct((SUBLANE, LANE), F32)),
        in_specs=[_HBM] * n, out_specs=(_SEM, _SEM, *[_HBM] * n, pl.BlockSpec(memory_space=pltpu.VMEM)),
        input_output_aliases={i: 2 + i for i in range(n)},
        compiler_params=pltpu.CompilerParams(has_side_effects=_EFFECT),
    )(*[pltpu.with_memory_space_constraint(b, pltpu.HBM) for b in bufs])
    return outs[0], outs[1], list(outs[2:2 + n]), outs[-1]


def _gather_wait(send_sems, recv_sems, bufs, shapes, kinds, after, *, name):
    n = len(bufs)

    def body(*refs):
        for cp in _gather_ici_copies(refs[:n], shapes, kinds, refs[n], refs[n + 1]):
            cp.wait_send()
            cp.wait_recv()

    outs = pl.pallas_call(
        body, name=name, out_shape=tuple(pltpu.HBM(b.shape, b.dtype) for b in bufs),
        in_specs=[_HBM] * n + [_SEM, _SEM, _ANY], out_specs=tuple([_HBM] * n), input_output_aliases={i: i for i in range(n)},
        compiler_params=pltpu.CompilerParams(has_side_effects=_EFFECT),
    )(*bufs, send_sems, recv_sems, after)
    return list(outs)


def _gather_forward(bufs, shapes, kinds, *, name):
    n = len(bufs)

    def body(*refs):
        outs = refs[n:2 * n]
        send_sems, recv_sems = refs[2 * n:]
        mx, my, mc = lax.axis_index("x"), lax.axis_index("y"), lax.axis_index("c")
        chips = [(1 - mx, my), (mx, 1 - my), (1 - mx, 1 - my)]
        copies = []
        for i, (r, c) in enumerate(shapes):
            for k, (px, py) in enumerate(chips):
                got = _half_block(outs[i], kinds[i], r, c, 2 * px + py, mc)
                cp = pltpu.make_async_remote_copy(src_ref=got, dst_ref=got, send_sem=send_sems.at[3 * i + k],
                                                  recv_sem=recv_sems.at[3 * i + k], device_id=(mx, my, 1 - mc), device_id_type=_MESH)
                cp.start()
                copies.append(cp)
        for cp in copies:
            cp.wait()

    return pl.pallas_call(
        body, out_shape=[jax.ShapeDtypeStruct(b.shape, b.dtype) for b in bufs], in_specs=[_ANY] * n, out_specs=[_ANY] * n,
        input_output_aliases={i: i for i in range(n)},
        scratch_shapes=[pltpu.SemaphoreType.DMA((3 * n,)), pltpu.SemaphoreType.DMA((3 * n,))], name=name)(*bufs)


def _chip_exchange_copies(pair_refs, land_refs, pairs, views, send_sems, recv_sems):
    mx, my, mc = lax.axis_index("x"), lax.axis_index("y"), lax.axis_index("c")
    me = 2 * mx + my
    chips = [(1 - mx, my), (mx, 1 - my), (1 - mx, 1 - my)]
    copies = []
    for i in range(len(pairs)):
        for k, (px, py) in enumerate(chips):
            j = 2 * px + py
            if views[i] == "chip":
                src = pair_refs[i].at[j]
            else:
                c = pairs[i].shape[1] // N_CHIPS
                src = pair_refs[i].at[:, pl.ds(pl.multiple_of(j * c, c), c)]
            copies.append(pltpu.make_async_remote_copy(
                src_ref=src, dst_ref=land_refs[i].at[me], send_sem=send_sems.at[3 * i + k], recv_sem=recv_sems.at[3 * i + k],
                device_id=(px, py, mc), device_id_type=_MESH))
    return copies


def _quad_shape(p, view):
    return p.shape if view == "chip" else (N_CHIPS, p.shape[0], p.shape[1] // N_CHIPS)


def _grads_to_chips_start(pairs, views, *, name):
    n = len(pairs)
    lands = [pltpu.with_memory_space_constraint(lax.empty(_quad_shape(p, v), p.dtype), pltpu.HBM) for p, v in zip(pairs, views)]

    def body(*refs):
        pair_refs, land_refs = refs[:n], refs[n:2 * n]
        send_sems, recv_sems = refs[2 * n], refs[2 * n + 1]
        token = refs[-1]
        for cp in _chip_exchange_copies(pair_refs, land_refs, pairs, views, send_sems, recv_sems):
            cp.start()
        token[...] = jnp.zeros_like(token)

    outs = pl.pallas_call(
        body, name=name,
        out_shape=(pltpu.SemaphoreType.DMA((3 * n,)), pltpu.SemaphoreType.DMA((3 * n,)),
                   *[pltpu.HBM(p.shape, p.dtype) for p in pairs], *[pltpu.HBM(l.shape, l.dtype) for l in lands],
                   jax.ShapeDtypeStruct((SUBLANE, LANE), F32)),
        in_specs=[_HBM] * (2 * n), out_specs=(_SEM, _SEM, *[_HBM] * (2 * n), pl.BlockSpec(memory_space=pltpu.VMEM)),
        input_output_aliases={i: 2 + i for i in range(2 * n)},
        compiler_params=pltpu.CompilerParams(has_side_effects=_EFFECT),
    )(*[pltpu.with_memory_space_constraint(p, pltpu.HBM) for p in pairs], *lands)
    return outs[0], outs[1], list(outs[2:2 + n]), list(outs[2 + n:2 + 2 * n]), outs[-1]


def _grads_to_chips_wait(send_sems, recv_sems, pairs, lands, views, after, *, name):
    n = len(pairs)

    def body(*refs):
        pair_refs, land_refs = refs[:n], refs[n:2 * n]
        s_sems, r_sems = refs[2 * n], refs[2 * n + 1]
        for cp in _chip_exchange_copies(pair_refs, land_refs, pairs, views, s_sems, r_sems):
            cp.wait_send()
            cp.wait_recv()

    outs = pl.pallas_call(
        body, name=name, out_shape=tuple(pltpu.HBM(x.shape, x.dtype) for x in list(pairs) + list(lands)),
        in_specs=[_HBM] * (2 * n) + [_SEM, _SEM, _ANY], out_specs=tuple([_HBM] * (2 * n)),
        input_output_aliases={i: i for i in range(2 * n)},
        compiler_params=pltpu.CompilerParams(has_side_effects=_EFFECT),
    )(*pairs, *lands, send_sems, recv_sems, after)
    return list(outs[n:])


def _grads_share(tots, *, name):
    n = len(tots)

    def body(*refs):
        ins, outs = refs[:n], refs[n:2 * n]
        send_sems, recv_sems = refs[2 * n:]
        mx, my, mc = lax.axis_index("x"), lax.axis_index("y"), lax.axis_index("c")
        copies = []
        for i in range(n):
            cp = pltpu.make_async_remote_copy(src_ref=ins[i], dst_ref=outs[i], send_sem=send_sems.at[i], recv_sem=recv_sems.at[i],
                                              device_id=(mx, my, 1 - mc), device_id_type=_MESH)
            cp.start()
            copies.append(cp)
        for cp in copies:
            cp.wait()

    return pl.pallas_call(
        body, out_shape=[jax.ShapeDtypeStruct(t.shape, t.dtype) for t in tots], in_specs=[_ANY] * n, out_specs=[_ANY] * n,
        scratch_shapes=[pltpu.SemaphoreType.DMA((n,)), pltpu.SemaphoreType.DMA((n,))], name=name)(*tots)


def _pair_sum(g, recv, view, c_idx, *, name):
    def body(c_ref, a_ref, b_ref, o_ref):
        o_ref[...] = (a_ref[...] + b_ref[...]).astype(WIRE_DTYPE)

    if view == "chip":
        nch, r, c = g.shape
        tr = _row_tile(r // 2, c * 4, 16)
        gv = g.reshape(nch, 2, r // 2, c)
        grid = (nch, (r // 2) // tr)
        in_specs = [pl.BlockSpec((None, None, tr, c), lambda j, i, c_ref: (j, c_ref[0], i, 0)),
                    pl.BlockSpec((None, tr, c), lambda j, i, c_ref: (j, i, 0))]
        out_spec = pl.BlockSpec((None, tr, c), lambda j, i, c_ref: (j, i, 0))
        sem = ("parallel", "parallel")
    else:
        r, c4 = g.shape
        tr = _row_tile(r // 2, c4 * 4, 16)
        gv = g.reshape(2, r // 2, c4)
        grid = ((r // 2) // tr,)
        in_specs = [pl.BlockSpec((None, tr, c4), lambda i, c_ref: (c_ref[0], i, 0)), pl.BlockSpec((tr, c4), lambda i, c_ref: (i, 0))]
        out_spec = pl.BlockSpec((tr, c4), lambda i, c_ref: (i, 0))
        sem = ("parallel",)
    grid_spec = pltpu.PrefetchScalarGridSpec(num_scalar_prefetch=1, grid=grid, in_specs=in_specs, out_specs=out_spec)
    return pl.pallas_call(body, grid_spec=grid_spec, out_shape=jax.ShapeDtypeStruct(recv.shape, WIRE_DTYPE),
                          compiler_params=_params(*sem), name=name)(c_idx, gv, recv)


def _quad_sum(gs, recvs, quads, view, chip_idx, c_idx, *, name):
    nl = len(quads)
    nch, rh, c = quads[0].shape
    tr = _row_tile(rh, c * 4, 16)

    def body(_, __, *refs):
        o_ref = refs[-1]
        per = nch + 1
        for l in range(nl):
            grp = refs[l * per:(l + 1) * per]
            acc = grp[0][...] + grp[1][...]
            for r in grp[2:]:
                acc = acc + r[...].astype(F32)
            o_ref[l] = acc

    if view == "chip":
        own = [pl.BlockSpec((None, None, tr, c), lambda i, j, h: (j[0], h[0], i, 0)),
               pl.BlockSpec((None, tr, c), lambda i, j, h: (j[0], i, 0))]
        gviews = [g.reshape(nch, 2, rh, c) for g in gs]
    else:
        own = [pl.BlockSpec((None, tr, c), lambda i, j, h: (h[0], i, j[0])), pl.BlockSpec((tr, c), lambda i, j, h: (i, j[0]))]
        gviews = [g.reshape(2, rh, nch * c) for g in gs]
    assert nch & (nch - 1) == 0
    got = [pl.BlockSpec((None, tr, c), functools.partial(lambda i, j, h, k: ((j[0] + k) & (nch - 1), i, 0), k=k))
           for k in range(1, nch)]
    ins = []
    for l in range(nl):
        ins += [gviews[l], recvs[l]] + [quads[l]] * (nch - 1)
    grid_spec = pltpu.PrefetchScalarGridSpec(
        num_scalar_prefetch=2, grid=(rh // tr,), in_specs=(own + got) * nl,
        out_specs=pl.BlockSpec((nl, tr, c), lambda i, j, h: (0, i, 0)))
    return pl.pallas_call(body, grid_spec=grid_spec, out_shape=jax.ShapeDtypeStruct((nl, rh, c), F32),
                          compiler_params=_params("parallel"), name=name)(chip_idx, c_idx, *ins)


def _sum_devices(g8, own, dev_idx, *, name):
    k, rows, cols = g8.shape

    def body(d_ref, a_ref, x_ref, o_ref):
        acc = None
        for i in range(k):
            term = jnp.where(d_ref[0] == i, x_ref[...], a_ref[i])
            acc = term if acc is None else acc + term
        o_ref[...] = acc

    grid_spec = pltpu.PrefetchScalarGridSpec(
        num_scalar_prefetch=1, grid=(1,),
        in_specs=[pl.BlockSpec((k, rows, cols), lambda i, d_ref: (0, 0, 0)), pl.BlockSpec((rows, cols), lambda i, d_ref: (0, 0))],
        out_specs=pl.BlockSpec((rows, cols), lambda i, d_ref: (0, 0)))
    return pl.pallas_call(body, grid_spec=grid_spec, out_shape=jax.ShapeDtypeStruct((rows, cols), g8.dtype),
                          compiler_params=_params("arbitrary"), name=name)(dev_idx, g8, own)


def _adamw(w, g, m, v, *, name):
    rows, cols = w.shape
    tr = rows
    for cand in (256, 128, 64, 32, 16, 8):
        if rows % cand == 0 and cand * cols <= 512 * 1024:
            tr = cand
            break
    c1 = 1.0 - ADAM_B1 ** ADAM_STEP
    c2 = 1.0 - ADAM_B2 ** ADAM_STEP

    def body(w_ref, g_ref, m_ref, v_ref, d_ref, nm_ref, nv_ref):
        gv = g_ref[...]
        nm = ADAM_B1 * m_ref[...] + (1.0 - ADAM_B1) * gv
        nv = ADAM_B2 * v_ref[...] + (1.0 - ADAM_B2) * (gv * gv)
        d_ref[...] = -ADAM_LR * ((nm / c1) / (jnp.sqrt(nv / c2) + ADAM_EPS) + ADAM_WD * w_ref[...])
        nm_ref[...] = nm
        nv_ref[...] = nv

    spec = pl.BlockSpec((tr, cols), lambda i: (i, 0))
    shp = jax.ShapeDtypeStruct((rows, cols), F32)
    return pl.pallas_call(body, grid=(rows // tr,), in_specs=[spec] * 4, out_specs=[spec] * 3, out_shape=[shp] * 3,
                          compiler_params=_params("parallel"), name=name)(w, g, m, v)


def _adamw_halves(w, m, v, mine, other, c_idx, *, name):
    nl, r, c = w.shape
    rh = r // 2
    tr = _row_tile(rh, c * 4)
    c1 = 1.0 - ADAM_B1 ** ADAM_STEP
    c2 = 1.0 - ADAM_B2 ** ADAM_STEP

    def body(c_ref, w_ref, m_ref, v_ref, a_ref, b_ref, g_ref, d_ref, nm_ref, nv_ref):
        gv = jnp.where(pl.program_id(1) == c_ref[0], a_ref[...], b_ref[...])
        nm = ADAM_B1 * m_ref[...] + (1.0 - ADAM_B1) * gv
        nv = ADAM_B2 * v_ref[...] + (1.0 - ADAM_B2) * (gv * gv)
        g_ref[...] = gv
        d_ref[...] = -ADAM_LR * ((nm / c1) / (jnp.sqrt(nv / c2) + ADAM_EPS) + ADAM_WD * w_ref[...])
        nm_ref[...] = nm
        nv_ref[...] = nv

    full = pl.BlockSpec((None, None, tr, c), lambda l, h, i, c_ref: (l, h, i, 0))
    half = pl.BlockSpec((None, tr, c), lambda l, h, i, c_ref: (l, i, 0))
    grid_spec = pltpu.PrefetchScalarGridSpec(num_scalar_prefetch=1, grid=(nl, 2, rh // tr),
                                             in_specs=[full] * 3 + [half] * 2, out_specs=[full] * 4)
    shp = jax.ShapeDtypeStruct((nl, 2, rh, c), F32)
    view = (nl, 2, rh, c)
    outs = pl.pallas_call(body, grid_spec=grid_spec, out_shape=[shp] * 4, compiler_params=_params("parallel", "parallel", "parallel"),
                          name=name)(c_idx, w.reshape(view), m.reshape(view), v.reshape(view), mine, other)
    return [o.reshape(nl, r, c) for o in outs]


WEIGHTS = ["mem_ln_g", "mem_ln_b", "w_in", "sg_ln_g", "sg_ln_b", "sg_w", "sg_b", "conv_w", "conv_b", "dt_bias", "a_log",
           "d_skip", "ssm_norm_g", "p_a", "p_b", "w_mix_o", "w_xq", "w_xkv", "w_xo", "w_ffn_in", "w_ffn_out", "ln_g", "ln_b"]
ARG_NAMES = ["x", "mem"] + WEIGHTS + ["loss_target"] + ["m_" + n for n in WEIGHTS] + ["v_" + n for n in WEIGHTS]
BIG = {"w_in": (1, (1024, 9248)), "p_a": (0, (1024, 1024)), "p_b": (0, (2048, 1024)), "w_mix_o": (0, (1024, 1024)),
       "w_xq": (0, (1024, 1024)), "w_xkv": (1, (1024, 2048)), "w_xo": (0, (1024, 1024)), "w_ffn_in": (1, (1024, 5632)),
       "w_ffn_out": (0, (2816, 1024))}
SMALL_SHARDED = {"conv_w": (4, 3072), "ln_g": (3, 1024), "ln_b": (3, 1024)}
SMALL = [n for n in WEIGHTS if n not in BIG]
W_IN_MAP = ((0, 4096, "main", 0), (4096, 7168, "main", XBC_COL0), (7168, 7200, "dt", 0), (7200, 9248, "main", GAB_COL0))
W_IN_SHARD = 9248 // N_CHIPS


def _w_in_chip_major(gm, gd):
    src = {"main": gm, "dt": gd}
    blocks = []
    for j in range(N_CHIPS):
        lo, hi = j * W_IN_SHARD, (j + 1) * W_IN_SHARD
        parts = [src[k][:, o + max(lo, a) - a:o + min(hi, b) - a] for a, b, k, o in W_IN_MAP if max(lo, a) < min(hi, b)]
        blocks.append(jnp.concatenate(parts, axis=1))
    return jnp.stack(blocks)


def _w_in_reassemble(wc):
    def cols(a, b):
        out = []
        for j in range(N_CHIPS):
            lo, hi = max(a, j * W_IN_SHARD), min(b, (j + 1) * W_IN_SHARD)
            if lo < hi:
                out.append(wc[j][:, lo - j * W_IN_SHARD:hi - j * W_IN_SHARD])
        return out

    main = sorted((m for m in W_IN_MAP if m[2] == "main"), key=lambda m: m[3])
    w_main = jnp.concatenate([p for a, b, _, _ in main for p in cols(a, b)], axis=1)
    (a, b, _, _), = [m for m in W_IN_MAP if m[2] == "dt"]
    w_dt = jnp.pad(jnp.concatenate(cols(a, b), axis=1), ((0, 0), (0, HEAD_PAD - (b - a))))
    return w_main, w_dt
GATHER_KIND = {"w_in": "chip", "p_a": "row", "p_b": "row", "w_mix_o": "row", "w_xq": "row", "w_xkv": "col", "w_xo": "row",
               "w_ffn_in": "col", "w_ffn_out": "row", "conv_w": "chip", "ln_g": "chip", "ln_b": "chip"}
GRAD_VIEW = {n: ("col" if k == "col" else "chip") for n, k in GATHER_KIND.items() if n in BIG}


def _shard_shape(name):
    axis, (r, c) = BIG[name]
    return (r // N_CHIPS, c) if axis == 0 else (r, c // N_CHIPS)


def _pad_rows(flat, cols, row_mult):
    n = flat.shape[0]
    rows = -(-n // cols)
    rows = -(-rows // row_mult) * row_mult
    return jnp.pad(flat, (0, rows * cols - n)).reshape(rows, cols)


def _gather_small_params(a, chip):
    names = list(SMALL_SHARDED)
    kinds = [GATHER_KIND[n] for n in names]
    bufs = [_cast_place(a[n], GATHER_KIND[n], F32, chip.reshape(1), name=f"place_{n}") for n in names]
    outs = _gather_params(bufs, [a[n].shape[1:] for n in names], kinds, name="gather_small_params")
    full = {}
    for n, o in zip(names, outs):
        _, _, r, c = o.shape
        full[n] = jnp.transpose(o, (0, 2, 1, 3)).reshape(DEPTH, r, N_CHIPS * c)
    return full


GATHER_GROUPS = (("w_in",), tuple(n for n in BIG if n != "w_in"))


def _gather_group_start(a, l, names, chip, after, *, tag):
    bufs = [_cast_place_layer(a[n], l, GATHER_KIND[n], chip.reshape(1), after, name=f"place_{n}_l{l}") for n in names]
    return _gather_start(bufs, [a[n].shape[1:] for n in names], [GATHER_KIND[n] for n in names], name=f"gather_start_{tag}")


def _gather_group_finish(a, names, flight, after, *, tag):
    send_sems, recv_sems, bufs, token = flight
    shapes, kinds = [a[n].shape[1:] for n in names], [GATHER_KIND[n] for n in names]
    bufs = _gather_wait(send_sems, recv_sems, bufs, shapes, kinds, token if after is None else after, name=f"gather_wait_{tag}")
    full = dict(zip(names, _gather_forward(bufs, shapes, kinds, name=f"gather_forward_{tag}")))
    if "w_in" in full:
        full["w_main"], full["w_dt"] = _w_in_reassemble(full.pop("w_in"))
    return full


def _layer_weights(a, big, small, l):
    w = dict(big)
    for n in SMALL_SHARDED:
        w[n] = small[n][l]
    for n in ["sg_ln_g", "sg_ln_b", "sg_w", "conv_b", "ssm_norm_g"]:
        w[n] = a[n][l]
    w["sg_bcol"] = a["sg_b"][l][..., None]
    for n in ["dt_bias", "a_log"]:
        w[n + "8"] = _pad_heads(a[n][l])
    w["d_skipx"] = _expand_heads(a["d_skip"][l])
    return w


def _grad_views(grads, names):
    gs = []
    for n in names:
        axis, _ = BIG[n]
        r, c = _shard_shape(n)
        if n == "w_in":
            gs.append(_w_in_chip_major(grads["w_main"], grads["w_dt"]))
        elif axis == 0:
            gs.append(grads[n].reshape(N_CHIPS, r, c))
        else:
            gs.append(grads[n])
    return gs


class _GradExchange:
    def __init__(self, grads, names, c_idx, tag):
        self.names, self.c_idx, self.tag = names, c_idx, tag
        self.views = [GRAD_VIEW[n] for n in names]
        self.gs = _grad_views(grads, names)

    def start(self):
        self.sems = _grads_to_sibling_start(self.gs, self.views, name=f"grads_to_sibling_start_{self.tag}")
        return self.sems[4]

    def cross(self, after):
        send_sems, recv_sems, gs, lands, token = self.sems
        self.gs, self.recv = _grads_to_sibling_wait(send_sems, recv_sems, gs, lands, self.views, token if after is None else after,
                                                    name=f"grads_to_sibling_wait_{self.tag}")
        cpre = self.c_idx.reshape(1)
        pairs = [_pair_sum(g, rv, v, cpre, name=f"grads_pair_sum_{n}_{self.tag}")
                 for g, rv, v, n in zip(self.gs, self.recv, self.views, self.names)]
        self.sems = _grads_to_chips_start(pairs, self.views, name=f"grads_to_chips_start_{self.tag}")
        return self.sems[4]

    def finish(self, after):
        send_sems, recv_sems, pairs, lands, _ = self.sems
        quads = _grads_to_chips_wait(send_sems, recv_sems, pairs, lands, self.views, after, name=f"grads_to_chips_wait_{self.tag}")
        return {n: (g, rv, q) for n, g, rv, q in zip(self.names, self.gs, self.recv, quads)}


def _finish_big_grads(parts, c_idx, chip):
    tots = [_quad_sum([parts[l][n][0] for l in range(DEPTH)], [parts[l][n][1] for l in range(DEPTH)],
                      [parts[l][n][2] for l in range(DEPTH)], GRAD_VIEW[n], chip.reshape(1), c_idx.reshape(1),
                      name=f"grads_chip_sum_{n}") for n in BIG]
    others = _grads_share(tots, name="grads_share")
    return {n: (t, o) for n, t, o in zip(BIG, tots, others)}


def _reduce_small_grads(small, chip, c_idx):
    names = list(small)
    flat = jnp.concatenate([small[n].reshape(-1) for n in names])
    packed = _pad_rows(flat, LANE, SUBLANE)
    g8 = _all_gather8(packed, name="gather_small_grads")
    tot = _sum_devices(g8, packed, (2 * chip + c_idx).reshape(1), name="small_grads_sum").reshape(-1)
    out, off = {}, 0
    for n in names:
        sz = small[n].size
        full = tot[off:off + sz].reshape(small[n].shape)
        off += sz
        if n in SMALL_SHARDED:
            cs = SMALL_SHARDED[n][1] // N_CHIPS
            full = lax.dynamic_slice_in_dim(full, chip * cs, cs, axis=-1)
        out[n] = full
    return out


def kernel(x, mem, mem_ln_g, mem_ln_b, w_in, sg_ln_g, sg_ln_b, sg_w, sg_b, conv_w, conv_b, dt_bias, a_log, d_skip, ssm_norm_g, p_a, p_b, w_mix_o, w_xq, w_xkv, w_xo, w_ffn_in, w_ffn_out, ln_g, ln_b, loss_target, m_mem_ln_g, m_mem_ln_b, m_w_in, m_sg_ln_g, m_sg_ln_b, m_sg_w, m_sg_b, m_conv_w, m_conv_b, m_dt_bias, m_a_log, m_d_skip, m_ssm_norm_g, m_p_a, m_p_b, m_w_mix_o, m_w_xq, m_w_xkv, m_w_xo, m_w_ffn_in, m_w_ffn_out, m_ln_g, m_ln_b, v_mem_ln_g, v_mem_ln_b, v_w_in, v_sg_ln_g, v_sg_ln_b, v_sg_w, v_sg_b, v_conv_w, v_conv_b, v_dt_bias, v_a_log, v_d_skip, v_ssm_norm_g, v_p_a, v_p_b, v_w_mix_o, v_w_xq, v_w_xkv, v_w_xo, v_w_ffn_in, v_w_ffn_out, v_ln_g, v_ln_b):
    a = dict(zip(ARG_NAMES, (x, mem, mem_ln_g, mem_ln_b, w_in, sg_ln_g, sg_ln_b, sg_w, sg_b, conv_w, conv_b, dt_bias, a_log, d_skip, ssm_norm_g, p_a, p_b, w_mix_o, w_xq, w_xkv, w_xo, w_ffn_in, w_ffn_out, ln_g, ln_b, loss_target, m_mem_ln_g, m_mem_ln_b, m_w_in, m_sg_ln_g, m_sg_ln_b, m_sg_w, m_sg_b, m_conv_w, m_conv_b, m_dt_bias, m_a_log, m_d_skip, m_ssm_norm_g, m_p_a, m_p_b, m_w_mix_o, m_w_xq, m_w_xkv, m_w_xo, m_w_ffn_in, m_w_ffn_out, m_ln_g, m_ln_b, v_mem_ln_g, v_mem_ln_b, v_w_in, v_sg_ln_g, v_sg_ln_b, v_sg_w, v_sg_b, v_conv_w, v_conv_b, v_dt_bias, v_a_log, v_d_skip, v_ssm_norm_g, v_p_a, v_p_b, v_w_mix_o, v_w_xq, v_w_xkv, v_w_xo, v_w_ffn_in, v_w_ffn_out, v_ln_g, v_ln_b)))
    c_idx = lax.axis_index("c").astype(jnp.int32)
    chip = (2 * lax.axis_index("x") + lax.axis_index("y")).astype(jnp.int32)

    small = _gather_small_params(a, chip)
    ga, gb = GATHER_GROUPS
    flights = {(0, 0): _gather_group_start(a, 0, ga, chip, None, tag="l0_a")}
    flights[0, 1] = _gather_group_start(a, 0, gb, chip, flights[0, 0][3], tag="l0_b")

    def layer_weights(after, l):
        first = _gather_group_finish(a, ga, flights[l, 0], after if l else None, tag=f"l{l}_a")

        def rest(w, after_b):
            more = _gather_group_finish(a, gb, flights[l, 1], after_b, tag=f"l{l}_b")
            if l + 1 < DEPTH:
                flights[l + 1, 0] = _gather_group_start(a, l + 1, ga, chip, more["p_a"], tag=f"l{l + 1}_a")
                flights[l + 1, 1] = _gather_group_start(a, l + 1, gb, chip, flights[l + 1, 0][3], tag=f"l{l + 1}_b")
                more["p_a"] = more["p_a"] + flights[l + 1, 1][3][0, 0].astype(MXU_DTYPE)
            return {k: v for k, v in {**w, **more}.items() if k != "rest"}

        return dict(_layer_weights(a, first, small, l), rest=rest)

    layers = [functools.partial(layer_weights, l=l) for l in range(DEPTH)]
    exchanges = []

    def start_exchange(l, names, grads_l):
        ex = _GradExchange(grads_l, names, c_idx, f"l{l}_{names[0]}")
        tokens = [ex.start()]
        if exchanges:
            tokens.append(exchanges[-1][1].cross(tokens[0]))
        exchanges.append((l, ex))
        if l == 0 and names == GRAD_GROUPS[-1]:
            tokens.append(ex.cross(None))
        return sum(tokens[1:], tokens[0])

    lsum, grad_x, grads, d_mem_g, d_mem_b = _local_step(x, mem, loss_target, mem_ln_g, mem_ln_b, layers, start_exchange)
    loss = lax.psum(0.5 * jnp.sum(lsum) / D_MODEL, ("x", "y", "c"))

    parts = [{} for _ in range(DEPTH)]
    for l, ex in exchanges:
        parts[l].update(ex.finish(grad_x))
    halves = _finish_big_grads(parts, c_idx, chip)
    gw = {}
    small = {"mem_ln_g": d_mem_g, "mem_ln_b": d_mem_b}
    for n in SMALL:
        if n in small:
            continue
        per_layer = []
        for l in range(DEPTH):
            g = grads[l][n]
            if n in ("dt_bias", "a_log", "d_skip"):
                g = g[0, :SSM_HEADS]
            per_layer.append(g.reshape(a[n].shape[1:-1] + (-1,)))
        small[n] = jnp.stack(per_layer)
    gw.update(_reduce_small_grads(small, chip, c_idx))

    delta, new_m, new_v = {}, {}, {}
    for n in BIG:
        mine, other = halves[n]
        gw[n], delta[n], new_m[n], new_v[n] = _adamw_halves(a[n], a["m_" + n], a["v_" + n], mine, other, c_idx.reshape(1),
                                                             name=f"adamw_{n}")
    packs = [_pad_rows(jnp.concatenate([src(n).reshape(-1) for n in SMALL]), LANE, SUBLANE)
             for src in (lambda n: a[n], lambda n: gw[n], lambda n: a["m_" + n], lambda n: a["v_" + n])]
    outs = _adamw(*packs, name="adamw_small")
    off = 0
    for n in SMALL:
        sz, shp = a[n].size, a[n].shape
        delta[n], new_m[n], new_v[n] = (o.reshape(-1)[off:off + sz].reshape(shp) for o in outs)
        off += sz
    return (loss, grad_x, *[gw[n].reshape(a[n].shape) for n in WEIGHTS], *[delta[n] for n in WEIGHTS],
            *[new_m[n] for n in WEIGHTS], *[new_v[n] for n in WEIGHTS])
```

```python
import functools
import math

import jax
import jax.numpy as jnp
from jax import lax
from jax.experimental import pallas as pl
from jax.experimental.pallas import tpu as pltpu

F32 = jnp.float32
MXU_DTYPE = jnp.bfloat16
WIRE_DTYPE = jnp.bfloat16

D_MODEL = 1024
DEPTH = 2
CHUNK = 128
SG_GROUPS = 8
SSM_INNER = 2048
SSM_HEADDIM = 64
SSM_HEADS = 32
SSM_STATE = 128
SSM_GROUPS = 4
SSM_CONV = 4
SSM_CONV_DIM = 3072
X_HEADS = 4
X_HEADDIM = 256
FFN_HIDDEN = 2816
ALPHA = float((2 * DEPTH) ** 0.25)
LN_EPS = 1e-5
RMS_EPS = 1e-5
ADAM_LR = 0.001
ADAM_B1 = 0.9
ADAM_B2 = 0.999
ADAM_EPS = 1e-08
ADAM_WD = 0.01
ADAM_STEP = 10

MAIN_COLS = 9216
UVZ_COLS = 4096
GAB_COL0 = 4096
XBC_COL0 = 6144
HEAD_PAD = 128

VMEM_LIMIT = 56 * 1024 * 1024
BLOCK_BYTES = 2 * 1024 * 1024
ROW_TILES = (512, 256, 128)
LANE = 128
SUBLANE = 8

N_CHIPS = 4
N_DEV = 8


def _pick(n, cands):
    for c in cands:
        if n % c == 0:
            return c
    return n


MM_TILE_MAX = 1408
MM_OPERAND_BYTES = 8 * 1024 * 1024


def _div_tile(n, limit):
    best = None
    for t in range(LANE, min(n, limit) + 1, LANE):
        if n % t == 0:
            best = t
    return n if best is None else best


def _params(*sem):
    return pltpu.CompilerParams(dimension_semantics=tuple(sem), vmem_limit_bytes=VMEM_LIMIT)


_ANY = pl.BlockSpec(memory_space=pl.ANY)
_MESH = pl.DeviceIdType.MESH


def _nt(a, b):
    return lax.dot_general(a, b, (((1,), (1,)), ((), ())), preferred_element_type=F32)


def _tn(a, b):
    return lax.dot_general(a, b, (((0,), (0,)), ((), ())), preferred_element_type=F32)


def _nn(a, b):
    return jnp.dot(a, b, preferred_element_type=F32)


def _sigmoid(x):
    return 0.5 * jnp.tanh(0.5 * x) + 0.5


def _split3(v):
    def top(x):
        bits = lax.bitcast_convert_type(x, jnp.uint32) & jnp.uint32(0xFFFF0000)
        return lax.bitcast_convert_type(bits, F32)

    v1 = top(v)
    r1 = v - v1
    v2 = top(r1)
    v3 = r1 - v2
    return v1.astype(jnp.bfloat16), v2.astype(jnp.bfloat16), v3.astype(jnp.bfloat16)


def _dot_exact(a, b, dn, data):
    if data == 0:
        mat = b.astype(jnp.bfloat16)
        return sum(lax.dot_general(p, mat, dn, preferred_element_type=F32) for p in _split3(a))
    mat = a.astype(jnp.bfloat16)
    return sum(lax.dot_general(mat, p, dn, preferred_element_type=F32) for p in _split3(b))


_DN_NN = (((1,), (0,)), ((), ()))
_DN_TN = (((0,), (0,)), ((), ()))


def _gelu(x):
    return 0.5 * x * (1.0 + lax.erf(x * (2.0 ** -0.5)))


def _gelu_grad(x):
    return 0.5 * (1.0 + lax.erf(x * (2.0 ** -0.5))) + x * jnp.exp(-0.5 * x * x) * (1.0 / math.sqrt(2.0 * math.pi))


def _mm(a, b, *, ta=False, tb=False, out_dtype=F32, after=None, name):
    if ta:
        kdim, m = a.shape
    else:
        m, kdim = a.shape
    if tb:
        n, k2 = b.shape[-2:]
    else:
        k2, n = b.shape[-2:]
    assert kdim == k2, (a.shape, b.shape, ta, tb)
    tm = _div_tile(m, MM_TILE_MAX)
    tn = _div_tile(n, MM_TILE_MAX)
    tk = _div_tile(kdim, MM_OPERAND_BYTES // (tm * a.dtype.itemsize + tn * b.dtype.itemsize))
    nk = kdim // tk
    dn = (((0 if ta else 1,), (1 if tb else 0,)), ((), ()))

    extra = [] if after is None else [after]

    def body(a_ref, b_ref, *rest):
        o_ref = rest[len(extra)]
        d = lax.dot_general(a_ref[...].astype(MXU_DTYPE), b_ref[...].astype(MXU_DTYPE), dn, preferred_element_type=F32)
        if nk == 1:
            o_ref[...] = d.astype(out_dtype)
            return
        acc_ref = rest[len(extra) + 1]
        k = pl.program_id(2)

        @pl.when(k == 0)
        def _():
            acc_ref[...] = d

        @pl.when(jnp.logical_and(k > 0, k < nk - 1))
        def _():
            acc_ref[...] += d

        @pl.when(k == nk - 1)
        def _():
            o_ref[...] = (acc_ref[...] + d).astype(out_dtype)

    a_spec = pl.BlockSpec((tk, tm), lambda i, j, k: (k, i)) if ta else pl.BlockSpec((tm, tk), lambda i, j, k: (i, k))
    b_spec = pl.BlockSpec((tn, tk), lambda i, j, k: (j, k)) if tb else pl.BlockSpec((tk, tn), lambda i, j, k: (k, j))
    return pl.pallas_call(
        body, grid=(m // tm, n // tn, nk), in_specs=[a_spec, b_spec] + [_ANY] * len(extra),
        out_specs=pl.BlockSpec((tm, tn), lambda i, j, k: (i, j)),
        out_shape=jax.ShapeDtypeStruct((m, n), out_dtype),
        scratch_shapes=[pltpu.VMEM((tm, tn), F32)] if nk > 1 else [],
        compiler_params=_params("parallel", "parallel", "arbitrary"), name=name)(a, b, *extra)


def _row_spec(tm, c, col=0):
    return pl.BlockSpec((tm, c), lambda i: (i, col))


def _par_spec(shape):
    nd = len(shape)
    return pl.BlockSpec(shape, lambda i: (0,) * nd)


def _ln_fwd(x, f, g, b, *, name):
    t, c = x.shape
    tm = _pick(t, ROW_TILES)
    has_f = f is not None

    def body(*refs):
        if has_f:
            x_ref, f_ref, g_ref, b_ref, y_ref, yb_ref, xh_ref, rs_ref = refs
            r = ALPHA * x_ref[...] + f_ref[...]
        else:
            x_ref, g_ref, b_ref, y_ref, yb_ref, xh_ref, rs_ref = refs
            r = x_ref[...]
        mu = jnp.mean(r, axis=-1, keepdims=True)
        xc = r - mu
        var = jnp.mean(xc * xc, axis=-1, keepdims=True)
        rstd = lax.rsqrt(var + LN_EPS)
        xh = xc * rstd
        y = xh * g_ref[...] + b_ref[...]
        y_ref[...] = y
        yb_ref[...] = y.astype(MXU_DTYPE)
        xh_ref[...] = xh
        rs_ref[...] = jnp.broadcast_to(rstd, rs_ref.shape)

    ins = [x] + ([f] if has_f else []) + [g.reshape(1, c), b.reshape(1, c)]
    in_specs = [_row_spec(tm, c)] * (2 if has_f else 1) + [_par_spec((1, c))] * 2
    return pl.pallas_call(
        body, grid=(t // tm,), in_specs=in_specs,
        out_specs=[_row_spec(tm, c), _row_spec(tm, c), _row_spec(tm, c), _row_spec(tm, LANE)],
        out_shape=[jax.ShapeDtypeStruct((t, c), F32), jax.ShapeDtypeStruct((t, c), MXU_DTYPE),
                   jax.ShapeDtypeStruct((t, c), F32), jax.ShapeDtypeStruct((t, LANE), F32)],
        compiler_params=_params("parallel"), name=name)(*ins)


def _ln_bwd(addends, scales, xh, rs, g, *, name):
    t, c = xh.shape
    tm = _pick(t, ROW_TILES)
    na = len(addends)

    def body(*refs):
        a_refs = refs[:na]
        xh_ref, rs_ref, g_ref, dp_ref, dpb_ref, dg_ref, db_ref = refs[na:]

        @pl.when(pl.program_id(0) == 0)
        def _():
            dg_ref[...] = jnp.zeros_like(dg_ref)
            db_ref[...] = jnp.zeros_like(db_ref)

        dy = None
        for s, r in zip(scales, a_refs):
            term = r[...] if s == 1.0 else s * r[...]
            dy = term if dy is None else dy + term
        xhv = xh_ref[...]
        dxh = dy * g_ref[...]
        m1 = jnp.mean(dxh, axis=-1, keepdims=True)
        m2 = jnp.mean(dxh * xhv, axis=-1, keepdims=True)
        dp = rs_ref[:, 0:1] * (dxh - m1 - xhv * m2)
        dp_ref[...] = dp
        dpb_ref[...] = dp.astype(MXU_DTYPE)
        dg_ref[...] += jnp.sum(dy * xhv, axis=0, keepdims=True)
        db_ref[...] += jnp.sum(dy, axis=0, keepdims=True)

    in_specs = [_row_spec(tm, c)] * (na + 1) + [_row_spec(tm, LANE), _par_spec((1, c))]
    return pl.pallas_call(
        body, grid=(t // tm,), in_specs=in_specs,
        out_specs=[_row_spec(tm, c), _row_spec(tm, c), _par_spec((1, c)), _par_spec((1, c))],
        out_shape=[jax.ShapeDtypeStruct((t, c), F32), jax.ShapeDtypeStruct((t, c), MXU_DTYPE),
                   jax.ShapeDtypeStruct((1, c), F32), jax.ShapeDtypeStruct((1, c), F32)],
        compiler_params=_params("arbitrary"), name=name)(*addends, xh, rs, g.reshape(1, c))


def _add_scaled(addends, scales, *, name):
    t, c = addends[0].shape
    tm = _pick(t, ROW_TILES)
    na = len(addends)

    def body(*refs):
        acc = None
        for s, r in zip(scales, refs[:na]):
            term = r[...] if s == 1.0 else s * r[...]
            acc = term if acc is None else acc + term
        refs[na][...] = acc

    return pl.pallas_call(
        body, grid=(t // tm,), in_specs=[_row_spec(tm, c)] * na, out_specs=_row_spec(tm, c),
        out_shape=jax.ShapeDtypeStruct((t, c), F32), compiler_params=_params("parallel"), name=name)(*addends)


def _loss_head(y, tgt, *, name):
    t, c = y.shape
    tm = _pick(t, ROW_TILES)

    def body(y_ref, t_ref, dy_ref, ls_ref):
        @pl.when(pl.program_id(0) == 0)
        def _():
            ls_ref[...] = jnp.zeros_like(ls_ref)

        e = y_ref[...] - t_ref[...]
        dy_ref[...] = e * (1.0 / c)
        ls_ref[...] += jnp.sum(e * e, axis=0, keepdims=True)

    return pl.pallas_call(
        body, grid=(t // tm,), in_specs=[_row_spec(tm, c)] * 2,
        out_specs=[_row_spec(tm, c), _par_spec((1, c))],
        out_shape=[jax.ShapeDtypeStruct((t, c), F32), jax.ShapeDtypeStruct((1, c), F32)],
        compiler_params=_params("arbitrary"), name=name)(y, tgt)


def _swiglu_fwd(h, *, name):
    t, two_f = h.shape
    fh = two_f // 2
    tm = _pick(t, (256, 128))

    def body(g_ref, u_ref, a_ref):
        g = g_ref[...]
        a_ref[...] = (g * _sigmoid(g) * u_ref[...]).astype(MXU_DTYPE)

    return pl.pallas_call(
        body, grid=(t // tm,), in_specs=[_row_spec(tm, fh, 0), _row_spec(tm, fh, 1)], out_specs=_row_spec(tm, fh),
        out_shape=jax.ShapeDtypeStruct((t, fh), MXU_DTYPE), compiler_params=_params("parallel"), name=name)(h, h)


def _swiglu_bwd(h, da, *, name):
    t, two_f = h.shape
    fh = two_f // 2
    tm = _pick(t, (256, 128))

    def body(g_ref, u_ref, da_ref, dh_ref):
        g = g_ref[...]
        s = _sigmoid(g)
        dav = da_ref[...]
        dh_ref[:, :fh] = (dav * u_ref[...] * (s * (1.0 + g * (1.0 - s)))).astype(MXU_DTYPE)
        dh_ref[:, fh:] = (dav * g * s).astype(MXU_DTYPE)

    return pl.pallas_call(
        body, grid=(t // tm,), in_specs=[_row_spec(tm, fh, 0), _row_spec(tm, fh, 1), _row_spec(tm, fh)],
        out_specs=_row_spec(tm, two_f), out_shape=jax.ShapeDtypeStruct((t, two_f), MXU_DTYPE),
        compiler_params=_params("parallel"), name=name)(h, h, da)


def _attn_probs(q, k):
    s = _nt(q, k) * (X_HEADDIM ** -0.5)
    s = s - jnp.max(s, axis=-1, keepdims=True)
    p = jnp.exp(s)
    return p / jnp.sum(p, axis=-1, keepdims=True)


def _attn_fwd(q, kv, *, bsz, name):
    t = q.shape[0]
    s = t // bsz
    ml = kv.shape[0] // bsz
    hd = X_HEADDIM

    def body(q_ref, k_ref, v_ref, o_ref):
        p = _attn_probs(q_ref[...], k_ref[...])
        o_ref[...] = _nn(p.astype(MXU_DTYPE), v_ref[...]).astype(MXU_DTYPE)

    return pl.pallas_call(
        body, grid=(bsz, X_HEADS),
        in_specs=[pl.BlockSpec((s, hd), lambda b, h: (b, h)), pl.BlockSpec((ml, hd), lambda b, h: (b, h)),
                  pl.BlockSpec((ml, hd), lambda b, h: (b, X_HEADS + h))],
        out_specs=pl.BlockSpec((s, hd), lambda b, h: (b, h)),
        out_shape=jax.ShapeDtypeStruct((t, D_MODEL), MXU_DTYPE),
        compiler_params=_params("parallel", "parallel"), name=name)(q, kv, kv)


def _attn_bwd(q, kv, do, *, bsz, name):
    t = q.shape[0]
    s = t // bsz
    ml = kv.shape[0] // bsz
    hd = X_HEADDIM

    def body(q_ref, k_ref, v_ref, do_ref, dq_ref, dk_ref, dv_ref):
        qv, kk, vv, dov = q_ref[...], k_ref[...], v_ref[...], do_ref[...]
        p = _attn_probs(qv, kk)
        dp = _nt(dov, vv)
        dv_ref[...] = _tn(p.astype(MXU_DTYPE), dov).astype(MXU_DTYPE)
        ds = (p * (dp - jnp.sum(dp * p, axis=-1, keepdims=True)) * (X_HEADDIM ** -0.5)).astype(MXU_DTYPE)
        dq_ref[...] = _nn(ds, kk).astype(MXU_DTYPE)
        dk_ref[...] = _tn(ds, qv).astype(MXU_DTYPE)

    blk_q = pl.BlockSpec((s, hd), lambda b, h: (b, h))
    blk_m = pl.BlockSpec((ml, hd), lambda b, h: (b, h))
    return pl.pallas_call(
        body, grid=(bsz, X_HEADS),
        in_specs=[blk_q, blk_m, pl.BlockSpec((ml, hd), lambda b, h: (b, X_HEADS + h)), blk_q],
        out_specs=[blk_q, blk_m, blk_m],
        out_shape=[jax.ShapeDtypeStruct((t, D_MODEL), MXU_DTYPE), jax.ShapeDtypeStruct((bsz * ml, D_MODEL), MXU_DTYPE),
                   jax.ShapeDtypeStruct((bsz * ml, D_MODEL), MXU_DTYPE)],
        compiler_params=_params("parallel", "parallel"), name=name)(q, kv, kv, do)


def _causal(n):
    row = lax.broadcasted_iota(jnp.int32, (n, n), 0)
    col = lax.broadcasted_iota(jnp.int32, (n, n), 1)
    return row >= col


def _sg_norm(v, g, b):
    gv = _gelu(v)
    mu = jnp.mean(gv, axis=-1, keepdims=True)
    xc = gv - mu
    var = jnp.mean(xc * xc, axis=-1, keepdims=True)
    rstd = lax.rsqrt(var + LN_EPS)
    xh = xc * rstd
    return xh, rstd, xh * g + b


def _sg_fwd(proj, ln_g, ln_b, w, bcol, *, name):
    t = proj.shape[0]
    c = D_MODEL
    gd = c // SG_GROUPS

    def body(u_ref, v_ref, g_ref, b_ref, w_ref, bc_ref, o_ref):
        gu = _gelu(u_ref[...])
        _, _, vn = _sg_norm(v_ref[...], g_ref[...], b_ref[...])
        mask = _causal(CHUNK)
        for g in range(SG_GROUPS):
            sl = slice(g * gd, (g + 1) * gd)
            wg = jnp.where(mask, w_ref[g], 0.0).astype(MXU_DTYPE)
            mixed = _nn(wg, vn[:, sl].astype(MXU_DTYPE)) + bc_ref[g]
            o_ref[:, sl] = (gu[:, sl] * mixed).astype(MXU_DTYPE)

    return pl.pallas_call(
        body, grid=(t // CHUNK,),
        in_specs=[_row_spec(CHUNK, c, 0), _row_spec(CHUNK, c, 1), _par_spec((1, c)), _par_spec((1, c)),
                  _par_spec((SG_GROUPS, CHUNK, CHUNK)), _par_spec((SG_GROUPS, CHUNK, 1))],
        out_specs=_row_spec(CHUNK, c), out_shape=jax.ShapeDtypeStruct((t, c), MXU_DTYPE),
        compiler_params=_params("parallel"), name=name)(proj, proj, ln_g.reshape(1, c), ln_b.reshape(1, c), w, bcol)


def _sg_bwd(proj, dsgo, ln_g, ln_b, w, bcol, dproj, *, name):
    t = proj.shape[0]
    c = D_MODEL
    gd = c // SG_GROUPS

    def body(u_ref, v_ref, d_ref, g_ref, b_ref, w_ref, bc_ref, _, duv_ref, dw_ref, dbc_ref, dg_ref, db_ref, dvn_ref):
        @pl.when(pl.program_id(0) == 0)
        def _():
            dw_ref[...] = jnp.zeros_like(dw_ref)
            dbc_ref[...] = jnp.zeros_like(dbc_ref)
            dg_ref[...] = jnp.zeros_like(dg_ref)
            db_ref[...] = jnp.zeros_like(db_ref)

        u = u_ref[...]
        v = v_ref[...]
        dso = d_ref[...]
        gu = _gelu(u)
        xh, rstd, vn = _sg_norm(v, g_ref[...], b_ref[...])
        mask = _causal(CHUNK)
        for g in range(SG_GROUPS):
            sl = slice(g * gd, (g + 1) * gd)
            wg = jnp.where(mask, w_ref[g], 0.0).astype(MXU_DTYPE)
            vng = vn[:, sl].astype(MXU_DTYPE)
            mixed = _nn(wg, vng) + bc_ref[g]
            duv_ref[:, sl] = (dso[:, sl] * mixed * _gelu_grad(u[:, sl])).astype(MXU_DTYPE)
            dmix = dso[:, sl] * gu[:, sl]
            dmb = dmix.astype(MXU_DTYPE)
            dbc_ref[g] += jnp.sum(dmix, axis=-1, keepdims=True)
            dw_ref[g] += jnp.where(mask, _nt(dmb, vng), 0.0)
            dvn_ref[:, sl] = _tn(wg, dmb)
        dvn = dvn_ref[...]
        dg_ref[...] += jnp.sum(dvn * xh, axis=0, keepdims=True)
        db_ref[...] += jnp.sum(dvn, axis=0, keepdims=True)
        dxh = dvn * g_ref[...]
        m1 = jnp.mean(dxh, axis=-1, keepdims=True)
        m2 = jnp.mean(dxh * xh, axis=-1, keepdims=True)
        dgv = rstd * (dxh - m1 - xh * m2)
        duv_ref[:, c:] = (dgv * _gelu_grad(v)).astype(MXU_DTYPE)

    return pl.pallas_call(
        body, grid=(t // CHUNK,),
        in_specs=[_row_spec(CHUNK, c, 0), _row_spec(CHUNK, c, 1), _row_spec(CHUNK, c), _par_spec((1, c)),
                  _par_spec((1, c)), _par_spec((SG_GROUPS, CHUNK, CHUNK)), _par_spec((SG_GROUPS, CHUNK, 1)), _ANY],
        out_specs=[_row_spec(CHUNK, 2 * c), _par_spec((SG_GROUPS, CHUNK, CHUNK)), _par_spec((SG_GROUPS, CHUNK, 1)),
                   _par_spec((1, c)), _par_spec((1, c))],
        out_shape=[jax.ShapeDtypeStruct(dproj.shape, dproj.dtype), jax.ShapeDtypeStruct((SG_GROUPS, CHUNK, CHUNK), F32),
                   jax.ShapeDtypeStruct((SG_GROUPS, CHUNK, 1), F32), jax.ShapeDtypeStruct((1, c), F32),
                   jax.ShapeDtypeStruct((1, c), F32)],
        scratch_shapes=[pltpu.VMEM((CHUNK, c), F32)], input_output_aliases={7: 0},
        compiler_params=_params("arbitrary"), name=name)(proj, proj, dsgo, ln_g.reshape(1, c), ln_b.reshape(1, c), w, bcol, dproj)


CONV_TC = 512


def _conv_taps(x):
    rows = lax.broadcasted_iota(jnp.int32, x.shape, 0)
    taps = [jnp.where(rows >= SSM_CONV - 1 - k, pltpu.roll(x, SSM_CONV - 1 - k, axis=0), 0.0) for k in range(SSM_CONV - 1)]
    return taps + [x]


def _conv_pre(taps, w_ref, b_ref):
    acc = b_ref[...]
    for k in range(SSM_CONV):
        acc = acc + taps[k] * w_ref[k:k + 1, :]
    return acc


def _conv_fwd(proj, w, b, *, bsz, name):
    t = proj.shape[0]
    s = t // bsz
    nj = SSM_CONV_DIM // CONV_TC
    c0 = XBC_COL0 // CONV_TC

    def body(x_ref, w_ref, b_ref, o_ref):
        pre = _conv_pre(_conv_taps(x_ref[...]), w_ref, b_ref)
        o_ref[...] = pre * _sigmoid(pre)

    return pl.pallas_call(
        body, grid=(bsz, nj),
        in_specs=[pl.BlockSpec((s, CONV_TC), lambda bb, j: (bb, c0 + j)), pl.BlockSpec((SSM_CONV, CONV_TC), lambda bb, j: (0, j)),
                  pl.BlockSpec((1, CONV_TC), lambda bb, j: (0, j))],
        out_specs=pl.BlockSpec((s, CONV_TC), lambda bb, j: (bb, j)),
        out_shape=jax.ShapeDtypeStruct((t, SSM_CONV_DIM), F32),
        compiler_params=_params("parallel", "parallel"), name=name)(proj, w, b.reshape(1, -1))


def _conv_bwd(proj, dact, w, b, dproj, *, bsz, name):
    t = proj.shape[0]
    s = t // bsz
    nj = SSM_CONV_DIM // CONV_TC
    c0 = XBC_COL0 // CONV_TC

    def body(x_ref, d_ref, w_ref, b_ref, _, dx_ref, dw_ref, db_ref):
        @pl.when(pl.program_id(1) == 0)
        def _():
            dw_ref[...] = jnp.zeros_like(dw_ref)
            db_ref[...] = jnp.zeros_like(db_ref)

        taps = _conv_taps(x_ref[...])
        pre = _conv_pre(taps, w_ref, b_ref)
        sg = _sigmoid(pre)
        dpre = d_ref[...] * (sg * (1.0 + pre * (1.0 - sg)))
        rows = lax.broadcasted_iota(jnp.int32, dpre.shape, 0)
        db_ref[...] += jnp.sum(dpre, axis=0, keepdims=True)
        dx = dpre * w_ref[SSM_CONV - 1:SSM_CONV, :]
        for k in range(SSM_CONV):
            dw_ref[k:k + 1, :] += jnp.sum(dpre * taps[k], axis=0, keepdims=True)
        for k in range(SSM_CONV - 1):
            sh = SSM_CONV - 1 - k
            dsh = jnp.where(rows < s - sh, pltpu.roll(dpre, s - sh, axis=0), 0.0)
            dx = dx + dsh * w_ref[k:k + 1, :]
        dx_ref[...] = dx.astype(MXU_DTYPE)

    return pl.pallas_call(
        body, grid=(nj, bsz),
        in_specs=[pl.BlockSpec((s, CONV_TC), lambda j, bb: (bb, c0 + j)), pl.BlockSpec((s, CONV_TC), lambda j, bb: (bb, j)),
                  pl.BlockSpec((SSM_CONV, CONV_TC), lambda j, bb: (0, j)), pl.BlockSpec((1, CONV_TC), lambda j, bb: (0, j)), _ANY],
        out_specs=[pl.BlockSpec((s, CONV_TC), lambda j, bb: (bb, c0 + j)), pl.BlockSpec((SSM_CONV, CONV_TC), lambda j, bb: (0, j)),
                   pl.BlockSpec((1, CONV_TC), lambda j, bb: (0, j))],
        out_shape=[jax.ShapeDtypeStruct(dproj.shape, dproj.dtype), jax.ShapeDtypeStruct((SSM_CONV, SSM_CONV_DIM), F32),
                   jax.ShapeDtypeStruct((1, SSM_CONV_DIM), F32)],
        input_output_aliases={4: 0},
        compiler_params=_params("parallel", "arbitrary"), name=name)(proj, dact, w, b.reshape(1, -1), dproj)


def _softplus(x):
    return jnp.maximum(x, 0.0) + jnp.log1p(jnp.exp(-jnp.abs(x)))


def _pad_heads(v):
    return jnp.broadcast_to(jnp.pad(v.astype(F32), (0, HEAD_PAD - SSM_HEADS))[None, :], (SUBLANE, HEAD_PAD))


def _ssd_prep(dt_raw, dt_bias8, a_log8, *, name):
    t = dt_raw.shape[0]
    n = CHUNK

    def body(r_ref, b_ref, al_ref, dt_ref, cs_ref, dtt_ref, cst_ref):
        dt = _softplus(r_ref[...] + b_ref[0:1, :])
        da = dt * (-jnp.exp(al_ref[0:1, :]))
        row = lax.broadcasted_iota(jnp.int32, (n, n), 0)
        col = lax.broadcasted_iota(jnp.int32, (n, n), 1)
        lower = (col <= row).astype(F32)
        upper = (row <= col).astype(F32)
        eye = (row == col).astype(F32)
        dt_ref[...] = dt
        cs_ref[...] = _dot_exact(lower, da, _DN_NN, 1)
        cst_ref[0] = _dot_exact(da, upper, _DN_TN, 0)
        dtt_ref[0] = _dot_exact(dt, eye, _DN_TN, 0)

    hp = HEAD_PAD
    return pl.pallas_call(
        body, grid=(t // n,),
        in_specs=[_row_spec(n, hp), _par_spec((SUBLANE, hp)), _par_spec((SUBLANE, hp))],
        out_specs=[_row_spec(n, hp), _row_spec(n, hp), pl.BlockSpec((1, hp, n), lambda i: (i, 0, 0)),
                   pl.BlockSpec((1, hp, n), lambda i: (i, 0, 0))],
        out_shape=[jax.ShapeDtypeStruct((t, hp), F32), jax.ShapeDtypeStruct((t, hp), F32),
                   jax.ShapeDtypeStruct((t // n, hp, n), F32), jax.ShapeDtypeStruct((t // n, hp, n), F32)],
        compiler_params=_params("parallel"), name=name)(dt_raw, dt_bias8, a_log8)


def _expand_mat():
    h = lax.broadcasted_iota(jnp.int32, (HEAD_PAD, SSM_INNER), 0)
    ch = lax.broadcasted_iota(jnp.int32, (HEAD_PAD, SSM_INNER), 1)
    return (ch // SSM_HEADDIM == h).astype(F32)


def _reduce_mat():
    ch = lax.broadcasted_iota(jnp.int32, (SSM_INNER, HEAD_PAD), 0)
    h = lax.broadcasted_iota(jnp.int32, (SSM_INNER, HEAD_PAD), 1)
    return (ch // SSM_HEADDIM == h).astype(F32)


def _expand(v, em):
    return _dot_exact(v, em, _DN_NN, 0)


def _expand_heads(v):
    return jnp.repeat(v.astype(F32), SSM_HEADDIM)[None, :]


def _decay_mat(cs_ref, cst_ref, h, mask):
    seg = cs_ref[:, h:h + 1] - cst_ref[0, h:h + 1, :]
    return jnp.where(mask, jnp.exp(jnp.minimum(seg, 0.0)), 0.0)


GROUP_CH = SSM_INNER // SSM_GROUPS
PAIRS_PER_GROUP = GROUP_CH // LANE
HEADS_PER_GROUP = SSM_HEADS // SSM_GROUPS
BM_COL0 = SSM_INNER
CM_COL0 = SSM_INNER + SSM_GROUPS * SSM_STATE


def _ssd_specs(nc, rev):
    def cidx(i):
        return (i // nc) * nc + (nc - 1 - i % nc) if rev else i

    n = CHUNK
    xs = pl.BlockSpec((n, SSM_INNER), lambda i: (cidx(i), 0))
    bm = pl.BlockSpec((n, GROUP_CH), lambda i: (cidx(i), BM_COL0 // GROUP_CH))
    cm = pl.BlockSpec((n, GROUP_CH), lambda i: (cidx(i), CM_COL0 // GROUP_CH))
    hv = pl.BlockSpec((n, HEAD_PAD), lambda i: (cidx(i), 0))
    hvt = pl.BlockSpec((1, HEAD_PAD, n), lambda i: (cidx(i), 0, 0))
    st = pl.BlockSpec((1, SSM_INNER, SSM_STATE), lambda i: (cidx(i), 0, 0))
    return xs, bm, cm, hv, hvt, st


def _ssd_fwd(xbc, dt, cs, dtt, cst, dskx, *, nc, name):
    t = xbc.shape[0]
    n = CHUNK
    xs_s, bm_s, cm_s, hv_s, hvt_s, st_s = _ssd_specs(nc, False)

    def body(xs_ref, bm_ref, cm_ref, dt_ref, cs_ref, dtt_ref, cst_ref, dsk_ref, y_ref, st_ref, prev):
        @pl.when(pl.program_id(0) % nc == 0)
        def _():
            prev[...] = jnp.zeros_like(prev)

        st_ref[0] = prev[...]
        em = _expand_mat()
        dtx = _expand(dt_ref[...], em)
        csx = _expand(cs_ref[...], em)
        dskx = dsk_ref[...]
        xs = xs_ref[...]
        xdt = xs * dtx
        ecs = jnp.exp(csx)
        dec = jnp.exp(csx[n - 1:n, :] - csx)
        mask = _causal(n)
        lane = lax.broadcasted_iota(jnp.int32, (n, LANE), 1)
        for g in range(SSM_GROUPS):
            gs = slice(g * SSM_STATE, (g + 1) * SSM_STATE)
            gc = slice(g * GROUP_CH, (g + 1) * GROUP_CH)
            cmat = cm_ref[:, gs].astype(MXU_DTYPE)
            bmat = bm_ref[:, gs].astype(MXU_DTYPE)
            cb = _nt(cmat, bmat)
            yoff = ecs[:, gc] * _nt(cmat, prev[gc, :].astype(MXU_DTYPE))
            for q in range(PAIRS_PER_GROUP):
                hp = g * PAIRS_PER_GROUP + q
                sl = slice(hp * LANE, (hp + 1) * LANE)
                xp = xdt[:, sl].astype(MXU_DTYPE)
                m0 = (cb * _decay_mat(cs_ref, cst_ref, 2 * hp, mask)).astype(MXU_DTYPE)
                m1 = (cb * _decay_mat(cs_ref, cst_ref, 2 * hp + 1, mask)).astype(MXU_DTYPE)
                yd = jnp.where(lane < SSM_HEADDIM, _nn(m0, xp), _nn(m1, xp))
                y_ref[:, sl] = yd + yoff[:, q * LANE:(q + 1) * LANE] + xs[:, sl] * dskx[:, sl]
            snew = _tn((xdt[:, gc] * dec[:, gc]).astype(MXU_DTYPE), bmat)
            for r in range(HEADS_PER_GROUP):
                h = g * HEADS_PER_GROUP + r
                rows = slice(h * SSM_HEADDIM, (h + 1) * SSM_HEADDIM)
                e = jnp.exp(cst_ref[0, h:h + 1, n - 1:n])
                prev[rows, :] = prev[rows, :] * e + snew[r * SSM_HEADDIM:(r + 1) * SSM_HEADDIM, :]

    return pl.pallas_call(
        body, grid=(t // n,),
        in_specs=[xs_s, bm_s, cm_s, hv_s, hv_s, hvt_s, hvt_s, _par_spec((1, SSM_INNER))],
        out_specs=[xs_s, st_s],
        out_shape=[jax.ShapeDtypeStruct((t, SSM_INNER), F32), jax.ShapeDtypeStruct((t // n, SSM_INNER, SSM_STATE), F32)],
        scratch_shapes=[pltpu.VMEM((SSM_INNER, SSM_STATE), F32)],
        compiler_params=_params("arbitrary"), name=name)(xbc, xbc, xbc, dt, cs, dtt, cst, dskx)


def _ssd_bwd(dy, xbc, dt, cs, dtt, cst, st, dskx, a_log8, dt_raw, dt_bias8, *, nc, name):
    t = xbc.shape[0]
    n = CHUNK
    xs_s, bm_s, cm_s, hv_s, hvt_s, st_s = _ssd_specs(nc, True)
    acc_s = _par_spec((1, HEAD_PAD))
    xbc_s = pl.BlockSpec((n, SSM_CONV_DIM), xs_s.index_map)

    def body(dy_ref, xs_ref, bm_ref, cm_ref, dt_ref, cs_ref, dtt_ref, cst_ref, st_ref, dsk_ref, al_ref, raw_ref, bias_ref,
             dxbc_ref, ddr_ref, dal_ref, dds_ref, dbias_ref, dprev, dxdt_s, tdec_s, tcs_s):
        @pl.when(pl.program_id(0) % nc == 0)
        def _():
            dprev[...] = jnp.zeros_like(dprev)

        @pl.when(pl.program_id(0) == 0)
        def _():
            dal_ref[...] = jnp.zeros_like(dal_ref)
            dds_ref[...] = jnp.zeros_like(dds_ref)
            dbias_ref[...] = jnp.zeros_like(dbias_ref)

        em = _expand_mat()
        rm = _reduce_mat()

        def head_reduce(v):
            return _dot_exact(v, rm, _DN_NN, 0)

        dtv = dt_ref[...]
        csv = cs_ref[...]
        dtx = _expand(dtv, em)
        csx = _expand(csv, em)
        dskx = dsk_ref[...]
        xs = xs_ref[...]
        dyv = dy_ref[...]
        xdt = xs * dtx
        ecs = jnp.exp(csx)
        dec = jnp.exp(csx[n - 1:n, :] - csx)
        mask = _causal(n)
        lane = lax.broadcasted_iota(jnp.int32, (n, LANE), 1)
        hlane = lax.broadcasted_iota(jnp.int32, (1, HEAD_PAD), 1)
        hsub = lax.broadcasted_iota(jnp.int32, (HEAD_PAD, 1), 0)
        rsum = jnp.zeros((n, HEAD_PAD), F32)
        csum = jnp.zeros((HEAD_PAD, n), F32)
        for g in range(SSM_GROUPS):
            gs = slice(g * SSM_STATE, (g + 1) * SSM_STATE)
            gc = slice(g * GROUP_CH, (g + 1) * GROUP_CH)
            cmat = cm_ref[:, gs].astype(MXU_DTYPE)
            bmat = bm_ref[:, gs].astype(MXU_DTYPE)
            cb = _nt(cmat, bmat)
            pg = st_ref[0, gc, :].astype(MXU_DTYPE)
            dpg = dprev[gc, :]
            dpgb = dpg.astype(MXU_DTYPE)
            z = _nt(cmat, pg)
            dyg = dyv[:, gc]
            dz = (dyg * ecs[:, gc]).astype(MXU_DTYPE)
            dc = _nn(dz, pg)
            dprev_y = _tn(dz, cmat)
            tcs_s[:, gc] = dyg * z * ecs[:, gc]
            xd = xdt[:, gc] * dec[:, gc]
            wmat = _nt(bmat, dpgb)
            db = _nn(xd.astype(MXU_DTYPE), dpgb)
            tdec_s[:, gc] = wmat * xd
            dxdt_g = wmat * dec[:, gc]
            dcb = jnp.zeros((n, n), F32)
            for q in range(PAIRS_PER_GROUP):
                hp = g * PAIRS_PER_GROUP + q
                sl = slice(hp * LANE, (hp + 1) * LANE)
                xp = xdt[:, sl].astype(MXU_DTYPE)
                dyp = dyv[:, sl]
                dypb = dyp.astype(MXU_DTYPE)
                dxp = None
                for hh in range(2):
                    h = 2 * hp + hh
                    lm = _decay_mat(cs_ref, cst_ref, h, mask)
                    mine = (lane < SSM_HEADDIM) if hh == 0 else (lane >= SSM_HEADDIM)
                    dm = _nt(jnp.where(mine, dyp, 0.0).astype(MXU_DTYPE), xp)
                    dml = dm * lm
                    dcb = dcb + dml
                    gseg = dml * cb
                    rsum = rsum + jnp.sum(gseg, axis=1, keepdims=True) * (hlane == h).astype(F32)
                    csum = csum + (hsub == h).astype(F32) * jnp.sum(gseg, axis=0, keepdims=True)
                    dxh = _tn((cb * lm).astype(MXU_DTYPE), dypb)
                    dxp = dxh if dxp is None else jnp.where(mine, dxh, dxp)
                dxdt_s[:, sl] = dxdt_g[:, q * LANE:(q + 1) * LANE] + dxp
            dcbb = dcb.astype(MXU_DTYPE)
            dxbc_ref[:, CM_COL0 + g * SSM_STATE:CM_COL0 + (g + 1) * SSM_STATE] = dc + _nn(dcbb, bmat)
            dxbc_ref[:, BM_COL0 + g * SSM_STATE:BM_COL0 + (g + 1) * SSM_STATE] = db + _tn(dcbb, cmat)
            for r in range(HEADS_PER_GROUP):
                h = g * HEADS_PER_GROUP + r
                rows = slice(h * SSM_HEADDIM, (h + 1) * SSM_HEADDIM)
                lr = slice(r * SSM_HEADDIM, (r + 1) * SSM_HEADDIM)
                e = jnp.exp(cst_ref[0, h:h + 1, n - 1:n])
                dprev[rows, :] = dpg[lr, :] * e + dprev_y[lr, :]
            tq = _dot_exact(dpg * st_ref[0, gc, :], rm[gc, :], _DN_TN, 0)
            if g == 0:
                qsum = jnp.sum(tq, axis=0, keepdims=True)
            else:
                qsum = qsum + jnp.sum(tq, axis=0, keepdims=True)
        dxdt = dxdt_s[...]
        dxbc_ref[:, 0:SSM_INNER] = dxdt * dtx + dyv * dskx
        ddt = head_reduce(dxdt * xs)
        edec = head_reduce(tdec_s[...])
        ycs = head_reduce(tcs_s[...])
        row = lax.broadcasted_iota(jnp.int32, (n, HEAD_PAD), 0)
        extra = jnp.sum(edec, axis=0, keepdims=True) + qsum * jnp.exp(csv[n - 1:n, :])
        dcs = rsum - csum.T + ycs - edec + jnp.where(row == n - 1, extra, 0.0)
        r2 = lax.broadcasted_iota(jnp.int32, (n, n), 0)
        c2 = lax.broadcasted_iota(jnp.int32, (n, n), 1)
        dda = _dot_exact((c2 >= r2).astype(F32), dcs, _DN_NN, 1)
        a_row = -jnp.exp(al_ref[0:1, :])
        ddt = ddt + dda * a_row
        dal_ref[...] += jnp.sum(dda * dtv, axis=0, keepdims=True) * a_row
        dds_ref[...] += jnp.sum(head_reduce(dyv * xs), axis=0, keepdims=True)
        ddr = ddt * _sigmoid(raw_ref[...] + bias_ref[0:1, :])
        ddr_ref[...] = ddr
        dbias_ref[...] += jnp.sum(ddr, axis=0, keepdims=True)

    par8 = _par_spec((SUBLANE, HEAD_PAD))
    return pl.pallas_call(
        body, grid=(t // n,),
        in_specs=[xs_s, xs_s, bm_s, cm_s, hv_s, hv_s, hvt_s, hvt_s, st_s, _par_spec((1, SSM_INNER)), par8, hv_s, par8],
        out_specs=[xbc_s, hv_s, acc_s, acc_s, acc_s],
        out_shape=[jax.ShapeDtypeStruct((t, SSM_CONV_DIM), F32), jax.ShapeDtypeStruct((t, HEAD_PAD), F32),
                   jax.ShapeDtypeStruct((1, HEAD_PAD), F32), jax.ShapeDtypeStruct((1, HEAD_PAD), F32),
                   jax.ShapeDtypeStruct((1, HEAD_PAD), F32)],
        scratch_shapes=[pltpu.VMEM((SSM_INNER, SSM_STATE), F32), pltpu.VMEM((n, SSM_INNER), F32),
                        pltpu.VMEM((n, SSM_INNER), F32), pltpu.VMEM((n, SSM_INNER), F32)],
        compiler_params=_params("arbitrary"), name=name)(dy, xbc, xbc, xbc, dt, cs, dtt, cst, st, dskx, a_log8, dt_raw, dt_bias8)


def _gate_norm_fwd(y, proj, norm_g, *, name):
    t, c = y.shape
    tm = _pick(t, (256, 128))

    def body(y_ref, z_ref, g_ref, o_ref):
        z = z_ref[...]
        yz = y_ref[...] * z * _sigmoid(z)
        for g in range(SSM_GROUPS):
            gc = slice(g * GROUP_CH, (g + 1) * GROUP_CH)
            seg = yz[:, gc]
            r = lax.rsqrt(jnp.mean(seg * seg, axis=-1, keepdims=True) + RMS_EPS)
            o_ref[:, gc] = (seg * r * g_ref[:, gc]).astype(MXU_DTYPE)

    return pl.pallas_call(
        body, grid=(t // tm,), in_specs=[_row_spec(tm, c), _row_spec(tm, c, 1), _par_spec((1, c))],
        out_specs=_row_spec(tm, c), out_shape=jax.ShapeDtypeStruct((t, c), MXU_DTYPE),
        compiler_params=_params("parallel"), name=name)(y, proj, norm_g.reshape(1, c))


def _gate_norm_bwd(dyb, y, proj, norm_g, dproj, *, name):
    t, c = y.shape
    tm = _pick(t, (256, 128))

    def body(d_ref, y_ref, z_ref, g_ref, _, dy_ref, dz_ref, dg_ref):
        @pl.when(pl.program_id(0) == 0)
        def _():
            dg_ref[...] = jnp.zeros_like(dg_ref)

        z = z_ref[...]
        yv = y_ref[...]
        sz = _sigmoid(z)
        silu = z * sz
        yz = yv * silu
        dv = d_ref[...]
        for g in range(SSM_GROUPS):
            gc = slice(g * GROUP_CH, (g + 1) * GROUP_CH)
            seg = yz[:, gc]
            r = lax.rsqrt(jnp.mean(seg * seg, axis=-1, keepdims=True) + RMS_EPS)
            nrm = seg * r
            dn = dv[:, gc] * g_ref[:, gc]
            dg_ref[:, gc] += jnp.sum(dv[:, gc] * nrm, axis=0, keepdims=True)
            dyz = r * (dn - nrm * jnp.mean(dn * nrm, axis=-1, keepdims=True))
            dy_ref[:, gc] = dyz * silu[:, gc]
            dz_ref[:, gc] = (dyz * yv[:, gc] * (sz[:, gc] * (1.0 + z[:, gc] * (1.0 - sz[:, gc])))).astype(MXU_DTYPE)

    return pl.pallas_call(
        body, grid=(t // tm,), in_specs=[_row_spec(tm, c), _row_spec(tm, c), _row_spec(tm, c, 1), _par_spec((1, c)), _ANY],
        out_specs=[_row_spec(tm, c), _row_spec(tm, c, 1), _par_spec((1, c))],
        out_shape=[jax.ShapeDtypeStruct((t, c), F32), jax.ShapeDtypeStruct(dproj.shape, dproj.dtype),
                   jax.ShapeDtypeStruct((1, c), F32)],
        input_output_aliases={4: 1},
        compiler_params=_params("arbitrary"), name=name)(dyb, y, proj, norm_g.reshape(1, c), dproj)


GA_COLBLK = GAB_COL0 // D_MODEL


def _merge_fwd(br_a, br_b, proj, *, name):
    t, c = br_a.shape
    tm = _pick(t, ROW_TILES)

    def body(a_ref, b_ref, ga_ref, gb_ref, o_ref):
        o_ref[...] = (_sigmoid(ga_ref[...]) * a_ref[...] + _sigmoid(gb_ref[...]) * b_ref[...]).astype(MXU_DTYPE)

    return pl.pallas_call(
        body, grid=(t // tm,),
        in_specs=[_row_spec(tm, c), _row_spec(tm, c), _row_spec(tm, c, GA_COLBLK), _row_spec(tm, c, GA_COLBLK + 1)],
        out_specs=_row_spec(tm, c), out_shape=jax.ShapeDtypeStruct((t, c), MXU_DTYPE),
        compiler_params=_params("parallel"), name=name)(br_a, br_b, proj, proj)


def _merge_bwd(dm, br_a, br_b, proj, *, name):
    t, c = br_a.shape
    tm = _pick(t, ROW_TILES)

    def body(dm_ref, a_ref, b_ref, ga_ref, gb_ref, da_ref, db_ref, dg_ref):
        d = dm_ref[...]
        sa = _sigmoid(ga_ref[...])
        sb = _sigmoid(gb_ref[...])
        da_ref[...] = (d * sa).astype(MXU_DTYPE)
        db_ref[...] = (d * sb).astype(MXU_DTYPE)
        dg_ref[:, :c] = (d * a_ref[...] * sa * (1.0 - sa)).astype(MXU_DTYPE)
        dg_ref[:, c:] = (d * b_ref[...] * sb * (1.0 - sb)).astype(MXU_DTYPE)

    return pl.pallas_call(
        body, grid=(t // tm,),
        in_specs=[_row_spec(tm, c), _row_spec(tm, c), _row_spec(tm, c), _row_spec(tm, c, GA_COLBLK), _row_spec(tm, c, GA_COLBLK + 1)],
        out_specs=[_row_spec(tm, c), _row_spec(tm, c), _row_spec(tm, 2 * c, GAB_COL0 // (2 * c))],
        out_shape=[jax.ShapeDtypeStruct((t, c), MXU_DTYPE), jax.ShapeDtypeStruct((t, c), MXU_DTYPE),
                   jax.ShapeDtypeStruct((t, MAIN_COLS), MXU_DTYPE)],
        compiler_params=_params("parallel"), name=name)(dm, br_a, br_b, proj, proj)


def _layer_fwd(x, xb, memn_b, w, *, bsz, tag):
    nc = x.shape[0] // bsz // CHUNK
    sv = {"x_in": xb}
    proj = _mm(xb, w["w_main"], name=f"{tag}_proj")
    dt_raw = _mm(xb, w["w_dt"], name=f"{tag}_dtproj")
    sgo = _sg_fwd(proj, w["sg_ln_g"], w["sg_ln_b"], w["sg_w"], w["sg_bcol"], name=f"{tag}_sg_fwd")
    xbc = _conv_fwd(proj, w["conv_w"], w["conv_b"], bsz=bsz, name=f"{tag}_conv_fwd")
    dt, cs, dtt, cst = _ssd_prep(dt_raw, w["dt_bias8"], w["a_log8"], name=f"{tag}_ssd_prep")
    y, st = _ssd_fwd(xbc, dt, cs, dtt, cst, w["d_skipx"], nc=nc, name=f"{tag}_ssd_fwd")
    yb = _gate_norm_fwd(y, proj, w["ssm_norm_g"], name=f"{tag}_gate_norm_fwd")
    if "rest" in w:
        w = w["rest"](w, yb)
    br_a = _mm(sgo, w["p_a"], name=f"{tag}_br_a")
    br_b = _mm(yb, w["p_b"], name=f"{tag}_br_b")
    merged = _merge_fwd(br_a, br_b, proj, name=f"{tag}_merge_fwd")
    mix = _mm(merged, w["w_mix_o"], name=f"{tag}_mix_o")
    x1, x1b, xh1, rs1 = _ln_fwd(x, mix, w["ln_g"][0], w["ln_b"][0], name=f"{tag}_ln1_fwd")
    sv.update(proj=proj, dt_raw=dt_raw, sgo=sgo, xbc=xbc, dt=dt, cs=cs, dtt=dtt, cst=cst, y=y, st=st, yb=yb,
              br_a=br_a, br_b=br_b, merged=merged, xh1=xh1, rs1=rs1, x1b=x1b)
    q = _mm(x1b, w["w_xq"], out_dtype=MXU_DTYPE, name=f"{tag}_q")
    kv = _mm(memn_b, w["w_xkv"], out_dtype=MXU_DTYPE, name=f"{tag}_kv")
    o = _attn_fwd(q, kv, bsz=bsz, name=f"{tag}_attn_fwd")
    att = _mm(o, w["w_xo"], name=f"{tag}_xo")
    x2, x2b, xh2, rs2 = _ln_fwd(x1, att, w["ln_g"][1], w["ln_b"][1], name=f"{tag}_ln2_fwd")
    sv.update(q=q, kv=kv, o=o, xh2=xh2, rs2=rs2, x2b=x2b)
    h = _mm(x2b, w["w_ffn_in"], name=f"{tag}_ffn_in")
    a = _swiglu_fwd(h, name=f"{tag}_swiglu_fwd")
    ffn = _mm(a, w["w_ffn_out"], name=f"{tag}_ffn_out")
    x3, x3b, xh3, rs3 = _ln_fwd(x2, ffn, w["ln_g"][2], w["ln_b"][2], name=f"{tag}_ln3_fwd")
    sv.update(h=h, a=a, xh3=xh3, rs3=rs3)
    return x3, x3b, sv, w


GRAD_GROUPS = (("w_ffn_out", "w_ffn_in", "w_xo", "w_xq", "w_xkv"), ("w_mix_o", "p_a", "p_b"), ("w_in",))


def _layer_bwd(dx3_addends, dx3_scales, memn_b, w, sv, on_group=None, *, bsz, tag):
    nc = sv["xh1"].shape[0] // bsz // CHUNK
    gr = {}

    def group_done(k):
        return on_group(GRAD_GROUPS[k], gr) if on_group is not None else None
    dp3, dp3b, dg3, db3 = _ln_bwd(dx3_addends, dx3_scales, sv["xh3"], sv["rs3"], w["ln_g"][2], name=f"{tag}_ln3_bwd")
    da = _mm(dp3b, w["w_ffn_out"], tb=True, name=f"{tag}_d_a")
    gr["w_ffn_out"] = _mm(sv["a"], dp3b, ta=True, name=f"{tag}_dw_ffn_out")
    dh = _swiglu_bwd(sv["h"], da, name=f"{tag}_swiglu_bwd")
    gr["w_ffn_in"] = _mm(sv["x2b"], dh, ta=True, name=f"{tag}_dw_ffn_in")
    dx2_br = _mm(dh, w["w_ffn_in"], tb=True, name=f"{tag}_dx2")
    dp2, dp2b, dg2, db2 = _ln_bwd([dp3, dx2_br], [ALPHA, 1.0], sv["xh2"], sv["rs2"], w["ln_g"][1], name=f"{tag}_ln2_bwd")
    do = _mm(dp2b, w["w_xo"], tb=True, out_dtype=MXU_DTYPE, name=f"{tag}_d_o")
    gr["w_xo"] = _mm(sv["o"], dp2b, ta=True, name=f"{tag}_dw_xo")
    dq, dk, dv = _attn_bwd(sv["q"], sv["kv"], do, bsz=bsz, name=f"{tag}_attn_bwd")
    dkv = jnp.concatenate([dk, dv], axis=1)
    gr["w_xq"] = _mm(sv["x1b"], dq, ta=True, name=f"{tag}_dw_xq")
    gr["w_xkv"] = _mm(memn_b, dkv, ta=True, name=f"{tag}_dw_xkv")
    dmemn = _mm(dkv, w["w_xkv"], tb=True, name=f"{tag}_d_memn")
    dx1_br = _mm(dq, w["w_xq"], tb=True, name=f"{tag}_dx1")
    token = group_done(0)
    ln_g1 = w["ln_g"][0] if token is None else w["ln_g"][0] + token[0, 0]
    dp1, dp1b, dg1, db1 = _ln_bwd([dp2, dx1_br], [ALPHA, 1.0], sv["xh1"], sv["rs1"], ln_g1, name=f"{tag}_ln1_bwd")
    gr["ln_g"] = jnp.concatenate([dg1, dg2, dg3], axis=0)
    gr["ln_b"] = jnp.concatenate([db1, db2, db3], axis=0)
    dmerged = _mm(dp1b, w["w_mix_o"], tb=True, name=f"{tag}_d_merged")
    gr["w_mix_o"] = _mm(sv["merged"], dp1b, ta=True, name=f"{tag}_dw_mix_o")
    dbr_a, dbr_b, dproj = _merge_bwd(dmerged, sv["br_a"], sv["br_b"], sv["proj"], name=f"{tag}_merge_bwd")
    gr["p_a"] = _mm(sv["sgo"], dbr_a, ta=True, name=f"{tag}_dw_p_a")
    gr["p_b"] = _mm(sv["yb"], dbr_b, ta=True, name=f"{tag}_dw_p_b")
    dsgo = _mm(dbr_a, w["p_a"], tb=True, name=f"{tag}_d_sgo")
    dyb = _mm(dbr_b, w["p_b"], tb=True, name=f"{tag}_d_yb")
    token = group_done(1)
    norm_g = w["ssm_norm_g"] if token is None else w["ssm_norm_g"] + token[0, 0]
    dy, dproj, gr["ssm_norm_g"] = _gate_norm_bwd(dyb, sv["y"], sv["proj"], norm_g, dproj, name=f"{tag}_gate_norm_bwd")
    dxbc, ddr, gr["a_log"], gr["d_skip"], gr["dt_bias"] = _ssd_bwd(
        dy, sv["xbc"], sv["dt"], sv["cs"], sv["dtt"], sv["cst"], sv["st"], w["d_skipx"], w["a_log8"], sv["dt_raw"],
        w["dt_bias8"], nc=nc, name=f"{tag}_ssd_bwd")
    dproj, gr["conv_w"], gr["conv_b"] = _conv_bwd(sv["proj"], dxbc, w["conv_w"], w["conv_b"], dproj, bsz=bsz, name=f"{tag}_conv_bwd")
    dproj, gr["sg_w"], dsg_bcol, gr["sg_ln_g"], gr["sg_ln_b"] = _sg_bwd(
        sv["proj"], dsgo, w["sg_ln_g"], w["sg_ln_b"], w["sg_w"], w["sg_bcol"], dproj, name=f"{tag}_sg_bwd")
    gr["sg_b"] = dsg_bcol[..., 0]
    gr["w_main"] = _mm(sv["x_in"], dproj, ta=True, name=f"{tag}_dw_main")
    gr["w_dt"] = _mm(sv["x_in"], ddr, ta=True, name=f"{tag}_dw_dt")
    token = group_done(2)
    dx_dt = _mm(ddr, w["w_dt"], tb=True, after=token, name=f"{tag}_dx_dt")
    dx_main = _mm(dproj, w["w_main"], tb=True, after=token, name=f"{tag}_dx_main")
    return [dp1, dx_main, dx_dt], [ALPHA, 1.0, 1.0], gr, dmemn


def _local_step(x, mem, tgt, mem_ln_g, mem_ln_b, layers, on_layer_grads=None):
    bsz, s, d = x.shape
    xf = x.reshape(bsz * s, d)
    memf = mem.reshape(-1, d)
    _, memn_b, mxh, mrs = _ln_fwd(memf, None, mem_ln_g, mem_ln_b, name="mem_ln_fwd")
    cur, curb, saved, weights = xf, xf, [], []
    for li, get_weights in enumerate(layers):
        cur, curb, sv, w = _layer_fwd(cur, curb, memn_b, get_weights(cur), bsz=bsz, tag=f"l{li}")
        saved.append(sv)
        weights.append(w)
    dy, lsum = _loss_head(cur, tgt.reshape(bsz * s, d), name="loss_head")
    addends, scales = [dy], [1.0]
    grads, dmem = [None] * len(layers), []
    for li in reversed(range(len(layers))):
        on_group = None if on_layer_grads is None else functools.partial(on_layer_grads, li)
        addends, scales, grads[li], dm = _layer_bwd(addends, scales, memn_b, weights[li], saved[li], on_group, bsz=bsz, tag=f"l{li}")
        dmem.append(dm)
    grad_x = _add_scaled(addends, scales, name="grad_x").reshape(bsz, s, d)
    _, _, dmg, dmb = _ln_bwd(dmem, [1.0] * len(dmem), mxh, mrs, mem_ln_g, name="mem_ln_bwd")
    return lsum, grad_x, grads, dmg[0], dmb[0]


_ANY = pl.BlockSpec(memory_space=pl.ANY)
_MESH = pl.DeviceIdType.MESH


def _all_gather8(x, *, name):
    def body(x_ref, out_ref, send_sems, recv_sems):
        mx, my, mc = lax.axis_index("x"), lax.axis_index("y"), lax.axis_index("c")
        me, sibling = (mx, my, mc), (mx, my, 1 - mc)
        chips = [(1 - mx, my), (mx, 1 - my), (1 - mx, 1 - my)]

        def blk(px, py, pc):
            return out_ref.at[4 * px + 2 * py + pc]

        def copy(k, block, to, src=None):
            return pltpu.make_async_remote_copy(
                src_ref=blk(*block) if src is None else src, dst_ref=blk(*block), send_sem=send_sems.at[k],
                recv_sem=recv_sems.at[k], device_id=to, device_id_type=_MESH)

        first = [copy(0, me, sibling, src=x_ref)]
        first += [copy(1 + j, me, (*chip, mc), src=x_ref) for j, chip in enumerate(chips)]
        for cp in first:
            cp.start()
        passed = [copy(4 + j, (*chip, mc), sibling) for j, chip in enumerate(chips)]
        for j, chip in enumerate(chips):
            copy(1 + j, (*chip, mc), me).wait_recv()
            passed[j].start()
        copy(0, sibling, me).wait_recv()
        for j, chip in enumerate(chips):
            copy(4 + j, (*chip, 1 - mc), me).wait_recv()
        for cp in first + passed:
            cp.wait_send()

    return pl.pallas_call(
        body, out_shape=jax.ShapeDtypeStruct((N_DEV,) + x.shape, x.dtype), in_specs=[_ANY], out_specs=_ANY,
        scratch_shapes=[pltpu.SemaphoreType.DMA((7,)), pltpu.SemaphoreType.DMA((7,))], name=name)(x)


def _row_tile(rows, row_bytes, mult=SUBLANE):
    best = None
    for tr in range(mult, rows + 1, mult):
        if rows % tr == 0 and (best is None or tr * row_bytes <= BLOCK_BYTES):
            best = tr
    return rows if best is None else best


def _gather_shape(r, c, kind):
    return {"row": (2, N_CHIPS * r, c), "col": (2, r, N_CHIPS * c), "chip": (2, N_CHIPS, r, c)}[kind]


def _cast_place(shard, kind, dtype, chip_idx, *, name):
    _, r, c = shard.shape
    tr = _row_tile(r, c * 4, 16)
    nt = r // tr

    def body(_, s_ref, o_ref):
        o_ref[...] = s_ref[...].astype(dtype)

    if kind == "row":
        out_spec = pl.BlockSpec((None, tr, c), lambda l, i, j_ref: (l, j_ref[0] * nt + i, 0))
    elif kind == "col":
        out_spec = pl.BlockSpec((None, tr, c), lambda l, i, j_ref: (l, i, j_ref[0]))
    else:
        out_spec = pl.BlockSpec((None, None, tr, c), lambda l, i, j_ref: (l, j_ref[0], i, 0))
    grid_spec = pltpu.PrefetchScalarGridSpec(
        num_scalar_prefetch=1, grid=(2, nt), in_specs=[pl.BlockSpec((None, tr, c), lambda l, i, j_ref: (l, i, 0))],
        out_specs=out_spec)
    return pl.pallas_call(body, grid_spec=grid_spec, out_shape=jax.ShapeDtypeStruct(_gather_shape(r, c, kind), dtype),
                          compiler_params=_params("parallel", "parallel"), name=name)(chip_idx, shard)


def _gather_params(bufs, shard_shapes, kinds, *, name):
    n = len(bufs)

    def body(*refs):
        outs = refs[n:2 * n]
        send_sems, recv_sems = refs[2 * n:]
        mx, my, mc = lax.axis_index("x"), lax.axis_index("y"), lax.axis_index("c")
        me, sibling = (mx, my, mc), (mx, my, 1 - mc)
        chips = [(1 - mx, my), (mx, 1 - my), (1 - mx, 1 - my)]

        def blk(i, px, py, pc):
            r, c = shard_shapes[i]
            j = 2 * px + py
            if kinds[i] == "row":
                return outs[i].at[pc, pl.ds(pl.multiple_of(j * r, r), r)]
            if kinds[i] == "col":
                return outs[i].at[pc, :, pl.ds(pl.multiple_of(j * c, c), c)]
            return outs[i].at[pc, j]

        def copy(i, k, block, to):
            return pltpu.make_async_remote_copy(
                src_ref=blk(i, *block), dst_ref=blk(i, *block), send_sem=send_sems.at[6 * i + k],
                recv_sem=recv_sems.at[6 * i + k], device_id=to, device_id_type=_MESH)

        sent = []
        for i in range(n):
            for j, chip in enumerate(chips):
                cp = copy(i, j, me, (*chip, mc))
                cp.start()
                sent.append(cp)
        for j, chip in enumerate(chips):
            for i in range(n):
                copy(i, j, (*chip, mc), me).wait_recv()
                fwd = copy(i, 3 + j, (*chip, mc), sibling)
                fwd.start()
                sent.append(fwd)
        for i in range(n):
            for j, chip in enumerate(chips):
                copy(i, 3 + j, (*chip, 1 - mc), me).wait_recv()
        for cp in sent:
            cp.wait_send()

    return pl.pallas_call(
        body, out_shape=[jax.ShapeDtypeStruct(b.shape, b.dtype) for b in bufs], in_specs=[_ANY] * n, out_specs=[_ANY] * n,
        input_output_aliases={i: i for i in range(n)},
        scratch_shapes=[pltpu.SemaphoreType.DMA((6 * n,)), pltpu.SemaphoreType.DMA((6 * n,))], name=name)(*bufs)


def _half(r, h):
    return pl.ds(pl.multiple_of(h * (r // 2), r // 2), r // 2)


_HBM = pl.BlockSpec(memory_space=pltpu.HBM)
_SEM = pl.BlockSpec(memory_space=pltpu.SEMAPHORE)
_EFFECT = pltpu.SideEffectType.DATAFLOW_SIDE_EFFECTING


def _sibling_copies(g_refs, land_refs, gs, views, send_sems, recv_sems):
    mx, my, mc = lax.axis_index("x"), lax.axis_index("y"), lax.axis_index("c")
    copies = []
    for i in range(len(gs)):
        if views[i] == "chip":
            src = g_refs[i].at[:, _half(gs[i].shape[1], 1 - mc)]
        else:
            src = g_refs[i].at[_half(gs[i].shape[0], 1 - mc)]
        copies.append(pltpu.make_async_remote_copy(src_ref=src, dst_ref=land_refs[i], send_sem=send_sems.at[i], recv_sem=recv_sems.at[i],
                                                   device_id=(mx, my, 1 - mc), device_id_type=_MESH))
    return copies


def _half_shape(g, view):
    return (g.shape[0], g.shape[1] // 2, g.shape[2]) if view == "chip" else (g.shape[0] // 2, g.shape[1])


def _grads_to_sibling_start(gs, views, *, name):
    n = len(gs)
    lands = [pltpu.with_memory_space_constraint(lax.empty(_half_shape(g, v), g.dtype), pltpu.HBM) for g, v in zip(gs, views)]

    def body(*refs):
        for cp in _sibling_copies(refs[:n], refs[n:2 * n], gs, views, refs[2 * n], refs[2 * n + 1]):
            cp.start()
        refs[-1][...] = jnp.zeros_like(refs[-1])

    outs = pl.pallas_call(
        body, name=name,
        out_shape=(pltpu.SemaphoreType.DMA((n,)), pltpu.SemaphoreType.DMA((n,)),
                   *[pltpu.HBM(x.shape, x.dtype) for x in list(gs) + lands], jax.ShapeDtypeStruct((SUBLANE, LANE), F32)),
        in_specs=[_HBM] * (2 * n), out_specs=(_SEM, _SEM, *[_HBM] * (2 * n), pl.BlockSpec(memory_space=pltpu.VMEM)),
        input_output_aliases={i: 2 + i for i in range(2 * n)},
        compiler_params=pltpu.CompilerParams(has_side_effects=_EFFECT),
    )(*[pltpu.with_memory_space_constraint(g, pltpu.HBM) for g in gs], *lands)
    return outs[0], outs[1], list(outs[2:2 + n]), list(outs[2 + n:2 + 2 * n]), outs[-1]


def _grads_to_sibling_wait(send_sems, recv_sems, gs, lands, views, after, *, name):
    n = len(gs)

    def body(*refs):
        for cp in _sibling_copies(refs[:n], refs[n:2 * n], gs, views, refs[2 * n], refs[2 * n + 1]):
            cp.wait_send()
            cp.wait_recv()

    outs = pl.pallas_call(
        body, name=name, out_shape=tuple(pltpu.HBM(x.shape, x.dtype) for x in list(gs) + list(lands)),
        in_specs=[_HBM] * (2 * n) + [_SEM, _SEM, _ANY], out_specs=tuple([_HBM] * (2 * n)),
        input_output_aliases={i: i for i in range(2 * n)},
        compiler_params=pltpu.CompilerParams(has_side_effects=_EFFECT),
    )(*gs, *lands, send_sems, recv_sems, after)
    return list(outs[:n]), list(outs[n:])


def _cast_place_layer(shard, l, kind, chip_idx, after, *, name):
    _, r, c = shard.shape
    tr = _row_tile(r, c * 4, 16)
    nt = r // tr

    def body(_, s_ref, *rest):
        rest[-1][...] = s_ref[...].astype(MXU_DTYPE)

    if kind == "row":
        out_spec = pl.BlockSpec((tr, c), lambda i, j_ref: (j_ref[0] * nt + i, 0))
    elif kind == "col":
        out_spec = pl.BlockSpec((tr, c), lambda i, j_ref: (i, j_ref[0]))
    else:
        out_spec = pl.BlockSpec((None, tr, c), lambda i, j_ref: (j_ref[0], i, 0))
    extra = [] if after is None else [after]
    grid_spec = pltpu.PrefetchScalarGridSpec(
        num_scalar_prefetch=1, grid=(nt,), in_specs=[pl.BlockSpec((None, tr, c), lambda i, j_ref: (l, i, 0))] + [_ANY] * len(extra),
        out_specs=out_spec)
    return pl.pallas_call(body, grid_spec=grid_spec, out_shape=jax.ShapeDtypeStruct(_gather_shape(r, c, kind)[1:], MXU_DTYPE),
                          compiler_params=_params("parallel"), name=name)(chip_idx, shard, *extra)


def _half_block(ref, kind, r, c, j, h):
    rows = _half(r, h)
    if kind == "row":
        return ref.at[pl.ds(pl.multiple_of(j * r + h * (r // 2), r // 2), r // 2)]
    if kind == "col":
        return ref.at[rows, pl.ds(pl.multiple_of(j * c, c), c)]
    return ref.at[j, rows]


def _gather_ici_copies(buf_refs, shapes, kinds, send_sems, recv_sems):
    mx, my, mc = lax.axis_index("x"), lax.axis_index("y"), lax.axis_index("c")
    chips = [(1 - mx, my), (mx, 1 - my), (1 - mx, 1 - my)]
    copies = []
    for i, (r, c) in enumerate(shapes):
        mine = _half_block(buf_refs[i], kinds[i], r, c, 2 * mx + my, mc)
        for k, (px, py) in enumerate(chips):
            copies.append(pltpu.make_async_remote_copy(
                src_ref=mine, dst_ref=mine, send_sem=send_sems.at[3 * i + k], recv_sem=recv_sems.at[3 * i + k],
                device_id=(px, py, mc), device_id_type=_MESH))
    return copies


def _gather_start(bufs, shapes, kinds, *, name):
    n = len(bufs)

    def body(*refs):
        send_sems, recv_sems, token = refs[n], refs[n + 1], refs[-1]
        for cp in _gather_ici_copies(refs[:n], shapes, kinds, send_sems, recv_sems):
            cp.start()
        token[...] = jnp.zeros_like(token)

    outs = pl.pallas_call(
        body, name=name,
        out_shape=(pltpu.SemaphoreType.DMA((3 * n,)), pltpu.SemaphoreType.DMA((3 * n,)),
                   *[pltpu.HBM(b.shape, b.dtype) for b in bufs], jax.ShapeDtypeStruct((SUBLANE, LANE), F32)),
        in_specs=[_HBM] * n, out_specs=(_SEM, _SEM, *[_HBM] * n, pl.BlockSpec(memory_space=pltpu.VMEM)),
        input_output_aliases={i: 2 + i for i in range(n)},
        compiler_params=pltpu.CompilerParams(has_side_effects=_EFFECT),
    )(*[pltpu.with_memory_space_constraint(b, pltpu.HBM) for b in bufs])
    return outs[0], outs[1], list(outs[2:2 + n]), outs[-1]


def _gather_wait(send_sems, recv_sems, bufs, shapes, kinds, after, *, name):
    n = len(bufs)

    def body(*refs):
        for cp in _gather_ici_copies(refs[:n], shapes, kinds, refs[n], refs[n + 1]):
            cp.wait_send()
            cp.wait_recv()

    outs = pl.pallas_call(
        body, name=name, out_shape=tuple(pltpu.HBM(b.shape, b.dtype) for b in bufs),
        in_specs=[_HBM] * n + [_SEM, _SEM, _ANY], out_specs=tuple([_HBM] * n), input_output_aliases={i: i for i in range(n)},
        compiler_params=pltpu.CompilerParams(has_side_effects=_EFFECT),
    )(*bufs, send_sems, recv_sems, after)
    return list(outs)


def _gather_forward(bufs, shapes, kinds, *, name):
    n = len(bufs)

    def body(*refs):
        outs = refs[n:2 * n]
        send_sems, recv_sems = refs[2 * n:]
        mx, my, mc = lax.axis_index("x"), lax.axis_index("y"), lax.axis_index("c")
        chips = [(1 - mx, my), (mx, 1 - my), (1 - mx, 1 - my)]
        copies = []
        for i, (r, c) in enumerate(shapes):
            for k, (px, py) in enumerate(chips):
                got = _half_block(outs[i], kinds[i], r, c, 2 * px + py, mc)
                cp = pltpu.make_async_remote_copy(src_ref=got, dst_ref=got, send_sem=send_sems.at[3 * i + k],
                                                  recv_sem=recv_sems.at[3 * i + k], device_id=(mx, my, 1 - mc), device_id_type=_MESH)
                cp.start()
                copies.append(cp)
        for cp in copies:
            cp.wait()

    return pl.pallas_call(
        body, out_shape=[jax.ShapeDtypeStruct(b.shape, b.dtype) for b in bufs], in_specs=[_ANY] * n, out_specs=[_ANY] * n,
        input_output_aliases={i: i for i in range(n)},
        scratch_shapes=[pltpu.SemaphoreType.DMA((3 * n,)), pltpu.SemaphoreType.DMA((3 * n,))], name=name)(*bufs)


def _chip_exchange_copies(pair_refs, land_refs, pairs, views, send_sems, recv_sems):
    mx, my, mc = lax.axis_index("x"), lax.axis_index("y"), lax.axis_index("c")
    me = 2 * mx + my
    chips = [(1 - mx, my), (mx, 1 - my), (1 - mx, 1 - my)]
    copies = []
    for i in range(len(pairs)):
        for k, (px, py) in enumerate(chips):
            j = 2 * px + py
            if views[i] == "chip":
                src = pair_refs[i].at[j]
            else:
                c = pairs[i].shape[1] // N_CHIPS
                src = pair_refs[i].at[:, pl.ds(pl.multiple_of(j * c, c), c)]
            copies.append(pltpu.make_async_remote_copy(
                src_ref=src, dst_ref=land_refs[i].at[me], send_sem=send_sems.at[3 * i + k], recv_sem=recv_sems.at[3 * i + k],
                device_id=(px, py, mc), device_id_type=_MESH))
    return copies


def _quad_shape(p, view):
    return p.shape if view == "chip" else (N_CHIPS, p.shape[0], p.shape[1] // N_CHIPS)


def _grads_to_chips_start(pairs, views, *, name):
    n = len(pairs)
    lands = [pltpu.with_memory_space_constraint(lax.empty(_quad_shape(p, v), p.dtype), pltpu.HBM) for p, v in zip(pairs, views)]

    def body(*refs):
        pair_refs, land_refs = refs[:n], refs[n:2 * n]
        send_sems, recv_sems = refs[2 * n], refs[2 * n + 1]
        token = refs[-1]
        for cp in _chip_exchange_copies(pair_refs, land_refs, pairs, views, send_sems, recv_sems):
            cp.start()
        token[...] = jnp.zeros_like(token)

    outs = pl.pallas_call(
        body, name=name,
        out_shape=(pltpu.SemaphoreType.DMA((3 * n,)), pltpu.SemaphoreType.DMA((3 * n,)),
                   *[pltpu.HBM(p.shape, p.dtype) for p in pairs], *[pltpu.HBM(l.shape, l.dtype) for l in lands],
                   jax.ShapeDtypeStruct((SUBLANE, LANE), F32)),
        in_specs=[_HBM] * (2 * n), out_specs=(_SEM, _SEM, *[_HBM] * (2 * n), pl.BlockSpec(memory_space=pltpu.VMEM)),
        input_output_aliases={i: 2 + i for i in range(2 * n)},
        compiler_params=pltpu.CompilerParams(has_side_effects=_EFFECT),
    )(*[pltpu.with_memory_space_constraint(p, pltpu.HBM) for p in pairs], *lands)
    return outs[0], outs[1], list(outs[2:2 + n]), list(outs[2 + n:2 + 2 * n]), outs[-1]


def _grads_to_chips_wait(send_sems, recv_sems, pairs, lands, views, after, *, name):
    n = len(pairs)

    def body(*refs):
        pair_refs, land_refs = refs[:n], refs[n:2 * n]
        s_sems, r_sems = refs[2 * n], refs[2 * n + 1]
        for cp in _chip_exchange_copies(pair_refs, land_refs, pairs, views, s_sems, r_sems):
            cp.wait_send()
            cp.wait_recv()

    outs = pl.pallas_call(
        body, name=name, out_shape=tuple(pltpu.HBM(x.shape, x.dtype) for x in list(pairs) + list(lands)),
        in_specs=[_HBM] * (2 * n) + [_SEM, _SEM, _ANY], out_specs=tuple([_HBM] * (2 * n)),
        input_output_aliases={i: i for i in range(2 * n)},
        compiler_params=pltpu.CompilerParams(has_side_effects=_EFFECT),
    )(*pairs, *lands, send_sems, recv_sems, after)
    return list(outs[n:])


def _grads_share(tots, *, name):
    n = len(tots)

    def body(*refs):
        ins, outs = refs[:n], refs[n:2 * n]
        send_sems, recv_sems = refs[2 * n:]
        mx, my, mc = lax.axis_index("x"), lax.axis_index("y"), lax.axis_index("c")
        copies = []
        for i in range(n):
            cp = pltpu.make_async_remote_copy(src_ref=ins[i], dst_ref=outs[i], send_sem=send_sems.at[i], recv_sem=recv_sems.at[i],
                                              device_id=(mx, my, 1 - mc), device_id_type=_MESH)
            cp.start()
            copies.append(cp)
        for cp in copies:
            cp.wait()

    return pl.pallas_call(
        body, out_shape=[jax.ShapeDtypeStruct(t.shape, t.dtype) for t in tots], in_specs=[_ANY] * n, out_specs=[_ANY] * n,
        scratch_shapes=[pltpu.SemaphoreType.DMA((n,)), pltpu.SemaphoreType.DMA((n,))], name=name)(*tots)


def _pair_sum(g, recv, view, c_idx, *, name):
    def body(c_ref, a_ref, b_ref, o_ref):
        o_ref[...] = (a_ref[...] + b_ref[...]).astype(WIRE_DTYPE)

    if view == "chip":
        nch, r, c = g.shape
        tr = _row_tile(r // 2, c * 4, 16)
        gv = g.reshape(nch, 2, r // 2, c)
        grid = (nch, (r // 2) // tr)
        in_specs = [pl.BlockSpec((None, None, tr, c), lambda j, i, c_ref: (j, c_ref[0], i, 0)),
                    pl.BlockSpec((None, tr, c), lambda j, i, c_ref: (j, i, 0))]
        out_spec = pl.BlockSpec((None, tr, c), lambda j, i, c_ref: (j, i, 0))
        sem = ("parallel", "parallel")
    else:
        r, c4 = g.shape
        tr = _row_tile(r // 2, c4 * 4, 16)
        gv = g.reshape(2, r // 2, c4)
        grid = ((r // 2) // tr,)
        in_specs = [pl.BlockSpec((None, tr, c4), lambda i, c_ref: (c_ref[0], i, 0)), pl.BlockSpec((tr, c4), lambda i, c_ref: (i, 0))]
        out_spec = pl.BlockSpec((tr, c4), lambda i, c_ref: (i, 0))
        sem = ("parallel",)
    grid_spec = pltpu.PrefetchScalarGridSpec(num_scalar_prefetch=1, grid=grid, in_specs=in_specs, out_specs=out_spec)
    return pl.pallas_call(body, grid_spec=grid_spec, out_shape=jax.ShapeDtypeStruct(recv.shape, WIRE_DTYPE),
                          compiler_params=_params(*sem), name=name)(c_idx, gv, recv)


def _quad_sum(gs, recvs, quads, view, chip_idx, c_idx, *, name):
    nl = len(quads)
    nch, rh, c = quads[0].shape
    tr = _row_tile(rh, c * 4, 16)

    def body(_, __, *refs):
        o_ref = refs[-1]
        per = nch + 1
        for l in range(nl):
            grp = refs[l * per:(l + 1) * per]
            acc = grp[0][...] + grp[1][...]
            for r in grp[2:]:
                acc = acc + r[...].astype(F32)
            o_ref[l] = acc

    if view == "chip":
        own = [pl.BlockSpec((None, None, tr, c), lambda i, j, h: (j[0], h[0], i, 0)),
               pl.BlockSpec((None, tr, c), lambda i, j, h: (j[0], i, 0))]
        gviews = [g.reshape(nch, 2, rh, c) for g in gs]
    else:
        own = [pl.BlockSpec((None, tr, c), lambda i, j, h: (h[0], i, j[0])), pl.BlockSpec((tr, c), lambda i, j, h: (i, j[0]))]
        gviews = [g.reshape(2, rh, nch * c) for g in gs]
    assert nch & (nch - 1) == 0
    got = [pl.BlockSpec((None, tr, c), functools.partial(lambda i, j, h, k: ((j[0] + k) & (nch - 1), i, 0), k=k))
           for k in range(1, nch)]
    ins = []
    for l in range(nl):
        ins += [gviews[l], recvs[l]] + [quads[l]] * (nch - 1)
    grid_spec = pltpu.PrefetchScalarGridSpec(
        num_scalar_prefetch=2, grid=(rh // tr,), in_specs=(own + got) * nl,
        out_specs=pl.BlockSpec((nl, tr, c), lambda i, j, h: (0, i, 0)))
    return pl.pallas_call(body, grid_spec=grid_spec, out_shape=jax.ShapeDtypeStruct((nl, rh, c), F32),
                          compiler_params=_params("parallel"), name=name)(chip_idx, c_idx, *ins)


def _sum_devices(g8, own, dev_idx, *, name):
    k, rows, cols = g8.shape

    def body(d_ref, a_ref, x_ref, o_ref):
        acc = None
        for i in range(k):
            term = jnp.where(d_ref[0] == i, x_ref[...], a_ref[i])
            acc = term if acc is None else acc + term
        o_ref[...] = acc

    grid_spec = pltpu.PrefetchScalarGridSpec(
        num_scalar_prefetch=1, grid=(1,),
        in_specs=[pl.BlockSpec((k, rows, cols), lambda i, d_ref: (0, 0, 0)), pl.BlockSpec((rows, cols), lambda i, d_ref: (0, 0))],
        out_specs=pl.BlockSpec((rows, cols), lambda i, d_ref: (0, 0)))
    return pl.pallas_call(body, grid_spec=grid_spec, out_shape=jax.ShapeDtypeStruct((rows, cols), g8.dtype),
                          compiler_params=_params("arbitrary"), name=name)(dev_idx, g8, own)


def _adamw(w, g, m, v, *, name):
    rows, cols = w.shape
    tr = rows
    for cand in (256, 128, 64, 32, 16, 8):
        if rows % cand == 0 and cand * cols <= 512 * 1024:
            tr = cand
            break
    c1 = 1.0 - ADAM_B1 ** ADAM_STEP
    c2 = 1.0 - ADAM_B2 ** ADAM_STEP

    def body(w_ref, g_ref, m_ref, v_ref, d_ref, nm_ref, nv_ref):
        gv = g_ref[...]
        nm = ADAM_B1 * m_ref[...] + (1.0 - ADAM_B1) * gv
        nv = ADAM_B2 * v_ref[...] + (1.0 - ADAM_B2) * (gv * gv)
        d_ref[...] = -ADAM_LR * ((nm / c1) / (jnp.sqrt(nv / c2) + ADAM_EPS) + ADAM_WD * w_ref[...])
        nm_ref[...] = nm
        nv_ref[...] = nv

    spec = pl.BlockSpec((tr, cols), lambda i: (i, 0))
    shp = jax.ShapeDtypeStruct((rows, cols), F32)
    return pl.pallas_call(body, grid=(rows // tr,), in_specs=[spec] * 4, out_specs=[spec] * 3, out_shape=[shp] * 3,
                          compiler_params=_params("parallel"), name=name)(w, g, m, v)


def _adamw_halves(w, m, v, mine, other, c_idx, *, name):
    nl, r, c = w.shape
    rh = r // 2
    tr = _row_tile(rh, c * 4)
    c1 = 1.0 - ADAM_B1 ** ADAM_STEP
    c2 = 1.0 - ADAM_B2 ** ADAM_STEP

    def body(c_ref, w_ref, m_ref, v_ref, a_ref, b_ref, g_ref, d_ref, nm_ref, nv_ref):
        gv = jnp.where(pl.program_id(1) == c_ref[0], a_ref[...], b_ref[...])
        nm = ADAM_B1 * m_ref[...] + (1.0 - ADAM_B1) * gv
        nv = ADAM_B2 * v_ref[...] + (1.0 - ADAM_B2) * (gv * gv)
        g_ref[...] = gv
        d_ref[...] = -ADAM_LR * ((nm / c1) / (jnp.sqrt(nv / c2) + ADAM_EPS) + ADAM_WD * w_ref[...])
        nm_ref[...] = nm
        nv_ref[...] = nv

    full = pl.BlockSpec((None, None, tr, c), lambda l, h, i, c_ref: (l, h, i, 0))
    half = pl.BlockSpec((None, tr, c), lambda l, h, i, c_ref: (l, i, 0))
    grid_spec = pltpu.PrefetchScalarGridSpec(num_scalar_prefetch=1, grid=(nl, 2, rh // tr),
                                             in_specs=[full] * 3 + [half] * 2, out_specs=[full] * 4)
    shp = jax.ShapeDtypeStruct((nl, 2, rh, c), F32)
    view = (nl, 2, rh, c)
    outs = pl.pallas_call(body, grid_spec=grid_spec, out_shape=[shp] * 4, compiler_params=_params("parallel", "parallel", "parallel"),
                          name=name)(c_idx, w.reshape(view), m.reshape(view), v.reshape(view), mine, other)
    return [o.reshape(nl, r, c) for o in outs]


WEIGHTS = ["mem_ln_g", "mem_ln_b", "w_in", "sg_ln_g", "sg_ln_b", "sg_w", "sg_b", "conv_w", "conv_b", "dt_bias", "a_log",
           "d_skip", "ssm_norm_g", "p_a", "p_b", "w_mix_o", "w_xq", "w_xkv", "w_xo", "w_ffn_in", "w_ffn_out", "ln_g", "ln_b"]
ARG_NAMES = ["x", "mem"] + WEIGHTS + ["loss_target"] + ["m_" + n for n in WEIGHTS] + ["v_" + n for n in WEIGHTS]
BIG = {"w_in": (1, (1024, 9248)), "p_a": (0, (1024, 1024)), "p_b": (0, (2048, 1024)), "w_mix_o": (0, (1024, 1024)),
       "w_xq": (0, (1024, 1024)), "w_xkv": (1, (1024, 2048)), "w_xo": (0, (1024, 1024)), "w_ffn_in": (1, (1024, 5632)),
       "w_ffn_out": (0, (2816, 1024))}
SMALL_SHARDED = {"conv_w": (4, 3072), "ln_g": (3, 1024), "ln_b": (3, 1024)}
SMALL = [n for n in WEIGHTS if n not in BIG]
W_IN_MAP = ((0, 4096, "main", 0), (4096, 7168, "main", XBC_COL0), (7168, 7200, "dt", 0), (7200, 9248, "main", GAB_COL0))
W_IN_SHARD = 9248 // N_CHIPS


def _w_in_chip_major(gm, gd):
    src = {"main": gm, "dt": gd}
    blocks = []
    for j in range(N_CHIPS):
        lo, hi = j * W_IN_SHARD, (j + 1) * W_IN_SHARD
        parts = [src[k][:, o + max(lo, a) - a:o + min(hi, b) - a] for a, b, k, o in W_IN_MAP if max(lo, a) < min(hi, b)]
        blocks.append(jnp.concatenate(parts, axis=1))
    return jnp.stack(blocks)


def _w_in_reassemble(wc):
    def cols(a, b):
        out = []
        for j in range(N_CHIPS):
            lo, hi = max(a, j * W_IN_SHARD), min(b, (j + 1) * W_IN_SHARD)
            if lo < hi:
                out.append(wc[j][:, lo - j * W_IN_SHARD:hi - j * W_IN_SHARD])
        return out

    main = sorted((m for m in W_IN_MAP if m[2] == "main"), key=lambda m: m[3])
    w_main = jnp.concatenate([p for a, b, _, _ in main for p in cols(a, b)], axis=1)
    (a, b, _, _), = [m for m in W_IN_MAP if m[2] == "dt"]
    w_dt = jnp.pad(jnp.concatenate(cols(a, b), axis=1), ((0, 0), (0, HEAD_PAD - (b - a))))
    return w_main, w_dt
GATHER_KIND = {"w_in": "chip", "p_a": "row", "p_b": "row", "w_mix_o": "row", "w_xq": "row", "w_xkv": "col", "w_xo": "row",
               "w_ffn_in": "col", "w_ffn_out": "row", "conv_w": "chip", "ln_g": "chip", "ln_b": "chip"}
GRAD_VIEW = {n: ("col" if k == "col" else "chip") for n, k in GATHER_KIND.items() if n in BIG}


def _shard_shape(name):
    axis, (r, c) = BIG[name]
    return (r // N_CHIPS, c) if axis == 0 else (r, c // N_CHIPS)


def _pad_rows(flat, cols, row_mult):
    n = flat.shape[0]
    rows = -(-n // cols)
    rows = -(-rows // row_mult) * row_mult
    return jnp.pad(flat, (0, rows * cols - n)).reshape(rows, cols)


def _gather_small_params(a, chip):
    names = list(SMALL_SHARDED)
    kinds = [GATHER_KIND[n] for n in names]
    bufs = [_cast_place(a[n], GATHER_KIND[n], F32, chip.reshape(1), name=f"place_{n}") for n in names]
    outs = _gather_params(bufs, [a[n].shape[1:] for n in names], kinds, name="gather_small_params")
    full = {}
    for n, o in zip(names, outs):
        _, _, r, c = o.shape
        full[n] = jnp.transpose(o, (0, 2, 1, 3)).reshape(DEPTH, r, N_CHIPS * c)
    return full


GATHER_GROUPS = (("w_in",), tuple(n for n in BIG if n != "w_in"))


def _gather_group_start(a, l, names, chip, after, *, tag):
    bufs = [_cast_place_layer(a[n], l, GATHER_KIND[n], chip.reshape(1), after, name=f"place_{n}_l{l}") for n in names]
    return _gather_start(bufs, [a[n].shape[1:] for n in names], [GATHER_KIND[n] for n in names], name=f"gather_start_{tag}")


def _gather_group_finish(a, names, flight, after, *, tag):
    send_sems, recv_sems, bufs, token = flight
    shapes, kinds = [a[n].shape[1:] for n in names], [GATHER_KIND[n] for n in names]
    bufs = _gather_wait(send_sems, recv_sems, bufs, shapes, kinds, token if after is None else after, name=f"gather_wait_{tag}")
    full = dict(zip(names, _gather_forward(bufs, shapes, kinds, name=f"gather_forward_{tag}")))
    if "w_in" in full:
        full["w_main"], full["w_dt"] = _w_in_reassemble(full.pop("w_in"))
    return full


def _layer_weights(a, big, small, l):
    w = dict(big)
    for n in SMALL_SHARDED:
        w[n] = small[n][l]
    for n in ["sg_ln_g", "sg_ln_b", "sg_w", "conv_b", "ssm_norm_g"]:
        w[n] = a[n][l]
    w["sg_bcol"] = a["sg_b"][l][..., None]
    for n in ["dt_bias", "a_log"]:
        w[n + "8"] = _pad_heads(a[n][l])
    w["d_skipx"] = _expand_heads(a["d_skip"][l])
    return w


def _grad_views(grads, names):
    gs = []
    for n in names:
        axis, _ = BIG[n]
        r, c = _shard_shape(n)
        if n == "w_in":
            gs.append(_w_in_chip_major(grads["w_main"], grads["w_dt"]))
        elif axis == 0:
            gs.append(grads[n].reshape(N_CHIPS, r, c))
        else:
            gs.append(grads[n])
    return gs


class _GradExchange:
    def __init__(self, grads, names, c_idx, tag):
        self.names, self.c_idx, self.tag = names, c_idx, tag
        self.views = [GRAD_VIEW[n] for n in names]
        self.gs = _grad_views(grads, names)

    def start(self):
        self.sems = _grads_to_sibling_start(self.gs, self.views, name=f"grads_to_sibling_start_{self.tag}")
        return self.sems[4]

    def cross(self, after):
        send_sems, recv_sems, gs, lands, token = self.sems
        self.gs, self.recv = _grads_to_sibling_wait(send_sems, recv_sems, gs, lands, self.views, token if after is None else after,
                                                    name=f"grads_to_sibling_wait_{self.tag}")
        cpre = self.c_idx.reshape(1)
        pairs = [_pair_sum(g, rv, v, cpre, name=f"grads_pair_sum_{n}_{self.tag}")
                 for g, rv, v, n in zip(self.gs, self.recv, self.views, self.names)]
        self.sems = _grads_to_chips_start(pairs, self.views, name=f"grads_to_chips_start_{self.tag}")
        return self.sems[4]

    def finish(self, after):
        send_sems, recv_sems, pairs, lands, _ = self.sems
        quads = _grads_to_chips_wait(send_sems, recv_sems, pairs, lands, self.views, after, name=f"grads_to_chips_wait_{self.tag}")
        return {n: (g, rv, q) for n, g, rv, q in zip(self.names, self.gs, self.recv, quads)}


def _finish_big_grads(parts, c_idx, chip):
    tots = [_quad_sum([parts[l][n][0] for l in range(DEPTH)], [parts[l][n][1] for l in range(DEPTH)],
                      [parts[l][n][2] for l in range(DEPTH)], GRAD_VIEW[n], chip.reshape(1), c_idx.reshape(1),
                      name=f"grads_chip_sum_{n}") for n in BIG]
    others = _grads_share(tots, name="grads_share")
    return {n: (t, o) for n, t, o in zip(BIG, tots, others)}


def _direct_copies(x_ref, land_ref, send_sems, recv_sems):
    mx, my, mc = lax.axis_index("x"), lax.axis_index("y"), lax.axis_index("c")
    me = 4 * mx + 2 * my + mc
    copies = []
    for k in range(N_DEV - 1):
        f = k + 1
        to = (mx ^ (f >> 2 & 1), my ^ (f >> 1 & 1), mc ^ (f & 1))
        copies.append(pltpu.make_async_remote_copy(src_ref=x_ref, dst_ref=land_ref.at[me], send_sem=send_sems.at[k],
                                                   recv_sem=recv_sems.at[k], device_id=to, device_id_type=_MESH))
    return copies


def _all_gather8_start(x, *, name):
    land = pltpu.with_memory_space_constraint(lax.empty((N_DEV,) + x.shape, x.dtype), pltpu.HBM)

    def body(x_ref, land_ref, send_sems, recv_sems, x_out, land_out, token):
        for cp in _direct_copies(x_ref, land_ref, send_sems, recv_sems):
            cp.start()
        token[...] = jnp.zeros_like(token)

    n = N_DEV - 1
    return pl.pallas_call(
        body, name=name,
        out_shape=(pltpu.SemaphoreType.DMA((n,)), pltpu.SemaphoreType.DMA((n,)), pltpu.HBM(x.shape, x.dtype),
                   pltpu.HBM(land.shape, land.dtype), jax.ShapeDtypeStruct((SUBLANE, LANE), F32)),
        in_specs=[_HBM, _HBM], out_specs=(_SEM, _SEM, _HBM, _HBM, pl.BlockSpec(memory_space=pltpu.VMEM)),
        input_output_aliases={0: 2, 1: 3}, compiler_params=pltpu.CompilerParams(has_side_effects=_EFFECT),
    )(pltpu.with_memory_space_constraint(x, pltpu.HBM), land)


def _all_gather8_wait(send_sems, recv_sems, x, land, after, *, name):
    def body(x_ref, land_ref, s_sems, r_sems, _, x_out, land_out):
        for cp in _direct_copies(x_ref, land_ref, s_sems, r_sems):
            cp.wait_send()
            cp.wait_recv()

    return pl.pallas_call(
        body, name=name, out_shape=(pltpu.HBM(x.shape, x.dtype), pltpu.HBM(land.shape, land.dtype)),
        in_specs=[_HBM, _HBM, _SEM, _SEM, _ANY], out_specs=(_HBM, _HBM), input_output_aliases={0: 0, 1: 1},
        compiler_params=pltpu.CompilerParams(has_side_effects=_EFFECT),
    )(x, land, send_sems, recv_sems, after)


def _pack_small(small):
    return _pad_rows(jnp.concatenate([small[n].reshape(-1) for n in small]), LANE, SUBLANE)


def _unpack_small(small, g8, packed, chip, c_idx, *, name):
    names = list(small)
    tot = _sum_devices(g8, packed, (2 * chip + c_idx).reshape(1), name=name).reshape(-1)
    out, off = {}, 0
    for n in names:
        sz = small[n].size
        full = tot[off:off + sz].reshape(small[n].shape)
        off += sz
        if n in SMALL_SHARDED:
            cs = SMALL_SHARDED[n][1] // N_CHIPS
            full = lax.dynamic_slice_in_dim(full, chip * cs, cs, axis=-1)
        out[n] = full
    return out


def kernel(x, mem, mem_ln_g, mem_ln_b, w_in, sg_ln_g, sg_ln_b, sg_w, sg_b, conv_w, conv_b, dt_bias, a_log, d_skip, ssm_norm_g, p_a, p_b, w_mix_o, w_xq, w_xkv, w_xo, w_ffn_in, w_ffn_out, ln_g, ln_b, loss_target, m_mem_ln_g, m_mem_ln_b, m_w_in, m_sg_ln_g, m_sg_ln_b, m_sg_w, m_sg_b, m_conv_w, m_conv_b, m_dt_bias, m_a_log, m_d_skip, m_ssm_norm_g, m_p_a, m_p_b, m_w_mix_o, m_w_xq, m_w_xkv, m_w_xo, m_w_ffn_in, m_w_ffn_out, m_ln_g, m_ln_b, v_mem_ln_g, v_mem_ln_b, v_w_in, v_sg_ln_g, v_sg_ln_b, v_sg_w, v_sg_b, v_conv_w, v_conv_b, v_dt_bias, v_a_log, v_d_skip, v_ssm_norm_g, v_p_a, v_p_b, v_w_mix_o, v_w_xq, v_w_xkv, v_w_xo, v_w_ffn_in, v_w_ffn_out, v_ln_g, v_ln_b):
    a = dict(zip(ARG_NAMES, (x, mem, mem_ln_g, mem_ln_b, w_in, sg_ln_g, sg_ln_b, sg_w, sg_b, conv_w, conv_b, dt_bias, a_log, d_skip, ssm_norm_g, p_a, p_b, w_mix_o, w_xq, w_xkv, w_xo, w_ffn_in, w_ffn_out, ln_g, ln_b, loss_target, m_mem_ln_g, m_mem_ln_b, m_w_in, m_sg_ln_g, m_sg_ln_b, m_sg_w, m_sg_b, m_conv_w, m_conv_b, m_dt_bias, m_a_log, m_d_skip, m_ssm_norm_g, m_p_a, m_p_b, m_w_mix_o, m_w_xq, m_w_xkv, m_w_xo, m_w_ffn_in, m_w_ffn_out, m_ln_g, m_ln_b, v_mem_ln_g, v_mem_ln_b, v_w_in, v_sg_ln_g, v_sg_ln_b, v_sg_w, v_sg_b, v_conv_w, v_conv_b, v_dt_bias, v_a_log, v_d_skip, v_ssm_norm_g, v_p_a, v_p_b, v_w_mix_o, v_w_xq, v_w_xkv, v_w_xo, v_w_ffn_in, v_w_ffn_out, v_ln_g, v_ln_b)))
    c_idx = lax.axis_index("c").astype(jnp.int32)
    chip = (2 * lax.axis_index("x") + lax.axis_index("y")).astype(jnp.int32)

    small = _gather_small_params(a, chip)
    ga, gb = GATHER_GROUPS
    flights = {(0, 0): _gather_group_start(a, 0, ga, chip, None, tag="l0_a")}
    flights[0, 1] = _gather_group_start(a, 0, gb, chip, flights[0, 0][3], tag="l0_b")

    def layer_weights(after, l):
        first = _gather_group_finish(a, ga, flights[l, 0], after if l else flights[l, 1][3], tag=f"l{l}_a")

        def rest(w, after_b):
            more = _gather_group_finish(a, gb, flights[l, 1], after_b, tag=f"l{l}_b")
            if l + 1 < DEPTH:
                flights[l + 1, 0] = _gather_group_start(a, l + 1, ga, chip, more["p_a"], tag=f"l{l + 1}_a")
                flights[l + 1, 1] = _gather_group_start(a, l + 1, gb, chip, flights[l + 1, 0][3], tag=f"l{l + 1}_b")
                more["p_a"] = more["p_a"] + flights[l + 1, 1][3][0, 0].astype(MXU_DTYPE)
            return {k: v for k, v in {**w, **more}.items() if k != "rest"}

        return dict(_layer_weights(a, first, small, l), rest=rest)

    layers = [functools.partial(layer_weights, l=l) for l in range(DEPTH)]
    exchanges, seen, small_flight = [], {}, {}

    def start_exchange(l, names, grads_l):
        ex = _GradExchange(grads_l, names, c_idx, f"l{l}_{names[0]}")
        tokens = [ex.start()]
        if exchanges:
            tokens.append(exchanges[-1][1].cross(tokens[0]))
        exchanges.append((l, ex))
        seen[l] = grads_l
        if l == 0 and names == GRAD_GROUPS[-1]:
            tokens.append(ex.cross(None))
            small = {}
            for n in SMALL:
                if n.startswith("mem_ln"):
                    continue
                per_layer = []
                for k in range(DEPTH):
                    g = seen[k][n]
                    if n in ("dt_bias", "a_log", "d_skip"):
                        g = g[0, :SSM_HEADS]
                    per_layer.append(g.reshape(a[n].shape[1:-1] + (-1,)))
                small[n] = jnp.stack(per_layer)
            small_flight["small"] = small
            small_flight["sems"] = _all_gather8_start(_pack_small(small), name="gather_small_grads_start")
            tokens.append(small_flight["sems"][4])
        return sum(tokens[1:], tokens[0])

    lsum, grad_x, grads, d_mem_g, d_mem_b = _local_step(x, mem, loss_target, mem_ln_g, mem_ln_b, layers, start_exchange)
    loss = lax.psum(0.5 * jnp.sum(lsum) / D_MODEL, ("x", "y", "c"))

    parts = [{} for _ in range(DEPTH)]
    for l, ex in exchanges:
        parts[l].update(ex.finish(grad_x))
    halves = _finish_big_grads(parts, c_idx, chip)
    gw = {}
    send_sems, recv_sems, packed, land, _ = small_flight["sems"]
    packed, g8 = _all_gather8_wait(send_sems, recv_sems, packed, land, grad_x, name="gather_small_grads_wait")
    gw.update(_unpack_small(small_flight["small"], g8, packed, chip, c_idx, name="small_grads_sum"))
    mem_small = {"mem_ln_g": d_mem_g, "mem_ln_b": d_mem_b}
    mem_packed = _pack_small(mem_small)
    gw.update(_unpack_small(mem_small, _all_gather8(mem_packed, name="gather_mem_ln_grads"), mem_packed, chip, c_idx,
                            name="mem_ln_grads_sum"))

    delta, new_m, new_v = {}, {}, {}
    for n in BIG:
        mine, other = halves[n]
        gw[n], delta[n], new_m[n], new_v[n] = _adamw_halves(a[n], a["m_" + n], a["v_" + n], mine, other, c_idx.reshape(1),
                                                             name=f"adamw_{n}")
    for n in SMALL:
        shp = a[n].shape
        view = (-1, LANE) if a[n].size % LANE == 0 else (1, -1)
        outs = _adamw(*[v.reshape(view) for v in (a[n], gw[n], a["m_" + n], a["v_" + n])], name=f"adamw_{n}")
        delta[n], new_m[n], new_v[n] = (o.reshape(shp) for o in outs)
    return (loss, grad_x, *[gw[n].reshape(a[n].shape) for n in WEIGHTS], *[delta[n] for n in WEIGHTS],
            *[new_m[n] for n in WEIGHTS], *[new_v[n] for n in WEIGHTS])
```

```python
import functools
import math

import jax
import jax.numpy as jnp
from jax import lax
from jax.experimental import pallas as pl
from jax.experimental.pallas import tpu as pltpu

F32 = jnp.float32
MXU_DTYPE = jnp.bfloat16
WIRE_DTYPE = jnp.bfloat16

D_MODEL = 1024
DEPTH = 2
CHUNK = 128
SG_GROUPS = 8
SSM_INNER = 2048
SSM_HEADDIM = 64
SSM_HEADS = 32
SSM_STATE = 128
SSM_GROUPS = 4
SSM_CONV = 4
SSM_CONV_DIM = 3072
X_HEADS = 4
X_HEADDIM = 256
FFN_HIDDEN = 2816
ALPHA = float((2 * DEPTH) ** 0.25)
LN_EPS = 1e-5
RMS_EPS = 1e-5
ADAM_LR = 0.001
ADAM_B1 = 0.9
ADAM_B2 = 0.999
ADAM_EPS = 1e-08
ADAM_WD = 0.01
ADAM_STEP = 10

MAIN_COLS = 9216
UVZ_COLS = 4096
GAB_COL0 = 4096
XBC_COL0 = 6144
HEAD_PAD = 128

VMEM_LIMIT = 56 * 1024 * 1024
BLOCK_BYTES = 2 * 1024 * 1024
ROW_TILES = (512, 256, 128)
LANE = 128
SUBLANE = 8

N_CHIPS = 4
N_DEV = 8


def _pick(n, cands):
    for c in cands:
        if n % c == 0:
            return c
    return n


MM_TILE_MAX = 1408
MM_OPERAND_BYTES = 8 * 1024 * 1024


def _div_tile(n, limit):
    best = None
    for t in range(LANE, min(n, limit) + 1, LANE):
        if n % t == 0:
            best = t
    return n if best is None else best


def _params(*sem):
    return pltpu.CompilerParams(dimension_semantics=tuple(sem), vmem_limit_bytes=VMEM_LIMIT)


_ANY = pl.BlockSpec(memory_space=pl.ANY)
_MESH = pl.DeviceIdType.MESH


def _nt(a, b):
    return lax.dot_general(a, b, (((1,), (1,)), ((), ())), preferred_element_type=F32)


def _tn(a, b):
    return lax.dot_general(a, b, (((0,), (0,)), ((), ())), preferred_element_type=F32)


def _nn(a, b):
    return jnp.dot(a, b, preferred_element_type=F32)


def _sigmoid(x):
    return 0.5 * jnp.tanh(0.5 * x) + 0.5


def _split3(v):
    def top(x):
        bits = lax.bitcast_convert_type(x, jnp.uint32) & jnp.uint32(0xFFFF0000)
        return lax.bitcast_convert_type(bits, F32)

    v1 = top(v)
    r1 = v - v1
    v2 = top(r1)
    v3 = r1 - v2
    return v1.astype(jnp.bfloat16), v2.astype(jnp.bfloat16), v3.astype(jnp.bfloat16)


def _dot_exact(a, b, dn, data):
    if data == 0:
        mat = b.astype(jnp.bfloat16)
        return sum(lax.dot_general(p, mat, dn, preferred_element_type=F32) for p in _split3(a))
    mat = a.astype(jnp.bfloat16)
    return sum(lax.dot_general(mat, p, dn, preferred_element_type=F32) for p in _split3(b))


_DN_NN = (((1,), (0,)), ((), ()))
_DN_TN = (((0,), (0,)), ((), ()))


def _gelu(x):
    return 0.5 * x * (1.0 + lax.erf(x * (2.0 ** -0.5)))


def _gelu_grad(x):
    return 0.5 * (1.0 + lax.erf(x * (2.0 ** -0.5))) + x * jnp.exp(-0.5 * x * x) * (1.0 / math.sqrt(2.0 * math.pi))


def _mm(a, b, *, ta=False, tb=False, out_dtype=F32, after=None, name):
    if ta:
        kdim, m = a.shape
    else:
        m, kdim = a.shape
    if tb:
        n, k2 = b.shape[-2:]
    else:
        k2, n = b.shape[-2:]
    assert kdim == k2, (a.shape, b.shape, ta, tb)
    tm = _div_tile(m, MM_TILE_MAX)
    tn = _div_tile(n, MM_TILE_MAX)
    tk = _div_tile(kdim, MM_OPERAND_BYTES // (tm * a.dtype.itemsize + tn * b.dtype.itemsize))
    nk = kdim // tk
    dn = (((0 if ta else 1,), (1 if tb else 0,)), ((), ()))

    extra = [] if after is None else [after]

    def body(a_ref, b_ref, *rest):
        o_ref = rest[len(extra)]
        d = lax.dot_general(a_ref[...].astype(MXU_DTYPE), b_ref[...].astype(MXU_DTYPE), dn, preferred_element_type=F32)
        if nk == 1:
            o_ref[...] = d.astype(out_dtype)
            return
        acc_ref = rest[len(extra) + 1]
        k = pl.program_id(2)

        @pl.when(k == 0)
        def _():
            acc_ref[...] = d

        @pl.when(jnp.logical_and(k > 0, k < nk - 1))
        def _():
            acc_ref[...] += d

        @pl.when(k == nk - 1)
        def _():
            o_ref[...] = (acc_ref[...] + d).astype(out_dtype)

    a_spec = pl.BlockSpec((tk, tm), lambda i, j, k: (k, i)) if ta else pl.BlockSpec((tm, tk), lambda i, j, k: (i, k))
    b_spec = pl.BlockSpec((tn, tk), lambda i, j, k: (j, k)) if tb else pl.BlockSpec((tk, tn), lambda i, j, k: (k, j))
    return pl.pallas_call(
        body, grid=(m // tm, n // tn, nk), in_specs=[a_spec, b_spec] + [_ANY] * len(extra),
        out_specs=pl.BlockSpec((tm, tn), lambda i, j, k: (i, j)),
        out_shape=jax.ShapeDtypeStruct((m, n), out_dtype),
        scratch_shapes=[pltpu.VMEM((tm, tn), F32)] if nk > 1 else [],
        compiler_params=_params("parallel", "parallel", "arbitrary"), name=name)(a, b, *extra)


def _row_spec(tm, c, col=0):
    return pl.BlockSpec((tm, c), lambda i: (i, col))


def _par_spec(shape):
    nd = len(shape)
    return pl.BlockSpec(shape, lambda i: (0,) * nd)


def _ln_fwd(x, f, g, b, *, name):
    t, c = x.shape
    tm = _pick(t, ROW_TILES)
    has_f = f is not None

    def body(*refs):
        if has_f:
            x_ref, f_ref, g_ref, b_ref, y_ref, yb_ref, xh_ref, rs_ref = refs
            r = ALPHA * x_ref[...] + f_ref[...]
        else:
            x_ref, g_ref, b_ref, y_ref, yb_ref, xh_ref, rs_ref = refs
            r = x_ref[...]
        mu = jnp.mean(r, axis=-1, keepdims=True)
        xc = r - mu
        var = jnp.mean(xc * xc, axis=-1, keepdims=True)
        rstd = lax.rsqrt(var + LN_EPS)
        xh = xc * rstd
        y = xh * g_ref[...] + b_ref[...]
        y_ref[...] = y
        yb_ref[...] = y.astype(MXU_DTYPE)
        xh_ref[...] = xh
        rs_ref[...] = jnp.broadcast_to(rstd, rs_ref.shape)

    ins = [x] + ([f] if has_f else []) + [g.reshape(1, c), b.reshape(1, c)]
    in_specs = [_row_spec(tm, c)] * (2 if has_f else 1) + [_par_spec((1, c))] * 2
    return pl.pallas_call(
        body, grid=(t // tm,), in_specs=in_specs,
        out_specs=[_row_spec(tm, c), _row_spec(tm, c), _row_spec(tm, c), _row_spec(tm, LANE)],
        out_shape=[jax.ShapeDtypeStruct((t, c), F32), jax.ShapeDtypeStruct((t, c), MXU_DTYPE),
                   jax.ShapeDtypeStruct((t, c), F32), jax.ShapeDtypeStruct((t, LANE), F32)],
        compiler_params=_params("parallel"), name=name)(*ins)


def _ln_bwd(addends, scales, xh, rs, g, *, name):
    t, c = xh.shape
    tm = _pick(t, ROW_TILES)
    na = len(addends)

    def body(*refs):
        a_refs = refs[:na]
        xh_ref, rs_ref, g_ref, dp_ref, dpb_ref, dg_ref, db_ref = refs[na:]

        @pl.when(pl.program_id(0) == 0)
        def _():
            dg_ref[...] = jnp.zeros_like(dg_ref)
            db_ref[...] = jnp.zeros_like(db_ref)

        dy = None
        for s, r in zip(scales, a_refs):
            term = r[...] if s == 1.0 else s * r[...]
            dy = term if dy is None else dy + term
        xhv = xh_ref[...]
        dxh = dy * g_ref[...]
        m1 = jnp.mean(dxh, axis=-1, keepdims=True)
        m2 = jnp.mean(dxh * xhv, axis=-1, keepdims=True)
        dp = rs_ref[:, 0:1] * (dxh - m1 - xhv * m2)
        dp_ref[...] = dp
        dpb_ref[...] = dp.astype(MXU_DTYPE)
        dg_ref[...] += jnp.sum(dy * xhv, axis=0, keepdims=True)
        db_ref[...] += jnp.sum(dy, axis=0, keepdims=True)

    in_specs = [_row_spec(tm, c)] * (na + 1) + [_row_spec(tm, LANE), _par_spec((1, c))]
    return pl.pallas_call(
        body, grid=(t // tm,), in_specs=in_specs,
        out_specs=[_row_spec(tm, c), _row_spec(tm, c), _par_spec((1, c)), _par_spec((1, c))],
        out_shape=[jax.ShapeDtypeStruct((t, c), F32), jax.ShapeDtypeStruct((t, c), MXU_DTYPE),
                   jax.ShapeDtypeStruct((1, c), F32), jax.ShapeDtypeStruct((1, c), F32)],
        compiler_params=_params("arbitrary"), name=name)(*addends, xh, rs, g.reshape(1, c))


def _add_scaled(addends, scales, *, name):
    t, c = addends[0].shape
    tm = _pick(t, ROW_TILES)
    na = len(addends)

    def body(*refs):
        acc = None
        for s, r in zip(scales, refs[:na]):
            term = r[...] if s == 1.0 else s * r[...]
            acc = term if acc is None else acc + term
        refs[na][...] = acc

    return pl.pallas_call(
        body, grid=(t // tm,), in_specs=[_row_spec(tm, c)] * na, out_specs=_row_spec(tm, c),
        out_shape=jax.ShapeDtypeStruct((t, c), F32), compiler_params=_params("parallel"), name=name)(*addends)


def _loss_head(y, tgt, *, name):
    t, c = y.shape
    tm = _pick(t, ROW_TILES)

    def body(y_ref, t_ref, dy_ref, ls_ref):
        @pl.when(pl.program_id(0) == 0)
        def _():
            ls_ref[...] = jnp.zeros_like(ls_ref)

        e = y_ref[...] - t_ref[...]
        dy_ref[...] = e * (1.0 / c)
        ls_ref[...] += jnp.sum(e * e, axis=0, keepdims=True)

    return pl.pallas_call(
        body, grid=(t // tm,), in_specs=[_row_spec(tm, c)] * 2,
        out_specs=[_row_spec(tm, c), _par_spec((1, c))],
        out_shape=[jax.ShapeDtypeStruct((t, c), F32), jax.ShapeDtypeStruct((1, c), F32)],
        compiler_params=_params("arbitrary"), name=name)(y, tgt)


def _swiglu_fwd(h, *, name):
    t, two_f = h.shape
    fh = two_f // 2
    tm = _pick(t, (256, 128))

    def body(g_ref, u_ref, a_ref):
        g = g_ref[...]
        a_ref[...] = (g * _sigmoid(g) * u_ref[...]).astype(MXU_DTYPE)

    return pl.pallas_call(
        body, grid=(t // tm,), in_specs=[_row_spec(tm, fh, 0), _row_spec(tm, fh, 1)], out_specs=_row_spec(tm, fh),
        out_shape=jax.ShapeDtypeStruct((t, fh), MXU_DTYPE), compiler_params=_params("parallel"), name=name)(h, h)


def _swiglu_bwd(h, da, *, name):
    t, two_f = h.shape
    fh = two_f // 2
    tm = _pick(t, (256, 128))

    def body(g_ref, u_ref, da_ref, dh_ref):
        g = g_ref[...]
        s = _sigmoid(g)
        dav = da_ref[...]
        dh_ref[:, :fh] = (dav * u_ref[...] * (s * (1.0 + g * (1.0 - s)))).astype(MXU_DTYPE)
        dh_ref[:, fh:] = (dav * g * s).astype(MXU_DTYPE)

    return pl.pallas_call(
        body, grid=(t // tm,), in_specs=[_row_spec(tm, fh, 0), _row_spec(tm, fh, 1), _row_spec(tm, fh)],
        out_specs=_row_spec(tm, two_f), out_shape=jax.ShapeDtypeStruct((t, two_f), MXU_DTYPE),
        compiler_params=_params("parallel"), name=name)(h, h, da)


def _attn_probs(q, k):
    s = _nt(q, k) * (X_HEADDIM ** -0.5)
    s = s - jnp.max(s, axis=-1, keepdims=True)
    p = jnp.exp(s)
    return p / jnp.sum(p, axis=-1, keepdims=True)


def _attn_fwd(q, kv, *, bsz, name):
    t = q.shape[0]
    s = t // bsz
    ml = kv.shape[0] // bsz
    hd = X_HEADDIM

    def body(q_ref, k_ref, v_ref, o_ref):
        p = _attn_probs(q_ref[...], k_ref[...])
        o_ref[...] = _nn(p.astype(MXU_DTYPE), v_ref[...]).astype(MXU_DTYPE)

    return pl.pallas_call(
        body, grid=(bsz, X_HEADS),
        in_specs=[pl.BlockSpec((s, hd), lambda b, h: (b, h)), pl.BlockSpec((ml, hd), lambda b, h: (b, h)),
                  pl.BlockSpec((ml, hd), lambda b, h: (b, X_HEADS + h))],
        out_specs=pl.BlockSpec((s, hd), lambda b, h: (b, h)),
        out_shape=jax.ShapeDtypeStruct((t, D_MODEL), MXU_DTYPE),
        compiler_params=_params("parallel", "parallel"), name=name)(q, kv, kv)


def _attn_bwd(q, kv, do, *, bsz, name):
    t = q.shape[0]
    s = t // bsz
    ml = kv.shape[0] // bsz
    hd = X_HEADDIM

    def body(q_ref, k_ref, v_ref, do_ref, dq_ref, dk_ref, dv_ref):
        qv, kk, vv, dov = q_ref[...], k_ref[...], v_ref[...], do_ref[...]
        p = _attn_probs(qv, kk)
        dp = _nt(dov, vv)
        dv_ref[...] = _tn(p.astype(MXU_DTYPE), dov).astype(MXU_DTYPE)
        ds = (p * (dp - jnp.sum(dp * p, axis=-1, keepdims=True)) * (X_HEADDIM ** -0.5)).astype(MXU_DTYPE)
        dq_ref[...] = _nn(ds, kk).astype(MXU_DTYPE)
        dk_ref[...] = _tn(ds, qv).astype(MXU_DTYPE)

    blk_q = pl.BlockSpec((s, hd), lambda b, h: (b, h))
    blk_m = pl.BlockSpec((ml, hd), lambda b, h: (b, h))
    return pl.pallas_call(
        body, grid=(bsz, X_HEADS),
        in_specs=[blk_q, blk_m, pl.BlockSpec((ml, hd), lambda b, h: (b, X_HEADS + h)), blk_q],
        out_specs=[blk_q, blk_m, blk_m],
        out_shape=[jax.ShapeDtypeStruct((t, D_MODEL), MXU_DTYPE), jax.ShapeDtypeStruct((bsz * ml, D_MODEL), MXU_DTYPE),
                   jax.ShapeDtypeStruct((bsz * ml, D_MODEL), MXU_DTYPE)],
        compiler_params=_params("parallel", "parallel"), name=name)(q, kv, kv, do)


def _causal(n):
    row = lax.broadcasted_iota(jnp.int32, (n, n), 0)
    col = lax.broadcasted_iota(jnp.int32, (n, n), 1)
    return row >= col


def _sg_norm(v, g, b):
    gv = _gelu(v)
    mu = jnp.mean(gv, axis=-1, keepdims=True)
    xc = gv - mu
    var = jnp.mean(xc * xc, axis=-1, keepdims=True)
    rstd = lax.rsqrt(var + LN_EPS)
    xh = xc * rstd
    return xh, rstd, xh * g + b


def _sg_fwd(proj, ln_g, ln_b, w, bcol, *, name):
    t = proj.shape[0]
    c = D_MODEL
    gd = c // SG_GROUPS

    def body(u_ref, v_ref, g_ref, b_ref, w_ref, bc_ref, o_ref):
        gu = _gelu(u_ref[...])
        _, _, vn = _sg_norm(v_ref[...], g_ref[...], b_ref[...])
        mask = _causal(CHUNK)
        for g in range(SG_GROUPS):
            sl = slice(g * gd, (g + 1) * gd)
            wg = jnp.where(mask, w_ref[g], 0.0).astype(MXU_DTYPE)
            mixed = _nn(wg, vn[:, sl].astype(MXU_DTYPE)) + bc_ref[g]
            o_ref[:, sl] = (gu[:, sl] * mixed).astype(MXU_DTYPE)

    return pl.pallas_call(
        body, grid=(t // CHUNK,),
        in_specs=[_row_spec(CHUNK, c, 0), _row_spec(CHUNK, c, 1), _par_spec((1, c)), _par_spec((1, c)),
                  _par_spec((SG_GROUPS, CHUNK, CHUNK)), _par_spec((SG_GROUPS, CHUNK, 1))],
        out_specs=_row_spec(CHUNK, c), out_shape=jax.ShapeDtypeStruct((t, c), MXU_DTYPE),
        compiler_params=_params("parallel"), name=name)(proj, proj, ln_g.reshape(1, c), ln_b.reshape(1, c), w, bcol)


def _sg_bwd(proj, dsgo, ln_g, ln_b, w, bcol, dproj, *, name):
    t = proj.shape[0]
    c = D_MODEL
    gd = c // SG_GROUPS

    def body(u_ref, v_ref, d_ref, g_ref, b_ref, w_ref, bc_ref, _, duv_ref, dw_ref, dbc_ref, dg_ref, db_ref, dvn_ref):
        @pl.when(pl.program_id(0) == 0)
        def _():
            dw_ref[...] = jnp.zeros_like(dw_ref)
            dbc_ref[...] = jnp.zeros_like(dbc_ref)
            dg_ref[...] = jnp.zeros_like(dg_ref)
            db_ref[...] = jnp.zeros_like(db_ref)

        u = u_ref[...]
        v = v_ref[...]
        dso = d_ref[...]
        gu = _gelu(u)
        xh, rstd, vn = _sg_norm(v, g_ref[...], b_ref[...])
        mask = _causal(CHUNK)
        for g in range(SG_GROUPS):
            sl = slice(g * gd, (g + 1) * gd)
            wg = jnp.where(mask, w_ref[g], 0.0).astype(MXU_DTYPE)
            vng = vn[:, sl].astype(MXU_DTYPE)
            mixed = _nn(wg, vng) + bc_ref[g]
            duv_ref[:, sl] = (dso[:, sl] * mixed * _gelu_grad(u[:, sl])).astype(MXU_DTYPE)
            dmix = dso[:, sl] * gu[:, sl]
            dmb = dmix.astype(MXU_DTYPE)
            dbc_ref[g] += jnp.sum(dmix, axis=-1, keepdims=True)
            dw_ref[g] += jnp.where(mask, _nt(dmb, vng), 0.0)
            dvn_ref[:, sl] = _tn(wg, dmb)
        dvn = dvn_ref[...]
        dg_ref[...] += jnp.sum(dvn * xh, axis=0, keepdims=True)
        db_ref[...] += jnp.sum(dvn, axis=0, keepdims=True)
        dxh = dvn * g_ref[...]
        m1 = jnp.mean(dxh, axis=-1, keepdims=True)
        m2 = jnp.mean(dxh * xh, axis=-1, keepdims=True)
        dgv = rstd * (dxh - m1 - xh * m2)
        duv_ref[:, c:] = (dgv * _gelu_grad(v)).astype(MXU_DTYPE)

    return pl.pallas_call(
        body, grid=(t // CHUNK,),
        in_specs=[_row_spec(CHUNK, c, 0), _row_spec(CHUNK, c, 1), _row_spec(CHUNK, c), _par_spec((1, c)),
                  _par_spec((1, c)), _par_spec((SG_GROUPS, CHUNK, CHUNK)), _par_spec((SG_GROUPS, CHUNK, 1)), _ANY],
        out_specs=[_row_spec(CHUNK, 2 * c), _par_spec((SG_GROUPS, CHUNK, CHUNK)), _par_spec((SG_GROUPS, CHUNK, 1)),
                   _par_spec((1, c)), _par_spec((1, c))],
        out_shape=[jax.ShapeDtypeStruct(dproj.shape, dproj.dtype), jax.ShapeDtypeStruct((SG_GROUPS, CHUNK, CHUNK), F32),
                   jax.ShapeDtypeStruct((SG_GROUPS, CHUNK, 1), F32), jax.ShapeDtypeStruct((1, c), F32),
                   jax.ShapeDtypeStruct((1, c), F32)],
        scratch_shapes=[pltpu.VMEM((CHUNK, c), F32)], input_output_aliases={7: 0},
        compiler_params=_params("arbitrary"), name=name)(proj, proj, dsgo, ln_g.reshape(1, c), ln_b.reshape(1, c), w, bcol, dproj)


CONV_TC = 512


def _conv_taps(x):
    rows = lax.broadcasted_iota(jnp.int32, x.shape, 0)
    taps = [jnp.where(rows >= SSM_CONV - 1 - k, pltpu.roll(x, SSM_CONV - 1 - k, axis=0), 0.0) for k in range(SSM_CONV - 1)]
    return taps + [x]


def _conv_pre(taps, w_ref, b_ref):
    acc = b_ref[...]
    for k in range(SSM_CONV):
        acc = acc + taps[k] * w_ref[k:k + 1, :]
    return acc


def _conv_fwd(proj, w, b, *, bsz, name):
    t = proj.shape[0]
    s = t // bsz
    nj = SSM_CONV_DIM // CONV_TC
    c0 = XBC_COL0 // CONV_TC

    def body(x_ref, w_ref, b_ref, o_ref):
        pre = _conv_pre(_conv_taps(x_ref[...]), w_ref, b_ref)
        o_ref[...] = pre * _sigmoid(pre)

    return pl.pallas_call(
        body, grid=(bsz, nj),
        in_specs=[pl.BlockSpec((s, CONV_TC), lambda bb, j: (bb, c0 + j)), pl.BlockSpec((SSM_CONV, CONV_TC), lambda bb, j: (0, j)),
                  pl.BlockSpec((1, CONV_TC), lambda bb, j: (0, j))],
        out_specs=pl.BlockSpec((s, CONV_TC), lambda bb, j: (bb, j)),
        out_shape=jax.ShapeDtypeStruct((t, SSM_CONV_DIM), F32),
        compiler_params=_params("parallel", "parallel"), name=name)(proj, w, b.reshape(1, -1))


def _conv_bwd(proj, dact, w, b, dproj, *, bsz, name):
    t = proj.shape[0]
    s = t // bsz
    nj = SSM_CONV_DIM // CONV_TC
    c0 = XBC_COL0 // CONV_TC

    def body(x_ref, d_ref, w_ref, b_ref, _, dx_ref, dw_ref, db_ref):
        @pl.when(pl.program_id(1) == 0)
        def _():
            dw_ref[...] = jnp.zeros_like(dw_ref)
            db_ref[...] = jnp.zeros_like(db_ref)

        taps = _conv_taps(x_ref[...])
        pre = _conv_pre(taps, w_ref, b_ref)
        sg = _sigmoid(pre)
        dpre = d_ref[...] * (sg * (1.0 + pre * (1.0 - sg)))
        rows = lax.broadcasted_iota(jnp.int32, dpre.shape, 0)
        db_ref[...] += jnp.sum(dpre, axis=0, keepdims=True)
        dx = dpre * w_ref[SSM_CONV - 1:SSM_CONV, :]
        for k in range(SSM_CONV):
            dw_ref[k:k + 1, :] += jnp.sum(dpre * taps[k], axis=0, keepdims=True)
        for k in range(SSM_CONV - 1):
            sh = SSM_CONV - 1 - k
            dsh = jnp.where(rows < s - sh, pltpu.roll(dpre, s - sh, axis=0), 0.0)
            dx = dx + dsh * w_ref[k:k + 1, :]
        dx_ref[...] = dx.astype(MXU_DTYPE)

    return pl.pallas_call(
        body, grid=(nj, bsz),
        in_specs=[pl.BlockSpec((s, CONV_TC), lambda j, bb: (bb, c0 + j)), pl.BlockSpec((s, CONV_TC), lambda j, bb: (bb, j)),
                  pl.BlockSpec((SSM_CONV, CONV_TC), lambda j, bb: (0, j)), pl.BlockSpec((1, CONV_TC), lambda j, bb: (0, j)), _ANY],
        out_specs=[pl.BlockSpec((s, CONV_TC), lambda j, bb: (bb, c0 + j)), pl.BlockSpec((SSM_CONV, CONV_TC), lambda j, bb: (0, j)),
                   pl.BlockSpec((1, CONV_TC), lambda j, bb: (0, j))],
        out_shape=[jax.ShapeDtypeStruct(dproj.shape, dproj.dtype), jax.ShapeDtypeStruct((SSM_CONV, SSM_CONV_DIM), F32),
                   jax.ShapeDtypeStruct((1, SSM_CONV_DIM), F32)],
        input_output_aliases={4: 0},
        compiler_params=_params("parallel", "arbitrary"), name=name)(proj, dact, w, b.reshape(1, -1), dproj)


def _softplus(x):
    return jnp.maximum(x, 0.0) + jnp.log1p(jnp.exp(-jnp.abs(x)))


def _pad_heads(v):
    return jnp.broadcast_to(jnp.pad(v.astype(F32), (0, HEAD_PAD - SSM_HEADS))[None, :], (SUBLANE, HEAD_PAD))


def _ssd_prep(dt_raw, dt_bias8, a_log8, *, name):
    t = dt_raw.shape[0]
    n = CHUNK

    def body(r_ref, b_ref, al_ref, dt_ref, cs_ref, dtt_ref, cst_ref):
        dt = _softplus(r_ref[...] + b_ref[0:1, :])
        da = dt * (-jnp.exp(al_ref[0:1, :]))
        row = lax.broadcasted_iota(jnp.int32, (n, n), 0)
        col = lax.broadcasted_iota(jnp.int32, (n, n), 1)
        lower = (col <= row).astype(F32)
        upper = (row <= col).astype(F32)
        eye = (row == col).astype(F32)
        dt_ref[...] = dt
        cs_ref[...] = _dot_exact(lower, da, _DN_NN, 1)
        cst_ref[0] = _dot_exact(da, upper, _DN_TN, 0)
        dtt_ref[0] = _dot_exact(dt, eye, _DN_TN, 0)

    hp = HEAD_PAD
    return pl.pallas_call(
        body, grid=(t // n,),
        in_specs=[_row_spec(n, hp), _par_spec((SUBLANE, hp)), _par_spec((SUBLANE, hp))],
        out_specs=[_row_spec(n, hp), _row_spec(n, hp), pl.BlockSpec((1, hp, n), lambda i: (i, 0, 0)),
                   pl.BlockSpec((1, hp, n), lambda i: (i, 0, 0))],
        out_shape=[jax.ShapeDtypeStruct((t, hp), F32), jax.ShapeDtypeStruct((t, hp), F32),
                   jax.ShapeDtypeStruct((t // n, hp, n), F32), jax.ShapeDtypeStruct((t // n, hp, n), F32)],
        compiler_params=_params("parallel"), name=name)(dt_raw, dt_bias8, a_log8)


def _expand_mat():
    h = lax.broadcasted_iota(jnp.int32, (HEAD_PAD, SSM_INNER), 0)
    ch = lax.broadcasted_iota(jnp.int32, (HEAD_PAD, SSM_INNER), 1)
    return (ch // SSM_HEADDIM == h).astype(F32)


def _reduce_mat():
    ch = lax.broadcasted_iota(jnp.int32, (SSM_INNER, HEAD_PAD), 0)
    h = lax.broadcasted_iota(jnp.int32, (SSM_INNER, HEAD_PAD), 1)
    return (ch // SSM_HEADDIM == h).astype(F32)


def _expand(v, em):
    return _dot_exact(v, em, _DN_NN, 0)


def _expand_heads(v):
    return jnp.repeat(v.astype(F32), SSM_HEADDIM)[None, :]


def _decay_mat(cs_ref, cst_ref, h, mask):
    seg = cs_ref[:, h:h + 1] - cst_ref[0, h:h + 1, :]
    return jnp.where(mask, jnp.exp(jnp.minimum(seg, 0.0)), 0.0)


GROUP_CH = SSM_INNER // SSM_GROUPS
PAIRS_PER_GROUP = GROUP_CH // LANE
HEADS_PER_GROUP = SSM_HEADS // SSM_GROUPS
BM_COL0 = SSM_INNER
CM_COL0 = SSM_INNER + SSM_GROUPS * SSM_STATE


def _ssd_specs(nc, rev):
    def cidx(i):
        return (i // nc) * nc + (nc - 1 - i % nc) if rev else i

    n = CHUNK
    xs = pl.BlockSpec((n, SSM_INNER), lambda i: (cidx(i), 0))
    bm = pl.BlockSpec((n, GROUP_CH), lambda i: (cidx(i), BM_COL0 // GROUP_CH))
    cm = pl.BlockSpec((n, GROUP_CH), lambda i: (cidx(i), CM_COL0 // GROUP_CH))
    hv = pl.BlockSpec((n, HEAD_PAD), lambda i: (cidx(i), 0))
    hvt = pl.BlockSpec((1, HEAD_PAD, n), lambda i: (cidx(i), 0, 0))
    st = pl.BlockSpec((1, SSM_INNER, SSM_STATE), lambda i: (cidx(i), 0, 0))
    return xs, bm, cm, hv, hvt, st


def _ssd_fwd(xbc, dt, cs, dtt, cst, dskx, *, nc, name):
    t = xbc.shape[0]
    n = CHUNK
    xs_s, bm_s, cm_s, hv_s, hvt_s, st_s = _ssd_specs(nc, False)

    def body(xs_ref, bm_ref, cm_ref, dt_ref, cs_ref, dtt_ref, cst_ref, dsk_ref, y_ref, st_ref, prev):
        @pl.when(pl.program_id(0) % nc == 0)
        def _():
            prev[...] = jnp.zeros_like(prev)

        st_ref[0] = prev[...]
        em = _expand_mat()
        dtx = _expand(dt_ref[...], em)
        csx = _expand(cs_ref[...], em)
        dskx = dsk_ref[...]
        xs = xs_ref[...]
        xdt = xs * dtx
        ecs = jnp.exp(csx)
        dec = jnp.exp(csx[n - 1:n, :] - csx)
        mask = _causal(n)
        lane = lax.broadcasted_iota(jnp.int32, (n, LANE), 1)
        for g in range(SSM_GROUPS):
            gs = slice(g * SSM_STATE, (g + 1) * SSM_STATE)
            gc = slice(g * GROUP_CH, (g + 1) * GROUP_CH)
            cmat = cm_ref[:, gs].astype(MXU_DTYPE)
            bmat = bm_ref[:, gs].astype(MXU_DTYPE)
            cb = _nt(cmat, bmat)
            yoff = ecs[:, gc] * _nt(cmat, prev[gc, :].astype(MXU_DTYPE))
            for q in range(PAIRS_PER_GROUP):
                hp = g * PAIRS_PER_GROUP + q
                sl = slice(hp * LANE, (hp + 1) * LANE)
                xp = xdt[:, sl].astype(MXU_DTYPE)
                m0 = (cb * _decay_mat(cs_ref, cst_ref, 2 * hp, mask)).astype(MXU_DTYPE)
                m1 = (cb * _decay_mat(cs_ref, cst_ref, 2 * hp + 1, mask)).astype(MXU_DTYPE)
                yd = jnp.where(lane < SSM_HEADDIM, _nn(m0, xp), _nn(m1, xp))
                y_ref[:, sl] = yd + yoff[:, q * LANE:(q + 1) * LANE] + xs[:, sl] * dskx[:, sl]
            snew = _tn((xdt[:, gc] * dec[:, gc]).astype(MXU_DTYPE), bmat)
            for r in range(HEADS_PER_GROUP):
                h = g * HEADS_PER_GROUP + r
                rows = slice(h * SSM_HEADDIM, (h + 1) * SSM_HEADDIM)
                e = jnp.exp(cst_ref[0, h:h + 1, n - 1:n])
                prev[rows, :] = prev[rows, :] * e + snew[r * SSM_HEADDIM:(r + 1) * SSM_HEADDIM, :]

    return pl.pallas_call(
        body, grid=(t // n,),
        in_specs=[xs_s, bm_s, cm_s, hv_s, hv_s, hvt_s, hvt_s, _par_spec((1, SSM_INNER))],
        out_specs=[xs_s, st_s],
        out_shape=[jax.ShapeDtypeStruct((t, SSM_INNER), F32), jax.ShapeDtypeStruct((t // n, SSM_INNER, SSM_STATE), F32)],
        scratch_shapes=[pltpu.VMEM((SSM_INNER, SSM_STATE), F32)],
        compiler_params=_params("arbitrary"), name=name)(xbc, xbc, xbc, dt, cs, dtt, cst, dskx)


def _ssd_bwd(dy, xbc, dt, cs, dtt, cst, st, dskx, a_log8, dt_raw, dt_bias8, *, nc, name):
    t = xbc.shape[0]
    n = CHUNK
    xs_s, bm_s, cm_s, hv_s, hvt_s, st_s = _ssd_specs(nc, True)
    acc_s = _par_spec((1, HEAD_PAD))
    xbc_s = pl.BlockSpec((n, SSM_CONV_DIM), xs_s.index_map)

    def body(dy_ref, xs_ref, bm_ref, cm_ref, dt_ref, cs_ref, dtt_ref, cst_ref, st_ref, dsk_ref, al_ref, raw_ref, bias_ref,
             dxbc_ref, ddr_ref, dal_ref, dds_ref, dbias_ref, dprev, dxdt_s, tdec_s, tcs_s):
        @pl.when(pl.program_id(0) % nc == 0)
        def _():
            dprev[...] = jnp.zeros_like(dprev)

        @pl.when(pl.program_id(0) == 0)
        def _():
            dal_ref[...] = jnp.zeros_like(dal_ref)
            dds_ref[...] = jnp.zeros_like(dds_ref)
            dbias_ref[...] = jnp.zeros_like(dbias_ref)

        em = _expand_mat()
        rm = _reduce_mat()

        def head_reduce(v):
            return _dot_exact(v, rm, _DN_NN, 0)

        dtv = dt_ref[...]
        csv = cs_ref[...]
        dtx = _expand(dtv, em)
        csx = _expand(csv, em)
        dskx = dsk_ref[...]
        xs = xs_ref[...]
        dyv = dy_ref[...]
        xdt = xs * dtx
        ecs = jnp.exp(csx)
        dec = jnp.exp(csx[n - 1:n, :] - csx)
        mask = _causal(n)
        lane = lax.broadcasted_iota(jnp.int32, (n, LANE), 1)
        hlane = lax.broadcasted_iota(jnp.int32, (1, HEAD_PAD), 1)
        hsub = lax.broadcasted_iota(jnp.int32, (HEAD_PAD, 1), 0)
        rsum = jnp.zeros((n, HEAD_PAD), F32)
        csum = jnp.zeros((HEAD_PAD, n), F32)
        for g in range(SSM_GROUPS):
            gs = slice(g * SSM_STATE, (g + 1) * SSM_STATE)
            gc = slice(g * GROUP_CH, (g + 1) * GROUP_CH)
            cmat = cm_ref[:, gs].astype(MXU_DTYPE)
            bmat = bm_ref[:, gs].astype(MXU_DTYPE)
            cb = _nt(cmat, bmat)
            pg = st_ref[0, gc, :].astype(MXU_DTYPE)
            dpg = dprev[gc, :]
            dpgb = dpg.astype(MXU_DTYPE)
            z = _nt(cmat, pg)
            dyg = dyv[:, gc]
            dz = (dyg * ecs[:, gc]).astype(MXU_DTYPE)
            dc = _nn(dz, pg)
            dprev_y = _tn(dz, cmat)
            tcs_s[:, gc] = dyg * z * ecs[:, gc]
            xd = xdt[:, gc] * dec[:, gc]
            wmat = _nt(bmat, dpgb)
            db = _nn(xd.astype(MXU_DTYPE), dpgb)
            tdec_s[:, gc] = wmat * xd
            dxdt_g = wmat * dec[:, gc]
            dcb = jnp.zeros((n, n), F32)
            for q in range(PAIRS_PER_GROUP):
                hp = g * PAIRS_PER_GROUP + q
                sl = slice(hp * LANE, (hp + 1) * LANE)
                xp = xdt[:, sl].astype(MXU_DTYPE)
                dyp = dyv[:, sl]
                dypb = dyp.astype(MXU_DTYPE)
                dxp = None
                for hh in range(2):
                    h = 2 * hp + hh
                    lm = _decay_mat(cs_ref, cst_ref, h, mask)
                    mine = (lane < SSM_HEADDIM) if hh == 0 else (lane >= SSM_HEADDIM)
                    dm = _nt(jnp.where(mine, dyp, 0.0).astype(MXU_DTYPE), xp)
                    dml = dm * lm
                    dcb = dcb + dml
                    gseg = dml * cb
                    rsum = rsum + jnp.sum(gseg, axis=1, keepdims=True) * (hlane == h).astype(F32)
                    csum = csum + (hsub == h).astype(F32) * jnp.sum(gseg, axis=0, keepdims=True)
                    dxh = _tn((cb * lm).astype(MXU_DTYPE), dypb)
                    dxp = dxh if dxp is None else jnp.where(mine, dxh, dxp)
                dxdt_s[:, sl] = dxdt_g[:, q * LANE:(q + 1) * LANE] + dxp
            dcbb = dcb.astype(MXU_DTYPE)
            dxbc_ref[:, CM_COL0 + g * SSM_STATE:CM_COL0 + (g + 1) * SSM_STATE] = dc + _nn(dcbb, bmat)
            dxbc_ref[:, BM_COL0 + g * SSM_STATE:BM_COL0 + (g + 1) * SSM_STATE] = db + _tn(dcbb, cmat)
            for r in range(HEADS_PER_GROUP):
                h = g * HEADS_PER_GROUP + r
                rows = slice(h * SSM_HEADDIM, (h + 1) * SSM_HEADDIM)
                lr = slice(r * SSM_HEADDIM, (r + 1) * SSM_HEADDIM)
                e = jnp.exp(cst_ref[0, h:h + 1, n - 1:n])
                dprev[rows, :] = dpg[lr, :] * e + dprev_y[lr, :]
            tq = _dot_exact(dpg * st_ref[0, gc, :], rm[gc, :], _DN_TN, 0)
            if g == 0:
                qsum = jnp.sum(tq, axis=0, keepdims=True)
            else:
                qsum = qsum + jnp.sum(tq, axis=0, keepdims=True)
        dxdt = dxdt_s[...]
        dxbc_ref[:, 0:SSM_INNER] = dxdt * dtx + dyv * dskx
        ddt = head_reduce(dxdt * xs)
        edec = head_reduce(tdec_s[...])
        ycs = head_reduce(tcs_s[...])
        row = lax.broadcasted_iota(jnp.int32, (n, HEAD_PAD), 0)
        extra = jnp.sum(edec, axis=0, keepdims=True) + qsum * jnp.exp(csv[n - 1:n, :])
        dcs = rsum - csum.T + ycs - edec + jnp.where(row == n - 1, extra, 0.0)
        r2 = lax.broadcasted_iota(jnp.int32, (n, n), 0)
        c2 = lax.broadcasted_iota(jnp.int32, (n, n), 1)
        dda = _dot_exact((c2 >= r2).astype(F32), dcs, _DN_NN, 1)
        a_row = -jnp.exp(al_ref[0:1, :])
        ddt = ddt + dda * a_row
        dal_ref[...] += jnp.sum(dda * dtv, axis=0, keepdims=True) * a_row
        dds_ref[...] += jnp.sum(head_reduce(dyv * xs), axis=0, keepdims=True)
        ddr = ddt * _sigmoid(raw_ref[...] + bias_ref[0:1, :])
        ddr_ref[...] = ddr
        dbias_ref[...] += jnp.sum(ddr, axis=0, keepdims=True)

    par8 = _par_spec((SUBLANE, HEAD_PAD))
    return pl.pallas_call(
        body, grid=(t // n,),
        in_specs=[xs_s, xs_s, bm_s, cm_s, hv_s, hv_s, hvt_s, hvt_s, st_s, _par_spec((1, SSM_INNER)), par8, hv_s, par8],
        out_specs=[xbc_s, hv_s, acc_s, acc_s, acc_s],
        out_shape=[jax.ShapeDtypeStruct((t, SSM_CONV_DIM), F32), jax.ShapeDtypeStruct((t, HEAD_PAD), F32),
                   jax.ShapeDtypeStruct((1, HEAD_PAD), F32), jax.ShapeDtypeStruct((1, HEAD_PAD), F32),
                   jax.ShapeDtypeStruct((1, HEAD_PAD), F32)],
        scratch_shapes=[pltpu.VMEM((SSM_INNER, SSM_STATE), F32), pltpu.VMEM((n, SSM_INNER), F32),
                        pltpu.VMEM((n, SSM_INNER), F32), pltpu.VMEM((n, SSM_INNER), F32)],
        compiler_params=_params("arbitrary"), name=name)(dy, xbc, xbc, xbc, dt, cs, dtt, cst, st, dskx, a_log8, dt_raw, dt_bias8)


def _gate_norm_fwd(y, proj, norm_g, *, name):
    t, c = y.shape
    tm = _pick(t, (256, 128))

    def body(y_ref, z_ref, g_ref, o_ref):
        z = z_ref[...]
        yz = y_ref[...] * z * _sigmoid(z)
        for g in range(SSM_GROUPS):
            gc = slice(g * GROUP_CH, (g + 1) * GROUP_CH)
            seg = yz[:, gc]
            r = lax.rsqrt(jnp.mean(seg * seg, axis=-1, keepdims=True) + RMS_EPS)
            o_ref[:, gc] = (seg * r * g_ref[:, gc]).astype(MXU_DTYPE)

    return pl.pallas_call(
        body, grid=(t // tm,), in_specs=[_row_spec(tm, c), _row_spec(tm, c, 1), _par_spec((1, c))],
        out_specs=_row_spec(tm, c), out_shape=jax.ShapeDtypeStruct((t, c), MXU_DTYPE),
        compiler_params=_params("parallel"), name=name)(y, proj, norm_g.reshape(1, c))


def _gate_norm_bwd(dyb, y, proj, norm_g, dproj, *, name):
    t, c = y.shape
    tm = _pick(t, (256, 128))

    def body(d_ref, y_ref, z_ref, g_ref, _, dy_ref, dz_ref, dg_ref):
        @pl.when(pl.program_id(0) == 0)
        def _():
            dg_ref[...] = jnp.zeros_like(dg_ref)

        z = z_ref[...]
        yv = y_ref[...]
        sz = _sigmoid(z)
        silu = z * sz
        yz = yv * silu
        dv = d_ref[...]
        for g in range(SSM_GROUPS):
            gc = slice(g * GROUP_CH, (g + 1) * GROUP_CH)
            seg = yz[:, gc]
            r = lax.rsqrt(jnp.mean(seg * seg, axis=-1, keepdims=True) + RMS_EPS)
            nrm = seg * r
            dn = dv[:, gc] * g_ref[:, gc]
            dg_ref[:, gc] += jnp.sum(dv[:, gc] * nrm, axis=0, keepdims=True)
            dyz = r * (dn - nrm * jnp.mean(dn * nrm, axis=-1, keepdims=True))
            dy_ref[:, gc] = dyz * silu[:, gc]
            dz_ref[:, gc] = (dyz * yv[:, gc] * (sz[:, gc] * (1.0 + z[:, gc] * (1.0 - sz[:, gc])))).astype(MXU_DTYPE)

    return pl.pallas_call(
        body, grid=(t // tm,), in_specs=[_row_spec(tm, c), _row_spec(tm, c), _row_spec(tm, c, 1), _par_spec((1, c)), _ANY],
        out_specs=[_row_spec(tm, c), _row_spec(tm, c, 1), _par_spec((1, c))],
        out_shape=[jax.ShapeDtypeStruct((t, c), F32), jax.ShapeDtypeStruct(dproj.shape, dproj.dtype),
                   jax.ShapeDtypeStruct((1, c), F32)],
        input_output_aliases={4: 1},
        compiler_params=_params("arbitrary"), name=name)(dyb, y, proj, norm_g.reshape(1, c), dproj)


GA_COLBLK = GAB_COL0 // D_MODEL


def _merge_fwd(br_a, br_b, proj, *, name):
    t, c = br_a.shape
    tm = _pick(t, ROW_TILES)

    def body(a_ref, b_ref, ga_ref, gb_ref, o_ref):
        o_ref[...] = (_sigmoid(ga_ref[...]) * a_ref[...] + _sigmoid(gb_ref[...]) * b_ref[...]).astype(MXU_DTYPE)

    return pl.pallas_call(
        body, grid=(t // tm,),
        in_specs=[_row_spec(tm, c), _row_spec(tm, c), _row_spec(tm, c, GA_COLBLK), _row_spec(tm, c, GA_COLBLK + 1)],
        out_specs=_row_spec(tm, c), out_shape=jax.ShapeDtypeStruct((t, c), MXU_DTYPE),
        compiler_params=_params("parallel"), name=name)(br_a, br_b, proj, proj)


def _merge_bwd(dm, br_a, br_b, proj, *, name):
    t, c = br_a.shape
    tm = _pick(t, ROW_TILES)

    def body(dm_ref, a_ref, b_ref, ga_ref, gb_ref, da_ref, db_ref, dg_ref):
        d = dm_ref[...]
        sa = _sigmoid(ga_ref[...])
        sb = _sigmoid(gb_ref[...])
        da_ref[...] = (d * sa).astype(MXU_DTYPE)
        db_ref[...] = (d * sb).astype(MXU_DTYPE)
        dg_ref[:, :c] = (d * a_ref[...] * sa * (1.0 - sa)).astype(MXU_DTYPE)
        dg_ref[:, c:] = (d * b_ref[...] * sb * (1.0 - sb)).astype(MXU_DTYPE)

    return pl.pallas_call(
        body, grid=(t // tm,),
        in_specs=[_row_spec(tm, c), _row_spec(tm, c), _row_spec(tm, c), _row_spec(tm, c, GA_COLBLK), _row_spec(tm, c, GA_COLBLK + 1)],
        out_specs=[_row_spec(tm, c), _row_spec(tm, c), _row_spec(tm, 2 * c, GAB_COL0 // (2 * c))],
        out_shape=[jax.ShapeDtypeStruct((t, c), MXU_DTYPE), jax.ShapeDtypeStruct((t, c), MXU_DTYPE),
                   jax.ShapeDtypeStruct((t, MAIN_COLS), MXU_DTYPE)],
        compiler_params=_params("parallel"), name=name)(dm, br_a, br_b, proj, proj)


def _layer_fwd(x, xb, memn_b, w, *, bsz, tag):
    nc = x.shape[0] // bsz // CHUNK
    sv = {"x_in": xb}
    proj = _mm(xb, w["w_main"], name=f"{tag}_proj")
    dt_raw = _mm(xb, w["w_dt"], name=f"{tag}_dtproj")
    sgo = _sg_fwd(proj, w["sg_ln_g"], w["sg_ln_b"], w["sg_w"], w["sg_bcol"], name=f"{tag}_sg_fwd")
    xbc = _conv_fwd(proj, w["conv_w"], w["conv_b"], bsz=bsz, name=f"{tag}_conv_fwd")
    dt, cs, dtt, cst = _ssd_prep(dt_raw, w["dt_bias8"], w["a_log8"], name=f"{tag}_ssd_prep")
    y, st = _ssd_fwd(xbc, dt, cs, dtt, cst, w["d_skipx"], nc=nc, name=f"{tag}_ssd_fwd")
    yb = _gate_norm_fwd(y, proj, w["ssm_norm_g"], name=f"{tag}_gate_norm_fwd")
    if "rest" in w:
        w = w["rest"](w, yb)
    br_a = _mm(sgo, w["p_a"], name=f"{tag}_br_a")
    br_b = _mm(yb, w["p_b"], name=f"{tag}_br_b")
    merged = _merge_fwd(br_a, br_b, proj, name=f"{tag}_merge_fwd")
    mix = _mm(merged, w["w_mix_o"], name=f"{tag}_mix_o")
    x1, x1b, xh1, rs1 = _ln_fwd(x, mix, w["ln_g"][0], w["ln_b"][0], name=f"{tag}_ln1_fwd")
    sv.update(proj=proj, dt_raw=dt_raw, sgo=sgo, xbc=xbc, dt=dt, cs=cs, dtt=dtt, cst=cst, y=y, st=st, yb=yb,
              br_a=br_a, br_b=br_b, merged=merged, xh1=xh1, rs1=rs1, x1b=x1b)
    q = _mm(x1b, w["w_xq"], out_dtype=MXU_DTYPE, name=f"{tag}_q")
    kv = _mm(memn_b, w["w_xkv"], out_dtype=MXU_DTYPE, name=f"{tag}_kv")
    o = _attn_fwd(q, kv, bsz=bsz, name=f"{tag}_attn_fwd")
    att = _mm(o, w["w_xo"], name=f"{tag}_xo")
    x2, x2b, xh2, rs2 = _ln_fwd(x1, att, w["ln_g"][1], w["ln_b"][1], name=f"{tag}_ln2_fwd")
    sv.update(q=q, kv=kv, o=o, xh2=xh2, rs2=rs2, x2b=x2b)
    h = _mm(x2b, w["w_ffn_in"], name=f"{tag}_ffn_in")
    a = _swiglu_fwd(h, name=f"{tag}_swiglu_fwd")
    ffn = _mm(a, w["w_ffn_out"], name=f"{tag}_ffn_out")
    x3, x3b, xh3, rs3 = _ln_fwd(x2, ffn, w["ln_g"][2], w["ln_b"][2], name=f"{tag}_ln3_fwd")
    sv.update(h=h, a=a, xh3=xh3, rs3=rs3)
    return x3, x3b, sv, w


GRAD_GROUPS = (("w_ffn_out", "w_ffn_in", "w_xo", "w_xq", "w_xkv"), ("w_mix_o", "p_a", "p_b"), ("w_in",))


def _layer_bwd(dx3_addends, dx3_scales, memn_b, w, sv, on_group=None, *, bsz, tag):
    nc = sv["xh1"].shape[0] // bsz // CHUNK
    gr = {}

    def group_done(k):
        return on_group(GRAD_GROUPS[k], gr) if on_group is not None else None
    dp3, dp3b, dg3, db3 = _ln_bwd(dx3_addends, dx3_scales, sv["xh3"], sv["rs3"], w["ln_g"][2], name=f"{tag}_ln3_bwd")
    da = _mm(dp3b, w["w_ffn_out"], tb=True, name=f"{tag}_d_a")
    gr["w_ffn_out"] = _mm(sv["a"], dp3b, ta=True, name=f"{tag}_dw_ffn_out")
    dh = _swiglu_bwd(sv["h"], da, name=f"{tag}_swiglu_bwd")
    gr["w_ffn_in"] = _mm(sv["x2b"], dh, ta=True, name=f"{tag}_dw_ffn_in")
    dx2_br = _mm(dh, w["w_ffn_in"], tb=True, name=f"{tag}_dx2")
    dp2, dp2b, dg2, db2 = _ln_bwd([dp3, dx2_br], [ALPHA, 1.0], sv["xh2"], sv["rs2"], w["ln_g"][1], name=f"{tag}_ln2_bwd")
    do = _mm(dp2b, w["w_xo"], tb=True, out_dtype=MXU_DTYPE, name=f"{tag}_d_o")
    gr["w_xo"] = _mm(sv["o"], dp2b, ta=True, name=f"{tag}_dw_xo")
    dq, dk, dv = _attn_bwd(sv["q"], sv["kv"], do, bsz=bsz, name=f"{tag}_attn_bwd")
    dkv = jnp.concatenate([dk, dv], axis=1)
    gr["w_xq"] = _mm(sv["x1b"], dq, ta=True, name=f"{tag}_dw_xq")
    gr["w_xkv"] = _mm(memn_b, dkv, ta=True, name=f"{tag}_dw_xkv")
    dmemn = _mm(dkv, w["w_xkv"], tb=True, name=f"{tag}_d_memn")
    dx1_br = _mm(dq, w["w_xq"], tb=True, name=f"{tag}_dx1")
    token = group_done(0)
    ln_g1 = w["ln_g"][0] if token is None else w["ln_g"][0] + token[0, 0]
    dp1, dp1b, dg1, db1 = _ln_bwd([dp2, dx1_br], [ALPHA, 1.0], sv["xh1"], sv["rs1"], ln_g1, name=f"{tag}_ln1_bwd")
    gr["ln_g"] = jnp.concatenate([dg1, dg2, dg3], axis=0)
    gr["ln_b"] = jnp.concatenate([db1, db2, db3], axis=0)
    dmerged = _mm(dp1b, w["w_mix_o"], tb=True, name=f"{tag}_d_merged")
    gr["w_mix_o"] = _mm(sv["merged"], dp1b, ta=True, name=f"{tag}_dw_mix_o")
    dbr_a, dbr_b, dproj = _merge_bwd(dmerged, sv["br_a"], sv["br_b"], sv["proj"], name=f"{tag}_merge_bwd")
    gr["p_a"] = _mm(sv["sgo"], dbr_a, ta=True, name=f"{tag}_dw_p_a")
    gr["p_b"] = _mm(sv["yb"], dbr_b, ta=True, name=f"{tag}_dw_p_b")
    dsgo = _mm(dbr_a, w["p_a"], tb=True, name=f"{tag}_d_sgo")
    dyb = _mm(dbr_b, w["p_b"], tb=True, name=f"{tag}_d_yb")
    token = group_done(1)
    norm_g = w["ssm_norm_g"] if token is None else w["ssm_norm_g"] + token[0, 0]
    dy, dproj, gr["ssm_norm_g"] = _gate_norm_bwd(dyb, sv["y"], sv["proj"], norm_g, dproj, name=f"{tag}_gate_norm_bwd")
    dxbc, ddr, gr["a_log"], gr["d_skip"], gr["dt_bias"] = _ssd_bwd(
        dy, sv["xbc"], sv["dt"], sv["cs"], sv["dtt"], sv["cst"], sv["st"], w["d_skipx"], w["a_log8"], sv["dt_raw"],
        w["dt_bias8"], nc=nc, name=f"{tag}_ssd_bwd")
    dproj, gr["conv_w"], gr["conv_b"] = _conv_bwd(sv["proj"], dxbc, w["conv_w"], w["conv_b"], dproj, bsz=bsz, name=f"{tag}_conv_bwd")
    dproj, gr["sg_w"], dsg_bcol, gr["sg_ln_g"], gr["sg_ln_b"] = _sg_bwd(
        sv["proj"], dsgo, w["sg_ln_g"], w["sg_ln_b"], w["sg_w"], w["sg_bcol"], dproj, name=f"{tag}_sg_bwd")
    gr["sg_b"] = dsg_bcol[..., 0]
    gr["w_main"] = _mm(sv["x_in"], dproj, ta=True, name=f"{tag}_dw_main")
    gr["w_dt"] = _mm(sv["x_in"], ddr, ta=True, name=f"{tag}_dw_dt")
    token = group_done(2)
    dx_dt = _mm(ddr, w["w_dt"], tb=True, after=token, name=f"{tag}_dx_dt")
    dx_main = _mm(dproj, w["w_main"], tb=True, after=token, name=f"{tag}_dx_main")
    return [dp1, dx_main, dx_dt], [ALPHA, 1.0, 1.0], gr, dmemn


def _local_step(x, mem, tgt, mem_ln_g, mem_ln_b, layers, on_layer_grads=None):
    bsz, s, d = x.shape
    xf = x.reshape(bsz * s, d)
    memf = mem.reshape(-1, d)
    _, memn_b, mxh, mrs = _ln_fwd(memf, None, mem_ln_g, mem_ln_b, name="mem_ln_fwd")
    cur, curb, saved, weights = xf, xf, [], []
    for li, get_weights in enumerate(layers):
        cur, curb, sv, w = _layer_fwd(cur, curb, memn_b, get_weights(cur), bsz=bsz, tag=f"l{li}")
        saved.append(sv)
        weights.append(w)
    dy, lsum = _loss_head(cur, tgt.reshape(bsz * s, d), name="loss_head")
    addends, scales = [dy], [1.0]
    grads, dmem = [None] * len(layers), []
    for li in reversed(range(len(layers))):
        on_group = None if on_layer_grads is None else functools.partial(on_layer_grads, li)
        addends, scales, grads[li], dm = _layer_bwd(addends, scales, memn_b, weights[li], saved[li], on_group, bsz=bsz, tag=f"l{li}")
        dmem.append(dm)
    grad_x = _add_scaled(addends, scales, name="grad_x").reshape(bsz, s, d)
    _, _, dmg, dmb = _ln_bwd(dmem, [1.0] * len(dmem), mxh, mrs, mem_ln_g, name="mem_ln_bwd")
    return lsum, grad_x, grads, dmg[0], dmb[0]


_ANY = pl.BlockSpec(memory_space=pl.ANY)
_MESH = pl.DeviceIdType.MESH


def _all_gather8(x, *, name):
    def body(x_ref, out_ref, send_sems, recv_sems):
        mx, my, mc = lax.axis_index("x"), lax.axis_index("y"), lax.axis_index("c")
        me, sibling = (mx, my, mc), (mx, my, 1 - mc)
        chips = [(1 - mx, my), (mx, 1 - my), (1 - mx, 1 - my)]

        def blk(px, py, pc):
            return out_ref.at[4 * px + 2 * py + pc]

        def copy(k, block, to, src=None):
            return pltpu.make_async_remote_copy(
                src_ref=blk(*block) if src is None else src, dst_ref=blk(*block), send_sem=send_sems.at[k],
                recv_sem=recv_sems.at[k], device_id=to, device_id_type=_MESH)

        first = [copy(0, me, sibling, src=x_ref)]
        first += [copy(1 + j, me, (*chip, mc), src=x_ref) for j, chip in enumerate(chips)]
        for cp in first:
            cp.start()
        passed = [copy(4 + j, (*chip, mc), sibling) for j, chip in enumerate(chips)]
        for j, chip in enumerate(chips):
            copy(1 + j, (*chip, mc), me).wait_recv()
            passed[j].start()
        copy(0, sibling, me).wait_recv()
        for j, chip in enumerate(chips):
            copy(4 + j, (*chip, 1 - mc), me).wait_recv()
        for cp in first + passed:
            cp.wait_send()

    return pl.pallas_call(
        body, out_shape=jax.ShapeDtypeStruct((N_DEV,) + x.shape, x.dtype), in_specs=[_ANY], out_specs=_ANY,
        scratch_shapes=[pltpu.SemaphoreType.DMA((7,)), pltpu.SemaphoreType.DMA((7,))], name=name)(x)


def _row_tile(rows, row_bytes, mult=SUBLANE):
    best = None
    for tr in range(mult, rows + 1, mult):
        if rows % tr == 0 and (best is None or tr * row_bytes <= BLOCK_BYTES):
            best = tr
    return rows if best is None else best


def _gather_shape(r, c, kind):
    return {"row": (2, N_CHIPS * r, c), "col": (2, r, N_CHIPS * c), "chip": (2, N_CHIPS, r, c)}[kind]


def _cast_place(shard, kind, dtype, chip_idx, *, name):
    _, r, c = shard.shape
    tr = _row_tile(r, c * 4, 16)
    nt = r // tr

    def body(_, s_ref, o_ref):
        o_ref[...] = s_ref[...].astype(dtype)

    if kind == "row":
        out_spec = pl.BlockSpec((None, tr, c), lambda l, i, j_ref: (l, j_ref[0] * nt + i, 0))
    elif kind == "col":
        out_spec = pl.BlockSpec((None, tr, c), lambda l, i, j_ref: (l, i, j_ref[0]))
    else:
        out_spec = pl.BlockSpec((None, None, tr, c), lambda l, i, j_ref: (l, j_ref[0], i, 0))
    grid_spec = pltpu.PrefetchScalarGridSpec(
        num_scalar_prefetch=1, grid=(2, nt), in_specs=[pl.BlockSpec((None, tr, c), lambda l, i, j_ref: (l, i, 0))],
        out_specs=out_spec)
    return pl.pallas_call(body, grid_spec=grid_spec, out_shape=jax.ShapeDtypeStruct(_gather_shape(r, c, kind), dtype),
                          compiler_params=_params("parallel", "parallel"), name=name)(chip_idx, shard)


def _gather_params(bufs, shard_shapes, kinds, *, name):
    n = len(bufs)

    def body(*refs):
        outs = refs[n:2 * n]
        send_sems, recv_sems = refs[2 * n:]
        mx, my, mc = lax.axis_index("x"), lax.axis_index("y"), lax.axis_index("c")
        me, sibling = (mx, my, mc), (mx, my, 1 - mc)
        chips = [(1 - mx, my), (mx, 1 - my), (1 - mx, 1 - my)]

        def blk(i, px, py, pc):
            r, c = shard_shapes[i]
            j = 2 * px + py
            if kinds[i] == "row":
                return outs[i].at[pc, pl.ds(pl.multiple_of(j * r, r), r)]
            if kinds[i] == "col":
                return outs[i].at[pc, :, pl.ds(pl.multiple_of(j * c, c), c)]
            return outs[i].at[pc, j]

        def copy(i, k, block, to):
            return pltpu.make_async_remote_copy(
                src_ref=blk(i, *block), dst_ref=blk(i, *block), send_sem=send_sems.at[6 * i + k],
                recv_sem=recv_sems.at[6 * i + k], device_id=to, device_id_type=_MESH)

        sent = []
        for i in range(n):
            for j, chip in enumerate(chips):
                cp = copy(i, j, me, (*chip, mc))
                cp.start()
                sent.append(cp)
        for j, chip in enumerate(chips):
            for i in range(n):
                copy(i, j, (*chip, mc), me).wait_recv()
                fwd = copy(i, 3 + j, (*chip, mc), sibling)
                fwd.start()
                sent.append(fwd)
        for i in range(n):
            for j, chip in enumerate(chips):
                copy(i, 3 + j, (*chip, 1 - mc), me).wait_recv()
        for cp in sent:
            cp.wait_send()

    return pl.pallas_call(
        body, out_shape=[jax.ShapeDtypeStruct(b.shape, b.dtype) for b in bufs], in_specs=[_ANY] * n, out_specs=[_ANY] * n,
        input_output_aliases={i: i for i in range(n)},
        scratch_shapes=[pltpu.SemaphoreType.DMA((6 * n,)), pltpu.SemaphoreType.DMA((6 * n,))], name=name)(*bufs)


def _half(r, h):
    return pl.ds(pl.multiple_of(h * (r // 2), r // 2), r // 2)


_HBM = pl.BlockSpec(memory_space=pltpu.HBM)
_SEM = pl.BlockSpec(memory_space=pltpu.SEMAPHORE)
_EFFECT = pltpu.SideEffectType.DATAFLOW_SIDE_EFFECTING


def _sibling_copies(g_refs, land_refs, gs, views, send_sems, recv_sems):
    mx, my, mc = lax.axis_index("x"), lax.axis_index("y"), lax.axis_index("c")
    copies = []
    for i in range(len(gs)):
        if views[i] == "chip":
            src = g_refs[i].at[:, _half(gs[i].shape[1], 1 - mc)]
        else:
            src = g_refs[i].at[_half(gs[i].shape[0], 1 - mc)]
        copies.append(pltpu.make_async_remote_copy(src_ref=src, dst_ref=land_refs[i], send_sem=send_sems.at[i], recv_sem=recv_sems.at[i],
                                                   device_id=(mx, my, 1 - mc), device_id_type=_MESH))
    return copies


def _half_shape(g, view):
    return (g.shape[0], g.shape[1] // 2, g.shape[2]) if view == "chip" else (g.shape[0] // 2, g.shape[1])


def _grads_to_sibling_start(gs, views, *, name):
    n = len(gs)
    lands = [pltpu.with_memory_space_constraint(lax.empty(_half_shape(g, v), g.dtype), pltpu.HBM) for g, v in zip(gs, views)]

    def body(*refs):
        for cp in _sibling_copies(refs[:n], refs[n:2 * n], gs, views, refs[2 * n], refs[2 * n + 1]):
            cp.start()
        refs[-1][...] = jnp.zeros_like(refs[-1])

    outs = pl.pallas_call(
        body, name=name,
        out_shape=(pltpu.SemaphoreType.DMA((n,)), pltpu.SemaphoreType.DMA((n,)),
                   *[pltpu.HBM(x.shape, x.dtype) for x in list(gs) + lands], jax.ShapeDtypeStruct((SUBLANE, LANE), F32)),
        in_specs=[_HBM] * (2 * n), out_specs=(_SEM, _SEM, *[_HBM] * (2 * n), pl.BlockSpec(memory_space=pltpu.VMEM)),
        input_output_aliases={i: 2 + i for i in range(2 * n)},
        compiler_params=pltpu.CompilerParams(has_side_effects=_EFFECT),
    )(*[pltpu.with_memory_space_constraint(g, pltpu.HBM) for g in gs], *lands)
    return outs[0], outs[1], list(outs[2:2 + n]), list(outs[2 + n:2 + 2 * n]), outs[-1]


def _grads_to_sibling_wait(send_sems, recv_sems, gs, lands, views, after, *, name):
    n = len(gs)

    def body(*refs):
        for cp in _sibling_copies(refs[:n], refs[n:2 * n], gs, views, refs[2 * n], refs[2 * n + 1]):
            cp.wait_send()
            cp.wait_recv()

    outs = pl.pallas_call(
        body, name=name, out_shape=tuple(pltpu.HBM(x.shape, x.dtype) for x in list(gs) + list(lands)),
        in_specs=[_HBM] * (2 * n) + [_SEM, _SEM, _ANY], out_specs=tuple([_HBM] * (2 * n)),
        input_output_aliases={i: i for i in range(2 * n)},
        compiler_params=pltpu.CompilerParams(has_side_effects=_EFFECT),
    )(*gs, *lands, send_sems, recv_sems, after)
    return list(outs[:n]), list(outs[n:])


def _cast_place_layer(shard, l, kind, chip_idx, after, *, name):
    _, r, c = shard.shape
    tr = _row_tile(r, c * 4, 16)
    nt = r // tr

    def body(_, s_ref, *rest):
        rest[-1][...] = s_ref[...].astype(MXU_DTYPE)

    if kind == "row":
        out_spec = pl.BlockSpec((tr, c), lambda i, j_ref: (j_ref[0] * nt + i, 0))
    elif kind == "col":
        out_spec = pl.BlockSpec((tr, c), lambda i, j_ref: (i, j_ref[0]))
    else:
        out_spec = pl.BlockSpec((None, tr, c), lambda i, j_ref: (j_ref[0], i, 0))
    extra = [] if after is None else [after]
    grid_spec = pltpu.PrefetchScalarGridSpec(
        num_scalar_prefetch=1, grid=(nt,), in_specs=[pl.BlockSpec((None, tr, c), lambda i, j_ref: (l, i, 0))] + [_ANY] * len(extra),
        out_specs=out_spec)
    return pl.pallas_call(body, grid_spec=grid_spec, out_shape=jax.ShapeDtypeStruct(_gather_shape(r, c, kind)[1:], MXU_DTYPE),
                          compiler_params=_params("parallel"), name=name)(chip_idx, shard, *extra)


def _half_block(ref, kind, r, c, j, h):
    rows = _half(r, h)
    if kind == "row":
        return ref.at[pl.ds(pl.multiple_of(j * r + h * (r // 2), r // 2), r // 2)]
    if kind == "col":
        return ref.at[rows, pl.ds(pl.multiple_of(j * c, c), c)]
    return ref.at[j, rows]


def _gather_ici_copies(buf_refs, shapes, kinds, send_sems, recv_sems):
    mx, my, mc = lax.axis_index("x"), lax.axis_index("y"), lax.axis_index("c")
    chips = [(1 - mx, my), (mx, 1 - my), (1 - mx, 1 - my)]
    copies = []
    for i, (r, c) in enumerate(shapes):
        mine = _half_block(buf_refs[i], kinds[i], r, c, 2 * mx + my, mc)
        for k, (px, py) in enumerate(chips):
            copies.append(pltpu.make_async_remote_copy(
                src_ref=mine, dst_ref=mine, send_sem=send_sems.at[3 * i + k], recv_sem=recv_sems.at[3 * i + k],
                device_id=(px, py, mc), device_id_type=_MESH))
    return copies


def _gather_start(bufs, shapes, kinds, *, name):
    n = len(bufs)

    def body(*refs):
        send_sems, recv_sems, token = refs[n], refs[n + 1], refs[-1]
        for cp in _gather_ici_copies(refs[:n], shapes, kinds, send_sems, recv_sems):
            cp.start()
        token[...] = jnp.zeros_like(token)

    outs = pl.pallas_call(
        body, name=name,
        out_shape=(pltpu.SemaphoreType.DMA((3 * n,)), pltpu.SemaphoreType.DMA((3 * n,)),
                   *[pltpu.HBM(b.shape, b.dtype) for b in bufs], jax.ShapeDtypeStruct((SUBLANE, LANE), F32)),
        in_specs=[_HBM] * n, out_specs=(_SEM, _SEM, *[_HBM] * n, pl.BlockSpec(memory_space=pltpu.VMEM)),
        input_output_aliases={i: 2 + i for i in range(n)},
        compiler_params=pltpu.CompilerParams(has_side_effects=_EFFECT),
    )(*[pltpu.with_memory_space_constraint(b, pltpu.HBM) for b in bufs])
    return outs[0], outs[1], list(outs[2:2 + n]), outs[-1]


def _gather_wait(send_sems, recv_sems, bufs, shapes, kinds, after, *, name):
    n = len(bufs)

    def body(*refs):
        for cp in _gather_ici_copies(refs[:n], shapes, kinds, refs[n], refs[n + 1]):
            cp.wait_send()
            cp.wait_recv()

    outs = pl.pallas_call(
        body, name=name, out_shape=tuple(pltpu.HBM(b.shape, b.dtype) for b in bufs),
        in_specs=[_HBM] * n + [_SEM, _SEM, _ANY], out_specs=tuple([_HBM] * n), input_output_aliases={i: i for i in range(n)},
        compiler_params=pltpu.CompilerParams(has_side_effects=_EFFECT),
    )(*bufs, send_sems, recv_sems, after)
    return list(outs)


def _gather_forward(bufs, shapes, kinds, *, name):
    n = len(bufs)

    def body(*refs):
        outs = refs[n:2 * n]
        send_sems, recv_sems = refs[2 * n:]
        mx, my, mc = lax.axis_index("x"), lax.axis_index("y"), lax.axis_index("c")
        chips = [(1 - mx, my), (mx, 1 - my), (1 - mx, 1 - my)]
        copies = []
        for i, (r, c) in enumerate(shapes):
            for k, (px, py) in enumerate(chips):
                got = _half_block(outs[i], kinds[i], r, c, 2 * px + py, mc)
                cp = pltpu.make_async_remote_copy(src_ref=got, dst_ref=got, send_sem=send_sems.at[3 * i + k],
                                                  recv_sem=recv_sems.at[3 * i + k], device_id=(mx, my, 1 - mc), device_id_type=_MESH)
                cp.start()
                copies.append(cp)
        for cp in copies:
            cp.wait()

    return pl.pallas_call(
        body, out_shape=[jax.ShapeDtypeStruct(b.shape, b.dtype) for b in bufs], in_specs=[_ANY] * n, out_specs=[_ANY] * n,
        input_output_aliases={i: i for i in range(n)},
        scratch_shapes=[pltpu.SemaphoreType.DMA((3 * n,)), pltpu.SemaphoreType.DMA((3 * n,))], name=name)(*bufs)


def _chip_exchange_copies(pair_refs, land_refs, pairs, views, send_sems, recv_sems):
    mx, my, mc = lax.axis_index("x"), lax.axis_index("y"), lax.axis_index("c")
    me = 2 * mx + my
    chips = [(1 - mx, my), (mx, 1 - my), (1 - mx, 1 - my)]
    copies = []
    for i in range(len(pairs)):
        for k, (px, py) in enumerate(chips):
            j = 2 * px + py
            if views[i] == "chip":
                src = pair_refs[i].at[j]
            else:
                c = pairs[i].shape[1] // N_CHIPS
                src = pair_refs[i].at[:, pl.ds(pl.multiple_of(j * c, c), c)]
            copies.append(pltpu.make_async_remote_copy(
                src_ref=src, dst_ref=land_refs[i].at[me], send_sem=send_sems.at[3 * i + k], recv_sem=recv_sems.at[3 * i + k],
                device_id=(px, py, mc), device_id_type=_MESH))
    return copies


def _quad_shape(p, view):
    return p.shape if view == "chip" else (N_CHIPS, p.shape[0], p.shape[1] // N_CHIPS)


def _grads_to_chips_start(pairs, views, *, name):
    n = len(pairs)
    lands = [pltpu.with_memory_space_constraint(lax.empty(_quad_shape(p, v), p.dtype), pltpu.HBM) for p, v in zip(pairs, views)]

    def body(*refs):
        pair_refs, land_refs = refs[:n], refs[n:2 * n]
        send_sems, recv_sems = refs[2 * n], refs[2 * n + 1]
        token = refs[-1]
        for cp in _chip_exchange_copies(pair_refs, land_refs, pairs, views, send_sems, recv_sems):
            cp.start()
        token[...] = jnp.zeros_like(token)

    outs = pl.pallas_call(
        body, name=name,
        out_shape=(pltpu.SemaphoreType.DMA((3 * n,)), pltpu.SemaphoreType.DMA((3 * n,)),
                   *[pltpu.HBM(p.shape, p.dtype) for p in pairs], *[pltpu.HBM(l.shape, l.dtype) for l in lands],
                   jax.ShapeDtypeStruct((SUBLANE, LANE), F32)),
        in_specs=[_HBM] * (2 * n), out_specs=(_SEM, _SEM, *[_HBM] * (2 * n), pl.BlockSpec(memory_space=pltpu.VMEM)),
        input_output_aliases={i: 2 + i for i in range(2 * n)},
        compiler_params=pltpu.CompilerParams(has_side_effects=_EFFECT),
    )(*[pltpu.with_memory_space_constraint(p, pltpu.HBM) for p in pairs], *lands)
    return outs[0], outs[1], list(outs[2:2 + n]), list(outs[2 + n:2 + 2 * n]), outs[-1]


def _grads_to_chips_wait(send_sems, recv_sems, pairs, lands, views, after, *, name):
    n = len(pairs)

    def body(*refs):
        pair_refs, land_refs = refs[:n], refs[n:2 * n]
        s_sems, r_sems = refs[2 * n], refs[2 * n + 1]
        for cp in _chip_exchange_copies(pair_refs, land_refs, pairs, views, s_sems, r_sems):
            cp.wait_send()
            cp.wait_recv()

    outs = pl.pallas_call(
        body, name=name, out_shape=tuple(pltpu.HBM(x.shape, x.dtype) for x in list(pairs) + list(lands)),
        in_specs=[_HBM] * (2 * n) + [_SEM, _SEM, _ANY], out_specs=tuple([_HBM] * (2 * n)),
        input_output_aliases={i: i for i in range(2 * n)},
        compiler_params=pltpu.CompilerParams(has_side_effects=_EFFECT),
    )(*pairs, *lands, send_sems, recv_sems, after)
    return list(outs[n:])


def _grads_share(tots, *, name):
    n = len(tots)

    def body(*refs):
        ins, outs = refs[:n], refs[n:2 * n]
        send_sems, recv_sems = refs[2 * n:]
        mx, my, mc = lax.axis_index("x"), lax.axis_index("y"), lax.axis_index("c")
        copies = []
        for i in range(n):
            cp = pltpu.make_async_remote_copy(src_ref=ins[i], dst_ref=outs[i], send_sem=send_sems.at[i], recv_sem=recv_sems.at[i],
                                              device_id=(mx, my, 1 - mc), device_id_type=_MESH)
            cp.start()
            copies.append(cp)
        for cp in copies:
            cp.wait()

    return pl.pallas_call(
        body, out_shape=[jax.ShapeDtypeStruct(t.shape, t.dtype) for t in tots], in_specs=[_ANY] * n, out_specs=[_ANY] * n,
        scratch_shapes=[pltpu.SemaphoreType.DMA((n,)), pltpu.SemaphoreType.DMA((n,))], name=name)(*tots)


def _pair_sum(g, recv, view, c_idx, *, name):
    def body(c_ref, a_ref, b_ref, o_ref):
        o_ref[...] = (a_ref[...] + b_ref[...]).astype(WIRE_DTYPE)

    if view == "chip":
        nch, r, c = g.shape
        tr = _row_tile(r // 2, c * 4, 16)
        gv = g.reshape(nch, 2, r // 2, c)
        grid = (nch, (r // 2) // tr)
        in_specs = [pl.BlockSpec((None, None, tr, c), lambda j, i, c_ref: (j, c_ref[0], i, 0)),
                    pl.BlockSpec((None, tr, c), lambda j, i, c_ref: (j, i, 0))]
        out_spec = pl.BlockSpec((None, tr, c), lambda j, i, c_ref: (j, i, 0))
        sem = ("parallel", "parallel")
    else:
        r, c4 = g.shape
        tr = _row_tile(r // 2, c4 * 4, 16)
        gv = g.reshape(2, r // 2, c4)
        grid = ((r // 2) // tr,)
        in_specs = [pl.BlockSpec((None, tr, c4), lambda i, c_ref: (c_ref[0], i, 0)), pl.BlockSpec((tr, c4), lambda i, c_ref: (i, 0))]
        out_spec = pl.BlockSpec((tr, c4), lambda i, c_ref: (i, 0))
        sem = ("parallel",)
    grid_spec = pltpu.PrefetchScalarGridSpec(num_scalar_prefetch=1, grid=grid, in_specs=in_specs, out_specs=out_spec)
    return pl.pallas_call(body, grid_spec=grid_spec, out_shape=jax.ShapeDtypeStruct(recv.shape, WIRE_DTYPE),
                          compiler_params=_params(*sem), name=name)(c_idx, gv, recv)


def _quad_sum(gs, recvs, quads, view, chip_idx, c_idx, *, name):
    nl = len(quads)
    nch, rh, c = quads[0].shape
    tr = _row_tile(rh, c * 4, 16)

    def body(_, __, *refs):
        o_ref = refs[-1]
        per = nch + 1
        for l in range(nl):
            grp = refs[l * per:(l + 1) * per]
            acc = grp[0][...] + grp[1][...]
            for r in grp[2:]:
                acc = acc + r[...].astype(F32)
            o_ref[l] = acc

    if view == "chip":
        own = [pl.BlockSpec((None, None, tr, c), lambda i, j, h: (j[0], h[0], i, 0)),
               pl.BlockSpec((None, tr, c), lambda i, j, h: (j[0], i, 0))]
        gviews = [g.reshape(nch, 2, rh, c) for g in gs]
    else:
        own = [pl.BlockSpec((None, tr, c), lambda i, j, h: (h[0], i, j[0])), pl.BlockSpec((tr, c), lambda i, j, h: (i, j[0]))]
        gviews = [g.reshape(2, rh, nch * c) for g in gs]
    assert nch & (nch - 1) == 0
    got = [pl.BlockSpec((None, tr, c), functools.partial(lambda i, j, h, k: ((j[0] + k) & (nch - 1), i, 0), k=k))
           for k in range(1, nch)]
    ins = []
    for l in range(nl):
        ins += [gviews[l], recvs[l]] + [quads[l]] * (nch - 1)
    grid_spec = pltpu.PrefetchScalarGridSpec(
        num_scalar_prefetch=2, grid=(rh // tr,), in_specs=(own + got) * nl,
        out_specs=pl.BlockSpec((nl, tr, c), lambda i, j, h: (0, i, 0)))
    return pl.pallas_call(body, grid_spec=grid_spec, out_shape=jax.ShapeDtypeStruct((nl, rh, c), F32),
                          compiler_params=_params("parallel"), name=name)(chip_idx, c_idx, *ins)


def _sum_devices(g8, own, dev_idx, *, name):
    k, rows, cols = g8.shape

    def body(d_ref, a_ref, x_ref, o_ref):
        acc = None
        for i in range(k):
            term = jnp.where(d_ref[0] == i, x_ref[...], a_ref[i])
            acc = term if acc is None else acc + term
        o_ref[...] = acc

    grid_spec = pltpu.PrefetchScalarGridSpec(
        num_scalar_prefetch=1, grid=(1,),
        in_specs=[pl.BlockSpec((k, rows, cols), lambda i, d_ref: (0, 0, 0)), pl.BlockSpec((rows, cols), lambda i, d_ref: (0, 0))],
        out_specs=pl.BlockSpec((rows, cols), lambda i, d_ref: (0, 0)))
    return pl.pallas_call(body, grid_spec=grid_spec, out_shape=jax.ShapeDtypeStruct((rows, cols), g8.dtype),
                          compiler_params=_params("arbitrary"), name=name)(dev_idx, g8, own)


def _adamw(w, g, m, v, *, name):
    rows, cols = w.shape
    tr = rows
    for cand in (256, 128, 64, 32, 16, 8):
        if rows % cand == 0 and cand * cols <= 512 * 1024:
            tr = cand
            break
    c1 = 1.0 - ADAM_B1 ** ADAM_STEP
    c2 = 1.0 - ADAM_B2 ** ADAM_STEP

    def body(w_ref, g_ref, m_ref, v_ref, d_ref, nm_ref, nv_ref):
        gv = g_ref[...]
        nm = ADAM_B1 * m_ref[...] + (1.0 - ADAM_B1) * gv
        nv = ADAM_B2 * v_ref[...] + (1.0 - ADAM_B2) * (gv * gv)
        d_ref[...] = -ADAM_LR * ((nm / c1) / (jnp.sqrt(nv / c2) + ADAM_EPS) + ADAM_WD * w_ref[...])
        nm_ref[...] = nm
        nv_ref[...] = nv

    spec = pl.BlockSpec((tr, cols), lambda i: (i, 0))
    shp = jax.ShapeDtypeStruct((rows, cols), F32)
    return pl.pallas_call(body, grid=(rows // tr,), in_specs=[spec] * 4, out_specs=[spec] * 3, out_shape=[shp] * 3,
                          compiler_params=_params("parallel"), name=name)(w, g, m, v)


def _adamw_halves(w, m, v, mine, other, c_idx, *, name):
    nl, r, c = w.shape
    rh = r // 2
    tr = _row_tile(rh, c * 4)
    c1 = 1.0 - ADAM_B1 ** ADAM_STEP
    c2 = 1.0 - ADAM_B2 ** ADAM_STEP

    def body(c_ref, w_ref, m_ref, v_ref, a_ref, b_ref, g_ref, d_ref, nm_ref, nv_ref):
        gv = jnp.where(pl.program_id(1) == c_ref[0], a_ref[...], b_ref[...])
        nm = ADAM_B1 * m_ref[...] + (1.0 - ADAM_B1) * gv
        nv = ADAM_B2 * v_ref[...] + (1.0 - ADAM_B2) * (gv * gv)
        g_ref[...] = gv
        d_ref[...] = -ADAM_LR * ((nm / c1) / (jnp.sqrt(nv / c2) + ADAM_EPS) + ADAM_WD * w_ref[...])
        nm_ref[...] = nm
        nv_ref[...] = nv

    full = pl.BlockSpec((None, None, tr, c), lambda l, h, i, c_ref: (l, h, i, 0))
    half = pl.BlockSpec((None, tr, c), lambda l, h, i, c_ref: (l, i, 0))
    grid_spec = pltpu.PrefetchScalarGridSpec(num_scalar_prefetch=1, grid=(nl, 2, rh // tr),
                                             in_specs=[full] * 3 + [half] * 2, out_specs=[full] * 4)
    shp = jax.ShapeDtypeStruct((nl, 2, rh, c), F32)
    view = (nl, 2, rh, c)
    outs = pl.pallas_call(body, grid_spec=grid_spec, out_shape=[shp] * 4, compiler_params=_params("parallel", "parallel", "parallel"),
                          name=name)(c_idx, w.reshape(view), m.reshape(view), v.reshape(view), mine, other)
    return [o.reshape(nl, r, c) for o in outs]


WEIGHTS = ["mem_ln_g", "mem_ln_b", "w_in", "sg_ln_g", "sg_ln_b", "sg_w", "sg_b", "conv_w", "conv_b", "dt_bias", "a_log",
           "d_skip", "ssm_norm_g", "p_a", "p_b", "w_mix_o", "w_xq", "w_xkv", "w_xo", "w_ffn_in", "w_ffn_out", "ln_g", "ln_b"]
ARG_NAMES = ["x", "mem"] + WEIGHTS + ["loss_target"] + ["m_" + n for n in WEIGHTS] + ["v_" + n for n in WEIGHTS]
BIG = {"w_in": (1, (1024, 9248)), "p_a": (0, (1024, 1024)), "p_b": (0, (2048, 1024)), "w_mix_o": (0, (1024, 1024)),
       "w_xq": (0, (1024, 1024)), "w_xkv": (1, (1024, 2048)), "w_xo": (0, (1024, 1024)), "w_ffn_in": (1, (1024, 5632)),
       "w_ffn_out": (0, (2816, 1024))}
SMALL_SHARDED = {"conv_w": (4, 3072), "ln_g": (3, 1024), "ln_b": (3, 1024)}
SMALL = [n for n in WEIGHTS if n not in BIG]
W_IN_MAP = ((0, 4096, "main", 0), (4096, 7168, "main", XBC_COL0), (7168, 7200, "dt", 0), (7200, 9248, "main", GAB_COL0))
W_IN_SHARD = 9248 // N_CHIPS


def _w_in_chip_major(gm, gd):
    src = {"main": gm, "dt": gd}
    blocks = []
    for j in range(N_CHIPS):
        lo, hi = j * W_IN_SHARD, (j + 1) * W_IN_SHARD
        parts = [src[k][:, o + max(lo, a) - a:o + min(hi, b) - a] for a, b, k, o in W_IN_MAP if max(lo, a) < min(hi, b)]
        blocks.append(jnp.concatenate(parts, axis=1))
    return jnp.stack(blocks)


def _w_in_reassemble(wc):
    def cols(a, b):
        out = []
        for j in range(N_CHIPS):
            lo, hi = max(a, j * W_IN_SHARD), min(b, (j + 1) * W_IN_SHARD)
            if lo < hi:
                out.append(wc[j][:, lo - j * W_IN_SHARD:hi - j * W_IN_SHARD])
        return out

    main = sorted((m for m in W_IN_MAP if m[2] == "main"), key=lambda m: m[3])
    w_main = jnp.concatenate([p for a, b, _, _ in main for p in cols(a, b)], axis=1)
    (a, b, _, _), = [m for m in W_IN_MAP if m[2] == "dt"]
    w_dt = jnp.pad(jnp.concatenate(cols(a, b), axis=1), ((0, 0), (0, HEAD_PAD - (b - a))))
    return w_main, w_dt
GATHER_KIND = {"w_in": "chip", "p_a": "row", "p_b": "row", "w_mix_o": "row", "w_xq": "row", "w_xkv": "col", "w_xo": "row",
               "w_ffn_in": "col", "w_ffn_out": "row", "conv_w": "chip", "ln_g": "chip", "ln_b": "chip"}
GRAD_VIEW = {n: ("col" if k == "col" else "chip") for n, k in GATHER_KIND.items() if n in BIG}


def _shard_shape(name):
    axis, (r, c) = BIG[name]
    return (r // N_CHIPS, c) if axis == 0 else (r, c // N_CHIPS)


def _pad_rows(flat, cols, row_mult):
    n = flat.shape[0]
    rows = -(-n // cols)
    rows = -(-rows // row_mult) * row_mult
    return jnp.pad(flat, (0, rows * cols - n)).reshape(rows, cols)


def _gather_small_params(a, chip):
    names = list(SMALL_SHARDED)
    kinds = [GATHER_KIND[n] for n in names]
    bufs = [_cast_place(a[n], GATHER_KIND[n], F32, chip.reshape(1), name=f"place_{n}") for n in names]
    outs = _gather_params(bufs, [a[n].shape[1:] for n in names], kinds, name="gather_small_params")
    full = {}
    for n, o in zip(names, outs):
        _, _, r, c = o.shape
        full[n] = jnp.transpose(o, (0, 2, 1, 3)).reshape(DEPTH, r, N_CHIPS * c)
    return full


GATHER_GROUPS = (("w_in",), tuple(n for n in BIG if n != "w_in"))


def _gather_group_start(a, l, names, chip, after, *, tag):
    bufs = [_cast_place_layer(a[n], l, GATHER_KIND[n], chip.reshape(1), after, name=f"place_{n}_l{l}") for n in names]
    return _gather_start(bufs, [a[n].shape[1:] for n in names], [GATHER_KIND[n] for n in names], name=f"gather_start_{tag}")


def _gather_group_finish(a, names, flight, after, *, tag):
    send_sems, recv_sems, bufs, token = flight
    shapes, kinds = [a[n].shape[1:] for n in names], [GATHER_KIND[n] for n in names]
    bufs = _gather_wait(send_sems, recv_sems, bufs, shapes, kinds, token if after is None else after, name=f"gather_wait_{tag}")
    full = dict(zip(names, _gather_forward(bufs, shapes, kinds, name=f"gather_forward_{tag}")))
    if "w_in" in full:
        full["w_main"], full["w_dt"] = _w_in_reassemble(full.pop("w_in"))
    return full


def _layer_weights(a, big, small, l):
    w = dict(big)
    for n in SMALL_SHARDED:
        w[n] = small[n][l]
    for n in ["sg_ln_g", "sg_ln_b", "sg_w", "conv_b", "ssm_norm_g"]:
        w[n] = a[n][l]
    w["sg_bcol"] = a["sg_b"][l][..., None]
    for n in ["dt_bias", "a_log"]:
        w[n + "8"] = _pad_heads(a[n][l])
    w["d_skipx"] = _expand_heads(a["d_skip"][l])
    return w


def _grad_views(grads, names):
    gs = []
    for n in names:
        axis, _ = BIG[n]
        r, c = _shard_shape(n)
        if n == "w_in":
            gs.append(_w_in_chip_major(grads["w_main"], grads["w_dt"]))
        elif axis == 0:
            gs.append(grads[n].reshape(N_CHIPS, r, c))
        else:
            gs.append(grads[n])
    return gs


class _GradExchange:
    def __init__(self, grads, names, c_idx, tag):
        self.names, self.c_idx, self.tag = names, c_idx, tag
        self.views = [GRAD_VIEW[n] for n in names]
        self.gs = _grad_views(grads, names)

    def start(self):
        self.sems = _grads_to_sibling_start(self.gs, self.views, name=f"grads_to_sibling_start_{self.tag}")
        return self.sems[4]

    def cross(self, after):
        send_sems, recv_sems, gs, lands, token = self.sems
        self.gs, self.recv = _grads_to_sibling_wait(send_sems, recv_sems, gs, lands, self.views, token if after is None else after,
                                                    name=f"grads_to_sibling_wait_{self.tag}")
        cpre = self.c_idx.reshape(1)
        pairs = [_pair_sum(g, rv, v, cpre, name=f"grads_pair_sum_{n}_{self.tag}")
                 for g, rv, v, n in zip(self.gs, self.recv, self.views, self.names)]
        self.sems = _grads_to_chips_start(pairs, self.views, name=f"grads_to_chips_start_{self.tag}")
        return self.sems[4]

    def finish(self, after):
        send_sems, recv_sems, pairs, lands, _ = self.sems
        quads = _grads_to_chips_wait(send_sems, recv_sems, pairs, lands, self.views, after, name=f"grads_to_chips_wait_{self.tag}")
        return {n: (g, rv, q) for n, g, rv, q in zip(self.names, self.gs, self.recv, quads)}


def _finish_big_grads(parts, c_idx, chip):
    tots = [_quad_sum([parts[l][n][0] for l in range(DEPTH)], [parts[l][n][1] for l in range(DEPTH)],
                      [parts[l][n][2] for l in range(DEPTH)], GRAD_VIEW[n], chip.reshape(1), c_idx.reshape(1),
                      name=f"grads_chip_sum_{n}") for n in BIG]
    others = _grads_share(tots, name="grads_share")
    return {n: (t, o) for n, t, o in zip(BIG, tots, others)}


def _direct_copies(x_ref, land_ref, send_sems, recv_sems):
    mx, my, mc = lax.axis_index("x"), lax.axis_index("y"), lax.axis_index("c")
    me = 4 * mx + 2 * my + mc
    copies = []
    for k in range(N_DEV - 1):
        f = k + 1
        to = (mx ^ (f >> 2 & 1), my ^ (f >> 1 & 1), mc ^ (f & 1))
        copies.append(pltpu.make_async_remote_copy(src_ref=x_ref, dst_ref=land_ref.at[me], send_sem=send_sems.at[k],
                                                   recv_sem=recv_sems.at[k], device_id=to, device_id_type=_MESH))
    return copies


def _all_gather8_start(x, *, name):
    land = pltpu.with_memory_space_constraint(lax.empty((N_DEV,) + x.shape, x.dtype), pltpu.HBM)

    def body(x_ref, land_ref, send_sems, recv_sems, x_out, land_out, token):
        for cp in _direct_copies(x_ref, land_ref, send_sems, recv_sems):
            cp.start()
        token[...] = jnp.zeros_like(token)

    n = N_DEV - 1
    return pl.pallas_call(
        body, name=name,
        out_shape=(pltpu.SemaphoreType.DMA((n,)), pltpu.SemaphoreType.DMA((n,)), pltpu.HBM(x.shape, x.dtype),
                   pltpu.HBM(land.shape, land.dtype), jax.ShapeDtypeStruct((SUBLANE, LANE), F32)),
        in_specs=[_HBM, _HBM], out_specs=(_SEM, _SEM, _HBM, _HBM, pl.BlockSpec(memory_space=pltpu.VMEM)),
        input_output_aliases={0: 2, 1: 3}, compiler_params=pltpu.CompilerParams(has_side_effects=_EFFECT),
    )(pltpu.with_memory_space_constraint(x, pltpu.HBM), land)


def _all_gather8_wait(send_sems, recv_sems, x, land, after, *, name):
    def body(x_ref, land_ref, s_sems, r_sems, _, x_out, land_out):
        for cp in _direct_copies(x_ref, land_ref, s_sems, r_sems):
            cp.wait_send()
            cp.wait_recv()

    return pl.pallas_call(
        body, name=name, out_shape=(pltpu.HBM(x.shape, x.dtype), pltpu.HBM(land.shape, land.dtype)),
        in_specs=[_HBM, _HBM, _SEM, _SEM, _ANY], out_specs=(_HBM, _HBM), input_output_aliases={0: 0, 1: 1},
        compiler_params=pltpu.CompilerParams(has_side_effects=_EFFECT),
    )(x, land, send_sems, recv_sems, after)


def _pack_small(small):
    return _pad_rows(jnp.concatenate([small[n].reshape(-1) for n in small]), LANE, SUBLANE)


def _unpack_small(small, g8, packed, chip, c_idx, *, name):
    names = list(small)
    tot = _sum_devices(g8, packed, (2 * chip + c_idx).reshape(1), name=name).reshape(-1)
    out, off = {}, 0
    for n in names:
        sz = small[n].size
        full = tot[off:off + sz].reshape(small[n].shape)
        off += sz
        if n in SMALL_SHARDED:
            cs = SMALL_SHARDED[n][1] // N_CHIPS
            full = lax.dynamic_slice_in_dim(full, chip * cs, cs, axis=-1)
        out[n] = full
    return out


def kernel(x, mem, mem_ln_g, mem_ln_b, w_in, sg_ln_g, sg_ln_b, sg_w, sg_b, conv_w, conv_b, dt_bias, a_log, d_skip, ssm_norm_g, p_a, p_b, w_mix_o, w_xq, w_xkv, w_xo, w_ffn_in, w_ffn_out, ln_g, ln_b, loss_target, m_mem_ln_g, m_mem_ln_b, m_w_in, m_sg_ln_g, m_sg_ln_b, m_sg_w, m_sg_b, m_conv_w, m_conv_b, m_dt_bias, m_a_log, m_d_skip, m_ssm_norm_g, m_p_a, m_p_b, m_w_mix_o, m_w_xq, m_w_xkv, m_w_xo, m_w_ffn_in, m_w_ffn_out, m_ln_g, m_ln_b, v_mem_ln_g, v_mem_ln_b, v_w_in, v_sg_ln_g, v_sg_ln_b, v_sg_w, v_sg_b, v_conv_w, v_conv_b, v_dt_bias, v_a_log, v_d_skip, v_ssm_norm_g, v_p_a, v_p_b, v_w_mix_o, v_w_xq, v_w_xkv, v_w_xo, v_w_ffn_in, v_w_ffn_out, v_ln_g, v_ln_b):
    a = dict(zip(ARG_NAMES, (x, mem, mem_ln_g, mem_ln_b, w_in, sg_ln_g, sg_ln_b, sg_w, sg_b, conv_w, conv_b, dt_bias, a_log, d_skip, ssm_norm_g, p_a, p_b, w_mix_o, w_xq, w_xkv, w_xo, w_ffn_in, w_ffn_out, ln_g, ln_b, loss_target, m_mem_ln_g, m_mem_ln_b, m_w_in, m_sg_ln_g, m_sg_ln_b, m_sg_w, m_sg_b, m_conv_w, m_conv_b, m_dt_bias, m_a_log, m_d_skip, m_ssm_norm_g, m_p_a, m_p_b, m_w_mix_o, m_w_xq, m_w_xkv, m_w_xo, m_w_ffn_in, m_w_ffn_out, m_ln_g, m_ln_b, v_mem_ln_g, v_mem_ln_b, v_w_in, v_sg_ln_g, v_sg_ln_b, v_sg_w, v_sg_b, v_conv_w, v_conv_b, v_dt_bias, v_a_log, v_d_skip, v_ssm_norm_g, v_p_a, v_p_b, v_w_mix_o, v_w_xq, v_w_xkv, v_w_xo, v_w_ffn_in, v_w_ffn_out, v_ln_g, v_ln_b)))
    c_idx = lax.axis_index("c").astype(jnp.int32)
    chip = (2 * lax.axis_index("x") + lax.axis_index("y")).astype(jnp.int32)

    small = _gather_small_params(a, chip)
    ga, gb = GATHER_GROUPS
    flights = {(0, 0): _gather_group_start(a, 0, ga, chip, small["ln_b"], tag="l0_a")}
    flights[0, 1] = _gather_group_start(a, 0, gb, chip, flights[0, 0][3], tag="l0_b")

    def layer_weights(after, l):
        first = _gather_group_finish(a, ga, flights[l, 0], after if l else flights[l, 1][3], tag=f"l{l}_a")

        def rest(w, after_b):
            more = _gather_group_finish(a, gb, flights[l, 1], after_b, tag=f"l{l}_b")
            if l + 1 < DEPTH:
                flights[l + 1, 0] = _gather_group_start(a, l + 1, ga, chip, more["p_a"], tag=f"l{l + 1}_a")
                flights[l + 1, 1] = _gather_group_start(a, l + 1, gb, chip, flights[l + 1, 0][3], tag=f"l{l + 1}_b")
                more["p_a"] = more["p_a"] + flights[l + 1, 1][3][0, 0].astype(MXU_DTYPE)
            return {k: v for k, v in {**w, **more}.items() if k != "rest"}

        return dict(_layer_weights(a, first, small, l), rest=rest)

    layers = [functools.partial(layer_weights, l=l) for l in range(DEPTH)]
    exchanges, seen, small_flight = [], {}, {}

    def start_exchange(l, names, grads_l):
        ex = _GradExchange(grads_l, names, c_idx, f"l{l}_{names[0]}")
        tokens = [ex.start()]
        if exchanges:
            tokens.append(exchanges[-1][1].cross(tokens[0]))
        exchanges.append((l, ex))
        seen[l] = grads_l
        if l == 0 and names == GRAD_GROUPS[-1]:
            tokens.append(ex.cross(None))
            small = {}
            for n in SMALL:
                if n.startswith("mem_ln"):
                    continue
                per_layer = []
                for k in range(DEPTH):
                    g = seen[k][n]
                    if n in ("dt_bias", "a_log", "d_skip"):
                        g = g[0, :SSM_HEADS]
                    per_layer.append(g.reshape(a[n].shape[1:-1] + (-1,)))
                small[n] = jnp.stack(per_layer)
            small_flight["small"] = small
            small_flight["sems"] = _all_gather8_start(_pack_small(small), name="gather_small_grads_start")
            tokens.append(small_flight["sems"][4])
        return sum(tokens[1:], tokens[0])

    lsum, grad_x, grads, d_mem_g, d_mem_b = _local_step(x, mem, loss_target, mem_ln_g, mem_ln_b, layers, start_exchange)
    loss = lax.psum(0.5 * jnp.sum(lsum) / D_MODEL, ("x", "y", "c"))

    parts = [{} for _ in range(DEPTH)]
    for l, ex in exchanges:
        parts[l].update(ex.finish(grad_x))
    halves = _finish_big_grads(parts, c_idx, chip)
    gw = {}
    send_sems, recv_sems, packed, land, _ = small_flight["sems"]
    packed, g8 = _all_gather8_wait(send_sems, recv_sems, packed, land, grad_x, name="gather_small_grads_wait")
    gw.update(_unpack_small(small_flight["small"], g8, packed, chip, c_idx, name="small_grads_sum"))
    mem_small = {"mem_ln_g": d_mem_g, "mem_ln_b": d_mem_b}
    mem_packed = _pack_small(mem_small)
    gw.update(_unpack_small(mem_small, _all_gather8(mem_packed, name="gather_mem_ln_grads"), mem_packed, chip, c_idx,
                            name="mem_ln_grads_sum"))

    delta, new_m, new_v = {}, {}, {}
    for n in BIG:
        mine, other = halves[n]
        gw[n], delta[n], new_m[n], new_v[n] = _adamw_halves(a[n], a["m_" + n], a["v_" + n], mine, other, c_idx.reshape(1),
                                                             name=f"adamw_{n}")
    for n in SMALL:
        shp = a[n].shape
        view = (-1, LANE) if a[n].size % LANE == 0 else (1, -1)
        outs = _adamw(*[v.reshape(view) for v in (a[n], gw[n], a["m_" + n], a["v_" + n])], name=f"adamw_{n}")
        delta[n], new_m[n], new_v[n] = (o.reshape(shp) for o in outs)
    return (loss, grad_x, *[gw[n].reshape(a[n].shape) for n in WEIGHTS], *[delta[n] for n in WEIGHTS],
            *[new_m[n] for n in WEIGHTS], *[new_v[n] for n in WEIGHTS])
```

```python
import functools
import math

import jax
import jax.numpy as jnp
from jax import lax
from jax.experimental import pallas as pl
from jax.experimental.pallas import tpu as pltpu

F32 = jnp.float32
MXU_DTYPE = jnp.bfloat16
WIRE_DTYPE = jnp.bfloat16
STASH_DTYPE = jnp.bfloat16

D_MODEL = 1024
DEPTH = 2
CHUNK = 128
SG_GROUPS = 8
SSM_INNER = 2048
SSM_HEADDIM = 64
SSM_HEADS = 32
SSM_STATE = 128
SSM_GROUPS = 4
SSM_CONV = 4
SSM_CONV_DIM = 3072
X_HEADS = 4
X_HEADDIM = 256
FFN_HIDDEN = 2816
ALPHA = float((2 * DEPTH) ** 0.25)
LN_EPS = 1e-5
RMS_EPS = 1e-5
ADAM_LR = 0.001
ADAM_B1 = 0.9
ADAM_B2 = 0.999
ADAM_EPS = 1e-08
ADAM_WD = 0.01
ADAM_STEP = 10

MAIN_COLS = 9216
UVZ_COLS = 4096
GAB_COL0 = 4096
XBC_COL0 = 6144
HEAD_PAD = 128

VMEM_LIMIT = 56 * 1024 * 1024
BLOCK_BYTES = 2 * 1024 * 1024
ROW_TILES = (512, 256, 128)
LANE = 128
SUBLANE = 8

N_CHIPS = 4
N_DEV = 8


def _pick(n, cands):
    for c in cands:
        if n % c == 0:
            return c
    return n


MM_TILE_MAX = 1408
MM_OPERAND_BYTES = 8 * 1024 * 1024


def _div_tile(n, limit):
    best = None
    for t in range(LANE, min(n, limit) + 1, LANE):
        if n % t == 0:
            best = t
    return n if best is None else best


def _params(*sem):
    return pltpu.CompilerParams(dimension_semantics=tuple(sem), vmem_limit_bytes=VMEM_LIMIT)


_ANY = pl.BlockSpec(memory_space=pl.ANY)
_MESH = pl.DeviceIdType.MESH


def _nt(a, b):
    return lax.dot_general(a, b, (((1,), (1,)), ((), ())), preferred_element_type=F32)


def _tn(a, b):
    return lax.dot_general(a, b, (((0,), (0,)), ((), ())), preferred_element_type=F32)


def _nn(a, b):
    return jnp.dot(a, b, preferred_element_type=F32)


def _sigmoid(x):
    return 0.5 * jnp.tanh(0.5 * x) + 0.5


def _split3(v):
    def top(x):
        bits = lax.bitcast_convert_type(x, jnp.uint32) & jnp.uint32(0xFFFF0000)
        return lax.bitcast_convert_type(bits, F32)

    v1 = top(v)
    r1 = v - v1
    v2 = top(r1)
    v3 = r1 - v2
    return v1.astype(jnp.bfloat16), v2.astype(jnp.bfloat16), v3.astype(jnp.bfloat16)


def _dot_exact(a, b, dn, data):
    if data == 0:
        mat = b.astype(jnp.bfloat16)
        return sum(lax.dot_general(p, mat, dn, preferred_element_type=F32) for p in _split3(a))
    mat = a.astype(jnp.bfloat16)
    return sum(lax.dot_general(mat, p, dn, preferred_element_type=F32) for p in _split3(b))


_DN_NN = (((1,), (0,)), ((), ()))
_DN_TN = (((0,), (0,)), ((), ()))


def _gelu(x):
    return 0.5 * x * (1.0 + lax.erf(x * (2.0 ** -0.5)))


def _gelu_grad(x):
    return 0.5 * (1.0 + lax.erf(x * (2.0 ** -0.5))) + x * jnp.exp(-0.5 * x * x) * (1.0 / math.sqrt(2.0 * math.pi))


def _mm(a, b, *, ta=False, tb=False, out_dtype=F32, after=None, name):
    if ta:
        kdim, m = a.shape
    else:
        m, kdim = a.shape
    if tb:
        n, k2 = b.shape[-2:]
    else:
        k2, n = b.shape[-2:]
    assert kdim == k2, (a.shape, b.shape, ta, tb)
    tm = _div_tile(m, MM_TILE_MAX)
    tn = _div_tile(n, MM_TILE_MAX)
    tk = _div_tile(kdim, MM_OPERAND_BYTES // (tm * a.dtype.itemsize + tn * b.dtype.itemsize))
    nk = kdim // tk
    dn = (((0 if ta else 1,), (1 if tb else 0,)), ((), ()))

    extra = [] if after is None else [after]

    def body(a_ref, b_ref, *rest):
        o_ref = rest[len(extra)]
        d = lax.dot_general(a_ref[...].astype(MXU_DTYPE), b_ref[...].astype(MXU_DTYPE), dn, preferred_element_type=F32)
        if nk == 1:
            o_ref[...] = d.astype(out_dtype)
            return
        acc_ref = rest[len(extra) + 1]
        k = pl.program_id(2)

        @pl.when(k == 0)
        def _():
            acc_ref[...] = d

        @pl.when(jnp.logical_and(k > 0, k < nk - 1))
        def _():
            acc_ref[...] += d

        @pl.when(k == nk - 1)
        def _():
            o_ref[...] = (acc_ref[...] + d).astype(out_dtype)

    a_spec = pl.BlockSpec((tk, tm), lambda i, j, k: (k, i)) if ta else pl.BlockSpec((tm, tk), lambda i, j, k: (i, k))
    b_spec = pl.BlockSpec((tn, tk), lambda i, j, k: (j, k)) if tb else pl.BlockSpec((tk, tn), lambda i, j, k: (k, j))
    return pl.pallas_call(
        body, grid=(m // tm, n // tn, nk), in_specs=[a_spec, b_spec] + [_ANY] * len(extra),
        out_specs=pl.BlockSpec((tm, tn), lambda i, j, k: (i, j)),
        out_shape=jax.ShapeDtypeStruct((m, n), out_dtype),
        scratch_shapes=[pltpu.VMEM((tm, tn), F32)] if nk > 1 else [],
        compiler_params=_params("parallel", "parallel", "arbitrary"), name=name)(a, b, *extra)


def _row_spec(tm, c, col=0):
    return pl.BlockSpec((tm, c), lambda i: (i, col))


def _par_spec(shape):
    nd = len(shape)
    return pl.BlockSpec(shape, lambda i: (0,) * nd)


def _ln_fwd(x, f, g, b, *, name):
    t, c = x.shape
    tm = _pick(t, ROW_TILES)
    has_f = f is not None

    def body(*refs):
        if has_f:
            x_ref, f_ref, g_ref, b_ref, y_ref, yb_ref, xh_ref, rs_ref = refs
            r = ALPHA * x_ref[...] + f_ref[...]
        else:
            x_ref, g_ref, b_ref, y_ref, yb_ref, xh_ref, rs_ref = refs
            r = x_ref[...]
        mu = jnp.mean(r, axis=-1, keepdims=True)
        xc = r - mu
        var = jnp.mean(xc * xc, axis=-1, keepdims=True)
        rstd = lax.rsqrt(var + LN_EPS)
        xh = xc * rstd
        y = xh * g_ref[...] + b_ref[...]
        y_ref[...] = y
        yb_ref[...] = y.astype(MXU_DTYPE)
        xh_ref[...] = xh
        rs_ref[...] = jnp.broadcast_to(rstd, rs_ref.shape)

    ins = [x] + ([f] if has_f else []) + [g.reshape(1, c), b.reshape(1, c)]
    in_specs = [_row_spec(tm, c)] * (2 if has_f else 1) + [_par_spec((1, c))] * 2
    return pl.pallas_call(
        body, grid=(t // tm,), in_specs=in_specs,
        out_specs=[_row_spec(tm, c), _row_spec(tm, c), _row_spec(tm, c), _row_spec(tm, LANE)],
        out_shape=[jax.ShapeDtypeStruct((t, c), F32), jax.ShapeDtypeStruct((t, c), MXU_DTYPE),
                   jax.ShapeDtypeStruct((t, c), F32), jax.ShapeDtypeStruct((t, LANE), F32)],
        compiler_params=_params("parallel"), name=name)(*ins)


def _ln_bwd(addends, scales, xh, rs, g, *, name):
    t, c = xh.shape
    tm = _pick(t, ROW_TILES)
    na = len(addends)

    def body(*refs):
        a_refs = refs[:na]
        xh_ref, rs_ref, g_ref, dp_ref, dpb_ref, dg_ref, db_ref = refs[na:]

        @pl.when(pl.program_id(0) == 0)
        def _():
            dg_ref[...] = jnp.zeros_like(dg_ref)
            db_ref[...] = jnp.zeros_like(db_ref)

        dy = None
        for s, r in zip(scales, a_refs):
            term = r[...] if s == 1.0 else s * r[...]
            dy = term if dy is None else dy + term
        xhv = xh_ref[...]
        dxh = dy * g_ref[...]
        m1 = jnp.mean(dxh, axis=-1, keepdims=True)
        m2 = jnp.mean(dxh * xhv, axis=-1, keepdims=True)
        dp = rs_ref[:, 0:1] * (dxh - m1 - xhv * m2)
        dp_ref[...] = dp
        dpb_ref[...] = dp.astype(MXU_DTYPE)
        dg_ref[...] += jnp.sum(dy * xhv, axis=0, keepdims=True)
        db_ref[...] += jnp.sum(dy, axis=0, keepdims=True)

    in_specs = [_row_spec(tm, c)] * (na + 1) + [_row_spec(tm, LANE), _par_spec((1, c))]
    return pl.pallas_call(
        body, grid=(t // tm,), in_specs=in_specs,
        out_specs=[_row_spec(tm, c), _row_spec(tm, c), _par_spec((1, c)), _par_spec((1, c))],
        out_shape=[jax.ShapeDtypeStruct((t, c), F32), jax.ShapeDtypeStruct((t, c), MXU_DTYPE),
                   jax.ShapeDtypeStruct((1, c), F32), jax.ShapeDtypeStruct((1, c), F32)],
        compiler_params=_params("arbitrary"), name=name)(*addends, xh, rs, g.reshape(1, c))


def _add_scaled(addends, scales, *, name):
    t, c = addends[0].shape
    tm = _pick(t, ROW_TILES)
    na = len(addends)

    def body(*refs):
        acc = None
        for s, r in zip(scales, refs[:na]):
            term = r[...] if s == 1.0 else s * r[...]
            acc = term if acc is None else acc + term
        refs[na][...] = acc

    return pl.pallas_call(
        body, grid=(t // tm,), in_specs=[_row_spec(tm, c)] * na, out_specs=_row_spec(tm, c),
        out_shape=jax.ShapeDtypeStruct((t, c), F32), compiler_params=_params("parallel"), name=name)(*addends)


def _loss_head(y, tgt, *, name):
    t, c = y.shape
    tm = _pick(t, ROW_TILES)

    def body(y_ref, t_ref, dy_ref, ls_ref):
        @pl.when(pl.program_id(0) == 0)
        def _():
            ls_ref[...] = jnp.zeros_like(ls_ref)

        e = y_ref[...] - t_ref[...]
        dy_ref[...] = e * (1.0 / c)
        ls_ref[...] += jnp.sum(e * e, axis=0, keepdims=True)

    return pl.pallas_call(
        body, grid=(t // tm,), in_specs=[_row_spec(tm, c)] * 2,
        out_specs=[_row_spec(tm, c), _par_spec((1, c))],
        out_shape=[jax.ShapeDtypeStruct((t, c), F32), jax.ShapeDtypeStruct((1, c), F32)],
        compiler_params=_params("arbitrary"), name=name)(y, tgt)


def _swiglu_fwd(h, *, name):
    t, two_f = h.shape
    fh = two_f // 2
    tm = _pick(t, (256, 128))

    def body(g_ref, u_ref, a_ref):
        g = g_ref[...].astype(F32)
        a_ref[...] = (g * _sigmoid(g) * u_ref[...].astype(F32)).astype(MXU_DTYPE)

    return pl.pallas_call(
        body, grid=(t // tm,), in_specs=[_row_spec(tm, fh, 0), _row_spec(tm, fh, 1)], out_specs=_row_spec(tm, fh),
        out_shape=jax.ShapeDtypeStruct((t, fh), MXU_DTYPE), compiler_params=_params("parallel"), name=name)(h, h)


def _swiglu_bwd(h, da, *, name):
    t, two_f = h.shape
    fh = two_f // 2
    tm = _pick(t, (256, 128))

    def body(g_ref, u_ref, da_ref, dh_ref):
        g = g_ref[...].astype(F32)
        s = _sigmoid(g)
        dav = da_ref[...]
        dh_ref[:, :fh] = (dav * u_ref[...].astype(F32) * (s * (1.0 + g * (1.0 - s)))).astype(MXU_DTYPE)
        dh_ref[:, fh:] = (dav * g * s).astype(MXU_DTYPE)

    return pl.pallas_call(
        body, grid=(t // tm,), in_specs=[_row_spec(tm, fh, 0), _row_spec(tm, fh, 1), _row_spec(tm, fh)],
        out_specs=_row_spec(tm, two_f), out_shape=jax.ShapeDtypeStruct((t, two_f), MXU_DTYPE),
        compiler_params=_params("parallel"), name=name)(h, h, da)


def _attn_probs(q, k):
    s = _nt(q, k) * (X_HEADDIM ** -0.5)
    s = s - jnp.max(s, axis=-1, keepdims=True)
    p = jnp.exp(s)
    return p / jnp.sum(p, axis=-1, keepdims=True)


def _attn_fwd(q, kv, *, bsz, name):
    t = q.shape[0]
    s = t // bsz
    ml = kv.shape[0] // bsz
    hd = X_HEADDIM

    def body(q_ref, k_ref, v_ref, o_ref):
        p = _attn_probs(q_ref[...], k_ref[...])
        o_ref[...] = _nn(p.astype(MXU_DTYPE), v_ref[...]).astype(MXU_DTYPE)

    return pl.pallas_call(
        body, grid=(bsz, X_HEADS),
        in_specs=[pl.BlockSpec((s, hd), lambda b, h: (b, h)), pl.BlockSpec((ml, hd), lambda b, h: (b, h)),
                  pl.BlockSpec((ml, hd), lambda b, h: (b, X_HEADS + h))],
        out_specs=pl.BlockSpec((s, hd), lambda b, h: (b, h)),
        out_shape=jax.ShapeDtypeStruct((t, D_MODEL), MXU_DTYPE),
        compiler_params=_params("parallel", "parallel"), name=name)(q, kv, kv)


def _attn_bwd(q, kv, do, *, bsz, name):
    t = q.shape[0]
    s = t // bsz
    ml = kv.shape[0] // bsz
    hd = X_HEADDIM

    def body(q_ref, k_ref, v_ref, do_ref, dq_ref, dk_ref, dv_ref):
        qv, kk, vv, dov = q_ref[...], k_ref[...], v_ref[...], do_ref[...]
        p = _attn_probs(qv, kk)
        dp = _nt(dov, vv)
        dv_ref[...] = _tn(p.astype(MXU_DTYPE), dov).astype(MXU_DTYPE)
        ds = (p * (dp - jnp.sum(dp * p, axis=-1, keepdims=True)) * (X_HEADDIM ** -0.5)).astype(MXU_DTYPE)
        dq_ref[...] = _nn(ds, kk).astype(MXU_DTYPE)
        dk_ref[...] = _tn(ds, qv).astype(MXU_DTYPE)

    blk_q = pl.BlockSpec((s, hd), lambda b, h: (b, h))
    blk_m = pl.BlockSpec((ml, hd), lambda b, h: (b, h))
    return pl.pallas_call(
        body, grid=(bsz, X_HEADS),
        in_specs=[blk_q, blk_m, pl.BlockSpec((ml, hd), lambda b, h: (b, X_HEADS + h)), blk_q],
        out_specs=[blk_q, blk_m, blk_m],
        out_shape=[jax.ShapeDtypeStruct((t, D_MODEL), MXU_DTYPE), jax.ShapeDtypeStruct((bsz * ml, D_MODEL), MXU_DTYPE),
                   jax.ShapeDtypeStruct((bsz * ml, D_MODEL), MXU_DTYPE)],
        compiler_params=_params("parallel", "parallel"), name=name)(q, kv, kv, do)


def _causal(n):
    row = lax.broadcasted_iota(jnp.int32, (n, n), 0)
    col = lax.broadcasted_iota(jnp.int32, (n, n), 1)
    return row >= col


def _sg_norm(v, g, b):
    gv = _gelu(v)
    mu = jnp.mean(gv, axis=-1, keepdims=True)
    xc = gv - mu
    var = jnp.mean(xc * xc, axis=-1, keepdims=True)
    rstd = lax.rsqrt(var + LN_EPS)
    xh = xc * rstd
    return xh, rstd, xh * g + b


def _sg_fwd(proj, ln_g, ln_b, w, bcol, *, name):
    t = proj.shape[0]
    c = D_MODEL
    gd = c // SG_GROUPS

    def body(u_ref, v_ref, g_ref, b_ref, w_ref, bc_ref, o_ref):
        gu = _gelu(u_ref[...].astype(F32))
        _, _, vn = _sg_norm(v_ref[...].astype(F32), g_ref[...], b_ref[...])
        mask = _causal(CHUNK)
        for g in range(SG_GROUPS):
            sl = slice(g * gd, (g + 1) * gd)
            wg = jnp.where(mask, w_ref[g], 0.0).astype(MXU_DTYPE)
            mixed = _nn(wg, vn[:, sl].astype(MXU_DTYPE)) + bc_ref[g]
            o_ref[:, sl] = (gu[:, sl] * mixed).astype(MXU_DTYPE)

    return pl.pallas_call(
        body, grid=(t // CHUNK,),
        in_specs=[_row_spec(CHUNK, c, 0), _row_spec(CHUNK, c, 1), _par_spec((1, c)), _par_spec((1, c)),
                  _par_spec((SG_GROUPS, CHUNK, CHUNK)), _par_spec((SG_GROUPS, CHUNK, 1))],
        out_specs=_row_spec(CHUNK, c), out_shape=jax.ShapeDtypeStruct((t, c), MXU_DTYPE),
        compiler_params=_params("parallel"), name=name)(proj, proj, ln_g.reshape(1, c), ln_b.reshape(1, c), w, bcol)


def _sg_bwd(proj, dsgo, ln_g, ln_b, w, bcol, dproj, *, name):
    t = proj.shape[0]
    c = D_MODEL
    gd = c // SG_GROUPS

    def body(u_ref, v_ref, d_ref, g_ref, b_ref, w_ref, bc_ref, _, duv_ref, dw_ref, dbc_ref, dg_ref, db_ref, dvn_ref):
        @pl.when(pl.program_id(0) == 0)
        def _():
            dw_ref[...] = jnp.zeros_like(dw_ref)
            dbc_ref[...] = jnp.zeros_like(dbc_ref)
            dg_ref[...] = jnp.zeros_like(dg_ref)
            db_ref[...] = jnp.zeros_like(db_ref)

        u = u_ref[...].astype(F32)
        v = v_ref[...].astype(F32)
        dso = d_ref[...]
        gu = _gelu(u)
        xh, rstd, vn = _sg_norm(v, g_ref[...], b_ref[...])
        mask = _causal(CHUNK)
        for g in range(SG_GROUPS):
            sl = slice(g * gd, (g + 1) * gd)
            wg = jnp.where(mask, w_ref[g], 0.0).astype(MXU_DTYPE)
            vng = vn[:, sl].astype(MXU_DTYPE)
            mixed = _nn(wg, vng) + bc_ref[g]
            duv_ref[:, sl] = (dso[:, sl] * mixed * _gelu_grad(u[:, sl])).astype(MXU_DTYPE)
            dmix = dso[:, sl] * gu[:, sl]
            dmb = dmix.astype(MXU_DTYPE)
            dbc_ref[g] += jnp.sum(dmix, axis=-1, keepdims=True)
            dw_ref[g] += jnp.where(mask, _nt(dmb, vng), 0.0)
            dvn_ref[:, sl] = _tn(wg, dmb)
        dvn = dvn_ref[...]
        dg_ref[...] += jnp.sum(dvn * xh, axis=0, keepdims=True)
        db_ref[...] += jnp.sum(dvn, axis=0, keepdims=True)
        dxh = dvn * g_ref[...]
        m1 = jnp.mean(dxh, axis=-1, keepdims=True)
        m2 = jnp.mean(dxh * xh, axis=-1, keepdims=True)
        dgv = rstd * (dxh - m1 - xh * m2)
        duv_ref[:, c:] = (dgv * _gelu_grad(v)).astype(MXU_DTYPE)

    return pl.pallas_call(
        body, grid=(t // CHUNK,),
        in_specs=[_row_spec(CHUNK, c, 0), _row_spec(CHUNK, c, 1), _row_spec(CHUNK, c), _par_spec((1, c)),
                  _par_spec((1, c)), _par_spec((SG_GROUPS, CHUNK, CHUNK)), _par_spec((SG_GROUPS, CHUNK, 1)), _ANY],
        out_specs=[_row_spec(CHUNK, 2 * c), _par_spec((SG_GROUPS, CHUNK, CHUNK)), _par_spec((SG_GROUPS, CHUNK, 1)),
                   _par_spec((1, c)), _par_spec((1, c))],
        out_shape=[jax.ShapeDtypeStruct(dproj.shape, dproj.dtype), jax.ShapeDtypeStruct((SG_GROUPS, CHUNK, CHUNK), F32),
                   jax.ShapeDtypeStruct((SG_GROUPS, CHUNK, 1), F32), jax.ShapeDtypeStruct((1, c), F32),
                   jax.ShapeDtypeStruct((1, c), F32)],
        scratch_shapes=[pltpu.VMEM((CHUNK, c), F32)], input_output_aliases={7: 0},
        compiler_params=_params("arbitrary"), name=name)(proj, proj, dsgo, ln_g.reshape(1, c), ln_b.reshape(1, c), w, bcol, dproj)


CONV_TC = 512


def _conv_taps(x):
    rows = lax.broadcasted_iota(jnp.int32, x.shape, 0)
    taps = [jnp.where(rows >= SSM_CONV - 1 - k, pltpu.roll(x, SSM_CONV - 1 - k, axis=0), 0.0) for k in range(SSM_CONV - 1)]
    return taps + [x]


def _conv_pre(taps, w_ref, b_ref):
    acc = b_ref[...]
    for k in range(SSM_CONV):
        acc = acc + taps[k] * w_ref[k:k + 1, :]
    return acc


def _conv_fwd(proj, w, b, *, bsz, name):
    t = proj.shape[0]
    s = t // bsz
    nj = SSM_CONV_DIM // CONV_TC
    c0 = XBC_COL0 // CONV_TC

    def body(x_ref, w_ref, b_ref, o_ref):
        pre = _conv_pre(_conv_taps(x_ref[...].astype(F32)), w_ref, b_ref)
        o_ref[...] = pre * _sigmoid(pre)

    return pl.pallas_call(
        body, grid=(bsz, nj),
        in_specs=[pl.BlockSpec((s, CONV_TC), lambda bb, j: (bb, c0 + j)), pl.BlockSpec((SSM_CONV, CONV_TC), lambda bb, j: (0, j)),
                  pl.BlockSpec((1, CONV_TC), lambda bb, j: (0, j))],
        out_specs=pl.BlockSpec((s, CONV_TC), lambda bb, j: (bb, j)),
        out_shape=jax.ShapeDtypeStruct((t, SSM_CONV_DIM), F32),
        compiler_params=_params("parallel", "parallel"), name=name)(proj, w, b.reshape(1, -1))


def _conv_bwd(proj, dact, w, b, dproj, *, bsz, name):
    t = proj.shape[0]
    s = t // bsz
    nj = SSM_CONV_DIM // CONV_TC
    c0 = XBC_COL0 // CONV_TC

    def body(x_ref, d_ref, w_ref, b_ref, _, dx_ref, dw_ref, db_ref):
        @pl.when(pl.program_id(1) == 0)
        def _():
            dw_ref[...] = jnp.zeros_like(dw_ref)
            db_ref[...] = jnp.zeros_like(db_ref)

        taps = _conv_taps(x_ref[...].astype(F32))
        pre = _conv_pre(taps, w_ref, b_ref)
        sg = _sigmoid(pre)
        dpre = d_ref[...] * (sg * (1.0 + pre * (1.0 - sg)))
        rows = lax.broadcasted_iota(jnp.int32, dpre.shape, 0)
        db_ref[...] += jnp.sum(dpre, axis=0, keepdims=True)
        dx = dpre * w_ref[SSM_CONV - 1:SSM_CONV, :]
        for k in range(SSM_CONV):
            dw_ref[k:k + 1, :] += jnp.sum(dpre * taps[k], axis=0, keepdims=True)
        for k in range(SSM_CONV - 1):
            sh = SSM_CONV - 1 - k
            dsh = jnp.where(rows < s - sh, pltpu.roll(dpre, s - sh, axis=0), 0.0)
            dx = dx + dsh * w_ref[k:k + 1, :]
        dx_ref[...] = dx.astype(MXU_DTYPE)

    return pl.pallas_call(
        body, grid=(nj, bsz),
        in_specs=[pl.BlockSpec((s, CONV_TC), lambda j, bb: (bb, c0 + j)), pl.BlockSpec((s, CONV_TC), lambda j, bb: (bb, j)),
                  pl.BlockSpec((SSM_CONV, CONV_TC), lambda j, bb: (0, j)), pl.BlockSpec((1, CONV_TC), lambda j, bb: (0, j)), _ANY],
        out_specs=[pl.BlockSpec((s, CONV_TC), lambda j, bb: (bb, c0 + j)), pl.BlockSpec((SSM_CONV, CONV_TC), lambda j, bb: (0, j)),
                   pl.BlockSpec((1, CONV_TC), lambda j, bb: (0, j))],
        out_shape=[jax.ShapeDtypeStruct(dproj.shape, dproj.dtype), jax.ShapeDtypeStruct((SSM_CONV, SSM_CONV_DIM), F32),
                   jax.ShapeDtypeStruct((1, SSM_CONV_DIM), F32)],
        input_output_aliases={4: 0},
        compiler_params=_params("parallel", "arbitrary"), name=name)(proj, dact, w, b.reshape(1, -1), dproj)


def _softplus(x):
    return jnp.maximum(x, 0.0) + jnp.log1p(jnp.exp(-jnp.abs(x)))


def _pad_heads(v):
    return jnp.broadcast_to(jnp.pad(v.astype(F32), (0, HEAD_PAD - SSM_HEADS))[None, :], (SUBLANE, HEAD_PAD))


def _ssd_prep(dt_raw, dt_bias8, a_log8, *, name):
    t = dt_raw.shape[0]
    n = CHUNK

    def body(r_ref, b_ref, al_ref, dt_ref, cs_ref, dtt_ref, cst_ref):
        dt = _softplus(r_ref[...] + b_ref[0:1, :])
        da = dt * (-jnp.exp(al_ref[0:1, :]))
        row = lax.broadcasted_iota(jnp.int32, (n, n), 0)
        col = lax.broadcasted_iota(jnp.int32, (n, n), 1)
        lower = (col <= row).astype(F32)
        upper = (row <= col).astype(F32)
        eye = (row == col).astype(F32)
        dt_ref[...] = dt
        cs_ref[...] = _dot_exact(lower, da, _DN_NN, 1)
        cst_ref[0] = _dot_exact(da, upper, _DN_TN, 0)
        dtt_ref[0] = _dot_exact(dt, eye, _DN_TN, 0)

    hp = HEAD_PAD
    return pl.pallas_call(
        body, grid=(t // n,),
        in_specs=[_row_spec(n, hp), _par_spec((SUBLANE, hp)), _par_spec((SUBLANE, hp))],
        out_specs=[_row_spec(n, hp), _row_spec(n, hp), pl.BlockSpec((1, hp, n), lambda i: (i, 0, 0)),
                   pl.BlockSpec((1, hp, n), lambda i: (i, 0, 0))],
        out_shape=[jax.ShapeDtypeStruct((t, hp), F32), jax.ShapeDtypeStruct((t, hp), F32),
                   jax.ShapeDtypeStruct((t // n, hp, n), F32), jax.ShapeDtypeStruct((t // n, hp, n), F32)],
        compiler_params=_params("parallel"), name=name)(dt_raw, dt_bias8, a_log8)


def _expand_mat():
    h = lax.broadcasted_iota(jnp.int32, (HEAD_PAD, SSM_INNER), 0)
    ch = lax.broadcasted_iota(jnp.int32, (HEAD_PAD, SSM_INNER), 1)
    return (ch // SSM_HEADDIM == h).astype(F32)


def _reduce_mat():
    ch = lax.broadcasted_iota(jnp.int32, (SSM_INNER, HEAD_PAD), 0)
    h = lax.broadcasted_iota(jnp.int32, (SSM_INNER, HEAD_PAD), 1)
    return (ch // SSM_HEADDIM == h).astype(F32)


def _expand(v, em):
    return _dot_exact(v, em, _DN_NN, 0)


def _expand_heads(v):
    return jnp.repeat(v.astype(F32), SSM_HEADDIM)[None, :]


def _decay_mat(cs_ref, cst_ref, h, mask):
    seg = cs_ref[:, h:h + 1] - cst_ref[0, h:h + 1, :]
    return jnp.where(mask, jnp.exp(jnp.minimum(seg, 0.0)), 0.0)


GROUP_CH = SSM_INNER // SSM_GROUPS
PAIRS_PER_GROUP = GROUP_CH // LANE
HEADS_PER_GROUP = SSM_HEADS // SSM_GROUPS
BM_COL0 = SSM_INNER
CM_COL0 = SSM_INNER + SSM_GROUPS * SSM_STATE


def _ssd_specs(nc, rev):
    def cidx(i):
        return (i // nc) * nc + (nc - 1 - i % nc) if rev else i

    n = CHUNK
    xs = pl.BlockSpec((n, SSM_INNER), lambda i: (cidx(i), 0))
    bm = pl.BlockSpec((n, GROUP_CH), lambda i: (cidx(i), BM_COL0 // GROUP_CH))
    cm = pl.BlockSpec((n, GROUP_CH), lambda i: (cidx(i), CM_COL0 // GROUP_CH))
    hv = pl.BlockSpec((n, HEAD_PAD), lambda i: (cidx(i), 0))
    hvt = pl.BlockSpec((1, HEAD_PAD, n), lambda i: (cidx(i), 0, 0))
    st = pl.BlockSpec((1, SSM_INNER, SSM_STATE), lambda i: (cidx(i), 0, 0))
    return xs, bm, cm, hv, hvt, st


def _ssd_fwd(xbc, dt, cs, dtt, cst, dskx, *, nc, name):
    t = xbc.shape[0]
    n = CHUNK
    xs_s, bm_s, cm_s, hv_s, hvt_s, st_s = _ssd_specs(nc, False)

    def body(xs_ref, bm_ref, cm_ref, dt_ref, cs_ref, dtt_ref, cst_ref, dsk_ref, y_ref, st_ref, prev):
        @pl.when(pl.program_id(0) % nc == 0)
        def _():
            prev[...] = jnp.zeros_like(prev)

        st_ref[0] = prev[...]
        em = _expand_mat()
        dtx = _expand(dt_ref[...], em)
        csx = _expand(cs_ref[...], em)
        dskx = dsk_ref[...]
        xs = xs_ref[...]
        xdt = xs * dtx
        ecs = jnp.exp(csx)
        dec = jnp.exp(csx[n - 1:n, :] - csx)
        mask = _causal(n)
        lane = lax.broadcasted_iota(jnp.int32, (n, LANE), 1)
        for g in range(SSM_GROUPS):
            gs = slice(g * SSM_STATE, (g + 1) * SSM_STATE)
            gc = slice(g * GROUP_CH, (g + 1) * GROUP_CH)
            cmat = cm_ref[:, gs].astype(MXU_DTYPE)
            bmat = bm_ref[:, gs].astype(MXU_DTYPE)
            cb = _nt(cmat, bmat)
            yoff = ecs[:, gc] * _nt(cmat, prev[gc, :].astype(MXU_DTYPE))
            for q in range(PAIRS_PER_GROUP):
                hp = g * PAIRS_PER_GROUP + q
                sl = slice(hp * LANE, (hp + 1) * LANE)
                xp = xdt[:, sl].astype(MXU_DTYPE)
                m0 = (cb * _decay_mat(cs_ref, cst_ref, 2 * hp, mask)).astype(MXU_DTYPE)
                m1 = (cb * _decay_mat(cs_ref, cst_ref, 2 * hp + 1, mask)).astype(MXU_DTYPE)
                yd = jnp.where(lane < SSM_HEADDIM, _nn(m0, xp), _nn(m1, xp))
                y_ref[:, sl] = yd + yoff[:, q * LANE:(q + 1) * LANE] + xs[:, sl] * dskx[:, sl]
            snew = _tn((xdt[:, gc] * dec[:, gc]).astype(MXU_DTYPE), bmat)
            for r in range(HEADS_PER_GROUP):
                h = g * HEADS_PER_GROUP + r
                rows = slice(h * SSM_HEADDIM, (h + 1) * SSM_HEADDIM)
                e = jnp.exp(cst_ref[0, h:h + 1, n - 1:n])
                prev[rows, :] = prev[rows, :] * e + snew[r * SSM_HEADDIM:(r + 1) * SSM_HEADDIM, :]

    return pl.pallas_call(
        body, grid=(t // n,),
        in_specs=[xs_s, bm_s, cm_s, hv_s, hv_s, hvt_s, hvt_s, _par_spec((1, SSM_INNER))],
        out_specs=[xs_s, st_s],
        out_shape=[jax.ShapeDtypeStruct((t, SSM_INNER), F32), jax.ShapeDtypeStruct((t // n, SSM_INNER, SSM_STATE), F32)],
        scratch_shapes=[pltpu.VMEM((SSM_INNER, SSM_STATE), F32)],
        compiler_params=_params("arbitrary"), name=name)(xbc, xbc, xbc, dt, cs, dtt, cst, dskx)


def _ssd_bwd(dy, xbc, dt, cs, dtt, cst, st, dskx, a_log8, dt_raw, dt_bias8, *, nc, name):
    t = xbc.shape[0]
    n = CHUNK
    xs_s, bm_s, cm_s, hv_s, hvt_s, st_s = _ssd_specs(nc, True)
    acc_s = _par_spec((1, HEAD_PAD))
    xbc_s = pl.BlockSpec((n, SSM_CONV_DIM), xs_s.index_map)

    def body(dy_ref, xs_ref, bm_ref, cm_ref, dt_ref, cs_ref, dtt_ref, cst_ref, st_ref, dsk_ref, al_ref, raw_ref, bias_ref,
             dxbc_ref, ddr_ref, dal_ref, dds_ref, dbias_ref, dprev, dxdt_s, tdec_s, tcs_s):
        @pl.when(pl.program_id(0) % nc == 0)
        def _():
            dprev[...] = jnp.zeros_like(dprev)

        @pl.when(pl.program_id(0) == 0)
        def _():
            dal_ref[...] = jnp.zeros_like(dal_ref)
            dds_ref[...] = jnp.zeros_like(dds_ref)
            dbias_ref[...] = jnp.zeros_like(dbias_ref)

        em = _expand_mat()
        rm = _reduce_mat()

        def head_reduce(v):
            return _dot_exact(v, rm, _DN_NN, 0)

        dtv = dt_ref[...]
        csv = cs_ref[...]
        dtx = _expand(dtv, em)
        csx = _expand(csv, em)
        dskx = dsk_ref[...]
        xs = xs_ref[...]
        dyv = dy_ref[...]
        xdt = xs * dtx
        ecs = jnp.exp(csx)
        dec = jnp.exp(csx[n - 1:n, :] - csx)
        mask = _causal(n)
        lane = lax.broadcasted_iota(jnp.int32, (n, LANE), 1)
        hlane = lax.broadcasted_iota(jnp.int32, (1, HEAD_PAD), 1)
        hsub = lax.broadcasted_iota(jnp.int32, (HEAD_PAD, 1), 0)
        rsum = jnp.zeros((n, HEAD_PAD), F32)
        csum = jnp.zeros((HEAD_PAD, n), F32)
        for g in range(SSM_GROUPS):
            gs = slice(g * SSM_STATE, (g + 1) * SSM_STATE)
            gc = slice(g * GROUP_CH, (g + 1) * GROUP_CH)
            cmat = cm_ref[:, gs].astype(MXU_DTYPE)
            bmat = bm_ref[:, gs].astype(MXU_DTYPE)
            cb = _nt(cmat, bmat)
            pg = st_ref[0, gc, :].astype(MXU_DTYPE)
            dpg = dprev[gc, :]
            dpgb = dpg.astype(MXU_DTYPE)
            z = _nt(cmat, pg)
            dyg = dyv[:, gc]
            dz = (dyg * ecs[:, gc]).astype(MXU_DTYPE)
            dc = _nn(dz, pg)
            dprev_y = _tn(dz, cmat)
            tcs_s[:, gc] = dyg * z * ecs[:, gc]
            xd = xdt[:, gc] * dec[:, gc]
            wmat = _nt(bmat, dpgb)
            db = _nn(xd.astype(MXU_DTYPE), dpgb)
            tdec_s[:, gc] = wmat * xd
            dxdt_g = wmat * dec[:, gc]
            dcb = jnp.zeros((n, n), F32)
            for q in range(PAIRS_PER_GROUP):
                hp = g * PAIRS_PER_GROUP + q
                sl = slice(hp * LANE, (hp + 1) * LANE)
                xp = xdt[:, sl].astype(MXU_DTYPE)
                dyp = dyv[:, sl]
                dypb = dyp.astype(MXU_DTYPE)
                dxp = None
                for hh in range(2):
                    h = 2 * hp + hh
                    lm = _decay_mat(cs_ref, cst_ref, h, mask)
                    mine = (lane < SSM_HEADDIM) if hh == 0 else (lane >= SSM_HEADDIM)
                    dm = _nt(jnp.where(mine, dyp, 0.0).astype(MXU_DTYPE), xp)
                    dml = dm * lm
                    dcb = dcb + dml
                    gseg = dml * cb
                    rsum = rsum + jnp.sum(gseg, axis=1, keepdims=True) * (hlane == h).astype(F32)
                    csum = csum + (hsub == h).astype(F32) * jnp.sum(gseg, axis=0, keepdims=True)
                    dxh = _tn((cb * lm).astype(MXU_DTYPE), dypb)
                    dxp = dxh if dxp is None else jnp.where(mine, dxh, dxp)
                dxdt_s[:, sl] = dxdt_g[:, q * LANE:(q + 1) * LANE] + dxp
            dcbb = dcb.astype(MXU_DTYPE)
            dxbc_ref[:, CM_COL0 + g * SSM_STATE:CM_COL0 + (g + 1) * SSM_STATE] = dc + _nn(dcbb, bmat)
            dxbc_ref[:, BM_COL0 + g * SSM_STATE:BM_COL0 + (g + 1) * SSM_STATE] = db + _tn(dcbb, cmat)
            for r in range(HEADS_PER_GROUP):
                h = g * HEADS_PER_GROUP + r
                rows = slice(h * SSM_HEADDIM, (h + 1) * SSM_HEADDIM)
                lr = slice(r * SSM_HEADDIM, (r + 1) * SSM_HEADDIM)
                e = jnp.exp(cst_ref[0, h:h + 1, n - 1:n])
                dprev[rows, :] = dpg[lr, :] * e + dprev_y[lr, :]
            tq = _dot_exact(dpg * st_ref[0, gc, :], rm[gc, :], _DN_TN, 0)
            if g == 0:
                qsum = jnp.sum(tq, axis=0, keepdims=True)
            else:
                qsum = qsum + jnp.sum(tq, axis=0, keepdims=True)
        dxdt = dxdt_s[...]
        dxbc_ref[:, 0:SSM_INNER] = dxdt * dtx + dyv * dskx
        ddt = head_reduce(dxdt * xs)
        edec = head_reduce(tdec_s[...])
        ycs = head_reduce(tcs_s[...])
        row = lax.broadcasted_iota(jnp.int32, (n, HEAD_PAD), 0)
        extra = jnp.sum(edec, axis=0, keepdims=True) + qsum * jnp.exp(csv[n - 1:n, :])
        dcs = rsum - csum.T + ycs - edec + jnp.where(row == n - 1, extra, 0.0)
        r2 = lax.broadcasted_iota(jnp.int32, (n, n), 0)
        c2 = lax.broadcasted_iota(jnp.int32, (n, n), 1)
        dda = _dot_exact((c2 >= r2).astype(F32), dcs, _DN_NN, 1)
        a_row = -jnp.exp(al_ref[0:1, :])
        ddt = ddt + dda * a_row
        dal_ref[...] += jnp.sum(dda * dtv, axis=0, keepdims=True) * a_row
        dds_ref[...] += jnp.sum(head_reduce(dyv * xs), axis=0, keepdims=True)
        ddr = ddt * _sigmoid(raw_ref[...] + bias_ref[0:1, :])
        ddr_ref[...] = ddr
        dbias_ref[...] += jnp.sum(ddr, axis=0, keepdims=True)

    par8 = _par_spec((SUBLANE, HEAD_PAD))
    return pl.pallas_call(
        body, grid=(t // n,),
        in_specs=[xs_s, xs_s, bm_s, cm_s, hv_s, hv_s, hvt_s, hvt_s, st_s, _par_spec((1, SSM_INNER)), par8, hv_s, par8],
        out_specs=[xbc_s, hv_s, acc_s, acc_s, acc_s],
        out_shape=[jax.ShapeDtypeStruct((t, SSM_CONV_DIM), F32), jax.ShapeDtypeStruct((t, HEAD_PAD), F32),
                   jax.ShapeDtypeStruct((1, HEAD_PAD), F32), jax.ShapeDtypeStruct((1, HEAD_PAD), F32),
                   jax.ShapeDtypeStruct((1, HEAD_PAD), F32)],
        scratch_shapes=[pltpu.VMEM((SSM_INNER, SSM_STATE), F32), pltpu.VMEM((n, SSM_INNER), F32),
                        pltpu.VMEM((n, SSM_INNER), F32), pltpu.VMEM((n, SSM_INNER), F32)],
        compiler_params=_params("arbitrary"), name=name)(dy, xbc, xbc, xbc, dt, cs, dtt, cst, st, dskx, a_log8, dt_raw, dt_bias8)


def _gate_norm_fwd(y, proj, norm_g, *, name):
    t, c = y.shape
    tm = _pick(t, (256, 128))

    def body(y_ref, z_ref, g_ref, o_ref):
        z = z_ref[...].astype(F32)
        yz = y_ref[...] * z * _sigmoid(z)
        for g in range(SSM_GROUPS):
            gc = slice(g * GROUP_CH, (g + 1) * GROUP_CH)
            seg = yz[:, gc]
            r = lax.rsqrt(jnp.mean(seg * seg, axis=-1, keepdims=True) + RMS_EPS)
            o_ref[:, gc] = (seg * r * g_ref[:, gc]).astype(MXU_DTYPE)

    return pl.pallas_call(
        body, grid=(t // tm,), in_specs=[_row_spec(tm, c), _row_spec(tm, c, 1), _par_spec((1, c))],
        out_specs=_row_spec(tm, c), out_shape=jax.ShapeDtypeStruct((t, c), MXU_DTYPE),
        compiler_params=_params("parallel"), name=name)(y, proj, norm_g.reshape(1, c))


def _gate_norm_bwd(dyb, y, proj, norm_g, dproj, *, name):
    t, c = y.shape
    tm = _pick(t, (256, 128))

    def body(d_ref, y_ref, z_ref, g_ref, _, dy_ref, dz_ref, dg_ref):
        @pl.when(pl.program_id(0) == 0)
        def _():
            dg_ref[...] = jnp.zeros_like(dg_ref)

        z = z_ref[...].astype(F32)
        yv = y_ref[...]
        sz = _sigmoid(z)
        silu = z * sz
        yz = yv * silu
        dv = d_ref[...]
        for g in range(SSM_GROUPS):
            gc = slice(g * GROUP_CH, (g + 1) * GROUP_CH)
            seg = yz[:, gc]
            r = lax.rsqrt(jnp.mean(seg * seg, axis=-1, keepdims=True) + RMS_EPS)
            nrm = seg * r
            dn = dv[:, gc] * g_ref[:, gc]
            dg_ref[:, gc] += jnp.sum(dv[:, gc] * nrm, axis=0, keepdims=True)
            dyz = r * (dn - nrm * jnp.mean(dn * nrm, axis=-1, keepdims=True))
            dy_ref[:, gc] = dyz * silu[:, gc]
            dz_ref[:, gc] = (dyz * yv[:, gc] * (sz[:, gc] * (1.0 + z[:, gc] * (1.0 - sz[:, gc])))).astype(MXU_DTYPE)

    return pl.pallas_call(
        body, grid=(t // tm,), in_specs=[_row_spec(tm, c), _row_spec(tm, c), _row_spec(tm, c, 1), _par_spec((1, c)), _ANY],
        out_specs=[_row_spec(tm, c), _row_spec(tm, c, 1), _par_spec((1, c))],
        out_shape=[jax.ShapeDtypeStruct((t, c), F32), jax.ShapeDtypeStruct(dproj.shape, dproj.dtype),
                   jax.ShapeDtypeStruct((1, c), F32)],
        input_output_aliases={4: 1},
        compiler_params=_params("arbitrary"), name=name)(dyb, y, proj, norm_g.reshape(1, c), dproj)


GA_COLBLK = GAB_COL0 // D_MODEL


def _merge_fwd(br_a, br_b, proj, *, name):
    t, c = br_a.shape
    tm = _pick(t, ROW_TILES)

    def body(a_ref, b_ref, ga_ref, gb_ref, o_ref):
        o_ref[...] = (_sigmoid(ga_ref[...].astype(F32)) * a_ref[...] + _sigmoid(gb_ref[...].astype(F32)) * b_ref[...]).astype(MXU_DTYPE)

    return pl.pallas_call(
        body, grid=(t // tm,),
        in_specs=[_row_spec(tm, c), _row_spec(tm, c), _row_spec(tm, c, GA_COLBLK), _row_spec(tm, c, GA_COLBLK + 1)],
        out_specs=_row_spec(tm, c), out_shape=jax.ShapeDtypeStruct((t, c), MXU_DTYPE),
        compiler_params=_params("parallel"), name=name)(br_a, br_b, proj, proj)


def _merge_bwd(dm, br_a, br_b, proj, *, name):
    t, c = br_a.shape
    tm = _pick(t, ROW_TILES)

    def body(dm_ref, a_ref, b_ref, ga_ref, gb_ref, da_ref, db_ref, dg_ref):
        d = dm_ref[...]
        sa = _sigmoid(ga_ref[...].astype(F32))
        sb = _sigmoid(gb_ref[...].astype(F32))
        da_ref[...] = (d * sa).astype(MXU_DTYPE)
        db_ref[...] = (d * sb).astype(MXU_DTYPE)
        dg_ref[:, :c] = (d * a_ref[...] * sa * (1.0 - sa)).astype(MXU_DTYPE)
        dg_ref[:, c:] = (d * b_ref[...] * sb * (1.0 - sb)).astype(MXU_DTYPE)

    return pl.pallas_call(
        body, grid=(t // tm,),
        in_specs=[_row_spec(tm, c), _row_spec(tm, c), _row_spec(tm, c), _row_spec(tm, c, GA_COLBLK), _row_spec(tm, c, GA_COLBLK + 1)],
        out_specs=[_row_spec(tm, c), _row_spec(tm, c), _row_spec(tm, 2 * c, GAB_COL0 // (2 * c))],
        out_shape=[jax.ShapeDtypeStruct((t, c), MXU_DTYPE), jax.ShapeDtypeStruct((t, c), MXU_DTYPE),
                   jax.ShapeDtypeStruct((t, MAIN_COLS), MXU_DTYPE)],
        compiler_params=_params("parallel"), name=name)(dm, br_a, br_b, proj, proj)


def _layer_fwd(x, xb, memn_b, w, *, bsz, tag):
    nc = x.shape[0] // bsz // CHUNK
    sv = {"x_in": xb}
    proj = _mm(xb, w["w_main"], out_dtype=STASH_DTYPE, name=f"{tag}_proj")
    dt_raw = _mm(xb, w["w_dt"], name=f"{tag}_dtproj")
    sgo = _sg_fwd(proj, w["sg_ln_g"], w["sg_ln_b"], w["sg_w"], w["sg_bcol"], name=f"{tag}_sg_fwd")
    xbc = _conv_fwd(proj, w["conv_w"], w["conv_b"], bsz=bsz, name=f"{tag}_conv_fwd")
    dt, cs, dtt, cst = _ssd_prep(dt_raw, w["dt_bias8"], w["a_log8"], name=f"{tag}_ssd_prep")
    y, st = _ssd_fwd(xbc, dt, cs, dtt, cst, w["d_skipx"], nc=nc, name=f"{tag}_ssd_fwd")
    yb = _gate_norm_fwd(y, proj, w["ssm_norm_g"], name=f"{tag}_gate_norm_fwd")
    if "rest" in w:
        w = w["rest"](w, yb)
    br_a = _mm(sgo, w["p_a"], name=f"{tag}_br_a")
    br_b = _mm(yb, w["p_b"], name=f"{tag}_br_b")
    merged = _merge_fwd(br_a, br_b, proj, name=f"{tag}_merge_fwd")
    mix = _mm(merged, w["w_mix_o"], name=f"{tag}_mix_o")
    x1, x1b, xh1, rs1 = _ln_fwd(x, mix, w["ln_g"][0], w["ln_b"][0], name=f"{tag}_ln1_fwd")
    sv.update(proj=proj, dt_raw=dt_raw, sgo=sgo, xbc=xbc, dt=dt, cs=cs, dtt=dtt, cst=cst, y=y, st=st, yb=yb,
              br_a=br_a, br_b=br_b, merged=merged, xh1=xh1, rs1=rs1, x1b=x1b)
    q = _mm(x1b, w["w_xq"], out_dtype=MXU_DTYPE, name=f"{tag}_q")
    kv = _mm(memn_b, w["w_xkv"], out_dtype=MXU_DTYPE, name=f"{tag}_kv")
    o = _attn_fwd(q, kv, bsz=bsz, name=f"{tag}_attn_fwd")
    att = _mm(o, w["w_xo"], name=f"{tag}_xo")
    x2, x2b, xh2, rs2 = _ln_fwd(x1, att, w["ln_g"][1], w["ln_b"][1], name=f"{tag}_ln2_fwd")
    sv.update(q=q, kv=kv, o=o, xh2=xh2, rs2=rs2, x2b=x2b)
    h = _mm(x2b, w["w_ffn_in"], out_dtype=STASH_DTYPE, name=f"{tag}_ffn_in")
    a = _swiglu_fwd(h, name=f"{tag}_swiglu_fwd")
    ffn = _mm(a, w["w_ffn_out"], name=f"{tag}_ffn_out")
    x3, x3b, xh3, rs3 = _ln_fwd(x2, ffn, w["ln_g"][2], w["ln_b"][2], name=f"{tag}_ln3_fwd")
    sv.update(h=h, a=a, xh3=xh3, rs3=rs3)
    return x3, x3b, sv, w


GRAD_GROUPS = (("w_ffn_out", "w_ffn_in", "w_xo", "w_xq", "w_xkv"), ("w_mix_o", "p_a", "p_b"), ("w_in",))


def _layer_bwd(dx3_addends, dx3_scales, memn_b, w, sv, on_group=None, *, bsz, tag):
    nc = sv["xh1"].shape[0] // bsz // CHUNK
    gr = {}

    def group_done(k):
        return on_group(GRAD_GROUPS[k], gr) if on_group is not None else None
    dp3, dp3b, dg3, db3 = _ln_bwd(dx3_addends, dx3_scales, sv["xh3"], sv["rs3"], w["ln_g"][2], name=f"{tag}_ln3_bwd")
    da = _mm(dp3b, w["w_ffn_out"], tb=True, name=f"{tag}_d_a")
    gr["w_ffn_out"] = _mm(sv["a"], dp3b, ta=True, name=f"{tag}_dw_ffn_out")
    dh = _swiglu_bwd(sv["h"], da, name=f"{tag}_swiglu_bwd")
    gr["w_ffn_in"] = _mm(sv["x2b"], dh, ta=True, name=f"{tag}_dw_ffn_in")
    dx2_br = _mm(dh, w["w_ffn_in"], tb=True, name=f"{tag}_dx2")
    dp2, dp2b, dg2, db2 = _ln_bwd([dp3, dx2_br], [ALPHA, 1.0], sv["xh2"], sv["rs2"], w["ln_g"][1], name=f"{tag}_ln2_bwd")
    do = _mm(dp2b, w["w_xo"], tb=True, out_dtype=MXU_DTYPE, name=f"{tag}_d_o")
    gr["w_xo"] = _mm(sv["o"], dp2b, ta=True, name=f"{tag}_dw_xo")
    dq, dk, dv = _attn_bwd(sv["q"], sv["kv"], do, bsz=bsz, name=f"{tag}_attn_bwd")
    dkv = jnp.concatenate([dk, dv], axis=1)
    gr["w_xq"] = _mm(sv["x1b"], dq, ta=True, name=f"{tag}_dw_xq")
    gr["w_xkv"] = _mm(memn_b, dkv, ta=True, name=f"{tag}_dw_xkv")
    dmemn = _mm(dkv, w["w_xkv"], tb=True, name=f"{tag}_d_memn")
    dx1_br = _mm(dq, w["w_xq"], tb=True, name=f"{tag}_dx1")
    token = group_done(0)
    ln_g1 = w["ln_g"][0] if token is None else w["ln_g"][0] + token[0, 0]
    dp1, dp1b, dg1, db1 = _ln_bwd([dp2, dx1_br], [ALPHA, 1.0], sv["xh1"], sv["rs1"], ln_g1, name=f"{tag}_ln1_bwd")
    gr["ln_g"] = jnp.concatenate([dg1, dg2, dg3], axis=0)
    gr["ln_b"] = jnp.concatenate([db1, db2, db3], axis=0)
    dmerged = _mm(dp1b, w["w_mix_o"], tb=True, name=f"{tag}_d_merged")
    gr["w_mix_o"] = _mm(sv["merged"], dp1b, ta=True, name=f"{tag}_dw_mix_o")
    dbr_a, dbr_b, dproj = _merge_bwd(dmerged, sv["br_a"], sv["br_b"], sv["proj"], name=f"{tag}_merge_bwd")
    gr["p_a"] = _mm(sv["sgo"], dbr_a, ta=True, name=f"{tag}_dw_p_a")
    gr["p_b"] = _mm(sv["yb"], dbr_b, ta=True, name=f"{tag}_dw_p_b")
    dsgo = _mm(dbr_a, w["p_a"], tb=True, name=f"{tag}_d_sgo")
    dyb = _mm(dbr_b, w["p_b"], tb=True, name=f"{tag}_d_yb")
    token = group_done(1)
    norm_g = w["ssm_norm_g"] if token is None else w["ssm_norm_g"] + token[0, 0]
    dy, dproj, gr["ssm_norm_g"] = _gate_norm_bwd(dyb, sv["y"], sv["proj"], norm_g, dproj, name=f"{tag}_gate_norm_bwd")
    dxbc, ddr, gr["a_log"], gr["d_skip"], gr["dt_bias"] = _ssd_bwd(
        dy, sv["xbc"], sv["dt"], sv["cs"], sv["dtt"], sv["cst"], sv["st"], w["d_skipx"], w["a_log8"], sv["dt_raw"],
        w["dt_bias8"], nc=nc, name=f"{tag}_ssd_bwd")
    dproj, gr["conv_w"], gr["conv_b"] = _conv_bwd(sv["proj"], dxbc, w["conv_w"], w["conv_b"], dproj, bsz=bsz, name=f"{tag}_conv_bwd")
    dproj, gr["sg_w"], dsg_bcol, gr["sg_ln_g"], gr["sg_ln_b"] = _sg_bwd(
        sv["proj"], dsgo, w["sg_ln_g"], w["sg_ln_b"], w["sg_w"], w["sg_bcol"], dproj, name=f"{tag}_sg_bwd")
    gr["sg_b"] = dsg_bcol[..., 0]
    gr["w_main"] = _mm(sv["x_in"], dproj, ta=True, name=f"{tag}_dw_main")
    gr["w_dt"] = _mm(sv["x_in"], ddr, ta=True, name=f"{tag}_dw_dt")
    token = group_done(2)
    dx_dt = _mm(ddr, w["w_dt"], tb=True, after=token, name=f"{tag}_dx_dt")
    dx_main = _mm(dproj, w["w_main"], tb=True, after=token, name=f"{tag}_dx_main")
    return [dp1, dx_main, dx_dt], [ALPHA, 1.0, 1.0], gr, dmemn


def _local_step(x, mem, tgt, mem_ln_g, mem_ln_b, layers, on_layer_grads=None):
    bsz, s, d = x.shape
    xf = x.reshape(bsz * s, d)
    memf = mem.reshape(-1, d)
    _, memn_b, mxh, mrs = _ln_fwd(memf, None, mem_ln_g, mem_ln_b, name="mem_ln_fwd")
    cur, curb, saved, weights = xf, xf, [], []
    for li, get_weights in enumerate(layers):
        cur, curb, sv, w = _layer_fwd(cur, curb, memn_b, get_weights(cur), bsz=bsz, tag=f"l{li}")
        saved.append(sv)
        weights.append(w)
    dy, lsum = _loss_head(cur, tgt.reshape(bsz * s, d), name="loss_head")
    addends, scales = [dy], [1.0]
    grads, dmem = [None] * len(layers), []
    for li in reversed(range(len(layers))):
        on_group = None if on_layer_grads is None else functools.partial(on_layer_grads, li)
        addends, scales, grads[li], dm = _layer_bwd(addends, scales, memn_b, weights[li], saved[li], on_group, bsz=bsz, tag=f"l{li}")
        dmem.append(dm)
    grad_x = _add_scaled(addends, scales, name="grad_x").reshape(bsz, s, d)
    _, _, dmg, dmb = _ln_bwd(dmem, [1.0] * len(dmem), mxh, mrs, mem_ln_g, name="mem_ln_bwd")
    return lsum, grad_x, grads, dmg[0], dmb[0]


_ANY = pl.BlockSpec(memory_space=pl.ANY)
_MESH = pl.DeviceIdType.MESH


def _all_gather8(x, *, name):
    def body(x_ref, out_ref, send_sems, recv_sems):
        mx, my, mc = lax.axis_index("x"), lax.axis_index("y"), lax.axis_index("c")
        me, sibling = (mx, my, mc), (mx, my, 1 - mc)
        chips = [(1 - mx, my), (mx, 1 - my), (1 - mx, 1 - my)]

        def blk(px, py, pc):
            return out_ref.at[4 * px + 2 * py + pc]

        def copy(k, block, to, src=None):
            return pltpu.make_async_remote_copy(
                src_ref=blk(*block) if src is None else src, dst_ref=blk(*block), send_sem=send_sems.at[k],
                recv_sem=recv_sems.at[k], device_id=to, device_id_type=_MESH)

        first = [copy(0, me, sibling, src=x_ref)]
        first += [copy(1 + j, me, (*chip, mc), src=x_ref) for j, chip in enumerate(chips)]
        for cp in first:
            cp.start()
        passed = [copy(4 + j, (*chip, mc), sibling) for j, chip in enumerate(chips)]
        for j, chip in enumerate(chips):
            copy(1 + j, (*chip, mc), me).wait_recv()
            passed[j].start()
        copy(0, sibling, me).wait_recv()
        for j, chip in enumerate(chips):
            copy(4 + j, (*chip, 1 - mc), me).wait_recv()
        for cp in first + passed:
            cp.wait_send()

    return pl.pallas_call(
        body, out_shape=jax.ShapeDtypeStruct((N_DEV,) + x.shape, x.dtype), in_specs=[_ANY], out_specs=_ANY,
        scratch_shapes=[pltpu.SemaphoreType.DMA((7,)), pltpu.SemaphoreType.DMA((7,))], name=name)(x)


def _row_tile(rows, row_bytes, mult=SUBLANE):
    best = None
    for tr in range(mult, rows + 1, mult):
        if rows % tr == 0 and (best is None or tr * row_bytes <= BLOCK_BYTES):
            best = tr
    return rows if best is None else best


def _gather_shape(r, c, kind):
    return {"row": (2, N_CHIPS * r, c), "col": (2, r, N_CHIPS * c), "chip": (2, N_CHIPS, r, c)}[kind]


def _cast_place(shard, kind, dtype, chip_idx, *, name):
    _, r, c = shard.shape
    tr = _row_tile(r, c * 4, 16)
    nt = r // tr

    def body(_, s_ref, o_ref):
        o_ref[...] = s_ref[...].astype(dtype)

    if kind == "row":
        out_spec = pl.BlockSpec((None, tr, c), lambda l, i, j_ref: (l, j_ref[0] * nt + i, 0))
    elif kind == "col":
        out_spec = pl.BlockSpec((None, tr, c), lambda l, i, j_ref: (l, i, j_ref[0]))
    else:
        out_spec = pl.BlockSpec((None, None, tr, c), lambda l, i, j_ref: (l, j_ref[0], i, 0))
    grid_spec = pltpu.PrefetchScalarGridSpec(
        num_scalar_prefetch=1, grid=(2, nt), in_specs=[pl.BlockSpec((None, tr, c), lambda l, i, j_ref: (l, i, 0))],
        out_specs=out_spec)
    return pl.pallas_call(body, grid_spec=grid_spec, out_shape=jax.ShapeDtypeStruct(_gather_shape(r, c, kind), dtype),
                          compiler_params=_params("parallel", "parallel"), name=name)(chip_idx, shard)


def _gather_params(bufs, shard_shapes, kinds, *, name):
    n = len(bufs)

    def body(*refs):
        outs = refs[n:2 * n]
        send_sems, recv_sems = refs[2 * n:]
        mx, my, mc = lax.axis_index("x"), lax.axis_index("y"), lax.axis_index("c")
        me, sibling = (mx, my, mc), (mx, my, 1 - mc)
        chips = [(1 - mx, my), (mx, 1 - my), (1 - mx, 1 - my)]

        def blk(i, px, py, pc):
            r, c = shard_shapes[i]
            j = 2 * px + py
            if kinds[i] == "row":
                return outs[i].at[pc, pl.ds(pl.multiple_of(j * r, r), r)]
            if kinds[i] == "col":
                return outs[i].at[pc, :, pl.ds(pl.multiple_of(j * c, c), c)]
            return outs[i].at[pc, j]

        def copy(i, k, block, to):
            return pltpu.make_async_remote_copy(
                src_ref=blk(i, *block), dst_ref=blk(i, *block), send_sem=send_sems.at[6 * i + k],
                recv_sem=recv_sems.at[6 * i + k], device_id=to, device_id_type=_MESH)

        sent = []
        for i in range(n):
            for j, chip in enumerate(chips):
                cp = copy(i, j, me, (*chip, mc))
                cp.start()
                sent.append(cp)
        for j, chip in enumerate(chips):
            for i in range(n):
                copy(i, j, (*chip, mc), me).wait_recv()
                fwd = copy(i, 3 + j, (*chip, mc), sibling)
                fwd.start()
                sent.append(fwd)
        for i in range(n):
            for j, chip in enumerate(chips):
                copy(i, 3 + j, (*chip, 1 - mc), me).wait_recv()
        for cp in sent:
            cp.wait_send()

    return pl.pallas_call(
        body, out_shape=[jax.ShapeDtypeStruct(b.shape, b.dtype) for b in bufs], in_specs=[_ANY] * n, out_specs=[_ANY] * n,
        input_output_aliases={i: i for i in range(n)},
        scratch_shapes=[pltpu.SemaphoreType.DMA((6 * n,)), pltpu.SemaphoreType.DMA((6 * n,))], name=name)(*bufs)


def _half(r, h):
    return pl.ds(pl.multiple_of(h * (r // 2), r // 2), r // 2)


_HBM = pl.BlockSpec(memory_space=pltpu.HBM)
_SEM = pl.BlockSpec(memory_space=pltpu.SEMAPHORE)
_EFFECT = pltpu.SideEffectType.DATAFLOW_SIDE_EFFECTING


def _sibling_copies(g_refs, land_refs, gs, views, send_sems, recv_sems):
    mx, my, mc = lax.axis_index("x"), lax.axis_index("y"), lax.axis_index("c")
    copies = []
    for i in range(len(gs)):
        if views[i] == "chip":
            src = g_refs[i].at[:, _half(gs[i].shape[1], 1 - mc)]
        else:
            src = g_refs[i].at[_half(gs[i].shape[0], 1 - mc)]
        copies.append(pltpu.make_async_remote_copy(src_ref=src, dst_ref=land_refs[i], send_sem=send_sems.at[i], recv_sem=recv_sems.at[i],
                                                   device_id=(mx, my, 1 - mc), device_id_type=_MESH))
    return copies


def _half_shape(g, view):
    return (g.shape[0], g.shape[1] // 2, g.shape[2]) if view == "chip" else (g.shape[0] // 2, g.shape[1])


def _grads_to_sibling_start(gs, views, *, name):
    n = len(gs)
    lands = [pltpu.with_memory_space_constraint(lax.empty(_half_shape(g, v), g.dtype), pltpu.HBM) for g, v in zip(gs, views)]

    def body(*refs):
        for cp in _sibling_copies(refs[:n], refs[n:2 * n], gs, views, refs[2 * n], refs[2 * n + 1]):
            cp.start()
        refs[-1][...] = jnp.zeros_like(refs[-1])

    outs = pl.pallas_call(
        body, name=name,
        out_shape=(pltpu.SemaphoreType.DMA((n,)), pltpu.SemaphoreType.DMA((n,)),
                   *[pltpu.HBM(x.shape, x.dtype) for x in list(gs) + lands], jax.ShapeDtypeStruct((SUBLANE, LANE), F32)),
        in_specs=[_HBM] * (2 * n), out_specs=(_SEM, _SEM, *[_HBM] * (2 * n), pl.BlockSpec(memory_space=pltpu.VMEM)),
        input_output_aliases={i: 2 + i for i in range(2 * n)},
        compiler_params=pltpu.CompilerParams(has_side_effects=_EFFECT),
    )(*[pltpu.with_memory_space_constraint(g, pltpu.HBM) for g in gs], *lands)
    return outs[0], outs[1], list(outs[2:2 + n]), list(outs[2 + n:2 + 2 * n]), outs[-1]


def _grads_to_sibling_wait(send_sems, recv_sems, gs, lands, views, after, *, name):
    n = len(gs)

    def body(*refs):
        for cp in _sibling_copies(refs[:n], refs[n:2 * n], gs, views, refs[2 * n], refs[2 * n + 1]):
            cp.wait_send()
            cp.wait_recv()

    outs = pl.pallas_call(
        body, name=name, out_shape=tuple(pltpu.HBM(x.shape, x.dtype) for x in list(gs) + list(lands)),
        in_specs=[_HBM] * (2 * n) + [_SEM, _SEM, _ANY], out_specs=tuple([_HBM] * (2 * n)),
        input_output_aliases={i: i for i in range(2 * n)},
        compiler_params=pltpu.CompilerParams(has_side_effects=_EFFECT),
    )(*gs, *lands, send_sems, recv_sems, after)
    return list(outs[:n]), list(outs[n:])


def _cast_place_layer(shard, l, kind, chip_idx, after, *, name):
    _, r, c = shard.shape
    tr = _row_tile(r, c * 4, 16)
    nt = r // tr

    def body(_, s_ref, *rest):
        rest[-1][...] = s_ref[...].astype(MXU_DTYPE)

    if kind == "row":
        out_spec = pl.BlockSpec((tr, c), lambda i, j_ref: (j_ref[0] * nt + i, 0))
    elif kind == "col":
        out_spec = pl.BlockSpec((tr, c), lambda i, j_ref: (i, j_ref[0]))
    else:
        out_spec = pl.BlockSpec((None, tr, c), lambda i, j_ref: (j_ref[0], i, 0))
    extra = [] if after is None else [after]
    grid_spec = pltpu.PrefetchScalarGridSpec(
        num_scalar_prefetch=1, grid=(nt,), in_specs=[pl.BlockSpec((None, tr, c), lambda i, j_ref: (l, i, 0))] + [_ANY] * len(extra),
        out_specs=out_spec)
    return pl.pallas_call(body, grid_spec=grid_spec, out_shape=jax.ShapeDtypeStruct(_gather_shape(r, c, kind)[1:], MXU_DTYPE),
                          compiler_params=_params("parallel"), name=name)(chip_idx, shard, *extra)


def _half_block(ref, kind, r, c, j, h):
    rows = _half(r, h)
    if kind == "row":
        return ref.at[pl.ds(pl.multiple_of(j * r + h * (r // 2), r // 2), r // 2)]
    if kind == "col":
        return ref.at[rows, pl.ds(pl.multiple_of(j * c, c), c)]
    return ref.at[j, rows]


def _gather_ici_copies(buf_refs, shapes, kinds, send_sems, recv_sems):
    mx, my, mc = lax.axis_index("x"), lax.axis_index("y"), lax.axis_index("c")
    chips = [(1 - mx, my), (mx, 1 - my), (1 - mx, 1 - my)]
    copies = []
    for i, (r, c) in enumerate(shapes):
        mine = _half_block(buf_refs[i], kinds[i], r, c, 2 * mx + my, mc)
        for k, (px, py) in enumerate(chips):
            copies.append(pltpu.make_async_remote_copy(
                src_ref=mine, dst_ref=mine, send_sem=send_sems.at[3 * i + k], recv_sem=recv_sems.at[3 * i + k],
                device_id=(px, py, mc), device_id_type=_MESH))
    return copies


def _gather_start(bufs, shapes, kinds, *, name):
    n = len(bufs)

    def body(*refs):
        send_sems, recv_sems, token = refs[n], refs[n + 1], refs[-1]
        for cp in _gather_ici_copies(refs[:n], shapes, kinds, send_sems, recv_sems):
            cp.start()
        token[...] = jnp.zeros_like(token)

    outs = pl.pallas_call(
        body, name=name,
        out_shape=(pltpu.SemaphoreType.DMA((3 * n,)), pltpu.SemaphoreType.DMA((3 * n,)),
                   *[pltpu.HBM(b.shape, b.dtype) for b in bufs], jax.ShapeDtypeStruct((SUBLANE, LANE), F32)),
        in_specs=[_HBM] * n, out_specs=(_SEM, _SEM, *[_HBM] * n, pl.BlockSpec(memory_space=pltpu.VMEM)),
        input_output_aliases={i: 2 + i for i in range(n)},
        compiler_params=pltpu.CompilerParams(has_side_effects=_EFFECT),
    )(*[pltpu.with_memory_space_constraint(b, pltpu.HBM) for b in bufs])
    return outs[0], outs[1], list(outs[2:2 + n]), outs[-1]


def _gather_wait(send_sems, recv_sems, bufs, shapes, kinds, after, *, name):
    n = len(bufs)

    def body(*refs):
        for cp in _gather_ici_copies(refs[:n], shapes, kinds, refs[n], refs[n + 1]):
            cp.wait_send()
            cp.wait_recv()

    outs = pl.pallas_call(
        body, name=name, out_shape=tuple(pltpu.HBM(b.shape, b.dtype) for b in bufs),
        in_specs=[_HBM] * n + [_SEM, _SEM, _ANY], out_specs=tuple([_HBM] * n), input_output_aliases={i: i for i in range(n)},
        compiler_params=pltpu.CompilerParams(has_side_effects=_EFFECT),
    )(*bufs, send_sems, recv_sems, after)
    return list(outs)


def _gather_forward(bufs, shapes, kinds, *, name):
    n = len(bufs)

    def body(*refs):
        outs = refs[n:2 * n]
        send_sems, recv_sems = refs[2 * n:]
        mx, my, mc = lax.axis_index("x"), lax.axis_index("y"), lax.axis_index("c")
        chips = [(1 - mx, my), (mx, 1 - my), (1 - mx, 1 - my)]
        copies = []
        for i, (r, c) in enumerate(shapes):
            for k, (px, py) in enumerate(chips):
                got = _half_block(outs[i], kinds[i], r, c, 2 * px + py, mc)
                cp = pltpu.make_async_remote_copy(src_ref=got, dst_ref=got, send_sem=send_sems.at[3 * i + k],
                                                  recv_sem=recv_sems.at[3 * i + k], device_id=(mx, my, 1 - mc), device_id_type=_MESH)
                cp.start()
                copies.append(cp)
        for cp in copies:
            cp.wait()

    return pl.pallas_call(
        body, out_shape=[jax.ShapeDtypeStruct(b.shape, b.dtype) for b in bufs], in_specs=[_ANY] * n, out_specs=[_ANY] * n,
        input_output_aliases={i: i for i in range(n)},
        scratch_shapes=[pltpu.SemaphoreType.DMA((3 * n,)), pltpu.SemaphoreType.DMA((3 * n,))], name=name)(*bufs)


def _chip_exchange_copies(pair_refs, land_refs, pairs, views, send_sems, recv_sems):
    mx, my, mc = lax.axis_index("x"), lax.axis_index("y"), lax.axis_index("c")
    me = 2 * mx + my
    chips = [(1 - mx, my), (mx, 1 - my), (1 - mx, 1 - my)]
    copies = []
    for i in range(len(pairs)):
        for k, (px, py) in enumerate(chips):
            j = 2 * px + py
            if views[i] == "chip":
                src = pair_refs[i].at[j]
            else:
                c = pairs[i].shape[1] // N_CHIPS
                src = pair_refs[i].at[:, pl.ds(pl.multiple_of(j * c, c), c)]
            copies.append(pltpu.make_async_remote_copy(
                src_ref=src, dst_ref=land_refs[i].at[me], send_sem=send_sems.at[3 * i + k], recv_sem=recv_sems.at[3 * i + k],
                device_id=(px, py, mc), device_id_type=_MESH))
    return copies


def _quad_shape(p, view):
    return p.shape if view == "chip" else (N_CHIPS, p.shape[0], p.shape[1] // N_CHIPS)


def _grads_to_chips_start(pairs, views, *, name):
    n = len(pairs)
    lands = [pltpu.with_memory_space_constraint(lax.empty(_quad_shape(p, v), p.dtype), pltpu.HBM) for p, v in zip(pairs, views)]

    def body(*refs):
        pair_refs, land_refs = refs[:n], refs[n:2 * n]
        send_sems, recv_sems = refs[2 * n], refs[2 * n + 1]
        token = refs[-1]
        for cp in _chip_exchange_copies(pair_refs, land_refs, pairs, views, send_sems, recv_sems):
            cp.start()
        token[...] = jnp.zeros_like(token)

    outs = pl.pallas_call(
        body, name=name,
        out_shape=(pltpu.SemaphoreType.DMA((3 * n,)), pltpu.SemaphoreType.DMA((3 * n,)),
                   *[pltpu.HBM(p.shape, p.dtype) for p in pairs], *[pltpu.HBM(l.shape, l.dtype) for l in lands],
                   jax.ShapeDtypeStruct((SUBLANE, LANE), F32)),
        in_specs=[_HBM] * (2 * n), out_specs=(_SEM, _SEM, *[_HBM] * (2 * n), pl.BlockSpec(memory_space=pltpu.VMEM)),
        input_output_aliases={i: 2 + i for i in range(2 * n)},
        compiler_params=pltpu.CompilerParams(has_side_effects=_EFFECT),
    )(*[pltpu.with_memory_space_constraint(p, pltpu.HBM) for p in pairs], *lands)
    return outs[0], outs[1], list(outs[2:2 + n]), list(outs[2 + n:2 + 2 * n]), outs[-1]


def _grads_to_chips_wait(send_sems, recv_sems, pairs, lands, views, after, *, name):
    n = len(pairs)

    def body(*refs):
        pair_refs, land_refs = refs[:n], refs[n:2 * n]
        s_sems, r_sems = refs[2 * n], refs[2 * n + 1]
        for cp in _chip_exchange_copies(pair_refs, land_refs, pairs, views, s_sems, r_sems):
            cp.wait_send()
            cp.wait_recv()

    outs = pl.pallas_call(
        body, name=name, out_shape=tuple(pltpu.HBM(x.shape, x.dtype) for x in list(pairs) + list(lands)),
        in_specs=[_HBM] * (2 * n) + [_SEM, _SEM, _ANY], out_specs=tuple([_HBM] * (2 * n)),
        input_output_aliases={i: i for i in range(2 * n)},
        compiler_params=pltpu.CompilerParams(has_side_effects=_EFFECT),
    )(*pairs, *lands, send_sems, recv_sems, after)
    return list(outs[n:])


def _grads_share(tots, *, name):
    n = len(tots)

    def body(*refs):
        ins, outs = refs[:n], refs[n:2 * n]
        send_sems, recv_sems = refs[2 * n:]
        mx, my, mc = lax.axis_index("x"), lax.axis_index("y"), lax.axis_index("c")
        copies = []
        for i in range(n):
            cp = pltpu.make_async_remote_copy(src_ref=ins[i], dst_ref=outs[i], send_sem=send_sems.at[i], recv_sem=recv_sems.at[i],
                                              device_id=(mx, my, 1 - mc), device_id_type=_MESH)
            cp.start()
            copies.append(cp)
        for cp in copies:
            cp.wait()

    return pl.pallas_call(
        body, out_shape=[jax.ShapeDtypeStruct(t.shape, t.dtype) for t in tots], in_specs=[_ANY] * n, out_specs=[_ANY] * n,
        scratch_shapes=[pltpu.SemaphoreType.DMA((n,)), pltpu.SemaphoreType.DMA((n,))], name=name)(*tots)


def _pair_sum(g, recv, view, c_idx, *, name):
    def body(c_ref, a_ref, b_ref, o_ref):
        o_ref[...] = (a_ref[...] + b_ref[...]).astype(WIRE_DTYPE)

    if view == "chip":
        nch, r, c = g.shape
        tr = _row_tile(r // 2, c * 4, 16)
        gv = g.reshape(nch, 2, r // 2, c)
        grid = (nch, (r // 2) // tr)
        in_specs = [pl.BlockSpec((None, None, tr, c), lambda j, i, c_ref: (j, c_ref[0], i, 0)),
                    pl.BlockSpec((None, tr, c), lambda j, i, c_ref: (j, i, 0))]
        out_spec = pl.BlockSpec((None, tr, c), lambda j, i, c_ref: (j, i, 0))
        sem = ("parallel", "parallel")
    else:
        r, c4 = g.shape
        tr = _row_tile(r // 2, c4 * 4, 16)
        gv = g.reshape(2, r // 2, c4)
        grid = ((r // 2) // tr,)
        in_specs = [pl.BlockSpec((None, tr, c4), lambda i, c_ref: (c_ref[0], i, 0)), pl.BlockSpec((tr, c4), lambda i, c_ref: (i, 0))]
        out_spec = pl.BlockSpec((tr, c4), lambda i, c_ref: (i, 0))
        sem = ("parallel",)
    grid_spec = pltpu.PrefetchScalarGridSpec(num_scalar_prefetch=1, grid=grid, in_specs=in_specs, out_specs=out_spec)
    return pl.pallas_call(body, grid_spec=grid_spec, out_shape=jax.ShapeDtypeStruct(recv.shape, WIRE_DTYPE),
                          compiler_params=_params(*sem), name=name)(c_idx, gv, recv)


def _quad_sum(gs, recvs, quads, view, chip_idx, c_idx, *, name):
    nl = len(quads)
    nch, rh, c = quads[0].shape
    tr = _row_tile(rh, c * 4, 16)

    def body(_, __, *refs):
        o_ref = refs[-1]
        per = nch + 1
        for l in range(nl):
            grp = refs[l * per:(l + 1) * per]
            acc = grp[0][...] + grp[1][...]
            for r in grp[2:]:
                acc = acc + r[...].astype(F32)
            o_ref[l] = acc

    if view == "chip":
        own = [pl.BlockSpec((None, None, tr, c), lambda i, j, h: (j[0], h[0], i, 0)),
               pl.BlockSpec((None, tr, c), lambda i, j, h: (j[0], i, 0))]
        gviews = [g.reshape(nch, 2, rh, c) for g in gs]
    else:
        own = [pl.BlockSpec((None, tr, c), lambda i, j, h: (h[0], i, j[0])), pl.BlockSpec((tr, c), lambda i, j, h: (i, j[0]))]
        gviews = [g.reshape(2, rh, nch * c) for g in gs]
    assert nch & (nch - 1) == 0
    got = [pl.BlockSpec((None, tr, c), functools.partial(lambda i, j, h, k: ((j[0] + k) & (nch - 1), i, 0), k=k))
           for k in range(1, nch)]
    ins = []
    for l in range(nl):
        ins += [gviews[l], recvs[l]] + [quads[l]] * (nch - 1)
    grid_spec = pltpu.PrefetchScalarGridSpec(
        num_scalar_prefetch=2, grid=(rh // tr,), in_specs=(own + got) * nl,
        out_specs=pl.BlockSpec((nl, tr, c), lambda i, j, h: (0, i, 0)))
    return pl.pallas_call(body, grid_spec=grid_spec, out_shape=jax.ShapeDtypeStruct((nl, rh, c), F32),
                          compiler_params=_params("parallel"), name=name)(chip_idx, c_idx, *ins)


def _sum_devices(g8, own, dev_idx, *, name):
    k, rows, cols = g8.shape

    def body(d_ref, a_ref, x_ref, o_ref):
        acc = None
        for i in range(k):
            term = jnp.where(d_ref[0] == i, x_ref[...], a_ref[i])
            acc = term if acc is None else acc + term
        o_ref[...] = acc

    grid_spec = pltpu.PrefetchScalarGridSpec(
        num_scalar_prefetch=1, grid=(1,),
        in_specs=[pl.BlockSpec((k, rows, cols), lambda i, d_ref: (0, 0, 0)), pl.BlockSpec((rows, cols), lambda i, d_ref: (0, 0))],
        out_specs=pl.BlockSpec((rows, cols), lambda i, d_ref: (0, 0)))
    return pl.pallas_call(body, grid_spec=grid_spec, out_shape=jax.ShapeDtypeStruct((rows, cols), g8.dtype),
                          compiler_params=_params("arbitrary"), name=name)(dev_idx, g8, own)


def _adamw(w, g, m, v, *, name):
    rows, cols = w.shape
    tr = rows
    for cand in (256, 128, 64, 32, 16, 8):
        if rows % cand == 0 and cand * cols <= 512 * 1024:
            tr = cand
            break
    c1 = 1.0 - ADAM_B1 ** ADAM_STEP
    c2 = 1.0 - ADAM_B2 ** ADAM_STEP

    def body(w_ref, g_ref, m_ref, v_ref, d_ref, nm_ref, nv_ref):
        gv = g_ref[...]
        nm = ADAM_B1 * m_ref[...] + (1.0 - ADAM_B1) * gv
        nv = ADAM_B2 * v_ref[...] + (1.0 - ADAM_B2) * (gv * gv)
        d_ref[...] = -ADAM_LR * ((nm / c1) / (jnp.sqrt(nv / c2) + ADAM_EPS) + ADAM_WD * w_ref[...])
        nm_ref[...] = nm
        nv_ref[...] = nv

    spec = pl.BlockSpec((tr, cols), lambda i: (i, 0))
    shp = jax.ShapeDtypeStruct((rows, cols), F32)
    return pl.pallas_call(body, grid=(rows // tr,), in_specs=[spec] * 4, out_specs=[spec] * 3, out_shape=[shp] * 3,
                          compiler_params=_params("parallel"), name=name)(w, g, m, v)


def _adamw_halves(w, m, v, mine, other, c_idx, *, name):
    nl, r, c = w.shape
    rh = r // 2
    tr = _row_tile(rh, c * 4)
    c1 = 1.0 - ADAM_B1 ** ADAM_STEP
    c2 = 1.0 - ADAM_B2 ** ADAM_STEP

    def body(c_ref, w_ref, m_ref, v_ref, a_ref, b_ref, g_ref, d_ref, nm_ref, nv_ref):
        gv = jnp.where(pl.program_id(1) == c_ref[0], a_ref[...], b_ref[...])
        nm = ADAM_B1 * m_ref[...] + (1.0 - ADAM_B1) * gv
        nv = ADAM_B2 * v_ref[...] + (1.0 - ADAM_B2) * (gv * gv)
        g_ref[...] = gv
        d_ref[...] = -ADAM_LR * ((nm / c1) / (jnp.sqrt(nv / c2) + ADAM_EPS) + ADAM_WD * w_ref[...])
        nm_ref[...] = nm
        nv_ref[...] = nv

    full = pl.BlockSpec((None, None, tr, c), lambda l, h, i, c_ref: (l, h, i, 0))
    half = pl.BlockSpec((None, tr, c), lambda l, h, i, c_ref: (l, i, 0))
    grid_spec = pltpu.PrefetchScalarGridSpec(num_scalar_prefetch=1, grid=(nl, 2, rh // tr),
                                             in_specs=[full] * 3 + [half] * 2, out_specs=[full] * 4)
    shp = jax.ShapeDtypeStruct((nl, 2, rh, c), F32)
    view = (nl, 2, rh, c)
    outs = pl.pallas_call(body, grid_spec=grid_spec, out_shape=[shp] * 4, compiler_params=_params("parallel", "parallel", "parallel"),
                          name=name)(c_idx, w.reshape(view), m.reshape(view), v.reshape(view), mine, other)
    return [o.reshape(nl, r, c) for o in outs]


WEIGHTS = ["mem_ln_g", "mem_ln_b", "w_in", "sg_ln_g", "sg_ln_b", "sg_w", "sg_b", "conv_w", "conv_b", "dt_bias", "a_log",
           "d_skip", "ssm_norm_g", "p_a", "p_b", "w_mix_o", "w_xq", "w_xkv", "w_xo", "w_ffn_in", "w_ffn_out", "ln_g", "ln_b"]
ARG_NAMES = ["x", "mem"] + WEIGHTS + ["loss_target"] + ["m_" + n for n in WEIGHTS] + ["v_" + n for n in WEIGHTS]
BIG = {"w_in": (1, (1024, 9248)), "p_a": (0, (1024, 1024)), "p_b": (0, (2048, 1024)), "w_mix_o": (0, (1024, 1024)),
       "w_xq": (0, (1024, 1024)), "w_xkv": (1, (1024, 2048)), "w_xo": (0, (1024, 1024)), "w_ffn_in": (1, (1024, 5632)),
       "w_ffn_out": (0, (2816, 1024))}
SMALL_SHARDED = {"conv_w": (4, 3072), "ln_g": (3, 1024), "ln_b": (3, 1024)}
SMALL = [n for n in WEIGHTS if n not in BIG]
W_IN_MAP = ((0, 4096, "main", 0), (4096, 7168, "main", XBC_COL0), (7168, 7200, "dt", 0), (7200, 9248, "main", GAB_COL0))
W_IN_SHARD = 9248 // N_CHIPS


def _w_in_chip_major(gm, gd):
    src = {"main": gm, "dt": gd}
    blocks = []
    for j in range(N_CHIPS):
        lo, hi = j * W_IN_SHARD, (j + 1) * W_IN_SHARD
        parts = [src[k][:, o + max(lo, a) - a:o + min(hi, b) - a] for a, b, k, o in W_IN_MAP if max(lo, a) < min(hi, b)]
        blocks.append(jnp.concatenate(parts, axis=1))
    return jnp.stack(blocks)


def _w_in_reassemble(wc):
    def cols(a, b):
        out = []
        for j in range(N_CHIPS):
            lo, hi = max(a, j * W_IN_SHARD), min(b, (j + 1) * W_IN_SHARD)
            if lo < hi:
                out.append(wc[j][:, lo - j * W_IN_SHARD:hi - j * W_IN_SHARD])
        return out

    main = sorted((m for m in W_IN_MAP if m[2] == "main"), key=lambda m: m[3])
    w_main = jnp.concatenate([p for a, b, _, _ in main for p in cols(a, b)], axis=1)
    (a, b, _, _), = [m for m in W_IN_MAP if m[2] == "dt"]
    w_dt = jnp.pad(jnp.concatenate(cols(a, b), axis=1), ((0, 0), (0, HEAD_PAD - (b - a))))
    return w_main, w_dt
GATHER_KIND = {"w_in": "chip", "p_a": "row", "p_b": "row", "w_mix_o": "row", "w_xq": "row", "w_xkv": "col", "w_xo": "row",
               "w_ffn_in": "col", "w_ffn_out": "row", "conv_w": "chip", "ln_g": "chip", "ln_b": "chip"}
GRAD_VIEW = {n: ("col" if k == "col" else "chip") for n, k in GATHER_KIND.items() if n in BIG}


def _shard_shape(name):
    axis, (r, c) = BIG[name]
    return (r // N_CHIPS, c) if axis == 0 else (r, c // N_CHIPS)


def _pad_rows(flat, cols, row_mult):
    n = flat.shape[0]
    rows = -(-n // cols)
    rows = -(-rows // row_mult) * row_mult
    return jnp.pad(flat, (0, rows * cols - n)).reshape(rows, cols)


def _gather_small_params(a, chip):
    names = list(SMALL_SHARDED)
    kinds = [GATHER_KIND[n] for n in names]
    bufs = [_cast_place(a[n], GATHER_KIND[n], F32, chip.reshape(1), name=f"place_{n}") for n in names]
    outs = _gather_params(bufs, [a[n].shape[1:] for n in names], kinds, name="gather_small_params")
    full = {}
    for n, o in zip(names, outs):
        _, _, r, c = o.shape
        full[n] = jnp.transpose(o, (0, 2, 1, 3)).reshape(DEPTH, r, N_CHIPS * c)
    return full


GATHER_GROUPS = (("w_in",), tuple(n for n in BIG if n != "w_in"))


def _gather_group_start(a, l, names, chip, after, *, tag):
    bufs = [_cast_place_layer(a[n], l, GATHER_KIND[n], chip.reshape(1), after, name=f"place_{n}_l{l}") for n in names]
    return _gather_start(bufs, [a[n].shape[1:] for n in names], [GATHER_KIND[n] for n in names], name=f"gather_start_{tag}")


def _gather_group_finish(a, names, flight, after, *, tag):
    send_sems, recv_sems, bufs, token = flight
    shapes, kinds = [a[n].shape[1:] for n in names], [GATHER_KIND[n] for n in names]
    bufs = _gather_wait(send_sems, recv_sems, bufs, shapes, kinds, token if after is None else after, name=f"gather_wait_{tag}")
    full = dict(zip(names, _gather_forward(bufs, shapes, kinds, name=f"gather_forward_{tag}")))
    if "w_in" in full:
        full["w_main"], full["w_dt"] = _w_in_reassemble(full.pop("w_in"))
    return full


def _layer_weights(a, big, small, l):
    w = dict(big)
    for n in SMALL_SHARDED:
        w[n] = small[n][l]
    for n in ["sg_ln_g", "sg_ln_b", "sg_w", "conv_b", "ssm_norm_g"]:
        w[n] = a[n][l]
    w["sg_bcol"] = a["sg_b"][l][..., None]
    for n in ["dt_bias", "a_log"]:
        w[n + "8"] = _pad_heads(a[n][l])
    w["d_skipx"] = _expand_heads(a["d_skip"][l])
    return w


def _grad_views(grads, names):
    gs = []
    for n in names:
        axis, _ = BIG[n]
        r, c = _shard_shape(n)
        if n == "w_in":
            gs.append(_w_in_chip_major(grads["w_main"], grads["w_dt"]))
        elif axis == 0:
            gs.append(grads[n].reshape(N_CHIPS, r, c))
        else:
            gs.append(grads[n])
    return gs


class _GradExchange:
    def __init__(self, grads, names, c_idx, tag):
        self.names, self.c_idx, self.tag = names, c_idx, tag
        self.views = [GRAD_VIEW[n] for n in names]
        self.gs = _grad_views(grads, names)

    def start(self):
        self.sems = _grads_to_sibling_start(self.gs, self.views, name=f"grads_to_sibling_start_{self.tag}")
        return self.sems[4]

    def cross(self, after):
        send_sems, recv_sems, gs, lands, token = self.sems
        self.gs, self.recv = _grads_to_sibling_wait(send_sems, recv_sems, gs, lands, self.views, token if after is None else after,
                                                    name=f"grads_to_sibling_wait_{self.tag}")
        cpre = self.c_idx.reshape(1)
        pairs = [_pair_sum(g, rv, v, cpre, name=f"grads_pair_sum_{n}_{self.tag}")
                 for g, rv, v, n in zip(self.gs, self.recv, self.views, self.names)]
        self.sems = _grads_to_chips_start(pairs, self.views, name=f"grads_to_chips_start_{self.tag}")
        return self.sems[4]

    def finish(self, after):
        send_sems, recv_sems, pairs, lands, _ = self.sems
        quads = _grads_to_chips_wait(send_sems, recv_sems, pairs, lands, self.views, after, name=f"grads_to_chips_wait_{self.tag}")
        return {n: (g, rv, q) for n, g, rv, q in zip(self.names, self.gs, self.recv, quads)}


def _finish_big_grads(parts, c_idx, chip):
    tots = [_quad_sum([parts[l][n][0] for l in range(DEPTH)], [parts[l][n][1] for l in range(DEPTH)],
                      [parts[l][n][2] for l in range(DEPTH)], GRAD_VIEW[n], chip.reshape(1), c_idx.reshape(1),
                      name=f"grads_chip_sum_{n}") for n in BIG]
    others = _grads_share(tots, name="grads_share")
    return {n: (t, o) for n, t, o in zip(BIG, tots, others)}


def _direct_copies(x_ref, land_ref, send_sems, recv_sems):
    mx, my, mc = lax.axis_index("x"), lax.axis_index("y"), lax.axis_index("c")
    me = 4 * mx + 2 * my + mc
    copies = []
    for k in range(N_DEV - 1):
        f = k + 1
        to = (mx ^ (f >> 2 & 1), my ^ (f >> 1 & 1), mc ^ (f & 1))
        copies.append(pltpu.make_async_remote_copy(src_ref=x_ref, dst_ref=land_ref.at[me], send_sem=send_sems.at[k],
                                                   recv_sem=recv_sems.at[k], device_id=to, device_id_type=_MESH))
    return copies


def _all_gather8_start(x, *, name):
    land = pltpu.with_memory_space_constraint(lax.empty((N_DEV,) + x.shape, x.dtype), pltpu.HBM)

    def body(x_ref, land_ref, send_sems, recv_sems, x_out, land_out, token):
        for cp in _direct_copies(x_ref, land_ref, send_sems, recv_sems):
            cp.start()
        token[...] = jnp.zeros_like(token)

    n = N_DEV - 1
    return pl.pallas_call(
        body, name=name,
        out_shape=(pltpu.SemaphoreType.DMA((n,)), pltpu.SemaphoreType.DMA((n,)), pltpu.HBM(x.shape, x.dtype),
                   pltpu.HBM(land.shape, land.dtype), jax.ShapeDtypeStruct((SUBLANE, LANE), F32)),
        in_specs=[_HBM, _HBM], out_specs=(_SEM, _SEM, _HBM, _HBM, pl.BlockSpec(memory_space=pltpu.VMEM)),
        input_output_aliases={0: 2, 1: 3}, compiler_params=pltpu.CompilerParams(has_side_effects=_EFFECT),
    )(pltpu.with_memory_space_constraint(x, pltpu.HBM), land)


def _all_gather8_wait(send_sems, recv_sems, x, land, after, *, name):
    def body(x_ref, land_ref, s_sems, r_sems, _, x_out, land_out):
        for cp in _direct_copies(x_ref, land_ref, s_sems, r_sems):
            cp.wait_send()
            cp.wait_recv()

    return pl.pallas_call(
        body, name=name, out_shape=(pltpu.HBM(x.shape, x.dtype), pltpu.HBM(land.shape, land.dtype)),
        in_specs=[_HBM, _HBM, _SEM, _SEM, _ANY], out_specs=(_HBM, _HBM), input_output_aliases={0: 0, 1: 1},
        compiler_params=pltpu.CompilerParams(has_side_effects=_EFFECT),
    )(x, land, send_sems, recv_sems, after)


def _pack_small(small):
    return _pad_rows(jnp.concatenate([small[n].reshape(-1) for n in small]), LANE, SUBLANE)


def _unpack_small(small, g8, packed, chip, c_idx, *, name):
    names = list(small)
    tot = _sum_devices(g8, packed, (2 * chip + c_idx).reshape(1), name=name).reshape(-1)
    out, off = {}, 0
    for n in names:
        sz = small[n].size
        full = tot[off:off + sz].reshape(small[n].shape)
        off += sz
        if n in SMALL_SHARDED:
            cs = SMALL_SHARDED[n][1] // N_CHIPS
            full = lax.dynamic_slice_in_dim(full, chip * cs, cs, axis=-1)
        out[n] = full
    return out


def kernel(x, mem, mem_ln_g, mem_ln_b, w_in, sg_ln_g, sg_ln_b, sg_w, sg_b, conv_w, conv_b, dt_bias, a_log, d_skip, ssm_norm_g, p_a, p_b, w_mix_o, w_xq, w_xkv, w_xo, w_ffn_in, w_ffn_out, ln_g, ln_b, loss_target, m_mem_ln_g, m_mem_ln_b, m_w_in, m_sg_ln_g, m_sg_ln_b, m_sg_w, m_sg_b, m_conv_w, m_conv_b, m_dt_bias, m_a_log, m_d_skip, m_ssm_norm_g, m_p_a, m_p_b, m_w_mix_o, m_w_xq, m_w_xkv, m_w_xo, m_w_ffn_in, m_w_ffn_out, m_ln_g, m_ln_b, v_mem_ln_g, v_mem_ln_b, v_w_in, v_sg_ln_g, v_sg_ln_b, v_sg_w, v_sg_b, v_conv_w, v_conv_b, v_dt_bias, v_a_log, v_d_skip, v_ssm_norm_g, v_p_a, v_p_b, v_w_mix_o, v_w_xq, v_w_xkv, v_w_xo, v_w_ffn_in, v_w_ffn_out, v_ln_g, v_ln_b):
    a = dict(zip(ARG_NAMES, (x, mem, mem_ln_g, mem_ln_b, w_in, sg_ln_g, sg_ln_b, sg_w, sg_b, conv_w, conv_b, dt_bias, a_log, d_skip, ssm_norm_g, p_a, p_b, w_mix_o, w_xq, w_xkv, w_xo, w_ffn_in, w_ffn_out, ln_g, ln_b, loss_target, m_mem_ln_g, m_mem_ln_b, m_w_in, m_sg_ln_g, m_sg_ln_b, m_sg_w, m_sg_b, m_conv_w, m_conv_b, m_dt_bias, m_a_log, m_d_skip, m_ssm_norm_g, m_p_a, m_p_b, m_w_mix_o, m_w_xq, m_w_xkv, m_w_xo, m_w_ffn_in, m_w_ffn_out, m_ln_g, m_ln_b, v_mem_ln_g, v_mem_ln_b, v_w_in, v_sg_ln_g, v_sg_ln_b, v_sg_w, v_sg_b, v_conv_w, v_conv_b, v_dt_bias, v_a_log, v_d_skip, v_ssm_norm_g, v_p_a, v_p_b, v_w_mix_o, v_w_xq, v_w_xkv, v_w_xo, v_w_ffn_in, v_w_ffn_out, v_ln_g, v_ln_b)))
    c_idx = lax.axis_index("c").astype(jnp.int32)
    chip = (2 * lax.axis_index("x") + lax.axis_index("y")).astype(jnp.int32)

    small = _gather_small_params(a, chip)
    ga, gb = GATHER_GROUPS
    flights = {(0, 0): _gather_group_start(a, 0, ga, chip, small["ln_b"], tag="l0_a")}
    flights[0, 1] = _gather_group_start(a, 0, gb, chip, flights[0, 0][3], tag="l0_b")

    def layer_weights(after, l):
        first = _gather_group_finish(a, ga, flights[l, 0], after if l else flights[l, 1][3], tag=f"l{l}_a")

        def rest(w, after_b):
            more = _gather_group_finish(a, gb, flights[l, 1], after_b, tag=f"l{l}_b")
            if l + 1 < DEPTH:
                flights[l + 1, 0] = _gather_group_start(a, l + 1, ga, chip, more["p_a"], tag=f"l{l + 1}_a")
                flights[l + 1, 1] = _gather_group_start(a, l + 1, gb, chip, flights[l + 1, 0][3], tag=f"l{l + 1}_b")
                more["p_a"] = more["p_a"] + flights[l + 1, 1][3][0, 0].astype(MXU_DTYPE)
            return {k: v for k, v in {**w, **more}.items() if k != "rest"}

        return dict(_layer_weights(a, first, small, l), rest=rest)

    layers = [functools.partial(layer_weights, l=l) for l in range(DEPTH)]
    exchanges, seen, small_flight = [], {}, {}

    def start_exchange(l, names, grads_l):
        ex = _GradExchange(grads_l, names, c_idx, f"l{l}_{names[0]}")
        tokens = [ex.start()]
        if exchanges:
            tokens.append(exchanges[-1][1].cross(tokens[0]))
        exchanges.append((l, ex))
        seen[l] = grads_l
        if l == 0 and names == GRAD_GROUPS[-1]:
            tokens.append(ex.cross(None))
            small = {}
            for n in SMALL:
                if n.startswith("mem_ln"):
                    continue
                per_layer = []
                for k in range(DEPTH):
                    g = seen[k][n]
                    if n in ("dt_bias", "a_log", "d_skip"):
                        g = g[0, :SSM_HEADS]
                    per_layer.append(g.reshape(a[n].shape[1:-1] + (-1,)))
                small[n] = jnp.stack(per_layer)
            small_flight["small"] = small
            small_flight["sems"] = _all_gather8_start(_pack_small(small), name="gather_small_grads_start")
            tokens.append(small_flight["sems"][4])
        return sum(tokens[1:], tokens[0])

    lsum, grad_x, grads, d_mem_g, d_mem_b = _local_step(x, mem, loss_target, mem_ln_g, mem_ln_b, layers, start_exchange)
    loss = lax.psum(0.5 * jnp.sum(lsum) / D_MODEL, ("x", "y", "c"))

    parts = [{} for _ in range(DEPTH)]
    for l, ex in exchanges:
        parts[l].update(ex.finish(grad_x))
    halves = _finish_big_grads(parts, c_idx, chip)
    gw = {}
    send_sems, recv_sems, packed, land, _ = small_flight["sems"]
    packed, g8 = _all_gather8_wait(send_sems, recv_sems, packed, land, grad_x, name="gather_small_grads_wait")
    gw.update(_unpack_small(small_flight["small"], g8, packed, chip, c_idx, name="small_grads_sum"))
    mem_small = {"mem_ln_g": d_mem_g, "mem_ln_b": d_mem_b}
    mem_packed = _pack_small(mem_small)
    gw.update(_unpack_small(mem_small, _all_gather8(mem_packed, name="gather_mem_ln_grads"), mem_packed, chip, c_idx,
                            name="mem_ln_grads_sum"))

    delta, new_m, new_v = {}, {}, {}
    for n in BIG:
        mine, other = halves[n]
        gw[n], delta[n], new_m[n], new_v[n] = _adamw_halves(a[n], a["m_" + n], a["v_" + n], mine, other, c_idx.reshape(1),
                                                             name=f"adamw_{n}")
    for n in SMALL:
        shp = a[n].shape
        view = (-1, LANE) if a[n].size % LANE == 0 else (1, -1)
        outs = _adamw(*[v.reshape(view) for v in (a[n], gw[n], a["m_" + n], a["v_" + n])], name=f"adamw_{n}")
        delta[n], new_m[n], new_v[n] = (o.reshape(shp) for o in outs)
    return (loss, grad_x, *[gw[n].reshape(a[n].shape) for n in WEIGHTS], *[delta[n] for n in WEIGHTS],
            *[new_m[n] for n in WEIGHTS], *[new_v[n] for n in WEIGHTS])
```

```python
import functools
import math

import jax
import jax.numpy as jnp
from jax import lax
from jax.experimental import pallas as pl
from jax.experimental.pallas import tpu as pltpu

F32 = jnp.float32
MXU_DTYPE = jnp.bfloat16
WIRE_DTYPE = jnp.bfloat16
STASH_DTYPE = jnp.bfloat16

D_MODEL = 1024
DEPTH = 2
CHUNK = 128
SG_GROUPS = 8
SSM_INNER = 2048
SSM_HEADDIM = 64
SSM_HEADS = 32
SSM_STATE = 128
SSM_GROUPS = 4
SSM_CONV = 4
SSM_CONV_DIM = 3072
X_HEADS = 4
X_HEADDIM = 256
FFN_HIDDEN = 2816
ALPHA = float((2 * DEPTH) ** 0.25)
LN_EPS = 1e-5
RMS_EPS = 1e-5
ADAM_LR = 0.001
ADAM_B1 = 0.9
ADAM_B2 = 0.999
ADAM_EPS = 1e-08
ADAM_WD = 0.01
ADAM_STEP = 10

MAIN_COLS = 9216
UVZ_COLS = 4096
GAB_COL0 = 4096
XBC_COL0 = 6144
HEAD_PAD = 128

VMEM_LIMIT = 56 * 1024 * 1024
BLOCK_BYTES = 2 * 1024 * 1024
ROW_TILES = (512, 256, 128)
LANE = 128
SUBLANE = 8

N_CHIPS = 4
N_DEV = 8


def _pick(n, cands):
    for c in cands:
        if n % c == 0:
            return c
    return n


MM_TILE_MAX = 1408
MM_OPERAND_BYTES = 8 * 1024 * 1024


def _div_tile(n, limit):
    best = None
    for t in range(LANE, min(n, limit) + 1, LANE):
        if n % t == 0:
            best = t
    return n if best is None else best


def _params(*sem):
    return pltpu.CompilerParams(dimension_semantics=tuple(sem), vmem_limit_bytes=VMEM_LIMIT)


_ANY = pl.BlockSpec(memory_space=pl.ANY)
_MESH = pl.DeviceIdType.MESH


def _nt(a, b):
    return lax.dot_general(a, b, (((1,), (1,)), ((), ())), preferred_element_type=F32)


def _tn(a, b):
    return lax.dot_general(a, b, (((0,), (0,)), ((), ())), preferred_element_type=F32)


def _nn(a, b):
    return jnp.dot(a, b, preferred_element_type=F32)


def _sigmoid(x):
    return 0.5 * jnp.tanh(0.5 * x) + 0.5


def _split3(v):
    def top(x):
        bits = lax.bitcast_convert_type(x, jnp.uint32) & jnp.uint32(0xFFFF0000)
        return lax.bitcast_convert_type(bits, F32)

    v1 = top(v)
    r1 = v - v1
    v2 = top(r1)
    v3 = r1 - v2
    return v1.astype(jnp.bfloat16), v2.astype(jnp.bfloat16), v3.astype(jnp.bfloat16)


def _dot_exact(a, b, dn, data):
    if data == 0:
        mat = b.astype(jnp.bfloat16)
        return sum(lax.dot_general(p, mat, dn, preferred_element_type=F32) for p in _split3(a))
    mat = a.astype(jnp.bfloat16)
    return sum(lax.dot_general(mat, p, dn, preferred_element_type=F32) for p in _split3(b))


_DN_NN = (((1,), (0,)), ((), ()))
_DN_TN = (((0,), (0,)), ((), ()))


def _gelu(x):
    return 0.5 * x * (1.0 + lax.erf(x * (2.0 ** -0.5)))


def _gelu_grad(x):
    return 0.5 * (1.0 + lax.erf(x * (2.0 ** -0.5))) + x * jnp.exp(-0.5 * x * x) * (1.0 / math.sqrt(2.0 * math.pi))


def _mm(a, b, *, ta=False, tb=False, out_dtype=F32, after=None, name):
    if ta:
        kdim, m = a.shape
    else:
        m, kdim = a.shape
    if tb:
        n, k2 = b.shape[-2:]
    else:
        k2, n = b.shape[-2:]
    assert kdim == k2, (a.shape, b.shape, ta, tb)
    tm = _div_tile(m, MM_TILE_MAX)
    tn = _div_tile(n, MM_TILE_MAX)
    tk = _div_tile(kdim, MM_OPERAND_BYTES // (tm * a.dtype.itemsize + tn * b.dtype.itemsize))
    nk = kdim // tk
    dn = (((0 if ta else 1,), (1 if tb else 0,)), ((), ()))

    extra = [] if after is None else [after]

    def body(a_ref, b_ref, *rest):
        o_ref = rest[len(extra)]
        d = lax.dot_general(a_ref[...].astype(MXU_DTYPE), b_ref[...].astype(MXU_DTYPE), dn, preferred_element_type=F32)
        if nk == 1:
            o_ref[...] = d.astype(out_dtype)
            return
        acc_ref = rest[len(extra) + 1]
        k = pl.program_id(2)

        @pl.when(k == 0)
        def _():
            acc_ref[...] = d

        @pl.when(jnp.logical_and(k > 0, k < nk - 1))
        def _():
            acc_ref[...] += d

        @pl.when(k == nk - 1)
        def _():
            o_ref[...] = (acc_ref[...] + d).astype(out_dtype)

    a_spec = pl.BlockSpec((tk, tm), lambda i, j, k: (k, i)) if ta else pl.BlockSpec((tm, tk), lambda i, j, k: (i, k))
    b_spec = pl.BlockSpec((tn, tk), lambda i, j, k: (j, k)) if tb else pl.BlockSpec((tk, tn), lambda i, j, k: (k, j))
    return pl.pallas_call(
        body, grid=(m // tm, n // tn, nk), in_specs=[a_spec, b_spec] + [_ANY] * len(extra),
        out_specs=pl.BlockSpec((tm, tn), lambda i, j, k: (i, j)),
        out_shape=jax.ShapeDtypeStruct((m, n), out_dtype),
        scratch_shapes=[pltpu.VMEM((tm, tn), F32)] if nk > 1 else [],
        compiler_params=_params("parallel", "parallel", "arbitrary"), name=name)(a, b, *extra)


def _row_spec(tm, c, col=0):
    return pl.BlockSpec((tm, c), lambda i: (i, col))


def _par_spec(shape):
    nd = len(shape)
    return pl.BlockSpec(shape, lambda i: (0,) * nd)


def _ln_fwd(x, f, g, b, *, name):
    t, c = x.shape
    tm = _pick(t, ROW_TILES)
    has_f = f is not None

    def body(*refs):
        if has_f:
            x_ref, f_ref, g_ref, b_ref, y_ref, yb_ref, xh_ref, rs_ref = refs
            r = ALPHA * x_ref[...] + f_ref[...]
        else:
            x_ref, g_ref, b_ref, y_ref, yb_ref, xh_ref, rs_ref = refs
            r = x_ref[...]
        mu = jnp.mean(r, axis=-1, keepdims=True)
        xc = r - mu
        var = jnp.mean(xc * xc, axis=-1, keepdims=True)
        rstd = lax.rsqrt(var + LN_EPS)
        xh = xc * rstd
        y = xh * g_ref[...] + b_ref[...]
        y_ref[...] = y
        yb_ref[...] = y.astype(MXU_DTYPE)
        xh_ref[...] = xh
        rs_ref[...] = jnp.broadcast_to(rstd, rs_ref.shape)

    ins = [x] + ([f] if has_f else []) + [g.reshape(1, c), b.reshape(1, c)]
    in_specs = [_row_spec(tm, c)] * (2 if has_f else 1) + [_par_spec((1, c))] * 2
    return pl.pallas_call(
        body, grid=(t // tm,), in_specs=in_specs,
        out_specs=[_row_spec(tm, c), _row_spec(tm, c), _row_spec(tm, c), _row_spec(tm, LANE)],
        out_shape=[jax.ShapeDtypeStruct((t, c), F32), jax.ShapeDtypeStruct((t, c), MXU_DTYPE),
                   jax.ShapeDtypeStruct((t, c), F32), jax.ShapeDtypeStruct((t, LANE), F32)],
        compiler_params=_params("parallel"), name=name)(*ins)


def _ln_bwd(addends, scales, xh, rs, g, *, name):
    t, c = xh.shape
    tm = _pick(t, ROW_TILES)
    na = len(addends)

    def body(*refs):
        a_refs = refs[:na]
        xh_ref, rs_ref, g_ref, dp_ref, dpb_ref, dg_ref, db_ref = refs[na:]

        @pl.when(pl.program_id(0) == 0)
        def _():
            dg_ref[...] = jnp.zeros_like(dg_ref)
            db_ref[...] = jnp.zeros_like(db_ref)

        dy = None
        for s, r in zip(scales, a_refs):
            term = r[...] if s == 1.0 else s * r[...]
            dy = term if dy is None else dy + term
        xhv = xh_ref[...]
        dxh = dy * g_ref[...]
        m1 = jnp.mean(dxh, axis=-1, keepdims=True)
        m2 = jnp.mean(dxh * xhv, axis=-1, keepdims=True)
        dp = rs_ref[:, 0:1] * (dxh - m1 - xhv * m2)
        dp_ref[...] = dp
        dpb_ref[...] = dp.astype(MXU_DTYPE)
        dg_ref[...] += jnp.sum(dy * xhv, axis=0, keepdims=True)
        db_ref[...] += jnp.sum(dy, axis=0, keepdims=True)

    in_specs = [_row_spec(tm, c)] * (na + 1) + [_row_spec(tm, LANE), _par_spec((1, c))]
    return pl.pallas_call(
        body, grid=(t // tm,), in_specs=in_specs,
        out_specs=[_row_spec(tm, c), _row_spec(tm, c), _par_spec((1, c)), _par_spec((1, c))],
        out_shape=[jax.ShapeDtypeStruct((t, c), F32), jax.ShapeDtypeStruct((t, c), MXU_DTYPE),
                   jax.ShapeDtypeStruct((1, c), F32), jax.ShapeDtypeStruct((1, c), F32)],
        compiler_params=_params("arbitrary"), name=name)(*addends, xh, rs, g.reshape(1, c))


def _add_scaled(addends, scales, *, name):
    t, c = addends[0].shape
    tm = _pick(t, ROW_TILES)
    na = len(addends)

    def body(*refs):
        acc = None
        for s, r in zip(scales, refs[:na]):
            term = r[...] if s == 1.0 else s * r[...]
            acc = term if acc is None else acc + term
        refs[na][...] = acc

    return pl.pallas_call(
        body, grid=(t // tm,), in_specs=[_row_spec(tm, c)] * na, out_specs=_row_spec(tm, c),
        out_shape=jax.ShapeDtypeStruct((t, c), F32), compiler_params=_params("parallel"), name=name)(*addends)


def _loss_head(y, tgt, *, name):
    t, c = y.shape
    tm = _pick(t, ROW_TILES)

    def body(y_ref, t_ref, dy_ref, ls_ref):
        @pl.when(pl.program_id(0) == 0)
        def _():
            ls_ref[...] = jnp.zeros_like(ls_ref)

        e = y_ref[...] - t_ref[...]
        dy_ref[...] = e * (1.0 / c)
        ls_ref[...] += jnp.sum(e * e, axis=0, keepdims=True)

    return pl.pallas_call(
        body, grid=(t // tm,), in_specs=[_row_spec(tm, c)] * 2,
        out_specs=[_row_spec(tm, c), _par_spec((1, c))],
        out_shape=[jax.ShapeDtypeStruct((t, c), F32), jax.ShapeDtypeStruct((1, c), F32)],
        compiler_params=_params("arbitrary"), name=name)(y, tgt)


def _swiglu_fwd(h, *, name):
    t, two_f = h.shape
    fh = two_f // 2
    tm = _pick(t, (256, 128))

    def body(g_ref, u_ref, a_ref):
        g = g_ref[...].astype(F32)
        a_ref[...] = (g * _sigmoid(g) * u_ref[...].astype(F32)).astype(MXU_DTYPE)

    return pl.pallas_call(
        body, grid=(t // tm,), in_specs=[_row_spec(tm, fh, 0), _row_spec(tm, fh, 1)], out_specs=_row_spec(tm, fh),
        out_shape=jax.ShapeDtypeStruct((t, fh), MXU_DTYPE), compiler_params=_params("parallel"), name=name)(h, h)


def _swiglu_bwd(h, da, *, name):
    t, two_f = h.shape
    fh = two_f // 2
    tm = _pick(t, (256, 128))

    def body(g_ref, u_ref, da_ref, dh_ref):
        g = g_ref[...].astype(F32)
        s = _sigmoid(g)
        dav = da_ref[...].astype(F32)
        dh_ref[:, :fh] = (dav * u_ref[...].astype(F32) * (s * (1.0 + g * (1.0 - s)))).astype(MXU_DTYPE)
        dh_ref[:, fh:] = (dav * g * s).astype(MXU_DTYPE)

    return pl.pallas_call(
        body, grid=(t // tm,), in_specs=[_row_spec(tm, fh, 0), _row_spec(tm, fh, 1), _row_spec(tm, fh)],
        out_specs=_row_spec(tm, two_f), out_shape=jax.ShapeDtypeStruct((t, two_f), MXU_DTYPE),
        compiler_params=_params("parallel"), name=name)(h, h, da)


def _attn_probs(q, k):
    s = _nt(q, k) * (X_HEADDIM ** -0.5)
    s = s - jnp.max(s, axis=-1, keepdims=True)
    p = jnp.exp(s)
    return p / jnp.sum(p, axis=-1, keepdims=True)


def _attn_fwd(q, kv, *, bsz, name):
    t = q.shape[0]
    s = t // bsz
    ml = kv.shape[0] // bsz
    hd = X_HEADDIM

    def body(q_ref, k_ref, v_ref, o_ref):
        p = _attn_probs(q_ref[...], k_ref[...])
        o_ref[...] = _nn(p.astype(MXU_DTYPE), v_ref[...]).astype(MXU_DTYPE)

    return pl.pallas_call(
        body, grid=(bsz, X_HEADS),
        in_specs=[pl.BlockSpec((s, hd), lambda b, h: (b, h)), pl.BlockSpec((ml, hd), lambda b, h: (b, h)),
                  pl.BlockSpec((ml, hd), lambda b, h: (b, X_HEADS + h))],
        out_specs=pl.BlockSpec((s, hd), lambda b, h: (b, h)),
        out_shape=jax.ShapeDtypeStruct((t, D_MODEL), MXU_DTYPE),
        compiler_params=_params("parallel", "parallel"), name=name)(q, kv, kv)


def _attn_bwd(q, kv, do, *, bsz, name):
    t = q.shape[0]
    s = t // bsz
    ml = kv.shape[0] // bsz
    hd = X_HEADDIM

    def body(q_ref, k_ref, v_ref, do_ref, dq_ref, dk_ref, dv_ref):
        qv, kk, vv, dov = q_ref[...], k_ref[...], v_ref[...], do_ref[...]
        p = _attn_probs(qv, kk)
        dp = _nt(dov, vv)
        dv_ref[...] = _tn(p.astype(MXU_DTYPE), dov).astype(MXU_DTYPE)
        ds = (p * (dp - jnp.sum(dp * p, axis=-1, keepdims=True)) * (X_HEADDIM ** -0.5)).astype(MXU_DTYPE)
        dq_ref[...] = _nn(ds, kk).astype(MXU_DTYPE)
        dk_ref[...] = _tn(ds, qv).astype(MXU_DTYPE)

    blk_q = pl.BlockSpec((s, hd), lambda b, h: (b, h))
    blk_m = pl.BlockSpec((ml, hd), lambda b, h: (b, h))
    return pl.pallas_call(
        body, grid=(bsz, X_HEADS),
        in_specs=[blk_q, blk_m, pl.BlockSpec((ml, hd), lambda b, h: (b, X_HEADS + h)), blk_q],
        out_specs=[blk_q, blk_m, blk_m],
        out_shape=[jax.ShapeDtypeStruct((t, D_MODEL), MXU_DTYPE), jax.ShapeDtypeStruct((bsz * ml, D_MODEL), MXU_DTYPE),
                   jax.ShapeDtypeStruct((bsz * ml, D_MODEL), MXU_DTYPE)],
        compiler_params=_params("parallel", "parallel"), name=name)(q, kv, kv, do)


def _causal(n):
    row = lax.broadcasted_iota(jnp.int32, (n, n), 0)
    col = lax.broadcasted_iota(jnp.int32, (n, n), 1)
    return row >= col


def _sg_norm(v, g, b):
    gv = _gelu(v)
    mu = jnp.mean(gv, axis=-1, keepdims=True)
    xc = gv - mu
    var = jnp.mean(xc * xc, axis=-1, keepdims=True)
    rstd = lax.rsqrt(var + LN_EPS)
    xh = xc * rstd
    return xh, rstd, xh * g + b


def _sg_fwd(proj, ln_g, ln_b, w, bcol, *, name):
    t = proj.shape[0]
    c = D_MODEL
    gd = c // SG_GROUPS

    def body(u_ref, v_ref, g_ref, b_ref, w_ref, bc_ref, o_ref):
        gu = _gelu(u_ref[...].astype(F32))
        _, _, vn = _sg_norm(v_ref[...].astype(F32), g_ref[...], b_ref[...])
        mask = _causal(CHUNK)
        for g in range(SG_GROUPS):
            sl = slice(g * gd, (g + 1) * gd)
            wg = jnp.where(mask, w_ref[g], 0.0).astype(MXU_DTYPE)
            mixed = _nn(wg, vn[:, sl].astype(MXU_DTYPE)) + bc_ref[g]
            o_ref[:, sl] = (gu[:, sl] * mixed).astype(MXU_DTYPE)

    return pl.pallas_call(
        body, grid=(t // CHUNK,),
        in_specs=[_row_spec(CHUNK, c, 0), _row_spec(CHUNK, c, 1), _par_spec((1, c)), _par_spec((1, c)),
                  _par_spec((SG_GROUPS, CHUNK, CHUNK)), _par_spec((SG_GROUPS, CHUNK, 1))],
        out_specs=_row_spec(CHUNK, c), out_shape=jax.ShapeDtypeStruct((t, c), MXU_DTYPE),
        compiler_params=_params("parallel"), name=name)(proj, proj, ln_g.reshape(1, c), ln_b.reshape(1, c), w, bcol)


def _sg_bwd(proj, dsgo, ln_g, ln_b, w, bcol, dproj, *, name):
    t = proj.shape[0]
    c = D_MODEL
    gd = c // SG_GROUPS

    def body(u_ref, v_ref, d_ref, g_ref, b_ref, w_ref, bc_ref, _, duv_ref, dw_ref, dbc_ref, dg_ref, db_ref, dvn_ref):
        @pl.when(pl.program_id(0) == 0)
        def _():
            dw_ref[...] = jnp.zeros_like(dw_ref)
            dbc_ref[...] = jnp.zeros_like(dbc_ref)
            dg_ref[...] = jnp.zeros_like(dg_ref)
            db_ref[...] = jnp.zeros_like(db_ref)

        u = u_ref[...].astype(F32)
        v = v_ref[...].astype(F32)
        dso = d_ref[...].astype(F32)
        gu = _gelu(u)
        xh, rstd, vn = _sg_norm(v, g_ref[...], b_ref[...])
        mask = _causal(CHUNK)
        for g in range(SG_GROUPS):
            sl = slice(g * gd, (g + 1) * gd)
            wg = jnp.where(mask, w_ref[g], 0.0).astype(MXU_DTYPE)
            vng = vn[:, sl].astype(MXU_DTYPE)
            mixed = _nn(wg, vng) + bc_ref[g]
            duv_ref[:, sl] = (dso[:, sl] * mixed * _gelu_grad(u[:, sl])).astype(MXU_DTYPE)
            dmix = dso[:, sl] * gu[:, sl]
            dmb = dmix.astype(MXU_DTYPE)
            dbc_ref[g] += jnp.sum(dmix, axis=-1, keepdims=True)
            dw_ref[g] += jnp.where(mask, _nt(dmb, vng), 0.0)
            dvn_ref[:, sl] = _tn(wg, dmb)
        dvn = dvn_ref[...]
        dg_ref[...] += jnp.sum(dvn * xh, axis=0, keepdims=True)
        db_ref[...] += jnp.sum(dvn, axis=0, keepdims=True)
        dxh = dvn * g_ref[...]
        m1 = jnp.mean(dxh, axis=-1, keepdims=True)
        m2 = jnp.mean(dxh * xh, axis=-1, keepdims=True)
        dgv = rstd * (dxh - m1 - xh * m2)
        duv_ref[:, c:] = (dgv * _gelu_grad(v)).astype(MXU_DTYPE)

    return pl.pallas_call(
        body, grid=(t // CHUNK,),
        in_specs=[_row_spec(CHUNK, c, 0), _row_spec(CHUNK, c, 1), _row_spec(CHUNK, c), _par_spec((1, c)),
                  _par_spec((1, c)), _par_spec((SG_GROUPS, CHUNK, CHUNK)), _par_spec((SG_GROUPS, CHUNK, 1)), _ANY],
        out_specs=[_row_spec(CHUNK, 2 * c), _par_spec((SG_GROUPS, CHUNK, CHUNK)), _par_spec((SG_GROUPS, CHUNK, 1)),
                   _par_spec((1, c)), _par_spec((1, c))],
        out_shape=[jax.ShapeDtypeStruct(dproj.shape, dproj.dtype), jax.ShapeDtypeStruct((SG_GROUPS, CHUNK, CHUNK), F32),
                   jax.ShapeDtypeStruct((SG_GROUPS, CHUNK, 1), F32), jax.ShapeDtypeStruct((1, c), F32),
                   jax.ShapeDtypeStruct((1, c), F32)],
        scratch_shapes=[pltpu.VMEM((CHUNK, c), F32)], input_output_aliases={7: 0},
        compiler_params=_params("arbitrary"), name=name)(proj, proj, dsgo, ln_g.reshape(1, c), ln_b.reshape(1, c), w, bcol, dproj)


CONV_TC = 512


def _conv_taps(x):
    rows = lax.broadcasted_iota(jnp.int32, x.shape, 0)
    taps = [jnp.where(rows >= SSM_CONV - 1 - k, pltpu.roll(x, SSM_CONV - 1 - k, axis=0), 0.0) for k in range(SSM_CONV - 1)]
    return taps + [x]


def _conv_pre(taps, w_ref, b_ref):
    acc = b_ref[...]
    for k in range(SSM_CONV):
        acc = acc + taps[k] * w_ref[k:k + 1, :]
    return acc


def _conv_fwd(proj, w, b, *, bsz, name):
    t = proj.shape[0]
    s = t // bsz
    nj = SSM_CONV_DIM // CONV_TC
    c0 = XBC_COL0 // CONV_TC

    def body(x_ref, w_ref, b_ref, o_ref):
        pre = _conv_pre(_conv_taps(x_ref[...].astype(F32)), w_ref, b_ref)
        o_ref[...] = pre * _sigmoid(pre)

    return pl.pallas_call(
        body, grid=(bsz, nj),
        in_specs=[pl.BlockSpec((s, CONV_TC), lambda bb, j: (bb, c0 + j)), pl.BlockSpec((SSM_CONV, CONV_TC), lambda bb, j: (0, j)),
                  pl.BlockSpec((1, CONV_TC), lambda bb, j: (0, j))],
        out_specs=pl.BlockSpec((s, CONV_TC), lambda bb, j: (bb, j)),
        out_shape=jax.ShapeDtypeStruct((t, SSM_CONV_DIM), F32),
        compiler_params=_params("parallel", "parallel"), name=name)(proj, w, b.reshape(1, -1))


def _conv_bwd(proj, dact, w, b, dproj, *, bsz, name):
    t = proj.shape[0]
    s = t // bsz
    nj = SSM_CONV_DIM // CONV_TC
    c0 = XBC_COL0 // CONV_TC

    def body(x_ref, d_ref, w_ref, b_ref, _, dx_ref, dw_ref, db_ref):
        @pl.when(pl.program_id(1) == 0)
        def _():
            dw_ref[...] = jnp.zeros_like(dw_ref)
            db_ref[...] = jnp.zeros_like(db_ref)

        taps = _conv_taps(x_ref[...].astype(F32))
        pre = _conv_pre(taps, w_ref, b_ref)
        sg = _sigmoid(pre)
        dpre = d_ref[...] * (sg * (1.0 + pre * (1.0 - sg)))
        rows = lax.broadcasted_iota(jnp.int32, dpre.shape, 0)
        db_ref[...] += jnp.sum(dpre, axis=0, keepdims=True)
        dx = dpre * w_ref[SSM_CONV - 1:SSM_CONV, :]
        for k in range(SSM_CONV):
            dw_ref[k:k + 1, :] += jnp.sum(dpre * taps[k], axis=0, keepdims=True)
        for k in range(SSM_CONV - 1):
            sh = SSM_CONV - 1 - k
            dsh = jnp.where(rows < s - sh, pltpu.roll(dpre, s - sh, axis=0), 0.0)
            dx = dx + dsh * w_ref[k:k + 1, :]
        dx_ref[...] = dx.astype(MXU_DTYPE)

    return pl.pallas_call(
        body, grid=(nj, bsz),
        in_specs=[pl.BlockSpec((s, CONV_TC), lambda j, bb: (bb, c0 + j)), pl.BlockSpec((s, CONV_TC), lambda j, bb: (bb, j)),
                  pl.BlockSpec((SSM_CONV, CONV_TC), lambda j, bb: (0, j)), pl.BlockSpec((1, CONV_TC), lambda j, bb: (0, j)), _ANY],
        out_specs=[pl.BlockSpec((s, CONV_TC), lambda j, bb: (bb, c0 + j)), pl.BlockSpec((SSM_CONV, CONV_TC), lambda j, bb: (0, j)),
                   pl.BlockSpec((1, CONV_TC), lambda j, bb: (0, j))],
        out_shape=[jax.ShapeDtypeStruct(dproj.shape, dproj.dtype), jax.ShapeDtypeStruct((SSM_CONV, SSM_CONV_DIM), F32),
                   jax.ShapeDtypeStruct((1, SSM_CONV_DIM), F32)],
        input_output_aliases={4: 0},
        compiler_params=_params("parallel", "arbitrary"), name=name)(proj, dact, w, b.reshape(1, -1), dproj)


def _softplus(x):
    return jnp.maximum(x, 0.0) + jnp.log1p(jnp.exp(-jnp.abs(x)))


def _pad_heads(v):
    return jnp.broadcast_to(jnp.pad(v.astype(F32), (0, HEAD_PAD - SSM_HEADS))[None, :], (SUBLANE, HEAD_PAD))


def _ssd_prep(dt_raw, dt_bias8, a_log8, *, name):
    t = dt_raw.shape[0]
    n = CHUNK

    def body(r_ref, b_ref, al_ref, dt_ref, cs_ref, dtt_ref, cst_ref):
        dt = _softplus(r_ref[...] + b_ref[0:1, :])
        da = dt * (-jnp.exp(al_ref[0:1, :]))
        row = lax.broadcasted_iota(jnp.int32, (n, n), 0)
        col = lax.broadcasted_iota(jnp.int32, (n, n), 1)
        lower = (col <= row).astype(F32)
        upper = (row <= col).astype(F32)
        eye = (row == col).astype(F32)
        dt_ref[...] = dt
        cs_ref[...] = _dot_exact(lower, da, _DN_NN, 1)
        cst_ref[0] = _dot_exact(da, upper, _DN_TN, 0)
        dtt_ref[0] = _dot_exact(dt, eye, _DN_TN, 0)

    hp = HEAD_PAD
    return pl.pallas_call(
        body, grid=(t // n,),
        in_specs=[_row_spec(n, hp), _par_spec((SUBLANE, hp)), _par_spec((SUBLANE, hp))],
        out_specs=[_row_spec(n, hp), _row_spec(n, hp), pl.BlockSpec((1, hp, n), lambda i: (i, 0, 0)),
                   pl.BlockSpec((1, hp, n), lambda i: (i, 0, 0))],
        out_shape=[jax.ShapeDtypeStruct((t, hp), F32), jax.ShapeDtypeStruct((t, hp), F32),
                   jax.ShapeDtypeStruct((t // n, hp, n), F32), jax.ShapeDtypeStruct((t // n, hp, n), F32)],
        compiler_params=_params("parallel"), name=name)(dt_raw, dt_bias8, a_log8)


def _expand_mat():
    h = lax.broadcasted_iota(jnp.int32, (HEAD_PAD, SSM_INNER), 0)
    ch = lax.broadcasted_iota(jnp.int32, (HEAD_PAD, SSM_INNER), 1)
    return (ch // SSM_HEADDIM == h).astype(F32)


def _reduce_mat():
    ch = lax.broadcasted_iota(jnp.int32, (SSM_INNER, HEAD_PAD), 0)
    h = lax.broadcasted_iota(jnp.int32, (SSM_INNER, HEAD_PAD), 1)
    return (ch // SSM_HEADDIM == h).astype(F32)


def _expand(v, em):
    return _dot_exact(v, em, _DN_NN, 0)


def _expand_heads(v):
    return jnp.repeat(v.astype(F32), SSM_HEADDIM)[None, :]


def _decay_mat(cs_ref, cst_ref, h, mask):
    seg = cs_ref[:, h:h + 1] - cst_ref[0, h:h + 1, :]
    return jnp.where(mask, jnp.exp(jnp.minimum(seg, 0.0)), 0.0)


GROUP_CH = SSM_INNER // SSM_GROUPS
PAIRS_PER_GROUP = GROUP_CH // LANE
HEADS_PER_GROUP = SSM_HEADS // SSM_GROUPS
BM_COL0 = SSM_INNER
CM_COL0 = SSM_INNER + SSM_GROUPS * SSM_STATE


def _ssd_specs(nc, rev):
    def cidx(i):
        return (i // nc) * nc + (nc - 1 - i % nc) if rev else i

    n = CHUNK
    xs = pl.BlockSpec((n, SSM_INNER), lambda i: (cidx(i), 0))
    bm = pl.BlockSpec((n, GROUP_CH), lambda i: (cidx(i), BM_COL0 // GROUP_CH))
    cm = pl.BlockSpec((n, GROUP_CH), lambda i: (cidx(i), CM_COL0 // GROUP_CH))
    hv = pl.BlockSpec((n, HEAD_PAD), lambda i: (cidx(i), 0))
    hvt = pl.BlockSpec((1, HEAD_PAD, n), lambda i: (cidx(i), 0, 0))
    st = pl.BlockSpec((1, SSM_INNER, SSM_STATE), lambda i: (cidx(i), 0, 0))
    return xs, bm, cm, hv, hvt, st


def _ssd_fwd(xbc, dt, cs, dtt, cst, dskx, *, nc, name):
    t = xbc.shape[0]
    n = CHUNK
    xs_s, bm_s, cm_s, hv_s, hvt_s, st_s = _ssd_specs(nc, False)

    def body(xs_ref, bm_ref, cm_ref, dt_ref, cs_ref, dtt_ref, cst_ref, dsk_ref, y_ref, st_ref, prev):
        @pl.when(pl.program_id(0) % nc == 0)
        def _():
            prev[...] = jnp.zeros_like(prev)

        st_ref[0] = prev[...]
        em = _expand_mat()
        dtx = _expand(dt_ref[...], em)
        csx = _expand(cs_ref[...], em)
        dskx = dsk_ref[...]
        xs = xs_ref[...]
        xdt = xs * dtx
        ecs = jnp.exp(csx)
        dec = jnp.exp(csx[n - 1:n, :] - csx)
        mask = _causal(n)
        lane = lax.broadcasted_iota(jnp.int32, (n, LANE), 1)
        for g in range(SSM_GROUPS):
            gs = slice(g * SSM_STATE, (g + 1) * SSM_STATE)
            gc = slice(g * GROUP_CH, (g + 1) * GROUP_CH)
            cmat = cm_ref[:, gs].astype(MXU_DTYPE)
            bmat = bm_ref[:, gs].astype(MXU_DTYPE)
            cb = _nt(cmat, bmat)
            yoff = ecs[:, gc] * _nt(cmat, prev[gc, :].astype(MXU_DTYPE))
            for q in range(PAIRS_PER_GROUP):
                hp = g * PAIRS_PER_GROUP + q
                sl = slice(hp * LANE, (hp + 1) * LANE)
                xp = xdt[:, sl].astype(MXU_DTYPE)
                m0 = (cb * _decay_mat(cs_ref, cst_ref, 2 * hp, mask)).astype(MXU_DTYPE)
                m1 = (cb * _decay_mat(cs_ref, cst_ref, 2 * hp + 1, mask)).astype(MXU_DTYPE)
                yd = jnp.where(lane < SSM_HEADDIM, _nn(m0, xp), _nn(m1, xp))
                y_ref[:, sl] = yd + yoff[:, q * LANE:(q + 1) * LANE] + xs[:, sl] * dskx[:, sl]
            snew = _tn((xdt[:, gc] * dec[:, gc]).astype(MXU_DTYPE), bmat)
            for r in range(HEADS_PER_GROUP):
                h = g * HEADS_PER_GROUP + r
                rows = slice(h * SSM_HEADDIM, (h + 1) * SSM_HEADDIM)
                e = jnp.exp(cst_ref[0, h:h + 1, n - 1:n])
                prev[rows, :] = prev[rows, :] * e + snew[r * SSM_HEADDIM:(r + 1) * SSM_HEADDIM, :]

    return pl.pallas_call(
        body, grid=(t // n,),
        in_specs=[xs_s, bm_s, cm_s, hv_s, hv_s, hvt_s, hvt_s, _par_spec((1, SSM_INNER))],
        out_specs=[xs_s, st_s],
        out_shape=[jax.ShapeDtypeStruct((t, SSM_INNER), F32), jax.ShapeDtypeStruct((t // n, SSM_INNER, SSM_STATE), F32)],
        scratch_shapes=[pltpu.VMEM((SSM_INNER, SSM_STATE), F32)],
        compiler_params=_params("arbitrary"), name=name)(xbc, xbc, xbc, dt, cs, dtt, cst, dskx)


def _ssd_bwd(dy, xbc, dt, cs, dtt, cst, st, dskx, a_log8, dt_raw, dt_bias8, *, nc, name):
    t = xbc.shape[0]
    n = CHUNK
    xs_s, bm_s, cm_s, hv_s, hvt_s, st_s = _ssd_specs(nc, True)
    acc_s = _par_spec((1, HEAD_PAD))
    xbc_s = pl.BlockSpec((n, SSM_CONV_DIM), xs_s.index_map)

    def body(dy_ref, xs_ref, bm_ref, cm_ref, dt_ref, cs_ref, dtt_ref, cst_ref, st_ref, dsk_ref, al_ref, raw_ref, bias_ref,
             dxbc_ref, ddr_ref, dal_ref, dds_ref, dbias_ref, dprev, dxdt_s, tdec_s, tcs_s):
        @pl.when(pl.program_id(0) % nc == 0)
        def _():
            dprev[...] = jnp.zeros_like(dprev)

        @pl.when(pl.program_id(0) == 0)
        def _():
            dal_ref[...] = jnp.zeros_like(dal_ref)
            dds_ref[...] = jnp.zeros_like(dds_ref)
            dbias_ref[...] = jnp.zeros_like(dbias_ref)

        em = _expand_mat()
        rm = _reduce_mat()

        def head_reduce(v):
            return _dot_exact(v, rm, _DN_NN, 0)

        dtv = dt_ref[...]
        csv = cs_ref[...]
        dtx = _expand(dtv, em)
        csx = _expand(csv, em)
        dskx = dsk_ref[...]
        xs = xs_ref[...]
        dyv = dy_ref[...]
        xdt = xs * dtx
        ecs = jnp.exp(csx)
        dec = jnp.exp(csx[n - 1:n, :] - csx)
        mask = _causal(n)
        lane = lax.broadcasted_iota(jnp.int32, (n, LANE), 1)
        hlane = lax.broadcasted_iota(jnp.int32, (1, HEAD_PAD), 1)
        hsub = lax.broadcasted_iota(jnp.int32, (HEAD_PAD, 1), 0)
        rsum = jnp.zeros((n, HEAD_PAD), F32)
        csum = jnp.zeros((HEAD_PAD, n), F32)
        for g in range(SSM_GROUPS):
            gs = slice(g * SSM_STATE, (g + 1) * SSM_STATE)
            gc = slice(g * GROUP_CH, (g + 1) * GROUP_CH)
            cmat = cm_ref[:, gs].astype(MXU_DTYPE)
            bmat = bm_ref[:, gs].astype(MXU_DTYPE)
            cb = _nt(cmat, bmat)
            pg = st_ref[0, gc, :].astype(MXU_DTYPE)
            dpg = dprev[gc, :]
            dpgb = dpg.astype(MXU_DTYPE)
            z = _nt(cmat, pg)
            dyg = dyv[:, gc]
            dz = (dyg * ecs[:, gc]).astype(MXU_DTYPE)
            dc = _nn(dz, pg)
            dprev_y = _tn(dz, cmat)
            tcs_s[:, gc] = dyg * z * ecs[:, gc]
            xd = xdt[:, gc] * dec[:, gc]
            wmat = _nt(bmat, dpgb)
            db = _nn(xd.astype(MXU_DTYPE), dpgb)
            tdec_s[:, gc] = wmat * xd
            dxdt_g = wmat * dec[:, gc]
            dcb = jnp.zeros((n, n), F32)
            for q in range(PAIRS_PER_GROUP):
                hp = g * PAIRS_PER_GROUP + q
                sl = slice(hp * LANE, (hp + 1) * LANE)
                xp = xdt[:, sl].astype(MXU_DTYPE)
                dyp = dyv[:, sl]
                dypb = dyp.astype(MXU_DTYPE)
                dxp = None
                for hh in range(2):
                    h = 2 * hp + hh
                    lm = _decay_mat(cs_ref, cst_ref, h, mask)
                    mine = (lane < SSM_HEADDIM) if hh == 0 else (lane >= SSM_HEADDIM)
                    dm = _nt(jnp.where(mine, dyp, 0.0).astype(MXU_DTYPE), xp)
                    dml = dm * lm
                    dcb = dcb + dml
                    gseg = dml * cb
                    rsum = rsum + jnp.sum(gseg, axis=1, keepdims=True) * (hlane == h).astype(F32)
                    csum = csum + (hsub == h).astype(F32) * jnp.sum(gseg, axis=0, keepdims=True)
                    dxh = _tn((cb * lm).astype(MXU_DTYPE), dypb)
                    dxp = dxh if dxp is None else jnp.where(mine, dxh, dxp)
                dxdt_s[:, sl] = dxdt_g[:, q * LANE:(q + 1) * LANE] + dxp
            dcbb = dcb.astype(MXU_DTYPE)
            dxbc_ref[:, CM_COL0 + g * SSM_STATE:CM_COL0 + (g + 1) * SSM_STATE] = dc + _nn(dcbb, bmat)
            dxbc_ref[:, BM_COL0 + g * SSM_STATE:BM_COL0 + (g + 1) * SSM_STATE] = db + _tn(dcbb, cmat)
            for r in range(HEADS_PER_GROUP):
                h = g * HEADS_PER_GROUP + r
                rows = slice(h * SSM_HEADDIM, (h + 1) * SSM_HEADDIM)
                lr = slice(r * SSM_HEADDIM, (r + 1) * SSM_HEADDIM)
                e = jnp.exp(cst_ref[0, h:h + 1, n - 1:n])
                dprev[rows, :] = dpg[lr, :] * e + dprev_y[lr, :]
            tq = _dot_exact(dpg * st_ref[0, gc, :], rm[gc, :], _DN_TN, 0)
            if g == 0:
                qsum = jnp.sum(tq, axis=0, keepdims=True)
            else:
                qsum = qsum + jnp.sum(tq, axis=0, keepdims=True)
        dxdt = dxdt_s[...]
        dxbc_ref[:, 0:SSM_INNER] = dxdt * dtx + dyv * dskx
        ddt = head_reduce(dxdt * xs)
        edec = head_reduce(tdec_s[...])
        ycs = head_reduce(tcs_s[...])
        row = lax.broadcasted_iota(jnp.int32, (n, HEAD_PAD), 0)
        extra = jnp.sum(edec, axis=0, keepdims=True) + qsum * jnp.exp(csv[n - 1:n, :])
        dcs = rsum - csum.T + ycs - edec + jnp.where(row == n - 1, extra, 0.0)
        r2 = lax.broadcasted_iota(jnp.int32, (n, n), 0)
        c2 = lax.broadcasted_iota(jnp.int32, (n, n), 1)
        dda = _dot_exact((c2 >= r2).astype(F32), dcs, _DN_NN, 1)
        a_row = -jnp.exp(al_ref[0:1, :])
        ddt = ddt + dda * a_row
        dal_ref[...] += jnp.sum(dda * dtv, axis=0, keepdims=True) * a_row
        dds_ref[...] += jnp.sum(head_reduce(dyv * xs), axis=0, keepdims=True)
        ddr = ddt * _sigmoid(raw_ref[...] + bias_ref[0:1, :])
        ddr_ref[...] = ddr
        dbias_ref[...] += jnp.sum(ddr, axis=0, keepdims=True)

    par8 = _par_spec((SUBLANE, HEAD_PAD))
    return pl.pallas_call(
        body, grid=(t // n,),
        in_specs=[xs_s, xs_s, bm_s, cm_s, hv_s, hv_s, hvt_s, hvt_s, st_s, _par_spec((1, SSM_INNER)), par8, hv_s, par8],
        out_specs=[xbc_s, hv_s, acc_s, acc_s, acc_s],
        out_shape=[jax.ShapeDtypeStruct((t, SSM_CONV_DIM), F32), jax.ShapeDtypeStruct((t, HEAD_PAD), F32),
                   jax.ShapeDtypeStruct((1, HEAD_PAD), F32), jax.ShapeDtypeStruct((1, HEAD_PAD), F32),
                   jax.ShapeDtypeStruct((1, HEAD_PAD), F32)],
        scratch_shapes=[pltpu.VMEM((SSM_INNER, SSM_STATE), F32), pltpu.VMEM((n, SSM_INNER), F32),
                        pltpu.VMEM((n, SSM_INNER), F32), pltpu.VMEM((n, SSM_INNER), F32)],
        compiler_params=_params("arbitrary"), name=name)(dy, xbc, xbc, xbc, dt, cs, dtt, cst, st, dskx, a_log8, dt_raw, dt_bias8)


def _gate_norm_fwd(y, proj, norm_g, *, name):
    t, c = y.shape
    tm = _pick(t, (256, 128))

    def body(y_ref, z_ref, g_ref, o_ref):
        z = z_ref[...].astype(F32)
        yz = y_ref[...] * z * _sigmoid(z)
        for g in range(SSM_GROUPS):
            gc = slice(g * GROUP_CH, (g + 1) * GROUP_CH)
            seg = yz[:, gc]
            r = lax.rsqrt(jnp.mean(seg * seg, axis=-1, keepdims=True) + RMS_EPS)
            o_ref[:, gc] = (seg * r * g_ref[:, gc]).astype(MXU_DTYPE)

    return pl.pallas_call(
        body, grid=(t // tm,), in_specs=[_row_spec(tm, c), _row_spec(tm, c, 1), _par_spec((1, c))],
        out_specs=_row_spec(tm, c), out_shape=jax.ShapeDtypeStruct((t, c), MXU_DTYPE),
        compiler_params=_params("parallel"), name=name)(y, proj, norm_g.reshape(1, c))


def _gate_norm_bwd(dyb, y, proj, norm_g, dproj, *, name):
    t, c = y.shape
    tm = _pick(t, (256, 128))

    def body(d_ref, y_ref, z_ref, g_ref, _, dy_ref, dz_ref, dg_ref):
        @pl.when(pl.program_id(0) == 0)
        def _():
            dg_ref[...] = jnp.zeros_like(dg_ref)

        z = z_ref[...].astype(F32)
        yv = y_ref[...]
        sz = _sigmoid(z)
        silu = z * sz
        yz = yv * silu
        dv = d_ref[...].astype(F32)
        for g in range(SSM_GROUPS):
            gc = slice(g * GROUP_CH, (g + 1) * GROUP_CH)
            seg = yz[:, gc]
            r = lax.rsqrt(jnp.mean(seg * seg, axis=-1, keepdims=True) + RMS_EPS)
            nrm = seg * r
            dn = dv[:, gc] * g_ref[:, gc]
            dg_ref[:, gc] += jnp.sum(dv[:, gc] * nrm, axis=0, keepdims=True)
            dyz = r * (dn - nrm * jnp.mean(dn * nrm, axis=-1, keepdims=True))
            dy_ref[:, gc] = dyz * silu[:, gc]
            dz_ref[:, gc] = (dyz * yv[:, gc] * (sz[:, gc] * (1.0 + z[:, gc] * (1.0 - sz[:, gc])))).astype(MXU_DTYPE)

    return pl.pallas_call(
        body, grid=(t // tm,), in_specs=[_row_spec(tm, c), _row_spec(tm, c), _row_spec(tm, c, 1), _par_spec((1, c)), _ANY],
        out_specs=[_row_spec(tm, c), _row_spec(tm, c, 1), _par_spec((1, c))],
        out_shape=[jax.ShapeDtypeStruct((t, c), F32), jax.ShapeDtypeStruct(dproj.shape, dproj.dtype),
                   jax.ShapeDtypeStruct((1, c), F32)],
        input_output_aliases={4: 1},
        compiler_params=_params("arbitrary"), name=name)(dyb, y, proj, norm_g.reshape(1, c), dproj)


GA_COLBLK = GAB_COL0 // D_MODEL


def _merge_fwd(br_a, br_b, proj, *, name):
    t, c = br_a.shape
    tm = _pick(t, ROW_TILES)

    def body(a_ref, b_ref, ga_ref, gb_ref, o_ref):
        o_ref[...] = (_sigmoid(ga_ref[...].astype(F32)) * a_ref[...].astype(F32)
                      + _sigmoid(gb_ref[...].astype(F32)) * b_ref[...].astype(F32)).astype(MXU_DTYPE)

    return pl.pallas_call(
        body, grid=(t // tm,),
        in_specs=[_row_spec(tm, c), _row_spec(tm, c), _row_spec(tm, c, GA_COLBLK), _row_spec(tm, c, GA_COLBLK + 1)],
        out_specs=_row_spec(tm, c), out_shape=jax.ShapeDtypeStruct((t, c), MXU_DTYPE),
        compiler_params=_params("parallel"), name=name)(br_a, br_b, proj, proj)


def _merge_bwd(dm, br_a, br_b, proj, *, name):
    t, c = br_a.shape
    tm = _pick(t, ROW_TILES)

    def body(dm_ref, a_ref, b_ref, ga_ref, gb_ref, da_ref, db_ref, dg_ref):
        d = dm_ref[...].astype(F32)
        sa = _sigmoid(ga_ref[...].astype(F32))
        sb = _sigmoid(gb_ref[...].astype(F32))
        da_ref[...] = (d * sa).astype(MXU_DTYPE)
        db_ref[...] = (d * sb).astype(MXU_DTYPE)
        dg_ref[:, :c] = (d * a_ref[...].astype(F32) * sa * (1.0 - sa)).astype(MXU_DTYPE)
        dg_ref[:, c:] = (d * b_ref[...].astype(F32) * sb * (1.0 - sb)).astype(MXU_DTYPE)

    return pl.pallas_call(
        body, grid=(t // tm,),
        in_specs=[_row_spec(tm, c), _row_spec(tm, c), _row_spec(tm, c), _row_spec(tm, c, GA_COLBLK), _row_spec(tm, c, GA_COLBLK + 1)],
        out_specs=[_row_spec(tm, c), _row_spec(tm, c), _row_spec(tm, 2 * c, GAB_COL0 // (2 * c))],
        out_shape=[jax.ShapeDtypeStruct((t, c), MXU_DTYPE), jax.ShapeDtypeStruct((t, c), MXU_DTYPE),
                   jax.ShapeDtypeStruct((t, MAIN_COLS), MXU_DTYPE)],
        compiler_params=_params("parallel"), name=name)(dm, br_a, br_b, proj, proj)


def _layer_fwd(x, xb, memn_b, w, *, bsz, tag):
    nc = x.shape[0] // bsz // CHUNK
    sv = {"x_in": xb}
    proj = _mm(xb, w["w_main"], out_dtype=STASH_DTYPE, name=f"{tag}_proj")
    dt_raw = _mm(xb, w["w_dt"], name=f"{tag}_dtproj")
    sgo = _sg_fwd(proj, w["sg_ln_g"], w["sg_ln_b"], w["sg_w"], w["sg_bcol"], name=f"{tag}_sg_fwd")
    xbc = _conv_fwd(proj, w["conv_w"], w["conv_b"], bsz=bsz, name=f"{tag}_conv_fwd")
    dt, cs, dtt, cst = _ssd_prep(dt_raw, w["dt_bias8"], w["a_log8"], name=f"{tag}_ssd_prep")
    y, st = _ssd_fwd(xbc, dt, cs, dtt, cst, w["d_skipx"], nc=nc, name=f"{tag}_ssd_fwd")
    yb = _gate_norm_fwd(y, proj, w["ssm_norm_g"], name=f"{tag}_gate_norm_fwd")
    if "rest" in w:
        w = w["rest"](w, yb)
    br_a = _mm(sgo, w["p_a"], out_dtype=STASH_DTYPE, name=f"{tag}_br_a")
    br_b = _mm(yb, w["p_b"], out_dtype=STASH_DTYPE, name=f"{tag}_br_b")
    merged = _merge_fwd(br_a, br_b, proj, name=f"{tag}_merge_fwd")
    mix = _mm(merged, w["w_mix_o"], name=f"{tag}_mix_o")
    x1, x1b, xh1, rs1 = _ln_fwd(x, mix, w["ln_g"][0], w["ln_b"][0], name=f"{tag}_ln1_fwd")
    sv.update(proj=proj, dt_raw=dt_raw, sgo=sgo, xbc=xbc, dt=dt, cs=cs, dtt=dtt, cst=cst, y=y, st=st, yb=yb,
              br_a=br_a, br_b=br_b, merged=merged, xh1=xh1, rs1=rs1, x1b=x1b)
    q = _mm(x1b, w["w_xq"], out_dtype=MXU_DTYPE, name=f"{tag}_q")
    kv = _mm(memn_b, w["w_xkv"], out_dtype=MXU_DTYPE, name=f"{tag}_kv")
    o = _attn_fwd(q, kv, bsz=bsz, name=f"{tag}_attn_fwd")
    att = _mm(o, w["w_xo"], name=f"{tag}_xo")
    x2, x2b, xh2, rs2 = _ln_fwd(x1, att, w["ln_g"][1], w["ln_b"][1], name=f"{tag}_ln2_fwd")
    sv.update(q=q, kv=kv, o=o, xh2=xh2, rs2=rs2, x2b=x2b)
    h = _mm(x2b, w["w_ffn_in"], out_dtype=STASH_DTYPE, name=f"{tag}_ffn_in")
    a = _swiglu_fwd(h, name=f"{tag}_swiglu_fwd")
    ffn = _mm(a, w["w_ffn_out"], name=f"{tag}_ffn_out")
    x3, x3b, xh3, rs3 = _ln_fwd(x2, ffn, w["ln_g"][2], w["ln_b"][2], name=f"{tag}_ln3_fwd")
    sv.update(h=h, a=a, xh3=xh3, rs3=rs3)
    return x3, x3b, sv, w


GRAD_GROUPS = (("w_ffn_out", "w_ffn_in", "w_xo", "w_xq", "w_xkv"), ("w_mix_o", "p_a", "p_b"), ("w_in",))


def _layer_bwd(dx3_addends, dx3_scales, memn_b, w, sv, on_group=None, *, bsz, tag):
    nc = sv["xh1"].shape[0] // bsz // CHUNK
    gr = {}

    def group_done(k):
        return on_group(GRAD_GROUPS[k], gr) if on_group is not None else None
    dp3, dp3b, dg3, db3 = _ln_bwd(dx3_addends, dx3_scales, sv["xh3"], sv["rs3"], w["ln_g"][2], name=f"{tag}_ln3_bwd")
    da = _mm(dp3b, w["w_ffn_out"], tb=True, out_dtype=STASH_DTYPE, name=f"{tag}_d_a")
    gr["w_ffn_out"] = _mm(sv["a"], dp3b, ta=True, name=f"{tag}_dw_ffn_out")
    dh = _swiglu_bwd(sv["h"], da, name=f"{tag}_swiglu_bwd")
    gr["w_ffn_in"] = _mm(sv["x2b"], dh, ta=True, name=f"{tag}_dw_ffn_in")
    dx2_br = _mm(dh, w["w_ffn_in"], tb=True, name=f"{tag}_dx2")
    dp2, dp2b, dg2, db2 = _ln_bwd([dp3, dx2_br], [ALPHA, 1.0], sv["xh2"], sv["rs2"], w["ln_g"][1], name=f"{tag}_ln2_bwd")
    do = _mm(dp2b, w["w_xo"], tb=True, out_dtype=MXU_DTYPE, name=f"{tag}_d_o")
    gr["w_xo"] = _mm(sv["o"], dp2b, ta=True, name=f"{tag}_dw_xo")
    dq, dk, dv = _attn_bwd(sv["q"], sv["kv"], do, bsz=bsz, name=f"{tag}_attn_bwd")
    dkv = jnp.concatenate([dk, dv], axis=1)
    gr["w_xq"] = _mm(sv["x1b"], dq, ta=True, name=f"{tag}_dw_xq")
    gr["w_xkv"] = _mm(memn_b, dkv, ta=True, name=f"{tag}_dw_xkv")
    dmemn = _mm(dkv, w["w_xkv"], tb=True, name=f"{tag}_d_memn")
    dx1_br = _mm(dq, w["w_xq"], tb=True, name=f"{tag}_dx1")
    token = group_done(0)
    ln_g1 = w["ln_g"][0] if token is None else w["ln_g"][0] + token[0, 0]
    dp1, dp1b, dg1, db1 = _ln_bwd([dp2, dx1_br], [ALPHA, 1.0], sv["xh1"], sv["rs1"], ln_g1, name=f"{tag}_ln1_bwd")
    gr["ln_g"] = jnp.concatenate([dg1, dg2, dg3], axis=0)
    gr["ln_b"] = jnp.concatenate([db1, db2, db3], axis=0)
    dmerged = _mm(dp1b, w["w_mix_o"], tb=True, out_dtype=STASH_DTYPE, name=f"{tag}_d_merged")
    gr["w_mix_o"] = _mm(sv["merged"], dp1b, ta=True, name=f"{tag}_dw_mix_o")
    dbr_a, dbr_b, dproj = _merge_bwd(dmerged, sv["br_a"], sv["br_b"], sv["proj"], name=f"{tag}_merge_bwd")
    gr["p_a"] = _mm(sv["sgo"], dbr_a, ta=True, name=f"{tag}_dw_p_a")
    gr["p_b"] = _mm(sv["yb"], dbr_b, ta=True, name=f"{tag}_dw_p_b")
    dsgo = _mm(dbr_a, w["p_a"], tb=True, out_dtype=STASH_DTYPE, name=f"{tag}_d_sgo")
    dyb = _mm(dbr_b, w["p_b"], tb=True, out_dtype=STASH_DTYPE, name=f"{tag}_d_yb")
    token = group_done(1)
    norm_g = w["ssm_norm_g"] if token is None else w["ssm_norm_g"] + token[0, 0]
    dy, dproj, gr["ssm_norm_g"] = _gate_norm_bwd(dyb, sv["y"], sv["proj"], norm_g, dproj, name=f"{tag}_gate_norm_bwd")
    dxbc, ddr, gr["a_log"], gr["d_skip"], gr["dt_bias"] = _ssd_bwd(
        dy, sv["xbc"], sv["dt"], sv["cs"], sv["dtt"], sv["cst"], sv["st"], w["d_skipx"], w["a_log8"], sv["dt_raw"],
        w["dt_bias8"], nc=nc, name=f"{tag}_ssd_bwd")
    dproj, gr["conv_w"], gr["conv_b"] = _conv_bwd(sv["proj"], dxbc, w["conv_w"], w["conv_b"], dproj, bsz=bsz, name=f"{tag}_conv_bwd")
    dproj, gr["sg_w"], dsg_bcol, gr["sg_ln_g"], gr["sg_ln_b"] = _sg_bwd(
        sv["proj"], dsgo, w["sg_ln_g"], w["sg_ln_b"], w["sg_w"], w["sg_bcol"], dproj, name=f"{tag}_sg_bwd")
    gr["sg_b"] = dsg_bcol[..., 0]
    gr["w_main"] = _mm(sv["x_in"], dproj, ta=True, name=f"{tag}_dw_main")
    gr["w_dt"] = _mm(sv["x_in"], ddr, ta=True, name=f"{tag}_dw_dt")
    token = group_done(2)
    dx_dt = _mm(ddr, w["w_dt"], tb=True, after=token, name=f"{tag}_dx_dt")
    dx_main = _mm(dproj, w["w_main"], tb=True, after=token, name=f"{tag}_dx_main")
    return [dp1, dx_main, dx_dt], [ALPHA, 1.0, 1.0], gr, dmemn


def _local_step(x, mem, tgt, mem_ln_g, mem_ln_b, layers, on_layer_grads=None):
    bsz, s, d = x.shape
    xf = x.reshape(bsz * s, d)
    memf = mem.reshape(-1, d)
    _, memn_b, mxh, mrs = _ln_fwd(memf, None, mem_ln_g, mem_ln_b, name="mem_ln_fwd")
    cur, curb, saved, weights = xf, xf, [], []
    for li, get_weights in enumerate(layers):
        cur, curb, sv, w = _layer_fwd(cur, curb, memn_b, get_weights(cur), bsz=bsz, tag=f"l{li}")
        saved.append(sv)
        weights.append(w)
    dy, lsum = _loss_head(cur, tgt.reshape(bsz * s, d), name="loss_head")
    addends, scales = [dy], [1.0]
    grads, dmem = [None] * len(layers), []
    for li in reversed(range(len(layers))):
        on_group = None if on_layer_grads is None else functools.partial(on_layer_grads, li)
        addends, scales, grads[li], dm = _layer_bwd(addends, scales, memn_b, weights[li], saved[li], on_group, bsz=bsz, tag=f"l{li}")
        dmem.append(dm)
    grad_x = _add_scaled(addends, scales, name="grad_x").reshape(bsz, s, d)
    _, _, dmg, dmb = _ln_bwd(dmem, [1.0] * len(dmem), mxh, mrs, mem_ln_g, name="mem_ln_bwd")
    return lsum, grad_x, grads, dmg[0], dmb[0]


_ANY = pl.BlockSpec(memory_space=pl.ANY)
_MESH = pl.DeviceIdType.MESH


def _all_gather8(x, *, name):
    def body(x_ref, out_ref, send_sems, recv_sems):
        mx, my, mc = lax.axis_index("x"), lax.axis_index("y"), lax.axis_index("c")
        me, sibling = (mx, my, mc), (mx, my, 1 - mc)
        chips = [(1 - mx, my), (mx, 1 - my), (1 - mx, 1 - my)]

        def blk(px, py, pc):
            return out_ref.at[4 * px + 2 * py + pc]

        def copy(k, block, to, src=None):
            return pltpu.make_async_remote_copy(
                src_ref=blk(*block) if src is None else src, dst_ref=blk(*block), send_sem=send_sems.at[k],
                recv_sem=recv_sems.at[k], device_id=to, device_id_type=_MESH)

        first = [copy(0, me, sibling, src=x_ref)]
        first += [copy(1 + j, me, (*chip, mc), src=x_ref) for j, chip in enumerate(chips)]
        for cp in first:
            cp.start()
        passed = [copy(4 + j, (*chip, mc), sibling) for j, chip in enumerate(chips)]
        for j, chip in enumerate(chips):
            copy(1 + j, (*chip, mc), me).wait_recv()
            passed[j].start()
        copy(0, sibling, me).wait_recv()
        for j, chip in enumerate(chips):
            copy(4 + j, (*chip, 1 - mc), me).wait_recv()
        for cp in first + passed:
            cp.wait_send()

    return pl.pallas_call(
        body, out_shape=jax.ShapeDtypeStruct((N_DEV,) + x.shape, x.dtype), in_specs=[_ANY], out_specs=_ANY,
        scratch_shapes=[pltpu.SemaphoreType.DMA((7,)), pltpu.SemaphoreType.DMA((7,))], name=name)(x)


def _row_tile(rows, row_bytes, mult=SUBLANE):
    best = None
    for tr in range(mult, rows + 1, mult):
        if rows % tr == 0 and (best is None or tr * row_bytes <= BLOCK_BYTES):
            best = tr
    return rows if best is None else best


def _gather_shape(r, c, kind):
    return {"row": (2, N_CHIPS * r, c), "col": (2, r, N_CHIPS * c), "chip": (2, N_CHIPS, r, c)}[kind]


def _cast_place(shard, kind, dtype, chip_idx, *, name):
    _, r, c = shard.shape
    tr = _row_tile(r, c * 4, 16)
    nt = r // tr

    def body(_, s_ref, o_ref):
        o_ref[...] = s_ref[...].astype(dtype)

    if kind == "row":
        out_spec = pl.BlockSpec((None, tr, c), lambda l, i, j_ref: (l, j_ref[0] * nt + i, 0))
    elif kind == "col":
        out_spec = pl.BlockSpec((None, tr, c), lambda l, i, j_ref: (l, i, j_ref[0]))
    else:
        out_spec = pl.BlockSpec((None, None, tr, c), lambda l, i, j_ref: (l, j_ref[0], i, 0))
    grid_spec = pltpu.PrefetchScalarGridSpec(
        num_scalar_prefetch=1, grid=(2, nt), in_specs=[pl.BlockSpec((None, tr, c), lambda l, i, j_ref: (l, i, 0))],
        out_specs=out_spec)
    return pl.pallas_call(body, grid_spec=grid_spec, out_shape=jax.ShapeDtypeStruct(_gather_shape(r, c, kind), dtype),
                          compiler_params=_params("parallel", "parallel"), name=name)(chip_idx, shard)


def _gather_params(bufs, shard_shapes, kinds, *, name):
    n = len(bufs)

    def body(*refs):
        outs = refs[n:2 * n]
        send_sems, recv_sems = refs[2 * n:]
        mx, my, mc = lax.axis_index("x"), lax.axis_index("y"), lax.axis_index("c")
        me, sibling = (mx, my, mc), (mx, my, 1 - mc)
        chips = [(1 - mx, my), (mx, 1 - my), (1 - mx, 1 - my)]

        def blk(i, px, py, pc):
            r, c = shard_shapes[i]
            j = 2 * px + py
            if kinds[i] == "row":
                return outs[i].at[pc, pl.ds(pl.multiple_of(j * r, r), r)]
            if kinds[i] == "col":
                return outs[i].at[pc, :, pl.ds(pl.multiple_of(j * c, c), c)]
            return outs[i].at[pc, j]

        def copy(i, k, block, to):
            return pltpu.make_async_remote_copy(
                src_ref=blk(i, *block), dst_ref=blk(i, *block), send_sem=send_sems.at[6 * i + k],
                recv_sem=recv_sems.at[6 * i + k], device_id=to, device_id_type=_MESH)

        sent = []
        for i in range(n):
            for j, chip in enumerate(chips):
                cp = copy(i, j, me, (*chip, mc))
                cp.start()
                sent.append(cp)
        for j, chip in enumerate(chips):
            for i in range(n):
                copy(i, j, (*chip, mc), me).wait_recv()
                fwd = copy(i, 3 + j, (*chip, mc), sibling)
                fwd.start()
                sent.append(fwd)
        for i in range(n):
            for j, chip in enumerate(chips):
                copy(i, 3 + j, (*chip, 1 - mc), me).wait_recv()
        for cp in sent:
            cp.wait_send()

    return pl.pallas_call(
        body, out_shape=[jax.ShapeDtypeStruct(b.shape, b.dtype) for b in bufs], in_specs=[_ANY] * n, out_specs=[_ANY] * n,
        input_output_aliases={i: i for i in range(n)},
        scratch_shapes=[pltpu.SemaphoreType.DMA((6 * n,)), pltpu.SemaphoreType.DMA((6 * n,))], name=name)(*bufs)


def _half(r, h):
    return pl.ds(pl.multiple_of(h * (r // 2), r // 2), r // 2)


_HBM = pl.BlockSpec(memory_space=pltpu.HBM)
_SEM = pl.BlockSpec(memory_space=pltpu.SEMAPHORE)
_EFFECT = pltpu.SideEffectType.DATAFLOW_SIDE_EFFECTING


def _sibling_copies(g_refs, land_refs, gs, views, send_sems, recv_sems):
    mx, my, mc = lax.axis_index("x"), lax.axis_index("y"), lax.axis_index("c")
    copies = []
    for i in range(len(gs)):
        if views[i] == "chip":
            src = g_refs[i].at[:, _half(gs[i].shape[1], 1 - mc)]
        else:
            src = g_refs[i].at[_half(gs[i].shape[0], 1 - mc)]
        copies.append(pltpu.make_async_remote_copy(src_ref=src, dst_ref=land_refs[i], send_sem=send_sems.at[i], recv_sem=recv_sems.at[i],
                                                   device_id=(mx, my, 1 - mc), device_id_type=_MESH))
    return copies


def _half_shape(g, view):
    return (g.shape[0], g.shape[1] // 2, g.shape[2]) if view == "chip" else (g.shape[0] // 2, g.shape[1])


def _grads_to_sibling_start(gs, views, *, name):
    n = len(gs)
    lands = [pltpu.with_memory_space_constraint(lax.empty(_half_shape(g, v), g.dtype), pltpu.HBM) for g, v in zip(gs, views)]

    def body(*refs):
        for cp in _sibling_copies(refs[:n], refs[n:2 * n], gs, views, refs[2 * n], refs[2 * n + 1]):
            cp.start()
        refs[-1][...] = jnp.zeros_like(refs[-1])

    outs = pl.pallas_call(
        body, name=name,
        out_shape=(pltpu.SemaphoreType.DMA((n,)), pltpu.SemaphoreType.DMA((n,)),
                   *[pltpu.HBM(x.shape, x.dtype) for x in list(gs) + lands], jax.ShapeDtypeStruct((SUBLANE, LANE), F32)),
        in_specs=[_HBM] * (2 * n), out_specs=(_SEM, _SEM, *[_HBM] * (2 * n), pl.BlockSpec(memory_space=pltpu.VMEM)),
        input_output_aliases={i: 2 + i for i in range(2 * n)},
        compiler_params=pltpu.CompilerParams(has_side_effects=_EFFECT),
    )(*[pltpu.with_memory_space_constraint(g, pltpu.HBM) for g in gs], *lands)
    return outs[0], outs[1], list(outs[2:2 + n]), list(outs[2 + n:2 + 2 * n]), outs[-1]


def _grads_to_sibling_wait(send_sems, recv_sems, gs, lands, views, after, *, name):
    n = len(gs)

    def body(*refs):
        for cp in _sibling_copies(refs[:n], refs[n:2 * n], gs, views, refs[2 * n], refs[2 * n + 1]):
            cp.wait_send()
            cp.wait_recv()

    outs = pl.pallas_call(
        body, name=name, out_shape=tuple(pltpu.HBM(x.shape, x.dtype) for x in list(gs) + list(lands)),
        in_specs=[_HBM] * (2 * n) + [_SEM, _SEM, _ANY], out_specs=tuple([_HBM] * (2 * n)),
        input_output_aliases={i: i for i in range(2 * n)},
        compiler_params=pltpu.CompilerParams(has_side_effects=_EFFECT),
    )(*gs, *lands, send_sems, recv_sems, after)
    return list(outs[:n]), list(outs[n:])


def _cast_place_layer(shard, l, kind, chip_idx, after, *, name):
    _, r, c = shard.shape
    tr = _row_tile(r, c * 4, 16)
    nt = r // tr

    def body(_, s_ref, *rest):
        rest[-1][...] = s_ref[...].astype(MXU_DTYPE)

    if kind == "row":
        out_spec = pl.BlockSpec((tr, c), lambda i, j_ref: (j_ref[0] * nt + i, 0))
    elif kind == "col":
        out_spec = pl.BlockSpec((tr, c), lambda i, j_ref: (i, j_ref[0]))
    else:
        out_spec = pl.BlockSpec((None, tr, c), lambda i, j_ref: (j_ref[0], i, 0))
    extra = [] if after is None else [after]
    grid_spec = pltpu.PrefetchScalarGridSpec(
        num_scalar_prefetch=1, grid=(nt,), in_specs=[pl.BlockSpec((None, tr, c), lambda i, j_ref: (l, i, 0))] + [_ANY] * len(extra),
        out_specs=out_spec)
    return pl.pallas_call(body, grid_spec=grid_spec, out_shape=jax.ShapeDtypeStruct(_gather_shape(r, c, kind)[1:], MXU_DTYPE),
                          compiler_params=_params("parallel"), name=name)(chip_idx, shard, *extra)


def _half_block(ref, kind, r, c, j, h):
    rows = _half(r, h)
    if kind == "row":
        return ref.at[pl.ds(pl.multiple_of(j * r + h * (r // 2), r // 2), r // 2)]
    if kind == "col":
        return ref.at[rows, pl.ds(pl.multiple_of(j * c, c), c)]
    return ref.at[j, rows]


def _gather_ici_copies(buf_refs, shapes, kinds, send_sems, recv_sems):
    mx, my, mc = lax.axis_index("x"), lax.axis_index("y"), lax.axis_index("c")
    chips = [(1 - mx, my), (mx, 1 - my), (1 - mx, 1 - my)]
    copies = []
    for i, (r, c) in enumerate(shapes):
        mine = _half_block(buf_refs[i], kinds[i], r, c, 2 * mx + my, mc)
        for k, (px, py) in enumerate(chips):
            copies.append(pltpu.make_async_remote_copy(
                src_ref=mine, dst_ref=mine, send_sem=send_sems.at[3 * i + k], recv_sem=recv_sems.at[3 * i + k],
                device_id=(px, py, mc), device_id_type=_MESH))
    return copies


def _gather_start(bufs, shapes, kinds, *, name):
    n = len(bufs)

    def body(*refs):
        send_sems, recv_sems, token = refs[n], refs[n + 1], refs[-1]
        for cp in _gather_ici_copies(refs[:n], shapes, kinds, send_sems, recv_sems):
            cp.start()
        token[...] = jnp.zeros_like(token)

    outs = pl.pallas_call(
        body, name=name,
        out_shape=(pltpu.SemaphoreType.DMA((3 * n,)), pltpu.SemaphoreType.DMA((3 * n,)),
                   *[pltpu.HBM(b.shape, b.dtype) for b in bufs], jax.ShapeDtypeStruct((SUBLANE, LANE), F32)),
        in_specs=[_HBM] * n, out_specs=(_SEM, _SEM, *[_HBM] * n, pl.BlockSpec(memory_space=pltpu.VMEM)),
        input_output_aliases={i: 2 + i for i in range(n)},
        compiler_params=pltpu.CompilerParams(has_side_effects=_EFFECT),
    )(*[pltpu.with_memory_space_constraint(b, pltpu.HBM) for b in bufs])
    return outs[0], outs[1], list(outs[2:2 + n]), outs[-1]


def _gather_wait(send_sems, recv_sems, bufs, shapes, kinds, after, *, name):
    n = len(bufs)

    def body(*refs):
        for cp in _gather_ici_copies(refs[:n], shapes, kinds, refs[n], refs[n + 1]):
            cp.wait_send()
            cp.wait_recv()

    outs = pl.pallas_call(
        body, name=name, out_shape=tuple(pltpu.HBM(b.shape, b.dtype) for b in bufs),
        in_specs=[_HBM] * n + [_SEM, _SEM, _ANY], out_specs=tuple([_HBM] * n), input_output_aliases={i: i for i in range(n)},
        compiler_params=pltpu.CompilerParams(has_side_effects=_EFFECT),
    )(*bufs, send_sems, recv_sems, after)
    return list(outs)


def _gather_forward(bufs, shapes, kinds, *, name):
    n = len(bufs)

    def body(*refs):
        outs = refs[n:2 * n]
        send_sems, recv_sems = refs[2 * n:]
        mx, my, mc = lax.axis_index("x"), lax.axis_index("y"), lax.axis_index("c")
        chips = [(1 - mx, my), (mx, 1 - my), (1 - mx, 1 - my)]
        copies = []
        for i, (r, c) in enumerate(shapes):
            for k, (px, py) in enumerate(chips):
                got = _half_block(outs[i], kinds[i], r, c, 2 * px + py, mc)
                cp = pltpu.make_async_remote_copy(src_ref=got, dst_ref=got, send_sem=send_sems.at[3 * i + k],
                                                  recv_sem=recv_sems.at[3 * i + k], device_id=(mx, my, 1 - mc), device_id_type=_MESH)
                cp.start()
                copies.append(cp)
        for cp in copies:
            cp.wait()

    return pl.pallas_call(
        body, out_shape=[jax.ShapeDtypeStruct(b.shape, b.dtype) for b in bufs], in_specs=[_ANY] * n, out_specs=[_ANY] * n,
        input_output_aliases={i: i for i in range(n)},
        scratch_shapes=[pltpu.SemaphoreType.DMA((3 * n,)), pltpu.SemaphoreType.DMA((3 * n,))], name=name)(*bufs)


def _chip_exchange_copies(pair_refs, land_refs, pairs, views, send_sems, recv_sems):
    mx, my, mc = lax.axis_index("x"), lax.axis_index("y"), lax.axis_index("c")
    me = 2 * mx + my
    chips = [(1 - mx, my), (mx, 1 - my), (1 - mx, 1 - my)]
    copies = []
    for i in range(len(pairs)):
        for k, (px, py) in enumerate(chips):
            j = 2 * px + py
            if views[i] == "chip":
                src = pair_refs[i].at[j]
            else:
                c = pairs[i].shape[1] // N_CHIPS
                src = pair_refs[i].at[:, pl.ds(pl.multiple_of(j * c, c), c)]
            copies.append(pltpu.make_async_remote_copy(
                src_ref=src, dst_ref=land_refs[i].at[me], send_sem=send_sems.at[3 * i + k], recv_sem=recv_sems.at[3 * i + k],
                device_id=(px, py, mc), device_id_type=_MESH))
    return copies


def _quad_shape(p, view):
    return p.shape if view == "chip" else (N_CHIPS, p.shape[0], p.shape[1] // N_CHIPS)


def _grads_to_chips_start(pairs, views, *, name):
    n = len(pairs)
    lands = [pltpu.with_memory_space_constraint(lax.empty(_quad_shape(p, v), p.dtype), pltpu.HBM) for p, v in zip(pairs, views)]

    def body(*refs):
        pair_refs, land_refs = refs[:n], refs[n:2 * n]
        send_sems, recv_sems = refs[2 * n], refs[2 * n + 1]
        token = refs[-1]
        for cp in _chip_exchange_copies(pair_refs, land_refs, pairs, views, send_sems, recv_sems):
            cp.start()
        token[...] = jnp.zeros_like(token)

    outs = pl.pallas_call(
        body, name=name,
        out_shape=(pltpu.SemaphoreType.DMA((3 * n,)), pltpu.SemaphoreType.DMA((3 * n,)),
                   *[pltpu.HBM(p.shape, p.dtype) for p in pairs], *[pltpu.HBM(l.shape, l.dtype) for l in lands],
                   jax.ShapeDtypeStruct((SUBLANE, LANE), F32)),
        in_specs=[_HBM] * (2 * n), out_specs=(_SEM, _SEM, *[_HBM] * (2 * n), pl.BlockSpec(memory_space=pltpu.VMEM)),
        input_output_aliases={i: 2 + i for i in range(2 * n)},
        compiler_params=pltpu.CompilerParams(has_side_effects=_EFFECT),
    )(*[pltpu.with_memory_space_constraint(p, pltpu.HBM) for p in pairs], *lands)
    return outs[0], outs[1], list(outs[2:2 + n]), list(outs[2 + n:2 + 2 * n]), outs[-1]


def _grads_to_chips_wait(send_sems, recv_sems, pairs, lands, views, after, *, name):
    n = len(pairs)

    def body(*refs):
        pair_refs, land_refs = refs[:n], refs[n:2 * n]
        s_sems, r_sems = refs[2 * n], refs[2 * n + 1]
        for cp in _chip_exchange_copies(pair_refs, land_refs, pairs, views, s_sems, r_sems):
            cp.wait_send()
            cp.wait_recv()

    outs = pl.pallas_call(
        body, name=name, out_shape=tuple(pltpu.HBM(x.shape, x.dtype) for x in list(pairs) + list(lands)),
        in_specs=[_HBM] * (2 * n) + [_SEM, _SEM, _ANY], out_specs=tuple([_HBM] * (2 * n)),
        input_output_aliases={i: i for i in range(2 * n)},
        compiler_params=pltpu.CompilerParams(has_side_effects=_EFFECT),
    )(*pairs, *lands, send_sems, recv_sems, after)
    return list(outs[n:])


def _grads_share(tots, *, name):
    n = len(tots)

    def body(*refs):
        ins, outs = refs[:n], refs[n:2 * n]
        send_sems, recv_sems = refs[2 * n:]
        mx, my, mc = lax.axis_index("x"), lax.axis_index("y"), lax.axis_index("c")
        copies = []
        for i in range(n):
            cp = pltpu.make_async_remote_copy(src_ref=ins[i], dst_ref=outs[i], send_sem=send_sems.at[i], recv_sem=recv_sems.at[i],
                                              device_id=(mx, my, 1 - mc), device_id_type=_MESH)
            cp.start()
            copies.append(cp)
        for cp in copies:
            cp.wait()

    return pl.pallas_call(
        body, out_shape=[jax.ShapeDtypeStruct(t.shape, t.dtype) for t in tots], in_specs=[_ANY] * n, out_specs=[_ANY] * n,
        scratch_shapes=[pltpu.SemaphoreType.DMA((n,)), pltpu.SemaphoreType.DMA((n,))], name=name)(*tots)


def _pair_sum(g, recv, view, c_idx, *, name):
    def body(c_ref, a_ref, b_ref, o_ref):
        o_ref[...] = (a_ref[...] + b_ref[...]).astype(WIRE_DTYPE)

    if view == "chip":
        nch, r, c = g.shape
        tr = _row_tile(r // 2, c * 4, 16)
        gv = g.reshape(nch, 2, r // 2, c)
        grid = (nch, (r // 2) // tr)
        in_specs = [pl.BlockSpec((None, None, tr, c), lambda j, i, c_ref: (j, c_ref[0], i, 0)),
                    pl.BlockSpec((None, tr, c), lambda j, i, c_ref: (j, i, 0))]
        out_spec = pl.BlockSpec((None, tr, c), lambda j, i, c_ref: (j, i, 0))
        sem = ("parallel", "parallel")
    else:
        r, c4 = g.shape
        tr = _row_tile(r // 2, c4 * 4, 16)
        gv = g.reshape(2, r // 2, c4)
        grid = ((r // 2) // tr,)
        in_specs = [pl.BlockSpec((None, tr, c4), lambda i, c_ref: (c_ref[0], i, 0)), pl.BlockSpec((tr, c4), lambda i, c_ref: (i, 0))]
        out_spec = pl.BlockSpec((tr, c4), lambda i, c_ref: (i, 0))
        sem = ("parallel",)
    grid_spec = pltpu.PrefetchScalarGridSpec(num_scalar_prefetch=1, grid=grid, in_specs=in_specs, out_specs=out_spec)
    return pl.pallas_call(body, grid_spec=grid_spec, out_shape=jax.ShapeDtypeStruct(recv.shape, WIRE_DTYPE),
                          compiler_params=_params(*sem), name=name)(c_idx, gv, recv)


def _quad_sum(gs, recvs, quads, view, chip_idx, c_idx, *, name):
    nl = len(quads)
    nch, rh, c = quads[0].shape
    tr = _row_tile(rh, c * 4, 16)

    def body(_, __, *refs):
        o_ref = refs[-1]
        per = nch + 1
        for l in range(nl):
            grp = refs[l * per:(l + 1) * per]
            acc = grp[0][...] + grp[1][...]
            for r in grp[2:]:
                acc = acc + r[...].astype(F32)
            o_ref[l] = acc

    if view == "chip":
        own = [pl.BlockSpec((None, None, tr, c), lambda i, j, h: (j[0], h[0], i, 0)),
               pl.BlockSpec((None, tr, c), lambda i, j, h: (j[0], i, 0))]
        gviews = [g.reshape(nch, 2, rh, c) for g in gs]
    else:
        own = [pl.BlockSpec((None, tr, c), lambda i, j, h: (h[0], i, j[0])), pl.BlockSpec((tr, c), lambda i, j, h: (i, j[0]))]
        gviews = [g.reshape(2, rh, nch * c) for g in gs]
    assert nch & (nch - 1) == 0
    got = [pl.BlockSpec((None, tr, c), functools.partial(lambda i, j, h, k: ((j[0] + k) & (nch - 1), i, 0), k=k))
           for k in range(1, nch)]
    ins = []
    for l in range(nl):
        ins += [gviews[l], recvs[l]] + [quads[l]] * (nch - 1)
    grid_spec = pltpu.PrefetchScalarGridSpec(
        num_scalar_prefetch=2, grid=(rh // tr,), in_specs=(own + got) * nl,
        out_specs=pl.BlockSpec((nl, tr, c), lambda i, j, h: (0, i, 0)))
    return pl.pallas_call(body, grid_spec=grid_spec, out_shape=jax.ShapeDtypeStruct((nl, rh, c), F32),
                          compiler_params=_params("parallel"), name=name)(chip_idx, c_idx, *ins)


def _sum_devices(g8, own, dev_idx, *, name):
    k, rows, cols = g8.shape

    def body(d_ref, a_ref, x_ref, o_ref):
        acc = None
        for i in range(k):
            term = jnp.where(d_ref[0] == i, x_ref[...], a_ref[i])
            acc = term if acc is None else acc + term
        o_ref[...] = acc

    grid_spec = pltpu.PrefetchScalarGridSpec(
        num_scalar_prefetch=1, grid=(1,),
        in_specs=[pl.BlockSpec((k, rows, cols), lambda i, d_ref: (0, 0, 0)), pl.BlockSpec((rows, cols), lambda i, d_ref: (0, 0))],
        out_specs=pl.BlockSpec((rows, cols), lambda i, d_ref: (0, 0)))
    return pl.pallas_call(body, grid_spec=grid_spec, out_shape=jax.ShapeDtypeStruct((rows, cols), g8.dtype),
                          compiler_params=_params("arbitrary"), name=name)(dev_idx, g8, own)


def _adamw(w, g, m, v, *, name):
    rows, cols = w.shape
    tr = rows
    for cand in (256, 128, 64, 32, 16, 8):
        if rows % cand == 0 and cand * cols <= 512 * 1024:
            tr = cand
            break
    c1 = 1.0 - ADAM_B1 ** ADAM_STEP
    c2 = 1.0 - ADAM_B2 ** ADAM_STEP

    def body(w_ref, g_ref, m_ref, v_ref, d_ref, nm_ref, nv_ref):
        gv = g_ref[...]
        nm = ADAM_B1 * m_ref[...] + (1.0 - ADAM_B1) * gv
        nv = ADAM_B2 * v_ref[...] + (1.0 - ADAM_B2) * (gv * gv)
        d_ref[...] = -ADAM_LR * ((nm / c1) / (jnp.sqrt(nv / c2) + ADAM_EPS) + ADAM_WD * w_ref[...])
        nm_ref[...] = nm
        nv_ref[...] = nv

    spec = pl.BlockSpec((tr, cols), lambda i: (i, 0))
    shp = jax.ShapeDtypeStruct((rows, cols), F32)
    return pl.pallas_call(body, grid=(rows // tr,), in_specs=[spec] * 4, out_specs=[spec] * 3, out_shape=[shp] * 3,
                          compiler_params=_params("parallel"), name=name)(w, g, m, v)


def _adamw_halves(w, m, v, mine, other, c_idx, *, name):
    nl, r, c = w.shape
    rh = r // 2
    tr = _row_tile(rh, c * 4)
    c1 = 1.0 - ADAM_B1 ** ADAM_STEP
    c2 = 1.0 - ADAM_B2 ** ADAM_STEP

    def body(c_ref, w_ref, m_ref, v_ref, a_ref, b_ref, g_ref, d_ref, nm_ref, nv_ref):
        gv = jnp.where(pl.program_id(1) == c_ref[0], a_ref[...], b_ref[...])
        nm = ADAM_B1 * m_ref[...] + (1.0 - ADAM_B1) * gv
        nv = ADAM_B2 * v_ref[...] + (1.0 - ADAM_B2) * (gv * gv)
        g_ref[...] = gv
        d_ref[...] = -ADAM_LR * ((nm / c1) / (jnp.sqrt(nv / c2) + ADAM_EPS) + ADAM_WD * w_ref[...])
        nm_ref[...] = nm
        nv_ref[...] = nv

    full = pl.BlockSpec((None, None, tr, c), lambda l, h, i, c_ref: (l, h, i, 0))
    half = pl.BlockSpec((None, tr, c), lambda l, h, i, c_ref: (l, i, 0))
    grid_spec = pltpu.PrefetchScalarGridSpec(num_scalar_prefetch=1, grid=(nl, 2, rh // tr),
                                             in_specs=[full] * 3 + [half] * 2, out_specs=[full] * 4)
    shp = jax.ShapeDtypeStruct((nl, 2, rh, c), F32)
    view = (nl, 2, rh, c)
    outs = pl.pallas_call(body, grid_spec=grid_spec, out_shape=[shp] * 4, compiler_params=_params("parallel", "parallel", "parallel"),
                          name=name)(c_idx, w.reshape(view), m.reshape(view), v.reshape(view), mine, other)
    return [o.reshape(nl, r, c) for o in outs]


WEIGHTS = ["mem_ln_g", "mem_ln_b", "w_in", "sg_ln_g", "sg_ln_b", "sg_w", "sg_b", "conv_w", "conv_b", "dt_bias", "a_log",
           "d_skip", "ssm_norm_g", "p_a", "p_b", "w_mix_o", "w_xq", "w_xkv", "w_xo", "w_ffn_in", "w_ffn_out", "ln_g", "ln_b"]
ARG_NAMES = ["x", "mem"] + WEIGHTS + ["loss_target"] + ["m_" + n for n in WEIGHTS] + ["v_" + n for n in WEIGHTS]
BIG = {"w_in": (1, (1024, 9248)), "p_a": (0, (1024, 1024)), "p_b": (0, (2048, 1024)), "w_mix_o": (0, (1024, 1024)),
       "w_xq": (0, (1024, 1024)), "w_xkv": (1, (1024, 2048)), "w_xo": (0, (1024, 1024)), "w_ffn_in": (1, (1024, 5632)),
       "w_ffn_out": (0, (2816, 1024))}
SMALL_SHARDED = {"conv_w": (4, 3072), "ln_g": (3, 1024), "ln_b": (3, 1024)}
SMALL = [n for n in WEIGHTS if n not in BIG]
W_IN_MAP = ((0, 4096, "main", 0), (4096, 7168, "main", XBC_COL0), (7168, 7200, "dt", 0), (7200, 9248, "main", GAB_COL0))
W_IN_SHARD = 9248 // N_CHIPS


def _w_in_chip_major(gm, gd):
    src = {"main": gm, "dt": gd}
    blocks = []
    for j in range(N_CHIPS):
        lo, hi = j * W_IN_SHARD, (j + 1) * W_IN_SHARD
        parts = [src[k][:, o + max(lo, a) - a:o + min(hi, b) - a] for a, b, k, o in W_IN_MAP if max(lo, a) < min(hi, b)]
        blocks.append(jnp.concatenate(parts, axis=1))
    return jnp.stack(blocks)


def _w_in_reassemble(wc):
    def cols(a, b):
        out = []
        for j in range(N_CHIPS):
            lo, hi = max(a, j * W_IN_SHARD), min(b, (j + 1) * W_IN_SHARD)
            if lo < hi:
                out.append(wc[j][:, lo - j * W_IN_SHARD:hi - j * W_IN_SHARD])
        return out

    main = sorted((m for m in W_IN_MAP if m[2] == "main"), key=lambda m: m[3])
    w_main = jnp.concatenate([p for a, b, _, _ in main for p in cols(a, b)], axis=1)
    (a, b, _, _), = [m for m in W_IN_MAP if m[2] == "dt"]
    w_dt = jnp.pad(jnp.concatenate(cols(a, b), axis=1), ((0, 0), (0, HEAD_PAD - (b - a))))
    return w_main, w_dt
GATHER_KIND = {"w_in": "chip", "p_a": "row", "p_b": "row", "w_mix_o": "row", "w_xq": "row", "w_xkv": "col", "w_xo": "row",
               "w_ffn_in": "col", "w_ffn_out": "row", "conv_w": "chip", "ln_g": "chip", "ln_b": "chip"}
GRAD_VIEW = {n: ("col" if k == "col" else "chip") for n, k in GATHER_KIND.items() if n in BIG}


def _shard_shape(name):
    axis, (r, c) = BIG[name]
    return (r // N_CHIPS, c) if axis == 0 else (r, c // N_CHIPS)


def _pad_rows(flat, cols, row_mult):
    n = flat.shape[0]
    rows = -(-n // cols)
    rows = -(-rows // row_mult) * row_mult
    return jnp.pad(flat, (0, rows * cols - n)).reshape(rows, cols)


def _gather_small_params(a, chip):
    names = list(SMALL_SHARDED)
    kinds = [GATHER_KIND[n] for n in names]
    bufs = [_cast_place(a[n], GATHER_KIND[n], F32, chip.reshape(1), name=f"place_{n}") for n in names]
    outs = _gather_params(bufs, [a[n].shape[1:] for n in names], kinds, name="gather_small_params")
    full = {}
    for n, o in zip(names, outs):
        _, _, r, c = o.shape
        full[n] = jnp.transpose(o, (0, 2, 1, 3)).reshape(DEPTH, r, N_CHIPS * c)
    return full


GATHER_GROUPS = (("w_in",), tuple(n for n in BIG if n != "w_in"))


def _gather_group_start(a, l, names, chip, after, *, tag):
    bufs = [_cast_place_layer(a[n], l, GATHER_KIND[n], chip.reshape(1), after, name=f"place_{n}_l{l}") for n in names]
    return _gather_start(bufs, [a[n].shape[1:] for n in names], [GATHER_KIND[n] for n in names], name=f"gather_start_{tag}")


def _gather_group_finish(a, names, flight, after, *, tag):
    send_sems, recv_sems, bufs, token = flight
    shapes, kinds = [a[n].shape[1:] for n in names], [GATHER_KIND[n] for n in names]
    bufs = _gather_wait(send_sems, recv_sems, bufs, shapes, kinds, token if after is None else after, name=f"gather_wait_{tag}")
    full = dict(zip(names, _gather_forward(bufs, shapes, kinds, name=f"gather_forward_{tag}")))
    if "w_in" in full:
        full["w_main"], full["w_dt"] = _w_in_reassemble(full.pop("w_in"))
    return full


def _layer_weights(a, big, small, l):
    w = dict(big)
    for n in SMALL_SHARDED:
        w[n] = small[n][l]
    for n in ["sg_ln_g", "sg_ln_b", "sg_w", "conv_b", "ssm_norm_g"]:
        w[n] = a[n][l]
    w["sg_bcol"] = a["sg_b"][l][..., None]
    for n in ["dt_bias", "a_log"]:
        w[n + "8"] = _pad_heads(a[n][l])
    w["d_skipx"] = _expand_heads(a["d_skip"][l])
    return w


def _grad_views(grads, names):
    gs = []
    for n in names:
        axis, _ = BIG[n]
        r, c = _shard_shape(n)
        if n == "w_in":
            gs.append(_w_in_chip_major(grads["w_main"], grads["w_dt"]))
        elif axis == 0:
            gs.append(grads[n].reshape(N_CHIPS, r, c))
        else:
            gs.append(grads[n])
    return gs


class _GradExchange:
    def __init__(self, grads, names, c_idx, tag):
        self.names, self.c_idx, self.tag = names, c_idx, tag
        self.views = [GRAD_VIEW[n] for n in names]
        self.gs = _grad_views(grads, names)

    def start(self):
        self.sems = _grads_to_sibling_start(self.gs, self.views, name=f"grads_to_sibling_start_{self.tag}")
        return self.sems[4]

    def cross(self, after):
        send_sems, recv_sems, gs, lands, token = self.sems
        self.gs, self.recv = _grads_to_sibling_wait(send_sems, recv_sems, gs, lands, self.views, token if after is None else after,
                                                    name=f"grads_to_sibling_wait_{self.tag}")
        cpre = self.c_idx.reshape(1)
        pairs = [_pair_sum(g, rv, v, cpre, name=f"grads_pair_sum_{n}_{self.tag}")
                 for g, rv, v, n in zip(self.gs, self.recv, self.views, self.names)]
        self.sems = _grads_to_chips_start(pairs, self.views, name=f"grads_to_chips_start_{self.tag}")
        return self.sems[4]

    def finish(self, after):
        send_sems, recv_sems, pairs, lands, _ = self.sems
        quads = _grads_to_chips_wait(send_sems, recv_sems, pairs, lands, self.views, after, name=f"grads_to_chips_wait_{self.tag}")
        return {n: (g, rv, q) for n, g, rv, q in zip(self.names, self.gs, self.recv, quads)}


def _finish_big_grads(parts, c_idx, chip):
    tots = [_quad_sum([parts[l][n][0] for l in range(DEPTH)], [parts[l][n][1] for l in range(DEPTH)],
                      [parts[l][n][2] for l in range(DEPTH)], GRAD_VIEW[n], chip.reshape(1), c_idx.reshape(1),
                      name=f"grads_chip_sum_{n}") for n in BIG]
    others = _grads_share(tots, name="grads_share")
    return {n: (t, o) for n, t, o in zip(BIG, tots, others)}


def _direct_copies(x_ref, land_ref, send_sems, recv_sems):
    mx, my, mc = lax.axis_index("x"), lax.axis_index("y"), lax.axis_index("c")
    me = 4 * mx + 2 * my + mc
    copies = []
    for k in range(N_DEV - 1):
        f = k + 1
        to = (mx ^ (f >> 2 & 1), my ^ (f >> 1 & 1), mc ^ (f & 1))
        copies.append(pltpu.make_async_remote_copy(src_ref=x_ref, dst_ref=land_ref.at[me], send_sem=send_sems.at[k],
                                                   recv_sem=recv_sems.at[k], device_id=to, device_id_type=_MESH))
    return copies


def _all_gather8_start(x, *, name):
    land = pltpu.with_memory_space_constraint(lax.empty((N_DEV,) + x.shape, x.dtype), pltpu.HBM)

    def body(x_ref, land_ref, send_sems, recv_sems, x_out, land_out, token):
        for cp in _direct_copies(x_ref, land_ref, send_sems, recv_sems):
            cp.start()
        token[...] = jnp.zeros_like(token)

    n = N_DEV - 1
    return pl.pallas_call(
        body, name=name,
        out_shape=(pltpu.SemaphoreType.DMA((n,)), pltpu.SemaphoreType.DMA((n,)), pltpu.HBM(x.shape, x.dtype),
                   pltpu.HBM(land.shape, land.dtype), jax.ShapeDtypeStruct((SUBLANE, LANE), F32)),
        in_specs=[_HBM, _HBM], out_specs=(_SEM, _SEM, _HBM, _HBM, pl.BlockSpec(memory_space=pltpu.VMEM)),
        input_output_aliases={0: 2, 1: 3}, compiler_params=pltpu.CompilerParams(has_side_effects=_EFFECT),
    )(pltpu.with_memory_space_constraint(x, pltpu.HBM), land)


def _all_gather8_wait(send_sems, recv_sems, x, land, after, *, name):
    def body(x_ref, land_ref, s_sems, r_sems, _, x_out, land_out):
        for cp in _direct_copies(x_ref, land_ref, s_sems, r_sems):
            cp.wait_send()
            cp.wait_recv()

    return pl.pallas_call(
        body, name=name, out_shape=(pltpu.HBM(x.shape, x.dtype), pltpu.HBM(land.shape, land.dtype)),
        in_specs=[_HBM, _HBM, _SEM, _SEM, _ANY], out_specs=(_HBM, _HBM), input_output_aliases={0: 0, 1: 1},
        compiler_params=pltpu.CompilerParams(has_side_effects=_EFFECT),
    )(x, land, send_sems, recv_sems, after)


def _pack_small(small):
    return _pad_rows(jnp.concatenate([small[n].reshape(-1) for n in small]), LANE, SUBLANE)


def _unpack_small(small, g8, packed, chip, c_idx, *, name):
    names = list(small)
    tot = _sum_devices(g8, packed, (2 * chip + c_idx).reshape(1), name=name).reshape(-1)
    out, off = {}, 0
    for n in names:
        sz = small[n].size
        full = tot[off:off + sz].reshape(small[n].shape)
        off += sz
        if n in SMALL_SHARDED:
            cs = SMALL_SHARDED[n][1] // N_CHIPS
            full = lax.dynamic_slice_in_dim(full, chip * cs, cs, axis=-1)
        out[n] = full
    return out


def kernel(x, mem, mem_ln_g, mem_ln_b, w_in, sg_ln_g, sg_ln_b, sg_w, sg_b, conv_w, conv_b, dt_bias, a_log, d_skip, ssm_norm_g, p_a, p_b, w_mix_o, w_xq, w_xkv, w_xo, w_ffn_in, w_ffn_out, ln_g, ln_b, loss_target, m_mem_ln_g, m_mem_ln_b, m_w_in, m_sg_ln_g, m_sg_ln_b, m_sg_w, m_sg_b, m_conv_w, m_conv_b, m_dt_bias, m_a_log, m_d_skip, m_ssm_norm_g, m_p_a, m_p_b, m_w_mix_o, m_w_xq, m_w_xkv, m_w_xo, m_w_ffn_in, m_w_ffn_out, m_ln_g, m_ln_b, v_mem_ln_g, v_mem_ln_b, v_w_in, v_sg_ln_g, v_sg_ln_b, v_sg_w, v_sg_b, v_conv_w, v_conv_b, v_dt_bias, v_a_log, v_d_skip, v_ssm_norm_g, v_p_a, v_p_b, v_w_mix_o, v_w_xq, v_w_xkv, v_w_xo, v_w_ffn_in, v_w_ffn_out, v_ln_g, v_ln_b):
    a = dict(zip(ARG_NAMES, (x, mem, mem_ln_g, mem_ln_b, w_in, sg_ln_g, sg_ln_b, sg_w, sg_b, conv_w, conv_b, dt_bias, a_log, d_skip, ssm_norm_g, p_a, p_b, w_mix_o, w_xq, w_xkv, w_xo, w_ffn_in, w_ffn_out, ln_g, ln_b, loss_target, m_mem_ln_g, m_mem_ln_b, m_w_in, m_sg_ln_g, m_sg_ln_b, m_sg_w, m_sg_b, m_conv_w, m_conv_b, m_dt_bias, m_a_log, m_d_skip, m_ssm_norm_g, m_p_a, m_p_b, m_w_mix_o, m_w_xq, m_w_xkv, m_w_xo, m_w_ffn_in, m_w_ffn_out, m_ln_g, m_ln_b, v_mem_ln_g, v_mem_ln_b, v_w_in, v_sg_ln_g, v_sg_ln_b, v_sg_w, v_sg_b, v_conv_w, v_conv_b, v_dt_bias, v_a_log, v_d_skip, v_ssm_norm_g, v_p_a, v_p_b, v_w_mix_o, v_w_xq, v_w_xkv, v_w_xo, v_w_ffn_in, v_w_ffn_out, v_ln_g, v_ln_b)))
    c_idx = lax.axis_index("c").astype(jnp.int32)
    chip = (2 * lax.axis_index("x") + lax.axis_index("y")).astype(jnp.int32)

    small = _gather_small_params(a, chip)
    ga, gb = GATHER_GROUPS
    flights = {(0, 0): _gather_group_start(a, 0, ga, chip, small["ln_b"], tag="l0_a")}
    flights[0, 1] = _gather_group_start(a, 0, gb, chip, flights[0, 0][3], tag="l0_b")

    def layer_weights(after, l):
        first = _gather_group_finish(a, ga, flights[l, 0], after if l else flights[l, 1][3], tag=f"l{l}_a")

        def rest(w, after_b):
            more = _gather_group_finish(a, gb, flights[l, 1], after_b, tag=f"l{l}_b")
            if l + 1 < DEPTH:
                flights[l + 1, 0] = _gather_group_start(a, l + 1, ga, chip, more["p_a"], tag=f"l{l + 1}_a")
                flights[l + 1, 1] = _gather_group_start(a, l + 1, gb, chip, flights[l + 1, 0][3], tag=f"l{l + 1}_b")
                more["p_a"] = more["p_a"] + flights[l + 1, 1][3][0, 0].astype(MXU_DTYPE)
            return {k: v for k, v in {**w, **more}.items() if k != "rest"}

        return dict(_layer_weights(a, first, small, l), rest=rest)

    layers = [functools.partial(layer_weights, l=l) for l in range(DEPTH)]
    exchanges, seen, small_flight = [], {}, {}

    def start_exchange(l, names, grads_l):
        ex = _GradExchange(grads_l, names, c_idx, f"l{l}_{names[0]}")
        tokens = [ex.start()]
        if exchanges:
            tokens.append(exchanges[-1][1].cross(tokens[0]))
        exchanges.append((l, ex))
        seen[l] = grads_l
        if l == 0 and names == GRAD_GROUPS[-1]:
            tokens.append(ex.cross(None))
            small = {}
            for n in SMALL:
                if n.startswith("mem_ln"):
                    continue
                per_layer = []
                for k in range(DEPTH):
                    g = seen[k][n]
                    if n in ("dt_bias", "a_log", "d_skip"):
                        g = g[0, :SSM_HEADS]
                    per_layer.append(g.reshape(a[n].shape[1:-1] + (-1,)))
                small[n] = jnp.stack(per_layer)
            small_flight["small"] = small
            small_flight["sems"] = _all_gather8_start(_pack_small(small), name="gather_small_grads_start")
            tokens.append(small_flight["sems"][4])
        return sum(tokens[1:], tokens[0])

    lsum, grad_x, grads, d_mem_g, d_mem_b = _local_step(x, mem, loss_target, mem_ln_g, mem_ln_b, layers, start_exchange)
    loss = lax.psum(0.5 * jnp.sum(lsum) / D_MODEL, ("x", "y", "c"))

    parts = [{} for _ in range(DEPTH)]
    for l, ex in exchanges:
        parts[l].update(ex.finish(grad_x))
    halves = _finish_big_grads(parts, c_idx, chip)
    gw = {}
    send_sems, recv_sems, packed, land, _ = small_flight["sems"]
    packed, g8 = _all_gather8_wait(send_sems, recv_sems, packed, land, grad_x, name="gather_small_grads_wait")
    gw.update(_unpack_small(small_flight["small"], g8, packed, chip, c_idx, name="small_grads_sum"))
    mem_small = {"mem_ln_g": d_mem_g, "mem_ln_b": d_mem_b}
    mem_packed = _pack_small(mem_small)
    gw.update(_unpack_small(mem_small, _all_gather8(mem_packed, name="gather_mem_ln_grads"), mem_packed, chip, c_idx,
                            name="mem_ln_grads_sum"))

    delta, new_m, new_v = {}, {}, {}
    for n in BIG:
        mine, other = halves[n]
        gw[n], delta[n], new_m[n], new_v[n] = _adamw_halves(a[n], a["m_" + n], a["v_" + n], mine, other, c_idx.reshape(1),
                                                             name=f"adamw_{n}")
    for n in SMALL:
        shp = a[n].shape
        view = (-1, LANE) if a[n].size % LANE == 0 else (1, -1)
        outs = _adamw(*[v.reshape(view) for v in (a[n], gw[n], a["m_" + n], a["v_" + n])], name=f"adamw_{n}")
        delta[n], new_m[n], new_v[n] = (o.reshape(shp) for o in outs)
    return (loss, grad_x, *[gw[n].reshape(a[n].shape) for n in WEIGHTS], *[delta[n] for n in WEIGHTS],
            *[new_m[n] for n in WEIGHTS], *[new_v[n] for n in WEIGHTS])
```

```python
import functools
import math

import jax
import jax.numpy as jnp
from jax import lax
from jax.experimental import pallas as pl
from jax.experimental.pallas import tpu as pltpu

F32 = jnp.float32
MXU_DTYPE = jnp.bfloat16
WIRE_DTYPE = jnp.bfloat16
STASH_DTYPE = jnp.bfloat16

D_MODEL = 1024
DEPTH = 2
CHUNK = 128
SG_GROUPS = 8
SSM_INNER = 2048
SSM_HEADDIM = 64
SSM_HEADS = 32
SSM_STATE = 128
SSM_GROUPS = 4
SSM_CONV = 4
SSM_CONV_DIM = 3072
X_HEADS = 4
X_HEADDIM = 256
FFN_HIDDEN = 2816
ALPHA = float((2 * DEPTH) ** 0.25)
LN_EPS = 1e-5
RMS_EPS = 1e-5
ADAM_LR = 0.001
ADAM_B1 = 0.9
ADAM_B2 = 0.999
ADAM_EPS = 1e-08
ADAM_WD = 0.01
ADAM_STEP = 10

MAIN_COLS = 9216
UVZ_COLS = 4096
GAB_COL0 = 4096
XBC_COL0 = 6144
HEAD_PAD = 128

VMEM_LIMIT = 56 * 1024 * 1024
BLOCK_BYTES = 2 * 1024 * 1024
ROW_TILES = (512, 256, 128)
LANE = 128
SUBLANE = 8

N_CHIPS = 4
N_DEV = 8


def _pick(n, cands):
    for c in cands:
        if n % c == 0:
            return c
    return n


MM_TILE_MAX = 1408
MM_OPERAND_BYTES = 8 * 1024 * 1024


def _div_tile(n, limit):
    best = None
    for t in range(LANE, min(n, limit) + 1, LANE):
        if n % t == 0:
            best = t
    return n if best is None else best


def _params(*sem):
    return pltpu.CompilerParams(dimension_semantics=tuple(sem), vmem_limit_bytes=VMEM_LIMIT)


_ANY = pl.BlockSpec(memory_space=pl.ANY)
_MESH = pl.DeviceIdType.MESH


def _nt(a, b):
    return lax.dot_general(a, b, (((1,), (1,)), ((), ())), preferred_element_type=F32)


def _tn(a, b):
    return lax.dot_general(a, b, (((0,), (0,)), ((), ())), preferred_element_type=F32)


def _nn(a, b):
    return jnp.dot(a, b, preferred_element_type=F32)


def _sigmoid(x):
    return 0.5 * jnp.tanh(0.5 * x) + 0.5


def _split3(v):
    def top(x):
        bits = lax.bitcast_convert_type(x, jnp.uint32) & jnp.uint32(0xFFFF0000)
        return lax.bitcast_convert_type(bits, F32)

    v1 = top(v)
    r1 = v - v1
    v2 = top(r1)
    v3 = r1 - v2
    return v1.astype(jnp.bfloat16), v2.astype(jnp.bfloat16), v3.astype(jnp.bfloat16)


def _dot_exact(a, b, dn, data):
    if data == 0:
        mat = b.astype(jnp.bfloat16)
        return sum(lax.dot_general(p, mat, dn, preferred_element_type=F32) for p in _split3(a))
    mat = a.astype(jnp.bfloat16)
    return sum(lax.dot_general(mat, p, dn, preferred_element_type=F32) for p in _split3(b))


_DN_NN = (((1,), (0,)), ((), ()))
_DN_TN = (((0,), (0,)), ((), ()))


def _gelu(x):
    return 0.5 * x * (1.0 + lax.erf(x * (2.0 ** -0.5)))


def _gelu_grad(x):
    return 0.5 * (1.0 + lax.erf(x * (2.0 ** -0.5))) + x * jnp.exp(-0.5 * x * x) * (1.0 / math.sqrt(2.0 * math.pi))


def _mm(a, b, *, ta=False, tb=False, out_dtype=F32, after=None, name):
    if ta:
        kdim, m = a.shape
    else:
        m, kdim = a.shape
    if tb:
        n, k2 = b.shape[-2:]
    else:
        k2, n = b.shape[-2:]
    assert kdim == k2, (a.shape, b.shape, ta, tb)
    tm = _div_tile(m, MM_TILE_MAX)
    tn = _div_tile(n, MM_TILE_MAX)
    tk = _div_tile(kdim, MM_OPERAND_BYTES // (tm * a.dtype.itemsize + tn * b.dtype.itemsize))
    nk = kdim // tk
    dn = (((0 if ta else 1,), (1 if tb else 0,)), ((), ()))

    extra = [] if after is None else [after]

    def body(a_ref, b_ref, *rest):
        o_ref = rest[len(extra)]
        d = lax.dot_general(a_ref[...].astype(MXU_DTYPE), b_ref[...].astype(MXU_DTYPE), dn, preferred_element_type=F32)
        if nk == 1:
            o_ref[...] = d.astype(out_dtype)
            return
        acc_ref = rest[len(extra) + 1]
        k = pl.program_id(2)

        @pl.when(k == 0)
        def _():
            acc_ref[...] = d

        @pl.when(jnp.logical_and(k > 0, k < nk - 1))
        def _():
            acc_ref[...] += d

        @pl.when(k == nk - 1)
        def _():
            o_ref[...] = (acc_ref[...] + d).astype(out_dtype)

    a_spec = pl.BlockSpec((tk, tm), lambda i, j, k: (k, i)) if ta else pl.BlockSpec((tm, tk), lambda i, j, k: (i, k))
    b_spec = pl.BlockSpec((tn, tk), lambda i, j, k: (j, k)) if tb else pl.BlockSpec((tk, tn), lambda i, j, k: (k, j))
    return pl.pallas_call(
        body, grid=(m // tm, n // tn, nk), in_specs=[a_spec, b_spec] + [_ANY] * len(extra),
        out_specs=pl.BlockSpec((tm, tn), lambda i, j, k: (i, j)),
        out_shape=jax.ShapeDtypeStruct((m, n), out_dtype),
        scratch_shapes=[pltpu.VMEM((tm, tn), F32)] if nk > 1 else [],
        compiler_params=_params("parallel", "parallel", "arbitrary"), name=name)(a, b, *extra)


def _row_spec(tm, c, col=0):
    return pl.BlockSpec((tm, c), lambda i: (i, col))


def _par_spec(shape):
    nd = len(shape)
    return pl.BlockSpec(shape, lambda i: (0,) * nd)


def _ln_fwd(x, f, g, b, *, name):
    t, c = x.shape
    tm = _pick(t, ROW_TILES)
    has_f = f is not None

    def body(*refs):
        if has_f:
            x_ref, f_ref, g_ref, b_ref, y_ref, yb_ref, xh_ref, rs_ref = refs
            r = ALPHA * x_ref[...] + f_ref[...]
        else:
            x_ref, g_ref, b_ref, y_ref, yb_ref, xh_ref, rs_ref = refs
            r = x_ref[...]
        mu = jnp.mean(r, axis=-1, keepdims=True)
        xc = r - mu
        var = jnp.mean(xc * xc, axis=-1, keepdims=True)
        rstd = lax.rsqrt(var + LN_EPS)
        xh = xc * rstd
        y = xh * g_ref[...] + b_ref[...]
        y_ref[...] = y
        yb_ref[...] = y.astype(MXU_DTYPE)
        xh_ref[...] = xh
        rs_ref[...] = jnp.broadcast_to(rstd, rs_ref.shape)

    ins = [x] + ([f] if has_f else []) + [g.reshape(1, c), b.reshape(1, c)]
    in_specs = [_row_spec(tm, c)] * (2 if has_f else 1) + [_par_spec((1, c))] * 2
    return pl.pallas_call(
        body, grid=(t // tm,), in_specs=in_specs,
        out_specs=[_row_spec(tm, c), _row_spec(tm, c), _row_spec(tm, c), _row_spec(tm, LANE)],
        out_shape=[jax.ShapeDtypeStruct((t, c), F32), jax.ShapeDtypeStruct((t, c), MXU_DTYPE),
                   jax.ShapeDtypeStruct((t, c), F32), jax.ShapeDtypeStruct((t, LANE), F32)],
        compiler_params=_params("parallel"), name=name)(*ins)


def _ln_bwd(addends, scales, xh, rs, g, *, name):
    t, c = xh.shape
    tm = _pick(t, ROW_TILES)
    na = len(addends)

    def body(*refs):
        a_refs = refs[:na]
        xh_ref, rs_ref, g_ref, dp_ref, dpb_ref, dg_ref, db_ref = refs[na:]

        @pl.when(pl.program_id(0) == 0)
        def _():
            dg_ref[...] = jnp.zeros_like(dg_ref)
            db_ref[...] = jnp.zeros_like(db_ref)

        dy = None
        for s, r in zip(scales, a_refs):
            term = r[...] if s == 1.0 else s * r[...]
            dy = term if dy is None else dy + term
        xhv = xh_ref[...]
        dxh = dy * g_ref[...]
        m1 = jnp.mean(dxh, axis=-1, keepdims=True)
        m2 = jnp.mean(dxh * xhv, axis=-1, keepdims=True)
        dp = rs_ref[:, 0:1] * (dxh - m1 - xhv * m2)
        dp_ref[...] = dp
        dpb_ref[...] = dp.astype(MXU_DTYPE)
        dg_ref[...] += jnp.sum(dy * xhv, axis=0, keepdims=True)
        db_ref[...] += jnp.sum(dy, axis=0, keepdims=True)

    in_specs = [_row_spec(tm, c)] * (na + 1) + [_row_spec(tm, LANE), _par_spec((1, c))]
    return pl.pallas_call(
        body, grid=(t // tm,), in_specs=in_specs,
        out_specs=[_row_spec(tm, c), _row_spec(tm, c), _par_spec((1, c)), _par_spec((1, c))],
        out_shape=[jax.ShapeDtypeStruct((t, c), F32), jax.ShapeDtypeStruct((t, c), MXU_DTYPE),
                   jax.ShapeDtypeStruct((1, c), F32), jax.ShapeDtypeStruct((1, c), F32)],
        compiler_params=_params("arbitrary"), name=name)(*addends, xh, rs, g.reshape(1, c))


def _add_scaled(addends, scales, *, name):
    t, c = addends[0].shape
    tm = _pick(t, ROW_TILES)
    na = len(addends)

    def body(*refs):
        acc = None
        for s, r in zip(scales, refs[:na]):
            term = r[...] if s == 1.0 else s * r[...]
            acc = term if acc is None else acc + term
        refs[na][...] = acc

    return pl.pallas_call(
        body, grid=(t // tm,), in_specs=[_row_spec(tm, c)] * na, out_specs=_row_spec(tm, c),
        out_shape=jax.ShapeDtypeStruct((t, c), F32), compiler_params=_params("parallel"), name=name)(*addends)


def _loss_head(y, tgt, *, name):
    t, c = y.shape
    tm = _pick(t, ROW_TILES)

    def body(y_ref, t_ref, dy_ref, ls_ref):
        @pl.when(pl.program_id(0) == 0)
        def _():
            ls_ref[...] = jnp.zeros_like(ls_ref)

        e = y_ref[...] - t_ref[...]
        dy_ref[...] = e * (1.0 / c)
        ls_ref[...] += jnp.sum(e * e, axis=0, keepdims=True)

    return pl.pallas_call(
        body, grid=(t // tm,), in_specs=[_row_spec(tm, c)] * 2,
        out_specs=[_row_spec(tm, c), _par_spec((1, c))],
        out_shape=[jax.ShapeDtypeStruct((t, c), F32), jax.ShapeDtypeStruct((1, c), F32)],
        compiler_params=_params("arbitrary"), name=name)(y, tgt)


def _swiglu_fwd(h, *, name):
    t, two_f = h.shape
    fh = two_f // 2
    tm = _pick(t, (256, 128))

    def body(g_ref, u_ref, a_ref):
        g = g_ref[...].astype(F32)
        a_ref[...] = (g * _sigmoid(g) * u_ref[...].astype(F32)).astype(MXU_DTYPE)

    return pl.pallas_call(
        body, grid=(t // tm,), in_specs=[_row_spec(tm, fh, 0), _row_spec(tm, fh, 1)], out_specs=_row_spec(tm, fh),
        out_shape=jax.ShapeDtypeStruct((t, fh), MXU_DTYPE), compiler_params=_params("parallel"), name=name)(h, h)


def _swiglu_bwd(h, da, *, name):
    t, two_f = h.shape
    fh = two_f // 2
    tm = _pick(t, (256, 128))

    def body(g_ref, u_ref, da_ref, dh_ref):
        g = g_ref[...].astype(F32)
        s = _sigmoid(g)
        dav = da_ref[...].astype(F32)
        dh_ref[:, :fh] = (dav * u_ref[...].astype(F32) * (s * (1.0 + g * (1.0 - s)))).astype(MXU_DTYPE)
        dh_ref[:, fh:] = (dav * g * s).astype(MXU_DTYPE)

    return pl.pallas_call(
        body, grid=(t // tm,), in_specs=[_row_spec(tm, fh, 0), _row_spec(tm, fh, 1), _row_spec(tm, fh)],
        out_specs=_row_spec(tm, two_f), out_shape=jax.ShapeDtypeStruct((t, two_f), MXU_DTYPE),
        compiler_params=_params("parallel"), name=name)(h, h, da)


def _attn_probs(q, k):
    s = _nt(q, k) * (X_HEADDIM ** -0.5)
    s = s - jnp.max(s, axis=-1, keepdims=True)
    p = jnp.exp(s)
    return p / jnp.sum(p, axis=-1, keepdims=True)


def _attn_fwd(q, kv, *, bsz, name):
    t = q.shape[0]
    s = t // bsz
    ml = kv.shape[0] // bsz
    hd = X_HEADDIM

    def body(q_ref, k_ref, v_ref, o_ref):
        p = _attn_probs(q_ref[...], k_ref[...])
        o_ref[...] = _nn(p.astype(MXU_DTYPE), v_ref[...]).astype(MXU_DTYPE)

    return pl.pallas_call(
        body, grid=(bsz, X_HEADS),
        in_specs=[pl.BlockSpec((s, hd), lambda b, h: (b, h)), pl.BlockSpec((ml, hd), lambda b, h: (b, h)),
                  pl.BlockSpec((ml, hd), lambda b, h: (b, X_HEADS + h))],
        out_specs=pl.BlockSpec((s, hd), lambda b, h: (b, h)),
        out_shape=jax.ShapeDtypeStruct((t, D_MODEL), MXU_DTYPE),
        compiler_params=_params("parallel", "parallel"), name=name)(q, kv, kv)


def _attn_bwd(q, kv, do, *, bsz, name):
    t = q.shape[0]
    s = t // bsz
    ml = kv.shape[0] // bsz
    hd = X_HEADDIM

    def body(q_ref, k_ref, v_ref, do_ref, dq_ref, dk_ref, dv_ref):
        qv, kk, vv, dov = q_ref[...], k_ref[...], v_ref[...], do_ref[...]
        p = _attn_probs(qv, kk)
        dp = _nt(dov, vv)
        dv_ref[...] = _tn(p.astype(MXU_DTYPE), dov).astype(MXU_DTYPE)
        ds = (p * (dp - jnp.sum(dp * p, axis=-1, keepdims=True)) * (X_HEADDIM ** -0.5)).astype(MXU_DTYPE)
        dq_ref[...] = _nn(ds, kk).astype(MXU_DTYPE)
        dk_ref[...] = _tn(ds, qv).astype(MXU_DTYPE)

    blk_q = pl.BlockSpec((s, hd), lambda b, h: (b, h))
    blk_m = pl.BlockSpec((ml, hd), lambda b, h: (b, h))
    return pl.pallas_call(
        body, grid=(bsz, X_HEADS),
        in_specs=[blk_q, blk_m, pl.BlockSpec((ml, hd), lambda b, h: (b, X_HEADS + h)), blk_q],
        out_specs=[blk_q, blk_m, blk_m],
        out_shape=[jax.ShapeDtypeStruct((t, D_MODEL), MXU_DTYPE), jax.ShapeDtypeStruct((bsz * ml, D_MODEL), MXU_DTYPE),
                   jax.ShapeDtypeStruct((bsz * ml, D_MODEL), MXU_DTYPE)],
        compiler_params=_params("parallel", "parallel"), name=name)(q, kv, kv, do)


def _causal(n):
    row = lax.broadcasted_iota(jnp.int32, (n, n), 0)
    col = lax.broadcasted_iota(jnp.int32, (n, n), 1)
    return row >= col


def _sg_norm(v, g, b):
    gv = _gelu(v)
    mu = jnp.mean(gv, axis=-1, keepdims=True)
    xc = gv - mu
    var = jnp.mean(xc * xc, axis=-1, keepdims=True)
    rstd = lax.rsqrt(var + LN_EPS)
    xh = xc * rstd
    return xh, rstd, xh * g + b


def _sg_fwd(proj, ln_g, ln_b, w, bcol, *, name):
    t = proj.shape[0]
    c = D_MODEL
    gd = c // SG_GROUPS

    def body(u_ref, v_ref, g_ref, b_ref, w_ref, bc_ref, o_ref):
        gu = _gelu(u_ref[...].astype(F32))
        _, _, vn = _sg_norm(v_ref[...].astype(F32), g_ref[...], b_ref[...])
        mask = _causal(CHUNK)
        for g in range(SG_GROUPS):
            sl = slice(g * gd, (g + 1) * gd)
            wg = jnp.where(mask, w_ref[g], 0.0).astype(MXU_DTYPE)
            mixed = _nn(wg, vn[:, sl].astype(MXU_DTYPE)) + bc_ref[g]
            o_ref[:, sl] = (gu[:, sl] * mixed).astype(MXU_DTYPE)

    return pl.pallas_call(
        body, grid=(t // CHUNK,),
        in_specs=[_row_spec(CHUNK, c, 0), _row_spec(CHUNK, c, 1), _par_spec((1, c)), _par_spec((1, c)),
                  _par_spec((SG_GROUPS, CHUNK, CHUNK)), _par_spec((SG_GROUPS, CHUNK, 1))],
        out_specs=_row_spec(CHUNK, c), out_shape=jax.ShapeDtypeStruct((t, c), MXU_DTYPE),
        compiler_params=_params("parallel"), name=name)(proj, proj, ln_g.reshape(1, c), ln_b.reshape(1, c), w, bcol)


def _sg_bwd(proj, dsgo, ln_g, ln_b, w, bcol, dproj, *, name):
    t = proj.shape[0]
    c = D_MODEL
    gd = c // SG_GROUPS

    def body(u_ref, v_ref, d_ref, g_ref, b_ref, w_ref, bc_ref, _, duv_ref, dw_ref, dbc_ref, dg_ref, db_ref, dvn_ref):
        @pl.when(pl.program_id(0) == 0)
        def _():
            dw_ref[...] = jnp.zeros_like(dw_ref)
            dbc_ref[...] = jnp.zeros_like(dbc_ref)
            dg_ref[...] = jnp.zeros_like(dg_ref)
            db_ref[...] = jnp.zeros_like(db_ref)

        u = u_ref[...].astype(F32)
        v = v_ref[...].astype(F32)
        dso = d_ref[...].astype(F32)
        gu = _gelu(u)
        xh, rstd, vn = _sg_norm(v, g_ref[...], b_ref[...])
        mask = _causal(CHUNK)
        for g in range(SG_GROUPS):
            sl = slice(g * gd, (g + 1) * gd)
            wg = jnp.where(mask, w_ref[g], 0.0).astype(MXU_DTYPE)
            vng = vn[:, sl].astype(MXU_DTYPE)
            mixed = _nn(wg, vng) + bc_ref[g]
            duv_ref[:, sl] = (dso[:, sl] * mixed * _gelu_grad(u[:, sl])).astype(MXU_DTYPE)
            dmix = dso[:, sl] * gu[:, sl]
            dmb = dmix.astype(MXU_DTYPE)
            dbc_ref[g] += jnp.sum(dmix, axis=-1, keepdims=True)
            dw_ref[g] += jnp.where(mask, _nt(dmb, vng), 0.0)
            dvn_ref[:, sl] = _tn(wg, dmb)
        dvn = dvn_ref[...]
        dg_ref[...] += jnp.sum(dvn * xh, axis=0, keepdims=True)
        db_ref[...] += jnp.sum(dvn, axis=0, keepdims=True)
        dxh = dvn * g_ref[...]
        m1 = jnp.mean(dxh, axis=-1, keepdims=True)
        m2 = jnp.mean(dxh * xh, axis=-1, keepdims=True)
        dgv = rstd * (dxh - m1 - xh * m2)
        duv_ref[:, c:] = (dgv * _gelu_grad(v)).astype(MXU_DTYPE)

    return pl.pallas_call(
        body, grid=(t // CHUNK,),
        in_specs=[_row_spec(CHUNK, c, 0), _row_spec(CHUNK, c, 1), _row_spec(CHUNK, c), _par_spec((1, c)),
                  _par_spec((1, c)), _par_spec((SG_GROUPS, CHUNK, CHUNK)), _par_spec((SG_GROUPS, CHUNK, 1)), _ANY],
        out_specs=[_row_spec(CHUNK, 2 * c), _par_spec((SG_GROUPS, CHUNK, CHUNK)), _par_spec((SG_GROUPS, CHUNK, 1)),
                   _par_spec((1, c)), _par_spec((1, c))],
        out_shape=[jax.ShapeDtypeStruct(dproj.shape, dproj.dtype), jax.ShapeDtypeStruct((SG_GROUPS, CHUNK, CHUNK), F32),
                   jax.ShapeDtypeStruct((SG_GROUPS, CHUNK, 1), F32), jax.ShapeDtypeStruct((1, c), F32),
                   jax.ShapeDtypeStruct((1, c), F32)],
        scratch_shapes=[pltpu.VMEM((CHUNK, c), F32)], input_output_aliases={7: 0},
        compiler_params=_params("arbitrary"), name=name)(proj, proj, dsgo, ln_g.reshape(1, c), ln_b.reshape(1, c), w, bcol, dproj)


CONV_TC = 512


def _conv_taps(x):
    rows = lax.broadcasted_iota(jnp.int32, x.shape, 0)
    taps = [jnp.where(rows >= SSM_CONV - 1 - k, pltpu.roll(x, SSM_CONV - 1 - k, axis=0), 0.0) for k in range(SSM_CONV - 1)]
    return taps + [x]


def _conv_pre(taps, w_ref, b_ref):
    acc = b_ref[...]
    for k in range(SSM_CONV):
        acc = acc + taps[k] * w_ref[k:k + 1, :]
    return acc


def _conv_fwd(proj, w, b, *, bsz, name):
    t = proj.shape[0]
    s = t // bsz
    nj = SSM_CONV_DIM // CONV_TC
    c0 = XBC_COL0 // CONV_TC

    def body(x_ref, w_ref, b_ref, o_ref):
        pre = _conv_pre(_conv_taps(x_ref[...].astype(F32)), w_ref, b_ref)
        o_ref[...] = pre * _sigmoid(pre)

    return pl.pallas_call(
        body, grid=(bsz, nj),
        in_specs=[pl.BlockSpec((s, CONV_TC), lambda bb, j: (bb, c0 + j)), pl.BlockSpec((SSM_CONV, CONV_TC), lambda bb, j: (0, j)),
                  pl.BlockSpec((1, CONV_TC), lambda bb, j: (0, j))],
        out_specs=pl.BlockSpec((s, CONV_TC), lambda bb, j: (bb, j)),
        out_shape=jax.ShapeDtypeStruct((t, SSM_CONV_DIM), F32),
        compiler_params=_params("parallel", "parallel"), name=name)(proj, w, b.reshape(1, -1))


def _conv_bwd(proj, dact, w, b, dproj, *, bsz, name):
    t = proj.shape[0]
    s = t // bsz
    nj = SSM_CONV_DIM // CONV_TC
    c0 = XBC_COL0 // CONV_TC

    def body(x_ref, d_ref, w_ref, b_ref, _, dx_ref, dw_ref, db_ref):
        @pl.when(pl.program_id(1) == 0)
        def _():
            dw_ref[...] = jnp.zeros_like(dw_ref)
            db_ref[...] = jnp.zeros_like(db_ref)

        taps = _conv_taps(x_ref[...].astype(F32))
        pre = _conv_pre(taps, w_ref, b_ref)
        sg = _sigmoid(pre)
        dpre = d_ref[...] * (sg * (1.0 + pre * (1.0 - sg)))
        rows = lax.broadcasted_iota(jnp.int32, dpre.shape, 0)
        db_ref[...] += jnp.sum(dpre, axis=0, keepdims=True)
        dx = dpre * w_ref[SSM_CONV - 1:SSM_CONV, :]
        for k in range(SSM_CONV):
            dw_ref[k:k + 1, :] += jnp.sum(dpre * taps[k], axis=0, keepdims=True)
        for k in range(SSM_CONV - 1):
            sh = SSM_CONV - 1 - k
            dsh = jnp.where(rows < s - sh, pltpu.roll(dpre, s - sh, axis=0), 0.0)
            dx = dx + dsh * w_ref[k:k + 1, :]
        dx_ref[...] = dx.astype(MXU_DTYPE)

    return pl.pallas_call(
        body, grid=(nj, bsz),
        in_specs=[pl.BlockSpec((s, CONV_TC), lambda j, bb: (bb, c0 + j)), pl.BlockSpec((s, CONV_TC), lambda j, bb: (bb, j)),
                  pl.BlockSpec((SSM_CONV, CONV_TC), lambda j, bb: (0, j)), pl.BlockSpec((1, CONV_TC), lambda j, bb: (0, j)), _ANY],
        out_specs=[pl.BlockSpec((s, CONV_TC), lambda j, bb: (bb, c0 + j)), pl.BlockSpec((SSM_CONV, CONV_TC), lambda j, bb: (0, j)),
                   pl.BlockSpec((1, CONV_TC), lambda j, bb: (0, j))],
        out_shape=[jax.ShapeDtypeStruct(dproj.shape, dproj.dtype), jax.ShapeDtypeStruct((SSM_CONV, SSM_CONV_DIM), F32),
                   jax.ShapeDtypeStruct((1, SSM_CONV_DIM), F32)],
        input_output_aliases={4: 0},
        compiler_params=_params("parallel", "arbitrary"), name=name)(proj, dact, w, b.reshape(1, -1), dproj)


def _softplus(x):
    return jnp.maximum(x, 0.0) + jnp.log1p(jnp.exp(-jnp.abs(x)))


def _pad_heads(v):
    return jnp.broadcast_to(jnp.pad(v.astype(F32), (0, HEAD_PAD - SSM_HEADS))[None, :], (SUBLANE, HEAD_PAD))


def _ssd_prep(dt_raw, dt_bias8, a_log8, *, name):
    t = dt_raw.shape[0]
    n = CHUNK

    def body(r_ref, b_ref, al_ref, dt_ref, cs_ref, dtt_ref, cst_ref):
        dt = _softplus(r_ref[...] + b_ref[0:1, :])
        da = dt * (-jnp.exp(al_ref[0:1, :]))
        row = lax.broadcasted_iota(jnp.int32, (n, n), 0)
        col = lax.broadcasted_iota(jnp.int32, (n, n), 1)
        lower = (col <= row).astype(F32)
        upper = (row <= col).astype(F32)
        eye = (row == col).astype(F32)
        dt_ref[...] = dt
        cs_ref[...] = _dot_exact(lower, da, _DN_NN, 1)
        cst_ref[0] = _dot_exact(da, upper, _DN_TN, 0)
        dtt_ref[0] = _dot_exact(dt, eye, _DN_TN, 0)

    hp = HEAD_PAD
    return pl.pallas_call(
        body, grid=(t // n,),
        in_specs=[_row_spec(n, hp), _par_spec((SUBLANE, hp)), _par_spec((SUBLANE, hp))],
        out_specs=[_row_spec(n, hp), _row_spec(n, hp), pl.BlockSpec((1, hp, n), lambda i: (i, 0, 0)),
                   pl.BlockSpec((1, hp, n), lambda i: (i, 0, 0))],
        out_shape=[jax.ShapeDtypeStruct((t, hp), F32), jax.ShapeDtypeStruct((t, hp), F32),
                   jax.ShapeDtypeStruct((t // n, hp, n), F32), jax.ShapeDtypeStruct((t // n, hp, n), F32)],
        compiler_params=_params("parallel"), name=name)(dt_raw, dt_bias8, a_log8)


def _expand_mat():
    h = lax.broadcasted_iota(jnp.int32, (HEAD_PAD, SSM_INNER), 0)
    ch = lax.broadcasted_iota(jnp.int32, (HEAD_PAD, SSM_INNER), 1)
    return (ch // SSM_HEADDIM == h).astype(F32)


def _reduce_mat():
    ch = lax.broadcasted_iota(jnp.int32, (SSM_INNER, HEAD_PAD), 0)
    h = lax.broadcasted_iota(jnp.int32, (SSM_INNER, HEAD_PAD), 1)
    return (ch // SSM_HEADDIM == h).astype(F32)


def _expand(v, em):
    return _dot_exact(v, em, _DN_NN, 0)


def _expand_heads(v):
    return jnp.repeat(v.astype(F32), SSM_HEADDIM)[None, :]


def _decay_mat(cs_ref, cst_ref, h, mask):
    seg = cs_ref[:, h:h + 1] - cst_ref[0, h:h + 1, :]
    return jnp.where(mask, jnp.exp(jnp.minimum(seg, 0.0)), 0.0)


GROUP_CH = SSM_INNER // SSM_GROUPS
PAIRS_PER_GROUP = GROUP_CH // LANE
HEADS_PER_GROUP = SSM_HEADS // SSM_GROUPS
BM_COL0 = SSM_INNER
CM_COL0 = SSM_INNER + SSM_GROUPS * SSM_STATE


def _ssd_specs(nc, rev):
    def cidx(i):
        return (i // nc) * nc + (nc - 1 - i % nc) if rev else i

    n = CHUNK
    xs = pl.BlockSpec((n, SSM_INNER), lambda i: (cidx(i), 0))
    bm = pl.BlockSpec((n, GROUP_CH), lambda i: (cidx(i), BM_COL0 // GROUP_CH))
    cm = pl.BlockSpec((n, GROUP_CH), lambda i: (cidx(i), CM_COL0 // GROUP_CH))
    hv = pl.BlockSpec((n, HEAD_PAD), lambda i: (cidx(i), 0))
    hvt = pl.BlockSpec((1, HEAD_PAD, n), lambda i: (cidx(i), 0, 0))
    st = pl.BlockSpec((1, SSM_INNER, SSM_STATE), lambda i: (cidx(i), 0, 0))
    return xs, bm, cm, hv, hvt, st


def _ssd_fwd(xbc, dt, cs, dtt, cst, dskx, *, nc, name):
    t = xbc.shape[0]
    n = CHUNK
    xs_s, bm_s, cm_s, hv_s, hvt_s, st_s = _ssd_specs(nc, False)

    def body(xs_ref, bm_ref, cm_ref, dt_ref, cs_ref, dtt_ref, cst_ref, dsk_ref, y_ref, st_ref, prev):
        @pl.when(pl.program_id(0) % nc == 0)
        def _():
            prev[...] = jnp.zeros_like(prev)

        st_ref[0] = prev[...]
        em = _expand_mat()
        dtx = _expand(dt_ref[...], em)
        csx = _expand(cs_ref[...], em)
        dskx = dsk_ref[...]
        xs = xs_ref[...]
        xdt = xs * dtx
        ecs = jnp.exp(csx)
        dec = jnp.exp(csx[n - 1:n, :] - csx)
        mask = _causal(n)
        lane = lax.broadcasted_iota(jnp.int32, (n, LANE), 1)
        for g in range(SSM_GROUPS):
            gs = slice(g * SSM_STATE, (g + 1) * SSM_STATE)
            gc = slice(g * GROUP_CH, (g + 1) * GROUP_CH)
            cmat = cm_ref[:, gs].astype(MXU_DTYPE)
            bmat = bm_ref[:, gs].astype(MXU_DTYPE)
            cb = _nt(cmat, bmat)
            yoff = ecs[:, gc] * _nt(cmat, prev[gc, :].astype(MXU_DTYPE))
            for q in range(PAIRS_PER_GROUP):
                hp = g * PAIRS_PER_GROUP + q
                sl = slice(hp * LANE, (hp + 1) * LANE)
                xp = xdt[:, sl].astype(MXU_DTYPE)
                m0 = (cb * _decay_mat(cs_ref, cst_ref, 2 * hp, mask)).astype(MXU_DTYPE)
                m1 = (cb * _decay_mat(cs_ref, cst_ref, 2 * hp + 1, mask)).astype(MXU_DTYPE)
                yd = jnp.where(lane < SSM_HEADDIM, _nn(m0, xp), _nn(m1, xp))
                y_ref[:, sl] = yd + yoff[:, q * LANE:(q + 1) * LANE] + xs[:, sl] * dskx[:, sl]
            snew = _tn((xdt[:, gc] * dec[:, gc]).astype(MXU_DTYPE), bmat)
            for r in range(HEADS_PER_GROUP):
                h = g * HEADS_PER_GROUP + r
                rows = slice(h * SSM_HEADDIM, (h + 1) * SSM_HEADDIM)
                e = jnp.exp(cst_ref[0, h:h + 1, n - 1:n])
                prev[rows, :] = prev[rows, :] * e + snew[r * SSM_HEADDIM:(r + 1) * SSM_HEADDIM, :]

    return pl.pallas_call(
        body, grid=(t // n,),
        in_specs=[xs_s, bm_s, cm_s, hv_s, hv_s, hvt_s, hvt_s, _par_spec((1, SSM_INNER))],
        out_specs=[xs_s, st_s],
        out_shape=[jax.ShapeDtypeStruct((t, SSM_INNER), F32), jax.ShapeDtypeStruct((t // n, SSM_INNER, SSM_STATE), F32)],
        scratch_shapes=[pltpu.VMEM((SSM_INNER, SSM_STATE), F32)],
        compiler_params=_params("arbitrary"), name=name)(xbc, xbc, xbc, dt, cs, dtt, cst, dskx)


def _ssd_bwd(dy, xbc, dt, cs, dtt, cst, st, dskx, a_log8, dt_raw, dt_bias8, *, nc, name):
    t = xbc.shape[0]
    n = CHUNK
    xs_s, bm_s, cm_s, hv_s, hvt_s, st_s = _ssd_specs(nc, True)
    acc_s = _par_spec((1, HEAD_PAD))
    xbc_s = pl.BlockSpec((n, SSM_CONV_DIM), xs_s.index_map)

    def body(dy_ref, xs_ref, bm_ref, cm_ref, dt_ref, cs_ref, dtt_ref, cst_ref, st_ref, dsk_ref, al_ref, raw_ref, bias_ref,
             dxbc_ref, ddr_ref, dal_ref, dds_ref, dbias_ref, dprev, dxdt_s, tdec_s, tcs_s):
        @pl.when(pl.program_id(0) % nc == 0)
        def _():
            dprev[...] = jnp.zeros_like(dprev)

        @pl.when(pl.program_id(0) == 0)
        def _():
            dal_ref[...] = jnp.zeros_like(dal_ref)
            dds_ref[...] = jnp.zeros_like(dds_ref)
            dbias_ref[...] = jnp.zeros_like(dbias_ref)

        em = _expand_mat()
        rm = _reduce_mat()

        def head_reduce(v):
            return _dot_exact(v, rm, _DN_NN, 0)

        dtv = dt_ref[...]
        csv = cs_ref[...]
        dtx = _expand(dtv, em)
        csx = _expand(csv, em)
        dskx = dsk_ref[...]
        xs = xs_ref[...]
        dyv = dy_ref[...]
        xdt = xs * dtx
        ecs = jnp.exp(csx)
        dec = jnp.exp(csx[n - 1:n, :] - csx)
        mask = _causal(n)
        lane = lax.broadcasted_iota(jnp.int32, (n, LANE), 1)
        hlane = lax.broadcasted_iota(jnp.int32, (1, HEAD_PAD), 1)
        hsub = lax.broadcasted_iota(jnp.int32, (HEAD_PAD, 1), 0)
        rsum = jnp.zeros((n, HEAD_PAD), F32)
        csum = jnp.zeros((HEAD_PAD, n), F32)
        for g in range(SSM_GROUPS):
            gs = slice(g * SSM_STATE, (g + 1) * SSM_STATE)
            gc = slice(g * GROUP_CH, (g + 1) * GROUP_CH)
            cmat = cm_ref[:, gs].astype(MXU_DTYPE)
            bmat = bm_ref[:, gs].astype(MXU_DTYPE)
            cb = _nt(cmat, bmat)
            pg = st_ref[0, gc, :].astype(MXU_DTYPE)
            dpg = dprev[gc, :]
            dpgb = dpg.astype(MXU_DTYPE)
            z = _nt(cmat, pg)
            dyg = dyv[:, gc]
            dz = (dyg * ecs[:, gc]).astype(MXU_DTYPE)
            dc = _nn(dz, pg)
            dprev_y = _tn(dz, cmat)
            tcs_s[:, gc] = dyg * z * ecs[:, gc]
            xd = xdt[:, gc] * dec[:, gc]
            wmat = _nt(bmat, dpgb)
            db = _nn(xd.astype(MXU_DTYPE), dpgb)
            tdec_s[:, gc] = wmat * xd
            dxdt_g = wmat * dec[:, gc]
            dcb = jnp.zeros((n, n), F32)
            for q in range(PAIRS_PER_GROUP):
                hp = g * PAIRS_PER_GROUP + q
                sl = slice(hp * LANE, (hp + 1) * LANE)
                xp = xdt[:, sl].astype(MXU_DTYPE)
                dyp = dyv[:, sl]
                dypb = dyp.astype(MXU_DTYPE)
                dxp = None
                for hh in range(2):
                    h = 2 * hp + hh
                    lm = _decay_mat(cs_ref, cst_ref, h, mask)
                    mine = (lane < SSM_HEADDIM) if hh == 0 else (lane >= SSM_HEADDIM)
                    dm = _nt(jnp.where(mine, dyp, 0.0).astype(MXU_DTYPE), xp)
                    dml = dm * lm
                    dcb = dcb + dml
                    gseg = dml * cb
                    rsum = rsum + jnp.sum(gseg, axis=1, keepdims=True) * (hlane == h).astype(F32)
                    csum = csum + (hsub == h).astype(F32) * jnp.sum(gseg, axis=0, keepdims=True)
                    dxh = _tn((cb * lm).astype(MXU_DTYPE), dypb)
                    dxp = dxh if dxp is None else jnp.where(mine, dxh, dxp)
                dxdt_s[:, sl] = dxdt_g[:, q * LANE:(q + 1) * LANE] + dxp
            dcbb = dcb.astype(MXU_DTYPE)
            dxbc_ref[:, CM_COL0 + g * SSM_STATE:CM_COL0 + (g + 1) * SSM_STATE] = dc + _nn(dcbb, bmat)
            dxbc_ref[:, BM_COL0 + g * SSM_STATE:BM_COL0 + (g + 1) * SSM_STATE] = db + _tn(dcbb, cmat)
            for r in range(HEADS_PER_GROUP):
                h = g * HEADS_PER_GROUP + r
                rows = slice(h * SSM_HEADDIM, (h + 1) * SSM_HEADDIM)
                lr = slice(r * SSM_HEADDIM, (r + 1) * SSM_HEADDIM)
                e = jnp.exp(cst_ref[0, h:h + 1, n - 1:n])
                dprev[rows, :] = dpg[lr, :] * e + dprev_y[lr, :]
            tq = _dot_exact(dpg * st_ref[0, gc, :], rm[gc, :], _DN_TN, 0)
            if g == 0:
                qsum = jnp.sum(tq, axis=0, keepdims=True)
            else:
                qsum = qsum + jnp.sum(tq, axis=0, keepdims=True)
        dxdt = dxdt_s[...]
        dxbc_ref[:, 0:SSM_INNER] = dxdt * dtx + dyv * dskx
        ddt = head_reduce(dxdt * xs)
        edec = head_reduce(tdec_s[...])
        ycs = head_reduce(tcs_s[...])
        row = lax.broadcasted_iota(jnp.int32, (n, HEAD_PAD), 0)
        extra = jnp.sum(edec, axis=0, keepdims=True) + qsum * jnp.exp(csv[n - 1:n, :])
        dcs = rsum - csum.T + ycs - edec + jnp.where(row == n - 1, extra, 0.0)
        r2 = lax.broadcasted_iota(jnp.int32, (n, n), 0)
        c2 = lax.broadcasted_iota(jnp.int32, (n, n), 1)
        dda = _dot_exact((c2 >= r2).astype(F32), dcs, _DN_NN, 1)
        a_row = -jnp.exp(al_ref[0:1, :])
        ddt = ddt + dda * a_row
        dal_ref[...] += jnp.sum(dda * dtv, axis=0, keepdims=True) * a_row
        dds_ref[...] += jnp.sum(head_reduce(dyv * xs), axis=0, keepdims=True)
        ddr = ddt * _sigmoid(raw_ref[...] + bias_ref[0:1, :])
        ddr_ref[...] = ddr
        dbias_ref[...] += jnp.sum(ddr, axis=0, keepdims=True)

    par8 = _par_spec((SUBLANE, HEAD_PAD))
    return pl.pallas_call(
        body, grid=(t // n,),
        in_specs=[xs_s, xs_s, bm_s, cm_s, hv_s, hv_s, hvt_s, hvt_s, st_s, _par_spec((1, SSM_INNER)), par8, hv_s, par8],
        out_specs=[xbc_s, hv_s, acc_s, acc_s, acc_s],
        out_shape=[jax.ShapeDtypeStruct((t, SSM_CONV_DIM), F32), jax.ShapeDtypeStruct((t, HEAD_PAD), F32),
                   jax.ShapeDtypeStruct((1, HEAD_PAD), F32), jax.ShapeDtypeStruct((1, HEAD_PAD), F32),
                   jax.ShapeDtypeStruct((1, HEAD_PAD), F32)],
        scratch_shapes=[pltpu.VMEM((SSM_INNER, SSM_STATE), F32), pltpu.VMEM((n, SSM_INNER), F32),
                        pltpu.VMEM((n, SSM_INNER), F32), pltpu.VMEM((n, SSM_INNER), F32)],
        compiler_params=_params("arbitrary"), name=name)(dy, xbc, xbc, xbc, dt, cs, dtt, cst, st, dskx, a_log8, dt_raw, dt_bias8)


def _gate_norm_fwd(y, proj, norm_g, *, name):
    t, c = y.shape
    tm = _pick(t, (256, 128))

    def body(y_ref, z_ref, g_ref, o_ref):
        z = z_ref[...].astype(F32)
        yz = y_ref[...] * z * _sigmoid(z)
        for g in range(SSM_GROUPS):
            gc = slice(g * GROUP_CH, (g + 1) * GROUP_CH)
            seg = yz[:, gc]
            r = lax.rsqrt(jnp.mean(seg * seg, axis=-1, keepdims=True) + RMS_EPS)
            o_ref[:, gc] = (seg * r * g_ref[:, gc]).astype(MXU_DTYPE)

    return pl.pallas_call(
        body, grid=(t // tm,), in_specs=[_row_spec(tm, c), _row_spec(tm, c, 1), _par_spec((1, c))],
        out_specs=_row_spec(tm, c), out_shape=jax.ShapeDtypeStruct((t, c), MXU_DTYPE),
        compiler_params=_params("parallel"), name=name)(y, proj, norm_g.reshape(1, c))


def _gate_norm_bwd(dyb, y, proj, norm_g, dproj, *, name):
    t, c = y.shape
    tm = _pick(t, (256, 128))

    def body(d_ref, y_ref, z_ref, g_ref, _, dy_ref, dz_ref, dg_ref):
        @pl.when(pl.program_id(0) == 0)
        def _():
            dg_ref[...] = jnp.zeros_like(dg_ref)

        z = z_ref[...].astype(F32)
        yv = y_ref[...]
        sz = _sigmoid(z)
        silu = z * sz
        yz = yv * silu
        dv = d_ref[...].astype(F32)
        for g in range(SSM_GROUPS):
            gc = slice(g * GROUP_CH, (g + 1) * GROUP_CH)
            seg = yz[:, gc]
            r = lax.rsqrt(jnp.mean(seg * seg, axis=-1, keepdims=True) + RMS_EPS)
            nrm = seg * r
            dn = dv[:, gc] * g_ref[:, gc]
            dg_ref[:, gc] += jnp.sum(dv[:, gc] * nrm, axis=0, keepdims=True)
            dyz = r * (dn - nrm * jnp.mean(dn * nrm, axis=-1, keepdims=True))
            dy_ref[:, gc] = dyz * silu[:, gc]
            dz_ref[:, gc] = (dyz * yv[:, gc] * (sz[:, gc] * (1.0 + z[:, gc] * (1.0 - sz[:, gc])))).astype(MXU_DTYPE)

    return pl.pallas_call(
        body, grid=(t // tm,), in_specs=[_row_spec(tm, c), _row_spec(tm, c), _row_spec(tm, c, 1), _par_spec((1, c)), _ANY],
        out_specs=[_row_spec(tm, c), _row_spec(tm, c, 1), _par_spec((1, c))],
        out_shape=[jax.ShapeDtypeStruct((t, c), F32), jax.ShapeDtypeStruct(dproj.shape, dproj.dtype),
                   jax.ShapeDtypeStruct((1, c), F32)],
        input_output_aliases={4: 1},
        compiler_params=_params("arbitrary"), name=name)(dyb, y, proj, norm_g.reshape(1, c), dproj)


GA_COLBLK = GAB_COL0 // D_MODEL


def _merge_fwd(br_a, br_b, proj, *, name):
    t, c = br_a.shape
    tm = _pick(t, ROW_TILES)

    def body(a_ref, b_ref, ga_ref, gb_ref, o_ref):
        o_ref[...] = (_sigmoid(ga_ref[...].astype(F32)) * a_ref[...].astype(F32)
                      + _sigmoid(gb_ref[...].astype(F32)) * b_ref[...].astype(F32)).astype(MXU_DTYPE)

    return pl.pallas_call(
        body, grid=(t // tm,),
        in_specs=[_row_spec(tm, c), _row_spec(tm, c), _row_spec(tm, c, GA_COLBLK), _row_spec(tm, c, GA_COLBLK + 1)],
        out_specs=_row_spec(tm, c), out_shape=jax.ShapeDtypeStruct((t, c), MXU_DTYPE),
        compiler_params=_params("parallel"), name=name)(br_a, br_b, proj, proj)


def _merge_bwd(dm, br_a, br_b, proj, *, name):
    t, c = br_a.shape
    tm = _pick(t, ROW_TILES)

    def body(dm_ref, a_ref, b_ref, ga_ref, gb_ref, da_ref, db_ref, dg_ref):
        d = dm_ref[...].astype(F32)
        sa = _sigmoid(ga_ref[...].astype(F32))
        sb = _sigmoid(gb_ref[...].astype(F32))
        da_ref[...] = (d * sa).astype(MXU_DTYPE)
        db_ref[...] = (d * sb).astype(MXU_DTYPE)
        dg_ref[:, :c] = (d * a_ref[...].astype(F32) * sa * (1.0 - sa)).astype(MXU_DTYPE)
        dg_ref[:, c:] = (d * b_ref[...].astype(F32) * sb * (1.0 - sb)).astype(MXU_DTYPE)

    return pl.pallas_call(
        body, grid=(t // tm,),
        in_specs=[_row_spec(tm, c), _row_spec(tm, c), _row_spec(tm, c), _row_spec(tm, c, GA_COLBLK), _row_spec(tm, c, GA_COLBLK + 1)],
        out_specs=[_row_spec(tm, c), _row_spec(tm, c), _row_spec(tm, 2 * c, GAB_COL0 // (2 * c))],
        out_shape=[jax.ShapeDtypeStruct((t, c), MXU_DTYPE), jax.ShapeDtypeStruct((t, c), MXU_DTYPE),
                   jax.ShapeDtypeStruct((t, MAIN_COLS), MXU_DTYPE)],
        compiler_params=_params("parallel"), name=name)(dm, br_a, br_b, proj, proj)


def _layer_fwd(x, xb, memn_b, w, *, bsz, tag):
    nc = x.shape[0] // bsz // CHUNK
    sv = {"x_in": xb}
    proj = _mm(xb, w["w_main"], out_dtype=STASH_DTYPE, name=f"{tag}_proj")
    dt_raw = _mm(xb, w["w_dt"], name=f"{tag}_dtproj")
    sgo = _sg_fwd(proj, w["sg_ln_g"], w["sg_ln_b"], w["sg_w"], w["sg_bcol"], name=f"{tag}_sg_fwd")
    xbc = _conv_fwd(proj, w["conv_w"], w["conv_b"], bsz=bsz, name=f"{tag}_conv_fwd")
    dt, cs, dtt, cst = _ssd_prep(dt_raw, w["dt_bias8"], w["a_log8"], name=f"{tag}_ssd_prep")
    y, st = _ssd_fwd(xbc, dt, cs, dtt, cst, w["d_skipx"], nc=nc, name=f"{tag}_ssd_fwd")
    yb = _gate_norm_fwd(y, proj, w["ssm_norm_g"], name=f"{tag}_gate_norm_fwd")
    if "rest" in w:
        w = w["rest"](w, yb)
    br_a = _mm(sgo, w["p_a"], out_dtype=STASH_DTYPE, name=f"{tag}_br_a")
    br_b = _mm(yb, w["p_b"], out_dtype=STASH_DTYPE, name=f"{tag}_br_b")
    merged = _merge_fwd(br_a, br_b, proj, name=f"{tag}_merge_fwd")
    mix = _mm(merged, w["w_mix_o"], name=f"{tag}_mix_o")
    x1, x1b, xh1, rs1 = _ln_fwd(x, mix, w["ln_g"][0], w["ln_b"][0], name=f"{tag}_ln1_fwd")
    sv.update(proj=proj, dt_raw=dt_raw, sgo=sgo, xbc=xbc, dt=dt, cs=cs, dtt=dtt, cst=cst, y=y, st=st, yb=yb,
              br_a=br_a, br_b=br_b, merged=merged, xh1=xh1, rs1=rs1, x1b=x1b)
    q = _mm(x1b, w["w_xq"], out_dtype=MXU_DTYPE, name=f"{tag}_q")
    kv = _mm(memn_b, w["w_xkv"], out_dtype=MXU_DTYPE, name=f"{tag}_kv")
    o = _attn_fwd(q, kv, bsz=bsz, name=f"{tag}_attn_fwd")
    att = _mm(o, w["w_xo"], name=f"{tag}_xo")
    x2, x2b, xh2, rs2 = _ln_fwd(x1, att, w["ln_g"][1], w["ln_b"][1], name=f"{tag}_ln2_fwd")
    sv.update(q=q, kv=kv, o=o, xh2=xh2, rs2=rs2, x2b=x2b)
    h = _mm(x2b, w["w_ffn_in"], out_dtype=STASH_DTYPE, name=f"{tag}_ffn_in")
    a = _swiglu_fwd(h, name=f"{tag}_swiglu_fwd")
    ffn = _mm(a, w["w_ffn_out"], name=f"{tag}_ffn_out")
    x3, x3b, xh3, rs3 = _ln_fwd(x2, ffn, w["ln_g"][2], w["ln_b"][2], name=f"{tag}_ln3_fwd")
    sv.update(h=h, a=a, xh3=xh3, rs3=rs3)
    return x3, x3b, sv, w


GRAD_GROUPS = (("w_ffn_out", "w_ffn_in", "w_xo", "w_xq", "w_xkv"), ("w_mix_o", "p_a", "p_b"), ("w_in",))


def _layer_bwd(dx3_addends, dx3_scales, memn_b, w, sv, on_group=None, *, bsz, tag):
    nc = sv["xh1"].shape[0] // bsz // CHUNK
    gr = {}

    def group_done(k):
        return on_group(GRAD_GROUPS[k], gr) if on_group is not None else None
    dp3, dp3b, dg3, db3 = _ln_bwd(dx3_addends, dx3_scales, sv["xh3"], sv["rs3"], w["ln_g"][2], name=f"{tag}_ln3_bwd")
    da = _mm(dp3b, w["w_ffn_out"], tb=True, out_dtype=STASH_DTYPE, name=f"{tag}_d_a")
    gr["w_ffn_out"] = _mm(sv["a"], dp3b, ta=True, name=f"{tag}_dw_ffn_out")
    dh = _swiglu_bwd(sv["h"], da, name=f"{tag}_swiglu_bwd")
    gr["w_ffn_in"] = _mm(sv["x2b"], dh, ta=True, name=f"{tag}_dw_ffn_in")
    dx2_br = _mm(dh, w["w_ffn_in"], tb=True, name=f"{tag}_dx2")
    dp2, dp2b, dg2, db2 = _ln_bwd([dp3, dx2_br], [ALPHA, 1.0], sv["xh2"], sv["rs2"], w["ln_g"][1], name=f"{tag}_ln2_bwd")
    do = _mm(dp2b, w["w_xo"], tb=True, out_dtype=MXU_DTYPE, name=f"{tag}_d_o")
    gr["w_xo"] = _mm(sv["o"], dp2b, ta=True, name=f"{tag}_dw_xo")
    dq, dk, dv = _attn_bwd(sv["q"], sv["kv"], do, bsz=bsz, name=f"{tag}_attn_bwd")
    dkv = jnp.concatenate([dk, dv], axis=1)
    gr["w_xq"] = _mm(sv["x1b"], dq, ta=True, name=f"{tag}_dw_xq")
    gr["w_xkv"] = _mm(memn_b, dkv, ta=True, name=f"{tag}_dw_xkv")
    dmemn = _mm(dkv, w["w_xkv"], tb=True, name=f"{tag}_d_memn")
    dx1_br = _mm(dq, w["w_xq"], tb=True, name=f"{tag}_dx1")
    token = group_done(0)
    ln_g1 = w["ln_g"][0] if token is None else w["ln_g"][0] + token[0, 0]
    dp1, dp1b, dg1, db1 = _ln_bwd([dp2, dx1_br], [ALPHA, 1.0], sv["xh1"], sv["rs1"], ln_g1, name=f"{tag}_ln1_bwd")
    gr["ln_g"] = jnp.concatenate([dg1, dg2, dg3], axis=0)
    gr["ln_b"] = jnp.concatenate([db1, db2, db3], axis=0)
    dmerged = _mm(dp1b, w["w_mix_o"], tb=True, out_dtype=STASH_DTYPE, name=f"{tag}_d_merged")
    gr["w_mix_o"] = _mm(sv["merged"], dp1b, ta=True, name=f"{tag}_dw_mix_o")
    dbr_a, dbr_b, dproj = _merge_bwd(dmerged, sv["br_a"], sv["br_b"], sv["proj"], name=f"{tag}_merge_bwd")
    gr["p_a"] = _mm(sv["sgo"], dbr_a, ta=True, name=f"{tag}_dw_p_a")
    gr["p_b"] = _mm(sv["yb"], dbr_b, ta=True, name=f"{tag}_dw_p_b")
    dsgo = _mm(dbr_a, w["p_a"], tb=True, out_dtype=STASH_DTYPE, name=f"{tag}_d_sgo")
    dyb = _mm(dbr_b, w["p_b"], tb=True, out_dtype=STASH_DTYPE, name=f"{tag}_d_yb")
    token = group_done(1)
    norm_g = w["ssm_norm_g"] if token is None else w["ssm_norm_g"] + token[0, 0]
    dy, dproj, gr["ssm_norm_g"] = _gate_norm_bwd(dyb, sv["y"], sv["proj"], norm_g, dproj, name=f"{tag}_gate_norm_bwd")
    dxbc, ddr, gr["a_log"], gr["d_skip"], gr["dt_bias"] = _ssd_bwd(
        dy, sv["xbc"], sv["dt"], sv["cs"], sv["dtt"], sv["cst"], sv["st"], w["d_skipx"], w["a_log8"], sv["dt_raw"],
        w["dt_bias8"], nc=nc, name=f"{tag}_ssd_bwd")
    dproj, gr["conv_w"], gr["conv_b"] = _conv_bwd(sv["proj"], dxbc, w["conv_w"], w["conv_b"], dproj, bsz=bsz, name=f"{tag}_conv_bwd")
    dproj, gr["sg_w"], dsg_bcol, gr["sg_ln_g"], gr["sg_ln_b"] = _sg_bwd(
        sv["proj"], dsgo, w["sg_ln_g"], w["sg_ln_b"], w["sg_w"], w["sg_bcol"], dproj, name=f"{tag}_sg_bwd")
    gr["sg_b"] = dsg_bcol[..., 0]
    gr["w_main"] = _mm(sv["x_in"], dproj, ta=True, name=f"{tag}_dw_main")
    gr["w_dt"] = _mm(sv["x_in"], ddr, ta=True, name=f"{tag}_dw_dt")
    token = group_done(2)
    dx_dt = _mm(ddr, w["w_dt"], tb=True, after=token, name=f"{tag}_dx_dt")
    dx_main = _mm(dproj, w["w_main"], tb=True, after=token, name=f"{tag}_dx_main")
    return [dp1, dx_main, dx_dt], [ALPHA, 1.0, 1.0], gr, dmemn


def _local_step(x, mem, tgt, mem_ln_g, mem_ln_b, layers, on_layer_grads=None):
    bsz, s, d = x.shape
    xf = x.reshape(bsz * s, d)
    memf = mem.reshape(-1, d)
    _, memn_b, mxh, mrs = _ln_fwd(memf, None, mem_ln_g, mem_ln_b, name="mem_ln_fwd")
    cur, curb, saved, weights = xf, xf, [], []
    for li, get_weights in enumerate(layers):
        cur, curb, sv, w = _layer_fwd(cur, curb, memn_b, get_weights(cur), bsz=bsz, tag=f"l{li}")
        saved.append(sv)
        weights.append(w)
    dy, lsum = _loss_head(cur, tgt.reshape(bsz * s, d), name="loss_head")
    addends, scales = [dy], [1.0]
    grads, dmem = [None] * len(layers), []
    for li in reversed(range(len(layers))):
        on_group = None if on_layer_grads is None else functools.partial(on_layer_grads, li)
        addends, scales, grads[li], dm = _layer_bwd(addends, scales, memn_b, weights[li], saved[li], on_group, bsz=bsz, tag=f"l{li}")
        dmem.append(dm)
    grad_x = _add_scaled(addends, scales, name="grad_x").reshape(bsz, s, d)
    _, _, dmg, dmb = _ln_bwd(dmem, [1.0] * len(dmem), mxh, mrs, mem_ln_g, name="mem_ln_bwd")
    return lsum, grad_x, grads, dmg[0], dmb[0]


_ANY = pl.BlockSpec(memory_space=pl.ANY)
_MESH = pl.DeviceIdType.MESH


def _all_gather8(x, *, name):
    def body(x_ref, out_ref, send_sems, recv_sems):
        mx, my, mc = lax.axis_index("x"), lax.axis_index("y"), lax.axis_index("c")
        me, sibling = (mx, my, mc), (mx, my, 1 - mc)
        chips = [(1 - mx, my), (mx, 1 - my), (1 - mx, 1 - my)]

        def blk(px, py, pc):
            return out_ref.at[4 * px + 2 * py + pc]

        def copy(k, block, to, src=None):
            return pltpu.make_async_remote_copy(
                src_ref=blk(*block) if src is None else src, dst_ref=blk(*block), send_sem=send_sems.at[k],
                recv_sem=recv_sems.at[k], device_id=to, device_id_type=_MESH)

        first = [copy(0, me, sibling, src=x_ref)]
        first += [copy(1 + j, me, (*chip, mc), src=x_ref) for j, chip in enumerate(chips)]
        for cp in first:
            cp.start()
        passed = [copy(4 + j, (*chip, mc), sibling) for j, chip in enumerate(chips)]
        for j, chip in enumerate(chips):
            copy(1 + j, (*chip, mc), me).wait_recv()
            passed[j].start()
        copy(0, sibling, me).wait_recv()
        for j, chip in enumerate(chips):
            copy(4 + j, (*chip, 1 - mc), me).wait_recv()
        for cp in first + passed:
            cp.wait_send()

    return pl.pallas_call(
        body, out_shape=jax.ShapeDtypeStruct((N_DEV,) + x.shape, x.dtype), in_specs=[_ANY], out_specs=_ANY,
        scratch_shapes=[pltpu.SemaphoreType.DMA((7,)), pltpu.SemaphoreType.DMA((7,))], name=name)(x)


def _row_tile(rows, row_bytes, mult=SUBLANE):
    best = None
    for tr in range(mult, rows + 1, mult):
        if rows % tr == 0 and (best is None or tr * row_bytes <= BLOCK_BYTES):
            best = tr
    return rows if best is None else best


def _gather_shape(r, c, kind):
    return {"row": (2, N_CHIPS * r, c), "col": (2, r, N_CHIPS * c), "chip": (2, N_CHIPS, r, c)}[kind]


def _cast_place(shard, kind, dtype, chip_idx, *, name):
    _, r, c = shard.shape
    tr = _row_tile(r, c * 4, 16)
    nt = r // tr

    def body(_, s_ref, o_ref):
        o_ref[...] = s_ref[...].astype(dtype)

    if kind == "row":
        out_spec = pl.BlockSpec((None, tr, c), lambda l, i, j_ref: (l, j_ref[0] * nt + i, 0))
    elif kind == "col":
        out_spec = pl.BlockSpec((None, tr, c), lambda l, i, j_ref: (l, i, j_ref[0]))
    else:
        out_spec = pl.BlockSpec((None, None, tr, c), lambda l, i, j_ref: (l, j_ref[0], i, 0))
    grid_spec = pltpu.PrefetchScalarGridSpec(
        num_scalar_prefetch=1, grid=(2, nt), in_specs=[pl.BlockSpec((None, tr, c), lambda l, i, j_ref: (l, i, 0))],
        out_specs=out_spec)
    return pl.pallas_call(body, grid_spec=grid_spec, out_shape=jax.ShapeDtypeStruct(_gather_shape(r, c, kind), dtype),
                          compiler_params=_params("parallel", "parallel"), name=name)(chip_idx, shard)


def _gather_params(bufs, shard_shapes, kinds, *, name):
    n = len(bufs)

    def body(*refs):
        outs = refs[n:2 * n]
        send_sems, recv_sems = refs[2 * n:]
        mx, my, mc = lax.axis_index("x"), lax.axis_index("y"), lax.axis_index("c")
        me, sibling = (mx, my, mc), (mx, my, 1 - mc)
        chips = [(1 - mx, my), (mx, 1 - my), (1 - mx, 1 - my)]

        def blk(i, px, py, pc):
            r, c = shard_shapes[i]
            j = 2 * px + py
            if kinds[i] == "row":
                return outs[i].at[pc, pl.ds(pl.multiple_of(j * r, r), r)]
            if kinds[i] == "col":
                return outs[i].at[pc, :, pl.ds(pl.multiple_of(j * c, c), c)]
            return outs[i].at[pc, j]

        def copy(i, k, block, to):
            return pltpu.make_async_remote_copy(
                src_ref=blk(i, *block), dst_ref=blk(i, *block), send_sem=send_sems.at[6 * i + k],
                recv_sem=recv_sems.at[6 * i + k], device_id=to, device_id_type=_MESH)

        sent = []
        for i in range(n):
            for j, chip in enumerate(chips):
                cp = copy(i, j, me, (*chip, mc))
                cp.start()
                sent.append(cp)
        for j, chip in enumerate(chips):
            for i in range(n):
                copy(i, j, (*chip, mc), me).wait_recv()
                fwd = copy(i, 3 + j, (*chip, mc), sibling)
                fwd.start()
                sent.append(fwd)
        for i in range(n):
            for j, chip in enumerate(chips):
                copy(i, 3 + j, (*chip, 1 - mc), me).wait_recv()
        for cp in sent:
            cp.wait_send()

    return pl.pallas_call(
        body, out_shape=[jax.ShapeDtypeStruct(b.shape, b.dtype) for b in bufs], in_specs=[_ANY] * n, out_specs=[_ANY] * n,
        input_output_aliases={i: i for i in range(n)},
        scratch_shapes=[pltpu.SemaphoreType.DMA((6 * n,)), pltpu.SemaphoreType.DMA((6 * n,))], name=name)(*bufs)


def _half(r, h):
    return pl.ds(pl.multiple_of(h * (r // 2), r // 2), r // 2)


_HBM = pl.BlockSpec(memory_space=pltpu.HBM)
_SEM = pl.BlockSpec(memory_space=pltpu.SEMAPHORE)
_EFFECT = pltpu.SideEffectType.DATAFLOW_SIDE_EFFECTING


def _sibling_copies(g_refs, land_refs, gs, views, send_sems, recv_sems):
    mx, my, mc = lax.axis_index("x"), lax.axis_index("y"), lax.axis_index("c")
    copies = []
    for i in range(len(gs)):
        if views[i] == "chip":
            src = g_refs[i].at[:, _half(gs[i].shape[1], 1 - mc)]
        else:
            src = g_refs[i].at[_half(gs[i].shape[0], 1 - mc)]
        copies.append(pltpu.make_async_remote_copy(src_ref=src, dst_ref=land_refs[i], send_sem=send_sems.at[i], recv_sem=recv_sems.at[i],
                                                   device_id=(mx, my, 1 - mc), device_id_type=_MESH))
    return copies


def _half_shape(g, view):
    return (g.shape[0], g.shape[1] // 2, g.shape[2]) if view == "chip" else (g.shape[0] // 2, g.shape[1])


def _grads_to_sibling_start(gs, views, *, name):
    n = len(gs)
    lands = [pltpu.with_memory_space_constraint(lax.empty(_half_shape(g, v), g.dtype), pltpu.HBM) for g, v in zip(gs, views)]

    def body(*refs):
        for cp in _sibling_copies(refs[:n], refs[n:2 * n], gs, views, refs[2 * n], refs[2 * n + 1]):
            cp.start()
        refs[-1][...] = jnp.zeros_like(refs[-1])

    outs = pl.pallas_call(
        body, name=name,
        out_shape=(pltpu.SemaphoreType.DMA((n,)), pltpu.SemaphoreType.DMA((n,)),
                   *[pltpu.HBM(x.shape, x.dtype) for x in list(gs) + lands], jax.ShapeDtypeStruct((SUBLANE, LANE), F32)),
        in_specs=[_HBM] * (2 * n), out_specs=(_SEM, _SEM, *[_HBM] * (2 * n), pl.BlockSpec(memory_space=pltpu.VMEM)),
        input_output_aliases={i: 2 + i for i in range(2 * n)},
        compiler_params=pltpu.CompilerParams(has_side_effects=_EFFECT),
    )(*[pltpu.with_memory_space_constraint(g, pltpu.HBM) for g in gs], *lands)
    return outs[0], outs[1], list(outs[2:2 + n]), list(outs[2 + n:2 + 2 * n]), outs[-1]


def _grads_to_sibling_wait(send_sems, recv_sems, gs, lands, views, after, *, name):
    n = len(gs)

    def body(*refs):
        for cp in _sibling_copies(refs[:n], refs[n:2 * n], gs, views, refs[2 * n], refs[2 * n + 1]):
            cp.wait_send()
            cp.wait_recv()

    outs = pl.pallas_call(
        body, name=name, out_shape=tuple(pltpu.HBM(x.shape, x.dtype) for x in list(gs) + list(lands)),
        in_specs=[_HBM] * (2 * n) + [_SEM, _SEM, _ANY], out_specs=tuple([_HBM] * (2 * n)),
        input_output_aliases={i: i for i in range(2 * n)},
        compiler_params=pltpu.CompilerParams(has_side_effects=_EFFECT),
    )(*gs, *lands, send_sems, recv_sems, after)
    return list(outs[:n]), list(outs[n:])


def _cast_place_layer(shard, l, kind, chip_idx, after, *, name):
    _, r, c = shard.shape
    tr = _row_tile(r, c * 4, 16)
    nt = r // tr

    def body(_, s_ref, *rest):
        rest[-1][...] = s_ref[...].astype(MXU_DTYPE)

    if kind == "row":
        out_spec = pl.BlockSpec((tr, c), lambda i, j_ref: (j_ref[0] * nt + i, 0))
    elif kind == "col":
        out_spec = pl.BlockSpec((tr, c), lambda i, j_ref: (i, j_ref[0]))
    else:
        out_spec = pl.BlockSpec((None, tr, c), lambda i, j_ref: (j_ref[0], i, 0))
    extra = [] if after is None else [after]
    grid_spec = pltpu.PrefetchScalarGridSpec(
        num_scalar_prefetch=1, grid=(nt,), in_specs=[pl.BlockSpec((None, tr, c), lambda i, j_ref: (l, i, 0))] + [_ANY] * len(extra),
        out_specs=out_spec)
    return pl.pallas_call(body, grid_spec=grid_spec, out_shape=jax.ShapeDtypeStruct(_gather_shape(r, c, kind)[1:], MXU_DTYPE),
                          compiler_params=_params("parallel"), name=name)(chip_idx, shard, *extra)


def _half_block(ref, kind, r, c, j, h):
    rows = _half(r, h)
    if kind == "row":
        return ref.at[pl.ds(pl.multiple_of(j * r + h * (r // 2), r // 2), r // 2)]
    if kind == "col":
        return ref.at[rows, pl.ds(pl.multiple_of(j * c, c), c)]
    return ref.at[j, rows]


def _gather_ici_copies(buf_refs, shapes, kinds, send_sems, recv_sems):
    mx, my, mc = lax.axis_index("x"), lax.axis_index("y"), lax.axis_index("c")
    chips = [(1 - mx, my), (mx, 1 - my), (1 - mx, 1 - my)]
    copies = []
    for i, (r, c) in enumerate(shapes):
        mine = _half_block(buf_refs[i], kinds[i], r, c, 2 * mx + my, mc)
        for k, (px, py) in enumerate(chips):
            copies.append(pltpu.make_async_remote_copy(
                src_ref=mine, dst_ref=mine, send_sem=send_sems.at[3 * i + k], recv_sem=recv_sems.at[3 * i + k],
                device_id=(px, py, mc), device_id_type=_MESH))
    return copies


def _gather_start(bufs, shapes, kinds, *, name):
    n = len(bufs)

    def body(*refs):
        send_sems, recv_sems, token = refs[n], refs[n + 1], refs[-1]
        for cp in _gather_ici_copies(refs[:n], shapes, kinds, send_sems, recv_sems):
            cp.start()
        token[...] = jnp.zeros_like(token)

    outs = pl.pallas_call(
        body, name=name,
        out_shape=(pltpu.SemaphoreType.DMA((3 * n,)), pltpu.SemaphoreType.DMA((3 * n,)),
                   *[pltpu.HBM(b.shape, b.dtype) for b in bufs], jax.ShapeDtypeStruct((SUBLANE, LANE), F32)),
        in_specs=[_HBM] * n, out_specs=(_SEM, _SEM, *[_HBM] * n, pl.BlockSpec(memory_space=pltpu.VMEM)),
        input_output_aliases={i: 2 + i for i in range(n)},
        compiler_params=pltpu.CompilerParams(has_side_effects=_EFFECT),
    )(*[pltpu.with_memory_space_constraint(b, pltpu.HBM) for b in bufs])
    return outs[0], outs[1], list(outs[2:2 + n]), outs[-1]


def _gather_wait(send_sems, recv_sems, bufs, shapes, kinds, after, *, name):
    n = len(bufs)

    def body(*refs):
        for cp in _gather_ici_copies(refs[:n], shapes, kinds, refs[n], refs[n + 1]):
            cp.wait_send()
            cp.wait_recv()

    outs = pl.pallas_call(
        body, name=name, out_shape=tuple(pltpu.HBM(b.shape, b.dtype) for b in bufs),
        in_specs=[_HBM] * n + [_SEM, _SEM, _ANY], out_specs=tuple([_HBM] * n), input_output_aliases={i: i for i in range(n)},
        compiler_params=pltpu.CompilerParams(has_side_effects=_EFFECT),
    )(*bufs, send_sems, recv_sems, after)
    return list(outs)


def _gather_forward(bufs, shapes, kinds, *, name):
    n = len(bufs)

    def body(*refs):
        outs = refs[n:2 * n]
        send_sems, recv_sems = refs[2 * n:]
        mx, my, mc = lax.axis_index("x"), lax.axis_index("y"), lax.axis_index("c")
        chips = [(1 - mx, my), (mx, 1 - my), (1 - mx, 1 - my)]
        copies = []
        for i, (r, c) in enumerate(shapes):
            for k, (px, py) in enumerate(chips):
                got = _half_block(outs[i], kinds[i], r, c, 2 * px + py, mc)
                cp = pltpu.make_async_remote_copy(src_ref=got, dst_ref=got, send_sem=send_sems.at[3 * i + k],
                                                  recv_sem=recv_sems.at[3 * i + k], device_id=(mx, my, 1 - mc), device_id_type=_MESH)
                cp.start()
                copies.append(cp)
        for cp in copies:
            cp.wait()

    return pl.pallas_call(
        body, out_shape=[jax.ShapeDtypeStruct(b.shape, b.dtype) for b in bufs], in_specs=[_ANY] * n, out_specs=[_ANY] * n,
        input_output_aliases={i: i for i in range(n)},
        scratch_shapes=[pltpu.SemaphoreType.DMA((3 * n,)), pltpu.SemaphoreType.DMA((3 * n,))], name=name)(*bufs)


def _chip_exchange_copies(pair_refs, land_refs, pairs, views, send_sems, recv_sems):
    mx, my, mc = lax.axis_index("x"), lax.axis_index("y"), lax.axis_index("c")
    me = 2 * mx + my
    chips = [(1 - mx, my), (mx, 1 - my), (1 - mx, 1 - my)]
    copies = []
    for i in range(len(pairs)):
        for k, (px, py) in enumerate(chips):
            j = 2 * px + py
            if views[i] == "chip":
                src = pair_refs[i].at[j]
            else:
                c = pairs[i].shape[1] // N_CHIPS
                src = pair_refs[i].at[:, pl.ds(pl.multiple_of(j * c, c), c)]
            copies.append(pltpu.make_async_remote_copy(
                src_ref=src, dst_ref=land_refs[i].at[me], send_sem=send_sems.at[3 * i + k], recv_sem=recv_sems.at[3 * i + k],
                device_id=(px, py, mc), device_id_type=_MESH))
    return copies


def _quad_shape(p, view):
    return p.shape if view == "chip" else (N_CHIPS, p.shape[0], p.shape[1] // N_CHIPS)


def _grads_to_chips_start(pairs, views, *, name):
    n = len(pairs)
    lands = [pltpu.with_memory_space_constraint(lax.empty(_quad_shape(p, v), p.dtype), pltpu.HBM) for p, v in zip(pairs, views)]

    def body(*refs):
        pair_refs, land_refs = refs[:n], refs[n:2 * n]
        send_sems, recv_sems = refs[2 * n], refs[2 * n + 1]
        token = refs[-1]
        for cp in _chip_exchange_copies(pair_refs, land_refs, pairs, views, send_sems, recv_sems):
            cp.start()
        token[...] = jnp.zeros_like(token)

    outs = pl.pallas_call(
        body, name=name,
        out_shape=(pltpu.SemaphoreType.DMA((3 * n,)), pltpu.SemaphoreType.DMA((3 * n,)),
                   *[pltpu.HBM(p.shape, p.dtype) for p in pairs], *[pltpu.HBM(l.shape, l.dtype) for l in lands],
                   jax.ShapeDtypeStruct((SUBLANE, LANE), F32)),
        in_specs=[_HBM] * (2 * n), out_specs=(_SEM, _SEM, *[_HBM] * (2 * n), pl.BlockSpec(memory_space=pltpu.VMEM)),
        input_output_aliases={i: 2 + i for i in range(2 * n)},
        compiler_params=pltpu.CompilerParams(has_side_effects=_EFFECT),
    )(*[pltpu.with_memory_space_constraint(p, pltpu.HBM) for p in pairs], *lands)
    return outs[0], outs[1], list(outs[2:2 + n]), list(outs[2 + n:2 + 2 * n]), outs[-1]


def _grads_to_chips_wait(send_sems, recv_sems, pairs, lands, views, after, *, name):
    n = len(pairs)

    def body(*refs):
        pair_refs, land_refs = refs[:n], refs[n:2 * n]
        s_sems, r_sems = refs[2 * n], refs[2 * n + 1]
        for cp in _chip_exchange_copies(pair_refs, land_refs, pairs, views, s_sems, r_sems):
            cp.wait_send()
            cp.wait_recv()

    outs = pl.pallas_call(
        body, name=name, out_shape=tuple(pltpu.HBM(x.shape, x.dtype) for x in list(pairs) + list(lands)),
        in_specs=[_HBM] * (2 * n) + [_SEM, _SEM, _ANY], out_specs=tuple([_HBM] * (2 * n)),
        input_output_aliases={i: i for i in range(2 * n)},
        compiler_params=pltpu.CompilerParams(has_side_effects=_EFFECT),
    )(*pairs, *lands, send_sems, recv_sems, after)
    return list(outs[n:])


def _grads_share(tots, *, name):
    n = len(tots)

    def body(*refs):
        ins, outs = refs[:n], refs[n:2 * n]
        send_sems, recv_sems = refs[2 * n:]
        mx, my, mc = lax.axis_index("x"), lax.axis_index("y"), lax.axis_index("c")
        copies = []
        for i in range(n):
            cp = pltpu.make_async_remote_copy(src_ref=ins[i], dst_ref=outs[i], send_sem=send_sems.at[i], recv_sem=recv_sems.at[i],
                                              device_id=(mx, my, 1 - mc), device_id_type=_MESH)
            cp.start()
            copies.append(cp)
        for cp in copies:
            cp.wait()

    return pl.pallas_call(
        body, out_shape=[jax.ShapeDtypeStruct(t.shape, t.dtype) for t in tots], in_specs=[_ANY] * n, out_specs=[_ANY] * n,
        scratch_shapes=[pltpu.SemaphoreType.DMA((n,)), pltpu.SemaphoreType.DMA((n,))], name=name)(*tots)


def _pair_sum(g, recv, view, c_idx, *, name):
    def body(c_ref, a_ref, b_ref, o_ref):
        o_ref[...] = (a_ref[...] + b_ref[...]).astype(WIRE_DTYPE)

    if view == "chip":
        nch, r, c = g.shape
        tr = _row_tile(r // 2, nch * c * 4, 16)
        gv = g.reshape(nch, 2, r // 2, c)
        grid = ((r // 2) // tr,)
        in_specs = [pl.BlockSpec((nch, None, tr, c), lambda i, c_ref: (0, c_ref[0], i, 0)),
                    pl.BlockSpec((nch, tr, c), lambda i, c_ref: (0, i, 0))]
        out_spec = pl.BlockSpec((nch, tr, c), lambda i, c_ref: (0, i, 0))
        sem = ("parallel",)
    else:
        r, c4 = g.shape
        tr = _row_tile(r // 2, c4 * 4, 16)
        gv = g.reshape(2, r // 2, c4)
        grid = ((r // 2) // tr,)
        in_specs = [pl.BlockSpec((None, tr, c4), lambda i, c_ref: (c_ref[0], i, 0)), pl.BlockSpec((tr, c4), lambda i, c_ref: (i, 0))]
        out_spec = pl.BlockSpec((tr, c4), lambda i, c_ref: (i, 0))
        sem = ("parallel",)
    grid_spec = pltpu.PrefetchScalarGridSpec(num_scalar_prefetch=1, grid=grid, in_specs=in_specs, out_specs=out_spec)
    return pl.pallas_call(body, grid_spec=grid_spec, out_shape=jax.ShapeDtypeStruct(recv.shape, WIRE_DTYPE),
                          compiler_params=_params(*sem), name=name)(c_idx, gv, recv)


def _quad_sum(gs, recvs, quads, view, chip_idx, c_idx, *, name):
    nl = len(quads)
    nch, rh, c = quads[0].shape
    tr = _row_tile(rh, c * 4, 16)

    def body(_, __, *refs):
        o_ref = refs[-1]
        per = nch + 1
        for l in range(nl):
            grp = refs[l * per:(l + 1) * per]
            acc = grp[0][...] + grp[1][...]
            for r in grp[2:]:
                acc = acc + r[...].astype(F32)
            o_ref[l] = acc

    if view == "chip":
        own = [pl.BlockSpec((None, None, tr, c), lambda i, j, h: (j[0], h[0], i, 0)),
               pl.BlockSpec((None, tr, c), lambda i, j, h: (j[0], i, 0))]
        gviews = [g.reshape(nch, 2, rh, c) for g in gs]
    else:
        own = [pl.BlockSpec((None, tr, c), lambda i, j, h: (h[0], i, j[0])), pl.BlockSpec((tr, c), lambda i, j, h: (i, j[0]))]
        gviews = [g.reshape(2, rh, nch * c) for g in gs]
    assert nch & (nch - 1) == 0
    got = [pl.BlockSpec((None, tr, c), functools.partial(lambda i, j, h, k: ((j[0] + k) & (nch - 1), i, 0), k=k))
           for k in range(1, nch)]
    ins = []
    for l in range(nl):
        ins += [gviews[l], recvs[l]] + [quads[l]] * (nch - 1)
    grid_spec = pltpu.PrefetchScalarGridSpec(
        num_scalar_prefetch=2, grid=(rh // tr,), in_specs=(own + got) * nl,
        out_specs=pl.BlockSpec((nl, tr, c), lambda i, j, h: (0, i, 0)))
    return pl.pallas_call(body, grid_spec=grid_spec, out_shape=jax.ShapeDtypeStruct((nl, rh, c), F32),
                          compiler_params=_params("parallel"), name=name)(chip_idx, c_idx, *ins)


def _sum_devices(g8, own, dev_idx, *, name):
    k, rows, cols = g8.shape

    def body(d_ref, a_ref, x_ref, o_ref):
        acc = None
        for i in range(k):
            term = jnp.where(d_ref[0] == i, x_ref[...], a_ref[i])
            acc = term if acc is None else acc + term
        o_ref[...] = acc

    grid_spec = pltpu.PrefetchScalarGridSpec(
        num_scalar_prefetch=1, grid=(1,),
        in_specs=[pl.BlockSpec((k, rows, cols), lambda i, d_ref: (0, 0, 0)), pl.BlockSpec((rows, cols), lambda i, d_ref: (0, 0))],
        out_specs=pl.BlockSpec((rows, cols), lambda i, d_ref: (0, 0)))
    return pl.pallas_call(body, grid_spec=grid_spec, out_shape=jax.ShapeDtypeStruct((rows, cols), g8.dtype),
                          compiler_params=_params("arbitrary"), name=name)(dev_idx, g8, own)


def _adamw(w, g, m, v, *, name):
    rows, cols = w.shape
    tr = rows
    for cand in (256, 128, 64, 32, 16, 8):
        if rows % cand == 0 and cand * cols <= 512 * 1024:
            tr = cand
            break
    c1 = 1.0 - ADAM_B1 ** ADAM_STEP
    c2 = 1.0 - ADAM_B2 ** ADAM_STEP

    def body(w_ref, g_ref, m_ref, v_ref, d_ref, nm_ref, nv_ref):
        gv = g_ref[...]
        nm = ADAM_B1 * m_ref[...] + (1.0 - ADAM_B1) * gv
        nv = ADAM_B2 * v_ref[...] + (1.0 - ADAM_B2) * (gv * gv)
        d_ref[...] = -ADAM_LR * ((nm / c1) / (jnp.sqrt(nv / c2) + ADAM_EPS) + ADAM_WD * w_ref[...])
        nm_ref[...] = nm
        nv_ref[...] = nv

    spec = pl.BlockSpec((tr, cols), lambda i: (i, 0))
    shp = jax.ShapeDtypeStruct((rows, cols), F32)
    return pl.pallas_call(body, grid=(rows // tr,), in_specs=[spec] * 4, out_specs=[spec] * 3, out_shape=[shp] * 3,
                          compiler_params=_params("parallel"), name=name)(w, g, m, v)


def _adamw_halves(w, m, v, mine, other, c_idx, *, name):
    nl, r, c = w.shape
    rh = r // 2
    tr = _row_tile(rh, c * 4)
    c1 = 1.0 - ADAM_B1 ** ADAM_STEP
    c2 = 1.0 - ADAM_B2 ** ADAM_STEP

    def body(c_ref, w_ref, m_ref, v_ref, a_ref, b_ref, g_ref, d_ref, nm_ref, nv_ref):
        gv = jnp.where(pl.program_id(1) == c_ref[0], a_ref[...], b_ref[...])
        nm = ADAM_B1 * m_ref[...] + (1.0 - ADAM_B1) * gv
        nv = ADAM_B2 * v_ref[...] + (1.0 - ADAM_B2) * (gv * gv)
        g_ref[...] = gv
        d_ref[...] = -ADAM_LR * ((nm / c1) / (jnp.sqrt(nv / c2) + ADAM_EPS) + ADAM_WD * w_ref[...])
        nm_ref[...] = nm
        nv_ref[...] = nv

    full = pl.BlockSpec((None, None, tr, c), lambda l, h, i, c_ref: (l, h, i, 0))
    half_mine = pl.BlockSpec((None, tr, c), lambda l, h, i, c_ref: (l, jnp.where(h == c_ref[0], i, 0), 0))
    half_other = pl.BlockSpec((None, tr, c), lambda l, h, i, c_ref: (l, jnp.where(h == c_ref[0], 0, i), 0))
    grid_spec = pltpu.PrefetchScalarGridSpec(num_scalar_prefetch=1, grid=(nl, 2, rh // tr),
                                             in_specs=[full] * 3 + [half_mine, half_other], out_specs=[full] * 4)
    shp = jax.ShapeDtypeStruct((nl, 2, rh, c), F32)
    view = (nl, 2, rh, c)
    outs = pl.pallas_call(body, grid_spec=grid_spec, out_shape=[shp] * 4, compiler_params=_params("arbitrary", "arbitrary", "arbitrary"),
                          name=name)(c_idx, w.reshape(view), m.reshape(view), v.reshape(view), mine, other)
    return [o.reshape(nl, r, c) for o in outs]


WEIGHTS = ["mem_ln_g", "mem_ln_b", "w_in", "sg_ln_g", "sg_ln_b", "sg_w", "sg_b", "conv_w", "conv_b", "dt_bias", "a_log",
           "d_skip", "ssm_norm_g", "p_a", "p_b", "w_mix_o", "w_xq", "w_xkv", "w_xo", "w_ffn_in", "w_ffn_out", "ln_g", "ln_b"]
ARG_NAMES = ["x", "mem"] + WEIGHTS + ["loss_target"] + ["m_" + n for n in WEIGHTS] + ["v_" + n for n in WEIGHTS]
BIG = {"w_in": (1, (1024, 9248)), "p_a": (0, (1024, 1024)), "p_b": (0, (2048, 1024)), "w_mix_o": (0, (1024, 1024)),
       "w_xq": (0, (1024, 1024)), "w_xkv": (1, (1024, 2048)), "w_xo": (0, (1024, 1024)), "w_ffn_in": (1, (1024, 5632)),
       "w_ffn_out": (0, (2816, 1024))}
SMALL_SHARDED = {"conv_w": (4, 3072), "ln_g": (3, 1024), "ln_b": (3, 1024)}
SMALL = [n for n in WEIGHTS if n not in BIG]
W_IN_MAP = ((0, 4096, "main", 0), (4096, 7168, "main", XBC_COL0), (7168, 7200, "dt", 0), (7200, 9248, "main", GAB_COL0))
W_IN_SHARD = 9248 // N_CHIPS


def _w_in_chip_major(gm, gd):
    src = {"main": gm, "dt": gd}
    blocks = []
    for j in range(N_CHIPS):
        lo, hi = j * W_IN_SHARD, (j + 1) * W_IN_SHARD
        parts = [src[k][:, o + max(lo, a) - a:o + min(hi, b) - a] for a, b, k, o in W_IN_MAP if max(lo, a) < min(hi, b)]
        blocks.append(jnp.concatenate(parts, axis=1))
    return jnp.stack(blocks)


def _w_in_reassemble(wc):
    def cols(a, b):
        out = []
        for j in range(N_CHIPS):
            lo, hi = max(a, j * W_IN_SHARD), min(b, (j + 1) * W_IN_SHARD)
            if lo < hi:
                out.append(wc[j][:, lo - j * W_IN_SHARD:hi - j * W_IN_SHARD])
        return out

    main = sorted((m for m in W_IN_MAP if m[2] == "main"), key=lambda m: m[3])
    w_main = jnp.concatenate([p for a, b, _, _ in main for p in cols(a, b)], axis=1)
    (a, b, _, _), = [m for m in W_IN_MAP if m[2] == "dt"]
    w_dt = jnp.pad(jnp.concatenate(cols(a, b), axis=1), ((0, 0), (0, HEAD_PAD - (b - a))))
    return w_main, w_dt
GATHER_KIND = {"w_in": "chip", "p_a": "row", "p_b": "row", "w_mix_o": "row", "w_xq": "row", "w_xkv": "col", "w_xo": "row",
               "w_ffn_in": "col", "w_ffn_out": "row", "conv_w": "chip", "ln_g": "chip", "ln_b": "chip"}
GRAD_VIEW = {n: ("col" if k == "col" else "chip") for n, k in GATHER_KIND.items() if n in BIG}


def _shard_shape(name):
    axis, (r, c) = BIG[name]
    return (r // N_CHIPS, c) if axis == 0 else (r, c // N_CHIPS)


def _pad_rows(flat, cols, row_mult):
    n = flat.shape[0]
    rows = -(-n // cols)
    rows = -(-rows // row_mult) * row_mult
    return jnp.pad(flat, (0, rows * cols - n)).reshape(rows, cols)


def _gather_small_params(a, chip):
    names = list(SMALL_SHARDED)
    kinds = [GATHER_KIND[n] for n in names]
    bufs = [_cast_place(a[n], GATHER_KIND[n], F32, chip.reshape(1), name=f"place_{n}") for n in names]
    outs = _gather_params(bufs, [a[n].shape[1:] for n in names], kinds, name="gather_small_params")
    full = {}
    for n, o in zip(names, outs):
        _, _, r, c = o.shape
        full[n] = jnp.transpose(o, (0, 2, 1, 3)).reshape(DEPTH, r, N_CHIPS * c)
    return full


GATHER_GROUPS = (("w_in",), tuple(n for n in BIG if n != "w_in"))


def _gather_group_start(a, l, names, chip, after, *, tag):
    bufs = [_cast_place_layer(a[n], l, GATHER_KIND[n], chip.reshape(1), after, name=f"place_{n}_l{l}") for n in names]
    return _gather_start(bufs, [a[n].shape[1:] for n in names], [GATHER_KIND[n] for n in names], name=f"gather_start_{tag}")


def _gather_group_finish(a, names, flight, after, *, tag):
    send_sems, recv_sems, bufs, token = flight
    shapes, kinds = [a[n].shape[1:] for n in names], [GATHER_KIND[n] for n in names]
    bufs = _gather_wait(send_sems, recv_sems, bufs, shapes, kinds, token if after is None else after, name=f"gather_wait_{tag}")
    full = dict(zip(names, _gather_forward(bufs, shapes, kinds, name=f"gather_forward_{tag}")))
    if "w_in" in full:
        full["w_main"], full["w_dt"] = _w_in_reassemble(full.pop("w_in"))
    return full


def _layer_weights(a, big, small, l):
    w = dict(big)
    for n in SMALL_SHARDED:
        w[n] = small[n][l]
    for n in ["sg_ln_g", "sg_ln_b", "sg_w", "conv_b", "ssm_norm_g"]:
        w[n] = a[n][l]
    w["sg_bcol"] = a["sg_b"][l][..., None]
    for n in ["dt_bias", "a_log"]:
        w[n + "8"] = _pad_heads(a[n][l])
    w["d_skipx"] = _expand_heads(a["d_skip"][l])
    return w


def _grad_views(grads, names):
    gs = []
    for n in names:
        axis, _ = BIG[n]
        r, c = _shard_shape(n)
        if n == "w_in":
            gs.append(_w_in_chip_major(grads["w_main"], grads["w_dt"]))
        elif axis == 0:
            gs.append(grads[n].reshape(N_CHIPS, r, c))
        else:
            gs.append(grads[n])
    return gs


class _GradExchange:
    def __init__(self, grads, names, c_idx, tag):
        self.names, self.c_idx, self.tag = names, c_idx, tag
        self.views = [GRAD_VIEW[n] for n in names]
        self.gs = _grad_views(grads, names)

    def start(self):
        self.sems = _grads_to_sibling_start(self.gs, self.views, name=f"grads_to_sibling_start_{self.tag}")
        return self.sems[4]

    def cross(self, after):
        send_sems, recv_sems, gs, lands, token = self.sems
        self.gs, self.recv = _grads_to_sibling_wait(send_sems, recv_sems, gs, lands, self.views, token if after is None else after,
                                                    name=f"grads_to_sibling_wait_{self.tag}")
        cpre = self.c_idx.reshape(1)
        pairs = [_pair_sum(g, rv, v, cpre, name=f"grads_pair_sum_{n}_{self.tag}")
                 for g, rv, v, n in zip(self.gs, self.recv, self.views, self.names)]
        self.sems = _grads_to_chips_start(pairs, self.views, name=f"grads_to_chips_start_{self.tag}")
        return self.sems[4]

    def finish(self, after):
        send_sems, recv_sems, pairs, lands, _ = self.sems
        quads = _grads_to_chips_wait(send_sems, recv_sems, pairs, lands, self.views, after, name=f"grads_to_chips_wait_{self.tag}")
        return {n: (g, rv, q) for n, g, rv, q in zip(self.names, self.gs, self.recv, quads)}


def _finish_big_grads(parts, c_idx, chip):
    tots = [_quad_sum([parts[l][n][0] for l in range(DEPTH)], [parts[l][n][1] for l in range(DEPTH)],
                      [parts[l][n][2] for l in range(DEPTH)], GRAD_VIEW[n], chip.reshape(1), c_idx.reshape(1),
                      name=f"grads_chip_sum_{n}") for n in BIG]
    others = _grads_share(tots, name="grads_share")
    return {n: (t, o) for n, t, o in zip(BIG, tots, others)}


def _direct_copies(x_ref, land_ref, send_sems, recv_sems):
    mx, my, mc = lax.axis_index("x"), lax.axis_index("y"), lax.axis_index("c")
    me = 4 * mx + 2 * my + mc
    copies = []
    for k in range(N_DEV - 1):
        f = k + 1
        to = (mx ^ (f >> 2 & 1), my ^ (f >> 1 & 1), mc ^ (f & 1))
        copies.append(pltpu.make_async_remote_copy(src_ref=x_ref, dst_ref=land_ref.at[me], send_sem=send_sems.at[k],
                                                   recv_sem=recv_sems.at[k], device_id=to, device_id_type=_MESH))
    return copies


def _all_gather8_start(x, *, name):
    land = pltpu.with_memory_space_constraint(lax.empty((N_DEV,) + x.shape, x.dtype), pltpu.HBM)

    def body(x_ref, land_ref, send_sems, recv_sems, x_out, land_out, token):
        for cp in _direct_copies(x_ref, land_ref, send_sems, recv_sems):
            cp.start()
        token[...] = jnp.zeros_like(token)

    n = N_DEV - 1
    return pl.pallas_call(
        body, name=name,
        out_shape=(pltpu.SemaphoreType.DMA((n,)), pltpu.SemaphoreType.DMA((n,)), pltpu.HBM(x.shape, x.dtype),
                   pltpu.HBM(land.shape, land.dtype), jax.ShapeDtypeStruct((SUBLANE, LANE), F32)),
        in_specs=[_HBM, _HBM], out_specs=(_SEM, _SEM, _HBM, _HBM, pl.BlockSpec(memory_space=pltpu.VMEM)),
        input_output_aliases={0: 2, 1: 3}, compiler_params=pltpu.CompilerParams(has_side_effects=_EFFECT),
    )(pltpu.with_memory_space_constraint(x, pltpu.HBM), land)


def _all_gather8_wait(send_sems, recv_sems, x, land, after, *, name):
    def body(x_ref, land_ref, s_sems, r_sems, _, x_out, land_out):
        for cp in _direct_copies(x_ref, land_ref, s_sems, r_sems):
            cp.wait_send()
            cp.wait_recv()

    return pl.pallas_call(
        body, name=name, out_shape=(pltpu.HBM(x.shape, x.dtype), pltpu.HBM(land.shape, land.dtype)),
        in_specs=[_HBM, _HBM, _SEM, _SEM, _ANY], out_specs=(_HBM, _HBM), input_output_aliases={0: 0, 1: 1},
        compiler_params=pltpu.CompilerParams(has_side_effects=_EFFECT),
    )(x, land, send_sems, recv_sems, after)


def _pack_small(small):
    return _pad_rows(jnp.concatenate([small[n].reshape(-1) for n in small]), LANE, SUBLANE)


def _unpack_small(small, g8, packed, chip, c_idx, *, name):
    names = list(small)
    tot = _sum_devices(g8, packed, (2 * chip + c_idx).reshape(1), name=name).reshape(-1)
    out, off = {}, 0
    for n in names:
        sz = small[n].size
        full = tot[off:off + sz].reshape(small[n].shape)
        off += sz
        if n in SMALL_SHARDED:
            cs = SMALL_SHARDED[n][1] // N_CHIPS
            full = lax.dynamic_slice_in_dim(full, chip * cs, cs, axis=-1)
        out[n] = full
    return out


def kernel(x, mem, mem_ln_g, mem_ln_b, w_in, sg_ln_g, sg_ln_b, sg_w, sg_b, conv_w, conv_b, dt_bias, a_log, d_skip, ssm_norm_g, p_a, p_b, w_mix_o, w_xq, w_xkv, w_xo, w_ffn_in, w_ffn_out, ln_g, ln_b, loss_target, m_mem_ln_g, m_mem_ln_b, m_w_in, m_sg_ln_g, m_sg_ln_b, m_sg_w, m_sg_b, m_conv_w, m_conv_b, m_dt_bias, m_a_log, m_d_skip, m_ssm_norm_g, m_p_a, m_p_b, m_w_mix_o, m_w_xq, m_w_xkv, m_w_xo, m_w_ffn_in, m_w_ffn_out, m_ln_g, m_ln_b, v_mem_ln_g, v_mem_ln_b, v_w_in, v_sg_ln_g, v_sg_ln_b, v_sg_w, v_sg_b, v_conv_w, v_conv_b, v_dt_bias, v_a_log, v_d_skip, v_ssm_norm_g, v_p_a, v_p_b, v_w_mix_o, v_w_xq, v_w_xkv, v_w_xo, v_w_ffn_in, v_w_ffn_out, v_ln_g, v_ln_b):
    a = dict(zip(ARG_NAMES, (x, mem, mem_ln_g, mem_ln_b, w_in, sg_ln_g, sg_ln_b, sg_w, sg_b, conv_w, conv_b, dt_bias, a_log, d_skip, ssm_norm_g, p_a, p_b, w_mix_o, w_xq, w_xkv, w_xo, w_ffn_in, w_ffn_out, ln_g, ln_b, loss_target, m_mem_ln_g, m_mem_ln_b, m_w_in, m_sg_ln_g, m_sg_ln_b, m_sg_w, m_sg_b, m_conv_w, m_conv_b, m_dt_bias, m_a_log, m_d_skip, m_ssm_norm_g, m_p_a, m_p_b, m_w_mix_o, m_w_xq, m_w_xkv, m_w_xo, m_w_ffn_in, m_w_ffn_out, m_ln_g, m_ln_b, v_mem_ln_g, v_mem_ln_b, v_w_in, v_sg_ln_g, v_sg_ln_b, v_sg_w, v_sg_b, v_conv_w, v_conv_b, v_dt_bias, v_a_log, v_d_skip, v_ssm_norm_g, v_p_a, v_p_b, v_w_mix_o, v_w_xq, v_w_xkv, v_w_xo, v_w_ffn_in, v_w_ffn_out, v_ln_g, v_ln_b)))
    c_idx = lax.axis_index("c").astype(jnp.int32)
    chip = (2 * lax.axis_index("x") + lax.axis_index("y")).astype(jnp.int32)

    small = _gather_small_params(a, chip)
    ga, gb = GATHER_GROUPS
    flights = {(0, 0): _gather_group_start(a, 0, ga, chip, small["ln_b"], tag="l0_a")}
    flights[0, 1] = _gather_group_start(a, 0, gb, chip, flights[0, 0][3], tag="l0_b")

    def layer_weights(after, l):
        first = _gather_group_finish(a, ga, flights[l, 0], after if l else flights[l, 1][3], tag=f"l{l}_a")

        def rest(w, after_b):
            more = _gather_group_finish(a, gb, flights[l, 1], after_b, tag=f"l{l}_b")
            if l + 1 < DEPTH:
                flights[l + 1, 0] = _gather_group_start(a, l + 1, ga, chip, more["p_a"], tag=f"l{l + 1}_a")
                flights[l + 1, 1] = _gather_group_start(a, l + 1, gb, chip, flights[l + 1, 0][3], tag=f"l{l + 1}_b")
                more["p_a"] = more["p_a"] + flights[l + 1, 1][3][0, 0].astype(MXU_DTYPE)
            return {k: v for k, v in {**w, **more}.items() if k != "rest"}

        return dict(_layer_weights(a, first, small, l), rest=rest)

    layers = [functools.partial(layer_weights, l=l) for l in range(DEPTH)]
    exchanges, seen, small_flight = [], {}, {}

    def start_exchange(l, names, grads_l):
        ex = _GradExchange(grads_l, names, c_idx, f"l{l}_{names[0]}")
        tokens = [ex.start()]
        if exchanges:
            tokens.append(exchanges[-1][1].cross(tokens[0]))
        exchanges.append((l, ex))
        seen[l] = grads_l
        if l == 0 and names == GRAD_GROUPS[-1]:
            tokens.append(ex.cross(None))
            small = {}
            for n in SMALL:
                if n.startswith("mem_ln"):
                    continue
                per_layer = []
                for k in range(DEPTH):
                    g = seen[k][n]
                    if n in ("dt_bias", "a_log", "d_skip"):
                        g = g[0, :SSM_HEADS]
                    per_layer.append(g.reshape(a[n].shape[1:-1] + (-1,)))
                small[n] = jnp.stack(per_layer)
            small_flight["small"] = small
            small_flight["sems"] = _all_gather8_start(_pack_small(small), name="gather_small_grads_start")
            tokens.append(small_flight["sems"][4])
        return sum(tokens[1:], tokens[0])

    lsum, grad_x, grads, d_mem_g, d_mem_b = _local_step(x, mem, loss_target, mem_ln_g, mem_ln_b, layers, start_exchange)
    loss = lax.psum(0.5 * jnp.sum(lsum) / D_MODEL, ("x", "y", "c"))

    parts = [{} for _ in range(DEPTH)]
    for l, ex in exchanges:
        parts[l].update(ex.finish(grad_x))
    halves = _finish_big_grads(parts, c_idx, chip)
    gw = {}
    send_sems, recv_sems, packed, land, _ = small_flight["sems"]
    packed, g8 = _all_gather8_wait(send_sems, recv_sems, packed, land, grad_x, name="gather_small_grads_wait")
    gw.update(_unpack_small(small_flight["small"], g8, packed, chip, c_idx, name="small_grads_sum"))
    mem_small = {"mem_ln_g": d_mem_g, "mem_ln_b": d_mem_b}
    mem_packed = _pack_small(mem_small)
    gw.update(_unpack_small(mem_small, _all_gather8(mem_packed, name="gather_mem_ln_grads"), mem_packed, chip, c_idx,
                            name="mem_ln_grads_sum"))

    delta, new_m, new_v = {}, {}, {}
    for n in BIG:
        mine, other = halves[n]
        gw[n], delta[n], new_m[n], new_v[n] = _adamw_halves(a[n], a["m_" + n], a["v_" + n], mine, other, c_idx.reshape(1),
                                                             name=f"adamw_{n}")
    for n in SMALL:
        shp = a[n].shape
        view = (-1, LANE) if a[n].size % LANE == 0 else (1, -1)
        outs = _adamw(*[v.reshape(view) for v in (a[n], gw[n], a["m_" + n], a["v_" + n])], name=f"adamw_{n}")
        delta[n], new_m[n], new_v[n] = (o.reshape(shp) for o in outs)
    return (loss, grad_x, *[gw[n].reshape(a[n].shape) for n in WEIGHTS], *[delta[n] for n in WEIGHTS],
            *[new_m[n] for n in WEIGHTS], *[new_v[n] for n in WEIGHTS])
```

```python
import functools
import math

import jax
import jax.numpy as jnp
from jax import lax
from jax.experimental import pallas as pl
from jax.experimental.pallas import tpu as pltpu

F32 = jnp.float32
MXU_DTYPE = jnp.bfloat16
WIRE_DTYPE = jnp.bfloat16
STASH_DTYPE = jnp.bfloat16

D_MODEL = 1024
DEPTH = 2
CHUNK = 128
SG_GROUPS = 8
SSM_INNER = 2048
SSM_HEADDIM = 64
SSM_HEADS = 32
SSM_STATE = 128
SSM_GROUPS = 4
SSM_CONV = 4
SSM_CONV_DIM = 3072
X_HEADS = 4
X_HEADDIM = 256
FFN_HIDDEN = 2816
ALPHA = float((2 * DEPTH) ** 0.25)
LN_EPS = 1e-5
RMS_EPS = 1e-5
ADAM_LR = 0.001
ADAM_B1 = 0.9
ADAM_B2 = 0.999
ADAM_EPS = 1e-08
ADAM_WD = 0.01
ADAM_STEP = 10

MAIN_COLS = 9216
UVZ_COLS = 4096
GAB_COL0 = 4096
XBC_COL0 = 6144
HEAD_PAD = 128

VMEM_LIMIT = 56 * 1024 * 1024
BLOCK_BYTES = 2 * 1024 * 1024
ROW_TILES = (512, 256, 128)
LANE = 128
SUBLANE = 8

N_CHIPS = 4
N_DEV = 8


def _pick(n, cands):
    for c in cands:
        if n % c == 0:
            return c
    return n


MM_TILE_MAX = 1408
MM_OPERAND_BYTES = 8 * 1024 * 1024


def _div_tile(n, limit):
    best = None
    for t in range(LANE, min(n, limit) + 1, LANE):
        if n % t == 0:
            best = t
    return n if best is None else best


def _params(*sem):
    return pltpu.CompilerParams(dimension_semantics=tuple(sem), vmem_limit_bytes=VMEM_LIMIT)


_ANY = pl.BlockSpec(memory_space=pl.ANY)
_MESH = pl.DeviceIdType.MESH


def _nt(a, b):
    return lax.dot_general(a, b, (((1,), (1,)), ((), ())), preferred_element_type=F32)


def _tn(a, b):
    return lax.dot_general(a, b, (((0,), (0,)), ((), ())), preferred_element_type=F32)


def _nn(a, b):
    return jnp.dot(a, b, preferred_element_type=F32)


def _sigmoid(x):
    return 0.5 * jnp.tanh(0.5 * x) + 0.5


def _split3(v):
    def top(x):
        bits = lax.bitcast_convert_type(x, jnp.uint32) & jnp.uint32(0xFFFF0000)
        return lax.bitcast_convert_type(bits, F32)

    v1 = top(v)
    r1 = v - v1
    v2 = top(r1)
    v3 = r1 - v2
    return v1.astype(jnp.bfloat16), v2.astype(jnp.bfloat16), v3.astype(jnp.bfloat16)


def _dot_exact(a, b, dn, data):
    if data == 0:
        mat = b.astype(jnp.bfloat16)
        return sum(lax.dot_general(p, mat, dn, preferred_element_type=F32) for p in _split3(a))
    mat = a.astype(jnp.bfloat16)
    return sum(lax.dot_general(mat, p, dn, preferred_element_type=F32) for p in _split3(b))


_DN_NN = (((1,), (0,)), ((), ()))
_DN_TN = (((0,), (0,)), ((), ()))


def _gelu(x):
    return 0.5 * x * (1.0 + lax.erf(x * (2.0 ** -0.5)))


def _gelu_grad(x):
    return 0.5 * (1.0 + lax.erf(x * (2.0 ** -0.5))) + x * jnp.exp(-0.5 * x * x) * (1.0 / math.sqrt(2.0 * math.pi))


def _mm(a, b, *, ta=False, tb=False, out_dtype=F32, after=None, name):
    if ta:
        kdim, m = a.shape
    else:
        m, kdim = a.shape
    if tb:
        n, k2 = b.shape[-2:]
    else:
        k2, n = b.shape[-2:]
    assert kdim == k2, (a.shape, b.shape, ta, tb)
    tm = _div_tile(m, MM_TILE_MAX)
    tn = _div_tile(n, MM_TILE_MAX)
    tk = _div_tile(kdim, MM_OPERAND_BYTES // (tm * a.dtype.itemsize + tn * b.dtype.itemsize))
    nk = kdim // tk
    dn = (((0 if ta else 1,), (1 if tb else 0,)), ((), ()))

    extra = [] if after is None else [after]

    def body(a_ref, b_ref, *rest):
        o_ref = rest[len(extra)]
        d = lax.dot_general(a_ref[...].astype(MXU_DTYPE), b_ref[...].astype(MXU_DTYPE), dn, preferred_element_type=F32)
        if nk == 1:
            o_ref[...] = d.astype(out_dtype)
            return
        acc_ref = rest[len(extra) + 1]
        k = pl.program_id(2)

        @pl.when(k == 0)
        def _():
            acc_ref[...] = d

        @pl.when(jnp.logical_and(k > 0, k < nk - 1))
        def _():
            acc_ref[...] += d

        @pl.when(k == nk - 1)
        def _():
            o_ref[...] = (acc_ref[...] + d).astype(out_dtype)

    a_spec = pl.BlockSpec((tk, tm), lambda i, j, k: (k, i)) if ta else pl.BlockSpec((tm, tk), lambda i, j, k: (i, k))
    b_spec = pl.BlockSpec((tn, tk), lambda i, j, k: (j, k)) if tb else pl.BlockSpec((tk, tn), lambda i, j, k: (k, j))
    return pl.pallas_call(
        body, grid=(m // tm, n // tn, nk), in_specs=[a_spec, b_spec] + [_ANY] * len(extra),
        out_specs=pl.BlockSpec((tm, tn), lambda i, j, k: (i, j)),
        out_shape=jax.ShapeDtypeStruct((m, n), out_dtype),
        scratch_shapes=[pltpu.VMEM((tm, tn), F32)] if nk > 1 else [],
        compiler_params=_params("parallel", "parallel", "arbitrary"), name=name)(a, b, *extra)


def _row_spec(tm, c, col=0):
    return pl.BlockSpec((tm, c), lambda i: (i, col))


def _par_spec(shape):
    nd = len(shape)
    return pl.BlockSpec(shape, lambda i: (0,) * nd)


def _ln_fwd(x, f, g, b, *, name):
    t, c = x.shape
    tm = _pick(t, ROW_TILES)
    has_f = f is not None

    def body(*refs):
        if has_f:
            x_ref, f_ref, g_ref, b_ref, y_ref, yb_ref, xh_ref, rs_ref = refs
            r = ALPHA * x_ref[...] + f_ref[...]
        else:
            x_ref, g_ref, b_ref, y_ref, yb_ref, xh_ref, rs_ref = refs
            r = x_ref[...]
        mu = jnp.mean(r, axis=-1, keepdims=True)
        xc = r - mu
        var = jnp.mean(xc * xc, axis=-1, keepdims=True)
        rstd = lax.rsqrt(var + LN_EPS)
        xh = xc * rstd
        y = xh * g_ref[...] + b_ref[...]
        y_ref[...] = y
        yb_ref[...] = y.astype(MXU_DTYPE)
        xh_ref[...] = xh
        rs_ref[...] = jnp.broadcast_to(rstd, rs_ref.shape)

    ins = [x] + ([f] if has_f else []) + [g.reshape(1, c), b.reshape(1, c)]
    in_specs = [_row_spec(tm, c)] * (2 if has_f else 1) + [_par_spec((1, c))] * 2
    return pl.pallas_call(
        body, grid=(t // tm,), in_specs=in_specs,
        out_specs=[_row_spec(tm, c), _row_spec(tm, c), _row_spec(tm, c), _row_spec(tm, LANE)],
        out_shape=[jax.ShapeDtypeStruct((t, c), F32), jax.ShapeDtypeStruct((t, c), MXU_DTYPE),
                   jax.ShapeDtypeStruct((t, c), F32), jax.ShapeDtypeStruct((t, LANE), F32)],
        compiler_params=_params("parallel"), name=name)(*ins)


def _ln_bwd(addends, scales, xh, rs, g, *, name):
    t, c = xh.shape
    tm = _pick(t, ROW_TILES)
    na = len(addends)

    def body(*refs):
        a_refs = refs[:na]
        xh_ref, rs_ref, g_ref, dp_ref, dpb_ref, dg_ref, db_ref = refs[na:]

        @pl.when(pl.program_id(0) == 0)
        def _():
            dg_ref[...] = jnp.zeros_like(dg_ref)
            db_ref[...] = jnp.zeros_like(db_ref)

        dy = None
        for s, r in zip(scales, a_refs):
            term = r[...] if s == 1.0 else s * r[...]
            dy = term if dy is None else dy + term
        xhv = xh_ref[...]
        dxh = dy * g_ref[...]
        m1 = jnp.mean(dxh, axis=-1, keepdims=True)
        m2 = jnp.mean(dxh * xhv, axis=-1, keepdims=True)
        dp = rs_ref[:, 0:1] * (dxh - m1 - xhv * m2)
        dp_ref[...] = dp
        dpb_ref[...] = dp.astype(MXU_DTYPE)
        dg_ref[...] += jnp.sum(dy * xhv, axis=0, keepdims=True)
        db_ref[...] += jnp.sum(dy, axis=0, keepdims=True)

    in_specs = [_row_spec(tm, c)] * (na + 1) + [_row_spec(tm, LANE), _par_spec((1, c))]
    return pl.pallas_call(
        body, grid=(t // tm,), in_specs=in_specs,
        out_specs=[_row_spec(tm, c), _row_spec(tm, c), _par_spec((1, c)), _par_spec((1, c))],
        out_shape=[jax.ShapeDtypeStruct((t, c), F32), jax.ShapeDtypeStruct((t, c), MXU_DTYPE),
                   jax.ShapeDtypeStruct((1, c), F32), jax.ShapeDtypeStruct((1, c), F32)],
        compiler_params=_params("arbitrary"), name=name)(*addends, xh, rs, g.reshape(1, c))


def _add_scaled(addends, scales, *, name):
    t, c = addends[0].shape
    tm = _pick(t, ROW_TILES)
    na = len(addends)

    def body(*refs):
        acc = None
        for s, r in zip(scales, refs[:na]):
            term = r[...] if s == 1.0 else s * r[...]
            acc = term if acc is None else acc + term
        refs[na][...] = acc

    return pl.pallas_call(
        body, grid=(t // tm,), in_specs=[_row_spec(tm, c)] * na, out_specs=_row_spec(tm, c),
        out_shape=jax.ShapeDtypeStruct((t, c), F32), compiler_params=_params("parallel"), name=name)(*addends)


def _loss_head(y, tgt, *, name):
    t, c = y.shape
    tm = _pick(t, ROW_TILES)

    def body(y_ref, t_ref, dy_ref, ls_ref):
        @pl.when(pl.program_id(0) == 0)
        def _():
            ls_ref[...] = jnp.zeros_like(ls_ref)

        e = y_ref[...] - t_ref[...]
        dy_ref[...] = e * (1.0 / c)
        ls_ref[...] += jnp.sum(e * e, axis=0, keepdims=True)

    return pl.pallas_call(
        body, grid=(t // tm,), in_specs=[_row_spec(tm, c)] * 2,
        out_specs=[_row_spec(tm, c), _par_spec((1, c))],
        out_shape=[jax.ShapeDtypeStruct((t, c), F32), jax.ShapeDtypeStruct((1, c), F32)],
        compiler_params=_params("arbitrary"), name=name)(y, tgt)


def _swiglu_fwd(h, *, name):
    t, two_f = h.shape
    fh = two_f // 2
    tm = _pick(t, (256, 128))

    def body(g_ref, u_ref, a_ref):
        g = g_ref[...].astype(F32)
        a_ref[...] = (g * _sigmoid(g) * u_ref[...].astype(F32)).astype(MXU_DTYPE)

    return pl.pallas_call(
        body, grid=(t // tm,), in_specs=[_row_spec(tm, fh, 0), _row_spec(tm, fh, 1)], out_specs=_row_spec(tm, fh),
        out_shape=jax.ShapeDtypeStruct((t, fh), MXU_DTYPE), compiler_params=_params("parallel"), name=name)(h, h)


def _swiglu_bwd(h, da, *, name):
    t, two_f = h.shape
    fh = two_f // 2
    tm = _pick(t, (256, 128))

    def body(g_ref, u_ref, da_ref, dh_ref):
        g = g_ref[...].astype(F32)
        s = _sigmoid(g)
        dav = da_ref[...].astype(F32)
        dh_ref[:, :fh] = (dav * u_ref[...].astype(F32) * (s * (1.0 + g * (1.0 - s)))).astype(MXU_DTYPE)
        dh_ref[:, fh:] = (dav * g * s).astype(MXU_DTYPE)

    return pl.pallas_call(
        body, grid=(t // tm,), in_specs=[_row_spec(tm, fh, 0), _row_spec(tm, fh, 1), _row_spec(tm, fh)],
        out_specs=_row_spec(tm, two_f), out_shape=jax.ShapeDtypeStruct((t, two_f), MXU_DTYPE),
        compiler_params=_params("parallel"), name=name)(h, h, da)


def _attn_probs(q, k):
    s = _nt(q, k) * (X_HEADDIM ** -0.5)
    s = s - jnp.max(s, axis=-1, keepdims=True)
    p = jnp.exp(s)
    return p / jnp.sum(p, axis=-1, keepdims=True)


def _attn_fwd(q, kv, *, bsz, name):
    t = q.shape[0]
    s = t // bsz
    ml = kv.shape[0] // bsz
    hd = X_HEADDIM

    def body(q_ref, k_ref, v_ref, o_ref):
        p = _attn_probs(q_ref[...], k_ref[...])
        o_ref[...] = _nn(p.astype(MXU_DTYPE), v_ref[...]).astype(MXU_DTYPE)

    return pl.pallas_call(
        body, grid=(bsz, X_HEADS),
        in_specs=[pl.BlockSpec((s, hd), lambda b, h: (b, h)), pl.BlockSpec((ml, hd), lambda b, h: (b, h)),
                  pl.BlockSpec((ml, hd), lambda b, h: (b, X_HEADS + h))],
        out_specs=pl.BlockSpec((s, hd), lambda b, h: (b, h)),
        out_shape=jax.ShapeDtypeStruct((t, D_MODEL), MXU_DTYPE),
        compiler_params=_params("parallel", "parallel"), name=name)(q, kv, kv)


def _attn_bwd(q, kv, do, *, bsz, name):
    t = q.shape[0]
    s = t // bsz
    ml = kv.shape[0] // bsz
    hd = X_HEADDIM

    def body(q_ref, k_ref, v_ref, do_ref, dq_ref, dk_ref, dv_ref):
        qv, kk, vv, dov = q_ref[...], k_ref[...], v_ref[...], do_ref[...]
        p = _attn_probs(qv, kk)
        dp = _nt(dov, vv)
        dv_ref[...] = _tn(p.astype(MXU_DTYPE), dov).astype(MXU_DTYPE)
        ds = (p * (dp - jnp.sum(dp * p, axis=-1, keepdims=True)) * (X_HEADDIM ** -0.5)).astype(MXU_DTYPE)
        dq_ref[...] = _nn(ds, kk).astype(MXU_DTYPE)
        dk_ref[...] = _tn(ds, qv).astype(MXU_DTYPE)

    blk_q = pl.BlockSpec((s, hd), lambda b, h: (b, h))
    blk_m = pl.BlockSpec((ml, hd), lambda b, h: (b, h))
    return pl.pallas_call(
        body, grid=(bsz, X_HEADS),
        in_specs=[blk_q, blk_m, pl.BlockSpec((ml, hd), lambda b, h: (b, X_HEADS + h)), blk_q],
        out_specs=[blk_q, blk_m, blk_m],
        out_shape=[jax.ShapeDtypeStruct((t, D_MODEL), MXU_DTYPE), jax.ShapeDtypeStruct((bsz * ml, D_MODEL), MXU_DTYPE),
                   jax.ShapeDtypeStruct((bsz * ml, D_MODEL), MXU_DTYPE)],
        compiler_params=_params("parallel", "parallel"), name=name)(q, kv, kv, do)


def _causal(n):
    row = lax.broadcasted_iota(jnp.int32, (n, n), 0)
    col = lax.broadcasted_iota(jnp.int32, (n, n), 1)
    return row >= col


def _sg_norm(v, g, b):
    gv = _gelu(v)
    mu = jnp.mean(gv, axis=-1, keepdims=True)
    xc = gv - mu
    var = jnp.mean(xc * xc, axis=-1, keepdims=True)
    rstd = lax.rsqrt(var + LN_EPS)
    xh = xc * rstd
    return xh, rstd, xh * g + b


def _sg_fwd(proj, ln_g, ln_b, w, bcol, *, name):
    t = proj.shape[0]
    c = D_MODEL
    gd = c // SG_GROUPS

    def body(u_ref, v_ref, g_ref, b_ref, w_ref, bc_ref, o_ref):
        gu = _gelu(u_ref[...].astype(F32))
        _, _, vn = _sg_norm(v_ref[...].astype(F32), g_ref[...], b_ref[...])
        mask = _causal(CHUNK)
        for g in range(SG_GROUPS):
            sl = slice(g * gd, (g + 1) * gd)
            wg = jnp.where(mask, w_ref[g], 0.0).astype(MXU_DTYPE)
            mixed = _nn(wg, vn[:, sl].astype(MXU_DTYPE)) + bc_ref[g]
            o_ref[:, sl] = (gu[:, sl] * mixed).astype(MXU_DTYPE)

    return pl.pallas_call(
        body, grid=(t // CHUNK,),
        in_specs=[_row_spec(CHUNK, c, 0), _row_spec(CHUNK, c, 1), _par_spec((1, c)), _par_spec((1, c)),
                  _par_spec((SG_GROUPS, CHUNK, CHUNK)), _par_spec((SG_GROUPS, CHUNK, 1))],
        out_specs=_row_spec(CHUNK, c), out_shape=jax.ShapeDtypeStruct((t, c), MXU_DTYPE),
        compiler_params=_params("parallel"), name=name)(proj, proj, ln_g.reshape(1, c), ln_b.reshape(1, c), w, bcol)


def _sg_bwd(proj, dsgo, ln_g, ln_b, w, bcol, dproj, *, name):
    t = proj.shape[0]
    c = D_MODEL
    gd = c // SG_GROUPS

    def body(u_ref, v_ref, d_ref, g_ref, b_ref, w_ref, bc_ref, _, duv_ref, dw_ref, dbc_ref, dg_ref, db_ref, dvn_ref):
        @pl.when(pl.program_id(0) == 0)
        def _():
            dw_ref[...] = jnp.zeros_like(dw_ref)
            dbc_ref[...] = jnp.zeros_like(dbc_ref)
            dg_ref[...] = jnp.zeros_like(dg_ref)
            db_ref[...] = jnp.zeros_like(db_ref)

        u = u_ref[...].astype(F32)
        v = v_ref[...].astype(F32)
        dso = d_ref[...].astype(F32)
        gu = _gelu(u)
        xh, rstd, vn = _sg_norm(v, g_ref[...], b_ref[...])
        mask = _causal(CHUNK)
        for g in range(SG_GROUPS):
            sl = slice(g * gd, (g + 1) * gd)
            wg = jnp.where(mask, w_ref[g], 0.0).astype(MXU_DTYPE)
            vng = vn[:, sl].astype(MXU_DTYPE)
            mixed = _nn(wg, vng) + bc_ref[g]
            duv_ref[:, sl] = (dso[:, sl] * mixed * _gelu_grad(u[:, sl])).astype(MXU_DTYPE)
            dmix = dso[:, sl] * gu[:, sl]
            dmb = dmix.astype(MXU_DTYPE)
            dbc_ref[g] += jnp.sum(dmix, axis=-1, keepdims=True)
            dw_ref[g] += jnp.where(mask, _nt(dmb, vng), 0.0)
            dvn_ref[:, sl] = _tn(wg, dmb)
        dvn = dvn_ref[...]
        dg_ref[...] += jnp.sum(dvn * xh, axis=0, keepdims=True)
        db_ref[...] += jnp.sum(dvn, axis=0, keepdims=True)
        dxh = dvn * g_ref[...]
        m1 = jnp.mean(dxh, axis=-1, keepdims=True)
        m2 = jnp.mean(dxh * xh, axis=-1, keepdims=True)
        dgv = rstd * (dxh - m1 - xh * m2)
        duv_ref[:, c:] = (dgv * _gelu_grad(v)).astype(MXU_DTYPE)

    return pl.pallas_call(
        body, grid=(t // CHUNK,),
        in_specs=[_row_spec(CHUNK, c, 0), _row_spec(CHUNK, c, 1), _row_spec(CHUNK, c), _par_spec((1, c)),
                  _par_spec((1, c)), _par_spec((SG_GROUPS, CHUNK, CHUNK)), _par_spec((SG_GROUPS, CHUNK, 1)), _ANY],
        out_specs=[_row_spec(CHUNK, 2 * c), _par_spec((SG_GROUPS, CHUNK, CHUNK)), _par_spec((SG_GROUPS, CHUNK, 1)),
                   _par_spec((1, c)), _par_spec((1, c))],
        out_shape=[jax.ShapeDtypeStruct(dproj.shape, dproj.dtype), jax.ShapeDtypeStruct((SG_GROUPS, CHUNK, CHUNK), F32),
                   jax.ShapeDtypeStruct((SG_GROUPS, CHUNK, 1), F32), jax.ShapeDtypeStruct((1, c), F32),
                   jax.ShapeDtypeStruct((1, c), F32)],
        scratch_shapes=[pltpu.VMEM((CHUNK, c), F32)], input_output_aliases={7: 0},
        compiler_params=_params("arbitrary"), name=name)(proj, proj, dsgo, ln_g.reshape(1, c), ln_b.reshape(1, c), w, bcol, dproj)


CONV_TC = 512


def _conv_taps(x):
    rows = lax.broadcasted_iota(jnp.int32, x.shape, 0)
    taps = [jnp.where(rows >= SSM_CONV - 1 - k, pltpu.roll(x, SSM_CONV - 1 - k, axis=0), 0.0) for k in range(SSM_CONV - 1)]
    return taps + [x]


def _conv_pre(taps, w_ref, b_ref):
    acc = b_ref[...]
    for k in range(SSM_CONV):
        acc = acc + taps[k] * w_ref[k:k + 1, :]
    return acc


def _conv_fwd(proj, w, b, *, bsz, name):
    t = proj.shape[0]
    s = t // bsz
    nj = SSM_CONV_DIM // CONV_TC
    c0 = XBC_COL0 // CONV_TC

    def body(x_ref, w_ref, b_ref, o_ref):
        pre = _conv_pre(_conv_taps(x_ref[...].astype(F32)), w_ref, b_ref)
        o_ref[...] = pre * _sigmoid(pre)

    return pl.pallas_call(
        body, grid=(bsz, nj),
        in_specs=[pl.BlockSpec((s, CONV_TC), lambda bb, j: (bb, c0 + j)), pl.BlockSpec((SSM_CONV, CONV_TC), lambda bb, j: (0, j)),
                  pl.BlockSpec((1, CONV_TC), lambda bb, j: (0, j))],
        out_specs=pl.BlockSpec((s, CONV_TC), lambda bb, j: (bb, j)),
        out_shape=jax.ShapeDtypeStruct((t, SSM_CONV_DIM), F32),
        compiler_params=_params("parallel", "parallel"), name=name)(proj, w, b.reshape(1, -1))


def _conv_bwd(proj, dact, w, b, dproj, *, bsz, name):
    t = proj.shape[0]
    s = t // bsz
    nj = SSM_CONV_DIM // CONV_TC
    c0 = XBC_COL0 // CONV_TC

    def body(x_ref, d_ref, w_ref, b_ref, _, dx_ref, dw_ref, db_ref):
        @pl.when(pl.program_id(1) == 0)
        def _():
            dw_ref[...] = jnp.zeros_like(dw_ref)
            db_ref[...] = jnp.zeros_like(db_ref)

        taps = _conv_taps(x_ref[...].astype(F32))
        pre = _conv_pre(taps, w_ref, b_ref)
        sg = _sigmoid(pre)
        dpre = d_ref[...] * (sg * (1.0 + pre * (1.0 - sg)))
        rows = lax.broadcasted_iota(jnp.int32, dpre.shape, 0)
        db_ref[...] += jnp.sum(dpre, axis=0, keepdims=True)
        dx = dpre * w_ref[SSM_CONV - 1:SSM_CONV, :]
        for k in range(SSM_CONV):
            dw_ref[k:k + 1, :] += jnp.sum(dpre * taps[k], axis=0, keepdims=True)
        for k in range(SSM_CONV - 1):
            sh = SSM_CONV - 1 - k
            dsh = jnp.where(rows < s - sh, pltpu.roll(dpre, s - sh, axis=0), 0.0)
            dx = dx + dsh * w_ref[k:k + 1, :]
        dx_ref[...] = dx.astype(MXU_DTYPE)

    return pl.pallas_call(
        body, grid=(nj, bsz),
        in_specs=[pl.BlockSpec((s, CONV_TC), lambda j, bb: (bb, c0 + j)), pl.BlockSpec((s, CONV_TC), lambda j, bb: (bb, j)),
                  pl.BlockSpec((SSM_CONV, CONV_TC), lambda j, bb: (0, j)), pl.BlockSpec((1, CONV_TC), lambda j, bb: (0, j)), _ANY],
        out_specs=[pl.BlockSpec((s, CONV_TC), lambda j, bb: (bb, c0 + j)), pl.BlockSpec((SSM_CONV, CONV_TC), lambda j, bb: (0, j)),
                   pl.BlockSpec((1, CONV_TC), lambda j, bb: (0, j))],
        out_shape=[jax.ShapeDtypeStruct(dproj.shape, dproj.dtype), jax.ShapeDtypeStruct((SSM_CONV, SSM_CONV_DIM), F32),
                   jax.ShapeDtypeStruct((1, SSM_CONV_DIM), F32)],
        input_output_aliases={4: 0},
        compiler_params=_params("parallel", "arbitrary"), name=name)(proj, dact, w, b.reshape(1, -1), dproj)


def _softplus(x):
    return jnp.maximum(x, 0.0) + jnp.log1p(jnp.exp(-jnp.abs(x)))


def _pad_heads(v):
    return jnp.broadcast_to(jnp.pad(v.astype(F32), (0, HEAD_PAD - SSM_HEADS))[None, :], (SUBLANE, HEAD_PAD))


def _ssd_prep(dt_raw, dt_bias8, a_log8, *, name):
    t = dt_raw.shape[0]
    n = CHUNK

    def body(r_ref, b_ref, al_ref, dt_ref, cs_ref, dtt_ref, cst_ref):
        dt = _softplus(r_ref[...] + b_ref[0:1, :])
        da = dt * (-jnp.exp(al_ref[0:1, :]))
        row = lax.broadcasted_iota(jnp.int32, (n, n), 0)
        col = lax.broadcasted_iota(jnp.int32, (n, n), 1)
        lower = (col <= row).astype(F32)
        upper = (row <= col).astype(F32)
        eye = (row == col).astype(F32)
        dt_ref[...] = dt
        cs_ref[...] = _dot_exact(lower, da, _DN_NN, 1)
        cst_ref[0] = _dot_exact(da, upper, _DN_TN, 0)
        dtt_ref[0] = _dot_exact(dt, eye, _DN_TN, 0)

    hp = HEAD_PAD
    return pl.pallas_call(
        body, grid=(t // n,),
        in_specs=[_row_spec(n, hp), _par_spec((SUBLANE, hp)), _par_spec((SUBLANE, hp))],
        out_specs=[_row_spec(n, hp), _row_spec(n, hp), pl.BlockSpec((1, hp, n), lambda i: (i, 0, 0)),
                   pl.BlockSpec((1, hp, n), lambda i: (i, 0, 0))],
        out_shape=[jax.ShapeDtypeStruct((t, hp), F32), jax.ShapeDtypeStruct((t, hp), F32),
                   jax.ShapeDtypeStruct((t // n, hp, n), F32), jax.ShapeDtypeStruct((t // n, hp, n), F32)],
        compiler_params=_params("parallel"), name=name)(dt_raw, dt_bias8, a_log8)


def _expand_mat():
    h = lax.broadcasted_iota(jnp.int32, (HEAD_PAD, SSM_INNER), 0)
    ch = lax.broadcasted_iota(jnp.int32, (HEAD_PAD, SSM_INNER), 1)
    return (ch // SSM_HEADDIM == h).astype(F32)


def _reduce_mat():
    ch = lax.broadcasted_iota(jnp.int32, (SSM_INNER, HEAD_PAD), 0)
    h = lax.broadcasted_iota(jnp.int32, (SSM_INNER, HEAD_PAD), 1)
    return (ch // SSM_HEADDIM == h).astype(F32)


def _expand(v, em):
    return _dot_exact(v, em, _DN_NN, 0)


def _expand_heads(v):
    return jnp.repeat(v.astype(F32), SSM_HEADDIM)[None, :]


def _decay_mat(cs_ref, cst_ref, h, mask):
    seg = cs_ref[:, h:h + 1] - cst_ref[0, h:h + 1, :]
    return jnp.where(mask, jnp.exp(jnp.minimum(seg, 0.0)), 0.0)


GROUP_CH = SSM_INNER // SSM_GROUPS
PAIRS_PER_GROUP = GROUP_CH // LANE
HEADS_PER_GROUP = SSM_HEADS // SSM_GROUPS
BM_COL0 = SSM_INNER
CM_COL0 = SSM_INNER + SSM_GROUPS * SSM_STATE


def _ssd_specs(nc, rev):
    def cidx(i):
        return (i // nc) * nc + (nc - 1 - i % nc) if rev else i

    n = CHUNK
    xs = pl.BlockSpec((n, SSM_INNER), lambda i: (cidx(i), 0))
    bm = pl.BlockSpec((n, GROUP_CH), lambda i: (cidx(i), BM_COL0 // GROUP_CH))
    cm = pl.BlockSpec((n, GROUP_CH), lambda i: (cidx(i), CM_COL0 // GROUP_CH))
    hv = pl.BlockSpec((n, HEAD_PAD), lambda i: (cidx(i), 0))
    hvt = pl.BlockSpec((1, HEAD_PAD, n), lambda i: (cidx(i), 0, 0))
    st = pl.BlockSpec((1, SSM_INNER, SSM_STATE), lambda i: (cidx(i), 0, 0))
    return xs, bm, cm, hv, hvt, st


def _ssd_fwd(xbc, dt, cs, dtt, cst, dskx, *, nc, name):
    t = xbc.shape[0]
    n = CHUNK
    xs_s, bm_s, cm_s, hv_s, hvt_s, st_s = _ssd_specs(nc, False)

    def body(xs_ref, bm_ref, cm_ref, dt_ref, cs_ref, dtt_ref, cst_ref, dsk_ref, y_ref, st_ref, prev):
        @pl.when(pl.program_id(0) % nc == 0)
        def _():
            prev[...] = jnp.zeros_like(prev)

        st_ref[0] = prev[...]
        em = _expand_mat()
        dtx = _expand(dt_ref[...], em)
        csx = _expand(cs_ref[...], em)
        dskx = dsk_ref[...]
        xs = xs_ref[...]
        xdt = xs * dtx
        ecs = jnp.exp(csx)
        dec = jnp.exp(csx[n - 1:n, :] - csx)
        mask = _causal(n)
        lane = lax.broadcasted_iota(jnp.int32, (n, LANE), 1)
        for g in range(SSM_GROUPS):
            gs = slice(g * SSM_STATE, (g + 1) * SSM_STATE)
            gc = slice(g * GROUP_CH, (g + 1) * GROUP_CH)
            cmat = cm_ref[:, gs].astype(MXU_DTYPE)
            bmat = bm_ref[:, gs].astype(MXU_DTYPE)
            cb = _nt(cmat, bmat)
            yoff = ecs[:, gc] * _nt(cmat, prev[gc, :].astype(MXU_DTYPE))
            for q in range(PAIRS_PER_GROUP):
                hp = g * PAIRS_PER_GROUP + q
                sl = slice(hp * LANE, (hp + 1) * LANE)
                xp = xdt[:, sl].astype(MXU_DTYPE)
                m0 = (cb * _decay_mat(cs_ref, cst_ref, 2 * hp, mask)).astype(MXU_DTYPE)
                m1 = (cb * _decay_mat(cs_ref, cst_ref, 2 * hp + 1, mask)).astype(MXU_DTYPE)
                yd = jnp.where(lane < SSM_HEADDIM, _nn(m0, xp), _nn(m1, xp))
                y_ref[:, sl] = yd + yoff[:, q * LANE:(q + 1) * LANE] + xs[:, sl] * dskx[:, sl]
            snew = _tn((xdt[:, gc] * dec[:, gc]).astype(MXU_DTYPE), bmat)
            for r in range(HEADS_PER_GROUP):
                h = g * HEADS_PER_GROUP + r
                rows = slice(h * SSM_HEADDIM, (h + 1) * SSM_HEADDIM)
                e = jnp.exp(cst_ref[0, h:h + 1, n - 1:n])
                prev[rows, :] = prev[rows, :] * e + snew[r * SSM_HEADDIM:(r + 1) * SSM_HEADDIM, :]

    return pl.pallas_call(
        body, grid=(t // n,),
        in_specs=[xs_s, bm_s, cm_s, hv_s, hv_s, hvt_s, hvt_s, _par_spec((1, SSM_INNER))],
        out_specs=[xs_s, st_s],
        out_shape=[jax.ShapeDtypeStruct((t, SSM_INNER), F32), jax.ShapeDtypeStruct((t // n, SSM_INNER, SSM_STATE), F32)],
        scratch_shapes=[pltpu.VMEM((SSM_INNER, SSM_STATE), F32)],
        compiler_params=_params("arbitrary"), name=name)(xbc, xbc, xbc, dt, cs, dtt, cst, dskx)


def _ssd_bwd(dy, xbc, dt, cs, dtt, cst, st, dskx, a_log8, dt_raw, dt_bias8, *, nc, name):
    t = xbc.shape[0]
    n = CHUNK
    xs_s, bm_s, cm_s, hv_s, hvt_s, st_s = _ssd_specs(nc, True)
    acc_s = _par_spec((1, HEAD_PAD))
    xbc_s = pl.BlockSpec((n, SSM_CONV_DIM), xs_s.index_map)

    def body(dy_ref, xs_ref, bm_ref, cm_ref, dt_ref, cs_ref, dtt_ref, cst_ref, st_ref, dsk_ref, al_ref, raw_ref, bias_ref,
             dxbc_ref, ddr_ref, dal_ref, dds_ref, dbias_ref, dprev, dxdt_s, tdec_s, tcs_s):
        @pl.when(pl.program_id(0) % nc == 0)
        def _():
            dprev[...] = jnp.zeros_like(dprev)

        @pl.when(pl.program_id(0) == 0)
        def _():
            dal_ref[...] = jnp.zeros_like(dal_ref)
            dds_ref[...] = jnp.zeros_like(dds_ref)
            dbias_ref[...] = jnp.zeros_like(dbias_ref)

        em = _expand_mat()
        rm = _reduce_mat()

        def head_reduce(v):
            return _dot_exact(v, rm, _DN_NN, 0)

        dtv = dt_ref[...]
        csv = cs_ref[...]
        dtx = _expand(dtv, em)
        csx = _expand(csv, em)
        dskx = dsk_ref[...]
        xs = xs_ref[...]
        dyv = dy_ref[...]
        xdt = xs * dtx
        ecs = jnp.exp(csx)
        dec = jnp.exp(csx[n - 1:n, :] - csx)
        mask = _causal(n)
        lane = lax.broadcasted_iota(jnp.int32, (n, LANE), 1)
        hlane = lax.broadcasted_iota(jnp.int32, (1, HEAD_PAD), 1)
        hsub = lax.broadcasted_iota(jnp.int32, (HEAD_PAD, 1), 0)
        rsum = jnp.zeros((n, HEAD_PAD), F32)
        csum = jnp.zeros((HEAD_PAD, n), F32)
        for g in range(SSM_GROUPS):
            gs = slice(g * SSM_STATE, (g + 1) * SSM_STATE)
            gc = slice(g * GROUP_CH, (g + 1) * GROUP_CH)
            cmat = cm_ref[:, gs].astype(MXU_DTYPE)
            bmat = bm_ref[:, gs].astype(MXU_DTYPE)
            cb = _nt(cmat, bmat)
            pg = st_ref[0, gc, :].astype(MXU_DTYPE)
            dpg = dprev[gc, :]
            dpgb = dpg.astype(MXU_DTYPE)
            z = _nt(cmat, pg)
            dyg = dyv[:, gc]
            dz = (dyg * ecs[:, gc]).astype(MXU_DTYPE)
            dc = _nn(dz, pg)
            dprev_y = _tn(dz, cmat)
            tcs_s[:, gc] = dyg * z * ecs[:, gc]
            xd = xdt[:, gc] * dec[:, gc]
            wmat = _nt(bmat, dpgb)
            db = _nn(xd.astype(MXU_DTYPE), dpgb)
            tdec_s[:, gc] = wmat * xd
            dxdt_g = wmat * dec[:, gc]
            dcb = jnp.zeros((n, n), F32)
            for q in range(PAIRS_PER_GROUP):
                hp = g * PAIRS_PER_GROUP + q
                sl = slice(hp * LANE, (hp + 1) * LANE)
                xp = xdt[:, sl].astype(MXU_DTYPE)
                dyp = dyv[:, sl]
                dypb = dyp.astype(MXU_DTYPE)
                dxp = None
                for hh in range(2):
                    h = 2 * hp + hh
                    lm = _decay_mat(cs_ref, cst_ref, h, mask)
                    mine = (lane < SSM_HEADDIM) if hh == 0 else (lane >= SSM_HEADDIM)
                    dm = _nt(jnp.where(mine, dyp, 0.0).astype(MXU_DTYPE), xp)
                    dml = dm * lm
                    dcb = dcb + dml
                    gseg = dml * cb
                    rsum = rsum + jnp.sum(gseg, axis=1, keepdims=True) * (hlane == h).astype(F32)
                    csum = csum + (hsub == h).astype(F32) * jnp.sum(gseg, axis=0, keepdims=True)
                    dxh = _tn((cb * lm).astype(MXU_DTYPE), dypb)
                    dxp = dxh if dxp is None else jnp.where(mine, dxh, dxp)
                dxdt_s[:, sl] = dxdt_g[:, q * LANE:(q + 1) * LANE] + dxp
            dcbb = dcb.astype(MXU_DTYPE)
            dxbc_ref[:, CM_COL0 + g * SSM_STATE:CM_COL0 + (g + 1) * SSM_STATE] = dc + _nn(dcbb, bmat)
            dxbc_ref[:, BM_COL0 + g * SSM_STATE:BM_COL0 + (g + 1) * SSM_STATE] = db + _tn(dcbb, cmat)
            for r in range(HEADS_PER_GROUP):
                h = g * HEADS_PER_GROUP + r
                rows = slice(h * SSM_HEADDIM, (h + 1) * SSM_HEADDIM)
                lr = slice(r * SSM_HEADDIM, (r + 1) * SSM_HEADDIM)
                e = jnp.exp(cst_ref[0, h:h + 1, n - 1:n])
                dprev[rows, :] = dpg[lr, :] * e + dprev_y[lr, :]
            tq = _dot_exact(dpg * st_ref[0, gc, :], rm[gc, :], _DN_TN, 0)
            if g == 0:
                qsum = jnp.sum(tq, axis=0, keepdims=True)
            else:
                qsum = qsum + jnp.sum(tq, axis=0, keepdims=True)
        dxdt = dxdt_s[...]
        dxbc_ref[:, 0:SSM_INNER] = dxdt * dtx + dyv * dskx
        ddt = head_reduce(dxdt * xs)
        edec = head_reduce(tdec_s[...])
        ycs = head_reduce(tcs_s[...])
        row = lax.broadcasted_iota(jnp.int32, (n, HEAD_PAD), 0)
        extra = jnp.sum(edec, axis=0, keepdims=True) + qsum * jnp.exp(csv[n - 1:n, :])
        dcs = rsum - csum.T + ycs - edec + jnp.where(row == n - 1, extra, 0.0)
        r2 = lax.broadcasted_iota(jnp.int32, (n, n), 0)
        c2 = lax.broadcasted_iota(jnp.int32, (n, n), 1)
        dda = _dot_exact((c2 >= r2).astype(F32), dcs, _DN_NN, 1)
        a_row = -jnp.exp(al_ref[0:1, :])
        ddt = ddt + dda * a_row
        dal_ref[...] += jnp.sum(dda * dtv, axis=0, keepdims=True) * a_row
        dds_ref[...] += jnp.sum(head_reduce(dyv * xs), axis=0, keepdims=True)
        ddr = ddt * _sigmoid(raw_ref[...] + bias_ref[0:1, :])
        ddr_ref[...] = ddr
        dbias_ref[...] += jnp.sum(ddr, axis=0, keepdims=True)

    par8 = _par_spec((SUBLANE, HEAD_PAD))
    return pl.pallas_call(
        body, grid=(t // n,),
        in_specs=[xs_s, xs_s, bm_s, cm_s, hv_s, hv_s, hvt_s, hvt_s, st_s, _par_spec((1, SSM_INNER)), par8, hv_s, par8],
        out_specs=[xbc_s, hv_s, acc_s, acc_s, acc_s],
        out_shape=[jax.ShapeDtypeStruct((t, SSM_CONV_DIM), F32), jax.ShapeDtypeStruct((t, HEAD_PAD), F32),
                   jax.ShapeDtypeStruct((1, HEAD_PAD), F32), jax.ShapeDtypeStruct((1, HEAD_PAD), F32),
                   jax.ShapeDtypeStruct((1, HEAD_PAD), F32)],
        scratch_shapes=[pltpu.VMEM((SSM_INNER, SSM_STATE), F32), pltpu.VMEM((n, SSM_INNER), F32),
                        pltpu.VMEM((n, SSM_INNER), F32), pltpu.VMEM((n, SSM_INNER), F32)],
        compiler_params=_params("arbitrary"), name=name)(dy, xbc, xbc, xbc, dt, cs, dtt, cst, st, dskx, a_log8, dt_raw, dt_bias8)


def _gate_norm_fwd(y, proj, norm_g, *, name):
    t, c = y.shape
    tm = _pick(t, (256, 128))

    def body(y_ref, z_ref, g_ref, o_ref):
        z = z_ref[...].astype(F32)
        yz = y_ref[...] * z * _sigmoid(z)
        for g in range(SSM_GROUPS):
            gc = slice(g * GROUP_CH, (g + 1) * GROUP_CH)
            seg = yz[:, gc]
            r = lax.rsqrt(jnp.mean(seg * seg, axis=-1, keepdims=True) + RMS_EPS)
            o_ref[:, gc] = (seg * r * g_ref[:, gc]).astype(MXU_DTYPE)

    return pl.pallas_call(
        body, grid=(t // tm,), in_specs=[_row_spec(tm, c), _row_spec(tm, c, 1), _par_spec((1, c))],
        out_specs=_row_spec(tm, c), out_shape=jax.ShapeDtypeStruct((t, c), MXU_DTYPE),
        compiler_params=_params("parallel"), name=name)(y, proj, norm_g.reshape(1, c))


def _gate_norm_bwd(dyb, y, proj, norm_g, dproj, *, name):
    t, c = y.shape
    tm = _pick(t, (256, 128))

    def body(d_ref, y_ref, z_ref, g_ref, _, dy_ref, dz_ref, dg_ref):
        @pl.when(pl.program_id(0) == 0)
        def _():
            dg_ref[...] = jnp.zeros_like(dg_ref)

        z = z_ref[...].astype(F32)
        yv = y_ref[...]
        sz = _sigmoid(z)
        silu = z * sz
        yz = yv * silu
        dv = d_ref[...].astype(F32)
        for g in range(SSM_GROUPS):
            gc = slice(g * GROUP_CH, (g + 1) * GROUP_CH)
            seg = yz[:, gc]
            r = lax.rsqrt(jnp.mean(seg * seg, axis=-1, keepdims=True) + RMS_EPS)
            nrm = seg * r
            dn = dv[:, gc] * g_ref[:, gc]
            dg_ref[:, gc] += jnp.sum(dv[:, gc] * nrm, axis=0, keepdims=True)
            dyz = r * (dn - nrm * jnp.mean(dn * nrm, axis=-1, keepdims=True))
            dy_ref[:, gc] = dyz * silu[:, gc]
            dz_ref[:, gc] = (dyz * yv[:, gc] * (sz[:, gc] * (1.0 + z[:, gc] * (1.0 - sz[:, gc])))).astype(MXU_DTYPE)

    return pl.pallas_call(
        body, grid=(t // tm,), in_specs=[_row_spec(tm, c), _row_spec(tm, c), _row_spec(tm, c, 1), _par_spec((1, c)), _ANY],
        out_specs=[_row_spec(tm, c), _row_spec(tm, c, 1), _par_spec((1, c))],
        out_shape=[jax.ShapeDtypeStruct((t, c), F32), jax.ShapeDtypeStruct(dproj.shape, dproj.dtype),
                   jax.ShapeDtypeStruct((1, c), F32)],
        input_output_aliases={4: 1},
        compiler_params=_params("arbitrary"), name=name)(dyb, y, proj, norm_g.reshape(1, c), dproj)


GA_COLBLK = GAB_COL0 // D_MODEL


def _merge_fwd(br_a, br_b, proj, *, name):
    t, c = br_a.shape
    tm = _pick(t, ROW_TILES)

    def body(a_ref, b_ref, ga_ref, gb_ref, o_ref):
        o_ref[...] = (_sigmoid(ga_ref[...].astype(F32)) * a_ref[...].astype(F32)
                      + _sigmoid(gb_ref[...].astype(F32)) * b_ref[...].astype(F32)).astype(MXU_DTYPE)

    return pl.pallas_call(
        body, grid=(t // tm,),
        in_specs=[_row_spec(tm, c), _row_spec(tm, c), _row_spec(tm, c, GA_COLBLK), _row_spec(tm, c, GA_COLBLK + 1)],
        out_specs=_row_spec(tm, c), out_shape=jax.ShapeDtypeStruct((t, c), MXU_DTYPE),
        compiler_params=_params("parallel"), name=name)(br_a, br_b, proj, proj)


def _merge_bwd(dm, br_a, br_b, proj, *, name):
    t, c = br_a.shape
    tm = _pick(t, ROW_TILES)

    def body(dm_ref, a_ref, b_ref, ga_ref, gb_ref, da_ref, db_ref, dg_ref):
        d = dm_ref[...].astype(F32)
        sa = _sigmoid(ga_ref[...].astype(F32))
        sb = _sigmoid(gb_ref[...].astype(F32))
        da_ref[...] = (d * sa).astype(MXU_DTYPE)
        db_ref[...] = (d * sb).astype(MXU_DTYPE)
        dg_ref[:, :c] = (d * a_ref[...].astype(F32) * sa * (1.0 - sa)).astype(MXU_DTYPE)
        dg_ref[:, c:] = (d * b_ref[...].astype(F32) * sb * (1.0 - sb)).astype(MXU_DTYPE)

    return pl.pallas_call(
        body, grid=(t // tm,),
        in_specs=[_row_spec(tm, c), _row_spec(tm, c), _row_spec(tm, c), _row_spec(tm, c, GA_COLBLK), _row_spec(tm, c, GA_COLBLK + 1)],
        out_specs=[_row_spec(tm, c), _row_spec(tm, c), _row_spec(tm, 2 * c, GAB_COL0 // (2 * c))],
        out_shape=[jax.ShapeDtypeStruct((t, c), MXU_DTYPE), jax.ShapeDtypeStruct((t, c), MXU_DTYPE),
                   jax.ShapeDtypeStruct((t, MAIN_COLS), MXU_DTYPE)],
        compiler_params=_params("parallel"), name=name)(dm, br_a, br_b, proj, proj)


def _layer_fwd(x, xb, memn_b, w, *, bsz, tag):
    nc = x.shape[0] // bsz // CHUNK
    sv = {"x_in": xb}
    proj = _mm(xb, w["w_main"], out_dtype=STASH_DTYPE, name=f"{tag}_proj")
    dt_raw = _mm(xb, w["w_dt"], name=f"{tag}_dtproj")
    sgo = _sg_fwd(proj, w["sg_ln_g"], w["sg_ln_b"], w["sg_w"], w["sg_bcol"], name=f"{tag}_sg_fwd")
    xbc = _conv_fwd(proj, w["conv_w"], w["conv_b"], bsz=bsz, name=f"{tag}_conv_fwd")
    dt, cs, dtt, cst = _ssd_prep(dt_raw, w["dt_bias8"], w["a_log8"], name=f"{tag}_ssd_prep")
    y, st = _ssd_fwd(xbc, dt, cs, dtt, cst, w["d_skipx"], nc=nc, name=f"{tag}_ssd_fwd")
    yb = _gate_norm_fwd(y, proj, w["ssm_norm_g"], name=f"{tag}_gate_norm_fwd")
    if "rest" in w:
        w = w["rest"](w, yb)
    br_a = _mm(sgo, w["p_a"], out_dtype=STASH_DTYPE, name=f"{tag}_br_a")
    br_b = _mm(yb, w["p_b"], out_dtype=STASH_DTYPE, name=f"{tag}_br_b")
    merged = _merge_fwd(br_a, br_b, proj, name=f"{tag}_merge_fwd")
    mix = _mm(merged, w["w_mix_o"], name=f"{tag}_mix_o")
    x1, x1b, xh1, rs1 = _ln_fwd(x, mix, w["ln_g"][0], w["ln_b"][0], name=f"{tag}_ln1_fwd")
    sv.update(proj=proj, dt_raw=dt_raw, sgo=sgo, xbc=xbc, dt=dt, cs=cs, dtt=dtt, cst=cst, y=y, st=st, yb=yb,
              br_a=br_a, br_b=br_b, merged=merged, xh1=xh1, rs1=rs1, x1b=x1b)
    q = _mm(x1b, w["w_xq"], out_dtype=MXU_DTYPE, name=f"{tag}_q")
    kv = _mm(memn_b, w["w_xkv"], out_dtype=MXU_DTYPE, name=f"{tag}_kv")
    o = _attn_fwd(q, kv, bsz=bsz, name=f"{tag}_attn_fwd")
    att = _mm(o, w["w_xo"], name=f"{tag}_xo")
    x2, x2b, xh2, rs2 = _ln_fwd(x1, att, w["ln_g"][1], w["ln_b"][1], name=f"{tag}_ln2_fwd")
    sv.update(q=q, kv=kv, o=o, xh2=xh2, rs2=rs2, x2b=x2b)
    h = _mm(x2b, w["w_ffn_in"], out_dtype=STASH_DTYPE, name=f"{tag}_ffn_in")
    a = _swiglu_fwd(h, name=f"{tag}_swiglu_fwd")
    ffn = _mm(a, w["w_ffn_out"], name=f"{tag}_ffn_out")
    x3, x3b, xh3, rs3 = _ln_fwd(x2, ffn, w["ln_g"][2], w["ln_b"][2], name=f"{tag}_ln3_fwd")
    sv.update(h=h, a=a, xh3=xh3, rs3=rs3)
    return x3, x3b, sv, w


GRAD_GROUPS = (("w_ffn_out", "w_ffn_in", "w_xo", "w_xq", "w_xkv"), ("w_mix_o", "p_a", "p_b"), ("w_in",))


def _layer_bwd(dx3_addends, dx3_scales, memn_b, w, sv, on_group=None, *, bsz, tag):
    nc = sv["xh1"].shape[0] // bsz // CHUNK
    gr = {}

    def group_done(k):
        return on_group(GRAD_GROUPS[k], gr) if on_group is not None else None
    dp3, dp3b, dg3, db3 = _ln_bwd(dx3_addends, dx3_scales, sv["xh3"], sv["rs3"], w["ln_g"][2], name=f"{tag}_ln3_bwd")
    da = _mm(dp3b, w["w_ffn_out"], tb=True, out_dtype=STASH_DTYPE, name=f"{tag}_d_a")
    gr["w_ffn_out"] = _mm(sv["a"], dp3b, ta=True, name=f"{tag}_dw_ffn_out")
    dh = _swiglu_bwd(sv["h"], da, name=f"{tag}_swiglu_bwd")
    gr["w_ffn_in"] = _mm(sv["x2b"], dh, ta=True, name=f"{tag}_dw_ffn_in")
    dx2_br = _mm(dh, w["w_ffn_in"], tb=True, name=f"{tag}_dx2")
    dp2, dp2b, dg2, db2 = _ln_bwd([dp3, dx2_br], [ALPHA, 1.0], sv["xh2"], sv["rs2"], w["ln_g"][1], name=f"{tag}_ln2_bwd")
    do = _mm(dp2b, w["w_xo"], tb=True, out_dtype=MXU_DTYPE, name=f"{tag}_d_o")
    gr["w_xo"] = _mm(sv["o"], dp2b, ta=True, name=f"{tag}_dw_xo")
    dq, dk, dv = _attn_bwd(sv["q"], sv["kv"], do, bsz=bsz, name=f"{tag}_attn_bwd")
    dkv = jnp.concatenate([dk, dv], axis=1)
    gr["w_xq"] = _mm(sv["x1b"], dq, ta=True, name=f"{tag}_dw_xq")
    gr["w_xkv"] = _mm(memn_b, dkv, ta=True, name=f"{tag}_dw_xkv")
    dmemn = _mm(dkv, w["w_xkv"], tb=True, name=f"{tag}_d_memn")
    dx1_br = _mm(dq, w["w_xq"], tb=True, name=f"{tag}_dx1")
    token = group_done(0)
    ln_g1 = w["ln_g"][0] if token is None else w["ln_g"][0] + token[0, 0]
    dp1, dp1b, dg1, db1 = _ln_bwd([dp2, dx1_br], [ALPHA, 1.0], sv["xh1"], sv["rs1"], ln_g1, name=f"{tag}_ln1_bwd")
    gr["ln_g"] = jnp.concatenate([dg1, dg2, dg3], axis=0)
    gr["ln_b"] = jnp.concatenate([db1, db2, db3], axis=0)
    dmerged = _mm(dp1b, w["w_mix_o"], tb=True, out_dtype=STASH_DTYPE, name=f"{tag}_d_merged")
    gr["w_mix_o"] = _mm(sv["merged"], dp1b, ta=True, name=f"{tag}_dw_mix_o")
    dbr_a, dbr_b, dproj = _merge_bwd(dmerged, sv["br_a"], sv["br_b"], sv["proj"], name=f"{tag}_merge_bwd")
    gr["p_a"] = _mm(sv["sgo"], dbr_a, ta=True, name=f"{tag}_dw_p_a")
    gr["p_b"] = _mm(sv["yb"], dbr_b, ta=True, name=f"{tag}_dw_p_b")
    dsgo = _mm(dbr_a, w["p_a"], tb=True, out_dtype=STASH_DTYPE, name=f"{tag}_d_sgo")
    dyb = _mm(dbr_b, w["p_b"], tb=True, out_dtype=STASH_DTYPE, name=f"{tag}_d_yb")
    token = group_done(1)
    norm_g = w["ssm_norm_g"] if token is None else w["ssm_norm_g"] + token[0, 0]
    dy, dproj, gr["ssm_norm_g"] = _gate_norm_bwd(dyb, sv["y"], sv["proj"], norm_g, dproj, name=f"{tag}_gate_norm_bwd")
    dxbc, ddr, gr["a_log"], gr["d_skip"], gr["dt_bias"] = _ssd_bwd(
        dy, sv["xbc"], sv["dt"], sv["cs"], sv["dtt"], sv["cst"], sv["st"], w["d_skipx"], w["a_log8"], sv["dt_raw"],
        w["dt_bias8"], nc=nc, name=f"{tag}_ssd_bwd")
    dproj, gr["conv_w"], gr["conv_b"] = _conv_bwd(sv["proj"], dxbc, w["conv_w"], w["conv_b"], dproj, bsz=bsz, name=f"{tag}_conv_bwd")
    dproj, gr["sg_w"], dsg_bcol, gr["sg_ln_g"], gr["sg_ln_b"] = _sg_bwd(
        sv["proj"], dsgo, w["sg_ln_g"], w["sg_ln_b"], w["sg_w"], w["sg_bcol"], dproj, name=f"{tag}_sg_bwd")
    gr["sg_b"] = dsg_bcol[..., 0]
    gr["w_main"] = _mm(sv["x_in"], dproj, ta=True, name=f"{tag}_dw_main")
    gr["w_dt"] = _mm(sv["x_in"], ddr, ta=True, name=f"{tag}_dw_dt")
    token = group_done(2)
    dx_dt = _mm(ddr, w["w_dt"], tb=True, after=token, name=f"{tag}_dx_dt")
    dx_main = _mm(dproj, w["w_main"], tb=True, after=token, name=f"{tag}_dx_main")
    return [dp1, dx_main, dx_dt], [ALPHA, 1.0, 1.0], gr, dmemn


def _local_step(x, mem, tgt, mem_ln_g, mem_ln_b, layers, on_layer_grads=None):
    bsz, s, d = x.shape
    xf = x.reshape(bsz * s, d)
    memf = mem.reshape(-1, d)
    _, memn_b, mxh, mrs = _ln_fwd(memf, None, mem_ln_g, mem_ln_b, name="mem_ln_fwd")
    cur, curb, saved, weights = xf, xf, [], []
    for li, get_weights in enumerate(layers):
        cur, curb, sv, w = _layer_fwd(cur, curb, memn_b, get_weights(cur), bsz=bsz, tag=f"l{li}")
        saved.append(sv)
        weights.append(w)
    dy, lsum = _loss_head(cur, tgt.reshape(bsz * s, d), name="loss_head")
    addends, scales = [dy], [1.0]
    grads, dmem = [None] * len(layers), []
    for li in reversed(range(len(layers))):
        on_group = None if on_layer_grads is None else functools.partial(on_layer_grads, li)
        addends, scales, grads[li], dm = _layer_bwd(addends, scales, memn_b, weights[li], saved[li], on_group, bsz=bsz, tag=f"l{li}")
        dmem.append(dm)
    grad_x = _add_scaled(addends, scales, name="grad_x").reshape(bsz, s, d)
    _, _, dmg, dmb = _ln_bwd(dmem, [1.0] * len(dmem), mxh, mrs, mem_ln_g, name="mem_ln_bwd")
    return lsum, grad_x, grads, dmg[0], dmb[0]


_ANY = pl.BlockSpec(memory_space=pl.ANY)
_MESH = pl.DeviceIdType.MESH


def _all_gather8(x, *, name):
    def body(x_ref, out_ref, send_sems, recv_sems):
        mx, my, mc = lax.axis_index("x"), lax.axis_index("y"), lax.axis_index("c")
        me, sibling = (mx, my, mc), (mx, my, 1 - mc)
        chips = [(1 - mx, my), (mx, 1 - my), (1 - mx, 1 - my)]

        def blk(px, py, pc):
            return out_ref.at[4 * px + 2 * py + pc]

        def copy(k, block, to, src=None):
            return pltpu.make_async_remote_copy(
                src_ref=blk(*block) if src is None else src, dst_ref=blk(*block), send_sem=send_sems.at[k],
                recv_sem=recv_sems.at[k], device_id=to, device_id_type=_MESH)

        first = [copy(0, me, sibling, src=x_ref)]
        first += [copy(1 + j, me, (*chip, mc), src=x_ref) for j, chip in enumerate(chips)]
        for cp in first:
            cp.start()
        passed = [copy(4 + j, (*chip, mc), sibling) for j, chip in enumerate(chips)]
        for j, chip in enumerate(chips):
            copy(1 + j, (*chip, mc), me).wait_recv()
            passed[j].start()
        copy(0, sibling, me).wait_recv()
        for j, chip in enumerate(chips):
            copy(4 + j, (*chip, 1 - mc), me).wait_recv()
        for cp in first + passed:
            cp.wait_send()

    return pl.pallas_call(
        body, out_shape=jax.ShapeDtypeStruct((N_DEV,) + x.shape, x.dtype), in_specs=[_ANY], out_specs=_ANY,
        scratch_shapes=[pltpu.SemaphoreType.DMA((7,)), pltpu.SemaphoreType.DMA((7,))], name=name)(x)


def _row_tile(rows, row_bytes, mult=SUBLANE):
    best = None
    for tr in range(mult, rows + 1, mult):
        if rows % tr == 0 and (best is None or tr * row_bytes <= BLOCK_BYTES):
            best = tr
    return rows if best is None else best


def _gather_shape(r, c, kind):
    return {"row": (2, N_CHIPS * r, c), "col": (2, r, N_CHIPS * c), "chip": (2, N_CHIPS, r, c)}[kind]


def _cast_place(shard, kind, dtype, chip_idx, *, name):
    _, r, c = shard.shape
    tr = _row_tile(r, c * 4, 16)
    nt = r // tr

    def body(_, s_ref, o_ref):
        o_ref[...] = s_ref[...].astype(dtype)

    if kind == "row":
        out_spec = pl.BlockSpec((None, tr, c), lambda l, i, j_ref: (l, j_ref[0] * nt + i, 0))
    elif kind == "col":
        out_spec = pl.BlockSpec((None, tr, c), lambda l, i, j_ref: (l, i, j_ref[0]))
    else:
        out_spec = pl.BlockSpec((None, None, tr, c), lambda l, i, j_ref: (l, j_ref[0], i, 0))
    grid_spec = pltpu.PrefetchScalarGridSpec(
        num_scalar_prefetch=1, grid=(2, nt), in_specs=[pl.BlockSpec((None, tr, c), lambda l, i, j_ref: (l, i, 0))],
        out_specs=out_spec)
    return pl.pallas_call(body, grid_spec=grid_spec, out_shape=jax.ShapeDtypeStruct(_gather_shape(r, c, kind), dtype),
                          compiler_params=_params("parallel", "parallel"), name=name)(chip_idx, shard)


def _gather_params(bufs, shard_shapes, kinds, *, name):
    n = len(bufs)

    def body(*refs):
        outs = refs[n:2 * n]
        send_sems, recv_sems = refs[2 * n:]
        mx, my, mc = lax.axis_index("x"), lax.axis_index("y"), lax.axis_index("c")
        me, sibling = (mx, my, mc), (mx, my, 1 - mc)
        chips = [(1 - mx, my), (mx, 1 - my), (1 - mx, 1 - my)]

        def blk(i, px, py, pc):
            r, c = shard_shapes[i]
            j = 2 * px + py
            if kinds[i] == "row":
                return outs[i].at[pc, pl.ds(pl.multiple_of(j * r, r), r)]
            if kinds[i] == "col":
                return outs[i].at[pc, :, pl.ds(pl.multiple_of(j * c, c), c)]
            return outs[i].at[pc, j]

        def copy(i, k, block, to):
            return pltpu.make_async_remote_copy(
                src_ref=blk(i, *block), dst_ref=blk(i, *block), send_sem=send_sems.at[6 * i + k],
                recv_sem=recv_sems.at[6 * i + k], device_id=to, device_id_type=_MESH)

        sent = []
        for i in range(n):
            for j, chip in enumerate(chips):
                cp = copy(i, j, me, (*chip, mc))
                cp.start()
                sent.append(cp)
        for j, chip in enumerate(chips):
            for i in range(n):
                copy(i, j, (*chip, mc), me).wait_recv()
                fwd = copy(i, 3 + j, (*chip, mc), sibling)
                fwd.start()
                sent.append(fwd)
        for i in range(n):
            for j, chip in enumerate(chips):
                copy(i, 3 + j, (*chip, 1 - mc), me).wait_recv()
        for cp in sent:
            cp.wait_send()

    return pl.pallas_call(
        body, out_shape=[jax.ShapeDtypeStruct(b.shape, b.dtype) for b in bufs], in_specs=[_ANY] * n, out_specs=[_ANY] * n,
        input_output_aliases={i: i for i in range(n)},
        scratch_shapes=[pltpu.SemaphoreType.DMA((6 * n,)), pltpu.SemaphoreType.DMA((6 * n,))], name=name)(*bufs)


def _half(r, h):
    return pl.ds(pl.multiple_of(h * (r // 2), r // 2), r // 2)


_HBM = pl.BlockSpec(memory_space=pltpu.HBM)
_SEM = pl.BlockSpec(memory_space=pltpu.SEMAPHORE)
_EFFECT = pltpu.SideEffectType.DATAFLOW_SIDE_EFFECTING


def _sibling_copies(g_refs, land_refs, gs, views, send_sems, recv_sems):
    mx, my, mc = lax.axis_index("x"), lax.axis_index("y"), lax.axis_index("c")
    copies = []
    for i in range(len(gs)):
        if views[i] == "chip":
            src = g_refs[i].at[:, _half(gs[i].shape[1], 1 - mc)]
        else:
            src = g_refs[i].at[_half(gs[i].shape[0], 1 - mc)]
        copies.append(pltpu.make_async_remote_copy(src_ref=src, dst_ref=land_refs[i], send_sem=send_sems.at[i], recv_sem=recv_sems.at[i],
                                                   device_id=(mx, my, 1 - mc), device_id_type=_MESH))
    return copies


def _half_shape(g, view):
    return (g.shape[0], g.shape[1] // 2, g.shape[2]) if view == "chip" else (g.shape[0] // 2, g.shape[1])


def _grads_to_sibling_start(gs, views, *, name):
    n = len(gs)
    lands = [pltpu.with_memory_space_constraint(lax.empty(_half_shape(g, v), g.dtype), pltpu.HBM) for g, v in zip(gs, views)]

    def body(*refs):
        for cp in _sibling_copies(refs[:n], refs[n:2 * n], gs, views, refs[2 * n], refs[2 * n + 1]):
            cp.start()
        refs[-1][...] = jnp.zeros_like(refs[-1])

    outs = pl.pallas_call(
        body, name=name,
        out_shape=(pltpu.SemaphoreType.DMA((n,)), pltpu.SemaphoreType.DMA((n,)),
                   *[pltpu.HBM(x.shape, x.dtype) for x in list(gs) + lands], jax.ShapeDtypeStruct((SUBLANE, LANE), F32)),
        in_specs=[_HBM] * (2 * n), out_specs=(_SEM, _SEM, *[_HBM] * (2 * n), pl.BlockSpec(memory_space=pltpu.VMEM)),
        input_output_aliases={i: 2 + i for i in range(2 * n)},
        compiler_params=pltpu.CompilerParams(has_side_effects=_EFFECT),
    )(*[pltpu.with_memory_space_constraint(g, pltpu.HBM) for g in gs], *lands)
    return outs[0], outs[1], list(outs[2:2 + n]), list(outs[2 + n:2 + 2 * n]), outs[-1]


def _grads_to_sibling_wait(send_sems, recv_sems, gs, lands, views, after, *, name):
    n = len(gs)

    def body(*refs):
        for cp in _sibling_copies(refs[:n], refs[n:2 * n], gs, views, refs[2 * n], refs[2 * n + 1]):
            cp.wait_send()
            cp.wait_recv()

    outs = pl.pallas_call(
        body, name=name, out_shape=tuple(pltpu.HBM(x.shape, x.dtype) for x in list(gs) + list(lands)),
        in_specs=[_HBM] * (2 * n) + [_SEM, _SEM, _ANY], out_specs=tuple([_HBM] * (2 * n)),
        input_output_aliases={i: i for i in range(2 * n)},
        compiler_params=pltpu.CompilerParams(has_side_effects=_EFFECT),
    )(*gs, *lands, send_sems, recv_sems, after)
    return list(outs[:n]), list(outs[n:])


def _cast_place_layer(shard, l, kind, chip_idx, after, *, name):
    _, r, c = shard.shape
    tr = _row_tile(r, c * 4, 16)
    nt = r // tr

    def body(_, s_ref, *rest):
        rest[-1][...] = s_ref[...].astype(MXU_DTYPE)

    if kind == "row":
        out_spec = pl.BlockSpec((tr, c), lambda i, j_ref: (j_ref[0] * nt + i, 0))
    elif kind == "col":
        out_spec = pl.BlockSpec((tr, c), lambda i, j_ref: (i, j_ref[0]))
    else:
        out_spec = pl.BlockSpec((None, tr, c), lambda i, j_ref: (j_ref[0], i, 0))
    extra = [] if after is None else [after]
    grid_spec = pltpu.PrefetchScalarGridSpec(
        num_scalar_prefetch=1, grid=(nt,), in_specs=[pl.BlockSpec((None, tr, c), lambda i, j_ref: (l, i, 0))] + [_ANY] * len(extra),
        out_specs=out_spec)
    return pl.pallas_call(body, grid_spec=grid_spec, out_shape=jax.ShapeDtypeStruct(_gather_shape(r, c, kind)[1:], MXU_DTYPE),
                          compiler_params=_params("parallel"), name=name)(chip_idx, shard, *extra)


def _half_block(ref, kind, r, c, j, h):
    rows = _half(r, h)
    if kind == "row":
        return ref.at[pl.ds(pl.multiple_of(j * r + h * (r // 2), r // 2), r // 2)]
    if kind == "col":
        return ref.at[rows, pl.ds(pl.multiple_of(j * c, c), c)]
    return ref.at[j, rows]


def _gather_ici_copies(buf_refs, shapes, kinds, send_sems, recv_sems):
    mx, my, mc = lax.axis_index("x"), lax.axis_index("y"), lax.axis_index("c")
    chips = [(1 - mx, my), (mx, 1 - my), (1 - mx, 1 - my)]
    copies = []
    for i, (r, c) in enumerate(shapes):
        mine = _half_block(buf_refs[i], kinds[i], r, c, 2 * mx + my, mc)
        for k, (px, py) in enumerate(chips):
            copies.append(pltpu.make_async_remote_copy(
                src_ref=mine, dst_ref=mine, send_sem=send_sems.at[3 * i + k], recv_sem=recv_sems.at[3 * i + k],
                device_id=(px, py, mc), device_id_type=_MESH))
    return copies


def _gather_start(bufs, shapes, kinds, *, name):
    n = len(bufs)

    def body(*refs):
        send_sems, recv_sems, token = refs[n], refs[n + 1], refs[-1]
        for cp in _gather_ici_copies(refs[:n], shapes, kinds, send_sems, recv_sems):
            cp.start()
        token[...] = jnp.zeros_like(token)

    outs = pl.pallas_call(
        body, name=name,
        out_shape=(pltpu.SemaphoreType.DMA((3 * n,)), pltpu.SemaphoreType.DMA((3 * n,)),
                   *[pltpu.HBM(b.shape, b.dtype) for b in bufs], jax.ShapeDtypeStruct((SUBLANE, LANE), F32)),
        in_specs=[_HBM] * n, out_specs=(_SEM, _SEM, *[_HBM] * n, pl.BlockSpec(memory_space=pltpu.VMEM)),
        input_output_aliases={i: 2 + i for i in range(n)},
        compiler_params=pltpu.CompilerParams(has_side_effects=_EFFECT),
    )(*[pltpu.with_memory_space_constraint(b, pltpu.HBM) for b in bufs])
    return outs[0], outs[1], list(outs[2:2 + n]), outs[-1]


def _gather_wait(send_sems, recv_sems, bufs, shapes, kinds, after, *, name):
    n = len(bufs)

    def body(*refs):
        for cp in _gather_ici_copies(refs[:n], shapes, kinds, refs[n], refs[n + 1]):
            cp.wait_send()
            cp.wait_recv()

    outs = pl.pallas_call(
        body, name=name, out_shape=tuple(pltpu.HBM(b.shape, b.dtype) for b in bufs),
        in_specs=[_HBM] * n + [_SEM, _SEM, _ANY], out_specs=tuple([_HBM] * n), input_output_aliases={i: i for i in range(n)},
        compiler_params=pltpu.CompilerParams(has_side_effects=_EFFECT),
    )(*bufs, send_sems, recv_sems, after)
    return list(outs)


def _gather_forward(bufs, shapes, kinds, *, name):
    n = len(bufs)

    def body(*refs):
        outs = refs[n:2 * n]
        send_sems, recv_sems = refs[2 * n:]
        mx, my, mc = lax.axis_index("x"), lax.axis_index("y"), lax.axis_index("c")
        chips = [(1 - mx, my), (mx, 1 - my), (1 - mx, 1 - my)]
        copies = []
        for i, (r, c) in enumerate(shapes):
            for k, (px, py) in enumerate(chips):
                got = _half_block(outs[i], kinds[i], r, c, 2 * px + py, mc)
                cp = pltpu.make_async_remote_copy(src_ref=got, dst_ref=got, send_sem=send_sems.at[3 * i + k],
                                                  recv_sem=recv_sems.at[3 * i + k], device_id=(mx, my, 1 - mc), device_id_type=_MESH)
                cp.start()
                copies.append(cp)
        for cp in copies:
            cp.wait()

    return pl.pallas_call(
        body, out_shape=[jax.ShapeDtypeStruct(b.shape, b.dtype) for b in bufs], in_specs=[_ANY] * n, out_specs=[_ANY] * n,
        input_output_aliases={i: i for i in range(n)},
        scratch_shapes=[pltpu.SemaphoreType.DMA((3 * n,)), pltpu.SemaphoreType.DMA((3 * n,))], name=name)(*bufs)


def _chip_exchange_copies(pair_refs, land_refs, pairs, views, send_sems, recv_sems):
    mx, my, mc = lax.axis_index("x"), lax.axis_index("y"), lax.axis_index("c")
    me = 2 * mx + my
    chips = [(1 - mx, my), (mx, 1 - my), (1 - mx, 1 - my)]
    copies = []
    for i in range(len(pairs)):
        for k, (px, py) in enumerate(chips):
            j = 2 * px + py
            if views[i] == "chip":
                src = pair_refs[i].at[j]
            else:
                c = pairs[i].shape[1] // N_CHIPS
                src = pair_refs[i].at[:, pl.ds(pl.multiple_of(j * c, c), c)]
            copies.append(pltpu.make_async_remote_copy(
                src_ref=src, dst_ref=land_refs[i].at[me], send_sem=send_sems.at[3 * i + k], recv_sem=recv_sems.at[3 * i + k],
                device_id=(px, py, mc), device_id_type=_MESH))
    return copies


def _quad_shape(p, view):
    return p.shape if view == "chip" else (N_CHIPS, p.shape[0], p.shape[1] // N_CHIPS)


def _grads_to_chips_start(pairs, views, *, name):
    n = len(pairs)
    lands = [pltpu.with_memory_space_constraint(lax.empty(_quad_shape(p, v), p.dtype), pltpu.HBM) for p, v in zip(pairs, views)]

    def body(*refs):
        pair_refs, land_refs = refs[:n], refs[n:2 * n]
        send_sems, recv_sems = refs[2 * n], refs[2 * n + 1]
        token = refs[-1]
        for cp in _chip_exchange_copies(pair_refs, land_refs, pairs, views, send_sems, recv_sems):
            cp.start()
        token[...] = jnp.zeros_like(token)

    outs = pl.pallas_call(
        body, name=name,
        out_shape=(pltpu.SemaphoreType.DMA((3 * n,)), pltpu.SemaphoreType.DMA((3 * n,)),
                   *[pltpu.HBM(p.shape, p.dtype) for p in pairs], *[pltpu.HBM(l.shape, l.dtype) for l in lands],
                   jax.ShapeDtypeStruct((SUBLANE, LANE), F32)),
        in_specs=[_HBM] * (2 * n), out_specs=(_SEM, _SEM, *[_HBM] * (2 * n), pl.BlockSpec(memory_space=pltpu.VMEM)),
        input_output_aliases={i: 2 + i for i in range(2 * n)},
        compiler_params=pltpu.CompilerParams(has_side_effects=_EFFECT),
    )(*[pltpu.with_memory_space_constraint(p, pltpu.HBM) for p in pairs], *lands)
    return outs[0], outs[1], list(outs[2:2 + n]), list(outs[2 + n:2 + 2 * n]), outs[-1]


def _grads_to_chips_wait(send_sems, recv_sems, pairs, lands, views, after, *, name):
    n = len(pairs)

    def body(*refs):
        pair_refs, land_refs = refs[:n], refs[n:2 * n]
        s_sems, r_sems = refs[2 * n], refs[2 * n + 1]
        for cp in _chip_exchange_copies(pair_refs, land_refs, pairs, views, s_sems, r_sems):
            cp.wait_send()
            cp.wait_recv()

    outs = pl.pallas_call(
        body, name=name, out_shape=tuple(pltpu.HBM(x.shape, x.dtype) for x in list(pairs) + list(lands)),
        in_specs=[_HBM] * (2 * n) + [_SEM, _SEM, _ANY], out_specs=tuple([_HBM] * (2 * n)),
        input_output_aliases={i: i for i in range(2 * n)},
        compiler_params=pltpu.CompilerParams(has_side_effects=_EFFECT),
    )(*pairs, *lands, send_sems, recv_sems, after)
    return list(outs[n:])


def _grads_share(tots, *, name):
    n = len(tots)

    def body(*refs):
        ins, outs = refs[:n], refs[n:2 * n]
        send_sems, recv_sems = refs[2 * n:]
        mx, my, mc = lax.axis_index("x"), lax.axis_index("y"), lax.axis_index("c")
        copies = []
        for i in range(n):
            cp = pltpu.make_async_remote_copy(src_ref=ins[i], dst_ref=outs[i], send_sem=send_sems.at[i], recv_sem=recv_sems.at[i],
                                              device_id=(mx, my, 1 - mc), device_id_type=_MESH)
            cp.start()
            copies.append(cp)
        for cp in copies:
            cp.wait()

    return pl.pallas_call(
        body, out_shape=[jax.ShapeDtypeStruct(t.shape, t.dtype) for t in tots], in_specs=[_ANY] * n, out_specs=[_ANY] * n,
        scratch_shapes=[pltpu.SemaphoreType.DMA((n,)), pltpu.SemaphoreType.DMA((n,))], name=name)(*tots)


def _pair_sum(g, recv, view, c_idx, *, name):
    def body(c_ref, a_ref, b_ref, o_ref):
        o_ref[...] = (a_ref[...] + b_ref[...]).astype(WIRE_DTYPE)

    if view == "chip":
        nch, r, c = g.shape
        tr = _row_tile(r // 2, nch * c * 4, 16)
        gv = g.reshape(nch, 2, r // 2, c)
        grid = ((r // 2) // tr,)
        in_specs = [pl.BlockSpec((nch, None, tr, c), lambda i, c_ref: (0, c_ref[0], i, 0)),
                    pl.BlockSpec((nch, tr, c), lambda i, c_ref: (0, i, 0))]
        out_spec = pl.BlockSpec((nch, tr, c), lambda i, c_ref: (0, i, 0))
        sem = ("parallel",)
    else:
        r, c4 = g.shape
        tr = _row_tile(r // 2, c4 * 4, 16)
        gv = g.reshape(2, r // 2, c4)
        grid = ((r // 2) // tr,)
        in_specs = [pl.BlockSpec((None, tr, c4), lambda i, c_ref: (c_ref[0], i, 0)), pl.BlockSpec((tr, c4), lambda i, c_ref: (i, 0))]
        out_spec = pl.BlockSpec((tr, c4), lambda i, c_ref: (i, 0))
        sem = ("parallel",)
    grid_spec = pltpu.PrefetchScalarGridSpec(num_scalar_prefetch=1, grid=grid, in_specs=in_specs, out_specs=out_spec)
    return pl.pallas_call(body, grid_spec=grid_spec, out_shape=jax.ShapeDtypeStruct(recv.shape, WIRE_DTYPE),
                          compiler_params=_params(*sem), name=name)(c_idx, gv, recv)


def _quad_sum(gs, recvs, quads, view, chip_idx, c_idx, *, name):
    nl = len(quads)
    nch, rh, c = quads[0].shape
    tr = _row_tile(rh, c * 4, 16)

    def body(_, __, *refs):
        o_ref = refs[-1]
        per = nch + 1
        for l in range(nl):
            grp = refs[l * per:(l + 1) * per]
            acc = grp[0][...] + grp[1][...]
            for r in grp[2:]:
                acc = acc + r[...].astype(F32)
            o_ref[l] = acc

    if view == "chip":
        own = [pl.BlockSpec((None, None, tr, c), lambda i, j, h: (j[0], h[0], i, 0)),
               pl.BlockSpec((None, tr, c), lambda i, j, h: (j[0], i, 0))]
        gviews = [g.reshape(nch, 2, rh, c) for g in gs]
    else:
        own = [pl.BlockSpec((None, tr, c), lambda i, j, h: (h[0], i, j[0])), pl.BlockSpec((tr, c), lambda i, j, h: (i, j[0]))]
        gviews = [g.reshape(2, rh, nch * c) for g in gs]
    assert nch & (nch - 1) == 0
    got = [pl.BlockSpec((None, tr, c), functools.partial(lambda i, j, h, k: ((j[0] + k) & (nch - 1), i, 0), k=k))
           for k in range(1, nch)]
    ins = []
    for l in range(nl):
        ins += [gviews[l], recvs[l]] + [quads[l]] * (nch - 1)
    grid_spec = pltpu.PrefetchScalarGridSpec(
        num_scalar_prefetch=2, grid=(rh // tr,), in_specs=(own + got) * nl,
        out_specs=pl.BlockSpec((nl, tr, c), lambda i, j, h: (0, i, 0)))
    return pl.pallas_call(body, grid_spec=grid_spec, out_shape=jax.ShapeDtypeStruct((nl, rh, c), F32),
                          compiler_params=_params("parallel"), name=name)(chip_idx, c_idx, *ins)


def _sum_devices(g8, own, dev_idx, *, name):
    k, rows, cols = g8.shape

    def body(d_ref, a_ref, x_ref, o_ref):
        acc = None
        for i in range(k):
            term = jnp.where(d_ref[0] == i, x_ref[...], a_ref[i])
            acc = term if acc is None else acc + term
        o_ref[...] = acc

    grid_spec = pltpu.PrefetchScalarGridSpec(
        num_scalar_prefetch=1, grid=(1,),
        in_specs=[pl.BlockSpec((k, rows, cols), lambda i, d_ref: (0, 0, 0)), pl.BlockSpec((rows, cols), lambda i, d_ref: (0, 0))],
        out_specs=pl.BlockSpec((rows, cols), lambda i, d_ref: (0, 0)))
    return pl.pallas_call(body, grid_spec=grid_spec, out_shape=jax.ShapeDtypeStruct((rows, cols), g8.dtype),
                          compiler_params=_params("arbitrary"), name=name)(dev_idx, g8, own)


def _adamw(w, g, m, v, *, name):
    rows, cols = w.shape
    tr = rows
    for cand in (256, 128, 64, 32, 16, 8):
        if rows % cand == 0 and cand * cols <= 512 * 1024:
            tr = cand
            break
    c1 = 1.0 - ADAM_B1 ** ADAM_STEP
    c2 = 1.0 - ADAM_B2 ** ADAM_STEP

    def body(w_ref, g_ref, m_ref, v_ref, d_ref, nm_ref, nv_ref):
        gv = g_ref[...]
        nm = ADAM_B1 * m_ref[...] + (1.0 - ADAM_B1) * gv
        nv = ADAM_B2 * v_ref[...] + (1.0 - ADAM_B2) * (gv * gv)
        d_ref[...] = -ADAM_LR * ((nm / c1) / (jnp.sqrt(nv / c2) + ADAM_EPS) + ADAM_WD * w_ref[...])
        nm_ref[...] = nm
        nv_ref[...] = nv

    spec = pl.BlockSpec((tr, cols), lambda i: (i, 0))
    shp = jax.ShapeDtypeStruct((rows, cols), F32)
    return pl.pallas_call(body, grid=(rows // tr,), in_specs=[spec] * 4, out_specs=[spec] * 3, out_shape=[shp] * 3,
                          compiler_params=_params("parallel"), name=name)(w, g, m, v)


def _adamw_halves(w, m, v, mine, other, c_idx, *, name):
    nl, r, c = w.shape
    rh = r // 2
    tr = _row_tile(rh, c * 4)
    c1 = 1.0 - ADAM_B1 ** ADAM_STEP
    c2 = 1.0 - ADAM_B2 ** ADAM_STEP

    def body(c_ref, w_ref, m_ref, v_ref, a_ref, b_ref, g_ref, d_ref, nm_ref, nv_ref):
        gv = jnp.where(pl.program_id(1) == c_ref[0], a_ref[...], b_ref[...])
        nm = ADAM_B1 * m_ref[...] + (1.0 - ADAM_B1) * gv
        nv = ADAM_B2 * v_ref[...] + (1.0 - ADAM_B2) * (gv * gv)
        g_ref[...] = gv
        d_ref[...] = -ADAM_LR * ((nm / c1) / (jnp.sqrt(nv / c2) + ADAM_EPS) + ADAM_WD * w_ref[...])
        nm_ref[...] = nm
        nv_ref[...] = nv

    nt = rh // tr
    full = pl.BlockSpec((None, tr, c), lambda l, h, i, c_ref: (l, h * nt + i, 0))
    half_mine = pl.BlockSpec((None, tr, c), lambda l, h, i, c_ref: (l, jnp.where(h == c_ref[0], i, 0), 0))
    half_other = pl.BlockSpec((None, tr, c), lambda l, h, i, c_ref: (l, jnp.where(h == c_ref[0], 0, i), 0))
    grid_spec = pltpu.PrefetchScalarGridSpec(num_scalar_prefetch=1, grid=(nl, 2, rh // tr),
                                             in_specs=[full] * 3 + [half_mine, half_other], out_specs=[full] * 4)
    shp = jax.ShapeDtypeStruct((nl, r, c), F32)
    return pl.pallas_call(body, grid_spec=grid_spec, out_shape=[shp] * 4, compiler_params=_params("arbitrary", "arbitrary", "arbitrary"),
                          name=name)(c_idx, w, m, v, mine, other)


WEIGHTS = ["mem_ln_g", "mem_ln_b", "w_in", "sg_ln_g", "sg_ln_b", "sg_w", "sg_b", "conv_w", "conv_b", "dt_bias", "a_log",
           "d_skip", "ssm_norm_g", "p_a", "p_b", "w_mix_o", "w_xq", "w_xkv", "w_xo", "w_ffn_in", "w_ffn_out", "ln_g", "ln_b"]
ARG_NAMES = ["x", "mem"] + WEIGHTS + ["loss_target"] + ["m_" + n for n in WEIGHTS] + ["v_" + n for n in WEIGHTS]
BIG = {"w_in": (1, (1024, 9248)), "p_a": (0, (1024, 1024)), "p_b": (0, (2048, 1024)), "w_mix_o": (0, (1024, 1024)),
       "w_xq": (0, (1024, 1024)), "w_xkv": (1, (1024, 2048)), "w_xo": (0, (1024, 1024)), "w_ffn_in": (1, (1024, 5632)),
       "w_ffn_out": (0, (2816, 1024))}
SMALL_SHARDED = {"conv_w": (4, 3072), "ln_g": (3, 1024), "ln_b": (3, 1024)}
SMALL = [n for n in WEIGHTS if n not in BIG]
W_IN_MAP = ((0, 4096, "main", 0), (4096, 7168, "main", XBC_COL0), (7168, 7200, "dt", 0), (7200, 9248, "main", GAB_COL0))
W_IN_SHARD = 9248 // N_CHIPS


def _w_in_chip_major(gm, gd):
    src = {"main": gm, "dt": gd}
    blocks = []
    for j in range(N_CHIPS):
        lo, hi = j * W_IN_SHARD, (j + 1) * W_IN_SHARD
        parts = [src[k][:, o + max(lo, a) - a:o + min(hi, b) - a] for a, b, k, o in W_IN_MAP if max(lo, a) < min(hi, b)]
        blocks.append(jnp.concatenate(parts, axis=1))
    return jnp.stack(blocks)


def _w_in_reassemble(wc):
    def cols(a, b):
        out = []
        for j in range(N_CHIPS):
            lo, hi = max(a, j * W_IN_SHARD), min(b, (j + 1) * W_IN_SHARD)
            if lo < hi:
                out.append(wc[j][:, lo - j * W_IN_SHARD:hi - j * W_IN_SHARD])
        return out

    main = sorted((m for m in W_IN_MAP if m[2] == "main"), key=lambda m: m[3])
    w_main = jnp.concatenate([p for a, b, _, _ in main for p in cols(a, b)], axis=1)
    (a, b, _, _), = [m for m in W_IN_MAP if m[2] == "dt"]
    w_dt = jnp.pad(jnp.concatenate(cols(a, b), axis=1), ((0, 0), (0, HEAD_PAD - (b - a))))
    return w_main, w_dt
GATHER_KIND = {"w_in": "chip", "p_a": "row", "p_b": "row", "w_mix_o": "row", "w_xq": "row", "w_xkv": "col", "w_xo": "row",
               "w_ffn_in": "col", "w_ffn_out": "row", "conv_w": "chip", "ln_g": "chip", "ln_b": "chip"}
GRAD_VIEW = {n: ("col" if k == "col" else "chip") for n, k in GATHER_KIND.items() if n in BIG}


def _shard_shape(name):
    axis, (r, c) = BIG[name]
    return (r // N_CHIPS, c) if axis == 0 else (r, c // N_CHIPS)


def _pad_rows(flat, cols, row_mult):
    n = flat.shape[0]
    rows = -(-n // cols)
    rows = -(-rows // row_mult) * row_mult
    return jnp.pad(flat, (0, rows * cols - n)).reshape(rows, cols)


def _gather_small_params(a, chip):
    names = list(SMALL_SHARDED)
    kinds = [GATHER_KIND[n] for n in names]
    bufs = [_cast_place(a[n], GATHER_KIND[n], F32, chip.reshape(1), name=f"place_{n}") for n in names]
    outs = _gather_params(bufs, [a[n].shape[1:] for n in names], kinds, name="gather_small_params")
    full = {}
    for n, o in zip(names, outs):
        _, _, r, c = o.shape
        full[n] = jnp.transpose(o, (0, 2, 1, 3)).reshape(DEPTH, r, N_CHIPS * c)
    return full


GATHER_GROUPS = (("w_in",), tuple(n for n in BIG if n != "w_in"))


def _gather_group_start(a, l, names, chip, after, *, tag):
    bufs = [_cast_place_layer(a[n], l, GATHER_KIND[n], chip.reshape(1), after, name=f"place_{n}_l{l}") for n in names]
    return _gather_start(bufs, [a[n].shape[1:] for n in names], [GATHER_KIND[n] for n in names], name=f"gather_start_{tag}")


def _gather_group_finish(a, names, flight, after, *, tag):
    send_sems, recv_sems, bufs, token = flight
    shapes, kinds = [a[n].shape[1:] for n in names], [GATHER_KIND[n] for n in names]
    bufs = _gather_wait(send_sems, recv_sems, bufs, shapes, kinds, token if after is None else after, name=f"gather_wait_{tag}")
    full = dict(zip(names, _gather_forward(bufs, shapes, kinds, name=f"gather_forward_{tag}")))
    if "w_in" in full:
        full["w_main"], full["w_dt"] = _w_in_reassemble(full.pop("w_in"))
    return full


def _layer_weights(a, big, small, l):
    w = dict(big)
    for n in SMALL_SHARDED:
        w[n] = small[n][l]
    for n in ["sg_ln_g", "sg_ln_b", "sg_w", "conv_b", "ssm_norm_g"]:
        w[n] = a[n][l]
    w["sg_bcol"] = a["sg_b"][l][..., None]
    for n in ["dt_bias", "a_log"]:
        w[n + "8"] = _pad_heads(a[n][l])
    w["d_skipx"] = _expand_heads(a["d_skip"][l])
    return w


def _grad_views(grads, names):
    gs = []
    for n in names:
        axis, _ = BIG[n]
        r, c = _shard_shape(n)
        if n == "w_in":
            gs.append(_w_in_chip_major(grads["w_main"], grads["w_dt"]))
        elif axis == 0:
            gs.append(grads[n].reshape(N_CHIPS, r, c))
        else:
            gs.append(grads[n])
    return gs


class _GradExchange:
    def __init__(self, grads, names, c_idx, tag):
        self.names, self.c_idx, self.tag = names, c_idx, tag
        self.views = [GRAD_VIEW[n] for n in names]
        self.gs = _grad_views(grads, names)

    def start(self):
        self.sems = _grads_to_sibling_start(self.gs, self.views, name=f"grads_to_sibling_start_{self.tag}")
        return self.sems[4]

    def cross(self, after):
        send_sems, recv_sems, gs, lands, token = self.sems
        self.gs, self.recv = _grads_to_sibling_wait(send_sems, recv_sems, gs, lands, self.views, token if after is None else after,
                                                    name=f"grads_to_sibling_wait_{self.tag}")
        cpre = self.c_idx.reshape(1)
        pairs = [_pair_sum(g, rv, v, cpre, name=f"grads_pair_sum_{n}_{self.tag}")
                 for g, rv, v, n in zip(self.gs, self.recv, self.views, self.names)]
        self.sems = _grads_to_chips_start(pairs, self.views, name=f"grads_to_chips_start_{self.tag}")
        return self.sems[4]

    def finish(self, after):
        send_sems, recv_sems, pairs, lands, _ = self.sems
        quads = _grads_to_chips_wait(send_sems, recv_sems, pairs, lands, self.views, after, name=f"grads_to_chips_wait_{self.tag}")
        return {n: (g, rv, q) for n, g, rv, q in zip(self.names, self.gs, self.recv, quads)}


def _finish_big_grads(parts, c_idx, chip):
    tots = [_quad_sum([parts[l][n][0] for l in range(DEPTH)], [parts[l][n][1] for l in range(DEPTH)],
                      [parts[l][n][2] for l in range(DEPTH)], GRAD_VIEW[n], chip.reshape(1), c_idx.reshape(1),
                      name=f"grads_chip_sum_{n}") for n in BIG]
    others = _grads_share(tots, name="grads_share")
    return {n: (t, o) for n, t, o in zip(BIG, tots, others)}


def _direct_copies(x_ref, land_ref, send_sems, recv_sems):
    mx, my, mc = lax.axis_index("x"), lax.axis_index("y"), lax.axis_index("c")
    me = 4 * mx + 2 * my + mc
    copies = []
    for k in range(N_DEV - 1):
        f = k + 1
        to = (mx ^ (f >> 2 & 1), my ^ (f >> 1 & 1), mc ^ (f & 1))
        copies.append(pltpu.make_async_remote_copy(src_ref=x_ref, dst_ref=land_ref.at[me], send_sem=send_sems.at[k],
                                                   recv_sem=recv_sems.at[k], device_id=to, device_id_type=_MESH))
    return copies


def _all_gather8_start(x, *, name):
    land = pltpu.with_memory_space_constraint(lax.empty((N_DEV,) + x.shape, x.dtype), pltpu.HBM)

    def body(x_ref, land_ref, send_sems, recv_sems, x_out, land_out, token):
        for cp in _direct_copies(x_ref, land_ref, send_sems, recv_sems):
            cp.start()
        token[...] = jnp.zeros_like(token)

    n = N_DEV - 1
    return pl.pallas_call(
        body, name=name,
        out_shape=(pltpu.SemaphoreType.DMA((n,)), pltpu.SemaphoreType.DMA((n,)), pltpu.HBM(x.shape, x.dtype),
                   pltpu.HBM(land.shape, land.dtype), jax.ShapeDtypeStruct((SUBLANE, LANE), F32)),
        in_specs=[_HBM, _HBM], out_specs=(_SEM, _SEM, _HBM, _HBM, pl.BlockSpec(memory_space=pltpu.VMEM)),
        input_output_aliases={0: 2, 1: 3}, compiler_params=pltpu.CompilerParams(has_side_effects=_EFFECT),
    )(pltpu.with_memory_space_constraint(x, pltpu.HBM), land)


def _all_gather8_wait(send_sems, recv_sems, x, land, after, *, name):
    def body(x_ref, land_ref, s_sems, r_sems, _, x_out, land_out):
        for cp in _direct_copies(x_ref, land_ref, s_sems, r_sems):
            cp.wait_send()
            cp.wait_recv()

    return pl.pallas_call(
        body, name=name, out_shape=(pltpu.HBM(x.shape, x.dtype), pltpu.HBM(land.shape, land.dtype)),
        in_specs=[_HBM, _HBM, _SEM, _SEM, _ANY], out_specs=(_HBM, _HBM), input_output_aliases={0: 0, 1: 1},
        compiler_params=pltpu.CompilerParams(has_side_effects=_EFFECT),
    )(x, land, send_sems, recv_sems, after)


def _pack_small(small):
    return _pad_rows(jnp.concatenate([small[n].reshape(-1) for n in small]), LANE, SUBLANE)


def _unpack_small(small, g8, packed, chip, c_idx, *, name):
    names = list(small)
    tot = _sum_devices(g8, packed, (2 * chip + c_idx).reshape(1), name=name).reshape(-1)
    out, off = {}, 0
    for n in names:
        sz = small[n].size
        full = tot[off:off + sz].reshape(small[n].shape)
        off += sz
        if n in SMALL_SHARDED:
            cs = SMALL_SHARDED[n][1] // N_CHIPS
            full = lax.dynamic_slice_in_dim(full, chip * cs, cs, axis=-1)
        out[n] = full
    return out


def kernel(x, mem, mem_ln_g, mem_ln_b, w_in, sg_ln_g, sg_ln_b, sg_w, sg_b, conv_w, conv_b, dt_bias, a_log, d_skip, ssm_norm_g, p_a, p_b, w_mix_o, w_xq, w_xkv, w_xo, w_ffn_in, w_ffn_out, ln_g, ln_b, loss_target, m_mem_ln_g, m_mem_ln_b, m_w_in, m_sg_ln_g, m_sg_ln_b, m_sg_w, m_sg_b, m_conv_w, m_conv_b, m_dt_bias, m_a_log, m_d_skip, m_ssm_norm_g, m_p_a, m_p_b, m_w_mix_o, m_w_xq, m_w_xkv, m_w_xo, m_w_ffn_in, m_w_ffn_out, m_ln_g, m_ln_b, v_mem_ln_g, v_mem_ln_b, v_w_in, v_sg_ln_g, v_sg_ln_b, v_sg_w, v_sg_b, v_conv_w, v_conv_b, v_dt_bias, v_a_log, v_d_skip, v_ssm_norm_g, v_p_a, v_p_b, v_w_mix_o, v_w_xq, v_w_xkv, v_w_xo, v_w_ffn_in, v_w_ffn_out, v_ln_g, v_ln_b):
    a = dict(zip(ARG_NAMES, (x, mem, mem_ln_g, mem_ln_b, w_in, sg_ln_g, sg_ln_b, sg_w, sg_b, conv_w, conv_b, dt_bias, a_log, d_skip, ssm_norm_g, p_a, p_b, w_mix_o, w_xq, w_xkv, w_xo, w_ffn_in, w_ffn_out, ln_g, ln_b, loss_target, m_mem_ln_g, m_mem_ln_b, m_w_in, m_sg_ln_g, m_sg_ln_b, m_sg_w, m_sg_b, m_conv_w, m_conv_b, m_dt_bias, m_a_log, m_d_skip, m_ssm_norm_g, m_p_a, m_p_b, m_w_mix_o, m_w_xq, m_w_xkv, m_w_xo, m_w_ffn_in, m_w_ffn_out, m_ln_g, m_ln_b, v_mem_ln_g, v_mem_ln_b, v_w_in, v_sg_ln_g, v_sg_ln_b, v_sg_w, v_sg_b, v_conv_w, v_conv_b, v_dt_bias, v_a_log, v_d_skip, v_ssm_norm_g, v_p_a, v_p_b, v_w_mix_o, v_w_xq, v_w_xkv, v_w_xo, v_w_ffn_in, v_w_ffn_out, v_ln_g, v_ln_b)))
    c_idx = lax.axis_index("c").astype(jnp.int32)
    chip = (2 * lax.axis_index("x") + lax.axis_index("y")).astype(jnp.int32)

    small = _gather_small_params(a, chip)
    ga, gb = GATHER_GROUPS
    flights = {(0, 0): _gather_group_start(a, 0, ga, chip, small["ln_b"], tag="l0_a")}
    flights[0, 1] = _gather_group_start(a, 0, gb, chip, flights[0, 0][3], tag="l0_b")

    def layer_weights(after, l):
        first = _gather_group_finish(a, ga, flights[l, 0], after if l else flights[l, 1][3], tag=f"l{l}_a")

        def rest(w, after_b):
            more = _gather_group_finish(a, gb, flights[l, 1], after_b, tag=f"l{l}_b")
            if l + 1 < DEPTH:
                flights[l + 1, 0] = _gather_group_start(a, l + 1, ga, chip, more["p_a"], tag=f"l{l + 1}_a")
                flights[l + 1, 1] = _gather_group_start(a, l + 1, gb, chip, flights[l + 1, 0][3], tag=f"l{l + 1}_b")
                more["p_a"] = more["p_a"] + flights[l + 1, 1][3][0, 0].astype(MXU_DTYPE)
            return {k: v for k, v in {**w, **more}.items() if k != "rest"}

        return dict(_layer_weights(a, first, small, l), rest=rest)

    layers = [functools.partial(layer_weights, l=l) for l in range(DEPTH)]
    exchanges, seen, small_flight = [], {}, {}

    def start_exchange(l, names, grads_l):
        ex = _GradExchange(grads_l, names, c_idx, f"l{l}_{names[0]}")
        tokens = [ex.start()]
        if exchanges:
            tokens.append(exchanges[-1][1].cross(tokens[0]))
        exchanges.append((l, ex))
        seen[l] = grads_l
        if l == 0 and names == GRAD_GROUPS[-1]:
            tokens.append(ex.cross(None))
            small = {}
            for n in SMALL:
                if n.startswith("mem_ln"):
                    continue
                per_layer = []
                for k in range(DEPTH):
                    g = seen[k][n]
                    if n in ("dt_bias", "a_log", "d_skip"):
                        g = g[0, :SSM_HEADS]
                    per_layer.append(g.reshape(a[n].shape[1:-1] + (-1,)))
                small[n] = jnp.stack(per_layer)
            small_flight["small"] = small
            small_flight["sems"] = _all_gather8_start(_pack_small(small), name="gather_small_grads_start")
            tokens.append(small_flight["sems"][4])
        return sum(tokens[1:], tokens[0])

    lsum, grad_x, grads, d_mem_g, d_mem_b = _local_step(x, mem, loss_target, mem_ln_g, mem_ln_b, layers, start_exchange)
    loss = lax.psum(0.5 * jnp.sum(lsum) / D_MODEL, ("x", "y", "c"))

    parts = [{} for _ in range(DEPTH)]
    for l, ex in exchanges:
        parts[l].update(ex.finish(grad_x))
    halves = _finish_big_grads(parts, c_idx, chip)
    gw = {}
    send_sems, recv_sems, packed, land, _ = small_flight["sems"]
    packed, g8 = _all_gather8_wait(send_sems, recv_sems, packed, land, grad_x, name="gather_small_grads_wait")
    gw.update(_unpack_small(small_flight["small"], g8, packed, chip, c_idx, name="small_grads_sum"))
    mem_small = {"mem_ln_g": d_mem_g, "mem_ln_b": d_mem_b}
    mem_packed = _pack_small(mem_small)
    gw.update(_unpack_small(mem_small, _all_gather8(mem_packed, name="gather_mem_ln_grads"), mem_packed, chip, c_idx,
                            name="mem_ln_grads_sum"))

    delta, new_m, new_v = {}, {}, {}
    for n in BIG:
        mine, other = halves[n]
        gw[n], delta[n], new_m[n], new_v[n] = _adamw_halves(a[n], a["m_" + n], a["v_" + n], mine, other, c_idx.reshape(1),
                                                             name=f"adamw_{n}")
    for n in SMALL:
        shp = a[n].shape
        view = (-1, LANE) if a[n].size % LANE == 0 else (1, -1)
        outs = _adamw(*[v.reshape(view) for v in (a[n], gw[n], a["m_" + n], a["v_" + n])], name=f"adamw_{n}")
        delta[n], new_m[n], new_v[n] = (o.reshape(shp) for o in outs)
    return (loss, grad_x, *[gw[n].reshape(a[n].shape) for n in WEIGHTS], *[delta[n] for n in WEIGHTS],
            *[new_m[n] for n in WEIGHTS], *[new_v[n] for n in WEIGHTS])
```

```python
import functools
import math

import jax
import jax.numpy as jnp
from jax import lax
from jax.experimental import pallas as pl
from jax.experimental.pallas import tpu as pltpu

F32 = jnp.float32
MXU_DTYPE = jnp.bfloat16
WIRE_DTYPE = jnp.bfloat16
STASH_DTYPE = jnp.bfloat16

D_MODEL = 1024
DEPTH = 2
CHUNK = 128
SG_GROUPS = 8
SSM_INNER = 2048
SSM_HEADDIM = 64
SSM_HEADS = 32
SSM_STATE = 128
SSM_GROUPS = 4
SSM_CONV = 4
SSM_CONV_DIM = 3072
X_HEADS = 4
X_HEADDIM = 256
FFN_HIDDEN = 2816
ALPHA = float((2 * DEPTH) ** 0.25)
LN_EPS = 1e-5
RMS_EPS = 1e-5
ADAM_LR = 0.001
ADAM_B1 = 0.9
ADAM_B2 = 0.999
ADAM_EPS = 1e-08
ADAM_WD = 0.01
ADAM_STEP = 10

MAIN_COLS = 9216
UVZ_COLS = 4096
GAB_COL0 = 4096
XBC_COL0 = 6144
HEAD_PAD = 128

VMEM_LIMIT = 56 * 1024 * 1024
BLOCK_BYTES = 2 * 1024 * 1024
ROW_TILES = (512, 256, 128)
LANE = 128
SUBLANE = 8

N_CHIPS = 4
N_DEV = 8


def _pick(n, cands):
    for c in cands:
        if n % c == 0:
            return c
    return n


MM_TILE_MAX = 1408
MM_OPERAND_BYTES = 8 * 1024 * 1024


def _div_tile(n, limit):
    best = None
    for t in range(LANE, min(n, limit) + 1, LANE):
        if n % t == 0:
            best = t
    return n if best is None else best


def _params(*sem):
    return pltpu.CompilerParams(dimension_semantics=tuple(sem), vmem_limit_bytes=VMEM_LIMIT)


_ANY = pl.BlockSpec(memory_space=pl.ANY)
_MESH = pl.DeviceIdType.MESH


def _nt(a, b):
    return lax.dot_general(a, b, (((1,), (1,)), ((), ())), preferred_element_type=F32)


def _tn(a, b):
    return lax.dot_general(a, b, (((0,), (0,)), ((), ())), preferred_element_type=F32)


def _nn(a, b):
    return jnp.dot(a, b, preferred_element_type=F32)


def _sigmoid(x):
    return 0.5 * jnp.tanh(0.5 * x) + 0.5


def _split3(v):
    def top(x):
        bits = lax.bitcast_convert_type(x, jnp.uint32) & jnp.uint32(0xFFFF0000)
        return lax.bitcast_convert_type(bits, F32)

    v1 = top(v)
    r1 = v - v1
    v2 = top(r1)
    v3 = r1 - v2
    return v1.astype(jnp.bfloat16), v2.astype(jnp.bfloat16), v3.astype(jnp.bfloat16)


def _dot_exact(a, b, dn, data):
    if data == 0:
        mat = b.astype(jnp.bfloat16)
        return sum(lax.dot_general(p, mat, dn, preferred_element_type=F32) for p in _split3(a))
    mat = a.astype(jnp.bfloat16)
    return sum(lax.dot_general(mat, p, dn, preferred_element_type=F32) for p in _split3(b))


_DN_NN = (((1,), (0,)), ((), ()))
_DN_TN = (((0,), (0,)), ((), ()))


def _gelu(x):
    return 0.5 * x * (1.0 + lax.erf(x * (2.0 ** -0.5)))


def _gelu_grad(x):
    return 0.5 * (1.0 + lax.erf(x * (2.0 ** -0.5))) + x * jnp.exp(-0.5 * x * x) * (1.0 / math.sqrt(2.0 * math.pi))


def _mm(a, b, *, ta=False, tb=False, out_dtype=F32, after=None, name):
    if ta:
        kdim, m = a.shape
    else:
        m, kdim = a.shape
    if tb:
        n, k2 = b.shape[-2:]
    else:
        k2, n = b.shape[-2:]
    assert kdim == k2, (a.shape, b.shape, ta, tb)
    tm = _div_tile(m, MM_TILE_MAX)
    tn = _div_tile(n, MM_TILE_MAX)
    tk = _div_tile(kdim, MM_OPERAND_BYTES // (tm * a.dtype.itemsize + tn * b.dtype.itemsize))
    nk = kdim // tk
    dn = (((0 if ta else 1,), (1 if tb else 0,)), ((), ()))

    extra = [] if after is None else [after]

    def body(a_ref, b_ref, *rest):
        o_ref = rest[len(extra)]
        d = lax.dot_general(a_ref[...].astype(MXU_DTYPE), b_ref[...].astype(MXU_DTYPE), dn, preferred_element_type=F32)
        if nk == 1:
            o_ref[...] = d.astype(out_dtype)
            return
        acc_ref = rest[len(extra) + 1]
        k = pl.program_id(2)

        @pl.when(k == 0)
        def _():
            acc_ref[...] = d

        @pl.when(jnp.logical_and(k > 0, k < nk - 1))
        def _():
            acc_ref[...] += d

        @pl.when(k == nk - 1)
        def _():
            o_ref[...] = (acc_ref[...] + d).astype(out_dtype)

    a_spec = pl.BlockSpec((tk, tm), lambda i, j, k: (k, i)) if ta else pl.BlockSpec((tm, tk), lambda i, j, k: (i, k))
    b_spec = pl.BlockSpec((tn, tk), lambda i, j, k: (j, k)) if tb else pl.BlockSpec((tk, tn), lambda i, j, k: (k, j))
    return pl.pallas_call(
        body, grid=(m // tm, n // tn, nk), in_specs=[a_spec, b_spec] + [_ANY] * len(extra),
        out_specs=pl.BlockSpec((tm, tn), lambda i, j, k: (i, j)),
        out_shape=jax.ShapeDtypeStruct((m, n), out_dtype),
        scratch_shapes=[pltpu.VMEM((tm, tn), F32)] if nk > 1 else [],
        compiler_params=_params("parallel", "parallel", "arbitrary"), name=name)(a, b, *extra)


def _row_spec(tm, c, col=0):
    return pl.BlockSpec((tm, c), lambda i: (i, col))


def _par_spec(shape):
    nd = len(shape)
    return pl.BlockSpec(shape, lambda i: (0,) * nd)


def _ln_fwd(x, f, g, b, *, name):
    t, c = x.shape
    tm = _pick(t, ROW_TILES)
    has_f = f is not None

    def body(*refs):
        if has_f:
            x_ref, f_ref, g_ref, b_ref, y_ref, yb_ref, xh_ref, rs_ref = refs
            r = ALPHA * x_ref[...] + f_ref[...]
        else:
            x_ref, g_ref, b_ref, y_ref, yb_ref, xh_ref, rs_ref = refs
            r = x_ref[...]
        mu = jnp.mean(r, axis=-1, keepdims=True)
        xc = r - mu
        var = jnp.mean(xc * xc, axis=-1, keepdims=True)
        rstd = lax.rsqrt(var + LN_EPS)
        xh = xc * rstd
        y = xh * g_ref[...] + b_ref[...]
        y_ref[...] = y
        yb_ref[...] = y.astype(MXU_DTYPE)
        xh_ref[...] = xh
        rs_ref[...] = jnp.broadcast_to(rstd, rs_ref.shape)

    ins = [x] + ([f] if has_f else []) + [g.reshape(1, c), b.reshape(1, c)]
    in_specs = [_row_spec(tm, c)] * (2 if has_f else 1) + [_par_spec((1, c))] * 2
    return pl.pallas_call(
        body, grid=(t // tm,), in_specs=in_specs,
        out_specs=[_row_spec(tm, c), _row_spec(tm, c), _row_spec(tm, c), _row_spec(tm, LANE)],
        out_shape=[jax.ShapeDtypeStruct((t, c), F32), jax.ShapeDtypeStruct((t, c), MXU_DTYPE),
                   jax.ShapeDtypeStruct((t, c), F32), jax.ShapeDtypeStruct((t, LANE), F32)],
        compiler_params=_params("parallel"), name=name)(*ins)


def _ln_bwd(addends, scales, xh, rs, g, *, name):
    t, c = xh.shape
    tm = _pick(t, ROW_TILES)
    na = len(addends)

    def body(*refs):
        a_refs = refs[:na]
        xh_ref, rs_ref, g_ref, dp_ref, dpb_ref, dg_ref, db_ref = refs[na:]

        @pl.when(pl.program_id(0) == 0)
        def _():
            dg_ref[...] = jnp.zeros_like(dg_ref)
            db_ref[...] = jnp.zeros_like(db_ref)

        dy = None
        for s, r in zip(scales, a_refs):
            term = r[...] if s == 1.0 else s * r[...]
            dy = term if dy is None else dy + term
        xhv = xh_ref[...]
        dxh = dy * g_ref[...]
        m1 = jnp.mean(dxh, axis=-1, keepdims=True)
        m2 = jnp.mean(dxh * xhv, axis=-1, keepdims=True)
        dp = rs_ref[:, 0:1] * (dxh - m1 - xhv * m2)
        dp_ref[...] = dp
        dpb_ref[...] = dp.astype(MXU_DTYPE)
        dg_ref[...] += jnp.sum(dy * xhv, axis=0, keepdims=True)
        db_ref[...] += jnp.sum(dy, axis=0, keepdims=True)

    in_specs = [_row_spec(tm, c)] * (na + 1) + [_row_spec(tm, LANE), _par_spec((1, c))]
    return pl.pallas_call(
        body, grid=(t // tm,), in_specs=in_specs,
        out_specs=[_row_spec(tm, c), _row_spec(tm, c), _par_spec((1, c)), _par_spec((1, c))],
        out_shape=[jax.ShapeDtypeStruct((t, c), F32), jax.ShapeDtypeStruct((t, c), MXU_DTYPE),
                   jax.ShapeDtypeStruct((1, c), F32), jax.ShapeDtypeStruct((1, c), F32)],
        compiler_params=_params("arbitrary"), name=name)(*addends, xh, rs, g.reshape(1, c))


def _add_scaled(addends, scales, *, name):
    t, c = addends[0].shape
    tm = _pick(t, ROW_TILES)
    na = len(addends)

    def body(*refs):
        acc = None
        for s, r in zip(scales, refs[:na]):
            term = r[...] if s == 1.0 else s * r[...]
            acc = term if acc is None else acc + term
        refs[na][...] = acc

    return pl.pallas_call(
        body, grid=(t // tm,), in_specs=[_row_spec(tm, c)] * na, out_specs=_row_spec(tm, c),
        out_shape=jax.ShapeDtypeStruct((t, c), F32), compiler_params=_params("parallel"), name=name)(*addends)


def _loss_head(y, tgt, *, name):
    t, c = y.shape
    tm = _pick(t, ROW_TILES)

    def body(y_ref, t_ref, dy_ref, ls_ref):
        @pl.when(pl.program_id(0) == 0)
        def _():
            ls_ref[...] = jnp.zeros_like(ls_ref)

        e = y_ref[...] - t_ref[...]
        dy_ref[...] = e * (1.0 / c)
        ls_ref[...] += jnp.sum(e * e, axis=0, keepdims=True)

    return pl.pallas_call(
        body, grid=(t // tm,), in_specs=[_row_spec(tm, c)] * 2,
        out_specs=[_row_spec(tm, c), _par_spec((1, c))],
        out_shape=[jax.ShapeDtypeStruct((t, c), F32), jax.ShapeDtypeStruct((1, c), F32)],
        compiler_params=_params("arbitrary"), name=name)(y, tgt)


def _swiglu_fwd(h, *, name):
    t, two_f = h.shape
    fh = two_f // 2
    tm = _pick(t, (256, 128))

    def body(g_ref, u_ref, a_ref):
        g = g_ref[...].astype(F32)
        a_ref[...] = (g * _sigmoid(g) * u_ref[...].astype(F32)).astype(MXU_DTYPE)

    return pl.pallas_call(
        body, grid=(t // tm,), in_specs=[_row_spec(tm, fh, 0), _row_spec(tm, fh, 1)], out_specs=_row_spec(tm, fh),
        out_shape=jax.ShapeDtypeStruct((t, fh), MXU_DTYPE), compiler_params=_params("parallel"), name=name)(h, h)


def _swiglu_bwd(h, da, *, name):
    t, two_f = h.shape
    fh = two_f // 2
    tm = _pick(t, (256, 128))

    def body(g_ref, u_ref, da_ref, dh_ref):
        g = g_ref[...].astype(F32)
        s = _sigmoid(g)
        dav = da_ref[...].astype(F32)
        dh_ref[:, :fh] = (dav * u_ref[...].astype(F32) * (s * (1.0 + g * (1.0 - s)))).astype(MXU_DTYPE)
        dh_ref[:, fh:] = (dav * g * s).astype(MXU_DTYPE)

    return pl.pallas_call(
        body, grid=(t // tm,), in_specs=[_row_spec(tm, fh, 0), _row_spec(tm, fh, 1), _row_spec(tm, fh)],
        out_specs=_row_spec(tm, two_f), out_shape=jax.ShapeDtypeStruct((t, two_f), MXU_DTYPE),
        compiler_params=_params("parallel"), name=name)(h, h, da)


def _attn_probs(q, k):
    s = _nt(q, k) * (X_HEADDIM ** -0.5)
    s = s - jnp.max(s, axis=-1, keepdims=True)
    p = jnp.exp(s)
    return p / jnp.sum(p, axis=-1, keepdims=True)


def _attn_fwd(q, kv, *, bsz, name):
    t = q.shape[0]
    s = t // bsz
    ml = kv.shape[0] // bsz
    hd = X_HEADDIM

    def body(q_ref, k_ref, v_ref, o_ref):
        p = _attn_probs(q_ref[...], k_ref[...])
        o_ref[...] = _nn(p.astype(MXU_DTYPE), v_ref[...]).astype(MXU_DTYPE)

    return pl.pallas_call(
        body, grid=(bsz, X_HEADS),
        in_specs=[pl.BlockSpec((s, hd), lambda b, h: (b, h)), pl.BlockSpec((ml, hd), lambda b, h: (b, h)),
                  pl.BlockSpec((ml, hd), lambda b, h: (b, X_HEADS + h))],
        out_specs=pl.BlockSpec((s, hd), lambda b, h: (b, h)),
        out_shape=jax.ShapeDtypeStruct((t, D_MODEL), MXU_DTYPE),
        compiler_params=_params("parallel", "parallel"), name=name)(q, kv, kv)


def _attn_bwd(q, kv, do, *, bsz, name):
    t = q.shape[0]
    s = t // bsz
    ml = kv.shape[0] // bsz
    hd = X_HEADDIM

    def body(q_ref, k_ref, v_ref, do_ref, dq_ref, dk_ref, dv_ref):
        qv, kk, vv, dov = q_ref[...], k_ref[...], v_ref[...], do_ref[...]
        p = _attn_probs(qv, kk)
        dp = _nt(dov, vv)
        dv_ref[...] = _tn(p.astype(MXU_DTYPE), dov).astype(MXU_DTYPE)
        ds = (p * (dp - jnp.sum(dp * p, axis=-1, keepdims=True)) * (X_HEADDIM ** -0.5)).astype(MXU_DTYPE)
        dq_ref[...] = _nn(ds, kk).astype(MXU_DTYPE)
        dk_ref[...] = _tn(ds, qv).astype(MXU_DTYPE)

    blk_q = pl.BlockSpec((s, hd), lambda b, h: (b, h))
    blk_m = pl.BlockSpec((ml, hd), lambda b, h: (b, h))
    return pl.pallas_call(
        body, grid=(bsz, X_HEADS),
        in_specs=[blk_q, blk_m, pl.BlockSpec((ml, hd), lambda b, h: (b, X_HEADS + h)), blk_q],
        out_specs=[blk_q, blk_m, blk_m],
        out_shape=[jax.ShapeDtypeStruct((t, D_MODEL), MXU_DTYPE), jax.ShapeDtypeStruct((bsz * ml, D_MODEL), MXU_DTYPE),
                   jax.ShapeDtypeStruct((bsz * ml, D_MODEL), MXU_DTYPE)],
        compiler_params=_params("parallel", "parallel"), name=name)(q, kv, kv, do)


def _causal(n):
    row = lax.broadcasted_iota(jnp.int32, (n, n), 0)
    col = lax.broadcasted_iota(jnp.int32, (n, n), 1)
    return row >= col


def _sg_norm(v, g, b):
    gv = _gelu(v)
    mu = jnp.mean(gv, axis=-1, keepdims=True)
    xc = gv - mu
    var = jnp.mean(xc * xc, axis=-1, keepdims=True)
    rstd = lax.rsqrt(var + LN_EPS)
    xh = xc * rstd
    return xh, rstd, xh * g + b


def _sg_fwd(proj, ln_g, ln_b, w, bcol, *, name):
    t = proj.shape[0]
    c = D_MODEL
    gd = c // SG_GROUPS

    def body(u_ref, v_ref, g_ref, b_ref, w_ref, bc_ref, o_ref):
        gu = _gelu(u_ref[...].astype(F32))
        _, _, vn = _sg_norm(v_ref[...].astype(F32), g_ref[...], b_ref[...])
        mask = _causal(CHUNK)
        for g in range(SG_GROUPS):
            sl = slice(g * gd, (g + 1) * gd)
            wg = jnp.where(mask, w_ref[g], 0.0).astype(MXU_DTYPE)
            mixed = _nn(wg, vn[:, sl].astype(MXU_DTYPE)) + bc_ref[g]
            o_ref[:, sl] = (gu[:, sl] * mixed).astype(MXU_DTYPE)

    return pl.pallas_call(
        body, grid=(t // CHUNK,),
        in_specs=[_row_spec(CHUNK, c, 0), _row_spec(CHUNK, c, 1), _par_spec((1, c)), _par_spec((1, c)),
                  _par_spec((SG_GROUPS, CHUNK, CHUNK)), _par_spec((SG_GROUPS, CHUNK, 1))],
        out_specs=_row_spec(CHUNK, c), out_shape=jax.ShapeDtypeStruct((t, c), MXU_DTYPE),
        compiler_params=_params("parallel"), name=name)(proj, proj, ln_g.reshape(1, c), ln_b.reshape(1, c), w, bcol)


def _sg_bwd(proj, dsgo, ln_g, ln_b, w, bcol, dproj, *, name):
    t = proj.shape[0]
    c = D_MODEL
    gd = c // SG_GROUPS

    def body(u_ref, v_ref, d_ref, g_ref, b_ref, w_ref, bc_ref, _, duv_ref, dw_ref, dbc_ref, dg_ref, db_ref, dvn_ref):
        @pl.when(pl.program_id(0) == 0)
        def _():
            dw_ref[...] = jnp.zeros_like(dw_ref)
            dbc_ref[...] = jnp.zeros_like(dbc_ref)
            dg_ref[...] = jnp.zeros_like(dg_ref)
            db_ref[...] = jnp.zeros_like(db_ref)

        u = u_ref[...].astype(F32)
        v = v_ref[...].astype(F32)
        dso = d_ref[...].astype(F32)
        gu = _gelu(u)
        xh, rstd, vn = _sg_norm(v, g_ref[...], b_ref[...])
        mask = _causal(CHUNK)
        for g in range(SG_GROUPS):
            sl = slice(g * gd, (g + 1) * gd)
            wg = jnp.where(mask, w_ref[g], 0.0).astype(MXU_DTYPE)
            vng = vn[:, sl].astype(MXU_DTYPE)
            mixed = _nn(wg, vng) + bc_ref[g]
            duv_ref[:, sl] = (dso[:, sl] * mixed * _gelu_grad(u[:, sl])).astype(MXU_DTYPE)
            dmix = dso[:, sl] * gu[:, sl]
            dmb = dmix.astype(MXU_DTYPE)
            dbc_ref[g] += jnp.sum(dmix, axis=-1, keepdims=True)
            dw_ref[g] += jnp.where(mask, _nt(dmb, vng), 0.0)
            dvn_ref[:, sl] = _tn(wg, dmb)
        dvn = dvn_ref[...]
        dg_ref[...] += jnp.sum(dvn * xh, axis=0, keepdims=True)
        db_ref[...] += jnp.sum(dvn, axis=0, keepdims=True)
        dxh = dvn * g_ref[...]
        m1 = jnp.mean(dxh, axis=-1, keepdims=True)
        m2 = jnp.mean(dxh * xh, axis=-1, keepdims=True)
        dgv = rstd * (dxh - m1 - xh * m2)
        duv_ref[:, c:] = (dgv * _gelu_grad(v)).astype(MXU_DTYPE)

    return pl.pallas_call(
        body, grid=(t // CHUNK,),
        in_specs=[_row_spec(CHUNK, c, 0), _row_spec(CHUNK, c, 1), _row_spec(CHUNK, c), _par_spec((1, c)),
                  _par_spec((1, c)), _par_spec((SG_GROUPS, CHUNK, CHUNK)), _par_spec((SG_GROUPS, CHUNK, 1)), _ANY],
        out_specs=[_row_spec(CHUNK, 2 * c), _par_spec((SG_GROUPS, CHUNK, CHUNK)), _par_spec((SG_GROUPS, CHUNK, 1)),
                   _par_spec((1, c)), _par_spec((1, c))],
        out_shape=[jax.ShapeDtypeStruct(dproj.shape, dproj.dtype), jax.ShapeDtypeStruct((SG_GROUPS, CHUNK, CHUNK), F32),
                   jax.ShapeDtypeStruct((SG_GROUPS, CHUNK, 1), F32), jax.ShapeDtypeStruct((1, c), F32),
                   jax.ShapeDtypeStruct((1, c), F32)],
        scratch_shapes=[pltpu.VMEM((CHUNK, c), F32)], input_output_aliases={7: 0},
        compiler_params=_params("arbitrary"), name=name)(proj, proj, dsgo, ln_g.reshape(1, c), ln_b.reshape(1, c), w, bcol, dproj)


CONV_TC = 512


def _conv_taps(x):
    rows = lax.broadcasted_iota(jnp.int32, x.shape, 0)
    taps = [jnp.where(rows >= SSM_CONV - 1 - k, pltpu.roll(x, SSM_CONV - 1 - k, axis=0), 0.0) for k in range(SSM_CONV - 1)]
    return taps + [x]


def _conv_pre(taps, w_ref, b_ref):
    acc = b_ref[...]
    for k in range(SSM_CONV):
        acc = acc + taps[k] * w_ref[k:k + 1, :]
    return acc


def _conv_fwd(proj, w, b, *, bsz, name):
    t = proj.shape[0]
    s = t // bsz
    nj = SSM_CONV_DIM // CONV_TC
    c0 = XBC_COL0 // CONV_TC

    def body(x_ref, w_ref, b_ref, o_ref):
        pre = _conv_pre(_conv_taps(x_ref[...].astype(F32)), w_ref, b_ref)
        o_ref[...] = (pre * _sigmoid(pre)).astype(o_ref.dtype)

    return pl.pallas_call(
        body, grid=(bsz, nj),
        in_specs=[pl.BlockSpec((s, CONV_TC), lambda bb, j: (bb, c0 + j)), pl.BlockSpec((SSM_CONV, CONV_TC), lambda bb, j: (0, j)),
                  pl.BlockSpec((1, CONV_TC), lambda bb, j: (0, j))],
        out_specs=pl.BlockSpec((s, CONV_TC), lambda bb, j: (bb, j)),
        out_shape=jax.ShapeDtypeStruct((t, SSM_CONV_DIM), STASH_DTYPE),
        compiler_params=_params("parallel", "parallel"), name=name)(proj, w, b.reshape(1, -1))


def _conv_bwd(proj, dact, w, b, dproj, *, bsz, name):
    t = proj.shape[0]
    s = t // bsz
    nj = SSM_CONV_DIM // CONV_TC
    c0 = XBC_COL0 // CONV_TC

    def body(x_ref, d_ref, w_ref, b_ref, _, dx_ref, dw_ref, db_ref):
        @pl.when(pl.program_id(1) == 0)
        def _():
            dw_ref[...] = jnp.zeros_like(dw_ref)
            db_ref[...] = jnp.zeros_like(db_ref)

        taps = _conv_taps(x_ref[...].astype(F32))
        pre = _conv_pre(taps, w_ref, b_ref)
        sg = _sigmoid(pre)
        dpre = d_ref[...].astype(F32) * (sg * (1.0 + pre * (1.0 - sg)))
        rows = lax.broadcasted_iota(jnp.int32, dpre.shape, 0)
        db_ref[...] += jnp.sum(dpre, axis=0, keepdims=True)
        dx = dpre * w_ref[SSM_CONV - 1:SSM_CONV, :]
        for k in range(SSM_CONV):
            dw_ref[k:k + 1, :] += jnp.sum(dpre * taps[k], axis=0, keepdims=True)
        for k in range(SSM_CONV - 1):
            sh = SSM_CONV - 1 - k
            dsh = jnp.where(rows < s - sh, pltpu.roll(dpre, s - sh, axis=0), 0.0)
            dx = dx + dsh * w_ref[k:k + 1, :]
        dx_ref[...] = dx.astype(MXU_DTYPE)

    return pl.pallas_call(
        body, grid=(nj, bsz),
        in_specs=[pl.BlockSpec((s, CONV_TC), lambda j, bb: (bb, c0 + j)), pl.BlockSpec((s, CONV_TC), lambda j, bb: (bb, j)),
                  pl.BlockSpec((SSM_CONV, CONV_TC), lambda j, bb: (0, j)), pl.BlockSpec((1, CONV_TC), lambda j, bb: (0, j)), _ANY],
        out_specs=[pl.BlockSpec((s, CONV_TC), lambda j, bb: (bb, c0 + j)), pl.BlockSpec((SSM_CONV, CONV_TC), lambda j, bb: (0, j)),
                   pl.BlockSpec((1, CONV_TC), lambda j, bb: (0, j))],
        out_shape=[jax.ShapeDtypeStruct(dproj.shape, dproj.dtype), jax.ShapeDtypeStruct((SSM_CONV, SSM_CONV_DIM), F32),
                   jax.ShapeDtypeStruct((1, SSM_CONV_DIM), F32)],
        input_output_aliases={4: 0},
        compiler_params=_params("parallel", "arbitrary"), name=name)(proj, dact, w, b.reshape(1, -1), dproj)


def _softplus(x):
    return jnp.maximum(x, 0.0) + jnp.log1p(jnp.exp(-jnp.abs(x)))


def _pad_heads(v):
    return jnp.broadcast_to(jnp.pad(v.astype(F32), (0, HEAD_PAD - SSM_HEADS))[None, :], (SUBLANE, HEAD_PAD))


def _ssd_prep(dt_raw, dt_bias8, a_log8, *, name):
    t = dt_raw.shape[0]
    n = CHUNK

    def body(r_ref, b_ref, al_ref, dt_ref, cs_ref, dtt_ref, cst_ref):
        dt = _softplus(r_ref[...] + b_ref[0:1, :])
        da = dt * (-jnp.exp(al_ref[0:1, :]))
        row = lax.broadcasted_iota(jnp.int32, (n, n), 0)
        col = lax.broadcasted_iota(jnp.int32, (n, n), 1)
        lower = (col <= row).astype(F32)
        upper = (row <= col).astype(F32)
        eye = (row == col).astype(F32)
        dt_ref[...] = dt
        cs_ref[...] = _dot_exact(lower, da, _DN_NN, 1)
        cst_ref[0] = _dot_exact(da, upper, _DN_TN, 0)
        dtt_ref[0] = _dot_exact(dt, eye, _DN_TN, 0)

    hp = HEAD_PAD
    return pl.pallas_call(
        body, grid=(t // n,),
        in_specs=[_row_spec(n, hp), _par_spec((SUBLANE, hp)), _par_spec((SUBLANE, hp))],
        out_specs=[_row_spec(n, hp), _row_spec(n, hp), pl.BlockSpec((1, hp, n), lambda i: (i, 0, 0)),
                   pl.BlockSpec((1, hp, n), lambda i: (i, 0, 0))],
        out_shape=[jax.ShapeDtypeStruct((t, hp), F32), jax.ShapeDtypeStruct((t, hp), F32),
                   jax.ShapeDtypeStruct((t // n, hp, n), F32), jax.ShapeDtypeStruct((t // n, hp, n), F32)],
        compiler_params=_params("parallel"), name=name)(dt_raw, dt_bias8, a_log8)


def _expand_mat():
    h = lax.broadcasted_iota(jnp.int32, (HEAD_PAD, SSM_INNER), 0)
    ch = lax.broadcasted_iota(jnp.int32, (HEAD_PAD, SSM_INNER), 1)
    return (ch // SSM_HEADDIM == h).astype(F32)


def _reduce_mat():
    ch = lax.broadcasted_iota(jnp.int32, (SSM_INNER, HEAD_PAD), 0)
    h = lax.broadcasted_iota(jnp.int32, (SSM_INNER, HEAD_PAD), 1)
    return (ch // SSM_HEADDIM == h).astype(F32)


def _expand(v, em):
    return _dot_exact(v, em, _DN_NN, 0)


def _expand_heads(v):
    return jnp.repeat(v.astype(F32), SSM_HEADDIM)[None, :]


def _decay_mat(cs_ref, cst_ref, h, mask):
    seg = cs_ref[:, h:h + 1] - cst_ref[0, h:h + 1, :]
    return jnp.where(mask, jnp.exp(jnp.minimum(seg, 0.0)), 0.0)


GROUP_CH = SSM_INNER // SSM_GROUPS
PAIRS_PER_GROUP = GROUP_CH // LANE
HEADS_PER_GROUP = SSM_HEADS // SSM_GROUPS
BM_COL0 = SSM_INNER
CM_COL0 = SSM_INNER + SSM_GROUPS * SSM_STATE


def _ssd_specs(nc, rev):
    def cidx(i):
        return (i // nc) * nc + (nc - 1 - i % nc) if rev else i

    n = CHUNK
    xs = pl.BlockSpec((n, SSM_INNER), lambda i: (cidx(i), 0))
    bm = pl.BlockSpec((n, GROUP_CH), lambda i: (cidx(i), BM_COL0 // GROUP_CH))
    cm = pl.BlockSpec((n, GROUP_CH), lambda i: (cidx(i), CM_COL0 // GROUP_CH))
    hv = pl.BlockSpec((n, HEAD_PAD), lambda i: (cidx(i), 0))
    hvt = pl.BlockSpec((1, HEAD_PAD, n), lambda i: (cidx(i), 0, 0))
    st = pl.BlockSpec((1, SSM_INNER, SSM_STATE), lambda i: (cidx(i), 0, 0))
    return xs, bm, cm, hv, hvt, st


def _ssd_fwd(xbc, dt, cs, dtt, cst, dskx, *, nc, name):
    t = xbc.shape[0]
    n = CHUNK
    xs_s, bm_s, cm_s, hv_s, hvt_s, st_s = _ssd_specs(nc, False)

    def body(xs_ref, bm_ref, cm_ref, dt_ref, cs_ref, dtt_ref, cst_ref, dsk_ref, y_ref, st_ref, prev):
        @pl.when(pl.program_id(0) % nc == 0)
        def _():
            prev[...] = jnp.zeros_like(prev)

        st_ref[0] = prev[...]
        em = _expand_mat()
        dtx = _expand(dt_ref[...], em)
        csx = _expand(cs_ref[...], em)
        dskx = dsk_ref[...]
        xs = xs_ref[...].astype(F32)
        xdt = xs * dtx
        ecs = jnp.exp(csx)
        dec = jnp.exp(csx[n - 1:n, :] - csx)
        mask = _causal(n)
        lane = lax.broadcasted_iota(jnp.int32, (n, LANE), 1)
        for g in range(SSM_GROUPS):
            gs = slice(g * SSM_STATE, (g + 1) * SSM_STATE)
            gc = slice(g * GROUP_CH, (g + 1) * GROUP_CH)
            cmat = cm_ref[:, gs].astype(MXU_DTYPE)
            bmat = bm_ref[:, gs].astype(MXU_DTYPE)
            cb = _nt(cmat, bmat)
            yoff = ecs[:, gc] * _nt(cmat, prev[gc, :].astype(MXU_DTYPE))
            for q in range(PAIRS_PER_GROUP):
                hp = g * PAIRS_PER_GROUP + q
                sl = slice(hp * LANE, (hp + 1) * LANE)
                xp = xdt[:, sl].astype(MXU_DTYPE)
                m0 = (cb * _decay_mat(cs_ref, cst_ref, 2 * hp, mask)).astype(MXU_DTYPE)
                m1 = (cb * _decay_mat(cs_ref, cst_ref, 2 * hp + 1, mask)).astype(MXU_DTYPE)
                yd = jnp.where(lane < SSM_HEADDIM, _nn(m0, xp), _nn(m1, xp))
                y_ref[:, sl] = (yd + yoff[:, q * LANE:(q + 1) * LANE] + xs[:, sl] * dskx[:, sl]).astype(y_ref.dtype)
            snew = _tn((xdt[:, gc] * dec[:, gc]).astype(MXU_DTYPE), bmat)
            for r in range(HEADS_PER_GROUP):
                h = g * HEADS_PER_GROUP + r
                rows = slice(h * SSM_HEADDIM, (h + 1) * SSM_HEADDIM)
                e = jnp.exp(cst_ref[0, h:h + 1, n - 1:n])
                prev[rows, :] = prev[rows, :] * e + snew[r * SSM_HEADDIM:(r + 1) * SSM_HEADDIM, :]

    return pl.pallas_call(
        body, grid=(t // n,),
        in_specs=[xs_s, bm_s, cm_s, hv_s, hv_s, hvt_s, hvt_s, _par_spec((1, SSM_INNER))],
        out_specs=[xs_s, st_s],
        out_shape=[jax.ShapeDtypeStruct((t, SSM_INNER), STASH_DTYPE), jax.ShapeDtypeStruct((t // n, SSM_INNER, SSM_STATE), F32)],
        scratch_shapes=[pltpu.VMEM((SSM_INNER, SSM_STATE), F32)],
        compiler_params=_params("arbitrary"), name=name)(xbc, xbc, xbc, dt, cs, dtt, cst, dskx)


def _ssd_bwd(dy, xbc, dt, cs, dtt, cst, st, dskx, a_log8, dt_raw, dt_bias8, *, nc, name):
    t = xbc.shape[0]
    n = CHUNK
    xs_s, bm_s, cm_s, hv_s, hvt_s, st_s = _ssd_specs(nc, True)
    acc_s = _par_spec((1, HEAD_PAD))
    xbc_s = pl.BlockSpec((n, SSM_CONV_DIM), xs_s.index_map)

    def body(dy_ref, xs_ref, bm_ref, cm_ref, dt_ref, cs_ref, dtt_ref, cst_ref, st_ref, dsk_ref, al_ref, raw_ref, bias_ref,
             dxbc_ref, ddr_ref, dal_ref, dds_ref, dbias_ref, dprev, dxdt_s, tdec_s, tcs_s):
        @pl.when(pl.program_id(0) % nc == 0)
        def _():
            dprev[...] = jnp.zeros_like(dprev)

        @pl.when(pl.program_id(0) == 0)
        def _():
            dal_ref[...] = jnp.zeros_like(dal_ref)
            dds_ref[...] = jnp.zeros_like(dds_ref)
            dbias_ref[...] = jnp.zeros_like(dbias_ref)

        em = _expand_mat()
        rm = _reduce_mat()

        def head_reduce(v):
            return _dot_exact(v, rm, _DN_NN, 0)

        dtv = dt_ref[...]
        csv = cs_ref[...]
        dtx = _expand(dtv, em)
        csx = _expand(csv, em)
        dskx = dsk_ref[...]
        xs = xs_ref[...].astype(F32)
        dyv = dy_ref[...].astype(F32)
        xdt = xs * dtx
        ecs = jnp.exp(csx)
        dec = jnp.exp(csx[n - 1:n, :] - csx)
        mask = _causal(n)
        lane = lax.broadcasted_iota(jnp.int32, (n, LANE), 1)
        hlane = lax.broadcasted_iota(jnp.int32, (1, HEAD_PAD), 1)
        hsub = lax.broadcasted_iota(jnp.int32, (HEAD_PAD, 1), 0)
        rsum = jnp.zeros((n, HEAD_PAD), F32)
        csum = jnp.zeros((HEAD_PAD, n), F32)
        for g in range(SSM_GROUPS):
            gs = slice(g * SSM_STATE, (g + 1) * SSM_STATE)
            gc = slice(g * GROUP_CH, (g + 1) * GROUP_CH)
            cmat = cm_ref[:, gs].astype(MXU_DTYPE)
            bmat = bm_ref[:, gs].astype(MXU_DTYPE)
            cb = _nt(cmat, bmat)
            pg = st_ref[0, gc, :].astype(MXU_DTYPE)
            dpg = dprev[gc, :]
            dpgb = dpg.astype(MXU_DTYPE)
            z = _nt(cmat, pg)
            dyg = dyv[:, gc]
            dz = (dyg * ecs[:, gc]).astype(MXU_DTYPE)
            dc = _nn(dz, pg)
            dprev_y = _tn(dz, cmat)
            tcs_s[:, gc] = dyg * z * ecs[:, gc]
            xd = xdt[:, gc] * dec[:, gc]
            wmat = _nt(bmat, dpgb)
            db = _nn(xd.astype(MXU_DTYPE), dpgb)
            tdec_s[:, gc] = wmat * xd
            dxdt_g = wmat * dec[:, gc]
            dcb = jnp.zeros((n, n), F32)
            for q in range(PAIRS_PER_GROUP):
                hp = g * PAIRS_PER_GROUP + q
                sl = slice(hp * LANE, (hp + 1) * LANE)
                xp = xdt[:, sl].astype(MXU_DTYPE)
                dyp = dyv[:, sl]
                dypb = dyp.astype(MXU_DTYPE)
                dxp = None
                for hh in range(2):
                    h = 2 * hp + hh
                    lm = _decay_mat(cs_ref, cst_ref, h, mask)
                    mine = (lane < SSM_HEADDIM) if hh == 0 else (lane >= SSM_HEADDIM)
                    dm = _nt(jnp.where(mine, dyp, 0.0).astype(MXU_DTYPE), xp)
                    dml = dm * lm
                    dcb = dcb + dml
                    gseg = dml * cb
                    rsum = rsum + jnp.sum(gseg, axis=1, keepdims=True) * (hlane == h).astype(F32)
                    csum = csum + (hsub == h).astype(F32) * jnp.sum(gseg, axis=0, keepdims=True)
                    dxh = _tn((cb * lm).astype(MXU_DTYPE), dypb)
                    dxp = dxh if dxp is None else jnp.where(mine, dxh, dxp)
                dxdt_s[:, sl] = dxdt_g[:, q * LANE:(q + 1) * LANE] + dxp
            dcbb = dcb.astype(MXU_DTYPE)
            dxbc_ref[:, CM_COL0 + g * SSM_STATE:CM_COL0 + (g + 1) * SSM_STATE] = (dc + _nn(dcbb, bmat)).astype(dxbc_ref.dtype)
            dxbc_ref[:, BM_COL0 + g * SSM_STATE:BM_COL0 + (g + 1) * SSM_STATE] = (db + _tn(dcbb, cmat)).astype(dxbc_ref.dtype)
            for r in range(HEADS_PER_GROUP):
                h = g * HEADS_PER_GROUP + r
                rows = slice(h * SSM_HEADDIM, (h + 1) * SSM_HEADDIM)
                lr = slice(r * SSM_HEADDIM, (r + 1) * SSM_HEADDIM)
                e = jnp.exp(cst_ref[0, h:h + 1, n - 1:n])
                dprev[rows, :] = dpg[lr, :] * e + dprev_y[lr, :]
            tq = _dot_exact(dpg * st_ref[0, gc, :], rm[gc, :], _DN_TN, 0)
            if g == 0:
                qsum = jnp.sum(tq, axis=0, keepdims=True)
            else:
                qsum = qsum + jnp.sum(tq, axis=0, keepdims=True)
        dxdt = dxdt_s[...]
        dxbc_ref[:, 0:SSM_INNER] = (dxdt * dtx + dyv * dskx).astype(dxbc_ref.dtype)
        ddt = head_reduce(dxdt * xs)
        edec = head_reduce(tdec_s[...])
        ycs = head_reduce(tcs_s[...])
        row = lax.broadcasted_iota(jnp.int32, (n, HEAD_PAD), 0)
        extra = jnp.sum(edec, axis=0, keepdims=True) + qsum * jnp.exp(csv[n - 1:n, :])
        dcs = rsum - csum.T + ycs - edec + jnp.where(row == n - 1, extra, 0.0)
        r2 = lax.broadcasted_iota(jnp.int32, (n, n), 0)
        c2 = lax.broadcasted_iota(jnp.int32, (n, n), 1)
        dda = _dot_exact((c2 >= r2).astype(F32), dcs, _DN_NN, 1)
        a_row = -jnp.exp(al_ref[0:1, :])
        ddt = ddt + dda * a_row
        dal_ref[...] += jnp.sum(dda * dtv, axis=0, keepdims=True) * a_row
        dds_ref[...] += jnp.sum(head_reduce(dyv * xs), axis=0, keepdims=True)
        ddr = ddt * _sigmoid(raw_ref[...] + bias_ref[0:1, :])
        ddr_ref[...] = ddr
        dbias_ref[...] += jnp.sum(ddr, axis=0, keepdims=True)

    par8 = _par_spec((SUBLANE, HEAD_PAD))
    return pl.pallas_call(
        body, grid=(t // n,),
        in_specs=[xs_s, xs_s, bm_s, cm_s, hv_s, hv_s, hvt_s, hvt_s, st_s, _par_spec((1, SSM_INNER)), par8, hv_s, par8],
        out_specs=[xbc_s, hv_s, acc_s, acc_s, acc_s],
        out_shape=[jax.ShapeDtypeStruct((t, SSM_CONV_DIM), STASH_DTYPE), jax.ShapeDtypeStruct((t, HEAD_PAD), F32),
                   jax.ShapeDtypeStruct((1, HEAD_PAD), F32), jax.ShapeDtypeStruct((1, HEAD_PAD), F32),
                   jax.ShapeDtypeStruct((1, HEAD_PAD), F32)],
        scratch_shapes=[pltpu.VMEM((SSM_INNER, SSM_STATE), F32), pltpu.VMEM((n, SSM_INNER), F32),
                        pltpu.VMEM((n, SSM_INNER), F32), pltpu.VMEM((n, SSM_INNER), F32)],
        compiler_params=_params("arbitrary"), name=name)(dy, xbc, xbc, xbc, dt, cs, dtt, cst, st, dskx, a_log8, dt_raw, dt_bias8)


def _gate_norm_fwd(y, proj, norm_g, *, name):
    t, c = y.shape
    tm = _pick(t, (256, 128))

    def body(y_ref, z_ref, g_ref, o_ref):
        z = z_ref[...].astype(F32)
        yz = y_ref[...].astype(F32) * z * _sigmoid(z)
        for g in range(SSM_GROUPS):
            gc = slice(g * GROUP_CH, (g + 1) * GROUP_CH)
            seg = yz[:, gc]
            r = lax.rsqrt(jnp.mean(seg * seg, axis=-1, keepdims=True) + RMS_EPS)
            o_ref[:, gc] = (seg * r * g_ref[:, gc]).astype(MXU_DTYPE)

    return pl.pallas_call(
        body, grid=(t // tm,), in_specs=[_row_spec(tm, c), _row_spec(tm, c, 1), _par_spec((1, c))],
        out_specs=_row_spec(tm, c), out_shape=jax.ShapeDtypeStruct((t, c), MXU_DTYPE),
        compiler_params=_params("parallel"), name=name)(y, proj, norm_g.reshape(1, c))


def _gate_norm_bwd(dyb, y, proj, norm_g, dproj, *, name):
    t, c = y.shape
    tm = _pick(t, (256, 128))

    def body(d_ref, y_ref, z_ref, g_ref, _, dy_ref, dz_ref, dg_ref):
        @pl.when(pl.program_id(0) == 0)
        def _():
            dg_ref[...] = jnp.zeros_like(dg_ref)

        z = z_ref[...].astype(F32)
        yv = y_ref[...].astype(F32)
        sz = _sigmoid(z)
        silu = z * sz
        yz = yv * silu
        dv = d_ref[...].astype(F32)
        for g in range(SSM_GROUPS):
            gc = slice(g * GROUP_CH, (g + 1) * GROUP_CH)
            seg = yz[:, gc]
            r = lax.rsqrt(jnp.mean(seg * seg, axis=-1, keepdims=True) + RMS_EPS)
            nrm = seg * r
            dn = dv[:, gc] * g_ref[:, gc]
            dg_ref[:, gc] += jnp.sum(dv[:, gc] * nrm, axis=0, keepdims=True)
            dyz = r * (dn - nrm * jnp.mean(dn * nrm, axis=-1, keepdims=True))
            dy_ref[:, gc] = (dyz * silu[:, gc]).astype(dy_ref.dtype)
            dz_ref[:, gc] = (dyz * yv[:, gc] * (sz[:, gc] * (1.0 + z[:, gc] * (1.0 - sz[:, gc])))).astype(MXU_DTYPE)

    return pl.pallas_call(
        body, grid=(t // tm,), in_specs=[_row_spec(tm, c), _row_spec(tm, c), _row_spec(tm, c, 1), _par_spec((1, c)), _ANY],
        out_specs=[_row_spec(tm, c), _row_spec(tm, c, 1), _par_spec((1, c))],
        out_shape=[jax.ShapeDtypeStruct((t, c), STASH_DTYPE), jax.ShapeDtypeStruct(dproj.shape, dproj.dtype),
                   jax.ShapeDtypeStruct((1, c), F32)],
        input_output_aliases={4: 1},
        compiler_params=_params("arbitrary"), name=name)(dyb, y, proj, norm_g.reshape(1, c), dproj)


GA_COLBLK = GAB_COL0 // D_MODEL


def _merge_fwd(br_a, br_b, proj, *, name):
    t, c = br_a.shape
    tm = _pick(t, ROW_TILES)

    def body(a_ref, b_ref, ga_ref, gb_ref, o_ref):
        o_ref[...] = (_sigmoid(ga_ref[...].astype(F32)) * a_ref[...].astype(F32)
                      + _sigmoid(gb_ref[...].astype(F32)) * b_ref[...].astype(F32)).astype(MXU_DTYPE)

    return pl.pallas_call(
        body, grid=(t // tm,),
        in_specs=[_row_spec(tm, c), _row_spec(tm, c), _row_spec(tm, c, GA_COLBLK), _row_spec(tm, c, GA_COLBLK + 1)],
        out_specs=_row_spec(tm, c), out_shape=jax.ShapeDtypeStruct((t, c), MXU_DTYPE),
        compiler_params=_params("parallel"), name=name)(br_a, br_b, proj, proj)


def _merge_bwd(dm, br_a, br_b, proj, *, name):
    t, c = br_a.shape
    tm = _pick(t, ROW_TILES)

    def body(dm_ref, a_ref, b_ref, ga_ref, gb_ref, da_ref, db_ref, dg_ref):
        d = dm_ref[...].astype(F32)
        sa = _sigmoid(ga_ref[...].astype(F32))
        sb = _sigmoid(gb_ref[...].astype(F32))
        da_ref[...] = (d * sa).astype(MXU_DTYPE)
        db_ref[...] = (d * sb).astype(MXU_DTYPE)
        dg_ref[:, :c] = (d * a_ref[...].astype(F32) * sa * (1.0 - sa)).astype(MXU_DTYPE)
        dg_ref[:, c:] = (d * b_ref[...].astype(F32) * sb * (1.0 - sb)).astype(MXU_DTYPE)

    return pl.pallas_call(
        body, grid=(t // tm,),
        in_specs=[_row_spec(tm, c), _row_spec(tm, c), _row_spec(tm, c), _row_spec(tm, c, GA_COLBLK), _row_spec(tm, c, GA_COLBLK + 1)],
        out_specs=[_row_spec(tm, c), _row_spec(tm, c), _row_spec(tm, 2 * c, GAB_COL0 // (2 * c))],
        out_shape=[jax.ShapeDtypeStruct((t, c), MXU_DTYPE), jax.ShapeDtypeStruct((t, c), MXU_DTYPE),
                   jax.ShapeDtypeStruct((t, MAIN_COLS), MXU_DTYPE)],
        compiler_params=_params("parallel"), name=name)(dm, br_a, br_b, proj, proj)


def _layer_fwd(x, xb, memn_b, w, *, bsz, tag):
    nc = x.shape[0] // bsz // CHUNK
    sv = {"x_in": xb}
    proj = _mm(xb, w["w_main"], out_dtype=STASH_DTYPE, name=f"{tag}_proj")
    dt_raw = _mm(xb, w["w_dt"], name=f"{tag}_dtproj")
    sgo = _sg_fwd(proj, w["sg_ln_g"], w["sg_ln_b"], w["sg_w"], w["sg_bcol"], name=f"{tag}_sg_fwd")
    xbc = _conv_fwd(proj, w["conv_w"], w["conv_b"], bsz=bsz, name=f"{tag}_conv_fwd")
    dt, cs, dtt, cst = _ssd_prep(dt_raw, w["dt_bias8"], w["a_log8"], name=f"{tag}_ssd_prep")
    y, st = _ssd_fwd(xbc, dt, cs, dtt, cst, w["d_skipx"], nc=nc, name=f"{tag}_ssd_fwd")
    yb = _gate_norm_fwd(y, proj, w["ssm_norm_g"], name=f"{tag}_gate_norm_fwd")
    if "rest" in w:
        w = w["rest"](w, yb)
    br_a = _mm(sgo, w["p_a"], out_dtype=STASH_DTYPE, name=f"{tag}_br_a")
    br_b = _mm(yb, w["p_b"], out_dtype=STASH_DTYPE, name=f"{tag}_br_b")
    merged = _merge_fwd(br_a, br_b, proj, name=f"{tag}_merge_fwd")
    mix = _mm(merged, w["w_mix_o"], name=f"{tag}_mix_o")
    x1, x1b, xh1, rs1 = _ln_fwd(x, mix, w["ln_g"][0], w["ln_b"][0], name=f"{tag}_ln1_fwd")
    sv.update(proj=proj, dt_raw=dt_raw, sgo=sgo, xbc=xbc, dt=dt, cs=cs, dtt=dtt, cst=cst, y=y, st=st, yb=yb,
              br_a=br_a, br_b=br_b, merged=merged, xh1=xh1, rs1=rs1, x1b=x1b)
    q = _mm(x1b, w["w_xq"], out_dtype=MXU_DTYPE, name=f"{tag}_q")
    kv = _mm(memn_b, w["w_xkv"], out_dtype=MXU_DTYPE, name=f"{tag}_kv")
    o = _attn_fwd(q, kv, bsz=bsz, name=f"{tag}_attn_fwd")
    att = _mm(o, w["w_xo"], name=f"{tag}_xo")
    x2, x2b, xh2, rs2 = _ln_fwd(x1, att, w["ln_g"][1], w["ln_b"][1], name=f"{tag}_ln2_fwd")
    sv.update(q=q, kv=kv, o=o, xh2=xh2, rs2=rs2, x2b=x2b)
    h = _mm(x2b, w["w_ffn_in"], out_dtype=STASH_DTYPE, name=f"{tag}_ffn_in")
    a = _swiglu_fwd(h, name=f"{tag}_swiglu_fwd")
    ffn = _mm(a, w["w_ffn_out"], name=f"{tag}_ffn_out")
    x3, x3b, xh3, rs3 = _ln_fwd(x2, ffn, w["ln_g"][2], w["ln_b"][2], name=f"{tag}_ln3_fwd")
    sv.update(h=h, a=a, xh3=xh3, rs3=rs3)
    return x3, x3b, sv, w


GRAD_GROUPS = (("w_ffn_out", "w_ffn_in", "w_xo", "w_xq", "w_xkv"), ("w_mix_o", "p_a", "p_b"), ("w_in",))


def _layer_bwd(dx3_addends, dx3_scales, memn_b, w, sv, on_group=None, *, bsz, tag):
    nc = sv["xh1"].shape[0] // bsz // CHUNK
    gr = {}

    def group_done(k):
        return on_group(GRAD_GROUPS[k], gr) if on_group is not None else None
    dp3, dp3b, dg3, db3 = _ln_bwd(dx3_addends, dx3_scales, sv["xh3"], sv["rs3"], w["ln_g"][2], name=f"{tag}_ln3_bwd")
    da = _mm(dp3b, w["w_ffn_out"], tb=True, out_dtype=STASH_DTYPE, name=f"{tag}_d_a")
    gr["w_ffn_out"] = _mm(sv["a"], dp3b, ta=True, name=f"{tag}_dw_ffn_out")
    dh = _swiglu_bwd(sv["h"], da, name=f"{tag}_swiglu_bwd")
    gr["w_ffn_in"] = _mm(sv["x2b"], dh, ta=True, name=f"{tag}_dw_ffn_in")
    dx2_br = _mm(dh, w["w_ffn_in"], tb=True, name=f"{tag}_dx2")
    dp2, dp2b, dg2, db2 = _ln_bwd([dp3, dx2_br], [ALPHA, 1.0], sv["xh2"], sv["rs2"], w["ln_g"][1], name=f"{tag}_ln2_bwd")
    do = _mm(dp2b, w["w_xo"], tb=True, out_dtype=MXU_DTYPE, name=f"{tag}_d_o")
    gr["w_xo"] = _mm(sv["o"], dp2b, ta=True, name=f"{tag}_dw_xo")
    dq, dk, dv = _attn_bwd(sv["q"], sv["kv"], do, bsz=bsz, name=f"{tag}_attn_bwd")
    dkv = jnp.concatenate([dk, dv], axis=1)
    gr["w_xq"] = _mm(sv["x1b"], dq, ta=True, name=f"{tag}_dw_xq")
    gr["w_xkv"] = _mm(memn_b, dkv, ta=True, name=f"{tag}_dw_xkv")
    dmemn = _mm(dkv, w["w_xkv"], tb=True, name=f"{tag}_d_memn")
    dx1_br = _mm(dq, w["w_xq"], tb=True, name=f"{tag}_dx1")
    token = group_done(0)
    ln_g1 = w["ln_g"][0] if token is None else w["ln_g"][0] + token[0, 0]
    dp1, dp1b, dg1, db1 = _ln_bwd([dp2, dx1_br], [ALPHA, 1.0], sv["xh1"], sv["rs1"], ln_g1, name=f"{tag}_ln1_bwd")
    gr["ln_g"] = jnp.concatenate([dg1, dg2, dg3], axis=0)
    gr["ln_b"] = jnp.concatenate([db1, db2, db3], axis=0)
    dmerged = _mm(dp1b, w["w_mix_o"], tb=True, out_dtype=STASH_DTYPE, name=f"{tag}_d_merged")
    gr["w_mix_o"] = _mm(sv["merged"], dp1b, ta=True, name=f"{tag}_dw_mix_o")
    dbr_a, dbr_b, dproj = _merge_bwd(dmerged, sv["br_a"], sv["br_b"], sv["proj"], name=f"{tag}_merge_bwd")
    gr["p_a"] = _mm(sv["sgo"], dbr_a, ta=True, name=f"{tag}_dw_p_a")
    gr["p_b"] = _mm(sv["yb"], dbr_b, ta=True, name=f"{tag}_dw_p_b")
    dsgo = _mm(dbr_a, w["p_a"], tb=True, out_dtype=STASH_DTYPE, name=f"{tag}_d_sgo")
    dyb = _mm(dbr_b, w["p_b"], tb=True, out_dtype=STASH_DTYPE, name=f"{tag}_d_yb")
    token = group_done(1)
    norm_g = w["ssm_norm_g"] if token is None else w["ssm_norm_g"] + token[0, 0]
    dy, dproj, gr["ssm_norm_g"] = _gate_norm_bwd(dyb, sv["y"], sv["proj"], norm_g, dproj, name=f"{tag}_gate_norm_bwd")
    dxbc, ddr, gr["a_log"], gr["d_skip"], gr["dt_bias"] = _ssd_bwd(
        dy, sv["xbc"], sv["dt"], sv["cs"], sv["dtt"], sv["cst"], sv["st"], w["d_skipx"], w["a_log8"], sv["dt_raw"],
        w["dt_bias8"], nc=nc, name=f"{tag}_ssd_bwd")
    dproj, gr["conv_w"], gr["conv_b"] = _conv_bwd(sv["proj"], dxbc, w["conv_w"], w["conv_b"], dproj, bsz=bsz, name=f"{tag}_conv_bwd")
    dproj, gr["sg_w"], dsg_bcol, gr["sg_ln_g"], gr["sg_ln_b"] = _sg_bwd(
        sv["proj"], dsgo, w["sg_ln_g"], w["sg_ln_b"], w["sg_w"], w["sg_bcol"], dproj, name=f"{tag}_sg_bwd")
    gr["sg_b"] = dsg_bcol[..., 0]
    gr["w_main"] = _mm(sv["x_in"], dproj, ta=True, name=f"{tag}_dw_main")
    gr["w_dt"] = _mm(sv["x_in"], ddr, ta=True, name=f"{tag}_dw_dt")
    token = group_done(2)
    dx_dt = _mm(ddr, w["w_dt"], tb=True, after=token, name=f"{tag}_dx_dt")
    dx_main = _mm(dproj, w["w_main"], tb=True, after=token, name=f"{tag}_dx_main")
    return [dp1, dx_main, dx_dt], [ALPHA, 1.0, 1.0], gr, dmemn


def _local_step(x, mem, tgt, mem_ln_g, mem_ln_b, layers, on_layer_grads=None):
    bsz, s, d = x.shape
    xf = x.reshape(bsz * s, d)
    memf = mem.reshape(-1, d)
    _, memn_b, mxh, mrs = _ln_fwd(memf, None, mem_ln_g, mem_ln_b, name="mem_ln_fwd")
    cur, curb, saved, weights = xf, xf, [], []
    for li, get_weights in enumerate(layers):
        cur, curb, sv, w = _layer_fwd(cur, curb, memn_b, get_weights(cur), bsz=bsz, tag=f"l{li}")
        saved.append(sv)
        weights.append(w)
    dy, lsum = _loss_head(cur, tgt.reshape(bsz * s, d), name="loss_head")
    addends, scales = [dy], [1.0]
    grads, dmem = [None] * len(layers), []
    for li in reversed(range(len(layers))):
        on_group = None if on_layer_grads is None else functools.partial(on_layer_grads, li)
        addends, scales, grads[li], dm = _layer_bwd(addends, scales, memn_b, weights[li], saved[li], on_group, bsz=bsz, tag=f"l{li}")
        dmem.append(dm)
    grad_x = _add_scaled(addends, scales, name="grad_x").reshape(bsz, s, d)
    _, _, dmg, dmb = _ln_bwd(dmem, [1.0] * len(dmem), mxh, mrs, mem_ln_g, name="mem_ln_bwd")
    return lsum, grad_x, grads, dmg[0], dmb[0]


_ANY = pl.BlockSpec(memory_space=pl.ANY)
_MESH = pl.DeviceIdType.MESH


def _all_gather8(x, *, name):
    def body(x_ref, out_ref, send_sems, recv_sems):
        mx, my, mc = lax.axis_index("x"), lax.axis_index("y"), lax.axis_index("c")
        me, sibling = (mx, my, mc), (mx, my, 1 - mc)
        chips = [(1 - mx, my), (mx, 1 - my), (1 - mx, 1 - my)]

        def blk(px, py, pc):
            return out_ref.at[4 * px + 2 * py + pc]

        def copy(k, block, to, src=None):
            return pltpu.make_async_remote_copy(
                src_ref=blk(*block) if src is None else src, dst_ref=blk(*block), send_sem=send_sems.at[k],
                recv_sem=recv_sems.at[k], device_id=to, device_id_type=_MESH)

        first = [copy(0, me, sibling, src=x_ref)]
        first += [copy(1 + j, me, (*chip, mc), src=x_ref) for j, chip in enumerate(chips)]
        for cp in first:
            cp.start()
        passed = [copy(4 + j, (*chip, mc), sibling) for j, chip in enumerate(chips)]
        for j, chip in enumerate(chips):
            copy(1 + j, (*chip, mc), me).wait_recv()
            passed[j].start()
        copy(0, sibling, me).wait_recv()
        for j, chip in enumerate(chips):
            copy(4 + j, (*chip, 1 - mc), me).wait_recv()
        for cp in first + passed:
            cp.wait_send()

    return pl.pallas_call(
        body, out_shape=jax.ShapeDtypeStruct((N_DEV,) + x.shape, x.dtype), in_specs=[_ANY], out_specs=_ANY,
        scratch_shapes=[pltpu.SemaphoreType.DMA((7,)), pltpu.SemaphoreType.DMA((7,))], name=name)(x)


def _row_tile(rows, row_bytes, mult=SUBLANE):
    best = None
    for tr in range(mult, rows + 1, mult):
        if rows % tr == 0 and (best is None or tr * row_bytes <= BLOCK_BYTES):
            best = tr
    return rows if best is None else best


def _gather_shape(r, c, kind):
    return {"row": (2, N_CHIPS * r, c), "col": (2, r, N_CHIPS * c), "chip": (2, N_CHIPS, r, c)}[kind]


def _cast_place(shard, kind, dtype, chip_idx, *, name):
    _, r, c = shard.shape
    tr = _row_tile(r, c * 4, 16)
    nt = r // tr

    def body(_, s_ref, o_ref):
        o_ref[...] = s_ref[...].astype(dtype)

    if kind == "row":
        out_spec = pl.BlockSpec((None, tr, c), lambda l, i, j_ref: (l, j_ref[0] * nt + i, 0))
    elif kind == "col":
        out_spec = pl.BlockSpec((None, tr, c), lambda l, i, j_ref: (l, i, j_ref[0]))
    else:
        out_spec = pl.BlockSpec((None, None, tr, c), lambda l, i, j_ref: (l, j_ref[0], i, 0))
    grid_spec = pltpu.PrefetchScalarGridSpec(
        num_scalar_prefetch=1, grid=(2, nt), in_specs=[pl.BlockSpec((None, tr, c), lambda l, i, j_ref: (l, i, 0))],
        out_specs=out_spec)
    return pl.pallas_call(body, grid_spec=grid_spec, out_shape=jax.ShapeDtypeStruct(_gather_shape(r, c, kind), dtype),
                          compiler_params=_params("parallel", "parallel"), name=name)(chip_idx, shard)


def _gather_params(bufs, shard_shapes, kinds, *, name):
    n = len(bufs)

    def body(*refs):
        outs = refs[n:2 * n]
        send_sems, recv_sems = refs[2 * n:]
        mx, my, mc = lax.axis_index("x"), lax.axis_index("y"), lax.axis_index("c")
        me, sibling = (mx, my, mc), (mx, my, 1 - mc)
        chips = [(1 - mx, my), (mx, 1 - my), (1 - mx, 1 - my)]

        def blk(i, px, py, pc):
            r, c = shard_shapes[i]
            j = 2 * px + py
            if kinds[i] == "row":
                return outs[i].at[pc, pl.ds(pl.multiple_of(j * r, r), r)]
            if kinds[i] == "col":
                return outs[i].at[pc, :, pl.ds(pl.multiple_of(j * c, c), c)]
            return outs[i].at[pc, j]

        def copy(i, k, block, to):
            return pltpu.make_async_remote_copy(
                src_ref=blk(i, *block), dst_ref=blk(i, *block), send_sem=send_sems.at[6 * i + k],
                recv_sem=recv_sems.at[6 * i + k], device_id=to, device_id_type=_MESH)

        sent = []
        for i in range(n):
            for j, chip in enumerate(chips):
                cp = copy(i, j, me, (*chip, mc))
                cp.start()
                sent.append(cp)
        for j, chip in enumerate(chips):
            for i in range(n):
                copy(i, j, (*chip, mc), me).wait_recv()
                fwd = copy(i, 3 + j, (*chip, mc), sibling)
                fwd.start()
                sent.append(fwd)
        for i in range(n):
            for j, chip in enumerate(chips):
                copy(i, 3 + j, (*chip, 1 - mc), me).wait_recv()
        for cp in sent:
            cp.wait_send()

    return pl.pallas_call(
        body, out_shape=[jax.ShapeDtypeStruct(b.shape, b.dtype) for b in bufs], in_specs=[_ANY] * n, out_specs=[_ANY] * n,
        input_output_aliases={i: i for i in range(n)},
        scratch_shapes=[pltpu.SemaphoreType.DMA((6 * n,)), pltpu.SemaphoreType.DMA((6 * n,))], name=name)(*bufs)


def _half(r, h):
    return pl.ds(pl.multiple_of(h * (r // 2), r // 2), r // 2)


_HBM = pl.BlockSpec(memory_space=pltpu.HBM)
_SEM = pl.BlockSpec(memory_space=pltpu.SEMAPHORE)
_EFFECT = pltpu.SideEffectType.DATAFLOW_SIDE_EFFECTING


def _sibling_copies(g_refs, land_refs, gs, views, send_sems, recv_sems):
    mx, my, mc = lax.axis_index("x"), lax.axis_index("y"), lax.axis_index("c")
    copies = []
    for i in range(len(gs)):
        if views[i] == "chip":
            src = g_refs[i].at[:, _half(gs[i].shape[1], 1 - mc)]
        else:
            src = g_refs[i].at[_half(gs[i].shape[0], 1 - mc)]
        copies.append(pltpu.make_async_remote_copy(src_ref=src, dst_ref=land_refs[i], send_sem=send_sems.at[i], recv_sem=recv_sems.at[i],
                                                   device_id=(mx, my, 1 - mc), device_id_type=_MESH))
    return copies


def _half_shape(g, view):
    return (g.shape[0], g.shape[1] // 2, g.shape[2]) if view == "chip" else (g.shape[0] // 2, g.shape[1])


def _grads_to_sibling_start(gs, views, *, name):
    n = len(gs)
    lands = [pltpu.with_memory_space_constraint(lax.empty(_half_shape(g, v), g.dtype), pltpu.HBM) for g, v in zip(gs, views)]

    def body(*refs):
        for cp in _sibling_copies(refs[:n], refs[n:2 * n], gs, views, refs[2 * n], refs[2 * n + 1]):
            cp.start()
        refs[-1][...] = jnp.zeros_like(refs[-1])

    outs = pl.pallas_call(
        body, name=name,
        out_shape=(pltpu.SemaphoreType.DMA((n,)), pltpu.SemaphoreType.DMA((n,)),
                   *[pltpu.HBM(x.shape, x.dtype) for x in list(gs) + lands], jax.ShapeDtypeStruct((SUBLANE, LANE), F32)),
        in_specs=[_HBM] * (2 * n), out_specs=(_SEM, _SEM, *[_HBM] * (2 * n), pl.BlockSpec(memory_space=pltpu.VMEM)),
        input_output_aliases={i: 2 + i for i in range(2 * n)},
        compiler_params=pltpu.CompilerParams(has_side_effects=_EFFECT),
    )(*[pltpu.with_memory_space_constraint(g, pltpu.HBM) for g in gs], *lands)
    return outs[0], outs[1], list(outs[2:2 + n]), list(outs[2 + n:2 + 2 * n]), outs[-1]


def _grads_to_sibling_wait(send_sems, recv_sems, gs, lands, views, after, *, name):
    n = len(gs)

    def body(*refs):
        for cp in _sibling_copies(refs[:n], refs[n:2 * n], gs, views, refs[2 * n], refs[2 * n + 1]):
            cp.wait_send()
            cp.wait_recv()

    outs = pl.pallas_call(
        body, name=name, out_shape=tuple(pltpu.HBM(x.shape, x.dtype) for x in list(gs) + list(lands)),
        in_specs=[_HBM] * (2 * n) + [_SEM, _SEM, _ANY], out_specs=tuple([_HBM] * (2 * n)),
        input_output_aliases={i: i for i in range(2 * n)},
        compiler_params=pltpu.CompilerParams(has_side_effects=_EFFECT),
    )(*gs, *lands, send_sems, recv_sems, after)
    return list(outs[:n]), list(outs[n:])


def _cast_place_layer(shard, l, kind, chip_idx, after, *, name):
    _, r, c = shard.shape
    tr = _row_tile(r, c * 4, 16)
    nt = r // tr

    def body(_, s_ref, *rest):
        rest[-1][...] = s_ref[...].astype(MXU_DTYPE)

    if kind == "row":
        out_spec = pl.BlockSpec((tr, c), lambda i, j_ref: (j_ref[0] * nt + i, 0))
    elif kind == "col":
        out_spec = pl.BlockSpec((tr, c), lambda i, j_ref: (i, j_ref[0]))
    else:
        out_spec = pl.BlockSpec((None, tr, c), lambda i, j_ref: (j_ref[0], i, 0))
    extra = [] if after is None else [after]
    grid_spec = pltpu.PrefetchScalarGridSpec(
        num_scalar_prefetch=1, grid=(nt,), in_specs=[pl.BlockSpec((None, tr, c), lambda i, j_ref: (l, i, 0))] + [_ANY] * len(extra),
        out_specs=out_spec)
    return pl.pallas_call(body, grid_spec=grid_spec, out_shape=jax.ShapeDtypeStruct(_gather_shape(r, c, kind)[1:], MXU_DTYPE),
                          compiler_params=_params("parallel"), name=name)(chip_idx, shard, *extra)


def _half_block(ref, kind, r, c, j, h):
    rows = _half(r, h)
    if kind == "row":
        return ref.at[pl.ds(pl.multiple_of(j * r + h * (r // 2), r // 2), r // 2)]
    if kind == "col":
        return ref.at[rows, pl.ds(pl.multiple_of(j * c, c), c)]
    return ref.at[j, rows]


def _gather_ici_copies(buf_refs, shapes, kinds, send_sems, recv_sems):
    mx, my, mc = lax.axis_index("x"), lax.axis_index("y"), lax.axis_index("c")
    chips = [(1 - mx, my), (mx, 1 - my), (1 - mx, 1 - my)]
    copies = []
    for i, (r, c) in enumerate(shapes):
        mine = _half_block(buf_refs[i], kinds[i], r, c, 2 * mx + my, mc)
        for k, (px, py) in enumerate(chips):
            copies.append(pltpu.make_async_remote_copy(
                src_ref=mine, dst_ref=mine, send_sem=send_sems.at[3 * i + k], recv_sem=recv_sems.at[3 * i + k],
                device_id=(px, py, mc), device_id_type=_MESH))
    return copies


def _gather_start(bufs, shapes, kinds, *, name):
    n = len(bufs)

    def body(*refs):
        send_sems, recv_sems, token = refs[n], refs[n + 1], refs[-1]
        for cp in _gather_ici_copies(refs[:n], shapes, kinds, send_sems, recv_sems):
            cp.start()
        token[...] = jnp.zeros_like(token)

    outs = pl.pallas_call(
        body, name=name,
        out_shape=(pltpu.SemaphoreType.DMA((3 * n,)), pltpu.SemaphoreType.DMA((3 * n,)),
                   *[pltpu.HBM(b.shape, b.dtype) for b in bufs], jax.ShapeDtypeStruct((SUBLANE, LANE), F32)),
        in_specs=[_HBM] * n, out_specs=(_SEM, _SEM, *[_HBM] * n, pl.BlockSpec(memory_space=pltpu.VMEM)),
        input_output_aliases={i: 2 + i for i in range(n)},
        compiler_params=pltpu.CompilerParams(has_side_effects=_EFFECT),
    )(*[pltpu.with_memory_space_constraint(b, pltpu.HBM) for b in bufs])
    return outs[0], outs[1], list(outs[2:2 + n]), outs[-1]


def _gather_wait(send_sems, recv_sems, bufs, shapes, kinds, after, *, name):
    n = len(bufs)

    def body(*refs):
        for cp in _gather_ici_copies(refs[:n], shapes, kinds, refs[n], refs[n + 1]):
            cp.wait_send()
            cp.wait_recv()

    outs = pl.pallas_call(
        body, name=name, out_shape=tuple(pltpu.HBM(b.shape, b.dtype) for b in bufs),
        in_specs=[_HBM] * n + [_SEM, _SEM, _ANY], out_specs=tuple([_HBM] * n), input_output_aliases={i: i for i in range(n)},
        compiler_params=pltpu.CompilerParams(has_side_effects=_EFFECT),
    )(*bufs, send_sems, recv_sems, after)
    return list(outs)


def _gather_forward(bufs, shapes, kinds, *, name):
    n = len(bufs)

    def body(*refs):
        outs = refs[n:2 * n]
        send_sems, recv_sems = refs[2 * n:]
        mx, my, mc = lax.axis_index("x"), lax.axis_index("y"), lax.axis_index("c")
        chips = [(1 - mx, my), (mx, 1 - my), (1 - mx, 1 - my)]
        copies = []
        for i, (r, c) in enumerate(shapes):
            for k, (px, py) in enumerate(chips):
                got = _half_block(outs[i], kinds[i], r, c, 2 * px + py, mc)
                cp = pltpu.make_async_remote_copy(src_ref=got, dst_ref=got, send_sem=send_sems.at[3 * i + k],
                                                  recv_sem=recv_sems.at[3 * i + k], device_id=(mx, my, 1 - mc), device_id_type=_MESH)
                cp.start()
                copies.append(cp)
        for cp in copies:
            cp.wait()

    return pl.pallas_call(
        body, out_shape=[jax.ShapeDtypeStruct(b.shape, b.dtype) for b in bufs], in_specs=[_ANY] * n, out_specs=[_ANY] * n,
        input_output_aliases={i: i for i in range(n)},
        scratch_shapes=[pltpu.SemaphoreType.DMA((3 * n,)), pltpu.SemaphoreType.DMA((3 * n,))], name=name)(*bufs)


def _chip_exchange_copies(pair_refs, land_refs, pairs, views, send_sems, recv_sems):
    mx, my, mc = lax.axis_index("x"), lax.axis_index("y"), lax.axis_index("c")
    me = 2 * mx + my
    chips = [(1 - mx, my), (mx, 1 - my), (1 - mx, 1 - my)]
    copies = []
    for i in range(len(pairs)):
        for k, (px, py) in enumerate(chips):
            j = 2 * px + py
            if views[i] == "chip":
                src = pair_refs[i].at[j]
            else:
                c = pairs[i].shape[1] // N_CHIPS
                src = pair_refs[i].at[:, pl.ds(pl.multiple_of(j * c, c), c)]
            copies.append(pltpu.make_async_remote_copy(
                src_ref=src, dst_ref=land_refs[i].at[me], send_sem=send_sems.at[3 * i + k], recv_sem=recv_sems.at[3 * i + k],
                device_id=(px, py, mc), device_id_type=_MESH))
    return copies


def _quad_shape(p, view):
    return p.shape if view == "chip" else (N_CHIPS, p.shape[0], p.shape[1] // N_CHIPS)


def _grads_to_chips_start(pairs, views, *, name):
    n = len(pairs)
    lands = [pltpu.with_memory_space_constraint(lax.empty(_quad_shape(p, v), p.dtype), pltpu.HBM) for p, v in zip(pairs, views)]

    def body(*refs):
        pair_refs, land_refs = refs[:n], refs[n:2 * n]
        send_sems, recv_sems = refs[2 * n], refs[2 * n + 1]
        token = refs[-1]
        for cp in _chip_exchange_copies(pair_refs, land_refs, pairs, views, send_sems, recv_sems):
            cp.start()
        token[...] = jnp.zeros_like(token)

    outs = pl.pallas_call(
        body, name=name,
        out_shape=(pltpu.SemaphoreType.DMA((3 * n,)), pltpu.SemaphoreType.DMA((3 * n,)),
                   *[pltpu.HBM(p.shape, p.dtype) for p in pairs], *[pltpu.HBM(l.shape, l.dtype) for l in lands],
                   jax.ShapeDtypeStruct((SUBLANE, LANE), F32)),
        in_specs=[_HBM] * (2 * n), out_specs=(_SEM, _SEM, *[_HBM] * (2 * n), pl.BlockSpec(memory_space=pltpu.VMEM)),
        input_output_aliases={i: 2 + i for i in range(2 * n)},
        compiler_params=pltpu.CompilerParams(has_side_effects=_EFFECT),
    )(*[pltpu.with_memory_space_constraint(p, pltpu.HBM) for p in pairs], *lands)
    return outs[0], outs[1], list(outs[2:2 + n]), list(outs[2 + n:2 + 2 * n]), outs[-1]


def _grads_to_chips_wait(send_sems, recv_sems, pairs, lands, views, after, *, name):
    n = len(pairs)

    def body(*refs):
        pair_refs, land_refs = refs[:n], refs[n:2 * n]
        s_sems, r_sems = refs[2 * n], refs[2 * n + 1]
        for cp in _chip_exchange_copies(pair_refs, land_refs, pairs, views, s_sems, r_sems):
            cp.wait_send()
            cp.wait_recv()

    outs = pl.pallas_call(
        body, name=name, out_shape=tuple(pltpu.HBM(x.shape, x.dtype) for x in list(pairs) + list(lands)),
        in_specs=[_HBM] * (2 * n) + [_SEM, _SEM, _ANY], out_specs=tuple([_HBM] * (2 * n)),
        input_output_aliases={i: i for i in range(2 * n)},
        compiler_params=pltpu.CompilerParams(has_side_effects=_EFFECT),
    )(*pairs, *lands, send_sems, recv_sems, after)
    return list(outs[n:])


def _grads_share(tots, *, name):
    n = len(tots)

    def body(*refs):
        ins, outs = refs[:n], refs[n:2 * n]
        send_sems, recv_sems = refs[2 * n:]
        mx, my, mc = lax.axis_index("x"), lax.axis_index("y"), lax.axis_index("c")
        copies = []
        for i in range(n):
            cp = pltpu.make_async_remote_copy(src_ref=ins[i], dst_ref=outs[i], send_sem=send_sems.at[i], recv_sem=recv_sems.at[i],
                                              device_id=(mx, my, 1 - mc), device_id_type=_MESH)
            cp.start()
            copies.append(cp)
        for cp in copies:
            cp.wait()

    return pl.pallas_call(
        body, out_shape=[jax.ShapeDtypeStruct(t.shape, t.dtype) for t in tots], in_specs=[_ANY] * n, out_specs=[_ANY] * n,
        scratch_shapes=[pltpu.SemaphoreType.DMA((n,)), pltpu.SemaphoreType.DMA((n,))], name=name)(*tots)


def _pair_sum(g, recv, view, c_idx, *, name):
    def body(c_ref, a_ref, b_ref, o_ref):
        o_ref[...] = (a_ref[...] + b_ref[...]).astype(WIRE_DTYPE)

    if view == "chip":
        nch, r, c = g.shape
        tr = _row_tile(r // 2, nch * c * 4, 16)
        gv = g.reshape(nch, 2, r // 2, c)
        grid = ((r // 2) // tr,)
        in_specs = [pl.BlockSpec((nch, None, tr, c), lambda i, c_ref: (0, c_ref[0], i, 0)),
                    pl.BlockSpec((nch, tr, c), lambda i, c_ref: (0, i, 0))]
        out_spec = pl.BlockSpec((nch, tr, c), lambda i, c_ref: (0, i, 0))
        sem = ("parallel",)
    else:
        r, c4 = g.shape
        tr = _row_tile(r // 2, c4 * 4, 16)
        gv = g.reshape(2, r // 2, c4)
        grid = ((r // 2) // tr,)
        in_specs = [pl.BlockSpec((None, tr, c4), lambda i, c_ref: (c_ref[0], i, 0)), pl.BlockSpec((tr, c4), lambda i, c_ref: (i, 0))]
        out_spec = pl.BlockSpec((tr, c4), lambda i, c_ref: (i, 0))
        sem = ("parallel",)
    grid_spec = pltpu.PrefetchScalarGridSpec(num_scalar_prefetch=1, grid=grid, in_specs=in_specs, out_specs=out_spec)
    return pl.pallas_call(body, grid_spec=grid_spec, out_shape=jax.ShapeDtypeStruct(recv.shape, WIRE_DTYPE),
                          compiler_params=_params(*sem), name=name)(c_idx, gv, recv)


def _quad_sum(gs, recvs, quads, view, chip_idx, c_idx, *, name):
    nl = len(quads)
    nch, rh, c = quads[0].shape
    tr = _row_tile(rh, c * 4, 16)

    def body(_, __, *refs):
        o_ref = refs[-1]
        per = nch + 1
        for l in range(nl):
            grp = refs[l * per:(l + 1) * per]
            acc = grp[0][...] + grp[1][...]
            for r in grp[2:]:
                acc = acc + r[...].astype(F32)
            o_ref[l] = acc

    if view == "chip":
        own = [pl.BlockSpec((None, None, tr, c), lambda i, j, h: (j[0], h[0], i, 0)),
               pl.BlockSpec((None, tr, c), lambda i, j, h: (j[0], i, 0))]
        gviews = [g.reshape(nch, 2, rh, c) for g in gs]
    else:
        own = [pl.BlockSpec((None, tr, c), lambda i, j, h: (h[0], i, j[0])), pl.BlockSpec((tr, c), lambda i, j, h: (i, j[0]))]
        gviews = [g.reshape(2, rh, nch * c) for g in gs]
    assert nch & (nch - 1) == 0
    got = [pl.BlockSpec((None, tr, c), functools.partial(lambda i, j, h, k: ((j[0] + k) & (nch - 1), i, 0), k=k))
           for k in range(1, nch)]
    ins = []
    for l in range(nl):
        ins += [gviews[l], recvs[l]] + [quads[l]] * (nch - 1)
    grid_spec = pltpu.PrefetchScalarGridSpec(
        num_scalar_prefetch=2, grid=(rh // tr,), in_specs=(own + got) * nl,
        out_specs=pl.BlockSpec((nl, tr, c), lambda i, j, h: (0, i, 0)))
    return pl.pallas_call(body, grid_spec=grid_spec, out_shape=jax.ShapeDtypeStruct((nl, rh, c), F32),
                          compiler_params=_params("parallel"), name=name)(chip_idx, c_idx, *ins)


def _sum_devices(g8, own, dev_idx, *, name):
    k, rows, cols = g8.shape

    def body(d_ref, a_ref, x_ref, o_ref):
        acc = None
        for i in range(k):
            term = jnp.where(d_ref[0] == i, x_ref[...], a_ref[i])
            acc = term if acc is None else acc + term
        o_ref[...] = acc

    grid_spec = pltpu.PrefetchScalarGridSpec(
        num_scalar_prefetch=1, grid=(1,),
        in_specs=[pl.BlockSpec((k, rows, cols), lambda i, d_ref: (0, 0, 0)), pl.BlockSpec((rows, cols), lambda i, d_ref: (0, 0))],
        out_specs=pl.BlockSpec((rows, cols), lambda i, d_ref: (0, 0)))
    return pl.pallas_call(body, grid_spec=grid_spec, out_shape=jax.ShapeDtypeStruct((rows, cols), g8.dtype),
                          compiler_params=_params("arbitrary"), name=name)(dev_idx, g8, own)


def _adamw(w, g, m, v, *, name):
    rows, cols = w.shape
    tr = rows
    for cand in (256, 128, 64, 32, 16, 8):
        if rows % cand == 0 and cand * cols <= 512 * 1024:
            tr = cand
            break
    c1 = 1.0 - ADAM_B1 ** ADAM_STEP
    c2 = 1.0 - ADAM_B2 ** ADAM_STEP

    def body(w_ref, g_ref, m_ref, v_ref, d_ref, nm_ref, nv_ref):
        gv = g_ref[...]
        nm = ADAM_B1 * m_ref[...] + (1.0 - ADAM_B1) * gv
        nv = ADAM_B2 * v_ref[...] + (1.0 - ADAM_B2) * (gv * gv)
        d_ref[...] = -ADAM_LR * ((nm / c1) / (jnp.sqrt(nv / c2) + ADAM_EPS) + ADAM_WD * w_ref[...])
        nm_ref[...] = nm
        nv_ref[...] = nv

    spec = pl.BlockSpec((tr, cols), lambda i: (i, 0))
    shp = jax.ShapeDtypeStruct((rows, cols), F32)
    return pl.pallas_call(body, grid=(rows // tr,), in_specs=[spec] * 4, out_specs=[spec] * 3, out_shape=[shp] * 3,
                          compiler_params=_params("parallel"), name=name)(w, g, m, v)


def _adamw_halves(w, m, v, mine, other, c_idx, *, name):
    nl, r, c = w.shape
    rh = r // 2
    tr = _row_tile(rh, c * 4)
    c1 = 1.0 - ADAM_B1 ** ADAM_STEP
    c2 = 1.0 - ADAM_B2 ** ADAM_STEP

    def body(c_ref, w_ref, m_ref, v_ref, a_ref, b_ref, g_ref, d_ref, nm_ref, nv_ref):
        gv = jnp.where(pl.program_id(1) == c_ref[0], a_ref[...], b_ref[...])
        nm = ADAM_B1 * m_ref[...] + (1.0 - ADAM_B1) * gv
        nv = ADAM_B2 * v_ref[...] + (1.0 - ADAM_B2) * (gv * gv)
        g_ref[...] = gv
        d_ref[...] = -ADAM_LR * ((nm / c1) / (jnp.sqrt(nv / c2) + ADAM_EPS) + ADAM_WD * w_ref[...])
        nm_ref[...] = nm
        nv_ref[...] = nv

    full = pl.BlockSpec((None, None, tr, c), lambda l, h, i, c_ref: (l, h, i, 0))
    half_mine = pl.BlockSpec((None, tr, c), lambda l, h, i, c_ref: (l, jnp.where(h == c_ref[0], i, 0), 0))
    half_other = pl.BlockSpec((None, tr, c), lambda l, h, i, c_ref: (l, jnp.where(h == c_ref[0], 0, i), 0))
    grid_spec = pltpu.PrefetchScalarGridSpec(num_scalar_prefetch=1, grid=(nl, 2, rh // tr),
                                             in_specs=[full] * 3 + [half_mine, half_other], out_specs=[full] * 4)
    shp = jax.ShapeDtypeStruct((nl, 2, rh, c), F32)
    view = (nl, 2, rh, c)
    outs = pl.pallas_call(body, grid_spec=grid_spec, out_shape=[shp] * 4, compiler_params=_params("arbitrary", "arbitrary", "arbitrary"),
                          name=name)(c_idx, w.reshape(view), m.reshape(view), v.reshape(view), mine, other)
    return [o.reshape(nl, r, c) for o in outs]


WEIGHTS = ["mem_ln_g", "mem_ln_b", "w_in", "sg_ln_g", "sg_ln_b", "sg_w", "sg_b", "conv_w", "conv_b", "dt_bias", "a_log",
           "d_skip", "ssm_norm_g", "p_a", "p_b", "w_mix_o", "w_xq", "w_xkv", "w_xo", "w_ffn_in", "w_ffn_out", "ln_g", "ln_b"]
ARG_NAMES = ["x", "mem"] + WEIGHTS + ["loss_target"] + ["m_" + n for n in WEIGHTS] + ["v_" + n for n in WEIGHTS]
BIG = {"w_in": (1, (1024, 9248)), "p_a": (0, (1024, 1024)), "p_b": (0, (2048, 1024)), "w_mix_o": (0, (1024, 1024)),
       "w_xq": (0, (1024, 1024)), "w_xkv": (1, (1024, 2048)), "w_xo": (0, (1024, 1024)), "w_ffn_in": (1, (1024, 5632)),
       "w_ffn_out": (0, (2816, 1024))}
SMALL_SHARDED = {"conv_w": (4, 3072), "ln_g": (3, 1024), "ln_b": (3, 1024)}
SMALL = [n for n in WEIGHTS if n not in BIG]
W_IN_MAP = ((0, 4096, "main", 0), (4096, 7168, "main", XBC_COL0), (7168, 7200, "dt", 0), (7200, 9248, "main", GAB_COL0))
W_IN_SHARD = 9248 // N_CHIPS


def _w_in_chip_major(gm, gd):
    src = {"main": gm, "dt": gd}
    blocks = []
    for j in range(N_CHIPS):
        lo, hi = j * W_IN_SHARD, (j + 1) * W_IN_SHARD
        parts = [src[k][:, o + max(lo, a) - a:o + min(hi, b) - a] for a, b, k, o in W_IN_MAP if max(lo, a) < min(hi, b)]
        blocks.append(jnp.concatenate(parts, axis=1))
    return jnp.stack(blocks)


def _w_in_reassemble(wc):
    def cols(a, b):
        out = []
        for j in range(N_CHIPS):
            lo, hi = max(a, j * W_IN_SHARD), min(b, (j + 1) * W_IN_SHARD)
            if lo < hi:
                out.append(wc[j][:, lo - j * W_IN_SHARD:hi - j * W_IN_SHARD])
        return out

    main = sorted((m for m in W_IN_MAP if m[2] == "main"), key=lambda m: m[3])
    w_main = jnp.concatenate([p for a, b, _, _ in main for p in cols(a, b)], axis=1)
    (a, b, _, _), = [m for m in W_IN_MAP if m[2] == "dt"]
    w_dt = jnp.pad(jnp.concatenate(cols(a, b), axis=1), ((0, 0), (0, HEAD_PAD - (b - a))))
    return w_main, w_dt
GATHER_KIND = {"w_in": "chip", "p_a": "row", "p_b": "row", "w_mix_o": "row", "w_xq": "row", "w_xkv": "col", "w_xo": "row",
               "w_ffn_in": "col", "w_ffn_out": "row", "conv_w": "chip", "ln_g": "chip", "ln_b": "chip"}
GRAD_VIEW = {n: ("col" if k == "col" else "chip") for n, k in GATHER_KIND.items() if n in BIG}


def _shard_shape(name):
    axis, (r, c) = BIG[name]
    return (r // N_CHIPS, c) if axis == 0 else (r, c // N_CHIPS)


def _pad_rows(flat, cols, row_mult):
    n = flat.shape[0]
    rows = -(-n // cols)
    rows = -(-rows // row_mult) * row_mult
    return jnp.pad(flat, (0, rows * cols - n)).reshape(rows, cols)


def _gather_small_params(a, chip):
    names = list(SMALL_SHARDED)
    kinds = [GATHER_KIND[n] for n in names]
    bufs = [_cast_place(a[n], GATHER_KIND[n], F32, chip.reshape(1), name=f"place_{n}") for n in names]
    outs = _gather_params(bufs, [a[n].shape[1:] for n in names], kinds, name="gather_small_params")
    full = {}
    for n, o in zip(names, outs):
        _, _, r, c = o.shape
        full[n] = jnp.transpose(o, (0, 2, 1, 3)).reshape(DEPTH, r, N_CHIPS * c)
    return full


GATHER_GROUPS = (("w_in",), tuple(n for n in BIG if n != "w_in"))


def _gather_group_start(a, l, names, chip, after, *, tag):
    bufs = [_cast_place_layer(a[n], l, GATHER_KIND[n], chip.reshape(1), after, name=f"place_{n}_l{l}") for n in names]
    return _gather_start(bufs, [a[n].shape[1:] for n in names], [GATHER_KIND[n] for n in names], name=f"gather_start_{tag}")


def _gather_group_finish(a, names, flight, after, *, tag):
    send_sems, recv_sems, bufs, token = flight
    shapes, kinds = [a[n].shape[1:] for n in names], [GATHER_KIND[n] for n in names]
    bufs = _gather_wait(send_sems, recv_sems, bufs, shapes, kinds, token if after is None else after, name=f"gather_wait_{tag}")
    full = dict(zip(names, _gather_forward(bufs, shapes, kinds, name=f"gather_forward_{tag}")))
    if "w_in" in full:
        full["w_main"], full["w_dt"] = _w_in_reassemble(full.pop("w_in"))
    return full


def _layer_weights(a, big, small, l):
    w = dict(big)
    for n in SMALL_SHARDED:
        w[n] = small[n][l]
    for n in ["sg_ln_g", "sg_ln_b", "sg_w", "conv_b", "ssm_norm_g"]:
        w[n] = a[n][l]
    w["sg_bcol"] = a["sg_b"][l][..., None]
    for n in ["dt_bias", "a_log"]:
        w[n + "8"] = _pad_heads(a[n][l])
    w["d_skipx"] = _expand_heads(a["d_skip"][l])
    return w


def _grad_views(grads, names):
    gs = []
    for n in names:
        axis, _ = BIG[n]
        r, c = _shard_shape(n)
        if n == "w_in":
            gs.append(_w_in_chip_major(grads["w_main"], grads["w_dt"]))
        elif axis == 0:
            gs.append(grads[n].reshape(N_CHIPS, r, c))
        else:
            gs.append(grads[n])
    return gs


class _GradExchange:
    def __init__(self, grads, names, c_idx, tag):
        self.names, self.c_idx, self.tag = names, c_idx, tag
        self.views = [GRAD_VIEW[n] for n in names]
        self.gs = _grad_views(grads, names)

    def start(self):
        self.sems = _grads_to_sibling_start(self.gs, self.views, name=f"grads_to_sibling_start_{self.tag}")
        return self.sems[4]

    def cross(self, after):
        send_sems, recv_sems, gs, lands, token = self.sems
        self.gs, self.recv = _grads_to_sibling_wait(send_sems, recv_sems, gs, lands, self.views, token if after is None else after,
                                                    name=f"grads_to_sibling_wait_{self.tag}")
        cpre = self.c_idx.reshape(1)
        pairs = [_pair_sum(g, rv, v, cpre, name=f"grads_pair_sum_{n}_{self.tag}")
                 for g, rv, v, n in zip(self.gs, self.recv, self.views, self.names)]
        self.sems = _grads_to_chips_start(pairs, self.views, name=f"grads_to_chips_start_{self.tag}")
        return self.sems[4]

    def finish(self, after):
        send_sems, recv_sems, pairs, lands, _ = self.sems
        quads = _grads_to_chips_wait(send_sems, recv_sems, pairs, lands, self.views, after, name=f"grads_to_chips_wait_{self.tag}")
        return {n: (g, rv, q) for n, g, rv, q in zip(self.names, self.gs, self.recv, quads)}


def _finish_big_grads(parts, c_idx, chip):
    tots = [_quad_sum([parts[l][n][0] for l in range(DEPTH)], [parts[l][n][1] for l in range(DEPTH)],
                      [parts[l][n][2] for l in range(DEPTH)], GRAD_VIEW[n], chip.reshape(1), c_idx.reshape(1),
                      name=f"grads_chip_sum_{n}") for n in BIG]
    others = _grads_share(tots, name="grads_share")
    return {n: (t, o) for n, t, o in zip(BIG, tots, others)}


def _direct_copies(x_ref, land_ref, send_sems, recv_sems):
    mx, my, mc = lax.axis_index("x"), lax.axis_index("y"), lax.axis_index("c")
    me = 4 * mx + 2 * my + mc
    copies = []
    for k in range(N_DEV - 1):
        f = k + 1
        to = (mx ^ (f >> 2 & 1), my ^ (f >> 1 & 1), mc ^ (f & 1))
        copies.append(pltpu.make_async_remote_copy(src_ref=x_ref, dst_ref=land_ref.at[me], send_sem=send_sems.at[k],
                                                   recv_sem=recv_sems.at[k], device_id=to, device_id_type=_MESH))
    return copies


def _all_gather8_start(x, *, name):
    land = pltpu.with_memory_space_constraint(lax.empty((N_DEV,) + x.shape, x.dtype), pltpu.HBM)

    def body(x_ref, land_ref, send_sems, recv_sems, x_out, land_out, token):
        for cp in _direct_copies(x_ref, land_ref, send_sems, recv_sems):
            cp.start()
        token[...] = jnp.zeros_like(token)

    n = N_DEV - 1
    return pl.pallas_call(
        body, name=name,
        out_shape=(pltpu.SemaphoreType.DMA((n,)), pltpu.SemaphoreType.DMA((n,)), pltpu.HBM(x.shape, x.dtype),
                   pltpu.HBM(land.shape, land.dtype), jax.ShapeDtypeStruct((SUBLANE, LANE), F32)),
        in_specs=[_HBM, _HBM], out_specs=(_SEM, _SEM, _HBM, _HBM, pl.BlockSpec(memory_space=pltpu.VMEM)),
        input_output_aliases={0: 2, 1: 3}, compiler_params=pltpu.CompilerParams(has_side_effects=_EFFECT),
    )(pltpu.with_memory_space_constraint(x, pltpu.HBM), land)


def _all_gather8_wait(send_sems, recv_sems, x, land, after, *, name):
    def body(x_ref, land_ref, s_sems, r_sems, _, x_out, land_out):
        for cp in _direct_copies(x_ref, land_ref, s_sems, r_sems):
            cp.wait_send()
            cp.wait_recv()

    return pl.pallas_call(
        body, name=name, out_shape=(pltpu.HBM(x.shape, x.dtype), pltpu.HBM(land.shape, land.dtype)),
        in_specs=[_HBM, _HBM, _SEM, _SEM, _ANY], out_specs=(_HBM, _HBM), input_output_aliases={0: 0, 1: 1},
        compiler_params=pltpu.CompilerParams(has_side_effects=_EFFECT),
    )(x, land, send_sems, recv_sems, after)


def _pack_small(small):
    return _pad_rows(jnp.concatenate([small[n].reshape(-1) for n in small]), LANE, SUBLANE)


def _unpack_small(small, g8, packed, chip, c_idx, *, name):
    names = list(small)
    tot = _sum_devices(g8, packed, (2 * chip + c_idx).reshape(1), name=name).reshape(-1)
    out, off = {}, 0
    for n in names:
        sz = small[n].size
        full = tot[off:off + sz].reshape(small[n].shape)
        off += sz
        if n in SMALL_SHARDED:
            cs = SMALL_SHARDED[n][1] // N_CHIPS
            full = lax.dynamic_slice_in_dim(full, chip * cs, cs, axis=-1)
        out[n] = full
    return out


def kernel(x, mem, mem_ln_g, mem_ln_b, w_in, sg_ln_g, sg_ln_b, sg_w, sg_b, conv_w, conv_b, dt_bias, a_log, d_skip, ssm_norm_g, p_a, p_b, w_mix_o, w_xq, w_xkv, w_xo, w_ffn_in, w_ffn_out, ln_g, ln_b, loss_target, m_mem_ln_g, m_mem_ln_b, m_w_in, m_sg_ln_g, m_sg_ln_b, m_sg_w, m_sg_b, m_conv_w, m_conv_b, m_dt_bias, m_a_log, m_d_skip, m_ssm_norm_g, m_p_a, m_p_b, m_w_mix_o, m_w_xq, m_w_xkv, m_w_xo, m_w_ffn_in, m_w_ffn_out, m_ln_g, m_ln_b, v_mem_ln_g, v_mem_ln_b, v_w_in, v_sg_ln_g, v_sg_ln_b, v_sg_w, v_sg_b, v_conv_w, v_conv_b, v_dt_bias, v_a_log, v_d_skip, v_ssm_norm_g, v_p_a, v_p_b, v_w_mix_o, v_w_xq, v_w_xkv, v_w_xo, v_w_ffn_in, v_w_ffn_out, v_ln_g, v_ln_b):
    a = dict(zip(ARG_NAMES, (x, mem, mem_ln_g, mem_ln_b, w_in, sg_ln_g, sg_ln_b, sg_w, sg_b, conv_w, conv_b, dt_bias, a_log, d_skip, ssm_norm_g, p_a, p_b, w_mix_o, w_xq, w_xkv, w_xo, w_ffn_in, w_ffn_out, ln_g, ln_b, loss_target, m_mem_ln_g, m_mem_ln_b, m_w_in, m_sg_ln_g, m_sg_ln_b, m_sg_w, m_sg_b, m_conv_w, m_conv_b, m_dt_bias, m_a_log, m_d_skip, m_ssm_norm_g, m_p_a, m_p_b, m_w_mix_o, m_w_xq, m_w_xkv, m_w_xo, m_w_ffn_in, m_w_ffn_out, m_ln_g, m_ln_b, v_mem_ln_g, v_mem_ln_b, v_w_in, v_sg_ln_g, v_sg_ln_b, v_sg_w, v_sg_b, v_conv_w, v_conv_b, v_dt_bias, v_a_log, v_d_skip, v_ssm_norm_g, v_p_a, v_p_b, v_w_mix_o, v_w_xq, v_w_xkv, v_w_xo, v_w_ffn_in, v_w_ffn_out, v_ln_g, v_ln_b)))
    c_idx = lax.axis_index("c").astype(jnp.int32)
    chip = (2 * lax.axis_index("x") + lax.axis_index("y")).astype(jnp.int32)

    small = _gather_small_params(a, chip)
    ga, gb = GATHER_GROUPS
    flights = {(0, 0): _gather_group_start(a, 0, ga, chip, small["ln_b"], tag="l0_a")}
    flights[0, 1] = _gather_group_start(a, 0, gb, chip, flights[0, 0][3], tag="l0_b")

    def layer_weights(after, l):
        first = _gather_group_finish(a, ga, flights[l, 0], after if l else flights[l, 1][3], tag=f"l{l}_a")

        def rest(w, after_b):
            more = _gather_group_finish(a, gb, flights[l, 1], after_b, tag=f"l{l}_b")
            if l + 1 < DEPTH:
                flights[l + 1, 0] = _gather_group_start(a, l + 1, ga, chip, more["p_a"], tag=f"l{l + 1}_a")
                flights[l + 1, 1] = _gather_group_start(a, l + 1, gb, chip, flights[l + 1, 0][3], tag=f"l{l + 1}_b")
                more["p_a"] = more["p_a"] + flights[l + 1, 1][3][0, 0].astype(MXU_DTYPE)
            return {k: v for k, v in {**w, **more}.items() if k != "rest"}

        return dict(_layer_weights(a, first, small, l), rest=rest)

    layers = [functools.partial(layer_weights, l=l) for l in range(DEPTH)]
    exchanges, seen, small_flight = [], {}, {}

    def start_exchange(l, names, grads_l):
        ex = _GradExchange(grads_l, names, c_idx, f"l{l}_{names[0]}")
        tokens = [ex.start()]
        if exchanges:
            tokens.append(exchanges[-1][1].cross(tokens[0]))
        exchanges.append((l, ex))
        seen[l] = grads_l
        if l == 0 and names == GRAD_GROUPS[-1]:
            tokens.append(ex.cross(None))
            small = {}
            for n in SMALL:
                if n.startswith("mem_ln"):
                    continue
                per_layer = []
                for k in range(DEPTH):
                    g = seen[k][n]
                    if n in ("dt_bias", "a_log", "d_skip"):
                        g = g[0, :SSM_HEADS]
                    per_layer.append(g.reshape(a[n].shape[1:-1] + (-1,)))
                small[n] = jnp.stack(per_layer)
            small_flight["small"] = small
            small_flight["sems"] = _all_gather8_start(_pack_small(small), name="gather_small_grads_start")
            tokens.append(small_flight["sems"][4])
        return sum(tokens[1:], tokens[0])

    lsum, grad_x, grads, d_mem_g, d_mem_b = _local_step(x, mem, loss_target, mem_ln_g, mem_ln_b, layers, start_exchange)
    loss = lax.psum(0.5 * jnp.sum(lsum) / D_MODEL, ("x", "y", "c"))

    parts = [{} for _ in range(DEPTH)]
    for l, ex in exchanges:
        parts[l].update(ex.finish(grad_x))
    halves = _finish_big_grads(parts, c_idx, chip)
    gw = {}
    send_sems, recv_sems, packed, land, _ = small_flight["sems"]
    packed, g8 = _all_gather8_wait(send_sems, recv_sems, packed, land, grad_x, name="gather_small_grads_wait")
    gw.update(_unpack_small(small_flight["small"], g8, packed, chip, c_idx, name="small_grads_sum"))
    mem_small = {"mem_ln_g": d_mem_g, "mem_ln_b": d_mem_b}
    mem_packed = _pack_small(mem_small)
    gw.update(_unpack_small(mem_small, _all_gather8(mem_packed, name="gather_mem_ln_grads"), mem_packed, chip, c_idx,
                            name="mem_ln_grads_sum"))

    delta, new_m, new_v = {}, {}, {}
    for n in BIG:
        mine, other = halves[n]
        gw[n], delta[n], new_m[n], new_v[n] = _adamw_halves(a[n], a["m_" + n], a["v_" + n], mine, other, c_idx.reshape(1),
                                                             name=f"adamw_{n}")
    for n in SMALL:
        shp = a[n].shape
        view = (-1, LANE) if a[n].size % LANE == 0 else (1, -1)
        outs = _adamw(*[v.reshape(view) for v in (a[n], gw[n], a["m_" + n], a["v_" + n])], name=f"adamw_{n}")
        delta[n], new_m[n], new_v[n] = (o.reshape(shp) for o in outs)
    return (loss, grad_x, *[gw[n].reshape(a[n].shape) for n in WEIGHTS], *[delta[n] for n in WEIGHTS],
            *[new_m[n] for n in WEIGHTS], *[new_v[n] for n in WEIGHTS])
```

```python
import functools
import math

import jax
import jax.numpy as jnp
from jax import lax
from jax.experimental import pallas as pl
from jax.experimental.pallas import tpu as pltpu

F32 = jnp.float32
MXU_DTYPE = jnp.bfloat16
WIRE_DTYPE = jnp.bfloat16
STASH_DTYPE = jnp.bfloat16

D_MODEL = 1024
DEPTH = 2
CHUNK = 128
SG_GROUPS = 8
SSM_INNER = 2048
SSM_HEADDIM = 64
SSM_HEADS = 32
SSM_STATE = 128
SSM_GROUPS = 4
SSM_CONV = 4
SSM_CONV_DIM = 3072
X_HEADS = 4
X_HEADDIM = 256
FFN_HIDDEN = 2816
ALPHA = float((2 * DEPTH) ** 0.25)
LN_EPS = 1e-5
RMS_EPS = 1e-5
ADAM_LR = 0.001
ADAM_B1 = 0.9
ADAM_B2 = 0.999
ADAM_EPS = 1e-08
ADAM_WD = 0.01
ADAM_STEP = 10

MAIN_COLS = 9216
UVZ_COLS = 4096
GAB_COL0 = 4096
XBC_COL0 = 6144
HEAD_PAD = 128

VMEM_LIMIT = 56 * 1024 * 1024
BLOCK_BYTES = 2 * 1024 * 1024
ROW_TILES = (512, 256, 128)
LANE = 128
SUBLANE = 8

N_CHIPS = 4
N_DEV = 8


def _pick(n, cands):
    for c in cands:
        if n % c == 0:
            return c
    return n


MM_TILE_MAX = 1536
MM_OPERAND_BYTES = 12 * 1024 * 1024


def _div_tile(n, limit):
    best = None
    for t in range(LANE, min(n, limit) + 1, LANE):
        if n % t == 0:
            best = t
    return n if best is None else best


def _params(*sem):
    return pltpu.CompilerParams(dimension_semantics=tuple(sem), vmem_limit_bytes=VMEM_LIMIT)


_ANY = pl.BlockSpec(memory_space=pl.ANY)
_MESH = pl.DeviceIdType.MESH


def _nt(a, b):
    return lax.dot_general(a, b, (((1,), (1,)), ((), ())), preferred_element_type=F32)


def _tn(a, b):
    return lax.dot_general(a, b, (((0,), (0,)), ((), ())), preferred_element_type=F32)


def _nn(a, b):
    return jnp.dot(a, b, preferred_element_type=F32)


def _sigmoid(x):
    return 0.5 * jnp.tanh(0.5 * x) + 0.5


def _split3(v):
    def top(x):
        bits = lax.bitcast_convert_type(x, jnp.uint32) & jnp.uint32(0xFFFF0000)
        return lax.bitcast_convert_type(bits, F32)

    v1 = top(v)
    r1 = v - v1
    v2 = top(r1)
    v3 = r1 - v2
    return v1.astype(jnp.bfloat16), v2.astype(jnp.bfloat16), v3.astype(jnp.bfloat16)


def _dot_exact(a, b, dn, data):
    if data == 0:
        mat = b.astype(jnp.bfloat16)
        return sum(lax.dot_general(p, mat, dn, preferred_element_type=F32) for p in _split3(a))
    mat = a.astype(jnp.bfloat16)
    return sum(lax.dot_general(mat, p, dn, preferred_element_type=F32) for p in _split3(b))


_DN_NN = (((1,), (0,)), ((), ()))
_DN_TN = (((0,), (0,)), ((), ()))


def _gelu(x):
    return 0.5 * x * (1.0 + lax.erf(x * (2.0 ** -0.5)))


def _gelu_grad(x):
    return 0.5 * (1.0 + lax.erf(x * (2.0 ** -0.5))) + x * jnp.exp(-0.5 * x * x) * (1.0 / math.sqrt(2.0 * math.pi))


def _mm(a, b, *, ta=False, tb=False, out_dtype=F32, after=None, name):
    if ta:
        kdim, m = a.shape
    else:
        m, kdim = a.shape
    if tb:
        n, k2 = b.shape[-2:]
    else:
        k2, n = b.shape[-2:]
    assert kdim == k2, (a.shape, b.shape, ta, tb)
    tm = _div_tile(m, MM_TILE_MAX)
    tn = _div_tile(n, MM_TILE_MAX)
    tk = _div_tile(kdim, MM_OPERAND_BYTES // (tm * a.dtype.itemsize + tn * b.dtype.itemsize))
    nk = kdim // tk
    dn = (((0 if ta else 1,), (1 if tb else 0,)), ((), ()))

    extra = [] if after is None else [after]

    def body(a_ref, b_ref, *rest):
        o_ref = rest[len(extra)]
        d = lax.dot_general(a_ref[...].astype(MXU_DTYPE), b_ref[...].astype(MXU_DTYPE), dn, preferred_element_type=F32)
        if nk == 1:
            o_ref[...] = d.astype(out_dtype)
            return
        acc_ref = rest[len(extra) + 1]
        k = pl.program_id(2)

        @pl.when(k == 0)
        def _():
            acc_ref[...] = d

        @pl.when(jnp.logical_and(k > 0, k < nk - 1))
        def _():
            acc_ref[...] += d

        @pl.when(k == nk - 1)
        def _():
            o_ref[...] = (acc_ref[...] + d).astype(out_dtype)

    a_spec = pl.BlockSpec((tk, tm), lambda i, j, k: (k, i)) if ta else pl.BlockSpec((tm, tk), lambda i, j, k: (i, k))
    b_spec = pl.BlockSpec((tn, tk), lambda i, j, k: (j, k)) if tb else pl.BlockSpec((tk, tn), lambda i, j, k: (k, j))
    return pl.pallas_call(
        body, grid=(m // tm, n // tn, nk), in_specs=[a_spec, b_spec] + [_ANY] * len(extra),
        out_specs=pl.BlockSpec((tm, tn), lambda i, j, k: (i, j)),
        out_shape=jax.ShapeDtypeStruct((m, n), out_dtype),
        scratch_shapes=[pltpu.VMEM((tm, tn), F32)] if nk > 1 else [],
        compiler_params=_params("parallel", "parallel", "arbitrary"), name=name)(a, b, *extra)


def _row_spec(tm, c, col=0):
    return pl.BlockSpec((tm, c), lambda i: (i, col))


def _par_spec(shape):
    nd = len(shape)
    return pl.BlockSpec(shape, lambda i: (0,) * nd)


def _ln_fwd(x, f, g, b, *, name):
    t, c = x.shape
    tm = _pick(t, ROW_TILES)
    has_f = f is not None

    def body(*refs):
        if has_f:
            x_ref, f_ref, g_ref, b_ref, y_ref, yb_ref, xh_ref, rs_ref = refs
            r = ALPHA * x_ref[...] + f_ref[...]
        else:
            x_ref, g_ref, b_ref, y_ref, yb_ref, xh_ref, rs_ref = refs
            r = x_ref[...]
        mu = jnp.mean(r, axis=-1, keepdims=True)
        xc = r - mu
        var = jnp.mean(xc * xc, axis=-1, keepdims=True)
        rstd = lax.rsqrt(var + LN_EPS)
        xh = xc * rstd
        y = xh * g_ref[...] + b_ref[...]
        y_ref[...] = y
        yb_ref[...] = y.astype(MXU_DTYPE)
        xh_ref[...] = xh
        rs_ref[...] = jnp.broadcast_to(rstd, rs_ref.shape)

    ins = [x] + ([f] if has_f else []) + [g.reshape(1, c), b.reshape(1, c)]
    in_specs = [_row_spec(tm, c)] * (2 if has_f else 1) + [_par_spec((1, c))] * 2
    return pl.pallas_call(
        body, grid=(t // tm,), in_specs=in_specs,
        out_specs=[_row_spec(tm, c), _row_spec(tm, c), _row_spec(tm, c), _row_spec(tm, LANE)],
        out_shape=[jax.ShapeDtypeStruct((t, c), F32), jax.ShapeDtypeStruct((t, c), MXU_DTYPE),
                   jax.ShapeDtypeStruct((t, c), F32), jax.ShapeDtypeStruct((t, LANE), F32)],
        compiler_params=_params("parallel"), name=name)(*ins)


def _ln_bwd(addends, scales, xh, rs, g, *, name):
    t, c = xh.shape
    tm = _pick(t, ROW_TILES)
    na = len(addends)

    def body(*refs):
        a_refs = refs[:na]
        xh_ref, rs_ref, g_ref, dp_ref, dpb_ref, dg_ref, db_ref = refs[na:]

        @pl.when(pl.program_id(0) == 0)
        def _():
            dg_ref[...] = jnp.zeros_like(dg_ref)
            db_ref[...] = jnp.zeros_like(db_ref)

        dy = None
        for s, r in zip(scales, a_refs):
            term = r[...] if s == 1.0 else s * r[...]
            dy = term if dy is None else dy + term
        xhv = xh_ref[...]
        dxh = dy * g_ref[...]
        m1 = jnp.mean(dxh, axis=-1, keepdims=True)
        m2 = jnp.mean(dxh * xhv, axis=-1, keepdims=True)
        dp = rs_ref[:, 0:1] * (dxh - m1 - xhv * m2)
        dp_ref[...] = dp
        dpb_ref[...] = dp.astype(MXU_DTYPE)
        dg_ref[...] += jnp.sum(dy * xhv, axis=0, keepdims=True)
        db_ref[...] += jnp.sum(dy, axis=0, keepdims=True)

    in_specs = [_row_spec(tm, c)] * (na + 1) + [_row_spec(tm, LANE), _par_spec((1, c))]
    return pl.pallas_call(
        body, grid=(t // tm,), in_specs=in_specs,
        out_specs=[_row_spec(tm, c), _row_spec(tm, c), _par_spec((1, c)), _par_spec((1, c))],
        out_shape=[jax.ShapeDtypeStruct((t, c), F32), jax.ShapeDtypeStruct((t, c), MXU_DTYPE),
                   jax.ShapeDtypeStruct((1, c), F32), jax.ShapeDtypeStruct((1, c), F32)],
        compiler_params=_params("arbitrary"), name=name)(*addends, xh, rs, g.reshape(1, c))


def _add_scaled(addends, scales, *, name):
    t, c = addends[0].shape
    tm = _pick(t, ROW_TILES)
    na = len(addends)

    def body(*refs):
        acc = None
        for s, r in zip(scales, refs[:na]):
            term = r[...] if s == 1.0 else s * r[...]
            acc = term if acc is None else acc + term
        refs[na][...] = acc

    return pl.pallas_call(
        body, grid=(t // tm,), in_specs=[_row_spec(tm, c)] * na, out_specs=_row_spec(tm, c),
        out_shape=jax.ShapeDtypeStruct((t, c), F32), compiler_params=_params("parallel"), name=name)(*addends)


def _loss_head(y, tgt, *, name):
    t, c = y.shape
    tm = _pick(t, ROW_TILES)

    def body(y_ref, t_ref, dy_ref, ls_ref):
        @pl.when(pl.program_id(0) == 0)
        def _():
            ls_ref[...] = jnp.zeros_like(ls_ref)

        e = y_ref[...] - t_ref[...]
        dy_ref[...] = e * (1.0 / c)
        ls_ref[...] += jnp.sum(e * e, axis=0, keepdims=True)

    return pl.pallas_call(
        body, grid=(t // tm,), in_specs=[_row_spec(tm, c)] * 2,
        out_specs=[_row_spec(tm, c), _par_spec((1, c))],
        out_shape=[jax.ShapeDtypeStruct((t, c), F32), jax.ShapeDtypeStruct((1, c), F32)],
        compiler_params=_params("arbitrary"), name=name)(y, tgt)


def _swiglu_fwd(h, *, name):
    t, two_f = h.shape
    fh = two_f // 2
    tm = _pick(t, (256, 128))

    def body(g_ref, u_ref, a_ref):
        g = g_ref[...].astype(F32)
        a_ref[...] = (g * _sigmoid(g) * u_ref[...].astype(F32)).astype(MXU_DTYPE)

    return pl.pallas_call(
        body, grid=(t // tm,), in_specs=[_row_spec(tm, fh, 0), _row_spec(tm, fh, 1)], out_specs=_row_spec(tm, fh),
        out_shape=jax.ShapeDtypeStruct((t, fh), MXU_DTYPE), compiler_params=_params("parallel"), name=name)(h, h)


def _swiglu_bwd(h, da, *, name):
    t, two_f = h.shape
    fh = two_f // 2
    tm = _pick(t, (256, 128))

    def body(g_ref, u_ref, da_ref, dh_ref):
        g = g_ref[...].astype(F32)
        s = _sigmoid(g)
        dav = da_ref[...].astype(F32)
        dh_ref[:, :fh] = (dav * u_ref[...].astype(F32) * (s * (1.0 + g * (1.0 - s)))).astype(MXU_DTYPE)
        dh_ref[:, fh:] = (dav * g * s).astype(MXU_DTYPE)

    return pl.pallas_call(
        body, grid=(t // tm,), in_specs=[_row_spec(tm, fh, 0), _row_spec(tm, fh, 1), _row_spec(tm, fh)],
        out_specs=_row_spec(tm, two_f), out_shape=jax.ShapeDtypeStruct((t, two_f), MXU_DTYPE),
        compiler_params=_params("parallel"), name=name)(h, h, da)


def _attn_probs(q, k):
    s = _nt(q, k) * (X_HEADDIM ** -0.5)
    s = s - jnp.max(s, axis=-1, keepdims=True)
    p = jnp.exp(s)
    return p / jnp.sum(p, axis=-1, keepdims=True)


def _attn_fwd(q, kv, *, bsz, name):
    t = q.shape[0]
    s = t // bsz
    ml = kv.shape[0] // bsz
    hd = X_HEADDIM

    def body(q_ref, k_ref, v_ref, o_ref):
        p = _attn_probs(q_ref[...], k_ref[...])
        o_ref[...] = _nn(p.astype(MXU_DTYPE), v_ref[...]).astype(MXU_DTYPE)

    return pl.pallas_call(
        body, grid=(bsz, X_HEADS),
        in_specs=[pl.BlockSpec((s, hd), lambda b, h: (b, h)), pl.BlockSpec((ml, hd), lambda b, h: (b, h)),
                  pl.BlockSpec((ml, hd), lambda b, h: (b, X_HEADS + h))],
        out_specs=pl.BlockSpec((s, hd), lambda b, h: (b, h)),
        out_shape=jax.ShapeDtypeStruct((t, D_MODEL), MXU_DTYPE),
        compiler_params=_params("parallel", "parallel"), name=name)(q, kv, kv)


def _attn_bwd(q, kv, do, *, bsz, name):
    t = q.shape[0]
    s = t // bsz
    ml = kv.shape[0] // bsz
    hd = X_HEADDIM

    def body(q_ref, k_ref, v_ref, do_ref, dq_ref, dk_ref, dv_ref):
        qv, kk, vv, dov = q_ref[...], k_ref[...], v_ref[...], do_ref[...]
        p = _attn_probs(qv, kk)
        dp = _nt(dov, vv)
        dv_ref[...] = _tn(p.astype(MXU_DTYPE), dov).astype(MXU_DTYPE)
        ds = (p * (dp - jnp.sum(dp * p, axis=-1, keepdims=True)) * (X_HEADDIM ** -0.5)).astype(MXU_DTYPE)
        dq_ref[...] = _nn(ds, kk).astype(MXU_DTYPE)
        dk_ref[...] = _tn(ds, qv).astype(MXU_DTYPE)

    blk_q = pl.BlockSpec((s, hd), lambda b, h: (b, h))
    blk_m = pl.BlockSpec((ml, hd), lambda b, h: (b, h))
    return pl.pallas_call(
        body, grid=(bsz, X_HEADS),
        in_specs=[blk_q, blk_m, pl.BlockSpec((ml, hd), lambda b, h: (b, X_HEADS + h)), blk_q],
        out_specs=[blk_q, blk_m, blk_m],
        out_shape=[jax.ShapeDtypeStruct((t, D_MODEL), MXU_DTYPE), jax.ShapeDtypeStruct((bsz * ml, D_MODEL), MXU_DTYPE),
                   jax.ShapeDtypeStruct((bsz * ml, D_MODEL), MXU_DTYPE)],
        compiler_params=_params("parallel", "parallel"), name=name)(q, kv, kv, do)


def _causal(n):
    row = lax.broadcasted_iota(jnp.int32, (n, n), 0)
    col = lax.broadcasted_iota(jnp.int32, (n, n), 1)
    return row >= col


def _sg_norm(v, g, b):
    gv = _gelu(v)
    mu = jnp.mean(gv, axis=-1, keepdims=True)
    xc = gv - mu
    var = jnp.mean(xc * xc, axis=-1, keepdims=True)
    rstd = lax.rsqrt(var + LN_EPS)
    xh = xc * rstd
    return xh, rstd, xh * g + b


def _sg_fwd(proj, ln_g, ln_b, w, bcol, *, name):
    t = proj.shape[0]
    c = D_MODEL
    gd = c // SG_GROUPS

    def body(u_ref, v_ref, g_ref, b_ref, w_ref, bc_ref, o_ref):
        gu = _gelu(u_ref[...].astype(F32))
        _, _, vn = _sg_norm(v_ref[...].astype(F32), g_ref[...], b_ref[...])
        mask = _causal(CHUNK)
        for g in range(SG_GROUPS):
            sl = slice(g * gd, (g + 1) * gd)
            wg = jnp.where(mask, w_ref[g], 0.0).astype(MXU_DTYPE)
            mixed = _nn(wg, vn[:, sl].astype(MXU_DTYPE)) + bc_ref[g]
            o_ref[:, sl] = (gu[:, sl] * mixed).astype(MXU_DTYPE)

    return pl.pallas_call(
        body, grid=(t // CHUNK,),
        in_specs=[_row_spec(CHUNK, c, 0), _row_spec(CHUNK, c, 1), _par_spec((1, c)), _par_spec((1, c)),
                  _par_spec((SG_GROUPS, CHUNK, CHUNK)), _par_spec((SG_GROUPS, CHUNK, 1))],
        out_specs=_row_spec(CHUNK, c), out_shape=jax.ShapeDtypeStruct((t, c), MXU_DTYPE),
        compiler_params=_params("parallel"), name=name)(proj, proj, ln_g.reshape(1, c), ln_b.reshape(1, c), w, bcol)


def _sg_bwd(proj, dsgo, ln_g, ln_b, w, bcol, dproj, *, name):
    t = proj.shape[0]
    c = D_MODEL
    gd = c // SG_GROUPS

    def body(u_ref, v_ref, d_ref, g_ref, b_ref, w_ref, bc_ref, _, duv_ref, dw_ref, dbc_ref, dg_ref, db_ref, dvn_ref):
        @pl.when(pl.program_id(0) == 0)
        def _():
            dw_ref[...] = jnp.zeros_like(dw_ref)
            dbc_ref[...] = jnp.zeros_like(dbc_ref)
            dg_ref[...] = jnp.zeros_like(dg_ref)
            db_ref[...] = jnp.zeros_like(db_ref)

        u = u_ref[...].astype(F32)
        v = v_ref[...].astype(F32)
        dso = d_ref[...].astype(F32)
        gu = _gelu(u)
        xh, rstd, vn = _sg_norm(v, g_ref[...], b_ref[...])
        mask = _causal(CHUNK)
        for g in range(SG_GROUPS):
            sl = slice(g * gd, (g + 1) * gd)
            wg = jnp.where(mask, w_ref[g], 0.0).astype(MXU_DTYPE)
            vng = vn[:, sl].astype(MXU_DTYPE)
            mixed = _nn(wg, vng) + bc_ref[g]
            duv_ref[:, sl] = (dso[:, sl] * mixed * _gelu_grad(u[:, sl])).astype(MXU_DTYPE)
            dmix = dso[:, sl] * gu[:, sl]
            dmb = dmix.astype(MXU_DTYPE)
            dbc_ref[g] += jnp.sum(dmix, axis=-1, keepdims=True)
            dw_ref[g] += jnp.where(mask, _nt(dmb, vng), 0.0)
            dvn_ref[:, sl] = _tn(wg, dmb)
        dvn = dvn_ref[...]
        dg_ref[...] += jnp.sum(dvn * xh, axis=0, keepdims=True)
        db_ref[...] += jnp.sum(dvn, axis=0, keepdims=True)
        dxh = dvn * g_ref[...]
        m1 = jnp.mean(dxh, axis=-1, keepdims=True)
        m2 = jnp.mean(dxh * xh, axis=-1, keepdims=True)
        dgv = rstd * (dxh - m1 - xh * m2)
        duv_ref[:, c:] = (dgv * _gelu_grad(v)).astype(MXU_DTYPE)

    return pl.pallas_call(
        body, grid=(t // CHUNK,),
        in_specs=[_row_spec(CHUNK, c, 0), _row_spec(CHUNK, c, 1), _row_spec(CHUNK, c), _par_spec((1, c)),
                  _par_spec((1, c)), _par_spec((SG_GROUPS, CHUNK, CHUNK)), _par_spec((SG_GROUPS, CHUNK, 1)), _ANY],
        out_specs=[_row_spec(CHUNK, 2 * c), _par_spec((SG_GROUPS, CHUNK, CHUNK)), _par_spec((SG_GROUPS, CHUNK, 1)),
                   _par_spec((1, c)), _par_spec((1, c))],
        out_shape=[jax.ShapeDtypeStruct(dproj.shape, dproj.dtype), jax.ShapeDtypeStruct((SG_GROUPS, CHUNK, CHUNK), F32),
                   jax.ShapeDtypeStruct((SG_GROUPS, CHUNK, 1), F32), jax.ShapeDtypeStruct((1, c), F32),
                   jax.ShapeDtypeStruct((1, c), F32)],
        scratch_shapes=[pltpu.VMEM((CHUNK, c), F32)], input_output_aliases={7: 0},
        compiler_params=_params("arbitrary"), name=name)(proj, proj, dsgo, ln_g.reshape(1, c), ln_b.reshape(1, c), w, bcol, dproj)


CONV_TC = 512


def _conv_taps(x):
    rows = lax.broadcasted_iota(jnp.int32, x.shape, 0)
    taps = [jnp.where(rows >= SSM_CONV - 1 - k, pltpu.roll(x, SSM_CONV - 1 - k, axis=0), 0.0) for k in range(SSM_CONV - 1)]
    return taps + [x]


def _conv_pre(taps, w_ref, b_ref):
    acc = b_ref[...]
    for k in range(SSM_CONV):
        acc = acc + taps[k] * w_ref[k:k + 1, :]
    return acc


def _conv_fwd(proj, w, b, *, bsz, name):
    t = proj.shape[0]
    s = t // bsz
    nj = SSM_CONV_DIM // CONV_TC
    c0 = XBC_COL0 // CONV_TC

    def body(x_ref, w_ref, b_ref, o_ref):
        pre = _conv_pre(_conv_taps(x_ref[...].astype(F32)), w_ref, b_ref)
        o_ref[...] = (pre * _sigmoid(pre)).astype(o_ref.dtype)

    return pl.pallas_call(
        body, grid=(bsz, nj),
        in_specs=[pl.BlockSpec((s, CONV_TC), lambda bb, j: (bb, c0 + j)), pl.BlockSpec((SSM_CONV, CONV_TC), lambda bb, j: (0, j)),
                  pl.BlockSpec((1, CONV_TC), lambda bb, j: (0, j))],
        out_specs=pl.BlockSpec((s, CONV_TC), lambda bb, j: (bb, j)),
        out_shape=jax.ShapeDtypeStruct((t, SSM_CONV_DIM), STASH_DTYPE),
        compiler_params=_params("parallel", "parallel"), name=name)(proj, w, b.reshape(1, -1))


def _conv_bwd(proj, dact, w, b, dproj, *, bsz, name):
    t = proj.shape[0]
    s = t // bsz
    nj = SSM_CONV_DIM // CONV_TC
    c0 = XBC_COL0 // CONV_TC

    def body(x_ref, d_ref, w_ref, b_ref, _, dx_ref, dw_ref, db_ref):
        @pl.when(pl.program_id(1) == 0)
        def _():
            dw_ref[...] = jnp.zeros_like(dw_ref)
            db_ref[...] = jnp.zeros_like(db_ref)

        taps = _conv_taps(x_ref[...].astype(F32))
        pre = _conv_pre(taps, w_ref, b_ref)
        sg = _sigmoid(pre)
        dpre = d_ref[...].astype(F32) * (sg * (1.0 + pre * (1.0 - sg)))
        rows = lax.broadcasted_iota(jnp.int32, dpre.shape, 0)
        db_ref[...] += jnp.sum(dpre, axis=0, keepdims=True)
        dx = dpre * w_ref[SSM_CONV - 1:SSM_CONV, :]
        for k in range(SSM_CONV):
            dw_ref[k:k + 1, :] += jnp.sum(dpre * taps[k], axis=0, keepdims=True)
        for k in range(SSM_CONV - 1):
            sh = SSM_CONV - 1 - k
            dsh = jnp.where(rows < s - sh, pltpu.roll(dpre, s - sh, axis=0), 0.0)
            dx = dx + dsh * w_ref[k:k + 1, :]
        dx_ref[...] = dx.astype(MXU_DTYPE)

    return pl.pallas_call(
        body, grid=(nj, bsz),
        in_specs=[pl.BlockSpec((s, CONV_TC), lambda j, bb: (bb, c0 + j)), pl.BlockSpec((s, CONV_TC), lambda j, bb: (bb, j)),
                  pl.BlockSpec((SSM_CONV, CONV_TC), lambda j, bb: (0, j)), pl.BlockSpec((1, CONV_TC), lambda j, bb: (0, j)), _ANY],
        out_specs=[pl.BlockSpec((s, CONV_TC), lambda j, bb: (bb, c0 + j)), pl.BlockSpec((SSM_CONV, CONV_TC), lambda j, bb: (0, j)),
                   pl.BlockSpec((1, CONV_TC), lambda j, bb: (0, j))],
        out_shape=[jax.ShapeDtypeStruct(dproj.shape, dproj.dtype), jax.ShapeDtypeStruct((SSM_CONV, SSM_CONV_DIM), F32),
                   jax.ShapeDtypeStruct((1, SSM_CONV_DIM), F32)],
        input_output_aliases={4: 0},
        compiler_params=_params("parallel", "arbitrary"), name=name)(proj, dact, w, b.reshape(1, -1), dproj)


def _softplus(x):
    return jnp.maximum(x, 0.0) + jnp.log1p(jnp.exp(-jnp.abs(x)))


def _pad_heads(v):
    return jnp.broadcast_to(jnp.pad(v.astype(F32), (0, HEAD_PAD - SSM_HEADS))[None, :], (SUBLANE, HEAD_PAD))


def _ssd_prep(dt_raw, dt_bias8, a_log8, *, name):
    t = dt_raw.shape[0]
    n = CHUNK

    def body(r_ref, b_ref, al_ref, dt_ref, cs_ref, dtt_ref, cst_ref):
        dt = _softplus(r_ref[...] + b_ref[0:1, :])
        da = dt * (-jnp.exp(al_ref[0:1, :]))
        row = lax.broadcasted_iota(jnp.int32, (n, n), 0)
        col = lax.broadcasted_iota(jnp.int32, (n, n), 1)
        lower = (col <= row).astype(F32)
        upper = (row <= col).astype(F32)
        eye = (row == col).astype(F32)
        dt_ref[...] = dt
        cs_ref[...] = _dot_exact(lower, da, _DN_NN, 1)
        cst_ref[0] = _dot_exact(da, upper, _DN_TN, 0)
        dtt_ref[0] = _dot_exact(dt, eye, _DN_TN, 0)

    hp = HEAD_PAD
    return pl.pallas_call(
        body, grid=(t // n,),
        in_specs=[_row_spec(n, hp), _par_spec((SUBLANE, hp)), _par_spec((SUBLANE, hp))],
        out_specs=[_row_spec(n, hp), _row_spec(n, hp), pl.BlockSpec((1, hp, n), lambda i: (i, 0, 0)),
                   pl.BlockSpec((1, hp, n), lambda i: (i, 0, 0))],
        out_shape=[jax.ShapeDtypeStruct((t, hp), F32), jax.ShapeDtypeStruct((t, hp), F32),
                   jax.ShapeDtypeStruct((t // n, hp, n), F32), jax.ShapeDtypeStruct((t // n, hp, n), F32)],
        compiler_params=_params("parallel"), name=name)(dt_raw, dt_bias8, a_log8)


def _expand_mat():
    h = lax.broadcasted_iota(jnp.int32, (HEAD_PAD, SSM_INNER), 0)
    ch = lax.broadcasted_iota(jnp.int32, (HEAD_PAD, SSM_INNER), 1)
    return (ch // SSM_HEADDIM == h).astype(F32)


def _reduce_mat():
    ch = lax.broadcasted_iota(jnp.int32, (SSM_INNER, HEAD_PAD), 0)
    h = lax.broadcasted_iota(jnp.int32, (SSM_INNER, HEAD_PAD), 1)
    return (ch // SSM_HEADDIM == h).astype(F32)


def _expand(v, em):
    return _dot_exact(v, em, _DN_NN, 0)


def _expand_heads(v):
    return jnp.repeat(v.astype(F32), SSM_HEADDIM)[None, :]


def _decay_mat(cs_ref, cst_ref, h, mask):
    seg = cs_ref[:, h:h + 1] - cst_ref[0, h:h + 1, :]
    return jnp.where(mask, jnp.exp(jnp.minimum(seg, 0.0)), 0.0)


GROUP_CH = SSM_INNER // SSM_GROUPS
PAIRS_PER_GROUP = GROUP_CH // LANE
HEADS_PER_GROUP = SSM_HEADS // SSM_GROUPS
BM_COL0 = SSM_INNER
CM_COL0 = SSM_INNER + SSM_GROUPS * SSM_STATE


def _ssd_specs(nc, rev):
    def cidx(i):
        return (i // nc) * nc + (nc - 1 - i % nc) if rev else i

    n = CHUNK
    xs = pl.BlockSpec((n, SSM_INNER), lambda i: (cidx(i), 0))
    bm = pl.BlockSpec((n, GROUP_CH), lambda i: (cidx(i), BM_COL0 // GROUP_CH))
    cm = pl.BlockSpec((n, GROUP_CH), lambda i: (cidx(i), CM_COL0 // GROUP_CH))
    hv = pl.BlockSpec((n, HEAD_PAD), lambda i: (cidx(i), 0))
    hvt = pl.BlockSpec((1, HEAD_PAD, n), lambda i: (cidx(i), 0, 0))
    st = pl.BlockSpec((1, SSM_INNER, SSM_STATE), lambda i: (cidx(i), 0, 0))
    return xs, bm, cm, hv, hvt, st


def _ssd_fwd(xbc, dt, cs, dtt, cst, dskx, *, nc, name):
    t = xbc.shape[0]
    n = CHUNK
    xs_s, bm_s, cm_s, hv_s, hvt_s, st_s = _ssd_specs(nc, False)

    def body(xs_ref, bm_ref, cm_ref, dt_ref, cs_ref, dtt_ref, cst_ref, dsk_ref, y_ref, st_ref, prev):
        @pl.when(pl.program_id(0) % nc == 0)
        def _():
            prev[...] = jnp.zeros_like(prev)

        st_ref[0] = prev[...]
        em = _expand_mat()
        dtx = _expand(dt_ref[...], em)
        csx = _expand(cs_ref[...], em)
        dskx = dsk_ref[...]
        xs = xs_ref[...].astype(F32)
        xdt = xs * dtx
        ecs = jnp.exp(csx)
        dec = jnp.exp(csx[n - 1:n, :] - csx)
        mask = _causal(n)
        lane = lax.broadcasted_iota(jnp.int32, (n, LANE), 1)
        for g in range(SSM_GROUPS):
            gs = slice(g * SSM_STATE, (g + 1) * SSM_STATE)
            gc = slice(g * GROUP_CH, (g + 1) * GROUP_CH)
            cmat = cm_ref[:, gs].astype(MXU_DTYPE)
            bmat = bm_ref[:, gs].astype(MXU_DTYPE)
            cb = _nt(cmat, bmat)
            yoff = ecs[:, gc] * _nt(cmat, prev[gc, :].astype(MXU_DTYPE))
            for q in range(PAIRS_PER_GROUP):
                hp = g * PAIRS_PER_GROUP + q
                sl = slice(hp * LANE, (hp + 1) * LANE)
                xp = xdt[:, sl].astype(MXU_DTYPE)
                m0 = (cb * _decay_mat(cs_ref, cst_ref, 2 * hp, mask)).astype(MXU_DTYPE)
                m1 = (cb * _decay_mat(cs_ref, cst_ref, 2 * hp + 1, mask)).astype(MXU_DTYPE)
                yd = jnp.where(lane < SSM_HEADDIM, _nn(m0, xp), _nn(m1, xp))
                y_ref[:, sl] = (yd + yoff[:, q * LANE:(q + 1) * LANE] + xs[:, sl] * dskx[:, sl]).astype(y_ref.dtype)
            snew = _tn((xdt[:, gc] * dec[:, gc]).astype(MXU_DTYPE), bmat)
            for r in range(HEADS_PER_GROUP):
                h = g * HEADS_PER_GROUP + r
                rows = slice(h * SSM_HEADDIM, (h + 1) * SSM_HEADDIM)
                e = jnp.exp(cst_ref[0, h:h + 1, n - 1:n])
                prev[rows, :] = prev[rows, :] * e + snew[r * SSM_HEADDIM:(r + 1) * SSM_HEADDIM, :]

    return pl.pallas_call(
        body, grid=(t // n,),
        in_specs=[xs_s, bm_s, cm_s, hv_s, hv_s, hvt_s, hvt_s, _par_spec((1, SSM_INNER))],
        out_specs=[xs_s, st_s],
        out_shape=[jax.ShapeDtypeStruct((t, SSM_INNER), STASH_DTYPE), jax.ShapeDtypeStruct((t // n, SSM_INNER, SSM_STATE), F32)],
        scratch_shapes=[pltpu.VMEM((SSM_INNER, SSM_STATE), F32)],
        compiler_params=_params("arbitrary"), name=name)(xbc, xbc, xbc, dt, cs, dtt, cst, dskx)


def _ssd_bwd(dy, xbc, dt, cs, dtt, cst, st, dskx, a_log8, dt_raw, dt_bias8, *, nc, name):
    t = xbc.shape[0]
    n = CHUNK
    xs_s, bm_s, cm_s, hv_s, hvt_s, st_s = _ssd_specs(nc, True)
    acc_s = _par_spec((1, HEAD_PAD))
    xbc_s = pl.BlockSpec((n, SSM_CONV_DIM), xs_s.index_map)

    def body(dy_ref, xs_ref, bm_ref, cm_ref, dt_ref, cs_ref, dtt_ref, cst_ref, st_ref, dsk_ref, al_ref, raw_ref, bias_ref,
             dxbc_ref, ddr_ref, dal_ref, dds_ref, dbias_ref, dprev, dxdt_s, tdec_s, tcs_s):
        @pl.when(pl.program_id(0) % nc == 0)
        def _():
            dprev[...] = jnp.zeros_like(dprev)

        @pl.when(pl.program_id(0) == 0)
        def _():
            dal_ref[...] = jnp.zeros_like(dal_ref)
            dds_ref[...] = jnp.zeros_like(dds_ref)
            dbias_ref[...] = jnp.zeros_like(dbias_ref)

        em = _expand_mat()
        rm = _reduce_mat()

        def head_reduce(v):
            return _dot_exact(v, rm, _DN_NN, 0)

        dtv = dt_ref[...]
        csv = cs_ref[...]
        dtx = _expand(dtv, em)
        csx = _expand(csv, em)
        dskx = dsk_ref[...]
        xs = xs_ref[...].astype(F32)
        dyv = dy_ref[...].astype(F32)
        xdt = xs * dtx
        ecs = jnp.exp(csx)
        dec = jnp.exp(csx[n - 1:n, :] - csx)
        mask = _causal(n)
        lane = lax.broadcasted_iota(jnp.int32, (n, LANE), 1)
        hlane = lax.broadcasted_iota(jnp.int32, (1, HEAD_PAD), 1)
        hsub = lax.broadcasted_iota(jnp.int32, (HEAD_PAD, 1), 0)
        rsum = jnp.zeros((n, HEAD_PAD), F32)
        csum = jnp.zeros((HEAD_PAD, n), F32)
        for g in range(SSM_GROUPS):
            gs = slice(g * SSM_STATE, (g + 1) * SSM_STATE)
            gc = slice(g * GROUP_CH, (g + 1) * GROUP_CH)
            cmat = cm_ref[:, gs].astype(MXU_DTYPE)
            bmat = bm_ref[:, gs].astype(MXU_DTYPE)
            cb = _nt(cmat, bmat)
            pg = st_ref[0, gc, :].astype(MXU_DTYPE)
            dpg = dprev[gc, :]
            dpgb = dpg.astype(MXU_DTYPE)
            z = _nt(cmat, pg)
            dyg = dyv[:, gc]
            dz = (dyg * ecs[:, gc]).astype(MXU_DTYPE)
            dc = _nn(dz, pg)
            dprev_y = _tn(dz, cmat)
            tcs_s[:, gc] = dyg * z * ecs[:, gc]
            xd = xdt[:, gc] * dec[:, gc]
            wmat = _nt(bmat, dpgb)
            db = _nn(xd.astype(MXU_DTYPE), dpgb)
            tdec_s[:, gc] = wmat * xd
            dxdt_g = wmat * dec[:, gc]
            dcb = jnp.zeros((n, n), F32)
            for q in range(PAIRS_PER_GROUP):
                hp = g * PAIRS_PER_GROUP + q
                sl = slice(hp * LANE, (hp + 1) * LANE)
                xp = xdt[:, sl].astype(MXU_DTYPE)
                dyp = dyv[:, sl]
                dypb = dyp.astype(MXU_DTYPE)
                dxp = None
                for hh in range(2):
                    h = 2 * hp + hh
                    lm = _decay_mat(cs_ref, cst_ref, h, mask)
                    mine = (lane < SSM_HEADDIM) if hh == 0 else (lane >= SSM_HEADDIM)
                    dm = _nt(jnp.where(mine, dyp, 0.0).astype(MXU_DTYPE), xp)
                    dml = dm * lm
                    dcb = dcb + dml
                    gseg = dml * cb
                    rsum = rsum + jnp.sum(gseg, axis=1, keepdims=True) * (hlane == h).astype(F32)
                    csum = csum + (hsub == h).astype(F32) * jnp.sum(gseg, axis=0, keepdims=True)
                    dxh = _tn((cb * lm).astype(MXU_DTYPE), dypb)
                    dxp = dxh if dxp is None else jnp.where(mine, dxh, dxp)
                dxdt_s[:, sl] = dxdt_g[:, q * LANE:(q + 1) * LANE] + dxp
            dcbb = dcb.astype(MXU_DTYPE)
            dxbc_ref[:, CM_COL0 + g * SSM_STATE:CM_COL0 + (g + 1) * SSM_STATE] = (dc + _nn(dcbb, bmat)).astype(dxbc_ref.dtype)
            dxbc_ref[:, BM_COL0 + g * SSM_STATE:BM_COL0 + (g + 1) * SSM_STATE] = (db + _tn(dcbb, cmat)).astype(dxbc_ref.dtype)
            for r in range(HEADS_PER_GROUP):
                h = g * HEADS_PER_GROUP + r
                rows = slice(h * SSM_HEADDIM, (h + 1) * SSM_HEADDIM)
                lr = slice(r * SSM_HEADDIM, (r + 1) * SSM_HEADDIM)
                e = jnp.exp(cst_ref[0, h:h + 1, n - 1:n])
                dprev[rows, :] = dpg[lr, :] * e + dprev_y[lr, :]
            tq = _dot_exact(dpg * st_ref[0, gc, :], rm[gc, :], _DN_TN, 0)
            if g == 0:
                qsum = jnp.sum(tq, axis=0, keepdims=True)
            else:
                qsum = qsum + jnp.sum(tq, axis=0, keepdims=True)
        dxdt = dxdt_s[...]
        dxbc_ref[:, 0:SSM_INNER] = (dxdt * dtx + dyv * dskx).astype(dxbc_ref.dtype)
        ddt = head_reduce(dxdt * xs)
        edec = head_reduce(tdec_s[...])
        ycs = head_reduce(tcs_s[...])
        row = lax.broadcasted_iota(jnp.int32, (n, HEAD_PAD), 0)
        extra = jnp.sum(edec, axis=0, keepdims=True) + qsum * jnp.exp(csv[n - 1:n, :])
        dcs = rsum - csum.T + ycs - edec + jnp.where(row == n - 1, extra, 0.0)
        r2 = lax.broadcasted_iota(jnp.int32, (n, n), 0)
        c2 = lax.broadcasted_iota(jnp.int32, (n, n), 1)
        dda = _dot_exact((c2 >= r2).astype(F32), dcs, _DN_NN, 1)
        a_row = -jnp.exp(al_ref[0:1, :])
        ddt = ddt + dda * a_row
        dal_ref[...] += jnp.sum(dda * dtv, axis=0, keepdims=True) * a_row
        dds_ref[...] += jnp.sum(head_reduce(dyv * xs), axis=0, keepdims=True)
        ddr = ddt * _sigmoid(raw_ref[...] + bias_ref[0:1, :])
        ddr_ref[...] = ddr
        dbias_ref[...] += jnp.sum(ddr, axis=0, keepdims=True)

    par8 = _par_spec((SUBLANE, HEAD_PAD))
    return pl.pallas_call(
        body, grid=(t // n,),
        in_specs=[xs_s, xs_s, bm_s, cm_s, hv_s, hv_s, hvt_s, hvt_s, st_s, _par_spec((1, SSM_INNER)), par8, hv_s, par8],
        out_specs=[xbc_s, hv_s, acc_s, acc_s, acc_s],
        out_shape=[jax.ShapeDtypeStruct((t, SSM_CONV_DIM), STASH_DTYPE), jax.ShapeDtypeStruct((t, HEAD_PAD), F32),
                   jax.ShapeDtypeStruct((1, HEAD_PAD), F32), jax.ShapeDtypeStruct((1, HEAD_PAD), F32),
                   jax.ShapeDtypeStruct((1, HEAD_PAD), F32)],
        scratch_shapes=[pltpu.VMEM((SSM_INNER, SSM_STATE), F32), pltpu.VMEM((n, SSM_INNER), F32),
                        pltpu.VMEM((n, SSM_INNER), F32), pltpu.VMEM((n, SSM_INNER), F32)],
        compiler_params=_params("arbitrary"), name=name)(dy, xbc, xbc, xbc, dt, cs, dtt, cst, st, dskx, a_log8, dt_raw, dt_bias8)


def _gate_norm_fwd(y, proj, norm_g, *, name):
    t, c = y.shape
    tm = _pick(t, (256, 128))

    def body(y_ref, z_ref, g_ref, o_ref):
        z = z_ref[...].astype(F32)
        yz = y_ref[...].astype(F32) * z * _sigmoid(z)
        for g in range(SSM_GROUPS):
            gc = slice(g * GROUP_CH, (g + 1) * GROUP_CH)
            seg = yz[:, gc]
            r = lax.rsqrt(jnp.mean(seg * seg, axis=-1, keepdims=True) + RMS_EPS)
            o_ref[:, gc] = (seg * r * g_ref[:, gc]).astype(MXU_DTYPE)

    return pl.pallas_call(
        body, grid=(t // tm,), in_specs=[_row_spec(tm, c), _row_spec(tm, c, 1), _par_spec((1, c))],
        out_specs=_row_spec(tm, c), out_shape=jax.ShapeDtypeStruct((t, c), MXU_DTYPE),
        compiler_params=_params("parallel"), name=name)(y, proj, norm_g.reshape(1, c))


def _gate_norm_bwd(dyb, y, proj, norm_g, dproj, *, name):
    t, c = y.shape
    tm = _pick(t, (256, 128))

    def body(d_ref, y_ref, z_ref, g_ref, _, dy_ref, dz_ref, dg_ref):
        @pl.when(pl.program_id(0) == 0)
        def _():
            dg_ref[...] = jnp.zeros_like(dg_ref)

        z = z_ref[...].astype(F32)
        yv = y_ref[...].astype(F32)
        sz = _sigmoid(z)
        silu = z * sz
        yz = yv * silu
        dv = d_ref[...].astype(F32)
        for g in range(SSM_GROUPS):
            gc = slice(g * GROUP_CH, (g + 1) * GROUP_CH)
            seg = yz[:, gc]
            r = lax.rsqrt(jnp.mean(seg * seg, axis=-1, keepdims=True) + RMS_EPS)
            nrm = seg * r
            dn = dv[:, gc] * g_ref[:, gc]
            dg_ref[:, gc] += jnp.sum(dv[:, gc] * nrm, axis=0, keepdims=True)
            dyz = r * (dn - nrm * jnp.mean(dn * nrm, axis=-1, keepdims=True))
            dy_ref[:, gc] = (dyz * silu[:, gc]).astype(dy_ref.dtype)
            dz_ref[:, gc] = (dyz * yv[:, gc] * (sz[:, gc] * (1.0 + z[:, gc] * (1.0 - sz[:, gc])))).astype(MXU_DTYPE)

    return pl.pallas_call(
        body, grid=(t // tm,), in_specs=[_row_spec(tm, c), _row_spec(tm, c), _row_spec(tm, c, 1), _par_spec((1, c)), _ANY],
        out_specs=[_row_spec(tm, c), _row_spec(tm, c, 1), _par_spec((1, c))],
        out_shape=[jax.ShapeDtypeStruct((t, c), STASH_DTYPE), jax.ShapeDtypeStruct(dproj.shape, dproj.dtype),
                   jax.ShapeDtypeStruct((1, c), F32)],
        input_output_aliases={4: 1},
        compiler_params=_params("arbitrary"), name=name)(dyb, y, proj, norm_g.reshape(1, c), dproj)


GA_COLBLK = GAB_COL0 // D_MODEL


def _merge_fwd(br_a, br_b, proj, *, name):
    t, c = br_a.shape
    tm = _pick(t, ROW_TILES)

    def body(a_ref, b_ref, ga_ref, gb_ref, o_ref):
        o_ref[...] = (_sigmoid(ga_ref[...].astype(F32)) * a_ref[...].astype(F32)
                      + _sigmoid(gb_ref[...].astype(F32)) * b_ref[...].astype(F32)).astype(MXU_DTYPE)

    return pl.pallas_call(
        body, grid=(t // tm,),
        in_specs=[_row_spec(tm, c), _row_spec(tm, c), _row_spec(tm, c, GA_COLBLK), _row_spec(tm, c, GA_COLBLK + 1)],
        out_specs=_row_spec(tm, c), out_shape=jax.ShapeDtypeStruct((t, c), MXU_DTYPE),
        compiler_params=_params("parallel"), name=name)(br_a, br_b, proj, proj)


def _merge_bwd(dm, br_a, br_b, proj, *, name):
    t, c = br_a.shape
    tm = _pick(t, ROW_TILES)

    def body(dm_ref, a_ref, b_ref, ga_ref, gb_ref, da_ref, db_ref, dg_ref):
        d = dm_ref[...].astype(F32)
        sa = _sigmoid(ga_ref[...].astype(F32))
        sb = _sigmoid(gb_ref[...].astype(F32))
        da_ref[...] = (d * sa).astype(MXU_DTYPE)
        db_ref[...] = (d * sb).astype(MXU_DTYPE)
        dg_ref[:, :c] = (d * a_ref[...].astype(F32) * sa * (1.0 - sa)).astype(MXU_DTYPE)
        dg_ref[:, c:] = (d * b_ref[...].astype(F32) * sb * (1.0 - sb)).astype(MXU_DTYPE)

    return pl.pallas_call(
        body, grid=(t // tm,),
        in_specs=[_row_spec(tm, c), _row_spec(tm, c), _row_spec(tm, c), _row_spec(tm, c, GA_COLBLK), _row_spec(tm, c, GA_COLBLK + 1)],
        out_specs=[_row_spec(tm, c), _row_spec(tm, c), _row_spec(tm, 2 * c, GAB_COL0 // (2 * c))],
        out_shape=[jax.ShapeDtypeStruct((t, c), MXU_DTYPE), jax.ShapeDtypeStruct((t, c), MXU_DTYPE),
                   jax.ShapeDtypeStruct((t, MAIN_COLS), MXU_DTYPE)],
        compiler_params=_params("parallel"), name=name)(dm, br_a, br_b, proj, proj)


def _layer_fwd(x, xb, memn_b, w, *, bsz, tag):
    nc = x.shape[0] // bsz // CHUNK
    sv = {"x_in": xb}
    proj = _mm(xb, w["w_main"], out_dtype=STASH_DTYPE, name=f"{tag}_proj")
    dt_raw = _mm(xb, w["w_dt"], name=f"{tag}_dtproj")
    sgo = _sg_fwd(proj, w["sg_ln_g"], w["sg_ln_b"], w["sg_w"], w["sg_bcol"], name=f"{tag}_sg_fwd")
    xbc = _conv_fwd(proj, w["conv_w"], w["conv_b"], bsz=bsz, name=f"{tag}_conv_fwd")
    dt, cs, dtt, cst = _ssd_prep(dt_raw, w["dt_bias8"], w["a_log8"], name=f"{tag}_ssd_prep")
    y, st = _ssd_fwd(xbc, dt, cs, dtt, cst, w["d_skipx"], nc=nc, name=f"{tag}_ssd_fwd")
    yb = _gate_norm_fwd(y, proj, w["ssm_norm_g"], name=f"{tag}_gate_norm_fwd")
    if "rest" in w:
        w = w["rest"](w, yb)
    br_a = _mm(sgo, w["p_a"], out_dtype=STASH_DTYPE, name=f"{tag}_br_a")
    br_b = _mm(yb, w["p_b"], out_dtype=STASH_DTYPE, name=f"{tag}_br_b")
    merged = _merge_fwd(br_a, br_b, proj, name=f"{tag}_merge_fwd")
    mix = _mm(merged, w["w_mix_o"], name=f"{tag}_mix_o")
    x1, x1b, xh1, rs1 = _ln_fwd(x, mix, w["ln_g"][0], w["ln_b"][0], name=f"{tag}_ln1_fwd")
    sv.update(proj=proj, dt_raw=dt_raw, sgo=sgo, xbc=xbc, dt=dt, cs=cs, dtt=dtt, cst=cst, y=y, st=st, yb=yb,
              br_a=br_a, br_b=br_b, merged=merged, xh1=xh1, rs1=rs1, x1b=x1b)
    q = _mm(x1b, w["w_xq"], out_dtype=MXU_DTYPE, name=f"{tag}_q")
    kv = _mm(memn_b, w["w_xkv"], out_dtype=MXU_DTYPE, name=f"{tag}_kv")
    o = _attn_fwd(q, kv, bsz=bsz, name=f"{tag}_attn_fwd")
    att = _mm(o, w["w_xo"], name=f"{tag}_xo")
    x2, x2b, xh2, rs2 = _ln_fwd(x1, att, w["ln_g"][1], w["ln_b"][1], name=f"{tag}_ln2_fwd")
    sv.update(q=q, kv=kv, o=o, xh2=xh2, rs2=rs2, x2b=x2b)
    h = _mm(x2b, w["w_ffn_in"], out_dtype=STASH_DTYPE, name=f"{tag}_ffn_in")
    a = _swiglu_fwd(h, name=f"{tag}_swiglu_fwd")
    ffn = _mm(a, w["w_ffn_out"], name=f"{tag}_ffn_out")
    x3, x3b, xh3, rs3 = _ln_fwd(x2, ffn, w["ln_g"][2], w["ln_b"][2], name=f"{tag}_ln3_fwd")
    sv.update(h=h, a=a, xh3=xh3, rs3=rs3)
    return x3, x3b, sv, w


GRAD_GROUPS = (("w_ffn_out", "w_ffn_in", "w_xo", "w_xq", "w_xkv"), ("w_mix_o", "p_a", "p_b"), ("w_in",))


def _layer_bwd(dx3_addends, dx3_scales, memn_b, w, sv, on_group=None, *, bsz, tag):
    nc = sv["xh1"].shape[0] // bsz // CHUNK
    gr = {}

    def group_done(k):
        return on_group(GRAD_GROUPS[k], gr) if on_group is not None else None
    dp3, dp3b, dg3, db3 = _ln_bwd(dx3_addends, dx3_scales, sv["xh3"], sv["rs3"], w["ln_g"][2], name=f"{tag}_ln3_bwd")
    da = _mm(dp3b, w["w_ffn_out"], tb=True, out_dtype=STASH_DTYPE, name=f"{tag}_d_a")
    gr["w_ffn_out"] = _mm(sv["a"], dp3b, ta=True, name=f"{tag}_dw_ffn_out")
    dh = _swiglu_bwd(sv["h"], da, name=f"{tag}_swiglu_bwd")
    gr["w_ffn_in"] = _mm(sv["x2b"], dh, ta=True, name=f"{tag}_dw_ffn_in")
    dx2_br = _mm(dh, w["w_ffn_in"], tb=True, name=f"{tag}_dx2")
    dp2, dp2b, dg2, db2 = _ln_bwd([dp3, dx2_br], [ALPHA, 1.0], sv["xh2"], sv["rs2"], w["ln_g"][1], name=f"{tag}_ln2_bwd")
    do = _mm(dp2b, w["w_xo"], tb=True, out_dtype=MXU_DTYPE, name=f"{tag}_d_o")
    gr["w_xo"] = _mm(sv["o"], dp2b, ta=True, name=f"{tag}_dw_xo")
    dq, dk, dv = _attn_bwd(sv["q"], sv["kv"], do, bsz=bsz, name=f"{tag}_attn_bwd")
    dkv = jnp.concatenate([dk, dv], axis=1)
    gr["w_xq"] = _mm(sv["x1b"], dq, ta=True, name=f"{tag}_dw_xq")
    gr["w_xkv"] = _mm(memn_b, dkv, ta=True, name=f"{tag}_dw_xkv")
    dmemn = _mm(dkv, w["w_xkv"], tb=True, name=f"{tag}_d_memn")
    dx1_br = _mm(dq, w["w_xq"], tb=True, name=f"{tag}_dx1")
    token = group_done(0)
    ln_g1 = w["ln_g"][0] if token is None else w["ln_g"][0] + token[0, 0]
    dp1, dp1b, dg1, db1 = _ln_bwd([dp2, dx1_br], [ALPHA, 1.0], sv["xh1"], sv["rs1"], ln_g1, name=f"{tag}_ln1_bwd")
    gr["ln_g"] = jnp.concatenate([dg1, dg2, dg3], axis=0)
    gr["ln_b"] = jnp.concatenate([db1, db2, db3], axis=0)
    dmerged = _mm(dp1b, w["w_mix_o"], tb=True, out_dtype=STASH_DTYPE, name=f"{tag}_d_merged")
    gr["w_mix_o"] = _mm(sv["merged"], dp1b, ta=True, name=f"{tag}_dw_mix_o")
    dbr_a, dbr_b, dproj = _merge_bwd(dmerged, sv["br_a"], sv["br_b"], sv["proj"], name=f"{tag}_merge_bwd")
    gr["p_a"] = _mm(sv["sgo"], dbr_a, ta=True, name=f"{tag}_dw_p_a")
    gr["p_b"] = _mm(sv["yb"], dbr_b, ta=True, name=f"{tag}_dw_p_b")
    dsgo = _mm(dbr_a, w["p_a"], tb=True, out_dtype=STASH_DTYPE, name=f"{tag}_d_sgo")
    dyb = _mm(dbr_b, w["p_b"], tb=True, out_dtype=STASH_DTYPE, name=f"{tag}_d_yb")
    token = group_done(1)
    norm_g = w["ssm_norm_g"] if token is None else w["ssm_norm_g"] + token[0, 0]
    dy, dproj, gr["ssm_norm_g"] = _gate_norm_bwd(dyb, sv["y"], sv["proj"], norm_g, dproj, name=f"{tag}_gate_norm_bwd")
    dxbc, ddr, gr["a_log"], gr["d_skip"], gr["dt_bias"] = _ssd_bwd(
        dy, sv["xbc"], sv["dt"], sv["cs"], sv["dtt"], sv["cst"], sv["st"], w["d_skipx"], w["a_log8"], sv["dt_raw"],
        w["dt_bias8"], nc=nc, name=f"{tag}_ssd_bwd")
    dproj, gr["conv_w"], gr["conv_b"] = _conv_bwd(sv["proj"], dxbc, w["conv_w"], w["conv_b"], dproj, bsz=bsz, name=f"{tag}_conv_bwd")
    dproj, gr["sg_w"], dsg_bcol, gr["sg_ln_g"], gr["sg_ln_b"] = _sg_bwd(
        sv["proj"], dsgo, w["sg_ln_g"], w["sg_ln_b"], w["sg_w"], w["sg_bcol"], dproj, name=f"{tag}_sg_bwd")
    gr["sg_b"] = dsg_bcol[..., 0]
    gr["w_main"] = _mm(sv["x_in"], dproj, ta=True, name=f"{tag}_dw_main")
    gr["w_dt"] = _mm(sv["x_in"], ddr, ta=True, name=f"{tag}_dw_dt")
    token = group_done(2)
    dx_dt = _mm(ddr, w["w_dt"], tb=True, after=token, name=f"{tag}_dx_dt")
    dx_main = _mm(dproj, w["w_main"], tb=True, after=token, name=f"{tag}_dx_main")
    return [dp1, dx_main, dx_dt], [ALPHA, 1.0, 1.0], gr, dmemn


def _local_step(x, mem, tgt, mem_ln_g, mem_ln_b, layers, on_layer_grads=None):
    bsz, s, d = x.shape
    xf = x.reshape(bsz * s, d)
    memf = mem.reshape(-1, d)
    _, memn_b, mxh, mrs = _ln_fwd(memf, None, mem_ln_g, mem_ln_b, name="mem_ln_fwd")
    cur, curb, saved, weights = xf, xf, [], []
    for li, get_weights in enumerate(layers):
        cur, curb, sv, w = _layer_fwd(cur, curb, memn_b, get_weights(cur), bsz=bsz, tag=f"l{li}")
        saved.append(sv)
        weights.append(w)
    dy, lsum = _loss_head(cur, tgt.reshape(bsz * s, d), name="loss_head")
    addends, scales = [dy], [1.0]
    grads, dmem = [None] * len(layers), []
    for li in reversed(range(len(layers))):
        on_group = None if on_layer_grads is None else functools.partial(on_layer_grads, li)
        addends, scales, grads[li], dm = _layer_bwd(addends, scales, memn_b, weights[li], saved[li], on_group, bsz=bsz, tag=f"l{li}")
        dmem.append(dm)
    grad_x = _add_scaled(addends, scales, name="grad_x").reshape(bsz, s, d)
    _, _, dmg, dmb = _ln_bwd(dmem, [1.0] * len(dmem), mxh, mrs, mem_ln_g, name="mem_ln_bwd")
    return lsum, grad_x, grads, dmg[0], dmb[0]


_ANY = pl.BlockSpec(memory_space=pl.ANY)
_MESH = pl.DeviceIdType.MESH


def _all_gather8(x, *, name):
    def body(x_ref, out_ref, send_sems, recv_sems):
        mx, my, mc = lax.axis_index("x"), lax.axis_index("y"), lax.axis_index("c")
        me, sibling = (mx, my, mc), (mx, my, 1 - mc)
        chips = [(1 - mx, my), (mx, 1 - my), (1 - mx, 1 - my)]

        def blk(px, py, pc):
            return out_ref.at[4 * px + 2 * py + pc]

        def copy(k, block, to, src=None):
            return pltpu.make_async_remote_copy(
                src_ref=blk(*block) if src is None else src, dst_ref=blk(*block), send_sem=send_sems.at[k],
                recv_sem=recv_sems.at[k], device_id=to, device_id_type=_MESH)

        first = [copy(0, me, sibling, src=x_ref)]
        first += [copy(1 + j, me, (*chip, mc), src=x_ref) for j, chip in enumerate(chips)]
        for cp in first:
            cp.start()
        passed = [copy(4 + j, (*chip, mc), sibling) for j, chip in enumerate(chips)]
        for j, chip in enumerate(chips):
            copy(1 + j, (*chip, mc), me).wait_recv()
            passed[j].start()
        copy(0, sibling, me).wait_recv()
        for j, chip in enumerate(chips):
            copy(4 + j, (*chip, 1 - mc), me).wait_recv()
        for cp in first + passed:
            cp.wait_send()

    return pl.pallas_call(
        body, out_shape=jax.ShapeDtypeStruct((N_DEV,) + x.shape, x.dtype), in_specs=[_ANY], out_specs=_ANY,
        scratch_shapes=[pltpu.SemaphoreType.DMA((7,)), pltpu.SemaphoreType.DMA((7,))], name=name)(x)


def _row_tile(rows, row_bytes, mult=SUBLANE):
    best = None
    for tr in range(mult, rows + 1, mult):
        if rows % tr == 0 and (best is None or tr * row_bytes <= BLOCK_BYTES):
            best = tr
    return rows if best is None else best


def _gather_shape(r, c, kind):
    return {"row": (2, N_CHIPS * r, c), "col": (2, r, N_CHIPS * c), "chip": (2, N_CHIPS, r, c)}[kind]


def _cast_place(shard, kind, dtype, chip_idx, *, name):
    _, r, c = shard.shape
    tr = _row_tile(r, c * 4, 16)
    nt = r // tr

    def body(_, s_ref, o_ref):
        o_ref[...] = s_ref[...].astype(dtype)

    if kind == "row":
        out_spec = pl.BlockSpec((None, tr, c), lambda l, i, j_ref: (l, j_ref[0] * nt + i, 0))
    elif kind == "col":
        out_spec = pl.BlockSpec((None, tr, c), lambda l, i, j_ref: (l, i, j_ref[0]))
    else:
        out_spec = pl.BlockSpec((None, None, tr, c), lambda l, i, j_ref: (l, j_ref[0], i, 0))
    grid_spec = pltpu.PrefetchScalarGridSpec(
        num_scalar_prefetch=1, grid=(2, nt), in_specs=[pl.BlockSpec((None, tr, c), lambda l, i, j_ref: (l, i, 0))],
        out_specs=out_spec)
    return pl.pallas_call(body, grid_spec=grid_spec, out_shape=jax.ShapeDtypeStruct(_gather_shape(r, c, kind), dtype),
                          compiler_params=_params("parallel", "parallel"), name=name)(chip_idx, shard)


def _gather_params(bufs, shard_shapes, kinds, *, name):
    n = len(bufs)

    def body(*refs):
        outs = refs[n:2 * n]
        send_sems, recv_sems = refs[2 * n:]
        mx, my, mc = lax.axis_index("x"), lax.axis_index("y"), lax.axis_index("c")
        me, sibling = (mx, my, mc), (mx, my, 1 - mc)
        chips = [(1 - mx, my), (mx, 1 - my), (1 - mx, 1 - my)]

        def blk(i, px, py, pc):
            r, c = shard_shapes[i]
            j = 2 * px + py
            if kinds[i] == "row":
                return outs[i].at[pc, pl.ds(pl.multiple_of(j * r, r), r)]
            if kinds[i] == "col":
                return outs[i].at[pc, :, pl.ds(pl.multiple_of(j * c, c), c)]
            return outs[i].at[pc, j]

        def copy(i, k, block, to):
            return pltpu.make_async_remote_copy(
                src_ref=blk(i, *block), dst_ref=blk(i, *block), send_sem=send_sems.at[6 * i + k],
                recv_sem=recv_sems.at[6 * i + k], device_id=to, device_id_type=_MESH)

        sent = []
        for i in range(n):
            for j, chip in enumerate(chips):
                cp = copy(i, j, me, (*chip, mc))
                cp.start()
                sent.append(cp)
        for j, chip in enumerate(chips):
            for i in range(n):
                copy(i, j, (*chip, mc), me).wait_recv()
                fwd = copy(i, 3 + j, (*chip, mc), sibling)
                fwd.start()
                sent.append(fwd)
        for i in range(n):
            for j, chip in enumerate(chips):
                copy(i, 3 + j, (*chip, 1 - mc), me).wait_recv()
        for cp in sent:
            cp.wait_send()

    return pl.pallas_call(
        body, out_shape=[jax.ShapeDtypeStruct(b.shape, b.dtype) for b in bufs], in_specs=[_ANY] * n, out_specs=[_ANY] * n,
        input_output_aliases={i: i for i in range(n)},
        scratch_shapes=[pltpu.SemaphoreType.DMA((6 * n,)), pltpu.SemaphoreType.DMA((6 * n,))], name=name)(*bufs)


def _half(r, h):
    return pl.ds(pl.multiple_of(h * (r // 2), r // 2), r // 2)


_HBM = pl.BlockSpec(memory_space=pltpu.HBM)
_SEM = pl.BlockSpec(memory_space=pltpu.SEMAPHORE)
_EFFECT = pltpu.SideEffectType.DATAFLOW_SIDE_EFFECTING


def _sibling_copies(g_refs, land_refs, gs, views, send_sems, recv_sems):
    mx, my, mc = lax.axis_index("x"), lax.axis_index("y"), lax.axis_index("c")
    copies = []
    for i in range(len(gs)):
        if views[i] == "chip":
            src = g_refs[i].at[:, _half(gs[i].shape[1], 1 - mc)]
        else:
            src = g_refs[i].at[_half(gs[i].shape[0], 1 - mc)]
        copies.append(pltpu.make_async_remote_copy(src_ref=src, dst_ref=land_refs[i], send_sem=send_sems.at[i], recv_sem=recv_sems.at[i],
                                                   device_id=(mx, my, 1 - mc), device_id_type=_MESH))
    return copies


def _half_shape(g, view):
    return (g.shape[0], g.shape[1] // 2, g.shape[2]) if view == "chip" else (g.shape[0] // 2, g.shape[1])


def _grads_to_sibling_start(gs, views, *, name):
    n = len(gs)
    lands = [pltpu.with_memory_space_constraint(lax.empty(_half_shape(g, v), g.dtype), pltpu.HBM) for g, v in zip(gs, views)]

    def body(*refs):
        for cp in _sibling_copies(refs[:n], refs[n:2 * n], gs, views, refs[2 * n], refs[2 * n + 1]):
            cp.start()
        refs[-1][...] = jnp.zeros_like(refs[-1])

    outs = pl.pallas_call(
        body, name=name,
        out_shape=(pltpu.SemaphoreType.DMA((n,)), pltpu.SemaphoreType.DMA((n,)),
                   *[pltpu.HBM(x.shape, x.dtype) for x in list(gs) + lands], jax.ShapeDtypeStruct((SUBLANE, LANE), F32)),
        in_specs=[_HBM] * (2 * n), out_specs=(_SEM, _SEM, *[_HBM] * (2 * n), pl.BlockSpec(memory_space=pltpu.VMEM)),
        input_output_aliases={i: 2 + i for i in range(2 * n)},
        compiler_params=pltpu.CompilerParams(has_side_effects=_EFFECT),
    )(*[pltpu.with_memory_space_constraint(g, pltpu.HBM) for g in gs], *lands)
    return outs[0], outs[1], list(outs[2:2 + n]), list(outs[2 + n:2 + 2 * n]), outs[-1]


def _grads_to_sibling_wait(send_sems, recv_sems, gs, lands, views, after, *, name):
    n = len(gs)

    def body(*refs):
        for cp in _sibling_copies(refs[:n], refs[n:2 * n], gs, views, refs[2 * n], refs[2 * n + 1]):
            cp.wait_send()
            cp.wait_recv()

    outs = pl.pallas_call(
        body, name=name, out_shape=tuple(pltpu.HBM(x.shape, x.dtype) for x in list(gs) + list(lands)),
        in_specs=[_HBM] * (2 * n) + [_SEM, _SEM, _ANY], out_specs=tuple([_HBM] * (2 * n)),
        input_output_aliases={i: i for i in range(2 * n)},
        compiler_params=pltpu.CompilerParams(has_side_effects=_EFFECT),
    )(*gs, *lands, send_sems, recv_sems, after)
    return list(outs[:n]), list(outs[n:])


def _cast_place_layer(shard, l, kind, chip_idx, after, *, name):
    _, r, c = shard.shape
    tr = _row_tile(r, c * 4, 16)
    nt = r // tr

    def body(_, s_ref, *rest):
        rest[-1][...] = s_ref[...].astype(MXU_DTYPE)

    if kind == "row":
        out_spec = pl.BlockSpec((tr, c), lambda i, j_ref: (j_ref[0] * nt + i, 0))
    elif kind == "col":
        out_spec = pl.BlockSpec((tr, c), lambda i, j_ref: (i, j_ref[0]))
    else:
        out_spec = pl.BlockSpec((None, tr, c), lambda i, j_ref: (j_ref[0], i, 0))
    extra = [] if after is None else [after]
    grid_spec = pltpu.PrefetchScalarGridSpec(
        num_scalar_prefetch=1, grid=(nt,), in_specs=[pl.BlockSpec((None, tr, c), lambda i, j_ref: (l, i, 0))] + [_ANY] * len(extra),
        out_specs=out_spec)
    return pl.pallas_call(body, grid_spec=grid_spec, out_shape=jax.ShapeDtypeStruct(_gather_shape(r, c, kind)[1:], MXU_DTYPE),
                          compiler_params=_params("parallel"), name=name)(chip_idx, shard, *extra)


def _half_block(ref, kind, r, c, j, h):
    rows = _half(r, h)
    if kind == "row":
        return ref.at[pl.ds(pl.multiple_of(j * r + h * (r // 2), r // 2), r // 2)]
    if kind == "col":
        return ref.at[rows, pl.ds(pl.multiple_of(j * c, c), c)]
    return ref.at[j, rows]


def _gather_ici_copies(buf_refs, shapes, kinds, send_sems, recv_sems):
    mx, my, mc = lax.axis_index("x"), lax.axis_index("y"), lax.axis_index("c")
    chips = [(1 - mx, my), (mx, 1 - my), (1 - mx, 1 - my)]
    copies = []
    for i, (r, c) in enumerate(shapes):
        mine = _half_block(buf_refs[i], kinds[i], r, c, 2 * mx + my, mc)
        for k, (px, py) in enumerate(chips):
            copies.append(pltpu.make_async_remote_copy(
                src_ref=mine, dst_ref=mine, send_sem=send_sems.at[3 * i + k], recv_sem=recv_sems.at[3 * i + k],
                device_id=(px, py, mc), device_id_type=_MESH))
    return copies


def _gather_start(bufs, shapes, kinds, *, name):
    n = len(bufs)

    def body(*refs):
        send_sems, recv_sems, token = refs[n], refs[n + 1], refs[-1]
        for cp in _gather_ici_copies(refs[:n], shapes, kinds, send_sems, recv_sems):
            cp.start()
        token[...] = jnp.zeros_like(token)

    outs = pl.pallas_call(
        body, name=name,
        out_shape=(pltpu.SemaphoreType.DMA((3 * n,)), pltpu.SemaphoreType.DMA((3 * n,)),
                   *[pltpu.HBM(b.shape, b.dtype) for b in bufs], jax.ShapeDtypeStruct((SUBLANE, LANE), F32)),
        in_specs=[_HBM] * n, out_specs=(_SEM, _SEM, *[_HBM] * n, pl.BlockSpec(memory_space=pltpu.VMEM)),
        input_output_aliases={i: 2 + i for i in range(n)},
        compiler_params=pltpu.CompilerParams(has_side_effects=_EFFECT),
    )(*[pltpu.with_memory_space_constraint(b, pltpu.HBM) for b in bufs])
    return outs[0], outs[1], list(outs[2:2 + n]), outs[-1]


def _gather_wait(send_sems, recv_sems, bufs, shapes, kinds, after, *, name):
    n = len(bufs)

    def body(*refs):
        for cp in _gather_ici_copies(refs[:n], shapes, kinds, refs[n], refs[n + 1]):
            cp.wait_send()
            cp.wait_recv()

    outs = pl.pallas_call(
        body, name=name, out_shape=tuple(pltpu.HBM(b.shape, b.dtype) for b in bufs),
        in_specs=[_HBM] * n + [_SEM, _SEM, _ANY], out_specs=tuple([_HBM] * n), input_output_aliases={i: i for i in range(n)},
        compiler_params=pltpu.CompilerParams(has_side_effects=_EFFECT),
    )(*bufs, send_sems, recv_sems, after)
    return list(outs)


def _gather_forward(bufs, shapes, kinds, *, name):
    n = len(bufs)

    def body(*refs):
        outs = refs[n:2 * n]
        send_sems, recv_sems = refs[2 * n:]
        mx, my, mc = lax.axis_index("x"), lax.axis_index("y"), lax.axis_index("c")
        chips = [(1 - mx, my), (mx, 1 - my), (1 - mx, 1 - my)]
        copies = []
        for i, (r, c) in enumerate(shapes):
            for k, (px, py) in enumerate(chips):
                got = _half_block(outs[i], kinds[i], r, c, 2 * px + py, mc)
                cp = pltpu.make_async_remote_copy(src_ref=got, dst_ref=got, send_sem=send_sems.at[3 * i + k],
                                                  recv_sem=recv_sems.at[3 * i + k], device_id=(mx, my, 1 - mc), device_id_type=_MESH)
                cp.start()
                copies.append(cp)
        for cp in copies:
            cp.wait()

    return pl.pallas_call(
        body, out_shape=[jax.ShapeDtypeStruct(b.shape, b.dtype) for b in bufs], in_specs=[_ANY] * n, out_specs=[_ANY] * n,
        input_output_aliases={i: i for i in range(n)},
        scratch_shapes=[pltpu.SemaphoreType.DMA((3 * n,)), pltpu.SemaphoreType.DMA((3 * n,))], name=name)(*bufs)


def _chip_exchange_copies(pair_refs, land_refs, pairs, views, send_sems, recv_sems):
    mx, my, mc = lax.axis_index("x"), lax.axis_index("y"), lax.axis_index("c")
    me = 2 * mx + my
    chips = [(1 - mx, my), (mx, 1 - my), (1 - mx, 1 - my)]
    copies = []
    for i in range(len(pairs)):
        for k, (px, py) in enumerate(chips):
            j = 2 * px + py
            if views[i] == "chip":
                src = pair_refs[i].at[j]
            else:
                c = pairs[i].shape[1] // N_CHIPS
                src = pair_refs[i].at[:, pl.ds(pl.multiple_of(j * c, c), c)]
            copies.append(pltpu.make_async_remote_copy(
                src_ref=src, dst_ref=land_refs[i].at[me], send_sem=send_sems.at[3 * i + k], recv_sem=recv_sems.at[3 * i + k],
                device_id=(px, py, mc), device_id_type=_MESH))
    return copies


def _quad_shape(p, view):
    return p.shape if view == "chip" else (N_CHIPS, p.shape[0], p.shape[1] // N_CHIPS)


def _grads_to_chips_start(pairs, views, *, name):
    n = len(pairs)
    lands = [pltpu.with_memory_space_constraint(lax.empty(_quad_shape(p, v), p.dtype), pltpu.HBM) for p, v in zip(pairs, views)]

    def body(*refs):
        pair_refs, land_refs = refs[:n], refs[n:2 * n]
        send_sems, recv_sems = refs[2 * n], refs[2 * n + 1]
        token = refs[-1]
        for cp in _chip_exchange_copies(pair_refs, land_refs, pairs, views, send_sems, recv_sems):
            cp.start()
        token[...] = jnp.zeros_like(token)

    outs = pl.pallas_call(
        body, name=name,
        out_shape=(pltpu.SemaphoreType.DMA((3 * n,)), pltpu.SemaphoreType.DMA((3 * n,)),
                   *[pltpu.HBM(p.shape, p.dtype) for p in pairs], *[pltpu.HBM(l.shape, l.dtype) for l in lands],
                   jax.ShapeDtypeStruct((SUBLANE, LANE), F32)),
        in_specs=[_HBM] * (2 * n), out_specs=(_SEM, _SEM, *[_HBM] * (2 * n), pl.BlockSpec(memory_space=pltpu.VMEM)),
        input_output_aliases={i: 2 + i for i in range(2 * n)},
        compiler_params=pltpu.CompilerParams(has_side_effects=_EFFECT),
    )(*[pltpu.with_memory_space_constraint(p, pltpu.HBM) for p in pairs], *lands)
    return outs[0], outs[1], list(outs[2:2 + n]), list(outs[2 + n:2 + 2 * n]), outs[-1]


def _grads_to_chips_wait(send_sems, recv_sems, pairs, lands, views, after, *, name):
    n = len(pairs)

    def body(*refs):
        pair_refs, land_refs = refs[:n], refs[n:2 * n]
        s_sems, r_sems = refs[2 * n], refs[2 * n + 1]
        for cp in _chip_exchange_copies(pair_refs, land_refs, pairs, views, s_sems, r_sems):
            cp.wait_send()
            cp.wait_recv()

    outs = pl.pallas_call(
        body, name=name, out_shape=tuple(pltpu.HBM(x.shape, x.dtype) for x in list(pairs) + list(lands)),
        in_specs=[_HBM] * (2 * n) + [_SEM, _SEM, _ANY], out_specs=tuple([_HBM] * (2 * n)),
        input_output_aliases={i: i for i in range(2 * n)},
        compiler_params=pltpu.CompilerParams(has_side_effects=_EFFECT),
    )(*pairs, *lands, send_sems, recv_sems, after)
    return list(outs[n:])


def _grads_share(tots, *, name):
    n = len(tots)

    def body(*refs):
        ins, outs = refs[:n], refs[n:2 * n]
        send_sems, recv_sems = refs[2 * n:]
        mx, my, mc = lax.axis_index("x"), lax.axis_index("y"), lax.axis_index("c")
        copies = []
        for i in range(n):
            cp = pltpu.make_async_remote_copy(src_ref=ins[i], dst_ref=outs[i], send_sem=send_sems.at[i], recv_sem=recv_sems.at[i],
                                              device_id=(mx, my, 1 - mc), device_id_type=_MESH)
            cp.start()
            copies.append(cp)
        for cp in copies:
            cp.wait()

    return pl.pallas_call(
        body, out_shape=[jax.ShapeDtypeStruct(t.shape, t.dtype) for t in tots], in_specs=[_ANY] * n, out_specs=[_ANY] * n,
        scratch_shapes=[pltpu.SemaphoreType.DMA((n,)), pltpu.SemaphoreType.DMA((n,))], name=name)(*tots)


def _pair_sum(g, recv, view, c_idx, *, name):
    def body(c_ref, a_ref, b_ref, o_ref):
        o_ref[...] = (a_ref[...] + b_ref[...]).astype(WIRE_DTYPE)

    if view == "chip":
        nch, r, c = g.shape
        tr = _row_tile(r // 2, nch * c * 4, 16)
        gv = g.reshape(nch, 2, r // 2, c)
        grid = ((r // 2) // tr,)
        in_specs = [pl.BlockSpec((nch, None, tr, c), lambda i, c_ref: (0, c_ref[0], i, 0)),
                    pl.BlockSpec((nch, tr, c), lambda i, c_ref: (0, i, 0))]
        out_spec = pl.BlockSpec((nch, tr, c), lambda i, c_ref: (0, i, 0))
        sem = ("parallel",)
    else:
        r, c4 = g.shape
        tr = _row_tile(r // 2, c4 * 4, 16)
        gv = g.reshape(2, r // 2, c4)
        grid = ((r // 2) // tr,)
        in_specs = [pl.BlockSpec((None, tr, c4), lambda i, c_ref: (c_ref[0], i, 0)), pl.BlockSpec((tr, c4), lambda i, c_ref: (i, 0))]
        out_spec = pl.BlockSpec((tr, c4), lambda i, c_ref: (i, 0))
        sem = ("parallel",)
    grid_spec = pltpu.PrefetchScalarGridSpec(num_scalar_prefetch=1, grid=grid, in_specs=in_specs, out_specs=out_spec)
    return pl.pallas_call(body, grid_spec=grid_spec, out_shape=jax.ShapeDtypeStruct(recv.shape, WIRE_DTYPE),
                          compiler_params=_params(*sem), name=name)(c_idx, gv, recv)


def _quad_sum(gs, recvs, quads, view, chip_idx, c_idx, *, name):
    nl = len(quads)
    nch, rh, c = quads[0].shape
    tr = _row_tile(rh, c * 4, 16)

    def body(_, __, *refs):
        o_ref = refs[-1]
        per = nch + 1
        for l in range(nl):
            grp = refs[l * per:(l + 1) * per]
            acc = grp[0][...] + grp[1][...]
            for r in grp[2:]:
                acc = acc + r[...].astype(F32)
            o_ref[l] = acc

    if view == "chip":
        own = [pl.BlockSpec((None, None, tr, c), lambda i, j, h: (j[0], h[0], i, 0)),
               pl.BlockSpec((None, tr, c), lambda i, j, h: (j[0], i, 0))]
        gviews = [g.reshape(nch, 2, rh, c) for g in gs]
    else:
        own = [pl.BlockSpec((None, tr, c), lambda i, j, h: (h[0], i, j[0])), pl.BlockSpec((tr, c), lambda i, j, h: (i, j[0]))]
        gviews = [g.reshape(2, rh, nch * c) for g in gs]
    assert nch & (nch - 1) == 0
    got = [pl.BlockSpec((None, tr, c), functools.partial(lambda i, j, h, k: ((j[0] + k) & (nch - 1), i, 0), k=k))
           for k in range(1, nch)]
    ins = []
    for l in range(nl):
        ins += [gviews[l], recvs[l]] + [quads[l]] * (nch - 1)
    grid_spec = pltpu.PrefetchScalarGridSpec(
        num_scalar_prefetch=2, grid=(rh // tr,), in_specs=(own + got) * nl,
        out_specs=pl.BlockSpec((nl, tr, c), lambda i, j, h: (0, i, 0)))
    return pl.pallas_call(body, grid_spec=grid_spec, out_shape=jax.ShapeDtypeStruct((nl, rh, c), F32),
                          compiler_params=_params("parallel"), name=name)(chip_idx, c_idx, *ins)


def _sum_devices(g8, own, dev_idx, *, name):
    k, rows, cols = g8.shape

    def body(d_ref, a_ref, x_ref, o_ref):
        acc = None
        for i in range(k):
            term = jnp.where(d_ref[0] == i, x_ref[...], a_ref[i])
            acc = term if acc is None else acc + term
        o_ref[...] = acc

    grid_spec = pltpu.PrefetchScalarGridSpec(
        num_scalar_prefetch=1, grid=(1,),
        in_specs=[pl.BlockSpec((k, rows, cols), lambda i, d_ref: (0, 0, 0)), pl.BlockSpec((rows, cols), lambda i, d_ref: (0, 0))],
        out_specs=pl.BlockSpec((rows, cols), lambda i, d_ref: (0, 0)))
    return pl.pallas_call(body, grid_spec=grid_spec, out_shape=jax.ShapeDtypeStruct((rows, cols), g8.dtype),
                          compiler_params=_params("arbitrary"), name=name)(dev_idx, g8, own)


def _adamw(w, g, m, v, *, name):
    rows, cols = w.shape
    tr = rows
    for cand in (256, 128, 64, 32, 16, 8):
        if rows % cand == 0 and cand * cols <= 512 * 1024:
            tr = cand
            break
    c1 = 1.0 - ADAM_B1 ** ADAM_STEP
    c2 = 1.0 - ADAM_B2 ** ADAM_STEP

    def body(w_ref, g_ref, m_ref, v_ref, d_ref, nm_ref, nv_ref):
        gv = g_ref[...]
        nm = ADAM_B1 * m_ref[...] + (1.0 - ADAM_B1) * gv
        nv = ADAM_B2 * v_ref[...] + (1.0 - ADAM_B2) * (gv * gv)
        d_ref[...] = -ADAM_LR * ((nm / c1) / (jnp.sqrt(nv / c2) + ADAM_EPS) + ADAM_WD * w_ref[...])
        nm_ref[...] = nm
        nv_ref[...] = nv

    spec = pl.BlockSpec((tr, cols), lambda i: (i, 0))
    shp = jax.ShapeDtypeStruct((rows, cols), F32)
    return pl.pallas_call(body, grid=(rows // tr,), in_specs=[spec] * 4, out_specs=[spec] * 3, out_shape=[shp] * 3,
                          compiler_params=_params("parallel"), name=name)(w, g, m, v)


def _adamw_halves(w, m, v, mine, other, c_idx, *, name):
    nl, r, c = w.shape
    rh = r // 2
    tr = _row_tile(rh, c * 4)
    c1 = 1.0 - ADAM_B1 ** ADAM_STEP
    c2 = 1.0 - ADAM_B2 ** ADAM_STEP

    def body(c_ref, w_ref, m_ref, v_ref, a_ref, b_ref, g_ref, d_ref, nm_ref, nv_ref):
        gv = jnp.where(pl.program_id(1) == c_ref[0], a_ref[...], b_ref[...])
        nm = ADAM_B1 * m_ref[...] + (1.0 - ADAM_B1) * gv
        nv = ADAM_B2 * v_ref[...] + (1.0 - ADAM_B2) * (gv * gv)
        g_ref[...] = gv
        d_ref[...] = -ADAM_LR * ((nm / c1) / (jnp.sqrt(nv / c2) + ADAM_EPS) + ADAM_WD * w_ref[...])
        nm_ref[...] = nm
        nv_ref[...] = nv

    full = pl.BlockSpec((None, None, tr, c), lambda l, h, i, c_ref: (l, h, i, 0))
    half_mine = pl.BlockSpec((None, tr, c), lambda l, h, i, c_ref: (l, jnp.where(h == c_ref[0], i, 0), 0))
    half_other = pl.BlockSpec((None, tr, c), lambda l, h, i, c_ref: (l, jnp.where(h == c_ref[0], 0, i), 0))
    grid_spec = pltpu.PrefetchScalarGridSpec(num_scalar_prefetch=1, grid=(nl, 2, rh // tr),
                                             in_specs=[full] * 3 + [half_mine, half_other], out_specs=[full] * 4)
    shp = jax.ShapeDtypeStruct((nl, 2, rh, c), F32)
    view = (nl, 2, rh, c)
    outs = pl.pallas_call(body, grid_spec=grid_spec, out_shape=[shp] * 4, compiler_params=_params("arbitrary", "arbitrary", "arbitrary"),
                          name=name)(c_idx, w.reshape(view), m.reshape(view), v.reshape(view), mine, other)
    return [o.reshape(nl, r, c) for o in outs]


WEIGHTS = ["mem_ln_g", "mem_ln_b", "w_in", "sg_ln_g", "sg_ln_b", "sg_w", "sg_b", "conv_w", "conv_b", "dt_bias", "a_log",
           "d_skip", "ssm_norm_g", "p_a", "p_b", "w_mix_o", "w_xq", "w_xkv", "w_xo", "w_ffn_in", "w_ffn_out", "ln_g", "ln_b"]
ARG_NAMES = ["x", "mem"] + WEIGHTS + ["loss_target"] + ["m_" + n for n in WEIGHTS] + ["v_" + n for n in WEIGHTS]
BIG = {"w_in": (1, (1024, 9248)), "p_a": (0, (1024, 1024)), "p_b": (0, (2048, 1024)), "w_mix_o": (0, (1024, 1024)),
       "w_xq": (0, (1024, 1024)), "w_xkv": (1, (1024, 2048)), "w_xo": (0, (1024, 1024)), "w_ffn_in": (1, (1024, 5632)),
       "w_ffn_out": (0, (2816, 1024))}
SMALL_SHARDED = {"conv_w": (4, 3072), "ln_g": (3, 1024), "ln_b": (3, 1024)}
SMALL = [n for n in WEIGHTS if n not in BIG]
W_IN_MAP = ((0, 4096, "main", 0), (4096, 7168, "main", XBC_COL0), (7168, 7200, "dt", 0), (7200, 9248, "main", GAB_COL0))
W_IN_SHARD = 9248 // N_CHIPS


def _w_in_chip_major(gm, gd):
    src = {"main": gm, "dt": gd}
    blocks = []
    for j in range(N_CHIPS):
        lo, hi = j * W_IN_SHARD, (j + 1) * W_IN_SHARD
        parts = [src[k][:, o + max(lo, a) - a:o + min(hi, b) - a] for a, b, k, o in W_IN_MAP if max(lo, a) < min(hi, b)]
        blocks.append(jnp.concatenate(parts, axis=1))
    return jnp.stack(blocks)


def _w_in_reassemble(wc):
    def cols(a, b):
        out = []
        for j in range(N_CHIPS):
            lo, hi = max(a, j * W_IN_SHARD), min(b, (j + 1) * W_IN_SHARD)
            if lo < hi:
                out.append(wc[j][:, lo - j * W_IN_SHARD:hi - j * W_IN_SHARD])
        return out

    main = sorted((m for m in W_IN_MAP if m[2] == "main"), key=lambda m: m[3])
    w_main = jnp.concatenate([p for a, b, _, _ in main for p in cols(a, b)], axis=1)
    (a, b, _, _), = [m for m in W_IN_MAP if m[2] == "dt"]
    w_dt = jnp.pad(jnp.concatenate(cols(a, b), axis=1), ((0, 0), (0, HEAD_PAD - (b - a))))
    return w_main, w_dt
GATHER_KIND = {"w_in": "chip", "p_a": "row", "p_b": "row", "w_mix_o": "row", "w_xq": "row", "w_xkv": "col", "w_xo": "row",
               "w_ffn_in": "col", "w_ffn_out": "row", "conv_w": "chip", "ln_g": "chip", "ln_b": "chip"}
GRAD_VIEW = {n: ("col" if k == "col" else "chip") for n, k in GATHER_KIND.items() if n in BIG}


def _shard_shape(name):
    axis, (r, c) = BIG[name]
    return (r // N_CHIPS, c) if axis == 0 else (r, c // N_CHIPS)


def _pad_rows(flat, cols, row_mult):
    n = flat.shape[0]
    rows = -(-n // cols)
    rows = -(-rows // row_mult) * row_mult
    return jnp.pad(flat, (0, rows * cols - n)).reshape(rows, cols)


def _gather_small_params(a, chip):
    names = list(SMALL_SHARDED)
    kinds = [GATHER_KIND[n] for n in names]
    bufs = [_cast_place(a[n], GATHER_KIND[n], F32, chip.reshape(1), name=f"place_{n}") for n in names]
    outs = _gather_params(bufs, [a[n].shape[1:] for n in names], kinds, name="gather_small_params")
    full = {}
    for n, o in zip(names, outs):
        _, _, r, c = o.shape
        full[n] = jnp.transpose(o, (0, 2, 1, 3)).reshape(DEPTH, r, N_CHIPS * c)
    return full


GATHER_GROUPS = (("w_in",), tuple(n for n in BIG if n != "w_in"))


def _gather_group_start(a, l, names, chip, after, *, tag):
    bufs = [_cast_place_layer(a[n], l, GATHER_KIND[n], chip.reshape(1), after, name=f"place_{n}_l{l}") for n in names]
    return _gather_start(bufs, [a[n].shape[1:] for n in names], [GATHER_KIND[n] for n in names], name=f"gather_start_{tag}")


def _gather_group_finish(a, names, flight, after, *, tag):
    send_sems, recv_sems, bufs, token = flight
    shapes, kinds = [a[n].shape[1:] for n in names], [GATHER_KIND[n] for n in names]
    bufs = _gather_wait(send_sems, recv_sems, bufs, shapes, kinds, token if after is None else after, name=f"gather_wait_{tag}")
    full = dict(zip(names, _gather_forward(bufs, shapes, kinds, name=f"gather_forward_{tag}")))
    if "w_in" in full:
        full["w_main"], full["w_dt"] = _w_in_reassemble(full.pop("w_in"))
    return full


def _layer_weights(a, big, small, l):
    w = dict(big)
    for n in SMALL_SHARDED:
        w[n] = small[n][l]
    for n in ["sg_ln_g", "sg_ln_b", "sg_w", "conv_b", "ssm_norm_g"]:
        w[n] = a[n][l]
    w["sg_bcol"] = a["sg_b"][l][..., None]
    for n in ["dt_bias", "a_log"]:
        w[n + "8"] = _pad_heads(a[n][l])
    w["d_skipx"] = _expand_heads(a["d_skip"][l])
    return w


def _grad_views(grads, names):
    gs = []
    for n in names:
        axis, _ = BIG[n]
        r, c = _shard_shape(n)
        if n == "w_in":
            gs.append(_w_in_chip_major(grads["w_main"], grads["w_dt"]))
        elif axis == 0:
            gs.append(grads[n].reshape(N_CHIPS, r, c))
        else:
            gs.append(grads[n])
    return gs


class _GradExchange:
    def __init__(self, grads, names, c_idx, tag):
        self.names, self.c_idx, self.tag = names, c_idx, tag
        self.views = [GRAD_VIEW[n] for n in names]
        self.gs = _grad_views(grads, names)

    def start(self):
        self.sems = _grads_to_sibling_start(self.gs, self.views, name=f"grads_to_sibling_start_{self.tag}")
        return self.sems[4]

    def cross(self, after):
        send_sems, recv_sems, gs, lands, token = self.sems
        self.gs, self.recv = _grads_to_sibling_wait(send_sems, recv_sems, gs, lands, self.views, token if after is None else after,
                                                    name=f"grads_to_sibling_wait_{self.tag}")
        cpre = self.c_idx.reshape(1)
        pairs = [_pair_sum(g, rv, v, cpre, name=f"grads_pair_sum_{n}_{self.tag}")
                 for g, rv, v, n in zip(self.gs, self.recv, self.views, self.names)]
        self.sems = _grads_to_chips_start(pairs, self.views, name=f"grads_to_chips_start_{self.tag}")
        return self.sems[4]

    def finish(self, after):
        send_sems, recv_sems, pairs, lands, _ = self.sems
        quads = _grads_to_chips_wait(send_sems, recv_sems, pairs, lands, self.views, after, name=f"grads_to_chips_wait_{self.tag}")
        return {n: (g, rv, q) for n, g, rv, q in zip(self.names, self.gs, self.recv, quads)}


def _finish_big_grads(parts, c_idx, chip):
    tots = [_quad_sum([parts[l][n][0] for l in range(DEPTH)], [parts[l][n][1] for l in range(DEPTH)],
                      [parts[l][n][2] for l in range(DEPTH)], GRAD_VIEW[n], chip.reshape(1), c_idx.reshape(1),
                      name=f"grads_chip_sum_{n}") for n in BIG]
    others = _grads_share(tots, name="grads_share")
    return {n: (t, o) for n, t, o in zip(BIG, tots, others)}


def _direct_copies(x_ref, land_ref, send_sems, recv_sems):
    mx, my, mc = lax.axis_index("x"), lax.axis_index("y"), lax.axis_index("c")
    me = 4 * mx + 2 * my + mc
    copies = []
    for k in range(N_DEV - 1):
        f = k + 1
        to = (mx ^ (f >> 2 & 1), my ^ (f >> 1 & 1), mc ^ (f & 1))
        copies.append(pltpu.make_async_remote_copy(src_ref=x_ref, dst_ref=land_ref.at[me], send_sem=send_sems.at[k],
                                                   recv_sem=recv_sems.at[k], device_id=to, device_id_type=_MESH))
    return copies


def _all_gather8_start(x, *, name):
    land = pltpu.with_memory_space_constraint(lax.empty((N_DEV,) + x.shape, x.dtype), pltpu.HBM)

    def body(x_ref, land_ref, send_sems, recv_sems, x_out, land_out, token):
        for cp in _direct_copies(x_ref, land_ref, send_sems, recv_sems):
            cp.start()
        token[...] = jnp.zeros_like(token)

    n = N_DEV - 1
    return pl.pallas_call(
        body, name=name,
        out_shape=(pltpu.SemaphoreType.DMA((n,)), pltpu.SemaphoreType.DMA((n,)), pltpu.HBM(x.shape, x.dtype),
                   pltpu.HBM(land.shape, land.dtype), jax.ShapeDtypeStruct((SUBLANE, LANE), F32)),
        in_specs=[_HBM, _HBM], out_specs=(_SEM, _SEM, _HBM, _HBM, pl.BlockSpec(memory_space=pltpu.VMEM)),
        input_output_aliases={0: 2, 1: 3}, compiler_params=pltpu.CompilerParams(has_side_effects=_EFFECT),
    )(pltpu.with_memory_space_constraint(x, pltpu.HBM), land)


def _all_gather8_wait(send_sems, recv_sems, x, land, after, *, name):
    def body(x_ref, land_ref, s_sems, r_sems, _, x_out, land_out):
        for cp in _direct_copies(x_ref, land_ref, s_sems, r_sems):
            cp.wait_send()
            cp.wait_recv()

    return pl.pallas_call(
        body, name=name, out_shape=(pltpu.HBM(x.shape, x.dtype), pltpu.HBM(land.shape, land.dtype)),
        in_specs=[_HBM, _HBM, _SEM, _SEM, _ANY], out_specs=(_HBM, _HBM), input_output_aliases={0: 0, 1: 1},
        compiler_params=pltpu.CompilerParams(has_side_effects=_EFFECT),
    )(x, land, send_sems, recv_sems, after)


def _pack_small(small):
    return _pad_rows(jnp.concatenate([small[n].reshape(-1) for n in small]), LANE, SUBLANE)


def _unpack_small(small, g8, packed, chip, c_idx, *, name):
    names = list(small)
    tot = _sum_devices(g8, packed, (2 * chip + c_idx).reshape(1), name=name).reshape(-1)
    out, off = {}, 0
    for n in names:
        sz = small[n].size
        full = tot[off:off + sz].reshape(small[n].shape)
        off += sz
        if n in SMALL_SHARDED:
            cs = SMALL_SHARDED[n][1] // N_CHIPS
            full = lax.dynamic_slice_in_dim(full, chip * cs, cs, axis=-1)
        out[n] = full
    return out


def kernel(x, mem, mem_ln_g, mem_ln_b, w_in, sg_ln_g, sg_ln_b, sg_w, sg_b, conv_w, conv_b, dt_bias, a_log, d_skip, ssm_norm_g, p_a, p_b, w_mix_o, w_xq, w_xkv, w_xo, w_ffn_in, w_ffn_out, ln_g, ln_b, loss_target, m_mem_ln_g, m_mem_ln_b, m_w_in, m_sg_ln_g, m_sg_ln_b, m_sg_w, m_sg_b, m_conv_w, m_conv_b, m_dt_bias, m_a_log, m_d_skip, m_ssm_norm_g, m_p_a, m_p_b, m_w_mix_o, m_w_xq, m_w_xkv, m_w_xo, m_w_ffn_in, m_w_ffn_out, m_ln_g, m_ln_b, v_mem_ln_g, v_mem_ln_b, v_w_in, v_sg_ln_g, v_sg_ln_b, v_sg_w, v_sg_b, v_conv_w, v_conv_b, v_dt_bias, v_a_log, v_d_skip, v_ssm_norm_g, v_p_a, v_p_b, v_w_mix_o, v_w_xq, v_w_xkv, v_w_xo, v_w_ffn_in, v_w_ffn_out, v_ln_g, v_ln_b):
    a = dict(zip(ARG_NAMES, (x, mem, mem_ln_g, mem_ln_b, w_in, sg_ln_g, sg_ln_b, sg_w, sg_b, conv_w, conv_b, dt_bias, a_log, d_skip, ssm_norm_g, p_a, p_b, w_mix_o, w_xq, w_xkv, w_xo, w_ffn_in, w_ffn_out, ln_g, ln_b, loss_target, m_mem_ln_g, m_mem_ln_b, m_w_in, m_sg_ln_g, m_sg_ln_b, m_sg_w, m_sg_b, m_conv_w, m_conv_b, m_dt_bias, m_a_log, m_d_skip, m_ssm_norm_g, m_p_a, m_p_b, m_w_mix_o, m_w_xq, m_w_xkv, m_w_xo, m_w_ffn_in, m_w_ffn_out, m_ln_g, m_ln_b, v_mem_ln_g, v_mem_ln_b, v_w_in, v_sg_ln_g, v_sg_ln_b, v_sg_w, v_sg_b, v_conv_w, v_conv_b, v_dt_bias, v_a_log, v_d_skip, v_ssm_norm_g, v_p_a, v_p_b, v_w_mix_o, v_w_xq, v_w_xkv, v_w_xo, v_w_ffn_in, v_w_ffn_out, v_ln_g, v_ln_b)))
    c_idx = lax.axis_index("c").astype(jnp.int32)
    chip = (2 * lax.axis_index("x") + lax.axis_index("y")).astype(jnp.int32)

    small = _gather_small_params(a, chip)
    ga, gb = GATHER_GROUPS
    flights = {(0, 0): _gather_group_start(a, 0, ga, chip, small["ln_b"], tag="l0_a")}
    flights[0, 1] = _gather_group_start(a, 0, gb, chip, flights[0, 0][3], tag="l0_b")

    def layer_weights(after, l):
        first = _gather_group_finish(a, ga, flights[l, 0], after if l else flights[l, 1][3], tag=f"l{l}_a")

        def rest(w, after_b):
            more = _gather_group_finish(a, gb, flights[l, 1], after_b, tag=f"l{l}_b")
            if l + 1 < DEPTH:
                flights[l + 1, 0] = _gather_group_start(a, l + 1, ga, chip, more["p_a"], tag=f"l{l + 1}_a")
                flights[l + 1, 1] = _gather_group_start(a, l + 1, gb, chip, flights[l + 1, 0][3], tag=f"l{l + 1}_b")
                more["p_a"] = more["p_a"] + flights[l + 1, 1][3][0, 0].astype(MXU_DTYPE)
            return {k: v for k, v in {**w, **more}.items() if k != "rest"}

        return dict(_layer_weights(a, first, small, l), rest=rest)

    layers = [functools.partial(layer_weights, l=l) for l in range(DEPTH)]
    exchanges, seen, small_flight = [], {}, {}

    def start_exchange(l, names, grads_l):
        ex = _GradExchange(grads_l, names, c_idx, f"l{l}_{names[0]}")
        tokens = [ex.start()]
        if exchanges:
            tokens.append(exchanges[-1][1].cross(tokens[0]))
        exchanges.append((l, ex))
        seen[l] = grads_l
        if l == 0 and names == GRAD_GROUPS[-1]:
            tokens.append(ex.cross(None))
            small = {}
            for n in SMALL:
                if n.startswith("mem_ln"):
                    continue
                per_layer = []
                for k in range(DEPTH):
                    g = seen[k][n]
                    if n in ("dt_bias", "a_log", "d_skip"):
                        g = g[0, :SSM_HEADS]
                    per_layer.append(g.reshape(a[n].shape[1:-1] + (-1,)))
                small[n] = jnp.stack(per_layer)
            small_flight["small"] = small
            small_flight["sems"] = _all_gather8_start(_pack_small(small), name="gather_small_grads_start")
            tokens.append(small_flight["sems"][4])
        return sum(tokens[1:], tokens[0])

    lsum, grad_x, grads, d_mem_g, d_mem_b = _local_step(x, mem, loss_target, mem_ln_g, mem_ln_b, layers, start_exchange)
    loss = lax.psum(0.5 * jnp.sum(lsum) / D_MODEL, ("x", "y", "c"))

    parts = [{} for _ in range(DEPTH)]
    for l, ex in exchanges:
        parts[l].update(ex.finish(grad_x))
    halves = _finish_big_grads(parts, c_idx, chip)
    gw = {}
    send_sems, recv_sems, packed, land, _ = small_flight["sems"]
    packed, g8 = _all_gather8_wait(send_sems, recv_sems, packed, land, grad_x, name="gather_small_grads_wait")
    gw.update(_unpack_small(small_flight["small"], g8, packed, chip, c_idx, name="small_grads_sum"))
    mem_small = {"mem_ln_g": d_mem_g, "mem_ln_b": d_mem_b}
    mem_packed = _pack_small(mem_small)
    gw.update(_unpack_small(mem_small, _all_gather8(mem_packed, name="gather_mem_ln_grads"), mem_packed, chip, c_idx,
                            name="mem_ln_grads_sum"))

    delta, new_m, new_v = {}, {}, {}
    for n in BIG:
        mine, other = halves[n]
        gw[n], delta[n], new_m[n], new_v[n] = _adamw_halves(a[n], a["m_" + n], a["v_" + n], mine, other, c_idx.reshape(1),
                                                             name=f"adamw_{n}")
    for n in SMALL:
        shp = a[n].shape
        view = (-1, LANE) if a[n].size % LANE == 0 else (1, -1)
        outs = _adamw(*[v.reshape(view) for v in (a[n], gw[n], a["m_" + n], a["v_" + n])], name=f"adamw_{n}")
        delta[n], new_m[n], new_v[n] = (o.reshape(shp) for o in outs)
    return (loss, grad_x, *[gw[n].reshape(a[n].shape) for n in WEIGHTS], *[delta[n] for n in WEIGHTS],
            *[new_m[n] for n in WEIGHTS], *[new_v[n] for n in WEIGHTS])
```

```python
import functools
import math

import jax
import jax.numpy as jnp
from jax import lax
from jax.experimental import pallas as pl
from jax.experimental.pallas import tpu as pltpu

F32 = jnp.float32
MXU_DTYPE = jnp.bfloat16
WIRE_DTYPE = jnp.bfloat16
STASH_DTYPE = jnp.bfloat16

D_MODEL = 1024
DEPTH = 2
CHUNK = 128
SG_GROUPS = 8
SSM_INNER = 2048
SSM_HEADDIM = 64
SSM_HEADS = 32
SSM_STATE = 128
SSM_GROUPS = 4
SSM_CONV = 4
SSM_CONV_DIM = 3072
X_HEADS = 4
X_HEADDIM = 256
FFN_HIDDEN = 2816
ALPHA = float((2 * DEPTH) ** 0.25)
LN_EPS = 1e-5
RMS_EPS = 1e-5
ADAM_LR = 0.001
ADAM_B1 = 0.9
ADAM_B2 = 0.999
ADAM_EPS = 1e-08
ADAM_WD = 0.01
ADAM_STEP = 10

MAIN_COLS = 9216
UVZ_COLS = 4096
GAB_COL0 = 4096
XBC_COL0 = 6144
HEAD_PAD = 128

VMEM_LIMIT = 56 * 1024 * 1024
BLOCK_BYTES = 2 * 1024 * 1024
ROW_TILES = (512, 256, 128)
LANE = 128
SUBLANE = 8

N_CHIPS = 4
N_DEV = 8


def _pick(n, cands):
    for c in cands:
        if n % c == 0:
            return c
    return n


MM_TILE_MAX = 2048
MM_OPERAND_BYTES = 12 * 1024 * 1024


def _div_tile(n, limit):
    best = None
    for t in range(LANE, min(n, limit) + 1, LANE):
        if n % t == 0:
            best = t
    return n if best is None else best


def _params(*sem):
    return pltpu.CompilerParams(dimension_semantics=tuple(sem), vmem_limit_bytes=VMEM_LIMIT)


_ANY = pl.BlockSpec(memory_space=pl.ANY)
_MESH = pl.DeviceIdType.MESH


def _nt(a, b):
    return lax.dot_general(a, b, (((1,), (1,)), ((), ())), preferred_element_type=F32)


def _tn(a, b):
    return lax.dot_general(a, b, (((0,), (0,)), ((), ())), preferred_element_type=F32)


def _nn(a, b):
    return jnp.dot(a, b, preferred_element_type=F32)


def _sigmoid(x):
    return 0.5 * jnp.tanh(0.5 * x) + 0.5


def _split3(v):
    def top(x):
        bits = lax.bitcast_convert_type(x, jnp.uint32) & jnp.uint32(0xFFFF0000)
        return lax.bitcast_convert_type(bits, F32)

    v1 = top(v)
    r1 = v - v1
    v2 = top(r1)
    v3 = r1 - v2
    return v1.astype(jnp.bfloat16), v2.astype(jnp.bfloat16), v3.astype(jnp.bfloat16)


def _dot_exact(a, b, dn, data):
    if data == 0:
        mat = b.astype(jnp.bfloat16)
        return sum(lax.dot_general(p, mat, dn, preferred_element_type=F32) for p in _split3(a))
    mat = a.astype(jnp.bfloat16)
    return sum(lax.dot_general(mat, p, dn, preferred_element_type=F32) for p in _split3(b))


_DN_NN = (((1,), (0,)), ((), ()))
_DN_TN = (((0,), (0,)), ((), ()))


def _gelu(x):
    return 0.5 * x * (1.0 + lax.erf(x * (2.0 ** -0.5)))


def _gelu_grad(x):
    return 0.5 * (1.0 + lax.erf(x * (2.0 ** -0.5))) + x * jnp.exp(-0.5 * x * x) * (1.0 / math.sqrt(2.0 * math.pi))


def _mm(a, b, *, ta=False, tb=False, out_dtype=F32, after=None, name):
    if ta:
        kdim, m = a.shape
    else:
        m, kdim = a.shape
    if tb:
        n, k2 = b.shape[-2:]
    else:
        k2, n = b.shape[-2:]
    assert kdim == k2, (a.shape, b.shape, ta, tb)
    tm = _div_tile(m, MM_TILE_MAX)
    tn = _div_tile(n, MM_TILE_MAX)
    tk = _div_tile(kdim, MM_OPERAND_BYTES // (tm * a.dtype.itemsize + tn * b.dtype.itemsize))
    nk = kdim // tk
    dn = (((0 if ta else 1,), (1 if tb else 0,)), ((), ()))

    extra = [] if after is None else [after]

    def body(a_ref, b_ref, *rest):
        o_ref = rest[len(extra)]
        d = lax.dot_general(a_ref[...].astype(MXU_DTYPE), b_ref[...].astype(MXU_DTYPE), dn, preferred_element_type=F32)
        if nk == 1:
            o_ref[...] = d.astype(out_dtype)
            return
        acc_ref = rest[len(extra) + 1]
        k = pl.program_id(2)

        @pl.when(k == 0)
        def _():
            acc_ref[...] = d

        @pl.when(jnp.logical_and(k > 0, k < nk - 1))
        def _():
            acc_ref[...] += d

        @pl.when(k == nk - 1)
        def _():
            o_ref[...] = (acc_ref[...] + d).astype(out_dtype)

    a_spec = pl.BlockSpec((tk, tm), lambda i, j, k: (k, i)) if ta else pl.BlockSpec((tm, tk), lambda i, j, k: (i, k))
    b_spec = pl.BlockSpec((tn, tk), lambda i, j, k: (j, k)) if tb else pl.BlockSpec((tk, tn), lambda i, j, k: (k, j))
    return pl.pallas_call(
        body, grid=(m // tm, n // tn, nk), in_specs=[a_spec, b_spec] + [_ANY] * len(extra),
        out_specs=pl.BlockSpec((tm, tn), lambda i, j, k: (i, j)),
        out_shape=jax.ShapeDtypeStruct((m, n), out_dtype),
        scratch_shapes=[pltpu.VMEM((tm, tn), F32)] if nk > 1 else [],
        compiler_params=_params("parallel", "parallel", "arbitrary"), name=name)(a, b, *extra)


def _row_spec(tm, c, col=0):
    return pl.BlockSpec((tm, c), lambda i: (i, col))


def _par_spec(shape):
    nd = len(shape)
    return pl.BlockSpec(shape, lambda i: (0,) * nd)


def _ln_fwd(x, f, g, b, *, name):
    t, c = x.shape
    tm = _pick(t, ROW_TILES)
    has_f = f is not None

    def body(*refs):
        if has_f:
            x_ref, f_ref, g_ref, b_ref, y_ref, yb_ref, xh_ref, rs_ref = refs
            r = ALPHA * x_ref[...] + f_ref[...]
        else:
            x_ref, g_ref, b_ref, y_ref, yb_ref, xh_ref, rs_ref = refs
            r = x_ref[...]
        mu = jnp.mean(r, axis=-1, keepdims=True)
        xc = r - mu
        var = jnp.mean(xc * xc, axis=-1, keepdims=True)
        rstd = lax.rsqrt(var + LN_EPS)
        xh = xc * rstd
        y = xh * g_ref[...] + b_ref[...]
        y_ref[...] = y
        yb_ref[...] = y.astype(MXU_DTYPE)
        xh_ref[...] = xh
        rs_ref[...] = jnp.broadcast_to(rstd, rs_ref.shape)

    ins = [x] + ([f] if has_f else []) + [g.reshape(1, c), b.reshape(1, c)]
    in_specs = [_row_spec(tm, c)] * (2 if has_f else 1) + [_par_spec((1, c))] * 2
    return pl.pallas_call(
        body, grid=(t // tm,), in_specs=in_specs,
        out_specs=[_row_spec(tm, c), _row_spec(tm, c), _row_spec(tm, c), _row_spec(tm, LANE)],
        out_shape=[jax.ShapeDtypeStruct((t, c), F32), jax.ShapeDtypeStruct((t, c), MXU_DTYPE),
                   jax.ShapeDtypeStruct((t, c), F32), jax.ShapeDtypeStruct((t, LANE), F32)],
        compiler_params=_params("parallel"), name=name)(*ins)


def _ln_bwd(addends, scales, xh, rs, g, *, name):
    t, c = xh.shape
    tm = _pick(t, ROW_TILES)
    na = len(addends)

    def body(*refs):
        a_refs = refs[:na]
        xh_ref, rs_ref, g_ref, dp_ref, dpb_ref, dg_ref, db_ref = refs[na:]

        @pl.when(pl.program_id(0) == 0)
        def _():
            dg_ref[...] = jnp.zeros_like(dg_ref)
            db_ref[...] = jnp.zeros_like(db_ref)

        dy = None
        for s, r in zip(scales, a_refs):
            term = r[...] if s == 1.0 else s * r[...]
            dy = term if dy is None else dy + term
        xhv = xh_ref[...]
        dxh = dy * g_ref[...]
        m1 = jnp.mean(dxh, axis=-1, keepdims=True)
        m2 = jnp.mean(dxh * xhv, axis=-1, keepdims=True)
        dp = rs_ref[:, 0:1] * (dxh - m1 - xhv * m2)
        dp_ref[...] = dp
        dpb_ref[...] = dp.astype(MXU_DTYPE)
        dg_ref[...] += jnp.sum(dy * xhv, axis=0, keepdims=True)
        db_ref[...] += jnp.sum(dy, axis=0, keepdims=True)

    in_specs = [_row_spec(tm, c)] * (na + 1) + [_row_spec(tm, LANE), _par_spec((1, c))]
    return pl.pallas_call(
        body, grid=(t // tm,), in_specs=in_specs,
        out_specs=[_row_spec(tm, c), _row_spec(tm, c), _par_spec((1, c)), _par_spec((1, c))],
        out_shape=[jax.ShapeDtypeStruct((t, c), F32), jax.ShapeDtypeStruct((t, c), MXU_DTYPE),
                   jax.ShapeDtypeStruct((1, c), F32), jax.ShapeDtypeStruct((1, c), F32)],
        compiler_params=_params("arbitrary"), name=name)(*addends, xh, rs, g.reshape(1, c))


def _add_scaled(addends, scales, *, name):
    t, c = addends[0].shape
    tm = _pick(t, ROW_TILES)
    na = len(addends)

    def body(*refs):
        acc = None
        for s, r in zip(scales, refs[:na]):
            term = r[...] if s == 1.0 else s * r[...]
            acc = term if acc is None else acc + term
        refs[na][...] = acc

    return pl.pallas_call(
        body, grid=(t // tm,), in_specs=[_row_spec(tm, c)] * na, out_specs=_row_spec(tm, c),
        out_shape=jax.ShapeDtypeStruct((t, c), F32), compiler_params=_params("parallel"), name=name)(*addends)


def _loss_head(y, tgt, *, name):
    t, c = y.shape
    tm = _pick(t, ROW_TILES)

    def body(y_ref, t_ref, dy_ref, ls_ref):
        @pl.when(pl.program_id(0) == 0)
        def _():
            ls_ref[...] = jnp.zeros_like(ls_ref)

        e = y_ref[...] - t_ref[...]
        dy_ref[...] = e * (1.0 / c)
        ls_ref[...] += jnp.sum(e * e, axis=0, keepdims=True)

    return pl.pallas_call(
        body, grid=(t // tm,), in_specs=[_row_spec(tm, c)] * 2,
        out_specs=[_row_spec(tm, c), _par_spec((1, c))],
        out_shape=[jax.ShapeDtypeStruct((t, c), F32), jax.ShapeDtypeStruct((1, c), F32)],
        compiler_params=_params("arbitrary"), name=name)(y, tgt)


def _swiglu_fwd(h, *, name):
    t, two_f = h.shape
    fh = two_f // 2
    tm = _pick(t, (256, 128))

    def body(g_ref, u_ref, a_ref):
        g = g_ref[...].astype(F32)
        a_ref[...] = (g * _sigmoid(g) * u_ref[...].astype(F32)).astype(MXU_DTYPE)

    return pl.pallas_call(
        body, grid=(t // tm,), in_specs=[_row_spec(tm, fh, 0), _row_spec(tm, fh, 1)], out_specs=_row_spec(tm, fh),
        out_shape=jax.ShapeDtypeStruct((t, fh), MXU_DTYPE), compiler_params=_params("parallel"), name=name)(h, h)


def _swiglu_bwd(h, da, *, name):
    t, two_f = h.shape
    fh = two_f // 2
    tm = _pick(t, (256, 128))

    def body(g_ref, u_ref, da_ref, dh_ref):
        g = g_ref[...].astype(F32)
        s = _sigmoid(g)
        dav = da_ref[...].astype(F32)
        dh_ref[:, :fh] = (dav * u_ref[...].astype(F32) * (s * (1.0 + g * (1.0 - s)))).astype(MXU_DTYPE)
        dh_ref[:, fh:] = (dav * g * s).astype(MXU_DTYPE)

    return pl.pallas_call(
        body, grid=(t // tm,), in_specs=[_row_spec(tm, fh, 0), _row_spec(tm, fh, 1), _row_spec(tm, fh)],
        out_specs=_row_spec(tm, two_f), out_shape=jax.ShapeDtypeStruct((t, two_f), MXU_DTYPE),
        compiler_params=_params("parallel"), name=name)(h, h, da)


def _attn_probs(q, k):
    s = _nt(q, k) * (X_HEADDIM ** -0.5)
    s = s - jnp.max(s, axis=-1, keepdims=True)
    p = jnp.exp(s)
    return p / jnp.sum(p, axis=-1, keepdims=True)


def _attn_fwd(q, kv, *, bsz, name):
    t = q.shape[0]
    s = t // bsz
    ml = kv.shape[0] // bsz
    hd = X_HEADDIM

    def body(q_ref, k_ref, v_ref, o_ref):
        p = _attn_probs(q_ref[...], k_ref[...])
        o_ref[...] = _nn(p.astype(MXU_DTYPE), v_ref[...]).astype(MXU_DTYPE)

    return pl.pallas_call(
        body, grid=(bsz, X_HEADS),
        in_specs=[pl.BlockSpec((s, hd), lambda b, h: (b, h)), pl.BlockSpec((ml, hd), lambda b, h: (b, h)),
                  pl.BlockSpec((ml, hd), lambda b, h: (b, X_HEADS + h))],
        out_specs=pl.BlockSpec((s, hd), lambda b, h: (b, h)),
        out_shape=jax.ShapeDtypeStruct((t, D_MODEL), MXU_DTYPE),
        compiler_params=_params("parallel", "parallel"), name=name)(q, kv, kv)


def _attn_bwd(q, kv, do, *, bsz, name):
    t = q.shape[0]
    s = t // bsz
    ml = kv.shape[0] // bsz
    hd = X_HEADDIM

    def body(q_ref, k_ref, v_ref, do_ref, dq_ref, dk_ref, dv_ref):
        qv, kk, vv, dov = q_ref[...], k_ref[...], v_ref[...], do_ref[...]
        p = _attn_probs(qv, kk)
        dp = _nt(dov, vv)
        dv_ref[...] = _tn(p.astype(MXU_DTYPE), dov).astype(MXU_DTYPE)
        ds = (p * (dp - jnp.sum(dp * p, axis=-1, keepdims=True)) * (X_HEADDIM ** -0.5)).astype(MXU_DTYPE)
        dq_ref[...] = _nn(ds, kk).astype(MXU_DTYPE)
        dk_ref[...] = _tn(ds, qv).astype(MXU_DTYPE)

    blk_q = pl.BlockSpec((s, hd), lambda b, h: (b, h))
    blk_m = pl.BlockSpec((ml, hd), lambda b, h: (b, h))
    return pl.pallas_call(
        body, grid=(bsz, X_HEADS),
        in_specs=[blk_q, blk_m, pl.BlockSpec((ml, hd), lambda b, h: (b, X_HEADS + h)), blk_q],
        out_specs=[blk_q, blk_m, blk_m],
        out_shape=[jax.ShapeDtypeStruct((t, D_MODEL), MXU_DTYPE), jax.ShapeDtypeStruct((bsz * ml, D_MODEL), MXU_DTYPE),
                   jax.ShapeDtypeStruct((bsz * ml, D_MODEL), MXU_DTYPE)],
        compiler_params=_params("parallel", "parallel"), name=name)(q, kv, kv, do)


def _causal(n):
    row = lax.broadcasted_iota(jnp.int32, (n, n), 0)
    col = lax.broadcasted_iota(jnp.int32, (n, n), 1)
    return row >= col


def _sg_norm(v, g, b):
    gv = _gelu(v)
    mu = jnp.mean(gv, axis=-1, keepdims=True)
    xc = gv - mu
    var = jnp.mean(xc * xc, axis=-1, keepdims=True)
    rstd = lax.rsqrt(var + LN_EPS)
    xh = xc * rstd
    return xh, rstd, xh * g + b


def _sg_fwd(proj, ln_g, ln_b, w, bcol, *, name):
    t = proj.shape[0]
    c = D_MODEL
    gd = c // SG_GROUPS

    def body(u_ref, v_ref, g_ref, b_ref, w_ref, bc_ref, o_ref):
        gu = _gelu(u_ref[...].astype(F32))
        _, _, vn = _sg_norm(v_ref[...].astype(F32), g_ref[...], b_ref[...])
        mask = _causal(CHUNK)
        for g in range(SG_GROUPS):
            sl = slice(g * gd, (g + 1) * gd)
            wg = jnp.where(mask, w_ref[g], 0.0).astype(MXU_DTYPE)
            mixed = _nn(wg, vn[:, sl].astype(MXU_DTYPE)) + bc_ref[g]
            o_ref[:, sl] = (gu[:, sl] * mixed).astype(MXU_DTYPE)

    return pl.pallas_call(
        body, grid=(t // CHUNK,),
        in_specs=[_row_spec(CHUNK, c, 0), _row_spec(CHUNK, c, 1), _par_spec((1, c)), _par_spec((1, c)),
                  _par_spec((SG_GROUPS, CHUNK, CHUNK)), _par_spec((SG_GROUPS, CHUNK, 1))],
        out_specs=_row_spec(CHUNK, c), out_shape=jax.ShapeDtypeStruct((t, c), MXU_DTYPE),
        compiler_params=_params("parallel"), name=name)(proj, proj, ln_g.reshape(1, c), ln_b.reshape(1, c), w, bcol)


def _sg_bwd(proj, dsgo, ln_g, ln_b, w, bcol, dproj, *, name):
    t = proj.shape[0]
    c = D_MODEL
    gd = c // SG_GROUPS

    def body(u_ref, v_ref, d_ref, g_ref, b_ref, w_ref, bc_ref, _, duv_ref, dw_ref, dbc_ref, dg_ref, db_ref, dvn_ref):
        @pl.when(pl.program_id(0) == 0)
        def _():
            dw_ref[...] = jnp.zeros_like(dw_ref)
            dbc_ref[...] = jnp.zeros_like(dbc_ref)
            dg_ref[...] = jnp.zeros_like(dg_ref)
            db_ref[...] = jnp.zeros_like(db_ref)

        u = u_ref[...].astype(F32)
        v = v_ref[...].astype(F32)
        dso = d_ref[...].astype(F32)
        gu = _gelu(u)
        xh, rstd, vn = _sg_norm(v, g_ref[...], b_ref[...])
        mask = _causal(CHUNK)
        for g in range(SG_GROUPS):
            sl = slice(g * gd, (g + 1) * gd)
            wg = jnp.where(mask, w_ref[g], 0.0).astype(MXU_DTYPE)
            vng = vn[:, sl].astype(MXU_DTYPE)
            mixed = _nn(wg, vng) + bc_ref[g]
            duv_ref[:, sl] = (dso[:, sl] * mixed * _gelu_grad(u[:, sl])).astype(MXU_DTYPE)
            dmix = dso[:, sl] * gu[:, sl]
            dmb = dmix.astype(MXU_DTYPE)
            dbc_ref[g] += jnp.sum(dmix, axis=-1, keepdims=True)
            dw_ref[g] += jnp.where(mask, _nt(dmb, vng), 0.0)
            dvn_ref[:, sl] = _tn(wg, dmb)
        dvn = dvn_ref[...]
        dg_ref[...] += jnp.sum(dvn * xh, axis=0, keepdims=True)
        db_ref[...] += jnp.sum(dvn, axis=0, keepdims=True)
        dxh = dvn * g_ref[...]
        m1 = jnp.mean(dxh, axis=-1, keepdims=True)
        m2 = jnp.mean(dxh * xh, axis=-1, keepdims=True)
        dgv = rstd * (dxh - m1 - xh * m2)
        duv_ref[:, c:] = (dgv * _gelu_grad(v)).astype(MXU_DTYPE)

    return pl.pallas_call(
        body, grid=(t // CHUNK,),
        in_specs=[_row_spec(CHUNK, c, 0), _row_spec(CHUNK, c, 1), _row_spec(CHUNK, c), _par_spec((1, c)),
                  _par_spec((1, c)), _par_spec((SG_GROUPS, CHUNK, CHUNK)), _par_spec((SG_GROUPS, CHUNK, 1)), _ANY],
        out_specs=[_row_spec(CHUNK, 2 * c), _par_spec((SG_GROUPS, CHUNK, CHUNK)), _par_spec((SG_GROUPS, CHUNK, 1)),
                   _par_spec((1, c)), _par_spec((1, c))],
        out_shape=[jax.ShapeDtypeStruct(dproj.shape, dproj.dtype), jax.ShapeDtypeStruct((SG_GROUPS, CHUNK, CHUNK), F32),
                   jax.ShapeDtypeStruct((SG_GROUPS, CHUNK, 1), F32), jax.ShapeDtypeStruct((1, c), F32),
                   jax.ShapeDtypeStruct((1, c), F32)],
        scratch_shapes=[pltpu.VMEM((CHUNK, c), F32)], input_output_aliases={7: 0},
        compiler_params=_params("arbitrary"), name=name)(proj, proj, dsgo, ln_g.reshape(1, c), ln_b.reshape(1, c), w, bcol, dproj)


CONV_TC = 512


def _conv_taps(x):
    rows = lax.broadcasted_iota(jnp.int32, x.shape, 0)
    taps = [jnp.where(rows >= SSM_CONV - 1 - k, pltpu.roll(x, SSM_CONV - 1 - k, axis=0), 0.0) for k in range(SSM_CONV - 1)]
    return taps + [x]


def _conv_pre(taps, w_ref, b_ref):
    acc = b_ref[...]
    for k in range(SSM_CONV):
        acc = acc + taps[k] * w_ref[k:k + 1, :]
    return acc


def _conv_fwd(proj, w, b, *, bsz, name):
    t = proj.shape[0]
    s = t // bsz
    nj = SSM_CONV_DIM // CONV_TC
    c0 = XBC_COL0 // CONV_TC

    def body(x_ref, w_ref, b_ref, o_ref):
        pre = _conv_pre(_conv_taps(x_ref[...].astype(F32)), w_ref, b_ref)
        o_ref[...] = (pre * _sigmoid(pre)).astype(o_ref.dtype)

    return pl.pallas_call(
        body, grid=(bsz, nj),
        in_specs=[pl.BlockSpec((s, CONV_TC), lambda bb, j: (bb, c0 + j)), pl.BlockSpec((SSM_CONV, CONV_TC), lambda bb, j: (0, j)),
                  pl.BlockSpec((1, CONV_TC), lambda bb, j: (0, j))],
        out_specs=pl.BlockSpec((s, CONV_TC), lambda bb, j: (bb, j)),
        out_shape=jax.ShapeDtypeStruct((t, SSM_CONV_DIM), STASH_DTYPE),
        compiler_params=_params("parallel", "parallel"), name=name)(proj, w, b.reshape(1, -1))


def _conv_bwd(proj, dact, w, b, dproj, *, bsz, name):
    t = proj.shape[0]
    s = t // bsz
    nj = SSM_CONV_DIM // CONV_TC
    c0 = XBC_COL0 // CONV_TC

    def body(x_ref, d_ref, w_ref, b_ref, _, dx_ref, dw_ref, db_ref):
        @pl.when(pl.program_id(1) == 0)
        def _():
            dw_ref[...] = jnp.zeros_like(dw_ref)
            db_ref[...] = jnp.zeros_like(db_ref)

        taps = _conv_taps(x_ref[...].astype(F32))
        pre = _conv_pre(taps, w_ref, b_ref)
        sg = _sigmoid(pre)
        dpre = d_ref[...].astype(F32) * (sg * (1.0 + pre * (1.0 - sg)))
        rows = lax.broadcasted_iota(jnp.int32, dpre.shape, 0)
        db_ref[...] += jnp.sum(dpre, axis=0, keepdims=True)
        dx = dpre * w_ref[SSM_CONV - 1:SSM_CONV, :]
        for k in range(SSM_CONV):
            dw_ref[k:k + 1, :] += jnp.sum(dpre * taps[k], axis=0, keepdims=True)
        for k in range(SSM_CONV - 1):
            sh = SSM_CONV - 1 - k
            dsh = jnp.where(rows < s - sh, pltpu.roll(dpre, s - sh, axis=0), 0.0)
            dx = dx + dsh * w_ref[k:k + 1, :]
        dx_ref[...] = dx.astype(MXU_DTYPE)

    return pl.pallas_call(
        body, grid=(nj, bsz),
        in_specs=[pl.BlockSpec((s, CONV_TC), lambda j, bb: (bb, c0 + j)), pl.BlockSpec((s, CONV_TC), lambda j, bb: (bb, j)),
                  pl.BlockSpec((SSM_CONV, CONV_TC), lambda j, bb: (0, j)), pl.BlockSpec((1, CONV_TC), lambda j, bb: (0, j)), _ANY],
        out_specs=[pl.BlockSpec((s, CONV_TC), lambda j, bb: (bb, c0 + j)), pl.BlockSpec((SSM_CONV, CONV_TC), lambda j, bb: (0, j)),
                   pl.BlockSpec((1, CONV_TC), lambda j, bb: (0, j))],
        out_shape=[jax.ShapeDtypeStruct(dproj.shape, dproj.dtype), jax.ShapeDtypeStruct((SSM_CONV, SSM_CONV_DIM), F32),
                   jax.ShapeDtypeStruct((1, SSM_CONV_DIM), F32)],
        input_output_aliases={4: 0},
        compiler_params=_params("parallel", "arbitrary"), name=name)(proj, dact, w, b.reshape(1, -1), dproj)


def _softplus(x):
    return jnp.maximum(x, 0.0) + jnp.log1p(jnp.exp(-jnp.abs(x)))


def _pad_heads(v):
    return jnp.broadcast_to(jnp.pad(v.astype(F32), (0, HEAD_PAD - SSM_HEADS))[None, :], (SUBLANE, HEAD_PAD))


def _ssd_prep(dt_raw, dt_bias8, a_log8, *, name):
    t = dt_raw.shape[0]
    n = CHUNK

    def body(r_ref, b_ref, al_ref, dt_ref, cs_ref, dtt_ref, cst_ref):
        dt = _softplus(r_ref[...] + b_ref[0:1, :])
        da = dt * (-jnp.exp(al_ref[0:1, :]))
        row = lax.broadcasted_iota(jnp.int32, (n, n), 0)
        col = lax.broadcasted_iota(jnp.int32, (n, n), 1)
        lower = (col <= row).astype(F32)
        upper = (row <= col).astype(F32)
        eye = (row == col).astype(F32)
        dt_ref[...] = dt
        cs_ref[...] = _dot_exact(lower, da, _DN_NN, 1)
        cst_ref[0] = _dot_exact(da, upper, _DN_TN, 0)
        dtt_ref[0] = _dot_exact(dt, eye, _DN_TN, 0)

    hp = HEAD_PAD
    return pl.pallas_call(
        body, grid=(t // n,),
        in_specs=[_row_spec(n, hp), _par_spec((SUBLANE, hp)), _par_spec((SUBLANE, hp))],
        out_specs=[_row_spec(n, hp), _row_spec(n, hp), pl.BlockSpec((1, hp, n), lambda i: (i, 0, 0)),
                   pl.BlockSpec((1, hp, n), lambda i: (i, 0, 0))],
        out_shape=[jax.ShapeDtypeStruct((t, hp), F32), jax.ShapeDtypeStruct((t, hp), F32),
                   jax.ShapeDtypeStruct((t // n, hp, n), F32), jax.ShapeDtypeStruct((t // n, hp, n), F32)],
        compiler_params=_params("parallel"), name=name)(dt_raw, dt_bias8, a_log8)


def _expand_mat():
    h = lax.broadcasted_iota(jnp.int32, (HEAD_PAD, SSM_INNER), 0)
    ch = lax.broadcasted_iota(jnp.int32, (HEAD_PAD, SSM_INNER), 1)
    return (ch // SSM_HEADDIM == h).astype(F32)


def _reduce_mat():
    ch = lax.broadcasted_iota(jnp.int32, (SSM_INNER, HEAD_PAD), 0)
    h = lax.broadcasted_iota(jnp.int32, (SSM_INNER, HEAD_PAD), 1)
    return (ch // SSM_HEADDIM == h).astype(F32)


def _expand(v, em):
    return _dot_exact(v, em, _DN_NN, 0)


def _expand_heads(v):
    return jnp.repeat(v.astype(F32), SSM_HEADDIM)[None, :]


def _decay_mat(cs_ref, cst_ref, h, mask):
    seg = cs_ref[:, h:h + 1] - cst_ref[0, h:h + 1, :]
    return jnp.where(mask, jnp.exp(jnp.minimum(seg, 0.0)), 0.0)


GROUP_CH = SSM_INNER // SSM_GROUPS
PAIRS_PER_GROUP = GROUP_CH // LANE
HEADS_PER_GROUP = SSM_HEADS // SSM_GROUPS
BM_COL0 = SSM_INNER
CM_COL0 = SSM_INNER + SSM_GROUPS * SSM_STATE


def _ssd_specs(nc, rev):
    def cidx(i):
        return (i // nc) * nc + (nc - 1 - i % nc) if rev else i

    n = CHUNK
    xs = pl.BlockSpec((n, SSM_INNER), lambda i: (cidx(i), 0))
    bm = pl.BlockSpec((n, GROUP_CH), lambda i: (cidx(i), BM_COL0 // GROUP_CH))
    cm = pl.BlockSpec((n, GROUP_CH), lambda i: (cidx(i), CM_COL0 // GROUP_CH))
    hv = pl.BlockSpec((n, HEAD_PAD), lambda i: (cidx(i), 0))
    hvt = pl.BlockSpec((1, HEAD_PAD, n), lambda i: (cidx(i), 0, 0))
    st = pl.BlockSpec((1, SSM_INNER, SSM_STATE), lambda i: (cidx(i), 0, 0))
    return xs, bm, cm, hv, hvt, st


def _ssd_fwd(xbc, dt, cs, dtt, cst, dskx, *, nc, name):
    t = xbc.shape[0]
    n = CHUNK
    xs_s, bm_s, cm_s, hv_s, hvt_s, st_s = _ssd_specs(nc, False)

    def body(xs_ref, bm_ref, cm_ref, dt_ref, cs_ref, dtt_ref, cst_ref, dsk_ref, y_ref, st_ref, prev):
        @pl.when(pl.program_id(0) % nc == 0)
        def _():
            prev[...] = jnp.zeros_like(prev)

        st_ref[0] = prev[...]
        em = _expand_mat()
        dtx = _expand(dt_ref[...], em)
        csx = _expand(cs_ref[...], em)
        dskx = dsk_ref[...]
        xs = xs_ref[...].astype(F32)
        xdt = xs * dtx
        ecs = jnp.exp(csx)
        dec = jnp.exp(csx[n - 1:n, :] - csx)
        mask = _causal(n)
        lane = lax.broadcasted_iota(jnp.int32, (n, LANE), 1)
        for g in range(SSM_GROUPS):
            gs = slice(g * SSM_STATE, (g + 1) * SSM_STATE)
            gc = slice(g * GROUP_CH, (g + 1) * GROUP_CH)
            cmat = cm_ref[:, gs].astype(MXU_DTYPE)
            bmat = bm_ref[:, gs].astype(MXU_DTYPE)
            cb = _nt(cmat, bmat)
            yoff = ecs[:, gc] * _nt(cmat, prev[gc, :].astype(MXU_DTYPE))
            for q in range(PAIRS_PER_GROUP):
                hp = g * PAIRS_PER_GROUP + q
                sl = slice(hp * LANE, (hp + 1) * LANE)
                xp = xdt[:, sl].astype(MXU_DTYPE)
                m0 = (cb * _decay_mat(cs_ref, cst_ref, 2 * hp, mask)).astype(MXU_DTYPE)
                m1 = (cb * _decay_mat(cs_ref, cst_ref, 2 * hp + 1, mask)).astype(MXU_DTYPE)
                yd = jnp.where(lane < SSM_HEADDIM, _nn(m0, xp), _nn(m1, xp))
                y_ref[:, sl] = (yd + yoff[:, q * LANE:(q + 1) * LANE] + xs[:, sl] * dskx[:, sl]).astype(y_ref.dtype)
            snew = _tn((xdt[:, gc] * dec[:, gc]).astype(MXU_DTYPE), bmat)
            for r in range(HEADS_PER_GROUP):
                h = g * HEADS_PER_GROUP + r
                rows = slice(h * SSM_HEADDIM, (h + 1) * SSM_HEADDIM)
                e = jnp.exp(cst_ref[0, h:h + 1, n - 1:n])
                prev[rows, :] = prev[rows, :] * e + snew[r * SSM_HEADDIM:(r + 1) * SSM_HEADDIM, :]

    return pl.pallas_call(
        body, grid=(t // n,),
        in_specs=[xs_s, bm_s, cm_s, hv_s, hv_s, hvt_s, hvt_s, _par_spec((1, SSM_INNER))],
        out_specs=[xs_s, st_s],
        out_shape=[jax.ShapeDtypeStruct((t, SSM_INNER), STASH_DTYPE), jax.ShapeDtypeStruct((t // n, SSM_INNER, SSM_STATE), F32)],
        scratch_shapes=[pltpu.VMEM((SSM_INNER, SSM_STATE), F32)],
        compiler_params=_params("arbitrary"), name=name)(xbc, xbc, xbc, dt, cs, dtt, cst, dskx)


def _ssd_bwd(dy, xbc, dt, cs, dtt, cst, st, dskx, a_log8, dt_raw, dt_bias8, *, nc, name):
    t = xbc.shape[0]
    n = CHUNK
    xs_s, bm_s, cm_s, hv_s, hvt_s, st_s = _ssd_specs(nc, True)
    acc_s = _par_spec((1, HEAD_PAD))
    xbc_s = pl.BlockSpec((n, SSM_CONV_DIM), xs_s.index_map)

    def body(dy_ref, xs_ref, bm_ref, cm_ref, dt_ref, cs_ref, dtt_ref, cst_ref, st_ref, dsk_ref, al_ref, raw_ref, bias_ref,
             dxbc_ref, ddr_ref, dal_ref, dds_ref, dbias_ref, dprev, dxdt_s, tdec_s, tcs_s):
        @pl.when(pl.program_id(0) % nc == 0)
        def _():
            dprev[...] = jnp.zeros_like(dprev)

        @pl.when(pl.program_id(0) == 0)
        def _():
            dal_ref[...] = jnp.zeros_like(dal_ref)
            dds_ref[...] = jnp.zeros_like(dds_ref)
            dbias_ref[...] = jnp.zeros_like(dbias_ref)

        em = _expand_mat()
        rm = _reduce_mat()

        def head_reduce(v):
            return _dot_exact(v, rm, _DN_NN, 0)

        dtv = dt_ref[...]
        csv = cs_ref[...]
        dtx = _expand(dtv, em)
        csx = _expand(csv, em)
        dskx = dsk_ref[...]
        xs = xs_ref[...].astype(F32)
        dyv = dy_ref[...].astype(F32)
        xdt = xs * dtx
        ecs = jnp.exp(csx)
        dec = jnp.exp(csx[n - 1:n, :] - csx)
        mask = _causal(n)
        lane = lax.broadcasted_iota(jnp.int32, (n, LANE), 1)
        hlane = lax.broadcasted_iota(jnp.int32, (1, HEAD_PAD), 1)
        hsub = lax.broadcasted_iota(jnp.int32, (HEAD_PAD, 1), 0)
        rsum = jnp.zeros((n, HEAD_PAD), F32)
        csum = jnp.zeros((HEAD_PAD, n), F32)
        for g in range(SSM_GROUPS):
            gs = slice(g * SSM_STATE, (g + 1) * SSM_STATE)
            gc = slice(g * GROUP_CH, (g + 1) * GROUP_CH)
            cmat = cm_ref[:, gs].astype(MXU_DTYPE)
            bmat = bm_ref[:, gs].astype(MXU_DTYPE)
            cb = _nt(cmat, bmat)
            pg = st_ref[0, gc, :].astype(MXU_DTYPE)
            dpg = dprev[gc, :]
            dpgb = dpg.astype(MXU_DTYPE)
            z = _nt(cmat, pg)
            dyg = dyv[:, gc]
            dz = (dyg * ecs[:, gc]).astype(MXU_DTYPE)
            dc = _nn(dz, pg)
            dprev_y = _tn(dz, cmat)
            tcs_s[:, gc] = dyg * z * ecs[:, gc]
            xd = xdt[:, gc] * dec[:, gc]
            wmat = _nt(bmat, dpgb)
            db = _nn(xd.astype(MXU_DTYPE), dpgb)
            tdec_s[:, gc] = wmat * xd
            dxdt_g = wmat * dec[:, gc]
            dcb = jnp.zeros((n, n), F32)
            for q in range(PAIRS_PER_GROUP):
                hp = g * PAIRS_PER_GROUP + q
                sl = slice(hp * LANE, (hp + 1) * LANE)
                xp = xdt[:, sl].astype(MXU_DTYPE)
                dyp = dyv[:, sl]
                dypb = dyp.astype(MXU_DTYPE)
                dxp = None
                for hh in range(2):
                    h = 2 * hp + hh
                    lm = _decay_mat(cs_ref, cst_ref, h, mask)
                    mine = (lane < SSM_HEADDIM) if hh == 0 else (lane >= SSM_HEADDIM)
                    dm = _nt(jnp.where(mine, dyp, 0.0).astype(MXU_DTYPE), xp)
                    dml = dm * lm
                    dcb = dcb + dml
                    gseg = dml * cb
                    rsum = rsum + jnp.sum(gseg, axis=1, keepdims=True) * (hlane == h).astype(F32)
                    csum = csum + (hsub == h).astype(F32) * jnp.sum(gseg, axis=0, keepdims=True)
                    dxh = _tn((cb * lm).astype(MXU_DTYPE), dypb)
                    dxp = dxh if dxp is None else jnp.where(mine, dxh, dxp)
                dxdt_s[:, sl] = dxdt_g[:, q * LANE:(q + 1) * LANE] + dxp
            dcbb = dcb.astype(MXU_DTYPE)
            dxbc_ref[:, CM_COL0 + g * SSM_STATE:CM_COL0 + (g + 1) * SSM_STATE] = (dc + _nn(dcbb, bmat)).astype(dxbc_ref.dtype)
            dxbc_ref[:, BM_COL0 + g * SSM_STATE:BM_COL0 + (g + 1) * SSM_STATE] = (db + _tn(dcbb, cmat)).astype(dxbc_ref.dtype)
            for r in range(HEADS_PER_GROUP):
                h = g * HEADS_PER_GROUP + r
                rows = slice(h * SSM_HEADDIM, (h + 1) * SSM_HEADDIM)
                lr = slice(r * SSM_HEADDIM, (r + 1) * SSM_HEADDIM)
                e = jnp.exp(cst_ref[0, h:h + 1, n - 1:n])
                dprev[rows, :] = dpg[lr, :] * e + dprev_y[lr, :]
            tq = _dot_exact(dpg * st_ref[0, gc, :], rm[gc, :], _DN_TN, 0)
            if g == 0:
                qsum = jnp.sum(tq, axis=0, keepdims=True)
            else:
                qsum = qsum + jnp.sum(tq, axis=0, keepdims=True)
        dxdt = dxdt_s[...]
        dxbc_ref[:, 0:SSM_INNER] = (dxdt * dtx + dyv * dskx).astype(dxbc_ref.dtype)
        ddt = head_reduce(dxdt * xs)
        edec = head_reduce(tdec_s[...])
        ycs = head_reduce(tcs_s[...])
        row = lax.broadcasted_iota(jnp.int32, (n, HEAD_PAD), 0)
        extra = jnp.sum(edec, axis=0, keepdims=True) + qsum * jnp.exp(csv[n - 1:n, :])
        dcs = rsum - csum.T + ycs - edec + jnp.where(row == n - 1, extra, 0.0)
        r2 = lax.broadcasted_iota(jnp.int32, (n, n), 0)
        c2 = lax.broadcasted_iota(jnp.int32, (n, n), 1)
        dda = _dot_exact((c2 >= r2).astype(F32), dcs, _DN_NN, 1)
        a_row = -jnp.exp(al_ref[0:1, :])
        ddt = ddt + dda * a_row
        dal_ref[...] += jnp.sum(dda * dtv, axis=0, keepdims=True) * a_row
        dds_ref[...] += jnp.sum(head_reduce(dyv * xs), axis=0, keepdims=True)
        ddr = ddt * _sigmoid(raw_ref[...] + bias_ref[0:1, :])
        ddr_ref[...] = ddr
        dbias_ref[...] += jnp.sum(ddr, axis=0, keepdims=True)

    par8 = _par_spec((SUBLANE, HEAD_PAD))
    return pl.pallas_call(
        body, grid=(t // n,),
        in_specs=[xs_s, xs_s, bm_s, cm_s, hv_s, hv_s, hvt_s, hvt_s, st_s, _par_spec((1, SSM_INNER)), par8, hv_s, par8],
        out_specs=[xbc_s, hv_s, acc_s, acc_s, acc_s],
        out_shape=[jax.ShapeDtypeStruct((t, SSM_CONV_DIM), STASH_DTYPE), jax.ShapeDtypeStruct((t, HEAD_PAD), F32),
                   jax.ShapeDtypeStruct((1, HEAD_PAD), F32), jax.ShapeDtypeStruct((1, HEAD_PAD), F32),
                   jax.ShapeDtypeStruct((1, HEAD_PAD), F32)],
        scratch_shapes=[pltpu.VMEM((SSM_INNER, SSM_STATE), F32), pltpu.VMEM((n, SSM_INNER), F32),
                        pltpu.VMEM((n, SSM_INNER), F32), pltpu.VMEM((n, SSM_INNER), F32)],
        compiler_params=_params("arbitrary"), name=name)(dy, xbc, xbc, xbc, dt, cs, dtt, cst, st, dskx, a_log8, dt_raw, dt_bias8)


def _gate_norm_fwd(y, proj, norm_g, *, name):
    t, c = y.shape
    tm = _pick(t, (256, 128))

    def body(y_ref, z_ref, g_ref, o_ref):
        z = z_ref[...].astype(F32)
        yz = y_ref[...].astype(F32) * z * _sigmoid(z)
        for g in range(SSM_GROUPS):
            gc = slice(g * GROUP_CH, (g + 1) * GROUP_CH)
            seg = yz[:, gc]
            r = lax.rsqrt(jnp.mean(seg * seg, axis=-1, keepdims=True) + RMS_EPS)
            o_ref[:, gc] = (seg * r * g_ref[:, gc]).astype(MXU_DTYPE)

    return pl.pallas_call(
        body, grid=(t // tm,), in_specs=[_row_spec(tm, c), _row_spec(tm, c, 1), _par_spec((1, c))],
        out_specs=_row_spec(tm, c), out_shape=jax.ShapeDtypeStruct((t, c), MXU_DTYPE),
        compiler_params=_params("parallel"), name=name)(y, proj, norm_g.reshape(1, c))


def _gate_norm_bwd(dyb, y, proj, norm_g, dproj, *, name):
    t, c = y.shape
    tm = _pick(t, (256, 128))

    def body(d_ref, y_ref, z_ref, g_ref, _, dy_ref, dz_ref, dg_ref):
        @pl.when(pl.program_id(0) == 0)
        def _():
            dg_ref[...] = jnp.zeros_like(dg_ref)

        z = z_ref[...].astype(F32)
        yv = y_ref[...].astype(F32)
        sz = _sigmoid(z)
        silu = z * sz
        yz = yv * silu
        dv = d_ref[...].astype(F32)
        for g in range(SSM_GROUPS):
            gc = slice(g * GROUP_CH, (g + 1) * GROUP_CH)
            seg = yz[:, gc]
            r = lax.rsqrt(jnp.mean(seg * seg, axis=-1, keepdims=True) + RMS_EPS)
            nrm = seg * r
            dn = dv[:, gc] * g_ref[:, gc]
            dg_ref[:, gc] += jnp.sum(dv[:, gc] * nrm, axis=0, keepdims=True)
            dyz = r * (dn - nrm * jnp.mean(dn * nrm, axis=-1, keepdims=True))
            dy_ref[:, gc] = (dyz * silu[:, gc]).astype(dy_ref.dtype)
            dz_ref[:, gc] = (dyz * yv[:, gc] * (sz[:, gc] * (1.0 + z[:, gc] * (1.0 - sz[:, gc])))).astype(MXU_DTYPE)

    return pl.pallas_call(
        body, grid=(t // tm,), in_specs=[_row_spec(tm, c), _row_spec(tm, c), _row_spec(tm, c, 1), _par_spec((1, c)), _ANY],
        out_specs=[_row_spec(tm, c), _row_spec(tm, c, 1), _par_spec((1, c))],
        out_shape=[jax.ShapeDtypeStruct((t, c), STASH_DTYPE), jax.ShapeDtypeStruct(dproj.shape, dproj.dtype),
                   jax.ShapeDtypeStruct((1, c), F32)],
        input_output_aliases={4: 1},
        compiler_params=_params("arbitrary"), name=name)(dyb, y, proj, norm_g.reshape(1, c), dproj)


GA_COLBLK = GAB_COL0 // D_MODEL


def _merge_fwd(br_a, br_b, proj, *, name):
    t, c = br_a.shape
    tm = _pick(t, ROW_TILES)

    def body(a_ref, b_ref, ga_ref, gb_ref, o_ref):
        o_ref[...] = (_sigmoid(ga_ref[...].astype(F32)) * a_ref[...].astype(F32)
                      + _sigmoid(gb_ref[...].astype(F32)) * b_ref[...].astype(F32)).astype(MXU_DTYPE)

    return pl.pallas_call(
        body, grid=(t // tm,),
        in_specs=[_row_spec(tm, c), _row_spec(tm, c), _row_spec(tm, c, GA_COLBLK), _row_spec(tm, c, GA_COLBLK + 1)],
        out_specs=_row_spec(tm, c), out_shape=jax.ShapeDtypeStruct((t, c), MXU_DTYPE),
        compiler_params=_params("parallel"), name=name)(br_a, br_b, proj, proj)


def _merge_bwd(dm, br_a, br_b, proj, *, name):
    t, c = br_a.shape
    tm = _pick(t, ROW_TILES)

    def body(dm_ref, a_ref, b_ref, ga_ref, gb_ref, da_ref, db_ref, dg_ref):
        d = dm_ref[...].astype(F32)
        sa = _sigmoid(ga_ref[...].astype(F32))
        sb = _sigmoid(gb_ref[...].astype(F32))
        da_ref[...] = (d * sa).astype(MXU_DTYPE)
        db_ref[...] = (d * sb).astype(MXU_DTYPE)
        dg_ref[:, :c] = (d * a_ref[...].astype(F32) * sa * (1.0 - sa)).astype(MXU_DTYPE)
        dg_ref[:, c:] = (d * b_ref[...].astype(F32) * sb * (1.0 - sb)).astype(MXU_DTYPE)

    return pl.pallas_call(
        body, grid=(t // tm,),
        in_specs=[_row_spec(tm, c), _row_spec(tm, c), _row_spec(tm, c), _row_spec(tm, c, GA_COLBLK), _row_spec(tm, c, GA_COLBLK + 1)],
        out_specs=[_row_spec(tm, c), _row_spec(tm, c), _row_spec(tm, 2 * c, GAB_COL0 // (2 * c))],
        out_shape=[jax.ShapeDtypeStruct((t, c), MXU_DTYPE), jax.ShapeDtypeStruct((t, c), MXU_DTYPE),
                   jax.ShapeDtypeStruct((t, MAIN_COLS), MXU_DTYPE)],
        compiler_params=_params("parallel"), name=name)(dm, br_a, br_b, proj, proj)


def _layer_fwd(x, xb, memn_b, w, *, bsz, tag):
    nc = x.shape[0] // bsz // CHUNK
    sv = {"x_in": xb}
    proj = _mm(xb, w["w_main"], out_dtype=STASH_DTYPE, name=f"{tag}_proj")
    dt_raw = _mm(xb, w["w_dt"], name=f"{tag}_dtproj")
    sgo = _sg_fwd(proj, w["sg_ln_g"], w["sg_ln_b"], w["sg_w"], w["sg_bcol"], name=f"{tag}_sg_fwd")
    xbc = _conv_fwd(proj, w["conv_w"], w["conv_b"], bsz=bsz, name=f"{tag}_conv_fwd")
    dt, cs, dtt, cst = _ssd_prep(dt_raw, w["dt_bias8"], w["a_log8"], name=f"{tag}_ssd_prep")
    y, st = _ssd_fwd(xbc, dt, cs, dtt, cst, w["d_skipx"], nc=nc, name=f"{tag}_ssd_fwd")
    yb = _gate_norm_fwd(y, proj, w["ssm_norm_g"], name=f"{tag}_gate_norm_fwd")
    if "rest" in w:
        w = w["rest"](w, yb)
    br_a = _mm(sgo, w["p_a"], out_dtype=STASH_DTYPE, name=f"{tag}_br_a")
    br_b = _mm(yb, w["p_b"], out_dtype=STASH_DTYPE, name=f"{tag}_br_b")
    merged = _merge_fwd(br_a, br_b, proj, name=f"{tag}_merge_fwd")
    mix = _mm(merged, w["w_mix_o"], name=f"{tag}_mix_o")
    x1, x1b, xh1, rs1 = _ln_fwd(x, mix, w["ln_g"][0], w["ln_b"][0], name=f"{tag}_ln1_fwd")
    sv.update(proj=proj, dt_raw=dt_raw, sgo=sgo, xbc=xbc, dt=dt, cs=cs, dtt=dtt, cst=cst, y=y, st=st, yb=yb,
              br_a=br_a, br_b=br_b, merged=merged, xh1=xh1, rs1=rs1, x1b=x1b)
    q = _mm(x1b, w["w_xq"], out_dtype=MXU_DTYPE, name=f"{tag}_q")
    kv = _mm(memn_b, w["w_xkv"], out_dtype=MXU_DTYPE, name=f"{tag}_kv")
    o = _attn_fwd(q, kv, bsz=bsz, name=f"{tag}_attn_fwd")
    att = _mm(o, w["w_xo"], name=f"{tag}_xo")
    x2, x2b, xh2, rs2 = _ln_fwd(x1, att, w["ln_g"][1], w["ln_b"][1], name=f"{tag}_ln2_fwd")
    sv.update(q=q, kv=kv, o=o, xh2=xh2, rs2=rs2, x2b=x2b)
    h = _mm(x2b, w["w_ffn_in"], out_dtype=STASH_DTYPE, name=f"{tag}_ffn_in")
    a = _swiglu_fwd(h, name=f"{tag}_swiglu_fwd")
    ffn = _mm(a, w["w_ffn_out"], name=f"{tag}_ffn_out")
    x3, x3b, xh3, rs3 = _ln_fwd(x2, ffn, w["ln_g"][2], w["ln_b"][2], name=f"{tag}_ln3_fwd")
    sv.update(h=h, a=a, xh3=xh3, rs3=rs3)
    return x3, x3b, sv, w


GRAD_GROUPS = (("w_ffn_out", "w_ffn_in", "w_xo", "w_xq", "w_xkv"), ("w_mix_o", "p_a", "p_b"), ("w_in",))


def _layer_bwd(dx3_addends, dx3_scales, memn_b, w, sv, on_group=None, *, bsz, tag):
    nc = sv["xh1"].shape[0] // bsz // CHUNK
    gr = {}

    def group_done(k):
        return on_group(GRAD_GROUPS[k], gr) if on_group is not None else None
    dp3, dp3b, dg3, db3 = _ln_bwd(dx3_addends, dx3_scales, sv["xh3"], sv["rs3"], w["ln_g"][2], name=f"{tag}_ln3_bwd")
    da = _mm(dp3b, w["w_ffn_out"], tb=True, out_dtype=STASH_DTYPE, name=f"{tag}_d_a")
    gr["w_ffn_out"] = _mm(sv["a"], dp3b, ta=True, name=f"{tag}_dw_ffn_out")
    dh = _swiglu_bwd(sv["h"], da, name=f"{tag}_swiglu_bwd")
    gr["w_ffn_in"] = _mm(sv["x2b"], dh, ta=True, name=f"{tag}_dw_ffn_in")
    dx2_br = _mm(dh, w["w_ffn_in"], tb=True, name=f"{tag}_dx2")
    dp2, dp2b, dg2, db2 = _ln_bwd([dp3, dx2_br], [ALPHA, 1.0], sv["xh2"], sv["rs2"], w["ln_g"][1], name=f"{tag}_ln2_bwd")
    do = _mm(dp2b, w["w_xo"], tb=True, out_dtype=MXU_DTYPE, name=f"{tag}_d_o")
    gr["w_xo"] = _mm(sv["o"], dp2b, ta=True, name=f"{tag}_dw_xo")
    dq, dk, dv = _attn_bwd(sv["q"], sv["kv"], do, bsz=bsz, name=f"{tag}_attn_bwd")
    dkv = jnp.concatenate([dk, dv], axis=1)
    gr["w_xq"] = _mm(sv["x1b"], dq, ta=True, name=f"{tag}_dw_xq")
    gr["w_xkv"] = _mm(memn_b, dkv, ta=True, name=f"{tag}_dw_xkv")
    dmemn = _mm(dkv, w["w_xkv"], tb=True, name=f"{tag}_d_memn")
    dx1_br = _mm(dq, w["w_xq"], tb=True, name=f"{tag}_dx1")
    token = group_done(0)
    ln_g1 = w["ln_g"][0] if token is None else w["ln_g"][0] + token[0, 0]
    dp1, dp1b, dg1, db1 = _ln_bwd([dp2, dx1_br], [ALPHA, 1.0], sv["xh1"], sv["rs1"], ln_g1, name=f"{tag}_ln1_bwd")
    gr["ln_g"] = jnp.concatenate([dg1, dg2, dg3], axis=0)
    gr["ln_b"] = jnp.concatenate([db1, db2, db3], axis=0)
    dmerged = _mm(dp1b, w["w_mix_o"], tb=True, out_dtype=STASH_DTYPE, name=f"{tag}_d_merged")
    gr["w_mix_o"] = _mm(sv["merged"], dp1b, ta=True, name=f"{tag}_dw_mix_o")
    dbr_a, dbr_b, dproj = _merge_bwd(dmerged, sv["br_a"], sv["br_b"], sv["proj"], name=f"{tag}_merge_bwd")
    gr["p_a"] = _mm(sv["sgo"], dbr_a, ta=True, name=f"{tag}_dw_p_a")
    gr["p_b"] = _mm(sv["yb"], dbr_b, ta=True, name=f"{tag}_dw_p_b")
    dsgo = _mm(dbr_a, w["p_a"], tb=True, out_dtype=STASH_DTYPE, name=f"{tag}_d_sgo")
    dyb = _mm(dbr_b, w["p_b"], tb=True, out_dtype=STASH_DTYPE, name=f"{tag}_d_yb")
    token = group_done(1)
    norm_g = w["ssm_norm_g"] if token is None else w["ssm_norm_g"] + token[0, 0]
    dy, dproj, gr["ssm_norm_g"] = _gate_norm_bwd(dyb, sv["y"], sv["proj"], norm_g, dproj, name=f"{tag}_gate_norm_bwd")
    dxbc, ddr, gr["a_log"], gr["d_skip"], gr["dt_bias"] = _ssd_bwd(
        dy, sv["xbc"], sv["dt"], sv["cs"], sv["dtt"], sv["cst"], sv["st"], w["d_skipx"], w["a_log8"], sv["dt_raw"],
        w["dt_bias8"], nc=nc, name=f"{tag}_ssd_bwd")
    dproj, gr["conv_w"], gr["conv_b"] = _conv_bwd(sv["proj"], dxbc, w["conv_w"], w["conv_b"], dproj, bsz=bsz, name=f"{tag}_conv_bwd")
    dproj, gr["sg_w"], dsg_bcol, gr["sg_ln_g"], gr["sg_ln_b"] = _sg_bwd(
        sv["proj"], dsgo, w["sg_ln_g"], w["sg_ln_b"], w["sg_w"], w["sg_bcol"], dproj, name=f"{tag}_sg_bwd")
    gr["sg_b"] = dsg_bcol[..., 0]
    gr["w_main"] = _mm(sv["x_in"], dproj, ta=True, name=f"{tag}_dw_main")
    gr["w_dt"] = _mm(sv["x_in"], ddr, ta=True, name=f"{tag}_dw_dt")
    token = group_done(2)
    dx_dt = _mm(ddr, w["w_dt"], tb=True, after=token, name=f"{tag}_dx_dt")
    dx_main = _mm(dproj, w["w_main"], tb=True, after=token, name=f"{tag}_dx_main")
    return [dp1, dx_main, dx_dt], [ALPHA, 1.0, 1.0], gr, dmemn


def _local_step(x, mem, tgt, mem_ln_g, mem_ln_b, layers, on_layer_grads=None):
    bsz, s, d = x.shape
    xf = x.reshape(bsz * s, d)
    memf = mem.reshape(-1, d)
    _, memn_b, mxh, mrs = _ln_fwd(memf, None, mem_ln_g, mem_ln_b, name="mem_ln_fwd")
    cur, curb, saved, weights = xf, xf, [], []
    for li, get_weights in enumerate(layers):
        cur, curb, sv, w = _layer_fwd(cur, curb, memn_b, get_weights(cur), bsz=bsz, tag=f"l{li}")
        saved.append(sv)
        weights.append(w)
    dy, lsum = _loss_head(cur, tgt.reshape(bsz * s, d), name="loss_head")
    addends, scales = [dy], [1.0]
    grads, dmem = [None] * len(layers), []
    for li in reversed(range(len(layers))):
        on_group = None if on_layer_grads is None else functools.partial(on_layer_grads, li)
        addends, scales, grads[li], dm = _layer_bwd(addends, scales, memn_b, weights[li], saved[li], on_group, bsz=bsz, tag=f"l{li}")
        dmem.append(dm)
    grad_x = _add_scaled(addends, scales, name="grad_x").reshape(bsz, s, d)
    _, _, dmg, dmb = _ln_bwd(dmem, [1.0] * len(dmem), mxh, mrs, mem_ln_g, name="mem_ln_bwd")
    return lsum, grad_x, grads, dmg[0], dmb[0]


_ANY = pl.BlockSpec(memory_space=pl.ANY)
_MESH = pl.DeviceIdType.MESH


def _all_gather8(x, *, name):
    def body(x_ref, out_ref, send_sems, recv_sems):
        mx, my, mc = lax.axis_index("x"), lax.axis_index("y"), lax.axis_index("c")
        me, sibling = (mx, my, mc), (mx, my, 1 - mc)
        chips = [(1 - mx, my), (mx, 1 - my), (1 - mx, 1 - my)]

        def blk(px, py, pc):
            return out_ref.at[4 * px + 2 * py + pc]

        def copy(k, block, to, src=None):
            return pltpu.make_async_remote_copy(
                src_ref=blk(*block) if src is None else src, dst_ref=blk(*block), send_sem=send_sems.at[k],
                recv_sem=recv_sems.at[k], device_id=to, device_id_type=_MESH)

        first = [copy(0, me, sibling, src=x_ref)]
        first += [copy(1 + j, me, (*chip, mc), src=x_ref) for j, chip in enumerate(chips)]
        for cp in first:
            cp.start()
        passed = [copy(4 + j, (*chip, mc), sibling) for j, chip in enumerate(chips)]
        for j, chip in enumerate(chips):
            copy(1 + j, (*chip, mc), me).wait_recv()
            passed[j].start()
        copy(0, sibling, me).wait_recv()
        for j, chip in enumerate(chips):
            copy(4 + j, (*chip, 1 - mc), me).wait_recv()
        for cp in first + passed:
            cp.wait_send()

    return pl.pallas_call(
        body, out_shape=jax.ShapeDtypeStruct((N_DEV,) + x.shape, x.dtype), in_specs=[_ANY], out_specs=_ANY,
        scratch_shapes=[pltpu.SemaphoreType.DMA((7,)), pltpu.SemaphoreType.DMA((7,))], name=name)(x)


def _row_tile(rows, row_bytes, mult=SUBLANE):
    best = None
    for tr in range(mult, rows + 1, mult):
        if rows % tr == 0 and (best is None or tr * row_bytes <= BLOCK_BYTES):
            best = tr
    return rows if best is None else best


def _gather_shape(r, c, kind):
    return {"row": (2, N_CHIPS * r, c), "col": (2, r, N_CHIPS * c), "chip": (2, N_CHIPS, r, c)}[kind]


def _cast_place(shard, kind, dtype, chip_idx, *, name):
    _, r, c = shard.shape
    tr = _row_tile(r, c * 4, 16)
    nt = r // tr

    def body(_, s_ref, o_ref):
        o_ref[...] = s_ref[...].astype(dtype)

    if kind == "row":
        out_spec = pl.BlockSpec((None, tr, c), lambda l, i, j_ref: (l, j_ref[0] * nt + i, 0))
    elif kind == "col":
        out_spec = pl.BlockSpec((None, tr, c), lambda l, i, j_ref: (l, i, j_ref[0]))
    else:
        out_spec = pl.BlockSpec((None, None, tr, c), lambda l, i, j_ref: (l, j_ref[0], i, 0))
    grid_spec = pltpu.PrefetchScalarGridSpec(
        num_scalar_prefetch=1, grid=(2, nt), in_specs=[pl.BlockSpec((None, tr, c), lambda l, i, j_ref: (l, i, 0))],
        out_specs=out_spec)
    return pl.pallas_call(body, grid_spec=grid_spec, out_shape=jax.ShapeDtypeStruct(_gather_shape(r, c, kind), dtype),
                          compiler_params=_params("parallel", "parallel"), name=name)(chip_idx, shard)


def _gather_params(bufs, shard_shapes, kinds, *, name):
    n = len(bufs)

    def body(*refs):
        outs = refs[n:2 * n]
        send_sems, recv_sems = refs[2 * n:]
        mx, my, mc = lax.axis_index("x"), lax.axis_index("y"), lax.axis_index("c")
        me, sibling = (mx, my, mc), (mx, my, 1 - mc)
        chips = [(1 - mx, my), (mx, 1 - my), (1 - mx, 1 - my)]

        def blk(i, px, py, pc):
            r, c = shard_shapes[i]
            j = 2 * px + py
            if kinds[i] == "row":
                return outs[i].at[pc, pl.ds(pl.multiple_of(j * r, r), r)]
            if kinds[i] == "col":
                return outs[i].at[pc, :, pl.ds(pl.multiple_of(j * c, c), c)]
            return outs[i].at[pc, j]

        def copy(i, k, block, to):
            return pltpu.make_async_remote_copy(
                src_ref=blk(i, *block), dst_ref=blk(i, *block), send_sem=send_sems.at[6 * i + k],
                recv_sem=recv_sems.at[6 * i + k], device_id=to, device_id_type=_MESH)

        sent = []
        for i in range(n):
            for j, chip in enumerate(chips):
                cp = copy(i, j, me, (*chip, mc))
                cp.start()
                sent.append(cp)
        for j, chip in enumerate(chips):
            for i in range(n):
                copy(i, j, (*chip, mc), me).wait_recv()
                fwd = copy(i, 3 + j, (*chip, mc), sibling)
                fwd.start()
                sent.append(fwd)
        for i in range(n):
            for j, chip in enumerate(chips):
                copy(i, 3 + j, (*chip, 1 - mc), me).wait_recv()
        for cp in sent:
            cp.wait_send()

    return pl.pallas_call(
        body, out_shape=[jax.ShapeDtypeStruct(b.shape, b.dtype) for b in bufs], in_specs=[_ANY] * n, out_specs=[_ANY] * n,
        input_output_aliases={i: i for i in range(n)},
        scratch_shapes=[pltpu.SemaphoreType.DMA((6 * n,)), pltpu.SemaphoreType.DMA((6 * n,))], name=name)(*bufs)


def _half(r, h):
    return pl.ds(pl.multiple_of(h * (r // 2), r // 2), r // 2)


_HBM = pl.BlockSpec(memory_space=pltpu.HBM)
_SEM = pl.BlockSpec(memory_space=pltpu.SEMAPHORE)
_EFFECT = pltpu.SideEffectType.DATAFLOW_SIDE_EFFECTING


def _sibling_copies(g_refs, land_refs, gs, views, send_sems, recv_sems):
    mx, my, mc = lax.axis_index("x"), lax.axis_index("y"), lax.axis_index("c")
    copies = []
    for i in range(len(gs)):
        if views[i] == "chip":
            src = g_refs[i].at[:, _half(gs[i].shape[1], 1 - mc)]
        else:
            src = g_refs[i].at[_half(gs[i].shape[0], 1 - mc)]
        copies.append(pltpu.make_async_remote_copy(src_ref=src, dst_ref=land_refs[i], send_sem=send_sems.at[i], recv_sem=recv_sems.at[i],
                                                   device_id=(mx, my, 1 - mc), device_id_type=_MESH))
    return copies


def _half_shape(g, view):
    return (g.shape[0], g.shape[1] // 2, g.shape[2]) if view == "chip" else (g.shape[0] // 2, g.shape[1])


def _grads_to_sibling_start(gs, views, *, name):
    n = len(gs)
    lands = [pltpu.with_memory_space_constraint(lax.empty(_half_shape(g, v), g.dtype), pltpu.HBM) for g, v in zip(gs, views)]

    def body(*refs):
        for cp in _sibling_copies(refs[:n], refs[n:2 * n], gs, views, refs[2 * n], refs[2 * n + 1]):
            cp.start()
        refs[-1][...] = jnp.zeros_like(refs[-1])

    outs = pl.pallas_call(
        body, name=name,
        out_shape=(pltpu.SemaphoreType.DMA((n,)), pltpu.SemaphoreType.DMA((n,)),
                   *[pltpu.HBM(x.shape, x.dtype) for x in list(gs) + lands], jax.ShapeDtypeStruct((SUBLANE, LANE), F32)),
        in_specs=[_HBM] * (2 * n), out_specs=(_SEM, _SEM, *[_HBM] * (2 * n), pl.BlockSpec(memory_space=pltpu.VMEM)),
        input_output_aliases={i: 2 + i for i in range(2 * n)},
        compiler_params=pltpu.CompilerParams(has_side_effects=_EFFECT),
    )(*[pltpu.with_memory_space_constraint(g, pltpu.HBM) for g in gs], *lands)
    return outs[0], outs[1], list(outs[2:2 + n]), list(outs[2 + n:2 + 2 * n]), outs[-1]


def _grads_to_sibling_wait(send_sems, recv_sems, gs, lands, views, after, *, name):
    n = len(gs)

    def body(*refs):
        for cp in _sibling_copies(refs[:n], refs[n:2 * n], gs, views, refs[2 * n], refs[2 * n + 1]):
            cp.wait_send()
            cp.wait_recv()

    outs = pl.pallas_call(
        body, name=name, out_shape=tuple(pltpu.HBM(x.shape, x.dtype) for x in list(gs) + list(lands)),
        in_specs=[_HBM] * (2 * n) + [_SEM, _SEM, _ANY], out_specs=tuple([_HBM] * (2 * n)),
        input_output_aliases={i: i for i in range(2 * n)},
        compiler_params=pltpu.CompilerParams(has_side_effects=_EFFECT),
    )(*gs, *lands, send_sems, recv_sems, after)
    return list(outs[:n]), list(outs[n:])


def _cast_place_layer(shard, l, kind, chip_idx, after, *, name):
    _, r, c = shard.shape
    tr = _row_tile(r, c * 4, 16)
    nt = r // tr

    def body(_, s_ref, *rest):
        rest[-1][...] = s_ref[...].astype(MXU_DTYPE)

    if kind == "row":
        out_spec = pl.BlockSpec((tr, c), lambda i, j_ref: (j_ref[0] * nt + i, 0))
    elif kind == "col":
        out_spec = pl.BlockSpec((tr, c), lambda i, j_ref: (i, j_ref[0]))
    else:
        out_spec = pl.BlockSpec((None, tr, c), lambda i, j_ref: (j_ref[0], i, 0))
    extra = [] if after is None else [after]
    grid_spec = pltpu.PrefetchScalarGridSpec(
        num_scalar_prefetch=1, grid=(nt,), in_specs=[pl.BlockSpec((None, tr, c), lambda i, j_ref: (l, i, 0))] + [_ANY] * len(extra),
        out_specs=out_spec)
    return pl.pallas_call(body, grid_spec=grid_spec, out_shape=jax.ShapeDtypeStruct(_gather_shape(r, c, kind)[1:], MXU_DTYPE),
                          compiler_params=_params("parallel"), name=name)(chip_idx, shard, *extra)


def _half_block(ref, kind, r, c, j, h):
    rows = _half(r, h)
    if kind == "row":
        return ref.at[pl.ds(pl.multiple_of(j * r + h * (r // 2), r // 2), r // 2)]
    if kind == "col":
        return ref.at[rows, pl.ds(pl.multiple_of(j * c, c), c)]
    return ref.at[j, rows]


def _gather_ici_copies(buf_refs, shapes, kinds, send_sems, recv_sems):
    mx, my, mc = lax.axis_index("x"), lax.axis_index("y"), lax.axis_index("c")
    chips = [(1 - mx, my), (mx, 1 - my), (1 - mx, 1 - my)]
    copies = []
    for i, (r, c) in enumerate(shapes):
        mine = _half_block(buf_refs[i], kinds[i], r, c, 2 * mx + my, mc)
        for k, (px, py) in enumerate(chips):
            copies.append(pltpu.make_async_remote_copy(
                src_ref=mine, dst_ref=mine, send_sem=send_sems.at[3 * i + k], recv_sem=recv_sems.at[3 * i + k],
                device_id=(px, py, mc), device_id_type=_MESH))
    return copies


def _gather_start(bufs, shapes, kinds, *, name):
    n = len(bufs)

    def body(*refs):
        send_sems, recv_sems, token = refs[n], refs[n + 1], refs[-1]
        for cp in _gather_ici_copies(refs[:n], shapes, kinds, send_sems, recv_sems):
            cp.start()
        token[...] = jnp.zeros_like(token)

    outs = pl.pallas_call(
        body, name=name,
        out_shape=(pltpu.SemaphoreType.DMA((3 * n,)), pltpu.SemaphoreType.DMA((3 * n,)),
                   *[pltpu.HBM(b.shape, b.dtype) for b in bufs], jax.ShapeDtypeStruct((SUBLANE, LANE), F32)),
        in_specs=[_HBM] * n, out_specs=(_SEM, _SEM, *[_HBM] * n, pl.BlockSpec(memory_space=pltpu.VMEM)),
        input_output_aliases={i: 2 + i for i in range(n)},
        compiler_params=pltpu.CompilerParams(has_side_effects=_EFFECT),
    )(*[pltpu.with_memory_space_constraint(b, pltpu.HBM) for b in bufs])
    return outs[0], outs[1], list(outs[2:2 + n]), outs[-1]


def _gather_wait(send_sems, recv_sems, bufs, shapes, kinds, after, *, name):
    n = len(bufs)

    def body(*refs):
        for cp in _gather_ici_copies(refs[:n], shapes, kinds, refs[n], refs[n + 1]):
            cp.wait_send()
            cp.wait_recv()

    outs = pl.pallas_call(
        body, name=name, out_shape=tuple(pltpu.HBM(b.shape, b.dtype) for b in bufs),
        in_specs=[_HBM] * n + [_SEM, _SEM, _ANY], out_specs=tuple([_HBM] * n), input_output_aliases={i: i for i in range(n)},
        compiler_params=pltpu.CompilerParams(has_side_effects=_EFFECT),
    )(*bufs, send_sems, recv_sems, after)
    return list(outs)


def _gather_forward(bufs, shapes, kinds, *, name):
    n = len(bufs)

    def body(*refs):
        outs = refs[n:2 * n]
        send_sems, recv_sems = refs[2 * n:]
        mx, my, mc = lax.axis_index("x"), lax.axis_index("y"), lax.axis_index("c")
        chips = [(1 - mx, my), (mx, 1 - my), (1 - mx, 1 - my)]
        copies = []
        for i, (r, c) in enumerate(shapes):
            for k, (px, py) in enumerate(chips):
                got = _half_block(outs[i], kinds[i], r, c, 2 * px + py, mc)
                cp = pltpu.make_async_remote_copy(src_ref=got, dst_ref=got, send_sem=send_sems.at[3 * i + k],
                                                  recv_sem=recv_sems.at[3 * i + k], device_id=(mx, my, 1 - mc), device_id_type=_MESH)
                cp.start()
                copies.append(cp)
        for cp in copies:
            cp.wait()

    return pl.pallas_call(
        body, out_shape=[jax.ShapeDtypeStruct(b.shape, b.dtype) for b in bufs], in_specs=[_ANY] * n, out_specs=[_ANY] * n,
        input_output_aliases={i: i for i in range(n)},
        scratch_shapes=[pltpu.SemaphoreType.DMA((3 * n,)), pltpu.SemaphoreType.DMA((3 * n,))], name=name)(*bufs)


def _chip_exchange_copies(pair_refs, land_refs, pairs, views, send_sems, recv_sems):
    mx, my, mc = lax.axis_index("x"), lax.axis_index("y"), lax.axis_index("c")
    me = 2 * mx + my
    chips = [(1 - mx, my), (mx, 1 - my), (1 - mx, 1 - my)]
    copies = []
    for i in range(len(pairs)):
        for k, (px, py) in enumerate(chips):
            j = 2 * px + py
            if views[i] == "chip":
                src = pair_refs[i].at[j]
            else:
                c = pairs[i].shape[1] // N_CHIPS
                src = pair_refs[i].at[:, pl.ds(pl.multiple_of(j * c, c), c)]
            copies.append(pltpu.make_async_remote_copy(
                src_ref=src, dst_ref=land_refs[i].at[me], send_sem=send_sems.at[3 * i + k], recv_sem=recv_sems.at[3 * i + k],
                device_id=(px, py, mc), device_id_type=_MESH))
    return copies


def _quad_shape(p, view):
    return p.shape if view == "chip" else (N_CHIPS, p.shape[0], p.shape[1] // N_CHIPS)


def _grads_to_chips_start(pairs, views, *, name):
    n = len(pairs)
    lands = [pltpu.with_memory_space_constraint(lax.empty(_quad_shape(p, v), p.dtype), pltpu.HBM) for p, v in zip(pairs, views)]

    def body(*refs):
        pair_refs, land_refs = refs[:n], refs[n:2 * n]
        send_sems, recv_sems = refs[2 * n], refs[2 * n + 1]
        token = refs[-1]
        for cp in _chip_exchange_copies(pair_refs, land_refs, pairs, views, send_sems, recv_sems):
            cp.start()
        token[...] = jnp.zeros_like(token)

    outs = pl.pallas_call(
        body, name=name,
        out_shape=(pltpu.SemaphoreType.DMA((3 * n,)), pltpu.SemaphoreType.DMA((3 * n,)),
                   *[pltpu.HBM(p.shape, p.dtype) for p in pairs], *[pltpu.HBM(l.shape, l.dtype) for l in lands],
                   jax.ShapeDtypeStruct((SUBLANE, LANE), F32)),
        in_specs=[_HBM] * (2 * n), out_specs=(_SEM, _SEM, *[_HBM] * (2 * n), pl.BlockSpec(memory_space=pltpu.VMEM)),
        input_output_aliases={i: 2 + i for i in range(2 * n)},
        compiler_params=pltpu.CompilerParams(has_side_effects=_EFFECT),
    )(*[pltpu.with_memory_space_constraint(p, pltpu.HBM) for p in pairs], *lands)
    return outs[0], outs[1], list(outs[2:2 + n]), list(outs[2 + n:2 + 2 * n]), outs[-1]


def _grads_to_chips_wait(send_sems, recv_sems, pairs, lands, views, after, *, name):
    n = len(pairs)

    def body(*refs):
        pair_refs, land_refs = refs[:n], refs[n:2 * n]
        s_sems, r_sems = refs[2 * n], refs[2 * n + 1]
        for cp in _chip_exchange_copies(pair_refs, land_refs, pairs, views, s_sems, r_sems):
            cp.wait_send()
            cp.wait_recv()

    outs = pl.pallas_call(
        body, name=name, out_shape=tuple(pltpu.HBM(x.shape, x.dtype) for x in list(pairs) + list(lands)),
        in_specs=[_HBM] * (2 * n) + [_SEM, _SEM, _ANY], out_specs=tuple([_HBM] * (2 * n)),
        input_output_aliases={i: i for i in range(2 * n)},
        compiler_params=pltpu.CompilerParams(has_side_effects=_EFFECT),
    )(*pairs, *lands, send_sems, recv_sems, after)
    return list(outs[n:])


def _grads_share(tots, *, name):
    n = len(tots)

    def body(*refs):
        ins, outs = refs[:n], refs[n:2 * n]
        send_sems, recv_sems = refs[2 * n:]
        mx, my, mc = lax.axis_index("x"), lax.axis_index("y"), lax.axis_index("c")
        copies = []
        for i in range(n):
            cp = pltpu.make_async_remote_copy(src_ref=ins[i], dst_ref=outs[i], send_sem=send_sems.at[i], recv_sem=recv_sems.at[i],
                                              device_id=(mx, my, 1 - mc), device_id_type=_MESH)
            cp.start()
            copies.append(cp)
        for cp in copies:
            cp.wait()

    return pl.pallas_call(
        body, out_shape=[jax.ShapeDtypeStruct(t.shape, t.dtype) for t in tots], in_specs=[_ANY] * n, out_specs=[_ANY] * n,
        scratch_shapes=[pltpu.SemaphoreType.DMA((n,)), pltpu.SemaphoreType.DMA((n,))], name=name)(*tots)


def _pair_sum(g, recv, view, c_idx, *, name):
    def body(c_ref, a_ref, b_ref, o_ref):
        o_ref[...] = (a_ref[...] + b_ref[...]).astype(WIRE_DTYPE)

    if view == "chip":
        nch, r, c = g.shape
        tr = _row_tile(r // 2, nch * c * 4, 16)
        gv = g.reshape(nch, 2, r // 2, c)
        grid = ((r // 2) // tr,)
        in_specs = [pl.BlockSpec((nch, None, tr, c), lambda i, c_ref: (0, c_ref[0], i, 0)),
                    pl.BlockSpec((nch, tr, c), lambda i, c_ref: (0, i, 0))]
        out_spec = pl.BlockSpec((nch, tr, c), lambda i, c_ref: (0, i, 0))
        sem = ("parallel",)
    else:
        r, c4 = g.shape
        tr = _row_tile(r // 2, c4 * 4, 16)
        gv = g.reshape(2, r // 2, c4)
        grid = ((r // 2) // tr,)
        in_specs = [pl.BlockSpec((None, tr, c4), lambda i, c_ref: (c_ref[0], i, 0)), pl.BlockSpec((tr, c4), lambda i, c_ref: (i, 0))]
        out_spec = pl.BlockSpec((tr, c4), lambda i, c_ref: (i, 0))
        sem = ("parallel",)
    grid_spec = pltpu.PrefetchScalarGridSpec(num_scalar_prefetch=1, grid=grid, in_specs=in_specs, out_specs=out_spec)
    return pl.pallas_call(body, grid_spec=grid_spec, out_shape=jax.ShapeDtypeStruct(recv.shape, WIRE_DTYPE),
                          compiler_params=_params(*sem), name=name)(c_idx, gv, recv)


def _quad_sum(gs, recvs, quads, view, chip_idx, c_idx, *, name):
    nl = len(quads)
    nch, rh, c = quads[0].shape
    tr = _row_tile(rh, c * 4, 16)

    def body(_, __, *refs):
        o_ref = refs[-1]
        per = nch + 1
        for l in range(nl):
            grp = refs[l * per:(l + 1) * per]
            acc = grp[0][...] + grp[1][...]
            for r in grp[2:]:
                acc = acc + r[...].astype(F32)
            o_ref[l] = acc

    if view == "chip":
        own = [pl.BlockSpec((None, None, tr, c), lambda i, j, h: (j[0], h[0], i, 0)),
               pl.BlockSpec((None, tr, c), lambda i, j, h: (j[0], i, 0))]
        gviews = [g.reshape(nch, 2, rh, c) for g in gs]
    else:
        own = [pl.BlockSpec((None, tr, c), lambda i, j, h: (h[0], i, j[0])), pl.BlockSpec((tr, c), lambda i, j, h: (i, j[0]))]
        gviews = [g.reshape(2, rh, nch * c) for g in gs]
    assert nch & (nch - 1) == 0
    got = [pl.BlockSpec((None, tr, c), functools.partial(lambda i, j, h, k: ((j[0] + k) & (nch - 1), i, 0), k=k))
           for k in range(1, nch)]
    ins = []
    for l in range(nl):
        ins += [gviews[l], recvs[l]] + [quads[l]] * (nch - 1)
    grid_spec = pltpu.PrefetchScalarGridSpec(
        num_scalar_prefetch=2, grid=(rh // tr,), in_specs=(own + got) * nl,
        out_specs=pl.BlockSpec((nl, tr, c), lambda i, j, h: (0, i, 0)))
    return pl.pallas_call(body, grid_spec=grid_spec, out_shape=jax.ShapeDtypeStruct((nl, rh, c), F32),
                          compiler_params=_params("parallel"), name=name)(chip_idx, c_idx, *ins)


def _sum_devices(g8, own, dev_idx, *, name):
    k, rows, cols = g8.shape

    def body(d_ref, a_ref, x_ref, o_ref):
        acc = None
        for i in range(k):
            term = jnp.where(d_ref[0] == i, x_ref[...], a_ref[i])
            acc = term if acc is None else acc + term
        o_ref[...] = acc

    grid_spec = pltpu.PrefetchScalarGridSpec(
        num_scalar_prefetch=1, grid=(1,),
        in_specs=[pl.BlockSpec((k, rows, cols), lambda i, d_ref: (0, 0, 0)), pl.BlockSpec((rows, cols), lambda i, d_ref: (0, 0))],
        out_specs=pl.BlockSpec((rows, cols), lambda i, d_ref: (0, 0)))
    return pl.pallas_call(body, grid_spec=grid_spec, out_shape=jax.ShapeDtypeStruct((rows, cols), g8.dtype),
                          compiler_params=_params("arbitrary"), name=name)(dev_idx, g8, own)


def _adamw(w, g, m, v, *, name):
    rows, cols = w.shape
    tr = rows
    for cand in (256, 128, 64, 32, 16, 8):
        if rows % cand == 0 and cand * cols <= 512 * 1024:
            tr = cand
            break
    c1 = 1.0 - ADAM_B1 ** ADAM_STEP
    c2 = 1.0 - ADAM_B2 ** ADAM_STEP

    def body(w_ref, g_ref, m_ref, v_ref, d_ref, nm_ref, nv_ref):
        gv = g_ref[...]
        nm = ADAM_B1 * m_ref[...] + (1.0 - ADAM_B1) * gv
        nv = ADAM_B2 * v_ref[...] + (1.0 - ADAM_B2) * (gv * gv)
        d_ref[...] = -ADAM_LR * ((nm / c1) / (jnp.sqrt(nv / c2) + ADAM_EPS) + ADAM_WD * w_ref[...])
        nm_ref[...] = nm
        nv_ref[...] = nv

    spec = pl.BlockSpec((tr, cols), lambda i: (i, 0))
    shp = jax.ShapeDtypeStruct((rows, cols), F32)
    return pl.pallas_call(body, grid=(rows // tr,), in_specs=[spec] * 4, out_specs=[spec] * 3, out_shape=[shp] * 3,
                          compiler_params=_params("parallel"), name=name)(w, g, m, v)


def _adamw_halves(w, m, v, mine, other, c_idx, *, name):
    nl, r, c = w.shape
    rh = r // 2
    tr = _row_tile(rh, c * 4)
    c1 = 1.0 - ADAM_B1 ** ADAM_STEP
    c2 = 1.0 - ADAM_B2 ** ADAM_STEP

    def body(c_ref, w_ref, m_ref, v_ref, a_ref, b_ref, g_ref, d_ref, nm_ref, nv_ref):
        gv = jnp.where(pl.program_id(1) == c_ref[0], a_ref[...], b_ref[...])
        nm = ADAM_B1 * m_ref[...] + (1.0 - ADAM_B1) * gv
        nv = ADAM_B2 * v_ref[...] + (1.0 - ADAM_B2) * (gv * gv)
        g_ref[...] = gv
        d_ref[...] = -ADAM_LR * ((nm / c1) / (jnp.sqrt(nv / c2) + ADAM_EPS) + ADAM_WD * w_ref[...])
        nm_ref[...] = nm
        nv_ref[...] = nv

    full = pl.BlockSpec((None, None, tr, c), lambda l, h, i, c_ref: (l, h, i, 0))
    half_mine = pl.BlockSpec((None, tr, c), lambda l, h, i, c_ref: (l, jnp.where(h == c_ref[0], i, 0), 0))
    half_other = pl.BlockSpec((None, tr, c), lambda l, h, i, c_ref: (l, jnp.where(h == c_ref[0], 0, i), 0))
    grid_spec = pltpu.PrefetchScalarGridSpec(num_scalar_prefetch=1, grid=(nl, 2, rh // tr),
                                             in_specs=[full] * 3 + [half_mine, half_other], out_specs=[full] * 4)
    shp = jax.ShapeDtypeStruct((nl, 2, rh, c), F32)
    view = (nl, 2, rh, c)
    outs = pl.pallas_call(body, grid_spec=grid_spec, out_shape=[shp] * 4, compiler_params=_params("arbitrary", "arbitrary", "arbitrary"),
                          name=name)(c_idx, w.reshape(view), m.reshape(view), v.reshape(view), mine, other)
    return [o.reshape(nl, r, c) for o in outs]


WEIGHTS = ["mem_ln_g", "mem_ln_b", "w_in", "sg_ln_g", "sg_ln_b", "sg_w", "sg_b", "conv_w", "conv_b", "dt_bias", "a_log",
           "d_skip", "ssm_norm_g", "p_a", "p_b", "w_mix_o", "w_xq", "w_xkv", "w_xo", "w_ffn_in", "w_ffn_out", "ln_g", "ln_b"]
ARG_NAMES = ["x", "mem"] + WEIGHTS + ["loss_target"] + ["m_" + n for n in WEIGHTS] + ["v_" + n for n in WEIGHTS]
BIG = {"w_in": (1, (1024, 9248)), "p_a": (0, (1024, 1024)), "p_b": (0, (2048, 1024)), "w_mix_o": (0, (1024, 1024)),
       "w_xq": (0, (1024, 1024)), "w_xkv": (1, (1024, 2048)), "w_xo": (0, (1024, 1024)), "w_ffn_in": (1, (1024, 5632)),
       "w_ffn_out": (0, (2816, 1024))}
SMALL_SHARDED = {"conv_w": (4, 3072), "ln_g": (3, 1024), "ln_b": (3, 1024)}
SMALL = [n for n in WEIGHTS if n not in BIG]
W_IN_MAP = ((0, 4096, "main", 0), (4096, 7168, "main", XBC_COL0), (7168, 7200, "dt", 0), (7200, 9248, "main", GAB_COL0))
W_IN_SHARD = 9248 // N_CHIPS


def _w_in_chip_major(gm, gd):
    src = {"main": gm, "dt": gd}
    blocks = []
    for j in range(N_CHIPS):
        lo, hi = j * W_IN_SHARD, (j + 1) * W_IN_SHARD
        parts = [src[k][:, o + max(lo, a) - a:o + min(hi, b) - a] for a, b, k, o in W_IN_MAP if max(lo, a) < min(hi, b)]
        blocks.append(jnp.concatenate(parts, axis=1))
    return jnp.stack(blocks)


def _w_in_reassemble(wc):
    def cols(a, b):
        out = []
        for j in range(N_CHIPS):
            lo, hi = max(a, j * W_IN_SHARD), min(b, (j + 1) * W_IN_SHARD)
            if lo < hi:
                out.append(wc[j][:, lo - j * W_IN_SHARD:hi - j * W_IN_SHARD])
        return out

    main = sorted((m for m in W_IN_MAP if m[2] == "main"), key=lambda m: m[3])
    w_main = jnp.concatenate([p for a, b, _, _ in main for p in cols(a, b)], axis=1)
    (a, b, _, _), = [m for m in W_IN_MAP if m[2] == "dt"]
    w_dt = jnp.pad(jnp.concatenate(cols(a, b), axis=1), ((0, 0), (0, HEAD_PAD - (b - a))))
    return w_main, w_dt
GATHER_KIND = {"w_in": "chip", "p_a": "row", "p_b": "row", "w_mix_o": "row", "w_xq": "row", "w_xkv": "col", "w_xo": "row",
               "w_ffn_in": "col", "w_ffn_out": "row", "conv_w": "chip", "ln_g": "chip", "ln_b": "chip"}
GRAD_VIEW = {n: ("col" if k == "col" else "chip") for n, k in GATHER_KIND.items() if n in BIG}


def _shard_shape(name):
    axis, (r, c) = BIG[name]
    return (r // N_CHIPS, c) if axis == 0 else (r, c // N_CHIPS)


def _pad_rows(flat, cols, row_mult):
    n = flat.shape[0]
    rows = -(-n // cols)
    rows = -(-rows // row_mult) * row_mult
    return jnp.pad(flat, (0, rows * cols - n)).reshape(rows, cols)


def _gather_small_params(a, chip):
    names = list(SMALL_SHARDED)
    kinds = [GATHER_KIND[n] for n in names]
    bufs = [_cast_place(a[n], GATHER_KIND[n], F32, chip.reshape(1), name=f"place_{n}") for n in names]
    outs = _gather_params(bufs, [a[n].shape[1:] for n in names], kinds, name="gather_small_params")
    full = {}
    for n, o in zip(names, outs):
        _, _, r, c = o.shape
        full[n] = jnp.transpose(o, (0, 2, 1, 3)).reshape(DEPTH, r, N_CHIPS * c)
    return full


GATHER_GROUPS = (("w_in",), tuple(n for n in BIG if n != "w_in"))


def _gather_group_start(a, l, names, chip, after, *, tag):
    bufs = [_cast_place_layer(a[n], l, GATHER_KIND[n], chip.reshape(1), after, name=f"place_{n}_l{l}") for n in names]
    return _gather_start(bufs, [a[n].shape[1:] for n in names], [GATHER_KIND[n] for n in names], name=f"gather_start_{tag}")


def _gather_group_finish(a, names, flight, after, *, tag):
    send_sems, recv_sems, bufs, token = flight
    shapes, kinds = [a[n].shape[1:] for n in names], [GATHER_KIND[n] for n in names]
    bufs = _gather_wait(send_sems, recv_sems, bufs, shapes, kinds, token if after is None else after, name=f"gather_wait_{tag}")
    full = dict(zip(names, _gather_forward(bufs, shapes, kinds, name=f"gather_forward_{tag}")))
    if "w_in" in full:
        full["w_main"], full["w_dt"] = _w_in_reassemble(full.pop("w_in"))
    return full


def _layer_weights(a, big, small, l):
    w = dict(big)
    for n in SMALL_SHARDED:
        w[n] = small[n][l]
    for n in ["sg_ln_g", "sg_ln_b", "sg_w", "conv_b", "ssm_norm_g"]:
        w[n] = a[n][l]
    w["sg_bcol"] = a["sg_b"][l][..., None]
    for n in ["dt_bias", "a_log"]:
        w[n + "8"] = _pad_heads(a[n][l])
    w["d_skipx"] = _expand_heads(a["d_skip"][l])
    return w


def _grad_views(grads, names):
    gs = []
    for n in names:
        axis, _ = BIG[n]
        r, c = _shard_shape(n)
        if n == "w_in":
            gs.append(_w_in_chip_major(grads["w_main"], grads["w_dt"]))
        elif axis == 0:
            gs.append(grads[n].reshape(N_CHIPS, r, c))
        else:
            gs.append(grads[n])
    return gs


class _GradExchange:
    def __init__(self, grads, names, c_idx, tag):
        self.names, self.c_idx, self.tag = names, c_idx, tag
        self.views = [GRAD_VIEW[n] for n in names]
        self.gs = _grad_views(grads, names)

    def start(self):
        self.sems = _grads_to_sibling_start(self.gs, self.views, name=f"grads_to_sibling_start_{self.tag}")
        return self.sems[4]

    def cross(self, after):
        send_sems, recv_sems, gs, lands, token = self.sems
        self.gs, self.recv = _grads_to_sibling_wait(send_sems, recv_sems, gs, lands, self.views, token if after is None else after,
                                                    name=f"grads_to_sibling_wait_{self.tag}")
        cpre = self.c_idx.reshape(1)
        pairs = [_pair_sum(g, rv, v, cpre, name=f"grads_pair_sum_{n}_{self.tag}")
                 for g, rv, v, n in zip(self.gs, self.recv, self.views, self.names)]
        self.sems = _grads_to_chips_start(pairs, self.views, name=f"grads_to_chips_start_{self.tag}")
        return self.sems[4]

    def finish(self, after):
        send_sems, recv_sems, pairs, lands, _ = self.sems
        quads = _grads_to_chips_wait(send_sems, recv_sems, pairs, lands, self.views, after, name=f"grads_to_chips_wait_{self.tag}")
        return {n: (g, rv, q) for n, g, rv, q in zip(self.names, self.gs, self.recv, quads)}


def _finish_big_grads(parts, c_idx, chip):
    tots = [_quad_sum([parts[l][n][0] for l in range(DEPTH)], [parts[l][n][1] for l in range(DEPTH)],
                      [parts[l][n][2] for l in range(DEPTH)], GRAD_VIEW[n], chip.reshape(1), c_idx.reshape(1),
                      name=f"grads_chip_sum_{n}") for n in BIG]
    others = _grads_share(tots, name="grads_share")
    return {n: (t, o) for n, t, o in zip(BIG, tots, others)}


def _direct_copies(x_ref, land_ref, send_sems, recv_sems):
    mx, my, mc = lax.axis_index("x"), lax.axis_index("y"), lax.axis_index("c")
    me = 4 * mx + 2 * my + mc
    copies = []
    for k in range(N_DEV - 1):
        f = k + 1
        to = (mx ^ (f >> 2 & 1), my ^ (f >> 1 & 1), mc ^ (f & 1))
        copies.append(pltpu.make_async_remote_copy(src_ref=x_ref, dst_ref=land_ref.at[me], send_sem=send_sems.at[k],
                                                   recv_sem=recv_sems.at[k], device_id=to, device_id_type=_MESH))
    return copies


def _all_gather8_start(x, *, name):
    land = pltpu.with_memory_space_constraint(lax.empty((N_DEV,) + x.shape, x.dtype), pltpu.HBM)

    def body(x_ref, land_ref, send_sems, recv_sems, x_out, land_out, token):
        for cp in _direct_copies(x_ref, land_ref, send_sems, recv_sems):
            cp.start()
        token[...] = jnp.zeros_like(token)

    n = N_DEV - 1
    return pl.pallas_call(
        body, name=name,
        out_shape=(pltpu.SemaphoreType.DMA((n,)), pltpu.SemaphoreType.DMA((n,)), pltpu.HBM(x.shape, x.dtype),
                   pltpu.HBM(land.shape, land.dtype), jax.ShapeDtypeStruct((SUBLANE, LANE), F32)),
        in_specs=[_HBM, _HBM], out_specs=(_SEM, _SEM, _HBM, _HBM, pl.BlockSpec(memory_space=pltpu.VMEM)),
        input_output_aliases={0: 2, 1: 3}, compiler_params=pltpu.CompilerParams(has_side_effects=_EFFECT),
    )(pltpu.with_memory_space_constraint(x, pltpu.HBM), land)


def _all_gather8_wait(send_sems, recv_sems, x, land, after, *, name):
    def body(x_ref, land_ref, s_sems, r_sems, _, x_out, land_out):
        for cp in _direct_copies(x_ref, land_ref, s_sems, r_sems):
            cp.wait_send()
            cp.wait_recv()

    return pl.pallas_call(
        body, name=name, out_shape=(pltpu.HBM(x.shape, x.dtype), pltpu.HBM(land.shape, land.dtype)),
        in_specs=[_HBM, _HBM, _SEM, _SEM, _ANY], out_specs=(_HBM, _HBM), input_output_aliases={0: 0, 1: 1},
        compiler_params=pltpu.CompilerParams(has_side_effects=_EFFECT),
    )(x, land, send_sems, recv_sems, after)


def _pack_small(small):
    return _pad_rows(jnp.concatenate([small[n].reshape(-1) for n in small]), LANE, SUBLANE)


def _unpack_small(small, g8, packed, chip, c_idx, *, name):
    names = list(small)
    tot = _sum_devices(g8, packed, (2 * chip + c_idx).reshape(1), name=name).reshape(-1)
    out, off = {}, 0
    for n in names:
        sz = small[n].size
        full = tot[off:off + sz].reshape(small[n].shape)
        off += sz
        if n in SMALL_SHARDED:
            cs = SMALL_SHARDED[n][1] // N_CHIPS
            full = lax.dynamic_slice_in_dim(full, chip * cs, cs, axis=-1)
        out[n] = full
    return out


def kernel(x, mem, mem_ln_g, mem_ln_b, w_in, sg_ln_g, sg_ln_b, sg_w, sg_b, conv_w, conv_b, dt_bias, a_log, d_skip, ssm_norm_g, p_a, p_b, w_mix_o, w_xq, w_xkv, w_xo, w_ffn_in, w_ffn_out, ln_g, ln_b, loss_target, m_mem_ln_g, m_mem_ln_b, m_w_in, m_sg_ln_g, m_sg_ln_b, m_sg_w, m_sg_b, m_conv_w, m_conv_b, m_dt_bias, m_a_log, m_d_skip, m_ssm_norm_g, m_p_a, m_p_b, m_w_mix_o, m_w_xq, m_w_xkv, m_w_xo, m_w_ffn_in, m_w_ffn_out, m_ln_g, m_ln_b, v_mem_ln_g, v_mem_ln_b, v_w_in, v_sg_ln_g, v_sg_ln_b, v_sg_w, v_sg_b, v_conv_w, v_conv_b, v_dt_bias, v_a_log, v_d_skip, v_ssm_norm_g, v_p_a, v_p_b, v_w_mix_o, v_w_xq, v_w_xkv, v_w_xo, v_w_ffn_in, v_w_ffn_out, v_ln_g, v_ln_b):
    a = dict(zip(ARG_NAMES, (x, mem, mem_ln_g, mem_ln_b, w_in, sg_ln_g, sg_ln_b, sg_w, sg_b, conv_w, conv_b, dt_bias, a_log, d_skip, ssm_norm_g, p_a, p_b, w_mix_o, w_xq, w_xkv, w_xo, w_ffn_in, w_ffn_out, ln_g, ln_b, loss_target, m_mem_ln_g, m_mem_ln_b, m_w_in, m_sg_ln_g, m_sg_ln_b, m_sg_w, m_sg_b, m_conv_w, m_conv_b, m_dt_bias, m_a_log, m_d_skip, m_ssm_norm_g, m_p_a, m_p_b, m_w_mix_o, m_w_xq, m_w_xkv, m_w_xo, m_w_ffn_in, m_w_ffn_out, m_ln_g, m_ln_b, v_mem_ln_g, v_mem_ln_b, v_w_in, v_sg_ln_g, v_sg_ln_b, v_sg_w, v_sg_b, v_conv_w, v_conv_b, v_dt_bias, v_a_log, v_d_skip, v_ssm_norm_g, v_p_a, v_p_b, v_w_mix_o, v_w_xq, v_w_xkv, v_w_xo, v_w_ffn_in, v_w_ffn_out, v_ln_g, v_ln_b)))
    c_idx = lax.axis_index("c").astype(jnp.int32)
    chip = (2 * lax.axis_index("x") + lax.axis_index("y")).astype(jnp.int32)

    small = _gather_small_params(a, chip)
    ga, gb = GATHER_GROUPS
    flights = {(0, 0): _gather_group_start(a, 0, ga, chip, small["ln_b"], tag="l0_a")}
    flights[0, 1] = _gather_group_start(a, 0, gb, chip, flights[0, 0][3], tag="l0_b")

    def layer_weights(after, l):
        first = _gather_group_finish(a, ga, flights[l, 0], after if l else flights[l, 1][3], tag=f"l{l}_a")

        def rest(w, after_b):
            more = _gather_group_finish(a, gb, flights[l, 1], after_b, tag=f"l{l}_b")
            if l + 1 < DEPTH:
                flights[l + 1, 0] = _gather_group_start(a, l + 1, ga, chip, more["p_a"], tag=f"l{l + 1}_a")
                flights[l + 1, 1] = _gather_group_start(a, l + 1, gb, chip, flights[l + 1, 0][3], tag=f"l{l + 1}_b")
                more["p_a"] = more["p_a"] + flights[l + 1, 1][3][0, 0].astype(MXU_DTYPE)
            return {k: v for k, v in {**w, **more}.items() if k != "rest"}

        return dict(_layer_weights(a, first, small, l), rest=rest)

    layers = [functools.partial(layer_weights, l=l) for l in range(DEPTH)]
    exchanges, seen, small_flight = [], {}, {}

    def start_exchange(l, names, grads_l):
        ex = _GradExchange(grads_l, names, c_idx, f"l{l}_{names[0]}")
        tokens = [ex.start()]
        if exchanges:
            tokens.append(exchanges[-1][1].cross(tokens[0]))
        exchanges.append((l, ex))
        seen[l] = grads_l
        if l == 0 and names == GRAD_GROUPS[-1]:
            tokens.append(ex.cross(None))
            small = {}
            for n in SMALL:
                if n.startswith("mem_ln"):
                    continue
                per_layer = []
                for k in range(DEPTH):
                    g = seen[k][n]
                    if n in ("dt_bias", "a_log", "d_skip"):
                        g = g[0, :SSM_HEADS]
                    per_layer.append(g.reshape(a[n].shape[1:-1] + (-1,)))
                small[n] = jnp.stack(per_layer)
            small_flight["small"] = small
            small_flight["sems"] = _all_gather8_start(_pack_small(small), name="gather_small_grads_start")
            tokens.append(small_flight["sems"][4])
        return sum(tokens[1:], tokens[0])

    lsum, grad_x, grads, d_mem_g, d_mem_b = _local_step(x, mem, loss_target, mem_ln_g, mem_ln_b, layers, start_exchange)
    loss = lax.psum(0.5 * jnp.sum(lsum) / D_MODEL, ("x", "y", "c"))

    parts = [{} for _ in range(DEPTH)]
    for l, ex in exchanges:
        parts[l].update(ex.finish(grad_x))
    halves = _finish_big_grads(parts, c_idx, chip)
    gw = {}
    send_sems, recv_sems, packed, land, _ = small_flight["sems"]
    packed, g8 = _all_gather8_wait(send_sems, recv_sems, packed, land, grad_x, name="gather_small_grads_wait")
    gw.update(_unpack_small(small_flight["small"], g8, packed, chip, c_idx, name="small_grads_sum"))
    mem_small = {"mem_ln_g": d_mem_g, "mem_ln_b": d_mem_b}
    mem_packed = _pack_small(mem_small)
    gw.update(_unpack_small(mem_small, _all_gather8(mem_packed, name="gather_mem_ln_grads"), mem_packed, chip, c_idx,
                            name="mem_ln_grads_sum"))

    delta, new_m, new_v = {}, {}, {}
    for n in BIG:
        mine, other = halves[n]
        gw[n], delta[n], new_m[n], new_v[n] = _adamw_halves(a[n], a["m_" + n], a["v_" + n], mine, other, c_idx.reshape(1),
                                                             name=f"adamw_{n}")
    for n in SMALL:
        shp = a[n].shape
        view = (-1, LANE) if a[n].size % LANE == 0 else (1, -1)
        outs = _adamw(*[v.reshape(view) for v in (a[n], gw[n], a["m_" + n], a["v_" + n])], name=f"adamw_{n}")
        delta[n], new_m[n], new_v[n] = (o.reshape(shp) for o in outs)
    return (loss, grad_x, *[gw[n].reshape(a[n].shape) for n in WEIGHTS], *[delta[n] for n in WEIGHTS],
            *[new_m[n] for n in WEIGHTS], *[new_v[n] for n in WEIGHTS])
```

```python
import functools
import math

import jax
import jax.numpy as jnp
from jax import lax
from jax.experimental import pallas as pl
from jax.experimental.pallas import tpu as pltpu

F32 = jnp.float32
MXU_DTYPE = jnp.bfloat16
WIRE_DTYPE = jnp.bfloat16
STASH_DTYPE = jnp.bfloat16

D_MODEL = 1024
DEPTH = 2
CHUNK = 128
SG_GROUPS = 8
SSM_INNER = 2048
SSM_HEADDIM = 64
SSM_HEADS = 32
SSM_STATE = 128
SSM_GROUPS = 4
SSM_CONV = 4
SSM_CONV_DIM = 3072
X_HEADS = 4
X_HEADDIM = 256
FFN_HIDDEN = 2816
ALPHA = float((2 * DEPTH) ** 0.25)
LN_EPS = 1e-5
RMS_EPS = 1e-5
ADAM_LR = 0.001
ADAM_B1 = 0.9
ADAM_B2 = 0.999
ADAM_EPS = 1e-08
ADAM_WD = 0.01
ADAM_STEP = 10

MAIN_COLS = 9216
UVZ_COLS = 4096
GAB_COL0 = 4096
XBC_COL0 = 6144
HEAD_PAD = 128

VMEM_LIMIT = 56 * 1024 * 1024
BLOCK_BYTES = 2 * 1024 * 1024
ROW_TILES = (1024, 512, 256, 128)
LANE = 128
SUBLANE = 8

N_CHIPS = 4
N_DEV = 8


def _pick(n, cands):
    for c in cands:
        if n % c == 0:
            return c
    return n


MM_TILE_MAX = 1536
MM_OPERAND_BYTES = 12 * 1024 * 1024


def _div_tile(n, limit):
    best = None
    for t in range(LANE, min(n, limit) + 1, LANE):
        if n % t == 0:
            best = t
    return n if best is None else best


def _params(*sem):
    return pltpu.CompilerParams(dimension_semantics=tuple(sem), vmem_limit_bytes=VMEM_LIMIT)


_ANY = pl.BlockSpec(memory_space=pl.ANY)
_MESH = pl.DeviceIdType.MESH


def _nt(a, b):
    return lax.dot_general(a, b, (((1,), (1,)), ((), ())), preferred_element_type=F32)


def _tn(a, b):
    return lax.dot_general(a, b, (((0,), (0,)), ((), ())), preferred_element_type=F32)


def _nn(a, b):
    return jnp.dot(a, b, preferred_element_type=F32)


def _sigmoid(x):
    return 0.5 * jnp.tanh(0.5 * x) + 0.5


def _split3(v):
    def top(x):
        bits = lax.bitcast_convert_type(x, jnp.uint32) & jnp.uint32(0xFFFF0000)
        return lax.bitcast_convert_type(bits, F32)

    v1 = top(v)
    r1 = v - v1
    v2 = top(r1)
    v3 = r1 - v2
    return v1.astype(jnp.bfloat16), v2.astype(jnp.bfloat16), v3.astype(jnp.bfloat16)


def _dot_exact(a, b, dn, data):
    if data == 0:
        mat = b.astype(jnp.bfloat16)
        return sum(lax.dot_general(p, mat, dn, preferred_element_type=F32) for p in _split3(a))
    mat = a.astype(jnp.bfloat16)
    return sum(lax.dot_general(mat, p, dn, preferred_element_type=F32) for p in _split3(b))


_DN_NN = (((1,), (0,)), ((), ()))
_DN_TN = (((0,), (0,)), ((), ()))


def _gelu(x):
    return 0.5 * x * (1.0 + lax.erf(x * (2.0 ** -0.5)))


def _gelu_grad(x):
    return 0.5 * (1.0 + lax.erf(x * (2.0 ** -0.5))) + x * jnp.exp(-0.5 * x * x) * (1.0 / math.sqrt(2.0 * math.pi))


def _mm(a, b, *, ta=False, tb=False, out_dtype=F32, after=None, name):
    if ta:
        kdim, m = a.shape
    else:
        m, kdim = a.shape
    if tb:
        n, k2 = b.shape[-2:]
    else:
        k2, n = b.shape[-2:]
    assert kdim == k2, (a.shape, b.shape, ta, tb)
    tm = _div_tile(m, MM_TILE_MAX)
    tn = _div_tile(n, MM_TILE_MAX)
    tk = _div_tile(kdim, MM_OPERAND_BYTES // (tm * a.dtype.itemsize + tn * b.dtype.itemsize))
    nk = kdim // tk
    dn = (((0 if ta else 1,), (1 if tb else 0,)), ((), ()))

    extra = [] if after is None else [after]

    def body(a_ref, b_ref, *rest):
        o_ref = rest[len(extra)]
        d = lax.dot_general(a_ref[...].astype(MXU_DTYPE), b_ref[...].astype(MXU_DTYPE), dn, preferred_element_type=F32)
        if nk == 1:
            o_ref[...] = d.astype(out_dtype)
            return
        acc_ref = rest[len(extra) + 1]
        k = pl.program_id(2)

        @pl.when(k == 0)
        def _():
            acc_ref[...] = d

        @pl.when(jnp.logical_and(k > 0, k < nk - 1))
        def _():
            acc_ref[...] += d

        @pl.when(k == nk - 1)
        def _():
            o_ref[...] = (acc_ref[...] + d).astype(out_dtype)

    a_spec = pl.BlockSpec((tk, tm), lambda i, j, k: (k, i)) if ta else pl.BlockSpec((tm, tk), lambda i, j, k: (i, k))
    b_spec = pl.BlockSpec((tn, tk), lambda i, j, k: (j, k)) if tb else pl.BlockSpec((tk, tn), lambda i, j, k: (k, j))
    return pl.pallas_call(
        body, grid=(m // tm, n // tn, nk), in_specs=[a_spec, b_spec] + [_ANY] * len(extra),
        out_specs=pl.BlockSpec((tm, tn), lambda i, j, k: (i, j)),
        out_shape=jax.ShapeDtypeStruct((m, n), out_dtype),
        scratch_shapes=[pltpu.VMEM((tm, tn), F32)] if nk > 1 else [],
        compiler_params=_params("parallel", "parallel", "arbitrary"), name=name)(a, b, *extra)


def _row_spec(tm, c, col=0):
    return pl.BlockSpec((tm, c), lambda i: (i, col))


def _par_spec(shape):
    nd = len(shape)
    return pl.BlockSpec(shape, lambda i: (0,) * nd)


def _ln_fwd(x, f, g, b, *, name):
    t, c = x.shape
    tm = _pick(t, ROW_TILES)
    has_f = f is not None

    def body(*refs):
        if has_f:
            x_ref, f_ref, g_ref, b_ref, y_ref, yb_ref, xh_ref, rs_ref = refs
            r = ALPHA * x_ref[...] + f_ref[...]
        else:
            x_ref, g_ref, b_ref, y_ref, yb_ref, xh_ref, rs_ref = refs
            r = x_ref[...]
        mu = jnp.mean(r, axis=-1, keepdims=True)
        xc = r - mu
        var = jnp.mean(xc * xc, axis=-1, keepdims=True)
        rstd = lax.rsqrt(var + LN_EPS)
        xh = xc * rstd
        y = xh * g_ref[...] + b_ref[...]
        y_ref[...] = y
        yb_ref[...] = y.astype(MXU_DTYPE)
        xh_ref[...] = xh
        rs_ref[...] = jnp.broadcast_to(rstd, rs_ref.shape)

    ins = [x] + ([f] if has_f else []) + [g.reshape(1, c), b.reshape(1, c)]
    in_specs = [_row_spec(tm, c)] * (2 if has_f else 1) + [_par_spec((1, c))] * 2
    return pl.pallas_call(
        body, grid=(t // tm,), in_specs=in_specs,
        out_specs=[_row_spec(tm, c), _row_spec(tm, c), _row_spec(tm, c), _row_spec(tm, LANE)],
        out_shape=[jax.ShapeDtypeStruct((t, c), F32), jax.ShapeDtypeStruct((t, c), MXU_DTYPE),
                   jax.ShapeDtypeStruct((t, c), F32), jax.ShapeDtypeStruct((t, LANE), F32)],
        compiler_params=_params("parallel"), name=name)(*ins)


def _ln_bwd(addends, scales, xh, rs, g, *, name):
    t, c = xh.shape
    tm = _pick(t, ROW_TILES)
    na = len(addends)

    def body(*refs):
        a_refs = refs[:na]
        xh_ref, rs_ref, g_ref, dp_ref, dpb_ref, dg_ref, db_ref = refs[na:]

        @pl.when(pl.program_id(0) == 0)
        def _():
            dg_ref[...] = jnp.zeros_like(dg_ref)
            db_ref[...] = jnp.zeros_like(db_ref)

        dy = None
        for s, r in zip(scales, a_refs):
            term = r[...] if s == 1.0 else s * r[...]
            dy = term if dy is None else dy + term
        xhv = xh_ref[...]
        dxh = dy * g_ref[...]
        m1 = jnp.mean(dxh, axis=-1, keepdims=True)
        m2 = jnp.mean(dxh * xhv, axis=-1, keepdims=True)
        dp = rs_ref[:, 0:1] * (dxh - m1 - xhv * m2)
        dp_ref[...] = dp
        dpb_ref[...] = dp.astype(MXU_DTYPE)
        dg_ref[...] += jnp.sum(dy * xhv, axis=0, keepdims=True)
        db_ref[...] += jnp.sum(dy, axis=0, keepdims=True)

    in_specs = [_row_spec(tm, c)] * (na + 1) + [_row_spec(tm, LANE), _par_spec((1, c))]
    return pl.pallas_call(
        body, grid=(t // tm,), in_specs=in_specs,
        out_specs=[_row_spec(tm, c), _row_spec(tm, c), _par_spec((1, c)), _par_spec((1, c))],
        out_shape=[jax.ShapeDtypeStruct((t, c), F32), jax.ShapeDtypeStruct((t, c), MXU_DTYPE),
                   jax.ShapeDtypeStruct((1, c), F32), jax.ShapeDtypeStruct((1, c), F32)],
        compiler_params=_params("arbitrary"), name=name)(*addends, xh, rs, g.reshape(1, c))


def _add_scaled(addends, scales, *, name):
    t, c = addends[0].shape
    tm = _pick(t, ROW_TILES)
    na = len(addends)

    def body(*refs):
        acc = None
        for s, r in zip(scales, refs[:na]):
            term = r[...] if s == 1.0 else s * r[...]
            acc = term if acc is None else acc + term
        refs[na][...] = acc

    return pl.pallas_call(
        body, grid=(t // tm,), in_specs=[_row_spec(tm, c)] * na, out_specs=_row_spec(tm, c),
        out_shape=jax.ShapeDtypeStruct((t, c), F32), compiler_params=_params("parallel"), name=name)(*addends)


def _loss_head(y, tgt, *, name):
    t, c = y.shape
    tm = _pick(t, ROW_TILES)

    def body(y_ref, t_ref, dy_ref, ls_ref):
        @pl.when(pl.program_id(0) == 0)
        def _():
            ls_ref[...] = jnp.zeros_like(ls_ref)

        e = y_ref[...] - t_ref[...]
        dy_ref[...] = e * (1.0 / c)
        ls_ref[...] += jnp.sum(e * e, axis=0, keepdims=True)

    return pl.pallas_call(
        body, grid=(t // tm,), in_specs=[_row_spec(tm, c)] * 2,
        out_specs=[_row_spec(tm, c), _par_spec((1, c))],
        out_shape=[jax.ShapeDtypeStruct((t, c), F32), jax.ShapeDtypeStruct((1, c), F32)],
        compiler_params=_params("arbitrary"), name=name)(y, tgt)


def _swiglu_fwd(h, *, name):
    t, two_f = h.shape
    fh = two_f // 2
    tm = _pick(t, (256, 128))

    def body(g_ref, u_ref, a_ref):
        g = g_ref[...].astype(F32)
        a_ref[...] = (g * _sigmoid(g) * u_ref[...].astype(F32)).astype(MXU_DTYPE)

    return pl.pallas_call(
        body, grid=(t // tm,), in_specs=[_row_spec(tm, fh, 0), _row_spec(tm, fh, 1)], out_specs=_row_spec(tm, fh),
        out_shape=jax.ShapeDtypeStruct((t, fh), MXU_DTYPE), compiler_params=_params("parallel"), name=name)(h, h)


def _swiglu_bwd(h, da, *, name):
    t, two_f = h.shape
    fh = two_f // 2
    tm = _pick(t, (256, 128))

    def body(g_ref, u_ref, da_ref, dh_ref):
        g = g_ref[...].astype(F32)
        s = _sigmoid(g)
        dav = da_ref[...].astype(F32)
        dh_ref[:, :fh] = (dav * u_ref[...].astype(F32) * (s * (1.0 + g * (1.0 - s)))).astype(MXU_DTYPE)
        dh_ref[:, fh:] = (dav * g * s).astype(MXU_DTYPE)

    return pl.pallas_call(
        body, grid=(t // tm,), in_specs=[_row_spec(tm, fh, 0), _row_spec(tm, fh, 1), _row_spec(tm, fh)],
        out_specs=_row_spec(tm, two_f), out_shape=jax.ShapeDtypeStruct((t, two_f), MXU_DTYPE),
        compiler_params=_params("parallel"), name=name)(h, h, da)


def _attn_probs(q, k):
    s = _nt(q, k) * (X_HEADDIM ** -0.5)
    s = s - jnp.max(s, axis=-1, keepdims=True)
    p = jnp.exp(s)
    return p / jnp.sum(p, axis=-1, keepdims=True)


def _attn_fwd(q, kv, *, bsz, name):
    t = q.shape[0]
    s = t // bsz
    ml = kv.shape[0] // bsz
    hd = X_HEADDIM

    def body(q_ref, k_ref, v_ref, o_ref):
        p = _attn_probs(q_ref[...], k_ref[...])
        o_ref[...] = _nn(p.astype(MXU_DTYPE), v_ref[...]).astype(MXU_DTYPE)

    return pl.pallas_call(
        body, grid=(bsz, X_HEADS),
        in_specs=[pl.BlockSpec((s, hd), lambda b, h: (b, h)), pl.BlockSpec((ml, hd), lambda b, h: (b, h)),
                  pl.BlockSpec((ml, hd), lambda b, h: (b, X_HEADS + h))],
        out_specs=pl.BlockSpec((s, hd), lambda b, h: (b, h)),
        out_shape=jax.ShapeDtypeStruct((t, D_MODEL), MXU_DTYPE),
        compiler_params=_params("parallel", "parallel"), name=name)(q, kv, kv)


def _attn_bwd(q, kv, do, *, bsz, name):
    t = q.shape[0]
    s = t // bsz
    ml = kv.shape[0] // bsz
    hd = X_HEADDIM

    def body(q_ref, k_ref, v_ref, do_ref, dq_ref, dk_ref, dv_ref):
        qv, kk, vv, dov = q_ref[...], k_ref[...], v_ref[...], do_ref[...]
        p = _attn_probs(qv, kk)
        dp = _nt(dov, vv)
        dv_ref[...] = _tn(p.astype(MXU_DTYPE), dov).astype(MXU_DTYPE)
        ds = (p * (dp - jnp.sum(dp * p, axis=-1, keepdims=True)) * (X_HEADDIM ** -0.5)).astype(MXU_DTYPE)
        dq_ref[...] = _nn(ds, kk).astype(MXU_DTYPE)
        dk_ref[...] = _tn(ds, qv).astype(MXU_DTYPE)

    blk_q = pl.BlockSpec((s, hd), lambda b, h: (b, h))
    blk_m = pl.BlockSpec((ml, hd), lambda b, h: (b, h))
    return pl.pallas_call(
        body, grid=(bsz, X_HEADS),
        in_specs=[blk_q, blk_m, pl.BlockSpec((ml, hd), lambda b, h: (b, X_HEADS + h)), blk_q],
        out_specs=[blk_q, blk_m, blk_m],
        out_shape=[jax.ShapeDtypeStruct((t, D_MODEL), MXU_DTYPE), jax.ShapeDtypeStruct((bsz * ml, D_MODEL), MXU_DTYPE),
                   jax.ShapeDtypeStruct((bsz * ml, D_MODEL), MXU_DTYPE)],
        compiler_params=_params("parallel", "parallel"), name=name)(q, kv, kv, do)


def _causal(n):
    row = lax.broadcasted_iota(jnp.int32, (n, n), 0)
    col = lax.broadcasted_iota(jnp.int32, (n, n), 1)
    return row >= col


def _sg_norm(v, g, b):
    gv = _gelu(v)
    mu = jnp.mean(gv, axis=-1, keepdims=True)
    xc = gv - mu
    var = jnp.mean(xc * xc, axis=-1, keepdims=True)
    rstd = lax.rsqrt(var + LN_EPS)
    xh = xc * rstd
    return xh, rstd, xh * g + b


def _sg_fwd(proj, ln_g, ln_b, w, bcol, *, name):
    t = proj.shape[0]
    c = D_MODEL
    gd = c // SG_GROUPS

    def body(u_ref, v_ref, g_ref, b_ref, w_ref, bc_ref, o_ref):
        gu = _gelu(u_ref[...].astype(F32))
        _, _, vn = _sg_norm(v_ref[...].astype(F32), g_ref[...], b_ref[...])
        mask = _causal(CHUNK)
        for g in range(SG_GROUPS):
            sl = slice(g * gd, (g + 1) * gd)
            wg = jnp.where(mask, w_ref[g], 0.0).astype(MXU_DTYPE)
            mixed = _nn(wg, vn[:, sl].astype(MXU_DTYPE)) + bc_ref[g]
            o_ref[:, sl] = (gu[:, sl] * mixed).astype(MXU_DTYPE)

    return pl.pallas_call(
        body, grid=(t // CHUNK,),
        in_specs=[_row_spec(CHUNK, c, 0), _row_spec(CHUNK, c, 1), _par_spec((1, c)), _par_spec((1, c)),
                  _par_spec((SG_GROUPS, CHUNK, CHUNK)), _par_spec((SG_GROUPS, CHUNK, 1))],
        out_specs=_row_spec(CHUNK, c), out_shape=jax.ShapeDtypeStruct((t, c), MXU_DTYPE),
        compiler_params=_params("parallel"), name=name)(proj, proj, ln_g.reshape(1, c), ln_b.reshape(1, c), w, bcol)


def _sg_bwd(proj, dsgo, ln_g, ln_b, w, bcol, dproj, *, name):
    t = proj.shape[0]
    c = D_MODEL
    gd = c // SG_GROUPS

    def body(u_ref, v_ref, d_ref, g_ref, b_ref, w_ref, bc_ref, _, duv_ref, dw_ref, dbc_ref, dg_ref, db_ref, dvn_ref):
        @pl.when(pl.program_id(0) == 0)
        def _():
            dw_ref[...] = jnp.zeros_like(dw_ref)
            dbc_ref[...] = jnp.zeros_like(dbc_ref)
            dg_ref[...] = jnp.zeros_like(dg_ref)
            db_ref[...] = jnp.zeros_like(db_ref)

        u = u_ref[...].astype(F32)
        v = v_ref[...].astype(F32)
        dso = d_ref[...].astype(F32)
        gu = _gelu(u)
        xh, rstd, vn = _sg_norm(v, g_ref[...], b_ref[...])
        mask = _causal(CHUNK)
        for g in range(SG_GROUPS):
            sl = slice(g * gd, (g + 1) * gd)
            wg = jnp.where(mask, w_ref[g], 0.0).astype(MXU_DTYPE)
            vng = vn[:, sl].astype(MXU_DTYPE)
            mixed = _nn(wg, vng) + bc_ref[g]
            duv_ref[:, sl] = (dso[:, sl] * mixed * _gelu_grad(u[:, sl])).astype(MXU_DTYPE)
            dmix = dso[:, sl] * gu[:, sl]
            dmb = dmix.astype(MXU_DTYPE)
            dbc_ref[g] += jnp.sum(dmix, axis=-1, keepdims=True)
            dw_ref[g] += jnp.where(mask, _nt(dmb, vng), 0.0)
            dvn_ref[:, sl] = _tn(wg, dmb)
        dvn = dvn_ref[...]
        dg_ref[...] += jnp.sum(dvn * xh, axis=0, keepdims=True)
        db_ref[...] += jnp.sum(dvn, axis=0, keepdims=True)
        dxh = dvn * g_ref[...]
        m1 = jnp.mean(dxh, axis=-1, keepdims=True)
        m2 = jnp.mean(dxh * xh, axis=-1, keepdims=True)
        dgv = rstd * (dxh - m1 - xh * m2)
        duv_ref[:, c:] = (dgv * _gelu_grad(v)).astype(MXU_DTYPE)

    return pl.pallas_call(
        body, grid=(t // CHUNK,),
        in_specs=[_row_spec(CHUNK, c, 0), _row_spec(CHUNK, c, 1), _row_spec(CHUNK, c), _par_spec((1, c)),
                  _par_spec((1, c)), _par_spec((SG_GROUPS, CHUNK, CHUNK)), _par_spec((SG_GROUPS, CHUNK, 1)), _ANY],
        out_specs=[_row_spec(CHUNK, 2 * c), _par_spec((SG_GROUPS, CHUNK, CHUNK)), _par_spec((SG_GROUPS, CHUNK, 1)),
                   _par_spec((1, c)), _par_spec((1, c))],
        out_shape=[jax.ShapeDtypeStruct(dproj.shape, dproj.dtype), jax.ShapeDtypeStruct((SG_GROUPS, CHUNK, CHUNK), F32),
                   jax.ShapeDtypeStruct((SG_GROUPS, CHUNK, 1), F32), jax.ShapeDtypeStruct((1, c), F32),
                   jax.ShapeDtypeStruct((1, c), F32)],
        scratch_shapes=[pltpu.VMEM((CHUNK, c), F32)], input_output_aliases={7: 0},
        compiler_params=_params("arbitrary"), name=name)(proj, proj, dsgo, ln_g.reshape(1, c), ln_b.reshape(1, c), w, bcol, dproj)


CONV_TC = 512


def _conv_taps(x):
    rows = lax.broadcasted_iota(jnp.int32, x.shape, 0)
    taps = [jnp.where(rows >= SSM_CONV - 1 - k, pltpu.roll(x, SSM_CONV - 1 - k, axis=0), 0.0) for k in range(SSM_CONV - 1)]
    return taps + [x]


def _conv_pre(taps, w_ref, b_ref):
    acc = b_ref[...]
    for k in range(SSM_CONV):
        acc = acc + taps[k] * w_ref[k:k + 1, :]
    return acc


def _conv_fwd(proj, w, b, *, bsz, name):
    t = proj.shape[0]
    s = t // bsz
    nj = SSM_CONV_DIM // CONV_TC
    c0 = XBC_COL0 // CONV_TC

    def body(x_ref, w_ref, b_ref, o_ref):
        pre = _conv_pre(_conv_taps(x_ref[...].astype(F32)), w_ref, b_ref)
        o_ref[...] = (pre * _sigmoid(pre)).astype(o_ref.dtype)

    return pl.pallas_call(
        body, grid=(bsz, nj),
        in_specs=[pl.BlockSpec((s, CONV_TC), lambda bb, j: (bb, c0 + j)), pl.BlockSpec((SSM_CONV, CONV_TC), lambda bb, j: (0, j)),
                  pl.BlockSpec((1, CONV_TC), lambda bb, j: (0, j))],
        out_specs=pl.BlockSpec((s, CONV_TC), lambda bb, j: (bb, j)),
        out_shape=jax.ShapeDtypeStruct((t, SSM_CONV_DIM), STASH_DTYPE),
        compiler_params=_params("parallel", "parallel"), name=name)(proj, w, b.reshape(1, -1))


def _conv_bwd(proj, dact, w, b, dproj, *, bsz, name):
    t = proj.shape[0]
    s = t // bsz
    nj = SSM_CONV_DIM // CONV_TC
    c0 = XBC_COL0 // CONV_TC

    def body(x_ref, d_ref, w_ref, b_ref, _, dx_ref, dw_ref, db_ref):
        @pl.when(pl.program_id(1) == 0)
        def _():
            dw_ref[...] = jnp.zeros_like(dw_ref)
            db_ref[...] = jnp.zeros_like(db_ref)

        taps = _conv_taps(x_ref[...].astype(F32))
        pre = _conv_pre(taps, w_ref, b_ref)
        sg = _sigmoid(pre)
        dpre = d_ref[...].astype(F32) * (sg * (1.0 + pre * (1.0 - sg)))
        rows = lax.broadcasted_iota(jnp.int32, dpre.shape, 0)
        db_ref[...] += jnp.sum(dpre, axis=0, keepdims=True)
        dx = dpre * w_ref[SSM_CONV - 1:SSM_CONV, :]
        for k in range(SSM_CONV):
            dw_ref[k:k + 1, :] += jnp.sum(dpre * taps[k], axis=0, keepdims=True)
        for k in range(SSM_CONV - 1):
            sh = SSM_CONV - 1 - k
            dsh = jnp.where(rows < s - sh, pltpu.roll(dpre, s - sh, axis=0), 0.0)
            dx = dx + dsh * w_ref[k:k + 1, :]
        dx_ref[...] = dx.astype(MXU_DTYPE)

    return pl.pallas_call(
        body, grid=(nj, bsz),
        in_specs=[pl.BlockSpec((s, CONV_TC), lambda j, bb: (bb, c0 + j)), pl.BlockSpec((s, CONV_TC), lambda j, bb: (bb, j)),
                  pl.BlockSpec((SSM_CONV, CONV_TC), lambda j, bb: (0, j)), pl.BlockSpec((1, CONV_TC), lambda j, bb: (0, j)), _ANY],
        out_specs=[pl.BlockSpec((s, CONV_TC), lambda j, bb: (bb, c0 + j)), pl.BlockSpec((SSM_CONV, CONV_TC), lambda j, bb: (0, j)),
                   pl.BlockSpec((1, CONV_TC), lambda j, bb: (0, j))],
        out_shape=[jax.ShapeDtypeStruct(dproj.shape, dproj.dtype), jax.ShapeDtypeStruct((SSM_CONV, SSM_CONV_DIM), F32),
                   jax.ShapeDtypeStruct((1, SSM_CONV_DIM), F32)],
        input_output_aliases={4: 0},
        compiler_params=_params("parallel", "arbitrary"), name=name)(proj, dact, w, b.reshape(1, -1), dproj)


def _softplus(x):
    return jnp.maximum(x, 0.0) + jnp.log1p(jnp.exp(-jnp.abs(x)))


def _pad_heads(v):
    return jnp.broadcast_to(jnp.pad(v.astype(F32), (0, HEAD_PAD - SSM_HEADS))[None, :], (SUBLANE, HEAD_PAD))


def _ssd_prep(dt_raw, dt_bias8, a_log8, *, name):
    t = dt_raw.shape[0]
    n = CHUNK

    def body(r_ref, b_ref, al_ref, dt_ref, cs_ref, dtt_ref, cst_ref):
        dt = _softplus(r_ref[...] + b_ref[0:1, :])
        da = dt * (-jnp.exp(al_ref[0:1, :]))
        row = lax.broadcasted_iota(jnp.int32, (n, n), 0)
        col = lax.broadcasted_iota(jnp.int32, (n, n), 1)
        lower = (col <= row).astype(F32)
        upper = (row <= col).astype(F32)
        eye = (row == col).astype(F32)
        dt_ref[...] = dt
        cs_ref[...] = _dot_exact(lower, da, _DN_NN, 1)
        cst_ref[0] = _dot_exact(da, upper, _DN_TN, 0)
        dtt_ref[0] = _dot_exact(dt, eye, _DN_TN, 0)

    hp = HEAD_PAD
    return pl.pallas_call(
        body, grid=(t // n,),
        in_specs=[_row_spec(n, hp), _par_spec((SUBLANE, hp)), _par_spec((SUBLANE, hp))],
        out_specs=[_row_spec(n, hp), _row_spec(n, hp), pl.BlockSpec((1, hp, n), lambda i: (i, 0, 0)),
                   pl.BlockSpec((1, hp, n), lambda i: (i, 0, 0))],
        out_shape=[jax.ShapeDtypeStruct((t, hp), F32), jax.ShapeDtypeStruct((t, hp), F32),
                   jax.ShapeDtypeStruct((t // n, hp, n), F32), jax.ShapeDtypeStruct((t // n, hp, n), F32)],
        compiler_params=_params("parallel"), name=name)(dt_raw, dt_bias8, a_log8)


def _expand_mat():
    h = lax.broadcasted_iota(jnp.int32, (HEAD_PAD, SSM_INNER), 0)
    ch = lax.broadcasted_iota(jnp.int32, (HEAD_PAD, SSM_INNER), 1)
    return (ch // SSM_HEADDIM == h).astype(F32)


def _reduce_mat():
    ch = lax.broadcasted_iota(jnp.int32, (SSM_INNER, HEAD_PAD), 0)
    h = lax.broadcasted_iota(jnp.int32, (SSM_INNER, HEAD_PAD), 1)
    return (ch // SSM_HEADDIM == h).astype(F32)


def _expand(v, em):
    return _dot_exact(v, em, _DN_NN, 0)


def _expand_heads(v):
    return jnp.repeat(v.astype(F32), SSM_HEADDIM)[None, :]


def _decay_mat(cs_ref, cst_ref, h, mask):
    seg = cs_ref[:, h:h + 1] - cst_ref[0, h:h + 1, :]
    return jnp.where(mask, jnp.exp(jnp.minimum(seg, 0.0)), 0.0)


GROUP_CH = SSM_INNER // SSM_GROUPS
PAIRS_PER_GROUP = GROUP_CH // LANE
HEADS_PER_GROUP = SSM_HEADS // SSM_GROUPS
BM_COL0 = SSM_INNER
CM_COL0 = SSM_INNER + SSM_GROUPS * SSM_STATE


def _ssd_specs(nc, rev):
    def cidx(i):
        return (i // nc) * nc + (nc - 1 - i % nc) if rev else i

    n = CHUNK
    xs = pl.BlockSpec((n, SSM_INNER), lambda i: (cidx(i), 0))
    bm = pl.BlockSpec((n, GROUP_CH), lambda i: (cidx(i), BM_COL0 // GROUP_CH))
    cm = pl.BlockSpec((n, GROUP_CH), lambda i: (cidx(i), CM_COL0 // GROUP_CH))
    hv = pl.BlockSpec((n, HEAD_PAD), lambda i: (cidx(i), 0))
    hvt = pl.BlockSpec((1, HEAD_PAD, n), lambda i: (cidx(i), 0, 0))
    st = pl.BlockSpec((1, SSM_INNER, SSM_STATE), lambda i: (cidx(i), 0, 0))
    return xs, bm, cm, hv, hvt, st


def _ssd_fwd(xbc, dt, cs, dtt, cst, dskx, *, nc, name):
    t = xbc.shape[0]
    n = CHUNK
    xs_s, bm_s, cm_s, hv_s, hvt_s, st_s = _ssd_specs(nc, False)

    def body(xs_ref, bm_ref, cm_ref, dt_ref, cs_ref, dtt_ref, cst_ref, dsk_ref, y_ref, st_ref, prev):
        @pl.when(pl.program_id(0) % nc == 0)
        def _():
            prev[...] = jnp.zeros_like(prev)

        st_ref[0] = prev[...]
        em = _expand_mat()
        dtx = _expand(dt_ref[...], em)
        csx = _expand(cs_ref[...], em)
        dskx = dsk_ref[...]
        xs = xs_ref[...].astype(F32)
        xdt = xs * dtx
        ecs = jnp.exp(csx)
        dec = jnp.exp(csx[n - 1:n, :] - csx)
        mask = _causal(n)
        lane = lax.broadcasted_iota(jnp.int32, (n, LANE), 1)
        for g in range(SSM_GROUPS):
            gs = slice(g * SSM_STATE, (g + 1) * SSM_STATE)
            gc = slice(g * GROUP_CH, (g + 1) * GROUP_CH)
            cmat = cm_ref[:, gs].astype(MXU_DTYPE)
            bmat = bm_ref[:, gs].astype(MXU_DTYPE)
            cb = _nt(cmat, bmat)
            yoff = ecs[:, gc] * _nt(cmat, prev[gc, :].astype(MXU_DTYPE))
            for q in range(PAIRS_PER_GROUP):
                hp = g * PAIRS_PER_GROUP + q
                sl = slice(hp * LANE, (hp + 1) * LANE)
                xp = xdt[:, sl].astype(MXU_DTYPE)
                m0 = (cb * _decay_mat(cs_ref, cst_ref, 2 * hp, mask)).astype(MXU_DTYPE)
                m1 = (cb * _decay_mat(cs_ref, cst_ref, 2 * hp + 1, mask)).astype(MXU_DTYPE)
                yd = jnp.where(lane < SSM_HEADDIM, _nn(m0, xp), _nn(m1, xp))
                y_ref[:, sl] = (yd + yoff[:, q * LANE:(q + 1) * LANE] + xs[:, sl] * dskx[:, sl]).astype(y_ref.dtype)
            snew = _tn((xdt[:, gc] * dec[:, gc]).astype(MXU_DTYPE), bmat)
            for r in range(HEADS_PER_GROUP):
                h = g * HEADS_PER_GROUP + r
                rows = slice(h * SSM_HEADDIM, (h + 1) * SSM_HEADDIM)
                e = jnp.exp(cst_ref[0, h:h + 1, n - 1:n])
                prev[rows, :] = prev[rows, :] * e + snew[r * SSM_HEADDIM:(r + 1) * SSM_HEADDIM, :]

    return pl.pallas_call(
        body, grid=(t // n,),
        in_specs=[xs_s, bm_s, cm_s, hv_s, hv_s, hvt_s, hvt_s, _par_spec((1, SSM_INNER))],
        out_specs=[xs_s, st_s],
        out_shape=[jax.ShapeDtypeStruct((t, SSM_INNER), STASH_DTYPE), jax.ShapeDtypeStruct((t // n, SSM_INNER, SSM_STATE), F32)],
        scratch_shapes=[pltpu.VMEM((SSM_INNER, SSM_STATE), F32)],
        compiler_params=_params("arbitrary"), name=name)(xbc, xbc, xbc, dt, cs, dtt, cst, dskx)


def _ssd_bwd(dy, xbc, dt, cs, dtt, cst, st, dskx, a_log8, dt_raw, dt_bias8, *, nc, name):
    t = xbc.shape[0]
    n = CHUNK
    xs_s, bm_s, cm_s, hv_s, hvt_s, st_s = _ssd_specs(nc, True)
    acc_s = _par_spec((1, HEAD_PAD))
    xbc_s = pl.BlockSpec((n, SSM_CONV_DIM), xs_s.index_map)

    def body(dy_ref, xs_ref, bm_ref, cm_ref, dt_ref, cs_ref, dtt_ref, cst_ref, st_ref, dsk_ref, al_ref, raw_ref, bias_ref,
             dxbc_ref, ddr_ref, dal_ref, dds_ref, dbias_ref, dprev, dxdt_s, tdec_s, tcs_s):
        @pl.when(pl.program_id(0) % nc == 0)
        def _():
            dprev[...] = jnp.zeros_like(dprev)

        @pl.when(pl.program_id(0) == 0)
        def _():
            dal_ref[...] = jnp.zeros_like(dal_ref)
            dds_ref[...] = jnp.zeros_like(dds_ref)
            dbias_ref[...] = jnp.zeros_like(dbias_ref)

        em = _expand_mat()
        rm = _reduce_mat()

        def head_reduce(v):
            return _dot_exact(v, rm, _DN_NN, 0)

        dtv = dt_ref[...]
        csv = cs_ref[...]
        dtx = _expand(dtv, em)
        csx = _expand(csv, em)
        dskx = dsk_ref[...]
        xs = xs_ref[...].astype(F32)
        dyv = dy_ref[...].astype(F32)
        xdt = xs * dtx
        ecs = jnp.exp(csx)
        dec = jnp.exp(csx[n - 1:n, :] - csx)
        mask = _causal(n)
        lane = lax.broadcasted_iota(jnp.int32, (n, LANE), 1)
        hlane = lax.broadcasted_iota(jnp.int32, (1, HEAD_PAD), 1)
        hsub = lax.broadcasted_iota(jnp.int32, (HEAD_PAD, 1), 0)
        rsum = jnp.zeros((n, HEAD_PAD), F32)
        csum = jnp.zeros((HEAD_PAD, n), F32)
        for g in range(SSM_GROUPS):
            gs = slice(g * SSM_STATE, (g + 1) * SSM_STATE)
            gc = slice(g * GROUP_CH, (g + 1) * GROUP_CH)
            cmat = cm_ref[:, gs].astype(MXU_DTYPE)
            bmat = bm_ref[:, gs].astype(MXU_DTYPE)
            cb = _nt(cmat, bmat)
            pg = st_ref[0, gc, :].astype(MXU_DTYPE)
            dpg = dprev[gc, :]
            dpgb = dpg.astype(MXU_DTYPE)
            z = _nt(cmat, pg)
            dyg = dyv[:, gc]
            dz = (dyg * ecs[:, gc]).astype(MXU_DTYPE)
            dc = _nn(dz, pg)
            dprev_y = _tn(dz, cmat)
            tcs_s[:, gc] = dyg * z * ecs[:, gc]
            xd = xdt[:, gc] * dec[:, gc]
            wmat = _nt(bmat, dpgb)
            db = _nn(xd.astype(MXU_DTYPE), dpgb)
            tdec_s[:, gc] = wmat * xd
            dxdt_g = wmat * dec[:, gc]
            dcb = jnp.zeros((n, n), F32)
            for q in range(PAIRS_PER_GROUP):
                hp = g * PAIRS_PER_GROUP + q
                sl = slice(hp * LANE, (hp + 1) * LANE)
                xp = xdt[:, sl].astype(MXU_DTYPE)
                dyp = dyv[:, sl]
                dypb = dyp.astype(MXU_DTYPE)
                dxp = None
                for hh in range(2):
                    h = 2 * hp + hh
                    lm = _decay_mat(cs_ref, cst_ref, h, mask)
                    mine = (lane < SSM_HEADDIM) if hh == 0 else (lane >= SSM_HEADDIM)
                    dm = _nt(jnp.where(mine, dyp, 0.0).astype(MXU_DTYPE), xp)
                    dml = dm * lm
                    dcb = dcb + dml
                    gseg = dml * cb
                    rsum = rsum + jnp.sum(gseg, axis=1, keepdims=True) * (hlane == h).astype(F32)
                    csum = csum + (hsub == h).astype(F32) * jnp.sum(gseg, axis=0, keepdims=True)
                    dxh = _tn((cb * lm).astype(MXU_DTYPE), dypb)
                    dxp = dxh if dxp is None else jnp.where(mine, dxh, dxp)
                dxdt_s[:, sl] = dxdt_g[:, q * LANE:(q + 1) * LANE] + dxp
            dcbb = dcb.astype(MXU_DTYPE)
            dxbc_ref[:, CM_COL0 + g * SSM_STATE:CM_COL0 + (g + 1) * SSM_STATE] = (dc + _nn(dcbb, bmat)).astype(dxbc_ref.dtype)
            dxbc_ref[:, BM_COL0 + g * SSM_STATE:BM_COL0 + (g + 1) * SSM_STATE] = (db + _tn(dcbb, cmat)).astype(dxbc_ref.dtype)
            for r in range(HEADS_PER_GROUP):
                h = g * HEADS_PER_GROUP + r
                rows = slice(h * SSM_HEADDIM, (h + 1) * SSM_HEADDIM)
                lr = slice(r * SSM_HEADDIM, (r + 1) * SSM_HEADDIM)
                e = jnp.exp(cst_ref[0, h:h + 1, n - 1:n])
                dprev[rows, :] = dpg[lr, :] * e + dprev_y[lr, :]
            tq = _dot_exact(dpg * st_ref[0, gc, :], rm[gc, :], _DN_TN, 0)
            if g == 0:
                qsum = jnp.sum(tq, axis=0, keepdims=True)
            else:
                qsum = qsum + jnp.sum(tq, axis=0, keepdims=True)
        dxdt = dxdt_s[...]
        dxbc_ref[:, 0:SSM_INNER] = (dxdt * dtx + dyv * dskx).astype(dxbc_ref.dtype)
        ddt = head_reduce(dxdt * xs)
        edec = head_reduce(tdec_s[...])
        ycs = head_reduce(tcs_s[...])
        row = lax.broadcasted_iota(jnp.int32, (n, HEAD_PAD), 0)
        extra = jnp.sum(edec, axis=0, keepdims=True) + qsum * jnp.exp(csv[n - 1:n, :])
        dcs = rsum - csum.T + ycs - edec + jnp.where(row == n - 1, extra, 0.0)
        r2 = lax.broadcasted_iota(jnp.int32, (n, n), 0)
        c2 = lax.broadcasted_iota(jnp.int32, (n, n), 1)
        dda = _dot_exact((c2 >= r2).astype(F32), dcs, _DN_NN, 1)
        a_row = -jnp.exp(al_ref[0:1, :])
        ddt = ddt + dda * a_row
        dal_ref[...] += jnp.sum(dda * dtv, axis=0, keepdims=True) * a_row
        dds_ref[...] += jnp.sum(head_reduce(dyv * xs), axis=0, keepdims=True)
        ddr = ddt * _sigmoid(raw_ref[...] + bias_ref[0:1, :])
        ddr_ref[...] = ddr
        dbias_ref[...] += jnp.sum(ddr, axis=0, keepdims=True)

    par8 = _par_spec((SUBLANE, HEAD_PAD))
    return pl.pallas_call(
        body, grid=(t // n,),
        in_specs=[xs_s, xs_s, bm_s, cm_s, hv_s, hv_s, hvt_s, hvt_s, st_s, _par_spec((1, SSM_INNER)), par8, hv_s, par8],
        out_specs=[xbc_s, hv_s, acc_s, acc_s, acc_s],
        out_shape=[jax.ShapeDtypeStruct((t, SSM_CONV_DIM), STASH_DTYPE), jax.ShapeDtypeStruct((t, HEAD_PAD), F32),
                   jax.ShapeDtypeStruct((1, HEAD_PAD), F32), jax.ShapeDtypeStruct((1, HEAD_PAD), F32),
                   jax.ShapeDtypeStruct((1, HEAD_PAD), F32)],
        scratch_shapes=[pltpu.VMEM((SSM_INNER, SSM_STATE), F32), pltpu.VMEM((n, SSM_INNER), F32),
                        pltpu.VMEM((n, SSM_INNER), F32), pltpu.VMEM((n, SSM_INNER), F32)],
        compiler_params=_params("arbitrary"), name=name)(dy, xbc, xbc, xbc, dt, cs, dtt, cst, st, dskx, a_log8, dt_raw, dt_bias8)


def _gate_norm_fwd(y, proj, norm_g, *, name):
    t, c = y.shape
    tm = _pick(t, (256, 128))

    def body(y_ref, z_ref, g_ref, o_ref):
        z = z_ref[...].astype(F32)
        yz = y_ref[...].astype(F32) * z * _sigmoid(z)
        for g in range(SSM_GROUPS):
            gc = slice(g * GROUP_CH, (g + 1) * GROUP_CH)
            seg = yz[:, gc]
            r = lax.rsqrt(jnp.mean(seg * seg, axis=-1, keepdims=True) + RMS_EPS)
            o_ref[:, gc] = (seg * r * g_ref[:, gc]).astype(MXU_DTYPE)

    return pl.pallas_call(
        body, grid=(t // tm,), in_specs=[_row_spec(tm, c), _row_spec(tm, c, 1), _par_spec((1, c))],
        out_specs=_row_spec(tm, c), out_shape=jax.ShapeDtypeStruct((t, c), MXU_DTYPE),
        compiler_params=_params("parallel"), name=name)(y, proj, norm_g.reshape(1, c))


def _gate_norm_bwd(dyb, y, proj, norm_g, dproj, *, name):
    t, c = y.shape
    tm = _pick(t, (256, 128))

    def body(d_ref, y_ref, z_ref, g_ref, _, dy_ref, dz_ref, dg_ref):
        @pl.when(pl.program_id(0) == 0)
        def _():
            dg_ref[...] = jnp.zeros_like(dg_ref)

        z = z_ref[...].astype(F32)
        yv = y_ref[...].astype(F32)
        sz = _sigmoid(z)
        silu = z * sz
        yz = yv * silu
        dv = d_ref[...].astype(F32)
        for g in range(SSM_GROUPS):
            gc = slice(g * GROUP_CH, (g + 1) * GROUP_CH)
            seg = yz[:, gc]
            r = lax.rsqrt(jnp.mean(seg * seg, axis=-1, keepdims=True) + RMS_EPS)
            nrm = seg * r
            dn = dv[:, gc] * g_ref[:, gc]
            dg_ref[:, gc] += jnp.sum(dv[:, gc] * nrm, axis=0, keepdims=True)
            dyz = r * (dn - nrm * jnp.mean(dn * nrm, axis=-1, keepdims=True))
            dy_ref[:, gc] = (dyz * silu[:, gc]).astype(dy_ref.dtype)
            dz_ref[:, gc] = (dyz * yv[:, gc] * (sz[:, gc] * (1.0 + z[:, gc] * (1.0 - sz[:, gc])))).astype(MXU_DTYPE)

    return pl.pallas_call(
        body, grid=(t // tm,), in_specs=[_row_spec(tm, c), _row_spec(tm, c), _row_spec(tm, c, 1), _par_spec((1, c)), _ANY],
        out_specs=[_row_spec(tm, c), _row_spec(tm, c, 1), _par_spec((1, c))],
        out_shape=[jax.ShapeDtypeStruct((t, c), STASH_DTYPE), jax.ShapeDtypeStruct(dproj.shape, dproj.dtype),
                   jax.ShapeDtypeStruct((1, c), F32)],
        input_output_aliases={4: 1},
        compiler_params=_params("arbitrary"), name=name)(dyb, y, proj, norm_g.reshape(1, c), dproj)


GA_COLBLK = GAB_COL0 // D_MODEL


def _merge_fwd(br_a, br_b, proj, *, name):
    t, c = br_a.shape
    tm = _pick(t, ROW_TILES)

    def body(a_ref, b_ref, ga_ref, gb_ref, o_ref):
        o_ref[...] = (_sigmoid(ga_ref[...].astype(F32)) * a_ref[...].astype(F32)
                      + _sigmoid(gb_ref[...].astype(F32)) * b_ref[...].astype(F32)).astype(MXU_DTYPE)

    return pl.pallas_call(
        body, grid=(t // tm,),
        in_specs=[_row_spec(tm, c), _row_spec(tm, c), _row_spec(tm, c, GA_COLBLK), _row_spec(tm, c, GA_COLBLK + 1)],
        out_specs=_row_spec(tm, c), out_shape=jax.ShapeDtypeStruct((t, c), MXU_DTYPE),
        compiler_params=_params("parallel"), name=name)(br_a, br_b, proj, proj)


def _merge_bwd(dm, br_a, br_b, proj, *, name):
    t, c = br_a.shape
    tm = _pick(t, ROW_TILES)

    def body(dm_ref, a_ref, b_ref, ga_ref, gb_ref, da_ref, db_ref, dg_ref):
        d = dm_ref[...].astype(F32)
        sa = _sigmoid(ga_ref[...].astype(F32))
        sb = _sigmoid(gb_ref[...].astype(F32))
        da_ref[...] = (d * sa).astype(MXU_DTYPE)
        db_ref[...] = (d * sb).astype(MXU_DTYPE)
        dg_ref[:, :c] = (d * a_ref[...].astype(F32) * sa * (1.0 - sa)).astype(MXU_DTYPE)
        dg_ref[:, c:] = (d * b_ref[...].astype(F32) * sb * (1.0 - sb)).astype(MXU_DTYPE)

    return pl.pallas_call(
        body, grid=(t // tm,),
        in_specs=[_row_spec(tm, c), _row_spec(tm, c), _row_spec(tm, c), _row_spec(tm, c, GA_COLBLK), _row_spec(tm, c, GA_COLBLK + 1)],
        out_specs=[_row_spec(tm, c), _row_spec(tm, c), _row_spec(tm, 2 * c, GAB_COL0 // (2 * c))],
        out_shape=[jax.ShapeDtypeStruct((t, c), MXU_DTYPE), jax.ShapeDtypeStruct((t, c), MXU_DTYPE),
                   jax.ShapeDtypeStruct((t, MAIN_COLS), MXU_DTYPE)],
        compiler_params=_params("parallel"), name=name)(dm, br_a, br_b, proj, proj)


def _layer_fwd(x, xb, memn_b, w, *, bsz, tag):
    nc = x.shape[0] // bsz // CHUNK
    sv = {"x_in": xb}
    proj = _mm(xb, w["w_main"], out_dtype=STASH_DTYPE, name=f"{tag}_proj")
    dt_raw = _mm(xb, w["w_dt"], name=f"{tag}_dtproj")
    sgo = _sg_fwd(proj, w["sg_ln_g"], w["sg_ln_b"], w["sg_w"], w["sg_bcol"], name=f"{tag}_sg_fwd")
    xbc = _conv_fwd(proj, w["conv_w"], w["conv_b"], bsz=bsz, name=f"{tag}_conv_fwd")
    dt, cs, dtt, cst = _ssd_prep(dt_raw, w["dt_bias8"], w["a_log8"], name=f"{tag}_ssd_prep")
    y, st = _ssd_fwd(xbc, dt, cs, dtt, cst, w["d_skipx"], nc=nc, name=f"{tag}_ssd_fwd")
    yb = _gate_norm_fwd(y, proj, w["ssm_norm_g"], name=f"{tag}_gate_norm_fwd")
    if "rest" in w:
        w = w["rest"](w, yb)
    br_a = _mm(sgo, w["p_a"], out_dtype=STASH_DTYPE, name=f"{tag}_br_a")
    br_b = _mm(yb, w["p_b"], out_dtype=STASH_DTYPE, name=f"{tag}_br_b")
    merged = _merge_fwd(br_a, br_b, proj, name=f"{tag}_merge_fwd")
    mix = _mm(merged, w["w_mix_o"], name=f"{tag}_mix_o")
    x1, x1b, xh1, rs1 = _ln_fwd(x, mix, w["ln_g"][0], w["ln_b"][0], name=f"{tag}_ln1_fwd")
    sv.update(proj=proj, dt_raw=dt_raw, sgo=sgo, xbc=xbc, dt=dt, cs=cs, dtt=dtt, cst=cst, y=y, st=st, yb=yb,
              br_a=br_a, br_b=br_b, merged=merged, xh1=xh1, rs1=rs1, x1b=x1b)
    q = _mm(x1b, w["w_xq"], out_dtype=MXU_DTYPE, name=f"{tag}_q")
    kv = _mm(memn_b, w["w_xkv"], out_dtype=MXU_DTYPE, name=f"{tag}_kv")
    o = _attn_fwd(q, kv, bsz=bsz, name=f"{tag}_attn_fwd")
    att = _mm(o, w["w_xo"], name=f"{tag}_xo")
    x2, x2b, xh2, rs2 = _ln_fwd(x1, att, w["ln_g"][1], w["ln_b"][1], name=f"{tag}_ln2_fwd")
    sv.update(q=q, kv=kv, o=o, xh2=xh2, rs2=rs2, x2b=x2b)
    h = _mm(x2b, w["w_ffn_in"], out_dtype=STASH_DTYPE, name=f"{tag}_ffn_in")
    a = _swiglu_fwd(h, name=f"{tag}_swiglu_fwd")
    ffn = _mm(a, w["w_ffn_out"], name=f"{tag}_ffn_out")
    x3, x3b, xh3, rs3 = _ln_fwd(x2, ffn, w["ln_g"][2], w["ln_b"][2], name=f"{tag}_ln3_fwd")
    sv.update(h=h, a=a, xh3=xh3, rs3=rs3)
    return x3, x3b, sv, w


GRAD_GROUPS = (("w_ffn_out", "w_ffn_in", "w_xo", "w_xq", "w_xkv"), ("w_mix_o", "p_a", "p_b"), ("w_in",))


def _layer_bwd(dx3_addends, dx3_scales, memn_b, w, sv, on_group=None, *, bsz, tag):
    nc = sv["xh1"].shape[0] // bsz // CHUNK
    gr = {}

    def group_done(k):
        return on_group(GRAD_GROUPS[k], gr) if on_group is not None else None
    dp3, dp3b, dg3, db3 = _ln_bwd(dx3_addends, dx3_scales, sv["xh3"], sv["rs3"], w["ln_g"][2], name=f"{tag}_ln3_bwd")
    da = _mm(dp3b, w["w_ffn_out"], tb=True, out_dtype=STASH_DTYPE, name=f"{tag}_d_a")
    gr["w_ffn_out"] = _mm(sv["a"], dp3b, ta=True, name=f"{tag}_dw_ffn_out")
    dh = _swiglu_bwd(sv["h"], da, name=f"{tag}_swiglu_bwd")
    gr["w_ffn_in"] = _mm(sv["x2b"], dh, ta=True, name=f"{tag}_dw_ffn_in")
    dx2_br = _mm(dh, w["w_ffn_in"], tb=True, name=f"{tag}_dx2")
    dp2, dp2b, dg2, db2 = _ln_bwd([dp3, dx2_br], [ALPHA, 1.0], sv["xh2"], sv["rs2"], w["ln_g"][1], name=f"{tag}_ln2_bwd")
    do = _mm(dp2b, w["w_xo"], tb=True, out_dtype=MXU_DTYPE, name=f"{tag}_d_o")
    gr["w_xo"] = _mm(sv["o"], dp2b, ta=True, name=f"{tag}_dw_xo")
    dq, dk, dv = _attn_bwd(sv["q"], sv["kv"], do, bsz=bsz, name=f"{tag}_attn_bwd")
    dkv = jnp.concatenate([dk, dv], axis=1)
    gr["w_xq"] = _mm(sv["x1b"], dq, ta=True, name=f"{tag}_dw_xq")
    gr["w_xkv"] = _mm(memn_b, dkv, ta=True, name=f"{tag}_dw_xkv")
    dmemn = _mm(dkv, w["w_xkv"], tb=True, name=f"{tag}_d_memn")
    dx1_br = _mm(dq, w["w_xq"], tb=True, name=f"{tag}_dx1")
    token = group_done(0)
    ln_g1 = w["ln_g"][0] if token is None else w["ln_g"][0] + token[0, 0]
    dp1, dp1b, dg1, db1 = _ln_bwd([dp2, dx1_br], [ALPHA, 1.0], sv["xh1"], sv["rs1"], ln_g1, name=f"{tag}_ln1_bwd")
    gr["ln_g"] = jnp.concatenate([dg1, dg2, dg3], axis=0)
    gr["ln_b"] = jnp.concatenate([db1, db2, db3], axis=0)
    dmerged = _mm(dp1b, w["w_mix_o"], tb=True, out_dtype=STASH_DTYPE, name=f"{tag}_d_merged")
    gr["w_mix_o"] = _mm(sv["merged"], dp1b, ta=True, name=f"{tag}_dw_mix_o")
    dbr_a, dbr_b, dproj = _merge_bwd(dmerged, sv["br_a"], sv["br_b"], sv["proj"], name=f"{tag}_merge_bwd")
    gr["p_a"] = _mm(sv["sgo"], dbr_a, ta=True, name=f"{tag}_dw_p_a")
    gr["p_b"] = _mm(sv["yb"], dbr_b, ta=True, name=f"{tag}_dw_p_b")
    dsgo = _mm(dbr_a, w["p_a"], tb=True, out_dtype=STASH_DTYPE, name=f"{tag}_d_sgo")
    dyb = _mm(dbr_b, w["p_b"], tb=True, out_dtype=STASH_DTYPE, name=f"{tag}_d_yb")
    token = group_done(1)
    norm_g = w["ssm_norm_g"] if token is None else w["ssm_norm_g"] + token[0, 0]
    dy, dproj, gr["ssm_norm_g"] = _gate_norm_bwd(dyb, sv["y"], sv["proj"], norm_g, dproj, name=f"{tag}_gate_norm_bwd")
    dxbc, ddr, gr["a_log"], gr["d_skip"], gr["dt_bias"] = _ssd_bwd(
        dy, sv["xbc"], sv["dt"], sv["cs"], sv["dtt"], sv["cst"], sv["st"], w["d_skipx"], w["a_log8"], sv["dt_raw"],
        w["dt_bias8"], nc=nc, name=f"{tag}_ssd_bwd")
    dproj, gr["conv_w"], gr["conv_b"] = _conv_bwd(sv["proj"], dxbc, w["conv_w"], w["conv_b"], dproj, bsz=bsz, name=f"{tag}_conv_bwd")
    dproj, gr["sg_w"], dsg_bcol, gr["sg_ln_g"], gr["sg_ln_b"] = _sg_bwd(
        sv["proj"], dsgo, w["sg_ln_g"], w["sg_ln_b"], w["sg_w"], w["sg_bcol"], dproj, name=f"{tag}_sg_bwd")
    gr["sg_b"] = dsg_bcol[..., 0]
    gr["w_main"] = _mm(sv["x_in"], dproj, ta=True, name=f"{tag}_dw_main")
    gr["w_dt"] = _mm(sv["x_in"], ddr, ta=True, name=f"{tag}_dw_dt")
    token = group_done(2)
    dx_dt = _mm(ddr, w["w_dt"], tb=True, after=token, name=f"{tag}_dx_dt")
    dx_main = _mm(dproj, w["w_main"], tb=True, after=token, name=f"{tag}_dx_main")
    return [dp1, dx_main, dx_dt], [ALPHA, 1.0, 1.0], gr, dmemn


def _local_step(x, mem, tgt, mem_ln_g, mem_ln_b, layers, on_layer_grads=None):
    bsz, s, d = x.shape
    xf = x.reshape(bsz * s, d)
    memf = mem.reshape(-1, d)
    _, memn_b, mxh, mrs = _ln_fwd(memf, None, mem_ln_g, mem_ln_b, name="mem_ln_fwd")
    cur, curb, saved, weights = xf, xf, [], []
    for li, get_weights in enumerate(layers):
        cur, curb, sv, w = _layer_fwd(cur, curb, memn_b, get_weights(cur), bsz=bsz, tag=f"l{li}")
        saved.append(sv)
        weights.append(w)
    dy, lsum = _loss_head(cur, tgt.reshape(bsz * s, d), name="loss_head")
    addends, scales = [dy], [1.0]
    grads, dmem = [None] * len(layers), []
    for li in reversed(range(len(layers))):
        on_group = None if on_layer_grads is None else functools.partial(on_layer_grads, li)
        addends, scales, grads[li], dm = _layer_bwd(addends, scales, memn_b, weights[li], saved[li], on_group, bsz=bsz, tag=f"l{li}")
        dmem.append(dm)
    grad_x = _add_scaled(addends, scales, name="grad_x").reshape(bsz, s, d)
    _, _, dmg, dmb = _ln_bwd(dmem, [1.0] * len(dmem), mxh, mrs, mem_ln_g, name="mem_ln_bwd")
    return lsum, grad_x, grads, dmg[0], dmb[0]


_ANY = pl.BlockSpec(memory_space=pl.ANY)
_MESH = pl.DeviceIdType.MESH


def _all_gather8(x, *, name):
    def body(x_ref, out_ref, send_sems, recv_sems):
        mx, my, mc = lax.axis_index("x"), lax.axis_index("y"), lax.axis_index("c")
        me, sibling = (mx, my, mc), (mx, my, 1 - mc)
        chips = [(1 - mx, my), (mx, 1 - my), (1 - mx, 1 - my)]

        def blk(px, py, pc):
            return out_ref.at[4 * px + 2 * py + pc]

        def copy(k, block, to, src=None):
            return pltpu.make_async_remote_copy(
                src_ref=blk(*block) if src is None else src, dst_ref=blk(*block), send_sem=send_sems.at[k],
                recv_sem=recv_sems.at[k], device_id=to, device_id_type=_MESH)

        first = [copy(0, me, sibling, src=x_ref)]
        first += [copy(1 + j, me, (*chip, mc), src=x_ref) for j, chip in enumerate(chips)]
        for cp in first:
            cp.start()
        passed = [copy(4 + j, (*chip, mc), sibling) for j, chip in enumerate(chips)]
        for j, chip in enumerate(chips):
            copy(1 + j, (*chip, mc), me).wait_recv()
            passed[j].start()
        copy(0, sibling, me).wait_recv()
        for j, chip in enumerate(chips):
            copy(4 + j, (*chip, 1 - mc), me).wait_recv()
        for cp in first + passed:
            cp.wait_send()

    return pl.pallas_call(
        body, out_shape=jax.ShapeDtypeStruct((N_DEV,) + x.shape, x.dtype), in_specs=[_ANY], out_specs=_ANY,
        scratch_shapes=[pltpu.SemaphoreType.DMA((7,)), pltpu.SemaphoreType.DMA((7,))], name=name)(x)


def _row_tile(rows, row_bytes, mult=SUBLANE):
    best = None
    for tr in range(mult, rows + 1, mult):
        if rows % tr == 0 and (best is None or tr * row_bytes <= BLOCK_BYTES):
            best = tr
    return rows if best is None else best


def _gather_shape(r, c, kind):
    return {"row": (2, N_CHIPS * r, c), "col": (2, r, N_CHIPS * c), "chip": (2, N_CHIPS, r, c)}[kind]


def _cast_place(shard, kind, dtype, chip_idx, *, name):
    _, r, c = shard.shape
    tr = _row_tile(r, c * 4, 16)
    nt = r // tr

    def body(_, s_ref, o_ref):
        o_ref[...] = s_ref[...].astype(dtype)

    if kind == "row":
        out_spec = pl.BlockSpec((None, tr, c), lambda l, i, j_ref: (l, j_ref[0] * nt + i, 0))
    elif kind == "col":
        out_spec = pl.BlockSpec((None, tr, c), lambda l, i, j_ref: (l, i, j_ref[0]))
    else:
        out_spec = pl.BlockSpec((None, None, tr, c), lambda l, i, j_ref: (l, j_ref[0], i, 0))
    grid_spec = pltpu.PrefetchScalarGridSpec(
        num_scalar_prefetch=1, grid=(2, nt), in_specs=[pl.BlockSpec((None, tr, c), lambda l, i, j_ref: (l, i, 0))],
        out_specs=out_spec)
    return pl.pallas_call(body, grid_spec=grid_spec, out_shape=jax.ShapeDtypeStruct(_gather_shape(r, c, kind), dtype),
                          compiler_params=_params("parallel", "parallel"), name=name)(chip_idx, shard)


def _gather_params(bufs, shard_shapes, kinds, *, name):
    n = len(bufs)

    def body(*refs):
        outs = refs[n:2 * n]
        send_sems, recv_sems = refs[2 * n:]
        mx, my, mc = lax.axis_index("x"), lax.axis_index("y"), lax.axis_index("c")
        me, sibling = (mx, my, mc), (mx, my, 1 - mc)
        chips = [(1 - mx, my), (mx, 1 - my), (1 - mx, 1 - my)]

        def blk(i, px, py, pc):
            r, c = shard_shapes[i]
            j = 2 * px + py
            if kinds[i] == "row":
                return outs[i].at[pc, pl.ds(pl.multiple_of(j * r, r), r)]
            if kinds[i] == "col":
                return outs[i].at[pc, :, pl.ds(pl.multiple_of(j * c, c), c)]
            return outs[i].at[pc, j]

        def copy(i, k, block, to):
            return pltpu.make_async_remote_copy(
                src_ref=blk(i, *block), dst_ref=blk(i, *block), send_sem=send_sems.at[6 * i + k],
                recv_sem=recv_sems.at[6 * i + k], device_id=to, device_id_type=_MESH)

        sent = []
        for i in range(n):
            for j, chip in enumerate(chips):
                cp = copy(i, j, me, (*chip, mc))
                cp.start()
                sent.append(cp)
        for j, chip in enumerate(chips):
            for i in range(n):
                copy(i, j, (*chip, mc), me).wait_recv()
                fwd = copy(i, 3 + j, (*chip, mc), sibling)
                fwd.start()
                sent.append(fwd)
        for i in range(n):
            for j, chip in enumerate(chips):
                copy(i, 3 + j, (*chip, 1 - mc), me).wait_recv()
        for cp in sent:
            cp.wait_send()

    return pl.pallas_call(
        body, out_shape=[jax.ShapeDtypeStruct(b.shape, b.dtype) for b in bufs], in_specs=[_ANY] * n, out_specs=[_ANY] * n,
        input_output_aliases={i: i for i in range(n)},
        scratch_shapes=[pltpu.SemaphoreType.DMA((6 * n,)), pltpu.SemaphoreType.DMA((6 * n,))], name=name)(*bufs)


def _half(r, h):
    return pl.ds(pl.multiple_of(h * (r // 2), r // 2), r // 2)


_HBM = pl.BlockSpec(memory_space=pltpu.HBM)
_SEM = pl.BlockSpec(memory_space=pltpu.SEMAPHORE)
_EFFECT = pltpu.SideEffectType.DATAFLOW_SIDE_EFFECTING


def _sibling_copies(g_refs, land_refs, gs, views, send_sems, recv_sems):
    mx, my, mc = lax.axis_index("x"), lax.axis_index("y"), lax.axis_index("c")
    copies = []
    for i in range(len(gs)):
        if views[i] == "chip":
            src = g_refs[i].at[:, _half(gs[i].shape[1], 1 - mc)]
        else:
            src = g_refs[i].at[_half(gs[i].shape[0], 1 - mc)]
        copies.append(pltpu.make_async_remote_copy(src_ref=src, dst_ref=land_refs[i], send_sem=send_sems.at[i], recv_sem=recv_sems.at[i],
                                                   device_id=(mx, my, 1 - mc), device_id_type=_MESH))
    return copies


def _half_shape(g, view):
    return (g.shape[0], g.shape[1] // 2, g.shape[2]) if view == "chip" else (g.shape[0] // 2, g.shape[1])


def _grads_to_sibling_start(gs, views, *, name):
    n = len(gs)
    lands = [pltpu.with_memory_space_constraint(lax.empty(_half_shape(g, v), g.dtype), pltpu.HBM) for g, v in zip(gs, views)]

    def body(*refs):
        for cp in _sibling_copies(refs[:n], refs[n:2 * n], gs, views, refs[2 * n], refs[2 * n + 1]):
            cp.start()
        refs[-1][...] = jnp.zeros_like(refs[-1])

    outs = pl.pallas_call(
        body, name=name,
        out_shape=(pltpu.SemaphoreType.DMA((n,)), pltpu.SemaphoreType.DMA((n,)),
                   *[pltpu.HBM(x.shape, x.dtype) for x in list(gs) + lands], jax.ShapeDtypeStruct((SUBLANE, LANE), F32)),
        in_specs=[_HBM] * (2 * n), out_specs=(_SEM, _SEM, *[_HBM] * (2 * n), pl.BlockSpec(memory_space=pltpu.VMEM)),
        input_output_aliases={i: 2 + i for i in range(2 * n)},
        compiler_params=pltpu.CompilerParams(has_side_effects=_EFFECT),
    )(*[pltpu.with_memory_space_constraint(g, pltpu.HBM) for g in gs], *lands)
    return outs[0], outs[1], list(outs[2:2 + n]), list(outs[2 + n:2 + 2 * n]), outs[-1]


def _grads_to_sibling_wait(send_sems, recv_sems, gs, lands, views, after, *, name):
    n = len(gs)

    def body(*refs):
        for cp in _sibling_copies(refs[:n], refs[n:2 * n], gs, views, refs[2 * n], refs[2 * n + 1]):
            cp.wait_send()
            cp.wait_recv()

    outs = pl.pallas_call(
        body, name=name, out_shape=tuple(pltpu.HBM(x.shape, x.dtype) for x in list(gs) + list(lands)),
        in_specs=[_HBM] * (2 * n) + [_SEM, _SEM, _ANY], out_specs=tuple([_HBM] * (2 * n)),
        input_output_aliases={i: i for i in range(2 * n)},
        compiler_params=pltpu.CompilerParams(has_side_effects=_EFFECT),
    )(*gs, *lands, send_sems, recv_sems, after)
    return list(outs[:n]), list(outs[n:])


def _cast_place_layer(shard, l, kind, chip_idx, after, *, name):
    _, r, c = shard.shape
    tr = _row_tile(r, c * 4, 16)
    nt = r // tr

    def body(_, s_ref, *rest):
        rest[-1][...] = s_ref[...].astype(MXU_DTYPE)

    if kind == "row":
        out_spec = pl.BlockSpec((tr, c), lambda i, j_ref: (j_ref[0] * nt + i, 0))
    elif kind == "col":
        out_spec = pl.BlockSpec((tr, c), lambda i, j_ref: (i, j_ref[0]))
    else:
        out_spec = pl.BlockSpec((None, tr, c), lambda i, j_ref: (j_ref[0], i, 0))
    extra = [] if after is None else [after]
    grid_spec = pltpu.PrefetchScalarGridSpec(
        num_scalar_prefetch=1, grid=(nt,), in_specs=[pl.BlockSpec((None, tr, c), lambda i, j_ref: (l, i, 0))] + [_ANY] * len(extra),
        out_specs=out_spec)
    return pl.pallas_call(body, grid_spec=grid_spec, out_shape=jax.ShapeDtypeStruct(_gather_shape(r, c, kind)[1:], MXU_DTYPE),
                          compiler_params=_params("parallel"), name=name)(chip_idx, shard, *extra)


def _half_block(ref, kind, r, c, j, h):
    rows = _half(r, h)
    if kind == "row":
        return ref.at[pl.ds(pl.multiple_of(j * r + h * (r // 2), r // 2), r // 2)]
    if kind == "col":
        return ref.at[rows, pl.ds(pl.multiple_of(j * c, c), c)]
    return ref.at[j, rows]


def _gather_ici_copies(buf_refs, shapes, kinds, send_sems, recv_sems):
    mx, my, mc = lax.axis_index("x"), lax.axis_index("y"), lax.axis_index("c")
    chips = [(1 - mx, my), (mx, 1 - my), (1 - mx, 1 - my)]
    copies = []
    for i, (r, c) in enumerate(shapes):
        mine = _half_block(buf_refs[i], kinds[i], r, c, 2 * mx + my, mc)
        for k, (px, py) in enumerate(chips):
            copies.append(pltpu.make_async_remote_copy(
                src_ref=mine, dst_ref=mine, send_sem=send_sems.at[3 * i + k], recv_sem=recv_sems.at[3 * i + k],
                device_id=(px, py, mc), device_id_type=_MESH))
    return copies


def _gather_start(bufs, shapes, kinds, *, name):
    n = len(bufs)

    def body(*refs):
        send_sems, recv_sems, token = refs[n], refs[n + 1], refs[-1]
        for cp in _gather_ici_copies(refs[:n], shapes, kinds, send_sems, recv_sems):
            cp.start()
        token[...] = jnp.zeros_like(token)

    outs = pl.pallas_call(
        body, name=name,
        out_shape=(pltpu.SemaphoreType.DMA((3 * n,)), pltpu.SemaphoreType.DMA((3 * n,)),
                   *[pltpu.HBM(b.shape, b.dtype) for b in bufs], jax.ShapeDtypeStruct((SUBLANE, LANE), F32)),
        in_specs=[_HBM] * n, out_specs=(_SEM, _SEM, *[_HBM] * n, pl.BlockSpec(memory_space=pltpu.VMEM)),
        input_output_aliases={i: 2 + i for i in range(n)},
        compiler_params=pltpu.CompilerParams(has_side_effects=_EFFECT),
    )(*[pltpu.with_memory_space_constraint(b, pltpu.HBM) for b in bufs])
    return outs[0], outs[1], list(outs[2:2 + n]), outs[-1]


def _gather_wait(send_sems, recv_sems, bufs, shapes, kinds, after, *, name):
    n = len(bufs)

    def body(*refs):
        for cp in _gather_ici_copies(refs[:n], shapes, kinds, refs[n], refs[n + 1]):
            cp.wait_send()
            cp.wait_recv()

    outs = pl.pallas_call(
        body, name=name, out_shape=tuple(pltpu.HBM(b.shape, b.dtype) for b in bufs),
        in_specs=[_HBM] * n + [_SEM, _SEM, _ANY], out_specs=tuple([_HBM] * n), input_output_aliases={i: i for i in range(n)},
        compiler_params=pltpu.CompilerParams(has_side_effects=_EFFECT),
    )(*bufs, send_sems, recv_sems, after)
    return list(outs)


def _gather_forward(bufs, shapes, kinds, *, name):
    n = len(bufs)

    def body(*refs):
        outs = refs[n:2 * n]
        send_sems, recv_sems = refs[2 * n:]
        mx, my, mc = lax.axis_index("x"), lax.axis_index("y"), lax.axis_index("c")
        chips = [(1 - mx, my), (mx, 1 - my), (1 - mx, 1 - my)]
        copies = []
        for i, (r, c) in enumerate(shapes):
            for k, (px, py) in enumerate(chips):
                got = _half_block(outs[i], kinds[i], r, c, 2 * px + py, mc)
                cp = pltpu.make_async_remote_copy(src_ref=got, dst_ref=got, send_sem=send_sems.at[3 * i + k],
                                                  recv_sem=recv_sems.at[3 * i + k], device_id=(mx, my, 1 - mc), device_id_type=_MESH)
                cp.start()
                copies.append(cp)
        for cp in copies:
            cp.wait()

    return pl.pallas_call(
        body, out_shape=[jax.ShapeDtypeStruct(b.shape, b.dtype) for b in bufs], in_specs=[_ANY] * n, out_specs=[_ANY] * n,
        input_output_aliases={i: i for i in range(n)},
        scratch_shapes=[pltpu.SemaphoreType.DMA((3 * n,)), pltpu.SemaphoreType.DMA((3 * n,))], name=name)(*bufs)


def _chip_exchange_copies(pair_refs, land_refs, pairs, views, send_sems, recv_sems):
    mx, my, mc = lax.axis_index("x"), lax.axis_index("y"), lax.axis_index("c")
    me = 2 * mx + my
    chips = [(1 - mx, my), (mx, 1 - my), (1 - mx, 1 - my)]
    copies = []
    for i in range(len(pairs)):
        for k, (px, py) in enumerate(chips):
            j = 2 * px + py
            if views[i] == "chip":
                src = pair_refs[i].at[j]
            else:
                c = pairs[i].shape[1] // N_CHIPS
                src = pair_refs[i].at[:, pl.ds(pl.multiple_of(j * c, c), c)]
            copies.append(pltpu.make_async_remote_copy(
                src_ref=src, dst_ref=land_refs[i].at[me], send_sem=send_sems.at[3 * i + k], recv_sem=recv_sems.at[3 * i + k],
                device_id=(px, py, mc), device_id_type=_MESH))
    return copies


def _quad_shape(p, view):
    return p.shape if view == "chip" else (N_CHIPS, p.shape[0], p.shape[1] // N_CHIPS)


def _grads_to_chips_start(pairs, views, *, name):
    n = len(pairs)
    lands = [pltpu.with_memory_space_constraint(lax.empty(_quad_shape(p, v), p.dtype), pltpu.HBM) for p, v in zip(pairs, views)]

    def body(*refs):
        pair_refs, land_refs = refs[:n], refs[n:2 * n]
        send_sems, recv_sems = refs[2 * n], refs[2 * n + 1]
        token = refs[-1]
        for cp in _chip_exchange_copies(pair_refs, land_refs, pairs, views, send_sems, recv_sems):
            cp.start()
        token[...] = jnp.zeros_like(token)

    outs = pl.pallas_call(
        body, name=name,
        out_shape=(pltpu.SemaphoreType.DMA((3 * n,)), pltpu.SemaphoreType.DMA((3 * n,)),
                   *[pltpu.HBM(p.shape, p.dtype) for p in pairs], *[pltpu.HBM(l.shape, l.dtype) for l in lands],
                   jax.ShapeDtypeStruct((SUBLANE, LANE), F32)),
        in_specs=[_HBM] * (2 * n), out_specs=(_SEM, _SEM, *[_HBM] * (2 * n), pl.BlockSpec(memory_space=pltpu.VMEM)),
        input_output_aliases={i: 2 + i for i in range(2 * n)},
        compiler_params=pltpu.CompilerParams(has_side_effects=_EFFECT),
    )(*[pltpu.with_memory_space_constraint(p, pltpu.HBM) for p in pairs], *lands)
    return outs[0], outs[1], list(outs[2:2 + n]), list(outs[2 + n:2 + 2 * n]), outs[-1]


def _grads_to_chips_wait(send_sems, recv_sems, pairs, lands, views, after, *, name):
    n = len(pairs)

    def body(*refs):
        pair_refs, land_refs = refs[:n], refs[n:2 * n]
        s_sems, r_sems = refs[2 * n], refs[2 * n + 1]
        for cp in _chip_exchange_copies(pair_refs, land_refs, pairs, views, s_sems, r_sems):
            cp.wait_send()
            cp.wait_recv()

    outs = pl.pallas_call(
        body, name=name, out_shape=tuple(pltpu.HBM(x.shape, x.dtype) for x in list(pairs) + list(lands)),
        in_specs=[_HBM] * (2 * n) + [_SEM, _SEM, _ANY], out_specs=tuple([_HBM] * (2 * n)),
        input_output_aliases={i: i for i in range(2 * n)},
        compiler_params=pltpu.CompilerParams(has_side_effects=_EFFECT),
    )(*pairs, *lands, send_sems, recv_sems, after)
    return list(outs[n:])


def _grads_share(tots, *, name):
    n = len(tots)

    def body(*refs):
        ins, outs = refs[:n], refs[n:2 * n]
        send_sems, recv_sems = refs[2 * n:]
        mx, my, mc = lax.axis_index("x"), lax.axis_index("y"), lax.axis_index("c")
        copies = []
        for i in range(n):
            cp = pltpu.make_async_remote_copy(src_ref=ins[i], dst_ref=outs[i], send_sem=send_sems.at[i], recv_sem=recv_sems.at[i],
                                              device_id=(mx, my, 1 - mc), device_id_type=_MESH)
            cp.start()
            copies.append(cp)
        for cp in copies:
            cp.wait()

    return pl.pallas_call(
        body, out_shape=[jax.ShapeDtypeStruct(t.shape, t.dtype) for t in tots], in_specs=[_ANY] * n, out_specs=[_ANY] * n,
        scratch_shapes=[pltpu.SemaphoreType.DMA((n,)), pltpu.SemaphoreType.DMA((n,))], name=name)(*tots)


def _pair_sum(g, recv, view, c_idx, *, name):
    def body(c_ref, a_ref, b_ref, o_ref):
        o_ref[...] = (a_ref[...] + b_ref[...]).astype(WIRE_DTYPE)

    if view == "chip":
        nch, r, c = g.shape
        tr = _row_tile(r // 2, nch * c * 4, 16)
        gv = g.reshape(nch, 2, r // 2, c)
        grid = ((r // 2) // tr,)
        in_specs = [pl.BlockSpec((nch, None, tr, c), lambda i, c_ref: (0, c_ref[0], i, 0)),
                    pl.BlockSpec((nch, tr, c), lambda i, c_ref: (0, i, 0))]
        out_spec = pl.BlockSpec((nch, tr, c), lambda i, c_ref: (0, i, 0))
        sem = ("parallel",)
    else:
        r, c4 = g.shape
        tr = _row_tile(r // 2, c4 * 4, 16)
        gv = g.reshape(2, r // 2, c4)
        grid = ((r // 2) // tr,)
        in_specs = [pl.BlockSpec((None, tr, c4), lambda i, c_ref: (c_ref[0], i, 0)), pl.BlockSpec((tr, c4), lambda i, c_ref: (i, 0))]
        out_spec = pl.BlockSpec((tr, c4), lambda i, c_ref: (i, 0))
        sem = ("parallel",)
    grid_spec = pltpu.PrefetchScalarGridSpec(num_scalar_prefetch=1, grid=grid, in_specs=in_specs, out_specs=out_spec)
    return pl.pallas_call(body, grid_spec=grid_spec, out_shape=jax.ShapeDtypeStruct(recv.shape, WIRE_DTYPE),
                          compiler_params=_params(*sem), name=name)(c_idx, gv, recv)


def _quad_sum(gs, recvs, quads, view, chip_idx, c_idx, *, name):
    nl = len(quads)
    nch, rh, c = quads[0].shape
    tr = _row_tile(rh, c * 4, 16)

    def body(_, __, *refs):
        o_ref = refs[-1]
        per = nch + 1
        for l in range(nl):
            grp = refs[l * per:(l + 1) * per]
            acc = grp[0][...] + grp[1][...]
            for r in grp[2:]:
                acc = acc + r[...].astype(F32)
            o_ref[l] = acc

    if view == "chip":
        own = [pl.BlockSpec((None, None, tr, c), lambda i, j, h: (j[0], h[0], i, 0)),
               pl.BlockSpec((None, tr, c), lambda i, j, h: (j[0], i, 0))]
        gviews = [g.reshape(nch, 2, rh, c) for g in gs]
    else:
        own = [pl.BlockSpec((None, tr, c), lambda i, j, h: (h[0], i, j[0])), pl.BlockSpec((tr, c), lambda i, j, h: (i, j[0]))]
        gviews = [g.reshape(2, rh, nch * c) for g in gs]
    assert nch & (nch - 1) == 0
    got = [pl.BlockSpec((None, tr, c), functools.partial(lambda i, j, h, k: ((j[0] + k) & (nch - 1), i, 0), k=k))
           for k in range(1, nch)]
    ins = []
    for l in range(nl):
        ins += [gviews[l], recvs[l]] + [quads[l]] * (nch - 1)
    grid_spec = pltpu.PrefetchScalarGridSpec(
        num_scalar_prefetch=2, grid=(rh // tr,), in_specs=(own + got) * nl,
        out_specs=pl.BlockSpec((nl, tr, c), lambda i, j, h: (0, i, 0)))
    return pl.pallas_call(body, grid_spec=grid_spec, out_shape=jax.ShapeDtypeStruct((nl, rh, c), F32),
                          compiler_params=_params("parallel"), name=name)(chip_idx, c_idx, *ins)


def _sum_devices(g8, own, dev_idx, *, name):
    k, rows, cols = g8.shape

    def body(d_ref, a_ref, x_ref, o_ref):
        acc = None
        for i in range(k):
            term = jnp.where(d_ref[0] == i, x_ref[...], a_ref[i])
            acc = term if acc is None else acc + term
        o_ref[...] = acc

    grid_spec = pltpu.PrefetchScalarGridSpec(
        num_scalar_prefetch=1, grid=(1,),
        in_specs=[pl.BlockSpec((k, rows, cols), lambda i, d_ref: (0, 0, 0)), pl.BlockSpec((rows, cols), lambda i, d_ref: (0, 0))],
        out_specs=pl.BlockSpec((rows, cols), lambda i, d_ref: (0, 0)))
    return pl.pallas_call(body, grid_spec=grid_spec, out_shape=jax.ShapeDtypeStruct((rows, cols), g8.dtype),
                          compiler_params=_params("arbitrary"), name=name)(dev_idx, g8, own)


def _adamw(w, g, m, v, *, name):
    rows, cols = w.shape
    tr = rows
    for cand in (256, 128, 64, 32, 16, 8):
        if rows % cand == 0 and cand * cols <= 512 * 1024:
            tr = cand
            break
    c1 = 1.0 - ADAM_B1 ** ADAM_STEP
    c2 = 1.0 - ADAM_B2 ** ADAM_STEP

    def body(w_ref, g_ref, m_ref, v_ref, d_ref, nm_ref, nv_ref):
        gv = g_ref[...]
        nm = ADAM_B1 * m_ref[...] + (1.0 - ADAM_B1) * gv
        nv = ADAM_B2 * v_ref[...] + (1.0 - ADAM_B2) * (gv * gv)
        d_ref[...] = -ADAM_LR * ((nm / c1) / (jnp.sqrt(nv / c2) + ADAM_EPS) + ADAM_WD * w_ref[...])
        nm_ref[...] = nm
        nv_ref[...] = nv

    spec = pl.BlockSpec((tr, cols), lambda i: (i, 0))
    shp = jax.ShapeDtypeStruct((rows, cols), F32)
    return pl.pallas_call(body, grid=(rows // tr,), in_specs=[spec] * 4, out_specs=[spec] * 3, out_shape=[shp] * 3,
                          compiler_params=_params("parallel"), name=name)(w, g, m, v)


def _adamw_halves(w, m, v, mine, other, c_idx, *, name):
    nl, r, c = w.shape
    rh = r // 2
    tr = _row_tile(rh, c * 4)
    c1 = 1.0 - ADAM_B1 ** ADAM_STEP
    c2 = 1.0 - ADAM_B2 ** ADAM_STEP

    def body(c_ref, w_ref, m_ref, v_ref, a_ref, b_ref, g_ref, d_ref, nm_ref, nv_ref):
        gv = jnp.where(pl.program_id(1) == c_ref[0], a_ref[...], b_ref[...])
        nm = ADAM_B1 * m_ref[...] + (1.0 - ADAM_B1) * gv
        nv = ADAM_B2 * v_ref[...] + (1.0 - ADAM_B2) * (gv * gv)
        g_ref[...] = gv
        d_ref[...] = -ADAM_LR * ((nm / c1) / (jnp.sqrt(nv / c2) + ADAM_EPS) + ADAM_WD * w_ref[...])
        nm_ref[...] = nm
        nv_ref[...] = nv

    full = pl.BlockSpec((None, None, tr, c), lambda l, h, i, c_ref: (l, h, i, 0))
    half_mine = pl.BlockSpec((None, tr, c), lambda l, h, i, c_ref: (l, jnp.where(h == c_ref[0], i, 0), 0))
    half_other = pl.BlockSpec((None, tr, c), lambda l, h, i, c_ref: (l, jnp.where(h == c_ref[0], 0, i), 0))
    grid_spec = pltpu.PrefetchScalarGridSpec(num_scalar_prefetch=1, grid=(nl, 2, rh // tr),
                                             in_specs=[full] * 3 + [half_mine, half_other], out_specs=[full] * 4)
    shp = jax.ShapeDtypeStruct((nl, 2, rh, c), F32)
    view = (nl, 2, rh, c)
    outs = pl.pallas_call(body, grid_spec=grid_spec, out_shape=[shp] * 4, compiler_params=_params("arbitrary", "arbitrary", "arbitrary"),
                          name=name)(c_idx, w.reshape(view), m.reshape(view), v.reshape(view), mine, other)
    return [o.reshape(nl, r, c) for o in outs]


WEIGHTS = ["mem_ln_g", "mem_ln_b", "w_in", "sg_ln_g", "sg_ln_b", "sg_w", "sg_b", "conv_w", "conv_b", "dt_bias", "a_log",
           "d_skip", "ssm_norm_g", "p_a", "p_b", "w_mix_o", "w_xq", "w_xkv", "w_xo", "w_ffn_in", "w_ffn_out", "ln_g", "ln_b"]
ARG_NAMES = ["x", "mem"] + WEIGHTS + ["loss_target"] + ["m_" + n for n in WEIGHTS] + ["v_" + n for n in WEIGHTS]
BIG = {"w_in": (1, (1024, 9248)), "p_a": (0, (1024, 1024)), "p_b": (0, (2048, 1024)), "w_mix_o": (0, (1024, 1024)),
       "w_xq": (0, (1024, 1024)), "w_xkv": (1, (1024, 2048)), "w_xo": (0, (1024, 1024)), "w_ffn_in": (1, (1024, 5632)),
       "w_ffn_out": (0, (2816, 1024))}
SMALL_SHARDED = {"conv_w": (4, 3072), "ln_g": (3, 1024), "ln_b": (3, 1024)}
SMALL = [n for n in WEIGHTS if n not in BIG]
W_IN_MAP = ((0, 4096, "main", 0), (4096, 7168, "main", XBC_COL0), (7168, 7200, "dt", 0), (7200, 9248, "main", GAB_COL0))
W_IN_SHARD = 9248 // N_CHIPS


def _w_in_chip_major(gm, gd):
    src = {"main": gm, "dt": gd}
    blocks = []
    for j in range(N_CHIPS):
        lo, hi = j * W_IN_SHARD, (j + 1) * W_IN_SHARD
        parts = [src[k][:, o + max(lo, a) - a:o + min(hi, b) - a] for a, b, k, o in W_IN_MAP if max(lo, a) < min(hi, b)]
        blocks.append(jnp.concatenate(parts, axis=1))
    return jnp.stack(blocks)


def _w_in_reassemble(wc):
    def cols(a, b):
        out = []
        for j in range(N_CHIPS):
            lo, hi = max(a, j * W_IN_SHARD), min(b, (j + 1) * W_IN_SHARD)
            if lo < hi:
                out.append(wc[j][:, lo - j * W_IN_SHARD:hi - j * W_IN_SHARD])
        return out

    main = sorted((m for m in W_IN_MAP if m[2] == "main"), key=lambda m: m[3])
    w_main = jnp.concatenate([p for a, b, _, _ in main for p in cols(a, b)], axis=1)
    (a, b, _, _), = [m for m in W_IN_MAP if m[2] == "dt"]
    w_dt = jnp.pad(jnp.concatenate(cols(a, b), axis=1), ((0, 0), (0, HEAD_PAD - (b - a))))
    return w_main, w_dt
GATHER_KIND = {"w_in": "chip", "p_a": "row", "p_b": "row", "w_mix_o": "row", "w_xq": "row", "w_xkv": "col", "w_xo": "row",
               "w_ffn_in": "col", "w_ffn_out": "row", "conv_w": "chip", "ln_g": "chip", "ln_b": "chip"}
GRAD_VIEW = {n: ("col" if k == "col" else "chip") for n, k in GATHER_KIND.items() if n in BIG}


def _shard_shape(name):
    axis, (r, c) = BIG[name]
    return (r // N_CHIPS, c) if axis == 0 else (r, c // N_CHIPS)


def _pad_rows(flat, cols, row_mult):
    n = flat.shape[0]
    rows = -(-n // cols)
    rows = -(-rows // row_mult) * row_mult
    return jnp.pad(flat, (0, rows * cols - n)).reshape(rows, cols)


def _gather_small_params(a, chip):
    names = list(SMALL_SHARDED)
    kinds = [GATHER_KIND[n] for n in names]
    bufs = [_cast_place(a[n], GATHER_KIND[n], F32, chip.reshape(1), name=f"place_{n}") for n in names]
    outs = _gather_params(bufs, [a[n].shape[1:] for n in names], kinds, name="gather_small_params")
    full = {}
    for n, o in zip(names, outs):
        _, _, r, c = o.shape
        full[n] = jnp.transpose(o, (0, 2, 1, 3)).reshape(DEPTH, r, N_CHIPS * c)
    return full


GATHER_GROUPS = (("w_in",), tuple(n for n in BIG if n != "w_in"))


def _gather_group_start(a, l, names, chip, after, *, tag):
    bufs = [_cast_place_layer(a[n], l, GATHER_KIND[n], chip.reshape(1), after, name=f"place_{n}_l{l}") for n in names]
    return _gather_start(bufs, [a[n].shape[1:] for n in names], [GATHER_KIND[n] for n in names], name=f"gather_start_{tag}")


def _gather_group_finish(a, names, flight, after, *, tag):
    send_sems, recv_sems, bufs, token = flight
    shapes, kinds = [a[n].shape[1:] for n in names], [GATHER_KIND[n] for n in names]
    bufs = _gather_wait(send_sems, recv_sems, bufs, shapes, kinds, token if after is None else after, name=f"gather_wait_{tag}")
    full = dict(zip(names, _gather_forward(bufs, shapes, kinds, name=f"gather_forward_{tag}")))
    if "w_in" in full:
        full["w_main"], full["w_dt"] = _w_in_reassemble(full.pop("w_in"))
    return full


def _layer_weights(a, big, small, l):
    w = dict(big)
    for n in SMALL_SHARDED:
        w[n] = small[n][l]
    for n in ["sg_ln_g", "sg_ln_b", "sg_w", "conv_b", "ssm_norm_g"]:
        w[n] = a[n][l]
    w["sg_bcol"] = a["sg_b"][l][..., None]
    for n in ["dt_bias", "a_log"]:
        w[n + "8"] = _pad_heads(a[n][l])
    w["d_skipx"] = _expand_heads(a["d_skip"][l])
    return w


def _grad_views(grads, names):
    gs = []
    for n in names:
        axis, _ = BIG[n]
        r, c = _shard_shape(n)
        if n == "w_in":
            gs.append(_w_in_chip_major(grads["w_main"], grads["w_dt"]))
        elif axis == 0:
            gs.append(grads[n].reshape(N_CHIPS, r, c))
        else:
            gs.append(grads[n])
    return gs


class _GradExchange:
    def __init__(self, grads, names, c_idx, tag):
        self.names, self.c_idx, self.tag = names, c_idx, tag
        self.views = [GRAD_VIEW[n] for n in names]
        self.gs = _grad_views(grads, names)

    def start(self):
        self.sems = _grads_to_sibling_start(self.gs, self.views, name=f"grads_to_sibling_start_{self.tag}")
        return self.sems[4]

    def cross(self, after):
        send_sems, recv_sems, gs, lands, token = self.sems
        self.gs, self.recv = _grads_to_sibling_wait(send_sems, recv_sems, gs, lands, self.views, token if after is None else after,
                                                    name=f"grads_to_sibling_wait_{self.tag}")
        cpre = self.c_idx.reshape(1)
        pairs = [_pair_sum(g, rv, v, cpre, name=f"grads_pair_sum_{n}_{self.tag}")
                 for g, rv, v, n in zip(self.gs, self.recv, self.views, self.names)]
        self.sems = _grads_to_chips_start(pairs, self.views, name=f"grads_to_chips_start_{self.tag}")
        return self.sems[4]

    def finish(self, after):
        send_sems, recv_sems, pairs, lands, _ = self.sems
        quads = _grads_to_chips_wait(send_sems, recv_sems, pairs, lands, self.views, after, name=f"grads_to_chips_wait_{self.tag}")
        return {n: (g, rv, q) for n, g, rv, q in zip(self.names, self.gs, self.recv, quads)}


def _finish_big_grads(parts, c_idx, chip):
    tots = [_quad_sum([parts[l][n][0] for l in range(DEPTH)], [parts[l][n][1] for l in range(DEPTH)],
                      [parts[l][n][2] for l in range(DEPTH)], GRAD_VIEW[n], chip.reshape(1), c_idx.reshape(1),
                      name=f"grads_chip_sum_{n}") for n in BIG]
    others = _grads_share(tots, name="grads_share")
    return {n: (t, o) for n, t, o in zip(BIG, tots, others)}


def _direct_copies(x_ref, land_ref, send_sems, recv_sems):
    mx, my, mc = lax.axis_index("x"), lax.axis_index("y"), lax.axis_index("c")
    me = 4 * mx + 2 * my + mc
    copies = []
    for k in range(N_DEV - 1):
        f = k + 1
        to = (mx ^ (f >> 2 & 1), my ^ (f >> 1 & 1), mc ^ (f & 1))
        copies.append(pltpu.make_async_remote_copy(src_ref=x_ref, dst_ref=land_ref.at[me], send_sem=send_sems.at[k],
                                                   recv_sem=recv_sems.at[k], device_id=to, device_id_type=_MESH))
    return copies


def _all_gather8_start(x, *, name):
    land = pltpu.with_memory_space_constraint(lax.empty((N_DEV,) + x.shape, x.dtype), pltpu.HBM)

    def body(x_ref, land_ref, send_sems, recv_sems, x_out, land_out, token):
        for cp in _direct_copies(x_ref, land_ref, send_sems, recv_sems):
            cp.start()
        token[...] = jnp.zeros_like(token)

    n = N_DEV - 1
    return pl.pallas_call(
        body, name=name,
        out_shape=(pltpu.SemaphoreType.DMA((n,)), pltpu.SemaphoreType.DMA((n,)), pltpu.HBM(x.shape, x.dtype),
                   pltpu.HBM(land.shape, land.dtype), jax.ShapeDtypeStruct((SUBLANE, LANE), F32)),
        in_specs=[_HBM, _HBM], out_specs=(_SEM, _SEM, _HBM, _HBM, pl.BlockSpec(memory_space=pltpu.VMEM)),
        input_output_aliases={0: 2, 1: 3}, compiler_params=pltpu.CompilerParams(has_side_effects=_EFFECT),
    )(pltpu.with_memory_space_constraint(x, pltpu.HBM), land)


def _all_gather8_wait(send_sems, recv_sems, x, land, after, *, name):
    def body(x_ref, land_ref, s_sems, r_sems, _, x_out, land_out):
        for cp in _direct_copies(x_ref, land_ref, s_sems, r_sems):
            cp.wait_send()
            cp.wait_recv()

    return pl.pallas_call(
        body, name=name, out_shape=(pltpu.HBM(x.shape, x.dtype), pltpu.HBM(land.shape, land.dtype)),
        in_specs=[_HBM, _HBM, _SEM, _SEM, _ANY], out_specs=(_HBM, _HBM), input_output_aliases={0: 0, 1: 1},
        compiler_params=pltpu.CompilerParams(has_side_effects=_EFFECT),
    )(x, land, send_sems, recv_sems, after)


def _pack_small(small):
    return _pad_rows(jnp.concatenate([small[n].reshape(-1) for n in small]), LANE, SUBLANE)


def _unpack_small(small, g8, packed, chip, c_idx, *, name):
    names = list(small)
    tot = _sum_devices(g8, packed, (2 * chip + c_idx).reshape(1), name=name).reshape(-1)
    out, off = {}, 0
    for n in names:
        sz = small[n].size
        full = tot[off:off + sz].reshape(small[n].shape)
        off += sz
        if n in SMALL_SHARDED:
            cs = SMALL_SHARDED[n][1] // N_CHIPS
            full = lax.dynamic_slice_in_dim(full, chip * cs, cs, axis=-1)
        out[n] = full
    return out


def kernel(x, mem, mem_ln_g, mem_ln_b, w_in, sg_ln_g, sg_ln_b, sg_w, sg_b, conv_w, conv_b, dt_bias, a_log, d_skip, ssm_norm_g, p_a, p_b, w_mix_o, w_xq, w_xkv, w_xo, w_ffn_in, w_ffn_out, ln_g, ln_b, loss_target, m_mem_ln_g, m_mem_ln_b, m_w_in, m_sg_ln_g, m_sg_ln_b, m_sg_w, m_sg_b, m_conv_w, m_conv_b, m_dt_bias, m_a_log, m_d_skip, m_ssm_norm_g, m_p_a, m_p_b, m_w_mix_o, m_w_xq, m_w_xkv, m_w_xo, m_w_ffn_in, m_w_ffn_out, m_ln_g, m_ln_b, v_mem_ln_g, v_mem_ln_b, v_w_in, v_sg_ln_g, v_sg_ln_b, v_sg_w, v_sg_b, v_conv_w, v_conv_b, v_dt_bias, v_a_log, v_d_skip, v_ssm_norm_g, v_p_a, v_p_b, v_w_mix_o, v_w_xq, v_w_xkv, v_w_xo, v_w_ffn_in, v_w_ffn_out, v_ln_g, v_ln_b):
    a = dict(zip(ARG_NAMES, (x, mem, mem_ln_g, mem_ln_b, w_in, sg_ln_g, sg_ln_b, sg_w, sg_b, conv_w, conv_b, dt_bias, a_log, d_skip, ssm_norm_g, p_a, p_b, w_mix_o, w_xq, w_xkv, w_xo, w_ffn_in, w_ffn_out, ln_g, ln_b, loss_target, m_mem_ln_g, m_mem_ln_b, m_w_in, m_sg_ln_g, m_sg_ln_b, m_sg_w, m_sg_b, m_conv_w, m_conv_b, m_dt_bias, m_a_log, m_d_skip, m_ssm_norm_g, m_p_a, m_p_b, m_w_mix_o, m_w_xq, m_w_xkv, m_w_xo, m_w_ffn_in, m_w_ffn_out, m_ln_g, m_ln_b, v_mem_ln_g, v_mem_ln_b, v_w_in, v_sg_ln_g, v_sg_ln_b, v_sg_w, v_sg_b, v_conv_w, v_conv_b, v_dt_bias, v_a_log, v_d_skip, v_ssm_norm_g, v_p_a, v_p_b, v_w_mix_o, v_w_xq, v_w_xkv, v_w_xo, v_w_ffn_in, v_w_ffn_out, v_ln_g, v_ln_b)))
    c_idx = lax.axis_index("c").astype(jnp.int32)
    chip = (2 * lax.axis_index("x") + lax.axis_index("y")).astype(jnp.int32)

    small = _gather_small_params(a, chip)
    ga, gb = GATHER_GROUPS
    flights = {(0, 0): _gather_group_start(a, 0, ga, chip, small["ln_b"], tag="l0_a")}
    flights[0, 1] = _gather_group_start(a, 0, gb, chip, flights[0, 0][3], tag="l0_b")

    def layer_weights(after, l):
        first = _gather_group_finish(a, ga, flights[l, 0], after if l else flights[l, 1][3], tag=f"l{l}_a")

        def rest(w, after_b):
            more = _gather_group_finish(a, gb, flights[l, 1], after_b, tag=f"l{l}_b")
            if l + 1 < DEPTH:
                flights[l + 1, 0] = _gather_group_start(a, l + 1, ga, chip, more["p_a"], tag=f"l{l + 1}_a")
                flights[l + 1, 1] = _gather_group_start(a, l + 1, gb, chip, flights[l + 1, 0][3], tag=f"l{l + 1}_b")
                more["p_a"] = more["p_a"] + flights[l + 1, 1][3][0, 0].astype(MXU_DTYPE)
            return {k: v for k, v in {**w, **more}.items() if k != "rest"}

        return dict(_layer_weights(a, first, small, l), rest=rest)

    layers = [functools.partial(layer_weights, l=l) for l in range(DEPTH)]
    exchanges, seen, small_flight = [], {}, {}

    def start_exchange(l, names, grads_l):
        ex = _GradExchange(grads_l, names, c_idx, f"l{l}_{names[0]}")
        tokens = [ex.start()]
        if exchanges:
            tokens.append(exchanges[-1][1].cross(tokens[0]))
        exchanges.append((l, ex))
        seen[l] = grads_l
        if l == 0 and names == GRAD_GROUPS[-1]:
            tokens.append(ex.cross(None))
            small = {}
            for n in SMALL:
                if n.startswith("mem_ln"):
                    continue
                per_layer = []
                for k in range(DEPTH):
                    g = seen[k][n]
                    if n in ("dt_bias", "a_log", "d_skip"):
                        g = g[0, :SSM_HEADS]
                    per_layer.append(g.reshape(a[n].shape[1:-1] + (-1,)))
                small[n] = jnp.stack(per_layer)
            small_flight["small"] = small
            small_flight["sems"] = _all_gather8_start(_pack_small(small), name="gather_small_grads_start")
            tokens.append(small_flight["sems"][4])
        return sum(tokens[1:], tokens[0])

    lsum, grad_x, grads, d_mem_g, d_mem_b = _local_step(x, mem, loss_target, mem_ln_g, mem_ln_b, layers, start_exchange)
    loss = lax.psum(0.5 * jnp.sum(lsum) / D_MODEL, ("x", "y", "c"))

    parts = [{} for _ in range(DEPTH)]
    for l, ex in exchanges:
        parts[l].update(ex.finish(grad_x))
    halves = _finish_big_grads(parts, c_idx, chip)
    gw = {}
    send_sems, recv_sems, packed, land, _ = small_flight["sems"]
    packed, g8 = _all_gather8_wait(send_sems, recv_sems, packed, land, grad_x, name="gather_small_grads_wait")
    gw.update(_unpack_small(small_flight["small"], g8, packed, chip, c_idx, name="small_grads_sum"))
    mem_small = {"mem_ln_g": d_mem_g, "mem_ln_b": d_mem_b}
    mem_packed = _pack_small(mem_small)
    gw.update(_unpack_small(mem_small, _all_gather8(mem_packed, name="gather_mem_ln_grads"), mem_packed, chip, c_idx,
                            name="mem_ln_grads_sum"))

    delta, new_m, new_v = {}, {}, {}
    for n in BIG:
        mine, other = halves[n]
        gw[n], delta[n], new_m[n], new_v[n] = _adamw_halves(a[n], a["m_" + n], a["v_" + n], mine, other, c_idx.reshape(1),
                                                             name=f"adamw_{n}")
    for n in SMALL:
        shp = a[n].shape
        view = (-1, LANE) if a[n].size % LANE == 0 else (1, -1)
        outs = _adamw(*[v.reshape(view) for v in (a[n], gw[n], a["m_" + n], a["v_" + n])], name=f"adamw_{n}")
        delta[n], new_m[n], new_v[n] = (o.reshape(shp) for o in outs)
    return (loss, grad_x, *[gw[n].reshape(a[n].shape) for n in WEIGHTS], *[delta[n] for n in WEIGHTS],
            *[new_m[n] for n in WEIGHTS], *[new_v[n] for n in WEIGHTS])
```

```python
import functools
import math

import jax
import jax.numpy as jnp
from jax import lax
from jax.experimental import pallas as pl
from jax.experimental.pallas import tpu as pltpu

F32 = jnp.float32
MXU_DTYPE = jnp.bfloat16
WIRE_DTYPE = jnp.bfloat16
STASH_DTYPE = jnp.bfloat16

D_MODEL = 1024
DEPTH = 2
CHUNK = 128
SG_GROUPS = 8
SSM_INNER = 2048
SSM_HEADDIM = 64
SSM_HEADS = 32
SSM_STATE = 128
SSM_GROUPS = 4
SSM_CONV = 4
SSM_CONV_DIM = 3072
X_HEADS = 4
X_HEADDIM = 256
FFN_HIDDEN = 2816
ALPHA = float((2 * DEPTH) ** 0.25)
LN_EPS = 1e-5
RMS_EPS = 1e-5
ADAM_LR = 0.001
ADAM_B1 = 0.9
ADAM_B2 = 0.999
ADAM_EPS = 1e-08
ADAM_WD = 0.01
ADAM_STEP = 10

MAIN_COLS = 9216
UVZ_COLS = 4096
GAB_COL0 = 4096
XBC_COL0 = 6144
HEAD_PAD = 128

VMEM_LIMIT = 56 * 1024 * 1024
BLOCK_BYTES = 2 * 1024 * 1024
ROW_TILES = (1024, 512, 256, 128)
WIDE_ROW_TILES = (512, 256, 128)
LANE = 128
SUBLANE = 8

N_CHIPS = 4
N_DEV = 8


def _pick(n, cands):
    for c in cands:
        if n % c == 0:
            return c
    return n


MM_TILE_MAX = 1536
MM_OPERAND_BYTES = 12 * 1024 * 1024


def _div_tile(n, limit):
    best = None
    for t in range(LANE, min(n, limit) + 1, LANE):
        if n % t == 0:
            best = t
    return n if best is None else best


def _params(*sem):
    return pltpu.CompilerParams(dimension_semantics=tuple(sem), vmem_limit_bytes=VMEM_LIMIT)


_ANY = pl.BlockSpec(memory_space=pl.ANY)
_MESH = pl.DeviceIdType.MESH


def _nt(a, b):
    return lax.dot_general(a, b, (((1,), (1,)), ((), ())), preferred_element_type=F32)


def _tn(a, b):
    return lax.dot_general(a, b, (((0,), (0,)), ((), ())), preferred_element_type=F32)


def _nn(a, b):
    return jnp.dot(a, b, preferred_element_type=F32)


def _sigmoid(x):
    return 0.5 * jnp.tanh(0.5 * x) + 0.5


def _split3(v):
    def top(x):
        bits = lax.bitcast_convert_type(x, jnp.uint32) & jnp.uint32(0xFFFF0000)
        return lax.bitcast_convert_type(bits, F32)

    v1 = top(v)
    r1 = v - v1
    v2 = top(r1)
    v3 = r1 - v2
    return v1.astype(jnp.bfloat16), v2.astype(jnp.bfloat16), v3.astype(jnp.bfloat16)


def _dot_exact(a, b, dn, data):
    if data == 0:
        mat = b.astype(jnp.bfloat16)
        return sum(lax.dot_general(p, mat, dn, preferred_element_type=F32) for p in _split3(a))
    mat = a.astype(jnp.bfloat16)
    return sum(lax.dot_general(mat, p, dn, preferred_element_type=F32) for p in _split3(b))


_DN_NN = (((1,), (0,)), ((), ()))
_DN_TN = (((0,), (0,)), ((), ()))


def _gelu(x):
    return 0.5 * x * (1.0 + lax.erf(x * (2.0 ** -0.5)))


def _gelu_grad(x):
    return 0.5 * (1.0 + lax.erf(x * (2.0 ** -0.5))) + x * jnp.exp(-0.5 * x * x) * (1.0 / math.sqrt(2.0 * math.pi))


def _mm(a, b, *, ta=False, tb=False, out_dtype=F32, after=None, name):
    if ta:
        kdim, m = a.shape
    else:
        m, kdim = a.shape
    if tb:
        n, k2 = b.shape[-2:]
    else:
        k2, n = b.shape[-2:]
    assert kdim == k2, (a.shape, b.shape, ta, tb)
    tm = _div_tile(m, MM_TILE_MAX)
    tn = _div_tile(n, MM_TILE_MAX)
    tk = _div_tile(kdim, MM_OPERAND_BYTES // (tm * a.dtype.itemsize + tn * b.dtype.itemsize))
    nk = kdim // tk
    dn = (((0 if ta else 1,), (1 if tb else 0,)), ((), ()))

    extra = [] if after is None else [after]

    def body(a_ref, b_ref, *rest):
        o_ref = rest[len(extra)]
        d = lax.dot_general(a_ref[...].astype(MXU_DTYPE), b_ref[...].astype(MXU_DTYPE), dn, preferred_element_type=F32)
        if nk == 1:
            o_ref[...] = d.astype(out_dtype)
            return
        acc_ref = rest[len(extra) + 1]
        k = pl.program_id(2)

        @pl.when(k == 0)
        def _():
            acc_ref[...] = d

        @pl.when(jnp.logical_and(k > 0, k < nk - 1))
        def _():
            acc_ref[...] += d

        @pl.when(k == nk - 1)
        def _():
            o_ref[...] = (acc_ref[...] + d).astype(out_dtype)

    a_spec = pl.BlockSpec((tk, tm), lambda i, j, k: (k, i)) if ta else pl.BlockSpec((tm, tk), lambda i, j, k: (i, k))
    b_spec = pl.BlockSpec((tn, tk), lambda i, j, k: (j, k)) if tb else pl.BlockSpec((tk, tn), lambda i, j, k: (k, j))
    return pl.pallas_call(
        body, grid=(m // tm, n // tn, nk), in_specs=[a_spec, b_spec] + [_ANY] * len(extra),
        out_specs=pl.BlockSpec((tm, tn), lambda i, j, k: (i, j)),
        out_shape=jax.ShapeDtypeStruct((m, n), out_dtype),
        scratch_shapes=[pltpu.VMEM((tm, tn), F32)] if nk > 1 else [],
        compiler_params=_params("parallel", "parallel", "arbitrary"), name=name)(a, b, *extra)


def _row_spec(tm, c, col=0):
    return pl.BlockSpec((tm, c), lambda i: (i, col))


def _par_spec(shape):
    nd = len(shape)
    return pl.BlockSpec(shape, lambda i: (0,) * nd)


def _ln_fwd(x, f, g, b, *, name):
    t, c = x.shape
    tm = _pick(t, ROW_TILES)
    has_f = f is not None

    def body(*refs):
        if has_f:
            x_ref, f_ref, g_ref, b_ref, y_ref, yb_ref, xh_ref, rs_ref = refs
            r = ALPHA * x_ref[...] + f_ref[...]
        else:
            x_ref, g_ref, b_ref, y_ref, yb_ref, xh_ref, rs_ref = refs
            r = x_ref[...]
        mu = jnp.mean(r, axis=-1, keepdims=True)
        xc = r - mu
        var = jnp.mean(xc * xc, axis=-1, keepdims=True)
        rstd = lax.rsqrt(var + LN_EPS)
        xh = xc * rstd
        y = xh * g_ref[...] + b_ref[...]
        y_ref[...] = y
        yb_ref[...] = y.astype(MXU_DTYPE)
        xh_ref[...] = xh
        rs_ref[...] = jnp.broadcast_to(rstd, rs_ref.shape)

    ins = [x] + ([f] if has_f else []) + [g.reshape(1, c), b.reshape(1, c)]
    in_specs = [_row_spec(tm, c)] * (2 if has_f else 1) + [_par_spec((1, c))] * 2
    return pl.pallas_call(
        body, grid=(t // tm,), in_specs=in_specs,
        out_specs=[_row_spec(tm, c), _row_spec(tm, c), _row_spec(tm, c), _row_spec(tm, LANE)],
        out_shape=[jax.ShapeDtypeStruct((t, c), F32), jax.ShapeDtypeStruct((t, c), MXU_DTYPE),
                   jax.ShapeDtypeStruct((t, c), F32), jax.ShapeDtypeStruct((t, LANE), F32)],
        compiler_params=_params("parallel"), name=name)(*ins)


def _ln_bwd(addends, scales, xh, rs, g, *, name):
    t, c = xh.shape
    tm = _pick(t, ROW_TILES)
    na = len(addends)

    def body(*refs):
        a_refs = refs[:na]
        xh_ref, rs_ref, g_ref, dp_ref, dpb_ref, dg_ref, db_ref = refs[na:]

        @pl.when(pl.program_id(0) == 0)
        def _():
            dg_ref[...] = jnp.zeros_like(dg_ref)
            db_ref[...] = jnp.zeros_like(db_ref)

        dy = None
        for s, r in zip(scales, a_refs):
            term = r[...] if s == 1.0 else s * r[...]
            dy = term if dy is None else dy + term
        xhv = xh_ref[...]
        dxh = dy * g_ref[...]
        m1 = jnp.mean(dxh, axis=-1, keepdims=True)
        m2 = jnp.mean(dxh * xhv, axis=-1, keepdims=True)
        dp = rs_ref[:, 0:1] * (dxh - m1 - xhv * m2)
        dp_ref[...] = dp
        dpb_ref[...] = dp.astype(MXU_DTYPE)
        dg_ref[...] += jnp.sum(dy * xhv, axis=0, keepdims=True)
        db_ref[...] += jnp.sum(dy, axis=0, keepdims=True)

    in_specs = [_row_spec(tm, c)] * (na + 1) + [_row_spec(tm, LANE), _par_spec((1, c))]
    return pl.pallas_call(
        body, grid=(t // tm,), in_specs=in_specs,
        out_specs=[_row_spec(tm, c), _row_spec(tm, c), _par_spec((1, c)), _par_spec((1, c))],
        out_shape=[jax.ShapeDtypeStruct((t, c), F32), jax.ShapeDtypeStruct((t, c), MXU_DTYPE),
                   jax.ShapeDtypeStruct((1, c), F32), jax.ShapeDtypeStruct((1, c), F32)],
        compiler_params=_params("arbitrary"), name=name)(*addends, xh, rs, g.reshape(1, c))


def _add_scaled(addends, scales, *, name):
    t, c = addends[0].shape
    tm = _pick(t, ROW_TILES)
    na = len(addends)

    def body(*refs):
        acc = None
        for s, r in zip(scales, refs[:na]):
            term = r[...] if s == 1.0 else s * r[...]
            acc = term if acc is None else acc + term
        refs[na][...] = acc

    return pl.pallas_call(
        body, grid=(t // tm,), in_specs=[_row_spec(tm, c)] * na, out_specs=_row_spec(tm, c),
        out_shape=jax.ShapeDtypeStruct((t, c), F32), compiler_params=_params("parallel"), name=name)(*addends)


def _loss_head(y, tgt, *, name):
    t, c = y.shape
    tm = _pick(t, ROW_TILES)

    def body(y_ref, t_ref, dy_ref, ls_ref):
        @pl.when(pl.program_id(0) == 0)
        def _():
            ls_ref[...] = jnp.zeros_like(ls_ref)

        e = y_ref[...] - t_ref[...]
        dy_ref[...] = e * (1.0 / c)
        ls_ref[...] += jnp.sum(e * e, axis=0, keepdims=True)

    return pl.pallas_call(
        body, grid=(t // tm,), in_specs=[_row_spec(tm, c)] * 2,
        out_specs=[_row_spec(tm, c), _par_spec((1, c))],
        out_shape=[jax.ShapeDtypeStruct((t, c), F32), jax.ShapeDtypeStruct((1, c), F32)],
        compiler_params=_params("arbitrary"), name=name)(y, tgt)


def _swiglu_fwd(h, *, name):
    t, two_f = h.shape
    fh = two_f // 2
    tm = _pick(t, WIDE_ROW_TILES)

    def body(g_ref, u_ref, a_ref):
        g = g_ref[...].astype(F32)
        a_ref[...] = (g * _sigmoid(g) * u_ref[...].astype(F32)).astype(MXU_DTYPE)

    return pl.pallas_call(
        body, grid=(t // tm,), in_specs=[_row_spec(tm, fh, 0), _row_spec(tm, fh, 1)], out_specs=_row_spec(tm, fh),
        out_shape=jax.ShapeDtypeStruct((t, fh), MXU_DTYPE), compiler_params=_params("parallel"), name=name)(h, h)


def _swiglu_bwd(h, da, *, name):
    t, two_f = h.shape
    fh = two_f // 2
    tm = _pick(t, WIDE_ROW_TILES)

    def body(g_ref, u_ref, da_ref, dh_ref):
        g = g_ref[...].astype(F32)
        s = _sigmoid(g)
        dav = da_ref[...].astype(F32)
        dh_ref[:, :fh] = (dav * u_ref[...].astype(F32) * (s * (1.0 + g * (1.0 - s)))).astype(MXU_DTYPE)
        dh_ref[:, fh:] = (dav * g * s).astype(MXU_DTYPE)

    return pl.pallas_call(
        body, grid=(t // tm,), in_specs=[_row_spec(tm, fh, 0), _row_spec(tm, fh, 1), _row_spec(tm, fh)],
        out_specs=_row_spec(tm, two_f), out_shape=jax.ShapeDtypeStruct((t, two_f), MXU_DTYPE),
        compiler_params=_params("parallel"), name=name)(h, h, da)


def _attn_probs(q, k):
    s = _nt(q, k) * (X_HEADDIM ** -0.5)
    s = s - jnp.max(s, axis=-1, keepdims=True)
    p = jnp.exp(s)
    return p / jnp.sum(p, axis=-1, keepdims=True)


def _attn_fwd(q, kv, *, bsz, name):
    t = q.shape[0]
    s = t // bsz
    ml = kv.shape[0] // bsz
    hd = X_HEADDIM

    def body(q_ref, k_ref, v_ref, o_ref):
        p = _attn_probs(q_ref[...], k_ref[...])
        o_ref[...] = _nn(p.astype(MXU_DTYPE), v_ref[...]).astype(MXU_DTYPE)

    return pl.pallas_call(
        body, grid=(bsz, X_HEADS),
        in_specs=[pl.BlockSpec((s, hd), lambda b, h: (b, h)), pl.BlockSpec((ml, hd), lambda b, h: (b, h)),
                  pl.BlockSpec((ml, hd), lambda b, h: (b, X_HEADS + h))],
        out_specs=pl.BlockSpec((s, hd), lambda b, h: (b, h)),
        out_shape=jax.ShapeDtypeStruct((t, D_MODEL), MXU_DTYPE),
        compiler_params=_params("parallel", "parallel"), name=name)(q, kv, kv)


def _attn_bwd(q, kv, do, *, bsz, name):
    t = q.shape[0]
    s = t // bsz
    ml = kv.shape[0] // bsz
    hd = X_HEADDIM

    def body(q_ref, k_ref, v_ref, do_ref, dq_ref, dk_ref, dv_ref):
        qv, kk, vv, dov = q_ref[...], k_ref[...], v_ref[...], do_ref[...]
        p = _attn_probs(qv, kk)
        dp = _nt(dov, vv)
        dv_ref[...] = _tn(p.astype(MXU_DTYPE), dov).astype(MXU_DTYPE)
        ds = (p * (dp - jnp.sum(dp * p, axis=-1, keepdims=True)) * (X_HEADDIM ** -0.5)).astype(MXU_DTYPE)
        dq_ref[...] = _nn(ds, kk).astype(MXU_DTYPE)
        dk_ref[...] = _tn(ds, qv).astype(MXU_DTYPE)

    blk_q = pl.BlockSpec((s, hd), lambda b, h: (b, h))
    blk_m = pl.BlockSpec((ml, hd), lambda b, h: (b, h))
    return pl.pallas_call(
        body, grid=(bsz, X_HEADS),
        in_specs=[blk_q, blk_m, pl.BlockSpec((ml, hd), lambda b, h: (b, X_HEADS + h)), blk_q],
        out_specs=[blk_q, blk_m, blk_m],
        out_shape=[jax.ShapeDtypeStruct((t, D_MODEL), MXU_DTYPE), jax.ShapeDtypeStruct((bsz * ml, D_MODEL), MXU_DTYPE),
                   jax.ShapeDtypeStruct((bsz * ml, D_MODEL), MXU_DTYPE)],
        compiler_params=_params("parallel", "parallel"), name=name)(q, kv, kv, do)


def _causal(n):
    row = lax.broadcasted_iota(jnp.int32, (n, n), 0)
    col = lax.broadcasted_iota(jnp.int32, (n, n), 1)
    return row >= col


def _sg_norm(v, g, b):
    gv = _gelu(v)
    mu = jnp.mean(gv, axis=-1, keepdims=True)
    xc = gv - mu
    var = jnp.mean(xc * xc, axis=-1, keepdims=True)
    rstd = lax.rsqrt(var + LN_EPS)
    xh = xc * rstd
    return xh, rstd, xh * g + b


def _sg_fwd(proj, ln_g, ln_b, w, bcol, *, name):
    t = proj.shape[0]
    c = D_MODEL
    gd = c // SG_GROUPS

    def body(u_ref, v_ref, g_ref, b_ref, w_ref, bc_ref, o_ref):
        gu = _gelu(u_ref[...].astype(F32))
        _, _, vn = _sg_norm(v_ref[...].astype(F32), g_ref[...], b_ref[...])
        mask = _causal(CHUNK)
        for g in range(SG_GROUPS):
            sl = slice(g * gd, (g + 1) * gd)
            wg = jnp.where(mask, w_ref[g], 0.0).astype(MXU_DTYPE)
            mixed = _nn(wg, vn[:, sl].astype(MXU_DTYPE)) + bc_ref[g]
            o_ref[:, sl] = (gu[:, sl] * mixed).astype(MXU_DTYPE)

    return pl.pallas_call(
        body, grid=(t // CHUNK,),
        in_specs=[_row_spec(CHUNK, c, 0), _row_spec(CHUNK, c, 1), _par_spec((1, c)), _par_spec((1, c)),
                  _par_spec((SG_GROUPS, CHUNK, CHUNK)), _par_spec((SG_GROUPS, CHUNK, 1))],
        out_specs=_row_spec(CHUNK, c), out_shape=jax.ShapeDtypeStruct((t, c), MXU_DTYPE),
        compiler_params=_params("parallel"), name=name)(proj, proj, ln_g.reshape(1, c), ln_b.reshape(1, c), w, bcol)


def _sg_bwd(proj, dsgo, ln_g, ln_b, w, bcol, dproj, *, name):
    t = proj.shape[0]
    c = D_MODEL
    gd = c // SG_GROUPS

    def body(u_ref, v_ref, d_ref, g_ref, b_ref, w_ref, bc_ref, _, duv_ref, dw_ref, dbc_ref, dg_ref, db_ref, dvn_ref):
        @pl.when(pl.program_id(0) == 0)
        def _():
            dw_ref[...] = jnp.zeros_like(dw_ref)
            dbc_ref[...] = jnp.zeros_like(dbc_ref)
            dg_ref[...] = jnp.zeros_like(dg_ref)
            db_ref[...] = jnp.zeros_like(db_ref)

        u = u_ref[...].astype(F32)
        v = v_ref[...].astype(F32)
        dso = d_ref[...].astype(F32)
        gu = _gelu(u)
        xh, rstd, vn = _sg_norm(v, g_ref[...], b_ref[...])
        mask = _causal(CHUNK)
        for g in range(SG_GROUPS):
            sl = slice(g * gd, (g + 1) * gd)
            wg = jnp.where(mask, w_ref[g], 0.0).astype(MXU_DTYPE)
            vng = vn[:, sl].astype(MXU_DTYPE)
            mixed = _nn(wg, vng) + bc_ref[g]
            duv_ref[:, sl] = (dso[:, sl] * mixed * _gelu_grad(u[:, sl])).astype(MXU_DTYPE)
            dmix = dso[:, sl] * gu[:, sl]
            dmb = dmix.astype(MXU_DTYPE)
            dbc_ref[g] += jnp.sum(dmix, axis=-1, keepdims=True)
            dw_ref[g] += jnp.where(mask, _nt(dmb, vng), 0.0)
            dvn_ref[:, sl] = _tn(wg, dmb)
        dvn = dvn_ref[...]
        dg_ref[...] += jnp.sum(dvn * xh, axis=0, keepdims=True)
        db_ref[...] += jnp.sum(dvn, axis=0, keepdims=True)
        dxh = dvn * g_ref[...]
        m1 = jnp.mean(dxh, axis=-1, keepdims=True)
        m2 = jnp.mean(dxh * xh, axis=-1, keepdims=True)
        dgv = rstd * (dxh - m1 - xh * m2)
        duv_ref[:, c:] = (dgv * _gelu_grad(v)).astype(MXU_DTYPE)

    return pl.pallas_call(
        body, grid=(t // CHUNK,),
        in_specs=[_row_spec(CHUNK, c, 0), _row_spec(CHUNK, c, 1), _row_spec(CHUNK, c), _par_spec((1, c)),
                  _par_spec((1, c)), _par_spec((SG_GROUPS, CHUNK, CHUNK)), _par_spec((SG_GROUPS, CHUNK, 1)), _ANY],
        out_specs=[_row_spec(CHUNK, 2 * c), _par_spec((SG_GROUPS, CHUNK, CHUNK)), _par_spec((SG_GROUPS, CHUNK, 1)),
                   _par_spec((1, c)), _par_spec((1, c))],
        out_shape=[jax.ShapeDtypeStruct(dproj.shape, dproj.dtype), jax.ShapeDtypeStruct((SG_GROUPS, CHUNK, CHUNK), F32),
                   jax.ShapeDtypeStruct((SG_GROUPS, CHUNK, 1), F32), jax.ShapeDtypeStruct((1, c), F32),
                   jax.ShapeDtypeStruct((1, c), F32)],
        scratch_shapes=[pltpu.VMEM((CHUNK, c), F32)], input_output_aliases={7: 0},
        compiler_params=_params("arbitrary"), name=name)(proj, proj, dsgo, ln_g.reshape(1, c), ln_b.reshape(1, c), w, bcol, dproj)


CONV_TC = 512


def _conv_taps(x):
    rows = lax.broadcasted_iota(jnp.int32, x.shape, 0)
    taps = [jnp.where(rows >= SSM_CONV - 1 - k, pltpu.roll(x, SSM_CONV - 1 - k, axis=0), 0.0) for k in range(SSM_CONV - 1)]
    return taps + [x]


def _conv_pre(taps, w_ref, b_ref):
    acc = b_ref[...]
    for k in range(SSM_CONV):
        acc = acc + taps[k] * w_ref[k:k + 1, :]
    return acc


def _conv_fwd(proj, w, b, *, bsz, name):
    t = proj.shape[0]
    s = t // bsz
    nj = SSM_CONV_DIM // CONV_TC
    c0 = XBC_COL0 // CONV_TC

    def body(x_ref, w_ref, b_ref, o_ref):
        pre = _conv_pre(_conv_taps(x_ref[...].astype(F32)), w_ref, b_ref)
        o_ref[...] = (pre * _sigmoid(pre)).astype(o_ref.dtype)

    return pl.pallas_call(
        body, grid=(bsz, nj),
        in_specs=[pl.BlockSpec((s, CONV_TC), lambda bb, j: (bb, c0 + j)), pl.BlockSpec((SSM_CONV, CONV_TC), lambda bb, j: (0, j)),
                  pl.BlockSpec((1, CONV_TC), lambda bb, j: (0, j))],
        out_specs=pl.BlockSpec((s, CONV_TC), lambda bb, j: (bb, j)),
        out_shape=jax.ShapeDtypeStruct((t, SSM_CONV_DIM), STASH_DTYPE),
        compiler_params=_params("parallel", "parallel"), name=name)(proj, w, b.reshape(1, -1))


def _conv_bwd(proj, dact, w, b, dproj, *, bsz, name):
    t = proj.shape[0]
    s = t // bsz
    nj = SSM_CONV_DIM // CONV_TC
    c0 = XBC_COL0 // CONV_TC

    def body(x_ref, d_ref, w_ref, b_ref, _, dx_ref, dw_ref, db_ref):
        @pl.when(pl.program_id(1) == 0)
        def _():
            dw_ref[...] = jnp.zeros_like(dw_ref)
            db_ref[...] = jnp.zeros_like(db_ref)

        taps = _conv_taps(x_ref[...].astype(F32))
        pre = _conv_pre(taps, w_ref, b_ref)
        sg = _sigmoid(pre)
        dpre = d_ref[...].astype(F32) * (sg * (1.0 + pre * (1.0 - sg)))
        rows = lax.broadcasted_iota(jnp.int32, dpre.shape, 0)
        db_ref[...] += jnp.sum(dpre, axis=0, keepdims=True)
        dx = dpre * w_ref[SSM_CONV - 1:SSM_CONV, :]
        for k in range(SSM_CONV):
            dw_ref[k:k + 1, :] += jnp.sum(dpre * taps[k], axis=0, keepdims=True)
        for k in range(SSM_CONV - 1):
            sh = SSM_CONV - 1 - k
            dsh = jnp.where(rows < s - sh, pltpu.roll(dpre, s - sh, axis=0), 0.0)
            dx = dx + dsh * w_ref[k:k + 1, :]
        dx_ref[...] = dx.astype(MXU_DTYPE)

    return pl.pallas_call(
        body, grid=(nj, bsz),
        in_specs=[pl.BlockSpec((s, CONV_TC), lambda j, bb: (bb, c0 + j)), pl.BlockSpec((s, CONV_TC), lambda j, bb: (bb, j)),
                  pl.BlockSpec((SSM_CONV, CONV_TC), lambda j, bb: (0, j)), pl.BlockSpec((1, CONV_TC), lambda j, bb: (0, j)), _ANY],
        out_specs=[pl.BlockSpec((s, CONV_TC), lambda j, bb: (bb, c0 + j)), pl.BlockSpec((SSM_CONV, CONV_TC), lambda j, bb: (0, j)),
                   pl.BlockSpec((1, CONV_TC), lambda j, bb: (0, j))],
        out_shape=[jax.ShapeDtypeStruct(dproj.shape, dproj.dtype), jax.ShapeDtypeStruct((SSM_CONV, SSM_CONV_DIM), F32),
                   jax.ShapeDtypeStruct((1, SSM_CONV_DIM), F32)],
        input_output_aliases={4: 0},
        compiler_params=_params("parallel", "arbitrary"), name=name)(proj, dact, w, b.reshape(1, -1), dproj)


def _softplus(x):
    return jnp.maximum(x, 0.0) + jnp.log1p(jnp.exp(-jnp.abs(x)))


def _pad_heads(v):
    return jnp.broadcast_to(jnp.pad(v.astype(F32), (0, HEAD_PAD - SSM_HEADS))[None, :], (SUBLANE, HEAD_PAD))


def _ssd_prep(dt_raw, dt_bias8, a_log8, *, name):
    t = dt_raw.shape[0]
    n = CHUNK

    def body(r_ref, b_ref, al_ref, dt_ref, cs_ref, dtt_ref, cst_ref):
        dt = _softplus(r_ref[...] + b_ref[0:1, :])
        da = dt * (-jnp.exp(al_ref[0:1, :]))
        row = lax.broadcasted_iota(jnp.int32, (n, n), 0)
        col = lax.broadcasted_iota(jnp.int32, (n, n), 1)
        lower = (col <= row).astype(F32)
        upper = (row <= col).astype(F32)
        eye = (row == col).astype(F32)
        dt_ref[...] = dt
        cs_ref[...] = _dot_exact(lower, da, _DN_NN, 1)
        cst_ref[0] = _dot_exact(da, upper, _DN_TN, 0)
        dtt_ref[0] = _dot_exact(dt, eye, _DN_TN, 0)

    hp = HEAD_PAD
    return pl.pallas_call(
        body, grid=(t // n,),
        in_specs=[_row_spec(n, hp), _par_spec((SUBLANE, hp)), _par_spec((SUBLANE, hp))],
        out_specs=[_row_spec(n, hp), _row_spec(n, hp), pl.BlockSpec((1, hp, n), lambda i: (i, 0, 0)),
                   pl.BlockSpec((1, hp, n), lambda i: (i, 0, 0))],
        out_shape=[jax.ShapeDtypeStruct((t, hp), F32), jax.ShapeDtypeStruct((t, hp), F32),
                   jax.ShapeDtypeStruct((t // n, hp, n), F32), jax.ShapeDtypeStruct((t // n, hp, n), F32)],
        compiler_params=_params("parallel"), name=name)(dt_raw, dt_bias8, a_log8)


def _expand_mat():
    h = lax.broadcasted_iota(jnp.int32, (HEAD_PAD, SSM_INNER), 0)
    ch = lax.broadcasted_iota(jnp.int32, (HEAD_PAD, SSM_INNER), 1)
    return (ch // SSM_HEADDIM == h).astype(F32)


def _reduce_mat():
    ch = lax.broadcasted_iota(jnp.int32, (SSM_INNER, HEAD_PAD), 0)
    h = lax.broadcasted_iota(jnp.int32, (SSM_INNER, HEAD_PAD), 1)
    return (ch // SSM_HEADDIM == h).astype(F32)


def _expand(v, em):
    return _dot_exact(v, em, _DN_NN, 0)


def _expand_heads(v):
    return jnp.repeat(v.astype(F32), SSM_HEADDIM)[None, :]


def _decay_mat(cs_ref, cst_ref, h, mask):
    seg = cs_ref[:, h:h + 1] - cst_ref[0, h:h + 1, :]
    return jnp.where(mask, jnp.exp(jnp.minimum(seg, 0.0)), 0.0)


GROUP_CH = SSM_INNER // SSM_GROUPS
PAIRS_PER_GROUP = GROUP_CH // LANE
HEADS_PER_GROUP = SSM_HEADS // SSM_GROUPS
BM_COL0 = SSM_INNER
CM_COL0 = SSM_INNER + SSM_GROUPS * SSM_STATE


def _ssd_specs(nc, rev):
    def cidx(i):
        return (i // nc) * nc + (nc - 1 - i % nc) if rev else i

    n = CHUNK
    xs = pl.BlockSpec((n, SSM_INNER), lambda i: (cidx(i), 0))
    bm = pl.BlockSpec((n, GROUP_CH), lambda i: (cidx(i), BM_COL0 // GROUP_CH))
    cm = pl.BlockSpec((n, GROUP_CH), lambda i: (cidx(i), CM_COL0 // GROUP_CH))
    hv = pl.BlockSpec((n, HEAD_PAD), lambda i: (cidx(i), 0))
    hvt = pl.BlockSpec((1, HEAD_PAD, n), lambda i: (cidx(i), 0, 0))
    st = pl.BlockSpec((1, SSM_INNER, SSM_STATE), lambda i: (cidx(i), 0, 0))
    return xs, bm, cm, hv, hvt, st


def _ssd_fwd(xbc, dt, cs, dtt, cst, dskx, *, nc, name):
    t = xbc.shape[0]
    n = CHUNK
    xs_s, bm_s, cm_s, hv_s, hvt_s, st_s = _ssd_specs(nc, False)

    def body(xs_ref, bm_ref, cm_ref, dt_ref, cs_ref, dtt_ref, cst_ref, dsk_ref, y_ref, st_ref, prev):
        @pl.when(pl.program_id(0) % nc == 0)
        def _():
            prev[...] = jnp.zeros_like(prev)

        st_ref[0] = prev[...]
        em = _expand_mat()
        dtx = _expand(dt_ref[...], em)
        csx = _expand(cs_ref[...], em)
        dskx = dsk_ref[...]
        xs = xs_ref[...].astype(F32)
        xdt = xs * dtx
        ecs = jnp.exp(csx)
        dec = jnp.exp(csx[n - 1:n, :] - csx)
        mask = _causal(n)
        lane = lax.broadcasted_iota(jnp.int32, (n, LANE), 1)
        for g in range(SSM_GROUPS):
            gs = slice(g * SSM_STATE, (g + 1) * SSM_STATE)
            gc = slice(g * GROUP_CH, (g + 1) * GROUP_CH)
            cmat = cm_ref[:, gs].astype(MXU_DTYPE)
            bmat = bm_ref[:, gs].astype(MXU_DTYPE)
            cb = _nt(cmat, bmat)
            yoff = ecs[:, gc] * _nt(cmat, prev[gc, :].astype(MXU_DTYPE))
            for q in range(PAIRS_PER_GROUP):
                hp = g * PAIRS_PER_GROUP + q
                sl = slice(hp * LANE, (hp + 1) * LANE)
                xp = xdt[:, sl].astype(MXU_DTYPE)
                m0 = (cb * _decay_mat(cs_ref, cst_ref, 2 * hp, mask)).astype(MXU_DTYPE)
                m1 = (cb * _decay_mat(cs_ref, cst_ref, 2 * hp + 1, mask)).astype(MXU_DTYPE)
                yd = jnp.where(lane < SSM_HEADDIM, _nn(m0, xp), _nn(m1, xp))
                y_ref[:, sl] = (yd + yoff[:, q * LANE:(q + 1) * LANE] + xs[:, sl] * dskx[:, sl]).astype(y_ref.dtype)
            snew = _tn((xdt[:, gc] * dec[:, gc]).astype(MXU_DTYPE), bmat)
            for r in range(HEADS_PER_GROUP):
                h = g * HEADS_PER_GROUP + r
                rows = slice(h * SSM_HEADDIM, (h + 1) * SSM_HEADDIM)
                e = jnp.exp(cst_ref[0, h:h + 1, n - 1:n])
                prev[rows, :] = prev[rows, :] * e + snew[r * SSM_HEADDIM:(r + 1) * SSM_HEADDIM, :]

    return pl.pallas_call(
        body, grid=(t // n,),
        in_specs=[xs_s, bm_s, cm_s, hv_s, hv_s, hvt_s, hvt_s, _par_spec((1, SSM_INNER))],
        out_specs=[xs_s, st_s],
        out_shape=[jax.ShapeDtypeStruct((t, SSM_INNER), STASH_DTYPE), jax.ShapeDtypeStruct((t // n, SSM_INNER, SSM_STATE), F32)],
        scratch_shapes=[pltpu.VMEM((SSM_INNER, SSM_STATE), F32)],
        compiler_params=_params("arbitrary"), name=name)(xbc, xbc, xbc, dt, cs, dtt, cst, dskx)


def _ssd_bwd(dy, xbc, dt, cs, dtt, cst, st, dskx, a_log8, dt_raw, dt_bias8, *, nc, name):
    t = xbc.shape[0]
    n = CHUNK
    xs_s, bm_s, cm_s, hv_s, hvt_s, st_s = _ssd_specs(nc, True)
    acc_s = _par_spec((1, HEAD_PAD))
    xbc_s = pl.BlockSpec((n, SSM_CONV_DIM), xs_s.index_map)

    def body(dy_ref, xs_ref, bm_ref, cm_ref, dt_ref, cs_ref, dtt_ref, cst_ref, st_ref, dsk_ref, al_ref, raw_ref, bias_ref,
             dxbc_ref, ddr_ref, dal_ref, dds_ref, dbias_ref, dprev, dxdt_s, tdec_s, tcs_s):
        @pl.when(pl.program_id(0) % nc == 0)
        def _():
            dprev[...] = jnp.zeros_like(dprev)

        @pl.when(pl.program_id(0) == 0)
        def _():
            dal_ref[...] = jnp.zeros_like(dal_ref)
            dds_ref[...] = jnp.zeros_like(dds_ref)
            dbias_ref[...] = jnp.zeros_like(dbias_ref)

        em = _expand_mat()
        rm = _reduce_mat()

        def head_reduce(v):
            return _dot_exact(v, rm, _DN_NN, 0)

        dtv = dt_ref[...]
        csv = cs_ref[...]
        dtx = _expand(dtv, em)
        csx = _expand(csv, em)
        dskx = dsk_ref[...]
        xs = xs_ref[...].astype(F32)
        dyv = dy_ref[...].astype(F32)
        xdt = xs * dtx
        ecs = jnp.exp(csx)
        dec = jnp.exp(csx[n - 1:n, :] - csx)
        mask = _causal(n)
        lane = lax.broadcasted_iota(jnp.int32, (n, LANE), 1)
        hlane = lax.broadcasted_iota(jnp.int32, (1, HEAD_PAD), 1)
        hsub = lax.broadcasted_iota(jnp.int32, (HEAD_PAD, 1), 0)
        rsum = jnp.zeros((n, HEAD_PAD), F32)
        csum = jnp.zeros((HEAD_PAD, n), F32)
        for g in range(SSM_GROUPS):
            gs = slice(g * SSM_STATE, (g + 1) * SSM_STATE)
            gc = slice(g * GROUP_CH, (g + 1) * GROUP_CH)
            cmat = cm_ref[:, gs].astype(MXU_DTYPE)
            bmat = bm_ref[:, gs].astype(MXU_DTYPE)
            cb = _nt(cmat, bmat)
            pg = st_ref[0, gc, :].astype(MXU_DTYPE)
            dpg = dprev[gc, :]
            dpgb = dpg.astype(MXU_DTYPE)
            z = _nt(cmat, pg)
            dyg = dyv[:, gc]
            dz = (dyg * ecs[:, gc]).astype(MXU_DTYPE)
            dc = _nn(dz, pg)
            dprev_y = _tn(dz, cmat)
            tcs_s[:, gc] = dyg * z * ecs[:, gc]
            xd = xdt[:, gc] * dec[:, gc]
            wmat = _nt(bmat, dpgb)
            db = _nn(xd.astype(MXU_DTYPE), dpgb)
            tdec_s[:, gc] = wmat * xd
            dxdt_g = wmat * dec[:, gc]
            dcb = jnp.zeros((n, n), F32)
            for q in range(PAIRS_PER_GROUP):
                hp = g * PAIRS_PER_GROUP + q
                sl = slice(hp * LANE, (hp + 1) * LANE)
                xp = xdt[:, sl].astype(MXU_DTYPE)
                dyp = dyv[:, sl]
                dypb = dyp.astype(MXU_DTYPE)
                dxp = None
                for hh in range(2):
                    h = 2 * hp + hh
                    lm = _decay_mat(cs_ref, cst_ref, h, mask)
                    mine = (lane < SSM_HEADDIM) if hh == 0 else (lane >= SSM_HEADDIM)
                    dm = _nt(jnp.where(mine, dyp, 0.0).astype(MXU_DTYPE), xp)
                    dml = dm * lm
                    dcb = dcb + dml
                    gseg = dml * cb
                    rsum = rsum + jnp.sum(gseg, axis=1, keepdims=True) * (hlane == h).astype(F32)
                    csum = csum + (hsub == h).astype(F32) * jnp.sum(gseg, axis=0, keepdims=True)
                    dxh = _tn((cb * lm).astype(MXU_DTYPE), dypb)
                    dxp = dxh if dxp is None else jnp.where(mine, dxh, dxp)
                dxdt_s[:, sl] = dxdt_g[:, q * LANE:(q + 1) * LANE] + dxp
            dcbb = dcb.astype(MXU_DTYPE)
            dxbc_ref[:, CM_COL0 + g * SSM_STATE:CM_COL0 + (g + 1) * SSM_STATE] = (dc + _nn(dcbb, bmat)).astype(dxbc_ref.dtype)
            dxbc_ref[:, BM_COL0 + g * SSM_STATE:BM_COL0 + (g + 1) * SSM_STATE] = (db + _tn(dcbb, cmat)).astype(dxbc_ref.dtype)
            for r in range(HEADS_PER_GROUP):
                h = g * HEADS_PER_GROUP + r
                rows = slice(h * SSM_HEADDIM, (h + 1) * SSM_HEADDIM)
                lr = slice(r * SSM_HEADDIM, (r + 1) * SSM_HEADDIM)
                e = jnp.exp(cst_ref[0, h:h + 1, n - 1:n])
                dprev[rows, :] = dpg[lr, :] * e + dprev_y[lr, :]
            tq = _dot_exact(dpg * st_ref[0, gc, :], rm[gc, :], _DN_TN, 0)
            if g == 0:
                qsum = jnp.sum(tq, axis=0, keepdims=True)
            else:
                qsum = qsum + jnp.sum(tq, axis=0, keepdims=True)
        dxdt = dxdt_s[...]
        dxbc_ref[:, 0:SSM_INNER] = (dxdt * dtx + dyv * dskx).astype(dxbc_ref.dtype)
        ddt = head_reduce(dxdt * xs)
        edec = head_reduce(tdec_s[...])
        ycs = head_reduce(tcs_s[...])
        row = lax.broadcasted_iota(jnp.int32, (n, HEAD_PAD), 0)
        extra = jnp.sum(edec, axis=0, keepdims=True) + qsum * jnp.exp(csv[n - 1:n, :])
        dcs = rsum - csum.T + ycs - edec + jnp.where(row == n - 1, extra, 0.0)
        r2 = lax.broadcasted_iota(jnp.int32, (n, n), 0)
        c2 = lax.broadcasted_iota(jnp.int32, (n, n), 1)
        dda = _dot_exact((c2 >= r2).astype(F32), dcs, _DN_NN, 1)
        a_row = -jnp.exp(al_ref[0:1, :])
        ddt = ddt + dda * a_row
        dal_ref[...] += jnp.sum(dda * dtv, axis=0, keepdims=True) * a_row
        dds_ref[...] += jnp.sum(head_reduce(dyv * xs), axis=0, keepdims=True)
        ddr = ddt * _sigmoid(raw_ref[...] + bias_ref[0:1, :])
        ddr_ref[...] = ddr
        dbias_ref[...] += jnp.sum(ddr, axis=0, keepdims=True)

    par8 = _par_spec((SUBLANE, HEAD_PAD))
    return pl.pallas_call(
        body, grid=(t // n,),
        in_specs=[xs_s, xs_s, bm_s, cm_s, hv_s, hv_s, hvt_s, hvt_s, st_s, _par_spec((1, SSM_INNER)), par8, hv_s, par8],
        out_specs=[xbc_s, hv_s, acc_s, acc_s, acc_s],
        out_shape=[jax.ShapeDtypeStruct((t, SSM_CONV_DIM), STASH_DTYPE), jax.ShapeDtypeStruct((t, HEAD_PAD), F32),
                   jax.ShapeDtypeStruct((1, HEAD_PAD), F32), jax.ShapeDtypeStruct((1, HEAD_PAD), F32),
                   jax.ShapeDtypeStruct((1, HEAD_PAD), F32)],
        scratch_shapes=[pltpu.VMEM((SSM_INNER, SSM_STATE), F32), pltpu.VMEM((n, SSM_INNER), F32),
                        pltpu.VMEM((n, SSM_INNER), F32), pltpu.VMEM((n, SSM_INNER), F32)],
        compiler_params=_params("arbitrary"), name=name)(dy, xbc, xbc, xbc, dt, cs, dtt, cst, st, dskx, a_log8, dt_raw, dt_bias8)


def _gate_norm_fwd(y, proj, norm_g, *, name):
    t, c = y.shape
    tm = _pick(t, WIDE_ROW_TILES)

    def body(y_ref, z_ref, g_ref, o_ref):
        z = z_ref[...].astype(F32)
        yz = y_ref[...].astype(F32) * z * _sigmoid(z)
        for g in range(SSM_GROUPS):
            gc = slice(g * GROUP_CH, (g + 1) * GROUP_CH)
            seg = yz[:, gc]
            r = lax.rsqrt(jnp.mean(seg * seg, axis=-1, keepdims=True) + RMS_EPS)
            o_ref[:, gc] = (seg * r * g_ref[:, gc]).astype(MXU_DTYPE)

    return pl.pallas_call(
        body, grid=(t // tm,), in_specs=[_row_spec(tm, c), _row_spec(tm, c, 1), _par_spec((1, c))],
        out_specs=_row_spec(tm, c), out_shape=jax.ShapeDtypeStruct((t, c), MXU_DTYPE),
        compiler_params=_params("parallel"), name=name)(y, proj, norm_g.reshape(1, c))


def _gate_norm_bwd(dyb, y, proj, norm_g, dproj, *, name):
    t, c = y.shape
    tm = _pick(t, WIDE_ROW_TILES)

    def body(d_ref, y_ref, z_ref, g_ref, _, dy_ref, dz_ref, dg_ref):
        @pl.when(pl.program_id(0) == 0)
        def _():
            dg_ref[...] = jnp.zeros_like(dg_ref)

        z = z_ref[...].astype(F32)
        yv = y_ref[...].astype(F32)
        sz = _sigmoid(z)
        silu = z * sz
        yz = yv * silu
        dv = d_ref[...].astype(F32)
        for g in range(SSM_GROUPS):
            gc = slice(g * GROUP_CH, (g + 1) * GROUP_CH)
            seg = yz[:, gc]
            r = lax.rsqrt(jnp.mean(seg * seg, axis=-1, keepdims=True) + RMS_EPS)
            nrm = seg * r
            dn = dv[:, gc] * g_ref[:, gc]
            dg_ref[:, gc] += jnp.sum(dv[:, gc] * nrm, axis=0, keepdims=True)
            dyz = r * (dn - nrm * jnp.mean(dn * nrm, axis=-1, keepdims=True))
            dy_ref[:, gc] = (dyz * silu[:, gc]).astype(dy_ref.dtype)
            dz_ref[:, gc] = (dyz * yv[:, gc] * (sz[:, gc] * (1.0 + z[:, gc] * (1.0 - sz[:, gc])))).astype(MXU_DTYPE)

    return pl.pallas_call(
        body, grid=(t // tm,), in_specs=[_row_spec(tm, c), _row_spec(tm, c), _row_spec(tm, c, 1), _par_spec((1, c)), _ANY],
        out_specs=[_row_spec(tm, c), _row_spec(tm, c, 1), _par_spec((1, c))],
        out_shape=[jax.ShapeDtypeStruct((t, c), STASH_DTYPE), jax.ShapeDtypeStruct(dproj.shape, dproj.dtype),
                   jax.ShapeDtypeStruct((1, c), F32)],
        input_output_aliases={4: 1},
        compiler_params=_params("arbitrary"), name=name)(dyb, y, proj, norm_g.reshape(1, c), dproj)


GA_COLBLK = GAB_COL0 // D_MODEL


def _merge_fwd(br_a, br_b, proj, *, name):
    t, c = br_a.shape
    tm = _pick(t, ROW_TILES)

    def body(a_ref, b_ref, ga_ref, gb_ref, o_ref):
        o_ref[...] = (_sigmoid(ga_ref[...].astype(F32)) * a_ref[...].astype(F32)
                      + _sigmoid(gb_ref[...].astype(F32)) * b_ref[...].astype(F32)).astype(MXU_DTYPE)

    return pl.pallas_call(
        body, grid=(t // tm,),
        in_specs=[_row_spec(tm, c), _row_spec(tm, c), _row_spec(tm, c, GA_COLBLK), _row_spec(tm, c, GA_COLBLK + 1)],
        out_specs=_row_spec(tm, c), out_shape=jax.ShapeDtypeStruct((t, c), MXU_DTYPE),
        compiler_params=_params("parallel"), name=name)(br_a, br_b, proj, proj)


def _merge_bwd(dm, br_a, br_b, proj, *, name):
    t, c = br_a.shape
    tm = _pick(t, ROW_TILES)

    def body(dm_ref, a_ref, b_ref, ga_ref, gb_ref, da_ref, db_ref, dg_ref):
        d = dm_ref[...].astype(F32)
        sa = _sigmoid(ga_ref[...].astype(F32))
        sb = _sigmoid(gb_ref[...].astype(F32))
        da_ref[...] = (d * sa).astype(MXU_DTYPE)
        db_ref[...] = (d * sb).astype(MXU_DTYPE)
        dg_ref[:, :c] = (d * a_ref[...].astype(F32) * sa * (1.0 - sa)).astype(MXU_DTYPE)
        dg_ref[:, c:] = (d * b_ref[...].astype(F32) * sb * (1.0 - sb)).astype(MXU_DTYPE)

    return pl.pallas_call(
        body, grid=(t // tm,),
        in_specs=[_row_spec(tm, c), _row_spec(tm, c), _row_spec(tm, c), _row_spec(tm, c, GA_COLBLK), _row_spec(tm, c, GA_COLBLK + 1)],
        out_specs=[_row_spec(tm, c), _row_spec(tm, c), _row_spec(tm, 2 * c, GAB_COL0 // (2 * c))],
        out_shape=[jax.ShapeDtypeStruct((t, c), MXU_DTYPE), jax.ShapeDtypeStruct((t, c), MXU_DTYPE),
                   jax.ShapeDtypeStruct((t, MAIN_COLS), MXU_DTYPE)],
        compiler_params=_params("parallel"), name=name)(dm, br_a, br_b, proj, proj)


def _layer_fwd(x, xb, memn_b, w, *, bsz, tag):
    nc = x.shape[0] // bsz // CHUNK
    sv = {"x_in": xb}
    proj = _mm(xb, w["w_main"], out_dtype=STASH_DTYPE, name=f"{tag}_proj")
    dt_raw = _mm(xb, w["w_dt"], name=f"{tag}_dtproj")
    sgo = _sg_fwd(proj, w["sg_ln_g"], w["sg_ln_b"], w["sg_w"], w["sg_bcol"], name=f"{tag}_sg_fwd")
    xbc = _conv_fwd(proj, w["conv_w"], w["conv_b"], bsz=bsz, name=f"{tag}_conv_fwd")
    dt, cs, dtt, cst = _ssd_prep(dt_raw, w["dt_bias8"], w["a_log8"], name=f"{tag}_ssd_prep")
    y, st = _ssd_fwd(xbc, dt, cs, dtt, cst, w["d_skipx"], nc=nc, name=f"{tag}_ssd_fwd")
    yb = _gate_norm_fwd(y, proj, w["ssm_norm_g"], name=f"{tag}_gate_norm_fwd")
    if "rest" in w:
        w = w["rest"](w, yb)
    br_a = _mm(sgo, w["p_a"], out_dtype=STASH_DTYPE, name=f"{tag}_br_a")
    br_b = _mm(yb, w["p_b"], out_dtype=STASH_DTYPE, name=f"{tag}_br_b")
    merged = _merge_fwd(br_a, br_b, proj, name=f"{tag}_merge_fwd")
    mix = _mm(merged, w["w_mix_o"], name=f"{tag}_mix_o")
    x1, x1b, xh1, rs1 = _ln_fwd(x, mix, w["ln_g"][0], w["ln_b"][0], name=f"{tag}_ln1_fwd")
    sv.update(proj=proj, dt_raw=dt_raw, sgo=sgo, xbc=xbc, dt=dt, cs=cs, dtt=dtt, cst=cst, y=y, st=st, yb=yb,
              br_a=br_a, br_b=br_b, merged=merged, xh1=xh1, rs1=rs1, x1b=x1b)
    q = _mm(x1b, w["w_xq"], out_dtype=MXU_DTYPE, name=f"{tag}_q")
    kv = _mm(memn_b, w["w_xkv"], out_dtype=MXU_DTYPE, name=f"{tag}_kv")
    o = _attn_fwd(q, kv, bsz=bsz, name=f"{tag}_attn_fwd")
    att = _mm(o, w["w_xo"], name=f"{tag}_xo")
    x2, x2b, xh2, rs2 = _ln_fwd(x1, att, w["ln_g"][1], w["ln_b"][1], name=f"{tag}_ln2_fwd")
    sv.update(q=q, kv=kv, o=o, xh2=xh2, rs2=rs2, x2b=x2b)
    h = _mm(x2b, w["w_ffn_in"], out_dtype=STASH_DTYPE, name=f"{tag}_ffn_in")
    a = _swiglu_fwd(h, name=f"{tag}_swiglu_fwd")
    ffn = _mm(a, w["w_ffn_out"], name=f"{tag}_ffn_out")
    x3, x3b, xh3, rs3 = _ln_fwd(x2, ffn, w["ln_g"][2], w["ln_b"][2], name=f"{tag}_ln3_fwd")
    sv.update(h=h, a=a, xh3=xh3, rs3=rs3)
    return x3, x3b, sv, w


GRAD_GROUPS = (("w_ffn_out", "w_ffn_in", "w_xo", "w_xq", "w_xkv"), ("w_mix_o", "p_a", "p_b"), ("w_in",))


def _layer_bwd(dx3_addends, dx3_scales, memn_b, w, sv, on_group=None, *, bsz, tag):
    nc = sv["xh1"].shape[0] // bsz // CHUNK
    gr = {}

    def group_done(k):
        return on_group(GRAD_GROUPS[k], gr) if on_group is not None else None
    dp3, dp3b, dg3, db3 = _ln_bwd(dx3_addends, dx3_scales, sv["xh3"], sv["rs3"], w["ln_g"][2], name=f"{tag}_ln3_bwd")
    da = _mm(dp3b, w["w_ffn_out"], tb=True, out_dtype=STASH_DTYPE, name=f"{tag}_d_a")
    gr["w_ffn_out"] = _mm(sv["a"], dp3b, ta=True, name=f"{tag}_dw_ffn_out")
    dh = _swiglu_bwd(sv["h"], da, name=f"{tag}_swiglu_bwd")
    gr["w_ffn_in"] = _mm(sv["x2b"], dh, ta=True, name=f"{tag}_dw_ffn_in")
    dx2_br = _mm(dh, w["w_ffn_in"], tb=True, name=f"{tag}_dx2")
    dp2, dp2b, dg2, db2 = _ln_bwd([dp3, dx2_br], [ALPHA, 1.0], sv["xh2"], sv["rs2"], w["ln_g"][1], name=f"{tag}_ln2_bwd")
    do = _mm(dp2b, w["w_xo"], tb=True, out_dtype=MXU_DTYPE, name=f"{tag}_d_o")
    gr["w_xo"] = _mm(sv["o"], dp2b, ta=True, name=f"{tag}_dw_xo")
    dq, dk, dv = _attn_bwd(sv["q"], sv["kv"], do, bsz=bsz, name=f"{tag}_attn_bwd")
    dkv = jnp.concatenate([dk, dv], axis=1)
    gr["w_xq"] = _mm(sv["x1b"], dq, ta=True, name=f"{tag}_dw_xq")
    gr["w_xkv"] = _mm(memn_b, dkv, ta=True, name=f"{tag}_dw_xkv")
    dmemn = _mm(dkv, w["w_xkv"], tb=True, name=f"{tag}_d_memn")
    dx1_br = _mm(dq, w["w_xq"], tb=True, name=f"{tag}_dx1")
    token = group_done(0)
    ln_g1 = w["ln_g"][0] if token is None else w["ln_g"][0] + token[0, 0]
    dp1, dp1b, dg1, db1 = _ln_bwd([dp2, dx1_br], [ALPHA, 1.0], sv["xh1"], sv["rs1"], ln_g1, name=f"{tag}_ln1_bwd")
    gr["ln_g"] = jnp.concatenate([dg1, dg2, dg3], axis=0)
    gr["ln_b"] = jnp.concatenate([db1, db2, db3], axis=0)
    dmerged = _mm(dp1b, w["w_mix_o"], tb=True, out_dtype=STASH_DTYPE, name=f"{tag}_d_merged")
    gr["w_mix_o"] = _mm(sv["merged"], dp1b, ta=True, name=f"{tag}_dw_mix_o")
    dbr_a, dbr_b, dproj = _merge_bwd(dmerged, sv["br_a"], sv["br_b"], sv["proj"], name=f"{tag}_merge_bwd")
    gr["p_a"] = _mm(sv["sgo"], dbr_a, ta=True, name=f"{tag}_dw_p_a")
    gr["p_b"] = _mm(sv["yb"], dbr_b, ta=True, name=f"{tag}_dw_p_b")
    dsgo = _mm(dbr_a, w["p_a"], tb=True, out_dtype=STASH_DTYPE, name=f"{tag}_d_sgo")
    dyb = _mm(dbr_b, w["p_b"], tb=True, out_dtype=STASH_DTYPE, name=f"{tag}_d_yb")
    token = group_done(1)
    norm_g = w["ssm_norm_g"] if token is None else w["ssm_norm_g"] + token[0, 0]
    dy, dproj, gr["ssm_norm_g"] = _gate_norm_bwd(dyb, sv["y"], sv["proj"], norm_g, dproj, name=f"{tag}_gate_norm_bwd")
    dxbc, ddr, gr["a_log"], gr["d_skip"], gr["dt_bias"] = _ssd_bwd(
        dy, sv["xbc"], sv["dt"], sv["cs"], sv["dtt"], sv["cst"], sv["st"], w["d_skipx"], w["a_log8"], sv["dt_raw"],
        w["dt_bias8"], nc=nc, name=f"{tag}_ssd_bwd")
    dproj, gr["conv_w"], gr["conv_b"] = _conv_bwd(sv["proj"], dxbc, w["conv_w"], w["conv_b"], dproj, bsz=bsz, name=f"{tag}_conv_bwd")
    dproj, gr["sg_w"], dsg_bcol, gr["sg_ln_g"], gr["sg_ln_b"] = _sg_bwd(
        sv["proj"], dsgo, w["sg_ln_g"], w["sg_ln_b"], w["sg_w"], w["sg_bcol"], dproj, name=f"{tag}_sg_bwd")
    gr["sg_b"] = dsg_bcol[..., 0]
    gr["w_main"] = _mm(sv["x_in"], dproj, ta=True, name=f"{tag}_dw_main")
    gr["w_dt"] = _mm(sv["x_in"], ddr, ta=True, name=f"{tag}_dw_dt")
    token = group_done(2)
    dx_dt = _mm(ddr, w["w_dt"], tb=True, after=token, name=f"{tag}_dx_dt")
    dx_main = _mm(dproj, w["w_main"], tb=True, after=token, name=f"{tag}_dx_main")
    return [dp1, dx_main, dx_dt], [ALPHA, 1.0, 1.0], gr, dmemn


def _local_step(x, mem, tgt, mem_ln_g, mem_ln_b, layers, on_layer_grads=None):
    bsz, s, d = x.shape
    xf = x.reshape(bsz * s, d)
    memf = mem.reshape(-1, d)
    _, memn_b, mxh, mrs = _ln_fwd(memf, None, mem_ln_g, mem_ln_b, name="mem_ln_fwd")
    cur, curb, saved, weights = xf, xf, [], []
    for li, get_weights in enumerate(layers):
        cur, curb, sv, w = _layer_fwd(cur, curb, memn_b, get_weights(cur), bsz=bsz, tag=f"l{li}")
        saved.append(sv)
        weights.append(w)
    dy, lsum = _loss_head(cur, tgt.reshape(bsz * s, d), name="loss_head")
    addends, scales = [dy], [1.0]
    grads, dmem = [None] * len(layers), []
    for li in reversed(range(len(layers))):
        on_group = None if on_layer_grads is None else functools.partial(on_layer_grads, li)
        addends, scales, grads[li], dm = _layer_bwd(addends, scales, memn_b, weights[li], saved[li], on_group, bsz=bsz, tag=f"l{li}")
        dmem.append(dm)
    grad_x = _add_scaled(addends, scales, name="grad_x").reshape(bsz, s, d)
    _, _, dmg, dmb = _ln_bwd(dmem, [1.0] * len(dmem), mxh, mrs, mem_ln_g, name="mem_ln_bwd")
    return lsum, grad_x, grads, dmg[0], dmb[0]


_ANY = pl.BlockSpec(memory_space=pl.ANY)
_MESH = pl.DeviceIdType.MESH


def _all_gather8(x, *, name):
    def body(x_ref, out_ref, send_sems, recv_sems):
        mx, my, mc = lax.axis_index("x"), lax.axis_index("y"), lax.axis_index("c")
        me, sibling = (mx, my, mc), (mx, my, 1 - mc)
        chips = [(1 - mx, my), (mx, 1 - my), (1 - mx, 1 - my)]

        def blk(px, py, pc):
            return out_ref.at[4 * px + 2 * py + pc]

        def copy(k, block, to, src=None):
            return pltpu.make_async_remote_copy(
                src_ref=blk(*block) if src is None else src, dst_ref=blk(*block), send_sem=send_sems.at[k],
                recv_sem=recv_sems.at[k], device_id=to, device_id_type=_MESH)

        first = [copy(0, me, sibling, src=x_ref)]
        first += [copy(1 + j, me, (*chip, mc), src=x_ref) for j, chip in enumerate(chips)]
        for cp in first:
            cp.start()
        passed = [copy(4 + j, (*chip, mc), sibling) for j, chip in enumerate(chips)]
        for j, chip in enumerate(chips):
            copy(1 + j, (*chip, mc), me).wait_recv()
            passed[j].start()
        copy(0, sibling, me).wait_recv()
        for j, chip in enumerate(chips):
            copy(4 + j, (*chip, 1 - mc), me).wait_recv()
        for cp in first + passed:
            cp.wait_send()

    return pl.pallas_call(
        body, out_shape=jax.ShapeDtypeStruct((N_DEV,) + x.shape, x.dtype), in_specs=[_ANY], out_specs=_ANY,
        scratch_shapes=[pltpu.SemaphoreType.DMA((7,)), pltpu.SemaphoreType.DMA((7,))], name=name)(x)


def _row_tile(rows, row_bytes, mult=SUBLANE):
    best = None
    for tr in range(mult, rows + 1, mult):
        if rows % tr == 0 and (best is None or tr * row_bytes <= BLOCK_BYTES):
            best = tr
    return rows if best is None else best


def _gather_shape(r, c, kind):
    return {"row": (2, N_CHIPS * r, c), "col": (2, r, N_CHIPS * c), "chip": (2, N_CHIPS, r, c)}[kind]


def _cast_place(shard, kind, dtype, chip_idx, *, name):
    _, r, c = shard.shape
    tr = _row_tile(r, c * 4, 16)
    nt = r // tr

    def body(_, s_ref, o_ref):
        o_ref[...] = s_ref[...].astype(dtype)

    if kind == "row":
        out_spec = pl.BlockSpec((None, tr, c), lambda l, i, j_ref: (l, j_ref[0] * nt + i, 0))
    elif kind == "col":
        out_spec = pl.BlockSpec((None, tr, c), lambda l, i, j_ref: (l, i, j_ref[0]))
    else:
        out_spec = pl.BlockSpec((None, None, tr, c), lambda l, i, j_ref: (l, j_ref[0], i, 0))
    grid_spec = pltpu.PrefetchScalarGridSpec(
        num_scalar_prefetch=1, grid=(2, nt), in_specs=[pl.BlockSpec((None, tr, c), lambda l, i, j_ref: (l, i, 0))],
        out_specs=out_spec)
    return pl.pallas_call(body, grid_spec=grid_spec, out_shape=jax.ShapeDtypeStruct(_gather_shape(r, c, kind), dtype),
                          compiler_params=_params("parallel", "parallel"), name=name)(chip_idx, shard)


def _gather_params(bufs, shard_shapes, kinds, *, name):
    n = len(bufs)

    def body(*refs):
        outs = refs[n:2 * n]
        send_sems, recv_sems = refs[2 * n:]
        mx, my, mc = lax.axis_index("x"), lax.axis_index("y"), lax.axis_index("c")
        me, sibling = (mx, my, mc), (mx, my, 1 - mc)
        chips = [(1 - mx, my), (mx, 1 - my), (1 - mx, 1 - my)]

        def blk(i, px, py, pc):
            r, c = shard_shapes[i]
            j = 2 * px + py
            if kinds[i] == "row":
                return outs[i].at[pc, pl.ds(pl.multiple_of(j * r, r), r)]
            if kinds[i] == "col":
                return outs[i].at[pc, :, pl.ds(pl.multiple_of(j * c, c), c)]
            return outs[i].at[pc, j]

        def copy(i, k, block, to):
            return pltpu.make_async_remote_copy(
                src_ref=blk(i, *block), dst_ref=blk(i, *block), send_sem=send_sems.at[6 * i + k],
                recv_sem=recv_sems.at[6 * i + k], device_id=to, device_id_type=_MESH)

        sent = []
        for i in range(n):
            for j, chip in enumerate(chips):
                cp = copy(i, j, me, (*chip, mc))
                cp.start()
                sent.append(cp)
        for j, chip in enumerate(chips):
            for i in range(n):
                copy(i, j, (*chip, mc), me).wait_recv()
                fwd = copy(i, 3 + j, (*chip, mc), sibling)
                fwd.start()
                sent.append(fwd)
        for i in range(n):
            for j, chip in enumerate(chips):
                copy(i, 3 + j, (*chip, 1 - mc), me).wait_recv()
        for cp in sent:
            cp.wait_send()

    return pl.pallas_call(
        body, out_shape=[jax.ShapeDtypeStruct(b.shape, b.dtype) for b in bufs], in_specs=[_ANY] * n, out_specs=[_ANY] * n,
        input_output_aliases={i: i for i in range(n)},
        scratch_shapes=[pltpu.SemaphoreType.DMA((6 * n,)), pltpu.SemaphoreType.DMA((6 * n,))], name=name)(*bufs)


def _half(r, h):
    return pl.ds(pl.multiple_of(h * (r // 2), r // 2), r // 2)


_HBM = pl.BlockSpec(memory_space=pltpu.HBM)
_SEM = pl.BlockSpec(memory_space=pltpu.SEMAPHORE)
_EFFECT = pltpu.SideEffectType.DATAFLOW_SIDE_EFFECTING


def _sibling_copies(g_refs, land_refs, gs, views, send_sems, recv_sems):
    mx, my, mc = lax.axis_index("x"), lax.axis_index("y"), lax.axis_index("c")
    copies = []
    for i in range(len(gs)):
        if views[i] == "chip":
            src = g_refs[i].at[:, _half(gs[i].shape[1], 1 - mc)]
        else:
            src = g_refs[i].at[_half(gs[i].shape[0], 1 - mc)]
        copies.append(pltpu.make_async_remote_copy(src_ref=src, dst_ref=land_refs[i], send_sem=send_sems.at[i], recv_sem=recv_sems.at[i],
                                                   device_id=(mx, my, 1 - mc), device_id_type=_MESH))
    return copies


def _half_shape(g, view):
    return (g.shape[0], g.shape[1] // 2, g.shape[2]) if view == "chip" else (g.shape[0] // 2, g.shape[1])


def _grads_to_sibling_start(gs, views, *, name):
    n = len(gs)
    lands = [pltpu.with_memory_space_constraint(lax.empty(_half_shape(g, v), g.dtype), pltpu.HBM) for g, v in zip(gs, views)]

    def body(*refs):
        for cp in _sibling_copies(refs[:n], refs[n:2 * n], gs, views, refs[2 * n], refs[2 * n + 1]):
            cp.start()
        refs[-1][...] = jnp.zeros_like(refs[-1])

    outs = pl.pallas_call(
        body, name=name,
        out_shape=(pltpu.SemaphoreType.DMA((n,)), pltpu.SemaphoreType.DMA((n,)),
                   *[pltpu.HBM(x.shape, x.dtype) for x in list(gs) + lands], jax.ShapeDtypeStruct((SUBLANE, LANE), F32)),
        in_specs=[_HBM] * (2 * n), out_specs=(_SEM, _SEM, *[_HBM] * (2 * n), pl.BlockSpec(memory_space=pltpu.VMEM)),
        input_output_aliases={i: 2 + i for i in range(2 * n)},
        compiler_params=pltpu.CompilerParams(has_side_effects=_EFFECT),
    )(*[pltpu.with_memory_space_constraint(g, pltpu.HBM) for g in gs], *lands)
    return outs[0], outs[1], list(outs[2:2 + n]), list(outs[2 + n:2 + 2 * n]), outs[-1]


def _grads_to_sibling_wait(send_sems, recv_sems, gs, lands, views, after, *, name):
    n = len(gs)

    def body(*refs):
        for cp in _sibling_copies(refs[:n], refs[n:2 * n], gs, views, refs[2 * n], refs[2 * n + 1]):
            cp.wait_send()
            cp.wait_recv()

    outs = pl.pallas_call(
        body, name=name, out_shape=tuple(pltpu.HBM(x.shape, x.dtype) for x in list(gs) + list(lands)),
        in_specs=[_HBM] * (2 * n) + [_SEM, _SEM, _ANY], out_specs=tuple([_HBM] * (2 * n)),
        input_output_aliases={i: i for i in range(2 * n)},
        compiler_params=pltpu.CompilerParams(has_side_effects=_EFFECT),
    )(*gs, *lands, send_sems, recv_sems, after)
    return list(outs[:n]), list(outs[n:])


def _cast_place_layer(shard, l, kind, chip_idx, after, *, name):
    _, r, c = shard.shape
    tr = _row_tile(r, c * 4, 16)
    nt = r // tr

    def body(_, s_ref, *rest):
        rest[-1][...] = s_ref[...].astype(MXU_DTYPE)

    if kind == "row":
        out_spec = pl.BlockSpec((tr, c), lambda i, j_ref: (j_ref[0] * nt + i, 0))
    elif kind == "col":
        out_spec = pl.BlockSpec((tr, c), lambda i, j_ref: (i, j_ref[0]))
    else:
        out_spec = pl.BlockSpec((None, tr, c), lambda i, j_ref: (j_ref[0], i, 0))
    extra = [] if after is None else [after]
    grid_spec = pltpu.PrefetchScalarGridSpec(
        num_scalar_prefetch=1, grid=(nt,), in_specs=[pl.BlockSpec((None, tr, c), lambda i, j_ref: (l, i, 0))] + [_ANY] * len(extra),
        out_specs=out_spec)
    return pl.pallas_call(body, grid_spec=grid_spec, out_shape=jax.ShapeDtypeStruct(_gather_shape(r, c, kind)[1:], MXU_DTYPE),
                          compiler_params=_params("parallel"), name=name)(chip_idx, shard, *extra)


def _half_block(ref, kind, r, c, j, h):
    rows = _half(r, h)
    if kind == "row":
        return ref.at[pl.ds(pl.multiple_of(j * r + h * (r // 2), r // 2), r // 2)]
    if kind == "col":
        return ref.at[rows, pl.ds(pl.multiple_of(j * c, c), c)]
    return ref.at[j, rows]


def _gather_ici_copies(buf_refs, shapes, kinds, send_sems, recv_sems):
    mx, my, mc = lax.axis_index("x"), lax.axis_index("y"), lax.axis_index("c")
    chips = [(1 - mx, my), (mx, 1 - my), (1 - mx, 1 - my)]
    copies = []
    for i, (r, c) in enumerate(shapes):
        mine = _half_block(buf_refs[i], kinds[i], r, c, 2 * mx + my, mc)
        for k, (px, py) in enumerate(chips):
            copies.append(pltpu.make_async_remote_copy(
                src_ref=mine, dst_ref=mine, send_sem=send_sems.at[3 * i + k], recv_sem=recv_sems.at[3 * i + k],
                device_id=(px, py, mc), device_id_type=_MESH))
    return copies


def _gather_start(bufs, shapes, kinds, *, name):
    n = len(bufs)

    def body(*refs):
        send_sems, recv_sems, token = refs[n], refs[n + 1], refs[-1]
        for cp in _gather_ici_copies(refs[:n], shapes, kinds, send_sems, recv_sems):
            cp.start()
        token[...] = jnp.zeros_like(token)

    outs = pl.pallas_call(
        body, name=name,
        out_shape=(pltpu.SemaphoreType.DMA((3 * n,)), pltpu.SemaphoreType.DMA((3 * n,)),
                   *[pltpu.HBM(b.shape, b.dtype) for b in bufs], jax.ShapeDtypeStruct((SUBLANE, LANE), F32)),
        in_specs=[_HBM] * n, out_specs=(_SEM, _SEM, *[_HBM] * n, pl.BlockSpec(memory_space=pltpu.VMEM)),
        input_output_aliases={i: 2 + i for i in range(n)},
        compiler_params=pltpu.CompilerParams(has_side_effects=_EFFECT),
    )(*[pltpu.with_memory_space_constraint(b, pltpu.HBM) for b in bufs])
    return outs[0], outs[1], list(outs[2:2 + n]), outs[-1]


def _gather_wait(send_sems, recv_sems, bufs, shapes, kinds, after, *, name):
    n = len(bufs)

    def body(*refs):
        for cp in _gather_ici_copies(refs[:n], shapes, kinds, refs[n], refs[n + 1]):
            cp.wait_send()
            cp.wait_recv()

    outs = pl.pallas_call(
        body, name=name, out_shape=tuple(pltpu.HBM(b.shape, b.dtype) for b in bufs),
        in_specs=[_HBM] * n + [_SEM, _SEM, _ANY], out_specs=tuple([_HBM] * n), input_output_aliases={i: i for i in range(n)},
        compiler_params=pltpu.CompilerParams(has_side_effects=_EFFECT),
    )(*bufs, send_sems, recv_sems, after)
    return list(outs)


def _gather_forward(bufs, shapes, kinds, *, name):
    n = len(bufs)

    def body(*refs):
        outs = refs[n:2 * n]
        send_sems, recv_sems = refs[2 * n:]
        mx, my, mc = lax.axis_index("x"), lax.axis_index("y"), lax.axis_index("c")
        chips = [(1 - mx, my), (mx, 1 - my), (1 - mx, 1 - my)]
        copies = []
        for i, (r, c) in enumerate(shapes):
            for k, (px, py) in enumerate(chips):
                got = _half_block(outs[i], kinds[i], r, c, 2 * px + py, mc)
                cp = pltpu.make_async_remote_copy(src_ref=got, dst_ref=got, send_sem=send_sems.at[3 * i + k],
                                                  recv_sem=recv_sems.at[3 * i + k], device_id=(mx, my, 1 - mc), device_id_type=_MESH)
                cp.start()
                copies.append(cp)
        for cp in copies:
            cp.wait()

    return pl.pallas_call(
        body, out_shape=[jax.ShapeDtypeStruct(b.shape, b.dtype) for b in bufs], in_specs=[_ANY] * n, out_specs=[_ANY] * n,
        input_output_aliases={i: i for i in range(n)},
        scratch_shapes=[pltpu.SemaphoreType.DMA((3 * n,)), pltpu.SemaphoreType.DMA((3 * n,))], name=name)(*bufs)


def _chip_exchange_copies(pair_refs, land_refs, pairs, views, send_sems, recv_sems):
    mx, my, mc = lax.axis_index("x"), lax.axis_index("y"), lax.axis_index("c")
    me = 2 * mx + my
    chips = [(1 - mx, my), (mx, 1 - my), (1 - mx, 1 - my)]
    copies = []
    for i in range(len(pairs)):
        for k, (px, py) in enumerate(chips):
            j = 2 * px + py
            if views[i] == "chip":
                src = pair_refs[i].at[j]
            else:
                c = pairs[i].shape[1] // N_CHIPS
                src = pair_refs[i].at[:, pl.ds(pl.multiple_of(j * c, c), c)]
            copies.append(pltpu.make_async_remote_copy(
                src_ref=src, dst_ref=land_refs[i].at[me], send_sem=send_sems.at[3 * i + k], recv_sem=recv_sems.at[3 * i + k],
                device_id=(px, py, mc), device_id_type=_MESH))
    return copies


def _quad_shape(p, view):
    return p.shape if view == "chip" else (N_CHIPS, p.shape[0], p.shape[1] // N_CHIPS)


def _grads_to_chips_start(pairs, views, *, name):
    n = len(pairs)
    lands = [pltpu.with_memory_space_constraint(lax.empty(_quad_shape(p, v), p.dtype), pltpu.HBM) for p, v in zip(pairs, views)]

    def body(*refs):
        pair_refs, land_refs = refs[:n], refs[n:2 * n]
        send_sems, recv_sems = refs[2 * n], refs[2 * n + 1]
        token = refs[-1]
        for cp in _chip_exchange_copies(pair_refs, land_refs, pairs, views, send_sems, recv_sems):
            cp.start()
        token[...] = jnp.zeros_like(token)

    outs = pl.pallas_call(
        body, name=name,
        out_shape=(pltpu.SemaphoreType.DMA((3 * n,)), pltpu.SemaphoreType.DMA((3 * n,)),
                   *[pltpu.HBM(p.shape, p.dtype) for p in pairs], *[pltpu.HBM(l.shape, l.dtype) for l in lands],
                   jax.ShapeDtypeStruct((SUBLANE, LANE), F32)),
        in_specs=[_HBM] * (2 * n), out_specs=(_SEM, _SEM, *[_HBM] * (2 * n), pl.BlockSpec(memory_space=pltpu.VMEM)),
        input_output_aliases={i: 2 + i for i in range(2 * n)},
        compiler_params=pltpu.CompilerParams(has_side_effects=_EFFECT),
    )(*[pltpu.with_memory_space_constraint(p, pltpu.HBM) for p in pairs], *lands)
    return outs[0], outs[1], list(outs[2:2 + n]), list(outs[2 + n:2 + 2 * n]), outs[-1]


def _grads_to_chips_wait(send_sems, recv_sems, pairs, lands, views, after, *, name):
    n = len(pairs)

    def body(*refs):
        pair_refs, land_refs = refs[:n], refs[n:2 * n]
        s_sems, r_sems = refs[2 * n], refs[2 * n + 1]
        for cp in _chip_exchange_copies(pair_refs, land_refs, pairs, views, s_sems, r_sems):
            cp.wait_send()
            cp.wait_recv()

    outs = pl.pallas_call(
        body, name=name, out_shape=tuple(pltpu.HBM(x.shape, x.dtype) for x in list(pairs) + list(lands)),
        in_specs=[_HBM] * (2 * n) + [_SEM, _SEM, _ANY], out_specs=tuple([_HBM] * (2 * n)),
        input_output_aliases={i: i for i in range(2 * n)},
        compiler_params=pltpu.CompilerParams(has_side_effects=_EFFECT),
    )(*pairs, *lands, send_sems, recv_sems, after)
    return list(outs[n:])


def _grads_share(tots, *, name):
    n = len(tots)

    def body(*refs):
        ins, outs = refs[:n], refs[n:2 * n]
        send_sems, recv_sems = refs[2 * n:]
        mx, my, mc = lax.axis_index("x"), lax.axis_index("y"), lax.axis_index("c")
        copies = []
        for i in range(n):
            cp = pltpu.make_async_remote_copy(src_ref=ins[i], dst_ref=outs[i], send_sem=send_sems.at[i], recv_sem=recv_sems.at[i],
                                              device_id=(mx, my, 1 - mc), device_id_type=_MESH)
            cp.start()
            copies.append(cp)
        for cp in copies:
            cp.wait()

    return pl.pallas_call(
        body, out_shape=[jax.ShapeDtypeStruct(t.shape, t.dtype) for t in tots], in_specs=[_ANY] * n, out_specs=[_ANY] * n,
        scratch_shapes=[pltpu.SemaphoreType.DMA((n,)), pltpu.SemaphoreType.DMA((n,))], name=name)(*tots)


def _pair_sum(g, recv, view, c_idx, *, name):
    def body(c_ref, a_ref, b_ref, o_ref):
        o_ref[...] = (a_ref[...] + b_ref[...]).astype(WIRE_DTYPE)

    if view == "chip":
        nch, r, c = g.shape
        tr = _row_tile(r // 2, nch * c * 4, 16)
        gv = g.reshape(nch, 2, r // 2, c)
        grid = ((r // 2) // tr,)
        in_specs = [pl.BlockSpec((nch, None, tr, c), lambda i, c_ref: (0, c_ref[0], i, 0)),
                    pl.BlockSpec((nch, tr, c), lambda i, c_ref: (0, i, 0))]
        out_spec = pl.BlockSpec((nch, tr, c), lambda i, c_ref: (0, i, 0))
        sem = ("parallel",)
    else:
        r, c4 = g.shape
        tr = _row_tile(r // 2, c4 * 4, 16)
        gv = g.reshape(2, r // 2, c4)
        grid = ((r // 2) // tr,)
        in_specs = [pl.BlockSpec((None, tr, c4), lambda i, c_ref: (c_ref[0], i, 0)), pl.BlockSpec((tr, c4), lambda i, c_ref: (i, 0))]
        out_spec = pl.BlockSpec((tr, c4), lambda i, c_ref: (i, 0))
        sem = ("parallel",)
    grid_spec = pltpu.PrefetchScalarGridSpec(num_scalar_prefetch=1, grid=grid, in_specs=in_specs, out_specs=out_spec)
    return pl.pallas_call(body, grid_spec=grid_spec, out_shape=jax.ShapeDtypeStruct(recv.shape, WIRE_DTYPE),
                          compiler_params=_params(*sem), name=name)(c_idx, gv, recv)


def _quad_sum(gs, recvs, quads, view, chip_idx, c_idx, *, name):
    nl = len(quads)
    nch, rh, c = quads[0].shape
    tr = _row_tile(rh, c * 4, 16)

    def body(_, __, *refs):
        o_ref = refs[-1]
        per = nch + 1
        for l in range(nl):
            grp = refs[l * per:(l + 1) * per]
            acc = grp[0][...] + grp[1][...]
            for r in grp[2:]:
                acc = acc + r[...].astype(F32)
            o_ref[l] = acc

    if view == "chip":
        own = [pl.BlockSpec((None, None, tr, c), lambda i, j, h: (j[0], h[0], i, 0)),
               pl.BlockSpec((None, tr, c), lambda i, j, h: (j[0], i, 0))]
        gviews = [g.reshape(nch, 2, rh, c) for g in gs]
    else:
        own = [pl.BlockSpec((None, tr, c), lambda i, j, h: (h[0], i, j[0])), pl.BlockSpec((tr, c), lambda i, j, h: (i, j[0]))]
        gviews = [g.reshape(2, rh, nch * c) for g in gs]
    assert nch & (nch - 1) == 0
    got = [pl.BlockSpec((None, tr, c), functools.partial(lambda i, j, h, k: ((j[0] + k) & (nch - 1), i, 0), k=k))
           for k in range(1, nch)]
    ins = []
    for l in range(nl):
        ins += [gviews[l], recvs[l]] + [quads[l]] * (nch - 1)
    grid_spec = pltpu.PrefetchScalarGridSpec(
        num_scalar_prefetch=2, grid=(rh // tr,), in_specs=(own + got) * nl,
        out_specs=pl.BlockSpec((nl, tr, c), lambda i, j, h: (0, i, 0)))
    return pl.pallas_call(body, grid_spec=grid_spec, out_shape=jax.ShapeDtypeStruct((nl, rh, c), F32),
                          compiler_params=_params("parallel"), name=name)(chip_idx, c_idx, *ins)


def _sum_devices(g8, own, dev_idx, *, name):
    k, rows, cols = g8.shape

    def body(d_ref, a_ref, x_ref, o_ref):
        acc = None
        for i in range(k):
            term = jnp.where(d_ref[0] == i, x_ref[...], a_ref[i])
            acc = term if acc is None else acc + term
        o_ref[...] = acc

    grid_spec = pltpu.PrefetchScalarGridSpec(
        num_scalar_prefetch=1, grid=(1,),
        in_specs=[pl.BlockSpec((k, rows, cols), lambda i, d_ref: (0, 0, 0)), pl.BlockSpec((rows, cols), lambda i, d_ref: (0, 0))],
        out_specs=pl.BlockSpec((rows, cols), lambda i, d_ref: (0, 0)))
    return pl.pallas_call(body, grid_spec=grid_spec, out_shape=jax.ShapeDtypeStruct((rows, cols), g8.dtype),
                          compiler_params=_params("arbitrary"), name=name)(dev_idx, g8, own)


def _adamw(w, g, m, v, *, name):
    rows, cols = w.shape
    tr = rows
    for cand in (256, 128, 64, 32, 16, 8):
        if rows % cand == 0 and cand * cols <= 512 * 1024:
            tr = cand
            break
    c1 = 1.0 - ADAM_B1 ** ADAM_STEP
    c2 = 1.0 - ADAM_B2 ** ADAM_STEP

    def body(w_ref, g_ref, m_ref, v_ref, d_ref, nm_ref, nv_ref):
        gv = g_ref[...]
        nm = ADAM_B1 * m_ref[...] + (1.0 - ADAM_B1) * gv
        nv = ADAM_B2 * v_ref[...] + (1.0 - ADAM_B2) * (gv * gv)
        d_ref[...] = -ADAM_LR * ((nm / c1) / (jnp.sqrt(nv / c2) + ADAM_EPS) + ADAM_WD * w_ref[...])
        nm_ref[...] = nm
        nv_ref[...] = nv

    spec = pl.BlockSpec((tr, cols), lambda i: (i, 0))
    shp = jax.ShapeDtypeStruct((rows, cols), F32)
    return pl.pallas_call(body, grid=(rows // tr,), in_specs=[spec] * 4, out_specs=[spec] * 3, out_shape=[shp] * 3,
                          compiler_params=_params("parallel"), name=name)(w, g, m, v)


def _adamw_halves(w, m, v, mine, other, c_idx, *, name):
    nl, r, c = w.shape
    rh = r // 2
    tr = _row_tile(rh, c * 4)
    c1 = 1.0 - ADAM_B1 ** ADAM_STEP
    c2 = 1.0 - ADAM_B2 ** ADAM_STEP

    def body(c_ref, w_ref, m_ref, v_ref, a_ref, b_ref, g_ref, d_ref, nm_ref, nv_ref):
        gv = jnp.where(pl.program_id(1) == c_ref[0], a_ref[...], b_ref[...])
        nm = ADAM_B1 * m_ref[...] + (1.0 - ADAM_B1) * gv
        nv = ADAM_B2 * v_ref[...] + (1.0 - ADAM_B2) * (gv * gv)
        g_ref[...] = gv
        d_ref[...] = -ADAM_LR * ((nm / c1) / (jnp.sqrt(nv / c2) + ADAM_EPS) + ADAM_WD * w_ref[...])
        nm_ref[...] = nm
        nv_ref[...] = nv

    full = pl.BlockSpec((None, None, tr, c), lambda l, h, i, c_ref: (l, h, i, 0))
    half_mine = pl.BlockSpec((None, tr, c), lambda l, h, i, c_ref: (l, jnp.where(h == c_ref[0], i, 0), 0))
    half_other = pl.BlockSpec((None, tr, c), lambda l, h, i, c_ref: (l, jnp.where(h == c_ref[0], 0, i), 0))
    grid_spec = pltpu.PrefetchScalarGridSpec(num_scalar_prefetch=1, grid=(nl, 2, rh // tr),
                                             in_specs=[full] * 3 + [half_mine, half_other], out_specs=[full] * 4)
    shp = jax.ShapeDtypeStruct((nl, 2, rh, c), F32)
    view = (nl, 2, rh, c)
    outs = pl.pallas_call(body, grid_spec=grid_spec, out_shape=[shp] * 4, compiler_params=_params("arbitrary", "arbitrary", "arbitrary"),
                          name=name)(c_idx, w.reshape(view), m.reshape(view), v.reshape(view), mine, other)
    return [o.reshape(nl, r, c) for o in outs]


WEIGHTS = ["mem_ln_g", "mem_ln_b", "w_in", "sg_ln_g", "sg_ln_b", "sg_w", "sg_b", "conv_w", "conv_b", "dt_bias", "a_log",
           "d_skip", "ssm_norm_g", "p_a", "p_b", "w_mix_o", "w_xq", "w_xkv", "w_xo", "w_ffn_in", "w_ffn_out", "ln_g", "ln_b"]
ARG_NAMES = ["x", "mem"] + WEIGHTS + ["loss_target"] + ["m_" + n for n in WEIGHTS] + ["v_" + n for n in WEIGHTS]
BIG = {"w_in": (1, (1024, 9248)), "p_a": (0, (1024, 1024)), "p_b": (0, (2048, 1024)), "w_mix_o": (0, (1024, 1024)),
       "w_xq": (0, (1024, 1024)), "w_xkv": (1, (1024, 2048)), "w_xo": (0, (1024, 1024)), "w_ffn_in": (1, (1024, 5632)),
       "w_ffn_out": (0, (2816, 1024))}
SMALL_SHARDED = {"conv_w": (4, 3072), "ln_g": (3, 1024), "ln_b": (3, 1024)}
SMALL = [n for n in WEIGHTS if n not in BIG]
W_IN_MAP = ((0, 4096, "main", 0), (4096, 7168, "main", XBC_COL0), (7168, 7200, "dt", 0), (7200, 9248, "main", GAB_COL0))
W_IN_SHARD = 9248 // N_CHIPS


def _w_in_chip_major(gm, gd):
    src = {"main": gm, "dt": gd}
    blocks = []
    for j in range(N_CHIPS):
        lo, hi = j * W_IN_SHARD, (j + 1) * W_IN_SHARD
        parts = [src[k][:, o + max(lo, a) - a:o + min(hi, b) - a] for a, b, k, o in W_IN_MAP if max(lo, a) < min(hi, b)]
        blocks.append(jnp.concatenate(parts, axis=1))
    return jnp.stack(blocks)


def _w_in_reassemble(wc):
    def cols(a, b):
        out = []
        for j in range(N_CHIPS):
            lo, hi = max(a, j * W_IN_SHARD), min(b, (j + 1) * W_IN_SHARD)
            if lo < hi:
                out.append(wc[j][:, lo - j * W_IN_SHARD:hi - j * W_IN_SHARD])
        return out

    main = sorted((m for m in W_IN_MAP if m[2] == "main"), key=lambda m: m[3])
    w_main = jnp.concatenate([p for a, b, _, _ in main for p in cols(a, b)], axis=1)
    (a, b, _, _), = [m for m in W_IN_MAP if m[2] == "dt"]
    w_dt = jnp.pad(jnp.concatenate(cols(a, b), axis=1), ((0, 0), (0, HEAD_PAD - (b - a))))
    return w_main, w_dt
GATHER_KIND = {"w_in": "chip", "p_a": "row", "p_b": "row", "w_mix_o": "row", "w_xq": "row", "w_xkv": "col", "w_xo": "row",
               "w_ffn_in": "col", "w_ffn_out": "row", "conv_w": "chip", "ln_g": "chip", "ln_b": "chip"}
GRAD_VIEW = {n: ("col" if k == "col" else "chip") for n, k in GATHER_KIND.items() if n in BIG}


def _shard_shape(name):
    axis, (r, c) = BIG[name]
    return (r // N_CHIPS, c) if axis == 0 else (r, c // N_CHIPS)


def _pad_rows(flat, cols, row_mult):
    n = flat.shape[0]
    rows = -(-n // cols)
    rows = -(-rows // row_mult) * row_mult
    return jnp.pad(flat, (0, rows * cols - n)).reshape(rows, cols)


def _gather_small_params(a, chip):
    names = list(SMALL_SHARDED)
    kinds = [GATHER_KIND[n] for n in names]
    bufs = [_cast_place(a[n], GATHER_KIND[n], F32, chip.reshape(1), name=f"place_{n}") for n in names]
    outs = _gather_params(bufs, [a[n].shape[1:] for n in names], kinds, name="gather_small_params")
    full = {}
    for n, o in zip(names, outs):
        _, _, r, c = o.shape
        full[n] = jnp.transpose(o, (0, 2, 1, 3)).reshape(DEPTH, r, N_CHIPS * c)
    return full


GATHER_GROUPS = (("w_in",), tuple(n for n in BIG if n != "w_in"))


def _gather_group_start(a, l, names, chip, after, *, tag):
    bufs = [_cast_place_layer(a[n], l, GATHER_KIND[n], chip.reshape(1), after, name=f"place_{n}_l{l}") for n in names]
    return _gather_start(bufs, [a[n].shape[1:] for n in names], [GATHER_KIND[n] for n in names], name=f"gather_start_{tag}")


def _gather_group_finish(a, names, flight, after, *, tag):
    send_sems, recv_sems, bufs, token = flight
    shapes, kinds = [a[n].shape[1:] for n in names], [GATHER_KIND[n] for n in names]
    bufs = _gather_wait(send_sems, recv_sems, bufs, shapes, kinds, token if after is None else after, name=f"gather_wait_{tag}")
    full = dict(zip(names, _gather_forward(bufs, shapes, kinds, name=f"gather_forward_{tag}")))
    if "w_in" in full:
        full["w_main"], full["w_dt"] = _w_in_reassemble(full.pop("w_in"))
    return full


def _layer_weights(a, big, small, l):
    w = dict(big)
    for n in SMALL_SHARDED:
        w[n] = small[n][l]
    for n in ["sg_ln_g", "sg_ln_b", "sg_w", "conv_b", "ssm_norm_g"]:
        w[n] = a[n][l]
    w["sg_bcol"] = a["sg_b"][l][..., None]
    for n in ["dt_bias", "a_log"]:
        w[n + "8"] = _pad_heads(a[n][l])
    w["d_skipx"] = _expand_heads(a["d_skip"][l])
    return w


def _grad_views(grads, names):
    gs = []
    for n in names:
        axis, _ = BIG[n]
        r, c = _shard_shape(n)
        if n == "w_in":
            gs.append(_w_in_chip_major(grads["w_main"], grads["w_dt"]))
        elif axis == 0:
            gs.append(grads[n].reshape(N_CHIPS, r, c))
        else:
            gs.append(grads[n])
    return gs


class _GradExchange:
    def __init__(self, grads, names, c_idx, tag):
        self.names, self.c_idx, self.tag = names, c_idx, tag
        self.views = [GRAD_VIEW[n] for n in names]
        self.gs = _grad_views(grads, names)

    def start(self):
        self.sems = _grads_to_sibling_start(self.gs, self.views, name=f"grads_to_sibling_start_{self.tag}")
        return self.sems[4]

    def cross(self, after):
        send_sems, recv_sems, gs, lands, token = self.sems
        self.gs, self.recv = _grads_to_sibling_wait(send_sems, recv_sems, gs, lands, self.views, token if after is None else after,
                                                    name=f"grads_to_sibling_wait_{self.tag}")
        cpre = self.c_idx.reshape(1)
        pairs = [_pair_sum(g, rv, v, cpre, name=f"grads_pair_sum_{n}_{self.tag}")
                 for g, rv, v, n in zip(self.gs, self.recv, self.views, self.names)]
        self.sems = _grads_to_chips_start(pairs, self.views, name=f"grads_to_chips_start_{self.tag}")
        return self.sems[4]

    def finish(self, after):
        send_sems, recv_sems, pairs, lands, _ = self.sems
        quads = _grads_to_chips_wait(send_sems, recv_sems, pairs, lands, self.views, after, name=f"grads_to_chips_wait_{self.tag}")
        return {n: (g, rv, q) for n, g, rv, q in zip(self.names, self.gs, self.recv, quads)}


def _finish_big_grads(parts, c_idx, chip):
    tots = [_quad_sum([parts[l][n][0] for l in range(DEPTH)], [parts[l][n][1] for l in range(DEPTH)],
                      [parts[l][n][2] for l in range(DEPTH)], GRAD_VIEW[n], chip.reshape(1), c_idx.reshape(1),
                      name=f"grads_chip_sum_{n}") for n in BIG]
    others = _grads_share(tots, name="grads_share")
    return {n: (t, o) for n, t, o in zip(BIG, tots, others)}


def _direct_copies(x_ref, land_ref, send_sems, recv_sems):
    mx, my, mc = lax.axis_index("x"), lax.axis_index("y"), lax.axis_index("c")
    me = 4 * mx + 2 * my + mc
    copies = []
    for k in range(N_DEV - 1):
        f = k + 1
        to = (mx ^ (f >> 2 & 1), my ^ (f >> 1 & 1), mc ^ (f & 1))
        copies.append(pltpu.make_async_remote_copy(src_ref=x_ref, dst_ref=land_ref.at[me], send_sem=send_sems.at[k],
                                                   recv_sem=recv_sems.at[k], device_id=to, device_id_type=_MESH))
    return copies


def _all_gather8_start(x, *, name):
    land = pltpu.with_memory_space_constraint(lax.empty((N_DEV,) + x.shape, x.dtype), pltpu.HBM)

    def body(x_ref, land_ref, send_sems, recv_sems, x_out, land_out, token):
        for cp in _direct_copies(x_ref, land_ref, send_sems, recv_sems):
            cp.start()
        token[...] = jnp.zeros_like(token)

    n = N_DEV - 1
    return pl.pallas_call(
        body, name=name,
        out_shape=(pltpu.SemaphoreType.DMA((n,)), pltpu.SemaphoreType.DMA((n,)), pltpu.HBM(x.shape, x.dtype),
                   pltpu.HBM(land.shape, land.dtype), jax.ShapeDtypeStruct((SUBLANE, LANE), F32)),
        in_specs=[_HBM, _HBM], out_specs=(_SEM, _SEM, _HBM, _HBM, pl.BlockSpec(memory_space=pltpu.VMEM)),
        input_output_aliases={0: 2, 1: 3}, compiler_params=pltpu.CompilerParams(has_side_effects=_EFFECT),
    )(pltpu.with_memory_space_constraint(x, pltpu.HBM), land)


def _all_gather8_wait(send_sems, recv_sems, x, land, after, *, name):
    def body(x_ref, land_ref, s_sems, r_sems, _, x_out, land_out):
        for cp in _direct_copies(x_ref, land_ref, s_sems, r_sems):
            cp.wait_send()
            cp.wait_recv()

    return pl.pallas_call(
        body, name=name, out_shape=(pltpu.HBM(x.shape, x.dtype), pltpu.HBM(land.shape, land.dtype)),
        in_specs=[_HBM, _HBM, _SEM, _SEM, _ANY], out_specs=(_HBM, _HBM), input_output_aliases={0: 0, 1: 1},
        compiler_params=pltpu.CompilerParams(has_side_effects=_EFFECT),
    )(x, land, send_sems, recv_sems, after)


def _pack_small(small):
    return _pad_rows(jnp.concatenate([small[n].reshape(-1) for n in small]), LANE, SUBLANE)


def _unpack_small(small, g8, packed, chip, c_idx, *, name):
    names = list(small)
    tot = _sum_devices(g8, packed, (2 * chip + c_idx).reshape(1), name=name).reshape(-1)
    out, off = {}, 0
    for n in names:
        sz = small[n].size
        full = tot[off:off + sz].reshape(small[n].shape)
        off += sz
        if n in SMALL_SHARDED:
            cs = SMALL_SHARDED[n][1] // N_CHIPS
            full = lax.dynamic_slice_in_dim(full, chip * cs, cs, axis=-1)
        out[n] = full
    return out


def kernel(x, mem, mem_ln_g, mem_ln_b, w_in, sg_ln_g, sg_ln_b, sg_w, sg_b, conv_w, conv_b, dt_bias, a_log, d_skip, ssm_norm_g, p_a, p_b, w_mix_o, w_xq, w_xkv, w_xo, w_ffn_in, w_ffn_out, ln_g, ln_b, loss_target, m_mem_ln_g, m_mem_ln_b, m_w_in, m_sg_ln_g, m_sg_ln_b, m_sg_w, m_sg_b, m_conv_w, m_conv_b, m_dt_bias, m_a_log, m_d_skip, m_ssm_norm_g, m_p_a, m_p_b, m_w_mix_o, m_w_xq, m_w_xkv, m_w_xo, m_w_ffn_in, m_w_ffn_out, m_ln_g, m_ln_b, v_mem_ln_g, v_mem_ln_b, v_w_in, v_sg_ln_g, v_sg_ln_b, v_sg_w, v_sg_b, v_conv_w, v_conv_b, v_dt_bias, v_a_log, v_d_skip, v_ssm_norm_g, v_p_a, v_p_b, v_w_mix_o, v_w_xq, v_w_xkv, v_w_xo, v_w_ffn_in, v_w_ffn_out, v_ln_g, v_ln_b):
    a = dict(zip(ARG_NAMES, (x, mem, mem_ln_g, mem_ln_b, w_in, sg_ln_g, sg_ln_b, sg_w, sg_b, conv_w, conv_b, dt_bias, a_log, d_skip, ssm_norm_g, p_a, p_b, w_mix_o, w_xq, w_xkv, w_xo, w_ffn_in, w_ffn_out, ln_g, ln_b, loss_target, m_mem_ln_g, m_mem_ln_b, m_w_in, m_sg_ln_g, m_sg_ln_b, m_sg_w, m_sg_b, m_conv_w, m_conv_b, m_dt_bias, m_a_log, m_d_skip, m_ssm_norm_g, m_p_a, m_p_b, m_w_mix_o, m_w_xq, m_w_xkv, m_w_xo, m_w_ffn_in, m_w_ffn_out, m_ln_g, m_ln_b, v_mem_ln_g, v_mem_ln_b, v_w_in, v_sg_ln_g, v_sg_ln_b, v_sg_w, v_sg_b, v_conv_w, v_conv_b, v_dt_bias, v_a_log, v_d_skip, v_ssm_norm_g, v_p_a, v_p_b, v_w_mix_o, v_w_xq, v_w_xkv, v_w_xo, v_w_ffn_in, v_w_ffn_out, v_ln_g, v_ln_b)))
    c_idx = lax.axis_index("c").astype(jnp.int32)
    chip = (2 * lax.axis_index("x") + lax.axis_index("y")).astype(jnp.int32)

    small = _gather_small_params(a, chip)
    ga, gb = GATHER_GROUPS
    flights = {(0, 0): _gather_group_start(a, 0, ga, chip, small["ln_b"], tag="l0_a")}
    flights[0, 1] = _gather_group_start(a, 0, gb, chip, flights[0, 0][3], tag="l0_b")

    def layer_weights(after, l):
        first = _gather_group_finish(a, ga, flights[l, 0], after if l else flights[l, 1][3], tag=f"l{l}_a")

        def rest(w, after_b):
            more = _gather_group_finish(a, gb, flights[l, 1], after_b, tag=f"l{l}_b")
            if l + 1 < DEPTH:
                flights[l + 1, 0] = _gather_group_start(a, l + 1, ga, chip, more["p_a"], tag=f"l{l + 1}_a")
                flights[l + 1, 1] = _gather_group_start(a, l + 1, gb, chip, flights[l + 1, 0][3], tag=f"l{l + 1}_b")
                more["p_a"] = more["p_a"] + flights[l + 1, 1][3][0, 0].astype(MXU_DTYPE)
            return {k: v for k, v in {**w, **more}.items() if k != "rest"}

        return dict(_layer_weights(a, first, small, l), rest=rest)

    layers = [functools.partial(layer_weights, l=l) for l in range(DEPTH)]
    exchanges, seen, small_flight = [], {}, {}

    def start_exchange(l, names, grads_l):
        ex = _GradExchange(grads_l, names, c_idx, f"l{l}_{names[0]}")
        tokens = [ex.start()]
        if exchanges:
            tokens.append(exchanges[-1][1].cross(tokens[0]))
        exchanges.append((l, ex))
        seen[l] = grads_l
        if l == 0 and names == GRAD_GROUPS[-1]:
            tokens.append(ex.cross(None))
            small = {}
            for n in SMALL:
                if n.startswith("mem_ln"):
                    continue
                per_layer = []
                for k in range(DEPTH):
                    g = seen[k][n]
                    if n in ("dt_bias", "a_log", "d_skip"):
                        g = g[0, :SSM_HEADS]
                    per_layer.append(g.reshape(a[n].shape[1:-1] + (-1,)))
                small[n] = jnp.stack(per_layer)
            small_flight["small"] = small
            small_flight["sems"] = _all_gather8_start(_pack_small(small), name="gather_small_grads_start")
            tokens.append(small_flight["sems"][4])
        return sum(tokens[1:], tokens[0])

    lsum, grad_x, grads, d_mem_g, d_mem_b = _local_step(x, mem, loss_target, mem_ln_g, mem_ln_b, layers, start_exchange)
    loss = lax.psum(0.5 * jnp.sum(lsum) / D_MODEL, ("x", "y", "c"))

    parts = [{} for _ in range(DEPTH)]
    for l, ex in exchanges:
        parts[l].update(ex.finish(grad_x))
    halves = _finish_big_grads(parts, c_idx, chip)
    gw = {}
    send_sems, recv_sems, packed, land, _ = small_flight["sems"]
    packed, g8 = _all_gather8_wait(send_sems, recv_sems, packed, land, grad_x, name="gather_small_grads_wait")
    gw.update(_unpack_small(small_flight["small"], g8, packed, chip, c_idx, name="small_grads_sum"))
    mem_small = {"mem_ln_g": d_mem_g, "mem_ln_b": d_mem_b}
    mem_packed = _pack_small(mem_small)
    gw.update(_unpack_small(mem_small, _all_gather8(mem_packed, name="gather_mem_ln_grads"), mem_packed, chip, c_idx,
                            name="mem_ln_grads_sum"))

    delta, new_m, new_v = {}, {}, {}
    for n in BIG:
        mine, other = halves[n]
        gw[n], delta[n], new_m[n], new_v[n] = _adamw_halves(a[n], a["m_" + n], a["v_" + n], mine, other, c_idx.reshape(1),
                                                             name=f"adamw_{n}")
    for n in SMALL:
        shp = a[n].shape
        view = (-1, LANE) if a[n].size % LANE == 0 else (1, -1)
        outs = _adamw(*[v.reshape(view) for v in (a[n], gw[n], a["m_" + n], a["v_" + n])], name=f"adamw_{n}")
        delta[n], new_m[n], new_v[n] = (o.reshape(shp) for o in outs)
    return (loss, grad_x, *[gw[n].reshape(a[n].shape) for n in WEIGHTS], *[delta[n] for n in WEIGHTS],
            *[new_m[n] for n in WEIGHTS], *[new_v[n] for n in WEIGHTS])
```

```python
import functools
import math

import jax
import jax.numpy as jnp
from jax import lax
from jax.experimental import pallas as pl
from jax.experimental.pallas import tpu as pltpu

F32 = jnp.float32
MXU_DTYPE = jnp.bfloat16
WIRE_DTYPE = jnp.bfloat16
STASH_DTYPE = jnp.bfloat16

D_MODEL = 1024
DEPTH = 2
CHUNK = 128
SG_GROUPS = 8
SSM_INNER = 2048
SSM_HEADDIM = 64
SSM_HEADS = 32
SSM_STATE = 128
SSM_GROUPS = 4
SSM_CONV = 4
SSM_CONV_DIM = 3072
X_HEADS = 4
X_HEADDIM = 256
FFN_HIDDEN = 2816
ALPHA = float((2 * DEPTH) ** 0.25)
LN_EPS = 1e-5
RMS_EPS = 1e-5
ADAM_LR = 0.001
ADAM_B1 = 0.9
ADAM_B2 = 0.999
ADAM_EPS = 1e-08
ADAM_WD = 0.01
ADAM_STEP = 10

MAIN_COLS = 9216
UVZ_COLS = 4096
GAB_COL0 = 4096
XBC_COL0 = 6144
HEAD_PAD = 128

VMEM_LIMIT = 56 * 1024 * 1024
BLOCK_BYTES = 2 * 1024 * 1024
ROW_TILES = (1024, 512, 256, 128)
WIDE_ROW_TILES = (512, 256, 128)
LANE = 128
SUBLANE = 8

N_CHIPS = 4
N_DEV = 8


def _pick(n, cands):
    for c in cands:
        if n % c == 0:
            return c
    return n


MM_TILE_MAX = 1536
MM_OPERAND_BYTES = 12 * 1024 * 1024


def _div_tile(n, limit):
    best = None
    for t in range(LANE, min(n, limit) + 1, LANE):
        if n % t == 0:
            best = t
    return n if best is None else best


def _params(*sem):
    return pltpu.CompilerParams(dimension_semantics=tuple(sem), vmem_limit_bytes=VMEM_LIMIT)


_ANY = pl.BlockSpec(memory_space=pl.ANY)
_MESH = pl.DeviceIdType.MESH


def _nt(a, b):
    return lax.dot_general(a, b, (((1,), (1,)), ((), ())), preferred_element_type=F32)


def _tn(a, b):
    return lax.dot_general(a, b, (((0,), (0,)), ((), ())), preferred_element_type=F32)


def _nn(a, b):
    return jnp.dot(a, b, preferred_element_type=F32)


def _sigmoid(x):
    return 0.5 * jnp.tanh(0.5 * x) + 0.5


def _split3(v):
    def top(x):
        bits = lax.bitcast_convert_type(x, jnp.uint32) & jnp.uint32(0xFFFF0000)
        return lax.bitcast_convert_type(bits, F32)

    v1 = top(v)
    r1 = v - v1
    v2 = top(r1)
    v3 = r1 - v2
    return v1.astype(jnp.bfloat16), v2.astype(jnp.bfloat16), v3.astype(jnp.bfloat16)


def _dot_exact(a, b, dn, data):
    if data == 0:
        mat = b.astype(jnp.bfloat16)
        return sum(lax.dot_general(p, mat, dn, preferred_element_type=F32) for p in _split3(a))
    mat = a.astype(jnp.bfloat16)
    return sum(lax.dot_general(mat, p, dn, preferred_element_type=F32) for p in _split3(b))


_DN_NN = (((1,), (0,)), ((), ()))
_DN_TN = (((0,), (0,)), ((), ()))


def _gelu(x):
    return 0.5 * x * (1.0 + lax.erf(x * (2.0 ** -0.5)))


def _gelu_grad(x):
    return 0.5 * (1.0 + lax.erf(x * (2.0 ** -0.5))) + x * jnp.exp(-0.5 * x * x) * (1.0 / math.sqrt(2.0 * math.pi))


def _mm(a, b, *, ta=False, tb=False, out_dtype=F32, after=None, name):
    if ta:
        kdim, m = a.shape
    else:
        m, kdim = a.shape
    if tb:
        n, k2 = b.shape[-2:]
    else:
        k2, n = b.shape[-2:]
    assert kdim == k2, (a.shape, b.shape, ta, tb)
    tm = _div_tile(m, MM_TILE_MAX)
    tn = _div_tile(n, MM_TILE_MAX)
    tk = _div_tile(kdim, MM_OPERAND_BYTES // (tm * a.dtype.itemsize + tn * b.dtype.itemsize))
    nk = kdim // tk
    dn = (((0 if ta else 1,), (1 if tb else 0,)), ((), ()))

    extra = [] if after is None else [after]

    def body(a_ref, b_ref, *rest):
        o_ref = rest[len(extra)]
        d = lax.dot_general(a_ref[...].astype(MXU_DTYPE), b_ref[...].astype(MXU_DTYPE), dn, preferred_element_type=F32)
        if nk == 1:
            o_ref[...] = d.astype(out_dtype)
            return
        acc_ref = rest[len(extra) + 1]
        k = pl.program_id(2)

        @pl.when(k == 0)
        def _():
            acc_ref[...] = d

        @pl.when(jnp.logical_and(k > 0, k < nk - 1))
        def _():
            acc_ref[...] += d

        @pl.when(k == nk - 1)
        def _():
            o_ref[...] = (acc_ref[...] + d).astype(out_dtype)

    a_spec = pl.BlockSpec((tk, tm), lambda i, j, k: (k, i)) if ta else pl.BlockSpec((tm, tk), lambda i, j, k: (i, k))
    b_spec = pl.BlockSpec((tn, tk), lambda i, j, k: (j, k)) if tb else pl.BlockSpec((tk, tn), lambda i, j, k: (k, j))
    return pl.pallas_call(
        body, grid=(m // tm, n // tn, nk), in_specs=[a_spec, b_spec] + [_ANY] * len(extra),
        out_specs=pl.BlockSpec((tm, tn), lambda i, j, k: (i, j)),
        out_shape=jax.ShapeDtypeStruct((m, n), out_dtype),
        scratch_shapes=[pltpu.VMEM((tm, tn), F32)] if nk > 1 else [],
        compiler_params=_params("parallel", "parallel", "arbitrary"), name=name)(a, b, *extra)


def _row_spec(tm, c, col=0):
    return pl.BlockSpec((tm, c), lambda i: (i, col))


def _par_spec(shape):
    nd = len(shape)
    return pl.BlockSpec(shape, lambda i: (0,) * nd)


def _ln_fwd(x, f, g, b, *, name):
    t, c = x.shape
    tm = _pick(t, ROW_TILES)
    has_f = f is not None

    def body(*refs):
        if has_f:
            x_ref, f_ref, g_ref, b_ref, y_ref, yb_ref, xh_ref, rs_ref = refs
            r = ALPHA * x_ref[...] + f_ref[...]
        else:
            x_ref, g_ref, b_ref, y_ref, yb_ref, xh_ref, rs_ref = refs
            r = x_ref[...]
        mu = jnp.mean(r, axis=-1, keepdims=True)
        xc = r - mu
        var = jnp.mean(xc * xc, axis=-1, keepdims=True)
        rstd = lax.rsqrt(var + LN_EPS)
        xh = xc * rstd
        y = xh * g_ref[...] + b_ref[...]
        y_ref[...] = y
        yb_ref[...] = y.astype(MXU_DTYPE)
        xh_ref[...] = xh
        rs_ref[...] = jnp.broadcast_to(rstd, rs_ref.shape)

    ins = [x] + ([f] if has_f else []) + [g.reshape(1, c), b.reshape(1, c)]
    in_specs = [_row_spec(tm, c)] * (2 if has_f else 1) + [_par_spec((1, c))] * 2
    return pl.pallas_call(
        body, grid=(t // tm,), in_specs=in_specs,
        out_specs=[_row_spec(tm, c), _row_spec(tm, c), _row_spec(tm, c), _row_spec(tm, LANE)],
        out_shape=[jax.ShapeDtypeStruct((t, c), F32), jax.ShapeDtypeStruct((t, c), MXU_DTYPE),
                   jax.ShapeDtypeStruct((t, c), F32), jax.ShapeDtypeStruct((t, LANE), F32)],
        compiler_params=_params("parallel"), name=name)(*ins)


def _ln_bwd(addends, scales, xh, rs, g, *, name):
    t, c = xh.shape
    tm = _pick(t, ROW_TILES)
    na = len(addends)

    def body(*refs):
        a_refs = refs[:na]
        xh_ref, rs_ref, g_ref, dp_ref, dpb_ref, dg_ref, db_ref = refs[na:]

        @pl.when(pl.program_id(0) == 0)
        def _():
            dg_ref[...] = jnp.zeros_like(dg_ref)
            db_ref[...] = jnp.zeros_like(db_ref)

        dy = None
        for s, r in zip(scales, a_refs):
            term = r[...] if s == 1.0 else s * r[...]
            dy = term if dy is None else dy + term
        xhv = xh_ref[...]
        dxh = dy * g_ref[...]
        m1 = jnp.mean(dxh, axis=-1, keepdims=True)
        m2 = jnp.mean(dxh * xhv, axis=-1, keepdims=True)
        dp = rs_ref[:, 0:1] * (dxh - m1 - xhv * m2)
        dp_ref[...] = dp
        dpb_ref[...] = dp.astype(MXU_DTYPE)
        dg_ref[...] += jnp.sum(dy * xhv, axis=0, keepdims=True)
        db_ref[...] += jnp.sum(dy, axis=0, keepdims=True)

    in_specs = [_row_spec(tm, c)] * (na + 1) + [_row_spec(tm, LANE), _par_spec((1, c))]
    return pl.pallas_call(
        body, grid=(t // tm,), in_specs=in_specs,
        out_specs=[_row_spec(tm, c), _row_spec(tm, c), _par_spec((1, c)), _par_spec((1, c))],
        out_shape=[jax.ShapeDtypeStruct((t, c), F32), jax.ShapeDtypeStruct((t, c), MXU_DTYPE),
                   jax.ShapeDtypeStruct((1, c), F32), jax.ShapeDtypeStruct((1, c), F32)],
        compiler_params=_params("arbitrary"), name=name)(*addends, xh, rs, g.reshape(1, c))


def _add_scaled(addends, scales, *, name):
    t, c = addends[0].shape
    tm = _pick(t, ROW_TILES)
    na = len(addends)

    def body(*refs):
        acc = None
        for s, r in zip(scales, refs[:na]):
            term = r[...] if s == 1.0 else s * r[...]
            acc = term if acc is None else acc + term
        refs[na][...] = acc

    return pl.pallas_call(
        body, grid=(t // tm,), in_specs=[_row_spec(tm, c)] * na, out_specs=_row_spec(tm, c),
        out_shape=jax.ShapeDtypeStruct((t, c), F32), compiler_params=_params("parallel"), name=name)(*addends)


def _loss_head(y, tgt, *, name):
    t, c = y.shape
    tm = _pick(t, ROW_TILES)

    def body(y_ref, t_ref, dy_ref, ls_ref):
        @pl.when(pl.program_id(0) == 0)
        def _():
            ls_ref[...] = jnp.zeros_like(ls_ref)

        e = y_ref[...] - t_ref[...]
        dy_ref[...] = e * (1.0 / c)
        ls_ref[...] += jnp.sum(e * e, axis=0, keepdims=True)

    return pl.pallas_call(
        body, grid=(t // tm,), in_specs=[_row_spec(tm, c)] * 2,
        out_specs=[_row_spec(tm, c), _par_spec((1, c))],
        out_shape=[jax.ShapeDtypeStruct((t, c), F32), jax.ShapeDtypeStruct((1, c), F32)],
        compiler_params=_params("arbitrary"), name=name)(y, tgt)


def _swiglu_fwd(h, *, name):
    t, two_f = h.shape
    fh = two_f // 2
    tm = _pick(t, WIDE_ROW_TILES)

    def body(g_ref, u_ref, a_ref):
        g = g_ref[...].astype(F32)
        a_ref[...] = (g * _sigmoid(g) * u_ref[...].astype(F32)).astype(MXU_DTYPE)

    return pl.pallas_call(
        body, grid=(t // tm,), in_specs=[_row_spec(tm, fh, 0), _row_spec(tm, fh, 1)], out_specs=_row_spec(tm, fh),
        out_shape=jax.ShapeDtypeStruct((t, fh), MXU_DTYPE), compiler_params=_params("parallel"), name=name)(h, h)


def _swiglu_bwd(h, da, *, name):
    t, two_f = h.shape
    fh = two_f // 2
    tm = _pick(t, WIDE_ROW_TILES)

    def body(g_ref, u_ref, da_ref, dh_ref):
        g = g_ref[...].astype(F32)
        s = _sigmoid(g)
        dav = da_ref[...].astype(F32)
        dh_ref[:, :fh] = (dav * u_ref[...].astype(F32) * (s * (1.0 + g * (1.0 - s)))).astype(MXU_DTYPE)
        dh_ref[:, fh:] = (dav * g * s).astype(MXU_DTYPE)

    return pl.pallas_call(
        body, grid=(t // tm,), in_specs=[_row_spec(tm, fh, 0), _row_spec(tm, fh, 1), _row_spec(tm, fh)],
        out_specs=_row_spec(tm, two_f), out_shape=jax.ShapeDtypeStruct((t, two_f), MXU_DTYPE),
        compiler_params=_params("parallel"), name=name)(h, h, da)


def _attn_probs(q, k):
    s = _nt(q, k) * (X_HEADDIM ** -0.5)
    s = s - jnp.max(s, axis=-1, keepdims=True)
    p = jnp.exp(s)
    return p / jnp.sum(p, axis=-1, keepdims=True)


def _attn_fwd(q, kv, *, bsz, name):
    t = q.shape[0]
    s = t // bsz
    ml = kv.shape[0] // bsz
    hd = X_HEADDIM

    def body(q_ref, k_ref, v_ref, o_ref):
        p = _attn_probs(q_ref[...], k_ref[...])
        o_ref[...] = _nn(p.astype(MXU_DTYPE), v_ref[...]).astype(MXU_DTYPE)

    return pl.pallas_call(
        body, grid=(bsz, X_HEADS),
        in_specs=[pl.BlockSpec((s, hd), lambda b, h: (b, h)), pl.BlockSpec((ml, hd), lambda b, h: (b, h)),
                  pl.BlockSpec((ml, hd), lambda b, h: (b, X_HEADS + h))],
        out_specs=pl.BlockSpec((s, hd), lambda b, h: (b, h)),
        out_shape=jax.ShapeDtypeStruct((t, D_MODEL), MXU_DTYPE),
        compiler_params=_params("parallel", "parallel"), name=name)(q, kv, kv)


def _attn_bwd(q, kv, do, *, bsz, name):
    t = q.shape[0]
    s = t // bsz
    ml = kv.shape[0] // bsz
    hd = X_HEADDIM

    def body(q_ref, k_ref, v_ref, do_ref, dq_ref, dk_ref, dv_ref):
        qv, kk, vv, dov = q_ref[...], k_ref[...], v_ref[...], do_ref[...]
        p = _attn_probs(qv, kk)
        dp = _nt(dov, vv)
        dv_ref[...] = _tn(p.astype(MXU_DTYPE), dov).astype(MXU_DTYPE)
        ds = (p * (dp - jnp.sum(dp * p, axis=-1, keepdims=True)) * (X_HEADDIM ** -0.5)).astype(MXU_DTYPE)
        dq_ref[...] = _nn(ds, kk).astype(MXU_DTYPE)
        dk_ref[...] = _tn(ds, qv).astype(MXU_DTYPE)

    blk_q = pl.BlockSpec((s, hd), lambda b, h: (b, h))
    blk_m = pl.BlockSpec((ml, hd), lambda b, h: (b, h))
    return pl.pallas_call(
        body, grid=(bsz, X_HEADS),
        in_specs=[blk_q, blk_m, pl.BlockSpec((ml, hd), lambda b, h: (b, X_HEADS + h)), blk_q],
        out_specs=[blk_q, blk_m, blk_m],
        out_shape=[jax.ShapeDtypeStruct((t, D_MODEL), MXU_DTYPE), jax.ShapeDtypeStruct((bsz * ml, D_MODEL), MXU_DTYPE),
                   jax.ShapeDtypeStruct((bsz * ml, D_MODEL), MXU_DTYPE)],
        compiler_params=_params("parallel", "parallel"), name=name)(q, kv, kv, do)


def _causal(n):
    row = lax.broadcasted_iota(jnp.int32, (n, n), 0)
    col = lax.broadcasted_iota(jnp.int32, (n, n), 1)
    return row >= col


def _sg_norm(v, g, b):
    gv = _gelu(v)
    mu = jnp.mean(gv, axis=-1, keepdims=True)
    xc = gv - mu
    var = jnp.mean(xc * xc, axis=-1, keepdims=True)
    rstd = lax.rsqrt(var + LN_EPS)
    xh = xc * rstd
    return xh, rstd, xh * g + b


def _sg_fwd(proj, ln_g, ln_b, w, bcol, *, name):
    t = proj.shape[0]
    c = D_MODEL
    gd = c // SG_GROUPS

    def body(u_ref, v_ref, g_ref, b_ref, w_ref, bc_ref, o_ref):
        gu = _gelu(u_ref[...].astype(F32))
        _, _, vn = _sg_norm(v_ref[...].astype(F32), g_ref[...], b_ref[...])
        mask = _causal(CHUNK)
        for g in range(SG_GROUPS):
            sl = slice(g * gd, (g + 1) * gd)
            wg = jnp.where(mask, w_ref[g], 0.0).astype(MXU_DTYPE)
            mixed = _nn(wg, vn[:, sl].astype(MXU_DTYPE)) + bc_ref[g]
            o_ref[:, sl] = (gu[:, sl] * mixed).astype(MXU_DTYPE)

    return pl.pallas_call(
        body, grid=(t // CHUNK,),
        in_specs=[_row_spec(CHUNK, c, 0), _row_spec(CHUNK, c, 1), _par_spec((1, c)), _par_spec((1, c)),
                  _par_spec((SG_GROUPS, CHUNK, CHUNK)), _par_spec((SG_GROUPS, CHUNK, 1))],
        out_specs=_row_spec(CHUNK, c), out_shape=jax.ShapeDtypeStruct((t, c), MXU_DTYPE),
        compiler_params=_params("parallel"), name=name)(proj, proj, ln_g.reshape(1, c), ln_b.reshape(1, c), w, bcol)


def _sg_bwd(proj, dsgo, ln_g, ln_b, w, bcol, dproj, *, name):
    t = proj.shape[0]
    c = D_MODEL
    gd = c // SG_GROUPS

    def body(u_ref, v_ref, d_ref, g_ref, b_ref, w_ref, bc_ref, _, duv_ref, dw_ref, dbc_ref, dg_ref, db_ref, dvn_ref):
        @pl.when(pl.program_id(0) == 0)
        def _():
            dw_ref[...] = jnp.zeros_like(dw_ref)
            dbc_ref[...] = jnp.zeros_like(dbc_ref)
            dg_ref[...] = jnp.zeros_like(dg_ref)
            db_ref[...] = jnp.zeros_like(db_ref)

        u = u_ref[...].astype(F32)
        v = v_ref[...].astype(F32)
        dso = d_ref[...].astype(F32)
        gu = _gelu(u)
        xh, rstd, vn = _sg_norm(v, g_ref[...], b_ref[...])
        mask = _causal(CHUNK)
        for g in range(SG_GROUPS):
            sl = slice(g * gd, (g + 1) * gd)
            wg = jnp.where(mask, w_ref[g], 0.0).astype(MXU_DTYPE)
            vng = vn[:, sl].astype(MXU_DTYPE)
            mixed = _nn(wg, vng) + bc_ref[g]
            duv_ref[:, sl] = (dso[:, sl] * mixed * _gelu_grad(u[:, sl])).astype(MXU_DTYPE)
            dmix = dso[:, sl] * gu[:, sl]
            dmb = dmix.astype(MXU_DTYPE)
            dbc_ref[g] += jnp.sum(dmix, axis=-1, keepdims=True)
            dw_ref[g] += jnp.where(mask, _nt(dmb, vng), 0.0)
            dvn_ref[:, sl] = _tn(wg, dmb)
        dvn = dvn_ref[...]
        dg_ref[...] += jnp.sum(dvn * xh, axis=0, keepdims=True)
        db_ref[...] += jnp.sum(dvn, axis=0, keepdims=True)
        dxh = dvn * g_ref[...]
        m1 = jnp.mean(dxh, axis=-1, keepdims=True)
        m2 = jnp.mean(dxh * xh, axis=-1, keepdims=True)
        dgv = rstd * (dxh - m1 - xh * m2)
        duv_ref[:, c:] = (dgv * _gelu_grad(v)).astype(MXU_DTYPE)

    return pl.pallas_call(
        body, grid=(t // CHUNK,),
        in_specs=[_row_spec(CHUNK, c, 0), _row_spec(CHUNK, c, 1), _row_spec(CHUNK, c), _par_spec((1, c)),
                  _par_spec((1, c)), _par_spec((SG_GROUPS, CHUNK, CHUNK)), _par_spec((SG_GROUPS, CHUNK, 1)), _ANY],
        out_specs=[_row_spec(CHUNK, 2 * c), _par_spec((SG_GROUPS, CHUNK, CHUNK)), _par_spec((SG_GROUPS, CHUNK, 1)),
                   _par_spec((1, c)), _par_spec((1, c))],
        out_shape=[jax.ShapeDtypeStruct(dproj.shape, dproj.dtype), jax.ShapeDtypeStruct((SG_GROUPS, CHUNK, CHUNK), F32),
                   jax.ShapeDtypeStruct((SG_GROUPS, CHUNK, 1), F32), jax.ShapeDtypeStruct((1, c), F32),
                   jax.ShapeDtypeStruct((1, c), F32)],
        scratch_shapes=[pltpu.VMEM((CHUNK, c), F32)], input_output_aliases={7: 0},
        compiler_params=_params("arbitrary"), name=name)(proj, proj, dsgo, ln_g.reshape(1, c), ln_b.reshape(1, c), w, bcol, dproj)


CONV_TC = 256


def _conv_taps(x):
    rows = lax.broadcasted_iota(jnp.int32, x.shape, 0)
    taps = [jnp.where(rows >= SSM_CONV - 1 - k, pltpu.roll(x, SSM_CONV - 1 - k, axis=0), 0.0) for k in range(SSM_CONV - 1)]
    return taps + [x]


def _conv_pre(taps, w_ref, b_ref):
    acc = b_ref[...]
    for k in range(SSM_CONV):
        acc = acc + taps[k] * w_ref[k:k + 1, :]
    return acc


def _conv_fwd(proj, w, b, *, bsz, name):
    t = proj.shape[0]
    s = t // bsz
    nj = SSM_CONV_DIM // CONV_TC
    c0 = XBC_COL0 // CONV_TC

    def body(x_ref, w_ref, b_ref, o_ref):
        pre = _conv_pre(_conv_taps(x_ref[...].astype(F32)), w_ref, b_ref)
        o_ref[...] = (pre * _sigmoid(pre)).astype(o_ref.dtype)

    return pl.pallas_call(
        body, grid=(bsz, nj),
        in_specs=[pl.BlockSpec((s, CONV_TC), lambda bb, j: (bb, c0 + j)), pl.BlockSpec((SSM_CONV, CONV_TC), lambda bb, j: (0, j)),
                  pl.BlockSpec((1, CONV_TC), lambda bb, j: (0, j))],
        out_specs=pl.BlockSpec((s, CONV_TC), lambda bb, j: (bb, j)),
        out_shape=jax.ShapeDtypeStruct((t, SSM_CONV_DIM), STASH_DTYPE),
        compiler_params=_params("parallel", "parallel"), name=name)(proj, w, b.reshape(1, -1))


def _conv_bwd(proj, dact, w, b, dproj, *, bsz, name):
    t = proj.shape[0]
    s = t // bsz
    nj = SSM_CONV_DIM // CONV_TC
    c0 = XBC_COL0 // CONV_TC

    def body(x_ref, d_ref, w_ref, b_ref, _, dx_ref, dw_ref, db_ref):
        @pl.when(pl.program_id(1) == 0)
        def _():
            dw_ref[...] = jnp.zeros_like(dw_ref)
            db_ref[...] = jnp.zeros_like(db_ref)

        taps = _conv_taps(x_ref[...].astype(F32))
        pre = _conv_pre(taps, w_ref, b_ref)
        sg = _sigmoid(pre)
        dpre = d_ref[...].astype(F32) * (sg * (1.0 + pre * (1.0 - sg)))
        rows = lax.broadcasted_iota(jnp.int32, dpre.shape, 0)
        db_ref[...] += jnp.sum(dpre, axis=0, keepdims=True)
        dx = dpre * w_ref[SSM_CONV - 1:SSM_CONV, :]
        for k in range(SSM_CONV):
            dw_ref[k:k + 1, :] += jnp.sum(dpre * taps[k], axis=0, keepdims=True)
        for k in range(SSM_CONV - 1):
            sh = SSM_CONV - 1 - k
            dsh = jnp.where(rows < s - sh, pltpu.roll(dpre, s - sh, axis=0), 0.0)
            dx = dx + dsh * w_ref[k:k + 1, :]
        dx_ref[...] = dx.astype(MXU_DTYPE)

    return pl.pallas_call(
        body, grid=(nj, bsz),
        in_specs=[pl.BlockSpec((s, CONV_TC), lambda j, bb: (bb, c0 + j)), pl.BlockSpec((s, CONV_TC), lambda j, bb: (bb, j)),
                  pl.BlockSpec((SSM_CONV, CONV_TC), lambda j, bb: (0, j)), pl.BlockSpec((1, CONV_TC), lambda j, bb: (0, j)), _ANY],
        out_specs=[pl.BlockSpec((s, CONV_TC), lambda j, bb: (bb, c0 + j)), pl.BlockSpec((SSM_CONV, CONV_TC), lambda j, bb: (0, j)),
                   pl.BlockSpec((1, CONV_TC), lambda j, bb: (0, j))],
        out_shape=[jax.ShapeDtypeStruct(dproj.shape, dproj.dtype), jax.ShapeDtypeStruct((SSM_CONV, SSM_CONV_DIM), F32),
                   jax.ShapeDtypeStruct((1, SSM_CONV_DIM), F32)],
        input_output_aliases={4: 0},
        compiler_params=_params("parallel", "arbitrary"), name=name)(proj, dact, w, b.reshape(1, -1), dproj)


def _softplus(x):
    return jnp.maximum(x, 0.0) + jnp.log1p(jnp.exp(-jnp.abs(x)))


def _pad_heads(v):
    return jnp.broadcast_to(jnp.pad(v.astype(F32), (0, HEAD_PAD - SSM_HEADS))[None, :], (SUBLANE, HEAD_PAD))


def _ssd_prep(dt_raw, dt_bias8, a_log8, *, name):
    t = dt_raw.shape[0]
    n = CHUNK

    def body(r_ref, b_ref, al_ref, dt_ref, cs_ref, dtt_ref, cst_ref):
        dt = _softplus(r_ref[...] + b_ref[0:1, :])
        da = dt * (-jnp.exp(al_ref[0:1, :]))
        row = lax.broadcasted_iota(jnp.int32, (n, n), 0)
        col = lax.broadcasted_iota(jnp.int32, (n, n), 1)
        lower = (col <= row).astype(F32)
        upper = (row <= col).astype(F32)
        eye = (row == col).astype(F32)
        dt_ref[...] = dt
        cs_ref[...] = _dot_exact(lower, da, _DN_NN, 1)
        cst_ref[0] = _dot_exact(da, upper, _DN_TN, 0)
        dtt_ref[0] = _dot_exact(dt, eye, _DN_TN, 0)

    hp = HEAD_PAD
    return pl.pallas_call(
        body, grid=(t // n,),
        in_specs=[_row_spec(n, hp), _par_spec((SUBLANE, hp)), _par_spec((SUBLANE, hp))],
        out_specs=[_row_spec(n, hp), _row_spec(n, hp), pl.BlockSpec((1, hp, n), lambda i: (i, 0, 0)),
                   pl.BlockSpec((1, hp, n), lambda i: (i, 0, 0))],
        out_shape=[jax.ShapeDtypeStruct((t, hp), F32), jax.ShapeDtypeStruct((t, hp), F32),
                   jax.ShapeDtypeStruct((t // n, hp, n), F32), jax.ShapeDtypeStruct((t // n, hp, n), F32)],
        compiler_params=_params("parallel"), name=name)(dt_raw, dt_bias8, a_log8)


def _expand_mat():
    h = lax.broadcasted_iota(jnp.int32, (HEAD_PAD, SSM_INNER), 0)
    ch = lax.broadcasted_iota(jnp.int32, (HEAD_PAD, SSM_INNER), 1)
    return (ch // SSM_HEADDIM == h).astype(F32)


def _reduce_mat():
    ch = lax.broadcasted_iota(jnp.int32, (SSM_INNER, HEAD_PAD), 0)
    h = lax.broadcasted_iota(jnp.int32, (SSM_INNER, HEAD_PAD), 1)
    return (ch // SSM_HEADDIM == h).astype(F32)


def _expand(v, em):
    return _dot_exact(v, em, _DN_NN, 0)


def _expand_heads(v):
    return jnp.repeat(v.astype(F32), SSM_HEADDIM)[None, :]


def _decay_mat(cs_ref, cst_ref, h, mask):
    seg = cs_ref[:, h:h + 1] - cst_ref[0, h:h + 1, :]
    return jnp.where(mask, jnp.exp(jnp.minimum(seg, 0.0)), 0.0)


GROUP_CH = SSM_INNER // SSM_GROUPS
PAIRS_PER_GROUP = GROUP_CH // LANE
HEADS_PER_GROUP = SSM_HEADS // SSM_GROUPS
BM_COL0 = SSM_INNER
CM_COL0 = SSM_INNER + SSM_GROUPS * SSM_STATE


def _ssd_specs(nc, rev):
    def cidx(i):
        return (i // nc) * nc + (nc - 1 - i % nc) if rev else i

    n = CHUNK
    xs = pl.BlockSpec((n, SSM_INNER), lambda i: (cidx(i), 0))
    bm = pl.BlockSpec((n, GROUP_CH), lambda i: (cidx(i), BM_COL0 // GROUP_CH))
    cm = pl.BlockSpec((n, GROUP_CH), lambda i: (cidx(i), CM_COL0 // GROUP_CH))
    hv = pl.BlockSpec((n, HEAD_PAD), lambda i: (cidx(i), 0))
    hvt = pl.BlockSpec((1, HEAD_PAD, n), lambda i: (cidx(i), 0, 0))
    st = pl.BlockSpec((1, SSM_INNER, SSM_STATE), lambda i: (cidx(i), 0, 0))
    return xs, bm, cm, hv, hvt, st


def _ssd_fwd(xbc, dt, cs, dtt, cst, dskx, *, nc, name):
    t = xbc.shape[0]
    n = CHUNK
    xs_s, bm_s, cm_s, hv_s, hvt_s, st_s = _ssd_specs(nc, False)

    def body(xs_ref, bm_ref, cm_ref, dt_ref, cs_ref, dtt_ref, cst_ref, dsk_ref, y_ref, st_ref, prev):
        @pl.when(pl.program_id(0) % nc == 0)
        def _():
            prev[...] = jnp.zeros_like(prev)

        st_ref[0] = prev[...]
        em = _expand_mat()
        dtx = _expand(dt_ref[...], em)
        csx = _expand(cs_ref[...], em)
        dskx = dsk_ref[...]
        xs = xs_ref[...].astype(F32)
        xdt = xs * dtx
        ecs = jnp.exp(csx)
        dec = jnp.exp(csx[n - 1:n, :] - csx)
        mask = _causal(n)
        lane = lax.broadcasted_iota(jnp.int32, (n, LANE), 1)
        for g in range(SSM_GROUPS):
            gs = slice(g * SSM_STATE, (g + 1) * SSM_STATE)
            gc = slice(g * GROUP_CH, (g + 1) * GROUP_CH)
            cmat = cm_ref[:, gs].astype(MXU_DTYPE)
            bmat = bm_ref[:, gs].astype(MXU_DTYPE)
            cb = _nt(cmat, bmat)
            yoff = ecs[:, gc] * _nt(cmat, prev[gc, :].astype(MXU_DTYPE))
            for q in range(PAIRS_PER_GROUP):
                hp = g * PAIRS_PER_GROUP + q
                sl = slice(hp * LANE, (hp + 1) * LANE)
                xp = xdt[:, sl].astype(MXU_DTYPE)
                m0 = (cb * _decay_mat(cs_ref, cst_ref, 2 * hp, mask)).astype(MXU_DTYPE)
                m1 = (cb * _decay_mat(cs_ref, cst_ref, 2 * hp + 1, mask)).astype(MXU_DTYPE)
                yd = jnp.where(lane < SSM_HEADDIM, _nn(m0, xp), _nn(m1, xp))
                y_ref[:, sl] = (yd + yoff[:, q * LANE:(q + 1) * LANE] + xs[:, sl] * dskx[:, sl]).astype(y_ref.dtype)
            snew = _tn((xdt[:, gc] * dec[:, gc]).astype(MXU_DTYPE), bmat)
            for r in range(HEADS_PER_GROUP):
                h = g * HEADS_PER_GROUP + r
                rows = slice(h * SSM_HEADDIM, (h + 1) * SSM_HEADDIM)
                e = jnp.exp(cst_ref[0, h:h + 1, n - 1:n])
                prev[rows, :] = prev[rows, :] * e + snew[r * SSM_HEADDIM:(r + 1) * SSM_HEADDIM, :]

    return pl.pallas_call(
        body, grid=(t // n,),
        in_specs=[xs_s, bm_s, cm_s, hv_s, hv_s, hvt_s, hvt_s, _par_spec((1, SSM_INNER))],
        out_specs=[xs_s, st_s],
        out_shape=[jax.ShapeDtypeStruct((t, SSM_INNER), STASH_DTYPE), jax.ShapeDtypeStruct((t // n, SSM_INNER, SSM_STATE), F32)],
        scratch_shapes=[pltpu.VMEM((SSM_INNER, SSM_STATE), F32)],
        compiler_params=_params("arbitrary"), name=name)(xbc, xbc, xbc, dt, cs, dtt, cst, dskx)


def _ssd_bwd(dy, xbc, dt, cs, dtt, cst, st, dskx, a_log8, dt_raw, dt_bias8, *, nc, name):
    t = xbc.shape[0]
    n = CHUNK
    xs_s, bm_s, cm_s, hv_s, hvt_s, st_s = _ssd_specs(nc, True)
    acc_s = _par_spec((1, HEAD_PAD))
    xbc_s = pl.BlockSpec((n, SSM_CONV_DIM), xs_s.index_map)

    def body(dy_ref, xs_ref, bm_ref, cm_ref, dt_ref, cs_ref, dtt_ref, cst_ref, st_ref, dsk_ref, al_ref, raw_ref, bias_ref,
             dxbc_ref, ddr_ref, dal_ref, dds_ref, dbias_ref, dprev, dxdt_s, tdec_s, tcs_s):
        @pl.when(pl.program_id(0) % nc == 0)
        def _():
            dprev[...] = jnp.zeros_like(dprev)

        @pl.when(pl.program_id(0) == 0)
        def _():
            dal_ref[...] = jnp.zeros_like(dal_ref)
            dds_ref[...] = jnp.zeros_like(dds_ref)
            dbias_ref[...] = jnp.zeros_like(dbias_ref)

        em = _expand_mat()
        rm = _reduce_mat()

        def head_reduce(v):
            return _dot_exact(v, rm, _DN_NN, 0)

        dtv = dt_ref[...]
        csv = cs_ref[...]
        dtx = _expand(dtv, em)
        csx = _expand(csv, em)
        dskx = dsk_ref[...]
        xs = xs_ref[...].astype(F32)
        dyv = dy_ref[...].astype(F32)
        xdt = xs * dtx
        ecs = jnp.exp(csx)
        dec = jnp.exp(csx[n - 1:n, :] - csx)
        mask = _causal(n)
        lane = lax.broadcasted_iota(jnp.int32, (n, LANE), 1)
        hlane = lax.broadcasted_iota(jnp.int32, (1, HEAD_PAD), 1)
        hsub = lax.broadcasted_iota(jnp.int32, (HEAD_PAD, 1), 0)
        rsum = jnp.zeros((n, HEAD_PAD), F32)
        csum = jnp.zeros((HEAD_PAD, n), F32)
        for g in range(SSM_GROUPS):
            gs = slice(g * SSM_STATE, (g + 1) * SSM_STATE)
            gc = slice(g * GROUP_CH, (g + 1) * GROUP_CH)
            cmat = cm_ref[:, gs].astype(MXU_DTYPE)
            bmat = bm_ref[:, gs].astype(MXU_DTYPE)
            cb = _nt(cmat, bmat)
            pg = st_ref[0, gc, :].astype(MXU_DTYPE)
            dpg = dprev[gc, :]
            dpgb = dpg.astype(MXU_DTYPE)
            z = _nt(cmat, pg)
            dyg = dyv[:, gc]
            dz = (dyg * ecs[:, gc]).astype(MXU_DTYPE)
            dc = _nn(dz, pg)
            dprev_y = _tn(dz, cmat)
            tcs_s[:, gc] = dyg * z * ecs[:, gc]
            xd = xdt[:, gc] * dec[:, gc]
            wmat = _nt(bmat, dpgb)
            db = _nn(xd.astype(MXU_DTYPE), dpgb)
            tdec_s[:, gc] = wmat * xd
            dxdt_g = wmat * dec[:, gc]
            dcb = jnp.zeros((n, n), F32)
            for q in range(PAIRS_PER_GROUP):
                hp = g * PAIRS_PER_GROUP + q
                sl = slice(hp * LANE, (hp + 1) * LANE)
                xp = xdt[:, sl].astype(MXU_DTYPE)
                dyp = dyv[:, sl]
                dypb = dyp.astype(MXU_DTYPE)
                dxp = None
                for hh in range(2):
                    h = 2 * hp + hh
                    lm = _decay_mat(cs_ref, cst_ref, h, mask)
                    mine = (lane < SSM_HEADDIM) if hh == 0 else (lane >= SSM_HEADDIM)
                    dm = _nt(jnp.where(mine, dyp, 0.0).astype(MXU_DTYPE), xp)
                    dml = dm * lm
                    dcb = dcb + dml
                    gseg = dml * cb
                    rsum = rsum + jnp.sum(gseg, axis=1, keepdims=True) * (hlane == h).astype(F32)
                    csum = csum + (hsub == h).astype(F32) * jnp.sum(gseg, axis=0, keepdims=True)
                    dxh = _tn((cb * lm).astype(MXU_DTYPE), dypb)
                    dxp = dxh if dxp is None else jnp.where(mine, dxh, dxp)
                dxdt_s[:, sl] = dxdt_g[:, q * LANE:(q + 1) * LANE] + dxp
            dcbb = dcb.astype(MXU_DTYPE)
            dxbc_ref[:, CM_COL0 + g * SSM_STATE:CM_COL0 + (g + 1) * SSM_STATE] = (dc + _nn(dcbb, bmat)).astype(dxbc_ref.dtype)
            dxbc_ref[:, BM_COL0 + g * SSM_STATE:BM_COL0 + (g + 1) * SSM_STATE] = (db + _tn(dcbb, cmat)).astype(dxbc_ref.dtype)
            for r in range(HEADS_PER_GROUP):
                h = g * HEADS_PER_GROUP + r
                rows = slice(h * SSM_HEADDIM, (h + 1) * SSM_HEADDIM)
                lr = slice(r * SSM_HEADDIM, (r + 1) * SSM_HEADDIM)
                e = jnp.exp(cst_ref[0, h:h + 1, n - 1:n])
                dprev[rows, :] = dpg[lr, :] * e + dprev_y[lr, :]
            tq = _dot_exact(dpg * st_ref[0, gc, :], rm[gc, :], _DN_TN, 0)
            if g == 0:
                qsum = jnp.sum(tq, axis=0, keepdims=True)
            else:
                qsum = qsum + jnp.sum(tq, axis=0, keepdims=True)
        dxdt = dxdt_s[...]
        dxbc_ref[:, 0:SSM_INNER] = (dxdt * dtx + dyv * dskx).astype(dxbc_ref.dtype)
        ddt = head_reduce(dxdt * xs)
        edec = head_reduce(tdec_s[...])
        ycs = head_reduce(tcs_s[...])
        row = lax.broadcasted_iota(jnp.int32, (n, HEAD_PAD), 0)
        extra = jnp.sum(edec, axis=0, keepdims=True) + qsum * jnp.exp(csv[n - 1:n, :])
        dcs = rsum - csum.T + ycs - edec + jnp.where(row == n - 1, extra, 0.0)
        r2 = lax.broadcasted_iota(jnp.int32, (n, n), 0)
        c2 = lax.broadcasted_iota(jnp.int32, (n, n), 1)
        dda = _dot_exact((c2 >= r2).astype(F32), dcs, _DN_NN, 1)
        a_row = -jnp.exp(al_ref[0:1, :])
        ddt = ddt + dda * a_row
        dal_ref[...] += jnp.sum(dda * dtv, axis=0, keepdims=True) * a_row
        dds_ref[...] += jnp.sum(head_reduce(dyv * xs), axis=0, keepdims=True)
        ddr = ddt * _sigmoid(raw_ref[...] + bias_ref[0:1, :])
        ddr_ref[...] = ddr
        dbias_ref[...] += jnp.sum(ddr, axis=0, keepdims=True)

    par8 = _par_spec((SUBLANE, HEAD_PAD))
    return pl.pallas_call(
        body, grid=(t // n,),
        in_specs=[xs_s, xs_s, bm_s, cm_s, hv_s, hv_s, hvt_s, hvt_s, st_s, _par_spec((1, SSM_INNER)), par8, hv_s, par8],
        out_specs=[xbc_s, hv_s, acc_s, acc_s, acc_s],
        out_shape=[jax.ShapeDtypeStruct((t, SSM_CONV_DIM), STASH_DTYPE), jax.ShapeDtypeStruct((t, HEAD_PAD), F32),
                   jax.ShapeDtypeStruct((1, HEAD_PAD), F32), jax.ShapeDtypeStruct((1, HEAD_PAD), F32),
                   jax.ShapeDtypeStruct((1, HEAD_PAD), F32)],
        scratch_shapes=[pltpu.VMEM((SSM_INNER, SSM_STATE), F32), pltpu.VMEM((n, SSM_INNER), F32),
                        pltpu.VMEM((n, SSM_INNER), F32), pltpu.VMEM((n, SSM_INNER), F32)],
        compiler_params=_params("arbitrary"), name=name)(dy, xbc, xbc, xbc, dt, cs, dtt, cst, st, dskx, a_log8, dt_raw, dt_bias8)


def _gate_norm_fwd(y, proj, norm_g, *, name):
    t, c = y.shape
    tm = _pick(t, WIDE_ROW_TILES)

    def body(y_ref, z_ref, g_ref, o_ref):
        z = z_ref[...].astype(F32)
        yz = y_ref[...].astype(F32) * z * _sigmoid(z)
        for g in range(SSM_GROUPS):
            gc = slice(g * GROUP_CH, (g + 1) * GROUP_CH)
            seg = yz[:, gc]
            r = lax.rsqrt(jnp.mean(seg * seg, axis=-1, keepdims=True) + RMS_EPS)
            o_ref[:, gc] = (seg * r * g_ref[:, gc]).astype(MXU_DTYPE)

    return pl.pallas_call(
        body, grid=(t // tm,), in_specs=[_row_spec(tm, c), _row_spec(tm, c, 1), _par_spec((1, c))],
        out_specs=_row_spec(tm, c), out_shape=jax.ShapeDtypeStruct((t, c), MXU_DTYPE),
        compiler_params=_params("parallel"), name=name)(y, proj, norm_g.reshape(1, c))


def _gate_norm_bwd(dyb, y, proj, norm_g, dproj, *, name):
    t, c = y.shape
    tm = _pick(t, WIDE_ROW_TILES)

    def body(d_ref, y_ref, z_ref, g_ref, _, dy_ref, dz_ref, dg_ref):
        @pl.when(pl.program_id(0) == 0)
        def _():
            dg_ref[...] = jnp.zeros_like(dg_ref)

        z = z_ref[...].astype(F32)
        yv = y_ref[...].astype(F32)
        sz = _sigmoid(z)
        silu = z * sz
        yz = yv * silu
        dv = d_ref[...].astype(F32)
        for g in range(SSM_GROUPS):
            gc = slice(g * GROUP_CH, (g + 1) * GROUP_CH)
            seg = yz[:, gc]
            r = lax.rsqrt(jnp.mean(seg * seg, axis=-1, keepdims=True) + RMS_EPS)
            nrm = seg * r
            dn = dv[:, gc] * g_ref[:, gc]
            dg_ref[:, gc] += jnp.sum(dv[:, gc] * nrm, axis=0, keepdims=True)
            dyz = r * (dn - nrm * jnp.mean(dn * nrm, axis=-1, keepdims=True))
            dy_ref[:, gc] = (dyz * silu[:, gc]).astype(dy_ref.dtype)
            dz_ref[:, gc] = (dyz * yv[:, gc] * (sz[:, gc] * (1.0 + z[:, gc] * (1.0 - sz[:, gc])))).astype(MXU_DTYPE)

    return pl.pallas_call(
        body, grid=(t // tm,), in_specs=[_row_spec(tm, c), _row_spec(tm, c), _row_spec(tm, c, 1), _par_spec((1, c)), _ANY],
        out_specs=[_row_spec(tm, c), _row_spec(tm, c, 1), _par_spec((1, c))],
        out_shape=[jax.ShapeDtypeStruct((t, c), STASH_DTYPE), jax.ShapeDtypeStruct(dproj.shape, dproj.dtype),
                   jax.ShapeDtypeStruct((1, c), F32)],
        input_output_aliases={4: 1},
        compiler_params=_params("arbitrary"), name=name)(dyb, y, proj, norm_g.reshape(1, c), dproj)


GA_COLBLK = GAB_COL0 // D_MODEL


def _merge_fwd(br_a, br_b, proj, *, name):
    t, c = br_a.shape
    tm = _pick(t, ROW_TILES)

    def body(a_ref, b_ref, ga_ref, gb_ref, o_ref):
        o_ref[...] = (_sigmoid(ga_ref[...].astype(F32)) * a_ref[...].astype(F32)
                      + _sigmoid(gb_ref[...].astype(F32)) * b_ref[...].astype(F32)).astype(MXU_DTYPE)

    return pl.pallas_call(
        body, grid=(t // tm,),
        in_specs=[_row_spec(tm, c), _row_spec(tm, c), _row_spec(tm, c, GA_COLBLK), _row_spec(tm, c, GA_COLBLK + 1)],
        out_specs=_row_spec(tm, c), out_shape=jax.ShapeDtypeStruct((t, c), MXU_DTYPE),
        compiler_params=_params("parallel"), name=name)(br_a, br_b, proj, proj)


def _merge_bwd(dm, br_a, br_b, proj, *, name):
    t, c = br_a.shape
    tm = _pick(t, ROW_TILES)

    def body(dm_ref, a_ref, b_ref, ga_ref, gb_ref, da_ref, db_ref, dg_ref):
        d = dm_ref[...].astype(F32)
        sa = _sigmoid(ga_ref[...].astype(F32))
        sb = _sigmoid(gb_ref[...].astype(F32))
        da_ref[...] = (d * sa).astype(MXU_DTYPE)
        db_ref[...] = (d * sb).astype(MXU_DTYPE)
        dg_ref[:, :c] = (d * a_ref[...].astype(F32) * sa * (1.0 - sa)).astype(MXU_DTYPE)
        dg_ref[:, c:] = (d * b_ref[...].astype(F32) * sb * (1.0 - sb)).astype(MXU_DTYPE)

    return pl.pallas_call(
        body, grid=(t // tm,),
        in_specs=[_row_spec(tm, c), _row_spec(tm, c), _row_spec(tm, c), _row_spec(tm, c, GA_COLBLK), _row_spec(tm, c, GA_COLBLK + 1)],
        out_specs=[_row_spec(tm, c), _row_spec(tm, c), _row_spec(tm, 2 * c, GAB_COL0 // (2 * c))],
        out_shape=[jax.ShapeDtypeStruct((t, c), MXU_DTYPE), jax.ShapeDtypeStruct((t, c), MXU_DTYPE),
                   jax.ShapeDtypeStruct((t, MAIN_COLS), MXU_DTYPE)],
        compiler_params=_params("parallel"), name=name)(dm, br_a, br_b, proj, proj)


def _layer_fwd(x, xb, memn_b, w, *, bsz, tag):
    nc = x.shape[0] // bsz // CHUNK
    sv = {"x_in": xb}
    proj = _mm(xb, w["w_main"], out_dtype=STASH_DTYPE, name=f"{tag}_proj")
    dt_raw = _mm(xb, w["w_dt"], name=f"{tag}_dtproj")
    sgo = _sg_fwd(proj, w["sg_ln_g"], w["sg_ln_b"], w["sg_w"], w["sg_bcol"], name=f"{tag}_sg_fwd")
    xbc = _conv_fwd(proj, w["conv_w"], w["conv_b"], bsz=bsz, name=f"{tag}_conv_fwd")
    dt, cs, dtt, cst = _ssd_prep(dt_raw, w["dt_bias8"], w["a_log8"], name=f"{tag}_ssd_prep")
    y, st = _ssd_fwd(xbc, dt, cs, dtt, cst, w["d_skipx"], nc=nc, name=f"{tag}_ssd_fwd")
    yb = _gate_norm_fwd(y, proj, w["ssm_norm_g"], name=f"{tag}_gate_norm_fwd")
    if "rest" in w:
        w = w["rest"](w, yb)
    br_a = _mm(sgo, w["p_a"], out_dtype=STASH_DTYPE, name=f"{tag}_br_a")
    br_b = _mm(yb, w["p_b"], out_dtype=STASH_DTYPE, name=f"{tag}_br_b")
    merged = _merge_fwd(br_a, br_b, proj, name=f"{tag}_merge_fwd")
    mix = _mm(merged, w["w_mix_o"], name=f"{tag}_mix_o")
    x1, x1b, xh1, rs1 = _ln_fwd(x, mix, w["ln_g"][0], w["ln_b"][0], name=f"{tag}_ln1_fwd")
    sv.update(proj=proj, dt_raw=dt_raw, sgo=sgo, xbc=xbc, dt=dt, cs=cs, dtt=dtt, cst=cst, y=y, st=st, yb=yb,
              br_a=br_a, br_b=br_b, merged=merged, xh1=xh1, rs1=rs1, x1b=x1b)
    q = _mm(x1b, w["w_xq"], out_dtype=MXU_DTYPE, name=f"{tag}_q")
    kv = _mm(memn_b, w["w_xkv"], out_dtype=MXU_DTYPE, name=f"{tag}_kv")
    o = _attn_fwd(q, kv, bsz=bsz, name=f"{tag}_attn_fwd")
    att = _mm(o, w["w_xo"], name=f"{tag}_xo")
    x2, x2b, xh2, rs2 = _ln_fwd(x1, att, w["ln_g"][1], w["ln_b"][1], name=f"{tag}_ln2_fwd")
    sv.update(q=q, kv=kv, o=o, xh2=xh2, rs2=rs2, x2b=x2b)
    h = _mm(x2b, w["w_ffn_in"], out_dtype=STASH_DTYPE, name=f"{tag}_ffn_in")
    a = _swiglu_fwd(h, name=f"{tag}_swiglu_fwd")
    ffn = _mm(a, w["w_ffn_out"], name=f"{tag}_ffn_out")
    x3, x3b, xh3, rs3 = _ln_fwd(x2, ffn, w["ln_g"][2], w["ln_b"][2], name=f"{tag}_ln3_fwd")
    sv.update(h=h, a=a, xh3=xh3, rs3=rs3)
    return x3, x3b, sv, w


GRAD_GROUPS = (("w_ffn_out", "w_ffn_in", "w_xo", "w_xq", "w_xkv"), ("w_mix_o", "p_a", "p_b"), ("w_in",))


def _layer_bwd(dx3_addends, dx3_scales, memn_b, w, sv, on_group=None, *, bsz, tag):
    nc = sv["xh1"].shape[0] // bsz // CHUNK
    gr = {}

    def group_done(k):
        return on_group(GRAD_GROUPS[k], gr) if on_group is not None else None
    dp3, dp3b, dg3, db3 = _ln_bwd(dx3_addends, dx3_scales, sv["xh3"], sv["rs3"], w["ln_g"][2], name=f"{tag}_ln3_bwd")
    da = _mm(dp3b, w["w_ffn_out"], tb=True, out_dtype=STASH_DTYPE, name=f"{tag}_d_a")
    gr["w_ffn_out"] = _mm(sv["a"], dp3b, ta=True, name=f"{tag}_dw_ffn_out")
    dh = _swiglu_bwd(sv["h"], da, name=f"{tag}_swiglu_bwd")
    gr["w_ffn_in"] = _mm(sv["x2b"], dh, ta=True, name=f"{tag}_dw_ffn_in")
    dx2_br = _mm(dh, w["w_ffn_in"], tb=True, name=f"{tag}_dx2")
    dp2, dp2b, dg2, db2 = _ln_bwd([dp3, dx2_br], [ALPHA, 1.0], sv["xh2"], sv["rs2"], w["ln_g"][1], name=f"{tag}_ln2_bwd")
    do = _mm(dp2b, w["w_xo"], tb=True, out_dtype=MXU_DTYPE, name=f"{tag}_d_o")
    gr["w_xo"] = _mm(sv["o"], dp2b, ta=True, name=f"{tag}_dw_xo")
    dq, dk, dv = _attn_bwd(sv["q"], sv["kv"], do, bsz=bsz, name=f"{tag}_attn_bwd")
    dkv = jnp.concatenate([dk, dv], axis=1)
    gr["w_xq"] = _mm(sv["x1b"], dq, ta=True, name=f"{tag}_dw_xq")
    gr["w_xkv"] = _mm(memn_b, dkv, ta=True, name=f"{tag}_dw_xkv")
    dmemn = _mm(dkv, w["w_xkv"], tb=True, name=f"{tag}_d_memn")
    dx1_br = _mm(dq, w["w_xq"], tb=True, name=f"{tag}_dx1")
    token = group_done(0)
    ln_g1 = w["ln_g"][0] if token is None else w["ln_g"][0] + token[0, 0]
    dp1, dp1b, dg1, db1 = _ln_bwd([dp2, dx1_br], [ALPHA, 1.0], sv["xh1"], sv["rs1"], ln_g1, name=f"{tag}_ln1_bwd")
    gr["ln_g"] = jnp.concatenate([dg1, dg2, dg3], axis=0)
    gr["ln_b"] = jnp.concatenate([db1, db2, db3], axis=0)
    dmerged = _mm(dp1b, w["w_mix_o"], tb=True, out_dtype=STASH_DTYPE, name=f"{tag}_d_merged")
    gr["w_mix_o"] = _mm(sv["merged"], dp1b, ta=True, name=f"{tag}_dw_mix_o")
    dbr_a, dbr_b, dproj = _merge_bwd(dmerged, sv["br_a"], sv["br_b"], sv["proj"], name=f"{tag}_merge_bwd")
    gr["p_a"] = _mm(sv["sgo"], dbr_a, ta=True, name=f"{tag}_dw_p_a")
    gr["p_b"] = _mm(sv["yb"], dbr_b, ta=True, name=f"{tag}_dw_p_b")
    dsgo = _mm(dbr_a, w["p_a"], tb=True, out_dtype=STASH_DTYPE, name=f"{tag}_d_sgo")
    dyb = _mm(dbr_b, w["p_b"], tb=True, out_dtype=STASH_DTYPE, name=f"{tag}_d_yb")
    token = group_done(1)
    norm_g = w["ssm_norm_g"] if token is None else w["ssm_norm_g"] + token[0, 0]
    dy, dproj, gr["ssm_norm_g"] = _gate_norm_bwd(dyb, sv["y"], sv["proj"], norm_g, dproj, name=f"{tag}_gate_norm_bwd")
    dxbc, ddr, gr["a_log"], gr["d_skip"], gr["dt_bias"] = _ssd_bwd(
        dy, sv["xbc"], sv["dt"], sv["cs"], sv["dtt"], sv["cst"], sv["st"], w["d_skipx"], w["a_log8"], sv["dt_raw"],
        w["dt_bias8"], nc=nc, name=f"{tag}_ssd_bwd")
    dproj, gr["conv_w"], gr["conv_b"] = _conv_bwd(sv["proj"], dxbc, w["conv_w"], w["conv_b"], dproj, bsz=bsz, name=f"{tag}_conv_bwd")
    dproj, gr["sg_w"], dsg_bcol, gr["sg_ln_g"], gr["sg_ln_b"] = _sg_bwd(
        sv["proj"], dsgo, w["sg_ln_g"], w["sg_ln_b"], w["sg_w"], w["sg_bcol"], dproj, name=f"{tag}_sg_bwd")
    gr["sg_b"] = dsg_bcol[..., 0]
    gr["w_main"] = _mm(sv["x_in"], dproj, ta=True, name=f"{tag}_dw_main")
    gr["w_dt"] = _mm(sv["x_in"], ddr, ta=True, name=f"{tag}_dw_dt")
    token = group_done(2)
    dx_dt = _mm(ddr, w["w_dt"], tb=True, after=token, name=f"{tag}_dx_dt")
    dx_main = _mm(dproj, w["w_main"], tb=True, after=token, name=f"{tag}_dx_main")
    return [dp1, dx_main, dx_dt], [ALPHA, 1.0, 1.0], gr, dmemn


def _local_step(x, mem, tgt, mem_ln_g, mem_ln_b, layers, on_layer_grads=None):
    bsz, s, d = x.shape
    xf = x.reshape(bsz * s, d)
    memf = mem.reshape(-1, d)
    _, memn_b, mxh, mrs = _ln_fwd(memf, None, mem_ln_g, mem_ln_b, name="mem_ln_fwd")
    cur, curb, saved, weights = xf, xf, [], []
    for li, get_weights in enumerate(layers):
        cur, curb, sv, w = _layer_fwd(cur, curb, memn_b, get_weights(cur), bsz=bsz, tag=f"l{li}")
        saved.append(sv)
        weights.append(w)
    dy, lsum = _loss_head(cur, tgt.reshape(bsz * s, d), name="loss_head")
    addends, scales = [dy], [1.0]
    grads, dmem = [None] * len(layers), []
    for li in reversed(range(len(layers))):
        on_group = None if on_layer_grads is None else functools.partial(on_layer_grads, li)
        addends, scales, grads[li], dm = _layer_bwd(addends, scales, memn_b, weights[li], saved[li], on_group, bsz=bsz, tag=f"l{li}")
        dmem.append(dm)
    grad_x = _add_scaled(addends, scales, name="grad_x").reshape(bsz, s, d)
    _, _, dmg, dmb = _ln_bwd(dmem, [1.0] * len(dmem), mxh, mrs, mem_ln_g, name="mem_ln_bwd")
    return lsum, grad_x, grads, dmg[0], dmb[0]


_ANY = pl.BlockSpec(memory_space=pl.ANY)
_MESH = pl.DeviceIdType.MESH


def _all_gather8(x, *, name):
    def body(x_ref, out_ref, send_sems, recv_sems):
        mx, my, mc = lax.axis_index("x"), lax.axis_index("y"), lax.axis_index("c")
        me, sibling = (mx, my, mc), (mx, my, 1 - mc)
        chips = [(1 - mx, my), (mx, 1 - my), (1 - mx, 1 - my)]

        def blk(px, py, pc):
            return out_ref.at[4 * px + 2 * py + pc]

        def copy(k, block, to, src=None):
            return pltpu.make_async_remote_copy(
                src_ref=blk(*block) if src is None else src, dst_ref=blk(*block), send_sem=send_sems.at[k],
                recv_sem=recv_sems.at[k], device_id=to, device_id_type=_MESH)

        first = [copy(0, me, sibling, src=x_ref)]
        first += [copy(1 + j, me, (*chip, mc), src=x_ref) for j, chip in enumerate(chips)]
        for cp in first:
            cp.start()
        passed = [copy(4 + j, (*chip, mc), sibling) for j, chip in enumerate(chips)]
        for j, chip in enumerate(chips):
            copy(1 + j, (*chip, mc), me).wait_recv()
            passed[j].start()
        copy(0, sibling, me).wait_recv()
        for j, chip in enumerate(chips):
            copy(4 + j, (*chip, 1 - mc), me).wait_recv()
        for cp in first + passed:
            cp.wait_send()

    return pl.pallas_call(
        body, out_shape=jax.ShapeDtypeStruct((N_DEV,) + x.shape, x.dtype), in_specs=[_ANY], out_specs=_ANY,
        scratch_shapes=[pltpu.SemaphoreType.DMA((7,)), pltpu.SemaphoreType.DMA((7,))], name=name)(x)


def _row_tile(rows, row_bytes, mult=SUBLANE):
    best = None
    for tr in range(mult, rows + 1, mult):
        if rows % tr == 0 and (best is None or tr * row_bytes <= BLOCK_BYTES):
            best = tr
    return rows if best is None else best


def _gather_shape(r, c, kind):
    return {"row": (2, N_CHIPS * r, c), "col": (2, r, N_CHIPS * c), "chip": (2, N_CHIPS, r, c)}[kind]


def _cast_place(shard, kind, dtype, chip_idx, *, name):
    _, r, c = shard.shape
    tr = _row_tile(r, c * 4, 16)
    nt = r // tr

    def body(_, s_ref, o_ref):
        o_ref[...] = s_ref[...].astype(dtype)

    if kind == "row":
        out_spec = pl.BlockSpec((None, tr, c), lambda l, i, j_ref: (l, j_ref[0] * nt + i, 0))
    elif kind == "col":
        out_spec = pl.BlockSpec((None, tr, c), lambda l, i, j_ref: (l, i, j_ref[0]))
    else:
        out_spec = pl.BlockSpec((None, None, tr, c), lambda l, i, j_ref: (l, j_ref[0], i, 0))
    grid_spec = pltpu.PrefetchScalarGridSpec(
        num_scalar_prefetch=1, grid=(2, nt), in_specs=[pl.BlockSpec((None, tr, c), lambda l, i, j_ref: (l, i, 0))],
        out_specs=out_spec)
    return pl.pallas_call(body, grid_spec=grid_spec, out_shape=jax.ShapeDtypeStruct(_gather_shape(r, c, kind), dtype),
                          compiler_params=_params("parallel", "parallel"), name=name)(chip_idx, shard)


def _gather_params(bufs, shard_shapes, kinds, *, name):
    n = len(bufs)

    def body(*refs):
        outs = refs[n:2 * n]
        send_sems, recv_sems = refs[2 * n:]
        mx, my, mc = lax.axis_index("x"), lax.axis_index("y"), lax.axis_index("c")
        me, sibling = (mx, my, mc), (mx, my, 1 - mc)
        chips = [(1 - mx, my), (mx, 1 - my), (1 - mx, 1 - my)]

        def blk(i, px, py, pc):
            r, c = shard_shapes[i]
            j = 2 * px + py
            if kinds[i] == "row":
                return outs[i].at[pc, pl.ds(pl.multiple_of(j * r, r), r)]
            if kinds[i] == "col":
                return outs[i].at[pc, :, pl.ds(pl.multiple_of(j * c, c), c)]
            return outs[i].at[pc, j]

        def copy(i, k, block, to):
            return pltpu.make_async_remote_copy(
                src_ref=blk(i, *block), dst_ref=blk(i, *block), send_sem=send_sems.at[6 * i + k],
                recv_sem=recv_sems.at[6 * i + k], device_id=to, device_id_type=_MESH)

        sent = []
        for i in range(n):
            for j, chip in enumerate(chips):
                cp = copy(i, j, me, (*chip, mc))
                cp.start()
                sent.append(cp)
        for j, chip in enumerate(chips):
            for i in range(n):
                copy(i, j, (*chip, mc), me).wait_recv()
                fwd = copy(i, 3 + j, (*chip, mc), sibling)
                fwd.start()
                sent.append(fwd)
        for i in range(n):
            for j, chip in enumerate(chips):
                copy(i, 3 + j, (*chip, 1 - mc), me).wait_recv()
        for cp in sent:
            cp.wait_send()

    return pl.pallas_call(
        body, out_shape=[jax.ShapeDtypeStruct(b.shape, b.dtype) for b in bufs], in_specs=[_ANY] * n, out_specs=[_ANY] * n,
        input_output_aliases={i: i for i in range(n)},
        scratch_shapes=[pltpu.SemaphoreType.DMA((6 * n,)), pltpu.SemaphoreType.DMA((6 * n,))], name=name)(*bufs)


def _half(r, h):
    return pl.ds(pl.multiple_of(h * (r // 2), r // 2), r // 2)


_HBM = pl.BlockSpec(memory_space=pltpu.HBM)
_SEM = pl.BlockSpec(memory_space=pltpu.SEMAPHORE)
_EFFECT = pltpu.SideEffectType.DATAFLOW_SIDE_EFFECTING


def _sibling_copies(g_refs, land_refs, gs, views, send_sems, recv_sems):
    mx, my, mc = lax.axis_index("x"), lax.axis_index("y"), lax.axis_index("c")
    copies = []
    for i in range(len(gs)):
        if views[i] == "chip":
            src = g_refs[i].at[:, _half(gs[i].shape[1], 1 - mc)]
        else:
            src = g_refs[i].at[_half(gs[i].shape[0], 1 - mc)]
        copies.append(pltpu.make_async_remote_copy(src_ref=src, dst_ref=land_refs[i], send_sem=send_sems.at[i], recv_sem=recv_sems.at[i],
                                                   device_id=(mx, my, 1 - mc), device_id_type=_MESH))
    return copies


def _half_shape(g, view):
    return (g.shape[0], g.shape[1] // 2, g.shape[2]) if view == "chip" else (g.shape[0] // 2, g.shape[1])


def _grads_to_sibling_start(gs, views, *, name):
    n = len(gs)
    lands = [pltpu.with_memory_space_constraint(lax.empty(_half_shape(g, v), g.dtype), pltpu.HBM) for g, v in zip(gs, views)]

    def body(*refs):
        for cp in _sibling_copies(refs[:n], refs[n:2 * n], gs, views, refs[2 * n], refs[2 * n + 1]):
            cp.start()
        refs[-1][...] = jnp.zeros_like(refs[-1])

    outs = pl.pallas_call(
        body, name=name,
        out_shape=(pltpu.SemaphoreType.DMA((n,)), pltpu.SemaphoreType.DMA((n,)),
                   *[pltpu.HBM(x.shape, x.dtype) for x in list(gs) + lands], jax.ShapeDtypeStruct((SUBLANE, LANE), F32)),
        in_specs=[_HBM] * (2 * n), out_specs=(_SEM, _SEM, *[_HBM] * (2 * n), pl.BlockSpec(memory_space=pltpu.VMEM)),
        input_output_aliases={i: 2 + i for i in range(2 * n)},
        compiler_params=pltpu.CompilerParams(has_side_effects=_EFFECT),
    )(*[pltpu.with_memory_space_constraint(g, pltpu.HBM) for g in gs], *lands)
    return outs[0], outs[1], list(outs[2:2 + n]), list(outs[2 + n:2 + 2 * n]), outs[-1]


def _grads_to_sibling_wait(send_sems, recv_sems, gs, lands, views, after, *, name):
    n = len(gs)

    def body(*refs):
        for cp in _sibling_copies(refs[:n], refs[n:2 * n], gs, views, refs[2 * n], refs[2 * n + 1]):
            cp.wait_send()
            cp.wait_recv()

    outs = pl.pallas_call(
        body, name=name, out_shape=tuple(pltpu.HBM(x.shape, x.dtype) for x in list(gs) + list(lands)),
        in_specs=[_HBM] * (2 * n) + [_SEM, _SEM, _ANY], out_specs=tuple([_HBM] * (2 * n)),
        input_output_aliases={i: i for i in range(2 * n)},
        compiler_params=pltpu.CompilerParams(has_side_effects=_EFFECT),
    )(*gs, *lands, send_sems, recv_sems, after)
    return list(outs[:n]), list(outs[n:])


def _cast_place_layer(shard, l, kind, chip_idx, after, *, name):
    _, r, c = shard.shape
    tr = _row_tile(r, c * 4, 16)
    nt = r // tr

    def body(_, s_ref, *rest):
        rest[-1][...] = s_ref[...].astype(MXU_DTYPE)

    if kind == "row":
        out_spec = pl.BlockSpec((tr, c), lambda i, j_ref: (j_ref[0] * nt + i, 0))
    elif kind == "col":
        out_spec = pl.BlockSpec((tr, c), lambda i, j_ref: (i, j_ref[0]))
    else:
        out_spec = pl.BlockSpec((None, tr, c), lambda i, j_ref: (j_ref[0], i, 0))
    extra = [] if after is None else [after]
    grid_spec = pltpu.PrefetchScalarGridSpec(
        num_scalar_prefetch=1, grid=(nt,), in_specs=[pl.BlockSpec((None, tr, c), lambda i, j_ref: (l, i, 0))] + [_ANY] * len(extra),
        out_specs=out_spec)
    return pl.pallas_call(body, grid_spec=grid_spec, out_shape=jax.ShapeDtypeStruct(_gather_shape(r, c, kind)[1:], MXU_DTYPE),
                          compiler_params=_params("parallel"), name=name)(chip_idx, shard, *extra)


def _half_block(ref, kind, r, c, j, h):
    rows = _half(r, h)
    if kind == "row":
        return ref.at[pl.ds(pl.multiple_of(j * r + h * (r // 2), r // 2), r // 2)]
    if kind == "col":
        return ref.at[rows, pl.ds(pl.multiple_of(j * c, c), c)]
    return ref.at[j, rows]


def _gather_ici_copies(buf_refs, shapes, kinds, send_sems, recv_sems):
    mx, my, mc = lax.axis_index("x"), lax.axis_index("y"), lax.axis_index("c")
    chips = [(1 - mx, my), (mx, 1 - my), (1 - mx, 1 - my)]
    copies = []
    for i, (r, c) in enumerate(shapes):
        mine = _half_block(buf_refs[i], kinds[i], r, c, 2 * mx + my, mc)
        for k, (px, py) in enumerate(chips):
            copies.append(pltpu.make_async_remote_copy(
                src_ref=mine, dst_ref=mine, send_sem=send_sems.at[3 * i + k], recv_sem=recv_sems.at[3 * i + k],
                device_id=(px, py, mc), device_id_type=_MESH))
    return copies


def _gather_start(bufs, shapes, kinds, *, name):
    n = len(bufs)

    def body(*refs):
        send_sems, recv_sems, token = refs[n], refs[n + 1], refs[-1]
        for cp in _gather_ici_copies(refs[:n], shapes, kinds, send_sems, recv_sems):
            cp.start()
        token[...] = jnp.zeros_like(token)

    outs = pl.pallas_call(
        body, name=name,
        out_shape=(pltpu.SemaphoreType.DMA((3 * n,)), pltpu.SemaphoreType.DMA((3 * n,)),
                   *[pltpu.HBM(b.shape, b.dtype) for b in bufs], jax.ShapeDtypeStruct((SUBLANE, LANE), F32)),
        in_specs=[_HBM] * n, out_specs=(_SEM, _SEM, *[_HBM] * n, pl.BlockSpec(memory_space=pltpu.VMEM)),
        input_output_aliases={i: 2 + i for i in range(n)},
        compiler_params=pltpu.CompilerParams(has_side_effects=_EFFECT),
    )(*[pltpu.with_memory_space_constraint(b, pltpu.HBM) for b in bufs])
    return outs[0], outs[1], list(outs[2:2 + n]), outs[-1]


def _gather_wait(send_sems, recv_sems, bufs, shapes, kinds, after, *, name):
    n = len(bufs)

    def body(*refs):
        for cp in _gather_ici_copies(refs[:n], shapes, kinds, refs[n], refs[n + 1]):
            cp.wait_send()
            cp.wait_recv()

    outs = pl.pallas_call(
        body, name=name, out_shape=tuple(pltpu.HBM(b.shape, b.dtype) for b in bufs),
        in_specs=[_HBM] * n + [_SEM, _SEM, _ANY], out_specs=tuple([_HBM] * n), input_output_aliases={i: i for i in range(n)},
        compiler_params=pltpu.CompilerParams(has_side_effects=_EFFECT),
    )(*bufs, send_sems, recv_sems, after)
    return list(outs)


def _gather_forward(bufs, shapes, kinds, *, name):
    n = len(bufs)

    def body(*refs):
        outs = refs[n:2 * n]
        send_sems, recv_sems = refs[2 * n:]
        mx, my, mc = lax.axis_index("x"), lax.axis_index("y"), lax.axis_index("c")
        chips = [(1 - mx, my), (mx, 1 - my), (1 - mx, 1 - my)]
        copies = []
        for i, (r, c) in enumerate(shapes):
            for k, (px, py) in enumerate(chips):
                got = _half_block(outs[i], kinds[i], r, c, 2 * px + py, mc)
                cp = pltpu.make_async_remote_copy(src_ref=got, dst_ref=got, send_sem=send_sems.at[3 * i + k],
                                                  recv_sem=recv_sems.at[3 * i + k], device_id=(mx, my, 1 - mc), device_id_type=_MESH)
                cp.start()
                copies.append(cp)
        for cp in copies:
            cp.wait()

    return pl.pallas_call(
        body, out_shape=[jax.ShapeDtypeStruct(b.shape, b.dtype) for b in bufs], in_specs=[_ANY] * n, out_specs=[_ANY] * n,
        input_output_aliases={i: i for i in range(n)},
        scratch_shapes=[pltpu.SemaphoreType.DMA((3 * n,)), pltpu.SemaphoreType.DMA((3 * n,))], name=name)(*bufs)


def _chip_exchange_copies(pair_refs, land_refs, pairs, views, send_sems, recv_sems):
    mx, my, mc = lax.axis_index("x"), lax.axis_index("y"), lax.axis_index("c")
    me = 2 * mx + my
    chips = [(1 - mx, my), (mx, 1 - my), (1 - mx, 1 - my)]
    copies = []
    for i in range(len(pairs)):
        for k, (px, py) in enumerate(chips):
            j = 2 * px + py
            if views[i] == "chip":
                src = pair_refs[i].at[j]
            else:
                c = pairs[i].shape[1] // N_CHIPS
                src = pair_refs[i].at[:, pl.ds(pl.multiple_of(j * c, c), c)]
            copies.append(pltpu.make_async_remote_copy(
                src_ref=src, dst_ref=land_refs[i].at[me], send_sem=send_sems.at[3 * i + k], recv_sem=recv_sems.at[3 * i + k],
                device_id=(px, py, mc), device_id_type=_MESH))
    return copies


def _quad_shape(p, view):
    return p.shape if view == "chip" else (N_CHIPS, p.shape[0], p.shape[1] // N_CHIPS)


def _grads_to_chips_start(pairs, views, *, name):
    n = len(pairs)
    lands = [pltpu.with_memory_space_constraint(lax.empty(_quad_shape(p, v), p.dtype), pltpu.HBM) for p, v in zip(pairs, views)]

    def body(*refs):
        pair_refs, land_refs = refs[:n], refs[n:2 * n]
        send_sems, recv_sems = refs[2 * n], refs[2 * n + 1]
        token = refs[-1]
        for cp in _chip_exchange_copies(pair_refs, land_refs, pairs, views, send_sems, recv_sems):
            cp.start()
        token[...] = jnp.zeros_like(token)

    outs = pl.pallas_call(
        body, name=name,
        out_shape=(pltpu.SemaphoreType.DMA((3 * n,)), pltpu.SemaphoreType.DMA((3 * n,)),
                   *[pltpu.HBM(p.shape, p.dtype) for p in pairs], *[pltpu.HBM(l.shape, l.dtype) for l in lands],
                   jax.ShapeDtypeStruct((SUBLANE, LANE), F32)),
        in_specs=[_HBM] * (2 * n), out_specs=(_SEM, _SEM, *[_HBM] * (2 * n), pl.BlockSpec(memory_space=pltpu.VMEM)),
        input_output_aliases={i: 2 + i for i in range(2 * n)},
        compiler_params=pltpu.CompilerParams(has_side_effects=_EFFECT),
    )(*[pltpu.with_memory_space_constraint(p, pltpu.HBM) for p in pairs], *lands)
    return outs[0], outs[1], list(outs[2:2 + n]), list(outs[2 + n:2 + 2 * n]), outs[-1]


def _grads_to_chips_wait(send_sems, recv_sems, pairs, lands, views, after, *, name):
    n = len(pairs)

    def body(*refs):
        pair_refs, land_refs = refs[:n], refs[n:2 * n]
        s_sems, r_sems = refs[2 * n], refs[2 * n + 1]
        for cp in _chip_exchange_copies(pair_refs, land_refs, pairs, views, s_sems, r_sems):
            cp.wait_send()
            cp.wait_recv()

    outs = pl.pallas_call(
        body, name=name, out_shape=tuple(pltpu.HBM(x.shape, x.dtype) for x in list(pairs) + list(lands)),
        in_specs=[_HBM] * (2 * n) + [_SEM, _SEM, _ANY], out_specs=tuple([_HBM] * (2 * n)),
        input_output_aliases={i: i for i in range(2 * n)},
        compiler_params=pltpu.CompilerParams(has_side_effects=_EFFECT),
    )(*pairs, *lands, send_sems, recv_sems, after)
    return list(outs[n:])


def _grads_share(tots, *, name):
    n = len(tots)

    def body(*refs):
        ins, outs = refs[:n], refs[n:2 * n]
        send_sems, recv_sems = refs[2 * n:]
        mx, my, mc = lax.axis_index("x"), lax.axis_index("y"), lax.axis_index("c")
        copies = []
        for i in range(n):
            cp = pltpu.make_async_remote_copy(src_ref=ins[i], dst_ref=outs[i], send_sem=send_sems.at[i], recv_sem=recv_sems.at[i],
                                              device_id=(mx, my, 1 - mc), device_id_type=_MESH)
            cp.start()
            copies.append(cp)
        for cp in copies:
            cp.wait()

    return pl.pallas_call(
        body, out_shape=[jax.ShapeDtypeStruct(t.shape, t.dtype) for t in tots], in_specs=[_ANY] * n, out_specs=[_ANY] * n,
        scratch_shapes=[pltpu.SemaphoreType.DMA((n,)), pltpu.SemaphoreType.DMA((n,))], name=name)(*tots)


def _pair_sum(g, recv, view, c_idx, *, name):
    def body(c_ref, a_ref, b_ref, o_ref):
        o_ref[...] = (a_ref[...] + b_ref[...]).astype(WIRE_DTYPE)

    if view == "chip":
        nch, r, c = g.shape
        tr = _row_tile(r // 2, nch * c * 4, 16)
        gv = g.reshape(nch, 2, r // 2, c)
        grid = ((r // 2) // tr,)
        in_specs = [pl.BlockSpec((nch, None, tr, c), lambda i, c_ref: (0, c_ref[0], i, 0)),
                    pl.BlockSpec((nch, tr, c), lambda i, c_ref: (0, i, 0))]
        out_spec = pl.BlockSpec((nch, tr, c), lambda i, c_ref: (0, i, 0))
        sem = ("parallel",)
    else:
        r, c4 = g.shape
        tr = _row_tile(r // 2, c4 * 4, 16)
        gv = g.reshape(2, r // 2, c4)
        grid = ((r // 2) // tr,)
        in_specs = [pl.BlockSpec((None, tr, c4), lambda i, c_ref: (c_ref[0], i, 0)), pl.BlockSpec((tr, c4), lambda i, c_ref: (i, 0))]
        out_spec = pl.BlockSpec((tr, c4), lambda i, c_ref: (i, 0))
        sem = ("parallel",)
    grid_spec = pltpu.PrefetchScalarGridSpec(num_scalar_prefetch=1, grid=grid, in_specs=in_specs, out_specs=out_spec)
    return pl.pallas_call(body, grid_spec=grid_spec, out_shape=jax.ShapeDtypeStruct(recv.shape, WIRE_DTYPE),
                          compiler_params=_params(*sem), name=name)(c_idx, gv, recv)


def _quad_sum(gs, recvs, quads, view, chip_idx, c_idx, *, name):
    nl = len(quads)
    nch, rh, c = quads[0].shape
    tr = _row_tile(rh, c * 4, 16)

    def body(_, __, *refs):
        o_ref = refs[-1]
        per = nch + 1
        for l in range(nl):
            grp = refs[l * per:(l + 1) * per]
            acc = grp[0][...] + grp[1][...]
            for r in grp[2:]:
                acc = acc + r[...].astype(F32)
            o_ref[l] = acc

    if view == "chip":
        own = [pl.BlockSpec((None, None, tr, c), lambda i, j, h: (j[0], h[0], i, 0)),
               pl.BlockSpec((None, tr, c), lambda i, j, h: (j[0], i, 0))]
        gviews = [g.reshape(nch, 2, rh, c) for g in gs]
    else:
        own = [pl.BlockSpec((None, tr, c), lambda i, j, h: (h[0], i, j[0])), pl.BlockSpec((tr, c), lambda i, j, h: (i, j[0]))]
        gviews = [g.reshape(2, rh, nch * c) for g in gs]
    assert nch & (nch - 1) == 0
    got = [pl.BlockSpec((None, tr, c), functools.partial(lambda i, j, h, k: ((j[0] + k) & (nch - 1), i, 0), k=k))
           for k in range(1, nch)]
    ins = []
    for l in range(nl):
        ins += [gviews[l], recvs[l]] + [quads[l]] * (nch - 1)
    grid_spec = pltpu.PrefetchScalarGridSpec(
        num_scalar_prefetch=2, grid=(rh // tr,), in_specs=(own + got) * nl,
        out_specs=pl.BlockSpec((nl, tr, c), lambda i, j, h: (0, i, 0)))
    return pl.pallas_call(body, grid_spec=grid_spec, out_shape=jax.ShapeDtypeStruct((nl, rh, c), F32),
                          compiler_params=_params("parallel"), name=name)(chip_idx, c_idx, *ins)


def _sum_devices(g8, own, dev_idx, *, name):
    k, rows, cols = g8.shape

    def body(d_ref, a_ref, x_ref, o_ref):
        acc = None
        for i in range(k):
            term = jnp.where(d_ref[0] == i, x_ref[...], a_ref[i])
            acc = term if acc is None else acc + term
        o_ref[...] = acc

    grid_spec = pltpu.PrefetchScalarGridSpec(
        num_scalar_prefetch=1, grid=(1,),
        in_specs=[pl.BlockSpec((k, rows, cols), lambda i, d_ref: (0, 0, 0)), pl.BlockSpec((rows, cols), lambda i, d_ref: (0, 0))],
        out_specs=pl.BlockSpec((rows, cols), lambda i, d_ref: (0, 0)))
    return pl.pallas_call(body, grid_spec=grid_spec, out_shape=jax.ShapeDtypeStruct((rows, cols), g8.dtype),
                          compiler_params=_params("arbitrary"), name=name)(dev_idx, g8, own)


def _adamw(w, g, m, v, *, name):
    rows, cols = w.shape
    tr = rows
    for cand in (256, 128, 64, 32, 16, 8):
        if rows % cand == 0 and cand * cols <= 512 * 1024:
            tr = cand
            break
    c1 = 1.0 - ADAM_B1 ** ADAM_STEP
    c2 = 1.0 - ADAM_B2 ** ADAM_STEP

    def body(w_ref, g_ref, m_ref, v_ref, d_ref, nm_ref, nv_ref):
        gv = g_ref[...]
        nm = ADAM_B1 * m_ref[...] + (1.0 - ADAM_B1) * gv
        nv = ADAM_B2 * v_ref[...] + (1.0 - ADAM_B2) * (gv * gv)
        d_ref[...] = -ADAM_LR * ((nm / c1) / (jnp.sqrt(nv / c2) + ADAM_EPS) + ADAM_WD * w_ref[...])
        nm_ref[...] = nm
        nv_ref[...] = nv

    spec = pl.BlockSpec((tr, cols), lambda i: (i, 0))
    shp = jax.ShapeDtypeStruct((rows, cols), F32)
    return pl.pallas_call(body, grid=(rows // tr,), in_specs=[spec] * 4, out_specs=[spec] * 3, out_shape=[shp] * 3,
                          compiler_params=_params("parallel"), name=name)(w, g, m, v)


def _adamw_halves(w, m, v, mine, other, c_idx, *, name):
    nl, r, c = w.shape
    rh = r // 2
    tr = _row_tile(rh, c * 4)
    c1 = 1.0 - ADAM_B1 ** ADAM_STEP
    c2 = 1.0 - ADAM_B2 ** ADAM_STEP

    def body(c_ref, w_ref, m_ref, v_ref, a_ref, b_ref, g_ref, d_ref, nm_ref, nv_ref):
        gv = jnp.where(pl.program_id(1) == c_ref[0], a_ref[...], b_ref[...])
        nm = ADAM_B1 * m_ref[...] + (1.0 - ADAM_B1) * gv
        nv = ADAM_B2 * v_ref[...] + (1.0 - ADAM_B2) * (gv * gv)
        g_ref[...] = gv
        d_ref[...] = -ADAM_LR * ((nm / c1) / (jnp.sqrt(nv / c2) + ADAM_EPS) + ADAM_WD * w_ref[...])
        nm_ref[...] = nm
        nv_ref[...] = nv

    full = pl.BlockSpec((None, None, tr, c), lambda l, h, i, c_ref: (l, h, i, 0))
    half_mine = pl.BlockSpec((None, tr, c), lambda l, h, i, c_ref: (l, jnp.where(h == c_ref[0], i, 0), 0))
    half_other = pl.BlockSpec((None, tr, c), lambda l, h, i, c_ref: (l, jnp.where(h == c_ref[0], 0, i), 0))
    grid_spec = pltpu.PrefetchScalarGridSpec(num_scalar_prefetch=1, grid=(nl, 2, rh // tr),
                                             in_specs=[full] * 3 + [half_mine, half_other], out_specs=[full] * 4)
    shp = jax.ShapeDtypeStruct((nl, 2, rh, c), F32)
    view = (nl, 2, rh, c)
    outs = pl.pallas_call(body, grid_spec=grid_spec, out_shape=[shp] * 4, compiler_params=_params("arbitrary", "arbitrary", "arbitrary"),
                          name=name)(c_idx, w.reshape(view), m.reshape(view), v.reshape(view), mine, other)
    return [o.reshape(nl, r, c) for o in outs]


WEIGHTS = ["mem_ln_g", "mem_ln_b", "w_in", "sg_ln_g", "sg_ln_b", "sg_w", "sg_b", "conv_w", "conv_b", "dt_bias", "a_log",
           "d_skip", "ssm_norm_g", "p_a", "p_b", "w_mix_o", "w_xq", "w_xkv", "w_xo", "w_ffn_in", "w_ffn_out", "ln_g", "ln_b"]
ARG_NAMES = ["x", "mem"] + WEIGHTS + ["loss_target"] + ["m_" + n for n in WEIGHTS] + ["v_" + n for n in WEIGHTS]
BIG = {"w_in": (1, (1024, 9248)), "p_a": (0, (1024, 1024)), "p_b": (0, (2048, 1024)), "w_mix_o": (0, (1024, 1024)),
       "w_xq": (0, (1024, 1024)), "w_xkv": (1, (1024, 2048)), "w_xo": (0, (1024, 1024)), "w_ffn_in": (1, (1024, 5632)),
       "w_ffn_out": (0, (2816, 1024))}
SMALL_SHARDED = {"conv_w": (4, 3072), "ln_g": (3, 1024), "ln_b": (3, 1024)}
SMALL = [n for n in WEIGHTS if n not in BIG]
W_IN_MAP = ((0, 4096, "main", 0), (4096, 7168, "main", XBC_COL0), (7168, 7200, "dt", 0), (7200, 9248, "main", GAB_COL0))
W_IN_SHARD = 9248 // N_CHIPS


def _w_in_chip_major(gm, gd):
    src = {"main": gm, "dt": gd}
    blocks = []
    for j in range(N_CHIPS):
        lo, hi = j * W_IN_SHARD, (j + 1) * W_IN_SHARD
        parts = [src[k][:, o + max(lo, a) - a:o + min(hi, b) - a] for a, b, k, o in W_IN_MAP if max(lo, a) < min(hi, b)]
        blocks.append(jnp.concatenate(parts, axis=1))
    return jnp.stack(blocks)


def _w_in_reassemble(wc):
    def cols(a, b):
        out = []
        for j in range(N_CHIPS):
            lo, hi = max(a, j * W_IN_SHARD), min(b, (j + 1) * W_IN_SHARD)
            if lo < hi:
                out.append(wc[j][:, lo - j * W_IN_SHARD:hi - j * W_IN_SHARD])
        return out

    main = sorted((m for m in W_IN_MAP if m[2] == "main"), key=lambda m: m[3])
    w_main = jnp.concatenate([p for a, b, _, _ in main for p in cols(a, b)], axis=1)
    (a, b, _, _), = [m for m in W_IN_MAP if m[2] == "dt"]
    w_dt = jnp.pad(jnp.concatenate(cols(a, b), axis=1), ((0, 0), (0, HEAD_PAD - (b - a))))
    return w_main, w_dt
GATHER_KIND = {"w_in": "chip", "p_a": "row", "p_b": "row", "w_mix_o": "row", "w_xq": "row", "w_xkv": "col", "w_xo": "row",
               "w_ffn_in": "col", "w_ffn_out": "row", "conv_w": "chip", "ln_g": "chip", "ln_b": "chip"}
GRAD_VIEW = {n: ("col" if k == "col" else "chip") for n, k in GATHER_KIND.items() if n in BIG}


def _shard_shape(name):
    axis, (r, c) = BIG[name]
    return (r // N_CHIPS, c) if axis == 0 else (r, c // N_CHIPS)


def _pad_rows(flat, cols, row_mult):
    n = flat.shape[0]
    rows = -(-n // cols)
    rows = -(-rows // row_mult) * row_mult
    return jnp.pad(flat, (0, rows * cols - n)).reshape(rows, cols)


def _gather_small_params(a, chip):
    names = list(SMALL_SHARDED)
    kinds = [GATHER_KIND[n] for n in names]
    bufs = [_cast_place(a[n], GATHER_KIND[n], F32, chip.reshape(1), name=f"place_{n}") for n in names]
    outs = _gather_params(bufs, [a[n].shape[1:] for n in names], kinds, name="gather_small_params")
    full = {}
    for n, o in zip(names, outs):
        _, _, r, c = o.shape
        full[n] = jnp.transpose(o, (0, 2, 1, 3)).reshape(DEPTH, r, N_CHIPS * c)
    return full


GATHER_GROUPS = (("w_in",), tuple(n for n in BIG if n != "w_in"))


def _gather_group_start(a, l, names, chip, after, *, tag):
    bufs = [_cast_place_layer(a[n], l, GATHER_KIND[n], chip.reshape(1), after, name=f"place_{n}_l{l}") for n in names]
    return _gather_start(bufs, [a[n].shape[1:] for n in names], [GATHER_KIND[n] for n in names], name=f"gather_start_{tag}")


def _gather_group_finish(a, names, flight, after, *, tag):
    send_sems, recv_sems, bufs, token = flight
    shapes, kinds = [a[n].shape[1:] for n in names], [GATHER_KIND[n] for n in names]
    bufs = _gather_wait(send_sems, recv_sems, bufs, shapes, kinds, token if after is None else after, name=f"gather_wait_{tag}")
    full = dict(zip(names, _gather_forward(bufs, shapes, kinds, name=f"gather_forward_{tag}")))
    if "w_in" in full:
        full["w_main"], full["w_dt"] = _w_in_reassemble(full.pop("w_in"))
    return full


def _layer_weights(a, big, small, l):
    w = dict(big)
    for n in SMALL_SHARDED:
        w[n] = small[n][l]
    for n in ["sg_ln_g", "sg_ln_b", "sg_w", "conv_b", "ssm_norm_g"]:
        w[n] = a[n][l]
    w["sg_bcol"] = a["sg_b"][l][..., None]
    for n in ["dt_bias", "a_log"]:
        w[n + "8"] = _pad_heads(a[n][l])
    w["d_skipx"] = _expand_heads(a["d_skip"][l])
    return w


def _grad_views(grads, names):
    gs = []
    for n in names:
        axis, _ = BIG[n]
        r, c = _shard_shape(n)
        if n == "w_in":
            gs.append(_w_in_chip_major(grads["w_main"], grads["w_dt"]))
        elif axis == 0:
            gs.append(grads[n].reshape(N_CHIPS, r, c))
        else:
            gs.append(grads[n])
    return gs


class _GradExchange:
    def __init__(self, grads, names, c_idx, tag):
        self.names, self.c_idx, self.tag = names, c_idx, tag
        self.views = [GRAD_VIEW[n] for n in names]
        self.gs = _grad_views(grads, names)

    def start(self):
        self.sems = _grads_to_sibling_start(self.gs, self.views, name=f"grads_to_sibling_start_{self.tag}")
        return self.sems[4]

    def cross(self, after):
        send_sems, recv_sems, gs, lands, token = self.sems
        self.gs, self.recv = _grads_to_sibling_wait(send_sems, recv_sems, gs, lands, self.views, token if after is None else after,
                                                    name=f"grads_to_sibling_wait_{self.tag}")
        cpre = self.c_idx.reshape(1)
        pairs = [_pair_sum(g, rv, v, cpre, name=f"grads_pair_sum_{n}_{self.tag}")
                 for g, rv, v, n in zip(self.gs, self.recv, self.views, self.names)]
        self.sems = _grads_to_chips_start(pairs, self.views, name=f"grads_to_chips_start_{self.tag}")
        return self.sems[4]

    def finish(self, after):
        send_sems, recv_sems, pairs, lands, _ = self.sems
        quads = _grads_to_chips_wait(send_sems, recv_sems, pairs, lands, self.views, after, name=f"grads_to_chips_wait_{self.tag}")
        return {n: (g, rv, q) for n, g, rv, q in zip(self.names, self.gs, self.recv, quads)}


def _finish_big_grads(parts, c_idx, chip):
    tots = [_quad_sum([parts[l][n][0] for l in range(DEPTH)], [parts[l][n][1] for l in range(DEPTH)],
                      [parts[l][n][2] for l in range(DEPTH)], GRAD_VIEW[n], chip.reshape(1), c_idx.reshape(1),
                      name=f"grads_chip_sum_{n}") for n in BIG]
    others = _grads_share(tots, name="grads_share")
    return {n: (t, o) for n, t, o in zip(BIG, tots, others)}


def _direct_copies(x_ref, land_ref, send_sems, recv_sems):
    mx, my, mc = lax.axis_index("x"), lax.axis_index("y"), lax.axis_index("c")
    me = 4 * mx + 2 * my + mc
    copies = []
    for k in range(N_DEV - 1):
        f = k + 1
        to = (mx ^ (f >> 2 & 1), my ^ (f >> 1 & 1), mc ^ (f & 1))
        copies.append(pltpu.make_async_remote_copy(src_ref=x_ref, dst_ref=land_ref.at[me], send_sem=send_sems.at[k],
                                                   recv_sem=recv_sems.at[k], device_id=to, device_id_type=_MESH))
    return copies


def _all_gather8_start(x, *, name):
    land = pltpu.with_memory_space_constraint(lax.empty((N_DEV,) + x.shape, x.dtype), pltpu.HBM)

    def body(x_ref, land_ref, send_sems, recv_sems, x_out, land_out, token):
        for cp in _direct_copies(x_ref, land_ref, send_sems, recv_sems):
            cp.start()
        token[...] = jnp.zeros_like(token)

    n = N_DEV - 1
    return pl.pallas_call(
        body, name=name,
        out_shape=(pltpu.SemaphoreType.DMA((n,)), pltpu.SemaphoreType.DMA((n,)), pltpu.HBM(x.shape, x.dtype),
                   pltpu.HBM(land.shape, land.dtype), jax.ShapeDtypeStruct((SUBLANE, LANE), F32)),
        in_specs=[_HBM, _HBM], out_specs=(_SEM, _SEM, _HBM, _HBM, pl.BlockSpec(memory_space=pltpu.VMEM)),
        input_output_aliases={0: 2, 1: 3}, compiler_params=pltpu.CompilerParams(has_side_effects=_EFFECT),
    )(pltpu.with_memory_space_constraint(x, pltpu.HBM), land)


def _all_gather8_wait(send_sems, recv_sems, x, land, after, *, name):
    def body(x_ref, land_ref, s_sems, r_sems, _, x_out, land_out):
        for cp in _direct_copies(x_ref, land_ref, s_sems, r_sems):
            cp.wait_send()
            cp.wait_recv()

    return pl.pallas_call(
        body, name=name, out_shape=(pltpu.HBM(x.shape, x.dtype), pltpu.HBM(land.shape, land.dtype)),
        in_specs=[_HBM, _HBM, _SEM, _SEM, _ANY], out_specs=(_HBM, _HBM), input_output_aliases={0: 0, 1: 1},
        compiler_params=pltpu.CompilerParams(has_side_effects=_EFFECT),
    )(x, land, send_sems, recv_sems, after)


def _pack_small(small):
    return _pad_rows(jnp.concatenate([small[n].reshape(-1) for n in small]), LANE, SUBLANE)


def _unpack_small(small, g8, packed, chip, c_idx, *, name):
    names = list(small)
    tot = _sum_devices(g8, packed, (2 * chip + c_idx).reshape(1), name=name).reshape(-1)
    out, off = {}, 0
    for n in names:
        sz = small[n].size
        full = tot[off:off + sz].reshape(small[n].shape)
        off += sz
        if n in SMALL_SHARDED:
            cs = SMALL_SHARDED[n][1] // N_CHIPS
            full = lax.dynamic_slice_in_dim(full, chip * cs, cs, axis=-1)
        out[n] = full
    return out


def kernel(x, mem, mem_ln_g, mem_ln_b, w_in, sg_ln_g, sg_ln_b, sg_w, sg_b, conv_w, conv_b, dt_bias, a_log, d_skip, ssm_norm_g, p_a, p_b, w_mix_o, w_xq, w_xkv, w_xo, w_ffn_in, w_ffn_out, ln_g, ln_b, loss_target, m_mem_ln_g, m_mem_ln_b, m_w_in, m_sg_ln_g, m_sg_ln_b, m_sg_w, m_sg_b, m_conv_w, m_conv_b, m_dt_bias, m_a_log, m_d_skip, m_ssm_norm_g, m_p_a, m_p_b, m_w_mix_o, m_w_xq, m_w_xkv, m_w_xo, m_w_ffn_in, m_w_ffn_out, m_ln_g, m_ln_b, v_mem_ln_g, v_mem_ln_b, v_w_in, v_sg_ln_g, v_sg_ln_b, v_sg_w, v_sg_b, v_conv_w, v_conv_b, v_dt_bias, v_a_log, v_d_skip, v_ssm_norm_g, v_p_a, v_p_b, v_w_mix_o, v_w_xq, v_w_xkv, v_w_xo, v_w_ffn_in, v_w_ffn_out, v_ln_g, v_ln_b):
    a = dict(zip(ARG_NAMES, (x, mem, mem_ln_g, mem_ln_b, w_in, sg_ln_g, sg_ln_b, sg_w, sg_b, conv_w, conv_b, dt_bias, a_log, d_skip, ssm_norm_g, p_a, p_b, w_mix_o, w_xq, w_xkv, w_xo, w_ffn_in, w_ffn_out, ln_g, ln_b, loss_target, m_mem_ln_g, m_mem_ln_b, m_w_in, m_sg_ln_g, m_sg_ln_b, m_sg_w, m_sg_b, m_conv_w, m_conv_b, m_dt_bias, m_a_log, m_d_skip, m_ssm_norm_g, m_p_a, m_p_b, m_w_mix_o, m_w_xq, m_w_xkv, m_w_xo, m_w_ffn_in, m_w_ffn_out, m_ln_g, m_ln_b, v_mem_ln_g, v_mem_ln_b, v_w_in, v_sg_ln_g, v_sg_ln_b, v_sg_w, v_sg_b, v_conv_w, v_conv_b, v_dt_bias, v_a_log, v_d_skip, v_ssm_norm_g, v_p_a, v_p_b, v_w_mix_o, v_w_xq, v_w_xkv, v_w_xo, v_w_ffn_in, v_w_ffn_out, v_ln_g, v_ln_b)))
    c_idx = lax.axis_index("c").astype(jnp.int32)
    chip = (2 * lax.axis_index("x") + lax.axis_index("y")).astype(jnp.int32)

    small = _gather_small_params(a, chip)
    ga, gb = GATHER_GROUPS
    flights = {(0, 0): _gather_group_start(a, 0, ga, chip, small["ln_b"], tag="l0_a")}
    flights[0, 1] = _gather_group_start(a, 0, gb, chip, flights[0, 0][3], tag="l0_b")

    def layer_weights(after, l):
        first = _gather_group_finish(a, ga, flights[l, 0], after if l else flights[l, 1][3], tag=f"l{l}_a")

        def rest(w, after_b):
            more = _gather_group_finish(a, gb, flights[l, 1], after_b, tag=f"l{l}_b")
            if l + 1 < DEPTH:
                flights[l + 1, 0] = _gather_group_start(a, l + 1, ga, chip, more["p_a"], tag=f"l{l + 1}_a")
                flights[l + 1, 1] = _gather_group_start(a, l + 1, gb, chip, flights[l + 1, 0][3], tag=f"l{l + 1}_b")
                more["p_a"] = more["p_a"] + flights[l + 1, 1][3][0, 0].astype(MXU_DTYPE)
            return {k: v for k, v in {**w, **more}.items() if k != "rest"}

        return dict(_layer_weights(a, first, small, l), rest=rest)

    layers = [functools.partial(layer_weights, l=l) for l in range(DEPTH)]
    exchanges, seen, small_flight = [], {}, {}

    def start_exchange(l, names, grads_l):
        ex = _GradExchange(grads_l, names, c_idx, f"l{l}_{names[0]}")
        tokens = [ex.start()]
        if exchanges:
            tokens.append(exchanges[-1][1].cross(tokens[0]))
        exchanges.append((l, ex))
        seen[l] = grads_l
        if l == 0 and names == GRAD_GROUPS[-1]:
            tokens.append(ex.cross(None))
            small = {}
            for n in SMALL:
                if n.startswith("mem_ln"):
                    continue
                per_layer = []
                for k in range(DEPTH):
                    g = seen[k][n]
                    if n in ("dt_bias", "a_log", "d_skip"):
                        g = g[0, :SSM_HEADS]
                    per_layer.append(g.reshape(a[n].shape[1:-1] + (-1,)))
                small[n] = jnp.stack(per_layer)
            small_flight["small"] = small
            small_flight["sems"] = _all_gather8_start(_pack_small(small), name="gather_small_grads_start")
            tokens.append(small_flight["sems"][4])
        return sum(tokens[1:], tokens[0])

    lsum, grad_x, grads, d_mem_g, d_mem_b = _local_step(x, mem, loss_target, mem_ln_g, mem_ln_b, layers, start_exchange)
    loss = lax.psum(0.5 * jnp.sum(lsum) / D_MODEL, ("x", "y", "c"))

    parts = [{} for _ in range(DEPTH)]
    for l, ex in exchanges:
        parts[l].update(ex.finish(grad_x))
    halves = _finish_big_grads(parts, c_idx, chip)
    gw = {}
    send_sems, recv_sems, packed, land, _ = small_flight["sems"]
    packed, g8 = _all_gather8_wait(send_sems, recv_sems, packed, land, grad_x, name="gather_small_grads_wait")
    gw.update(_unpack_small(small_flight["small"], g8, packed, chip, c_idx, name="small_grads_sum"))
    mem_small = {"mem_ln_g": d_mem_g, "mem_ln_b": d_mem_b}
    mem_packed = _pack_small(mem_small)
    gw.update(_unpack_small(mem_small, _all_gather8(mem_packed, name="gather_mem_ln_grads"), mem_packed, chip, c_idx,
                            name="mem_ln_grads_sum"))

    delta, new_m, new_v = {}, {}, {}
    for n in BIG:
        mine, other = halves[n]
        gw[n], delta[n], new_m[n], new_v[n] = _adamw_halves(a[n], a["m_" + n], a["v_" + n], mine, other, c_idx.reshape(1),
                                                             name=f"adamw_{n}")
    for n in SMALL:
        shp = a[n].shape
        view = (-1, LANE) if a[n].size % LANE == 0 else (1, -1)
        outs = _adamw(*[v.reshape(view) for v in (a[n], gw[n], a["m_" + n], a["v_" + n])], name=f"adamw_{n}")
        delta[n], new_m[n], new_v[n] = (o.reshape(shp) for o in outs)
    return (loss, grad_x, *[gw[n].reshape(a[n].shape) for n in WEIGHTS], *[delta[n] for n in WEIGHTS],
            *[new_m[n] for n in WEIGHTS], *[new_v[n] for n in WEIGHTS])
```
